```python
import math
import jax, jax.numpy as jnp
from jax import lax
import numpy as np

D_MODEL = 1024
BATCH = 8
SEQ = 4096
DEPTH = 1

CHUNK = 64
N_HEADS = 16
N_KV_HEADS = 4
HEAD_DIM = 64
WINDOW = 128
WINDOW_CHUNKS = WINDOW // CHUNK
N_BUCKETS = 32
MAX_DISTANCE = 128
LRU_WIDTH = D_MODEL
LRU_BLOCKS = 16
LRU_BLOCK = LRU_WIDTH // LRU_BLOCKS
CONV_WIDTH = 4
LRU_C = 8.0
D_FF = 2816
RMS_EPS = 1e-6
Q_W = N_HEADS * HEAD_DIM
KV_W = N_KV_HEADS * HEAD_DIM
IN_W = Q_W + 2 * KV_W + 2 * LRU_WIDTH
N_BRANCHES = 2
NEG_INF = -1e30

kernel_name = "hybrid_rglru_swa_sink_macaron"


def rms_norm(x, g):
    xf = x.astype(jnp.float32)
    y = xf * lax.rsqrt(jnp.mean(xf * xf, axis=-1, keepdims=True) + RMS_EPS)
    return (y * g.astype(jnp.float32)).astype(x.dtype)


def swiglu(x, w1, w3, w2):
    return (jax.nn.silu(x @ w1) * (x @ w3)) @ w2


def t5_bucket(rel):
    nb = N_BUCKETS // 2
    max_exact = nb // 2
    ret = jnp.where(rel > 0, nb, 0)
    n = jnp.abs(rel)
    nf = jnp.maximum(n, 1).astype(jnp.float32)
    large = max_exact + (jnp.log(nf / max_exact) / math.log(MAX_DISTANCE / max_exact)
                         * (nb - max_exact)).astype(jnp.int32)
    large = jnp.minimum(large, nb - 1)
    return ret + jnp.where(n < max_exact, n, large)


def band_rel_bias(table):
    kb = (WINDOW_CHUNKS + 1) * CHUNK
    i = jnp.arange(CHUNK)[:, None]
    j = jnp.arange(kb)[None, :]
    buckets = t5_bucket(j - WINDOW_CHUNKS * CHUNK - i)
    return jnp.transpose(table[buckets], (2, 0, 1))


def causal_conv(x, w, b):
    s = x.shape[1]
    xp = jnp.pad(x, ((0, 0), (CONV_WIDTH - 1, 0), (0, 0)))
    y = b
    for tap in range(CONV_WIDTH):
        y = y + xp[:, tap:tap + s] * w[tap]
    return y


def rg_lru(x, wa, ba, wx, bx, lam):
    b, s, _ = x.shape
    xf = x.astype(jnp.float32)
    xb = xf.reshape(b, s, LRU_BLOCKS, LRU_BLOCK)
    r = jax.nn.sigmoid(jnp.einsum('bshi,hij->bshj', xb, wa.astype(jnp.float32)).reshape(b, s, LRU_WIDTH) + ba)
    ig = jax.nn.sigmoid(jnp.einsum('bshi,hij->bshj', xb, wx.astype(jnp.float32)).reshape(b, s, LRU_WIDTH) + bx)
    log_a = -LRU_C * r * jax.nn.softplus(-lam.astype(jnp.float32))
    a = jnp.exp(log_a)
    u = jnp.sqrt(-jnp.expm1(2.0 * log_a)) * (ig * xf)

    def combine(c1, c2):
        a1, b1 = c1
        a2, b2 = c2
        return a1 * a2, a2 * b1 + b2

    _, h = lax.associative_scan(combine, (a, u), axis=1)
    return h.astype(x.dtype)


def swa_sink_attention(q, k, v, sinks, bias):
    b, s, _ = q.shape
    nc = s // CHUNK
    kb = (WINDOW_CHUNKS + 1) * CHUNK
    rep = N_HEADS // N_KV_HEADS
    qc = q.reshape(b, nc, CHUNK, N_KV_HEADS, rep, HEAD_DIM)
    pad = ((0, 0), (WINDOW_CHUNKS * CHUNK, 0), (0, 0))
    kp = jnp.pad(k, pad).reshape(b, nc + WINDOW_CHUNKS, CHUNK, N_KV_HEADS, HEAD_DIM)
    vp = jnp.pad(v, pad).reshape(b, nc + WINDOW_CHUNKS, CHUNK, N_KV_HEADS, HEAD_DIM)
    kband = jnp.concatenate([kp[:, w:w + nc] for w in range(WINDOW_CHUNKS + 1)], axis=2)
    vband = jnp.concatenate([vp[:, w:w + nc] for w in range(WINDOW_CHUNKS + 1)], axis=2)
    sc = jnp.einsum('bnqgrd,bnkgd->bngrqk', qc, kband).astype(jnp.float32) * (HEAD_DIM ** -0.5)
    sc = sc + bias.astype(jnp.float32).reshape(N_KV_HEADS, rep, CHUNK, kb)
    key_chunk = jnp.arange(nc)[:, None] - WINDOW_CHUNKS + jnp.arange(kb)[None, :] // CHUNK
    valid = key_chunk >= 0
    sc = jnp.where(valid[None, :, None, None, None, :], sc, NEG_INF)
    sink = jnp.broadcast_to(sinks.astype(jnp.float32).reshape(1, 1, N_KV_HEADS, rep, 1, 1),
                            sc.shape[:-1] + (1,))
    p = jax.nn.softmax(jnp.concatenate([sc, sink], axis=-1), axis=-1)[..., :-1]
    o = jnp.einsum('bngrqk,bnkgd->bnqgrd', p.astype(vband.dtype), vband)
    return o.reshape(b, s, Q_W)


def _fwd_setup_inputs(seed: int = 0) -> dict:
    key = jax.random.key(seed)
    ks = jax.random.split(key, 32)
    f32 = jnp.float32
    L, D = DEPTH, D_MODEL

    def nrm(k, shape, scale):
        return jax.random.normal(k, shape, f32) * scale

    def gain(k):
        return 1.0 + 0.05 * jax.random.normal(k, (L, D), f32)

    a0 = jax.random.uniform(ks[10], (L, LRU_WIDTH), f32, 0.9, 0.999)
    return {
        "x": jax.random.normal(ks[0], (BATCH, SEQ, D), f32),
        "ffn1_pre_g": gain(ks[1]),
        "ffn1_w1": nrm(ks[2], (L, D, D_FF), D ** -0.5),
        "ffn1_w3": nrm(ks[3], (L, D, D_FF), D ** -0.5),
        "ffn1_w2": nrm(ks[4], (L, D_FF, D), D_FF ** -0.5),
        "ffn1_post_g": gain(ks[5]),
        "mix_pre_g": gain(ks[6]),
        "w_in": nrm(ks[7], (L, D, IN_W), D ** -0.5),
        "conv_w": nrm(ks[8], (L, CONV_WIDTH, LRU_WIDTH), CONV_WIDTH ** -0.5),
        "conv_b": nrm(ks[9], (L, LRU_WIDTH), 0.02),
        "rg_a_w": nrm(ks[11], (L, LRU_BLOCKS, LRU_BLOCK, LRU_BLOCK), LRU_BLOCK ** -0.5),
        "rg_a_b": nrm(ks[12], (L, LRU_WIDTH), 0.1),
        "rg_x_w": nrm(ks[13], (L, LRU_BLOCKS, LRU_BLOCK, LRU_BLOCK), LRU_BLOCK ** -0.5),
        "rg_x_b": nrm(ks[14], (L, LRU_WIDTH), 0.1),
        "lru_lambda": jnp.log(a0) - jnp.log1p(-a0),
        "w_lru_out": nrm(ks[15], (L, LRU_WIDTH, D), LRU_WIDTH ** -0.5),
        "attn_sinks": nrm(ks[16], (L, N_HEADS), 0.5),
        "rel_bias": nrm(ks[17], (N_BUCKETS, N_HEADS), 0.5),
        "w_attn_out": nrm(ks[18], (L, Q_W, D), Q_W ** -0.5),
        "w_gate": nrm(ks[19], (L, D, N_BRANCHES * D), D ** -0.5),
        "b_gate": nrm(ks[20], (L, N_BRANCHES * D), 0.1),
        "w_o": nrm(ks[21], (L, D, D), D ** -0.5),
        "mix_post_g": gain(ks[22]),
        "ffn2_pre_g": gain(ks[23]),
        "ffn2_w1": nrm(ks[24], (L, D, D_FF), D ** -0.5),
        "ffn2_w3": nrm(ks[25], (L, D, D_FF), D ** -0.5),
        "ffn2_w2": nrm(ks[26], (L, D_FF, D), D_FF ** -0.5),
        "ffn2_post_g": gain(ks[27]),
    }


def _fwd_reference(x, ffn1_pre_g, ffn1_w1, ffn1_w3, ffn1_w2, ffn1_post_g, mix_pre_g, w_in, conv_w, conv_b,
              rg_a_w, rg_a_b, rg_x_w, rg_x_b, lru_lambda, w_lru_out, attn_sinks, rel_bias, w_attn_out,
              w_gate, b_gate, w_o, mix_post_g, ffn2_pre_g, ffn2_w1, ffn2_w3, ffn2_w2, ffn2_post_g):
    b, s, d = x.shape
    bias = band_rel_bias(rel_bias)
    splits = [Q_W, Q_W + KV_W, Q_W + 2 * KV_W, Q_W + 2 * KV_W + LRU_WIDTH]
    h = x
    for l in range(DEPTH):
        f = swiglu(rms_norm(h, ffn1_pre_g[l]), ffn1_w1[l], ffn1_w3[l], ffn1_w2[l])
        h = h + 0.5 * rms_norm(f, ffn1_post_g[l])
        u = rms_norm(h, mix_pre_g[l])
        q, k, v, xr, xg = jnp.split(u @ w_in[l], splits, axis=-1)
        xr = causal_conv(xr, conv_w[l], conv_b[l])
        hr = rg_lru(xr, rg_a_w[l], rg_a_b[l], rg_x_w[l], rg_x_b[l], lru_lambda[l])
        y_a = (hr * jax.nn.gelu(xg)) @ w_lru_out[l]
        y_b = swa_sink_attention(q, k, v, attn_sinks[l], bias) @ w_attn_out[l]
        g = jax.nn.sigmoid(u @ w_gate[l] + b_gate[l]).reshape(b, s, N_BRANCHES, d)
        merged = g[:, :, 0] * y_a + g[:, :, 1] * y_b
        h = h + rms_norm(merged @ w_o[l], mix_post_g[l])
        f = swiglu(rms_norm(h, ffn2_pre_g[l]), ffn2_w1[l], ffn2_w3[l], ffn2_w2[l])
        h = h + 0.5 * rms_norm(f, ffn2_post_g[l])
    return h


import jax as _jax
import jax.numpy as _jnp

TWIN_FORMAT = 'train_step'
FWD_PARAMS = ['x', 'ffn1_pre_g', 'ffn1_w1', 'ffn1_w3', 'ffn1_w2', 'ffn1_post_g', 'mix_pre_g', 'w_in', 'conv_w', 'conv_b', 'rg_a_w', 'rg_a_b', 'rg_x_w', 'rg_x_b', 'lru_lambda', 'w_lru_out', 'attn_sinks', 'rel_bias', 'w_attn_out', 'w_gate', 'b_gate', 'w_o', 'mix_post_g', 'ffn2_pre_g', 'ffn2_w1', 'ffn2_w3', 'ffn2_w2', 'ffn2_post_g']
TWIN_WEIGHTS = ['ffn1_pre_g', 'ffn1_w1', 'ffn1_w3', 'ffn1_w2', 'ffn1_post_g', 'mix_pre_g', 'w_in', 'conv_w', 'conv_b', 'rg_a_w', 'rg_a_b', 'rg_x_w', 'rg_x_b', 'lru_lambda', 'w_lru_out', 'attn_sinks', 'rel_bias', 'w_attn_out', 'w_gate', 'b_gate', 'w_o', 'mix_post_g', 'ffn2_pre_g', 'ffn2_w1', 'ffn2_w3', 'ffn2_w2', 'ffn2_post_g']
TWIN_DIFF_INPUT = 'x'
TWIN_INPUTS = ['x', 'ffn1_pre_g', 'ffn1_w1', 'ffn1_w3', 'ffn1_w2', 'ffn1_post_g', 'mix_pre_g', 'w_in', 'conv_w', 'conv_b', 'rg_a_w', 'rg_a_b', 'rg_x_w', 'rg_x_b', 'lru_lambda', 'w_lru_out', 'attn_sinks', 'rel_bias', 'w_attn_out', 'w_gate', 'b_gate', 'w_o', 'mix_post_g', 'ffn2_pre_g', 'ffn2_w1', 'ffn2_w3', 'ffn2_w2', 'ffn2_post_g', 'loss_target', 'm_ffn1_pre_g', 'm_ffn1_w1', 'm_ffn1_w3', 'm_ffn1_w2', 'm_ffn1_post_g', 'm_mix_pre_g', 'm_w_in', 'm_conv_w', 'm_conv_b', 'm_rg_a_w', 'm_rg_a_b', 'm_rg_x_w', 'm_rg_x_b', 'm_lru_lambda', 'm_w_lru_out', 'm_attn_sinks', 'm_rel_bias', 'm_w_attn_out', 'm_w_gate', 'm_b_gate', 'm_w_o', 'm_mix_post_g', 'm_ffn2_pre_g', 'm_ffn2_w1', 'm_ffn2_w3', 'm_ffn2_w2', 'm_ffn2_post_g', 'v_ffn1_pre_g', 'v_ffn1_w1', 'v_ffn1_w3', 'v_ffn1_w2', 'v_ffn1_post_g', 'v_mix_pre_g', 'v_w_in', 'v_conv_w', 'v_conv_b', 'v_rg_a_w', 'v_rg_a_b', 'v_rg_x_w', 'v_rg_x_b', 'v_lru_lambda', 'v_w_lru_out', 'v_attn_sinks', 'v_rel_bias', 'v_w_attn_out', 'v_w_gate', 'v_b_gate', 'v_w_o', 'v_mix_post_g', 'v_ffn2_pre_g', 'v_ffn2_w1', 'v_ffn2_w3', 'v_ffn2_w2', 'v_ffn2_post_g']
TWIN_OUTPUTS = ['loss', 'grad_x', 'grad_ffn1_pre_g', 'grad_ffn1_w1', 'grad_ffn1_w3', 'grad_ffn1_w2', 'grad_ffn1_post_g', 'grad_mix_pre_g', 'grad_w_in', 'grad_conv_w', 'grad_conv_b', 'grad_rg_a_w', 'grad_rg_a_b', 'grad_rg_x_w', 'grad_rg_x_b', 'grad_lru_lambda', 'grad_w_lru_out', 'grad_attn_sinks', 'grad_rel_bias', 'grad_w_attn_out', 'grad_w_gate', 'grad_b_gate', 'grad_w_o', 'grad_mix_post_g', 'grad_ffn2_pre_g', 'grad_ffn2_w1', 'grad_ffn2_w3', 'grad_ffn2_w2', 'grad_ffn2_post_g', 'delta_ffn1_pre_g', 'delta_ffn1_w1', 'delta_ffn1_w3', 'delta_ffn1_w2', 'delta_ffn1_post_g', 'delta_mix_pre_g', 'delta_w_in', 'delta_conv_w', 'delta_conv_b', 'delta_rg_a_w', 'delta_rg_a_b', 'delta_rg_x_w', 'delta_rg_x_b', 'delta_lru_lambda', 'delta_w_lru_out', 'delta_attn_sinks', 'delta_rel_bias', 'delta_w_attn_out', 'delta_w_gate', 'delta_b_gate', 'delta_w_o', 'delta_mix_post_g', 'delta_ffn2_pre_g', 'delta_ffn2_w1', 'delta_ffn2_w3', 'delta_ffn2_w2', 'delta_ffn2_post_g', 'new_m_ffn1_pre_g', 'new_m_ffn1_w1', 'new_m_ffn1_w3', 'new_m_ffn1_w2', 'new_m_ffn1_post_g', 'new_m_mix_pre_g', 'new_m_w_in', 'new_m_conv_w', 'new_m_conv_b', 'new_m_rg_a_w', 'new_m_rg_a_b', 'new_m_rg_x_w', 'new_m_rg_x_b', 'new_m_lru_lambda', 'new_m_w_lru_out', 'new_m_attn_sinks', 'new_m_rel_bias', 'new_m_w_attn_out', 'new_m_w_gate', 'new_m_b_gate', 'new_m_w_o', 'new_m_mix_post_g', 'new_m_ffn2_pre_g', 'new_m_ffn2_w1', 'new_m_ffn2_w3', 'new_m_ffn2_w2', 'new_m_ffn2_post_g', 'new_v_ffn1_pre_g', 'new_v_ffn1_w1', 'new_v_ffn1_w3', 'new_v_ffn1_w2', 'new_v_ffn1_post_g', 'new_v_mix_pre_g', 'new_v_w_in', 'new_v_conv_w', 'new_v_conv_b', 'new_v_rg_a_w', 'new_v_rg_a_b', 'new_v_rg_x_w', 'new_v_rg_x_b', 'new_v_lru_lambda', 'new_v_w_lru_out', 'new_v_attn_sinks', 'new_v_rel_bias', 'new_v_w_attn_out', 'new_v_w_gate', 'new_v_b_gate', 'new_v_w_o', 'new_v_mix_post_g', 'new_v_ffn2_pre_g', 'new_v_ffn2_w1', 'new_v_ffn2_w3', 'new_v_ffn2_w2', 'new_v_ffn2_post_g']
TWIN_LEAF_KINDS = {'loss': 'loss', 'grad_x': 'grad_x', 'grad_ffn1_pre_g': 'grad_w', 'grad_ffn1_w1': 'grad_w', 'grad_ffn1_w3': 'grad_w', 'grad_ffn1_w2': 'grad_w', 'grad_ffn1_post_g': 'grad_w', 'grad_mix_pre_g': 'grad_w', 'grad_w_in': 'grad_w', 'grad_conv_w': 'grad_w', 'grad_conv_b': 'grad_w', 'grad_rg_a_w': 'grad_w', 'grad_rg_a_b': 'grad_w', 'grad_rg_x_w': 'grad_w', 'grad_rg_x_b': 'grad_w', 'grad_lru_lambda': 'grad_w', 'grad_w_lru_out': 'grad_w', 'grad_attn_sinks': 'grad_w', 'grad_rel_bias': 'grad_w', 'grad_w_attn_out': 'grad_w', 'grad_w_gate': 'grad_w', 'grad_b_gate': 'grad_w', 'grad_w_o': 'grad_w', 'grad_mix_post_g': 'grad_w', 'grad_ffn2_pre_g': 'grad_w', 'grad_ffn2_w1': 'grad_w', 'grad_ffn2_w3': 'grad_w', 'grad_ffn2_w2': 'grad_w', 'grad_ffn2_post_g': 'grad_w', 'delta_ffn1_pre_g': 'delta_w', 'delta_ffn1_w1': 'delta_w', 'delta_ffn1_w3': 'delta_w', 'delta_ffn1_w2': 'delta_w', 'delta_ffn1_post_g': 'delta_w', 'delta_mix_pre_g': 'delta_w', 'delta_w_in': 'delta_w', 'delta_conv_w': 'delta_w', 'delta_conv_b': 'delta_w', 'delta_rg_a_w': 'delta_w', 'delta_rg_a_b': 'delta_w', 'delta_rg_x_w': 'delta_w', 'delta_rg_x_b': 'delta_w', 'delta_lru_lambda': 'delta_w', 'delta_w_lru_out': 'delta_w', 'delta_attn_sinks': 'delta_w', 'delta_rel_bias': 'delta_w', 'delta_w_attn_out': 'delta_w', 'delta_w_gate': 'delta_w', 'delta_b_gate': 'delta_w', 'delta_w_o': 'delta_w', 'delta_mix_post_g': 'delta_w', 'delta_ffn2_pre_g': 'delta_w', 'delta_ffn2_w1': 'delta_w', 'delta_ffn2_w3': 'delta_w', 'delta_ffn2_w2': 'delta_w', 'delta_ffn2_post_g': 'delta_w', 'new_m_ffn1_pre_g': 'new_m', 'new_m_ffn1_w1': 'new_m', 'new_m_ffn1_w3': 'new_m', 'new_m_ffn1_w2': 'new_m', 'new_m_ffn1_post_g': 'new_m', 'new_m_mix_pre_g': 'new_m', 'new_m_w_in': 'new_m', 'new_m_conv_w': 'new_m', 'new_m_conv_b': 'new_m', 'new_m_rg_a_w': 'new_m', 'new_m_rg_a_b': 'new_m', 'new_m_rg_x_w': 'new_m', 'new_m_rg_x_b': 'new_m', 'new_m_lru_lambda': 'new_m', 'new_m_w_lru_out': 'new_m', 'new_m_attn_sinks': 'new_m', 'new_m_rel_bias': 'new_m', 'new_m_w_attn_out': 'new_m', 'new_m_w_gate': 'new_m', 'new_m_b_gate': 'new_m', 'new_m_w_o': 'new_m', 'new_m_mix_post_g': 'new_m', 'new_m_ffn2_pre_g': 'new_m', 'new_m_ffn2_w1': 'new_m', 'new_m_ffn2_w3': 'new_m', 'new_m_ffn2_w2': 'new_m', 'new_m_ffn2_post_g': 'new_m', 'new_v_ffn1_pre_g': 'new_v', 'new_v_ffn1_w1': 'new_v', 'new_v_ffn1_w3': 'new_v', 'new_v_ffn1_w2': 'new_v', 'new_v_ffn1_post_g': 'new_v', 'new_v_mix_pre_g': 'new_v', 'new_v_w_in': 'new_v', 'new_v_conv_w': 'new_v', 'new_v_conv_b': 'new_v', 'new_v_rg_a_w': 'new_v', 'new_v_rg_a_b': 'new_v', 'new_v_rg_x_w': 'new_v', 'new_v_rg_x_b': 'new_v', 'new_v_lru_lambda': 'new_v', 'new_v_w_lru_out': 'new_v', 'new_v_attn_sinks': 'new_v', 'new_v_rel_bias': 'new_v', 'new_v_w_attn_out': 'new_v', 'new_v_w_gate': 'new_v', 'new_v_b_gate': 'new_v', 'new_v_w_o': 'new_v', 'new_v_mix_post_g': 'new_v', 'new_v_ffn2_pre_g': 'new_v', 'new_v_ffn2_w1': 'new_v', 'new_v_ffn2_w3': 'new_v', 'new_v_ffn2_w2': 'new_v', 'new_v_ffn2_post_g': 'new_v'}


def _forward(args):
    return _fwd_reference(*[args[k] for k in FWD_PARAMS])


def _output_shape():
    def fwd():
        inp = _fwd_setup_inputs(0)
        return _fwd_reference(*[inp[k] for k in FWD_PARAMS])
    out = _jax.eval_shape(fwd)
    return out.shape, out.dtype

N_MICROBATCH = 1
ADAM_LR = 0.001
ADAM_B1 = 0.9
ADAM_B2 = 0.999
ADAM_EPS = 1e-08
ADAM_WD = 0.01
ADAM_STEP = 10
PER_EXAMPLE_BATCH_AXIS = {'x': 0, 'loss_target': 0}
SHARED_INPUTS = []
_WEIGHT_DTYPES = {'ffn1_pre_g': _jnp.float32, 'ffn1_w1': _jnp.float32, 'ffn1_w3': _jnp.float32, 'ffn1_w2': _jnp.float32, 'ffn1_post_g': _jnp.float32, 'mix_pre_g': _jnp.float32, 'w_in': _jnp.float32, 'conv_w': _jnp.float32, 'conv_b': _jnp.float32, 'rg_a_w': _jnp.float32, 'rg_a_b': _jnp.float32, 'rg_x_w': _jnp.float32, 'rg_x_b': _jnp.float32, 'lru_lambda': _jnp.float32, 'w_lru_out': _jnp.float32, 'attn_sinks': _jnp.float32, 'rel_bias': _jnp.float32, 'w_attn_out': _jnp.float32, 'w_gate': _jnp.float32, 'b_gate': _jnp.float32, 'w_o': _jnp.float32, 'mix_post_g': _jnp.float32, 'ffn2_pre_g': _jnp.float32, 'ffn2_w1': _jnp.float32, 'ffn2_w3': _jnp.float32, 'ffn2_w2': _jnp.float32, 'ffn2_post_g': _jnp.float32}
MOMENT_SCALE = {'ffn1_pre_g': 4.732133e-01, 'ffn1_w1': 1.816922e-01, 'ffn1_w3': 1.910318e-01, 'ffn1_w2': 3.189910e-01, 'ffn1_post_g': 7.985652e+00, 'mix_pre_g': 5.599410e-01, 'w_in': 2.805209e-01, 'conv_w': 3.848132e-01, 'conv_b': 4.171486e+00, 'rg_a_w': 1.340485e-01, 'rg_a_b': 9.267494e-02, 'rg_x_w': 2.474640e-01, 'rg_x_b': 1.443324e-01, 'lru_lambda': 1.769804e-01, 'w_lru_out': 3.924681e-01, 'attn_sinks': 1.084997e-02, 'rel_bias': 1.721953e-01, 'w_attn_out': 1.426126e-01, 'w_gate': 9.133490e-02, 'b_gate': 1.193447e-01, 'w_o': 4.309968e-01, 'mix_post_g': 3.212943e+01, 'ffn2_pre_g': 3.331899e-01, 'ffn2_w1': 1.084233e-01, 'ffn2_w3': 1.787711e-01, 'ffn2_w2': 2.955771e-01, 'ffn2_post_g': 7.985172e+00}


def _to_microbatches(a, axis):
    t = _jnp.moveaxis(a, axis, 0)
    t = t.reshape((N_MICROBATCH, t.shape[0] // N_MICROBATCH) + t.shape[1:])
    return _jnp.moveaxis(t, 1, axis + 1)


def setup_inputs(seed: int = 0) -> dict:
    inp = _fwd_setup_inputs(seed)
    key = _jax.random.fold_in(_jax.random.key(seed), 7919)
    shape, _ = _output_shape()
    out = dict(inp)
    out["loss_target"] = _jax.random.normal(_jax.random.fold_in(key, 0), shape, _jnp.float32)
    for i, name in enumerate(TWIN_WEIGHTS):
        w = inp[name].astype(_jnp.float32)
        if MOMENT_SCALE is None:
            s = _jnp.sqrt(_jnp.mean(_jnp.square(w)) + 1e-30)
        else:
            s = MOMENT_SCALE[name]
        km, kv = _jax.random.split(_jax.random.fold_in(key, i + 1))
        out[name] = w
        out["m_" + name] = s * _jax.random.normal(km, w.shape, _jnp.float32)
        out["v_" + name] = (s * s) * _jax.random.uniform(kv, w.shape, _jnp.float32, 0.5, 1.5)
    if N_MICROBATCH > 1:
        for name, axis in PER_EXAMPLE_BATCH_AXIS.items():
            out[name] = _to_microbatches(out[name], axis)
    return {'x': out['x'], 'ffn1_pre_g': out['ffn1_pre_g'], 'ffn1_w1': out['ffn1_w1'], 'ffn1_w3': out['ffn1_w3'], 'ffn1_w2': out['ffn1_w2'], 'ffn1_post_g': out['ffn1_post_g'], 'mix_pre_g': out['mix_pre_g'], 'w_in': out['w_in'], 'conv_w': out['conv_w'], 'conv_b': out['conv_b'], 'rg_a_w': out['rg_a_w'], 'rg_a_b': out['rg_a_b'], 'rg_x_w': out['rg_x_w'], 'rg_x_b': out['rg_x_b'], 'lru_lambda': out['lru_lambda'], 'w_lru_out': out['w_lru_out'], 'attn_sinks': out['attn_sinks'], 'rel_bias': out['rel_bias'], 'w_attn_out': out['w_attn_out'], 'w_gate': out['w_gate'], 'b_gate': out['b_gate'], 'w_o': out['w_o'], 'mix_post_g': out['mix_post_g'], 'ffn2_pre_g': out['ffn2_pre_g'], 'ffn2_w1': out['ffn2_w1'], 'ffn2_w3': out['ffn2_w3'], 'ffn2_w2': out['ffn2_w2'], 'ffn2_post_g': out['ffn2_post_g'], 'loss_target': out['loss_target'], 'm_ffn1_pre_g': out['m_ffn1_pre_g'], 'm_ffn1_w1': out['m_ffn1_w1'], 'm_ffn1_w3': out['m_ffn1_w3'], 'm_ffn1_w2': out['m_ffn1_w2'], 'm_ffn1_post_g': out['m_ffn1_post_g'], 'm_mix_pre_g': out['m_mix_pre_g'], 'm_w_in': out['m_w_in'], 'm_conv_w': out['m_conv_w'], 'm_conv_b': out['m_conv_b'], 'm_rg_a_w': out['m_rg_a_w'], 'm_rg_a_b': out['m_rg_a_b'], 'm_rg_x_w': out['m_rg_x_w'], 'm_rg_x_b': out['m_rg_x_b'], 'm_lru_lambda': out['m_lru_lambda'], 'm_w_lru_out': out['m_w_lru_out'], 'm_attn_sinks': out['m_attn_sinks'], 'm_rel_bias': out['m_rel_bias'], 'm_w_attn_out': out['m_w_attn_out'], 'm_w_gate': out['m_w_gate'], 'm_b_gate': out['m_b_gate'], 'm_w_o': out['m_w_o'], 'm_mix_post_g': out['m_mix_post_g'], 'm_ffn2_pre_g': out['m_ffn2_pre_g'], 'm_ffn2_w1': out['m_ffn2_w1'], 'm_ffn2_w3': out['m_ffn2_w3'], 'm_ffn2_w2': out['m_ffn2_w2'], 'm_ffn2_post_g': out['m_ffn2_post_g'], 'v_ffn1_pre_g': out['v_ffn1_pre_g'], 'v_ffn1_w1': out['v_ffn1_w1'], 'v_ffn1_w3': out['v_ffn1_w3'], 'v_ffn1_w2': out['v_ffn1_w2'], 'v_ffn1_post_g': out['v_ffn1_post_g'], 'v_mix_pre_g': out['v_mix_pre_g'], 'v_w_in': out['v_w_in'], 'v_conv_w': out['v_conv_w'], 'v_conv_b': out['v_conv_b'], 'v_rg_a_w': out['v_rg_a_w'], 'v_rg_a_b': out['v_rg_a_b'], 'v_rg_x_w': out['v_rg_x_w'], 'v_rg_x_b': out['v_rg_x_b'], 'v_lru_lambda': out['v_lru_lambda'], 'v_w_lru_out': out['v_w_lru_out'], 'v_attn_sinks': out['v_attn_sinks'], 'v_rel_bias': out['v_rel_bias'], 'v_w_attn_out': out['v_w_attn_out'], 'v_w_gate': out['v_w_gate'], 'v_b_gate': out['v_b_gate'], 'v_w_o': out['v_w_o'], 'v_mix_post_g': out['v_mix_post_g'], 'v_ffn2_pre_g': out['v_ffn2_pre_g'], 'v_ffn2_w1': out['v_ffn2_w1'], 'v_ffn2_w3': out['v_ffn2_w3'], 'v_ffn2_w2': out['v_ffn2_w2'], 'v_ffn2_post_g': out['v_ffn2_post_g']}


def _loss(weights, diff, rest, loss_target):
    with _jax.named_scope("forward"):
        args = {**rest, TWIN_DIFF_INPUT: diff, **{k: w.astype(_WEIGHT_DTYPES[k]) for k, w in weights.items()}}
        y = _forward(args)
    with _jax.named_scope("loss_head"):
        err = _jnp.square(y.astype(_jnp.float32) - loss_target)
        return 0.5 * _jnp.sum(_jnp.mean(err, axis=-1)) if err.ndim else 0.5 * err


def _adamw(w, g, m, v):
    m = ADAM_B1 * m + (1.0 - ADAM_B1) * g
    v = ADAM_B2 * v + (1.0 - ADAM_B2) * _jnp.square(g)
    m_hat = m / (1.0 - ADAM_B1 ** ADAM_STEP)
    v_hat = v / (1.0 - ADAM_B2 ** ADAM_STEP)
    delta = -ADAM_LR * (m_hat / (_jnp.sqrt(v_hat) + ADAM_EPS) + ADAM_WD * w)
    return delta, m, v


def reference(x, ffn1_pre_g, ffn1_w1, ffn1_w3, ffn1_w2, ffn1_post_g, mix_pre_g, w_in, conv_w, conv_b, rg_a_w, rg_a_b, rg_x_w, rg_x_b, lru_lambda, w_lru_out, attn_sinks, rel_bias, w_attn_out, w_gate, b_gate, w_o, mix_post_g, ffn2_pre_g, ffn2_w1, ffn2_w3, ffn2_w2, ffn2_post_g, loss_target, m_ffn1_pre_g, m_ffn1_w1, m_ffn1_w3, m_ffn1_w2, m_ffn1_post_g, m_mix_pre_g, m_w_in, m_conv_w, m_conv_b, m_rg_a_w, m_rg_a_b, m_rg_x_w, m_rg_x_b, m_lru_lambda, m_w_lru_out, m_attn_sinks, m_rel_bias, m_w_attn_out, m_w_gate, m_b_gate, m_w_o, m_mix_post_g, m_ffn2_pre_g, m_ffn2_w1, m_ffn2_w3, m_ffn2_w2, m_ffn2_post_g, v_ffn1_pre_g, v_ffn1_w1, v_ffn1_w3, v_ffn1_w2, v_ffn1_post_g, v_mix_pre_g, v_w_in, v_conv_w, v_conv_b, v_rg_a_w, v_rg_a_b, v_rg_x_w, v_rg_x_b, v_lru_lambda, v_w_lru_out, v_attn_sinks, v_rel_bias, v_w_attn_out, v_w_gate, v_b_gate, v_w_o, v_mix_post_g, v_ffn2_pre_g, v_ffn2_w1, v_ffn2_w3, v_ffn2_w2, v_ffn2_post_g):
    given = dict(x=x, ffn1_pre_g=ffn1_pre_g, ffn1_w1=ffn1_w1, ffn1_w3=ffn1_w3, ffn1_w2=ffn1_w2, ffn1_post_g=ffn1_post_g, mix_pre_g=mix_pre_g, w_in=w_in, conv_w=conv_w, conv_b=conv_b, rg_a_w=rg_a_w, rg_a_b=rg_a_b, rg_x_w=rg_x_w, rg_x_b=rg_x_b, lru_lambda=lru_lambda, w_lru_out=w_lru_out, attn_sinks=attn_sinks, rel_bias=rel_bias, w_attn_out=w_attn_out, w_gate=w_gate, b_gate=b_gate, w_o=w_o, mix_post_g=mix_post_g, ffn2_pre_g=ffn2_pre_g, ffn2_w1=ffn2_w1, ffn2_w3=ffn2_w3, ffn2_w2=ffn2_w2, ffn2_post_g=ffn2_post_g, loss_target=loss_target, m_ffn1_pre_g=m_ffn1_pre_g, m_ffn1_w1=m_ffn1_w1, m_ffn1_w3=m_ffn1_w3, m_ffn1_w2=m_ffn1_w2, m_ffn1_post_g=m_ffn1_post_g, m_mix_pre_g=m_mix_pre_g, m_w_in=m_w_in, m_conv_w=m_conv_w, m_conv_b=m_conv_b, m_rg_a_w=m_rg_a_w, m_rg_a_b=m_rg_a_b, m_rg_x_w=m_rg_x_w, m_rg_x_b=m_rg_x_b, m_lru_lambda=m_lru_lambda, m_w_lru_out=m_w_lru_out, m_attn_sinks=m_attn_sinks, m_rel_bias=m_rel_bias, m_w_attn_out=m_w_attn_out, m_w_gate=m_w_gate, m_b_gate=m_b_gate, m_w_o=m_w_o, m_mix_post_g=m_mix_post_g, m_ffn2_pre_g=m_ffn2_pre_g, m_ffn2_w1=m_ffn2_w1, m_ffn2_w3=m_ffn2_w3, m_ffn2_w2=m_ffn2_w2, m_ffn2_post_g=m_ffn2_post_g, v_ffn1_pre_g=v_ffn1_pre_g, v_ffn1_w1=v_ffn1_w1, v_ffn1_w3=v_ffn1_w3, v_ffn1_w2=v_ffn1_w2, v_ffn1_post_g=v_ffn1_post_g, v_mix_pre_g=v_mix_pre_g, v_w_in=v_w_in, v_conv_w=v_conv_w, v_conv_b=v_conv_b, v_rg_a_w=v_rg_a_w, v_rg_a_b=v_rg_a_b, v_rg_x_w=v_rg_x_w, v_rg_x_b=v_rg_x_b, v_lru_lambda=v_lru_lambda, v_w_lru_out=v_w_lru_out, v_attn_sinks=v_attn_sinks, v_rel_bias=v_rel_bias, v_w_attn_out=v_w_attn_out, v_w_gate=v_w_gate, v_b_gate=v_b_gate, v_w_o=v_w_o, v_mix_post_g=v_mix_post_g, v_ffn2_pre_g=v_ffn2_pre_g, v_ffn2_w1=v_ffn2_w1, v_ffn2_w3=v_ffn2_w3, v_ffn2_w2=v_ffn2_w2, v_ffn2_post_g=v_ffn2_post_g)
    weights = {n: given[n] for n in TWIN_WEIGHTS}
    shared = {n: given[n] for n in SHARED_INPUTS}
    per_example = {n: given[n] for n in ['x']}
    grad_fn = _jax.value_and_grad(_loss, argnums=(0, 1))

    def one_microbatch(ex, loss_target):
        ex = dict(ex)
        diff = ex.pop(TWIN_DIFF_INPUT)
        return grad_fn(weights, diff, {**shared, **ex}, loss_target)

    if N_MICROBATCH == 1:
        loss, (grad_w, grad_x) = one_microbatch(per_example, given["loss_target"])
    else:
        def body(carry, xs):
            loss_sum, grad_sum = carry
            l_k, (gw_k, gx_k) = one_microbatch(xs[0], xs[1])
            with _jax.named_scope("update"):
                return (loss_sum + l_k, _jax.tree.map(_jnp.add, grad_sum, gw_k)), gx_k

        init = (_jnp.zeros((), _jnp.float32), _jax.tree.map(_jnp.zeros_like, weights))
        (loss, grad_w), grad_x = _jax.lax.scan(body, init, (per_example, given["loss_target"]))
    with _jax.named_scope("update"):
        delta_w, new_m, new_v = {}, {}, {}
        for n in TWIN_WEIGHTS:
            delta_w[n], new_m[n], new_v[n] = _adamw(weights[n], grad_w[n], given["m_" + n], given["v_" + n])
    return (loss, grad_x, *[grad_w[n] for n in TWIN_WEIGHTS], *[delta_w[n] for n in TWIN_WEIGHTS],
            *[new_m[n] for n in TWIN_WEIGHTS], *[new_v[n] for n in TWIN_WEIGHTS])
```

```python
import functools
import math

import jax
import jax.numpy as jnp
from jax import lax
from jax.experimental import pallas as pl
from jax.experimental.pallas import tpu as pltpu

F32, BF16 = jnp.float32, jnp.bfloat16
D = 1024
NSH = 4
FF_S = 704
IN_S = 896
GATE_S = 512
KV_W = 256
CHUNK = 64
KB = 192
N_HEADS = 16
HEAD_DIM = 64
N_BUCKETS = 32
PAD_KEYS = 128
RMS_EPS = 1e-6
NEG_INF = -1e30
LRU_C = 8.0
TM = 256
VMEM_LIMIT = 56 * 1024 * 1024
ADAM_LR, ADAM_B1, ADAM_B2, ADAM_EPS, ADAM_WD, ADAM_STEP = 0.001, 0.9, 0.999, 1e-08, 0.01, 10
SMALL_ROWS = 1216
SMALL_SLICE = SMALL_ROWS // 8
MESH = pl.DeviceIdType.MESH

BIG = ["ffn1_w1", "ffn1_w3", "ffn1_w2", "w_in", "w_lru_out", "w_attn_out", "w_gate", "w_o", "ffn2_w1", "ffn2_w3", "ffn2_w2"]
SMALL = [("ffn1_pre_g", 1024), ("ffn1_post_g", 1024), ("mix_pre_g", 1024), ("conv_w", 4096), ("conv_b", 1024),
         ("rg_a_w", 65536), ("rg_a_b", 1024), ("rg_x_w", 65536), ("rg_x_b", 1024), ("lru_lambda", 1024),
         ("attn_sinks", 1024), ("rel_bias", 1024), ("b_gate", 2048), ("mix_post_g", 1024), ("ffn2_pre_g", 1024),
         ("ffn2_post_g", 1024)]
WEIGHTS = ["ffn1_pre_g", "ffn1_w1", "ffn1_w3", "ffn1_w2", "ffn1_post_g", "mix_pre_g", "w_in", "conv_w", "conv_b", "rg_a_w",
           "rg_a_b", "rg_x_w", "rg_x_b", "lru_lambda", "w_lru_out", "attn_sinks", "rel_bias", "w_attn_out", "w_gate", "b_gate",
           "w_o", "mix_post_g", "ffn2_pre_g", "ffn2_w1", "ffn2_w3", "ffn2_w2", "ffn2_post_g"]


def _params(*sem):
    return pltpu.CompilerParams(dimension_semantics=sem or None, vmem_limit_bytes=VMEM_LIMIT)


def _nn(a, b):
    return jnp.dot(a, b, preferred_element_type=F32)


def _nt(a, b):
    return lax.dot_general(a, b, (((1,), (1,)), ((), ())), preferred_element_type=F32)


def _tn(a, b):
    return lax.dot_general(a, b, (((0,), (0,)), ((), ())), preferred_element_type=F32)


def _rms(x, g):
    rstd = lax.rsqrt(jnp.mean(x * x, axis=-1, keepdims=True) + RMS_EPS)
    return (x * rstd) * g


def _rms_bwd(dout, x, g):
    rstd = lax.rsqrt(jnp.mean(x * x, axis=-1, keepdims=True) + RMS_EPS)
    xhat = x * rstd
    dg = jnp.sum(dout * xhat, axis=0, keepdims=True)
    dxhat = dout * g
    dx = rstd * (dxhat - xhat * jnp.mean(dxhat * xhat, axis=-1, keepdims=True))
    return dx, dg


_GELU_K = math.sqrt(2.0 / math.pi)


def _gelu(x):
    return x * (0.5 * (1.0 + jnp.tanh(_GELU_K * (x + 0.044715 * (x * x * x)))))


def _gelu_grad(x):
    t = jnp.tanh(_GELU_K * (x + 0.044715 * (x * x * x)))
    return 0.5 * (1.0 + t) + x * (0.5 * (1.0 - t * t) * (_GELU_K * (1.0 + 3.0 * 0.044715 * (x * x))))


def _softplus_neg(lam):
    z = -lam
    u = jnp.exp(-jnp.abs(z))
    w = 1.0 + u
    log1p_u = jnp.where(w == 1.0, u, jnp.log(w) * (u / (w - 1.0)))
    return jnp.maximum(z, 0.0) + log1p_u


def _lru_coeffs(r, sp):
    log_a = (-LRU_C * r) * sp
    a = jnp.exp(log_a)
    t = jnp.tanh(log_a)
    s = jnp.sqrt(-2.0 * t / (1.0 - t))
    return a, s


def _row_spec(tm, width):
    return pl.BlockSpec((tm, width), lambda i: (i, 0))


def _vec_spec(width):
    return pl.BlockSpec((1, width), lambda i: (0, 0))


_WHOLE = pl.BlockSpec(memory_space=pltpu.VMEM)


def _tile(t):
    return min(TM, t)


def _ffn_fwd(x, gpre, w1g, w3g, w2g, gpost, name):
    t = x.shape[0]
    tm = _tile(t)

    def body(x_ref, gpre_ref, w1_ref, w3_ref, w2_ref, gpost_ref, h_ref, a_ref, b_ref, hm_ref, f_ref):
        xv = x_ref[...]
        nb = _rms(xv, gpre_ref[...]).astype(BF16)
        f = jnp.zeros((tm, D), F32)
        for s in range(NSH):
            a = _nn(nb, w1_ref[s])
            b = _nn(nb, w3_ref[s])
            hmb = ((a * jax.nn.sigmoid(a)) * b).astype(BF16)
            a_ref[s] = a.astype(BF16)
            b_ref[s] = b.astype(BF16)
            hm_ref[s] = hmb
            f = f + _nn(hmb, w2_ref[s])
        f_ref[...] = f
        h_ref[...] = xv + 0.5 * _rms(f, gpost_ref[...])

    sh = pl.BlockSpec((NSH, tm, FF_S), lambda i: (0, i, 0))
    act = jax.ShapeDtypeStruct((NSH, t, FF_S), BF16)
    return pl.pallas_call(
        body, grid=(t // tm,), name=name,
        in_specs=[_row_spec(tm, D), _vec_spec(D), _WHOLE, _WHOLE, _WHOLE, _vec_spec(D)],
        out_specs=[_row_spec(tm, D), sh, sh, sh, _row_spec(tm, D)],
        out_shape=[jax.ShapeDtypeStruct((t, D), F32), act, act, act, jax.ShapeDtypeStruct((t, D), F32)],
        compiler_params=_params("arbitrary"),
    )(x, gpre, w1g, w3g, w2g, gpost)


def _loss_dy(y, target):
    t = y.shape[0]
    tm = _tile(t)

    def body(y_ref, t_ref, dy_ref, l_ref):
        @pl.when(pl.program_id(0) == 0)
        def _():
            l_ref[...] = jnp.zeros_like(l_ref)

        e = y_ref[...] - t_ref[...]
        dy_ref[...] = e * (1.0 / D)
        sq = jnp.sum(jnp.sum(e * e, axis=0, keepdims=True), axis=1, keepdims=True)
        l_ref[...] = l_ref[...] + sq

    return pl.pallas_call(
        body, grid=(t // tm,), name="loss_dy",
        in_specs=[_row_spec(tm, D), _row_spec(tm, D)],
        out_specs=[_row_spec(tm, D), pl.BlockSpec((1, 128), lambda i: (0, 0))],
        out_shape=[jax.ShapeDtypeStruct((t, D), F32), jax.ShapeDtypeStruct((1, 128), F32)],
        compiler_params=_params("arbitrary"),
    )(y, target)


def _mix_proj(h1, gmix, w_in_g, w_gate_g, b_gate):
    t = h1.shape[0]
    tm = _tile(t)

    def body(h_ref, g_ref, win_ref, wg_ref, bg_ref, u_ref, q_ref, k_ref, v_ref, xr_ref, xg_ref, gate_ref):
        ub = _rms(h_ref[...], g_ref[...]).astype(BF16)
        u_ref[...] = ub
        p0 = _nn(ub, win_ref[0])
        q_ref[:, 0:896] = p0.astype(BF16)
        p1 = _nn(ub, win_ref[1])
        q_ref[:, 896:1024] = p1[:, 0:128].astype(BF16)
        k_ref[...] = p1[:, 128:384].astype(BF16)
        v_ref[...] = p1[:, 384:640].astype(BF16)
        xr_ref[:, 0:256] = p1[:, 640:896]
        p2 = _nn(ub, win_ref[2])
        xr_ref[:, 256:1024] = p2[:, 0:768]
        xg_ref[:, 0:128] = p2[:, 768:896]
        xg_ref[:, 128:1024] = _nn(ub, win_ref[3])
        for s in range(NSH):
            sl = slice(s * GATE_S, (s + 1) * GATE_S)
            gate_ref[:, sl] = jax.nn.sigmoid(_nn(ub, wg_ref[s]) + bg_ref[:, sl])

    return pl.pallas_call(
        body, grid=(t // tm,), name="mix_proj",
        in_specs=[_row_spec(tm, D), _vec_spec(D), _WHOLE, _WHOLE, _vec_spec(2 * D)],
        out_specs=[_row_spec(tm, D), _row_spec(tm, D), _row_spec(tm, KV_W), _row_spec(tm, KV_W), _row_spec(tm, D),
                   _row_spec(tm, D), _row_spec(tm, 2 * D)],
        out_shape=[jax.ShapeDtypeStruct((t, D), BF16), jax.ShapeDtypeStruct((t, D), BF16),
                   jax.ShapeDtypeStruct((t, KV_W), BF16), jax.ShapeDtypeStruct((t, KV_W), BF16),
                   jax.ShapeDtypeStruct((t, D), F32), jax.ShapeDtypeStruct((t, D), F32),
                   jax.ShapeDtypeStruct((t, 2 * D), F32)],
        compiler_params=_params("arbitrary"),
    )(h1, gmix, w_in_g, w_gate_g, b_gate)


def _rglru_fwd(xr, xg, conv_w, conv_b, wa2, ba, wx2, bx, lam):
    t = xr.shape[0]
    tm = _tile(t)
    nb8 = tm // 8

    def body(xr_ref, xrp_ref, xg_ref, cw_ref, cb_ref, wa_ref, ba_ref, wx_ref, bx_ref, lam_ref,
             hr_ref, yain_ref, xc_ref, r_ref, ig_ref, ext, a_sc, h_sc):
        i = pl.program_id(0)

        @pl.when(i == 0)
        def _():
            h_sc[...] = jnp.zeros_like(h_sc)

        ext[0:8, :] = jnp.where(i == 0, 0.0, xrp_ref[...])
        ext[8:8 + tm, :] = xr_ref[...]
        xc = jnp.broadcast_to(cb_ref[...], (tm, D))
        for tap in range(4):
            xc = xc + ext[pl.ds(5 + tap, tm), :] * cw_ref[tap:tap + 1, :]
        xc_ref[...] = xc
        xcb = xc.astype(BF16)
        for p in range(8):
            sl = slice(p * 128, (p + 1) * 128)
            r_ref[:, sl] = jax.nn.sigmoid(_nn(xcb[:, sl], wa_ref[p]) + ba_ref[:, sl])
            ig_ref[:, sl] = jax.nn.sigmoid(_nn(xcb[:, sl], wx_ref[p]) + bx_ref[:, sl])
        a, s = _lru_coeffs(r_ref[...], _softplus_neg(lam_ref[...]))
        a_sc[...] = a
        hr_ref[...] = s * (ig_ref[...] * xc)

        def blk(j, h):
            st = pl.multiple_of(j * 8, 8)
            a8 = a_sc[pl.ds(st, 8), :]
            u8 = hr_ref[pl.ds(st, 8), :]
            rows = []
            for k in range(8):
                h = a8[k:k + 1, :] * h + u8[k:k + 1, :]
                rows.append(h)
            hr_ref[pl.ds(st, 8), :] = jnp.concatenate(rows, axis=0)
            return h

        h_sc[0:1, :] = lax.fori_loop(0, nb8, blk, h_sc[0:1, :])
        yain_ref[...] = (hr_ref[...] * _gelu(xg_ref[...])).astype(BF16)

    prev = pl.BlockSpec((8, D), lambda i: (jnp.maximum(i * nb8 - 1, 0), 0))
    full = lambda shape: pl.BlockSpec(shape, lambda i: tuple(0 for _ in shape))
    f32 = jax.ShapeDtypeStruct((t, D), F32)
    return pl.pallas_call(
        body, grid=(t // tm,), name="rglru_fwd",
        in_specs=[_row_spec(tm, D), prev, _row_spec(tm, D), full((4, D)), _vec_spec(D), full((8, 128, 128)), _vec_spec(D),
                  full((8, 128, 128)), _vec_spec(D), _vec_spec(D)],
        out_specs=[_row_spec(tm, D)] * 5,
        out_shape=[f32, jax.ShapeDtypeStruct((t, D), BF16), f32, f32, f32],
        scratch_shapes=[pltpu.VMEM((tm + 8, D), F32), pltpu.VMEM((tm, D), F32), pltpu.VMEM((8, D), F32)],
        compiler_params=_params("arbitrary"),
    )(xr, xr, xg, conv_w, conv_b, wa2, ba, wx2, bx, lam)


def _bias_fwd(table_t, onehot_t):
    def body(t_ref, e_ref, o_ref):
        o_ref[...] = jnp.dot(t_ref[...], e_ref[...], preferred_element_type=F32, precision=lax.Precision.HIGHEST)

    return pl.pallas_call(body, out_shape=jax.ShapeDtypeStruct((N_HEADS, CHUNK * KB), F32), name="bias_fwd",
                          compiler_params=_params())(table_t, onehot_t)


def _bias_bwd(dbias_flat, onehot_t):
    def body(d_ref, e_ref, o_ref):
        o_ref[...] = lax.dot_general(d_ref[...], e_ref[...], (((1,), (1,)), ((), ())), preferred_element_type=F32,
                                     precision=lax.Precision.HIGHEST)

    return pl.pallas_call(body, out_shape=jax.ShapeDtypeStruct((N_HEADS, N_BUCKETS), F32), name="bias_bwd",
                          compiler_params=_params())(dbias_flat, onehot_t)


def _attn_probs(qh, kg, bias, sink, valid):
    s = _nt(qh, kg) * (HEAD_DIM ** -0.5) + bias
    s = jnp.where(valid, s, NEG_INF)
    m = jnp.maximum(jnp.max(s, axis=-1, keepdims=True), sink)
    e = jnp.exp(s - m)
    es = jnp.exp(sink - m)
    inv = 1.0 / (jnp.sum(e, axis=-1, keepdims=True) + es)
    return e * inv, es * inv


def _valid_keys(c):
    return lax.broadcasted_iota(jnp.int32, (CHUNK, KB), 1) + c * CHUNK >= PAD_KEYS


def _attn_fwd(sinks, q, kp, vp, bias):
    t = q.shape[0]

    def body(sink_ref, q_ref, kp_ref, vp_ref, bias_ref, o_ref):
        c = pl.program_id(0)
        st = pl.multiple_of(c * CHUNK, CHUNK)
        kw = kp_ref[pl.ds(st, KB), :]
        vw = vp_ref[pl.ds(st, KB), :]
        valid = _valid_keys(c)
        outs = []
        for h in range(N_HEADS):
            g = h // 4
            gs = slice(g * HEAD_DIM, (g + 1) * HEAD_DIM)
            p, _ = _attn_probs(q_ref[:, h * HEAD_DIM:(h + 1) * HEAD_DIM], kw[:, gs], bias_ref[h], sink_ref[h], valid)
            outs.append(_nn(p.astype(BF16), vw[:, gs]))
        o_ref[...] = jnp.concatenate(outs, axis=1).astype(BF16)

    return pl.pallas_call(
        body, grid=(t // CHUNK,), name="attn_fwd",
        in_specs=[pl.BlockSpec(memory_space=pltpu.SMEM), _row_spec(CHUNK, D), _WHOLE, _WHOLE, _WHOLE],
        out_specs=_row_spec(CHUNK, D),
        out_shape=jax.ShapeDtypeStruct((t, D), BF16),
        compiler_params=_params("arbitrary"),
    )(sinks, q, kp, vp, bias)


def _merge_fwd(yain, o, gate, h1, w_lru, w_att, w_o, gpost):
    t = h1.shape[0]
    tm = _tile(t)

    def body(ya_ref, o_ref, g_ref, h_ref, wl_ref, wa_ref, wo_ref, gp_ref, h2_ref, mo_ref, mg_ref, ya_out, yb_out):
        ya = _nn(ya_ref[...], wl_ref[...])
        yb = _nn(o_ref[...], wa_ref[...])
        mg = (g_ref[:, 0:D] * ya + g_ref[:, D:2 * D] * yb).astype(BF16)
        mo = _nn(mg, wo_ref[...])
        ya_out[...] = ya.astype(BF16)
        yb_out[...] = yb.astype(BF16)
        mg_ref[...] = mg
        mo_ref[...] = mo
        h2_ref[...] = h_ref[...] + _rms(mo, gp_ref[...])

    f32 = jax.ShapeDtypeStruct((t, D), F32)
    b16 = jax.ShapeDtypeStruct((t, D), BF16)
    return pl.pallas_call(
        body, grid=(t // tm,), name="merge_fwd",
        in_specs=[_row_spec(tm, D), _row_spec(tm, D), _row_spec(tm, 2 * D), _row_spec(tm, D), _WHOLE, _WHOLE, _WHOLE,
                  _vec_spec(D)],
        out_specs=[_row_spec(tm, D)] * 5,
        out_shape=[f32, f32, b16, b16, b16],
        compiler_params=_params("arbitrary"),
    )(yain, o, gate, h1, w_lru, w_att, w_o, gpost)


def _ffn_bwd(dh, x, f, a, b, gpre, gpost, w1g, w3g, w2g, name):
    t = x.shape[0]
    tm = _tile(t)

    def body(dh_ref, x_ref, f_ref, a_ref, b_ref, gpre_ref, gpost_ref, w1_ref, w3_ref, w2_ref,
             dx_ref, n_ref, da_ref, db_ref, df_ref, dgpre_ref, dgpost_ref):
        @pl.when(pl.program_id(0) == 0)
        def _():
            dgpre_ref[...] = jnp.zeros_like(dgpre_ref)
            dgpost_ref[...] = jnp.zeros_like(dgpost_ref)

        dhv = dh_ref[...]
        xv = x_ref[...]
        df, dgp = _rms_bwd(0.5 * dhv, f_ref[...], gpost_ref[...])
        dgpost_ref[...] += dgp
        dfb = df.astype(BF16)
        df_ref[...] = dfb
        n_ref[...] = _rms(xv, gpre_ref[...]).astype(BF16)
        dn = jnp.zeros((tm, D), F32)
        for s in range(NSH):
            av = a_ref[s].astype(F32)
            bv = b_ref[s].astype(F32)
            sg = jax.nn.sigmoid(av)
            dhm = _nt(dfb, w2_ref[s])
            dab = (dhm * bv * (sg * (1.0 + av * (1.0 - sg)))).astype(BF16)
            dbb = (dhm * (av * sg)).astype(BF16)
            da_ref[s] = dab
            db_ref[s] = dbb
            dn = dn + _nt(dab, w1_ref[s]) + _nt(dbb, w3_ref[s])
        dxn, dg = _rms_bwd(dn, xv, gpre_ref[...])
        dgpre_ref[...] += dg
        dx_ref[...] = dhv + dxn

    sh = pl.BlockSpec((NSH, tm, FF_S), lambda i: (0, i, 0))
    act = jax.ShapeDtypeStruct((NSH, t, FF_S), BF16)
    vec = jax.ShapeDtypeStruct((1, D), F32)
    return pl.pallas_call(
        body, grid=(t // tm,), name=name,
        in_specs=[_row_spec(tm, D), _row_spec(tm, D), _row_spec(tm, D), sh, sh, _vec_spec(D), _vec_spec(D), _WHOLE, _WHOLE,
                  _WHOLE],
        out_specs=[_row_spec(tm, D), _row_spec(tm, D), sh, sh, _row_spec(tm, D), _vec_spec(D), _vec_spec(D)],
        out_shape=[jax.ShapeDtypeStruct((t, D), F32), jax.ShapeDtypeStruct((t, D), BF16), act, act,
                   jax.ShapeDtypeStruct((t, D), BF16), vec, vec],
        compiler_params=_params("arbitrary"),
    )(dh, x, f, a, b, gpre, gpost, w1g, w3g, w2g)


def _wgrad(a, b, a_spec, b_spec, out_spec, out_shape, grid, name):
    def body(a_ref, b_ref, o_ref):
        o_ref[...] = _tn(a_ref[...], b_ref[...])

    return pl.pallas_call(body, grid=grid, name=name, in_specs=[a_spec, b_spec], out_specs=out_spec,
                          out_shape=jax.ShapeDtypeStruct(out_shape, F32),
                          compiler_params=_params(*("arbitrary",) * len(grid)))(a, b)


def _wgrad_cols(act, dsh, width, name):
    t = act.shape[0]
    if dsh.ndim == 3:
        b_spec = pl.BlockSpec((None, t, width), lambda s, k: (s, 0, 0))
    else:
        b_spec = pl.BlockSpec((t, width), lambda s, k: (0, s))
    return _wgrad(act, dsh, pl.BlockSpec((t, 512), lambda s, k: (0, k)), b_spec,
                  pl.BlockSpec((None, 512, width), lambda s, k: (s, k, 0)), (NSH, D, width), (NSH, 2), name)


def _wgrad_rows(hm, df, name):
    t = df.shape[0]
    return _wgrad(hm, df, pl.BlockSpec((None, t, FF_S), lambda s, j: (s, 0, 0)), pl.BlockSpec((t, 512), lambda s, j: (0, j)),
                  pl.BlockSpec((None, FF_S, 512), lambda s, j: (s, 0, j)), (NSH, FF_S, D), (NSH, 2), name)


def _wgrad_sq(a, b, name):
    t = a.shape[0]
    return _wgrad(a, b, pl.BlockSpec((t, 512), lambda i, j: (0, i)), pl.BlockSpec((t, 512), lambda i, j: (0, j)),
                  pl.BlockSpec((512, 512), lambda i, j: (i, j)), (D, D), (2, 2), name)


def _mix_bwd1(dh2, mo, gpost, gate, ya, yb, xg, hr, w_o, w_lru, w_att):
    t = dh2.shape[0]
    tm = _tile(t)

    def body(dh_ref, mo_ref, gp_ref, g_ref, ya_ref, yb_ref, xg_ref, hr_ref, wo_ref, wl_ref, wa_ref,
             dmo_ref, dya_ref, dyb_ref, dgate_ref, dhr_ref, dxg_ref, do_ref, dgp_ref, dbg_ref):
        @pl.when(pl.program_id(0) == 0)
        def _():
            dgp_ref[...] = jnp.zeros_like(dgp_ref)
            dbg_ref[...] = jnp.zeros_like(dbg_ref)

        dmo, dgp = _rms_bwd(dh_ref[...], mo_ref[...], gp_ref[...])
        dgp_ref[...] += dgp
        dmob = dmo.astype(BF16)
        dmo_ref[...] = dmob
        dm = _nt(dmob, wo_ref[...])
        g0 = g_ref[:, 0:D]
        g1 = g_ref[:, D:2 * D]
        dyab = (dm * g0).astype(BF16)
        dybb = (dm * g1).astype(BF16)
        dya_ref[...] = dyab
        dyb_ref[...] = dybb
        dg0 = dm * ya_ref[...].astype(F32) * (g0 * (1.0 - g0))
        dg1 = dm * yb_ref[...].astype(F32) * (g1 * (1.0 - g1))
        dgate_ref[:, 0:D] = dg0.astype(BF16)
        dgate_ref[:, D:2 * D] = dg1.astype(BF16)
        dbg_ref[:, 0:D] += jnp.sum(dg0, axis=0, keepdims=True)
        dbg_ref[:, D:2 * D] += jnp.sum(dg1, axis=0, keepdims=True)
        dyain = _nt(dyab, wl_ref[...])
        do_ref[...] = _nt(dybb, wa_ref[...]).astype(BF16)
        xgv = xg_ref[...]
        dhr_ref[...] = dyain * _gelu(xgv)
        dxg_ref[...] = (dyain * hr_ref[...] * _gelu_grad(xgv)).astype(BF16)

    b16 = jax.ShapeDtypeStruct((t, D), BF16)
    return pl.pallas_call(
        body, grid=(t // tm,), name="mix_bwd1",
        in_specs=[_row_spec(tm, D), _row_spec(tm, D), _vec_spec(D), _row_spec(tm, 2 * D), _row_spec(tm, D), _row_spec(tm, D),
                  _row_spec(tm, D), _row_spec(tm, D), _WHOLE, _WHOLE, _WHOLE],
        out_specs=[_row_spec(tm, D), _row_spec(tm, D), _row_spec(tm, D), _row_spec(tm, 2 * D), _row_spec(tm, D),
                   _row_spec(tm, D), _row_spec(tm, D), _vec_spec(D), _vec_spec(2 * D)],
        out_shape=[b16, b16, b16, jax.ShapeDtypeStruct((t, 2 * D), BF16), jax.ShapeDtypeStruct((t, D), F32), b16, b16,
                   jax.ShapeDtypeStruct((1, D), F32), jax.ShapeDtypeStruct((1, 2 * D), F32)],
        compiler_params=_params("arbitrary"),
    )(dh2, mo, gpost, gate, ya, yb, xg, hr, w_o, w_lru, w_att)


def _rglru_bwd(dhr, hr, xc, r, ig, xr, conv_w, wa2, wx2, lam):
    t = dhr.shape[0]
    tm = _tile(t)
    nb8 = tm // 8
    nt = t // tm

    def body(dhr_ref, hr_ref, hrp_ref, xc_ref, r_ref, ig_ref, xr_ref, xrp_ref, cw_ref, wa_ref, wx_ref, lam_ref,
             dxr_ref, dwa_ref, dwx_ref, dba_ref, dbx_ref, dlam_ref, dcw_ref, dcb_ref,
             ext_h, ext_x, ext_d, a_sc, g_sc, c_sc, nxt_sc):
        i = pl.program_id(0)
        first_tile = i == nt - 1

        @pl.when(i == 0)
        def _():
            c_sc[...] = jnp.zeros_like(c_sc)
            nxt_sc[...] = jnp.zeros_like(nxt_sc)
            for ref in (dwa_ref, dwx_ref, dba_ref, dbx_ref, dlam_ref, dcw_ref, dcb_ref):
                ref[...] = jnp.zeros_like(ref)

        lamv = lam_ref[...]
        sp = _softplus_neg(lamv)
        rv = r_ref[...]
        igv = ig_ref[...]
        xcv = xc_ref[...]
        a, s = _lru_coeffs(rv, sp)
        a_sc[...] = a

        def blk(jj, c):
            st = pl.multiple_of((nb8 - 1 - jj) * 8, 8)
            d8 = dhr_ref[pl.ds(st, 8), :]
            a8 = a_sc[pl.ds(st, 8), :]
            rows = [None] * 8
            for k in range(7, -1, -1):
                g = d8[k:k + 1, :] + c
                c = a8[k:k + 1, :] * g
                rows[k] = g
            g_sc[pl.ds(st, 8), :] = jnp.concatenate(rows, axis=0)
            return c

        c_sc[0:1, :] = lax.fori_loop(0, nb8, blk, c_sc[0:1, :])
        g = g_sc[...]
        ext_h[0:8, :] = jnp.where(first_tile, 0.0, hrp_ref[...])
        ext_h[8:8 + tm, :] = hr_ref[...]
        hprev = ext_h[pl.ds(7, tm), :]
        d_s = g * (igv * xcv)
        dig = g * s * xcv
        dxc = g * s * igv
        dla = (g * hprev) * a - d_s * ((a * a) / s)
        dr_pre = (dla * (-LRU_C * sp)) * (rv * (1.0 - rv))
        di_pre = dig * (igv * (1.0 - igv))
        dlam_ref[...] += jnp.sum(dla * (LRU_C * rv), axis=0, keepdims=True) * jax.nn.sigmoid(-lamv)
        dba_ref[...] += jnp.sum(dr_pre, axis=0, keepdims=True)
        dbx_ref[...] += jnp.sum(di_pre, axis=0, keepdims=True)
        drb = dr_pre.astype(BF16)
        dib = di_pre.astype(BF16)
        xcb = xcv.astype(BF16)
        ext_d[tm:tm + 8, :] = nxt_sc[...]
        for p in range(8):
            sl = slice(p * 128, (p + 1) * 128)
            ext_d[0:tm, sl] = dxc[:, sl] + _nt(drb[:, sl], wa_ref[p]) + _nt(dib[:, sl], wx_ref[p])
            dwa_ref[p] += _tn(xcb[:, sl], drb[:, sl])
            dwx_ref[p] += _tn(xcb[:, sl], dib[:, sl])
        dxcv = ext_d[0:tm, :]
        nxt_sc[...] = ext_d[0:8, :]
        dcb_ref[...] += jnp.sum(dxcv, axis=0, keepdims=True)
        ext_x[0:8, :] = jnp.where(first_tile, 0.0, xrp_ref[...])
        ext_x[8:8 + tm, :] = xr_ref[...]
        dxr = jnp.zeros((tm, D), F32)
        for tap in range(4):
            dxr = dxr + ext_d[pl.ds(3 - tap, tm), :] * cw_ref[tap:tap + 1, :]
            dcw_ref[tap:tap + 1, :] += jnp.sum(dxcv * ext_x[pl.ds(5 + tap, tm), :], axis=0, keepdims=True)
        dxr_ref[...] = dxr.astype(BF16)

    rev = pl.BlockSpec((tm, D), lambda i: (nt - 1 - i, 0))
    prev = pl.BlockSpec((8, D), lambda i: (jnp.maximum((nt - 1 - i) * nb8 - 1, 0), 0))
    full = lambda shape: pl.BlockSpec(shape, lambda i: tuple(0 for _ in shape))
    vec = jax.ShapeDtypeStruct((1, D), F32)
    blocks = jax.ShapeDtypeStruct((8, 128, 128), F32)
    return pl.pallas_call(
        body, grid=(nt,), name="rglru_bwd",
        in_specs=[rev, rev, prev, rev, rev, rev, rev, prev, full((4, D)), full((8, 128, 128)), full((8, 128, 128)),
                  _vec_spec(D)],
        out_specs=[rev, full((8, 128, 128)), full((8, 128, 128)), _vec_spec(D), _vec_spec(D), _vec_spec(D), full((4, D)),
                   _vec_spec(D)],
        out_shape=[jax.ShapeDtypeStruct((t, D), BF16), blocks, blocks, vec, vec, vec, jax.ShapeDtypeStruct((4, D), F32), vec],
        scratch_shapes=[pltpu.VMEM((tm + 8, D), F32), pltpu.VMEM((tm + 8, D), F32), pltpu.VMEM((tm + 8, D), F32),
                        pltpu.VMEM((tm, D), F32), pltpu.VMEM((tm, D), F32), pltpu.VMEM((8, D), F32), pltpu.VMEM((8, D), F32)],
        compiler_params=_params("arbitrary"),
    )(dhr, hr, hr, xc, r, ig, xr, xr, conv_w, wa2, wx2, lam)


def _attn_bwd(sinks, q, kp, vp, bias, do):
    t = q.shape[0]
    tp = t + PAD_KEYS

    def body(sink_ref, q_ref, kp_ref, vp_ref, bias_ref, do_ref, dq_ref, dk_ref, dv_ref, dbias_ref, ds_ref):
        c = pl.program_id(0)

        @pl.when(c == 0)
        def _():
            for ref in (dk_ref, dv_ref, dbias_ref, ds_ref):
                ref[...] = jnp.zeros_like(ref)

        st = pl.multiple_of(c * CHUNK, CHUNK)
        kw = kp_ref[pl.ds(st, KB), :]
        vw = vp_ref[pl.ds(st, KB), :]
        valid = _valid_keys(c)
        dqs, dks, dvs = [], [], []
        for g in range(4):
            gs = slice(g * HEAD_DIM, (g + 1) * HEAD_DIM)
            kg = kw[:, gs]
            vg = vw[:, gs]
            dkg = jnp.zeros((KB, HEAD_DIM), F32)
            dvg = jnp.zeros((KB, HEAD_DIM), F32)
            for h in range(4 * g, 4 * g + 4):
                hs = slice(h * HEAD_DIM, (h + 1) * HEAD_DIM)
                qh = q_ref[:, hs]
                doh = do_ref[:, hs]
                p, ps = _attn_probs(qh, kg, bias_ref[h], sink_ref[h], valid)
                dp = _nt(doh, vg)
                delta = jnp.sum(p * dp, axis=-1, keepdims=True)
                dsc = p * (dp - delta)
                dbias_ref[h] += dsc
                dsink = jnp.sum(-(ps * delta), axis=0, keepdims=True)
                ds_ref[h:h + 1, :] += jnp.broadcast_to(dsink, (1, 128))
                dsb = (dsc * (HEAD_DIM ** -0.5)).astype(BF16)
                dqs.append(_nn(dsb, kg))
                dkg = dkg + _tn(dsb, qh)
                dvg = dvg + _tn(p.astype(BF16), doh)
            dks.append(dkg)
            dvs.append(dvg)
        dq_ref[...] = jnp.concatenate(dqs, axis=1).astype(BF16)
        dk_ref[pl.ds(st, KB), :] += jnp.concatenate(dks, axis=1)
        dv_ref[pl.ds(st, KB), :] += jnp.concatenate(dvs, axis=1)

    full = lambda shape: pl.BlockSpec(shape, lambda i: tuple(0 for _ in shape))
    return pl.pallas_call(
        body, grid=(t // CHUNK,), name="attn_bwd",
        in_specs=[pl.BlockSpec(memory_space=pltpu.SMEM), _row_spec(CHUNK, D), _WHOLE, _WHOLE, _WHOLE, _row_spec(CHUNK, D)],
        out_specs=[_row_spec(CHUNK, D), full((tp, KV_W)), full((tp, KV_W)), full((N_HEADS, CHUNK, KB)), full((N_HEADS, 128))],
        out_shape=[jax.ShapeDtypeStruct((t, D), BF16), jax.ShapeDtypeStruct((tp, KV_W), F32),
                   jax.ShapeDtypeStruct((tp, KV_W), F32), jax.ShapeDtypeStruct((N_HEADS, CHUNK, KB), F32),
                   jax.ShapeDtypeStruct((N_HEADS, 128), F32)],
        compiler_params=_params("arbitrary"),
    )(sinks, q, kp, vp, bias, do)


def _mix_bwd2(dproj, dgate, h1, dh2, gmix, w_in_g, w_gate_g):
    t = h1.shape[0]
    tm = _tile(t)

    def body(dp_ref, dg_ref, h_ref, dh_ref, g_ref, win_ref, wg_ref, dh1_ref, dgm_ref):
        @pl.when(pl.program_id(0) == 0)
        def _():
            dgm_ref[...] = jnp.zeros_like(dgm_ref)

        du = jnp.zeros((tm, D), F32)
        for s in range(NSH):
            du = du + _nt(dp_ref[:, s * IN_S:(s + 1) * IN_S], win_ref[s])
            du = du + _nt(dg_ref[:, s * GATE_S:(s + 1) * GATE_S], wg_ref[s])
        dxn, dg = _rms_bwd(du, h_ref[...], g_ref[...])
        dgm_ref[...] += dg
        dh1_ref[...] = dh_ref[...] + dxn

    return pl.pallas_call(
        body, grid=(t // tm,), name="mix_bwd2",
        in_specs=[_row_spec(tm, NSH * IN_S), _row_spec(tm, 2 * D), _row_spec(tm, D), _row_spec(tm, D), _vec_spec(D), _WHOLE,
                  _WHOLE],
        out_specs=[_row_spec(tm, D), _vec_spec(D)],
        out_shape=[jax.ShapeDtypeStruct((t, D), F32), jax.ShapeDtypeStruct((1, D), F32)],
        compiler_params=_params("arbitrary"),
    )(dproj, dgate, h1, dh2, gmix, w_in_g, w_gate_g)


def _band_onehot():
    nb = N_BUCKETS // 2
    max_exact = nb // 2
    rel = jnp.arange(KB)[None, :] - PAD_KEYS - jnp.arange(CHUNK)[:, None]
    ret = jnp.where(rel > 0, nb, 0)
    n = jnp.abs(rel)
    nf = jnp.maximum(n, 1).astype(jnp.float32)
    large = max_exact + (jnp.log(nf / max_exact) / math.log(128 / max_exact) * (nb - max_exact)).astype(jnp.int32)
    large = jnp.minimum(large, nb - 1)
    buckets = (ret + jnp.where(n < max_exact, n, large)).reshape(1, CHUNK * KB)
    return (buckets == jnp.arange(N_BUCKETS)[:, None]).astype(F32)


def _pair_blocks(w):
    z = jnp.zeros((8, 128, 128), w.dtype)
    return z.at[:, 0:64, 0:64].set(w[0::2]).at[:, 64:128, 64:128].set(w[1::2])


def _unpair_blocks(w2):
    return jnp.stack([w2[:, 0:64, 0:64], w2[:, 64:128, 64:128]], axis=1).reshape(16, 64, 64)


def _local_step(x, target, wg, sm):
    row = lambda v: v.reshape(1, -1)
    onehot_t = _band_onehot()
    bias = _bias_fwd(sm["rel_bias"].T, onehot_t).reshape(N_HEADS, CHUNK, KB)
    wa2 = _pair_blocks(sm["rg_a_w"]).astype(BF16)
    wx2 = _pair_blocks(sm["rg_x_w"]).astype(BF16)
    sinks = sm["attn_sinks"].reshape(N_HEADS)
    w_lru = wg["w_lru_out"].reshape(D, D)
    w_att = wg["w_attn_out"].reshape(D, D)
    w_o = wg["w_o"].reshape(D, D)

    h1, a1, b1, hm1, f1 = _ffn_fwd(x, row(sm["ffn1_pre_g"]), wg["ffn1_w1"], wg["ffn1_w3"], wg["ffn1_w2"],
                                   row(sm["ffn1_post_g"]), "ffn1_fwd")
    u, q, k, v, xr, xg, gate = _mix_proj(h1, row(sm["mix_pre_g"]), wg["w_in"], wg["w_gate"], row(sm["b_gate"]))
    hr, yain, xc, r, ig = _rglru_fwd(xr, xg, sm["conv_w"], row(sm["conv_b"]), wa2, row(sm["rg_a_b"]), wx2,
                                     row(sm["rg_x_b"]), row(sm["lru_lambda"]))
    kp = jnp.pad(k, ((PAD_KEYS, 0), (0, 0)))
    vp = jnp.pad(v, ((PAD_KEYS, 0), (0, 0)))
    o = _attn_fwd(sinks, q, kp, vp, bias)
    h2, mo, merged, ya, yb = _merge_fwd(yain, o, gate, h1, w_lru, w_att, w_o, row(sm["mix_post_g"]))
    y, a2, b2, hm2, f2 = _ffn_fwd(h2, row(sm["ffn2_pre_g"]), wg["ffn2_w1"], wg["ffn2_w3"], wg["ffn2_w2"],
                                  row(sm["ffn2_post_g"]), "ffn2_fwd")
    dy, sq = _loss_dy(y, target)

    big, small = {}, {}
    dh2, n2, da2, db2, df2, small["ffn2_pre_g"], small["ffn2_post_g"] = _ffn_bwd(
        dy, h2, f2, a2, b2, row(sm["ffn2_pre_g"]), row(sm["ffn2_post_g"]), wg["ffn2_w1"], wg["ffn2_w3"], wg["ffn2_w2"],
        "ffn2_bwd")
    big["ffn2_w1"] = _wgrad_cols(n2, da2, FF_S, "dw_ffn2_w1")
    big["ffn2_w3"] = _wgrad_cols(n2, db2, FF_S, "dw_ffn2_w3")
    big["ffn2_w2"] = _wgrad_rows(hm2, df2, "dw_ffn2_w2")
    dmo, dya, dyb, dgate, dhr, dxg, do, small["mix_post_g"], small["b_gate"] = _mix_bwd1(
        dh2, mo, row(sm["mix_post_g"]), gate, ya, yb, xg, hr, w_o, w_lru, w_att)
    big["w_o"] = _wgrad_sq(merged, dmo, "dw_w_o").reshape(NSH, D // NSH, D)
    big["w_lru_out"] = _wgrad_sq(yain, dya, "dw_w_lru_out").reshape(NSH, D // NSH, D)
    big["w_attn_out"] = _wgrad_sq(o, dyb, "dw_w_attn_out").reshape(NSH, D // NSH, D)
    (dxr, dwa2, dwx2, small["rg_a_b"], small["rg_x_b"], small["lru_lambda"], small["conv_w"], small["conv_b"]) = _rglru_bwd(
        dhr, hr, xc, r, ig, xr, sm["conv_w"], wa2, wx2, row(sm["lru_lambda"]))
    small["rg_a_w"] = _unpair_blocks(dwa2)
    small["rg_x_w"] = _unpair_blocks(dwx2)
    dq, dkp, dvp, dbias, dsinks = _attn_bwd(sinks, q, kp, vp, bias, do)
    small["attn_sinks"] = dsinks[:, 0]
    small["rel_bias"] = _bias_bwd(dbias.reshape(N_HEADS, CHUNK * KB), onehot_t).T
    dproj = jnp.concatenate([dq, dkp[PAD_KEYS:].astype(BF16), dvp[PAD_KEYS:].astype(BF16), dxr, dxg], axis=1)
    big["w_in"] = _wgrad_cols(u, dproj, IN_S, "dw_w_in")
    big["w_gate"] = _wgrad_cols(u, dgate, GATE_S, "dw_w_gate")
    dh1, small["mix_pre_g"] = _mix_bwd2(dproj, dgate, h1, dh2, row(sm["mix_pre_g"]), wg["w_in"], wg["w_gate"])
    dx, n1, da1, db1, df1, small["ffn1_pre_g"], small["ffn1_post_g"] = _ffn_bwd(
        dh1, x, f1, a1, b1, row(sm["ffn1_pre_g"]), row(sm["ffn1_post_g"]), wg["ffn1_w1"], wg["ffn1_w3"], wg["ffn1_w2"],
        "ffn1_bwd")
    big["ffn1_w1"] = _wgrad_cols(n1, da1, FF_S, "dw_ffn1_w1")
    big["ffn1_w3"] = _wgrad_cols(n1, db1, FF_S, "dw_ffn1_w3")
    big["ffn1_w2"] = _wgrad_rows(hm1, df1, "dw_ffn1_w2")
    return sq, dx, big, small


_ANY = pl.BlockSpec(memory_space=pl.ANY)


def _place():
    return lax.axis_index("x"), lax.axis_index("y"), lax.axis_index("c")


def _other_chips(x, y):
    return [(1 - x, y), (x, 1 - y), (1 - x, 1 - y)]


def _cast_bf16(w, name):
    rows = w.shape[0] // 4

    def body(w_ref, o_ref):
        o_ref[...] = w_ref[...].astype(BF16)

    spec = pl.BlockSpec((rows, w.shape[1]), lambda i: (i, 0))
    return pl.pallas_call(body, grid=(4,), in_specs=[spec], out_specs=spec, name=name,
                          out_shape=jax.ShapeDtypeStruct(w.shape, BF16), compiler_params=_params("arbitrary"))(w)


def _all_gather_chips(shards):
    n = len(shards)

    def body(*refs):
        ins, outs = refs[:n], refs[n:2 * n]
        local_sem, send_sems, recv_sems = refs[2 * n:]
        x, y, c = _place()
        me = 2 * x + y
        chips = _other_chips(x, y)
        for w in range(n):
            pltpu.make_async_copy(ins[w], outs[w].at[me], local_sem.at[w]).start()
            for j, (px, py) in enumerate(chips):
                pltpu.make_async_remote_copy(src_ref=ins[w], dst_ref=outs[w].at[me], send_sem=send_sems.at[3 * w + j],
                                             recv_sem=recv_sems.at[3 * w + j], device_id=(px, py, c),
                                             device_id_type=MESH).start()
        for w in range(n):
            pltpu.make_async_copy(ins[w], outs[w].at[me], local_sem.at[w]).wait()
            for j, (px, py) in enumerate(chips):
                pltpu.make_async_remote_copy(src_ref=ins[w], dst_ref=outs[w].at[2 * px + py], send_sem=send_sems.at[3 * w + j],
                                             recv_sem=recv_sems.at[3 * w + j], device_id=(px, py, c),
                                             device_id_type=MESH).wait()

    return pl.pallas_call(
        body, name="gather_weights", in_specs=[_ANY] * n, out_specs=[_ANY] * n,
        out_shape=[jax.ShapeDtypeStruct((NSH,) + s.shape, s.dtype) for s in shards],
        scratch_shapes=[pltpu.SemaphoreType.DMA((n,)), pltpu.SemaphoreType.DMA((3 * n,)), pltpu.SemaphoreType.DMA((3 * n,))],
        compiler_params=pltpu.CompilerParams(has_side_effects=True),
    )(*shards)


def _swap_halves(grads):
    n = len(grads)

    def body(*refs):
        ins, outs = refs[:n], refs[n:2 * n]
        send_sems, recv_sems = refs[2 * n:]
        x, y, c = _place()
        copies = []
        for w in range(n):
            rh = ins[w].shape[1] // 2
            theirs = ins[w].at[:, pl.ds(pl.multiple_of((1 - c) * rh, 8), rh), :]
            copies.append(pltpu.make_async_remote_copy(src_ref=theirs, dst_ref=outs[w], send_sem=send_sems.at[w],
                                                       recv_sem=recv_sems.at[w], device_id=(x, y, 1 - c), device_id_type=MESH))
            copies[-1].start()
        for cp in copies:
            cp.wait()

    return pl.pallas_call(
        body, name="swap_halves", in_specs=[_ANY] * n, out_specs=[_ANY] * n,
        out_shape=[jax.ShapeDtypeStruct((NSH, g.shape[1] // 2, g.shape[2]), F32) for g in grads],
        scratch_shapes=[pltpu.SemaphoreType.DMA((n,)), pltpu.SemaphoreType.DMA((n,))],
        compiler_params=pltpu.CompilerParams(has_side_effects=True),
    )(*grads)


def _chip_sum(g, got, name):
    _, r, cc = g.shape
    rh = r // 2

    def body(g_ref, got_ref, hb_ref, own_ref):
        x, y, c = _place()
        s = pl.program_id(0)
        h = g_ref[pl.ds(pl.multiple_of(c * rh, 8), rh), :] + got_ref[...]
        hb_ref[...] = h.astype(BF16)

        @pl.when(s == 2 * x + y)
        def _():
            own_ref[...] = h

    return pl.pallas_call(
        body, grid=(NSH,), name=name,
        in_specs=[pl.BlockSpec((None, r, cc), lambda s: (s, 0, 0)), pl.BlockSpec((None, rh, cc), lambda s: (s, 0, 0))],
        out_specs=[pl.BlockSpec((None, rh, cc), lambda s: (s, 0, 0)), pl.BlockSpec((rh, cc), lambda s: (0, 0))],
        out_shape=[jax.ShapeDtypeStruct((NSH, rh, cc), BF16), jax.ShapeDtypeStruct((rh, cc), F32)],
        compiler_params=_params("arbitrary"),
    )(g, got)


def _send_to_owners(sums):
    n = len(sums)

    def body(*refs):
        ins, outs = refs[:n], refs[n:2 * n]
        send_sems, recv_sems = refs[2 * n:]
        x, y, c = _place()
        copies = []
        for w in range(n):
            for j, (px, py) in enumerate(_other_chips(x, y)):
                copies.append(pltpu.make_async_remote_copy(src_ref=ins[w].at[2 * px + py], dst_ref=outs[w].at[j],
                                                           send_sem=send_sems.at[3 * w + j], recv_sem=recv_sems.at[3 * w + j],
                                                           device_id=(px, py, c), device_id_type=MESH))
                copies[-1].start()
        for cp in copies:
            cp.wait()

    return pl.pallas_call(
        body, name="send_to_owners", in_specs=[_ANY] * n, out_specs=[_ANY] * n,
        out_shape=[jax.ShapeDtypeStruct((3,) + s.shape[1:], BF16) for s in sums],
        scratch_shapes=[pltpu.SemaphoreType.DMA((3 * n,)), pltpu.SemaphoreType.DMA((3 * n,))],
        compiler_params=pltpu.CompilerParams(has_side_effects=True),
    )(*sums)


def _owner_sum(own, got, name):
    rh, cc = own.shape
    rows = rh // 2

    def body(own_ref, got_ref, o_ref):
        o_ref[...] = ((own_ref[...] + got_ref[0].astype(F32)) + got_ref[1].astype(F32)) + got_ref[2].astype(F32)

    return pl.pallas_call(
        body, grid=(2,), name=name,
        in_specs=[pl.BlockSpec((rows, cc), lambda i: (i, 0)), pl.BlockSpec((3, rows, cc), lambda i: (0, i, 0))],
        out_specs=pl.BlockSpec((rows, cc), lambda i: (i, 0)),
        out_shape=jax.ShapeDtypeStruct((rh, cc), F32), compiler_params=_params("arbitrary"),
    )(own, got)


def _join_halves(halves):
    n = len(halves)

    def body(*refs):
        ins, outs = refs[:n], refs[n:2 * n]
        local_sem, send_sems, recv_sems = refs[2 * n:]
        x, y, c = _place()
        for w in range(n):
            rh = ins[w].shape[0]
            mine = outs[w].at[pl.ds(pl.multiple_of(c * rh, 8), rh), :]
            pltpu.make_async_copy(ins[w], mine, local_sem.at[w]).start()
            pltpu.make_async_remote_copy(src_ref=ins[w], dst_ref=mine, send_sem=send_sems.at[w], recv_sem=recv_sems.at[w],
                                         device_id=(x, y, 1 - c), device_id_type=MESH).start()
        for w in range(n):
            rh = ins[w].shape[0]
            mine = outs[w].at[pl.ds(pl.multiple_of(c * rh, 8), rh), :]
            theirs = outs[w].at[pl.ds(pl.multiple_of((1 - c) * rh, 8), rh), :]
            pltpu.make_async_copy(ins[w], mine, local_sem.at[w]).wait()
            pltpu.make_async_remote_copy(src_ref=ins[w], dst_ref=theirs, send_sem=send_sems.at[w], recv_sem=recv_sems.at[w],
                                         device_id=(x, y, 1 - c), device_id_type=MESH).wait()

    return pl.pallas_call(
        body, name="join_halves", in_specs=[_ANY] * n, out_specs=[_ANY] * n,
        out_shape=[jax.ShapeDtypeStruct((2 * h.shape[0], h.shape[1]), F32) for h in halves],
        scratch_shapes=[pltpu.SemaphoreType.DMA((n,)), pltpu.SemaphoreType.DMA((n,)), pltpu.SemaphoreType.DMA((n,))],
        compiler_params=pltpu.CompilerParams(has_side_effects=True),
    )(*halves)


def _all_reduce_small(part):
    def body(p_ref, o_ref, rbuf, send1, recv1, send2, recv2):
        x, y, c = _place()
        me = 4 * x + 2 * y + c
        peers = []
        for k in range(1, 8):
            px, py, pc = x ^ ((k >> 2) & 1), y ^ ((k >> 1) & 1), c ^ (k & 1)
            peers.append((k, (px, py, pc), 4 * px + 2 * py + pc))

        def rows(d):
            return pl.ds(pl.multiple_of(d * SMALL_SLICE, 8), SMALL_SLICE)

        first = [pltpu.make_async_remote_copy(src_ref=p_ref.at[rows(idx), :], dst_ref=rbuf.at[me], send_sem=send1.at[k],
                                              recv_sem=recv1.at[k], device_id=dev, device_id_type=MESH)
                 for k, dev, idx in peers]
        for cp in first:
            cp.start()
        rbuf[me] = p_ref[rows(me), :]
        for k, dev, idx in peers:
            pltpu.make_async_remote_copy(src_ref=p_ref.at[rows(idx), :], dst_ref=rbuf.at[idx], send_sem=send1.at[k],
                                         recv_sem=recv1.at[k], device_id=dev, device_id_type=MESH).wait_recv()
        acc = rbuf[0]
        for d in range(1, 8):
            acc = acc + rbuf[d]
        o_ref[rows(me), :] = acc
        second = [pltpu.make_async_remote_copy(src_ref=o_ref.at[rows(me), :], dst_ref=o_ref.at[rows(me), :],
                                               send_sem=send2.at[k], recv_sem=recv2.at[k], device_id=dev, device_id_type=MESH)
                  for k, dev, idx in peers]
        for cp in second:
            cp.start()
        for k, dev, idx in peers:
            pltpu.make_async_remote_copy(src_ref=o_ref.at[rows(me), :], dst_ref=o_ref.at[rows(idx), :], send_sem=send2.at[k],
                                         recv_sem=recv2.at[k], device_id=dev, device_id_type=MESH).wait_recv()
        for cp in first + second:
            cp.wait_send()

    return pl.pallas_call(
        body, name="all_reduce_small", in_specs=[_WHOLE], out_specs=_WHOLE,
        out_shape=jax.ShapeDtypeStruct((SMALL_ROWS, 128), F32),
        scratch_shapes=[pltpu.VMEM((8, SMALL_SLICE, 128), F32)] + [pltpu.SemaphoreType.DMA((8,))] * 4,
        compiler_params=pltpu.CompilerParams(has_side_effects=True),
    )(part)


def _adamw(w, g, m, v, name):
    rows = w.shape[0] // 4

    def body(w_ref, g_ref, m_ref, v_ref, d_ref, nm_ref, nv_ref):
        gv = g_ref[...]
        nm = ADAM_B1 * m_ref[...] + (1.0 - ADAM_B1) * gv
        nv = ADAM_B2 * v_ref[...] + (1.0 - ADAM_B2) * (gv * gv)
        m_hat = nm / (1.0 - ADAM_B1 ** ADAM_STEP)
        v_hat = nv / (1.0 - ADAM_B2 ** ADAM_STEP)
        d_ref[...] = -ADAM_LR * (m_hat / (jnp.sqrt(v_hat) + ADAM_EPS) + ADAM_WD * w_ref[...])
        nm_ref[...] = nm
        nv_ref[...] = nv

    spec = pl.BlockSpec((rows, w.shape[1]), lambda i: (i, 0))
    out = jax.ShapeDtypeStruct(w.shape, F32)
    return pl.pallas_call(body, grid=(4,), in_specs=[spec] * 4, out_specs=[spec] * 3, out_shape=[out] * 3, name=name,
                          compiler_params=_params("arbitrary"))(w, g, m, v)


def _pack_small(vals):
    parts = []
    for name, size in SMALL:
        flat = vals[name].reshape(-1).astype(F32)
        parts.append(jnp.pad(flat, (0, size - flat.shape[0])))
    flat = jnp.concatenate(parts)
    return jnp.pad(flat, (0, SMALL_ROWS * 128 - flat.shape[0])).reshape(SMALL_ROWS, 128)


def _unpack_small(packed, shapes):
    flat = packed.reshape(-1)
    out, off = {}, 0
    for name, size in SMALL:
        n = math.prod(shapes[name])
        out[name] = flat[off:off + n].reshape(shapes[name])
        off += size
    return out


def kernel(x, ffn1_pre_g, ffn1_w1, ffn1_w3, ffn1_w2, ffn1_post_g, mix_pre_g, w_in, conv_w, conv_b, rg_a_w, rg_a_b, rg_x_w, rg_x_b, lru_lambda, w_lru_out, attn_sinks, rel_bias, w_attn_out, w_gate, b_gate, w_o, mix_post_g, ffn2_pre_g, ffn2_w1, ffn2_w3, ffn2_w2, ffn2_post_g, loss_target, m_ffn1_pre_g, m_ffn1_w1, m_ffn1_w3, m_ffn1_w2, m_ffn1_post_g, m_mix_pre_g, m_w_in, m_conv_w, m_conv_b, m_rg_a_w, m_rg_a_b, m_rg_x_w, m_rg_x_b, m_lru_lambda, m_w_lru_out, m_attn_sinks, m_rel_bias, m_w_attn_out, m_w_gate, m_b_gate, m_w_o, m_mix_post_g, m_ffn2_pre_g, m_ffn2_w1, m_ffn2_w3, m_ffn2_w2, m_ffn2_post_g, v_ffn1_pre_g, v_ffn1_w1, v_ffn1_w3, v_ffn1_w2, v_ffn1_post_g, v_mix_pre_g, v_w_in, v_conv_w, v_conv_b, v_rg_a_w, v_rg_a_b, v_rg_x_w, v_rg_x_b, v_lru_lambda, v_w_lru_out, v_attn_sinks, v_rel_bias, v_w_attn_out, v_w_gate, v_b_gate, v_w_o, v_mix_post_g, v_ffn2_pre_g, v_ffn2_w1, v_ffn2_w3, v_ffn2_w2, v_ffn2_post_g):
    given = dict(locals())
    chip = 2 * lax.axis_index("x") + lax.axis_index("y")

    shards = [_cast_bf16(given[n][0], "cast_" + n) for n in BIG] + [given["conv_w"][0]]
    gathered = _all_gather_chips(shards)
    wg = dict(zip(BIG, gathered[:-1]))
    small_shapes = {n: given[n].shape for n, _ in SMALL}
    small_shapes["conv_w"] = (1, 4, D)
    sm = {n: (given[n][0] if given[n].shape[0] == 1 and n != "rel_bias" else given[n]) for n, _ in SMALL}
    sm["conv_w"] = jnp.transpose(gathered[-1], (1, 0, 2)).reshape(4, D)

    sq, dx, big, small = _local_step(x[0], loss_target[0], wg, sm)
    loss = lax.psum(sq[0, 0] * (0.5 / D), ("x", "y", "c"))

    full = [big[n] for n in BIG]
    from_sibling = _swap_halves(full)
    sums = [_chip_sum(g, got, "chip_sum_" + n) for n, g, got in zip(BIG, full, from_sibling)]
    from_chips = _send_to_owners([s[0] for s in sums])
    halves = [_owner_sum(s[1], got, "owner_sum_" + n) for n, s, got in zip(BIG, sums, from_chips)]
    grads = dict(zip(BIG, _join_halves(halves)))
    small_g = _unpack_small(_all_reduce_small(_pack_small(small)), small_shapes)

    delta, new_m, new_v = {}, {}, {}
    for n in BIG:
        delta[n], new_m[n], new_v[n] = (r[None] for r in _adamw(given[n][0], grads[n], given["m_" + n][0], given["v_" + n][0],
                                                                 "adamw_" + n))
        grads[n] = grads[n][None]

    def widen(a):
        return lax.dynamic_update_slice(jnp.zeros((1, 4, D), F32), a, (0, 0, chip * (D // NSH)))

    packed = [_pack_small({n: (widen(given[pre + n]) if n == "conv_w" else given[pre + n]) for n, _ in SMALL})
              for pre in ("", "m_", "v_")]
    packed_g = _pack_small(small_g)
    rows = SMALL_ROWS // 4
    outs = _adamw(packed[0], packed_g, packed[1], packed[2], "adamw_small")
    for dst, arr in zip((delta, new_m, new_v), outs):
        dst.update(_unpack_small(arr, small_shapes))
    small_out = dict(small_g)
    for d in (small_out, delta, new_m, new_v):
        d["conv_w"] = lax.dynamic_slice(d["conv_w"], (0, 0, chip * (D // NSH)), (1, 4, D // NSH))
    grads.update(small_out)
    return (loss, dx[None], *[grads[n] for n in WEIGHTS], *[delta[n] for n in WEIGHTS], *[new_m[n] for n in WEIGHTS],
            *[new_v[n] for n in WEIGHTS])
```

```python
import functools
import math

import jax
import jax.numpy as jnp
from jax import lax
from jax.experimental import pallas as pl
from jax.experimental.pallas import tpu as pltpu

F32, BF16 = jnp.float32, jnp.bfloat16
D = 1024
NSH = 4
FF_S = 704
IN_S = 896
GATE_S = 512
KV_W = 256
CHUNK = 64
KB = 192
N_HEADS = 16
HEAD_DIM = 64
N_BUCKETS = 32
KP = 256
PAD_KEYS = 128
RMS_EPS = 1e-6
NEG_INF = -1e30
LRU_C = 8.0
TM = 256
VMEM_LIMIT = 56 * 1024 * 1024
ADAM_LR, ADAM_B1, ADAM_B2, ADAM_EPS, ADAM_WD, ADAM_STEP = 0.001, 0.9, 0.999, 1e-08, 0.01, 10
SMALL_ROWS = 1216
SMALL_SLICE = SMALL_ROWS // 8
MESH = pl.DeviceIdType.MESH

BIG = ["ffn1_w1", "ffn1_w3", "ffn1_w2", "w_in", "w_lru_out", "w_attn_out", "w_gate", "w_o", "ffn2_w1", "ffn2_w3", "ffn2_w2"]
SMALL = [("ffn1_pre_g", 1024), ("ffn1_post_g", 1024), ("mix_pre_g", 1024), ("conv_w", 4096), ("conv_b", 1024),
         ("rg_a_w", 65536), ("rg_a_b", 1024), ("rg_x_w", 65536), ("rg_x_b", 1024), ("lru_lambda", 1024),
         ("attn_sinks", 1024), ("rel_bias", 1024), ("b_gate", 2048), ("mix_post_g", 1024), ("ffn2_pre_g", 1024),
         ("ffn2_post_g", 1024)]
WEIGHTS = ["ffn1_pre_g", "ffn1_w1", "ffn1_w3", "ffn1_w2", "ffn1_post_g", "mix_pre_g", "w_in", "conv_w", "conv_b", "rg_a_w",
           "rg_a_b", "rg_x_w", "rg_x_b", "lru_lambda", "w_lru_out", "attn_sinks", "rel_bias", "w_attn_out", "w_gate", "b_gate",
           "w_o", "mix_post_g", "ffn2_pre_g", "ffn2_w1", "ffn2_w3", "ffn2_w2", "ffn2_post_g"]


def _params(*sem):
    return pltpu.CompilerParams(dimension_semantics=sem or None, vmem_limit_bytes=VMEM_LIMIT)


def _nn(a, b):
    return jnp.dot(a, b, preferred_element_type=F32)


def _nt(a, b):
    return lax.dot_general(a, b, (((1,), (1,)), ((), ())), preferred_element_type=F32)


def _tn(a, b):
    return lax.dot_general(a, b, (((0,), (0,)), ((), ())), preferred_element_type=F32)


def _rms(x, g):
    rstd = lax.rsqrt(jnp.mean(x * x, axis=-1, keepdims=True) + RMS_EPS)
    return (x * rstd) * g


def _rms_bwd(dout, x, g):
    rstd = lax.rsqrt(jnp.mean(x * x, axis=-1, keepdims=True) + RMS_EPS)
    xhat = x * rstd
    dg = jnp.sum(dout * xhat, axis=0, keepdims=True)
    dxhat = dout * g
    dx = rstd * (dxhat - xhat * jnp.mean(dxhat * xhat, axis=-1, keepdims=True))
    return dx, dg


_GELU_K = math.sqrt(2.0 / math.pi)


def _gelu(x):
    return x * (0.5 * (1.0 + jnp.tanh(_GELU_K * (x + 0.044715 * (x * x * x)))))


def _gelu_grad(x):
    t = jnp.tanh(_GELU_K * (x + 0.044715 * (x * x * x)))
    return 0.5 * (1.0 + t) + x * (0.5 * (1.0 - t * t) * (_GELU_K * (1.0 + 3.0 * 0.044715 * (x * x))))


def _softplus_neg(lam):
    z = -lam
    u = jnp.exp(-jnp.abs(z))
    w = 1.0 + u
    log1p_u = jnp.where(w == 1.0, u, jnp.log(w) * (u / (w - 1.0)))
    return jnp.maximum(z, 0.0) + log1p_u


def _lru_coeffs(r, sp):
    log_a = (-LRU_C * r) * sp
    a = jnp.exp(log_a)
    t = jnp.tanh(log_a)
    s = jnp.sqrt(-2.0 * t / (1.0 - t))
    return a, s


def _row_spec(tm, width):
    return pl.BlockSpec((tm, width), lambda i: (i, 0))


def _vec_spec(width):
    return pl.BlockSpec((1, width), lambda i: (0, 0))


_WHOLE = pl.BlockSpec(memory_space=pltpu.VMEM)


def _tile(t):
    return min(TM, t)


def _ffn_fwd(x, gpre, w1g, w3g, w2g, gpost, name):
    t = x.shape[0]
    tm = _tile(t)

    def body(x_ref, gpre_ref, w1_ref, w3_ref, w2_ref, gpost_ref, h_ref, a_ref, b_ref, hm_ref, f_ref):
        xv = x_ref[...]
        nb = _rms(xv, gpre_ref[...]).astype(BF16)
        f = jnp.zeros((tm, D), F32)
        for s in range(NSH):
            a = _nn(nb, w1_ref[s])
            b = _nn(nb, w3_ref[s])
            hmb = ((a * jax.nn.sigmoid(a)) * b).astype(BF16)
            a_ref[s] = a.astype(BF16)
            b_ref[s] = b.astype(BF16)
            hm_ref[s] = hmb
            f = f + _nn(hmb, w2_ref[s])
        f_ref[...] = f
        h_ref[...] = xv + 0.5 * _rms(f, gpost_ref[...])

    sh = pl.BlockSpec((NSH, tm, FF_S), lambda i: (0, i, 0))
    act = jax.ShapeDtypeStruct((NSH, t, FF_S), BF16)
    return pl.pallas_call(
        body, grid=(t // tm,), name=name,
        in_specs=[_row_spec(tm, D), _vec_spec(D), _WHOLE, _WHOLE, _WHOLE, _vec_spec(D)],
        out_specs=[_row_spec(tm, D), sh, sh, sh, _row_spec(tm, D)],
        out_shape=[jax.ShapeDtypeStruct((t, D), F32), act, act, act, jax.ShapeDtypeStruct((t, D), F32)],
        compiler_params=_params("arbitrary"),
    )(x, gpre, w1g, w3g, w2g, gpost)


def _loss_dy(y, target):
    t = y.shape[0]
    tm = _tile(t)

    def body(y_ref, t_ref, dy_ref, l_ref):
        @pl.when(pl.program_id(0) == 0)
        def _():
            l_ref[...] = jnp.zeros_like(l_ref)

        e = y_ref[...] - t_ref[...]
        dy_ref[...] = e * (1.0 / D)
        sq = jnp.sum(jnp.sum(e * e, axis=0, keepdims=True), axis=1, keepdims=True)
        l_ref[...] = l_ref[...] + sq

    return pl.pallas_call(
        body, grid=(t // tm,), name="loss_dy",
        in_specs=[_row_spec(tm, D), _row_spec(tm, D)],
        out_specs=[_row_spec(tm, D), pl.BlockSpec((1, 128), lambda i: (0, 0))],
        out_shape=[jax.ShapeDtypeStruct((t, D), F32), jax.ShapeDtypeStruct((1, 128), F32)],
        compiler_params=_params("arbitrary"),
    )(y, target)


def _mix_proj(h1, gmix, w_in_g, w_gate_g, b_gate):
    t = h1.shape[0]
    tm = _tile(t)

    def body(h_ref, g_ref, win_ref, wg_ref, bg_ref, u_ref, q_ref, k_ref, v_ref, xr_ref, xg_ref, gate_ref):
        ub = _rms(h_ref[...], g_ref[...]).astype(BF16)
        u_ref[...] = ub
        p0 = _nn(ub, win_ref[0])
        q_ref[:, 0:896] = p0.astype(BF16)
        p1 = _nn(ub, win_ref[1])
        q_ref[:, 896:1024] = p1[:, 0:128].astype(BF16)
        k_ref[...] = p1[:, 128:384].astype(BF16)
        v_ref[...] = p1[:, 384:640].astype(BF16)
        xr_ref[:, 0:256] = p1[:, 640:896]
        p2 = _nn(ub, win_ref[2])
        xr_ref[:, 256:1024] = p2[:, 0:768]
        xg_ref[:, 0:128] = p2[:, 768:896]
        xg_ref[:, 128:1024] = _nn(ub, win_ref[3])
        for s in range(NSH):
            sl = slice(s * GATE_S, (s + 1) * GATE_S)
            gate_ref[:, sl] = jax.nn.sigmoid(_nn(ub, wg_ref[s]) + bg_ref[:, sl])

    return pl.pallas_call(
        body, grid=(t // tm,), name="mix_proj",
        in_specs=[_row_spec(tm, D), _vec_spec(D), _WHOLE, _WHOLE, _vec_spec(2 * D)],
        out_specs=[_row_spec(tm, D), _row_spec(tm, D), _row_spec(tm, KV_W), _row_spec(tm, KV_W), _row_spec(tm, D),
                   _row_spec(tm, D), _row_spec(tm, 2 * D)],
        out_shape=[jax.ShapeDtypeStruct((t, D), BF16), jax.ShapeDtypeStruct((t, D), BF16),
                   jax.ShapeDtypeStruct((t, KV_W), BF16), jax.ShapeDtypeStruct((t, KV_W), BF16),
                   jax.ShapeDtypeStruct((t, D), F32), jax.ShapeDtypeStruct((t, D), F32),
                   jax.ShapeDtypeStruct((t, 2 * D), F32)],
        compiler_params=_params("arbitrary"),
    )(h1, gmix, w_in_g, w_gate_g, b_gate)


def _rglru_fwd(xr, xg, conv_w, conv_b, wa2, ba, wx2, bx, lam):
    t = xr.shape[0]
    tm = _tile(t)
    nb8 = tm // 8

    def body(xr_ref, xrp_ref, xg_ref, cw_ref, cb_ref, wa_ref, ba_ref, wx_ref, bx_ref, lam_ref,
             hr_ref, yain_ref, xc_ref, r_ref, ig_ref, ext, a_sc, h_sc):
        i = pl.program_id(0)

        @pl.when(i == 0)
        def _():
            h_sc[...] = jnp.zeros_like(h_sc)

        ext[0:8, :] = jnp.where(i == 0, 0.0, xrp_ref[...])
        ext[8:8 + tm, :] = xr_ref[...]
        xc = jnp.broadcast_to(cb_ref[...], (tm, D))
        for tap in range(4):
            xc = xc + ext[pl.ds(5 + tap, tm), :] * cw_ref[tap:tap + 1, :]
        xc_ref[...] = xc
        xcb = xc.astype(BF16)
        for p in range(8):
            sl = slice(p * 128, (p + 1) * 128)
            r_ref[:, sl] = jax.nn.sigmoid(_nn(xcb[:, sl], wa_ref[p]) + ba_ref[:, sl])
            ig_ref[:, sl] = jax.nn.sigmoid(_nn(xcb[:, sl], wx_ref[p]) + bx_ref[:, sl])
        a, s = _lru_coeffs(r_ref[...], _softplus_neg(lam_ref[...]))
        a_sc[...] = a
        hr_ref[...] = s * (ig_ref[...] * xc)

        def blk(j, h):
            st = pl.multiple_of(j * 8, 8)
            a8 = a_sc[pl.ds(st, 8), :]
            u8 = hr_ref[pl.ds(st, 8), :]
            rows = []
            for k in range(8):
                h = a8[k:k + 1, :] * h + u8[k:k + 1, :]
                rows.append(h)
            hr_ref[pl.ds(st, 8), :] = jnp.concatenate(rows, axis=0)
            return h

        h_sc[0:1, :] = lax.fori_loop(0, nb8, blk, h_sc[0:1, :])
        yain_ref[...] = (hr_ref[...] * _gelu(xg_ref[...])).astype(BF16)

    prev = pl.BlockSpec((8, D), lambda i: (jnp.maximum(i * nb8 - 1, 0), 0))
    full = lambda shape: pl.BlockSpec(shape, lambda i: tuple(0 for _ in shape))
    f32 = jax.ShapeDtypeStruct((t, D), F32)
    return pl.pallas_call(
        body, grid=(t // tm,), name="rglru_fwd",
        in_specs=[_row_spec(tm, D), prev, _row_spec(tm, D), full((4, D)), _vec_spec(D), full((8, 128, 128)), _vec_spec(D),
                  full((8, 128, 128)), _vec_spec(D), _vec_spec(D)],
        out_specs=[_row_spec(tm, D)] * 5,
        out_shape=[f32, jax.ShapeDtypeStruct((t, D), BF16), f32, f32, f32],
        scratch_shapes=[pltpu.VMEM((tm + 8, D), F32), pltpu.VMEM((tm, D), F32), pltpu.VMEM((8, D), F32)],
        compiler_params=_params("arbitrary"),
    )(xr, xr, xg, conv_w, conv_b, wa2, ba, wx2, bx, lam)


def _bias_fwd(table_t, onehot_t):
    def body(t_ref, e_ref, o_ref):
        o_ref[...] = jnp.dot(t_ref[...], e_ref[...], preferred_element_type=F32, precision=lax.Precision.HIGHEST)

    return pl.pallas_call(body, out_shape=jax.ShapeDtypeStruct((N_HEADS, CHUNK * KB), F32), name="bias_fwd",
                          compiler_params=_params())(table_t, onehot_t)


def _bias_bwd(dbias_flat, onehot_t, ds_rows):
    def body(d_ref, e_ref, s_ref, o_ref, so_ref):
        o_ref[...] = lax.dot_general(d_ref[...], e_ref[...], (((1,), (1,)), ((), ())), preferred_element_type=F32,
                                     precision=lax.Precision.HIGHEST)
        so_ref[...] = jnp.zeros_like(so_ref)
        for r in range(4):
            so_ref[:, r:r + 1] = jnp.sum(s_ref[:, r * CHUNK:(r + 1) * CHUNK], axis=1, keepdims=True)

    return pl.pallas_call(body, out_shape=[jax.ShapeDtypeStruct((N_HEADS, N_BUCKETS), F32), jax.ShapeDtypeStruct((8, 128), F32)],
                          name="bias_bwd", compiler_params=_params())(dbias_flat, onehot_t, ds_rows)


def _stack_heads(q):
    return jnp.concatenate(
        [jnp.concatenate([q[:, (4 * g + r) * HEAD_DIM:(4 * g + r + 1) * HEAD_DIM] for g in range(4)], axis=1)
         for r in range(4)], axis=0)


def _unstack_heads(o):
    return jnp.concatenate([o[r * CHUNK:(r + 1) * CHUNK, g * HEAD_DIM:(g + 1) * HEAD_DIM] for g in range(4) for r in range(4)],
                           axis=1)


def _block_diag(w, mask):
    return jnp.concatenate([w] * 4, axis=0) * mask


def _attn_softmax(q_all, kbd, bias_t, sink_rows, c):
    s = _nt(kbd, q_all) * (HEAD_DIM ** -0.5) + bias_t
    j = lax.broadcasted_iota(jnp.int32, (4 * KP, 1), 0) % KP
    s = jnp.where((j < KB) & (j + c * CHUNK >= PAD_KEYS), s, NEG_INF)
    ps, sinks = [], []
    for g in range(4):
        sg = s[g * KP:(g + 1) * KP, :]
        sink = sink_rows[g:g + 1, :]
        m = jnp.maximum(jnp.max(sg, axis=0, keepdims=True), sink)
        e = jnp.exp(sg - m)
        es = jnp.exp(sink - m)
        inv = 1.0 / (jnp.sum(e, axis=0, keepdims=True) + es)
        ps.append(e * inv)
        sinks.append(es * inv)
    return ps, sinks


def _attn_fwd(sink_rows, q, kp, vp, bias_t, mask):
    t = q.shape[0]

    def body(sink_ref, q_ref, kp_ref, vp_ref, bias_ref, mask_ref, o_ref):
        c = pl.program_id(0)
        st = pl.multiple_of(c * CHUNK, CHUNK)
        kbd = _block_diag(kp_ref[pl.ds(st, KP), :], mask_ref[...])
        vbd = _block_diag(vp_ref[pl.ds(st, KP), :], mask_ref[...])
        ps, _ = _attn_softmax(_stack_heads(q_ref[...]), kbd, bias_ref[...], sink_ref[...], c)
        p_t = jnp.concatenate(ps, axis=0).astype(BF16)
        o_ref[...] = _unstack_heads(_tn(p_t, vbd)).astype(BF16)

    return pl.pallas_call(
        body, grid=(t // CHUNK,), name="attn_fwd",
        in_specs=[_WHOLE, _row_spec(CHUNK, D), _WHOLE, _WHOLE, _WHOLE, _WHOLE],
        out_specs=_row_spec(CHUNK, D),
        out_shape=jax.ShapeDtypeStruct((t, D), BF16),
        compiler_params=_params("arbitrary"),
    )(sink_rows, q, kp, vp, bias_t, mask)


def _merge_fwd(yain, o, gate, h1, w_lru, w_att, w_o, gpost):
    t = h1.shape[0]
    tm = _tile(t)

    def body(ya_ref, o_ref, g_ref, h_ref, wl_ref, wa_ref, wo_ref, gp_ref, h2_ref, mo_ref, mg_ref, ya_out, yb_out):
        ya = _nn(ya_ref[...], wl_ref[...])
        yb = _nn(o_ref[...], wa_ref[...])
        mg = (g_ref[:, 0:D] * ya + g_ref[:, D:2 * D] * yb).astype(BF16)
        mo = _nn(mg, wo_ref[...])
        ya_out[...] = ya.astype(BF16)
        yb_out[...] = yb.astype(BF16)
        mg_ref[...] = mg
        mo_ref[...] = mo
        h2_ref[...] = h_ref[...] + _rms(mo, gp_ref[...])

    f32 = jax.ShapeDtypeStruct((t, D), F32)
    b16 = jax.ShapeDtypeStruct((t, D), BF16)
    return pl.pallas_call(
        body, grid=(t // tm,), name="merge_fwd",
        in_specs=[_row_spec(tm, D), _row_spec(tm, D), _row_spec(tm, 2 * D), _row_spec(tm, D), _WHOLE, _WHOLE, _WHOLE,
                  _vec_spec(D)],
        out_specs=[_row_spec(tm, D)] * 5,
        out_shape=[f32, f32, b16, b16, b16],
        compiler_params=_params("arbitrary"),
    )(yain, o, gate, h1, w_lru, w_att, w_o, gpost)


def _ffn_bwd(dh, x, f, a, b, gpre, gpost, w1g, w3g, w2g, name):
    t = x.shape[0]
    tm = _tile(t)

    def body(dh_ref, x_ref, f_ref, a_ref, b_ref, gpre_ref, gpost_ref, w1_ref, w3_ref, w2_ref,
             dx_ref, n_ref, da_ref, db_ref, df_ref, dgpre_ref, dgpost_ref):
        @pl.when(pl.program_id(0) == 0)
        def _():
            dgpre_ref[...] = jnp.zeros_like(dgpre_ref)
            dgpost_ref[...] = jnp.zeros_like(dgpost_ref)

        dhv = dh_ref[...]
        xv = x_ref[...]
        df, dgp = _rms_bwd(0.5 * dhv, f_ref[...], gpost_ref[...])
        dgpost_ref[...] += dgp
        dfb = df.astype(BF16)
        df_ref[...] = dfb
        n_ref[...] = _rms(xv, gpre_ref[...]).astype(BF16)
        dn = jnp.zeros((tm, D), F32)
        for s in range(NSH):
            av = a_ref[s].astype(F32)
            bv = b_ref[s].astype(F32)
            sg = jax.nn.sigmoid(av)
            dhm = _nt(dfb, w2_ref[s])
            dab = (dhm * bv * (sg * (1.0 + av * (1.0 - sg)))).astype(BF16)
            dbb = (dhm * (av * sg)).astype(BF16)
            da_ref[s] = dab
            db_ref[s] = dbb
            dn = dn + _nt(dab, w1_ref[s]) + _nt(dbb, w3_ref[s])
        dxn, dg = _rms_bwd(dn, xv, gpre_ref[...])
        dgpre_ref[...] += dg
        dx_ref[...] = dhv + dxn

    sh = pl.BlockSpec((NSH, tm, FF_S), lambda i: (0, i, 0))
    act = jax.ShapeDtypeStruct((NSH, t, FF_S), BF16)
    vec = jax.ShapeDtypeStruct((1, D), F32)
    return pl.pallas_call(
        body, grid=(t // tm,), name=name,
        in_specs=[_row_spec(tm, D), _row_spec(tm, D), _row_spec(tm, D), sh, sh, _vec_spec(D), _vec_spec(D), _WHOLE, _WHOLE,
                  _WHOLE],
        out_specs=[_row_spec(tm, D), _row_spec(tm, D), sh, sh, _row_spec(tm, D), _vec_spec(D), _vec_spec(D)],
        out_shape=[jax.ShapeDtypeStruct((t, D), F32), jax.ShapeDtypeStruct((t, D), BF16), act, act,
                   jax.ShapeDtypeStruct((t, D), BF16), vec, vec],
        compiler_params=_params("arbitrary"),
    )(dh, x, f, a, b, gpre, gpost, w1g, w3g, w2g)


def _wgrad(a, b, a_spec, b_spec, out_spec, out_shape, grid, name):
    def body(a_ref, b_ref, o_ref):
        o_ref[...] = _tn(a_ref[...], b_ref[...])

    return pl.pallas_call(body, grid=grid, name=name, in_specs=[a_spec, b_spec], out_specs=out_spec,
                          out_shape=jax.ShapeDtypeStruct(out_shape, F32),
                          compiler_params=_params(*("arbitrary",) * len(grid)))(a, b)


def _wgrad_cols(act, dsh, width, name):
    t = act.shape[0]
    if dsh.ndim == 3:
        b_spec = pl.BlockSpec((None, t, width), lambda s, k: (s, 0, 0))
    else:
        b_spec = pl.BlockSpec((t, width), lambda s, k: (0, s))
    return _wgrad(act, dsh, pl.BlockSpec((t, 512), lambda s, k: (0, k)), b_spec,
                  pl.BlockSpec((None, 512, width), lambda s, k: (s, k, 0)), (NSH, D, width), (NSH, 2), name)


def _wgrad_rows(hm, df, name):
    t = df.shape[0]
    return _wgrad(hm, df, pl.BlockSpec((None, t, FF_S), lambda s, j: (s, 0, 0)), pl.BlockSpec((t, 512), lambda s, j: (0, j)),
                  pl.BlockSpec((None, FF_S, 512), lambda s, j: (s, 0, j)), (NSH, FF_S, D), (NSH, 2), name)


def _wgrad_sq(a, b, name):
    t = a.shape[0]
    return _wgrad(a, b, pl.BlockSpec((t, 512), lambda i, j: (0, i)), pl.BlockSpec((t, 512), lambda i, j: (0, j)),
                  pl.BlockSpec((512, 512), lambda i, j: (i, j)), (D, D), (2, 2), name)


def _mix_bwd1(dh2, mo, gpost, gate, ya, yb, xg, hr, w_o, w_lru, w_att):
    t = dh2.shape[0]
    tm = _tile(t)

    def body(dh_ref, mo_ref, gp_ref, g_ref, ya_ref, yb_ref, xg_ref, hr_ref, wo_ref, wl_ref, wa_ref,
             dmo_ref, dya_ref, dyb_ref, dgate_ref, dhr_ref, dxg_ref, do_ref, dgp_ref, dbg_ref):
        @pl.when(pl.program_id(0) == 0)
        def _():
            dgp_ref[...] = jnp.zeros_like(dgp_ref)
            dbg_ref[...] = jnp.zeros_like(dbg_ref)

        dmo, dgp = _rms_bwd(dh_ref[...], mo_ref[...], gp_ref[...])
        dgp_ref[...] += dgp
        dmob = dmo.astype(BF16)
        dmo_ref[...] = dmob
        dm = _nt(dmob, wo_ref[...])
        g0 = g_ref[:, 0:D]
        g1 = g_ref[:, D:2 * D]
        dyab = (dm * g0).astype(BF16)
        dybb = (dm * g1).astype(BF16)
        dya_ref[...] = dyab
        dyb_ref[...] = dybb
        dg0 = dm * ya_ref[...].astype(F32) * (g0 * (1.0 - g0))
        dg1 = dm * yb_ref[...].astype(F32) * (g1 * (1.0 - g1))
        dgate_ref[:, 0:D] = dg0.astype(BF16)
        dgate_ref[:, D:2 * D] = dg1.astype(BF16)
        dbg_ref[:, 0:D] += jnp.sum(dg0, axis=0, keepdims=True)
        dbg_ref[:, D:2 * D] += jnp.sum(dg1, axis=0, keepdims=True)
        dyain = _nt(dyab, wl_ref[...])
        do_ref[...] = _nt(dybb, wa_ref[...]).astype(BF16)
        xgv = xg_ref[...]
        dhr_ref[...] = dyain * _gelu(xgv)
        dxg_ref[...] = (dyain * hr_ref[...] * _gelu_grad(xgv)).astype(BF16)

    b16 = jax.ShapeDtypeStruct((t, D), BF16)
    return pl.pallas_call(
        body, grid=(t // tm,), name="mix_bwd1",
        in_specs=[_row_spec(tm, D), _row_spec(tm, D), _vec_spec(D), _row_spec(tm, 2 * D), _row_spec(tm, D), _row_spec(tm, D),
                  _row_spec(tm, D), _row_spec(tm, D), _WHOLE, _WHOLE, _WHOLE],
        out_specs=[_row_spec(tm, D), _row_spec(tm, D), _row_spec(tm, D), _row_spec(tm, 2 * D), _row_spec(tm, D),
                   _row_spec(tm, D), _row_spec(tm, D), _vec_spec(D), _vec_spec(2 * D)],
        out_shape=[b16, b16, b16, jax.ShapeDtypeStruct((t, 2 * D), BF16), jax.ShapeDtypeStruct((t, D), F32), b16, b16,
                   jax.ShapeDtypeStruct((1, D), F32), jax.ShapeDtypeStruct((1, 2 * D), F32)],
        compiler_params=_params("arbitrary"),
    )(dh2, mo, gpost, gate, ya, yb, xg, hr, w_o, w_lru, w_att)


def _rglru_bwd(dhr, hr, xc, r, ig, xr, conv_w, wa2, wx2, lam):
    t = dhr.shape[0]
    tm = _tile(t)
    nb8 = tm // 8
    nt = t // tm

    def body(dhr_ref, hr_ref, hrp_ref, xc_ref, r_ref, ig_ref, xr_ref, xrp_ref, cw_ref, wa_ref, wx_ref, lam_ref,
             dxr_ref, dwa_ref, dwx_ref, dba_ref, dbx_ref, dlam_ref, dcw_ref, dcb_ref,
             ext_h, ext_x, ext_d, a_sc, g_sc, c_sc, nxt_sc):
        i = pl.program_id(0)
        first_tile = i == nt - 1

        @pl.when(i == 0)
        def _():
            c_sc[...] = jnp.zeros_like(c_sc)
            nxt_sc[...] = jnp.zeros_like(nxt_sc)
            for ref in (dwa_ref, dwx_ref, dba_ref, dbx_ref, dlam_ref, dcw_ref, dcb_ref):
                ref[...] = jnp.zeros_like(ref)

        lamv = lam_ref[...]
        sp = _softplus_neg(lamv)
        rv = r_ref[...]
        igv = ig_ref[...]
        xcv = xc_ref[...]
        a, s = _lru_coeffs(rv, sp)
        a_sc[...] = a

        def blk(jj, c):
            st = pl.multiple_of((nb8 - 1 - jj) * 8, 8)
            d8 = dhr_ref[pl.ds(st, 8), :]
            a8 = a_sc[pl.ds(st, 8), :]
            rows = [None] * 8
            for k in range(7, -1, -1):
                g = d8[k:k + 1, :] + c
                c = a8[k:k + 1, :] * g
                rows[k] = g
            g_sc[pl.ds(st, 8), :] = jnp.concatenate(rows, axis=0)
            return c

        c_sc[0:1, :] = lax.fori_loop(0, nb8, blk, c_sc[0:1, :])
        g = g_sc[...]
        ext_h[0:8, :] = jnp.where(first_tile, 0.0, hrp_ref[...])
        ext_h[8:8 + tm, :] = hr_ref[...]
        hprev = ext_h[pl.ds(7, tm), :]
        d_s = g * (igv * xcv)
        dig = g * s * xcv
        dxc = g * s * igv
        dla = (g * hprev) * a - d_s * ((a * a) / s)
        dr_pre = (dla * (-LRU_C * sp)) * (rv * (1.0 - rv))
        di_pre = dig * (igv * (1.0 - igv))
        dlam_ref[...] += jnp.sum(dla * (LRU_C * rv), axis=0, keepdims=True) * jax.nn.sigmoid(-lamv)
        dba_ref[...] += jnp.sum(dr_pre, axis=0, keepdims=True)
        dbx_ref[...] += jnp.sum(di_pre, axis=0, keepdims=True)
        drb = dr_pre.astype(BF16)
        dib = di_pre.astype(BF16)
        xcb = xcv.astype(BF16)
        ext_d[tm:tm + 8, :] = nxt_sc[...]
        for p in range(8):
            sl = slice(p * 128, (p + 1) * 128)
            ext_d[0:tm, sl] = dxc[:, sl] + _nt(drb[:, sl], wa_ref[p]) + _nt(dib[:, sl], wx_ref[p])
            dwa_ref[p] += _tn(xcb[:, sl], drb[:, sl])
            dwx_ref[p] += _tn(xcb[:, sl], dib[:, sl])
        dxcv = ext_d[0:tm, :]
        nxt_sc[...] = ext_d[0:8, :]
        dcb_ref[...] += jnp.sum(dxcv, axis=0, keepdims=True)
        ext_x[0:8, :] = jnp.where(first_tile, 0.0, xrp_ref[...])
        ext_x[8:8 + tm, :] = xr_ref[...]
        dxr = jnp.zeros((tm, D), F32)
        for tap in range(4):
            dxr = dxr + ext_d[pl.ds(3 - tap, tm), :] * cw_ref[tap:tap + 1, :]
            dcw_ref[tap:tap + 1, :] += jnp.sum(dxcv * ext_x[pl.ds(5 + tap, tm), :], axis=0, keepdims=True)
        dxr_ref[...] = dxr.astype(BF16)

    rev = pl.BlockSpec((tm, D), lambda i: (nt - 1 - i, 0))
    prev = pl.BlockSpec((8, D), lambda i: (jnp.maximum((nt - 1 - i) * nb8 - 1, 0), 0))
    full = lambda shape: pl.BlockSpec(shape, lambda i: tuple(0 for _ in shape))
    vec = jax.ShapeDtypeStruct((1, D), F32)
    blocks = jax.ShapeDtypeStruct((8, 128, 128), F32)
    return pl.pallas_call(
        body, grid=(nt,), name="rglru_bwd",
        in_specs=[rev, rev, prev, rev, rev, rev, rev, prev, full((4, D)), full((8, 128, 128)), full((8, 128, 128)),
                  _vec_spec(D)],
        out_specs=[rev, full((8, 128, 128)), full((8, 128, 128)), _vec_spec(D), _vec_spec(D), _vec_spec(D), full((4, D)),
                   _vec_spec(D)],
        out_shape=[jax.ShapeDtypeStruct((t, D), BF16), blocks, blocks, vec, vec, vec, jax.ShapeDtypeStruct((4, D), F32), vec],
        scratch_shapes=[pltpu.VMEM((tm + 8, D), F32), pltpu.VMEM((tm + 8, D), F32), pltpu.VMEM((tm + 8, D), F32),
                        pltpu.VMEM((tm, D), F32), pltpu.VMEM((tm, D), F32), pltpu.VMEM((8, D), F32), pltpu.VMEM((8, D), F32)],
        compiler_params=_params("arbitrary"),
    )(dhr, hr, hr, xc, r, ig, xr, xr, conv_w, wa2, wx2, lam)


def _attn_bwd(sink_rows, q, kp, vp, bias_t, mask, do):
    t = q.shape[0]
    tp = kp.shape[0]

    def body(sink_ref, q_ref, kp_ref, vp_ref, bias_ref, mask_ref, do_ref, dq_ref, dk_ref, dv_ref, dbias_ref, ds_ref):
        c = pl.program_id(0)

        @pl.when(c == 0)
        def _():
            for ref in (dk_ref, dv_ref, dbias_ref, ds_ref):
                ref[...] = jnp.zeros_like(ref)

        st = pl.multiple_of(c * CHUNK, CHUNK)
        maskv = mask_ref[...]
        kbd = _block_diag(kp_ref[pl.ds(st, KP), :], maskv)
        vbd = _block_diag(vp_ref[pl.ds(st, KP), :], maskv)
        q_all = _stack_heads(q_ref[...])
        do_all = _stack_heads(do_ref[...])
        ps, sinks = _attn_softmax(q_all, kbd, bias_ref[...], sink_ref[...], c)
        dp = _nt(vbd, do_all)
        dscs = []
        for g in range(4):
            dpg = dp[g * KP:(g + 1) * KP, :]
            delta = jnp.sum(ps[g] * dpg, axis=0, keepdims=True)
            dscs.append(ps[g] * (dpg - delta))
            ds_ref[g:g + 1, :] += -(sinks[g] * delta)
        dsc = jnp.concatenate(dscs, axis=0)
        dbias_ref[...] += dsc
        dsb = (dsc * (HEAD_DIM ** -0.5)).astype(BF16)
        dq_ref[...] = _unstack_heads(_tn(dsb, kbd)).astype(BF16)

        lane_group = lax.broadcasted_iota(jnp.int32, (1, 4 * HEAD_DIM), 1) // HEAD_DIM

        def own_blocks(full):
            out = full[0:KP]
            for g in range(1, 4):
                out = jnp.where(lane_group == g, full[g * KP:(g + 1) * KP], out)
            return out

        dk_ref[pl.ds(st, KP), :] += own_blocks(_nn(dsb, q_all))
        dv_ref[pl.ds(st, KP), :] += own_blocks(_nn(jnp.concatenate(ps, axis=0).astype(BF16), do_all))

    full = lambda shape: pl.BlockSpec(shape, lambda i: tuple(0 for _ in shape))
    return pl.pallas_call(
        body, grid=(t // CHUNK,), name="attn_bwd",
        in_specs=[_WHOLE, _row_spec(CHUNK, D), _WHOLE, _WHOLE, _WHOLE, _WHOLE, _row_spec(CHUNK, D)],
        out_specs=[_row_spec(CHUNK, D), full((tp, KV_W)), full((tp, KV_W)), full((4 * KP, 4 * CHUNK)), full((8, 4 * CHUNK))],
        out_shape=[jax.ShapeDtypeStruct((t, D), BF16), jax.ShapeDtypeStruct((tp, KV_W), F32),
                   jax.ShapeDtypeStruct((tp, KV_W), F32), jax.ShapeDtypeStruct((4 * KP, 4 * CHUNK), F32),
                   jax.ShapeDtypeStruct((8, 4 * CHUNK), F32)],
        compiler_params=_params("arbitrary"),
    )(sink_rows, q, kp, vp, bias_t, mask, do)


def _mix_bwd2(dproj, dgate, h1, dh2, gmix, w_in_g, w_gate_g):
    t = h1.shape[0]
    tm = _tile(t)

    def body(dp_ref, dg_ref, h_ref, dh_ref, g_ref, win_ref, wg_ref, dh1_ref, dgm_ref):
        @pl.when(pl.program_id(0) == 0)
        def _():
            dgm_ref[...] = jnp.zeros_like(dgm_ref)

        du = jnp.zeros((tm, D), F32)
        for s in range(NSH):
            du = du + _nt(dp_ref[:, s * IN_S:(s + 1) * IN_S], win_ref[s])
            du = du + _nt(dg_ref[:, s * GATE_S:(s + 1) * GATE_S], wg_ref[s])
        dxn, dg = _rms_bwd(du, h_ref[...], g_ref[...])
        dgm_ref[...] += dg
        dh1_ref[...] = dh_ref[...] + dxn

    return pl.pallas_call(
        body, grid=(t // tm,), name="mix_bwd2",
        in_specs=[_row_spec(tm, NSH * IN_S), _row_spec(tm, 2 * D), _row_spec(tm, D), _row_spec(tm, D), _vec_spec(D), _WHOLE,
                  _WHOLE],
        out_specs=[_row_spec(tm, D), _vec_spec(D)],
        out_shape=[jax.ShapeDtypeStruct((t, D), F32), jax.ShapeDtypeStruct((1, D), F32)],
        compiler_params=_params("arbitrary"),
    )(dproj, dgate, h1, dh2, gmix, w_in_g, w_gate_g)


def _band_onehot():
    nb = N_BUCKETS // 2
    max_exact = nb // 2
    rel = jnp.arange(KB)[None, :] - PAD_KEYS - jnp.arange(CHUNK)[:, None]
    ret = jnp.where(rel > 0, nb, 0)
    n = jnp.abs(rel)
    nf = jnp.maximum(n, 1).astype(jnp.float32)
    large = max_exact + (jnp.log(nf / max_exact) / math.log(128 / max_exact) * (nb - max_exact)).astype(jnp.int32)
    large = jnp.minimum(large, nb - 1)
    buckets = (ret + jnp.where(n < max_exact, n, large)).reshape(1, CHUNK * KB)
    return (buckets == jnp.arange(N_BUCKETS)[:, None]).astype(F32)


def _pair_blocks(w):
    z = jnp.zeros((8, 128, 128), w.dtype)
    return z.at[:, 0:64, 0:64].set(w[0::2]).at[:, 64:128, 64:128].set(w[1::2])


def _unpair_blocks(w2):
    return jnp.stack([w2[:, 0:64, 0:64], w2[:, 64:128, 64:128]], axis=1).reshape(16, 64, 64)


def _local_step(x, target, wg, sm):
    row = lambda v: v.reshape(1, -1)
    onehot_t = _band_onehot()
    bias = _bias_fwd(sm["rel_bias"].T, onehot_t).reshape(4, 4, CHUNK, KB)
    bias_t = jnp.pad(jnp.transpose(bias, (0, 3, 1, 2)), ((0, 0), (0, KP - KB), (0, 0), (0, 0))).reshape(4 * KP, 4 * CHUNK)
    sink_rows = jnp.pad(jnp.repeat(sm["attn_sinks"].reshape(4, 4), CHUNK, axis=1), ((0, 4), (0, 0)))
    grp = jnp.arange(4 * KP)[:, None] // KP == jnp.arange(4 * HEAD_DIM)[None, :] // HEAD_DIM
    mask = (grp & (jnp.arange(4 * KP)[:, None] % KP < KB)).astype(BF16)
    wa2 = _pair_blocks(sm["rg_a_w"]).astype(BF16)
    wx2 = _pair_blocks(sm["rg_x_w"]).astype(BF16)
    w_lru = wg["w_lru_out"].reshape(D, D)
    w_att = wg["w_attn_out"].reshape(D, D)
    w_o = wg["w_o"].reshape(D, D)

    h1, a1, b1, hm1, f1 = _ffn_fwd(x, row(sm["ffn1_pre_g"]), wg["ffn1_w1"], wg["ffn1_w3"], wg["ffn1_w2"],
                                   row(sm["ffn1_post_g"]), "ffn1_fwd")
    u, q, k, v, xr, xg, gate = _mix_proj(h1, row(sm["mix_pre_g"]), wg["w_in"], wg["w_gate"], row(sm["b_gate"]))
    hr, yain, xc, r, ig = _rglru_fwd(xr, xg, sm["conv_w"], row(sm["conv_b"]), wa2, row(sm["rg_a_b"]), wx2,
                                     row(sm["rg_x_b"]), row(sm["lru_lambda"]))
    kp = jnp.pad(k, ((PAD_KEYS, KP - KB), (0, 0)))
    vp = jnp.pad(v, ((PAD_KEYS, KP - KB), (0, 0)))
    o = _attn_fwd(sink_rows, q, kp, vp, bias_t, mask)
    h2, mo, merged, ya, yb = _merge_fwd(yain, o, gate, h1, w_lru, w_att, w_o, row(sm["mix_post_g"]))
    y, a2, b2, hm2, f2 = _ffn_fwd(h2, row(sm["ffn2_pre_g"]), wg["ffn2_w1"], wg["ffn2_w3"], wg["ffn2_w2"],
                                  row(sm["ffn2_post_g"]), "ffn2_fwd")
    dy, sq = _loss_dy(y, target)

    big, small = {}, {}
    dh2, n2, da2, db2, df2, small["ffn2_pre_g"], small["ffn2_post_g"] = _ffn_bwd(
        dy, h2, f2, a2, b2, row(sm["ffn2_pre_g"]), row(sm["ffn2_post_g"]), wg["ffn2_w1"], wg["ffn2_w3"], wg["ffn2_w2"],
        "ffn2_bwd")
    big["ffn2_w1"] = _wgrad_cols(n2, da2, FF_S, "dw_ffn2_w1")
    big["ffn2_w3"] = _wgrad_cols(n2, db2, FF_S, "dw_ffn2_w3")
    big["ffn2_w2"] = _wgrad_rows(hm2, df2, "dw_ffn2_w2")
    dmo, dya, dyb, dgate, dhr, dxg, do, small["mix_post_g"], small["b_gate"] = _mix_bwd1(
        dh2, mo, row(sm["mix_post_g"]), gate, ya, yb, xg, hr, w_o, w_lru, w_att)
    big["w_o"] = _wgrad_sq(merged, dmo, "dw_w_o").reshape(NSH, D // NSH, D)
    big["w_lru_out"] = _wgrad_sq(yain, dya, "dw_w_lru_out").reshape(NSH, D // NSH, D)
    big["w_attn_out"] = _wgrad_sq(o, dyb, "dw_w_attn_out").reshape(NSH, D // NSH, D)
    (dxr, dwa2, dwx2, small["rg_a_b"], small["rg_x_b"], small["lru_lambda"], small["conv_w"], small["conv_b"]) = _rglru_bwd(
        dhr, hr, xc, r, ig, xr, sm["conv_w"], wa2, wx2, row(sm["lru_lambda"]))
    small["rg_a_w"] = _unpair_blocks(dwa2)
    small["rg_x_w"] = _unpair_blocks(dwx2)
    dq, dkp, dvp, dbias_t, ds_rows = _attn_bwd(sink_rows, q, kp, vp, bias_t, mask, do)
    dbias = jnp.transpose(dbias_t.reshape(4, KP, 4, CHUNK)[:, :KB], (0, 2, 3, 1)).reshape(N_HEADS, CHUNK * KB)
    drel_t, dsinks = _bias_bwd(dbias, onehot_t, ds_rows)
    small["attn_sinks"] = dsinks[0:4, 0:4].reshape(N_HEADS)
    small["rel_bias"] = drel_t.T
    t = x.shape[0]
    dproj = jnp.concatenate([dq, dkp[PAD_KEYS:PAD_KEYS + t].astype(BF16), dvp[PAD_KEYS:PAD_KEYS + t].astype(BF16), dxr, dxg],
                            axis=1)
    big["w_in"] = _wgrad_cols(u, dproj, IN_S, "dw_w_in")
    big["w_gate"] = _wgrad_cols(u, dgate, GATE_S, "dw_w_gate")
    dh1, small["mix_pre_g"] = _mix_bwd2(dproj, dgate, h1, dh2, row(sm["mix_pre_g"]), wg["w_in"], wg["w_gate"])
    dx, n1, da1, db1, df1, small["ffn1_pre_g"], small["ffn1_post_g"] = _ffn_bwd(
        dh1, x, f1, a1, b1, row(sm["ffn1_pre_g"]), row(sm["ffn1_post_g"]), wg["ffn1_w1"], wg["ffn1_w3"], wg["ffn1_w2"],
        "ffn1_bwd")
    big["ffn1_w1"] = _wgrad_cols(n1, da1, FF_S, "dw_ffn1_w1")
    big["ffn1_w3"] = _wgrad_cols(n1, db1, FF_S, "dw_ffn1_w3")
    big["ffn1_w2"] = _wgrad_rows(hm1, df1, "dw_ffn1_w2")
    return sq, dx, big, small


_ANY = pl.BlockSpec(memory_space=pl.ANY)


def _place():
    return lax.axis_index("x"), lax.axis_index("y"), lax.axis_index("c")


def _other_chips(x, y):
    return [(1 - x, y), (x, 1 - y), (1 - x, 1 - y)]


def _cast_bf16(w, name):
    rows = w.shape[0] // 4

    def body(w_ref, o_ref):
        o_ref[...] = w_ref[...].astype(BF16)

    spec = pl.BlockSpec((rows, w.shape[1]), lambda i: (i, 0))
    return pl.pallas_call(body, grid=(4,), in_specs=[spec], out_specs=spec, name=name,
                          out_shape=jax.ShapeDtypeStruct(w.shape, BF16), compiler_params=_params("arbitrary"))(w)


def _all_gather_chips(shards):
    n = len(shards)

    def body(*refs):
        ins, outs = refs[:n], refs[n:2 * n]
        local_sem, send_sems, recv_sems = refs[2 * n:]
        x, y, c = _place()
        me = 2 * x + y
        chips = _other_chips(x, y)
        for w in range(n):
            pltpu.make_async_copy(ins[w], outs[w].at[me], local_sem.at[w]).start()
            for j, (px, py) in enumerate(chips):
                pltpu.make_async_remote_copy(src_ref=ins[w], dst_ref=outs[w].at[me], send_sem=send_sems.at[3 * w + j],
                                             recv_sem=recv_sems.at[3 * w + j], device_id=(px, py, c),
                                             device_id_type=MESH).start()
        for w in range(n):
            pltpu.make_async_copy(ins[w], outs[w].at[me], local_sem.at[w]).wait()
            for j, (px, py) in enumerate(chips):
                pltpu.make_async_remote_copy(src_ref=ins[w], dst_ref=outs[w].at[2 * px + py], send_sem=send_sems.at[3 * w + j],
                                             recv_sem=recv_sems.at[3 * w + j], device_id=(px, py, c),
                                             device_id_type=MESH).wait()

    return pl.pallas_call(
        body, name="gather_weights", in_specs=[_ANY] * n, out_specs=[_ANY] * n,
        out_shape=[jax.ShapeDtypeStruct((NSH,) + s.shape, s.dtype) for s in shards],
        scratch_shapes=[pltpu.SemaphoreType.DMA((n,)), pltpu.SemaphoreType.DMA((3 * n,)), pltpu.SemaphoreType.DMA((3 * n,))],
        compiler_params=pltpu.CompilerParams(has_side_effects=True),
    )(*shards)


def _swap_halves(grads):
    n = len(grads)

    def body(*refs):
        ins, outs = refs[:n], refs[n:2 * n]
        send_sems, recv_sems = refs[2 * n:]
        x, y, c = _place()
        copies = []
        for w in range(n):
            rh = ins[w].shape[1] // 2
            theirs = ins[w].at[:, pl.ds(pl.multiple_of((1 - c) * rh, 8), rh), :]
            copies.append(pltpu.make_async_remote_copy(src_ref=theirs, dst_ref=outs[w], send_sem=send_sems.at[w],
                                                       recv_sem=recv_sems.at[w], device_id=(x, y, 1 - c), device_id_type=MESH))
            copies[-1].start()
        for cp in copies:
            cp.wait()

    return pl.pallas_call(
        body, name="swap_halves", in_specs=[_ANY] * n, out_specs=[_ANY] * n,
        out_shape=[jax.ShapeDtypeStruct((NSH, g.shape[1] // 2, g.shape[2]), F32) for g in grads],
        scratch_shapes=[pltpu.SemaphoreType.DMA((n,)), pltpu.SemaphoreType.DMA((n,))],
        compiler_params=pltpu.CompilerParams(has_side_effects=True),
    )(*grads)


def _chip_sum(g, got, name):
    _, r, cc = g.shape
    rh = r // 2

    def body(g_ref, got_ref, hb_ref, own_ref):
        x, y, c = _place()
        s = pl.program_id(0)
        h = g_ref[pl.ds(pl.multiple_of(c * rh, 8), rh), :] + got_ref[...]
        hb_ref[...] = h.astype(BF16)

        @pl.when(s == 2 * x + y)
        def _():
            own_ref[...] = h

    return pl.pallas_call(
        body, grid=(NSH,), name=name,
        in_specs=[pl.BlockSpec((None, r, cc), lambda s: (s, 0, 0)), pl.BlockSpec((None, rh, cc), lambda s: (s, 0, 0))],
        out_specs=[pl.BlockSpec((None, rh, cc), lambda s: (s, 0, 0)), pl.BlockSpec((rh, cc), lambda s: (0, 0))],
        out_shape=[jax.ShapeDtypeStruct((NSH, rh, cc), BF16), jax.ShapeDtypeStruct((rh, cc), F32)],
        compiler_params=_params("arbitrary"),
    )(g, got)


def _send_to_owners(sums):
    n = len(sums)

    def body(*refs):
        ins, outs = refs[:n], refs[n:2 * n]
        send_sems, recv_sems = refs[2 * n:]
        x, y, c = _place()
        copies = []
        for w in range(n):
            for j, (px, py) in enumerate(_other_chips(x, y)):
                copies.append(pltpu.make_async_remote_copy(src_ref=ins[w].at[2 * px + py], dst_ref=outs[w].at[j],
                                                           send_sem=send_sems.at[3 * w + j], recv_sem=recv_sems.at[3 * w + j],
                                                           device_id=(px, py, c), device_id_type=MESH))
                copies[-1].start()
        for cp in copies:
            cp.wait()

    return pl.pallas_call(
        body, name="send_to_owners", in_specs=[_ANY] * n, out_specs=[_ANY] * n,
        out_shape=[jax.ShapeDtypeStruct((3,) + s.shape[1:], BF16) for s in sums],
        scratch_shapes=[pltpu.SemaphoreType.DMA((3 * n,)), pltpu.SemaphoreType.DMA((3 * n,))],
        compiler_params=pltpu.CompilerParams(has_side_effects=True),
    )(*sums)


def _owner_sum(own, got, name):
    rh, cc = own.shape
    rows = rh // 2

    def body(own_ref, got_ref, o_ref):
        o_ref[...] = ((own_ref[...] + got_ref[0].astype(F32)) + got_ref[1].astype(F32)) + got_ref[2].astype(F32)

    return pl.pallas_call(
        body, grid=(2,), name=name,
        in_specs=[pl.BlockSpec((rows, cc), lambda i: (i, 0)), pl.BlockSpec((3, rows, cc), lambda i: (0, i, 0))],
        out_specs=pl.BlockSpec((rows, cc), lambda i: (i, 0)),
        out_shape=jax.ShapeDtypeStruct((rh, cc), F32), compiler_params=_params("arbitrary"),
    )(own, got)


def _send_halves(halves):
    n = len(halves)

    def body(*refs):
        ins, outs = refs[:n], refs[n:2 * n]
        send_sems, recv_sems = refs[2 * n:]
        x, y, c = _place()
        copies = [pltpu.make_async_remote_copy(src_ref=ins[w], dst_ref=outs[w], send_sem=send_sems.at[w], recv_sem=recv_sems.at[w],
                                               device_id=(x, y, 1 - c), device_id_type=MESH) for w in range(n)]
        for cp in copies:
            cp.start()
        for cp in copies:
            cp.wait()

    return pl.pallas_call(
        body, name="send_halves", in_specs=[_ANY] * n, out_specs=[_ANY] * n,
        out_shape=[jax.ShapeDtypeStruct(h.shape, F32) for h in halves],
        scratch_shapes=[pltpu.SemaphoreType.DMA((n,)), pltpu.SemaphoreType.DMA((n,))],
        compiler_params=pltpu.CompilerParams(has_side_effects=True),
    )(*halves)


def _all_reduce_small(part):
    def body(p_ref, o_ref, rbuf, send1, recv1, send2, recv2):
        x, y, c = _place()
        me = 4 * x + 2 * y + c
        peers = []
        for k in range(1, 8):
            px, py, pc = x ^ ((k >> 2) & 1), y ^ ((k >> 1) & 1), c ^ (k & 1)
            peers.append((k, (px, py, pc), 4 * px + 2 * py + pc))

        def rows(d):
            return pl.ds(pl.multiple_of(d * SMALL_SLICE, 8), SMALL_SLICE)

        first = [pltpu.make_async_remote_copy(src_ref=p_ref.at[rows(idx), :], dst_ref=rbuf.at[me], send_sem=send1.at[k],
                                              recv_sem=recv1.at[k], device_id=dev, device_id_type=MESH)
                 for k, dev, idx in peers]
        for cp in first:
            cp.start()
        rbuf[me] = p_ref[rows(me), :]
        for k, dev, idx in peers:
            pltpu.make_async_remote_copy(src_ref=p_ref.at[rows(idx), :], dst_ref=rbuf.at[idx], send_sem=send1.at[k],
                                         recv_sem=recv1.at[k], device_id=dev, device_id_type=MESH).wait_recv()
        acc = rbuf[0]
        for d in range(1, 8):
            acc = acc + rbuf[d]
        o_ref[rows(me), :] = acc
        second = [pltpu.make_async_remote_copy(src_ref=o_ref.at[rows(me), :], dst_ref=o_ref.at[rows(me), :],
                                               send_sem=send2.at[k], recv_sem=recv2.at[k], device_id=dev, device_id_type=MESH)
                  for k, dev, idx in peers]
        for cp in second:
            cp.start()
        for k, dev, idx in peers:
            pltpu.make_async_remote_copy(src_ref=o_ref.at[rows(me), :], dst_ref=o_ref.at[rows(idx), :], send_sem=send2.at[k],
                                         recv_sem=recv2.at[k], device_id=dev, device_id_type=MESH).wait_recv()
        for cp in first + second:
            cp.wait_send()

    return pl.pallas_call(
        body, name="all_reduce_small", in_specs=[_WHOLE], out_specs=_WHOLE,
        out_shape=jax.ShapeDtypeStruct((SMALL_ROWS, 128), F32),
        scratch_shapes=[pltpu.VMEM((8, SMALL_SLICE, 128), F32)] + [pltpu.SemaphoreType.DMA((8,))] * 4,
        compiler_params=pltpu.CompilerParams(has_side_effects=True),
    )(part)


def _adamw_update(w, gv, m, v):
    nm = ADAM_B1 * m + (1.0 - ADAM_B1) * gv
    nv = ADAM_B2 * v + (1.0 - ADAM_B2) * (gv * gv)
    m_hat = nm / (1.0 - ADAM_B1 ** ADAM_STEP)
    v_hat = nv / (1.0 - ADAM_B2 ** ADAM_STEP)
    return -ADAM_LR * (m_hat / (jnp.sqrt(v_hat) + ADAM_EPS) + ADAM_WD * w), nm, nv


def _adamw(w, g, m, v, name):
    rows = w.shape[0] // 4

    def body(w_ref, g_ref, m_ref, v_ref, d_ref, nm_ref, nv_ref):
        d_ref[...], nm_ref[...], nv_ref[...] = _adamw_update(w_ref[...], g_ref[...], m_ref[...], v_ref[...])

    spec = pl.BlockSpec((rows, w.shape[1]), lambda i: (i, 0))
    out = jax.ShapeDtypeStruct(w.shape, F32)
    return pl.pallas_call(body, grid=(4,), in_specs=[spec] * 4, out_specs=[spec] * 3, out_shape=[out] * 3, name=name,
                          compiler_params=_params("arbitrary"))(w, g, m, v)


def _adamw_halves(w, mine, theirs, m, v, name):
    rh, cc = mine.shape
    rows = rh // 2

    def body(w_ref, mine_ref, theirs_ref, m_ref, v_ref, g_ref, d_ref, nm_ref, nv_ref):
        gv = jnp.where(pl.program_id(0) == lax.axis_index("c"), mine_ref[...], theirs_ref[...])
        g_ref[...] = gv
        d_ref[...], nm_ref[...], nv_ref[...] = _adamw_update(w_ref[...], gv, m_ref[...], v_ref[...])

    spec = pl.BlockSpec((rows, cc), lambda h, i: (2 * h + i, 0))
    half = pl.BlockSpec((rows, cc), lambda h, i: (i, 0))
    out = jax.ShapeDtypeStruct(w.shape, F32)
    return pl.pallas_call(body, grid=(2, 2), in_specs=[spec, half, half, spec, spec], out_specs=[spec] * 4,
                          out_shape=[out] * 4, name=name, compiler_params=_params("arbitrary", "arbitrary"))(w, mine, theirs, m, v)


def _pack_small(vals):
    parts = []
    for name, size in SMALL:
        flat = vals[name].reshape(-1).astype(F32)
        parts.append(jnp.pad(flat, (0, size - flat.shape[0])))
    flat = jnp.concatenate(parts)
    return jnp.pad(flat, (0, SMALL_ROWS * 128 - flat.shape[0])).reshape(SMALL_ROWS, 128)


def _unpack_small(packed, shapes):
    flat = packed.reshape(-1)
    out, off = {}, 0
    for name, size in SMALL:
        n = math.prod(shapes[name])
        out[name] = flat[off:off + n].reshape(shapes[name])
        off += size
    return out


def kernel(x, ffn1_pre_g, ffn1_w1, ffn1_w3, ffn1_w2, ffn1_post_g, mix_pre_g, w_in, conv_w, conv_b, rg_a_w, rg_a_b, rg_x_w, rg_x_b, lru_lambda, w_lru_out, attn_sinks, rel_bias, w_attn_out, w_gate, b_gate, w_o, mix_post_g, ffn2_pre_g, ffn2_w1, ffn2_w3, ffn2_w2, ffn2_post_g, loss_target, m_ffn1_pre_g, m_ffn1_w1, m_ffn1_w3, m_ffn1_w2, m_ffn1_post_g, m_mix_pre_g, m_w_in, m_conv_w, m_conv_b, m_rg_a_w, m_rg_a_b, m_rg_x_w, m_rg_x_b, m_lru_lambda, m_w_lru_out, m_attn_sinks, m_rel_bias, m_w_attn_out, m_w_gate, m_b_gate, m_w_o, m_mix_post_g, m_ffn2_pre_g, m_ffn2_w1, m_ffn2_w3, m_ffn2_w2, m_ffn2_post_g, v_ffn1_pre_g, v_ffn1_w1, v_ffn1_w3, v_ffn1_w2, v_ffn1_post_g, v_mix_pre_g, v_w_in, v_conv_w, v_conv_b, v_rg_a_w, v_rg_a_b, v_rg_x_w, v_rg_x_b, v_lru_lambda, v_w_lru_out, v_attn_sinks, v_rel_bias, v_w_attn_out, v_w_gate, v_b_gate, v_w_o, v_mix_post_g, v_ffn2_pre_g, v_ffn2_w1, v_ffn2_w3, v_ffn2_w2, v_ffn2_post_g):
    given = dict(locals())
    chip = 2 * lax.axis_index("x") + lax.axis_index("y")

    shards = [_cast_bf16(given[n][0], "cast_" + n) for n in BIG] + [given["conv_w"][0]]
    gathered = _all_gather_chips(shards)
    wg = dict(zip(BIG, gathered[:-1]))
    small_shapes = {n: given[n].shape for n, _ in SMALL}
    small_shapes["conv_w"] = (1, 4, D)
    sm = {n: (given[n][0] if given[n].shape[0] == 1 and n != "rel_bias" else given[n]) for n, _ in SMALL}
    sm["conv_w"] = jnp.transpose(gathered[-1], (1, 0, 2)).reshape(4, D)

    sq, dx, big, small = _local_step(x[0], loss_target[0], wg, sm)
    loss = lax.psum(sq[0, 0] * (0.5 / D), ("x", "y", "c"))

    full = [big[n] for n in BIG]
    from_sibling = _swap_halves(full)
    sums = [_chip_sum(g, got, "chip_sum_" + n) for n, g, got in zip(BIG, full, from_sibling)]
    from_chips = _send_to_owners([s[0] for s in sums])
    halves = [_owner_sum(s[1], got, "owner_sum_" + n) for n, s, got in zip(BIG, sums, from_chips)]
    from_sibling2 = _send_halves(halves)
    small_g = _unpack_small(_all_reduce_small(_pack_small(small)), small_shapes)

    grads, delta, new_m, new_v = {}, {}, {}, {}
    for n, mine, theirs in zip(BIG, halves, from_sibling2):
        grads[n], delta[n], new_m[n], new_v[n] = (r[None] for r in _adamw_halves(
            given[n][0], mine, theirs, given["m_" + n][0], given["v_" + n][0], "adamw_" + n))

    def widen(a):
        return lax.dynamic_update_slice(jnp.zeros((1, 4, D), F32), a, (0, 0, chip * (D // NSH)))

    packed = [_pack_small({n: (widen(given[pre + n]) if n == "conv_w" else given[pre + n]) for n, _ in SMALL})
              for pre in ("", "m_", "v_")]
    packed_g = _pack_small(small_g)
    outs = _adamw(packed[0], packed_g, packed[1], packed[2], "adamw_small")
    for dst, arr in zip((delta, new_m, new_v), outs):
        dst.update(_unpack_small(arr, small_shapes))
    small_out = dict(small_g)
    for d in (small_out, delta, new_m, new_v):
        d["conv_w"] = lax.dynamic_slice(d["conv_w"], (0, 0, chip * (D // NSH)), (1, 4, D // NSH))
    grads.update(small_out)
    return (loss, dx[None], *[grads[n] for n in WEIGHTS], *[delta[n] for n in WEIGHTS], *[new_m[n] for n in WEIGHTS],
            *[new_v[n] for n in WEIGHTS])
```

```python
import functools
import math

import jax
import jax.numpy as jnp
from jax import lax
from jax.experimental import pallas as pl
from jax.experimental.pallas import tpu as pltpu

F32, BF16 = jnp.float32, jnp.bfloat16
D = 1024
NSH = 4
FF_S = 704
IN_S = 896
GATE_S = 512
KV_W = 256
CHUNK = 64
KB = 192
N_HEADS = 16
HEAD_DIM = 64
N_BUCKETS = 32
KP = 256
PAD_KEYS = 128
RMS_EPS = 1e-6
NEG_INF = -1e30
LRU_C = 8.0
TM = 256
VMEM_LIMIT = 56 * 1024 * 1024
ADAM_LR, ADAM_B1, ADAM_B2, ADAM_EPS, ADAM_WD, ADAM_STEP = 0.001, 0.9, 0.999, 1e-08, 0.01, 10
SMALL_ROWS = 1216
SMALL_SLICE = SMALL_ROWS // 8
MESH = pl.DeviceIdType.MESH

BIG = ["ffn1_w1", "ffn1_w3", "ffn1_w2", "w_in", "w_lru_out", "w_attn_out", "w_gate", "w_o", "ffn2_w1", "ffn2_w3", "ffn2_w2"]
SMALL = [("ffn1_pre_g", 1024), ("ffn1_post_g", 1024), ("mix_pre_g", 1024), ("conv_w", 4096), ("conv_b", 1024),
         ("rg_a_w", 65536), ("rg_a_b", 1024), ("rg_x_w", 65536), ("rg_x_b", 1024), ("lru_lambda", 1024),
         ("attn_sinks", 1024), ("rel_bias", 1024), ("b_gate", 2048), ("mix_post_g", 1024), ("ffn2_pre_g", 1024),
         ("ffn2_post_g", 1024)]
WEIGHTS = ["ffn1_pre_g", "ffn1_w1", "ffn1_w3", "ffn1_w2", "ffn1_post_g", "mix_pre_g", "w_in", "conv_w", "conv_b", "rg_a_w",
           "rg_a_b", "rg_x_w", "rg_x_b", "lru_lambda", "w_lru_out", "attn_sinks", "rel_bias", "w_attn_out", "w_gate", "b_gate",
           "w_o", "mix_post_g", "ffn2_pre_g", "ffn2_w1", "ffn2_w3", "ffn2_w2", "ffn2_post_g"]


def _params(*sem):
    return pltpu.CompilerParams(dimension_semantics=sem or None, vmem_limit_bytes=VMEM_LIMIT)


def _nn(a, b):
    return jnp.dot(a, b, preferred_element_type=F32)


def _nt(a, b):
    return lax.dot_general(a, b, (((1,), (1,)), ((), ())), preferred_element_type=F32)


def _tn(a, b):
    return lax.dot_general(a, b, (((0,), (0,)), ((), ())), preferred_element_type=F32)


def _rms(x, g):
    rstd = lax.rsqrt(jnp.mean(x * x, axis=-1, keepdims=True) + RMS_EPS)
    return (x * rstd) * g


def _rms_bwd(dout, x, g):
    rstd = lax.rsqrt(jnp.mean(x * x, axis=-1, keepdims=True) + RMS_EPS)
    xhat = x * rstd
    dg = jnp.sum(dout * xhat, axis=0, keepdims=True)
    dxhat = dout * g
    dx = rstd * (dxhat - xhat * jnp.mean(dxhat * xhat, axis=-1, keepdims=True))
    return dx, dg


_GELU_K = math.sqrt(2.0 / math.pi)


def _gelu(x):
    return x * (0.5 * (1.0 + jnp.tanh(_GELU_K * (x + 0.044715 * (x * x * x)))))


def _gelu_grad(x):
    t = jnp.tanh(_GELU_K * (x + 0.044715 * (x * x * x)))
    return 0.5 * (1.0 + t) + x * (0.5 * (1.0 - t * t) * (_GELU_K * (1.0 + 3.0 * 0.044715 * (x * x))))


def _softplus_neg(lam):
    z = -lam
    u = jnp.exp(-jnp.abs(z))
    w = 1.0 + u
    log1p_u = jnp.where(w == 1.0, u, jnp.log(w) * (u / (w - 1.0)))
    return jnp.maximum(z, 0.0) + log1p_u


def _lru_coeffs(r, sp):
    log_a = (-LRU_C * r) * sp
    a = jnp.exp(log_a)
    t = jnp.tanh(log_a)
    s = jnp.sqrt(-2.0 * t / (1.0 - t))
    return a, s


def _row_spec(tm, width):
    return pl.BlockSpec((tm, width), lambda i: (i, 0))


def _vec_spec(width):
    return pl.BlockSpec((1, width), lambda i: (0, 0))


_WHOLE = pl.BlockSpec(memory_space=pltpu.VMEM)


def _tile(t):
    return min(TM, t)


def _ffn_fwd(x, gpre, w1g, w3g, w2g, gpost, name):
    t = x.shape[0]
    tm = _tile(t)

    def body(x_ref, gpre_ref, w1_ref, w3_ref, w2_ref, gpost_ref, h_ref, a_ref, b_ref, hm_ref, f_ref):
        xv = x_ref[...]
        nb = _rms(xv, gpre_ref[...]).astype(BF16)
        f = jnp.zeros((tm, D), F32)
        for s in range(NSH):
            a = _nn(nb, w1_ref[s])
            b = _nn(nb, w3_ref[s])
            hmb = ((a * jax.nn.sigmoid(a)) * b).astype(BF16)
            a_ref[s] = a.astype(BF16)
            b_ref[s] = b.astype(BF16)
            hm_ref[s] = hmb
            f = f + _nn(hmb, w2_ref[s])
        f_ref[...] = f
        h_ref[...] = xv + 0.5 * _rms(f, gpost_ref[...])

    sh = pl.BlockSpec((NSH, tm, FF_S), lambda i: (0, i, 0))
    act = jax.ShapeDtypeStruct((NSH, t, FF_S), BF16)
    return pl.pallas_call(
        body, grid=(t // tm,), name=name,
        in_specs=[_row_spec(tm, D), _vec_spec(D), _WHOLE, _WHOLE, _WHOLE, _vec_spec(D)],
        out_specs=[_row_spec(tm, D), sh, sh, sh, _row_spec(tm, D)],
        out_shape=[jax.ShapeDtypeStruct((t, D), F32), act, act, act, jax.ShapeDtypeStruct((t, D), F32)],
        compiler_params=_params("arbitrary"),
    )(x, gpre, w1g, w3g, w2g, gpost)


def _loss_dy(y, target):
    t = y.shape[0]
    tm = _tile(t)

    def body(y_ref, t_ref, dy_ref, l_ref):
        @pl.when(pl.program_id(0) == 0)
        def _():
            l_ref[...] = jnp.zeros_like(l_ref)

        e = y_ref[...] - t_ref[...]
        dy_ref[...] = e * (1.0 / D)
        sq = jnp.sum(jnp.sum(e * e, axis=0, keepdims=True), axis=1, keepdims=True)
        l_ref[...] = l_ref[...] + sq

    return pl.pallas_call(
        body, grid=(t // tm,), name="loss_dy",
        in_specs=[_row_spec(tm, D), _row_spec(tm, D)],
        out_specs=[_row_spec(tm, D), pl.BlockSpec((1, 128), lambda i: (0, 0))],
        out_shape=[jax.ShapeDtypeStruct((t, D), F32), jax.ShapeDtypeStruct((1, 128), F32)],
        compiler_params=_params("arbitrary"),
    )(y, target)


def _mix_proj(h1, gmix, w_in_g, w_gate_g, b_gate):
    t = h1.shape[0]
    tm = _tile(t)

    def body(h_ref, g_ref, win_ref, wg_ref, bg_ref, u_ref, q_ref, k_ref, v_ref, xr_ref, xg_ref, gate_ref):
        ub = _rms(h_ref[...], g_ref[...]).astype(BF16)
        u_ref[...] = ub
        p0 = _nn(ub, win_ref[0])
        q_ref[:, 0:896] = p0.astype(BF16)
        p1 = _nn(ub, win_ref[1])
        q_ref[:, 896:1024] = p1[:, 0:128].astype(BF16)
        k_ref[...] = p1[:, 128:384].astype(BF16)
        v_ref[...] = p1[:, 384:640].astype(BF16)
        xr_ref[:, 0:256] = p1[:, 640:896]
        p2 = _nn(ub, win_ref[2])
        xr_ref[:, 256:1024] = p2[:, 0:768]
        xg_ref[:, 0:128] = p2[:, 768:896]
        xg_ref[:, 128:1024] = _nn(ub, win_ref[3])
        for s in range(NSH):
            sl = slice(s * GATE_S, (s + 1) * GATE_S)
            gate_ref[:, sl] = jax.nn.sigmoid(_nn(ub, wg_ref[s]) + bg_ref[:, sl])

    return pl.pallas_call(
        body, grid=(t // tm,), name="mix_proj",
        in_specs=[_row_spec(tm, D), _vec_spec(D), _WHOLE, _WHOLE, _vec_spec(2 * D)],
        out_specs=[_row_spec(tm, D), _row_spec(tm, D), _row_spec(tm, KV_W), _row_spec(tm, KV_W), _row_spec(tm, D),
                   _row_spec(tm, D), _row_spec(tm, 2 * D)],
        out_shape=[jax.ShapeDtypeStruct((t, D), BF16), jax.ShapeDtypeStruct((t, D), BF16),
                   jax.ShapeDtypeStruct((t, KV_W), BF16), jax.ShapeDtypeStruct((t, KV_W), BF16),
                   jax.ShapeDtypeStruct((t, D), F32), jax.ShapeDtypeStruct((t, D), F32),
                   jax.ShapeDtypeStruct((t, 2 * D), F32)],
        compiler_params=_params("arbitrary"),
    )(h1, gmix, w_in_g, w_gate_g, b_gate)


def _rglru_fwd(xr, xg, conv_w, conv_b, wa2, ba, wx2, bx, lam):
    t = xr.shape[0]
    tm = _tile(t)
    nb8 = tm // 8

    def body(xr_ref, xrp_ref, xg_ref, cw_ref, cb_ref, wa_ref, ba_ref, wx_ref, bx_ref, lam_ref,
             hr_ref, yain_ref, xc_ref, r_ref, ig_ref, ext, a_sc, h_sc):
        i = pl.program_id(0)

        @pl.when(i == 0)
        def _():
            h_sc[...] = jnp.zeros_like(h_sc)

        ext[0:8, :] = jnp.where(i == 0, 0.0, xrp_ref[...])
        ext[8:8 + tm, :] = xr_ref[...]
        xc = jnp.broadcast_to(cb_ref[...], (tm, D))
        for tap in range(4):
            xc = xc + ext[pl.ds(5 + tap, tm), :] * cw_ref[tap:tap + 1, :]
        xc_ref[...] = xc
        xcb = xc.astype(BF16)
        for p in range(8):
            sl = slice(p * 128, (p + 1) * 128)
            r_ref[:, sl] = jax.nn.sigmoid(_nn(xcb[:, sl], wa_ref[p]) + ba_ref[:, sl])
            ig_ref[:, sl] = jax.nn.sigmoid(_nn(xcb[:, sl], wx_ref[p]) + bx_ref[:, sl])
        a, s = _lru_coeffs(r_ref[...], _softplus_neg(lam_ref[...]))
        a_sc[...] = a
        hr_ref[...] = s * (ig_ref[...] * xc)

        def blk(j, h):
            st = pl.multiple_of(j * 8, 8)
            a8 = a_sc[pl.ds(st, 8), :]
            u8 = hr_ref[pl.ds(st, 8), :]
            rows = []
            for k in range(8):
                h = a8[k:k + 1, :] * h + u8[k:k + 1, :]
                rows.append(h)
            hr_ref[pl.ds(st, 8), :] = jnp.concatenate(rows, axis=0)
            return h

        h_sc[0:1, :] = lax.fori_loop(0, nb8, blk, h_sc[0:1, :])
        yain_ref[...] = (hr_ref[...] * _gelu(xg_ref[...])).astype(BF16)

    prev = pl.BlockSpec((8, D), lambda i: (jnp.maximum(i * nb8 - 1, 0), 0))
    full = lambda shape: pl.BlockSpec(shape, lambda i: tuple(0 for _ in shape))
    f32 = jax.ShapeDtypeStruct((t, D), F32)
    return pl.pallas_call(
        body, grid=(t // tm,), name="rglru_fwd",
        in_specs=[_row_spec(tm, D), prev, _row_spec(tm, D), full((4, D)), _vec_spec(D), full((8, 128, 128)), _vec_spec(D),
                  full((8, 128, 128)), _vec_spec(D), _vec_spec(D)],
        out_specs=[_row_spec(tm, D)] * 5,
        out_shape=[f32, jax.ShapeDtypeStruct((t, D), BF16), f32, f32, f32],
        scratch_shapes=[pltpu.VMEM((tm + 8, D), F32), pltpu.VMEM((tm, D), F32), pltpu.VMEM((8, D), F32)],
        compiler_params=_params("arbitrary"),
    )(xr, xr, xg, conv_w, conv_b, wa2, ba, wx2, bx, lam)


def _bias_fwd(table_t, onehot_t):
    def body(t_ref, e_ref, o_ref):
        o_ref[...] = jnp.dot(t_ref[...], e_ref[...], preferred_element_type=F32, precision=lax.Precision.HIGHEST)

    return pl.pallas_call(body, out_shape=jax.ShapeDtypeStruct((N_HEADS, CHUNK * KB), F32), name="bias_fwd",
                          compiler_params=_params())(table_t, onehot_t)


def _bias_bwd(dbias_flat, onehot_t, ds_rows):
    def body(d_ref, e_ref, s_ref, o_ref, so_ref):
        o_ref[...] = lax.dot_general(d_ref[...], e_ref[...], (((1,), (1,)), ((), ())), preferred_element_type=F32,
                                     precision=lax.Precision.HIGHEST)
        so_ref[...] = jnp.zeros_like(so_ref)
        for r in range(4):
            so_ref[:, r:r + 1] = jnp.sum(s_ref[:, r * CHUNK:(r + 1) * CHUNK], axis=1, keepdims=True)

    return pl.pallas_call(body, out_shape=[jax.ShapeDtypeStruct((N_HEADS, N_BUCKETS), F32), jax.ShapeDtypeStruct((8, 128), F32)],
                          name="bias_bwd", compiler_params=_params())(dbias_flat, onehot_t, ds_rows)


def _stack_heads(q):
    return jnp.concatenate(
        [jnp.concatenate([q[:, (4 * g + r) * HEAD_DIM:(4 * g + r + 1) * HEAD_DIM] for g in range(4)], axis=1)
         for r in range(4)], axis=0)


def _unstack_heads(o):
    return jnp.concatenate([o[r * CHUNK:(r + 1) * CHUNK, g * HEAD_DIM:(g + 1) * HEAD_DIM] for g in range(4) for r in range(4)],
                           axis=1)


def _block_diag(w, mask):
    return jnp.concatenate([w] * 4, axis=0) * mask


def _attn_softmax(q_all, kbd, bias_t, sink_rows, c):
    s = _nt(kbd, q_all) * (HEAD_DIM ** -0.5) + bias_t
    j = lax.broadcasted_iota(jnp.int32, (4 * KP, 1), 0) % KP
    s = jnp.where((j < KB) & (j + c * CHUNK >= PAD_KEYS), s, NEG_INF)
    ps, sinks = [], []
    for g in range(4):
        sg = s[g * KP:(g + 1) * KP, :]
        sink = sink_rows[g:g + 1, :]
        m = jnp.maximum(jnp.max(sg, axis=0, keepdims=True), sink)
        e = jnp.exp(sg - m)
        es = jnp.exp(sink - m)
        inv = 1.0 / (jnp.sum(e, axis=0, keepdims=True) + es)
        ps.append(e * inv)
        sinks.append(es * inv)
    return ps, sinks


def _attn_fwd(sink_rows, q, kp, vp, bias_t, mask):
    t = q.shape[0]

    def body(sink_ref, q_ref, kp_ref, vp_ref, bias_ref, mask_ref, o_ref):
        c = pl.program_id(0)
        st = pl.multiple_of(c * CHUNK, CHUNK)
        kbd = _block_diag(kp_ref[pl.ds(st, KP), :], mask_ref[...])
        vbd = _block_diag(vp_ref[pl.ds(st, KP), :], mask_ref[...])
        ps, _ = _attn_softmax(_stack_heads(q_ref[...]), kbd, bias_ref[...], sink_ref[...], c)
        p_t = jnp.concatenate(ps, axis=0).astype(BF16)
        o_ref[...] = _unstack_heads(_tn(p_t, vbd)).astype(BF16)

    return pl.pallas_call(
        body, grid=(t // CHUNK,), name="attn_fwd",
        in_specs=[_WHOLE, _row_spec(CHUNK, D), _WHOLE, _WHOLE, _WHOLE, _WHOLE],
        out_specs=_row_spec(CHUNK, D),
        out_shape=jax.ShapeDtypeStruct((t, D), BF16),
        compiler_params=_params("arbitrary"),
    )(sink_rows, q, kp, vp, bias_t, mask)


def _merge_fwd(yain, o, gate, h1, w_lru, w_att, w_o, gpost):
    t = h1.shape[0]
    tm = _tile(t)

    def body(ya_ref, o_ref, g_ref, h_ref, wl_ref, wa_ref, wo_ref, gp_ref, h2_ref, mo_ref, mg_ref, ya_out, yb_out):
        ya = _nn(ya_ref[...], wl_ref[...])
        yb = _nn(o_ref[...], wa_ref[...])
        mg = (g_ref[:, 0:D] * ya + g_ref[:, D:2 * D] * yb).astype(BF16)
        mo = _nn(mg, wo_ref[...])
        ya_out[...] = ya.astype(BF16)
        yb_out[...] = yb.astype(BF16)
        mg_ref[...] = mg
        mo_ref[...] = mo
        h2_ref[...] = h_ref[...] + _rms(mo, gp_ref[...])

    f32 = jax.ShapeDtypeStruct((t, D), F32)
    b16 = jax.ShapeDtypeStruct((t, D), BF16)
    return pl.pallas_call(
        body, grid=(t // tm,), name="merge_fwd",
        in_specs=[_row_spec(tm, D), _row_spec(tm, D), _row_spec(tm, 2 * D), _row_spec(tm, D), _WHOLE, _WHOLE, _WHOLE,
                  _vec_spec(D)],
        out_specs=[_row_spec(tm, D)] * 5,
        out_shape=[f32, f32, b16, b16, b16],
        compiler_params=_params("arbitrary"),
    )(yain, o, gate, h1, w_lru, w_att, w_o, gpost)


def _ffn_bwd(dh, x, f, a, b, gpre, gpost, w1g, w3g, w2g, name):
    t = x.shape[0]
    tm = _tile(t)

    def body(dh_ref, x_ref, f_ref, a_ref, b_ref, gpre_ref, gpost_ref, w1_ref, w3_ref, w2_ref,
             dx_ref, n_ref, da_ref, db_ref, df_ref, dgpre_ref, dgpost_ref):
        @pl.when(pl.program_id(0) == 0)
        def _():
            dgpre_ref[...] = jnp.zeros_like(dgpre_ref)
            dgpost_ref[...] = jnp.zeros_like(dgpost_ref)

        dhv = dh_ref[...]
        xv = x_ref[...]
        df, dgp = _rms_bwd(0.5 * dhv, f_ref[...], gpost_ref[...])
        dgpost_ref[...] += dgp
        dfb = df.astype(BF16)
        df_ref[...] = dfb
        n_ref[...] = _rms(xv, gpre_ref[...]).astype(BF16)
        dn = jnp.zeros((tm, D), F32)
        for s in range(NSH):
            av = a_ref[s].astype(F32)
            bv = b_ref[s].astype(F32)
            sg = jax.nn.sigmoid(av)
            dhm = _nt(dfb, w2_ref[s])
            dab = (dhm * bv * (sg * (1.0 + av * (1.0 - sg)))).astype(BF16)
            dbb = (dhm * (av * sg)).astype(BF16)
            da_ref[s] = dab
            db_ref[s] = dbb
            dn = dn + _nt(dab, w1_ref[s]) + _nt(dbb, w3_ref[s])
        dxn, dg = _rms_bwd(dn, xv, gpre_ref[...])
        dgpre_ref[...] += dg
        dx_ref[...] = dhv + dxn

    sh = pl.BlockSpec((NSH, tm, FF_S), lambda i: (0, i, 0))
    act = jax.ShapeDtypeStruct((NSH, t, FF_S), BF16)
    vec = jax.ShapeDtypeStruct((1, D), F32)
    return pl.pallas_call(
        body, grid=(t // tm,), name=name,
        in_specs=[_row_spec(tm, D), _row_spec(tm, D), _row_spec(tm, D), sh, sh, _vec_spec(D), _vec_spec(D), _WHOLE, _WHOLE,
                  _WHOLE],
        out_specs=[_row_spec(tm, D), _row_spec(tm, D), sh, sh, _row_spec(tm, D), _vec_spec(D), _vec_spec(D)],
        out_shape=[jax.ShapeDtypeStruct((t, D), F32), jax.ShapeDtypeStruct((t, D), BF16), act, act,
                   jax.ShapeDtypeStruct((t, D), BF16), vec, vec],
        compiler_params=_params("arbitrary"),
    )(dh, x, f, a, b, gpre, gpost, w1g, w3g, w2g)


def _wgrad(a, b, a_spec, b_spec, out_spec, out_shape, grid, name):
    def body(a_ref, b_ref, o_ref):
        o_ref[...] = _tn(a_ref[...], b_ref[...])

    return pl.pallas_call(body, grid=grid, name=name, in_specs=[a_spec, b_spec], out_specs=out_spec,
                          out_shape=jax.ShapeDtypeStruct(out_shape, F32),
                          compiler_params=_params(*("arbitrary",) * len(grid)))(a, b)


def _wgrad_cols(act, dsh, width, name):
    t = act.shape[0]
    if dsh.ndim == 3:
        b_spec = pl.BlockSpec((None, t, width), lambda s, k: (s, 0, 0))
    else:
        b_spec = pl.BlockSpec((t, width), lambda s, k: (0, s))
    return _wgrad(act, dsh, pl.BlockSpec((t, 512), lambda s, k: (0, k)), b_spec,
                  pl.BlockSpec((None, 512, width), lambda s, k: (s, k, 0)), (NSH, D, width), (NSH, 2), name)


def _wgrad_rows(hm, df, name):
    t = df.shape[0]
    return _wgrad(hm, df, pl.BlockSpec((None, t, FF_S), lambda s, j: (s, 0, 0)), pl.BlockSpec((t, 512), lambda s, j: (0, j)),
                  pl.BlockSpec((None, FF_S, 512), lambda s, j: (s, 0, j)), (NSH, FF_S, D), (NSH, 2), name)


def _wgrad_sq(a, b, name):
    t = a.shape[0]
    return _wgrad(a, b, pl.BlockSpec((t, 512), lambda i, j: (0, i)), pl.BlockSpec((t, 512), lambda i, j: (0, j)),
                  pl.BlockSpec((512, 512), lambda i, j: (i, j)), (D, D), (2, 2), name)


def _mix_bwd1(dh2, mo, gpost, gate, ya, yb, xg, hr, w_o, w_lru, w_att):
    t = dh2.shape[0]
    tm = _tile(t)

    def body(dh_ref, mo_ref, gp_ref, g_ref, ya_ref, yb_ref, xg_ref, hr_ref, wo_ref, wl_ref, wa_ref,
             dmo_ref, dya_ref, dyb_ref, dgate_ref, dhr_ref, dxg_ref, do_ref, dgp_ref, dbg_ref):
        @pl.when(pl.program_id(0) == 0)
        def _():
            dgp_ref[...] = jnp.zeros_like(dgp_ref)
            dbg_ref[...] = jnp.zeros_like(dbg_ref)

        dmo, dgp = _rms_bwd(dh_ref[...], mo_ref[...], gp_ref[...])
        dgp_ref[...] += dgp
        dmob = dmo.astype(BF16)
        dmo_ref[...] = dmob
        dm = _nt(dmob, wo_ref[...])
        g0 = g_ref[:, 0:D]
        g1 = g_ref[:, D:2 * D]
        dyab = (dm * g0).astype(BF16)
        dybb = (dm * g1).astype(BF16)
        dya_ref[...] = dyab
        dyb_ref[...] = dybb
        dg0 = dm * ya_ref[...].astype(F32) * (g0 * (1.0 - g0))
        dg1 = dm * yb_ref[...].astype(F32) * (g1 * (1.0 - g1))
        dgate_ref[:, 0:D] = dg0.astype(BF16)
        dgate_ref[:, D:2 * D] = dg1.astype(BF16)
        dbg_ref[:, 0:D] += jnp.sum(dg0, axis=0, keepdims=True)
        dbg_ref[:, D:2 * D] += jnp.sum(dg1, axis=0, keepdims=True)
        dyain = _nt(dyab, wl_ref[...])
        do_ref[...] = _nt(dybb, wa_ref[...]).astype(BF16)
        xgv = xg_ref[...]
        dhr_ref[...] = dyain * _gelu(xgv)
        dxg_ref[...] = (dyain * hr_ref[...] * _gelu_grad(xgv)).astype(BF16)

    b16 = jax.ShapeDtypeStruct((t, D), BF16)
    return pl.pallas_call(
        body, grid=(t // tm,), name="mix_bwd1",
        in_specs=[_row_spec(tm, D), _row_spec(tm, D), _vec_spec(D), _row_spec(tm, 2 * D), _row_spec(tm, D), _row_spec(tm, D),
                  _row_spec(tm, D), _row_spec(tm, D), _WHOLE, _WHOLE, _WHOLE],
        out_specs=[_row_spec(tm, D), _row_spec(tm, D), _row_spec(tm, D), _row_spec(tm, 2 * D), _row_spec(tm, D),
                   _row_spec(tm, D), _row_spec(tm, D), _vec_spec(D), _vec_spec(2 * D)],
        out_shape=[b16, b16, b16, jax.ShapeDtypeStruct((t, 2 * D), BF16), jax.ShapeDtypeStruct((t, D), F32), b16, b16,
                   jax.ShapeDtypeStruct((1, D), F32), jax.ShapeDtypeStruct((1, 2 * D), F32)],
        compiler_params=_params("arbitrary"),
    )(dh2, mo, gpost, gate, ya, yb, xg, hr, w_o, w_lru, w_att)


def _rglru_bwd(dhr, hr, xc, r, ig, xr, conv_w, wa2, wx2, lam):
    t = dhr.shape[0]
    tm = _tile(t)
    nb8 = tm // 8
    nt = t // tm

    def body(dhr_ref, hr_ref, hrp_ref, xc_ref, r_ref, ig_ref, xr_ref, xrp_ref, cw_ref, wa_ref, wx_ref, lam_ref,
             dxr_ref, dwa_ref, dwx_ref, dba_ref, dbx_ref, dlam_ref, dcw_ref, dcb_ref,
             ext_h, ext_x, ext_d, a_sc, g_sc, c_sc, nxt_sc):
        i = pl.program_id(0)
        first_tile = i == nt - 1

        @pl.when(i == 0)
        def _():
            c_sc[...] = jnp.zeros_like(c_sc)
            nxt_sc[...] = jnp.zeros_like(nxt_sc)
            for ref in (dwa_ref, dwx_ref, dba_ref, dbx_ref, dlam_ref, dcw_ref, dcb_ref):
                ref[...] = jnp.zeros_like(ref)

        lamv = lam_ref[...]
        sp = _softplus_neg(lamv)
        rv = r_ref[...]
        igv = ig_ref[...]
        xcv = xc_ref[...]
        a, s = _lru_coeffs(rv, sp)
        a_sc[...] = a

        def blk(jj, c):
            st = pl.multiple_of((nb8 - 1 - jj) * 8, 8)
            d8 = dhr_ref[pl.ds(st, 8), :]
            a8 = a_sc[pl.ds(st, 8), :]
            rows = [None] * 8
            for k in range(7, -1, -1):
                g = d8[k:k + 1, :] + c
                c = a8[k:k + 1, :] * g
                rows[k] = g
            g_sc[pl.ds(st, 8), :] = jnp.concatenate(rows, axis=0)
            return c

        c_sc[0:1, :] = lax.fori_loop(0, nb8, blk, c_sc[0:1, :])
        g = g_sc[...]
        ext_h[0:8, :] = jnp.where(first_tile, 0.0, hrp_ref[...])
        ext_h[8:8 + tm, :] = hr_ref[...]
        hprev = ext_h[pl.ds(7, tm), :]
        d_s = g * (igv * xcv)
        dig = g * s * xcv
        dxc = g * s * igv
        dla = (g * hprev) * a - d_s * ((a * a) / s)
        dr_pre = (dla * (-LRU_C * sp)) * (rv * (1.0 - rv))
        di_pre = dig * (igv * (1.0 - igv))
        dlam_ref[...] += jnp.sum(dla * (LRU_C * rv), axis=0, keepdims=True) * jax.nn.sigmoid(-lamv)
        dba_ref[...] += jnp.sum(dr_pre, axis=0, keepdims=True)
        dbx_ref[...] += jnp.sum(di_pre, axis=0, keepdims=True)
        drb = dr_pre.astype(BF16)
        dib = di_pre.astype(BF16)
        xcb = xcv.astype(BF16)
        ext_d[tm:tm + 8, :] = nxt_sc[...]
        for p in range(8):
            sl = slice(p * 128, (p + 1) * 128)
            ext_d[0:tm, sl] = dxc[:, sl] + _nt(drb[:, sl], wa_ref[p]) + _nt(dib[:, sl], wx_ref[p])
            dwa_ref[p] += _tn(xcb[:, sl], drb[:, sl])
            dwx_ref[p] += _tn(xcb[:, sl], dib[:, sl])
        dxcv = ext_d[0:tm, :]
        nxt_sc[...] = ext_d[0:8, :]
        dcb_ref[...] += jnp.sum(dxcv, axis=0, keepdims=True)
        ext_x[0:8, :] = jnp.where(first_tile, 0.0, xrp_ref[...])
        ext_x[8:8 + tm, :] = xr_ref[...]
        dxr = jnp.zeros((tm, D), F32)
        for tap in range(4):
            dxr = dxr + ext_d[pl.ds(3 - tap, tm), :] * cw_ref[tap:tap + 1, :]
            dcw_ref[tap:tap + 1, :] += jnp.sum(dxcv * ext_x[pl.ds(5 + tap, tm), :], axis=0, keepdims=True)
        dxr_ref[...] = dxr.astype(BF16)

    rev = pl.BlockSpec((tm, D), lambda i: (nt - 1 - i, 0))
    prev = pl.BlockSpec((8, D), lambda i: (jnp.maximum((nt - 1 - i) * nb8 - 1, 0), 0))
    full = lambda shape: pl.BlockSpec(shape, lambda i: tuple(0 for _ in shape))
    vec = jax.ShapeDtypeStruct((1, D), F32)
    blocks = jax.ShapeDtypeStruct((8, 128, 128), F32)
    return pl.pallas_call(
        body, grid=(nt,), name="rglru_bwd",
        in_specs=[rev, rev, prev, rev, rev, rev, rev, prev, full((4, D)), full((8, 128, 128)), full((8, 128, 128)),
                  _vec_spec(D)],
        out_specs=[rev, full((8, 128, 128)), full((8, 128, 128)), _vec_spec(D), _vec_spec(D), _vec_spec(D), full((4, D)),
                   _vec_spec(D)],
        out_shape=[jax.ShapeDtypeStruct((t, D), BF16), blocks, blocks, vec, vec, vec, jax.ShapeDtypeStruct((4, D), F32), vec],
        scratch_shapes=[pltpu.VMEM((tm + 8, D), F32), pltpu.VMEM((tm + 8, D), F32), pltpu.VMEM((tm + 8, D), F32),
                        pltpu.VMEM((tm, D), F32), pltpu.VMEM((tm, D), F32), pltpu.VMEM((8, D), F32), pltpu.VMEM((8, D), F32)],
        compiler_params=_params("arbitrary"),
    )(dhr, hr, hr, xc, r, ig, xr, xr, conv_w, wa2, wx2, lam)


def _attn_bwd(sink_rows, q, kp, vp, bias_t, mask, do):
    t = q.shape[0]
    tp = kp.shape[0]

    def body(sink_ref, q_ref, kp_ref, vp_ref, bias_ref, mask_ref, do_ref, dq_ref, dk_ref, dv_ref, dbias_ref, ds_ref):
        c = pl.program_id(0)

        @pl.when(c == 0)
        def _():
            for ref in (dk_ref, dv_ref, dbias_ref, ds_ref):
                ref[...] = jnp.zeros_like(ref)

        st = pl.multiple_of(c * CHUNK, CHUNK)
        maskv = mask_ref[...]
        kbd = _block_diag(kp_ref[pl.ds(st, KP), :], maskv)
        vbd = _block_diag(vp_ref[pl.ds(st, KP), :], maskv)
        q_all = _stack_heads(q_ref[...])
        do_all = _stack_heads(do_ref[...])
        ps, sinks = _attn_softmax(q_all, kbd, bias_ref[...], sink_ref[...], c)
        dp = _nt(vbd, do_all)
        dscs = []
        for g in range(4):
            dpg = dp[g * KP:(g + 1) * KP, :]
            delta = jnp.sum(ps[g] * dpg, axis=0, keepdims=True)
            dscs.append(ps[g] * (dpg - delta))
            ds_ref[g:g + 1, :] += -(sinks[g] * delta)
        dsc = jnp.concatenate(dscs, axis=0)
        dbias_ref[...] += dsc
        dsb = (dsc * (HEAD_DIM ** -0.5)).astype(BF16)
        dq_ref[...] = _unstack_heads(_tn(dsb, kbd)).astype(BF16)

        lane_group = lax.broadcasted_iota(jnp.int32, (1, 4 * HEAD_DIM), 1) // HEAD_DIM

        def own_blocks(full):
            out = full[0:KP]
            for g in range(1, 4):
                out = jnp.where(lane_group == g, full[g * KP:(g + 1) * KP], out)
            return out

        dk_ref[pl.ds(st, KP), :] += own_blocks(_nn(dsb, q_all))
        dv_ref[pl.ds(st, KP), :] += own_blocks(_nn(jnp.concatenate(ps, axis=0).astype(BF16), do_all))

    full = lambda shape: pl.BlockSpec(shape, lambda i: tuple(0 for _ in shape))
    return pl.pallas_call(
        body, grid=(t // CHUNK,), name="attn_bwd",
        in_specs=[_WHOLE, _row_spec(CHUNK, D), _WHOLE, _WHOLE, _WHOLE, _WHOLE, _row_spec(CHUNK, D)],
        out_specs=[_row_spec(CHUNK, D), full((tp, KV_W)), full((tp, KV_W)), full((4 * KP, 4 * CHUNK)), full((8, 4 * CHUNK))],
        out_shape=[jax.ShapeDtypeStruct((t, D), BF16), jax.ShapeDtypeStruct((tp, KV_W), F32),
                   jax.ShapeDtypeStruct((tp, KV_W), F32), jax.ShapeDtypeStruct((4 * KP, 4 * CHUNK), F32),
                   jax.ShapeDtypeStruct((8, 4 * CHUNK), F32)],
        compiler_params=_params("arbitrary"),
    )(sink_rows, q, kp, vp, bias_t, mask, do)


def _mix_bwd2(dproj, dgate, h1, dh2, gmix, w_in_g, w_gate_g):
    t = h1.shape[0]
    tm = _tile(t)

    def body(dp_ref, dg_ref, h_ref, dh_ref, g_ref, win_ref, wg_ref, dh1_ref, dgm_ref):
        @pl.when(pl.program_id(0) == 0)
        def _():
            dgm_ref[...] = jnp.zeros_like(dgm_ref)

        du = jnp.zeros((tm, D), F32)
        for s in range(NSH):
            du = du + _nt(dp_ref[:, s * IN_S:(s + 1) * IN_S], win_ref[s])
            du = du + _nt(dg_ref[:, s * GATE_S:(s + 1) * GATE_S], wg_ref[s])
        dxn, dg = _rms_bwd(du, h_ref[...], g_ref[...])
        dgm_ref[...] += dg
        dh1_ref[...] = dh_ref[...] + dxn

    return pl.pallas_call(
        body, grid=(t // tm,), name="mix_bwd2",
        in_specs=[_row_spec(tm, NSH * IN_S), _row_spec(tm, 2 * D), _row_spec(tm, D), _row_spec(tm, D), _vec_spec(D), _WHOLE,
                  _WHOLE],
        out_specs=[_row_spec(tm, D), _vec_spec(D)],
        out_shape=[jax.ShapeDtypeStruct((t, D), F32), jax.ShapeDtypeStruct((1, D), F32)],
        compiler_params=_params("arbitrary"),
    )(dproj, dgate, h1, dh2, gmix, w_in_g, w_gate_g)


def _band_onehot():
    nb = N_BUCKETS // 2
    max_exact = nb // 2
    rel = jnp.arange(KB)[None, :] - PAD_KEYS - jnp.arange(CHUNK)[:, None]
    ret = jnp.where(rel > 0, nb, 0)
    n = jnp.abs(rel)
    nf = jnp.maximum(n, 1).astype(jnp.float32)
    large = max_exact + (jnp.log(nf / max_exact) / math.log(128 / max_exact) * (nb - max_exact)).astype(jnp.int32)
    large = jnp.minimum(large, nb - 1)
    buckets = (ret + jnp.where(n < max_exact, n, large)).reshape(1, CHUNK * KB)
    return (buckets == jnp.arange(N_BUCKETS)[:, None]).astype(F32)


def _pair_blocks(w):
    z = jnp.zeros((8, 128, 128), w.dtype)
    return z.at[:, 0:64, 0:64].set(w[0::2]).at[:, 64:128, 64:128].set(w[1::2])


def _unpair_blocks(w2):
    return jnp.stack([w2[:, 0:64, 0:64], w2[:, 64:128, 64:128]], axis=1).reshape(16, 64, 64)


def _local_step(x, target, weights, sm):
    row = lambda v: v.reshape(1, -1)
    wg = dict(weights("ffn1", x))
    sm = dict(sm, conv_w=wg["conv_w"])
    onehot_t = _band_onehot()
    bias = _bias_fwd(sm["rel_bias"].T, onehot_t).reshape(4, 4, CHUNK, KB)
    bias_t = jnp.pad(jnp.transpose(bias, (0, 3, 1, 2)), ((0, 0), (0, KP - KB), (0, 0), (0, 0))).reshape(4 * KP, 4 * CHUNK)
    sink_rows = jnp.pad(jnp.repeat(sm["attn_sinks"].reshape(4, 4), CHUNK, axis=1), ((0, 4), (0, 0)))
    grp = jnp.arange(4 * KP)[:, None] // KP == jnp.arange(4 * HEAD_DIM)[None, :] // HEAD_DIM
    mask = (grp & (jnp.arange(4 * KP)[:, None] % KP < KB)).astype(BF16)
    wa2 = _pair_blocks(sm["rg_a_w"]).astype(BF16)
    wx2 = _pair_blocks(sm["rg_x_w"]).astype(BF16)

    h1, a1, b1, hm1, f1 = _ffn_fwd(x, row(sm["ffn1_pre_g"]), wg["ffn1_w1"], wg["ffn1_w3"], wg["ffn1_w2"],
                                   row(sm["ffn1_post_g"]), "ffn1_fwd")
    wg.update(weights("mix", h1))
    w_lru = wg["w_lru_out"].reshape(D, D)
    w_att = wg["w_attn_out"].reshape(D, D)
    w_o = wg["w_o"].reshape(D, D)
    u, q, k, v, xr, xg, gate = _mix_proj(h1, row(sm["mix_pre_g"]), wg["w_in"], wg["w_gate"], row(sm["b_gate"]))
    hr, yain, xc, r, ig = _rglru_fwd(xr, xg, sm["conv_w"], row(sm["conv_b"]), wa2, row(sm["rg_a_b"]), wx2,
                                     row(sm["rg_x_b"]), row(sm["lru_lambda"]))
    kp = jnp.pad(k, ((PAD_KEYS, KP - KB), (0, 0)))
    vp = jnp.pad(v, ((PAD_KEYS, KP - KB), (0, 0)))
    o = _attn_fwd(sink_rows, q, kp, vp, bias_t, mask)
    wg.update(weights("ffn2", o))
    h2, mo, merged, ya, yb = _merge_fwd(yain, o, gate, h1, w_lru, w_att, w_o, row(sm["mix_post_g"]))
    y, a2, b2, hm2, f2 = _ffn_fwd(h2, row(sm["ffn2_pre_g"]), wg["ffn2_w1"], wg["ffn2_w3"], wg["ffn2_w2"],
                                  row(sm["ffn2_post_g"]), "ffn2_fwd")
    dy, sq = _loss_dy(y, target)

    big, small = {}, {}
    dh2, n2, da2, db2, df2, small["ffn2_pre_g"], small["ffn2_post_g"] = _ffn_bwd(
        dy, h2, f2, a2, b2, row(sm["ffn2_pre_g"]), row(sm["ffn2_post_g"]), wg["ffn2_w1"], wg["ffn2_w3"], wg["ffn2_w2"],
        "ffn2_bwd")
    big["ffn2_w1"] = _wgrad_cols(n2, da2, FF_S, "dw_ffn2_w1")
    big["ffn2_w3"] = _wgrad_cols(n2, db2, FF_S, "dw_ffn2_w3")
    big["ffn2_w2"] = _wgrad_rows(hm2, df2, "dw_ffn2_w2")
    dmo, dya, dyb, dgate, dhr, dxg, do, small["mix_post_g"], small["b_gate"] = _mix_bwd1(
        dh2, mo, row(sm["mix_post_g"]), gate, ya, yb, xg, hr, w_o, w_lru, w_att)
    big["w_o"] = _wgrad_sq(merged, dmo, "dw_w_o").reshape(NSH, D // NSH, D)
    big["w_lru_out"] = _wgrad_sq(yain, dya, "dw_w_lru_out").reshape(NSH, D // NSH, D)
    big["w_attn_out"] = _wgrad_sq(o, dyb, "dw_w_attn_out").reshape(NSH, D // NSH, D)
    (dxr, dwa2, dwx2, small["rg_a_b"], small["rg_x_b"], small["lru_lambda"], small["conv_w"], small["conv_b"]) = _rglru_bwd(
        dhr, hr, xc, r, ig, xr, sm["conv_w"], wa2, wx2, row(sm["lru_lambda"]))
    small["rg_a_w"] = _unpair_blocks(dwa2)
    small["rg_x_w"] = _unpair_blocks(dwx2)
    dq, dkp, dvp, dbias_t, ds_rows = _attn_bwd(sink_rows, q, kp, vp, bias_t, mask, do)
    dbias = jnp.transpose(dbias_t.reshape(4, KP, 4, CHUNK)[:, :KB], (0, 2, 3, 1)).reshape(N_HEADS, CHUNK * KB)
    drel_t, dsinks = _bias_bwd(dbias, onehot_t, ds_rows)
    small["attn_sinks"] = dsinks[0:4, 0:4].reshape(N_HEADS)
    small["rel_bias"] = drel_t.T
    t = x.shape[0]
    dproj = jnp.concatenate([dq, dkp[PAD_KEYS:PAD_KEYS + t].astype(BF16), dvp[PAD_KEYS:PAD_KEYS + t].astype(BF16), dxr, dxg],
                            axis=1)
    big["w_in"] = _wgrad_cols(u, dproj, IN_S, "dw_w_in")
    big["w_gate"] = _wgrad_cols(u, dgate, GATE_S, "dw_w_gate")
    dh1, small["mix_pre_g"] = _mix_bwd2(dproj, dgate, h1, dh2, row(sm["mix_pre_g"]), wg["w_in"], wg["w_gate"])
    dx, n1, da1, db1, df1, small["ffn1_pre_g"], small["ffn1_post_g"] = _ffn_bwd(
        dh1, x, f1, a1, b1, row(sm["ffn1_pre_g"]), row(sm["ffn1_post_g"]), wg["ffn1_w1"], wg["ffn1_w3"], wg["ffn1_w2"],
        "ffn1_bwd")
    big["ffn1_w1"] = _wgrad_cols(n1, da1, FF_S, "dw_ffn1_w1")
    big["ffn1_w3"] = _wgrad_cols(n1, db1, FF_S, "dw_ffn1_w3")
    big["ffn1_w2"] = _wgrad_rows(hm1, df1, "dw_ffn1_w2")
    return sq, dx, big, small


_ANY = pl.BlockSpec(memory_space=pl.ANY)


def _place():
    return lax.axis_index("x"), lax.axis_index("y"), lax.axis_index("c")


def _other_chips(x, y):
    return [(1 - x, y), (x, 1 - y), (1 - x, 1 - y)]


_HBM = pl.BlockSpec(memory_space=pltpu.HBM)
_SEM = pl.BlockSpec(memory_space=pltpu.SEMAPHORE)
_EFFECT = pltpu.SideEffectType.DATAFLOW_SIDE_EFFECTING


def _cast_into_slot(w, chip, name):
    r, cc = w.shape
    rows = r // 4

    def body(chip_ref, w_ref, o_ref):
        o_ref[...] = w_ref[...].astype(BF16)

    return pl.pallas_call(
        body, name=name, out_shape=jax.ShapeDtypeStruct((NSH, r, cc), BF16),
        grid_spec=pltpu.PrefetchScalarGridSpec(
            num_scalar_prefetch=1, grid=(4,), in_specs=[pl.BlockSpec((rows, cc), lambda i, chip: (i, 0))],
            out_specs=pl.BlockSpec((None, rows, cc), lambda i, chip: (chip[0], i, 0))),
        compiler_params=_params("arbitrary"))(chip, w)


def _piece(ref, slot, c):
    if ref.dtype == F32:
        return ref.at[slot]
    rh = ref.shape[1] // 2
    return ref.at[slot, pl.ds(pl.multiple_of(c * rh, 16), rh), :]


def _gather_start(stages):
    flat = [b for stage in stages for b in stage]
    n, ns = len(flat), len(stages)

    def body(*refs):
        ins, sems, token = refs[:n], refs[n:n + 2 * ns], refs[-1]
        x, y, c = _place()
        me = 2 * x + y
        k = 0
        for s, stage in enumerate(stages):
            for i in range(len(stage)):
                for j, (px, py) in enumerate(_other_chips(x, y)):
                    piece = _piece(ins[k], me, c)
                    pltpu.make_async_remote_copy(src_ref=piece, dst_ref=piece, send_sem=sems[2 * s].at[3 * i + j],
                                                 recv_sem=sems[2 * s + 1].at[3 * i + j], device_id=(px, py, c),
                                                 device_id_type=MESH).start()
                k += 1
        token[...] = jnp.zeros_like(token)

    sem_shapes = [pltpu.SemaphoreType.DMA((3 * len(stage),)) for stage in stages for _ in range(2)]
    outs = pl.pallas_call(
        body, name="gather_start", in_specs=[_HBM] * n,
        out_specs=[_SEM] * (2 * ns) + [_HBM] * n + [pl.BlockSpec(memory_space=pltpu.VMEM)],
        out_shape=sem_shapes + [pltpu.HBM(b.shape, b.dtype) for b in flat] + [jax.ShapeDtypeStruct((8, 128), F32)],
        input_output_aliases={i: 2 * ns + i for i in range(n)},
        compiler_params=pltpu.CompilerParams(has_side_effects=_EFFECT),
    )(*[pltpu.with_memory_space_constraint(b, pltpu.HBM) for b in flat])
    sems, bufs, token = outs[:2 * ns], list(outs[2 * ns:2 * ns + n]), outs[-1]
    per_stage, k = [], 0
    for s, stage in enumerate(stages):
        per_stage.append((sems[2 * s], sems[2 * s + 1], bufs[k:k + len(stage)]))
        k += len(stage)
    return per_stage, token


def _gather_wait(send_sems, recv_sems, bufs, after, name):
    n = len(bufs)

    def body(*refs):
        ins, ssem, rsem = refs[:n], refs[n], refs[n + 1]
        x, y, c = _place()
        me = 2 * x + y
        for i in range(n):
            for j, (px, py) in enumerate(_other_chips(x, y)):
                cp = pltpu.make_async_remote_copy(src_ref=_piece(ins[i], me, c), dst_ref=_piece(ins[i], 2 * px + py, c),
                                                  send_sem=ssem.at[3 * i + j], recv_sem=rsem.at[3 * i + j],
                                                  device_id=(px, py, c), device_id_type=MESH)
                cp.wait_send()
                cp.wait_recv()

    return pl.pallas_call(
        body, name=name, in_specs=[_HBM] * n + [_SEM, _SEM, _ANY], out_specs=[_HBM] * n,
        out_shape=[pltpu.HBM(b.shape, b.dtype) for b in bufs], input_output_aliases={i: i for i in range(n)},
        compiler_params=pltpu.CompilerParams(has_side_effects=_EFFECT),
    )(*bufs, send_sems, recv_sems, after)


def _sibling_fill(bufs, name):
    n = len(bufs)

    def body(*refs):
        ins, outs = refs[:n], refs[n:2 * n]
        send_sems, recv_sems = refs[2 * n:]
        x, y, c = _place()
        copies = []
        for i in range(n):
            for j, (px, py) in enumerate(_other_chips(x, y)):
                copies.append(pltpu.make_async_remote_copy(
                    src_ref=_piece(ins[i], 2 * px + py, c), dst_ref=_piece(outs[i], 2 * px + py, c),
                    send_sem=send_sems.at[3 * i + j], recv_sem=recv_sems.at[3 * i + j], device_id=(x, y, 1 - c),
                    device_id_type=MESH))
                copies[-1].start()
        for cp in copies:
            cp.wait()

    return pl.pallas_call(
        body, name=name, in_specs=[_ANY] * n, out_specs=[_ANY] * n,
        out_shape=[jax.ShapeDtypeStruct(b.shape, b.dtype) for b in bufs], input_output_aliases={i: i for i in range(n)},
        scratch_shapes=[pltpu.SemaphoreType.DMA((3 * n,)), pltpu.SemaphoreType.DMA((3 * n,))],
        compiler_params=pltpu.CompilerParams(has_side_effects=True),
    )(*bufs)


def _swap_halves(grads):
    n = len(grads)

    def body(*refs):
        ins, outs = refs[:n], refs[n:2 * n]
        send_sems, recv_sems = refs[2 * n:]
        x, y, c = _place()
        copies = []
        for w in range(n):
            rh = ins[w].shape[1] // 2
            theirs = ins[w].at[:, pl.ds(pl.multiple_of((1 - c) * rh, 8), rh), :]
            copies.append(pltpu.make_async_remote_copy(src_ref=theirs, dst_ref=outs[w], send_sem=send_sems.at[w],
                                                       recv_sem=recv_sems.at[w], device_id=(x, y, 1 - c), device_id_type=MESH))
            copies[-1].start()
        for cp in copies:
            cp.wait()

    return pl.pallas_call(
        body, name="swap_halves", in_specs=[_ANY] * n, out_specs=[_ANY] * n,
        out_shape=[jax.ShapeDtypeStruct((NSH, g.shape[1] // 2, g.shape[2]), F32) for g in grads],
        scratch_shapes=[pltpu.SemaphoreType.DMA((n,)), pltpu.SemaphoreType.DMA((n,))],
        compiler_params=pltpu.CompilerParams(has_side_effects=True),
    )(*grads)


def _chip_sum(g, got, name):
    _, r, cc = g.shape
    rh = r // 2

    def body(g_ref, got_ref, hb_ref, own_ref):
        x, y, c = _place()
        s = pl.program_id(0)
        h = g_ref[pl.ds(pl.multiple_of(c * rh, 8), rh), :] + got_ref[...]
        hb_ref[...] = h.astype(BF16)

        @pl.when(s == 2 * x + y)
        def _():
            own_ref[...] = h

    return pl.pallas_call(
        body, grid=(NSH,), name=name,
        in_specs=[pl.BlockSpec((None, r, cc), lambda s: (s, 0, 0)), pl.BlockSpec((None, rh, cc), lambda s: (s, 0, 0))],
        out_specs=[pl.BlockSpec((None, rh, cc), lambda s: (s, 0, 0)), pl.BlockSpec((rh, cc), lambda s: (0, 0))],
        out_shape=[jax.ShapeDtypeStruct((NSH, rh, cc), BF16), jax.ShapeDtypeStruct((rh, cc), F32)],
        compiler_params=_params("arbitrary"),
    )(g, got)


def _send_to_owners(sums):
    n = len(sums)

    def body(*refs):
        ins, outs = refs[:n], refs[n:2 * n]
        send_sems, recv_sems = refs[2 * n:]
        x, y, c = _place()
        copies = []
        for w in range(n):
            for j, (px, py) in enumerate(_other_chips(x, y)):
                copies.append(pltpu.make_async_remote_copy(src_ref=ins[w].at[2 * px + py], dst_ref=outs[w].at[j],
                                                           send_sem=send_sems.at[3 * w + j], recv_sem=recv_sems.at[3 * w + j],
                                                           device_id=(px, py, c), device_id_type=MESH))
                copies[-1].start()
        for cp in copies:
            cp.wait()

    return pl.pallas_call(
        body, name="send_to_owners", in_specs=[_ANY] * n, out_specs=[_ANY] * n,
        out_shape=[jax.ShapeDtypeStruct((3,) + s.shape[1:], BF16) for s in sums],
        scratch_shapes=[pltpu.SemaphoreType.DMA((3 * n,)), pltpu.SemaphoreType.DMA((3 * n,))],
        compiler_params=pltpu.CompilerParams(has_side_effects=True),
    )(*sums)


def _owner_sum(own, got, name):
    rh, cc = own.shape
    rows = rh // 2

    def body(own_ref, got_ref, o_ref):
        o_ref[...] = ((own_ref[...] + got_ref[0].astype(F32)) + got_ref[1].astype(F32)) + got_ref[2].astype(F32)

    return pl.pallas_call(
        body, grid=(2,), name=name,
        in_specs=[pl.BlockSpec((rows, cc), lambda i: (i, 0)), pl.BlockSpec((3, rows, cc), lambda i: (0, i, 0))],
        out_specs=pl.BlockSpec((rows, cc), lambda i: (i, 0)),
        out_shape=jax.ShapeDtypeStruct((rh, cc), F32), compiler_params=_params("arbitrary"),
    )(own, got)


def _send_halves(halves):
    n = len(halves)

    def body(*refs):
        ins, outs = refs[:n], refs[n:2 * n]
        send_sems, recv_sems = refs[2 * n:]
        x, y, c = _place()
        copies = [pltpu.make_async_remote_copy(src_ref=ins[w], dst_ref=outs[w], send_sem=send_sems.at[w], recv_sem=recv_sems.at[w],
                                               device_id=(x, y, 1 - c), device_id_type=MESH) for w in range(n)]
        for cp in copies:
            cp.start()
        for cp in copies:
            cp.wait()

    return pl.pallas_call(
        body, name="send_halves", in_specs=[_ANY] * n, out_specs=[_ANY] * n,
        out_shape=[jax.ShapeDtypeStruct(h.shape, F32) for h in halves],
        scratch_shapes=[pltpu.SemaphoreType.DMA((n,)), pltpu.SemaphoreType.DMA((n,))],
        compiler_params=pltpu.CompilerParams(has_side_effects=True),
    )(*halves)


def _all_reduce_small(part):
    def body(p_ref, o_ref, rbuf, send1, recv1, send2, recv2):
        x, y, c = _place()
        me = 4 * x + 2 * y + c
        peers = []
        for k in range(1, 8):
            px, py, pc = x ^ ((k >> 2) & 1), y ^ ((k >> 1) & 1), c ^ (k & 1)
            peers.append((k, (px, py, pc), 4 * px + 2 * py + pc))

        def rows(d):
            return pl.ds(pl.multiple_of(d * SMALL_SLICE, 8), SMALL_SLICE)

        first = [pltpu.make_async_remote_copy(src_ref=p_ref.at[rows(idx), :], dst_ref=rbuf.at[me], send_sem=send1.at[k],
                                              recv_sem=recv1.at[k], device_id=dev, device_id_type=MESH)
                 for k, dev, idx in peers]
        for cp in first:
            cp.start()
        rbuf[me] = p_ref[rows(me), :]
        for k, dev, idx in peers:
            pltpu.make_async_remote_copy(src_ref=p_ref.at[rows(idx), :], dst_ref=rbuf.at[idx], send_sem=send1.at[k],
                                         recv_sem=recv1.at[k], device_id=dev, device_id_type=MESH).wait_recv()
        acc = rbuf[0]
        for d in range(1, 8):
            acc = acc + rbuf[d]
        o_ref[rows(me), :] = acc
        second = [pltpu.make_async_remote_copy(src_ref=o_ref.at[rows(me), :], dst_ref=o_ref.at[rows(me), :],
                                               send_sem=send2.at[k], recv_sem=recv2.at[k], device_id=dev, device_id_type=MESH)
                  for k, dev, idx in peers]
        for cp in second:
            cp.start()
        for k, dev, idx in peers:
            pltpu.make_async_remote_copy(src_ref=o_ref.at[rows(me), :], dst_ref=o_ref.at[rows(idx), :], send_sem=send2.at[k],
                                         recv_sem=recv2.at[k], device_id=dev, device_id_type=MESH).wait_recv()
        for cp in first + second:
            cp.wait_send()

    return pl.pallas_call(
        body, name="all_reduce_small", in_specs=[_WHOLE], out_specs=_WHOLE,
        out_shape=jax.ShapeDtypeStruct((SMALL_ROWS, 128), F32),
        scratch_shapes=[pltpu.VMEM((8, SMALL_SLICE, 128), F32)] + [pltpu.SemaphoreType.DMA((8,))] * 4,
        compiler_params=pltpu.CompilerParams(has_side_effects=True),
    )(part)


def _adamw_update(w, gv, m, v):
    nm = ADAM_B1 * m + (1.0 - ADAM_B1) * gv
    nv = ADAM_B2 * v + (1.0 - ADAM_B2) * (gv * gv)
    m_hat = nm / (1.0 - ADAM_B1 ** ADAM_STEP)
    v_hat = nv / (1.0 - ADAM_B2 ** ADAM_STEP)
    return -ADAM_LR * (m_hat / (jnp.sqrt(v_hat) + ADAM_EPS) + ADAM_WD * w), nm, nv


def _adamw(w, g, m, v, name):
    rows = w.shape[0] // 4

    def body(w_ref, g_ref, m_ref, v_ref, d_ref, nm_ref, nv_ref):
        d_ref[...], nm_ref[...], nv_ref[...] = _adamw_update(w_ref[...], g_ref[...], m_ref[...], v_ref[...])

    spec = pl.BlockSpec((rows, w.shape[1]), lambda i: (i, 0))
    out = jax.ShapeDtypeStruct(w.shape, F32)
    return pl.pallas_call(body, grid=(4,), in_specs=[spec] * 4, out_specs=[spec] * 3, out_shape=[out] * 3, name=name,
                          compiler_params=_params("arbitrary"))(w, g, m, v)


def _adamw_halves(w, mine, theirs, m, v, name):
    rh, cc = mine.shape
    rows = rh // 2

    def body(w_ref, mine_ref, theirs_ref, m_ref, v_ref, g_ref, d_ref, nm_ref, nv_ref):
        gv = jnp.where(pl.program_id(0) == lax.axis_index("c"), mine_ref[...], theirs_ref[...])
        g_ref[...] = gv
        d_ref[...], nm_ref[...], nv_ref[...] = _adamw_update(w_ref[...], gv, m_ref[...], v_ref[...])

    spec = pl.BlockSpec((rows, cc), lambda h, i: (2 * h + i, 0))
    half = pl.BlockSpec((rows, cc), lambda h, i: (i, 0))
    out = jax.ShapeDtypeStruct(w.shape, F32)
    return pl.pallas_call(body, grid=(2, 2), in_specs=[spec, half, half, spec, spec], out_specs=[spec] * 4,
                          out_shape=[out] * 4, name=name, compiler_params=_params("arbitrary", "arbitrary"))(w, mine, theirs, m, v)


def _pack_small(vals):
    parts = []
    for name, size in SMALL:
        flat = vals[name].reshape(-1).astype(F32)
        parts.append(jnp.pad(flat, (0, size - flat.shape[0])))
    flat = jnp.concatenate(parts)
    return jnp.pad(flat, (0, SMALL_ROWS * 128 - flat.shape[0])).reshape(SMALL_ROWS, 128)


def _unpack_small(packed, shapes):
    flat = packed.reshape(-1)
    out, off = {}, 0
    for name, size in SMALL:
        n = math.prod(shapes[name])
        out[name] = flat[off:off + n].reshape(shapes[name])
        off += size
    return out


def kernel(x, ffn1_pre_g, ffn1_w1, ffn1_w3, ffn1_w2, ffn1_post_g, mix_pre_g, w_in, conv_w, conv_b, rg_a_w, rg_a_b, rg_x_w, rg_x_b, lru_lambda, w_lru_out, attn_sinks, rel_bias, w_attn_out, w_gate, b_gate, w_o, mix_post_g, ffn2_pre_g, ffn2_w1, ffn2_w3, ffn2_w2, ffn2_post_g, loss_target, m_ffn1_pre_g, m_ffn1_w1, m_ffn1_w3, m_ffn1_w2, m_ffn1_post_g, m_mix_pre_g, m_w_in, m_conv_w, m_conv_b, m_rg_a_w, m_rg_a_b, m_rg_x_w, m_rg_x_b, m_lru_lambda, m_w_lru_out, m_attn_sinks, m_rel_bias, m_w_attn_out, m_w_gate, m_b_gate, m_w_o, m_mix_post_g, m_ffn2_pre_g, m_ffn2_w1, m_ffn2_w3, m_ffn2_w2, m_ffn2_post_g, v_ffn1_pre_g, v_ffn1_w1, v_ffn1_w3, v_ffn1_w2, v_ffn1_post_g, v_mix_pre_g, v_w_in, v_conv_w, v_conv_b, v_rg_a_w, v_rg_a_b, v_rg_x_w, v_rg_x_b, v_lru_lambda, v_w_lru_out, v_attn_sinks, v_rel_bias, v_w_attn_out, v_w_gate, v_b_gate, v_w_o, v_mix_post_g, v_ffn2_pre_g, v_ffn2_w1, v_ffn2_w3, v_ffn2_w2, v_ffn2_post_g):
    given = dict(locals())
    chip = 2 * lax.axis_index("x") + lax.axis_index("y")

    chip_arr = jnp.reshape(chip, (1,)).astype(jnp.int32)
    bufs = {n: _cast_into_slot(given[n][0], chip_arr, "cast_" + n) for n in BIG}
    bufs["conv_w"] = lax.dynamic_update_slice(jnp.zeros((NSH, 4, D // NSH), F32), given["conv_w"], (chip, 0, 0))
    stage_names = {"ffn1": ["ffn1_w1", "ffn1_w3", "ffn1_w2", "conv_w"],
                   "mix": ["w_in", "w_gate", "w_lru_out", "w_attn_out", "w_o"],
                   "ffn2": ["ffn2_w1", "ffn2_w3", "ffn2_w2"]}
    in_flight, token = _gather_start([[bufs[n] for n in names] for names in stage_names.values()])
    in_flight = dict(zip(stage_names, in_flight))

    def weights(stage, after):
        names = stage_names[stage]
        send_sems, recv_sems, landing = in_flight[stage]
        landed = _gather_wait(send_sems, recv_sems, landing, after, "gather_wait_" + stage)
        halves = [b for b in landed if b.dtype == BF16]
        out = dict(zip([n for n, b in zip(names, landed) if b.dtype == BF16], _sibling_fill(halves, "sibling_fill_" + stage)))
        if "conv_w" in names:
            out["conv_w"] = jnp.transpose(landed[names.index("conv_w")], (1, 0, 2)).reshape(4, D)
        return out

    small_shapes = {n: given[n].shape for n, _ in SMALL}
    small_shapes["conv_w"] = (1, 4, D)
    sm = {n: (given[n][0] if given[n].shape[0] == 1 and n != "rel_bias" else given[n]) for n, _ in SMALL if n != "conv_w"}

    sq, dx, big, small = _local_step(x[0], loss_target[0], weights, sm)
    loss = lax.psum(sq[0, 0] * (0.5 / D), ("x", "y", "c"))

    full = [big[n] for n in BIG]
    from_sibling = _swap_halves(full)
    sums = [_chip_sum(g, got, "chip_sum_" + n) for n, g, got in zip(BIG, full, from_sibling)]
    from_chips = _send_to_owners([s[0] for s in sums])
    halves = [_owner_sum(s[1], got, "owner_sum_" + n) for n, s, got in zip(BIG, sums, from_chips)]
    from_sibling2 = _send_halves(halves)
    small_g = _unpack_small(_all_reduce_small(_pack_small(small)), small_shapes)

    grads, delta, new_m, new_v = {}, {}, {}, {}
    for n, mine, theirs in zip(BIG, halves, from_sibling2):
        grads[n], delta[n], new_m[n], new_v[n] = (r[None] for r in _adamw_halves(
            given[n][0], mine, theirs, given["m_" + n][0], given["v_" + n][0], "adamw_" + n))

    def widen(a):
        return lax.dynamic_update_slice(jnp.zeros((1, 4, D), F32), a, (0, 0, chip * (D // NSH)))

    packed = [_pack_small({n: (widen(given[pre + n]) if n == "conv_w" else given[pre + n]) for n, _ in SMALL})
              for pre in ("", "m_", "v_")]
    packed_g = _pack_small(small_g)
    outs = _adamw(packed[0], packed_g, packed[1], packed[2], "adamw_small")
    for dst, arr in zip((delta, new_m, new_v), outs):
        dst.update(_unpack_small(arr, small_shapes))
    small_out = dict(small_g)
    for d in (small_out, delta, new_m, new_v):
        d["conv_w"] = lax.dynamic_slice(d["conv_w"], (0, 0, chip * (D // NSH)), (1, 4, D // NSH))
    grads.update(small_out)
    return (loss, dx[None], *[grads[n] for n in WEIGHTS], *[delta[n] for n in WEIGHTS], *[new_m[n] for n in WEIGHTS],
            *[new_v[n] for n in WEIGHTS])
```

```python
import functools
import math

import jax
import jax.numpy as jnp
from jax import lax
from jax.experimental import pallas as pl
from jax.experimental.pallas import tpu as pltpu

F32, BF16 = jnp.float32, jnp.bfloat16
D = 1024
NSH = 4
FF_S = 704
IN_S = 896
GATE_S = 512
KV_W = 256
CHUNK = 64
KB = 192
N_HEADS = 16
HEAD_DIM = 64
N_BUCKETS = 32
KP = 256
PAD_KEYS = 128
RMS_EPS = 1e-6
NEG_INF = -1e30
LRU_C = 8.0
TM = 256
VMEM_LIMIT = 56 * 1024 * 1024
ADAM_LR, ADAM_B1, ADAM_B2, ADAM_EPS, ADAM_WD, ADAM_STEP = 0.001, 0.9, 0.999, 1e-08, 0.01, 10
SMALL_ROWS = 1216
SMALL_SLICE = SMALL_ROWS // 8
MESH = pl.DeviceIdType.MESH

BIG = ["ffn1_w1", "ffn1_w3", "ffn1_w2", "w_in", "w_lru_out", "w_attn_out", "w_gate", "w_o", "ffn2_w1", "ffn2_w3", "ffn2_w2"]
SMALL = [("ffn1_pre_g", 1024), ("ffn1_post_g", 1024), ("mix_pre_g", 1024), ("conv_w", 4096), ("conv_b", 1024),
         ("rg_a_w", 65536), ("rg_a_b", 1024), ("rg_x_w", 65536), ("rg_x_b", 1024), ("lru_lambda", 1024),
         ("attn_sinks", 1024), ("rel_bias", 1024), ("b_gate", 2048), ("mix_post_g", 1024), ("ffn2_pre_g", 1024),
         ("ffn2_post_g", 1024)]
WEIGHTS = ["ffn1_pre_g", "ffn1_w1", "ffn1_w3", "ffn1_w2", "ffn1_post_g", "mix_pre_g", "w_in", "conv_w", "conv_b", "rg_a_w",
           "rg_a_b", "rg_x_w", "rg_x_b", "lru_lambda", "w_lru_out", "attn_sinks", "rel_bias", "w_attn_out", "w_gate", "b_gate",
           "w_o", "mix_post_g", "ffn2_pre_g", "ffn2_w1", "ffn2_w3", "ffn2_w2", "ffn2_post_g"]


def _params(*sem):
    return pltpu.CompilerParams(dimension_semantics=sem or None, vmem_limit_bytes=VMEM_LIMIT)


def _nn(a, b):
    return jnp.dot(a, b, preferred_element_type=F32)


def _nt(a, b):
    return lax.dot_general(a, b, (((1,), (1,)), ((), ())), preferred_element_type=F32)


def _tn(a, b):
    return lax.dot_general(a, b, (((0,), (0,)), ((), ())), preferred_element_type=F32)


def _rms(x, g):
    rstd = lax.rsqrt(jnp.mean(x * x, axis=-1, keepdims=True) + RMS_EPS)
    return (x * rstd) * g


def _rms_bwd(dout, x, g):
    rstd = lax.rsqrt(jnp.mean(x * x, axis=-1, keepdims=True) + RMS_EPS)
    xhat = x * rstd
    dg = jnp.sum(dout * xhat, axis=0, keepdims=True)
    dxhat = dout * g
    dx = rstd * (dxhat - xhat * jnp.mean(dxhat * xhat, axis=-1, keepdims=True))
    return dx, dg


_GELU_K = math.sqrt(2.0 / math.pi)


def _gelu(x):
    return x * (0.5 * (1.0 + jnp.tanh(_GELU_K * (x + 0.044715 * (x * x * x)))))


def _gelu_grad(x):
    t = jnp.tanh(_GELU_K * (x + 0.044715 * (x * x * x)))
    return 0.5 * (1.0 + t) + x * (0.5 * (1.0 - t * t) * (_GELU_K * (1.0 + 3.0 * 0.044715 * (x * x))))


def _softplus_neg(lam):
    z = -lam
    u = jnp.exp(-jnp.abs(z))
    w = 1.0 + u
    log1p_u = jnp.where(w == 1.0, u, jnp.log(w) * (u / (w - 1.0)))
    return jnp.maximum(z, 0.0) + log1p_u


def _lru_coeffs(r, sp):
    log_a = (-LRU_C * r) * sp
    a = jnp.exp(log_a)
    t = jnp.tanh(log_a)
    s = jnp.sqrt(-2.0 * t / (1.0 - t))
    return a, s


def _row_spec(tm, width):
    return pl.BlockSpec((tm, width), lambda i: (i, 0))


def _vec_spec(width):
    return pl.BlockSpec((1, width), lambda i: (0, 0))


_WHOLE = pl.BlockSpec(memory_space=pltpu.VMEM)


def _tile(t):
    return min(TM, t)


def _ffn_fwd(x, gpre, w1g, w3g, w2g, gpost, name):
    t = x.shape[0]
    tm = _tile(t)

    def body(x_ref, gpre_ref, w1_ref, w3_ref, w2_ref, gpost_ref, h_ref, a_ref, b_ref, hm_ref, f_ref):
        xv = x_ref[...]
        nb = _rms(xv, gpre_ref[...]).astype(BF16)
        f = jnp.zeros((tm, D), F32)
        for s in range(NSH):
            a = _nn(nb, w1_ref[s])
            b = _nn(nb, w3_ref[s])
            hmb = ((a * jax.nn.sigmoid(a)) * b).astype(BF16)
            a_ref[s] = a.astype(BF16)
            b_ref[s] = b.astype(BF16)
            hm_ref[s] = hmb
            f = f + _nn(hmb, w2_ref[s])
        f_ref[...] = f
        h_ref[...] = xv + 0.5 * _rms(f, gpost_ref[...])

    sh = pl.BlockSpec((NSH, tm, FF_S), lambda i: (0, i, 0))
    act = jax.ShapeDtypeStruct((NSH, t, FF_S), BF16)
    return pl.pallas_call(
        body, grid=(t // tm,), name=name,
        in_specs=[_row_spec(tm, D), _vec_spec(D), _WHOLE, _WHOLE, _WHOLE, _vec_spec(D)],
        out_specs=[_row_spec(tm, D), sh, sh, sh, _row_spec(tm, D)],
        out_shape=[jax.ShapeDtypeStruct((t, D), F32), act, act, act, jax.ShapeDtypeStruct((t, D), F32)],
        compiler_params=_params("arbitrary"),
    )(x, gpre, w1g, w3g, w2g, gpost)


def _loss_dy(y, target):
    t = y.shape[0]
    tm = _tile(t)

    def body(y_ref, t_ref, dy_ref, l_ref):
        @pl.when(pl.program_id(0) == 0)
        def _():
            l_ref[...] = jnp.zeros_like(l_ref)

        e = y_ref[...] - t_ref[...]
        dy_ref[...] = e * (1.0 / D)
        sq = jnp.sum(jnp.sum(e * e, axis=0, keepdims=True), axis=1, keepdims=True)
        l_ref[...] = l_ref[...] + sq

    return pl.pallas_call(
        body, grid=(t // tm,), name="loss_dy",
        in_specs=[_row_spec(tm, D), _row_spec(tm, D)],
        out_specs=[_row_spec(tm, D), pl.BlockSpec((1, 128), lambda i: (0, 0))],
        out_shape=[jax.ShapeDtypeStruct((t, D), F32), jax.ShapeDtypeStruct((1, 128), F32)],
        compiler_params=_params("arbitrary"),
    )(y, target)


def _mix_proj(h1, gmix, w_in_g, w_gate_g, b_gate):
    t = h1.shape[0]
    tm = _tile(t)

    def body(h_ref, g_ref, win_ref, wg_ref, bg_ref, u_ref, q_ref, k_ref, v_ref, xr_ref, xg_ref, gate_ref):
        ub = _rms(h_ref[...], g_ref[...]).astype(BF16)
        u_ref[...] = ub
        p0 = _nn(ub, win_ref[0])
        q_ref[:, 0:896] = p0.astype(BF16)
        p1 = _nn(ub, win_ref[1])
        q_ref[:, 896:1024] = p1[:, 0:128].astype(BF16)
        k_ref[...] = p1[:, 128:384].astype(BF16)
        v_ref[...] = p1[:, 384:640].astype(BF16)
        xr_ref[:, 0:256] = p1[:, 640:896]
        p2 = _nn(ub, win_ref[2])
        xr_ref[:, 256:1024] = p2[:, 0:768]
        xg_ref[:, 0:128] = p2[:, 768:896]
        xg_ref[:, 128:1024] = _nn(ub, win_ref[3])
        for s in range(NSH):
            sl = slice(s * GATE_S, (s + 1) * GATE_S)
            gate_ref[:, sl] = jax.nn.sigmoid(_nn(ub, wg_ref[s]) + bg_ref[:, sl])

    return pl.pallas_call(
        body, grid=(t // tm,), name="mix_proj",
        in_specs=[_row_spec(tm, D), _vec_spec(D), _WHOLE, _WHOLE, _vec_spec(2 * D)],
        out_specs=[_row_spec(tm, D), _row_spec(tm, D), _row_spec(tm, KV_W), _row_spec(tm, KV_W), _row_spec(tm, D),
                   _row_spec(tm, D), _row_spec(tm, 2 * D)],
        out_shape=[jax.ShapeDtypeStruct((t, D), BF16), jax.ShapeDtypeStruct((t, D), BF16),
                   jax.ShapeDtypeStruct((t, KV_W), BF16), jax.ShapeDtypeStruct((t, KV_W), BF16),
                   jax.ShapeDtypeStruct((t, D), F32), jax.ShapeDtypeStruct((t, D), F32),
                   jax.ShapeDtypeStruct((t, 2 * D), F32)],
        compiler_params=_params("arbitrary"),
    )(h1, gmix, w_in_g, w_gate_g, b_gate)


def _rglru_fwd(xr, xg, conv_w, conv_b, wa2, ba, wx2, bx, lam):
    t = xr.shape[0]
    tm = _tile(t)
    nb8 = tm // 8

    def body(xr_ref, xrp_ref, xg_ref, cw_ref, cb_ref, wa_ref, ba_ref, wx_ref, bx_ref, lam_ref,
             hr_ref, yain_ref, xc_ref, r_ref, ig_ref, ext, a_sc, h_sc):
        i = pl.program_id(0)

        @pl.when(i == 0)
        def _():
            h_sc[...] = jnp.zeros_like(h_sc)

        ext[0:8, :] = jnp.where(i == 0, 0.0, xrp_ref[...])
        ext[8:8 + tm, :] = xr_ref[...]
        xc = jnp.broadcast_to(cb_ref[...], (tm, D))
        for tap in range(4):
            xc = xc + ext[pl.ds(5 + tap, tm), :] * cw_ref[tap:tap + 1, :]
        xc_ref[...] = xc
        xcb = xc.astype(BF16)
        for p in range(8):
            sl = slice(p * 128, (p + 1) * 128)
            r_ref[:, sl] = jax.nn.sigmoid(_nn(xcb[:, sl], wa_ref[p]) + ba_ref[:, sl])
            ig_ref[:, sl] = jax.nn.sigmoid(_nn(xcb[:, sl], wx_ref[p]) + bx_ref[:, sl])
        a, s = _lru_coeffs(r_ref[...], _softplus_neg(lam_ref[...]))
        a_sc[...] = a
        hr_ref[...] = s * (ig_ref[...] * xc)

        def blk(j, h):
            st = pl.multiple_of(j * 8, 8)
            a8 = a_sc[pl.ds(st, 8), :]
            u8 = hr_ref[pl.ds(st, 8), :]
            rows = []
            for k in range(8):
                h = a8[k:k + 1, :] * h + u8[k:k + 1, :]
                rows.append(h)
            hr_ref[pl.ds(st, 8), :] = jnp.concatenate(rows, axis=0)
            return h

        h_sc[0:1, :] = lax.fori_loop(0, nb8, blk, h_sc[0:1, :])
        yain_ref[...] = (hr_ref[...] * _gelu(xg_ref[...])).astype(BF16)

    prev = pl.BlockSpec((8, D), lambda i: (jnp.maximum(i * nb8 - 1, 0), 0))
    full = lambda shape: pl.BlockSpec(shape, lambda i: tuple(0 for _ in shape))
    f32 = jax.ShapeDtypeStruct((t, D), F32)
    return pl.pallas_call(
        body, grid=(t // tm,), name="rglru_fwd",
        in_specs=[_row_spec(tm, D), prev, _row_spec(tm, D), full((4, D)), _vec_spec(D), full((8, 128, 128)), _vec_spec(D),
                  full((8, 128, 128)), _vec_spec(D), _vec_spec(D)],
        out_specs=[_row_spec(tm, D)] * 5,
        out_shape=[f32, jax.ShapeDtypeStruct((t, D), BF16), f32, f32, f32],
        scratch_shapes=[pltpu.VMEM((tm + 8, D), F32), pltpu.VMEM((tm, D), F32), pltpu.VMEM((8, D), F32)],
        compiler_params=_params("arbitrary"),
    )(xr, xr, xg, conv_w, conv_b, wa2, ba, wx2, bx, lam)


def _bias_fwd(table_t, onehot_t):
    def body(t_ref, e_ref, o_ref):
        o_ref[...] = jnp.dot(t_ref[...], e_ref[...], preferred_element_type=F32, precision=lax.Precision.HIGHEST)

    return pl.pallas_call(body, out_shape=jax.ShapeDtypeStruct((N_HEADS, CHUNK * KB), F32), name="bias_fwd",
                          compiler_params=_params())(table_t, onehot_t)


def _bias_bwd(dbias_flat, onehot_t, ds_rows):
    def body(d_ref, e_ref, s_ref, o_ref, so_ref):
        o_ref[...] = lax.dot_general(d_ref[...], e_ref[...], (((1,), (1,)), ((), ())), preferred_element_type=F32,
                                     precision=lax.Precision.HIGHEST)
        so_ref[...] = jnp.zeros_like(so_ref)
        for r in range(4):
            so_ref[:, r:r + 1] = jnp.sum(s_ref[:, r * CHUNK:(r + 1) * CHUNK], axis=1, keepdims=True)

    return pl.pallas_call(body, out_shape=[jax.ShapeDtypeStruct((N_HEADS, N_BUCKETS), F32), jax.ShapeDtypeStruct((8, 128), F32)],
                          name="bias_bwd", compiler_params=_params())(dbias_flat, onehot_t, ds_rows)


def _stack_heads(q):
    return jnp.concatenate(
        [jnp.concatenate([q[:, (4 * g + r) * HEAD_DIM:(4 * g + r + 1) * HEAD_DIM] for g in range(4)], axis=1)
         for r in range(4)], axis=0)


def _unstack_heads(o):
    return jnp.concatenate([o[r * CHUNK:(r + 1) * CHUNK, g * HEAD_DIM:(g + 1) * HEAD_DIM] for g in range(4) for r in range(4)],
                           axis=1)


def _block_diag(w, mask):
    return jnp.concatenate([w] * 4, axis=0) * mask


def _attn_softmax(q_all, kbd, bias_t, sink_rows, c):
    s = _nt(kbd, q_all) * (HEAD_DIM ** -0.5) + bias_t
    j = lax.broadcasted_iota(jnp.int32, (4 * KP, 1), 0) % KP
    s = jnp.where((j < KB) & (j + c * CHUNK >= PAD_KEYS), s, NEG_INF)
    ps, sinks = [], []
    for g in range(4):
        sg = s[g * KP:(g + 1) * KP, :]
        sink = sink_rows[g:g + 1, :]
        m = jnp.maximum(jnp.max(sg, axis=0, keepdims=True), sink)
        e = jnp.exp(sg - m)
        es = jnp.exp(sink - m)
        inv = 1.0 / (jnp.sum(e, axis=0, keepdims=True) + es)
        ps.append(e * inv)
        sinks.append(es * inv)
    return ps, sinks


def _attn_fwd(sink_rows, q, kp, vp, bias_t, mask):
    t = q.shape[0]

    def body(sink_ref, q_ref, kp_ref, vp_ref, bias_ref, mask_ref, o_ref):
        c = pl.program_id(0)
        st = pl.multiple_of(c * CHUNK, CHUNK)
        kbd = _block_diag(kp_ref[pl.ds(st, KP), :], mask_ref[...])
        vbd = _block_diag(vp_ref[pl.ds(st, KP), :], mask_ref[...])
        ps, _ = _attn_softmax(_stack_heads(q_ref[...]), kbd, bias_ref[...], sink_ref[...], c)
        p_t = jnp.concatenate(ps, axis=0).astype(BF16)
        o_ref[...] = _unstack_heads(_tn(p_t, vbd)).astype(BF16)

    return pl.pallas_call(
        body, grid=(t // CHUNK,), name="attn_fwd",
        in_specs=[_WHOLE, _row_spec(CHUNK, D), _WHOLE, _WHOLE, _WHOLE, _WHOLE],
        out_specs=_row_spec(CHUNK, D),
        out_shape=jax.ShapeDtypeStruct((t, D), BF16),
        compiler_params=_params("arbitrary"),
    )(sink_rows, q, kp, vp, bias_t, mask)


def _merge_fwd(yain, o, gate, h1, w_lru, w_att, w_o, gpost):
    t = h1.shape[0]
    tm = _tile(t)

    def body(ya_ref, o_ref, g_ref, h_ref, wl_ref, wa_ref, wo_ref, gp_ref, h2_ref, mo_ref, mg_ref, ya_out, yb_out):
        ya = _nn(ya_ref[...], wl_ref[...])
        yb = _nn(o_ref[...], wa_ref[...])
        mg = (g_ref[:, 0:D] * ya + g_ref[:, D:2 * D] * yb).astype(BF16)
        mo = _nn(mg, wo_ref[...])
        ya_out[...] = ya.astype(BF16)
        yb_out[...] = yb.astype(BF16)
        mg_ref[...] = mg
        mo_ref[...] = mo
        h2_ref[...] = h_ref[...] + _rms(mo, gp_ref[...])

    f32 = jax.ShapeDtypeStruct((t, D), F32)
    b16 = jax.ShapeDtypeStruct((t, D), BF16)
    return pl.pallas_call(
        body, grid=(t // tm,), name="merge_fwd",
        in_specs=[_row_spec(tm, D), _row_spec(tm, D), _row_spec(tm, 2 * D), _row_spec(tm, D), _WHOLE, _WHOLE, _WHOLE,
                  _vec_spec(D)],
        out_specs=[_row_spec(tm, D)] * 5,
        out_shape=[f32, f32, b16, b16, b16],
        compiler_params=_params("arbitrary"),
    )(yain, o, gate, h1, w_lru, w_att, w_o, gpost)


def _ffn_bwd(dh, x, f, a, b, gpre, gpost, w1g, w3g, w2g, name):
    t = x.shape[0]
    tm = _tile(t)

    def body(dh_ref, x_ref, f_ref, a_ref, b_ref, gpre_ref, gpost_ref, w1_ref, w3_ref, w2_ref,
             dx_ref, n_ref, da_ref, db_ref, df_ref, dgpre_ref, dgpost_ref):
        @pl.when(pl.program_id(0) == 0)
        def _():
            dgpre_ref[...] = jnp.zeros_like(dgpre_ref)
            dgpost_ref[...] = jnp.zeros_like(dgpost_ref)

        dhv = dh_ref[...]
        xv = x_ref[...]
        df, dgp = _rms_bwd(0.5 * dhv, f_ref[...], gpost_ref[...])
        dgpost_ref[...] += dgp
        dfb = df.astype(BF16)
        df_ref[...] = dfb
        n_ref[...] = _rms(xv, gpre_ref[...]).astype(BF16)
        dn = jnp.zeros((tm, D), F32)
        for s in range(NSH):
            av = a_ref[s].astype(F32)
            bv = b_ref[s].astype(F32)
            sg = jax.nn.sigmoid(av)
            dhm = _nt(dfb, w2_ref[s])
            dab = (dhm * bv * (sg * (1.0 + av * (1.0 - sg)))).astype(BF16)
            dbb = (dhm * (av * sg)).astype(BF16)
            da_ref[s] = dab
            db_ref[s] = dbb
            dn = dn + _nt(dab, w1_ref[s]) + _nt(dbb, w3_ref[s])
        dxn, dg = _rms_bwd(dn, xv, gpre_ref[...])
        dgpre_ref[...] += dg
        dx_ref[...] = dhv + dxn

    sh = pl.BlockSpec((NSH, tm, FF_S), lambda i: (0, i, 0))
    act = jax.ShapeDtypeStruct((NSH, t, FF_S), BF16)
    vec = jax.ShapeDtypeStruct((1, D), F32)
    return pl.pallas_call(
        body, grid=(t // tm,), name=name,
        in_specs=[_row_spec(tm, D), _row_spec(tm, D), _row_spec(tm, D), sh, sh, _vec_spec(D), _vec_spec(D), _WHOLE, _WHOLE,
                  _WHOLE],
        out_specs=[_row_spec(tm, D), _row_spec(tm, D), sh, sh, _row_spec(tm, D), _vec_spec(D), _vec_spec(D)],
        out_shape=[jax.ShapeDtypeStruct((t, D), F32), jax.ShapeDtypeStruct((t, D), BF16), act, act,
                   jax.ShapeDtypeStruct((t, D), BF16), vec, vec],
        compiler_params=_params("arbitrary"),
    )(dh, x, f, a, b, gpre, gpost, w1g, w3g, w2g)


def _wgrad(a, b, a_spec, b_spec, out_spec, out_shape, grid, name):
    def body(a_ref, b_ref, o_ref):
        o_ref[...] = _tn(a_ref[...], b_ref[...])

    return pl.pallas_call(body, grid=grid, name=name, in_specs=[a_spec, b_spec], out_specs=out_spec,
                          out_shape=jax.ShapeDtypeStruct(out_shape, F32),
                          compiler_params=_params(*("arbitrary",) * len(grid)))(a, b)


def _wgrad_cols(act, dsh, width, name):
    t = act.shape[0]
    if dsh.ndim == 3:
        b_spec = pl.BlockSpec((None, t, width), lambda s, k: (s, 0, 0))
    else:
        b_spec = pl.BlockSpec((t, width), lambda s, k: (0, s))
    return _wgrad(act, dsh, pl.BlockSpec((t, 512), lambda s, k: (0, k)), b_spec,
                  pl.BlockSpec((None, 512, width), lambda s, k: (s, k, 0)), (NSH, D, width), (NSH, 2), name)


def _wgrad_rows(hm, df, name):
    t = df.shape[0]
    return _wgrad(hm, df, pl.BlockSpec((None, t, FF_S), lambda s, j: (s, 0, 0)), pl.BlockSpec((t, 512), lambda s, j: (0, j)),
                  pl.BlockSpec((None, FF_S, 512), lambda s, j: (s, 0, j)), (NSH, FF_S, D), (NSH, 2), name)


def _wgrad_sq(a, b, name):
    t = a.shape[0]
    return _wgrad(a, b, pl.BlockSpec((t, 512), lambda i, j: (0, i)), pl.BlockSpec((t, 512), lambda i, j: (0, j)),
                  pl.BlockSpec((512, 512), lambda i, j: (i, j)), (D, D), (2, 2), name)


def _mix_bwd1(dh2, mo, gpost, gate, ya, yb, xg, hr, w_o, w_lru, w_att):
    t = dh2.shape[0]
    tm = _tile(t)

    def body(dh_ref, mo_ref, gp_ref, g_ref, ya_ref, yb_ref, xg_ref, hr_ref, wo_ref, wl_ref, wa_ref,
             dmo_ref, dya_ref, dyb_ref, dgate_ref, dhr_ref, dxg_ref, do_ref, dgp_ref, dbg_ref):
        @pl.when(pl.program_id(0) == 0)
        def _():
            dgp_ref[...] = jnp.zeros_like(dgp_ref)
            dbg_ref[...] = jnp.zeros_like(dbg_ref)

        dmo, dgp = _rms_bwd(dh_ref[...], mo_ref[...], gp_ref[...])
        dgp_ref[...] += dgp
        dmob = dmo.astype(BF16)
        dmo_ref[...] = dmob
        dm = _nt(dmob, wo_ref[...])
        g0 = g_ref[:, 0:D]
        g1 = g_ref[:, D:2 * D]
        dyab = (dm * g0).astype(BF16)
        dybb = (dm * g1).astype(BF16)
        dya_ref[...] = dyab
        dyb_ref[...] = dybb
        dg0 = dm * ya_ref[...].astype(F32) * (g0 * (1.0 - g0))
        dg1 = dm * yb_ref[...].astype(F32) * (g1 * (1.0 - g1))
        dgate_ref[:, 0:D] = dg0.astype(BF16)
        dgate_ref[:, D:2 * D] = dg1.astype(BF16)
        dbg_ref[:, 0:D] += jnp.sum(dg0, axis=0, keepdims=True)
        dbg_ref[:, D:2 * D] += jnp.sum(dg1, axis=0, keepdims=True)
        dyain = _nt(dyab, wl_ref[...])
        do_ref[...] = _nt(dybb, wa_ref[...]).astype(BF16)
        xgv = xg_ref[...]
        dhr_ref[...] = dyain * _gelu(xgv)
        dxg_ref[...] = (dyain * hr_ref[...] * _gelu_grad(xgv)).astype(BF16)

    b16 = jax.ShapeDtypeStruct((t, D), BF16)
    return pl.pallas_call(
        body, grid=(t // tm,), name="mix_bwd1",
        in_specs=[_row_spec(tm, D), _row_spec(tm, D), _vec_spec(D), _row_spec(tm, 2 * D), _row_spec(tm, D), _row_spec(tm, D),
                  _row_spec(tm, D), _row_spec(tm, D), _WHOLE, _WHOLE, _WHOLE],
        out_specs=[_row_spec(tm, D), _row_spec(tm, D), _row_spec(tm, D), _row_spec(tm, 2 * D), _row_spec(tm, D),
                   _row_spec(tm, D), _row_spec(tm, D), _vec_spec(D), _vec_spec(2 * D)],
        out_shape=[b16, b16, b16, jax.ShapeDtypeStruct((t, 2 * D), BF16), jax.ShapeDtypeStruct((t, D), F32), b16, b16,
                   jax.ShapeDtypeStruct((1, D), F32), jax.ShapeDtypeStruct((1, 2 * D), F32)],
        compiler_params=_params("arbitrary"),
    )(dh2, mo, gpost, gate, ya, yb, xg, hr, w_o, w_lru, w_att)


def _rglru_bwd(dhr, hr, xc, r, ig, xr, conv_w, wa2, wx2, lam):
    t = dhr.shape[0]
    tm = _tile(t)
    nb8 = tm // 8
    nt = t // tm

    def body(dhr_ref, hr_ref, hrp_ref, xc_ref, r_ref, ig_ref, xr_ref, xrp_ref, cw_ref, wa_ref, wx_ref, lam_ref,
             dxr_ref, dwa_ref, dwx_ref, dba_ref, dbx_ref, dlam_ref, dcw_ref, dcb_ref,
             ext_h, ext_x, ext_d, a_sc, g_sc, c_sc, nxt_sc):
        i = pl.program_id(0)
        first_tile = i == nt - 1

        @pl.when(i == 0)
        def _():
            c_sc[...] = jnp.zeros_like(c_sc)
            nxt_sc[...] = jnp.zeros_like(nxt_sc)
            for ref in (dwa_ref, dwx_ref, dba_ref, dbx_ref, dlam_ref, dcw_ref, dcb_ref):
                ref[...] = jnp.zeros_like(ref)

        lamv = lam_ref[...]
        sp = _softplus_neg(lamv)
        rv = r_ref[...]
        igv = ig_ref[...]
        xcv = xc_ref[...]
        a, s = _lru_coeffs(rv, sp)
        a_sc[...] = a

        def blk(jj, c):
            st = pl.multiple_of((nb8 - 1 - jj) * 8, 8)
            d8 = dhr_ref[pl.ds(st, 8), :]
            a8 = a_sc[pl.ds(st, 8), :]
            rows = [None] * 8
            for k in range(7, -1, -1):
                g = d8[k:k + 1, :] + c
                c = a8[k:k + 1, :] * g
                rows[k] = g
            g_sc[pl.ds(st, 8), :] = jnp.concatenate(rows, axis=0)
            return c

        c_sc[0:1, :] = lax.fori_loop(0, nb8, blk, c_sc[0:1, :])
        g = g_sc[...]
        ext_h[0:8, :] = jnp.where(first_tile, 0.0, hrp_ref[...])
        ext_h[8:8 + tm, :] = hr_ref[...]
        hprev = ext_h[pl.ds(7, tm), :]
        d_s = g * (igv * xcv)
        dig = g * s * xcv
        dxc = g * s * igv
        dla = (g * hprev) * a - d_s * ((a * a) / s)
        dr_pre = (dla * (-LRU_C * sp)) * (rv * (1.0 - rv))
        di_pre = dig * (igv * (1.0 - igv))
        dlam_ref[...] += jnp.sum(dla * (LRU_C * rv), axis=0, keepdims=True) * jax.nn.sigmoid(-lamv)
        dba_ref[...] += jnp.sum(dr_pre, axis=0, keepdims=True)
        dbx_ref[...] += jnp.sum(di_pre, axis=0, keepdims=True)
        drb = dr_pre.astype(BF16)
        dib = di_pre.astype(BF16)
        xcb = xcv.astype(BF16)
        ext_d[tm:tm + 8, :] = nxt_sc[...]
        for p in range(8):
            sl = slice(p * 128, (p + 1) * 128)
            ext_d[0:tm, sl] = dxc[:, sl] + _nt(drb[:, sl], wa_ref[p]) + _nt(dib[:, sl], wx_ref[p])
            dwa_ref[p] += _tn(xcb[:, sl], drb[:, sl])
            dwx_ref[p] += _tn(xcb[:, sl], dib[:, sl])
        dxcv = ext_d[0:tm, :]
        nxt_sc[...] = ext_d[0:8, :]
        dcb_ref[...] += jnp.sum(dxcv, axis=0, keepdims=True)
        ext_x[0:8, :] = jnp.where(first_tile, 0.0, xrp_ref[...])
        ext_x[8:8 + tm, :] = xr_ref[...]
        dxr = jnp.zeros((tm, D), F32)
        for tap in range(4):
            dxr = dxr + ext_d[pl.ds(3 - tap, tm), :] * cw_ref[tap:tap + 1, :]
            dcw_ref[tap:tap + 1, :] += jnp.sum(dxcv * ext_x[pl.ds(5 + tap, tm), :], axis=0, keepdims=True)
        dxr_ref[...] = dxr.astype(BF16)

    rev = pl.BlockSpec((tm, D), lambda i: (nt - 1 - i, 0))
    prev = pl.BlockSpec((8, D), lambda i: (jnp.maximum((nt - 1 - i) * nb8 - 1, 0), 0))
    full = lambda shape: pl.BlockSpec(shape, lambda i: tuple(0 for _ in shape))
    vec = jax.ShapeDtypeStruct((1, D), F32)
    blocks = jax.ShapeDtypeStruct((8, 128, 128), F32)
    return pl.pallas_call(
        body, grid=(nt,), name="rglru_bwd",
        in_specs=[rev, rev, prev, rev, rev, rev, rev, prev, full((4, D)), full((8, 128, 128)), full((8, 128, 128)),
                  _vec_spec(D)],
        out_specs=[rev, full((8, 128, 128)), full((8, 128, 128)), _vec_spec(D), _vec_spec(D), _vec_spec(D), full((4, D)),
                   _vec_spec(D)],
        out_shape=[jax.ShapeDtypeStruct((t, D), BF16), blocks, blocks, vec, vec, vec, jax.ShapeDtypeStruct((4, D), F32), vec],
        scratch_shapes=[pltpu.VMEM((tm + 8, D), F32), pltpu.VMEM((tm + 8, D), F32), pltpu.VMEM((tm + 8, D), F32),
                        pltpu.VMEM((tm, D), F32), pltpu.VMEM((tm, D), F32), pltpu.VMEM((8, D), F32), pltpu.VMEM((8, D), F32)],
        compiler_params=_params("arbitrary"),
    )(dhr, hr, hr, xc, r, ig, xr, xr, conv_w, wa2, wx2, lam)


def _attn_bwd(sink_rows, q, kp, vp, bias_t, mask, do):
    t = q.shape[0]
    tp = kp.shape[0]

    def body(sink_ref, q_ref, kp_ref, vp_ref, bias_ref, mask_ref, do_ref, dq_ref, dk_ref, dv_ref, dbias_ref, ds_ref):
        c = pl.program_id(0)

        @pl.when(c == 0)
        def _():
            for ref in (dk_ref, dv_ref, dbias_ref, ds_ref):
                ref[...] = jnp.zeros_like(ref)

        st = pl.multiple_of(c * CHUNK, CHUNK)
        maskv = mask_ref[...]
        kbd = _block_diag(kp_ref[pl.ds(st, KP), :], maskv)
        vbd = _block_diag(vp_ref[pl.ds(st, KP), :], maskv)
        q_all = _stack_heads(q_ref[...])
        do_all = _stack_heads(do_ref[...])
        ps, sinks = _attn_softmax(q_all, kbd, bias_ref[...], sink_ref[...], c)
        dp = _nt(vbd, do_all)
        dscs = []
        for g in range(4):
            dpg = dp[g * KP:(g + 1) * KP, :]
            delta = jnp.sum(ps[g] * dpg, axis=0, keepdims=True)
            dscs.append(ps[g] * (dpg - delta))
            ds_ref[g:g + 1, :] += -(sinks[g] * delta)
        dsc = jnp.concatenate(dscs, axis=0)
        dbias_ref[...] += dsc
        dsb = (dsc * (HEAD_DIM ** -0.5)).astype(BF16)
        dq_ref[...] = _unstack_heads(_tn(dsb, kbd)).astype(BF16)

        lane_group = lax.broadcasted_iota(jnp.int32, (1, 4 * HEAD_DIM), 1) // HEAD_DIM

        def own_blocks(full):
            out = full[0:KP]
            for g in range(1, 4):
                out = jnp.where(lane_group == g, full[g * KP:(g + 1) * KP], out)
            return out

        dk_ref[pl.ds(st, KP), :] += own_blocks(_nn(dsb, q_all))
        dv_ref[pl.ds(st, KP), :] += own_blocks(_nn(jnp.concatenate(ps, axis=0).astype(BF16), do_all))

    full = lambda shape: pl.BlockSpec(shape, lambda i: tuple(0 for _ in shape))
    return pl.pallas_call(
        body, grid=(t // CHUNK,), name="attn_bwd",
        in_specs=[_WHOLE, _row_spec(CHUNK, D), _WHOLE, _WHOLE, _WHOLE, _WHOLE, _row_spec(CHUNK, D)],
        out_specs=[_row_spec(CHUNK, D), full((tp, KV_W)), full((tp, KV_W)), full((4 * KP, 4 * CHUNK)), full((8, 4 * CHUNK))],
        out_shape=[jax.ShapeDtypeStruct((t, D), BF16), jax.ShapeDtypeStruct((tp, KV_W), F32),
                   jax.ShapeDtypeStruct((tp, KV_W), F32), jax.ShapeDtypeStruct((4 * KP, 4 * CHUNK), F32),
                   jax.ShapeDtypeStruct((8, 4 * CHUNK), F32)],
        compiler_params=_params("arbitrary"),
    )(sink_rows, q, kp, vp, bias_t, mask, do)


def _mix_bwd2(dproj, dgate, h1, dh2, gmix, w_in_g, w_gate_g):
    t = h1.shape[0]
    tm = _tile(t)

    def body(dp_ref, dg_ref, h_ref, dh_ref, g_ref, win_ref, wg_ref, dh1_ref, dgm_ref):
        @pl.when(pl.program_id(0) == 0)
        def _():
            dgm_ref[...] = jnp.zeros_like(dgm_ref)

        du = jnp.zeros((tm, D), F32)
        for s in range(NSH):
            du = du + _nt(dp_ref[:, s * IN_S:(s + 1) * IN_S], win_ref[s])
            du = du + _nt(dg_ref[:, s * GATE_S:(s + 1) * GATE_S], wg_ref[s])
        dxn, dg = _rms_bwd(du, h_ref[...], g_ref[...])
        dgm_ref[...] += dg
        dh1_ref[...] = dh_ref[...] + dxn

    return pl.pallas_call(
        body, grid=(t // tm,), name="mix_bwd2",
        in_specs=[_row_spec(tm, NSH * IN_S), _row_spec(tm, 2 * D), _row_spec(tm, D), _row_spec(tm, D), _vec_spec(D), _WHOLE,
                  _WHOLE],
        out_specs=[_row_spec(tm, D), _vec_spec(D)],
        out_shape=[jax.ShapeDtypeStruct((t, D), F32), jax.ShapeDtypeStruct((1, D), F32)],
        compiler_params=_params("arbitrary"),
    )(dproj, dgate, h1, dh2, gmix, w_in_g, w_gate_g)


def _band_onehot():
    nb = N_BUCKETS // 2
    max_exact = nb // 2
    rel = jnp.arange(KB)[None, :] - PAD_KEYS - jnp.arange(CHUNK)[:, None]
    ret = jnp.where(rel > 0, nb, 0)
    n = jnp.abs(rel)
    nf = jnp.maximum(n, 1).astype(jnp.float32)
    large = max_exact + (jnp.log(nf / max_exact) / math.log(128 / max_exact) * (nb - max_exact)).astype(jnp.int32)
    large = jnp.minimum(large, nb - 1)
    buckets = (ret + jnp.where(n < max_exact, n, large)).reshape(1, CHUNK * KB)
    return (buckets == jnp.arange(N_BUCKETS)[:, None]).astype(F32)


def _pair_blocks(w):
    z = jnp.zeros((8, 128, 128), w.dtype)
    return z.at[:, 0:64, 0:64].set(w[0::2]).at[:, 64:128, 64:128].set(w[1::2])


def _unpair_blocks(w2):
    return jnp.stack([w2[:, 0:64, 0:64], w2[:, 64:128, 64:128]], axis=1).reshape(16, 64, 64)


def _local_step(x, target, weights, sm, reducer):
    row = lambda v: v.reshape(1, -1)
    wg = dict(weights("ffn1", x))
    sm = dict(sm, conv_w=wg["conv_w"])
    onehot_t = _band_onehot()
    bias = _bias_fwd(sm["rel_bias"].T, onehot_t).reshape(4, 4, CHUNK, KB)
    bias_t = jnp.pad(jnp.transpose(bias, (0, 3, 1, 2)), ((0, 0), (0, KP - KB), (0, 0), (0, 0))).reshape(4 * KP, 4 * CHUNK)
    sink_rows = jnp.pad(jnp.repeat(sm["attn_sinks"].reshape(4, 4), CHUNK, axis=1), ((0, 4), (0, 0)))
    grp = jnp.arange(4 * KP)[:, None] // KP == jnp.arange(4 * HEAD_DIM)[None, :] // HEAD_DIM
    mask = (grp & (jnp.arange(4 * KP)[:, None] % KP < KB)).astype(BF16)
    wa2 = _pair_blocks(sm["rg_a_w"]).astype(BF16)
    wx2 = _pair_blocks(sm["rg_x_w"]).astype(BF16)

    h1, a1, b1, hm1, f1 = _ffn_fwd(x, row(sm["ffn1_pre_g"]), wg["ffn1_w1"], wg["ffn1_w3"], wg["ffn1_w2"],
                                   row(sm["ffn1_post_g"]), "ffn1_fwd")
    wg.update(weights("mix", h1))
    w_lru = wg["w_lru_out"].reshape(D, D)
    w_att = wg["w_attn_out"].reshape(D, D)
    w_o = wg["w_o"].reshape(D, D)
    u, q, k, v, xr, xg, gate = _mix_proj(h1, row(sm["mix_pre_g"]), wg["w_in"], wg["w_gate"], row(sm["b_gate"]))
    hr, yain, xc, r, ig = _rglru_fwd(xr, xg, sm["conv_w"], row(sm["conv_b"]), wa2, row(sm["rg_a_b"]), wx2,
                                     row(sm["rg_x_b"]), row(sm["lru_lambda"]))
    kp = jnp.pad(k, ((PAD_KEYS, KP - KB), (0, 0)))
    vp = jnp.pad(v, ((PAD_KEYS, KP - KB), (0, 0)))
    o = _attn_fwd(sink_rows, q, kp, vp, bias_t, mask)
    wg.update(weights("ffn2", o))
    h2, mo, merged, ya, yb = _merge_fwd(yain, o, gate, h1, w_lru, w_att, w_o, row(sm["mix_post_g"]))
    y, a2, b2, hm2, f2 = _ffn_fwd(h2, row(sm["ffn2_pre_g"]), wg["ffn2_w1"], wg["ffn2_w3"], wg["ffn2_w2"],
                                  row(sm["ffn2_post_g"]), "ffn2_fwd")
    dy, sq = _loss_dy(y, target)

    big, small = {}, {}
    dh2, n2, da2, db2, df2, small["ffn2_pre_g"], small["ffn2_post_g"] = _ffn_bwd(
        dy, h2, f2, a2, b2, row(sm["ffn2_pre_g"]), row(sm["ffn2_post_g"]), wg["ffn2_w1"], wg["ffn2_w3"], wg["ffn2_w2"],
        "ffn2_bwd")
    big["ffn2_w1"] = _wgrad_cols(n2, da2, FF_S, "dw_ffn2_w1")
    big["ffn2_w3"] = _wgrad_cols(n2, db2, FF_S, "dw_ffn2_w3")
    big["ffn2_w2"] = _wgrad_rows(hm2, df2, "dw_ffn2_w2")
    reducer.begin("ffn2", {n: big[n] for n in ("ffn2_w1", "ffn2_w3", "ffn2_w2")})
    dmo, dya, dyb, dgate, dhr, dxg, do, small["mix_post_g"], small["b_gate"] = _mix_bwd1(
        dh2, mo, row(sm["mix_post_g"]), gate, ya, yb, xg, hr, w_o, w_lru, w_att)
    big["w_o"] = _wgrad_sq(merged, dmo, "dw_w_o").reshape(NSH, D // NSH, D)
    big["w_lru_out"] = _wgrad_sq(yain, dya, "dw_w_lru_out").reshape(NSH, D // NSH, D)
    big["w_attn_out"] = _wgrad_sq(o, dyb, "dw_w_attn_out").reshape(NSH, D // NSH, D)
    reducer.advance("ffn2", big["w_attn_out"])
    (dxr, dwa2, dwx2, small["rg_a_b"], small["rg_x_b"], small["lru_lambda"], small["conv_w"], small["conv_b"]) = _rglru_bwd(
        dhr, hr, xc, r, ig, xr, sm["conv_w"], wa2, wx2, row(sm["lru_lambda"]))
    small["rg_a_w"] = _unpair_blocks(dwa2)
    small["rg_x_w"] = _unpair_blocks(dwx2)
    dq, dkp, dvp, dbias_t, ds_rows = _attn_bwd(sink_rows, q, kp, vp, bias_t, mask, do)
    dbias = jnp.transpose(dbias_t.reshape(4, KP, 4, CHUNK)[:, :KB], (0, 2, 3, 1)).reshape(N_HEADS, CHUNK * KB)
    drel_t, dsinks = _bias_bwd(dbias, onehot_t, ds_rows)
    small["attn_sinks"] = dsinks[0:4, 0:4].reshape(N_HEADS)
    small["rel_bias"] = drel_t.T
    t = x.shape[0]
    dproj = jnp.concatenate([dq, dkp[PAD_KEYS:PAD_KEYS + t].astype(BF16), dvp[PAD_KEYS:PAD_KEYS + t].astype(BF16), dxr, dxg],
                            axis=1)
    big["w_in"] = _wgrad_cols(u, dproj, IN_S, "dw_w_in")
    big["w_gate"] = _wgrad_cols(u, dgate, GATE_S, "dw_w_gate")
    reducer.begin("mix", {n: big[n] for n in ("w_in", "w_gate", "w_lru_out", "w_attn_out", "w_o")})
    dh1, small["mix_pre_g"] = _mix_bwd2(dproj, dgate, h1, dh2, row(sm["mix_pre_g"]), wg["w_in"], wg["w_gate"])
    dx, n1, da1, db1, df1, small["ffn1_pre_g"], small["ffn1_post_g"] = _ffn_bwd(
        dh1, x, f1, a1, b1, row(sm["ffn1_pre_g"]), row(sm["ffn1_post_g"]), wg["ffn1_w1"], wg["ffn1_w3"], wg["ffn1_w2"],
        "ffn1_bwd")
    reducer.advance("mix", dx)
    big["ffn1_w1"] = _wgrad_cols(n1, da1, FF_S, "dw_ffn1_w1")
    big["ffn1_w3"] = _wgrad_cols(n1, db1, FF_S, "dw_ffn1_w3")
    big["ffn1_w2"] = _wgrad_rows(hm1, df1, "dw_ffn1_w2")
    reducer.begin("ffn1", {n: big[n] for n in ("ffn1_w1", "ffn1_w3", "ffn1_w2")})
    return sq, dx, big, small


_ANY = pl.BlockSpec(memory_space=pl.ANY)


def _place():
    return lax.axis_index("x"), lax.axis_index("y"), lax.axis_index("c")


def _other_chips(x, y):
    return [(1 - x, y), (x, 1 - y), (1 - x, 1 - y)]


_HBM = pl.BlockSpec(memory_space=pltpu.HBM)
_SEM = pl.BlockSpec(memory_space=pltpu.SEMAPHORE)
_EFFECT = pltpu.SideEffectType.DATAFLOW_SIDE_EFFECTING


def _cast_into_slot(w, chip, name):
    r, cc = w.shape
    rows = r // 4

    def body(chip_ref, w_ref, o_ref):
        o_ref[...] = w_ref[...].astype(BF16)

    return pl.pallas_call(
        body, name=name, out_shape=jax.ShapeDtypeStruct((NSH, r, cc), BF16),
        grid_spec=pltpu.PrefetchScalarGridSpec(
            num_scalar_prefetch=1, grid=(4,), in_specs=[pl.BlockSpec((rows, cc), lambda i, chip: (i, 0))],
            out_specs=pl.BlockSpec((None, rows, cc), lambda i, chip: (chip[0], i, 0))),
        compiler_params=_params("arbitrary"))(chip, w)


def _piece(ref, slot, c):
    if ref.dtype == F32:
        return ref.at[slot]
    rh = ref.shape[1] // 2
    return ref.at[slot, pl.ds(pl.multiple_of(c * rh, 16), rh), :]


def _gather_start(stages):
    flat = [b for stage in stages for b in stage]
    n, ns = len(flat), len(stages)

    def body(*refs):
        ins, sems, token = refs[:n], refs[n:n + 2 * ns], refs[-1]
        x, y, c = _place()
        me = 2 * x + y
        k = 0
        for s, stage in enumerate(stages):
            for i in range(len(stage)):
                for j, (px, py) in enumerate(_other_chips(x, y)):
                    piece = _piece(ins[k], me, c)
                    pltpu.make_async_remote_copy(src_ref=piece, dst_ref=piece, send_sem=sems[2 * s].at[3 * i + j],
                                                 recv_sem=sems[2 * s + 1].at[3 * i + j], device_id=(px, py, c),
                                                 device_id_type=MESH).start()
                k += 1
        token[...] = jnp.zeros_like(token)

    sem_shapes = [pltpu.SemaphoreType.DMA((3 * len(stage),)) for stage in stages for _ in range(2)]
    outs = pl.pallas_call(
        body, name="gather_start", in_specs=[_HBM] * n,
        out_specs=[_SEM] * (2 * ns) + [_HBM] * n + [pl.BlockSpec(memory_space=pltpu.VMEM)],
        out_shape=sem_shapes + [pltpu.HBM(b.shape, b.dtype) for b in flat] + [jax.ShapeDtypeStruct((8, 128), F32)],
        input_output_aliases={i: 2 * ns + i for i in range(n)},
        compiler_params=pltpu.CompilerParams(has_side_effects=_EFFECT),
    )(*[pltpu.with_memory_space_constraint(b, pltpu.HBM) for b in flat])
    sems, bufs, token = outs[:2 * ns], list(outs[2 * ns:2 * ns + n]), outs[-1]
    per_stage, k = [], 0
    for s, stage in enumerate(stages):
        per_stage.append((sems[2 * s], sems[2 * s + 1], bufs[k:k + len(stage)]))
        k += len(stage)
    return per_stage, token


def _gather_wait(send_sems, recv_sems, bufs, after, name):
    n = len(bufs)

    def body(*refs):
        ins, ssem, rsem = refs[:n], refs[n], refs[n + 1]
        x, y, c = _place()
        me = 2 * x + y
        for i in range(n):
            for j, (px, py) in enumerate(_other_chips(x, y)):
                cp = pltpu.make_async_remote_copy(src_ref=_piece(ins[i], me, c), dst_ref=_piece(ins[i], 2 * px + py, c),
                                                  send_sem=ssem.at[3 * i + j], recv_sem=rsem.at[3 * i + j],
                                                  device_id=(px, py, c), device_id_type=MESH)
                cp.wait_send()
                cp.wait_recv()

    return pl.pallas_call(
        body, name=name, in_specs=[_HBM] * n + [_SEM, _SEM, _ANY], out_specs=[_HBM] * n,
        out_shape=[pltpu.HBM(b.shape, b.dtype) for b in bufs], input_output_aliases={i: i for i in range(n)},
        compiler_params=pltpu.CompilerParams(has_side_effects=_EFFECT),
    )(*bufs, send_sems, recv_sems, after)


def _sibling_fill(bufs, name):
    n = len(bufs)

    def body(*refs):
        ins, outs = refs[:n], refs[n:2 * n]
        send_sems, recv_sems = refs[2 * n:]
        x, y, c = _place()
        copies = []
        for i in range(n):
            for j, (px, py) in enumerate(_other_chips(x, y)):
                copies.append(pltpu.make_async_remote_copy(
                    src_ref=_piece(ins[i], 2 * px + py, c), dst_ref=_piece(outs[i], 2 * px + py, c),
                    send_sem=send_sems.at[3 * i + j], recv_sem=recv_sems.at[3 * i + j], device_id=(x, y, 1 - c),
                    device_id_type=MESH))
                copies[-1].start()
        for cp in copies:
            cp.wait()

    return pl.pallas_call(
        body, name=name, in_specs=[_ANY] * n, out_specs=[_ANY] * n,
        out_shape=[jax.ShapeDtypeStruct(b.shape, b.dtype) for b in bufs], input_output_aliases={i: i for i in range(n)},
        scratch_shapes=[pltpu.SemaphoreType.DMA((3 * n,)), pltpu.SemaphoreType.DMA((3 * n,))],
        compiler_params=pltpu.CompilerParams(has_side_effects=True),
    )(*bufs)


def _swap_plan(srcs, lands):
    x, y, c = _place()
    plan = []
    for src, land in zip(srcs, lands):
        rh = src.shape[1] // 2
        plan.append((src.at[:, pl.ds(pl.multiple_of((1 - c) * rh, 8), rh), :], land, (x, y, 1 - c)))
    return plan


def _owners_plan(srcs, lands):
    x, y, c = _place()
    return [(src.at[2 * px + py], land.at[j], (px, py, c))
            for src, land in zip(srcs, lands) for j, (px, py) in enumerate(_other_chips(x, y))]


def _exchange_start(srcs, lands, plan, copies, name):
    n = len(srcs)

    def body(*refs):
        send_sems, recv_sems = refs[2 * n], refs[2 * n + 1]
        for k, (src, dst, dev) in enumerate(plan(refs[:n], refs[n:2 * n])):
            pltpu.make_async_remote_copy(src_ref=src, dst_ref=dst, send_sem=send_sems.at[k], recv_sem=recv_sems.at[k],
                                         device_id=dev, device_id_type=MESH).start()

    both = list(srcs) + list(lands)
    outs = pl.pallas_call(
        body, name=name, in_specs=[_HBM] * (2 * n), out_specs=[_SEM, _SEM] + [_HBM] * (2 * n),
        out_shape=[pltpu.SemaphoreType.DMA((copies,)), pltpu.SemaphoreType.DMA((copies,))]
        + [pltpu.HBM(b.shape, b.dtype) for b in both],
        input_output_aliases={i: 2 + i for i in range(2 * n)},
        compiler_params=pltpu.CompilerParams(has_side_effects=_EFFECT),
    )(*[pltpu.with_memory_space_constraint(b, pltpu.HBM) for b in both])
    return (outs[0], outs[1]), list(outs[2:2 + n]), list(outs[2 + n:])


def _exchange_wait(sems, srcs, lands, plan, after, name):
    n = len(srcs)

    def body(*refs):
        send_sems, recv_sems = refs[2 * n], refs[2 * n + 1]
        for k, (src, dst, dev) in enumerate(plan(refs[:n], refs[n:2 * n])):
            cp = pltpu.make_async_remote_copy(src_ref=src, dst_ref=dst, send_sem=send_sems.at[k], recv_sem=recv_sems.at[k],
                                              device_id=dev, device_id_type=MESH)
            cp.wait_send()
            cp.wait_recv()

    both = list(srcs) + list(lands)
    outs = pl.pallas_call(
        body, name=name, in_specs=[_HBM] * (2 * n) + [_SEM, _SEM, _ANY], out_specs=[_HBM] * (2 * n),
        out_shape=[pltpu.HBM(b.shape, b.dtype) for b in both], input_output_aliases={i: i for i in range(2 * n)},
        compiler_params=pltpu.CompilerParams(has_side_effects=_EFFECT),
    )(*both, sems[0], sems[1], after)
    return list(outs[:n]), list(outs[n:])


class _Reducer:
    def __init__(self):
        self.state = {}

    def begin(self, stage, grads):
        names = list(grads)
        full = [grads[n] for n in names]
        lands = [lax.empty((NSH, g.shape[1] // 2, g.shape[2]), F32) for g in full]
        self.state[stage] = (names,) + _exchange_start(full, lands, _swap_plan, len(full), "swap_start_" + stage)

    def advance(self, stage, after):
        names, sems, full, lands = self.state[stage]
        full, got = _exchange_wait(sems, full, lands, _swap_plan, after, "swap_wait_" + stage)
        sums = [_chip_sum(g, a, "chip_sum_" + n) for n, g, a in zip(names, full, got)]
        lands = [lax.empty((3,) + s[0].shape[1:], BF16) for s in sums]
        self.state[stage] = (names, [s[1] for s in sums]) + _exchange_start(
            [s[0] for s in sums], lands, _owners_plan, 3 * len(sums), "owners_start_" + stage)

    def finish(self, stage, after):
        names, own, sems, sent, lands = self.state[stage]
        _, got = _exchange_wait(sems, sent, lands, _owners_plan, after, "owners_wait_" + stage)
        return {n: _owner_sum(o, g, "owner_sum_" + n) for n, o, g in zip(names, own, got)}


def _chip_sum(g, got, name):
    _, r, cc = g.shape
    rh = r // 2

    def body(g_ref, got_ref, hb_ref, own_ref):
        x, y, c = _place()
        s = pl.program_id(0)
        h = g_ref[pl.ds(pl.multiple_of(c * rh, 8), rh), :] + got_ref[...]
        hb_ref[...] = h.astype(BF16)

        @pl.when(s == 2 * x + y)
        def _():
            own_ref[...] = h

    return pl.pallas_call(
        body, grid=(NSH,), name=name,
        in_specs=[pl.BlockSpec((None, r, cc), lambda s: (s, 0, 0)), pl.BlockSpec((None, rh, cc), lambda s: (s, 0, 0))],
        out_specs=[pl.BlockSpec((None, rh, cc), lambda s: (s, 0, 0)), pl.BlockSpec((rh, cc), lambda s: (0, 0))],
        out_shape=[jax.ShapeDtypeStruct((NSH, rh, cc), BF16), jax.ShapeDtypeStruct((rh, cc), F32)],
        compiler_params=_params("arbitrary"),
    )(g, got)


def _owner_sum(own, got, name):
    rh, cc = own.shape
    rows = rh // 2

    def body(own_ref, got_ref, o_ref):
        o_ref[...] = ((own_ref[...] + got_ref[0].astype(F32)) + got_ref[1].astype(F32)) + got_ref[2].astype(F32)

    return pl.pallas_call(
        body, grid=(2,), name=name,
        in_specs=[pl.BlockSpec((rows, cc), lambda i: (i, 0)), pl.BlockSpec((3, rows, cc), lambda i: (0, i, 0))],
        out_specs=pl.BlockSpec((rows, cc), lambda i: (i, 0)),
        out_shape=jax.ShapeDtypeStruct((rh, cc), F32), compiler_params=_params("arbitrary"),
    )(own, got)


def _send_halves(halves, name):
    n = len(halves)

    def body(*refs):
        ins, outs = refs[:n], refs[n:2 * n]
        send_sems, recv_sems = refs[2 * n:]
        x, y, c = _place()
        copies = [pltpu.make_async_remote_copy(src_ref=ins[w], dst_ref=outs[w], send_sem=send_sems.at[w], recv_sem=recv_sems.at[w],
                                               device_id=(x, y, 1 - c), device_id_type=MESH) for w in range(n)]
        for cp in copies:
            cp.start()
        for cp in copies:
            cp.wait()

    return pl.pallas_call(
        body, name=name, in_specs=[_ANY] * n, out_specs=[_ANY] * n,
        out_shape=[jax.ShapeDtypeStruct(h.shape, F32) for h in halves],
        scratch_shapes=[pltpu.SemaphoreType.DMA((n,)), pltpu.SemaphoreType.DMA((n,))],
        compiler_params=pltpu.CompilerParams(has_side_effects=True),
    )(*halves)


def _all_reduce_small(part):
    def body(p_ref, o_ref, rbuf, send1, recv1, send2, recv2):
        x, y, c = _place()
        me = 4 * x + 2 * y + c
        peers = []
        for k in range(1, 8):
            px, py, pc = x ^ ((k >> 2) & 1), y ^ ((k >> 1) & 1), c ^ (k & 1)
            peers.append((k, (px, py, pc), 4 * px + 2 * py + pc))

        def rows(d):
            return pl.ds(pl.multiple_of(d * SMALL_SLICE, 8), SMALL_SLICE)

        first = [pltpu.make_async_remote_copy(src_ref=p_ref.at[rows(idx), :], dst_ref=rbuf.at[me], send_sem=send1.at[k],
                                              recv_sem=recv1.at[k], device_id=dev, device_id_type=MESH)
                 for k, dev, idx in peers]
        for cp in first:
            cp.start()
        rbuf[me] = p_ref[rows(me), :]
        for k, dev, idx in peers:
            pltpu.make_async_remote_copy(src_ref=p_ref.at[rows(idx), :], dst_ref=rbuf.at[idx], send_sem=send1.at[k],
                                         recv_sem=recv1.at[k], device_id=dev, device_id_type=MESH).wait_recv()
        acc = rbuf[0]
        for d in range(1, 8):
            acc = acc + rbuf[d]
        o_ref[rows(me), :] = acc
        second = [pltpu.make_async_remote_copy(src_ref=o_ref.at[rows(me), :], dst_ref=o_ref.at[rows(me), :],
                                               send_sem=send2.at[k], recv_sem=recv2.at[k], device_id=dev, device_id_type=MESH)
                  for k, dev, idx in peers]
        for cp in second:
            cp.start()
        for k, dev, idx in peers:
            pltpu.make_async_remote_copy(src_ref=o_ref.at[rows(me), :], dst_ref=o_ref.at[rows(idx), :], send_sem=send2.at[k],
                                         recv_sem=recv2.at[k], device_id=dev, device_id_type=MESH).wait_recv()
        for cp in first + second:
            cp.wait_send()

    return pl.pallas_call(
        body, name="all_reduce_small", in_specs=[_WHOLE], out_specs=_WHOLE,
        out_shape=jax.ShapeDtypeStruct((SMALL_ROWS, 128), F32),
        scratch_shapes=[pltpu.VMEM((8, SMALL_SLICE, 128), F32)] + [pltpu.SemaphoreType.DMA((8,))] * 4,
        compiler_params=pltpu.CompilerParams(has_side_effects=True),
    )(part)


def _adamw_update(w, gv, m, v):
    nm = ADAM_B1 * m + (1.0 - ADAM_B1) * gv
    nv = ADAM_B2 * v + (1.0 - ADAM_B2) * (gv * gv)
    m_hat = nm / (1.0 - ADAM_B1 ** ADAM_STEP)
    v_hat = nv / (1.0 - ADAM_B2 ** ADAM_STEP)
    return -ADAM_LR * (m_hat / (jnp.sqrt(v_hat) + ADAM_EPS) + ADAM_WD * w), nm, nv


def _adamw(w, g, m, v, name):
    rows = w.shape[0] // 4

    def body(w_ref, g_ref, m_ref, v_ref, d_ref, nm_ref, nv_ref):
        d_ref[...], nm_ref[...], nv_ref[...] = _adamw_update(w_ref[...], g_ref[...], m_ref[...], v_ref[...])

    spec = pl.BlockSpec((rows, w.shape[1]), lambda i: (i, 0))
    out = jax.ShapeDtypeStruct(w.shape, F32)
    return pl.pallas_call(body, grid=(4,), in_specs=[spec] * 4, out_specs=[spec] * 3, out_shape=[out] * 3, name=name,
                          compiler_params=_params("arbitrary"))(w, g, m, v)


def _adamw_halves(w, mine, theirs, m, v, name):
    rh, cc = mine.shape
    rows = rh // 2

    def body(w_ref, mine_ref, theirs_ref, m_ref, v_ref, g_ref, d_ref, nm_ref, nv_ref):
        gv = jnp.where(pl.program_id(0) == lax.axis_index("c"), mine_ref[...], theirs_ref[...])
        g_ref[...] = gv
        d_ref[...], nm_ref[...], nv_ref[...] = _adamw_update(w_ref[...], gv, m_ref[...], v_ref[...])

    spec = pl.BlockSpec((rows, cc), lambda h, i: (2 * h + i, 0))
    half = pl.BlockSpec((rows, cc), lambda h, i: (i, 0))
    out = jax.ShapeDtypeStruct(w.shape, F32)
    return pl.pallas_call(body, grid=(2, 2), in_specs=[spec, half, half, spec, spec], out_specs=[spec] * 4,
                          out_shape=[out] * 4, name=name, compiler_params=_params("arbitrary", "arbitrary"))(w, mine, theirs, m, v)


def _pack_small(vals):
    parts = []
    for name, size in SMALL:
        flat = vals[name].reshape(-1).astype(F32)
        parts.append(jnp.pad(flat, (0, size - flat.shape[0])))
    flat = jnp.concatenate(parts)
    return jnp.pad(flat, (0, SMALL_ROWS * 128 - flat.shape[0])).reshape(SMALL_ROWS, 128)


def _unpack_small(packed, shapes):
    flat = packed.reshape(-1)
    out, off = {}, 0
    for name, size in SMALL:
        n = math.prod(shapes[name])
        out[name] = flat[off:off + n].reshape(shapes[name])
        off += size
    return out


def kernel(x, ffn1_pre_g, ffn1_w1, ffn1_w3, ffn1_w2, ffn1_post_g, mix_pre_g, w_in, conv_w, conv_b, rg_a_w, rg_a_b, rg_x_w, rg_x_b, lru_lambda, w_lru_out, attn_sinks, rel_bias, w_attn_out, w_gate, b_gate, w_o, mix_post_g, ffn2_pre_g, ffn2_w1, ffn2_w3, ffn2_w2, ffn2_post_g, loss_target, m_ffn1_pre_g, m_ffn1_w1, m_ffn1_w3, m_ffn1_w2, m_ffn1_post_g, m_mix_pre_g, m_w_in, m_conv_w, m_conv_b, m_rg_a_w, m_rg_a_b, m_rg_x_w, m_rg_x_b, m_lru_lambda, m_w_lru_out, m_attn_sinks, m_rel_bias, m_w_attn_out, m_w_gate, m_b_gate, m_w_o, m_mix_post_g, m_ffn2_pre_g, m_ffn2_w1, m_ffn2_w3, m_ffn2_w2, m_ffn2_post_g, v_ffn1_pre_g, v_ffn1_w1, v_ffn1_w3, v_ffn1_w2, v_ffn1_post_g, v_mix_pre_g, v_w_in, v_conv_w, v_conv_b, v_rg_a_w, v_rg_a_b, v_rg_x_w, v_rg_x_b, v_lru_lambda, v_w_lru_out, v_attn_sinks, v_rel_bias, v_w_attn_out, v_w_gate, v_b_gate, v_w_o, v_mix_post_g, v_ffn2_pre_g, v_ffn2_w1, v_ffn2_w3, v_ffn2_w2, v_ffn2_post_g):
    given = dict(locals())
    chip = 2 * lax.axis_index("x") + lax.axis_index("y")

    chip_arr = jnp.reshape(chip, (1,)).astype(jnp.int32)
    bufs = {n: _cast_into_slot(given[n][0], chip_arr, "cast_" + n) for n in BIG}
    bufs["conv_w"] = lax.dynamic_update_slice(jnp.zeros((NSH, 4, D // NSH), F32), given["conv_w"], (chip, 0, 0))
    stage_names = {"ffn1": ["ffn1_w1", "ffn1_w3", "ffn1_w2", "conv_w"],
                   "mix": ["w_in", "w_gate", "w_lru_out", "w_attn_out", "w_o"],
                   "ffn2": ["ffn2_w1", "ffn2_w3", "ffn2_w2"]}
    in_flight, token = _gather_start([[bufs[n] for n in names] for names in stage_names.values()])
    in_flight = dict(zip(stage_names, in_flight))

    def weights(stage, after):
        names = stage_names[stage]
        send_sems, recv_sems, landing = in_flight[stage]
        landed = _gather_wait(send_sems, recv_sems, landing, after, "gather_wait_" + stage)
        halves = [b for b in landed if b.dtype == BF16]
        out = dict(zip([n for n, b in zip(names, landed) if b.dtype == BF16], _sibling_fill(halves, "sibling_fill_" + stage)))
        if "conv_w" in names:
            out["conv_w"] = jnp.transpose(landed[names.index("conv_w")], (1, 0, 2)).reshape(4, D)
        return out

    small_shapes = {n: given[n].shape for n, _ in SMALL}
    small_shapes["conv_w"] = (1, 4, D)
    sm = {n: (given[n][0] if given[n].shape[0] == 1 and n != "rel_bias" else given[n]) for n, _ in SMALL if n != "conv_w"}

    reducer = _Reducer()
    sq, dx, _, small = _local_step(x[0], loss_target[0], weights, sm, reducer)
    loss = lax.psum(sq[0, 0] * (0.5 / D), ("x", "y", "c"))

    reduced_small = _all_reduce_small(_pack_small(small))
    small_g = _unpack_small(reduced_small, small_shapes)
    reducer.advance("ffn1", reduced_small)
    grads, delta, new_m, new_v = {}, {}, {}, {}
    after = reduced_small
    for stage in ("ffn2", "mix", "ffn1"):
        halves = reducer.finish(stage, after)
        from_sibling = _send_halves(list(halves.values()), "send_halves_" + stage)
        for (n, mine), theirs in zip(halves.items(), from_sibling):
            grads[n], delta[n], new_m[n], new_v[n] = (r[None] for r in _adamw_halves(
                given[n][0], mine, theirs, given["m_" + n][0], given["v_" + n][0], "adamw_" + n))
            after = new_v[n]

    def widen(a):
        return lax.dynamic_update_slice(jnp.zeros((1, 4, D), F32), a, (0, 0, chip * (D // NSH)))

    packed = [_pack_small({n: (widen(given[pre + n]) if n == "conv_w" else given[pre + n]) for n, _ in SMALL})
              for pre in ("", "m_", "v_")]
    packed_g = _pack_small(small_g)
    outs = _adamw(packed[0], packed_g, packed[1], packed[2], "adamw_small")
    for dst, arr in zip((delta, new_m, new_v), outs):
        dst.update(_unpack_small(arr, small_shapes))
    small_out = dict(small_g)
    for d in (small_out, delta, new_m, new_v):
        d["conv_w"] = lax.dynamic_slice(d["conv_w"], (0, 0, chip * (D // NSH)), (1, 4, D // NSH))
    grads.update(small_out)
    return (loss, dx[None], *[grads[n] for n in WEIGHTS], *[delta[n] for n in WEIGHTS], *[new_m[n] for n in WEIGHTS],
            *[new_v[n] for n in WEIGHTS])
```

```python
import functools
import math

import jax
import jax.numpy as jnp
from jax import lax
from jax.experimental import pallas as pl
from jax.experimental.pallas import tpu as pltpu

F32, BF16 = jnp.float32, jnp.bfloat16
D = 1024
NSH = 4
FF_S = 704
IN_S = 896
GATE_S = 512
KV_W = 256
CHUNK = 64
KB = 192
N_HEADS = 16
HEAD_DIM = 64
N_BUCKETS = 32
KP = 256
PAD_KEYS = 128
RMS_EPS = 1e-6
NEG_INF = -1e30
LRU_C = 8.0
TM = 256
VMEM_LIMIT = 56 * 1024 * 1024
ADAM_LR, ADAM_B1, ADAM_B2, ADAM_EPS, ADAM_WD, ADAM_STEP = 0.001, 0.9, 0.999, 1e-08, 0.01, 10
SMALL_ROWS = 1216
SMALL_SLICE = SMALL_ROWS // 8
MESH = pl.DeviceIdType.MESH

BIG = ["ffn1_w1", "ffn1_w3", "ffn1_w2", "w_in", "w_lru_out", "w_attn_out", "w_gate", "w_o", "ffn2_w1", "ffn2_w3", "ffn2_w2"]
SMALL = [("ffn1_pre_g", 1024), ("ffn1_post_g", 1024), ("mix_pre_g", 1024), ("conv_w", 4096), ("conv_b", 1024),
         ("rg_a_w", 65536), ("rg_a_b", 1024), ("rg_x_w", 65536), ("rg_x_b", 1024), ("lru_lambda", 1024),
         ("attn_sinks", 1024), ("rel_bias", 1024), ("b_gate", 2048), ("mix_post_g", 1024), ("ffn2_pre_g", 1024),
         ("ffn2_post_g", 1024)]
WEIGHTS = ["ffn1_pre_g", "ffn1_w1", "ffn1_w3", "ffn1_w2", "ffn1_post_g", "mix_pre_g", "w_in", "conv_w", "conv_b", "rg_a_w",
           "rg_a_b", "rg_x_w", "rg_x_b", "lru_lambda", "w_lru_out", "attn_sinks", "rel_bias", "w_attn_out", "w_gate", "b_gate",
           "w_o", "mix_post_g", "ffn2_pre_g", "ffn2_w1", "ffn2_w3", "ffn2_w2", "ffn2_post_g"]


def _params(*sem):
    return pltpu.CompilerParams(dimension_semantics=sem or None, vmem_limit_bytes=VMEM_LIMIT)


def _nn(a, b):
    return jnp.dot(a, b, preferred_element_type=F32)


def _nt(a, b):
    return lax.dot_general(a, b, (((1,), (1,)), ((), ())), preferred_element_type=F32)


def _tn(a, b):
    return lax.dot_general(a, b, (((0,), (0,)), ((), ())), preferred_element_type=F32)


def _rms(x, g):
    rstd = lax.rsqrt(jnp.mean(x * x, axis=-1, keepdims=True) + RMS_EPS)
    return (x * rstd) * g


def _rms_bwd(dout, x, g):
    rstd = lax.rsqrt(jnp.mean(x * x, axis=-1, keepdims=True) + RMS_EPS)
    xhat = x * rstd
    dg = jnp.sum(dout * xhat, axis=0, keepdims=True)
    dxhat = dout * g
    dx = rstd * (dxhat - xhat * jnp.mean(dxhat * xhat, axis=-1, keepdims=True))
    return dx, dg


_GELU_K = math.sqrt(2.0 / math.pi)


def _gelu(x):
    return x * (0.5 * (1.0 + jnp.tanh(_GELU_K * (x + 0.044715 * (x * x * x)))))


def _gelu_grad(x):
    t = jnp.tanh(_GELU_K * (x + 0.044715 * (x * x * x)))
    return 0.5 * (1.0 + t) + x * (0.5 * (1.0 - t * t) * (_GELU_K * (1.0 + 3.0 * 0.044715 * (x * x))))


def _softplus_neg(lam):
    z = -lam
    u = jnp.exp(-jnp.abs(z))
    w = 1.0 + u
    log1p_u = jnp.where(w == 1.0, u, jnp.log(w) * (u / (w - 1.0)))
    return jnp.maximum(z, 0.0) + log1p_u


def _lru_coeffs(r, sp):
    log_a = (-LRU_C * r) * sp
    a = jnp.exp(log_a)
    t = jnp.tanh(log_a)
    s = jnp.sqrt(-2.0 * t / (1.0 - t))
    return a, s


def _row_spec(tm, width):
    return pl.BlockSpec((tm, width), lambda i: (i, 0))


def _vec_spec(width):
    return pl.BlockSpec((1, width), lambda i: (0, 0))


_WHOLE = pl.BlockSpec(memory_space=pltpu.VMEM)


def _tile(t):
    return min(TM, t)


def _ffn_fwd(x, gpre, w1g, w3g, w2g, gpost, name):
    t = x.shape[0]
    tm = _tile(t)

    def body(x_ref, gpre_ref, w1_ref, w3_ref, w2_ref, gpost_ref, h_ref, a_ref, b_ref, hm_ref, f_ref):
        xv = x_ref[...]
        nb = _rms(xv, gpre_ref[...]).astype(BF16)
        f = jnp.zeros((tm, D), F32)
        for s in range(NSH):
            a = _nn(nb, w1_ref[s])
            b = _nn(nb, w3_ref[s])
            hmb = ((a * jax.nn.sigmoid(a)) * b).astype(BF16)
            a_ref[s] = a.astype(BF16)
            b_ref[s] = b.astype(BF16)
            hm_ref[s] = hmb
            f = f + _nn(hmb, w2_ref[s])
        f_ref[...] = f
        h_ref[...] = xv + 0.5 * _rms(f, gpost_ref[...])

    sh = pl.BlockSpec((NSH, tm, FF_S), lambda i: (0, i, 0))
    act = jax.ShapeDtypeStruct((NSH, t, FF_S), BF16)
    return pl.pallas_call(
        body, grid=(t // tm,), name=name,
        in_specs=[_row_spec(tm, D), _vec_spec(D), _WHOLE, _WHOLE, _WHOLE, _vec_spec(D)],
        out_specs=[_row_spec(tm, D), sh, sh, sh, _row_spec(tm, D)],
        out_shape=[jax.ShapeDtypeStruct((t, D), F32), act, act, act, jax.ShapeDtypeStruct((t, D), F32)],
        compiler_params=_params("arbitrary"),
    )(x, gpre, w1g, w3g, w2g, gpost)


def _loss_dy(y, target):
    t = y.shape[0]
    tm = _tile(t)

    def body(y_ref, t_ref, dy_ref, l_ref):
        @pl.when(pl.program_id(0) == 0)
        def _():
            l_ref[...] = jnp.zeros_like(l_ref)

        e = y_ref[...] - t_ref[...]
        dy_ref[...] = e * (1.0 / D)
        sq = jnp.sum(jnp.sum(e * e, axis=0, keepdims=True), axis=1, keepdims=True)
        l_ref[...] = l_ref[...] + sq

    return pl.pallas_call(
        body, grid=(t // tm,), name="loss_dy",
        in_specs=[_row_spec(tm, D), _row_spec(tm, D)],
        out_specs=[_row_spec(tm, D), pl.BlockSpec((1, 128), lambda i: (0, 0))],
        out_shape=[jax.ShapeDtypeStruct((t, D), F32), jax.ShapeDtypeStruct((1, 128), F32)],
        compiler_params=_params("arbitrary"),
    )(y, target)


def _mix_proj(h1, gmix, w_in_g, w_gate_g, b_gate):
    t = h1.shape[0]
    tm = _tile(t)

    def body(h_ref, g_ref, win_ref, wg_ref, bg_ref, u_ref, q_ref, k_ref, v_ref, xr_ref, xg_ref, gate_ref):
        ub = _rms(h_ref[...], g_ref[...]).astype(BF16)
        u_ref[...] = ub
        p0 = _nn(ub, win_ref[0])
        q_ref[:, 0:896] = p0.astype(BF16)
        p1 = _nn(ub, win_ref[1])
        q_ref[:, 896:1024] = p1[:, 0:128].astype(BF16)
        k_ref[...] = p1[:, 128:384].astype(BF16)
        v_ref[...] = p1[:, 384:640].astype(BF16)
        xr_ref[:, 0:256] = p1[:, 640:896]
        p2 = _nn(ub, win_ref[2])
        xr_ref[:, 256:1024] = p2[:, 0:768]
        xg_ref[:, 0:128] = p2[:, 768:896]
        xg_ref[:, 128:1024] = _nn(ub, win_ref[3])
        for s in range(NSH):
            sl = slice(s * GATE_S, (s + 1) * GATE_S)
            gate_ref[:, sl] = jax.nn.sigmoid(_nn(ub, wg_ref[s]) + bg_ref[:, sl])

    return pl.pallas_call(
        body, grid=(t // tm,), name="mix_proj",
        in_specs=[_row_spec(tm, D), _vec_spec(D), _WHOLE, _WHOLE, _vec_spec(2 * D)],
        out_specs=[_row_spec(tm, D), _row_spec(tm, D), _row_spec(tm, KV_W), _row_spec(tm, KV_W), _row_spec(tm, D),
                   _row_spec(tm, D), _row_spec(tm, 2 * D)],
        out_shape=[jax.ShapeDtypeStruct((t, D), BF16), jax.ShapeDtypeStruct((t, D), BF16),
                   jax.ShapeDtypeStruct((t, KV_W), BF16), jax.ShapeDtypeStruct((t, KV_W), BF16),
                   jax.ShapeDtypeStruct((t, D), F32), jax.ShapeDtypeStruct((t, D), F32),
                   jax.ShapeDtypeStruct((t, 2 * D), F32)],
        compiler_params=_params("arbitrary"),
    )(h1, gmix, w_in_g, w_gate_g, b_gate)


def _rglru_fwd(xr, xg, conv_w, conv_b, wa2, ba, wx2, bx, lam):
    t = xr.shape[0]
    tm = _tile(t)
    nb8 = tm // 8

    def body(xr_ref, xrp_ref, xg_ref, cw_ref, cb_ref, wa_ref, ba_ref, wx_ref, bx_ref, lam_ref,
             hr_ref, yain_ref, xc_ref, r_ref, ig_ref, ext, a_sc, h_sc):
        i = pl.program_id(0)

        @pl.when(i == 0)
        def _():
            h_sc[...] = jnp.zeros_like(h_sc)

        ext[0:8, :] = jnp.where(i == 0, 0.0, xrp_ref[...])
        ext[8:8 + tm, :] = xr_ref[...]
        xc = jnp.broadcast_to(cb_ref[...], (tm, D))
        for tap in range(4):
            xc = xc + ext[pl.ds(5 + tap, tm), :] * cw_ref[tap:tap + 1, :]
        xc_ref[...] = xc
        xcb = xc.astype(BF16)
        for p in range(8):
            sl = slice(p * 128, (p + 1) * 128)
            r_ref[:, sl] = jax.nn.sigmoid(_nn(xcb[:, sl], wa_ref[p]) + ba_ref[:, sl])
            ig_ref[:, sl] = jax.nn.sigmoid(_nn(xcb[:, sl], wx_ref[p]) + bx_ref[:, sl])
        a, s = _lru_coeffs(r_ref[...], _softplus_neg(lam_ref[...]))
        a_sc[...] = a
        hr_ref[...] = s * (ig_ref[...] * xc)

        def blk(j, h):
            st = pl.multiple_of(j * 8, 8)
            a8 = a_sc[pl.ds(st, 8), :]
            u8 = hr_ref[pl.ds(st, 8), :]
            rows = []
            for k in range(8):
                h = a8[k:k + 1, :] * h + u8[k:k + 1, :]
                rows.append(h)
            hr_ref[pl.ds(st, 8), :] = jnp.concatenate(rows, axis=0)
            return h

        h_sc[0:1, :] = lax.fori_loop(0, nb8, blk, h_sc[0:1, :])
        yain_ref[...] = (hr_ref[...] * _gelu(xg_ref[...])).astype(BF16)

    prev = pl.BlockSpec((8, D), lambda i: (jnp.maximum(i * nb8 - 1, 0), 0))
    full = lambda shape: pl.BlockSpec(shape, lambda i: tuple(0 for _ in shape))
    f32 = jax.ShapeDtypeStruct((t, D), F32)
    return pl.pallas_call(
        body, grid=(t // tm,), name="rglru_fwd",
        in_specs=[_row_spec(tm, D), prev, _row_spec(tm, D), full((4, D)), _vec_spec(D), full((8, 128, 128)), _vec_spec(D),
                  full((8, 128, 128)), _vec_spec(D), _vec_spec(D)],
        out_specs=[_row_spec(tm, D)] * 5,
        out_shape=[f32, jax.ShapeDtypeStruct((t, D), BF16), f32, f32, f32],
        scratch_shapes=[pltpu.VMEM((tm + 8, D), F32), pltpu.VMEM((tm, D), F32), pltpu.VMEM((8, D), F32)],
        compiler_params=_params("arbitrary"),
    )(xr, xr, xg, conv_w, conv_b, wa2, ba, wx2, bx, lam)


def _bias_fwd(table_t, onehot_t):
    def body(t_ref, e_ref, o_ref):
        o_ref[...] = jnp.dot(t_ref[...], e_ref[...], preferred_element_type=F32, precision=lax.Precision.HIGHEST)

    return pl.pallas_call(body, out_shape=jax.ShapeDtypeStruct((N_HEADS, CHUNK * KB), F32), name="bias_fwd",
                          compiler_params=_params())(table_t, onehot_t)


def _bias_bwd(dbias_flat, onehot_t, ds_rows):
    def body(d_ref, e_ref, s_ref, o_ref, so_ref):
        o_ref[...] = lax.dot_general(d_ref[...], e_ref[...], (((1,), (1,)), ((), ())), preferred_element_type=F32,
                                     precision=lax.Precision.HIGHEST)
        so_ref[...] = jnp.zeros_like(so_ref)
        for r in range(4):
            so_ref[:, r:r + 1] = jnp.sum(s_ref[:, r * CHUNK:(r + 1) * CHUNK], axis=1, keepdims=True)

    return pl.pallas_call(body, out_shape=[jax.ShapeDtypeStruct((N_HEADS, N_BUCKETS), F32), jax.ShapeDtypeStruct((8, 128), F32)],
                          name="bias_bwd", compiler_params=_params())(dbias_flat, onehot_t, ds_rows)


def _stack_heads(q):
    return jnp.concatenate(
        [jnp.concatenate([q[:, (4 * g + r) * HEAD_DIM:(4 * g + r + 1) * HEAD_DIM] for g in range(4)], axis=1)
         for r in range(4)], axis=0)


def _unstack_heads(o):
    return jnp.concatenate([o[r * CHUNK:(r + 1) * CHUNK, g * HEAD_DIM:(g + 1) * HEAD_DIM] for g in range(4) for r in range(4)],
                           axis=1)


def _block_diag(w, mask):
    return jnp.concatenate([w] * 4, axis=0) * mask


def _attn_softmax(q_all, kbd, bias_t, sink_rows, c):
    s = _nt(kbd, q_all) * (HEAD_DIM ** -0.5) + bias_t
    j = lax.broadcasted_iota(jnp.int32, (4 * KP, 1), 0) % KP
    s = jnp.where((j < KB) & (j + c * CHUNK >= PAD_KEYS), s, NEG_INF)
    ps, sinks = [], []
    for g in range(4):
        sg = s[g * KP:(g + 1) * KP, :]
        sink = sink_rows[g:g + 1, :]
        m = jnp.maximum(jnp.max(sg, axis=0, keepdims=True), sink)
        e = jnp.exp(sg - m)
        es = jnp.exp(sink - m)
        inv = 1.0 / (jnp.sum(e, axis=0, keepdims=True) + es)
        ps.append(e * inv)
        sinks.append(es * inv)
    return ps, sinks


def _attn_fwd(sink_rows, q, kp, vp, bias_t, mask):
    t = q.shape[0]

    def body(sink_ref, q_ref, kp_ref, vp_ref, bias_ref, mask_ref, o_ref):
        c = pl.program_id(0)
        st = pl.multiple_of(c * CHUNK, CHUNK)
        kbd = _block_diag(kp_ref[pl.ds(st, KP), :], mask_ref[...])
        vbd = _block_diag(vp_ref[pl.ds(st, KP), :], mask_ref[...])
        ps, _ = _attn_softmax(_stack_heads(q_ref[...]), kbd, bias_ref[...], sink_ref[...], c)
        p_t = jnp.concatenate(ps, axis=0).astype(BF16)
        o_ref[...] = _unstack_heads(_tn(p_t, vbd)).astype(BF16)

    return pl.pallas_call(
        body, grid=(t // CHUNK,), name="attn_fwd",
        in_specs=[_WHOLE, _row_spec(CHUNK, D), _WHOLE, _WHOLE, _WHOLE, _WHOLE],
        out_specs=_row_spec(CHUNK, D),
        out_shape=jax.ShapeDtypeStruct((t, D), BF16),
        compiler_params=_params("arbitrary"),
    )(sink_rows, q, kp, vp, bias_t, mask)


def _merge_fwd(yain, o, gate, h1, w_lru, w_att, w_o, gpost):
    t = h1.shape[0]
    tm = _tile(t)

    def body(ya_ref, o_ref, g_ref, h_ref, wl_ref, wa_ref, wo_ref, gp_ref, h2_ref, mo_ref, mg_ref, ya_out, yb_out):
        ya = _nn(ya_ref[...], wl_ref[...])
        yb = _nn(o_ref[...], wa_ref[...])
        mg = (g_ref[:, 0:D] * ya + g_ref[:, D:2 * D] * yb).astype(BF16)
        mo = _nn(mg, wo_ref[...])
        ya_out[...] = ya.astype(BF16)
        yb_out[...] = yb.astype(BF16)
        mg_ref[...] = mg
        mo_ref[...] = mo
        h2_ref[...] = h_ref[...] + _rms(mo, gp_ref[...])

    f32 = jax.ShapeDtypeStruct((t, D), F32)
    b16 = jax.ShapeDtypeStruct((t, D), BF16)
    return pl.pallas_call(
        body, grid=(t // tm,), name="merge_fwd",
        in_specs=[_row_spec(tm, D), _row_spec(tm, D), _row_spec(tm, 2 * D), _row_spec(tm, D), _WHOLE, _WHOLE, _WHOLE,
                  _vec_spec(D)],
        out_specs=[_row_spec(tm, D)] * 5,
        out_shape=[f32, f32, b16, b16, b16],
        compiler_params=_params("arbitrary"),
    )(yain, o, gate, h1, w_lru, w_att, w_o, gpost)


def _ffn_bwd(dh, x, f, a, b, gpre, gpost, w1g, w3g, w2g, name):
    t = x.shape[0]
    tm = _tile(t)

    def body(dh_ref, x_ref, f_ref, a_ref, b_ref, gpre_ref, gpost_ref, w1_ref, w3_ref, w2_ref,
             dx_ref, n_ref, da_ref, db_ref, df_ref, dgpre_ref, dgpost_ref):
        @pl.when(pl.program_id(0) == 0)
        def _():
            dgpre_ref[...] = jnp.zeros_like(dgpre_ref)
            dgpost_ref[...] = jnp.zeros_like(dgpost_ref)

        dhv = dh_ref[...]
        xv = x_ref[...]
        df, dgp = _rms_bwd(0.5 * dhv, f_ref[...], gpost_ref[...])
        dgpost_ref[...] += dgp
        dfb = df.astype(BF16)
        df_ref[...] = dfb
        n_ref[...] = _rms(xv, gpre_ref[...]).astype(BF16)
        dn = jnp.zeros((tm, D), F32)
        for s in range(NSH):
            av = a_ref[s].astype(F32)
            bv = b_ref[s].astype(F32)
            sg = jax.nn.sigmoid(av)
            dhm = _nt(dfb, w2_ref[s])
            dab = (dhm * bv * (sg * (1.0 + av * (1.0 - sg)))).astype(BF16)
            dbb = (dhm * (av * sg)).astype(BF16)
            da_ref[s] = dab
            db_ref[s] = dbb
            dn = dn + _nt(dab, w1_ref[s]) + _nt(dbb, w3_ref[s])
        dxn, dg = _rms_bwd(dn, xv, gpre_ref[...])
        dgpre_ref[...] += dg
        dx_ref[...] = dhv + dxn

    sh = pl.BlockSpec((NSH, tm, FF_S), lambda i: (0, i, 0))
    act = jax.ShapeDtypeStruct((NSH, t, FF_S), BF16)
    vec = jax.ShapeDtypeStruct((1, D), F32)
    return pl.pallas_call(
        body, grid=(t // tm,), name=name,
        in_specs=[_row_spec(tm, D), _row_spec(tm, D), _row_spec(tm, D), sh, sh, _vec_spec(D), _vec_spec(D), _WHOLE, _WHOLE,
                  _WHOLE],
        out_specs=[_row_spec(tm, D), _row_spec(tm, D), sh, sh, _row_spec(tm, D), _vec_spec(D), _vec_spec(D)],
        out_shape=[jax.ShapeDtypeStruct((t, D), F32), jax.ShapeDtypeStruct((t, D), BF16), act, act,
                   jax.ShapeDtypeStruct((t, D), BF16), vec, vec],
        compiler_params=_params("arbitrary"),
    )(dh, x, f, a, b, gpre, gpost, w1g, w3g, w2g)


def _behind(body, after):
    if after is None:
        return body, [], []

    def ordered(_, *refs):
        body(*refs)

    return ordered, [_ANY], [after]


def _wgrad(a, b, a_spec, b_spec, out_spec, out_shape, grid, name, after=None):
    def body(a_ref, b_ref, o_ref):
        o_ref[...] = _tn(a_ref[...], b_ref[...])

    body, specs, operands = _behind(body, after)
    return pl.pallas_call(body, grid=grid, name=name, in_specs=specs + [a_spec, b_spec], out_specs=out_spec,
                          out_shape=jax.ShapeDtypeStruct(out_shape, F32),
                          compiler_params=_params(*("arbitrary",) * len(grid)))(*operands, a, b)


def _wgrad_cols(act, dsh, width, name, after=None):
    t = act.shape[0]
    if dsh.ndim == 3:
        b_spec = pl.BlockSpec((None, t, width), lambda s, k: (s, 0, 0))
    else:
        b_spec = pl.BlockSpec((t, width), lambda s, k: (0, s))
    return _wgrad(act, dsh, pl.BlockSpec((t, 512), lambda s, k: (0, k)), b_spec,
                  pl.BlockSpec((None, 512, width), lambda s, k: (s, k, 0)), (NSH, D, width), (NSH, 2), name, after)


def _wgrad_rows(hm, df, name, after=None):
    t = df.shape[0]
    return _wgrad(hm, df, pl.BlockSpec((None, t, FF_S), lambda s, j: (s, 0, 0)), pl.BlockSpec((t, 512), lambda s, j: (0, j)),
                  pl.BlockSpec((None, FF_S, 512), lambda s, j: (s, 0, j)), (NSH, FF_S, D), (NSH, 2), name, after)


def _wgrad_sq(a, b, name, after=None):
    t = a.shape[0]
    return _wgrad(a, b, pl.BlockSpec((t, 512), lambda i, j: (0, i)), pl.BlockSpec((t, 512), lambda i, j: (0, j)),
                  pl.BlockSpec((512, 512), lambda i, j: (i, j)), (D, D), (2, 2), name, after)


def _mix_bwd1(dh2, mo, gpost, gate, ya, yb, xg, hr, w_o, w_lru, w_att, after):
    t = dh2.shape[0]
    tm = _tile(t)

    def body(dh_ref, mo_ref, gp_ref, g_ref, ya_ref, yb_ref, xg_ref, hr_ref, wo_ref, wl_ref, wa_ref,
             dmo_ref, dya_ref, dyb_ref, dgate_ref, dhr_ref, dxg_ref, do_ref, dgp_ref, dbg_ref):
        @pl.when(pl.program_id(0) == 0)
        def _():
            dgp_ref[...] = jnp.zeros_like(dgp_ref)
            dbg_ref[...] = jnp.zeros_like(dbg_ref)

        dmo, dgp = _rms_bwd(dh_ref[...], mo_ref[...], gp_ref[...])
        dgp_ref[...] += dgp
        dmob = dmo.astype(BF16)
        dmo_ref[...] = dmob
        dm = _nt(dmob, wo_ref[...])
        g0 = g_ref[:, 0:D]
        g1 = g_ref[:, D:2 * D]
        dyab = (dm * g0).astype(BF16)
        dybb = (dm * g1).astype(BF16)
        dya_ref[...] = dyab
        dyb_ref[...] = dybb
        dg0 = dm * ya_ref[...].astype(F32) * (g0 * (1.0 - g0))
        dg1 = dm * yb_ref[...].astype(F32) * (g1 * (1.0 - g1))
        dgate_ref[:, 0:D] = dg0.astype(BF16)
        dgate_ref[:, D:2 * D] = dg1.astype(BF16)
        dbg_ref[:, 0:D] += jnp.sum(dg0, axis=0, keepdims=True)
        dbg_ref[:, D:2 * D] += jnp.sum(dg1, axis=0, keepdims=True)
        dyain = _nt(dyab, wl_ref[...])
        do_ref[...] = _nt(dybb, wa_ref[...]).astype(BF16)
        xgv = xg_ref[...]
        dhr_ref[...] = dyain * _gelu(xgv)
        dxg_ref[...] = (dyain * hr_ref[...] * _gelu_grad(xgv)).astype(BF16)

    b16 = jax.ShapeDtypeStruct((t, D), BF16)
    body, specs, operands = _behind(body, after)
    return pl.pallas_call(
        body, grid=(t // tm,), name="mix_bwd1",
        in_specs=specs + [_row_spec(tm, D), _row_spec(tm, D), _vec_spec(D), _row_spec(tm, 2 * D), _row_spec(tm, D),
                          _row_spec(tm, D), _row_spec(tm, D), _row_spec(tm, D), _WHOLE, _WHOLE, _WHOLE],
        out_specs=[_row_spec(tm, D), _row_spec(tm, D), _row_spec(tm, D), _row_spec(tm, 2 * D), _row_spec(tm, D),
                   _row_spec(tm, D), _row_spec(tm, D), _vec_spec(D), _vec_spec(2 * D)],
        out_shape=[b16, b16, b16, jax.ShapeDtypeStruct((t, 2 * D), BF16), jax.ShapeDtypeStruct((t, D), F32), b16, b16,
                   jax.ShapeDtypeStruct((1, D), F32), jax.ShapeDtypeStruct((1, 2 * D), F32)],
        compiler_params=_params("arbitrary"),
    )(*operands, dh2, mo, gpost, gate, ya, yb, xg, hr, w_o, w_lru, w_att)


def _rglru_bwd(dhr, hr, xc, r, ig, xr, conv_w, wa2, wx2, lam, after):
    t = dhr.shape[0]
    tm = _tile(t)
    nb8 = tm // 8
    nt = t // tm

    def body(dhr_ref, hr_ref, hrp_ref, xc_ref, r_ref, ig_ref, xr_ref, xrp_ref, cw_ref, wa_ref, wx_ref, lam_ref,
             dxr_ref, dwa_ref, dwx_ref, dba_ref, dbx_ref, dlam_ref, dcw_ref, dcb_ref,
             ext_h, ext_x, ext_d, a_sc, g_sc, c_sc, nxt_sc):
        i = pl.program_id(0)
        first_tile = i == nt - 1

        @pl.when(i == 0)
        def _():
            c_sc[...] = jnp.zeros_like(c_sc)
            nxt_sc[...] = jnp.zeros_like(nxt_sc)
            for ref in (dwa_ref, dwx_ref, dba_ref, dbx_ref, dlam_ref, dcw_ref, dcb_ref):
                ref[...] = jnp.zeros_like(ref)

        lamv = lam_ref[...]
        sp = _softplus_neg(lamv)
        rv = r_ref[...]
        igv = ig_ref[...]
        xcv = xc_ref[...]
        a, s = _lru_coeffs(rv, sp)
        a_sc[...] = a

        def blk(jj, c):
            st = pl.multiple_of((nb8 - 1 - jj) * 8, 8)
            d8 = dhr_ref[pl.ds(st, 8), :]
            a8 = a_sc[pl.ds(st, 8), :]
            rows = [None] * 8
            for k in range(7, -1, -1):
                g = d8[k:k + 1, :] + c
                c = a8[k:k + 1, :] * g
                rows[k] = g
            g_sc[pl.ds(st, 8), :] = jnp.concatenate(rows, axis=0)
            return c

        c_sc[0:1, :] = lax.fori_loop(0, nb8, blk, c_sc[0:1, :])
        g = g_sc[...]
        ext_h[0:8, :] = jnp.where(first_tile, 0.0, hrp_ref[...])
        ext_h[8:8 + tm, :] = hr_ref[...]
        hprev = ext_h[pl.ds(7, tm), :]
        d_s = g * (igv * xcv)
        dig = g * s * xcv
        dxc = g * s * igv
        dla = (g * hprev) * a - d_s * ((a * a) / s)
        dr_pre = (dla * (-LRU_C * sp)) * (rv * (1.0 - rv))
        di_pre = dig * (igv * (1.0 - igv))
        dlam_ref[...] += jnp.sum(dla * (LRU_C * rv), axis=0, keepdims=True) * jax.nn.sigmoid(-lamv)
        dba_ref[...] += jnp.sum(dr_pre, axis=0, keepdims=True)
        dbx_ref[...] += jnp.sum(di_pre, axis=0, keepdims=True)
        drb = dr_pre.astype(BF16)
        dib = di_pre.astype(BF16)
        xcb = xcv.astype(BF16)
        ext_d[tm:tm + 8, :] = nxt_sc[...]
        for p in range(8):
            sl = slice(p * 128, (p + 1) * 128)
            ext_d[0:tm, sl] = dxc[:, sl] + _nt(drb[:, sl], wa_ref[p]) + _nt(dib[:, sl], wx_ref[p])
            dwa_ref[p] += _tn(xcb[:, sl], drb[:, sl])
            dwx_ref[p] += _tn(xcb[:, sl], dib[:, sl])
        dxcv = ext_d[0:tm, :]
        nxt_sc[...] = ext_d[0:8, :]
        dcb_ref[...] += jnp.sum(dxcv, axis=0, keepdims=True)
        ext_x[0:8, :] = jnp.where(first_tile, 0.0, xrp_ref[...])
        ext_x[8:8 + tm, :] = xr_ref[...]
        dxr = jnp.zeros((tm, D), F32)
        for tap in range(4):
            dxr = dxr + ext_d[pl.ds(3 - tap, tm), :] * cw_ref[tap:tap + 1, :]
            dcw_ref[tap:tap + 1, :] += jnp.sum(dxcv * ext_x[pl.ds(5 + tap, tm), :], axis=0, keepdims=True)
        dxr_ref[...] = dxr.astype(BF16)

    rev = pl.BlockSpec((tm, D), lambda i: (nt - 1 - i, 0))
    prev = pl.BlockSpec((8, D), lambda i: (jnp.maximum((nt - 1 - i) * nb8 - 1, 0), 0))
    full = lambda shape: pl.BlockSpec(shape, lambda i: tuple(0 for _ in shape))
    vec = jax.ShapeDtypeStruct((1, D), F32)
    blocks = jax.ShapeDtypeStruct((8, 128, 128), F32)
    body, specs, operands = _behind(body, after)
    return pl.pallas_call(
        body, grid=(nt,), name="rglru_bwd",
        in_specs=specs + [rev, rev, prev, rev, rev, rev, rev, prev, full((4, D)), full((8, 128, 128)), full((8, 128, 128)),
                          _vec_spec(D)],
        out_specs=[rev, full((8, 128, 128)), full((8, 128, 128)), _vec_spec(D), _vec_spec(D), _vec_spec(D), full((4, D)),
                   _vec_spec(D)],
        out_shape=[jax.ShapeDtypeStruct((t, D), BF16), blocks, blocks, vec, vec, vec, jax.ShapeDtypeStruct((4, D), F32), vec],
        scratch_shapes=[pltpu.VMEM((tm + 8, D), F32), pltpu.VMEM((tm + 8, D), F32), pltpu.VMEM((tm + 8, D), F32),
                        pltpu.VMEM((tm, D), F32), pltpu.VMEM((tm, D), F32), pltpu.VMEM((8, D), F32), pltpu.VMEM((8, D), F32)],
        compiler_params=_params("arbitrary"),
    )(*operands, dhr, hr, hr, xc, r, ig, xr, xr, conv_w, wa2, wx2, lam)


def _attn_bwd(sink_rows, q, kp, vp, bias_t, mask, do):
    t = q.shape[0]
    tp = kp.shape[0]

    def body(sink_ref, q_ref, kp_ref, vp_ref, bias_ref, mask_ref, do_ref, dq_ref, dk_ref, dv_ref, dbias_ref, ds_ref):
        c = pl.program_id(0)

        @pl.when(c == 0)
        def _():
            for ref in (dk_ref, dv_ref, dbias_ref, ds_ref):
                ref[...] = jnp.zeros_like(ref)

        st = pl.multiple_of(c * CHUNK, CHUNK)
        maskv = mask_ref[...]
        kbd = _block_diag(kp_ref[pl.ds(st, KP), :], maskv)
        vbd = _block_diag(vp_ref[pl.ds(st, KP), :], maskv)
        q_all = _stack_heads(q_ref[...])
        do_all = _stack_heads(do_ref[...])
        ps, sinks = _attn_softmax(q_all, kbd, bias_ref[...], sink_ref[...], c)
        dp = _nt(vbd, do_all)
        dscs = []
        for g in range(4):
            dpg = dp[g * KP:(g + 1) * KP, :]
            delta = jnp.sum(ps[g] * dpg, axis=0, keepdims=True)
            dscs.append(ps[g] * (dpg - delta))
            ds_ref[g:g + 1, :] += -(sinks[g] * delta)
        dsc = jnp.concatenate(dscs, axis=0)
        dbias_ref[...] += dsc
        dsb = (dsc * (HEAD_DIM ** -0.5)).astype(BF16)
        dq_ref[...] = _unstack_heads(_tn(dsb, kbd)).astype(BF16)

        lane_group = lax.broadcasted_iota(jnp.int32, (1, 4 * HEAD_DIM), 1) // HEAD_DIM

        def own_blocks(full):
            out = full[0:KP]
            for g in range(1, 4):
                out = jnp.where(lane_group == g, full[g * KP:(g + 1) * KP], out)
            return out

        dk_ref[pl.ds(st, KP), :] += own_blocks(_nn(dsb, q_all))
        dv_ref[pl.ds(st, KP), :] += own_blocks(_nn(jnp.concatenate(ps, axis=0).astype(BF16), do_all))

    full = lambda shape: pl.BlockSpec(shape, lambda i: tuple(0 for _ in shape))
    return pl.pallas_call(
        body, grid=(t // CHUNK,), name="attn_bwd",
        in_specs=[_WHOLE, _row_spec(CHUNK, D), _WHOLE, _WHOLE, _WHOLE, _WHOLE, _row_spec(CHUNK, D)],
        out_specs=[_row_spec(CHUNK, D), full((tp, KV_W)), full((tp, KV_W)), full((4 * KP, 4 * CHUNK)), full((8, 4 * CHUNK))],
        out_shape=[jax.ShapeDtypeStruct((t, D), BF16), jax.ShapeDtypeStruct((tp, KV_W), F32),
                   jax.ShapeDtypeStruct((tp, KV_W), F32), jax.ShapeDtypeStruct((4 * KP, 4 * CHUNK), F32),
                   jax.ShapeDtypeStruct((8, 4 * CHUNK), F32)],
        compiler_params=_params("arbitrary"),
    )(sink_rows, q, kp, vp, bias_t, mask, do)


def _mix_bwd2(dproj, dgate, h1, dh2, gmix, w_in_g, w_gate_g, after):
    t = h1.shape[0]
    tm = _tile(t)

    def body(dp_ref, dg_ref, h_ref, dh_ref, g_ref, win_ref, wg_ref, dh1_ref, dgm_ref):
        @pl.when(pl.program_id(0) == 0)
        def _():
            dgm_ref[...] = jnp.zeros_like(dgm_ref)

        du = jnp.zeros((tm, D), F32)
        for s in range(NSH):
            du = du + _nt(dp_ref[:, s * IN_S:(s + 1) * IN_S], win_ref[s])
            du = du + _nt(dg_ref[:, s * GATE_S:(s + 1) * GATE_S], wg_ref[s])
        dxn, dg = _rms_bwd(du, h_ref[...], g_ref[...])
        dgm_ref[...] += dg
        dh1_ref[...] = dh_ref[...] + dxn

    body, specs, operands = _behind(body, after)
    return pl.pallas_call(
        body, grid=(t // tm,), name="mix_bwd2",
        in_specs=specs + [_row_spec(tm, NSH * IN_S), _row_spec(tm, 2 * D), _row_spec(tm, D), _row_spec(tm, D), _vec_spec(D),
                          _WHOLE, _WHOLE],
        out_specs=[_row_spec(tm, D), _vec_spec(D)],
        out_shape=[jax.ShapeDtypeStruct((t, D), F32), jax.ShapeDtypeStruct((1, D), F32)],
        compiler_params=_params("arbitrary"),
    )(*operands, dproj, dgate, h1, dh2, gmix, w_in_g, w_gate_g)


def _band_onehot():
    nb = N_BUCKETS // 2
    max_exact = nb // 2
    rel = jnp.arange(KB)[None, :] - PAD_KEYS - jnp.arange(CHUNK)[:, None]
    ret = jnp.where(rel > 0, nb, 0)
    n = jnp.abs(rel)
    nf = jnp.maximum(n, 1).astype(jnp.float32)
    large = max_exact + (jnp.log(nf / max_exact) / math.log(128 / max_exact) * (nb - max_exact)).astype(jnp.int32)
    large = jnp.minimum(large, nb - 1)
    buckets = (ret + jnp.where(n < max_exact, n, large)).reshape(1, CHUNK * KB)
    return (buckets == jnp.arange(N_BUCKETS)[:, None]).astype(F32)


def _pair_blocks(w):
    z = jnp.zeros((8, 128, 128), w.dtype)
    return z.at[:, 0:64, 0:64].set(w[0::2]).at[:, 64:128, 64:128].set(w[1::2])


def _unpair_blocks(w2):
    return jnp.stack([w2[:, 0:64, 0:64], w2[:, 64:128, 64:128]], axis=1).reshape(16, 64, 64)


def _local_step(x, target, weights, sm, reducer):
    row = lambda v: v.reshape(1, -1)
    wg = dict(weights("ffn1", x))
    sm = dict(sm, conv_w=wg["conv_w"])
    onehot_t = _band_onehot()
    bias = _bias_fwd(sm["rel_bias"].T, onehot_t).reshape(4, 4, CHUNK, KB)
    bias_t = jnp.pad(jnp.transpose(bias, (0, 3, 1, 2)), ((0, 0), (0, KP - KB), (0, 0), (0, 0))).reshape(4 * KP, 4 * CHUNK)
    sink_rows = jnp.pad(jnp.repeat(sm["attn_sinks"].reshape(4, 4), CHUNK, axis=1), ((0, 4), (0, 0)))
    grp = jnp.arange(4 * KP)[:, None] // KP == jnp.arange(4 * HEAD_DIM)[None, :] // HEAD_DIM
    mask = (grp & (jnp.arange(4 * KP)[:, None] % KP < KB)).astype(BF16)
    wa2 = _pair_blocks(sm["rg_a_w"]).astype(BF16)
    wx2 = _pair_blocks(sm["rg_x_w"]).astype(BF16)

    h1, a1, b1, hm1, f1 = _ffn_fwd(x, row(sm["ffn1_pre_g"]), wg["ffn1_w1"], wg["ffn1_w3"], wg["ffn1_w2"],
                                   row(sm["ffn1_post_g"]), "ffn1_fwd")
    wg.update(weights("mix", h1))
    w_lru = wg["w_lru_out"].reshape(D, D)
    w_att = wg["w_attn_out"].reshape(D, D)
    w_o = wg["w_o"].reshape(D, D)
    u, q, k, v, xr, xg, gate = _mix_proj(h1, row(sm["mix_pre_g"]), wg["w_in"], wg["w_gate"], row(sm["b_gate"]))
    hr, yain, xc, r, ig = _rglru_fwd(xr, xg, sm["conv_w"], row(sm["conv_b"]), wa2, row(sm["rg_a_b"]), wx2,
                                     row(sm["rg_x_b"]), row(sm["lru_lambda"]))
    kp = jnp.pad(k, ((PAD_KEYS, KP - KB), (0, 0)))
    vp = jnp.pad(v, ((PAD_KEYS, KP - KB), (0, 0)))
    o = _attn_fwd(sink_rows, q, kp, vp, bias_t, mask)
    wg.update(weights("ffn2", o))
    h2, mo, merged, ya, yb = _merge_fwd(yain, o, gate, h1, w_lru, w_att, w_o, row(sm["mix_post_g"]))
    y, a2, b2, hm2, f2 = _ffn_fwd(h2, row(sm["ffn2_pre_g"]), wg["ffn2_w1"], wg["ffn2_w3"], wg["ffn2_w2"],
                                  row(sm["ffn2_post_g"]), "ffn2_fwd")
    dy, sq = _loss_dy(y, target)

    big, small = {}, {}
    dh2, n2, da2, db2, df2, small["ffn2_pre_g"], small["ffn2_post_g"] = _ffn_bwd(
        dy, h2, f2, a2, b2, row(sm["ffn2_pre_g"]), row(sm["ffn2_post_g"]), wg["ffn2_w1"], wg["ffn2_w3"], wg["ffn2_w2"],
        "ffn2_bwd")
    big["ffn2_w1"] = _wgrad_cols(n2, da2, FF_S, "dw_ffn2_w1")
    big["ffn2_w3"] = _wgrad_cols(n2, db2, FF_S, "dw_ffn2_w3")
    big["ffn2_w2"] = _wgrad_rows(hm2, df2, "dw_ffn2_w2")
    token = reducer.begin("ffn2", {n: big[n] for n in ("ffn2_w1", "ffn2_w3", "ffn2_w2")})
    dmo, dya, dyb, dgate, dhr, dxg, do, small["mix_post_g"], small["b_gate"] = _mix_bwd1(
        dh2, mo, row(sm["mix_post_g"]), gate, ya, yb, xg, hr, w_o, w_lru, w_att, token)
    big["w_o"] = _wgrad_sq(merged, dmo, "dw_w_o").reshape(NSH, D // NSH, D)
    big["w_lru_out"] = _wgrad_sq(yain, dya, "dw_w_lru_out").reshape(NSH, D // NSH, D)
    big["w_attn_out"] = _wgrad_sq(o, dyb, "dw_w_attn_out").reshape(NSH, D // NSH, D)
    token = reducer.advance("ffn2", big["w_attn_out"])
    (dxr, dwa2, dwx2, small["rg_a_b"], small["rg_x_b"], small["lru_lambda"], small["conv_w"], small["conv_b"]) = _rglru_bwd(
        dhr, hr, xc, r, ig, xr, sm["conv_w"], wa2, wx2, row(sm["lru_lambda"]), token)
    small["rg_a_w"] = _unpair_blocks(dwa2)
    small["rg_x_w"] = _unpair_blocks(dwx2)
    dq, dkp, dvp, dbias_t, ds_rows = _attn_bwd(sink_rows, q, kp, vp, bias_t, mask, do)
    dbias = jnp.transpose(dbias_t.reshape(4, KP, 4, CHUNK)[:, :KB], (0, 2, 3, 1)).reshape(N_HEADS, CHUNK * KB)
    drel_t, dsinks = _bias_bwd(dbias, onehot_t, ds_rows)
    small["attn_sinks"] = dsinks[0:4, 0:4].reshape(N_HEADS)
    small["rel_bias"] = drel_t.T
    t = x.shape[0]
    dproj = jnp.concatenate([dq, dkp[PAD_KEYS:PAD_KEYS + t].astype(BF16), dvp[PAD_KEYS:PAD_KEYS + t].astype(BF16), dxr, dxg],
                            axis=1)
    big["w_in"] = _wgrad_cols(u, dproj, IN_S, "dw_w_in")
    big["w_gate"] = _wgrad_cols(u, dgate, GATE_S, "dw_w_gate")
    token = reducer.begin("mix", {n: big[n] for n in ("w_in", "w_gate", "w_lru_out", "w_attn_out", "w_o")})
    dh1, small["mix_pre_g"] = _mix_bwd2(dproj, dgate, h1, dh2, row(sm["mix_pre_g"]), wg["w_in"], wg["w_gate"], token)
    dx, n1, da1, db1, df1, small["ffn1_pre_g"], small["ffn1_post_g"] = _ffn_bwd(
        dh1, x, f1, a1, b1, row(sm["ffn1_pre_g"]), row(sm["ffn1_post_g"]), wg["ffn1_w1"], wg["ffn1_w3"], wg["ffn1_w2"],
        "ffn1_bwd")
    token = reducer.advance("mix", dx)
    big["ffn1_w1"] = _wgrad_cols(n1, da1, FF_S, "dw_ffn1_w1", token)
    big["ffn1_w3"] = _wgrad_cols(n1, db1, FF_S, "dw_ffn1_w3", token)
    big["ffn1_w2"] = _wgrad_rows(hm1, df1, "dw_ffn1_w2", token)
    reducer.begin("ffn1", {n: big[n] for n in ("ffn1_w1", "ffn1_w3", "ffn1_w2")})
    return sq, dx, big, small


_ANY = pl.BlockSpec(memory_space=pl.ANY)


def _place():
    return lax.axis_index("x"), lax.axis_index("y"), lax.axis_index("c")


def _other_chips(x, y):
    return [(1 - x, y), (x, 1 - y), (1 - x, 1 - y)]


_HBM = pl.BlockSpec(memory_space=pltpu.HBM)
_SEM = pl.BlockSpec(memory_space=pltpu.SEMAPHORE)
_EFFECT = pltpu.SideEffectType.DATAFLOW_SIDE_EFFECTING


def _cast_into_slot(w, chip, name):
    r, cc = w.shape
    rows = r // 4

    def body(chip_ref, w_ref, o_ref):
        o_ref[...] = w_ref[...].astype(BF16)

    return pl.pallas_call(
        body, name=name, out_shape=jax.ShapeDtypeStruct((NSH, r, cc), BF16),
        grid_spec=pltpu.PrefetchScalarGridSpec(
            num_scalar_prefetch=1, grid=(4,), in_specs=[pl.BlockSpec((rows, cc), lambda i, chip: (i, 0))],
            out_specs=pl.BlockSpec((None, rows, cc), lambda i, chip: (chip[0], i, 0))),
        compiler_params=_params("arbitrary"))(chip, w)


def _piece(ref, slot, c):
    if ref.dtype == F32:
        return ref.at[slot]
    rh = ref.shape[1] // 2
    return ref.at[slot, pl.ds(pl.multiple_of(c * rh, 16), rh), :]


def _gather_start(stages, name):
    flat = [b for stage in stages for b in stage]
    n, ns = len(flat), len(stages)

    def body(*refs):
        ins, sems, token = refs[:n], refs[n:n + 2 * ns], refs[-1]
        x, y, c = _place()
        me = 2 * x + y
        k = 0
        for s, stage in enumerate(stages):
            for i in range(len(stage)):
                for j, (px, py) in enumerate(_other_chips(x, y)):
                    piece = _piece(ins[k], me, c)
                    pltpu.make_async_remote_copy(src_ref=piece, dst_ref=piece, send_sem=sems[2 * s].at[3 * i + j],
                                                 recv_sem=sems[2 * s + 1].at[3 * i + j], device_id=(px, py, c),
                                                 device_id_type=MESH).start()
                k += 1
        token[...] = jnp.zeros_like(token)

    sem_shapes = [pltpu.SemaphoreType.DMA((3 * len(stage),)) for stage in stages for _ in range(2)]
    outs = pl.pallas_call(
        body, name=name, in_specs=[_HBM] * n,
        out_specs=[_SEM] * (2 * ns) + [_HBM] * n + [pl.BlockSpec(memory_space=pltpu.VMEM)],
        out_shape=sem_shapes + [pltpu.HBM(b.shape, b.dtype) for b in flat] + [jax.ShapeDtypeStruct((8, 128), F32)],
        input_output_aliases={i: 2 * ns + i for i in range(n)},
        compiler_params=pltpu.CompilerParams(has_side_effects=_EFFECT),
    )(*[pltpu.with_memory_space_constraint(b, pltpu.HBM) for b in flat])
    sems, bufs, token = outs[:2 * ns], list(outs[2 * ns:2 * ns + n]), outs[-1]
    per_stage, k = [], 0
    for s, stage in enumerate(stages):
        per_stage.append((sems[2 * s], sems[2 * s + 1], bufs[k:k + len(stage)]))
        k += len(stage)
    return per_stage, token


def _gather_wait(send_sems, recv_sems, bufs, after, name):
    n = len(bufs)

    def body(*refs):
        ins, ssem, rsem = refs[:n], refs[n], refs[n + 1]
        x, y, c = _place()
        me = 2 * x + y
        for i in range(n):
            for j, (px, py) in enumerate(_other_chips(x, y)):
                cp = pltpu.make_async_remote_copy(src_ref=_piece(ins[i], me, c), dst_ref=_piece(ins[i], 2 * px + py, c),
                                                  send_sem=ssem.at[3 * i + j], recv_sem=rsem.at[3 * i + j],
                                                  device_id=(px, py, c), device_id_type=MESH)
                cp.wait_send()
                cp.wait_recv()

    return pl.pallas_call(
        body, name=name, in_specs=[_HBM] * n + [_SEM, _SEM, _ANY], out_specs=[_HBM] * n,
        out_shape=[pltpu.HBM(b.shape, b.dtype) for b in bufs], input_output_aliases={i: i for i in range(n)},
        compiler_params=pltpu.CompilerParams(has_side_effects=_EFFECT),
    )(*bufs, send_sems, recv_sems, after)


def _sibling_fill(bufs, name):
    n = len(bufs)

    def body(*refs):
        ins, outs = refs[:n], refs[n:2 * n]
        send_sems, recv_sems = refs[2 * n:]
        x, y, c = _place()
        copies = []
        for i in range(n):
            for j, (px, py) in enumerate(_other_chips(x, y)):
                copies.append(pltpu.make_async_remote_copy(
                    src_ref=_piece(ins[i], 2 * px + py, c), dst_ref=_piece(outs[i], 2 * px + py, c),
                    send_sem=send_sems.at[3 * i + j], recv_sem=recv_sems.at[3 * i + j], device_id=(x, y, 1 - c),
                    device_id_type=MESH))
                copies[-1].start()
        for cp in copies:
            cp.wait()

    return pl.pallas_call(
        body, name=name, in_specs=[_ANY] * n, out_specs=[_ANY] * n,
        out_shape=[jax.ShapeDtypeStruct(b.shape, b.dtype) for b in bufs], input_output_aliases={i: i for i in range(n)},
        scratch_shapes=[pltpu.SemaphoreType.DMA((3 * n,)), pltpu.SemaphoreType.DMA((3 * n,))],
        compiler_params=pltpu.CompilerParams(has_side_effects=True),
    )(*bufs)


def _swap_plan(srcs, lands):
    x, y, c = _place()
    plan = []
    for src, land in zip(srcs, lands):
        rh = src.shape[1] // 2
        plan.append((src.at[:, pl.ds(pl.multiple_of((1 - c) * rh, 8), rh), :], land, (x, y, 1 - c)))
    return plan


def _owners_plan(srcs, lands):
    x, y, c = _place()
    return [(src.at[2 * px + py], land.at[j], (px, py, c))
            for src, land in zip(srcs, lands) for j, (px, py) in enumerate(_other_chips(x, y))]


def _exchange_start(srcs, lands, plan, copies, name):
    n = len(srcs)

    def body(*refs):
        send_sems, recv_sems, token = refs[2 * n], refs[2 * n + 1], refs[-1]
        for k, (src, dst, dev) in enumerate(plan(refs[:n], refs[n:2 * n])):
            pltpu.make_async_remote_copy(src_ref=src, dst_ref=dst, send_sem=send_sems.at[k], recv_sem=recv_sems.at[k],
                                         device_id=dev, device_id_type=MESH).start()
        token[...] = jnp.zeros_like(token)

    both = list(srcs) + list(lands)
    outs = pl.pallas_call(
        body, name=name, in_specs=[_HBM] * (2 * n),
        out_specs=[_SEM, _SEM] + [_HBM] * (2 * n) + [pl.BlockSpec(memory_space=pltpu.VMEM)],
        out_shape=[pltpu.SemaphoreType.DMA((copies,)), pltpu.SemaphoreType.DMA((copies,))]
        + [pltpu.HBM(b.shape, b.dtype) for b in both] + [jax.ShapeDtypeStruct((8, 128), F32)],
        input_output_aliases={i: 2 + i for i in range(2 * n)},
        compiler_params=pltpu.CompilerParams(has_side_effects=_EFFECT),
    )(*[pltpu.with_memory_space_constraint(b, pltpu.HBM) for b in both])
    return (outs[0], outs[1]), list(outs[2:2 + n]), list(outs[2 + n:2 + 2 * n]), outs[-1]


def _exchange_wait(sems, srcs, lands, plan, after, name):
    n = len(srcs)

    def body(*refs):
        send_sems, recv_sems = refs[2 * n], refs[2 * n + 1]
        for k, (src, dst, dev) in enumerate(plan(refs[:n], refs[n:2 * n])):
            cp = pltpu.make_async_remote_copy(src_ref=src, dst_ref=dst, send_sem=send_sems.at[k], recv_sem=recv_sems.at[k],
                                              device_id=dev, device_id_type=MESH)
            cp.wait_send()
            cp.wait_recv()

    both = list(srcs) + list(lands)
    outs = pl.pallas_call(
        body, name=name, in_specs=[_HBM] * (2 * n) + [_SEM, _SEM, _ANY], out_specs=[_HBM] * (2 * n),
        out_shape=[pltpu.HBM(b.shape, b.dtype) for b in both], input_output_aliases={i: i for i in range(2 * n)},
        compiler_params=pltpu.CompilerParams(has_side_effects=_EFFECT),
    )(*both, sems[0], sems[1], after)
    return list(outs[:n]), list(outs[n:])


class _Reducer:
    def __init__(self):
        self.state = {}

    def begin(self, stage, grads):
        names = list(grads)
        full = [grads[n] for n in names]
        lands = [lax.empty((NSH, g.shape[1] // 2, g.shape[2]), F32) for g in full]
        sems, full, lands, token = _exchange_start(full, lands, _swap_plan, len(full), "swap_start_" + stage)
        self.state[stage] = (names, sems, full, lands)
        return token

    def advance(self, stage, after):
        names, sems, full, lands = self.state[stage]
        full, got = _exchange_wait(sems, full, lands, _swap_plan, after, "swap_wait_" + stage)
        sums = [_chip_sum(g, a, "chip_sum_" + n) for n, g, a in zip(names, full, got)]
        lands = [lax.empty((3,) + s[0].shape[1:], BF16) for s in sums]
        sems, sent, lands, token = _exchange_start([s[0] for s in sums], lands, _owners_plan, 3 * len(sums),
                                                   "owners_start_" + stage)
        self.state[stage] = (names, [s[1] for s in sums], sems, sent, lands)
        return token

    def finish(self, stage, after):
        names, own, sems, sent, lands = self.state[stage]
        _, got = _exchange_wait(sems, sent, lands, _owners_plan, after, "owners_wait_" + stage)
        return {n: _owner_sum(o, g, "owner_sum_" + n) for n, o, g in zip(names, own, got)}


def _chip_sum(g, got, name):
    _, r, cc = g.shape
    rh = r // 2

    def body(g_ref, got_ref, hb_ref, own_ref):
        x, y, c = _place()
        s = pl.program_id(0)
        h = g_ref[pl.ds(pl.multiple_of(c * rh, 8), rh), :] + got_ref[...]
        hb_ref[...] = h.astype(BF16)

        @pl.when(s == 2 * x + y)
        def _():
            own_ref[...] = h

    return pl.pallas_call(
        body, grid=(NSH,), name=name,
        in_specs=[pl.BlockSpec((None, r, cc), lambda s: (s, 0, 0)), pl.BlockSpec((None, rh, cc), lambda s: (s, 0, 0))],
        out_specs=[pl.BlockSpec((None, rh, cc), lambda s: (s, 0, 0)), pl.BlockSpec((rh, cc), lambda s: (0, 0))],
        out_shape=[jax.ShapeDtypeStruct((NSH, rh, cc), BF16), jax.ShapeDtypeStruct((rh, cc), F32)],
        compiler_params=_params("arbitrary"),
    )(g, got)


def _owner_sum(own, got, name):
    rh, cc = own.shape
    rows = rh // 2

    def body(own_ref, got_ref, o_ref):
        o_ref[...] = ((own_ref[...] + got_ref[0].astype(F32)) + got_ref[1].astype(F32)) + got_ref[2].astype(F32)

    return pl.pallas_call(
        body, grid=(2,), name=name,
        in_specs=[pl.BlockSpec((rows, cc), lambda i: (i, 0)), pl.BlockSpec((3, rows, cc), lambda i: (0, i, 0))],
        out_specs=pl.BlockSpec((rows, cc), lambda i: (i, 0)),
        out_shape=jax.ShapeDtypeStruct((rh, cc), F32), compiler_params=_params("arbitrary"),
    )(own, got)


def _send_halves(halves, name):
    n = len(halves)

    def body(*refs):
        ins, outs = refs[:n], refs[n:2 * n]
        send_sems, recv_sems = refs[2 * n:]
        x, y, c = _place()
        copies = [pltpu.make_async_remote_copy(src_ref=ins[w], dst_ref=outs[w], send_sem=send_sems.at[w], recv_sem=recv_sems.at[w],
                                               device_id=(x, y, 1 - c), device_id_type=MESH) for w in range(n)]
        for cp in copies:
            cp.start()
        for cp in copies:
            cp.wait()

    return pl.pallas_call(
        body, name=name, in_specs=[_ANY] * n, out_specs=[_ANY] * n,
        out_shape=[jax.ShapeDtypeStruct(h.shape, F32) for h in halves],
        scratch_shapes=[pltpu.SemaphoreType.DMA((n,)), pltpu.SemaphoreType.DMA((n,))],
        compiler_params=pltpu.CompilerParams(has_side_effects=True),
    )(*halves)


def _all_reduce_small(part):
    def body(p_ref, o_ref, rbuf, send1, recv1, send2, recv2):
        x, y, c = _place()
        me = 4 * x + 2 * y + c
        peers = []
        for k in range(1, 8):
            px, py, pc = x ^ ((k >> 2) & 1), y ^ ((k >> 1) & 1), c ^ (k & 1)
            peers.append((k, (px, py, pc), 4 * px + 2 * py + pc))

        def rows(d):
            return pl.ds(pl.multiple_of(d * SMALL_SLICE, 8), SMALL_SLICE)

        first = [pltpu.make_async_remote_copy(src_ref=p_ref.at[rows(idx), :], dst_ref=rbuf.at[me], send_sem=send1.at[k],
                                              recv_sem=recv1.at[k], device_id=dev, device_id_type=MESH)
                 for k, dev, idx in peers]
        for cp in first:
            cp.start()
        rbuf[me] = p_ref[rows(me), :]
        for k, dev, idx in peers:
            pltpu.make_async_remote_copy(src_ref=p_ref.at[rows(idx), :], dst_ref=rbuf.at[idx], send_sem=send1.at[k],
                                         recv_sem=recv1.at[k], device_id=dev, device_id_type=MESH).wait_recv()
        acc = rbuf[0]
        for d in range(1, 8):
            acc = acc + rbuf[d]
        o_ref[rows(me), :] = acc
        second = [pltpu.make_async_remote_copy(src_ref=o_ref.at[rows(me), :], dst_ref=o_ref.at[rows(me), :],
                                               send_sem=send2.at[k], recv_sem=recv2.at[k], device_id=dev, device_id_type=MESH)
                  for k, dev, idx in peers]
        for cp in second:
            cp.start()
        for k, dev, idx in peers:
            pltpu.make_async_remote_copy(src_ref=o_ref.at[rows(me), :], dst_ref=o_ref.at[rows(idx), :], send_sem=send2.at[k],
                                         recv_sem=recv2.at[k], device_id=dev, device_id_type=MESH).wait_recv()
        for cp in first + second:
            cp.wait_send()

    return pl.pallas_call(
        body, name="all_reduce_small", in_specs=[_WHOLE], out_specs=_WHOLE,
        out_shape=jax.ShapeDtypeStruct((SMALL_ROWS, 128), F32),
        scratch_shapes=[pltpu.VMEM((8, SMALL_SLICE, 128), F32)] + [pltpu.SemaphoreType.DMA((8,))] * 4,
        compiler_params=pltpu.CompilerParams(has_side_effects=True),
    )(part)


def _adamw_update(w, gv, m, v):
    nm = ADAM_B1 * m + (1.0 - ADAM_B1) * gv
    nv = ADAM_B2 * v + (1.0 - ADAM_B2) * (gv * gv)
    m_hat = nm / (1.0 - ADAM_B1 ** ADAM_STEP)
    v_hat = nv / (1.0 - ADAM_B2 ** ADAM_STEP)
    return -ADAM_LR * (m_hat / (jnp.sqrt(v_hat) + ADAM_EPS) + ADAM_WD * w), nm, nv


def _adamw(w, g, m, v, name):
    rows = w.shape[0] // 4

    def body(w_ref, g_ref, m_ref, v_ref, d_ref, nm_ref, nv_ref):
        d_ref[...], nm_ref[...], nv_ref[...] = _adamw_update(w_ref[...], g_ref[...], m_ref[...], v_ref[...])

    spec = pl.BlockSpec((rows, w.shape[1]), lambda i: (i, 0))
    out = jax.ShapeDtypeStruct(w.shape, F32)
    return pl.pallas_call(body, grid=(4,), in_specs=[spec] * 4, out_specs=[spec] * 3, out_shape=[out] * 3, name=name,
                          compiler_params=_params("arbitrary"))(w, g, m, v)


def _adamw_halves(w, mine, theirs, m, v, name):
    rh, cc = mine.shape
    rows = rh // 2

    def body(w_ref, mine_ref, theirs_ref, m_ref, v_ref, g_ref, d_ref, nm_ref, nv_ref):
        gv = jnp.where(pl.program_id(0) == lax.axis_index("c"), mine_ref[...], theirs_ref[...])
        g_ref[...] = gv
        d_ref[...], nm_ref[...], nv_ref[...] = _adamw_update(w_ref[...], gv, m_ref[...], v_ref[...])

    spec = pl.BlockSpec((rows, cc), lambda h, i: (2 * h + i, 0))
    half = pl.BlockSpec((rows, cc), lambda h, i: (i, 0))
    out = jax.ShapeDtypeStruct(w.shape, F32)
    return pl.pallas_call(body, grid=(2, 2), in_specs=[spec, half, half, spec, spec], out_specs=[spec] * 4,
                          out_shape=[out] * 4, name=name, compiler_params=_params("arbitrary", "arbitrary"))(w, mine, theirs, m, v)


def _pack_small(vals):
    parts = []
    for name, size in SMALL:
        flat = vals[name].reshape(-1).astype(F32)
        parts.append(jnp.pad(flat, (0, size - flat.shape[0])))
    flat = jnp.concatenate(parts)
    return jnp.pad(flat, (0, SMALL_ROWS * 128 - flat.shape[0])).reshape(SMALL_ROWS, 128)


def _unpack_small(packed, shapes):
    flat = packed.reshape(-1)
    out, off = {}, 0
    for name, size in SMALL:
        n = math.prod(shapes[name])
        out[name] = flat[off:off + n].reshape(shapes[name])
        off += size
    return out


def kernel(x, ffn1_pre_g, ffn1_w1, ffn1_w3, ffn1_w2, ffn1_post_g, mix_pre_g, w_in, conv_w, conv_b, rg_a_w, rg_a_b, rg_x_w, rg_x_b, lru_lambda, w_lru_out, attn_sinks, rel_bias, w_attn_out, w_gate, b_gate, w_o, mix_post_g, ffn2_pre_g, ffn2_w1, ffn2_w3, ffn2_w2, ffn2_post_g, loss_target, m_ffn1_pre_g, m_ffn1_w1, m_ffn1_w3, m_ffn1_w2, m_ffn1_post_g, m_mix_pre_g, m_w_in, m_conv_w, m_conv_b, m_rg_a_w, m_rg_a_b, m_rg_x_w, m_rg_x_b, m_lru_lambda, m_w_lru_out, m_attn_sinks, m_rel_bias, m_w_attn_out, m_w_gate, m_b_gate, m_w_o, m_mix_post_g, m_ffn2_pre_g, m_ffn2_w1, m_ffn2_w3, m_ffn2_w2, m_ffn2_post_g, v_ffn1_pre_g, v_ffn1_w1, v_ffn1_w3, v_ffn1_w2, v_ffn1_post_g, v_mix_pre_g, v_w_in, v_conv_w, v_conv_b, v_rg_a_w, v_rg_a_b, v_rg_x_w, v_rg_x_b, v_lru_lambda, v_w_lru_out, v_attn_sinks, v_rel_bias, v_w_attn_out, v_w_gate, v_b_gate, v_w_o, v_mix_post_g, v_ffn2_pre_g, v_ffn2_w1, v_ffn2_w3, v_ffn2_w2, v_ffn2_post_g):
    given = dict(locals())
    chip = 2 * lax.axis_index("x") + lax.axis_index("y")

    chip_arr = jnp.reshape(chip, (1,)).astype(jnp.int32)
    stage_names = {"ffn1": ["ffn1_w1", "ffn1_w3", "ffn1_w2", "conv_w"],
                   "mix": ["w_in", "w_gate", "w_lru_out", "w_attn_out", "w_o"],
                   "ffn2": ["ffn2_w1", "ffn2_w3", "ffn2_w2"]}
    in_flight = {}
    for stage, names in stage_names.items():
        bufs = [lax.dynamic_update_slice(jnp.zeros((NSH, 4, D // NSH), F32), given[n], (chip, 0, 0)) if n == "conv_w"
                else _cast_into_slot(given[n][0], chip_arr, "cast_" + n) for n in names]
        in_flight[stage] = _gather_start([bufs], "gather_start_" + stage)[0][0]

    def weights(stage, after):
        names = stage_names[stage]
        send_sems, recv_sems, landing = in_flight[stage]
        landed = _gather_wait(send_sems, recv_sems, landing, after, "gather_wait_" + stage)
        halves = [b for b in landed if b.dtype == BF16]
        out = dict(zip([n for n, b in zip(names, landed) if b.dtype == BF16], _sibling_fill(halves, "sibling_fill_" + stage)))
        if "conv_w" in names:
            out["conv_w"] = jnp.transpose(landed[names.index("conv_w")], (1, 0, 2)).reshape(4, D)
        return out

    small_shapes = {n: given[n].shape for n, _ in SMALL}
    small_shapes["conv_w"] = (1, 4, D)
    sm = {n: (given[n][0] if given[n].shape[0] == 1 and n != "rel_bias" else given[n]) for n, _ in SMALL if n != "conv_w"}

    reducer = _Reducer()
    sq, dx, _, small = _local_step(x[0], loss_target[0], weights, sm, reducer)
    loss = lax.psum(sq[0, 0] * (0.5 / D), ("x", "y", "c"))

    reduced_small = _all_reduce_small(_pack_small(small))
    small_g = _unpack_small(reduced_small, small_shapes)
    reducer.advance("ffn1", reduced_small)
    grads, delta, new_m, new_v = {}, {}, {}, {}
    after = reduced_small
    for stage in ("ffn2", "mix", "ffn1"):
        halves = reducer.finish(stage, after)
        from_sibling = _send_halves(list(halves.values()), "send_halves_" + stage)
        for (n, mine), theirs in zip(halves.items(), from_sibling):
            grads[n], delta[n], new_m[n], new_v[n] = (r[None] for r in _adamw_halves(
                given[n][0], mine, theirs, given["m_" + n][0], given["v_" + n][0], "adamw_" + n))
            after = new_v[n]

    def widen(a):
        return lax.dynamic_update_slice(jnp.zeros((1, 4, D), F32), a, (0, 0, chip * (D // NSH)))

    packed = [_pack_small({n: (widen(given[pre + n]) if n == "conv_w" else given[pre + n]) for n, _ in SMALL})
              for pre in ("", "m_", "v_")]
    packed_g = _pack_small(small_g)
    outs = _adamw(packed[0], packed_g, packed[1], packed[2], "adamw_small")
    for dst, arr in zip((delta, new_m, new_v), outs):
        dst.update(_unpack_small(arr, small_shapes))
    small_out = dict(small_g)
    for d in (small_out, delta, new_m, new_v):
        d["conv_w"] = lax.dynamic_slice(d["conv_w"], (0, 0, chip * (D // NSH)), (1, 4, D // NSH))
    grads.update(small_out)
    return (loss, dx[None], *[grads[n] for n in WEIGHTS], *[delta[n] for n in WEIGHTS], *[new_m[n] for n in WEIGHTS],
            *[new_v[n] for n in WEIGHTS])
```

```python
import functools
import math

import jax
import jax.numpy as jnp
from jax import lax
from jax.experimental import pallas as pl
from jax.experimental.pallas import tpu as pltpu

F32, BF16 = jnp.float32, jnp.bfloat16
D = 1024
NSH = 4
FF_S = 704
IN_S = 896
GATE_S = 512
KV_W = 256
CHUNK = 64
KB = 192
N_HEADS = 16
HEAD_DIM = 64
N_BUCKETS = 32
KP = 256
PAD_KEYS = 128
RMS_EPS = 1e-6
NEG_INF = -1e30
LRU_C = 8.0
TM = 256
VMEM_LIMIT = 56 * 1024 * 1024
ADAM_LR, ADAM_B1, ADAM_B2, ADAM_EPS, ADAM_WD, ADAM_STEP = 0.001, 0.9, 0.999, 1e-08, 0.01, 10
SMALL_ROWS = 1216
SMALL_SLICE = SMALL_ROWS // 8
MESH = pl.DeviceIdType.MESH

BIG = ["ffn1_w1", "ffn1_w3", "ffn1_w2", "w_in", "w_lru_out", "w_attn_out", "w_gate", "w_o", "ffn2_w1", "ffn2_w3", "ffn2_w2"]
SMALL = [("ffn1_pre_g", 1024), ("ffn1_post_g", 1024), ("mix_pre_g", 1024), ("conv_w", 4096), ("conv_b", 1024),
         ("rg_a_w", 65536), ("rg_a_b", 1024), ("rg_x_w", 65536), ("rg_x_b", 1024), ("lru_lambda", 1024),
         ("attn_sinks", 1024), ("rel_bias", 1024), ("b_gate", 2048), ("mix_post_g", 1024), ("ffn2_pre_g", 1024),
         ("ffn2_post_g", 1024)]
WEIGHTS = ["ffn1_pre_g", "ffn1_w1", "ffn1_w3", "ffn1_w2", "ffn1_post_g", "mix_pre_g", "w_in", "conv_w", "conv_b", "rg_a_w",
           "rg_a_b", "rg_x_w", "rg_x_b", "lru_lambda", "w_lru_out", "attn_sinks", "rel_bias", "w_attn_out", "w_gate", "b_gate",
           "w_o", "mix_post_g", "ffn2_pre_g", "ffn2_w1", "ffn2_w3", "ffn2_w2", "ffn2_post_g"]


def _params(*sem):
    return pltpu.CompilerParams(dimension_semantics=sem or None, vmem_limit_bytes=VMEM_LIMIT)


def _nn(a, b):
    return jnp.dot(a, b, preferred_element_type=F32)


def _nt(a, b):
    return lax.dot_general(a, b, (((1,), (1,)), ((), ())), preferred_element_type=F32)


def _tn(a, b):
    return lax.dot_general(a, b, (((0,), (0,)), ((), ())), preferred_element_type=F32)


def _rms(x, g):
    rstd = lax.rsqrt(jnp.mean(x * x, axis=-1, keepdims=True) + RMS_EPS)
    return (x * rstd) * g


def _rms_bwd(dout, x, g):
    rstd = lax.rsqrt(jnp.mean(x * x, axis=-1, keepdims=True) + RMS_EPS)
    xhat = x * rstd
    dg = jnp.sum(dout * xhat, axis=0, keepdims=True)
    dxhat = dout * g
    dx = rstd * (dxhat - xhat * jnp.mean(dxhat * xhat, axis=-1, keepdims=True))
    return dx, dg


_GELU_K = math.sqrt(2.0 / math.pi)


def _gelu(x):
    return x * (0.5 * (1.0 + jnp.tanh(_GELU_K * (x + 0.044715 * (x * x * x)))))


def _gelu_grad(x):
    t = jnp.tanh(_GELU_K * (x + 0.044715 * (x * x * x)))
    return 0.5 * (1.0 + t) + x * (0.5 * (1.0 - t * t) * (_GELU_K * (1.0 + 3.0 * 0.044715 * (x * x))))


def _softplus_neg(lam):
    z = -lam
    u = jnp.exp(-jnp.abs(z))
    w = 1.0 + u
    log1p_u = jnp.where(w == 1.0, u, jnp.log(w) * (u / (w - 1.0)))
    return jnp.maximum(z, 0.0) + log1p_u


def _lru_coeffs(r, sp):
    log_a = (-LRU_C * r) * sp
    a = jnp.exp(log_a)
    t = jnp.tanh(log_a)
    s = jnp.sqrt(-2.0 * t / (1.0 - t))
    return a, s


def _row_spec(tm, width):
    return pl.BlockSpec((tm, width), lambda i: (i, 0))


def _vec_spec(width):
    return pl.BlockSpec((1, width), lambda i: (0, 0))


_WHOLE = pl.BlockSpec(memory_space=pltpu.VMEM)


def _tile(t):
    return min(TM, t)


def _ffn_fwd(x, gpre, w1g, w3g, w2g, gpost, name):
    t = x.shape[0]
    tm = _tile(t)

    def body(x_ref, gpre_ref, w1_ref, w3_ref, w2_ref, gpost_ref, h_ref, a_ref, b_ref, hm_ref, f_ref):
        xv = x_ref[...]
        nb = _rms(xv, gpre_ref[...]).astype(BF16)
        f = jnp.zeros((tm, D), F32)
        for s in range(NSH):
            a = _nt(nb, w1_ref[s])
            b = _nt(nb, w3_ref[s])
            hmb = ((a * jax.nn.sigmoid(a)) * b).astype(BF16)
            a_ref[s] = a.astype(BF16)
            b_ref[s] = b.astype(BF16)
            hm_ref[s] = hmb
            f = f + _nn(hmb, w2_ref[s])
        f_ref[...] = f
        h_ref[...] = xv + 0.5 * _rms(f, gpost_ref[...])

    sh = pl.BlockSpec((NSH, tm, FF_S), lambda i: (0, i, 0))
    act = jax.ShapeDtypeStruct((NSH, t, FF_S), BF16)
    return pl.pallas_call(
        body, grid=(t // tm,), name=name,
        in_specs=[_row_spec(tm, D), _vec_spec(D), _WHOLE, _WHOLE, _WHOLE, _vec_spec(D)],
        out_specs=[_row_spec(tm, D), sh, sh, sh, _row_spec(tm, D)],
        out_shape=[jax.ShapeDtypeStruct((t, D), F32), act, act, act, jax.ShapeDtypeStruct((t, D), F32)],
        compiler_params=_params("arbitrary"),
    )(x, gpre, w1g, w3g, w2g, gpost)


def _loss_dy(y, target):
    t = y.shape[0]
    tm = _tile(t)

    def body(y_ref, t_ref, dy_ref, l_ref):
        @pl.when(pl.program_id(0) == 0)
        def _():
            l_ref[...] = jnp.zeros_like(l_ref)

        e = y_ref[...] - t_ref[...]
        dy_ref[...] = e * (1.0 / D)
        sq = jnp.sum(jnp.sum(e * e, axis=0, keepdims=True), axis=1, keepdims=True)
        l_ref[...] = l_ref[...] + sq

    return pl.pallas_call(
        body, grid=(t // tm,), name="loss_dy",
        in_specs=[_row_spec(tm, D), _row_spec(tm, D)],
        out_specs=[_row_spec(tm, D), pl.BlockSpec((1, 128), lambda i: (0, 0))],
        out_shape=[jax.ShapeDtypeStruct((t, D), F32), jax.ShapeDtypeStruct((1, 128), F32)],
        compiler_params=_params("arbitrary"),
    )(y, target)


def _mix_proj(h1, gmix, w_in_g, w_gate_g, b_gate):
    t = h1.shape[0]
    tm = _tile(t)

    def body(h_ref, g_ref, win_ref, wg_ref, bg_ref, u_ref, q_ref, k_ref, v_ref, xr_ref, xg_ref, gate_ref):
        ub = _rms(h_ref[...], g_ref[...]).astype(BF16)
        u_ref[...] = ub
        p0 = _nn(ub, win_ref[0])
        q_ref[:, 0:896] = p0.astype(BF16)
        p1 = _nn(ub, win_ref[1])
        q_ref[:, 896:1024] = p1[:, 0:128].astype(BF16)
        k_ref[...] = p1[:, 128:384].astype(BF16)
        v_ref[...] = p1[:, 384:640].astype(BF16)
        xr_ref[:, 0:256] = p1[:, 640:896]
        p2 = _nn(ub, win_ref[2])
        xr_ref[:, 256:1024] = p2[:, 0:768]
        xg_ref[:, 0:128] = p2[:, 768:896]
        xg_ref[:, 128:1024] = _nn(ub, win_ref[3])
        for s in range(NSH):
            sl = slice(s * GATE_S, (s + 1) * GATE_S)
            gate_ref[:, sl] = jax.nn.sigmoid(_nn(ub, wg_ref[s]) + bg_ref[:, sl])

    return pl.pallas_call(
        body, grid=(t // tm,), name="mix_proj",
        in_specs=[_row_spec(tm, D), _vec_spec(D), _WHOLE, _WHOLE, _vec_spec(2 * D)],
        out_specs=[_row_spec(tm, D), _row_spec(tm, D), _row_spec(tm, KV_W), _row_spec(tm, KV_W), _row_spec(tm, D),
                   _row_spec(tm, D), _row_spec(tm, 2 * D)],
        out_shape=[jax.ShapeDtypeStruct((t, D), BF16), jax.ShapeDtypeStruct((t, D), BF16),
                   jax.ShapeDtypeStruct((t, KV_W), BF16), jax.ShapeDtypeStruct((t, KV_W), BF16),
                   jax.ShapeDtypeStruct((t, D), F32), jax.ShapeDtypeStruct((t, D), F32),
                   jax.ShapeDtypeStruct((t, 2 * D), F32)],
        compiler_params=_params("arbitrary"),
    )(h1, gmix, w_in_g, w_gate_g, b_gate)


def _rglru_fwd(xr, xg, conv_w, conv_b, wa2, ba, wx2, bx, lam):
    t = xr.shape[0]
    tm = _tile(t)
    nb8 = tm // 8

    def body(xr_ref, xrp_ref, xg_ref, cw_ref, cb_ref, wa_ref, ba_ref, wx_ref, bx_ref, lam_ref,
             hr_ref, yain_ref, xc_ref, r_ref, ig_ref, ext, a_sc, h_sc):
        i = pl.program_id(0)

        @pl.when(i == 0)
        def _():
            h_sc[...] = jnp.zeros_like(h_sc)

        ext[0:8, :] = jnp.where(i == 0, 0.0, xrp_ref[...])
        ext[8:8 + tm, :] = xr_ref[...]
        xc = jnp.broadcast_to(cb_ref[...], (tm, D))
        for tap in range(4):
            xc = xc + ext[pl.ds(5 + tap, tm), :] * cw_ref[tap:tap + 1, :]
        xc_ref[...] = xc
        xcb = xc.astype(BF16)
        for p in range(8):
            sl = slice(p * 128, (p + 1) * 128)
            r_ref[:, sl] = jax.nn.sigmoid(_nn(xcb[:, sl], wa_ref[p]) + ba_ref[:, sl])
            ig_ref[:, sl] = jax.nn.sigmoid(_nn(xcb[:, sl], wx_ref[p]) + bx_ref[:, sl])
        a, s = _lru_coeffs(r_ref[...], _softplus_neg(lam_ref[...]))
        a_sc[...] = a
        hr_ref[...] = s * (ig_ref[...] * xc)

        def blk(j, h):
            st = pl.multiple_of(j * 8, 8)
            a8 = a_sc[pl.ds(st, 8), :]
            u8 = hr_ref[pl.ds(st, 8), :]
            rows = []
            for k in range(8):
                h = a8[k:k + 1, :] * h + u8[k:k + 1, :]
                rows.append(h)
            hr_ref[pl.ds(st, 8), :] = jnp.concatenate(rows, axis=0)
            return h

        h_sc[0:1, :] = lax.fori_loop(0, nb8, blk, h_sc[0:1, :])
        yain_ref[...] = (hr_ref[...] * _gelu(xg_ref[...])).astype(BF16)

    prev = pl.BlockSpec((8, D), lambda i: (jnp.maximum(i * nb8 - 1, 0), 0))
    full = lambda shape: pl.BlockSpec(shape, lambda i: tuple(0 for _ in shape))
    f32 = jax.ShapeDtypeStruct((t, D), F32)
    return pl.pallas_call(
        body, grid=(t // tm,), name="rglru_fwd",
        in_specs=[_row_spec(tm, D), prev, _row_spec(tm, D), full((4, D)), _vec_spec(D), full((8, 128, 128)), _vec_spec(D),
                  full((8, 128, 128)), _vec_spec(D), _vec_spec(D)],
        out_specs=[_row_spec(tm, D)] * 5,
        out_shape=[f32, jax.ShapeDtypeStruct((t, D), BF16), f32, f32, f32],
        scratch_shapes=[pltpu.VMEM((tm + 8, D), F32), pltpu.VMEM((tm, D), F32), pltpu.VMEM((8, D), F32)],
        compiler_params=_params("arbitrary"),
    )(xr, xr, xg, conv_w, conv_b, wa2, ba, wx2, bx, lam)


def _bias_fwd(table_t, onehot_t):
    def body(t_ref, e_ref, o_ref):
        o_ref[...] = jnp.dot(t_ref[...], e_ref[...], preferred_element_type=F32, precision=lax.Precision.HIGHEST)

    return pl.pallas_call(body, out_shape=jax.ShapeDtypeStruct((N_HEADS, CHUNK * KB), F32), name="bias_fwd",
                          compiler_params=_params())(table_t, onehot_t)


def _bias_bwd(dbias_flat, onehot_t, ds_rows):
    def body(d_ref, e_ref, s_ref, o_ref, so_ref):
        o_ref[...] = lax.dot_general(d_ref[...], e_ref[...], (((1,), (1,)), ((), ())), preferred_element_type=F32,
                                     precision=lax.Precision.HIGHEST)
        so_ref[...] = jnp.zeros_like(so_ref)
        for r in range(4):
            so_ref[:, r:r + 1] = jnp.sum(s_ref[:, r * CHUNK:(r + 1) * CHUNK], axis=1, keepdims=True)

    return pl.pallas_call(body, out_shape=[jax.ShapeDtypeStruct((N_HEADS, N_BUCKETS), F32), jax.ShapeDtypeStruct((8, 128), F32)],
                          name="bias_bwd", compiler_params=_params())(dbias_flat, onehot_t, ds_rows)


def _stack_heads(q):
    return jnp.concatenate(
        [jnp.concatenate([q[:, (4 * g + r) * HEAD_DIM:(4 * g + r + 1) * HEAD_DIM] for g in range(4)], axis=1)
         for r in range(4)], axis=0)


def _unstack_heads(o):
    return jnp.concatenate([o[r * CHUNK:(r + 1) * CHUNK, g * HEAD_DIM:(g + 1) * HEAD_DIM] for g in range(4) for r in range(4)],
                           axis=1)


def _block_diag(w, mask):
    return jnp.concatenate([w] * 4, axis=0) * mask


def _attn_softmax(q_all, kbd, bias_t, sink_rows, c):
    s = _nt(kbd, q_all) * (HEAD_DIM ** -0.5) + bias_t
    j = lax.broadcasted_iota(jnp.int32, (4 * KP, 1), 0) % KP
    s = jnp.where((j < KB) & (j + c * CHUNK >= PAD_KEYS), s, NEG_INF)
    ps, sinks = [], []
    for g in range(4):
        sg = s[g * KP:(g + 1) * KP, :]
        sink = sink_rows[g:g + 1, :]
        m = jnp.maximum(jnp.max(sg, axis=0, keepdims=True), sink)
        e = jnp.exp(sg - m)
        es = jnp.exp(sink - m)
        inv = 1.0 / (jnp.sum(e, axis=0, keepdims=True) + es)
        ps.append(e * inv)
        sinks.append(es * inv)
    return ps, sinks


def _attn_fwd(sink_rows, q, kp, vp, bias_t, mask):
    t = q.shape[0]

    def body(sink_ref, q_ref, kp_ref, vp_ref, bias_ref, mask_ref, o_ref):
        c = pl.program_id(0)
        st = pl.multiple_of(c * CHUNK, CHUNK)
        kbd = _block_diag(kp_ref[pl.ds(st, KP), :], mask_ref[...])
        vbd = _block_diag(vp_ref[pl.ds(st, KP), :], mask_ref[...])
        ps, _ = _attn_softmax(_stack_heads(q_ref[...]), kbd, bias_ref[...], sink_ref[...], c)
        p_t = jnp.concatenate(ps, axis=0).astype(BF16)
        o_ref[...] = _unstack_heads(_tn(p_t, vbd)).astype(BF16)

    return pl.pallas_call(
        body, grid=(t // CHUNK,), name="attn_fwd",
        in_specs=[_WHOLE, _row_spec(CHUNK, D), _WHOLE, _WHOLE, _WHOLE, _WHOLE],
        out_specs=_row_spec(CHUNK, D),
        out_shape=jax.ShapeDtypeStruct((t, D), BF16),
        compiler_params=_params("arbitrary"),
    )(sink_rows, q, kp, vp, bias_t, mask)


def _merge_fwd(yain, o, gate, h1, w_lru, w_att, w_o, gpost):
    t = h1.shape[0]
    tm = _tile(t)

    def body(ya_ref, o_ref, g_ref, h_ref, wl_ref, wa_ref, wo_ref, gp_ref, h2_ref, mo_ref, mg_ref, ya_out, yb_out):
        ya = _nn(ya_ref[...], wl_ref[...])
        yb = _nn(o_ref[...], wa_ref[...])
        mg = (g_ref[:, 0:D] * ya + g_ref[:, D:2 * D] * yb).astype(BF16)
        mo = _nn(mg, wo_ref[...])
        ya_out[...] = ya.astype(BF16)
        yb_out[...] = yb.astype(BF16)
        mg_ref[...] = mg
        mo_ref[...] = mo
        h2_ref[...] = h_ref[...] + _rms(mo, gp_ref[...])

    f32 = jax.ShapeDtypeStruct((t, D), F32)
    b16 = jax.ShapeDtypeStruct((t, D), BF16)
    return pl.pallas_call(
        body, grid=(t // tm,), name="merge_fwd",
        in_specs=[_row_spec(tm, D), _row_spec(tm, D), _row_spec(tm, 2 * D), _row_spec(tm, D), _WHOLE, _WHOLE, _WHOLE,
                  _vec_spec(D)],
        out_specs=[_row_spec(tm, D)] * 5,
        out_shape=[f32, f32, b16, b16, b16],
        compiler_params=_params("arbitrary"),
    )(yain, o, gate, h1, w_lru, w_att, w_o, gpost)


def _ffn_bwd(dh, x, f, a, b, gpre, gpost, w1g, w3g, w2g, name):
    t = x.shape[0]
    tm = _tile(t)

    def body(dh_ref, x_ref, f_ref, a_ref, b_ref, gpre_ref, gpost_ref, w1_ref, w3_ref, w2_ref,
             dx_ref, n_ref, da_ref, db_ref, df_ref, dgpre_ref, dgpost_ref):
        @pl.when(pl.program_id(0) == 0)
        def _():
            dgpre_ref[...] = jnp.zeros_like(dgpre_ref)
            dgpost_ref[...] = jnp.zeros_like(dgpost_ref)

        dhv = dh_ref[...]
        xv = x_ref[...]
        df, dgp = _rms_bwd(0.5 * dhv, f_ref[...], gpost_ref[...])
        dgpost_ref[...] += dgp
        dfb = df.astype(BF16)
        df_ref[...] = dfb
        n_ref[...] = _rms(xv, gpre_ref[...]).astype(BF16)
        dn = jnp.zeros((tm, D), F32)
        for s in range(NSH):
            av = a_ref[s].astype(F32)
            bv = b_ref[s].astype(F32)
            sg = jax.nn.sigmoid(av)
            dhm = _nt(dfb, w2_ref[s])
            dab = (dhm * bv * (sg * (1.0 + av * (1.0 - sg)))).astype(BF16)
            dbb = (dhm * (av * sg)).astype(BF16)
            da_ref[s] = dab
            db_ref[s] = dbb
            dn = dn + _nn(dab, w1_ref[s]) + _nn(dbb, w3_ref[s])
        dxn, dg = _rms_bwd(dn, xv, gpre_ref[...])
        dgpre_ref[...] += dg
        dx_ref[...] = dhv + dxn

    sh = pl.BlockSpec((NSH, tm, FF_S), lambda i: (0, i, 0))
    act = jax.ShapeDtypeStruct((NSH, t, FF_S), BF16)
    vec = jax.ShapeDtypeStruct((1, D), F32)
    return pl.pallas_call(
        body, grid=(t // tm,), name=name,
        in_specs=[_row_spec(tm, D), _row_spec(tm, D), _row_spec(tm, D), sh, sh, _vec_spec(D), _vec_spec(D), _WHOLE, _WHOLE,
                  _WHOLE],
        out_specs=[_row_spec(tm, D), _row_spec(tm, D), sh, sh, _row_spec(tm, D), _vec_spec(D), _vec_spec(D)],
        out_shape=[jax.ShapeDtypeStruct((t, D), F32), jax.ShapeDtypeStruct((t, D), BF16), act, act,
                   jax.ShapeDtypeStruct((t, D), BF16), vec, vec],
        compiler_params=_params("arbitrary"),
    )(dh, x, f, a, b, gpre, gpost, w1g, w3g, w2g)


def _behind(body, after):
    if after is None:
        return body, [], []

    def ordered(_, *refs):
        body(*refs)

    return ordered, [_ANY], [after]


def _wgrad(a, b, a_spec, b_spec, out_spec, out_shape, grid, name, after=None):
    def body(a_ref, b_ref, o_ref):
        o_ref[...] = _tn(a_ref[...], b_ref[...])

    body, specs, operands = _behind(body, after)
    return pl.pallas_call(body, grid=grid, name=name, in_specs=specs + [a_spec, b_spec], out_specs=out_spec,
                          out_shape=jax.ShapeDtypeStruct(out_shape, F32),
                          compiler_params=_params(*("arbitrary",) * len(grid)))(*operands, a, b)


def _wgrad_cols(act, dsh, width, name, after=None):
    t = act.shape[0]
    if dsh.ndim == 3:
        b_spec = pl.BlockSpec((None, t, width), lambda s, k: (s, 0, 0))
    else:
        b_spec = pl.BlockSpec((t, width), lambda s, k: (0, s))
    return _wgrad(act, dsh, pl.BlockSpec((t, 512), lambda s, k: (0, k)), b_spec,
                  pl.BlockSpec((None, 512, width), lambda s, k: (s, k, 0)), (NSH, D, width), (NSH, 2), name, after)


def _wgrad_rows(hm, df, name, after=None):
    t = df.shape[0]
    return _wgrad(hm, df, pl.BlockSpec((None, t, FF_S), lambda s, j: (s, 0, 0)), pl.BlockSpec((t, 512), lambda s, j: (0, j)),
                  pl.BlockSpec((None, FF_S, 512), lambda s, j: (s, 0, j)), (NSH, FF_S, D), (NSH, 2), name, after)


def _wgrad_sq(a, b, name, after=None):
    t = a.shape[0]
    return _wgrad(a, b, pl.BlockSpec((t, 512), lambda i, j: (0, i)), pl.BlockSpec((t, 512), lambda i, j: (0, j)),
                  pl.BlockSpec((512, 512), lambda i, j: (i, j)), (D, D), (2, 2), name, after)


def _mix_bwd1(dh2, mo, gpost, gate, ya, yb, xg, hr, w_o, w_lru, w_att, after):
    t = dh2.shape[0]
    tm = _tile(t)

    def body(dh_ref, mo_ref, gp_ref, g_ref, ya_ref, yb_ref, xg_ref, hr_ref, wo_ref, wl_ref, wa_ref,
             dmo_ref, dya_ref, dyb_ref, dgate_ref, dhr_ref, dxg_ref, do_ref, dgp_ref, dbg_ref):
        @pl.when(pl.program_id(0) == 0)
        def _():
            dgp_ref[...] = jnp.zeros_like(dgp_ref)
            dbg_ref[...] = jnp.zeros_like(dbg_ref)

        dmo, dgp = _rms_bwd(dh_ref[...], mo_ref[...], gp_ref[...])
        dgp_ref[...] += dgp
        dmob = dmo.astype(BF16)
        dmo_ref[...] = dmob
        dm = _nt(dmob, wo_ref[...])
        g0 = g_ref[:, 0:D]
        g1 = g_ref[:, D:2 * D]
        dyab = (dm * g0).astype(BF16)
        dybb = (dm * g1).astype(BF16)
        dya_ref[...] = dyab
        dyb_ref[...] = dybb
        dg0 = dm * ya_ref[...].astype(F32) * (g0 * (1.0 - g0))
        dg1 = dm * yb_ref[...].astype(F32) * (g1 * (1.0 - g1))
        dgate_ref[:, 0:D] = dg0.astype(BF16)
        dgate_ref[:, D:2 * D] = dg1.astype(BF16)
        dbg_ref[:, 0:D] += jnp.sum(dg0, axis=0, keepdims=True)
        dbg_ref[:, D:2 * D] += jnp.sum(dg1, axis=0, keepdims=True)
        dyain = _nt(dyab, wl_ref[...])
        do_ref[...] = _nt(dybb, wa_ref[...]).astype(BF16)
        xgv = xg_ref[...]
        dhr_ref[...] = dyain * _gelu(xgv)
        dxg_ref[...] = (dyain * hr_ref[...] * _gelu_grad(xgv)).astype(BF16)

    b16 = jax.ShapeDtypeStruct((t, D), BF16)
    body, specs, operands = _behind(body, after)
    return pl.pallas_call(
        body, grid=(t // tm,), name="mix_bwd1",
        in_specs=specs + [_row_spec(tm, D), _row_spec(tm, D), _vec_spec(D), _row_spec(tm, 2 * D), _row_spec(tm, D),
                          _row_spec(tm, D), _row_spec(tm, D), _row_spec(tm, D), _WHOLE, _WHOLE, _WHOLE],
        out_specs=[_row_spec(tm, D), _row_spec(tm, D), _row_spec(tm, D), _row_spec(tm, 2 * D), _row_spec(tm, D),
                   _row_spec(tm, D), _row_spec(tm, D), _vec_spec(D), _vec_spec(2 * D)],
        out_shape=[b16, b16, b16, jax.ShapeDtypeStruct((t, 2 * D), BF16), jax.ShapeDtypeStruct((t, D), F32), b16, b16,
                   jax.ShapeDtypeStruct((1, D), F32), jax.ShapeDtypeStruct((1, 2 * D), F32)],
        compiler_params=_params("arbitrary"),
    )(*operands, dh2, mo, gpost, gate, ya, yb, xg, hr, w_o, w_lru, w_att)


def _rglru_bwd(dhr, hr, xc, r, ig, xr, conv_w, wa2, wx2, lam, after):
    t = dhr.shape[0]
    tm = _tile(t)
    nb8 = tm // 8
    nt = t // tm

    def body(dhr_ref, hr_ref, hrp_ref, xc_ref, r_ref, ig_ref, xr_ref, xrp_ref, cw_ref, wa_ref, wx_ref, lam_ref,
             dxr_ref, dwa_ref, dwx_ref, dba_ref, dbx_ref, dlam_ref, dcw_ref, dcb_ref,
             ext_h, ext_x, ext_d, a_sc, g_sc, c_sc, nxt_sc):
        i = pl.program_id(0)
        first_tile = i == nt - 1

        @pl.when(i == 0)
        def _():
            c_sc[...] = jnp.zeros_like(c_sc)
            nxt_sc[...] = jnp.zeros_like(nxt_sc)
            for ref in (dwa_ref, dwx_ref, dba_ref, dbx_ref, dlam_ref, dcw_ref, dcb_ref):
                ref[...] = jnp.zeros_like(ref)

        lamv = lam_ref[...]
        sp = _softplus_neg(lamv)
        rv = r_ref[...]
        igv = ig_ref[...]
        xcv = xc_ref[...]
        a, s = _lru_coeffs(rv, sp)
        a_sc[...] = a

        def blk(jj, c):
            st = pl.multiple_of((nb8 - 1 - jj) * 8, 8)
            d8 = dhr_ref[pl.ds(st, 8), :]
            a8 = a_sc[pl.ds(st, 8), :]
            rows = [None] * 8
            for k in range(7, -1, -1):
                g = d8[k:k + 1, :] + c
                c = a8[k:k + 1, :] * g
                rows[k] = g
            g_sc[pl.ds(st, 8), :] = jnp.concatenate(rows, axis=0)
            return c

        c_sc[0:1, :] = lax.fori_loop(0, nb8, blk, c_sc[0:1, :])
        g = g_sc[...]
        ext_h[0:8, :] = jnp.where(first_tile, 0.0, hrp_ref[...])
        ext_h[8:8 + tm, :] = hr_ref[...]
        hprev = ext_h[pl.ds(7, tm), :]
        d_s = g * (igv * xcv)
        dig = g * s * xcv
        dxc = g * s * igv
        dla = (g * hprev) * a - d_s * ((a * a) / s)
        dr_pre = (dla * (-LRU_C * sp)) * (rv * (1.0 - rv))
        di_pre = dig * (igv * (1.0 - igv))
        dlam_ref[...] += jnp.sum(dla * (LRU_C * rv), axis=0, keepdims=True) * jax.nn.sigmoid(-lamv)
        dba_ref[...] += jnp.sum(dr_pre, axis=0, keepdims=True)
        dbx_ref[...] += jnp.sum(di_pre, axis=0, keepdims=True)
        drb = dr_pre.astype(BF16)
        dib = di_pre.astype(BF16)
        xcb = xcv.astype(BF16)
        ext_d[tm:tm + 8, :] = nxt_sc[...]
        for p in range(8):
            sl = slice(p * 128, (p + 1) * 128)
            ext_d[0:tm, sl] = dxc[:, sl] + _nt(drb[:, sl], wa_ref[p]) + _nt(dib[:, sl], wx_ref[p])
            dwa_ref[p] += _tn(xcb[:, sl], drb[:, sl])
            dwx_ref[p] += _tn(xcb[:, sl], dib[:, sl])
        dxcv = ext_d[0:tm, :]
        nxt_sc[...] = ext_d[0:8, :]
        dcb_ref[...] += jnp.sum(dxcv, axis=0, keepdims=True)
        ext_x[0:8, :] = jnp.where(first_tile, 0.0, xrp_ref[...])
        ext_x[8:8 + tm, :] = xr_ref[...]
        dxr = jnp.zeros((tm, D), F32)
        for tap in range(4):
            dxr = dxr + ext_d[pl.ds(3 - tap, tm), :] * cw_ref[tap:tap + 1, :]
            dcw_ref[tap:tap + 1, :] += jnp.sum(dxcv * ext_x[pl.ds(5 + tap, tm), :], axis=0, keepdims=True)
        dxr_ref[...] = dxr.astype(BF16)

    rev = pl.BlockSpec((tm, D), lambda i: (nt - 1 - i, 0))
    prev = pl.BlockSpec((8, D), lambda i: (jnp.maximum((nt - 1 - i) * nb8 - 1, 0), 0))
    full = lambda shape: pl.BlockSpec(shape, lambda i: tuple(0 for _ in shape))
    vec = jax.ShapeDtypeStruct((1, D), F32)
    blocks = jax.ShapeDtypeStruct((8, 128, 128), F32)
    body, specs, operands = _behind(body, after)
    return pl.pallas_call(
        body, grid=(nt,), name="rglru_bwd",
        in_specs=specs + [rev, rev, prev, rev, rev, rev, rev, prev, full((4, D)), full((8, 128, 128)), full((8, 128, 128)),
                          _vec_spec(D)],
        out_specs=[rev, full((8, 128, 128)), full((8, 128, 128)), _vec_spec(D), _vec_spec(D), _vec_spec(D), full((4, D)),
                   _vec_spec(D)],
        out_shape=[jax.ShapeDtypeStruct((t, D), BF16), blocks, blocks, vec, vec, vec, jax.ShapeDtypeStruct((4, D), F32), vec],
        scratch_shapes=[pltpu.VMEM((tm + 8, D), F32), pltpu.VMEM((tm + 8, D), F32), pltpu.VMEM((tm + 8, D), F32),
                        pltpu.VMEM((tm, D), F32), pltpu.VMEM((tm, D), F32), pltpu.VMEM((8, D), F32), pltpu.VMEM((8, D), F32)],
        compiler_params=_params("arbitrary"),
    )(*operands, dhr, hr, hr, xc, r, ig, xr, xr, conv_w, wa2, wx2, lam)


def _attn_bwd(sink_rows, q, kp, vp, bias_t, mask, do):
    t = q.shape[0]
    tp = kp.shape[0]

    def body(sink_ref, q_ref, kp_ref, vp_ref, bias_ref, mask_ref, do_ref, dq_ref, dk_ref, dv_ref, dbias_ref, ds_ref):
        c = pl.program_id(0)

        @pl.when(c == 0)
        def _():
            for ref in (dk_ref, dv_ref, dbias_ref, ds_ref):
                ref[...] = jnp.zeros_like(ref)

        st = pl.multiple_of(c * CHUNK, CHUNK)
        maskv = mask_ref[...]
        kbd = _block_diag(kp_ref[pl.ds(st, KP), :], maskv)
        vbd = _block_diag(vp_ref[pl.ds(st, KP), :], maskv)
        q_all = _stack_heads(q_ref[...])
        do_all = _stack_heads(do_ref[...])
        ps, sinks = _attn_softmax(q_all, kbd, bias_ref[...], sink_ref[...], c)
        dp = _nt(vbd, do_all)
        dscs = []
        for g in range(4):
            dpg = dp[g * KP:(g + 1) * KP, :]
            delta = jnp.sum(ps[g] * dpg, axis=0, keepdims=True)
            dscs.append(ps[g] * (dpg - delta))
            ds_ref[g:g + 1, :] += -(sinks[g] * delta)
        dsc = jnp.concatenate(dscs, axis=0)
        dbias_ref[...] += dsc
        dsb = (dsc * (HEAD_DIM ** -0.5)).astype(BF16)
        dq_ref[...] = _unstack_heads(_tn(dsb, kbd)).astype(BF16)

        lane_group = lax.broadcasted_iota(jnp.int32, (1, 4 * HEAD_DIM), 1) // HEAD_DIM

        def own_blocks(full):
            out = full[0:KP]
            for g in range(1, 4):
                out = jnp.where(lane_group == g, full[g * KP:(g + 1) * KP], out)
            return out

        dk_ref[pl.ds(st, KP), :] += own_blocks(_nn(dsb, q_all))
        dv_ref[pl.ds(st, KP), :] += own_blocks(_nn(jnp.concatenate(ps, axis=0).astype(BF16), do_all))

    full = lambda shape: pl.BlockSpec(shape, lambda i: tuple(0 for _ in shape))
    return pl.pallas_call(
        body, grid=(t // CHUNK,), name="attn_bwd",
        in_specs=[_WHOLE, _row_spec(CHUNK, D), _WHOLE, _WHOLE, _WHOLE, _WHOLE, _row_spec(CHUNK, D)],
        out_specs=[_row_spec(CHUNK, D), full((tp, KV_W)), full((tp, KV_W)), full((4 * KP, 4 * CHUNK)), full((8, 4 * CHUNK))],
        out_shape=[jax.ShapeDtypeStruct((t, D), BF16), jax.ShapeDtypeStruct((tp, KV_W), F32),
                   jax.ShapeDtypeStruct((tp, KV_W), F32), jax.ShapeDtypeStruct((4 * KP, 4 * CHUNK), F32),
                   jax.ShapeDtypeStruct((8, 4 * CHUNK), F32)],
        compiler_params=_params("arbitrary"),
    )(sink_rows, q, kp, vp, bias_t, mask, do)


def _mix_bwd2(dproj, dgate, h1, dh2, gmix, w_in_g, w_gate_g, after):
    t = h1.shape[0]
    tm = _tile(t)

    def body(dp_ref, dg_ref, h_ref, dh_ref, g_ref, win_ref, wg_ref, dh1_ref, dgm_ref):
        @pl.when(pl.program_id(0) == 0)
        def _():
            dgm_ref[...] = jnp.zeros_like(dgm_ref)

        du = jnp.zeros((tm, D), F32)
        for s in range(NSH):
            du = du + _nt(dp_ref[:, s * IN_S:(s + 1) * IN_S], win_ref[s])
            du = du + _nt(dg_ref[:, s * GATE_S:(s + 1) * GATE_S], wg_ref[s])
        dxn, dg = _rms_bwd(du, h_ref[...], g_ref[...])
        dgm_ref[...] += dg
        dh1_ref[...] = dh_ref[...] + dxn

    body, specs, operands = _behind(body, after)
    return pl.pallas_call(
        body, grid=(t // tm,), name="mix_bwd2",
        in_specs=specs + [_row_spec(tm, NSH * IN_S), _row_spec(tm, 2 * D), _row_spec(tm, D), _row_spec(tm, D), _vec_spec(D),
                          _WHOLE, _WHOLE],
        out_specs=[_row_spec(tm, D), _vec_spec(D)],
        out_shape=[jax.ShapeDtypeStruct((t, D), F32), jax.ShapeDtypeStruct((1, D), F32)],
        compiler_params=_params("arbitrary"),
    )(*operands, dproj, dgate, h1, dh2, gmix, w_in_g, w_gate_g)


def _band_onehot():
    nb = N_BUCKETS // 2
    max_exact = nb // 2
    rel = jnp.arange(KB)[None, :] - PAD_KEYS - jnp.arange(CHUNK)[:, None]
    ret = jnp.where(rel > 0, nb, 0)
    n = jnp.abs(rel)
    nf = jnp.maximum(n, 1).astype(jnp.float32)
    large = max_exact + (jnp.log(nf / max_exact) / math.log(128 / max_exact) * (nb - max_exact)).astype(jnp.int32)
    large = jnp.minimum(large, nb - 1)
    buckets = (ret + jnp.where(n < max_exact, n, large)).reshape(1, CHUNK * KB)
    return (buckets == jnp.arange(N_BUCKETS)[:, None]).astype(F32)


def _pair_blocks(w):
    z = jnp.zeros((8, 128, 128), w.dtype)
    return z.at[:, 0:64, 0:64].set(w[0::2]).at[:, 64:128, 64:128].set(w[1::2])


def _unpair_blocks(w2):
    return jnp.stack([w2[:, 0:64, 0:64], w2[:, 64:128, 64:128]], axis=1).reshape(16, 64, 64)


def _local_step(x, target, weights, sm, reducer):
    row = lambda v: v.reshape(1, -1)
    wg = dict(weights("ffn1", x))
    sm = dict(sm, conv_w=wg["conv_w"])
    onehot_t = _band_onehot()
    bias = _bias_fwd(sm["rel_bias"].T, onehot_t).reshape(4, 4, CHUNK, KB)
    bias_t = jnp.pad(jnp.transpose(bias, (0, 3, 1, 2)), ((0, 0), (0, KP - KB), (0, 0), (0, 0))).reshape(4 * KP, 4 * CHUNK)
    sink_rows = jnp.pad(jnp.repeat(sm["attn_sinks"].reshape(4, 4), CHUNK, axis=1), ((0, 4), (0, 0)))
    grp = jnp.arange(4 * KP)[:, None] // KP == jnp.arange(4 * HEAD_DIM)[None, :] // HEAD_DIM
    mask = (grp & (jnp.arange(4 * KP)[:, None] % KP < KB)).astype(BF16)
    wa2 = _pair_blocks(sm["rg_a_w"]).astype(BF16)
    wx2 = _pair_blocks(sm["rg_x_w"]).astype(BF16)

    h1, a1, b1, hm1, f1 = _ffn_fwd(x, row(sm["ffn1_pre_g"]), wg["ffn1_w1"], wg["ffn1_w3"], wg["ffn1_w2"],
                                   row(sm["ffn1_post_g"]), "ffn1_fwd")
    wg.update(weights("mix", h1))
    w_lru = wg["w_lru_out"].reshape(D, D)
    w_att = wg["w_attn_out"].reshape(D, D)
    w_o = wg["w_o"].reshape(D, D)
    u, q, k, v, xr, xg, gate = _mix_proj(h1, row(sm["mix_pre_g"]), wg["w_in"], wg["w_gate"], row(sm["b_gate"]))
    hr, yain, xc, r, ig = _rglru_fwd(xr, xg, sm["conv_w"], row(sm["conv_b"]), wa2, row(sm["rg_a_b"]), wx2,
                                     row(sm["rg_x_b"]), row(sm["lru_lambda"]))
    kp = jnp.pad(k, ((PAD_KEYS, KP - KB), (0, 0)))
    vp = jnp.pad(v, ((PAD_KEYS, KP - KB), (0, 0)))
    o = _attn_fwd(sink_rows, q, kp, vp, bias_t, mask)
    wg.update(weights("ffn2", o))
    h2, mo, merged, ya, yb = _merge_fwd(yain, o, gate, h1, w_lru, w_att, w_o, row(sm["mix_post_g"]))
    y, a2, b2, hm2, f2 = _ffn_fwd(h2, row(sm["ffn2_pre_g"]), wg["ffn2_w1"], wg["ffn2_w3"], wg["ffn2_w2"],
                                  row(sm["ffn2_post_g"]), "ffn2_fwd")
    dy, sq = _loss_dy(y, target)

    big, small = {}, {}
    dh2, n2, da2, db2, df2, small["ffn2_pre_g"], small["ffn2_post_g"] = _ffn_bwd(
        dy, h2, f2, a2, b2, row(sm["ffn2_pre_g"]), row(sm["ffn2_post_g"]), wg["ffn2_w1"], wg["ffn2_w3"], wg["ffn2_w2"],
        "ffn2_bwd")
    big["ffn2_w1"] = _wgrad_rows(da2, n2, "dw_ffn2_w1")
    big["ffn2_w3"] = _wgrad_rows(db2, n2, "dw_ffn2_w3")
    big["ffn2_w2"] = _wgrad_rows(hm2, df2, "dw_ffn2_w2")
    token = reducer.begin("ffn2", {n: big[n] for n in ("ffn2_w1", "ffn2_w3", "ffn2_w2")})
    dmo, dya, dyb, dgate, dhr, dxg, do, small["mix_post_g"], small["b_gate"] = _mix_bwd1(
        dh2, mo, row(sm["mix_post_g"]), gate, ya, yb, xg, hr, w_o, w_lru, w_att, token)
    big["w_o"] = _wgrad_sq(merged, dmo, "dw_w_o").reshape(NSH, D // NSH, D)
    big["w_lru_out"] = _wgrad_sq(yain, dya, "dw_w_lru_out").reshape(NSH, D // NSH, D)
    big["w_attn_out"] = _wgrad_sq(o, dyb, "dw_w_attn_out").reshape(NSH, D // NSH, D)
    token = reducer.advance("ffn2", big["w_attn_out"])
    (dxr, dwa2, dwx2, small["rg_a_b"], small["rg_x_b"], small["lru_lambda"], small["conv_w"], small["conv_b"]) = _rglru_bwd(
        dhr, hr, xc, r, ig, xr, sm["conv_w"], wa2, wx2, row(sm["lru_lambda"]), token)
    small["rg_a_w"] = _unpair_blocks(dwa2)
    small["rg_x_w"] = _unpair_blocks(dwx2)
    dq, dkp, dvp, dbias_t, ds_rows = _attn_bwd(sink_rows, q, kp, vp, bias_t, mask, do)
    dbias = jnp.transpose(dbias_t.reshape(4, KP, 4, CHUNK)[:, :KB], (0, 2, 3, 1)).reshape(N_HEADS, CHUNK * KB)
    drel_t, dsinks = _bias_bwd(dbias, onehot_t, ds_rows)
    small["attn_sinks"] = dsinks[0:4, 0:4].reshape(N_HEADS)
    small["rel_bias"] = drel_t.T
    t = x.shape[0]
    dproj = jnp.concatenate([dq, dkp[PAD_KEYS:PAD_KEYS + t].astype(BF16), dvp[PAD_KEYS:PAD_KEYS + t].astype(BF16), dxr, dxg],
                            axis=1)
    big["w_in"] = _wgrad_cols(u, dproj, IN_S, "dw_w_in")
    big["w_gate"] = _wgrad_cols(u, dgate, GATE_S, "dw_w_gate")
    token = reducer.begin("mix", {n: big[n] for n in ("w_in", "w_gate", "w_lru_out", "w_attn_out", "w_o")})
    dh1, small["mix_pre_g"] = _mix_bwd2(dproj, dgate, h1, dh2, row(sm["mix_pre_g"]), wg["w_in"], wg["w_gate"], token)
    dx, n1, da1, db1, df1, small["ffn1_pre_g"], small["ffn1_post_g"] = _ffn_bwd(
        dh1, x, f1, a1, b1, row(sm["ffn1_pre_g"]), row(sm["ffn1_post_g"]), wg["ffn1_w1"], wg["ffn1_w3"], wg["ffn1_w2"],
        "ffn1_bwd")
    token = reducer.advance("mix", dx)
    big["ffn1_w1"] = _wgrad_rows(da1, n1, "dw_ffn1_w1", token)
    big["ffn1_w3"] = _wgrad_rows(db1, n1, "dw_ffn1_w3", token)
    big["ffn1_w2"] = _wgrad_rows(hm1, df1, "dw_ffn1_w2", token)
    reducer.begin("ffn1", {n: big[n] for n in ("ffn1_w1", "ffn1_w3", "ffn1_w2")})
    return sq, dx, big, small


_ANY = pl.BlockSpec(memory_space=pl.ANY)


def _place():
    return lax.axis_index("x"), lax.axis_index("y"), lax.axis_index("c")


def _other_chips(x, y):
    return [(1 - x, y), (x, 1 - y), (1 - x, 1 - y)]


_HBM = pl.BlockSpec(memory_space=pltpu.HBM)
_SEM = pl.BlockSpec(memory_space=pltpu.SEMAPHORE)
_EFFECT = pltpu.SideEffectType.DATAFLOW_SIDE_EFFECTING


def _cast_into_slot(w, chip, name):
    r, cc = w.shape
    rows = r // 4

    def body(chip_ref, w_ref, o_ref):
        o_ref[...] = w_ref[...].astype(BF16)

    return pl.pallas_call(
        body, name=name, out_shape=jax.ShapeDtypeStruct((NSH, r, cc), BF16),
        grid_spec=pltpu.PrefetchScalarGridSpec(
            num_scalar_prefetch=1, grid=(4,), in_specs=[pl.BlockSpec((rows, cc), lambda i, chip: (i, 0))],
            out_specs=pl.BlockSpec((None, rows, cc), lambda i, chip: (chip[0], i, 0))),
        compiler_params=_params("arbitrary"))(chip, w)


def _piece(ref, slot, c):
    if ref.dtype == F32:
        return ref.at[slot]
    rh = ref.shape[1] // 2
    return ref.at[slot, pl.ds(pl.multiple_of(c * rh, 16), rh), :]


def _gather_start(stages, name):
    flat = [b for stage in stages for b in stage]
    n, ns = len(flat), len(stages)

    def body(*refs):
        ins, sems, token = refs[:n], refs[n:n + 2 * ns], refs[-1]
        x, y, c = _place()
        me = 2 * x + y
        k = 0
        for s, stage in enumerate(stages):
            for i in range(len(stage)):
                for j, (px, py) in enumerate(_other_chips(x, y)):
                    piece = _piece(ins[k], me, c)
                    pltpu.make_async_remote_copy(src_ref=piece, dst_ref=piece, send_sem=sems[2 * s].at[3 * i + j],
                                                 recv_sem=sems[2 * s + 1].at[3 * i + j], device_id=(px, py, c),
                                                 device_id_type=MESH).start()
                k += 1
        token[...] = jnp.zeros_like(token)

    sem_shapes = [pltpu.SemaphoreType.DMA((3 * len(stage),)) for stage in stages for _ in range(2)]
    outs = pl.pallas_call(
        body, name=name, in_specs=[_HBM] * n,
        out_specs=[_SEM] * (2 * ns) + [_HBM] * n + [pl.BlockSpec(memory_space=pltpu.VMEM)],
        out_shape=sem_shapes + [pltpu.HBM(b.shape, b.dtype) for b in flat] + [jax.ShapeDtypeStruct((8, 128), F32)],
        input_output_aliases={i: 2 * ns + i for i in range(n)},
        compiler_params=pltpu.CompilerParams(has_side_effects=_EFFECT),
    )(*[pltpu.with_memory_space_constraint(b, pltpu.HBM) for b in flat])
    sems, bufs, token = outs[:2 * ns], list(outs[2 * ns:2 * ns + n]), outs[-1]
    per_stage, k = [], 0
    for s, stage in enumerate(stages):
        per_stage.append((sems[2 * s], sems[2 * s + 1], bufs[k:k + len(stage)]))
        k += len(stage)
    return per_stage, token


def _gather_wait(send_sems, recv_sems, bufs, after, name):
    n = len(bufs)

    def body(*refs):
        ins, ssem, rsem = refs[:n], refs[n], refs[n + 1]
        x, y, c = _place()
        me = 2 * x + y
        for i in range(n):
            for j, (px, py) in enumerate(_other_chips(x, y)):
                cp = pltpu.make_async_remote_copy(src_ref=_piece(ins[i], me, c), dst_ref=_piece(ins[i], 2 * px + py, c),
                                                  send_sem=ssem.at[3 * i + j], recv_sem=rsem.at[3 * i + j],
                                                  device_id=(px, py, c), device_id_type=MESH)
                cp.wait_send()
                cp.wait_recv()

    return pl.pallas_call(
        body, name=name, in_specs=[_HBM] * n + [_SEM, _SEM, _ANY], out_specs=[_HBM] * n,
        out_shape=[pltpu.HBM(b.shape, b.dtype) for b in bufs], input_output_aliases={i: i for i in range(n)},
        compiler_params=pltpu.CompilerParams(has_side_effects=_EFFECT),
    )(*bufs, send_sems, recv_sems, after)


def _sibling_fill(bufs, name):
    n = len(bufs)

    def body(*refs):
        ins, outs = refs[:n], refs[n:2 * n]
        send_sems, recv_sems = refs[2 * n:]
        x, y, c = _place()
        copies = []
        for i in range(n):
            for j, (px, py) in enumerate(_other_chips(x, y)):
                copies.append(pltpu.make_async_remote_copy(
                    src_ref=_piece(ins[i], 2 * px + py, c), dst_ref=_piece(outs[i], 2 * px + py, c),
                    send_sem=send_sems.at[3 * i + j], recv_sem=recv_sems.at[3 * i + j], device_id=(x, y, 1 - c),
                    device_id_type=MESH))
                copies[-1].start()
        for cp in copies:
            cp.wait()

    return pl.pallas_call(
        body, name=name, in_specs=[_ANY] * n, out_specs=[_ANY] * n,
        out_shape=[jax.ShapeDtypeStruct(b.shape, b.dtype) for b in bufs], input_output_aliases={i: i for i in range(n)},
        scratch_shapes=[pltpu.SemaphoreType.DMA((3 * n,)), pltpu.SemaphoreType.DMA((3 * n,))],
        compiler_params=pltpu.CompilerParams(has_side_effects=True),
    )(*bufs)


def _swap_plan(srcs, lands):
    x, y, c = _place()
    plan = []
    for src, land in zip(srcs, lands):
        rh = src.shape[1] // 2
        plan.append((src.at[:, pl.ds(pl.multiple_of((1 - c) * rh, 8), rh), :], land, (x, y, 1 - c)))
    return plan


def _owners_plan(srcs, lands):
    x, y, c = _place()
    return [(src.at[2 * px + py], land.at[j], (px, py, c))
            for src, land in zip(srcs, lands) for j, (px, py) in enumerate(_other_chips(x, y))]


def _exchange_start(srcs, lands, plan, copies, name):
    n = len(srcs)

    def body(*refs):
        send_sems, recv_sems, token = refs[2 * n], refs[2 * n + 1], refs[-1]
        for k, (src, dst, dev) in enumerate(plan(refs[:n], refs[n:2 * n])):
            pltpu.make_async_remote_copy(src_ref=src, dst_ref=dst, send_sem=send_sems.at[k], recv_sem=recv_sems.at[k],
                                         device_id=dev, device_id_type=MESH).start()
        token[...] = jnp.zeros_like(token)

    both = list(srcs) + list(lands)
    outs = pl.pallas_call(
        body, name=name, in_specs=[_HBM] * (2 * n),
        out_specs=[_SEM, _SEM] + [_HBM] * (2 * n) + [pl.BlockSpec(memory_space=pltpu.VMEM)],
        out_shape=[pltpu.SemaphoreType.DMA((copies,)), pltpu.SemaphoreType.DMA((copies,))]
        + [pltpu.HBM(b.shape, b.dtype) for b in both] + [jax.ShapeDtypeStruct((8, 128), F32)],
        input_output_aliases={i: 2 + i for i in range(2 * n)},
        compiler_params=pltpu.CompilerParams(has_side_effects=_EFFECT),
    )(*[pltpu.with_memory_space_constraint(b, pltpu.HBM) for b in both])
    return (outs[0], outs[1]), list(outs[2:2 + n]), list(outs[2 + n:2 + 2 * n]), outs[-1]


def _exchange_wait(sems, srcs, lands, plan, after, name):
    n = len(srcs)

    def body(*refs):
        send_sems, recv_sems = refs[2 * n], refs[2 * n + 1]
        for k, (src, dst, dev) in enumerate(plan(refs[:n], refs[n:2 * n])):
            cp = pltpu.make_async_remote_copy(src_ref=src, dst_ref=dst, send_sem=send_sems.at[k], recv_sem=recv_sems.at[k],
                                              device_id=dev, device_id_type=MESH)
            cp.wait_send()
            cp.wait_recv()

    both = list(srcs) + list(lands)
    outs = pl.pallas_call(
        body, name=name, in_specs=[_HBM] * (2 * n) + [_SEM, _SEM, _ANY], out_specs=[_HBM] * (2 * n),
        out_shape=[pltpu.HBM(b.shape, b.dtype) for b in both], input_output_aliases={i: i for i in range(2 * n)},
        compiler_params=pltpu.CompilerParams(has_side_effects=_EFFECT),
    )(*both, sems[0], sems[1], after)
    return list(outs[:n]), list(outs[n:])


class _Reducer:
    def __init__(self):
        self.state = {}

    def begin(self, stage, grads):
        names = list(grads)
        full = [grads[n] for n in names]
        lands = [lax.empty((NSH, g.shape[1] // 2, g.shape[2]), F32) for g in full]
        sems, full, lands, token = _exchange_start(full, lands, _swap_plan, len(full), "swap_start_" + stage)
        self.state[stage] = (names, sems, full, lands)
        return token

    def advance(self, stage, after):
        names, sems, full, lands = self.state[stage]
        full, got = _exchange_wait(sems, full, lands, _swap_plan, after, "swap_wait_" + stage)
        sums = [_chip_sum(g, a, "chip_sum_" + n) for n, g, a in zip(names, full, got)]
        lands = [lax.empty((3,) + s[0].shape[1:], BF16) for s in sums]
        sems, sent, lands, token = _exchange_start([s[0] for s in sums], lands, _owners_plan, 3 * len(sums),
                                                   "owners_start_" + stage)
        self.state[stage] = (names, [s[1] for s in sums], sems, sent, lands)
        return token

    def finish(self, stage, after):
        names, own, sems, sent, lands = self.state[stage]
        _, got = _exchange_wait(sems, sent, lands, _owners_plan, after, "owners_wait_" + stage)
        return {n: _owner_sum(o, g, "owner_sum_" + n) for n, o, g in zip(names, own, got)}


def _chip_sum(g, got, name):
    _, r, cc = g.shape
    rh = r // 2

    def body(g_ref, got_ref, hb_ref, own_ref):
        x, y, c = _place()
        s = pl.program_id(0)
        h = g_ref[pl.ds(pl.multiple_of(c * rh, 8), rh), :] + got_ref[...]
        hb_ref[...] = h.astype(BF16)

        @pl.when(s == 2 * x + y)
        def _():
            own_ref[...] = h

    return pl.pallas_call(
        body, grid=(NSH,), name=name,
        in_specs=[pl.BlockSpec((None, r, cc), lambda s: (s, 0, 0)), pl.BlockSpec((None, rh, cc), lambda s: (s, 0, 0))],
        out_specs=[pl.BlockSpec((None, rh, cc), lambda s: (s, 0, 0)), pl.BlockSpec((rh, cc), lambda s: (0, 0))],
        out_shape=[jax.ShapeDtypeStruct((NSH, rh, cc), BF16), jax.ShapeDtypeStruct((rh, cc), F32)],
        compiler_params=_params("arbitrary"),
    )(g, got)


def _owner_sum(own, got, name):
    rh, cc = own.shape
    rows = rh // 2

    def body(own_ref, got_ref, o_ref):
        o_ref[...] = ((own_ref[...] + got_ref[0].astype(F32)) + got_ref[1].astype(F32)) + got_ref[2].astype(F32)

    return pl.pallas_call(
        body, grid=(2,), name=name,
        in_specs=[pl.BlockSpec((rows, cc), lambda i: (i, 0)), pl.BlockSpec((3, rows, cc), lambda i: (0, i, 0))],
        out_specs=pl.BlockSpec((rows, cc), lambda i: (i, 0)),
        out_shape=jax.ShapeDtypeStruct((rh, cc), F32), compiler_params=_params("arbitrary"),
    )(own, got)


def _send_halves(halves, name):
    n = len(halves)

    def body(*refs):
        ins, outs = refs[:n], refs[n:2 * n]
        send_sems, recv_sems = refs[2 * n:]
        x, y, c = _place()
        copies = [pltpu.make_async_remote_copy(src_ref=ins[w], dst_ref=outs[w], send_sem=send_sems.at[w], recv_sem=recv_sems.at[w],
                                               device_id=(x, y, 1 - c), device_id_type=MESH) for w in range(n)]
        for cp in copies:
            cp.start()
        for cp in copies:
            cp.wait()

    return pl.pallas_call(
        body, name=name, in_specs=[_ANY] * n, out_specs=[_ANY] * n,
        out_shape=[jax.ShapeDtypeStruct(h.shape, F32) for h in halves],
        scratch_shapes=[pltpu.SemaphoreType.DMA((n,)), pltpu.SemaphoreType.DMA((n,))],
        compiler_params=pltpu.CompilerParams(has_side_effects=True),
    )(*halves)


def _all_reduce_small(part):
    def body(p_ref, o_ref, rbuf, send1, recv1, send2, recv2):
        x, y, c = _place()
        me = 4 * x + 2 * y + c
        peers = []
        for k in range(1, 8):
            px, py, pc = x ^ ((k >> 2) & 1), y ^ ((k >> 1) & 1), c ^ (k & 1)
            peers.append((k, (px, py, pc), 4 * px + 2 * py + pc))

        def rows(d):
            return pl.ds(pl.multiple_of(d * SMALL_SLICE, 8), SMALL_SLICE)

        first = [pltpu.make_async_remote_copy(src_ref=p_ref.at[rows(idx), :], dst_ref=rbuf.at[me], send_sem=send1.at[k],
                                              recv_sem=recv1.at[k], device_id=dev, device_id_type=MESH)
                 for k, dev, idx in peers]
        for cp in first:
            cp.start()
        rbuf[me] = p_ref[rows(me), :]
        for k, dev, idx in peers:
            pltpu.make_async_remote_copy(src_ref=p_ref.at[rows(idx), :], dst_ref=rbuf.at[idx], send_sem=send1.at[k],
                                         recv_sem=recv1.at[k], device_id=dev, device_id_type=MESH).wait_recv()
        acc = rbuf[0]
        for d in range(1, 8):
            acc = acc + rbuf[d]
        o_ref[rows(me), :] = acc
        second = [pltpu.make_async_remote_copy(src_ref=o_ref.at[rows(me), :], dst_ref=o_ref.at[rows(me), :],
                                               send_sem=send2.at[k], recv_sem=recv2.at[k], device_id=dev, device_id_type=MESH)
                  for k, dev, idx in peers]
        for cp in second:
            cp.start()
        for k, dev, idx in peers:
            pltpu.make_async_remote_copy(src_ref=o_ref.at[rows(me), :], dst_ref=o_ref.at[rows(idx), :], send_sem=send2.at[k],
                                         recv_sem=recv2.at[k], device_id=dev, device_id_type=MESH).wait_recv()
        for cp in first + second:
            cp.wait_send()

    return pl.pallas_call(
        body, name="all_reduce_small", in_specs=[_WHOLE], out_specs=_WHOLE,
        out_shape=jax.ShapeDtypeStruct((SMALL_ROWS, 128), F32),
        scratch_shapes=[pltpu.VMEM((8, SMALL_SLICE, 128), F32)] + [pltpu.SemaphoreType.DMA((8,))] * 4,
        compiler_params=pltpu.CompilerParams(has_side_effects=True),
    )(part)


def _adamw_update(w, gv, m, v):
    nm = ADAM_B1 * m + (1.0 - ADAM_B1) * gv
    nv = ADAM_B2 * v + (1.0 - ADAM_B2) * (gv * gv)
    m_hat = nm / (1.0 - ADAM_B1 ** ADAM_STEP)
    v_hat = nv / (1.0 - ADAM_B2 ** ADAM_STEP)
    return -ADAM_LR * (m_hat / (jnp.sqrt(v_hat) + ADAM_EPS) + ADAM_WD * w), nm, nv


def _adamw(w, g, m, v, name):
    rows = w.shape[0] // 4

    def body(w_ref, g_ref, m_ref, v_ref, d_ref, nm_ref, nv_ref):
        d_ref[...], nm_ref[...], nv_ref[...] = _adamw_update(w_ref[...], g_ref[...], m_ref[...], v_ref[...])

    spec = pl.BlockSpec((rows, w.shape[1]), lambda i: (i, 0))
    out = jax.ShapeDtypeStruct(w.shape, F32)
    return pl.pallas_call(body, grid=(4,), in_specs=[spec] * 4, out_specs=[spec] * 3, out_shape=[out] * 3, name=name,
                          compiler_params=_params("arbitrary"))(w, g, m, v)


def _adamw_halves(w, mine, theirs, m, v, name):
    rh, cc = mine.shape
    rows = rh // 2

    def body(w_ref, mine_ref, theirs_ref, m_ref, v_ref, g_ref, d_ref, nm_ref, nv_ref):
        gv = jnp.where(pl.program_id(0) == lax.axis_index("c"), mine_ref[...], theirs_ref[...])
        g_ref[...] = gv
        d_ref[...], nm_ref[...], nv_ref[...] = _adamw_update(w_ref[...], gv, m_ref[...], v_ref[...])

    spec = pl.BlockSpec((rows, cc), lambda h, i: (2 * h + i, 0))
    half = pl.BlockSpec((rows, cc), lambda h, i: (i, 0))
    out = jax.ShapeDtypeStruct(w.shape, F32)
    return pl.pallas_call(body, grid=(2, 2), in_specs=[spec, half, half, spec, spec], out_specs=[spec] * 4,
                          out_shape=[out] * 4, name=name, compiler_params=_params("arbitrary", "arbitrary"))(w, mine, theirs, m, v)


SMALL_USED = sum(size for _, size in SMALL) // 128


def _pack_small(vals, tail=None):
    parts = []
    for name, size in SMALL:
        flat = vals[name].reshape(-1).astype(F32)
        parts.append(jnp.pad(flat, (0, size - flat.shape[0])))
    if tail is not None:
        parts.append(tail.reshape(128))
    flat = jnp.concatenate(parts)
    return jnp.pad(flat, (0, SMALL_ROWS * 128 - flat.shape[0])).reshape(SMALL_ROWS, 128)


def _unpack_small(packed, shapes):
    flat = packed.reshape(-1)
    out, off = {}, 0
    for name, size in SMALL:
        n = math.prod(shapes[name])
        out[name] = flat[off:off + n].reshape(shapes[name])
        off += size
    return out


def kernel(x, ffn1_pre_g, ffn1_w1, ffn1_w3, ffn1_w2, ffn1_post_g, mix_pre_g, w_in, conv_w, conv_b, rg_a_w, rg_a_b, rg_x_w, rg_x_b, lru_lambda, w_lru_out, attn_sinks, rel_bias, w_attn_out, w_gate, b_gate, w_o, mix_post_g, ffn2_pre_g, ffn2_w1, ffn2_w3, ffn2_w2, ffn2_post_g, loss_target, m_ffn1_pre_g, m_ffn1_w1, m_ffn1_w3, m_ffn1_w2, m_ffn1_post_g, m_mix_pre_g, m_w_in, m_conv_w, m_conv_b, m_rg_a_w, m_rg_a_b, m_rg_x_w, m_rg_x_b, m_lru_lambda, m_w_lru_out, m_attn_sinks, m_rel_bias, m_w_attn_out, m_w_gate, m_b_gate, m_w_o, m_mix_post_g, m_ffn2_pre_g, m_ffn2_w1, m_ffn2_w3, m_ffn2_w2, m_ffn2_post_g, v_ffn1_pre_g, v_ffn1_w1, v_ffn1_w3, v_ffn1_w2, v_ffn1_post_g, v_mix_pre_g, v_w_in, v_conv_w, v_conv_b, v_rg_a_w, v_rg_a_b, v_rg_x_w, v_rg_x_b, v_lru_lambda, v_w_lru_out, v_attn_sinks, v_rel_bias, v_w_attn_out, v_w_gate, v_b_gate, v_w_o, v_mix_post_g, v_ffn2_pre_g, v_ffn2_w1, v_ffn2_w3, v_ffn2_w2, v_ffn2_post_g):
    given = dict(locals())
    chip = 2 * lax.axis_index("x") + lax.axis_index("y")
    transposed = ("ffn1_w1", "ffn1_w3", "ffn2_w1", "ffn2_w3")

    def shard(name, moment=""):
        w = given[moment + name][0]
        return w.T if name in transposed else w

    def unshard(name, w):
        return (w.T if name in transposed else w)[None]

    def my_columns(a):
        own = lax.broadcasted_iota(jnp.int32, (1, 4, D), 2) // (D // NSH) == chip
        return jnp.where(own, jnp.tile(a, (1, 1, NSH)), 0.0)

    def only_my_columns(a):
        parts = a.reshape(1, 4, NSH, D // NSH)
        return sum(jnp.where(chip == s, parts[:, :, s], 0.0) for s in range(NSH))

    chip_arr = jnp.reshape(chip, (1,)).astype(jnp.int32)
    stage_names = {"ffn1": ["ffn1_w1", "ffn1_w3", "ffn1_w2", "conv_w"],
                   "mix": ["w_in", "w_gate", "w_lru_out", "w_attn_out", "w_o"],
                   "ffn2": ["ffn2_w1", "ffn2_w3", "ffn2_w2"]}
    in_flight = {}
    for stage, names in stage_names.items():
        bufs = [jnp.where(lax.broadcasted_iota(jnp.int32, (NSH, 4, D // NSH), 0) == chip, given[n], 0.0) if n == "conv_w"
                else _cast_into_slot(shard(n), chip_arr, "cast_" + n) for n in names]
        (in_flight[stage],), all_started = _gather_start([bufs], "gather_start_" + stage)

    def weights(stage, after):
        names = stage_names[stage]
        send_sems, recv_sems, landing = in_flight[stage]
        if stage == "ffn1":
            after = all_started
        landed = _gather_wait(send_sems, recv_sems, landing, after, "gather_wait_" + stage)
        halves = [b for b in landed if b.dtype == BF16]
        out = dict(zip([n for n, b in zip(names, landed) if b.dtype == BF16], _sibling_fill(halves, "sibling_fill_" + stage)))
        if "conv_w" in names:
            out["conv_w"] = jnp.transpose(landed[names.index("conv_w")], (1, 0, 2)).reshape(4, D)
        return out

    small_shapes = {n: given[n].shape for n, _ in SMALL}
    small_shapes["conv_w"] = (1, 4, D)
    sm = {n: (given[n][0] if given[n].shape[0] == 1 and n != "rel_bias" else given[n]) for n, _ in SMALL if n != "conv_w"}

    reducer = _Reducer()
    sq, dx, _, small = _local_step(x[0], loss_target[0], weights, sm, reducer)

    reduced_small = _all_reduce_small(_pack_small(small, tail=sq))
    loss = reduced_small[SMALL_USED, 0] * (0.5 / D)
    small_g = _unpack_small(reduced_small, small_shapes)
    reducer.advance("ffn1", reduced_small)
    grads, delta, new_m, new_v = {}, {}, {}, {}
    after = reduced_small
    for stage in ("ffn2", "mix", "ffn1"):
        halves = reducer.finish(stage, after)
        from_sibling = _send_halves(list(halves.values()), "send_halves_" + stage)
        for (n, mine), theirs in zip(halves.items(), from_sibling):
            grads[n], delta[n], new_m[n], new_v[n] = (unshard(n, r) for r in _adamw_halves(
                shard(n), mine, theirs, shard(n, "m_"), shard(n, "v_"), "adamw_" + n))
            after = new_v[n]

    packed = [_pack_small({n: (my_columns(given[pre + n]) if n == "conv_w" else given[pre + n]) for n, _ in SMALL})
              for pre in ("", "m_", "v_")]
    outs = _adamw(packed[0], reduced_small, packed[1], packed[2], "adamw_small")
    for dst, arr in zip((delta, new_m, new_v), outs):
        dst.update(_unpack_small(arr, small_shapes))
    small_out = dict(small_g)
    for d in (small_out, delta, new_m, new_v):
        d["conv_w"] = only_my_columns(d["conv_w"])
    grads.update(small_out)
    return (loss, dx[None], *[grads[n] for n in WEIGHTS], *[delta[n] for n in WEIGHTS], *[new_m[n] for n in WEIGHTS],
            *[new_v[n] for n in WEIGHTS])
```

```python
import functools
import math

import jax
import jax.numpy as jnp
from jax import lax
from jax.experimental import pallas as pl
from jax.experimental.pallas import tpu as pltpu

F32, BF16 = jnp.float32, jnp.bfloat16
D = 1024
NSH = 4
FF_S = 704
IN_S = 896
GATE_S = 512
KV_W = 256
CHUNK = 64
KB = 192
N_HEADS = 16
HEAD_DIM = 64
N_BUCKETS = 32
KP = 256
PAD_KEYS = 128
RMS_EPS = 1e-6
NEG_INF = -1e30
LRU_C = 8.0
TM = 256
VMEM_LIMIT = 56 * 1024 * 1024
ADAM_LR, ADAM_B1, ADAM_B2, ADAM_EPS, ADAM_WD, ADAM_STEP = 0.001, 0.9, 0.999, 1e-08, 0.01, 10
SMALL_ROWS = 1216
SMALL_SLICE = SMALL_ROWS // 8
MESH = pl.DeviceIdType.MESH

BIG = ["ffn1_w1", "ffn1_w3", "ffn1_w2", "w_in", "w_lru_out", "w_attn_out", "w_gate", "w_o", "ffn2_w1", "ffn2_w3", "ffn2_w2"]
SMALL = [("ffn1_pre_g", 1024), ("ffn1_post_g", 1024), ("mix_pre_g", 1024), ("conv_w", 4096), ("conv_b", 1024),
         ("rg_a_w", 65536), ("rg_a_b", 1024), ("rg_x_w", 65536), ("rg_x_b", 1024), ("lru_lambda", 1024),
         ("attn_sinks", 1024), ("rel_bias", 1024), ("b_gate", 2048), ("mix_post_g", 1024), ("ffn2_pre_g", 1024),
         ("ffn2_post_g", 1024)]
WEIGHTS = ["ffn1_pre_g", "ffn1_w1", "ffn1_w3", "ffn1_w2", "ffn1_post_g", "mix_pre_g", "w_in", "conv_w", "conv_b", "rg_a_w",
           "rg_a_b", "rg_x_w", "rg_x_b", "lru_lambda", "w_lru_out", "attn_sinks", "rel_bias", "w_attn_out", "w_gate", "b_gate",
           "w_o", "mix_post_g", "ffn2_pre_g", "ffn2_w1", "ffn2_w3", "ffn2_w2", "ffn2_post_g"]


def _params(*sem):
    return pltpu.CompilerParams(dimension_semantics=sem or None, vmem_limit_bytes=VMEM_LIMIT)


def _nn(a, b):
    return jnp.dot(a, b, preferred_element_type=F32)


def _nt(a, b):
    return lax.dot_general(a, b, (((1,), (1,)), ((), ())), preferred_element_type=F32)


def _tn(a, b):
    return lax.dot_general(a, b, (((0,), (0,)), ((), ())), preferred_element_type=F32)


def _rms(x, g):
    rstd = lax.rsqrt(jnp.mean(x * x, axis=-1, keepdims=True) + RMS_EPS)
    return (x * rstd) * g


def _rms_bwd(dout, x, g):
    rstd = lax.rsqrt(jnp.mean(x * x, axis=-1, keepdims=True) + RMS_EPS)
    xhat = x * rstd
    dg = jnp.sum(dout * xhat, axis=0, keepdims=True)
    dxhat = dout * g
    dx = rstd * (dxhat - xhat * jnp.mean(dxhat * xhat, axis=-1, keepdims=True))
    return dx, dg


_GELU_K = math.sqrt(2.0 / math.pi)


def _gelu(x):
    return x * (0.5 * (1.0 + jnp.tanh(_GELU_K * (x + 0.044715 * (x * x * x)))))


def _gelu_grad(x):
    t = jnp.tanh(_GELU_K * (x + 0.044715 * (x * x * x)))
    return 0.5 * (1.0 + t) + x * (0.5 * (1.0 - t * t) * (_GELU_K * (1.0 + 3.0 * 0.044715 * (x * x))))


def _softplus_neg(lam):
    z = -lam
    u = jnp.exp(-jnp.abs(z))
    w = 1.0 + u
    log1p_u = jnp.where(w == 1.0, u, jnp.log(w) * (u / (w - 1.0)))
    return jnp.maximum(z, 0.0) + log1p_u


def _lru_coeffs(r, sp):
    log_a = (-LRU_C * r) * sp
    a = jnp.exp(log_a)
    t = jnp.tanh(log_a)
    s = jnp.sqrt(-2.0 * t / (1.0 - t))
    return a, s


def _row_spec(tm, width):
    return pl.BlockSpec((tm, width), lambda i: (i, 0))


def _vec_spec(width):
    return pl.BlockSpec((1, width), lambda i: (0, 0))


_WHOLE = pl.BlockSpec(memory_space=pltpu.VMEM)


def _tile(t):
    return min(TM, t)


def _ffn_fwd(x, gpre, w1g, w3g, w2g, gpost, name, target=None):
    t = x.shape[0]
    tm = _tile(t)
    last = target is not None

    def body(x_ref, gpre_ref, w1_ref, w3_ref, w2_ref, gpost_ref, *refs):
        t_ref, (h_ref, a_ref, b_ref, hm_ref, f_ref), l_ref = (refs[0] if last else None), refs[last:last + 5], refs[-1]
        xv = x_ref[...]
        nb = _rms(xv, gpre_ref[...]).astype(BF16)
        f = jnp.zeros((tm, D), F32)
        for s in range(NSH):
            a = _nt(nb, w1_ref[s])
            b = _nt(nb, w3_ref[s])
            hmb = ((a * jax.nn.sigmoid(a)) * b).astype(BF16)
            a_ref[s] = a.astype(BF16)
            b_ref[s] = b.astype(BF16)
            hm_ref[s] = hmb
            f = f + _nn(hmb, w2_ref[s])
        f_ref[...] = f
        h = xv + 0.5 * _rms(f, gpost_ref[...])
        if last:
            @pl.when(pl.program_id(0) == 0)
            def _():
                l_ref[...] = jnp.zeros_like(l_ref)

            e = h - t_ref[...]
            h_ref[...] = e * (1.0 / D)
            l_ref[...] += jnp.sum(jnp.sum(e * e, axis=0, keepdims=True), axis=1, keepdims=True)
        else:
            h_ref[...] = h

    sh = pl.BlockSpec((NSH, tm, FF_S), lambda i: (0, i, 0))
    act = jax.ShapeDtypeStruct((NSH, t, FF_S), BF16)
    return pl.pallas_call(
        body, grid=(t // tm,), name=name,
        in_specs=[_row_spec(tm, D), _vec_spec(D), _WHOLE, _WHOLE, _WHOLE, _vec_spec(D)] + [_row_spec(tm, D)] * last,
        out_specs=[_row_spec(tm, D), sh, sh, sh, _row_spec(tm, D)] + [pl.BlockSpec((1, 128), lambda i: (0, 0))] * last,
        out_shape=[jax.ShapeDtypeStruct((t, D), F32), act, act, act, jax.ShapeDtypeStruct((t, D), F32)]
        + [jax.ShapeDtypeStruct((1, 128), F32)] * last,
        compiler_params=_params("arbitrary"),
    )(x, gpre, w1g, w3g, w2g, gpost, *([target] if last else []))


def _mix_proj(h1, gmix, w_in_g, w_gate_g, b_gate):
    t = h1.shape[0]
    tm = _tile(t)

    def body(h_ref, g_ref, win_ref, wg_ref, bg_ref, u_ref, q_ref, k_ref, v_ref, xr_ref, xg_ref, gate_ref):
        ub = _rms(h_ref[...], g_ref[...]).astype(BF16)
        u_ref[...] = ub
        p0 = _nn(ub, win_ref[0])
        q_ref[:, 0:896] = p0.astype(BF16)
        p1 = _nn(ub, win_ref[1])
        q_ref[:, 896:1024] = p1[:, 0:128].astype(BF16)
        k_ref[...] = p1[:, 128:384].astype(BF16)
        v_ref[...] = p1[:, 384:640].astype(BF16)
        xr_ref[:, 0:256] = p1[:, 640:896]
        p2 = _nn(ub, win_ref[2])
        xr_ref[:, 256:1024] = p2[:, 0:768]
        xg_ref[:, 0:128] = p2[:, 768:896]
        xg_ref[:, 128:1024] = _nn(ub, win_ref[3])
        for s in range(NSH):
            sl = slice(s * GATE_S, (s + 1) * GATE_S)
            gate_ref[:, sl] = jax.nn.sigmoid(_nn(ub, wg_ref[s]) + bg_ref[:, sl])

    return pl.pallas_call(
        body, grid=(t // tm,), name="mix_proj",
        in_specs=[_row_spec(tm, D), _vec_spec(D), _WHOLE, _WHOLE, _vec_spec(2 * D)],
        out_specs=[_row_spec(tm, D), _row_spec(tm, D), _row_spec(tm, KV_W), _row_spec(tm, KV_W), _row_spec(tm, D),
                   _row_spec(tm, D), _row_spec(tm, 2 * D)],
        out_shape=[jax.ShapeDtypeStruct((t, D), BF16), jax.ShapeDtypeStruct((t, D), BF16),
                   jax.ShapeDtypeStruct((t, KV_W), BF16), jax.ShapeDtypeStruct((t, KV_W), BF16),
                   jax.ShapeDtypeStruct((t, D), F32), jax.ShapeDtypeStruct((t, D), F32),
                   jax.ShapeDtypeStruct((t, 2 * D), F32)],
        compiler_params=_params("arbitrary"),
    )(h1, gmix, w_in_g, w_gate_g, b_gate)


def _rglru_fwd(xr, xg, conv_w, conv_b, wa2, ba, wx2, bx, lam):
    t = xr.shape[0]
    tm = _tile(t)
    nb8 = tm // 8

    def body(xr_ref, xrp_ref, xg_ref, cw_ref, cb_ref, wa_ref, ba_ref, wx_ref, bx_ref, lam_ref,
             hr_ref, yain_ref, xc_ref, r_ref, ig_ref, ext, a_sc, h_sc):
        i = pl.program_id(0)

        @pl.when(i == 0)
        def _():
            h_sc[...] = jnp.zeros_like(h_sc)

        ext[0:8, :] = jnp.where(i == 0, 0.0, xrp_ref[...])
        ext[8:8 + tm, :] = xr_ref[...]
        xc = jnp.broadcast_to(cb_ref[...], (tm, D))
        for tap in range(4):
            xc = xc + ext[pl.ds(5 + tap, tm), :] * cw_ref[tap:tap + 1, :]
        xc_ref[...] = xc
        xcb = xc.astype(BF16)
        for p in range(8):
            sl = slice(p * 128, (p + 1) * 128)
            r_ref[:, sl] = jax.nn.sigmoid(_nn(xcb[:, sl], wa_ref[p]) + ba_ref[:, sl])
            ig_ref[:, sl] = jax.nn.sigmoid(_nn(xcb[:, sl], wx_ref[p]) + bx_ref[:, sl])
        a, s = _lru_coeffs(r_ref[...], _softplus_neg(lam_ref[...]))
        a_sc[...] = a
        hr_ref[...] = s * (ig_ref[...] * xc)

        def blk(j, h):
            st = pl.multiple_of(j * 8, 8)
            a8 = a_sc[pl.ds(st, 8), :]
            u8 = hr_ref[pl.ds(st, 8), :]
            rows = []
            for k in range(8):
                h = a8[k:k + 1, :] * h + u8[k:k + 1, :]
                rows.append(h)
            hr_ref[pl.ds(st, 8), :] = jnp.concatenate(rows, axis=0)
            return h

        h_sc[0:1, :] = lax.fori_loop(0, nb8, blk, h_sc[0:1, :])
        yain_ref[...] = (hr_ref[...] * _gelu(xg_ref[...])).astype(BF16)

    prev = pl.BlockSpec((8, D), lambda i: (jnp.maximum(i * nb8 - 1, 0), 0))
    full = lambda shape: pl.BlockSpec(shape, lambda i: tuple(0 for _ in shape))
    f32 = jax.ShapeDtypeStruct((t, D), F32)
    return pl.pallas_call(
        body, grid=(t // tm,), name="rglru_fwd",
        in_specs=[_row_spec(tm, D), prev, _row_spec(tm, D), full((4, D)), _vec_spec(D), full((8, 128, 128)), _vec_spec(D),
                  full((8, 128, 128)), _vec_spec(D), _vec_spec(D)],
        out_specs=[_row_spec(tm, D)] * 5,
        out_shape=[f32, jax.ShapeDtypeStruct((t, D), BF16), f32, f32, f32],
        scratch_shapes=[pltpu.VMEM((tm + 8, D), F32), pltpu.VMEM((tm, D), F32), pltpu.VMEM((8, D), F32)],
        compiler_params=_params("arbitrary"),
    )(xr, xr, xg, conv_w, conv_b, wa2, ba, wx2, bx, lam)


def _bias_fwd(table_t, onehot_t):
    def body(t_ref, e_ref, o_ref):
        o_ref[...] = jnp.dot(t_ref[...], e_ref[...], preferred_element_type=F32, precision=lax.Precision.HIGHEST)

    return pl.pallas_call(body, out_shape=jax.ShapeDtypeStruct((N_HEADS, CHUNK * KB), F32), name="bias_fwd",
                          compiler_params=_params())(table_t, onehot_t)


def _bias_bwd(dbias_flat, onehot_t, ds_rows):
    def body(d_ref, e_ref, s_ref, o_ref, so_ref):
        o_ref[...] = lax.dot_general(d_ref[...], e_ref[...], (((1,), (1,)), ((), ())), preferred_element_type=F32,
                                     precision=lax.Precision.HIGHEST)
        so_ref[...] = jnp.zeros_like(so_ref)
        for r in range(4):
            so_ref[:, r:r + 1] = jnp.sum(s_ref[:, r * CHUNK:(r + 1) * CHUNK], axis=1, keepdims=True)

    return pl.pallas_call(body, out_shape=[jax.ShapeDtypeStruct((N_HEADS, N_BUCKETS), F32), jax.ShapeDtypeStruct((8, 128), F32)],
                          name="bias_bwd", compiler_params=_params())(dbias_flat, onehot_t, ds_rows)


def _stack_heads(q):
    return jnp.concatenate(
        [jnp.concatenate([q[:, (4 * g + r) * HEAD_DIM:(4 * g + r + 1) * HEAD_DIM] for g in range(4)], axis=1)
         for r in range(4)], axis=0)


def _unstack_heads(o):
    return jnp.concatenate([o[r * CHUNK:(r + 1) * CHUNK, g * HEAD_DIM:(g + 1) * HEAD_DIM] for g in range(4) for r in range(4)],
                           axis=1)


def _block_diag(w, mask):
    return jnp.concatenate([w] * 4, axis=0) * mask


def _attn_softmax(q_all, kbd, bias_t, sink_rows, c):
    s = _nt(kbd, q_all) * (HEAD_DIM ** -0.5) + bias_t
    j = lax.broadcasted_iota(jnp.int32, (4 * KP, 1), 0) % KP
    s = jnp.where((j < KB) & (j + c * CHUNK >= PAD_KEYS), s, NEG_INF)
    ps, sinks = [], []
    for g in range(4):
        sg = s[g * KP:(g + 1) * KP, :]
        sink = sink_rows[g:g + 1, :]
        m = jnp.maximum(jnp.max(sg, axis=0, keepdims=True), sink)
        e = jnp.exp(sg - m)
        es = jnp.exp(sink - m)
        inv = 1.0 / (jnp.sum(e, axis=0, keepdims=True) + es)
        ps.append(e * inv)
        sinks.append(es * inv)
    return ps, sinks


def _attn_fwd(sink_rows, q, kp, vp, bias_t, mask):
    t = q.shape[0]

    def body(sink_ref, q_ref, kp_ref, vp_ref, bias_ref, mask_ref, o_ref):
        c = pl.program_id(0)
        st = pl.multiple_of(c * CHUNK, CHUNK)
        kbd = _block_diag(kp_ref[pl.ds(st, KP), :], mask_ref[...])
        vbd = _block_diag(vp_ref[pl.ds(st, KP), :], mask_ref[...])
        ps, _ = _attn_softmax(_stack_heads(q_ref[...]), kbd, bias_ref[...], sink_ref[...], c)
        p_t = jnp.concatenate(ps, axis=0).astype(BF16)
        o_ref[...] = _unstack_heads(_tn(p_t, vbd)).astype(BF16)

    return pl.pallas_call(
        body, grid=(t // CHUNK,), name="attn_fwd",
        in_specs=[_WHOLE, _row_spec(CHUNK, D), _WHOLE, _WHOLE, _WHOLE, _WHOLE],
        out_specs=_row_spec(CHUNK, D),
        out_shape=jax.ShapeDtypeStruct((t, D), BF16),
        compiler_params=_params("arbitrary"),
    )(sink_rows, q, kp, vp, bias_t, mask)


def _merge_fwd(yain, o, gate, h1, w_lru, w_att, w_o, gpost):
    t = h1.shape[0]
    tm = _tile(t)

    def body(ya_ref, o_ref, g_ref, h_ref, wl_ref, wa_ref, wo_ref, gp_ref, h2_ref, mo_ref, mg_ref, ya_out, yb_out):
        ya = _nn(ya_ref[...], wl_ref[...])
        yb = _nn(o_ref[...], wa_ref[...])
        mg = (g_ref[:, 0:D] * ya + g_ref[:, D:2 * D] * yb).astype(BF16)
        mo = _nn(mg, wo_ref[...])
        ya_out[...] = ya.astype(BF16)
        yb_out[...] = yb.astype(BF16)
        mg_ref[...] = mg
        mo_ref[...] = mo
        h2_ref[...] = h_ref[...] + _rms(mo, gp_ref[...])

    f32 = jax.ShapeDtypeStruct((t, D), F32)
    b16 = jax.ShapeDtypeStruct((t, D), BF16)
    return pl.pallas_call(
        body, grid=(t // tm,), name="merge_fwd",
        in_specs=[_row_spec(tm, D), _row_spec(tm, D), _row_spec(tm, 2 * D), _row_spec(tm, D), _WHOLE, _WHOLE, _WHOLE,
                  _vec_spec(D)],
        out_specs=[_row_spec(tm, D)] * 5,
        out_shape=[f32, f32, b16, b16, b16],
        compiler_params=_params("arbitrary"),
    )(yain, o, gate, h1, w_lru, w_att, w_o, gpost)


def _ffn_bwd(dh, x, f, a, b, gpre, gpost, w1g, w3g, w2g, name):
    t = x.shape[0]
    tm = _tile(t)

    def body(dh_ref, x_ref, f_ref, a_ref, b_ref, gpre_ref, gpost_ref, w1_ref, w3_ref, w2_ref,
             dx_ref, n_ref, da_ref, db_ref, df_ref, dgpre_ref, dgpost_ref):
        @pl.when(pl.program_id(0) == 0)
        def _():
            dgpre_ref[...] = jnp.zeros_like(dgpre_ref)
            dgpost_ref[...] = jnp.zeros_like(dgpost_ref)

        dhv = dh_ref[...]
        xv = x_ref[...]
        df, dgp = _rms_bwd(0.5 * dhv, f_ref[...], gpost_ref[...])
        dgpost_ref[...] += dgp
        dfb = df.astype(BF16)
        df_ref[...] = dfb
        n_ref[...] = _rms(xv, gpre_ref[...]).astype(BF16)
        dn = jnp.zeros((tm, D), F32)
        for s in range(NSH):
            av = a_ref[s].astype(F32)
            bv = b_ref[s].astype(F32)
            sg = jax.nn.sigmoid(av)
            dhm = _nt(dfb, w2_ref[s])
            dab = (dhm * bv * (sg * (1.0 + av * (1.0 - sg)))).astype(BF16)
            dbb = (dhm * (av * sg)).astype(BF16)
            da_ref[s] = dab
            db_ref[s] = dbb
            dn = dn + _nn(dab, w1_ref[s]) + _nn(dbb, w3_ref[s])
        dxn, dg = _rms_bwd(dn, xv, gpre_ref[...])
        dgpre_ref[...] += dg
        dx_ref[...] = dhv + dxn

    sh = pl.BlockSpec((NSH, tm, FF_S), lambda i: (0, i, 0))
    act = jax.ShapeDtypeStruct((NSH, t, FF_S), BF16)
    vec = jax.ShapeDtypeStruct((1, D), F32)
    return pl.pallas_call(
        body, grid=(t // tm,), name=name,
        in_specs=[_row_spec(tm, D), _row_spec(tm, D), _row_spec(tm, D), sh, sh, _vec_spec(D), _vec_spec(D), _WHOLE, _WHOLE,
                  _WHOLE],
        out_specs=[_row_spec(tm, D), _row_spec(tm, D), sh, sh, _row_spec(tm, D), _vec_spec(D), _vec_spec(D)],
        out_shape=[jax.ShapeDtypeStruct((t, D), F32), jax.ShapeDtypeStruct((t, D), BF16), act, act,
                   jax.ShapeDtypeStruct((t, D), BF16), vec, vec],
        compiler_params=_params("arbitrary"),
    )(dh, x, f, a, b, gpre, gpost, w1g, w3g, w2g)


def _behind(body, after):
    if after is None:
        return body, [], []

    def ordered(_, *refs):
        body(*refs)

    return ordered, [_ANY], [after]


def _ffn_bwd_acts(dh, x, f, a, b, gpre, gpost, w2g, name):
    t = x.shape[0]
    tm = _tile(t)

    def body(dh_ref, x_ref, f_ref, a_ref, b_ref, gpre_ref, gpost_ref, w2_ref, n_ref, da_ref, db_ref, df_ref, dgpost_ref):
        @pl.when(pl.program_id(0) == 0)
        def _():
            dgpost_ref[...] = jnp.zeros_like(dgpost_ref)

        df, dgp = _rms_bwd(0.5 * dh_ref[...], f_ref[...], gpost_ref[...])
        dgpost_ref[...] += dgp
        dfb = df.astype(BF16)
        df_ref[...] = dfb
        n_ref[...] = _rms(x_ref[...], gpre_ref[...]).astype(BF16)
        for s in range(NSH):
            av = a_ref[s].astype(F32)
            bv = b_ref[s].astype(F32)
            sg = jax.nn.sigmoid(av)
            dhm = _nt(dfb, w2_ref[s])
            da_ref[s] = (dhm * bv * (sg * (1.0 + av * (1.0 - sg)))).astype(BF16)
            db_ref[s] = (dhm * (av * sg)).astype(BF16)

    sh = pl.BlockSpec((NSH, tm, FF_S), lambda i: (0, i, 0))
    act = jax.ShapeDtypeStruct((NSH, t, FF_S), BF16)
    b16 = jax.ShapeDtypeStruct((t, D), BF16)
    return pl.pallas_call(
        body, grid=(t // tm,), name=name,
        in_specs=[_row_spec(tm, D), _row_spec(tm, D), _row_spec(tm, D), sh, sh, _vec_spec(D), _vec_spec(D), _WHOLE],
        out_specs=[_row_spec(tm, D), sh, sh, _row_spec(tm, D), _vec_spec(D)],
        out_shape=[b16, act, act, b16, jax.ShapeDtypeStruct((1, D), F32)],
        compiler_params=_params("arbitrary"),
    )(dh, x, f, a, b, gpre, gpost, w2g)


def _ffn_bwd_input(dh, x, da, db, gpre, w1g, w3g, name, after):
    t = x.shape[0]
    tm = _tile(t)

    def body(dh_ref, x_ref, da_ref, db_ref, gpre_ref, w1_ref, w3_ref, dx_ref, dgpre_ref):
        @pl.when(pl.program_id(0) == 0)
        def _():
            dgpre_ref[...] = jnp.zeros_like(dgpre_ref)

        dn = jnp.zeros((tm, D), F32)
        for s in range(NSH):
            dn = dn + _nn(da_ref[s], w1_ref[s]) + _nn(db_ref[s], w3_ref[s])
        dxn, dg = _rms_bwd(dn, x_ref[...], gpre_ref[...])
        dgpre_ref[...] += dg
        dx_ref[...] = dh_ref[...] + dxn

    sh = pl.BlockSpec((NSH, tm, FF_S), lambda i: (0, i, 0))
    body, specs, operands = _behind(body, after)
    return pl.pallas_call(
        body, grid=(t // tm,), name=name,
        in_specs=specs + [_row_spec(tm, D), _row_spec(tm, D), sh, sh, _vec_spec(D), _WHOLE, _WHOLE],
        out_specs=[_row_spec(tm, D), _vec_spec(D)],
        out_shape=[jax.ShapeDtypeStruct((t, D), F32), jax.ShapeDtypeStruct((1, D), F32)],
        compiler_params=_params("arbitrary"),
    )(*operands, dh, x, da, db, gpre, w1g, w3g)


def _wgrad(a, b, a_spec, b_spec, out_spec, out_shape, grid, name, after=None):
    def body(a_ref, b_ref, o_ref):
        o_ref[...] = _tn(a_ref[...], b_ref[...])

    body, specs, operands = _behind(body, after)
    return pl.pallas_call(body, grid=grid, name=name, in_specs=specs + [a_spec, b_spec], out_specs=out_spec,
                          out_shape=jax.ShapeDtypeStruct(out_shape, F32),
                          compiler_params=_params(*("arbitrary",) * len(grid)))(*operands, a, b)


def _wgrad_cols(act, dsh, width, name, after=None):
    t = act.shape[0]
    if dsh.ndim == 3:
        b_spec = pl.BlockSpec((None, t, width), lambda s, k: (s, 0, 0))
    else:
        b_spec = pl.BlockSpec((t, width), lambda s, k: (0, s))
    return _wgrad(act, dsh, pl.BlockSpec((t, 512), lambda s, k: (0, k)), b_spec,
                  pl.BlockSpec((None, 512, width), lambda s, k: (s, k, 0)), (NSH, D, width), (NSH, 2), name, after)


def _wgrad_rows(hm, df, name, after=None):
    t = df.shape[0]
    return _wgrad(hm, df, pl.BlockSpec((None, t, FF_S), lambda s, j: (s, 0, 0)), pl.BlockSpec((t, 512), lambda s, j: (0, j)),
                  pl.BlockSpec((None, FF_S, 512), lambda s, j: (s, 0, j)), (NSH, FF_S, D), (NSH, 2), name, after)


def _wgrad_sq(a, b, name, after=None):
    t = a.shape[0]
    return _wgrad(a, b, pl.BlockSpec((t, 512), lambda i, j: (0, i)), pl.BlockSpec((t, 512), lambda i, j: (0, j)),
                  pl.BlockSpec((512, 512), lambda i, j: (i, j)), (D, D), (2, 2), name, after)


def _mix_bwd1(dh2, mo, gpost, gate, ya, yb, xg, hr, w_o, w_lru, w_att, after):
    t = dh2.shape[0]
    tm = _tile(t)

    def body(dh_ref, mo_ref, gp_ref, g_ref, ya_ref, yb_ref, xg_ref, hr_ref, wo_ref, wl_ref, wa_ref,
             dmo_ref, dya_ref, dyb_ref, dgate_ref, dhr_ref, dxg_ref, do_ref, dgp_ref, dbg_ref):
        @pl.when(pl.program_id(0) == 0)
        def _():
            dgp_ref[...] = jnp.zeros_like(dgp_ref)
            dbg_ref[...] = jnp.zeros_like(dbg_ref)

        dmo, dgp = _rms_bwd(dh_ref[...], mo_ref[...], gp_ref[...])
        dgp_ref[...] += dgp
        dmob = dmo.astype(BF16)
        dmo_ref[...] = dmob
        dm = _nt(dmob, wo_ref[...])
        g0 = g_ref[:, 0:D]
        g1 = g_ref[:, D:2 * D]
        dyab = (dm * g0).astype(BF16)
        dybb = (dm * g1).astype(BF16)
        dya_ref[...] = dyab
        dyb_ref[...] = dybb
        dg0 = dm * ya_ref[...].astype(F32) * (g0 * (1.0 - g0))
        dg1 = dm * yb_ref[...].astype(F32) * (g1 * (1.0 - g1))
        dgate_ref[:, 0:D] = dg0.astype(BF16)
        dgate_ref[:, D:2 * D] = dg1.astype(BF16)
        dbg_ref[:, 0:D] += jnp.sum(dg0, axis=0, keepdims=True)
        dbg_ref[:, D:2 * D] += jnp.sum(dg1, axis=0, keepdims=True)
        dyain = _nt(dyab, wl_ref[...])
        do_ref[...] = _nt(dybb, wa_ref[...]).astype(BF16)
        xgv = xg_ref[...]
        dhr_ref[...] = dyain * _gelu(xgv)
        dxg_ref[...] = (dyain * hr_ref[...] * _gelu_grad(xgv)).astype(BF16)

    b16 = jax.ShapeDtypeStruct((t, D), BF16)
    body, specs, operands = _behind(body, after)
    return pl.pallas_call(
        body, grid=(t // tm,), name="mix_bwd1",
        in_specs=specs + [_row_spec(tm, D), _row_spec(tm, D), _vec_spec(D), _row_spec(tm, 2 * D), _row_spec(tm, D),
                          _row_spec(tm, D), _row_spec(tm, D), _row_spec(tm, D), _WHOLE, _WHOLE, _WHOLE],
        out_specs=[_row_spec(tm, D), _row_spec(tm, D), _row_spec(tm, D), _row_spec(tm, 2 * D), _row_spec(tm, D),
                   _row_spec(tm, D), _row_spec(tm, D), _vec_spec(D), _vec_spec(2 * D)],
        out_shape=[b16, b16, b16, jax.ShapeDtypeStruct((t, 2 * D), BF16), jax.ShapeDtypeStruct((t, D), F32), b16, b16,
                   jax.ShapeDtypeStruct((1, D), F32), jax.ShapeDtypeStruct((1, 2 * D), F32)],
        compiler_params=_params("arbitrary"),
    )(*operands, dh2, mo, gpost, gate, ya, yb, xg, hr, w_o, w_lru, w_att)


def _rglru_bwd(dhr, hr, xc, r, ig, xr, conv_w, wa2, wx2, lam, after):
    t = dhr.shape[0]
    tm = _tile(t)
    nb8 = tm // 8
    nt = t // tm

    def body(dhr_ref, hr_ref, hrp_ref, xc_ref, r_ref, ig_ref, xr_ref, xrp_ref, cw_ref, wa_ref, wx_ref, lam_ref,
             dxr_ref, dwa_ref, dwx_ref, dba_ref, dbx_ref, dlam_ref, dcw_ref, dcb_ref,
             ext_h, ext_x, ext_d, a_sc, g_sc, c_sc, nxt_sc):
        i = pl.program_id(0)
        first_tile = i == nt - 1

        @pl.when(i == 0)
        def _():
            c_sc[...] = jnp.zeros_like(c_sc)
            nxt_sc[...] = jnp.zeros_like(nxt_sc)
            for ref in (dwa_ref, dwx_ref, dba_ref, dbx_ref, dlam_ref, dcw_ref, dcb_ref):
                ref[...] = jnp.zeros_like(ref)

        lamv = lam_ref[...]
        sp = _softplus_neg(lamv)
        rv = r_ref[...]
        igv = ig_ref[...]
        xcv = xc_ref[...]
        a, s = _lru_coeffs(rv, sp)
        a_sc[...] = a

        def blk(jj, c):
            st = pl.multiple_of((nb8 - 1 - jj) * 8, 8)
            d8 = dhr_ref[pl.ds(st, 8), :]
            a8 = a_sc[pl.ds(st, 8), :]
            rows = [None] * 8
            for k in range(7, -1, -1):
                g = d8[k:k + 1, :] + c
                c = a8[k:k + 1, :] * g
                rows[k] = g
            g_sc[pl.ds(st, 8), :] = jnp.concatenate(rows, axis=0)
            return c

        c_sc[0:1, :] = lax.fori_loop(0, nb8, blk, c_sc[0:1, :])
        g = g_sc[...]
        ext_h[0:8, :] = jnp.where(first_tile, 0.0, hrp_ref[...])
        ext_h[8:8 + tm, :] = hr_ref[...]
        hprev = ext_h[pl.ds(7, tm), :]
        d_s = g * (igv * xcv)
        dig = g * s * xcv
        dxc = g * s * igv
        dla = (g * hprev) * a - d_s * ((a * a) / s)
        dr_pre = (dla * (-LRU_C * sp)) * (rv * (1.0 - rv))
        di_pre = dig * (igv * (1.0 - igv))
        dlam_ref[...] += jnp.sum(dla * (LRU_C * rv), axis=0, keepdims=True) * jax.nn.sigmoid(-lamv)
        dba_ref[...] += jnp.sum(dr_pre, axis=0, keepdims=True)
        dbx_ref[...] += jnp.sum(di_pre, axis=0, keepdims=True)
        drb = dr_pre.astype(BF16)
        dib = di_pre.astype(BF16)
        xcb = xcv.astype(BF16)
        ext_d[tm:tm + 8, :] = nxt_sc[...]
        for p in range(8):
            sl = slice(p * 128, (p + 1) * 128)
            ext_d[0:tm, sl] = dxc[:, sl] + _nt(drb[:, sl], wa_ref[p]) + _nt(dib[:, sl], wx_ref[p])
            dwa_ref[p] += _tn(xcb[:, sl], drb[:, sl])
            dwx_ref[p] += _tn(xcb[:, sl], dib[:, sl])
        dxcv = ext_d[0:tm, :]
        nxt_sc[...] = ext_d[0:8, :]
        dcb_ref[...] += jnp.sum(dxcv, axis=0, keepdims=True)
        ext_x[0:8, :] = jnp.where(first_tile, 0.0, xrp_ref[...])
        ext_x[8:8 + tm, :] = xr_ref[...]
        dxr = jnp.zeros((tm, D), F32)
        for tap in range(4):
            dxr = dxr + ext_d[pl.ds(3 - tap, tm), :] * cw_ref[tap:tap + 1, :]
            dcw_ref[tap:tap + 1, :] += jnp.sum(dxcv * ext_x[pl.ds(5 + tap, tm), :], axis=0, keepdims=True)
        dxr_ref[...] = dxr.astype(BF16)

    rev = pl.BlockSpec((tm, D), lambda i: (nt - 1 - i, 0))
    prev = pl.BlockSpec((8, D), lambda i: (jnp.maximum((nt - 1 - i) * nb8 - 1, 0), 0))
    full = lambda shape: pl.BlockSpec(shape, lambda i: tuple(0 for _ in shape))
    vec = jax.ShapeDtypeStruct((1, D), F32)
    blocks = jax.ShapeDtypeStruct((8, 128, 128), F32)
    body, specs, operands = _behind(body, after)
    return pl.pallas_call(
        body, grid=(nt,), name="rglru_bwd",
        in_specs=specs + [rev, rev, prev, rev, rev, rev, rev, prev, full((4, D)), full((8, 128, 128)), full((8, 128, 128)),
                          _vec_spec(D)],
        out_specs=[rev, full((8, 128, 128)), full((8, 128, 128)), _vec_spec(D), _vec_spec(D), _vec_spec(D), full((4, D)),
                   _vec_spec(D)],
        out_shape=[jax.ShapeDtypeStruct((t, D), BF16), blocks, blocks, vec, vec, vec, jax.ShapeDtypeStruct((4, D), F32), vec],
        scratch_shapes=[pltpu.VMEM((tm + 8, D), F32), pltpu.VMEM((tm + 8, D), F32), pltpu.VMEM((tm + 8, D), F32),
                        pltpu.VMEM((tm, D), F32), pltpu.VMEM((tm, D), F32), pltpu.VMEM((8, D), F32), pltpu.VMEM((8, D), F32)],
        compiler_params=_params("arbitrary"),
    )(*operands, dhr, hr, hr, xc, r, ig, xr, xr, conv_w, wa2, wx2, lam)


def _attn_bwd(sink_rows, q, kp, vp, bias_t, mask, do):
    t = q.shape[0]
    tp = kp.shape[0]

    def body(sink_ref, q_ref, kp_ref, vp_ref, bias_ref, mask_ref, do_ref, dq_ref, dk_ref, dv_ref, dbias_ref, ds_ref):
        c = pl.program_id(0)

        @pl.when(c == 0)
        def _():
            for ref in (dk_ref, dv_ref, dbias_ref, ds_ref):
                ref[...] = jnp.zeros_like(ref)

        st = pl.multiple_of(c * CHUNK, CHUNK)
        maskv = mask_ref[...]
        kbd = _block_diag(kp_ref[pl.ds(st, KP), :], maskv)
        vbd = _block_diag(vp_ref[pl.ds(st, KP), :], maskv)
        q_all = _stack_heads(q_ref[...])
        do_all = _stack_heads(do_ref[...])
        ps, sinks = _attn_softmax(q_all, kbd, bias_ref[...], sink_ref[...], c)
        dp = _nt(vbd, do_all)
        dscs = []
        for g in range(4):
            dpg = dp[g * KP:(g + 1) * KP, :]
            delta = jnp.sum(ps[g] * dpg, axis=0, keepdims=True)
            dscs.append(ps[g] * (dpg - delta))
            ds_ref[g:g + 1, :] += -(sinks[g] * delta)
        dsc = jnp.concatenate(dscs, axis=0)
        dbias_ref[...] += dsc
        dsb = (dsc * (HEAD_DIM ** -0.5)).astype(BF16)
        dq_ref[...] = _unstack_heads(_tn(dsb, kbd)).astype(BF16)

        lane_group = lax.broadcasted_iota(jnp.int32, (1, 4 * HEAD_DIM), 1) // HEAD_DIM

        def own_blocks(full):
            out = full[0:KP]
            for g in range(1, 4):
                out = jnp.where(lane_group == g, full[g * KP:(g + 1) * KP], out)
            return out

        dk_ref[pl.ds(st, KP), :] += own_blocks(_nn(dsb, q_all))
        dv_ref[pl.ds(st, KP), :] += own_blocks(_nn(jnp.concatenate(ps, axis=0).astype(BF16), do_all))

    full = lambda shape: pl.BlockSpec(shape, lambda i: tuple(0 for _ in shape))
    return pl.pallas_call(
        body, grid=(t // CHUNK,), name="attn_bwd",
        in_specs=[_WHOLE, _row_spec(CHUNK, D), _WHOLE, _WHOLE, _WHOLE, _WHOLE, _row_spec(CHUNK, D)],
        out_specs=[_row_spec(CHUNK, D), full((tp, KV_W)), full((tp, KV_W)), full((4 * KP, 4 * CHUNK)), full((8, 4 * CHUNK))],
        out_shape=[jax.ShapeDtypeStruct((t, D), BF16), jax.ShapeDtypeStruct((tp, KV_W), F32),
                   jax.ShapeDtypeStruct((tp, KV_W), F32), jax.ShapeDtypeStruct((4 * KP, 4 * CHUNK), F32),
                   jax.ShapeDtypeStruct((8, 4 * CHUNK), F32)],
        compiler_params=_params("arbitrary"),
    )(sink_rows, q, kp, vp, bias_t, mask, do)


def _mix_bwd2(dproj, dgate, h1, dh2, gmix, w_in_g, w_gate_g, after):
    t = h1.shape[0]
    tm = _tile(t)

    def body(dp_ref, dg_ref, h_ref, dh_ref, g_ref, win_ref, wg_ref, dh1_ref, dgm_ref):
        @pl.when(pl.program_id(0) == 0)
        def _():
            dgm_ref[...] = jnp.zeros_like(dgm_ref)

        du = jnp.zeros((tm, D), F32)
        for s in range(NSH):
            du = du + _nt(dp_ref[:, s * IN_S:(s + 1) * IN_S], win_ref[s])
            du = du + _nt(dg_ref[:, s * GATE_S:(s + 1) * GATE_S], wg_ref[s])
        dxn, dg = _rms_bwd(du, h_ref[...], g_ref[...])
        dgm_ref[...] += dg
        dh1_ref[...] = dh_ref[...] + dxn

    body, specs, operands = _behind(body, after)
    return pl.pallas_call(
        body, grid=(t // tm,), name="mix_bwd2",
        in_specs=specs + [_row_spec(tm, NSH * IN_S), _row_spec(tm, 2 * D), _row_spec(tm, D), _row_spec(tm, D), _vec_spec(D),
                          _WHOLE, _WHOLE],
        out_specs=[_row_spec(tm, D), _vec_spec(D)],
        out_shape=[jax.ShapeDtypeStruct((t, D), F32), jax.ShapeDtypeStruct((1, D), F32)],
        compiler_params=_params("arbitrary"),
    )(*operands, dproj, dgate, h1, dh2, gmix, w_in_g, w_gate_g)


def _band_onehot():
    nb = N_BUCKETS // 2
    max_exact = nb // 2
    rel = jnp.arange(KB)[None, :] - PAD_KEYS - jnp.arange(CHUNK)[:, None]
    ret = jnp.where(rel > 0, nb, 0)
    n = jnp.abs(rel)
    nf = jnp.maximum(n, 1).astype(jnp.float32)
    large = max_exact + (jnp.log(nf / max_exact) / math.log(128 / max_exact) * (nb - max_exact)).astype(jnp.int32)
    large = jnp.minimum(large, nb - 1)
    buckets = (ret + jnp.where(n < max_exact, n, large)).reshape(1, CHUNK * KB)
    return (buckets == jnp.arange(N_BUCKETS)[:, None]).astype(F32)


def _pair_blocks(w):
    z = jnp.zeros((8, 128, 128), w.dtype)
    return z.at[:, 0:64, 0:64].set(w[0::2]).at[:, 64:128, 64:128].set(w[1::2])


def _unpair_blocks(w2):
    return jnp.stack([w2[:, 0:64, 0:64], w2[:, 64:128, 64:128]], axis=1).reshape(16, 64, 64)


def _local_step(x, target, weights, sm, reducer):
    row = lambda v: v.reshape(1, -1)
    wg = dict(weights("ffn1", x))
    sm = dict(sm, conv_w=wg["conv_w"])
    onehot_t = _band_onehot()
    bias = _bias_fwd(sm["rel_bias"].T, onehot_t).reshape(4, 4, CHUNK, KB)
    bias_t = jnp.pad(jnp.transpose(bias, (0, 3, 1, 2)), ((0, 0), (0, KP - KB), (0, 0), (0, 0))).reshape(4 * KP, 4 * CHUNK)
    sink_rows = jnp.pad(jnp.repeat(sm["attn_sinks"].reshape(4, 4), CHUNK, axis=1), ((0, 4), (0, 0)))
    grp = jnp.arange(4 * KP)[:, None] // KP == jnp.arange(4 * HEAD_DIM)[None, :] // HEAD_DIM
    mask = (grp & (jnp.arange(4 * KP)[:, None] % KP < KB)).astype(BF16)
    wa2 = _pair_blocks(sm["rg_a_w"]).astype(BF16)
    wx2 = _pair_blocks(sm["rg_x_w"]).astype(BF16)

    h1, a1, b1, hm1, f1 = _ffn_fwd(x, row(sm["ffn1_pre_g"]), wg["ffn1_w1"], wg["ffn1_w3"], wg["ffn1_w2"],
                                   row(sm["ffn1_post_g"]), "ffn1_fwd")
    wg.update(weights("mix", h1))
    w_lru = wg["w_lru_out"].reshape(D, D)
    w_att = wg["w_attn_out"].reshape(D, D)
    w_o = wg["w_o"].reshape(D, D)
    u, q, k, v, xr, xg, gate = _mix_proj(h1, row(sm["mix_pre_g"]), wg["w_in"], wg["w_gate"], row(sm["b_gate"]))
    hr, yain, xc, r, ig = _rglru_fwd(xr, xg, sm["conv_w"], row(sm["conv_b"]), wa2, row(sm["rg_a_b"]), wx2,
                                     row(sm["rg_x_b"]), row(sm["lru_lambda"]))
    kp = jnp.pad(k, ((PAD_KEYS, KP - KB), (0, 0)))
    vp = jnp.pad(v, ((PAD_KEYS, KP - KB), (0, 0)))
    o = _attn_fwd(sink_rows, q, kp, vp, bias_t, mask)
    wg.update(weights("ffn2", o))
    h2, mo, merged, ya, yb = _merge_fwd(yain, o, gate, h1, w_lru, w_att, w_o, row(sm["mix_post_g"]))
    dy, a2, b2, hm2, f2, sq = _ffn_fwd(h2, row(sm["ffn2_pre_g"]), wg["ffn2_w1"], wg["ffn2_w3"], wg["ffn2_w2"],
                                       row(sm["ffn2_post_g"]), "ffn2_fwd", target)

    big, small = {}, {}
    dh2, n2, da2, db2, df2, small["ffn2_pre_g"], small["ffn2_post_g"] = _ffn_bwd(
        dy, h2, f2, a2, b2, row(sm["ffn2_pre_g"]), row(sm["ffn2_post_g"]), wg["ffn2_w1"], wg["ffn2_w3"], wg["ffn2_w2"],
        "ffn2_bwd")
    big["ffn2_w1"] = _wgrad_rows(da2, n2, "dw_ffn2_w1")
    big["ffn2_w3"] = _wgrad_rows(db2, n2, "dw_ffn2_w3")
    big["ffn2_w2"] = _wgrad_rows(hm2, df2, "dw_ffn2_w2")
    token = reducer.begin("ffn2", {n: big[n] for n in ("ffn2_w1", "ffn2_w3", "ffn2_w2")})
    dmo, dya, dyb, dgate, dhr, dxg, do, small["mix_post_g"], small["b_gate"] = _mix_bwd1(
        dh2, mo, row(sm["mix_post_g"]), gate, ya, yb, xg, hr, w_o, w_lru, w_att, token)
    big["w_o"] = _wgrad_sq(merged, dmo, "dw_w_o").reshape(NSH, D // NSH, D)
    big["w_lru_out"] = _wgrad_sq(yain, dya, "dw_w_lru_out").reshape(NSH, D // NSH, D)
    big["w_attn_out"] = _wgrad_sq(o, dyb, "dw_w_attn_out").reshape(NSH, D // NSH, D)
    token = reducer.advance("ffn2", big["w_attn_out"])
    (dxr, dwa2, dwx2, small["rg_a_b"], small["rg_x_b"], small["lru_lambda"], small["conv_w"], small["conv_b"]) = _rglru_bwd(
        dhr, hr, xc, r, ig, xr, sm["conv_w"], wa2, wx2, row(sm["lru_lambda"]), token)
    small["rg_a_w"] = _unpair_blocks(dwa2)
    small["rg_x_w"] = _unpair_blocks(dwx2)
    dq, dkp, dvp, dbias_t, ds_rows = _attn_bwd(sink_rows, q, kp, vp, bias_t, mask, do)
    dbias = jnp.transpose(dbias_t.reshape(4, KP, 4, CHUNK)[:, :KB], (0, 2, 3, 1)).reshape(N_HEADS, CHUNK * KB)
    drel_t, dsinks = _bias_bwd(dbias, onehot_t, ds_rows)
    small["attn_sinks"] = dsinks[0:4, 0:4].reshape(N_HEADS)
    small["rel_bias"] = drel_t.T
    t = x.shape[0]
    dproj = jnp.concatenate([dq, dkp[PAD_KEYS:PAD_KEYS + t].astype(BF16), dvp[PAD_KEYS:PAD_KEYS + t].astype(BF16), dxr, dxg],
                            axis=1)
    big["w_in"] = _wgrad_cols(u, dproj, IN_S, "dw_w_in")
    big["w_gate"] = _wgrad_cols(u, dgate, GATE_S, "dw_w_gate")
    token = reducer.begin("mix", {n: big[n] for n in ("w_in", "w_gate", "w_lru_out", "w_attn_out", "w_o")})
    dh1, small["mix_pre_g"] = _mix_bwd2(dproj, dgate, h1, dh2, row(sm["mix_pre_g"]), wg["w_in"], wg["w_gate"], token)
    n1, da1, db1, df1, small["ffn1_post_g"] = _ffn_bwd_acts(
        dh1, x, f1, a1, b1, row(sm["ffn1_pre_g"]), row(sm["ffn1_post_g"]), wg["ffn1_w2"], "ffn1_bwd_acts")
    token = reducer.advance("mix", df1)
    big["ffn1_w1"] = _wgrad_rows(da1, n1, "dw_ffn1_w1", token)
    big["ffn1_w3"] = _wgrad_rows(db1, n1, "dw_ffn1_w3", token)
    big["ffn1_w2"] = _wgrad_rows(hm1, df1, "dw_ffn1_w2", token)
    token = reducer.begin("ffn1", {n: big[n] for n in ("ffn1_w1", "ffn1_w3", "ffn1_w2")})
    dx, small["ffn1_pre_g"] = _ffn_bwd_input(dh1, x, da1, db1, row(sm["ffn1_pre_g"]), wg["ffn1_w1"], wg["ffn1_w3"],
                                             "ffn1_bwd_input", token)
    return sq, dx, big, small


_ANY = pl.BlockSpec(memory_space=pl.ANY)


def _place():
    return lax.axis_index("x"), lax.axis_index("y"), lax.axis_index("c")


def _other_chips(x, y):
    return [(1 - x, y), (x, 1 - y), (1 - x, 1 - y)]


_HBM = pl.BlockSpec(memory_space=pltpu.HBM)
_SEM = pl.BlockSpec(memory_space=pltpu.SEMAPHORE)
_EFFECT = pltpu.SideEffectType.DATAFLOW_SIDE_EFFECTING


def _cast_into_slot(w, chip, name, after=None):
    r, cc = w.shape
    rows = r // 4

    def body(chip_ref, *refs):
        w_ref, o_ref = refs[-2:]
        o_ref[...] = w_ref[...].astype(BF16)

    extra = [] if after is None else [after]
    return pl.pallas_call(
        body, name=name, out_shape=jax.ShapeDtypeStruct((NSH, r, cc), BF16),
        grid_spec=pltpu.PrefetchScalarGridSpec(
            num_scalar_prefetch=1, grid=(4,), in_specs=[_ANY] * len(extra) + [pl.BlockSpec((rows, cc), lambda i, chip: (i, 0))],
            out_specs=pl.BlockSpec((None, rows, cc), lambda i, chip: (chip[0], i, 0))),
        compiler_params=_params("arbitrary"))(chip, *extra, w)


def _piece(ref, slot, c):
    if ref.dtype == F32:
        return ref.at[slot]
    rh = ref.shape[1] // 2
    return ref.at[slot, pl.ds(pl.multiple_of(c * rh, 16), rh), :]


def _gather_start(stages, name):
    flat = [b for stage in stages for b in stage]
    n, ns = len(flat), len(stages)

    def body(*refs):
        ins, sems, token = refs[:n], refs[n:n + 2 * ns], refs[-1]
        x, y, c = _place()
        me = 2 * x + y
        k = 0
        for s, stage in enumerate(stages):
            for i in range(len(stage)):
                for j, (px, py) in enumerate(_other_chips(x, y)):
                    piece = _piece(ins[k], me, c)
                    pltpu.make_async_remote_copy(src_ref=piece, dst_ref=piece, send_sem=sems[2 * s].at[3 * i + j],
                                                 recv_sem=sems[2 * s + 1].at[3 * i + j], device_id=(px, py, c),
                                                 device_id_type=MESH).start()
                k += 1
        token[...] = jnp.zeros_like(token)

    sem_shapes = [pltpu.SemaphoreType.DMA((3 * len(stage),)) for stage in stages for _ in range(2)]
    outs = pl.pallas_call(
        body, name=name, in_specs=[_HBM] * n,
        out_specs=[_SEM] * (2 * ns) + [_HBM] * n + [pl.BlockSpec(memory_space=pltpu.VMEM)],
        out_shape=sem_shapes + [pltpu.HBM(b.shape, b.dtype) for b in flat] + [jax.ShapeDtypeStruct((8, 128), F32)],
        input_output_aliases={i: 2 * ns + i for i in range(n)},
        compiler_params=pltpu.CompilerParams(has_side_effects=_EFFECT),
    )(*[pltpu.with_memory_space_constraint(b, pltpu.HBM) for b in flat])
    sems, bufs, token = outs[:2 * ns], list(outs[2 * ns:2 * ns + n]), outs[-1]
    per_stage, k = [], 0
    for s, stage in enumerate(stages):
        per_stage.append((sems[2 * s], sems[2 * s + 1], bufs[k:k + len(stage)]))
        k += len(stage)
    return per_stage, token


def _gather_wait(send_sems, recv_sems, bufs, after, name):
    n = len(bufs)

    def body(*refs):
        ins, ssem, rsem = refs[:n], refs[n], refs[n + 1]
        x, y, c = _place()
        me = 2 * x + y
        for i in range(n):
            for j, (px, py) in enumerate(_other_chips(x, y)):
                cp = pltpu.make_async_remote_copy(src_ref=_piece(ins[i], me, c), dst_ref=_piece(ins[i], 2 * px + py, c),
                                                  send_sem=ssem.at[3 * i + j], recv_sem=rsem.at[3 * i + j],
                                                  device_id=(px, py, c), device_id_type=MESH)
                cp.wait_send()
                cp.wait_recv()

    return pl.pallas_call(
        body, name=name, in_specs=[_HBM] * n + [_SEM, _SEM, _ANY], out_specs=[_HBM] * n,
        out_shape=[pltpu.HBM(b.shape, b.dtype) for b in bufs], input_output_aliases={i: i for i in range(n)},
        compiler_params=pltpu.CompilerParams(has_side_effects=_EFFECT),
    )(*bufs, send_sems, recv_sems, after)


def _sibling_fill(bufs, name):
    n = len(bufs)

    def body(*refs):
        ins, outs = refs[:n], refs[n:2 * n]
        send_sems, recv_sems = refs[2 * n:]
        x, y, c = _place()
        copies = []
        for i in range(n):
            for j, (px, py) in enumerate(_other_chips(x, y)):
                copies.append(pltpu.make_async_remote_copy(
                    src_ref=_piece(ins[i], 2 * px + py, c), dst_ref=_piece(outs[i], 2 * px + py, c),
                    send_sem=send_sems.at[3 * i + j], recv_sem=recv_sems.at[3 * i + j], device_id=(x, y, 1 - c),
                    device_id_type=MESH))
                copies[-1].start()
        for cp in copies:
            cp.wait()

    return pl.pallas_call(
        body, name=name, in_specs=[_ANY] * n, out_specs=[_ANY] * n,
        out_shape=[jax.ShapeDtypeStruct(b.shape, b.dtype) for b in bufs], input_output_aliases={i: i for i in range(n)},
        scratch_shapes=[pltpu.SemaphoreType.DMA((3 * n,)), pltpu.SemaphoreType.DMA((3 * n,))],
        compiler_params=pltpu.CompilerParams(has_side_effects=True),
    )(*bufs)


def _swap_plan(srcs, lands):
    x, y, c = _place()
    plan = []
    for src, land in zip(srcs, lands):
        rh = src.shape[1] // 2
        plan.append((src.at[:, pl.ds(pl.multiple_of((1 - c) * rh, 8), rh), :], land, (x, y, 1 - c)))
    return plan


def _owners_plan(srcs, lands):
    x, y, c = _place()
    return [(src.at[2 * px + py], land.at[j], (px, py, c))
            for src, land in zip(srcs, lands) for j, (px, py) in enumerate(_other_chips(x, y))]


def _exchange_start(srcs, lands, plan, copies, name):
    n = len(srcs)

    def body(*refs):
        send_sems, recv_sems, token = refs[2 * n], refs[2 * n + 1], refs[-1]
        for k, (src, dst, dev) in enumerate(plan(refs[:n], refs[n:2 * n])):
            pltpu.make_async_remote_copy(src_ref=src, dst_ref=dst, send_sem=send_sems.at[k], recv_sem=recv_sems.at[k],
                                         device_id=dev, device_id_type=MESH).start()
        token[...] = jnp.zeros_like(token)

    both = list(srcs) + list(lands)
    outs = pl.pallas_call(
        body, name=name, in_specs=[_HBM] * (2 * n),
        out_specs=[_SEM, _SEM] + [_HBM] * (2 * n) + [pl.BlockSpec(memory_space=pltpu.VMEM)],
        out_shape=[pltpu.SemaphoreType.DMA((copies,)), pltpu.SemaphoreType.DMA((copies,))]
        + [pltpu.HBM(b.shape, b.dtype) for b in both] + [jax.ShapeDtypeStruct((8, 128), F32)],
        input_output_aliases={i: 2 + i for i in range(2 * n)},
        compiler_params=pltpu.CompilerParams(has_side_effects=_EFFECT),
    )(*[pltpu.with_memory_space_constraint(b, pltpu.HBM) for b in both])
    return (outs[0], outs[1]), list(outs[2:2 + n]), list(outs[2 + n:2 + 2 * n]), outs[-1]


def _exchange_wait(sems, srcs, lands, plan, after, name):
    n = len(srcs)

    def body(*refs):
        send_sems, recv_sems = refs[2 * n], refs[2 * n + 1]
        for k, (src, dst, dev) in enumerate(plan(refs[:n], refs[n:2 * n])):
            cp = pltpu.make_async_remote_copy(src_ref=src, dst_ref=dst, send_sem=send_sems.at[k], recv_sem=recv_sems.at[k],
                                              device_id=dev, device_id_type=MESH)
            cp.wait_send()
            cp.wait_recv()

    both = list(srcs) + list(lands)
    afters = list(after) if isinstance(after, (list, tuple)) else [after]
    outs = pl.pallas_call(
        body, name=name, in_specs=[_HBM] * (2 * n) + [_SEM, _SEM] + [_ANY] * len(afters), out_specs=[_HBM] * (2 * n),
        out_shape=[pltpu.HBM(b.shape, b.dtype) for b in both], input_output_aliases={i: i for i in range(2 * n)},
        compiler_params=pltpu.CompilerParams(has_side_effects=_EFFECT),
    )(*both, sems[0], sems[1], *afters)
    return list(outs[:n]), list(outs[n:])


class _Reducer:
    def __init__(self):
        self.state = {}

    def begin(self, stage, grads):
        names = list(grads)
        full = [grads[n] for n in names]
        lands = [lax.empty((NSH, g.shape[1] // 2, g.shape[2]), F32) for g in full]
        sems, full, lands, token = _exchange_start(full, lands, _swap_plan, len(full), "swap_start_" + stage)
        self.state[stage] = (names, sems, full, lands)
        return token

    def advance(self, stage, after):
        names, sems, full, lands = self.state[stage]
        full, got = _exchange_wait(sems, full, lands, _swap_plan, after, "swap_wait_" + stage)
        sums = [_chip_sum(g, a, "chip_sum_" + n) for n, g, a in zip(names, full, got)]
        lands = [lax.empty((3,) + s[0].shape[1:], BF16) for s in sums]
        sems, sent, lands, token = _exchange_start([s[0] for s in sums], lands, _owners_plan, 3 * len(sums),
                                                   "owners_start_" + stage)
        self.state[stage] = (names, [s[1] for s in sums], sems, sent, lands)
        return token

    def finish(self, stage, after):
        names, own, sems, sent, lands = self.state[stage]
        _, got = _exchange_wait(sems, sent, lands, _owners_plan, after, "owners_wait_" + stage)
        return {n: _owner_sum(o, g, "owner_sum_" + n) for n, o, g in zip(names, own, got)}


def _chip_sum(g, got, name):
    _, r, cc = g.shape
    rh = r // 2

    def body(g_ref, got_ref, hb_ref, own_ref):
        x, y, c = _place()
        s = pl.program_id(0)
        h = g_ref[pl.ds(pl.multiple_of(c * rh, 8), rh), :] + got_ref[...]
        hb_ref[...] = h.astype(BF16)

        @pl.when(s == 2 * x + y)
        def _():
            own_ref[...] = h

    return pl.pallas_call(
        body, grid=(NSH,), name=name,
        in_specs=[pl.BlockSpec((None, r, cc), lambda s: (s, 0, 0)), pl.BlockSpec((None, rh, cc), lambda s: (s, 0, 0))],
        out_specs=[pl.BlockSpec((None, rh, cc), lambda s: (s, 0, 0)), pl.BlockSpec((rh, cc), lambda s: (0, 0))],
        out_shape=[jax.ShapeDtypeStruct((NSH, rh, cc), BF16), jax.ShapeDtypeStruct((rh, cc), F32)],
        compiler_params=_params("arbitrary"),
    )(g, got)


def _owner_sum(own, got, name):
    rh, cc = own.shape
    rows = rh // 2

    def body(own_ref, got_ref, o_ref):
        o_ref[...] = ((own_ref[...] + got_ref[0].astype(F32)) + got_ref[1].astype(F32)) + got_ref[2].astype(F32)

    return pl.pallas_call(
        body, grid=(2,), name=name,
        in_specs=[pl.BlockSpec((rows, cc), lambda i: (i, 0)), pl.BlockSpec((3, rows, cc), lambda i: (0, i, 0))],
        out_specs=pl.BlockSpec((rows, cc), lambda i: (i, 0)),
        out_shape=jax.ShapeDtypeStruct((rh, cc), F32), compiler_params=_params("arbitrary"),
    )(own, got)


def _send_halves(halves, name):
    n = len(halves)

    def body(*refs):
        ins, outs = refs[:n], refs[n:2 * n]
        send_sems, recv_sems = refs[2 * n:]
        x, y, c = _place()
        copies = [pltpu.make_async_remote_copy(src_ref=ins[w], dst_ref=outs[w], send_sem=send_sems.at[w], recv_sem=recv_sems.at[w],
                                               device_id=(x, y, 1 - c), device_id_type=MESH) for w in range(n)]
        for cp in copies:
            cp.start()
        for cp in copies:
            cp.wait()

    return pl.pallas_call(
        body, name=name, in_specs=[_ANY] * n, out_specs=[_ANY] * n,
        out_shape=[jax.ShapeDtypeStruct(h.shape, F32) for h in halves],
        scratch_shapes=[pltpu.SemaphoreType.DMA((n,)), pltpu.SemaphoreType.DMA((n,))],
        compiler_params=pltpu.CompilerParams(has_side_effects=True),
    )(*halves)


def _all_reduce_small(part):
    def body(p_ref, o_ref, rbuf, send1, recv1, send2, recv2):
        x, y, c = _place()
        me = 4 * x + 2 * y + c
        peers = []
        for k in range(1, 8):
            px, py, pc = x ^ ((k >> 2) & 1), y ^ ((k >> 1) & 1), c ^ (k & 1)
            peers.append((k, (px, py, pc), 4 * px + 2 * py + pc))

        def rows(d):
            return pl.ds(pl.multiple_of(d * SMALL_SLICE, 8), SMALL_SLICE)

        first = [pltpu.make_async_remote_copy(src_ref=p_ref.at[rows(idx), :], dst_ref=rbuf.at[me], send_sem=send1.at[k],
                                              recv_sem=recv1.at[k], device_id=dev, device_id_type=MESH)
                 for k, dev, idx in peers]
        for cp in first:
            cp.start()
        rbuf[me] = p_ref[rows(me), :]
        for k, dev, idx in peers:
            pltpu.make_async_remote_copy(src_ref=p_ref.at[rows(idx), :], dst_ref=rbuf.at[idx], send_sem=send1.at[k],
                                         recv_sem=recv1.at[k], device_id=dev, device_id_type=MESH).wait_recv()
        acc = rbuf[0]
        for d in range(1, 8):
            acc = acc + rbuf[d]
        o_ref[rows(me), :] = acc
        second = [pltpu.make_async_remote_copy(src_ref=o_ref.at[rows(me), :], dst_ref=o_ref.at[rows(me), :],
                                               send_sem=send2.at[k], recv_sem=recv2.at[k], device_id=dev, device_id_type=MESH)
                  for k, dev, idx in peers]
        for cp in second:
            cp.start()
        for k, dev, idx in peers:
            pltpu.make_async_remote_copy(src_ref=o_ref.at[rows(me), :], dst_ref=o_ref.at[rows(idx), :], send_sem=send2.at[k],
                                         recv_sem=recv2.at[k], device_id=dev, device_id_type=MESH).wait_recv()
        for cp in first + second:
            cp.wait_send()

    return pl.pallas_call(
        body, name="all_reduce_small", in_specs=[_WHOLE], out_specs=_WHOLE,
        out_shape=jax.ShapeDtypeStruct((SMALL_ROWS, 128), F32),
        scratch_shapes=[pltpu.VMEM((8, SMALL_SLICE, 128), F32)] + [pltpu.SemaphoreType.DMA((8,))] * 4,
        compiler_params=pltpu.CompilerParams(has_side_effects=True),
    )(part)


def _adamw_update(w, gv, m, v):
    nm = ADAM_B1 * m + (1.0 - ADAM_B1) * gv
    nv = ADAM_B2 * v + (1.0 - ADAM_B2) * (gv * gv)
    m_hat = nm / (1.0 - ADAM_B1 ** ADAM_STEP)
    v_hat = nv / (1.0 - ADAM_B2 ** ADAM_STEP)
    return -ADAM_LR * (m_hat / (jnp.sqrt(v_hat) + ADAM_EPS) + ADAM_WD * w), nm, nv


def _adamw(w, g, m, v, name):
    rows = w.shape[0] // 4

    def body(w_ref, g_ref, m_ref, v_ref, d_ref, nm_ref, nv_ref):
        d_ref[...], nm_ref[...], nv_ref[...] = _adamw_update(w_ref[...], g_ref[...], m_ref[...], v_ref[...])

    spec = pl.BlockSpec((rows, w.shape[1]), lambda i: (i, 0))
    out = jax.ShapeDtypeStruct(w.shape, F32)
    return pl.pallas_call(body, grid=(4,), in_specs=[spec] * 4, out_specs=[spec] * 3, out_shape=[out] * 3, name=name,
                          compiler_params=_params("arbitrary"))(w, g, m, v)


def _adamw_halves(w, mine, theirs, m, v, name):
    rh, cc = mine.shape
    rows = rh // 2

    def body(w_ref, mine_ref, theirs_ref, m_ref, v_ref, g_ref, d_ref, nm_ref, nv_ref):
        gv = jnp.where(pl.program_id(0) == lax.axis_index("c"), mine_ref[...], theirs_ref[...])
        g_ref[...] = gv
        d_ref[...], nm_ref[...], nv_ref[...] = _adamw_update(w_ref[...], gv, m_ref[...], v_ref[...])

    spec = pl.BlockSpec((rows, cc), lambda h, i: (2 * h + i, 0))
    half = pl.BlockSpec((rows, cc), lambda h, i: (i, 0))
    out = jax.ShapeDtypeStruct(w.shape, F32)
    return pl.pallas_call(body, grid=(2, 2), in_specs=[spec, half, half, spec, spec], out_specs=[spec] * 4,
                          out_shape=[out] * 4, name=name, compiler_params=_params("arbitrary", "arbitrary"))(w, mine, theirs, m, v)


SMALL_USED = sum(size for _, size in SMALL) // 128


def _pack_small(vals, tail=None):
    parts = []
    for name, size in SMALL:
        flat = vals[name].reshape(-1).astype(F32)
        parts.append(jnp.pad(flat, (0, size - flat.shape[0])))
    if tail is not None:
        parts.append(tail.reshape(128))
    flat = jnp.concatenate(parts)
    return jnp.pad(flat, (0, SMALL_ROWS * 128 - flat.shape[0])).reshape(SMALL_ROWS, 128)


def _unpack_small(packed, shapes):
    flat = packed.reshape(-1)
    out, off = {}, 0
    for name, size in SMALL:
        n = math.prod(shapes[name])
        out[name] = flat[off:off + n].reshape(shapes[name])
        off += size
    return out


def kernel(x, ffn1_pre_g, ffn1_w1, ffn1_w3, ffn1_w2, ffn1_post_g, mix_pre_g, w_in, conv_w, conv_b, rg_a_w, rg_a_b, rg_x_w, rg_x_b, lru_lambda, w_lru_out, attn_sinks, rel_bias, w_attn_out, w_gate, b_gate, w_o, mix_post_g, ffn2_pre_g, ffn2_w1, ffn2_w3, ffn2_w2, ffn2_post_g, loss_target, m_ffn1_pre_g, m_ffn1_w1, m_ffn1_w3, m_ffn1_w2, m_ffn1_post_g, m_mix_pre_g, m_w_in, m_conv_w, m_conv_b, m_rg_a_w, m_rg_a_b, m_rg_x_w, m_rg_x_b, m_lru_lambda, m_w_lru_out, m_attn_sinks, m_rel_bias, m_w_attn_out, m_w_gate, m_b_gate, m_w_o, m_mix_post_g, m_ffn2_pre_g, m_ffn2_w1, m_ffn2_w3, m_ffn2_w2, m_ffn2_post_g, v_ffn1_pre_g, v_ffn1_w1, v_ffn1_w3, v_ffn1_w2, v_ffn1_post_g, v_mix_pre_g, v_w_in, v_conv_w, v_conv_b, v_rg_a_w, v_rg_a_b, v_rg_x_w, v_rg_x_b, v_lru_lambda, v_w_lru_out, v_attn_sinks, v_rel_bias, v_w_attn_out, v_w_gate, v_b_gate, v_w_o, v_mix_post_g, v_ffn2_pre_g, v_ffn2_w1, v_ffn2_w3, v_ffn2_w2, v_ffn2_post_g):
    given = dict(locals())
    chip = 2 * lax.axis_index("x") + lax.axis_index("y")
    transposed = ("ffn1_w1", "ffn1_w3", "ffn2_w1", "ffn2_w3")

    def shard(name, moment=""):
        w = given[moment + name][0]
        return w.T if name in transposed else w

    def unshard(name, w):
        return (w.T if name in transposed else w)[None]

    def my_columns(a):
        own = lax.broadcasted_iota(jnp.int32, (1, 4, D), 2) // (D // NSH) == chip
        return jnp.where(own, jnp.tile(a, (1, 1, NSH)), 0.0)

    def only_my_columns(a):
        parts = a.reshape(1, 4, NSH, D // NSH)
        return sum(jnp.where(chip == s, parts[:, :, s], 0.0) for s in range(NSH))

    chip_arr = jnp.reshape(chip, (1,)).astype(jnp.int32)
    stage_names = {"ffn1": ["ffn1_w1", "ffn1_w3", "ffn1_w2", "conv_w"],
                   "mix": ["w_in", "w_gate", "w_lru_out", "w_attn_out", "w_o"],
                   "ffn2": ["ffn2_w1", "ffn2_w3", "ffn2_w2"]}
    in_flight, started = {}, None
    for stage, names in stage_names.items():
        bufs = [jnp.where(lax.broadcasted_iota(jnp.int32, (NSH, 4, D // NSH), 0) == chip, given[n], 0.0) if n == "conv_w"
                else _cast_into_slot(shard(n), chip_arr, "cast_" + n, started) for n in names]
        (in_flight[stage],), started = _gather_start([bufs], "gather_start_" + stage)
    all_started = started

    def weights(stage, after):
        names = stage_names[stage]
        send_sems, recv_sems, landing = in_flight[stage]
        if stage == "ffn1":
            after = all_started
        landed = _gather_wait(send_sems, recv_sems, landing, after, "gather_wait_" + stage)
        halves = [b for b in landed if b.dtype == BF16]
        out = dict(zip([n for n, b in zip(names, landed) if b.dtype == BF16], _sibling_fill(halves, "sibling_fill_" + stage)))
        if "conv_w" in names:
            out["conv_w"] = jnp.transpose(landed[names.index("conv_w")], (1, 0, 2)).reshape(4, D)
        return out

    small_shapes = {n: given[n].shape for n, _ in SMALL}
    small_shapes["conv_w"] = (1, 4, D)
    sm = {n: (given[n][0] if given[n].shape[0] == 1 and n != "rel_bias" else given[n]) for n, _ in SMALL if n != "conv_w"}

    reducer = _Reducer()
    sq, dx, _, small = _local_step(x[0], loss_target[0], weights, sm, reducer)

    reducer.advance("ffn1", dx)
    reduced_small = _all_reduce_small(_pack_small(small, tail=sq))
    loss = reduced_small[SMALL_USED, 0] * (0.5 / D)
    small_g = _unpack_small(reduced_small, small_shapes)
    grads, delta, new_m, new_v = {}, {}, {}, {}
    after = [reduced_small]
    for stage in ("ffn2", "mix", "ffn1"):
        halves = reducer.finish(stage, after)
        from_sibling = _send_halves(list(halves.values()), "send_halves_" + stage)
        for (n, mine), theirs in zip(halves.items(), from_sibling):
            grads[n], delta[n], new_m[n], new_v[n] = (unshard(n, r) for r in _adamw_halves(
                shard(n), mine, theirs, shard(n, "m_"), shard(n, "v_"), "adamw_" + n))
            after.append(new_v[n])

    packed = [_pack_small({n: (my_columns(given[pre + n]) if n == "conv_w" else given[pre + n]) for n, _ in SMALL})
              for pre in ("", "m_", "v_")]
    outs = _adamw(packed[0], reduced_small, packed[1], packed[2], "adamw_small")
    for dst, arr in zip((delta, new_m, new_v), outs):
        dst.update(_unpack_small(arr, small_shapes))
    small_out = dict(small_g)
    for d in (small_out, delta, new_m, new_v):
        d["conv_w"] = only_my_columns(d["conv_w"])
    grads.update(small_out)
    return (loss, dx[None], *[grads[n] for n in WEIGHTS], *[delta[n] for n in WEIGHTS], *[new_m[n] for n in WEIGHTS],
            *[new_v[n] for n in WEIGHTS])
```

```python
import functools
import math

import jax
import jax.numpy as jnp
from jax import lax
from jax.experimental import pallas as pl
from jax.experimental.pallas import tpu as pltpu

F32, BF16 = jnp.float32, jnp.bfloat16
D = 1024
NSH = 4
FF_S = 704
IN_S = 896
GATE_S = 512
KV_W = 256
CHUNK = 64
KB = 192
N_HEADS = 16
HEAD_DIM = 64
N_BUCKETS = 32
KP = 256
PAD_KEYS = 128
RMS_EPS = 1e-6
NEG_INF = -1e30
LRU_C = 8.0
TM = 512
TM_SCAN = 256
VMEM_LIMIT = 56 * 1024 * 1024
ADAM_LR, ADAM_B1, ADAM_B2, ADAM_EPS, ADAM_WD, ADAM_STEP = 0.001, 0.9, 0.999, 1e-08, 0.01, 10
SMALL_ROWS = 1216
SMALL_SLICE = SMALL_ROWS // 8
MESH = pl.DeviceIdType.MESH

BIG = ["ffn1_w1", "ffn1_w3", "ffn1_w2", "w_in", "w_lru_out", "w_attn_out", "w_gate", "w_o", "ffn2_w1", "ffn2_w3", "ffn2_w2"]
SMALL = [("ffn1_pre_g", 1024), ("ffn1_post_g", 1024), ("mix_pre_g", 1024), ("conv_w", 4096), ("conv_b", 1024),
         ("rg_a_w", 65536), ("rg_a_b", 1024), ("rg_x_w", 65536), ("rg_x_b", 1024), ("lru_lambda", 1024),
         ("attn_sinks", 1024), ("rel_bias", 1024), ("b_gate", 2048), ("mix_post_g", 1024), ("ffn2_pre_g", 1024),
         ("ffn2_post_g", 1024)]
WEIGHTS = ["ffn1_pre_g", "ffn1_w1", "ffn1_w3", "ffn1_w2", "ffn1_post_g", "mix_pre_g", "w_in", "conv_w", "conv_b", "rg_a_w",
           "rg_a_b", "rg_x_w", "rg_x_b", "lru_lambda", "w_lru_out", "attn_sinks", "rel_bias", "w_attn_out", "w_gate", "b_gate",
           "w_o", "mix_post_g", "ffn2_pre_g", "ffn2_w1", "ffn2_w3", "ffn2_w2", "ffn2_post_g"]


def _params(*sem):
    return pltpu.CompilerParams(dimension_semantics=sem or None, vmem_limit_bytes=VMEM_LIMIT)


def _nn(a, b):
    return jnp.dot(a, b, preferred_element_type=F32)


def _nt(a, b):
    return lax.dot_general(a, b, (((1,), (1,)), ((), ())), preferred_element_type=F32)


def _tn(a, b):
    return lax.dot_general(a, b, (((0,), (0,)), ((), ())), preferred_element_type=F32)


def _rms(x, g):
    rstd = lax.rsqrt(jnp.mean(x * x, axis=-1, keepdims=True) + RMS_EPS)
    return (x * rstd) * g


def _rms_bwd(dout, x, g):
    rstd = lax.rsqrt(jnp.mean(x * x, axis=-1, keepdims=True) + RMS_EPS)
    xhat = x * rstd
    dg = jnp.sum(dout * xhat, axis=0, keepdims=True)
    dxhat = dout * g
    dx = rstd * (dxhat - xhat * jnp.mean(dxhat * xhat, axis=-1, keepdims=True))
    return dx, dg


_GELU_K = math.sqrt(2.0 / math.pi)


def _gelu(x):
    return x * (0.5 * (1.0 + jnp.tanh(_GELU_K * (x + 0.044715 * (x * x * x)))))


def _gelu_grad(x):
    t = jnp.tanh(_GELU_K * (x + 0.044715 * (x * x * x)))
    return 0.5 * (1.0 + t) + x * (0.5 * (1.0 - t * t) * (_GELU_K * (1.0 + 3.0 * 0.044715 * (x * x))))


def _softplus_neg(lam):
    z = -lam
    u = jnp.exp(-jnp.abs(z))
    w = 1.0 + u
    log1p_u = jnp.where(w == 1.0, u, jnp.log(w) * (u / (w - 1.0)))
    return jnp.maximum(z, 0.0) + log1p_u


def _lru_coeffs(r, sp):
    log_a = (-LRU_C * r) * sp
    a = jnp.exp(log_a)
    t = jnp.tanh(log_a)
    s = jnp.sqrt(-2.0 * t / (1.0 - t))
    return a, s


def _row_spec(tm, width):
    return pl.BlockSpec((tm, width), lambda i: (i, 0))


def _vec_spec(width):
    return pl.BlockSpec((1, width), lambda i: (0, 0))


_WHOLE = pl.BlockSpec(memory_space=pltpu.VMEM)


def _tile(t, tm=TM):
    return min(tm, t)


def _ffn_fwd(x, gpre, w1g, w3g, w2g, gpost, name, target=None):
    t = x.shape[0]
    tm = _tile(t)
    last = target is not None

    def body(x_ref, gpre_ref, w1_ref, w3_ref, w2_ref, gpost_ref, *refs):
        t_ref, (h_ref, a_ref, b_ref, hm_ref, f_ref), l_ref = (refs[0] if last else None), refs[last:last + 5], refs[-1]
        xv = x_ref[...]
        nb = _rms(xv, gpre_ref[...]).astype(BF16)
        f = jnp.zeros((tm, D), F32)
        for s in range(NSH):
            a = _nt(nb, w1_ref[s])
            b = _nt(nb, w3_ref[s])
            hmb = ((a * jax.nn.sigmoid(a)) * b).astype(BF16)
            a_ref[s] = a.astype(BF16)
            b_ref[s] = b.astype(BF16)
            hm_ref[s] = hmb
            f = f + _nn(hmb, w2_ref[s])
        f_ref[...] = f
        h = xv + 0.5 * _rms(f, gpost_ref[...])
        if last:
            @pl.when(pl.program_id(0) == 0)
            def _():
                l_ref[...] = jnp.zeros_like(l_ref)

            e = h - t_ref[...]
            h_ref[...] = e * (1.0 / D)
            l_ref[...] += jnp.sum(jnp.sum(e * e, axis=0, keepdims=True), axis=1, keepdims=True)
        else:
            h_ref[...] = h

    sh = pl.BlockSpec((NSH, tm, FF_S), lambda i: (0, i, 0))
    act = jax.ShapeDtypeStruct((NSH, t, FF_S), BF16)
    return pl.pallas_call(
        body, grid=(t // tm,), name=name,
        in_specs=[_row_spec(tm, D), _vec_spec(D), _WHOLE, _WHOLE, _WHOLE, _vec_spec(D)] + [_row_spec(tm, D)] * last,
        out_specs=[_row_spec(tm, D), sh, sh, sh, _row_spec(tm, D)] + [pl.BlockSpec((1, 128), lambda i: (0, 0))] * last,
        out_shape=[jax.ShapeDtypeStruct((t, D), F32), act, act, act, jax.ShapeDtypeStruct((t, D), F32)]
        + [jax.ShapeDtypeStruct((1, 128), F32)] * last,
        compiler_params=_params("arbitrary"),
    )(x, gpre, w1g, w3g, w2g, gpost, *([target] if last else []))


def _mix_proj(h1, gmix, w_in_g, w_gate_g, b_gate):
    t = h1.shape[0]
    tm = _tile(t)

    def body(h_ref, g_ref, win_ref, wg_ref, bg_ref, u_ref, q_ref, k_ref, v_ref, xr_ref, xg_ref, gate_ref):
        ub = _rms(h_ref[...], g_ref[...]).astype(BF16)
        u_ref[...] = ub
        p0 = _nn(ub, win_ref[0])
        q_ref[:, 0:896] = p0.astype(BF16)
        p1 = _nn(ub, win_ref[1])
        q_ref[:, 896:1024] = p1[:, 0:128].astype(BF16)
        k_ref[...] = p1[:, 128:384].astype(BF16)
        v_ref[...] = p1[:, 384:640].astype(BF16)
        xr_ref[:, 0:256] = p1[:, 640:896]
        p2 = _nn(ub, win_ref[2])
        xr_ref[:, 256:1024] = p2[:, 0:768]
        xg_ref[:, 0:128] = p2[:, 768:896]
        xg_ref[:, 128:1024] = _nn(ub, win_ref[3])
        for s in range(NSH):
            sl = slice(s * GATE_S, (s + 1) * GATE_S)
            gate_ref[:, sl] = jax.nn.sigmoid(_nn(ub, wg_ref[s]) + bg_ref[:, sl])

    return pl.pallas_call(
        body, grid=(t // tm,), name="mix_proj",
        in_specs=[_row_spec(tm, D), _vec_spec(D), _WHOLE, _WHOLE, _vec_spec(2 * D)],
        out_specs=[_row_spec(tm, D), _row_spec(tm, D), _row_spec(tm, KV_W), _row_spec(tm, KV_W), _row_spec(tm, D),
                   _row_spec(tm, D), _row_spec(tm, 2 * D)],
        out_shape=[jax.ShapeDtypeStruct((t, D), BF16), jax.ShapeDtypeStruct((t, D), BF16),
                   jax.ShapeDtypeStruct((t, KV_W), BF16), jax.ShapeDtypeStruct((t, KV_W), BF16),
                   jax.ShapeDtypeStruct((t, D), F32), jax.ShapeDtypeStruct((t, D), F32),
                   jax.ShapeDtypeStruct((t, 2 * D), F32)],
        compiler_params=_params("arbitrary"),
    )(h1, gmix, w_in_g, w_gate_g, b_gate)


def _rglru_fwd(xr, xg, conv_w, conv_b, wa2, ba, wx2, bx, lam):
    t = xr.shape[0]
    tm = _tile(t, TM_SCAN)
    nb8 = tm // 8

    def body(xr_ref, xrp_ref, xg_ref, cw_ref, cb_ref, wa_ref, ba_ref, wx_ref, bx_ref, lam_ref,
             hr_ref, yain_ref, xc_ref, r_ref, ig_ref, ext, a_sc, h_sc):
        i = pl.program_id(0)

        @pl.when(i == 0)
        def _():
            h_sc[...] = jnp.zeros_like(h_sc)

        ext[0:8, :] = jnp.where(i == 0, 0.0, xrp_ref[...])
        ext[8:8 + tm, :] = xr_ref[...]
        xc = jnp.broadcast_to(cb_ref[...], (tm, D))
        for tap in range(4):
            xc = xc + ext[pl.ds(5 + tap, tm), :] * cw_ref[tap:tap + 1, :]
        xc_ref[...] = xc
        xcb = xc.astype(BF16)
        for p in range(8):
            sl = slice(p * 128, (p + 1) * 128)
            r_ref[:, sl] = jax.nn.sigmoid(_nn(xcb[:, sl], wa_ref[p]) + ba_ref[:, sl])
            ig_ref[:, sl] = jax.nn.sigmoid(_nn(xcb[:, sl], wx_ref[p]) + bx_ref[:, sl])
        a, s = _lru_coeffs(r_ref[...], _softplus_neg(lam_ref[...]))
        a_sc[...] = a
        hr_ref[...] = s * (ig_ref[...] * xc)

        def blk(j, h):
            st = pl.multiple_of(j * 8, 8)
            a8 = a_sc[pl.ds(st, 8), :]
            u8 = hr_ref[pl.ds(st, 8), :]
            rows = []
            for k in range(8):
                h = a8[k:k + 1, :] * h + u8[k:k + 1, :]
                rows.append(h)
            hr_ref[pl.ds(st, 8), :] = jnp.concatenate(rows, axis=0)
            return h

        h_sc[0:1, :] = lax.fori_loop(0, nb8, blk, h_sc[0:1, :])
        yain_ref[...] = (hr_ref[...] * _gelu(xg_ref[...])).astype(BF16)

    prev = pl.BlockSpec((8, D), lambda i: (jnp.maximum(i * nb8 - 1, 0), 0))
    full = lambda shape: pl.BlockSpec(shape, lambda i: tuple(0 for _ in shape))
    f32 = jax.ShapeDtypeStruct((t, D), F32)
    return pl.pallas_call(
        body, grid=(t // tm,), name="rglru_fwd",
        in_specs=[_row_spec(tm, D), prev, _row_spec(tm, D), full((4, D)), _vec_spec(D), full((8, 128, 128)), _vec_spec(D),
                  full((8, 128, 128)), _vec_spec(D), _vec_spec(D)],
        out_specs=[_row_spec(tm, D)] * 5,
        out_shape=[f32, jax.ShapeDtypeStruct((t, D), BF16), f32, f32, f32],
        scratch_shapes=[pltpu.VMEM((tm + 8, D), F32), pltpu.VMEM((tm, D), F32), pltpu.VMEM((8, D), F32)],
        compiler_params=_params("arbitrary"),
    )(xr, xr, xg, conv_w, conv_b, wa2, ba, wx2, bx, lam)


def _bias_fwd(table_t, onehot_t):
    def body(t_ref, e_ref, o_ref):
        o_ref[...] = jnp.dot(t_ref[...], e_ref[...], preferred_element_type=F32, precision=lax.Precision.HIGHEST)

    return pl.pallas_call(body, out_shape=jax.ShapeDtypeStruct((N_HEADS, CHUNK * KB), F32), name="bias_fwd",
                          compiler_params=_params())(table_t, onehot_t)


def _bias_bwd(dbias_flat, onehot_t, ds_rows):
    def body(d_ref, e_ref, s_ref, o_ref, so_ref):
        o_ref[...] = lax.dot_general(d_ref[...], e_ref[...], (((1,), (1,)), ((), ())), preferred_element_type=F32,
                                     precision=lax.Precision.HIGHEST)
        so_ref[...] = jnp.zeros_like(so_ref)
        for r in range(4):
            so_ref[:, r:r + 1] = jnp.sum(s_ref[:, r * CHUNK:(r + 1) * CHUNK], axis=1, keepdims=True)

    return pl.pallas_call(body, out_shape=[jax.ShapeDtypeStruct((N_HEADS, N_BUCKETS), F32), jax.ShapeDtypeStruct((8, 128), F32)],
                          name="bias_bwd", compiler_params=_params())(dbias_flat, onehot_t, ds_rows)


def _stack_heads(q):
    return jnp.concatenate(
        [jnp.concatenate([q[:, (4 * g + r) * HEAD_DIM:(4 * g + r + 1) * HEAD_DIM] for g in range(4)], axis=1)
         for r in range(4)], axis=0)


def _unstack_heads(o):
    return jnp.concatenate([o[r * CHUNK:(r + 1) * CHUNK, g * HEAD_DIM:(g + 1) * HEAD_DIM] for g in range(4) for r in range(4)],
                           axis=1)


def _block_diag(w, mask):
    return jnp.concatenate([w] * 4, axis=0) * mask


def _attn_softmax(q_all, kbd, bias_t, sink_rows, c):
    s = _nt(kbd, q_all) * (HEAD_DIM ** -0.5) + bias_t
    j = lax.broadcasted_iota(jnp.int32, (4 * KP, 1), 0) % KP
    s = jnp.where((j < KB) & (j + c * CHUNK >= PAD_KEYS), s, NEG_INF)
    ps, sinks = [], []
    for g in range(4):
        sg = s[g * KP:(g + 1) * KP, :]
        sink = sink_rows[g:g + 1, :]
        m = jnp.maximum(jnp.max(sg, axis=0, keepdims=True), sink)
        e = jnp.exp(sg - m)
        es = jnp.exp(sink - m)
        inv = 1.0 / (jnp.sum(e, axis=0, keepdims=True) + es)
        ps.append(e * inv)
        sinks.append(es * inv)
    return ps, sinks


def _attn_fwd(sink_rows, q, kp, vp, bias_t, mask):
    t = q.shape[0]

    def body(sink_ref, q_ref, kp_ref, vp_ref, bias_ref, mask_ref, o_ref):
        c = pl.program_id(0)
        st = pl.multiple_of(c * CHUNK, CHUNK)
        kbd = _block_diag(kp_ref[pl.ds(st, KP), :], mask_ref[...])
        vbd = _block_diag(vp_ref[pl.ds(st, KP), :], mask_ref[...])
        ps, _ = _attn_softmax(_stack_heads(q_ref[...]), kbd, bias_ref[...], sink_ref[...], c)
        p_t = jnp.concatenate(ps, axis=0).astype(BF16)
        o_ref[...] = _unstack_heads(_tn(p_t, vbd)).astype(BF16)

    return pl.pallas_call(
        body, grid=(t // CHUNK,), name="attn_fwd",
        in_specs=[_WHOLE, _row_spec(CHUNK, D), _WHOLE, _WHOLE, _WHOLE, _WHOLE],
        out_specs=_row_spec(CHUNK, D),
        out_shape=jax.ShapeDtypeStruct((t, D), BF16),
        compiler_params=_params("arbitrary"),
    )(sink_rows, q, kp, vp, bias_t, mask)


def _merge_fwd(yain, o, gate, h1, w_lru, w_att, w_o, gpost):
    t = h1.shape[0]
    tm = _tile(t)

    def body(ya_ref, o_ref, g_ref, h_ref, wl_ref, wa_ref, wo_ref, gp_ref, h2_ref, mo_ref, mg_ref, ya_out, yb_out):
        ya = _nn(ya_ref[...], wl_ref[...])
        yb = _nn(o_ref[...], wa_ref[...])
        mg = (g_ref[:, 0:D] * ya + g_ref[:, D:2 * D] * yb).astype(BF16)
        mo = _nn(mg, wo_ref[...])
        ya_out[...] = ya.astype(BF16)
        yb_out[...] = yb.astype(BF16)
        mg_ref[...] = mg
        mo_ref[...] = mo
        h2_ref[...] = h_ref[...] + _rms(mo, gp_ref[...])

    f32 = jax.ShapeDtypeStruct((t, D), F32)
    b16 = jax.ShapeDtypeStruct((t, D), BF16)
    return pl.pallas_call(
        body, grid=(t // tm,), name="merge_fwd",
        in_specs=[_row_spec(tm, D), _row_spec(tm, D), _row_spec(tm, 2 * D), _row_spec(tm, D), _WHOLE, _WHOLE, _WHOLE,
                  _vec_spec(D)],
        out_specs=[_row_spec(tm, D)] * 5,
        out_shape=[f32, f32, b16, b16, b16],
        compiler_params=_params("arbitrary"),
    )(yain, o, gate, h1, w_lru, w_att, w_o, gpost)


def _ffn_bwd(dh, x, f, a, b, gpre, gpost, w1g, w3g, w2g, name):
    t = x.shape[0]
    tm = _tile(t, TM_SCAN)

    def body(dh_ref, x_ref, f_ref, a_ref, b_ref, gpre_ref, gpost_ref, w1_ref, w3_ref, w2_ref,
             dx_ref, n_ref, da_ref, db_ref, df_ref, dgpre_ref, dgpost_ref):
        @pl.when(pl.program_id(0) == 0)
        def _():
            dgpre_ref[...] = jnp.zeros_like(dgpre_ref)
            dgpost_ref[...] = jnp.zeros_like(dgpost_ref)

        dhv = dh_ref[...]
        xv = x_ref[...]
        df, dgp = _rms_bwd(0.5 * dhv, f_ref[...], gpost_ref[...])
        dgpost_ref[...] += dgp
        dfb = df.astype(BF16)
        df_ref[...] = dfb
        n_ref[...] = _rms(xv, gpre_ref[...]).astype(BF16)
        dn = jnp.zeros((tm, D), F32)
        for s in range(NSH):
            av = a_ref[s].astype(F32)
            bv = b_ref[s].astype(F32)
            sg = jax.nn.sigmoid(av)
            dhm = _nt(dfb, w2_ref[s])
            dab = (dhm * bv * (sg * (1.0 + av * (1.0 - sg)))).astype(BF16)
            dbb = (dhm * (av * sg)).astype(BF16)
            da_ref[s] = dab
            db_ref[s] = dbb
            dn = dn + _nn(dab, w1_ref[s]) + _nn(dbb, w3_ref[s])
        dxn, dg = _rms_bwd(dn, xv, gpre_ref[...])
        dgpre_ref[...] += dg
        dx_ref[...] = dhv + dxn

    sh = pl.BlockSpec((NSH, tm, FF_S), lambda i: (0, i, 0))
    act = jax.ShapeDtypeStruct((NSH, t, FF_S), BF16)
    vec = jax.ShapeDtypeStruct((1, D), F32)
    return pl.pallas_call(
        body, grid=(t // tm,), name=name,
        in_specs=[_row_spec(tm, D), _row_spec(tm, D), _row_spec(tm, D), sh, sh, _vec_spec(D), _vec_spec(D), _WHOLE, _WHOLE,
                  _WHOLE],
        out_specs=[_row_spec(tm, D), _row_spec(tm, D), sh, sh, _row_spec(tm, D), _vec_spec(D), _vec_spec(D)],
        out_shape=[jax.ShapeDtypeStruct((t, D), F32), jax.ShapeDtypeStruct((t, D), BF16), act, act,
                   jax.ShapeDtypeStruct((t, D), BF16), vec, vec],
        compiler_params=_params("arbitrary"),
    )(dh, x, f, a, b, gpre, gpost, w1g, w3g, w2g)


def _behind(body, after):
    if after is None:
        return body, [], []

    def ordered(_, *refs):
        body(*refs)

    return ordered, [_ANY], [after]


def _ffn_bwd_acts(dh, x, f, a, b, gpre, gpost, w2g, name):
    t = x.shape[0]
    tm = _tile(t)

    def body(dh_ref, x_ref, f_ref, a_ref, b_ref, gpre_ref, gpost_ref, w2_ref, n_ref, da_ref, db_ref, df_ref, dgpost_ref):
        @pl.when(pl.program_id(0) == 0)
        def _():
            dgpost_ref[...] = jnp.zeros_like(dgpost_ref)

        df, dgp = _rms_bwd(0.5 * dh_ref[...], f_ref[...], gpost_ref[...])
        dgpost_ref[...] += dgp
        dfb = df.astype(BF16)
        df_ref[...] = dfb
        n_ref[...] = _rms(x_ref[...], gpre_ref[...]).astype(BF16)
        for s in range(NSH):
            av = a_ref[s].astype(F32)
            bv = b_ref[s].astype(F32)
            sg = jax.nn.sigmoid(av)
            dhm = _nt(dfb, w2_ref[s])
            da_ref[s] = (dhm * bv * (sg * (1.0 + av * (1.0 - sg)))).astype(BF16)
            db_ref[s] = (dhm * (av * sg)).astype(BF16)

    sh = pl.BlockSpec((NSH, tm, FF_S), lambda i: (0, i, 0))
    act = jax.ShapeDtypeStruct((NSH, t, FF_S), BF16)
    b16 = jax.ShapeDtypeStruct((t, D), BF16)
    return pl.pallas_call(
        body, grid=(t // tm,), name=name,
        in_specs=[_row_spec(tm, D), _row_spec(tm, D), _row_spec(tm, D), sh, sh, _vec_spec(D), _vec_spec(D), _WHOLE],
        out_specs=[_row_spec(tm, D), sh, sh, _row_spec(tm, D), _vec_spec(D)],
        out_shape=[b16, act, act, b16, jax.ShapeDtypeStruct((1, D), F32)],
        compiler_params=_params("arbitrary"),
    )(dh, x, f, a, b, gpre, gpost, w2g)


def _ffn_bwd_input(dh, x, da, db, gpre, w1g, w3g, name, after):
    t = x.shape[0]
    tm = _tile(t)

    def body(dh_ref, x_ref, da_ref, db_ref, gpre_ref, w1_ref, w3_ref, dx_ref, dgpre_ref):
        @pl.when(pl.program_id(0) == 0)
        def _():
            dgpre_ref[...] = jnp.zeros_like(dgpre_ref)

        dn = jnp.zeros((tm, D), F32)
        for s in range(NSH):
            dn = dn + _nn(da_ref[s], w1_ref[s]) + _nn(db_ref[s], w3_ref[s])
        dxn, dg = _rms_bwd(dn, x_ref[...], gpre_ref[...])
        dgpre_ref[...] += dg
        dx_ref[...] = dh_ref[...] + dxn

    sh = pl.BlockSpec((NSH, tm, FF_S), lambda i: (0, i, 0))
    body, specs, operands = _behind(body, after)
    return pl.pallas_call(
        body, grid=(t // tm,), name=name,
        in_specs=specs + [_row_spec(tm, D), _row_spec(tm, D), sh, sh, _vec_spec(D), _WHOLE, _WHOLE],
        out_specs=[_row_spec(tm, D), _vec_spec(D)],
        out_shape=[jax.ShapeDtypeStruct((t, D), F32), jax.ShapeDtypeStruct((1, D), F32)],
        compiler_params=_params("arbitrary"),
    )(*operands, dh, x, da, db, gpre, w1g, w3g)


def _wgrad(a, b, a_spec, b_spec, out_spec, out_shape, grid, name, after=None):
    def body(a_ref, b_ref, o_ref):
        o_ref[...] = _tn(a_ref[...], b_ref[...])

    body, specs, operands = _behind(body, after)
    return pl.pallas_call(body, grid=grid, name=name, in_specs=specs + [a_spec, b_spec], out_specs=out_spec,
                          out_shape=jax.ShapeDtypeStruct(out_shape, F32),
                          compiler_params=_params(*("arbitrary",) * len(grid)))(*operands, a, b)


def _wgrad_cols(act, dsh, width, name, after=None):
    t = act.shape[0]
    if dsh.ndim == 3:
        b_spec = pl.BlockSpec((None, t, width), lambda s, k: (s, 0, 0))
    else:
        b_spec = pl.BlockSpec((t, width), lambda s, k: (0, s))
    return _wgrad(act, dsh, pl.BlockSpec((t, 512), lambda s, k: (0, k)), b_spec,
                  pl.BlockSpec((None, 512, width), lambda s, k: (s, k, 0)), (NSH, D, width), (NSH, 2), name, after)


def _wgrad_rows(hm, df, name, after=None):
    t = df.shape[0]
    return _wgrad(hm, df, pl.BlockSpec((None, t, FF_S), lambda s: (s, 0, 0)), pl.BlockSpec((t, D), lambda s: (0, 0)),
                  pl.BlockSpec((None, FF_S, D), lambda s: (s, 0, 0)), (NSH, FF_S, D), (NSH,), name, after)


def _wgrad_sq(a, b, name, after=None):
    t = a.shape[0]
    return _wgrad(a, b, pl.BlockSpec((t, 512), lambda i, j: (0, i)), pl.BlockSpec((t, 512), lambda i, j: (0, j)),
                  pl.BlockSpec((512, 512), lambda i, j: (i, j)), (D, D), (2, 2), name, after)


def _mix_bwd1(dh2, mo, gpost, gate, ya, yb, xg, hr, w_o, w_lru, w_att, after):
    t = dh2.shape[0]
    tm = _tile(t, TM_SCAN)

    def body(dh_ref, mo_ref, gp_ref, g_ref, ya_ref, yb_ref, xg_ref, hr_ref, wo_ref, wl_ref, wa_ref,
             dmo_ref, dya_ref, dyb_ref, dgate_ref, dhr_ref, dxg_ref, do_ref, dgp_ref, dbg_ref):
        @pl.when(pl.program_id(0) == 0)
        def _():
            dgp_ref[...] = jnp.zeros_like(dgp_ref)
            dbg_ref[...] = jnp.zeros_like(dbg_ref)

        dmo, dgp = _rms_bwd(dh_ref[...], mo_ref[...], gp_ref[...])
        dgp_ref[...] += dgp
        dmob = dmo.astype(BF16)
        dmo_ref[...] = dmob
        dm = _nt(dmob, wo_ref[...])
        g0 = g_ref[:, 0:D]
        g1 = g_ref[:, D:2 * D]
        dyab = (dm * g0).astype(BF16)
        dybb = (dm * g1).astype(BF16)
        dya_ref[...] = dyab
        dyb_ref[...] = dybb
        dg0 = dm * ya_ref[...].astype(F32) * (g0 * (1.0 - g0))
        dg1 = dm * yb_ref[...].astype(F32) * (g1 * (1.0 - g1))
        dgate_ref[:, 0:D] = dg0.astype(BF16)
        dgate_ref[:, D:2 * D] = dg1.astype(BF16)
        dbg_ref[:, 0:D] += jnp.sum(dg0, axis=0, keepdims=True)
        dbg_ref[:, D:2 * D] += jnp.sum(dg1, axis=0, keepdims=True)
        dyain = _nt(dyab, wl_ref[...])
        do_ref[...] = _nt(dybb, wa_ref[...]).astype(BF16)
        xgv = xg_ref[...]
        dhr_ref[...] = dyain * _gelu(xgv)
        dxg_ref[...] = (dyain * hr_ref[...] * _gelu_grad(xgv)).astype(BF16)

    b16 = jax.ShapeDtypeStruct((t, D), BF16)
    body, specs, operands = _behind(body, after)
    return pl.pallas_call(
        body, grid=(t // tm,), name="mix_bwd1",
        in_specs=specs + [_row_spec(tm, D), _row_spec(tm, D), _vec_spec(D), _row_spec(tm, 2 * D), _row_spec(tm, D),
                          _row_spec(tm, D), _row_spec(tm, D), _row_spec(tm, D), _WHOLE, _WHOLE, _WHOLE],
        out_specs=[_row_spec(tm, D), _row_spec(tm, D), _row_spec(tm, D), _row_spec(tm, 2 * D), _row_spec(tm, D),
                   _row_spec(tm, D), _row_spec(tm, D), _vec_spec(D), _vec_spec(2 * D)],
        out_shape=[b16, b16, b16, jax.ShapeDtypeStruct((t, 2 * D), BF16), jax.ShapeDtypeStruct((t, D), F32), b16, b16,
                   jax.ShapeDtypeStruct((1, D), F32), jax.ShapeDtypeStruct((1, 2 * D), F32)],
        compiler_params=_params("arbitrary"),
    )(*operands, dh2, mo, gpost, gate, ya, yb, xg, hr, w_o, w_lru, w_att)


def _rglru_bwd(dhr, hr, xc, r, ig, xr, conv_w, wa2, wx2, lam, after):
    t = dhr.shape[0]
    tm = _tile(t, TM_SCAN)
    nb8 = tm // 8
    nt = t // tm

    def body(dhr_ref, hr_ref, hrp_ref, xc_ref, r_ref, ig_ref, xr_ref, xrp_ref, cw_ref, wa_ref, wx_ref, lam_ref,
             dxr_ref, dwa_ref, dwx_ref, dba_ref, dbx_ref, dlam_ref, dcw_ref, dcb_ref,
             ext_h, ext_x, ext_d, a_sc, g_sc, c_sc, nxt_sc):
        i = pl.program_id(0)
        first_tile = i == nt - 1

        @pl.when(i == 0)
        def _():
            c_sc[...] = jnp.zeros_like(c_sc)
            nxt_sc[...] = jnp.zeros_like(nxt_sc)
            for ref in (dwa_ref, dwx_ref, dba_ref, dbx_ref, dlam_ref, dcw_ref, dcb_ref):
                ref[...] = jnp.zeros_like(ref)

        lamv = lam_ref[...]
        sp = _softplus_neg(lamv)
        rv = r_ref[...]
        igv = ig_ref[...]
        xcv = xc_ref[...]
        a, s = _lru_coeffs(rv, sp)
        a_sc[...] = a

        def blk(jj, c):
            st = pl.multiple_of((nb8 - 1 - jj) * 8, 8)
            d8 = dhr_ref[pl.ds(st, 8), :]
            a8 = a_sc[pl.ds(st, 8), :]
            rows = [None] * 8
            for k in range(7, -1, -1):
                g = d8[k:k + 1, :] + c
                c = a8[k:k + 1, :] * g
                rows[k] = g
            g_sc[pl.ds(st, 8), :] = jnp.concatenate(rows, axis=0)
            return c

        c_sc[0:1, :] = lax.fori_loop(0, nb8, blk, c_sc[0:1, :])
        g = g_sc[...]
        ext_h[0:8, :] = jnp.where(first_tile, 0.0, hrp_ref[...])
        ext_h[8:8 + tm, :] = hr_ref[...]
        hprev = ext_h[pl.ds(7, tm), :]
        d_s = g * (igv * xcv)
        dig = g * s * xcv
        dxc = g * s * igv
        dla = (g * hprev) * a - d_s * ((a * a) / s)
        dr_pre = (dla * (-LRU_C * sp)) * (rv * (1.0 - rv))
        di_pre = dig * (igv * (1.0 - igv))
        dlam_ref[...] += jnp.sum(dla * (LRU_C * rv), axis=0, keepdims=True) * jax.nn.sigmoid(-lamv)
        dba_ref[...] += jnp.sum(dr_pre, axis=0, keepdims=True)
        dbx_ref[...] += jnp.sum(di_pre, axis=0, keepdims=True)
        drb = dr_pre.astype(BF16)
        dib = di_pre.astype(BF16)
        xcb = xcv.astype(BF16)
        ext_d[tm:tm + 8, :] = nxt_sc[...]
        for p in range(8):
            sl = slice(p * 128, (p + 1) * 128)
            ext_d[0:tm, sl] = dxc[:, sl] + _nt(drb[:, sl], wa_ref[p]) + _nt(dib[:, sl], wx_ref[p])
            dwa_ref[p] += _tn(xcb[:, sl], drb[:, sl])
            dwx_ref[p] += _tn(xcb[:, sl], dib[:, sl])
        dxcv = ext_d[0:tm, :]
        nxt_sc[...] = ext_d[0:8, :]
        dcb_ref[...] += jnp.sum(dxcv, axis=0, keepdims=True)
        ext_x[0:8, :] = jnp.where(first_tile, 0.0, xrp_ref[...])
        ext_x[8:8 + tm, :] = xr_ref[...]
        dxr = jnp.zeros((tm, D), F32)
        for tap in range(4):
            dxr = dxr + ext_d[pl.ds(3 - tap, tm), :] * cw_ref[tap:tap + 1, :]
            dcw_ref[tap:tap + 1, :] += jnp.sum(dxcv * ext_x[pl.ds(5 + tap, tm), :], axis=0, keepdims=True)
        dxr_ref[...] = dxr.astype(BF16)

    rev = pl.BlockSpec((tm, D), lambda i: (nt - 1 - i, 0))
    prev = pl.BlockSpec((8, D), lambda i: (jnp.maximum((nt - 1 - i) * nb8 - 1, 0), 0))
    full = lambda shape: pl.BlockSpec(shape, lambda i: tuple(0 for _ in shape))
    vec = jax.ShapeDtypeStruct((1, D), F32)
    blocks = jax.ShapeDtypeStruct((8, 128, 128), F32)
    body, specs, operands = _behind(body, after)
    return pl.pallas_call(
        body, grid=(nt,), name="rglru_bwd",
        in_specs=specs + [rev, rev, prev, rev, rev, rev, rev, prev, full((4, D)), full((8, 128, 128)), full((8, 128, 128)),
                          _vec_spec(D)],
        out_specs=[rev, full((8, 128, 128)), full((8, 128, 128)), _vec_spec(D), _vec_spec(D), _vec_spec(D), full((4, D)),
                   _vec_spec(D)],
        out_shape=[jax.ShapeDtypeStruct((t, D), BF16), blocks, blocks, vec, vec, vec, jax.ShapeDtypeStruct((4, D), F32), vec],
        scratch_shapes=[pltpu.VMEM((tm + 8, D), F32), pltpu.VMEM((tm + 8, D), F32), pltpu.VMEM((tm + 8, D), F32),
                        pltpu.VMEM((tm, D), F32), pltpu.VMEM((tm, D), F32), pltpu.VMEM((8, D), F32), pltpu.VMEM((8, D), F32)],
        compiler_params=_params("arbitrary"),
    )(*operands, dhr, hr, hr, xc, r, ig, xr, xr, conv_w, wa2, wx2, lam)


def _attn_bwd(sink_rows, q, kp, vp, bias_t, mask, do):
    t = q.shape[0]
    tp = kp.shape[0]

    def body(sink_ref, q_ref, kp_ref, vp_ref, bias_ref, mask_ref, do_ref, dq_ref, dk_ref, dv_ref, dbias_ref, ds_ref):
        c = pl.program_id(0)

        @pl.when(c == 0)
        def _():
            for ref in (dk_ref, dv_ref, dbias_ref, ds_ref):
                ref[...] = jnp.zeros_like(ref)

        st = pl.multiple_of(c * CHUNK, CHUNK)
        maskv = mask_ref[...]
        kbd = _block_diag(kp_ref[pl.ds(st, KP), :], maskv)
        vbd = _block_diag(vp_ref[pl.ds(st, KP), :], maskv)
        q_all = _stack_heads(q_ref[...])
        do_all = _stack_heads(do_ref[...])
        ps, sinks = _attn_softmax(q_all, kbd, bias_ref[...], sink_ref[...], c)
        dp = _nt(vbd, do_all)
        dscs = []
        for g in range(4):
            dpg = dp[g * KP:(g + 1) * KP, :]
            delta = jnp.sum(ps[g] * dpg, axis=0, keepdims=True)
            dscs.append(ps[g] * (dpg - delta))
            ds_ref[g:g + 1, :] += -(sinks[g] * delta)
        dsc = jnp.concatenate(dscs, axis=0)
        dbias_ref[...] += dsc
        dsb = (dsc * (HEAD_DIM ** -0.5)).astype(BF16)
        dq_ref[...] = _unstack_heads(_tn(dsb, kbd)).astype(BF16)

        lane_group = lax.broadcasted_iota(jnp.int32, (1, 4 * HEAD_DIM), 1) // HEAD_DIM

        def own_blocks(full):
            out = full[0:KP]
            for g in range(1, 4):
                out = jnp.where(lane_group == g, full[g * KP:(g + 1) * KP], out)
            return out

        dk_ref[pl.ds(st, KP), :] += own_blocks(_nn(dsb, q_all))
        dv_ref[pl.ds(st, KP), :] += own_blocks(_nn(jnp.concatenate(ps, axis=0).astype(BF16), do_all))

    full = lambda shape: pl.BlockSpec(shape, lambda i: tuple(0 for _ in shape))
    return pl.pallas_call(
        body, grid=(t // CHUNK,), name="attn_bwd",
        in_specs=[_WHOLE, _row_spec(CHUNK, D), _WHOLE, _WHOLE, _WHOLE, _WHOLE, _row_spec(CHUNK, D)],
        out_specs=[_row_spec(CHUNK, D), full((tp, KV_W)), full((tp, KV_W)), full((4 * KP, 4 * CHUNK)), full((8, 4 * CHUNK))],
        out_shape=[jax.ShapeDtypeStruct((t, D), BF16), jax.ShapeDtypeStruct((tp, KV_W), F32),
                   jax.ShapeDtypeStruct((tp, KV_W), F32), jax.ShapeDtypeStruct((4 * KP, 4 * CHUNK), F32),
                   jax.ShapeDtypeStruct((8, 4 * CHUNK), F32)],
        compiler_params=_params("arbitrary"),
    )(sink_rows, q, kp, vp, bias_t, mask, do)


def _mix_bwd2(dproj, dgate, h1, dh2, gmix, w_in_g, w_gate_g, after):
    t = h1.shape[0]
    tm = _tile(t)

    def body(dp_ref, dg_ref, h_ref, dh_ref, g_ref, win_ref, wg_ref, dh1_ref, dgm_ref):
        @pl.when(pl.program_id(0) == 0)
        def _():
            dgm_ref[...] = jnp.zeros_like(dgm_ref)

        du = jnp.zeros((tm, D), F32)
        for s in range(NSH):
            du = du + _nt(dp_ref[:, s * IN_S:(s + 1) * IN_S], win_ref[s])
            du = du + _nt(dg_ref[:, s * GATE_S:(s + 1) * GATE_S], wg_ref[s])
        dxn, dg = _rms_bwd(du, h_ref[...], g_ref[...])
        dgm_ref[...] += dg
        dh1_ref[...] = dh_ref[...] + dxn

    body, specs, operands = _behind(body, after)
    return pl.pallas_call(
        body, grid=(t // tm,), name="mix_bwd2",
        in_specs=specs + [_row_spec(tm, NSH * IN_S), _row_spec(tm, 2 * D), _row_spec(tm, D), _row_spec(tm, D), _vec_spec(D),
                          _WHOLE, _WHOLE],
        out_specs=[_row_spec(tm, D), _vec_spec(D)],
        out_shape=[jax.ShapeDtypeStruct((t, D), F32), jax.ShapeDtypeStruct((1, D), F32)],
        compiler_params=_params("arbitrary"),
    )(*operands, dproj, dgate, h1, dh2, gmix, w_in_g, w_gate_g)


def _band_onehot():
    nb = N_BUCKETS // 2
    max_exact = nb // 2
    rel = jnp.arange(KB)[None, :] - PAD_KEYS - jnp.arange(CHUNK)[:, None]
    ret = jnp.where(rel > 0, nb, 0)
    n = jnp.abs(rel)
    nf = jnp.maximum(n, 1).astype(jnp.float32)
    large = max_exact + (jnp.log(nf / max_exact) / math.log(128 / max_exact) * (nb - max_exact)).astype(jnp.int32)
    large = jnp.minimum(large, nb - 1)
    buckets = (ret + jnp.where(n < max_exact, n, large)).reshape(1, CHUNK * KB)
    return (buckets == jnp.arange(N_BUCKETS)[:, None]).astype(F32)


def _pair_blocks(w):
    z = jnp.zeros((8, 128, 128), w.dtype)
    return z.at[:, 0:64, 0:64].set(w[0::2]).at[:, 64:128, 64:128].set(w[1::2])


def _unpair_blocks(w2):
    return jnp.stack([w2[:, 0:64, 0:64], w2[:, 64:128, 64:128]], axis=1).reshape(16, 64, 64)


def _local_step(x, target, weights, sm, reducer):
    row = lambda v: v.reshape(1, -1)
    wg = dict(weights("ffn1", x))
    sm = dict(sm, conv_w=wg["conv_w"])
    onehot_t = _band_onehot()
    bias = _bias_fwd(sm["rel_bias"].T, onehot_t).reshape(4, 4, CHUNK, KB)
    bias_t = jnp.pad(jnp.transpose(bias, (0, 3, 1, 2)), ((0, 0), (0, KP - KB), (0, 0), (0, 0))).reshape(4 * KP, 4 * CHUNK)
    sink_rows = jnp.pad(jnp.repeat(sm["attn_sinks"].reshape(4, 4), CHUNK, axis=1), ((0, 4), (0, 0)))
    grp = jnp.arange(4 * KP)[:, None] // KP == jnp.arange(4 * HEAD_DIM)[None, :] // HEAD_DIM
    mask = (grp & (jnp.arange(4 * KP)[:, None] % KP < KB)).astype(BF16)
    wa2 = _pair_blocks(sm["rg_a_w"]).astype(BF16)
    wx2 = _pair_blocks(sm["rg_x_w"]).astype(BF16)

    h1, a1, b1, hm1, f1 = _ffn_fwd(x, row(sm["ffn1_pre_g"]), wg["ffn1_w1"], wg["ffn1_w3"], wg["ffn1_w2"],
                                   row(sm["ffn1_post_g"]), "ffn1_fwd")
    wg.update(weights("mix", h1))
    w_lru = wg["w_lru_out"].reshape(D, D)
    w_att = wg["w_attn_out"].reshape(D, D)
    w_o = wg["w_o"].reshape(D, D)
    u, q, k, v, xr, xg, gate = _mix_proj(h1, row(sm["mix_pre_g"]), wg["w_in"], wg["w_gate"], row(sm["b_gate"]))
    hr, yain, xc, r, ig = _rglru_fwd(xr, xg, sm["conv_w"], row(sm["conv_b"]), wa2, row(sm["rg_a_b"]), wx2,
                                     row(sm["rg_x_b"]), row(sm["lru_lambda"]))
    kp = jnp.pad(k, ((PAD_KEYS, KP - KB), (0, 0)))
    vp = jnp.pad(v, ((PAD_KEYS, KP - KB), (0, 0)))
    o = _attn_fwd(sink_rows, q, kp, vp, bias_t, mask)
    wg.update(weights("ffn2", o))
    h2, mo, merged, ya, yb = _merge_fwd(yain, o, gate, h1, w_lru, w_att, w_o, row(sm["mix_post_g"]))
    dy, a2, b2, hm2, f2, sq = _ffn_fwd(h2, row(sm["ffn2_pre_g"]), wg["ffn2_w1"], wg["ffn2_w3"], wg["ffn2_w2"],
                                       row(sm["ffn2_post_g"]), "ffn2_fwd", target)

    big, small = {}, {}
    dh2, n2, da2, db2, df2, small["ffn2_pre_g"], small["ffn2_post_g"] = _ffn_bwd(
        dy, h2, f2, a2, b2, row(sm["ffn2_pre_g"]), row(sm["ffn2_post_g"]), wg["ffn2_w1"], wg["ffn2_w3"], wg["ffn2_w2"],
        "ffn2_bwd")
    big["ffn2_w1"] = _wgrad_rows(da2, n2, "dw_ffn2_w1")
    big["ffn2_w3"] = _wgrad_rows(db2, n2, "dw_ffn2_w3")
    big["ffn2_w2"] = _wgrad_rows(hm2, df2, "dw_ffn2_w2")
    token = reducer.begin("ffn2", {n: big[n] for n in ("ffn2_w1", "ffn2_w3", "ffn2_w2")})
    dmo, dya, dyb, dgate, dhr, dxg, do, small["mix_post_g"], small["b_gate"] = _mix_bwd1(
        dh2, mo, row(sm["mix_post_g"]), gate, ya, yb, xg, hr, w_o, w_lru, w_att, token)
    big["w_o"] = _wgrad_sq(merged, dmo, "dw_w_o").reshape(NSH, D // NSH, D)
    big["w_lru_out"] = _wgrad_sq(yain, dya, "dw_w_lru_out").reshape(NSH, D // NSH, D)
    big["w_attn_out"] = _wgrad_sq(o, dyb, "dw_w_attn_out").reshape(NSH, D // NSH, D)
    token = reducer.advance("ffn2", big["w_attn_out"])
    (dxr, dwa2, dwx2, small["rg_a_b"], small["rg_x_b"], small["lru_lambda"], small["conv_w"], small["conv_b"]) = _rglru_bwd(
        dhr, hr, xc, r, ig, xr, sm["conv_w"], wa2, wx2, row(sm["lru_lambda"]), token)
    small["rg_a_w"] = _unpair_blocks(dwa2)
    small["rg_x_w"] = _unpair_blocks(dwx2)
    dq, dkp, dvp, dbias_t, ds_rows = _attn_bwd(sink_rows, q, kp, vp, bias_t, mask, do)
    dbias = jnp.transpose(dbias_t.reshape(4, KP, 4, CHUNK)[:, :KB], (0, 2, 3, 1)).reshape(N_HEADS, CHUNK * KB)
    drel_t, dsinks = _bias_bwd(dbias, onehot_t, ds_rows)
    small["attn_sinks"] = dsinks[0:4, 0:4].reshape(N_HEADS)
    small["rel_bias"] = drel_t.T
    t = x.shape[0]
    dproj = jnp.concatenate([dq, dkp[PAD_KEYS:PAD_KEYS + t].astype(BF16), dvp[PAD_KEYS:PAD_KEYS + t].astype(BF16), dxr, dxg],
                            axis=1)
    big["w_in"] = _wgrad_cols(u, dproj, IN_S, "dw_w_in")
    big["w_gate"] = _wgrad_cols(u, dgate, GATE_S, "dw_w_gate")
    token = reducer.begin("mix", {n: big[n] for n in ("w_in", "w_gate", "w_lru_out", "w_attn_out", "w_o")})
    dh1, small["mix_pre_g"] = _mix_bwd2(dproj, dgate, h1, dh2, row(sm["mix_pre_g"]), wg["w_in"], wg["w_gate"], token)
    n1, da1, db1, df1, small["ffn1_post_g"] = _ffn_bwd_acts(
        dh1, x, f1, a1, b1, row(sm["ffn1_pre_g"]), row(sm["ffn1_post_g"]), wg["ffn1_w2"], "ffn1_bwd_acts")
    token = reducer.advance("mix", df1)
    big["ffn1_w1"] = _wgrad_rows(da1, n1, "dw_ffn1_w1", token)
    big["ffn1_w3"] = _wgrad_rows(db1, n1, "dw_ffn1_w3", token)
    big["ffn1_w2"] = _wgrad_rows(hm1, df1, "dw_ffn1_w2", token)
    token = reducer.begin("ffn1", {n: big[n] for n in ("ffn1_w1", "ffn1_w3", "ffn1_w2")})
    dx, small["ffn1_pre_g"] = _ffn_bwd_input(dh1, x, da1, db1, row(sm["ffn1_pre_g"]), wg["ffn1_w1"], wg["ffn1_w3"],
                                             "ffn1_bwd_input", token)
    return sq, dx, big, small


_ANY = pl.BlockSpec(memory_space=pl.ANY)


def _place():
    return lax.axis_index("x"), lax.axis_index("y"), lax.axis_index("c")


def _other_chips(x, y):
    return [(1 - x, y), (x, 1 - y), (1 - x, 1 - y)]


_HBM = pl.BlockSpec(memory_space=pltpu.HBM)
_SEM = pl.BlockSpec(memory_space=pltpu.SEMAPHORE)
_EFFECT = pltpu.SideEffectType.DATAFLOW_SIDE_EFFECTING


def _cast_into_slot(w, chip, name, after=None):
    r, cc = w.shape
    rows = r // 4

    def body(chip_ref, *refs):
        w_ref, o_ref = refs[-2:]
        o_ref[...] = w_ref[...].astype(BF16)

    extra = [] if after is None else [after]
    return pl.pallas_call(
        body, name=name, out_shape=jax.ShapeDtypeStruct((NSH, r, cc), BF16),
        grid_spec=pltpu.PrefetchScalarGridSpec(
            num_scalar_prefetch=1, grid=(4,), in_specs=[_ANY] * len(extra) + [pl.BlockSpec((rows, cc), lambda i, chip: (i, 0))],
            out_specs=pl.BlockSpec((None, rows, cc), lambda i, chip: (chip[0], i, 0))),
        compiler_params=_params("arbitrary"))(chip, *extra, w)


def _piece(ref, slot, c):
    if ref.dtype == F32:
        return ref.at[slot]
    rh = ref.shape[1] // 2
    return ref.at[slot, pl.ds(pl.multiple_of(c * rh, 16), rh), :]


def _gather_start(stages, name):
    flat = [b for stage in stages for b in stage]
    n, ns = len(flat), len(stages)

    def body(*refs):
        ins, sems, token = refs[:n], refs[n:n + 2 * ns], refs[-1]
        x, y, c = _place()
        me = 2 * x + y
        k = 0
        for s, stage in enumerate(stages):
            for i in range(len(stage)):
                for j, (px, py) in enumerate(_other_chips(x, y)):
                    piece = _piece(ins[k], me, c)
                    pltpu.make_async_remote_copy(src_ref=piece, dst_ref=piece, send_sem=sems[2 * s].at[3 * i + j],
                                                 recv_sem=sems[2 * s + 1].at[3 * i + j], device_id=(px, py, c),
                                                 device_id_type=MESH).start()
                k += 1
        token[...] = jnp.zeros_like(token)

    sem_shapes = [pltpu.SemaphoreType.DMA((3 * len(stage),)) for stage in stages for _ in range(2)]
    outs = pl.pallas_call(
        body, name=name, in_specs=[_HBM] * n,
        out_specs=[_SEM] * (2 * ns) + [_HBM] * n + [pl.BlockSpec(memory_space=pltpu.VMEM)],
        out_shape=sem_shapes + [pltpu.HBM(b.shape, b.dtype) for b in flat] + [jax.ShapeDtypeStruct((8, 128), F32)],
        input_output_aliases={i: 2 * ns + i for i in range(n)},
        compiler_params=pltpu.CompilerParams(has_side_effects=_EFFECT),
    )(*[pltpu.with_memory_space_constraint(b, pltpu.HBM) for b in flat])
    sems, bufs, token = outs[:2 * ns], list(outs[2 * ns:2 * ns + n]), outs[-1]
    per_stage, k = [], 0
    for s, stage in enumerate(stages):
        per_stage.append((sems[2 * s], sems[2 * s + 1], bufs[k:k + len(stage)]))
        k += len(stage)
    return per_stage, token


def _gather_wait(send_sems, recv_sems, bufs, after, name):
    n = len(bufs)

    def body(*refs):
        ins, ssem, rsem = refs[:n], refs[n], refs[n + 1]
        x, y, c = _place()
        me = 2 * x + y
        for i in range(n):
            for j, (px, py) in enumerate(_other_chips(x, y)):
                cp = pltpu.make_async_remote_copy(src_ref=_piece(ins[i], me, c), dst_ref=_piece(ins[i], 2 * px + py, c),
                                                  send_sem=ssem.at[3 * i + j], recv_sem=rsem.at[3 * i + j],
                                                  device_id=(px, py, c), device_id_type=MESH)
                cp.wait_send()
                cp.wait_recv()

    return pl.pallas_call(
        body, name=name, in_specs=[_HBM] * n + [_SEM, _SEM, _ANY], out_specs=[_HBM] * n,
        out_shape=[pltpu.HBM(b.shape, b.dtype) for b in bufs], input_output_aliases={i: i for i in range(n)},
        compiler_params=pltpu.CompilerParams(has_side_effects=_EFFECT),
    )(*bufs, send_sems, recv_sems, after)


def _sibling_fill(bufs, name):
    n = len(bufs)

    def body(*refs):
        ins, outs = refs[:n], refs[n:2 * n]
        send_sems, recv_sems = refs[2 * n:]
        x, y, c = _place()
        copies = []
        for i in range(n):
            for j, (px, py) in enumerate(_other_chips(x, y)):
                copies.append(pltpu.make_async_remote_copy(
                    src_ref=_piece(ins[i], 2 * px + py, c), dst_ref=_piece(outs[i], 2 * px + py, c),
                    send_sem=send_sems.at[3 * i + j], recv_sem=recv_sems.at[3 * i + j], device_id=(x, y, 1 - c),
                    device_id_type=MESH))
                copies[-1].start()
        for cp in copies:
            cp.wait()

    return pl.pallas_call(
        body, name=name, in_specs=[_ANY] * n, out_specs=[_ANY] * n,
        out_shape=[jax.ShapeDtypeStruct(b.shape, b.dtype) for b in bufs], input_output_aliases={i: i for i in range(n)},
        scratch_shapes=[pltpu.SemaphoreType.DMA((3 * n,)), pltpu.SemaphoreType.DMA((3 * n,))],
        compiler_params=pltpu.CompilerParams(has_side_effects=True),
    )(*bufs)


def _swap_plan(srcs, lands):
    x, y, c = _place()
    plan = []
    for src, land in zip(srcs, lands):
        rh = src.shape[1] // 2
        plan.append((src.at[:, pl.ds(pl.multiple_of((1 - c) * rh, 8), rh), :], land, (x, y, 1 - c)))
    return plan


def _owners_plan(srcs, lands):
    x, y, c = _place()
    return [(src.at[2 * px + py], land.at[j], (px, py, c))
            for src, land in zip(srcs, lands) for j, (px, py) in enumerate(_other_chips(x, y))]


def _exchange_start(srcs, lands, plan, copies, name):
    n = len(srcs)

    def body(*refs):
        send_sems, recv_sems, token = refs[2 * n], refs[2 * n + 1], refs[-1]
        for k, (src, dst, dev) in enumerate(plan(refs[:n], refs[n:2 * n])):
            pltpu.make_async_remote_copy(src_ref=src, dst_ref=dst, send_sem=send_sems.at[k], recv_sem=recv_sems.at[k],
                                         device_id=dev, device_id_type=MESH).start()
        token[...] = jnp.zeros_like(token)

    both = list(srcs) + list(lands)
    outs = pl.pallas_call(
        body, name=name, in_specs=[_HBM] * (2 * n),
        out_specs=[_SEM, _SEM] + [_HBM] * (2 * n) + [pl.BlockSpec(memory_space=pltpu.VMEM)],
        out_shape=[pltpu.SemaphoreType.DMA((copies,)), pltpu.SemaphoreType.DMA((copies,))]
        + [pltpu.HBM(b.shape, b.dtype) for b in both] + [jax.ShapeDtypeStruct((8, 128), F32)],
        input_output_aliases={i: 2 + i for i in range(2 * n)},
        compiler_params=pltpu.CompilerParams(has_side_effects=_EFFECT),
    )(*[pltpu.with_memory_space_constraint(b, pltpu.HBM) for b in both])
    return (outs[0], outs[1]), list(outs[2:2 + n]), list(outs[2 + n:2 + 2 * n]), outs[-1]


def _exchange_wait(sems, srcs, lands, plan, after, name):
    n = len(srcs)

    def body(*refs):
        send_sems, recv_sems = refs[2 * n], refs[2 * n + 1]
        for k, (src, dst, dev) in enumerate(plan(refs[:n], refs[n:2 * n])):
            cp = pltpu.make_async_remote_copy(src_ref=src, dst_ref=dst, send_sem=send_sems.at[k], recv_sem=recv_sems.at[k],
                                              device_id=dev, device_id_type=MESH)
            cp.wait_send()
            cp.wait_recv()

    both = list(srcs) + list(lands)
    afters = list(after) if isinstance(after, (list, tuple)) else [after]
    outs = pl.pallas_call(
        body, name=name, in_specs=[_HBM] * (2 * n) + [_SEM, _SEM] + [_ANY] * len(afters), out_specs=[_HBM] * (2 * n),
        out_shape=[pltpu.HBM(b.shape, b.dtype) for b in both], input_output_aliases={i: i for i in range(2 * n)},
        compiler_params=pltpu.CompilerParams(has_side_effects=_EFFECT),
    )(*both, sems[0], sems[1], *afters)
    return list(outs[:n]), list(outs[n:])


class _Reducer:
    def __init__(self, where):
        self.state = {}
        self.where = where

    def begin(self, stage, grads):
        names = list(grads)
        full = [grads[n] for n in names]
        lands = [lax.empty((NSH, g.shape[1] // 2, g.shape[2]), F32) for g in full]
        sems, full, lands, token = _exchange_start(full, lands, _swap_plan, len(full), "swap_start_" + stage)
        self.state[stage] = (names, sems, full, lands)
        return token

    def advance(self, stage, after):
        names, sems, full, lands = self.state[stage]
        full, got = _exchange_wait(sems, full, lands, _swap_plan, after, "swap_wait_" + stage)
        sums = [_chip_sum(g, a, self.where, "chip_sum_" + n) for n, g, a in zip(names, full, got)]
        lands = [lax.empty((3,) + s[0].shape[1:], BF16) for s in sums]
        sems, sent, lands, token = _exchange_start([s[0] for s in sums], lands, _owners_plan, 3 * len(sums),
                                                   "owners_start_" + stage)
        self.state[stage] = (names, [s[1] for s in sums], sems, sent, lands)
        return token

    def finish(self, stage, after):
        names, own, sems, sent, lands = self.state[stage]
        _, got = _exchange_wait(sems, sent, lands, _owners_plan, after, "owners_wait_" + stage)
        return {n: _owner_sum(o, g, "owner_sum_" + n) for n, o, g in zip(names, own, got)}


def _chip_sum(g, got, where, name):
    _, r, cc = g.shape
    rq = r // 4

    def body(where_ref, g_ref, got_ref, hb_ref, own_ref):
        h = g_ref[...] + got_ref[...]
        hb_ref[...] = h.astype(BF16)

        @pl.when(pl.program_id(1) == where_ref[1])
        def _():
            own_ref[...] = h

    return pl.pallas_call(
        body, name=name,
        grid_spec=pltpu.PrefetchScalarGridSpec(
            num_scalar_prefetch=1, grid=(2, NSH),
            in_specs=[pl.BlockSpec((None, rq, cc), lambda j, s, where: (s, 2 * where[0] + j, 0)),
                      pl.BlockSpec((None, rq, cc), lambda j, s, where: (s, j, 0))],
            out_specs=[pl.BlockSpec((None, rq, cc), lambda j, s, where: (s, j, 0)),
                       pl.BlockSpec((rq, cc), lambda j, s, where: (j, 0))]),
        out_shape=[jax.ShapeDtypeStruct((NSH, r // 2, cc), BF16), jax.ShapeDtypeStruct((r // 2, cc), F32)],
        compiler_params=_params("arbitrary", "arbitrary"),
    )(where, g, got)


def _owner_sum(own, got, name):
    rh, cc = own.shape
    rows = rh // 2

    def body(own_ref, got_ref, o_ref):
        o_ref[...] = ((own_ref[...] + got_ref[0].astype(F32)) + got_ref[1].astype(F32)) + got_ref[2].astype(F32)

    return pl.pallas_call(
        body, grid=(2,), name=name,
        in_specs=[pl.BlockSpec((rows, cc), lambda i: (i, 0)), pl.BlockSpec((3, rows, cc), lambda i: (0, i, 0))],
        out_specs=pl.BlockSpec((rows, cc), lambda i: (i, 0)),
        out_shape=jax.ShapeDtypeStruct((rh, cc), F32), compiler_params=_params("arbitrary"),
    )(own, got)


def _send_halves(halves, name):
    n = len(halves)

    def body(*refs):
        ins, outs = refs[:n], refs[n:2 * n]
        send_sems, recv_sems = refs[2 * n:]
        x, y, c = _place()
        copies = [pltpu.make_async_remote_copy(src_ref=ins[w], dst_ref=outs[w], send_sem=send_sems.at[w], recv_sem=recv_sems.at[w],
                                               device_id=(x, y, 1 - c), device_id_type=MESH) for w in range(n)]
        for cp in copies:
            cp.start()
        for cp in copies:
            cp.wait()

    return pl.pallas_call(
        body, name=name, in_specs=[_ANY] * n, out_specs=[_ANY] * n,
        out_shape=[jax.ShapeDtypeStruct(h.shape, F32) for h in halves],
        scratch_shapes=[pltpu.SemaphoreType.DMA((n,)), pltpu.SemaphoreType.DMA((n,))],
        compiler_params=pltpu.CompilerParams(has_side_effects=True),
    )(*halves)


def _all_reduce_small(part):
    def body(p_ref, o_ref, rbuf, send1, recv1, send2, recv2):
        x, y, c = _place()
        me = 4 * x + 2 * y + c
        peers = []
        for k in range(1, 8):
            px, py, pc = x ^ ((k >> 2) & 1), y ^ ((k >> 1) & 1), c ^ (k & 1)
            peers.append((k, (px, py, pc), 4 * px + 2 * py + pc))

        def rows(d):
            return pl.ds(pl.multiple_of(d * SMALL_SLICE, 8), SMALL_SLICE)

        first = [pltpu.make_async_remote_copy(src_ref=p_ref.at[rows(idx), :], dst_ref=rbuf.at[me], send_sem=send1.at[k],
                                              recv_sem=recv1.at[k], device_id=dev, device_id_type=MESH)
                 for k, dev, idx in peers]
        for cp in first:
            cp.start()
        rbuf[me] = p_ref[rows(me), :]
        for k, dev, idx in peers:
            pltpu.make_async_remote_copy(src_ref=p_ref.at[rows(idx), :], dst_ref=rbuf.at[idx], send_sem=send1.at[k],
                                         recv_sem=recv1.at[k], device_id=dev, device_id_type=MESH).wait_recv()
        acc = rbuf[0]
        for d in range(1, 8):
            acc = acc + rbuf[d]
        o_ref[rows(me), :] = acc
        second = [pltpu.make_async_remote_copy(src_ref=o_ref.at[rows(me), :], dst_ref=o_ref.at[rows(me), :],
                                               send_sem=send2.at[k], recv_sem=recv2.at[k], device_id=dev, device_id_type=MESH)
                  for k, dev, idx in peers]
        for cp in second:
            cp.start()
        for k, dev, idx in peers:
            pltpu.make_async_remote_copy(src_ref=o_ref.at[rows(me), :], dst_ref=o_ref.at[rows(idx), :], send_sem=send2.at[k],
                                         recv_sem=recv2.at[k], device_id=dev, device_id_type=MESH).wait_recv()
        for cp in first + second:
            cp.wait_send()

    return pl.pallas_call(
        body, name="all_reduce_small", in_specs=[_WHOLE], out_specs=_WHOLE,
        out_shape=jax.ShapeDtypeStruct((SMALL_ROWS, 128), F32),
        scratch_shapes=[pltpu.VMEM((8, SMALL_SLICE, 128), F32)] + [pltpu.SemaphoreType.DMA((8,))] * 4,
        compiler_params=pltpu.CompilerParams(has_side_effects=True),
    )(part)


def _adamw_update(w, gv, m, v):
    nm = ADAM_B1 * m + (1.0 - ADAM_B1) * gv
    nv = ADAM_B2 * v + (1.0 - ADAM_B2) * (gv * gv)
    m_hat = nm / (1.0 - ADAM_B1 ** ADAM_STEP)
    v_hat = nv / (1.0 - ADAM_B2 ** ADAM_STEP)
    return -ADAM_LR * (m_hat / (jnp.sqrt(v_hat) + ADAM_EPS) + ADAM_WD * w), nm, nv


def _adamw(w, g, m, v, name):
    rows = w.shape[0] // 4

    def body(w_ref, g_ref, m_ref, v_ref, d_ref, nm_ref, nv_ref):
        d_ref[...], nm_ref[...], nv_ref[...] = _adamw_update(w_ref[...], g_ref[...], m_ref[...], v_ref[...])

    spec = pl.BlockSpec((rows, w.shape[1]), lambda i: (i, 0))
    out = jax.ShapeDtypeStruct(w.shape, F32)
    return pl.pallas_call(body, grid=(4,), in_specs=[spec] * 4, out_specs=[spec] * 3, out_shape=[out] * 3, name=name,
                          compiler_params=_params("arbitrary"))(w, g, m, v)


def _adamw_halves(w, mine, theirs, m, v, name):
    rh, cc = mine.shape
    steps = 4
    rows = rh // steps

    def body(w_ref, mine_ref, theirs_ref, m_ref, v_ref, g_ref, d_ref, nm_ref, nv_ref):
        gv = jnp.where(pl.program_id(0) == lax.axis_index("c"), mine_ref[...], theirs_ref[...])
        g_ref[...] = gv
        d_ref[...], nm_ref[...], nv_ref[...] = _adamw_update(w_ref[...], gv, m_ref[...], v_ref[...])

    spec = pl.BlockSpec((rows, cc), lambda h, i: (steps * h + i, 0))
    half = pl.BlockSpec((rows, cc), lambda h, i: (i, 0))
    out = jax.ShapeDtypeStruct(w.shape, F32)
    return pl.pallas_call(body, grid=(2, steps), in_specs=[spec, half, half, spec, spec], out_specs=[spec] * 4,
                          out_shape=[out] * 4, name=name, compiler_params=_params("arbitrary", "arbitrary"))(w, mine, theirs, m, v)


SMALL_USED = sum(size for _, size in SMALL) // 128


def _pack_small(vals, tail=None):
    parts = []
    for name, size in SMALL:
        flat = vals[name].reshape(-1).astype(F32)
        parts.append(jnp.pad(flat, (0, size - flat.shape[0])))
    if tail is not None:
        parts.append(tail.reshape(128))
    flat = jnp.concatenate(parts)
    return jnp.pad(flat, (0, SMALL_ROWS * 128 - flat.shape[0])).reshape(SMALL_ROWS, 128)


def _unpack_small(packed, shapes):
    flat = packed.reshape(-1)
    out, off = {}, 0
    for name, size in SMALL:
        n = math.prod(shapes[name])
        out[name] = flat[off:off + n].reshape(shapes[name])
        off += size
    return out


def kernel(x, ffn1_pre_g, ffn1_w1, ffn1_w3, ffn1_w2, ffn1_post_g, mix_pre_g, w_in, conv_w, conv_b, rg_a_w, rg_a_b, rg_x_w, rg_x_b, lru_lambda, w_lru_out, attn_sinks, rel_bias, w_attn_out, w_gate, b_gate, w_o, mix_post_g, ffn2_pre_g, ffn2_w1, ffn2_w3, ffn2_w2, ffn2_post_g, loss_target, m_ffn1_pre_g, m_ffn1_w1, m_ffn1_w3, m_ffn1_w2, m_ffn1_post_g, m_mix_pre_g, m_w_in, m_conv_w, m_conv_b, m_rg_a_w, m_rg_a_b, m_rg_x_w, m_rg_x_b, m_lru_lambda, m_w_lru_out, m_attn_sinks, m_rel_bias, m_w_attn_out, m_w_gate, m_b_gate, m_w_o, m_mix_post_g, m_ffn2_pre_g, m_ffn2_w1, m_ffn2_w3, m_ffn2_w2, m_ffn2_post_g, v_ffn1_pre_g, v_ffn1_w1, v_ffn1_w3, v_ffn1_w2, v_ffn1_post_g, v_mix_pre_g, v_w_in, v_conv_w, v_conv_b, v_rg_a_w, v_rg_a_b, v_rg_x_w, v_rg_x_b, v_lru_lambda, v_w_lru_out, v_attn_sinks, v_rel_bias, v_w_attn_out, v_w_gate, v_b_gate, v_w_o, v_mix_post_g, v_ffn2_pre_g, v_ffn2_w1, v_ffn2_w3, v_ffn2_w2, v_ffn2_post_g):
    given = dict(locals())
    chip = 2 * lax.axis_index("x") + lax.axis_index("y")
    transposed = ("ffn1_w1", "ffn1_w3", "ffn2_w1", "ffn2_w3")

    def shard(name, moment=""):
        w = given[moment + name][0]
        return w.T if name in transposed else w

    def unshard(name, w):
        return (w.T if name in transposed else w)[None]

    def my_columns(a):
        own = lax.broadcasted_iota(jnp.int32, (1, 4, D), 2) // (D // NSH) == chip
        return jnp.where(own, jnp.tile(a, (1, 1, NSH)), 0.0)

    def only_my_columns(a):
        parts = a.reshape(1, 4, NSH, D // NSH)
        return sum(jnp.where(chip == s, parts[:, :, s], 0.0) for s in range(NSH))

    chip_arr = jnp.reshape(chip, (1,)).astype(jnp.int32)
    stage_names = {"ffn1": ["ffn1_w1", "ffn1_w3", "ffn1_w2", "conv_w"],
                   "mix": ["w_in", "w_gate", "w_lru_out", "w_attn_out", "w_o"],
                   "ffn2": ["ffn2_w1", "ffn2_w3", "ffn2_w2"]}
    in_flight, started = {}, None
    for stage, names in stage_names.items():
        bufs = [jnp.where(lax.broadcasted_iota(jnp.int32, (NSH, 4, D // NSH), 0) == chip, given[n], 0.0) if n == "conv_w"
                else _cast_into_slot(shard(n), chip_arr, "cast_" + n, started) for n in names]
        (in_flight[stage],), started = _gather_start([bufs], "gather_start_" + stage)
    all_started = started

    def weights(stage, after):
        names = stage_names[stage]
        send_sems, recv_sems, landing = in_flight[stage]
        if stage == "ffn1":
            after = all_started
        landed = _gather_wait(send_sems, recv_sems, landing, after, "gather_wait_" + stage)
        halves = [b for b in landed if b.dtype == BF16]
        out = dict(zip([n for n, b in zip(names, landed) if b.dtype == BF16], _sibling_fill(halves, "sibling_fill_" + stage)))
        if "conv_w" in names:
            out["conv_w"] = jnp.transpose(landed[names.index("conv_w")], (1, 0, 2)).reshape(4, D)
        return out

    small_shapes = {n: given[n].shape for n, _ in SMALL}
    small_shapes["conv_w"] = (1, 4, D)
    sm = {n: (given[n][0] if given[n].shape[0] == 1 and n != "rel_bias" else given[n]) for n, _ in SMALL if n != "conv_w"}

    reducer = _Reducer(jnp.stack([lax.axis_index("c"), chip]).astype(jnp.int32))
    sq, dx, _, small = _local_step(x[0], loss_target[0], weights, sm, reducer)

    reducer.advance("ffn1", dx)
    reduced_small = _all_reduce_small(_pack_small(small, tail=sq))
    loss = reduced_small[SMALL_USED, 0] * (0.5 / D)
    small_g = _unpack_small(reduced_small, small_shapes)
    grads, delta, new_m, new_v = {}, {}, {}, {}
    after = [reduced_small]
    for stage in ("ffn2", "mix", "ffn1"):
        halves = reducer.finish(stage, after)
        from_sibling = _send_halves(list(halves.values()), "send_halves_" + stage)
        for (n, mine), theirs in zip(halves.items(), from_sibling):
            grads[n], delta[n], new_m[n], new_v[n] = (unshard(n, r) for r in _adamw_halves(
                shard(n), mine, theirs, shard(n, "m_"), shard(n, "v_"), "adamw_" + n))
            after.append(new_v[n])

    packed = [_pack_small({n: (my_columns(given[pre + n]) if n == "conv_w" else given[pre + n]) for n, _ in SMALL})
              for pre in ("", "m_", "v_")]
    outs = _adamw(packed[0], reduced_small, packed[1], packed[2], "adamw_small")
    for dst, arr in zip((delta, new_m, new_v), outs):
        dst.update(_unpack_small(arr, small_shapes))
    small_out = dict(small_g)
    for d in (small_out, delta, new_m, new_v):
        d["conv_w"] = only_my_columns(d["conv_w"])
    grads.update(small_out)
    return (loss, dx[None], *[grads[n] for n in WEIGHTS], *[delta[n] for n in WEIGHTS], *[new_m[n] for n in WEIGHTS],
            *[new_v[n] for n in WEIGHTS])
```

```python
import functools
import math

import jax
import jax.numpy as jnp
from jax import lax
from jax.experimental import pallas as pl
from jax.experimental.pallas import tpu as pltpu

F32, BF16 = jnp.float32, jnp.bfloat16
D = 1024
NSH = 4
FF_S = 704
IN_S = 896
GATE_S = 512
KV_W = 256
CHUNK = 64
KB = 192
N_HEADS = 16
HEAD_DIM = 64
N_BUCKETS = 32
KP = 192
PAD_KEYS = 128
RMS_EPS = 1e-6
NEG_INF = -1e30
LRU_C = 8.0
TM = 512
TM_SCAN = 256
VMEM_LIMIT = 56 * 1024 * 1024
ADAM_LR, ADAM_B1, ADAM_B2, ADAM_EPS, ADAM_WD, ADAM_STEP = 0.001, 0.9, 0.999, 1e-08, 0.01, 10
SMALL_ROWS = 1216
SMALL_SLICE = SMALL_ROWS // 8
MESH = pl.DeviceIdType.MESH

BIG = ["ffn1_w1", "ffn1_w3", "ffn1_w2", "w_in", "w_lru_out", "w_attn_out", "w_gate", "w_o", "ffn2_w1", "ffn2_w3", "ffn2_w2"]
SMALL = [("ffn1_pre_g", 1024), ("ffn1_post_g", 1024), ("mix_pre_g", 1024), ("conv_w", 4096), ("conv_b", 1024),
         ("rg_a_w", 65536), ("rg_a_b", 1024), ("rg_x_w", 65536), ("rg_x_b", 1024), ("lru_lambda", 1024),
         ("attn_sinks", 1024), ("rel_bias", 1024), ("b_gate", 2048), ("mix_post_g", 1024), ("ffn2_pre_g", 1024),
         ("ffn2_post_g", 1024)]
WEIGHTS = ["ffn1_pre_g", "ffn1_w1", "ffn1_w3", "ffn1_w2", "ffn1_post_g", "mix_pre_g", "w_in", "conv_w", "conv_b", "rg_a_w",
           "rg_a_b", "rg_x_w", "rg_x_b", "lru_lambda", "w_lru_out", "attn_sinks", "rel_bias", "w_attn_out", "w_gate", "b_gate",
           "w_o", "mix_post_g", "ffn2_pre_g", "ffn2_w1", "ffn2_w3", "ffn2_w2", "ffn2_post_g"]


def _params(*sem):
    return pltpu.CompilerParams(dimension_semantics=sem or None, vmem_limit_bytes=VMEM_LIMIT)


def _nn(a, b):
    return jnp.dot(a, b, preferred_element_type=F32)


def _nt(a, b):
    return lax.dot_general(a, b, (((1,), (1,)), ((), ())), preferred_element_type=F32)


def _tn(a, b):
    return lax.dot_general(a, b, (((0,), (0,)), ((), ())), preferred_element_type=F32)


def _rms(x, g):
    rstd = lax.rsqrt(jnp.mean(x * x, axis=-1, keepdims=True) + RMS_EPS)
    return (x * rstd) * g


def _rms_bwd(dout, x, g):
    rstd = lax.rsqrt(jnp.mean(x * x, axis=-1, keepdims=True) + RMS_EPS)
    xhat = x * rstd
    dg = jnp.sum(dout * xhat, axis=0, keepdims=True)
    dxhat = dout * g
    dx = rstd * (dxhat - xhat * jnp.mean(dxhat * xhat, axis=-1, keepdims=True))
    return dx, dg


_GELU_K = math.sqrt(2.0 / math.pi)


def _gelu(x):
    return x * (0.5 * (1.0 + jnp.tanh(_GELU_K * (x + 0.044715 * (x * x * x)))))


def _gelu_grad(x):
    t = jnp.tanh(_GELU_K * (x + 0.044715 * (x * x * x)))
    return 0.5 * (1.0 + t) + x * (0.5 * (1.0 - t * t) * (_GELU_K * (1.0 + 3.0 * 0.044715 * (x * x))))


def _softplus_neg(lam):
    z = -lam
    u = jnp.exp(-jnp.abs(z))
    w = 1.0 + u
    log1p_u = jnp.where(w == 1.0, u, jnp.log(w) * (u / (w - 1.0)))
    return jnp.maximum(z, 0.0) + log1p_u


def _lru_coeffs(r, sp):
    log_a = (-LRU_C * r) * sp
    a = jnp.exp(log_a)
    t = jnp.tanh(log_a)
    s = jnp.sqrt(-2.0 * t / (1.0 - t))
    return a, s


def _row_spec(tm, width):
    return pl.BlockSpec((tm, width), lambda i: (i, 0))


def _vec_spec(width):
    return pl.BlockSpec((1, width), lambda i: (0, 0))


_WHOLE = pl.BlockSpec(memory_space=pltpu.VMEM)


def _tile(t, tm=TM):
    return min(tm, t)


def _ffn_fwd(x, gpre, w1g, w3g, w2g, gpost, name, target=None):
    t = x.shape[0]
    tm = _tile(t)
    last = target is not None

    def body(x_ref, gpre_ref, w1_ref, w3_ref, w2_ref, gpost_ref, *refs):
        t_ref, (h_ref, a_ref, b_ref, hm_ref, f_ref), l_ref = (refs[0] if last else None), refs[last:last + 5], refs[-1]
        xv = x_ref[...]
        nb = _rms(xv, gpre_ref[...]).astype(BF16)
        f = jnp.zeros((tm, D), F32)
        for s in range(NSH):
            a = _nt(nb, w1_ref[s])
            b = _nt(nb, w3_ref[s])
            hmb = ((a * jax.nn.sigmoid(a)) * b).astype(BF16)
            a_ref[s] = a.astype(BF16)
            b_ref[s] = b.astype(BF16)
            hm_ref[s] = hmb
            f = f + _nn(hmb, w2_ref[s])
        f_ref[...] = f
        h = xv + 0.5 * _rms(f, gpost_ref[...])
        if last:
            @pl.when(pl.program_id(0) == 0)
            def _():
                l_ref[...] = jnp.zeros_like(l_ref)

            e = h - t_ref[...]
            h_ref[...] = e * (1.0 / D)
            l_ref[...] += jnp.sum(jnp.sum(e * e, axis=0, keepdims=True), axis=1, keepdims=True)
        else:
            h_ref[...] = h

    sh = pl.BlockSpec((NSH, tm, FF_S), lambda i: (0, i, 0))
    act = jax.ShapeDtypeStruct((NSH, t, FF_S), BF16)
    return pl.pallas_call(
        body, grid=(t // tm,), name=name,
        in_specs=[_row_spec(tm, D), _vec_spec(D), _WHOLE, _WHOLE, _WHOLE, _vec_spec(D)] + [_row_spec(tm, D)] * last,
        out_specs=[_row_spec(tm, D), sh, sh, sh, _row_spec(tm, D)] + [pl.BlockSpec((1, 128), lambda i: (0, 0))] * last,
        out_shape=[jax.ShapeDtypeStruct((t, D), F32), act, act, act, jax.ShapeDtypeStruct((t, D), F32)]
        + [jax.ShapeDtypeStruct((1, 128), F32)] * last,
        compiler_params=_params("arbitrary"),
    )(x, gpre, w1g, w3g, w2g, gpost, *([target] if last else []))


def _mix_proj(h1, gmix, w_in_g, w_gate_g, b_gate):
    t = h1.shape[0]
    tm = _tile(t)

    def body(h_ref, g_ref, win_ref, wg_ref, bg_ref, u_ref, q_ref, k_ref, v_ref, xr_ref, xg_ref, gate_ref):
        ub = _rms(h_ref[...], g_ref[...]).astype(BF16)
        u_ref[...] = ub
        p0 = _nn(ub, win_ref[0])
        q_ref[:, 0:896] = p0.astype(BF16)
        p1 = _nn(ub, win_ref[1])
        q_ref[:, 896:1024] = p1[:, 0:128].astype(BF16)
        k_ref[...] = p1[:, 128:384].astype(BF16)
        v_ref[...] = p1[:, 384:640].astype(BF16)
        xr_ref[:, 0:256] = p1[:, 640:896]
        p2 = _nn(ub, win_ref[2])
        xr_ref[:, 256:1024] = p2[:, 0:768]
        xg_ref[:, 0:128] = p2[:, 768:896]
        xg_ref[:, 128:1024] = _nn(ub, win_ref[3])
        for s in range(NSH):
            sl = slice(s * GATE_S, (s + 1) * GATE_S)
            gate_ref[:, sl] = jax.nn.sigmoid(_nn(ub, wg_ref[s]) + bg_ref[:, sl])

    return pl.pallas_call(
        body, grid=(t // tm,), name="mix_proj",
        in_specs=[_row_spec(tm, D), _vec_spec(D), _WHOLE, _WHOLE, _vec_spec(2 * D)],
        out_specs=[_row_spec(tm, D), _row_spec(tm, D), _row_spec(tm, KV_W), _row_spec(tm, KV_W), _row_spec(tm, D),
                   _row_spec(tm, D), _row_spec(tm, 2 * D)],
        out_shape=[jax.ShapeDtypeStruct((t, D), BF16), jax.ShapeDtypeStruct((t, D), BF16),
                   jax.ShapeDtypeStruct((t, KV_W), BF16), jax.ShapeDtypeStruct((t, KV_W), BF16),
                   jax.ShapeDtypeStruct((t, D), F32), jax.ShapeDtypeStruct((t, D), F32),
                   jax.ShapeDtypeStruct((t, 2 * D), F32)],
        compiler_params=_params("arbitrary"),
    )(h1, gmix, w_in_g, w_gate_g, b_gate)


def _rglru_fwd(xr, xg, conv_w, conv_b, wa2, ba, wx2, bx, lam):
    t = xr.shape[0]
    tm = _tile(t, TM_SCAN)
    nb8 = tm // 8

    def body(xr_ref, xrp_ref, xg_ref, cw_ref, cb_ref, wa_ref, ba_ref, wx_ref, bx_ref, lam_ref,
             hr_ref, yain_ref, xc_ref, r_ref, ig_ref, ext, a_sc, h_sc):
        i = pl.program_id(0)

        @pl.when(i == 0)
        def _():
            h_sc[...] = jnp.zeros_like(h_sc)

        ext[0:8, :] = jnp.where(i == 0, 0.0, xrp_ref[...])
        ext[8:8 + tm, :] = xr_ref[...]
        xc = jnp.broadcast_to(cb_ref[...], (tm, D))
        for tap in range(4):
            xc = xc + ext[pl.ds(5 + tap, tm), :] * cw_ref[tap:tap + 1, :]
        xc_ref[...] = xc
        xcb = xc.astype(BF16)
        for p in range(8):
            sl = slice(p * 128, (p + 1) * 128)
            r_ref[:, sl] = jax.nn.sigmoid(_nn(xcb[:, sl], wa_ref[p]) + ba_ref[:, sl])
            ig_ref[:, sl] = jax.nn.sigmoid(_nn(xcb[:, sl], wx_ref[p]) + bx_ref[:, sl])
        a, s = _lru_coeffs(r_ref[...], _softplus_neg(lam_ref[...]))
        a_sc[...] = a
        hr_ref[...] = s * (ig_ref[...] * xc)

        def blk(j, h):
            st = pl.multiple_of(j * 8, 8)
            a8 = a_sc[pl.ds(st, 8), :]
            u8 = hr_ref[pl.ds(st, 8), :]
            rows = []
            for k in range(8):
                h = a8[k:k + 1, :] * h + u8[k:k + 1, :]
                rows.append(h)
            hr_ref[pl.ds(st, 8), :] = jnp.concatenate(rows, axis=0)
            return h

        h_sc[0:1, :] = lax.fori_loop(0, nb8, blk, h_sc[0:1, :])
        yain_ref[...] = (hr_ref[...] * _gelu(xg_ref[...])).astype(BF16)

    prev = pl.BlockSpec((8, D), lambda i: (jnp.maximum(i * nb8 - 1, 0), 0))
    full = lambda shape: pl.BlockSpec(shape, lambda i: tuple(0 for _ in shape))
    f32 = jax.ShapeDtypeStruct((t, D), F32)
    return pl.pallas_call(
        body, grid=(t // tm,), name="rglru_fwd",
        in_specs=[_row_spec(tm, D), prev, _row_spec(tm, D), full((4, D)), _vec_spec(D), full((8, 128, 128)), _vec_spec(D),
                  full((8, 128, 128)), _vec_spec(D), _vec_spec(D)],
        out_specs=[_row_spec(tm, D)] * 5,
        out_shape=[f32, jax.ShapeDtypeStruct((t, D), BF16), f32, f32, f32],
        scratch_shapes=[pltpu.VMEM((tm + 8, D), F32), pltpu.VMEM((tm, D), F32), pltpu.VMEM((8, D), F32)],
        compiler_params=_params("arbitrary"),
    )(xr, xr, xg, conv_w, conv_b, wa2, ba, wx2, bx, lam)


def _bias_fwd(table_t, onehot_t):
    def body(t_ref, e_ref, o_ref):
        o_ref[...] = jnp.dot(t_ref[...], e_ref[...], preferred_element_type=F32, precision=lax.Precision.HIGHEST)

    return pl.pallas_call(body, out_shape=jax.ShapeDtypeStruct((N_HEADS, CHUNK * KB), F32), name="bias_fwd",
                          compiler_params=_params())(table_t, onehot_t)


def _bias_bwd(dbias_flat, onehot_t, ds_rows):
    def body(d_ref, e_ref, s_ref, o_ref, so_ref):
        o_ref[...] = lax.dot_general(d_ref[...], e_ref[...], (((1,), (1,)), ((), ())), preferred_element_type=F32,
                                     precision=lax.Precision.HIGHEST)
        so_ref[...] = jnp.zeros_like(so_ref)
        for r in range(4):
            so_ref[:, r:r + 1] = jnp.sum(s_ref[:, r * CHUNK:(r + 1) * CHUNK], axis=1, keepdims=True)

    return pl.pallas_call(body, out_shape=[jax.ShapeDtypeStruct((N_HEADS, N_BUCKETS), F32), jax.ShapeDtypeStruct((8, 128), F32)],
                          name="bias_bwd", compiler_params=_params())(dbias_flat, onehot_t, ds_rows)


def _stack_heads(q):
    return jnp.concatenate(
        [jnp.concatenate([q[:, (4 * g + r) * HEAD_DIM:(4 * g + r + 1) * HEAD_DIM] for g in range(4)], axis=1)
         for r in range(4)], axis=0)


def _unstack_heads(o):
    return jnp.concatenate([o[r * CHUNK:(r + 1) * CHUNK, g * HEAD_DIM:(g + 1) * HEAD_DIM] for g in range(4) for r in range(4)],
                           axis=1)


def _block_diag(w, mask):
    return jnp.concatenate([w] * 4, axis=0) * mask


def _group_softmax(qk, bias_g, sink, valid):
    s = qk * (HEAD_DIM ** -0.5) + bias_g
    s = jnp.where(valid, s, NEG_INF)
    m = jnp.maximum(jnp.max(s, axis=0, keepdims=True), sink)
    e = jnp.exp(s - m)
    es = jnp.exp(sink - m)
    inv = 1.0 / (jnp.sum(e, axis=0, keepdims=True) + es)
    return e * inv, es * inv


def _attn_fwd(sink_rows, q, kp, vp, bias_t, mask):
    t = q.shape[0]

    def body(sink_ref, q_ref, kp_ref, vp_ref, bias_ref, mask_ref, o_ref):
        c = pl.program_id(0)
        st = pl.multiple_of(c * CHUNK, CHUNK)
        kw = kp_ref[pl.ds(st, KP), :]
        vw = vp_ref[pl.ds(st, KP), :]
        q_all = _stack_heads(q_ref[...])
        valid = lax.broadcasted_iota(jnp.int32, (KP, 1), 0) + c * CHUNK >= PAD_KEYS
        owns = [mask_ref[g * KP:(g + 1) * KP, :] for g in range(4)]
        scores = [_nt(kw * owns[g], q_all) for g in range(4)]
        ps = [_group_softmax(scores[g], bias_ref[g * KP:(g + 1) * KP, :], sink_ref[g:g + 1, :], valid)[0] for g in range(4)]
        o_all = sum(_tn(ps[g].astype(BF16), vw * owns[g]) for g in range(4))
        o_ref[...] = _unstack_heads(o_all).astype(BF16)

    return pl.pallas_call(
        body, grid=(t // CHUNK,), name="attn_fwd",
        in_specs=[_WHOLE, _row_spec(CHUNK, D), _WHOLE, _WHOLE, _WHOLE, _WHOLE],
        out_specs=_row_spec(CHUNK, D),
        out_shape=jax.ShapeDtypeStruct((t, D), BF16),
        compiler_params=_params("arbitrary"),
    )(sink_rows, q, kp, vp, bias_t, mask)


def _merge_fwd(yain, o, gate, h1, w_lru, w_att, w_o, gpost):
    t = h1.shape[0]
    tm = _tile(t)

    def body(ya_ref, o_ref, g_ref, h_ref, wl_ref, wa_ref, wo_ref, gp_ref, h2_ref, mo_ref, mg_ref, ya_out, yb_out):
        ya = _nn(ya_ref[...], wl_ref[...])
        yb = _nn(o_ref[...], wa_ref[...])
        mg = (g_ref[:, 0:D] * ya + g_ref[:, D:2 * D] * yb).astype(BF16)
        mo = _nn(mg, wo_ref[...])
        ya_out[...] = ya.astype(BF16)
        yb_out[...] = yb.astype(BF16)
        mg_ref[...] = mg
        mo_ref[...] = mo
        h2_ref[...] = h_ref[...] + _rms(mo, gp_ref[...])

    f32 = jax.ShapeDtypeStruct((t, D), F32)
    b16 = jax.ShapeDtypeStruct((t, D), BF16)
    return pl.pallas_call(
        body, grid=(t // tm,), name="merge_fwd",
        in_specs=[_row_spec(tm, D), _row_spec(tm, D), _row_spec(tm, 2 * D), _row_spec(tm, D), _WHOLE, _WHOLE, _WHOLE,
                  _vec_spec(D)],
        out_specs=[_row_spec(tm, D)] * 5,
        out_shape=[f32, f32, b16, b16, b16],
        compiler_params=_params("arbitrary"),
    )(yain, o, gate, h1, w_lru, w_att, w_o, gpost)


def _ffn_bwd(dh, x, f, a, b, gpre, gpost, w1g, w3g, w2g, name):
    t = x.shape[0]
    tm = _tile(t, TM_SCAN)

    def body(dh_ref, x_ref, f_ref, a_ref, b_ref, gpre_ref, gpost_ref, w1_ref, w3_ref, w2_ref,
             dx_ref, n_ref, da_ref, db_ref, df_ref, dgpre_ref, dgpost_ref):
        @pl.when(pl.program_id(0) == 0)
        def _():
            dgpre_ref[...] = jnp.zeros_like(dgpre_ref)
            dgpost_ref[...] = jnp.zeros_like(dgpost_ref)

        dhv = dh_ref[...]
        xv = x_ref[...]
        df, dgp = _rms_bwd(0.5 * dhv, f_ref[...], gpost_ref[...])
        dgpost_ref[...] += dgp
        dfb = df.astype(BF16)
        df_ref[...] = dfb
        n_ref[...] = _rms(xv, gpre_ref[...]).astype(BF16)
        dn = jnp.zeros((tm, D), F32)
        for s in range(NSH):
            av = a_ref[s].astype(F32)
            bv = b_ref[s].astype(F32)
            sg = jax.nn.sigmoid(av)
            dhm = _nt(dfb, w2_ref[s])
            dab = (dhm * bv * (sg * (1.0 + av * (1.0 - sg)))).astype(BF16)
            dbb = (dhm * (av * sg)).astype(BF16)
            da_ref[s] = dab
            db_ref[s] = dbb
            dn = dn + _nn(dab, w1_ref[s]) + _nn(dbb, w3_ref[s])
        dxn, dg = _rms_bwd(dn, xv, gpre_ref[...])
        dgpre_ref[...] += dg
        dx_ref[...] = dhv + dxn

    sh = pl.BlockSpec((NSH, tm, FF_S), lambda i: (0, i, 0))
    act = jax.ShapeDtypeStruct((NSH, t, FF_S), BF16)
    vec = jax.ShapeDtypeStruct((1, D), F32)
    return pl.pallas_call(
        body, grid=(t // tm,), name=name,
        in_specs=[_row_spec(tm, D), _row_spec(tm, D), _row_spec(tm, D), sh, sh, _vec_spec(D), _vec_spec(D), _WHOLE, _WHOLE,
                  _WHOLE],
        out_specs=[_row_spec(tm, D), _row_spec(tm, D), sh, sh, _row_spec(tm, D), _vec_spec(D), _vec_spec(D)],
        out_shape=[jax.ShapeDtypeStruct((t, D), F32), jax.ShapeDtypeStruct((t, D), BF16), act, act,
                   jax.ShapeDtypeStruct((t, D), BF16), vec, vec],
        compiler_params=_params("arbitrary"),
    )(dh, x, f, a, b, gpre, gpost, w1g, w3g, w2g)


def _behind(body, after):
    if after is None:
        return body, [], []

    def ordered(_, *refs):
        body(*refs)

    return ordered, [_ANY], [after]


def _ffn_bwd_acts(dh, x, f, a, b, gpre, gpost, w2g, name):
    t = x.shape[0]
    tm = _tile(t)

    def body(dh_ref, x_ref, f_ref, a_ref, b_ref, gpre_ref, gpost_ref, w2_ref, n_ref, da_ref, db_ref, df_ref, dgpost_ref):
        @pl.when(pl.program_id(0) == 0)
        def _():
            dgpost_ref[...] = jnp.zeros_like(dgpost_ref)

        df, dgp = _rms_bwd(0.5 * dh_ref[...], f_ref[...], gpost_ref[...])
        dgpost_ref[...] += dgp
        dfb = df.astype(BF16)
        df_ref[...] = dfb
        n_ref[...] = _rms(x_ref[...], gpre_ref[...]).astype(BF16)
        for s in range(NSH):
            av = a_ref[s].astype(F32)
            bv = b_ref[s].astype(F32)
            sg = jax.nn.sigmoid(av)
            dhm = _nt(dfb, w2_ref[s])
            da_ref[s] = (dhm * bv * (sg * (1.0 + av * (1.0 - sg)))).astype(BF16)
            db_ref[s] = (dhm * (av * sg)).astype(BF16)

    sh = pl.BlockSpec((NSH, tm, FF_S), lambda i: (0, i, 0))
    act = jax.ShapeDtypeStruct((NSH, t, FF_S), BF16)
    b16 = jax.ShapeDtypeStruct((t, D), BF16)
    return pl.pallas_call(
        body, grid=(t // tm,), name=name,
        in_specs=[_row_spec(tm, D), _row_spec(tm, D), _row_spec(tm, D), sh, sh, _vec_spec(D), _vec_spec(D), _WHOLE],
        out_specs=[_row_spec(tm, D), sh, sh, _row_spec(tm, D), _vec_spec(D)],
        out_shape=[b16, act, act, b16, jax.ShapeDtypeStruct((1, D), F32)],
        compiler_params=_params("arbitrary"),
    )(dh, x, f, a, b, gpre, gpost, w2g)


def _ffn_bwd_input(dh, x, da, db, gpre, w1g, w3g, name, after):
    t = x.shape[0]
    tm = _tile(t)

    def body(dh_ref, x_ref, da_ref, db_ref, gpre_ref, w1_ref, w3_ref, dx_ref, dgpre_ref):
        @pl.when(pl.program_id(0) == 0)
        def _():
            dgpre_ref[...] = jnp.zeros_like(dgpre_ref)

        dn = jnp.zeros((tm, D), F32)
        for s in range(NSH):
            dn = dn + _nn(da_ref[s], w1_ref[s]) + _nn(db_ref[s], w3_ref[s])
        dxn, dg = _rms_bwd(dn, x_ref[...], gpre_ref[...])
        dgpre_ref[...] += dg
        dx_ref[...] = dh_ref[...] + dxn

    sh = pl.BlockSpec((NSH, tm, FF_S), lambda i: (0, i, 0))
    body, specs, operands = _behind(body, after)
    return pl.pallas_call(
        body, grid=(t // tm,), name=name,
        in_specs=specs + [_row_spec(tm, D), _row_spec(tm, D), sh, sh, _vec_spec(D), _WHOLE, _WHOLE],
        out_specs=[_row_spec(tm, D), _vec_spec(D)],
        out_shape=[jax.ShapeDtypeStruct((t, D), F32), jax.ShapeDtypeStruct((1, D), F32)],
        compiler_params=_params("arbitrary"),
    )(*operands, dh, x, da, db, gpre, w1g, w3g)


def _wgrad(a, b, a_spec, b_spec, out_spec, out_shape, grid, name, after=None):
    def body(a_ref, b_ref, o_ref):
        o_ref[...] = _tn(a_ref[...], b_ref[...]).astype(BF16)

    body, specs, operands = _behind(body, after)
    return pl.pallas_call(body, grid=grid, name=name, in_specs=specs + [a_spec, b_spec], out_specs=out_spec,
                          out_shape=jax.ShapeDtypeStruct(out_shape, BF16),
                          compiler_params=_params(*("arbitrary",) * len(grid)))(*operands, a, b)


def _wgrad_cols(act, dsh, width, name, after=None):
    t = act.shape[0]
    if dsh.ndim == 3:
        b_spec = pl.BlockSpec((None, t, width), lambda s, k: (s, 0, 0))
    else:
        b_spec = pl.BlockSpec((t, width), lambda s, k: (0, s))
    return _wgrad(act, dsh, pl.BlockSpec((t, 512), lambda s, k: (0, k)), b_spec,
                  pl.BlockSpec((None, 512, width), lambda s, k: (s, k, 0)), (NSH, D, width), (NSH, 2), name, after)


def _wgrad_rows(hm, df, name, after=None):
    t = df.shape[0]
    return _wgrad(hm, df, pl.BlockSpec((None, t, FF_S), lambda s: (s, 0, 0)), pl.BlockSpec((t, D), lambda s: (0, 0)),
                  pl.BlockSpec((None, FF_S, D), lambda s: (s, 0, 0)), (NSH, FF_S, D), (NSH,), name, after)


def _wgrad_sq(a, b, name, after=None):
    t = a.shape[0]
    return _wgrad(a, b, pl.BlockSpec((t, 512), lambda i, j: (0, i)), pl.BlockSpec((t, 512), lambda i, j: (0, j)),
                  pl.BlockSpec((512, 512), lambda i, j: (i, j)), (D, D), (2, 2), name, after)


def _mix_bwd1(dh2, mo, gpost, gate, ya, yb, xg, hr, w_o, w_lru, w_att, after):
    t = dh2.shape[0]
    tm = _tile(t, TM_SCAN)

    def body(dh_ref, mo_ref, gp_ref, g_ref, ya_ref, yb_ref, xg_ref, hr_ref, wo_ref, wl_ref, wa_ref,
             dmo_ref, dya_ref, dyb_ref, dgate_ref, dhr_ref, dxg_ref, do_ref, dgp_ref, dbg_ref):
        @pl.when(pl.program_id(0) == 0)
        def _():
            dgp_ref[...] = jnp.zeros_like(dgp_ref)
            dbg_ref[...] = jnp.zeros_like(dbg_ref)

        dmo, dgp = _rms_bwd(dh_ref[...], mo_ref[...], gp_ref[...])
        dgp_ref[...] += dgp
        dmob = dmo.astype(BF16)
        dmo_ref[...] = dmob
        dm = _nt(dmob, wo_ref[...])
        g0 = g_ref[:, 0:D]
        g1 = g_ref[:, D:2 * D]
        dyab = (dm * g0).astype(BF16)
        dybb = (dm * g1).astype(BF16)
        dya_ref[...] = dyab
        dyb_ref[...] = dybb
        dg0 = dm * ya_ref[...].astype(F32) * (g0 * (1.0 - g0))
        dg1 = dm * yb_ref[...].astype(F32) * (g1 * (1.0 - g1))
        dgate_ref[:, 0:D] = dg0.astype(BF16)
        dgate_ref[:, D:2 * D] = dg1.astype(BF16)
        dbg_ref[:, 0:D] += jnp.sum(dg0, axis=0, keepdims=True)
        dbg_ref[:, D:2 * D] += jnp.sum(dg1, axis=0, keepdims=True)
        dyain = _nt(dyab, wl_ref[...])
        do_ref[...] = _nt(dybb, wa_ref[...]).astype(BF16)
        xgv = xg_ref[...]
        dhr_ref[...] = dyain * _gelu(xgv)
        dxg_ref[...] = (dyain * hr_ref[...] * _gelu_grad(xgv)).astype(BF16)

    b16 = jax.ShapeDtypeStruct((t, D), BF16)
    body, specs, operands = _behind(body, after)
    return pl.pallas_call(
        body, grid=(t // tm,), name="mix_bwd1",
        in_specs=specs + [_row_spec(tm, D), _row_spec(tm, D), _vec_spec(D), _row_spec(tm, 2 * D), _row_spec(tm, D),
                          _row_spec(tm, D), _row_spec(tm, D), _row_spec(tm, D), _WHOLE, _WHOLE, _WHOLE],
        out_specs=[_row_spec(tm, D), _row_spec(tm, D), _row_spec(tm, D), _row_spec(tm, 2 * D), _row_spec(tm, D),
                   _row_spec(tm, D), _row_spec(tm, D), _vec_spec(D), _vec_spec(2 * D)],
        out_shape=[b16, b16, b16, jax.ShapeDtypeStruct((t, 2 * D), BF16), jax.ShapeDtypeStruct((t, D), F32), b16, b16,
                   jax.ShapeDtypeStruct((1, D), F32), jax.ShapeDtypeStruct((1, 2 * D), F32)],
        compiler_params=_params("arbitrary"),
    )(*operands, dh2, mo, gpost, gate, ya, yb, xg, hr, w_o, w_lru, w_att)


def _rglru_bwd(dhr, hr, xc, r, ig, xr, conv_w, wa2, wx2, lam, after):
    t = dhr.shape[0]
    tm = _tile(t, TM_SCAN)
    nb8 = tm // 8
    nt = t // tm

    def body(dhr_ref, hr_ref, hrp_ref, xc_ref, r_ref, ig_ref, xr_ref, xrp_ref, cw_ref, wa_ref, wx_ref, lam_ref,
             dxr_ref, dwa_ref, dwx_ref, dba_ref, dbx_ref, dlam_ref, dcw_ref, dcb_ref,
             ext_h, ext_x, ext_d, a_sc, g_sc, c_sc, nxt_sc):
        i = pl.program_id(0)
        first_tile = i == nt - 1

        @pl.when(i == 0)
        def _():
            c_sc[...] = jnp.zeros_like(c_sc)
            nxt_sc[...] = jnp.zeros_like(nxt_sc)
            for ref in (dwa_ref, dwx_ref, dba_ref, dbx_ref, dlam_ref, dcw_ref, dcb_ref):
                ref[...] = jnp.zeros_like(ref)

        lamv = lam_ref[...]
        sp = _softplus_neg(lamv)
        rv = r_ref[...]
        igv = ig_ref[...]
        xcv = xc_ref[...]
        a, s = _lru_coeffs(rv, sp)
        a_sc[...] = a

        def blk(jj, c):
            st = pl.multiple_of((nb8 - 1 - jj) * 8, 8)
            d8 = dhr_ref[pl.ds(st, 8), :]
            a8 = a_sc[pl.ds(st, 8), :]
            rows = [None] * 8
            for k in range(7, -1, -1):
                g = d8[k:k + 1, :] + c
                c = a8[k:k + 1, :] * g
                rows[k] = g
            g_sc[pl.ds(st, 8), :] = jnp.concatenate(rows, axis=0)
            return c

        c_sc[0:1, :] = lax.fori_loop(0, nb8, blk, c_sc[0:1, :])
        g = g_sc[...]
        ext_h[0:8, :] = jnp.where(first_tile, 0.0, hrp_ref[...])
        ext_h[8:8 + tm, :] = hr_ref[...]
        hprev = ext_h[pl.ds(7, tm), :]
        d_s = g * (igv * xcv)
        dig = g * s * xcv
        dxc = g * s * igv
        dla = (g * hprev) * a - d_s * ((a * a) / s)
        dr_pre = (dla * (-LRU_C * sp)) * (rv * (1.0 - rv))
        di_pre = dig * (igv * (1.0 - igv))
        dlam_ref[...] += jnp.sum(dla * (LRU_C * rv), axis=0, keepdims=True) * jax.nn.sigmoid(-lamv)
        dba_ref[...] += jnp.sum(dr_pre, axis=0, keepdims=True)
        dbx_ref[...] += jnp.sum(di_pre, axis=0, keepdims=True)
        drb = dr_pre.astype(BF16)
        dib = di_pre.astype(BF16)
        xcb = xcv.astype(BF16)
        ext_d[tm:tm + 8, :] = nxt_sc[...]
        for p in range(8):
            sl = slice(p * 128, (p + 1) * 128)
            ext_d[0:tm, sl] = dxc[:, sl] + _nt(drb[:, sl], wa_ref[p]) + _nt(dib[:, sl], wx_ref[p])
            dwa_ref[p] += _tn(xcb[:, sl], drb[:, sl])
            dwx_ref[p] += _tn(xcb[:, sl], dib[:, sl])
        dxcv = ext_d[0:tm, :]
        nxt_sc[...] = ext_d[0:8, :]
        dcb_ref[...] += jnp.sum(dxcv, axis=0, keepdims=True)
        ext_x[0:8, :] = jnp.where(first_tile, 0.0, xrp_ref[...])
        ext_x[8:8 + tm, :] = xr_ref[...]
        dxr = jnp.zeros((tm, D), F32)
        for tap in range(4):
            dxr = dxr + ext_d[pl.ds(3 - tap, tm), :] * cw_ref[tap:tap + 1, :]
            dcw_ref[tap:tap + 1, :] += jnp.sum(dxcv * ext_x[pl.ds(5 + tap, tm), :], axis=0, keepdims=True)
        dxr_ref[...] = dxr.astype(BF16)

    rev = pl.BlockSpec((tm, D), lambda i: (nt - 1 - i, 0))
    prev = pl.BlockSpec((8, D), lambda i: (jnp.maximum((nt - 1 - i) * nb8 - 1, 0), 0))
    full = lambda shape: pl.BlockSpec(shape, lambda i: tuple(0 for _ in shape))
    vec = jax.ShapeDtypeStruct((1, D), F32)
    blocks = jax.ShapeDtypeStruct((8, 128, 128), F32)
    body, specs, operands = _behind(body, after)
    return pl.pallas_call(
        body, grid=(nt,), name="rglru_bwd",
        in_specs=specs + [rev, rev, prev, rev, rev, rev, rev, prev, full((4, D)), full((8, 128, 128)), full((8, 128, 128)),
                          _vec_spec(D)],
        out_specs=[rev, full((8, 128, 128)), full((8, 128, 128)), _vec_spec(D), _vec_spec(D), _vec_spec(D), full((4, D)),
                   _vec_spec(D)],
        out_shape=[jax.ShapeDtypeStruct((t, D), BF16), blocks, blocks, vec, vec, vec, jax.ShapeDtypeStruct((4, D), F32), vec],
        scratch_shapes=[pltpu.VMEM((tm + 8, D), F32), pltpu.VMEM((tm + 8, D), F32), pltpu.VMEM((tm + 8, D), F32),
                        pltpu.VMEM((tm, D), F32), pltpu.VMEM((tm, D), F32), pltpu.VMEM((8, D), F32), pltpu.VMEM((8, D), F32)],
        compiler_params=_params("arbitrary"),
    )(*operands, dhr, hr, hr, xc, r, ig, xr, xr, conv_w, wa2, wx2, lam)


def _attn_bwd(sink_rows, q, kp, vp, bias_t, mask, do):
    t = q.shape[0]
    tp = kp.shape[0]

    def body(sink_ref, q_ref, kp_ref, vp_ref, bias_ref, mask_ref, do_ref, dq_ref, dk_ref, dv_ref, dbias_ref, ds_ref):
        c = pl.program_id(0)

        @pl.when(c == 0)
        def _():
            for ref in (dk_ref, dv_ref, dbias_ref, ds_ref):
                ref[...] = jnp.zeros_like(ref)

        st = pl.multiple_of(c * CHUNK, CHUNK)
        maskv = mask_ref[...]
        kbd = _block_diag(kp_ref[pl.ds(st, KP), :], maskv)
        vbd = _block_diag(vp_ref[pl.ds(st, KP), :], maskv)
        q_all = _stack_heads(q_ref[...])
        do_all = _stack_heads(do_ref[...])
        valid = lax.broadcasted_iota(jnp.int32, (KP, 1), 0) + c * CHUNK >= PAD_KEYS
        qk = _nt(kbd, q_all)
        dp = _nt(vbd, do_all)
        ps, dscs = [], []
        for g in range(4):
            rows = slice(g * KP, (g + 1) * KP)
            p, sink_p = _group_softmax(qk[rows], bias_ref[rows, :], sink_ref[g:g + 1, :], valid)
            delta = jnp.sum(p * dp[rows], axis=0, keepdims=True)
            ps.append(p)
            dscs.append(p * (dp[rows] - delta))
            ds_ref[g:g + 1, :] += -(sink_p * delta)
        dsc = jnp.concatenate(dscs, axis=0)
        dbias_ref[...] += dsc
        dsb = (dsc * (HEAD_DIM ** -0.5)).astype(BF16)
        dq_ref[...] = _unstack_heads(_tn(dsb, kbd)).astype(BF16)

        lane_group = lax.broadcasted_iota(jnp.int32, (1, 4 * HEAD_DIM), 1) // HEAD_DIM

        def own_blocks(full):
            out = full[0:KP]
            for g in range(1, 4):
                out = jnp.where(lane_group == g, full[g * KP:(g + 1) * KP], out)
            return out

        dk_ref[pl.ds(st, KP), :] += own_blocks(_nn(dsb, q_all))
        dv_ref[pl.ds(st, KP), :] += own_blocks(_nn(jnp.concatenate(ps, axis=0).astype(BF16), do_all))

    full = lambda shape: pl.BlockSpec(shape, lambda i: tuple(0 for _ in shape))
    return pl.pallas_call(
        body, grid=(t // CHUNK,), name="attn_bwd",
        in_specs=[_WHOLE, _row_spec(CHUNK, D), _WHOLE, _WHOLE, _WHOLE, _WHOLE, _row_spec(CHUNK, D)],
        out_specs=[_row_spec(CHUNK, D), full((tp, KV_W)), full((tp, KV_W)), full((4 * KP, 4 * CHUNK)), full((8, 4 * CHUNK))],
        out_shape=[jax.ShapeDtypeStruct((t, D), BF16), jax.ShapeDtypeStruct((tp, KV_W), F32),
                   jax.ShapeDtypeStruct((tp, KV_W), F32), jax.ShapeDtypeStruct((4 * KP, 4 * CHUNK), F32),
                   jax.ShapeDtypeStruct((8, 4 * CHUNK), F32)],
        compiler_params=_params("arbitrary"),
    )(sink_rows, q, kp, vp, bias_t, mask, do)


def _mix_bwd2(dproj, dgate, h1, dh2, gmix, w_in_g, w_gate_g, after):
    t = h1.shape[0]
    tm = _tile(t)

    def body(dp_ref, dg_ref, h_ref, dh_ref, g_ref, win_ref, wg_ref, dh1_ref, dgm_ref):
        @pl.when(pl.program_id(0) == 0)
        def _():
            dgm_ref[...] = jnp.zeros_like(dgm_ref)

        du = jnp.zeros((tm, D), F32)
        for s in range(NSH):
            du = du + _nt(dp_ref[:, s * IN_S:(s + 1) * IN_S], win_ref[s])
            du = du + _nt(dg_ref[:, s * GATE_S:(s + 1) * GATE_S], wg_ref[s])
        dxn, dg = _rms_bwd(du, h_ref[...], g_ref[...])
        dgm_ref[...] += dg
        dh1_ref[...] = dh_ref[...] + dxn

    body, specs, operands = _behind(body, after)
    return pl.pallas_call(
        body, grid=(t // tm,), name="mix_bwd2",
        in_specs=specs + [_row_spec(tm, NSH * IN_S), _row_spec(tm, 2 * D), _row_spec(tm, D), _row_spec(tm, D), _vec_spec(D),
                          _WHOLE, _WHOLE],
        out_specs=[_row_spec(tm, D), _vec_spec(D)],
        out_shape=[jax.ShapeDtypeStruct((t, D), F32), jax.ShapeDtypeStruct((1, D), F32)],
        compiler_params=_params("arbitrary"),
    )(*operands, dproj, dgate, h1, dh2, gmix, w_in_g, w_gate_g)


def _band_onehot():
    nb = N_BUCKETS // 2
    max_exact = nb // 2
    rel = jnp.arange(KB)[None, :] - PAD_KEYS - jnp.arange(CHUNK)[:, None]
    ret = jnp.where(rel > 0, nb, 0)
    n = jnp.abs(rel)
    nf = jnp.maximum(n, 1).astype(jnp.float32)
    large = max_exact + (jnp.log(nf / max_exact) / math.log(128 / max_exact) * (nb - max_exact)).astype(jnp.int32)
    large = jnp.minimum(large, nb - 1)
    buckets = (ret + jnp.where(n < max_exact, n, large)).reshape(1, CHUNK * KB)
    return (buckets == jnp.arange(N_BUCKETS)[:, None]).astype(F32)


def _pair_blocks(w):
    z = jnp.zeros((8, 128, 128), w.dtype)
    return z.at[:, 0:64, 0:64].set(w[0::2]).at[:, 64:128, 64:128].set(w[1::2])


def _unpair_blocks(w2):
    return jnp.stack([w2[:, 0:64, 0:64], w2[:, 64:128, 64:128]], axis=1).reshape(16, 64, 64)


def _local_step(x, target, weights, sm, reducer):
    row = lambda v: v.reshape(1, -1)
    wg = dict(weights("ffn1", x))
    sm = dict(sm, conv_w=wg["conv_w"])
    onehot_t = _band_onehot()
    bias = _bias_fwd(sm["rel_bias"].T, onehot_t).reshape(4, 4, CHUNK, KB)
    bias_t = jnp.pad(jnp.transpose(bias, (0, 3, 1, 2)), ((0, 0), (0, KP - KB), (0, 0), (0, 0))).reshape(4 * KP, 4 * CHUNK)
    sink_rows = jnp.pad(jnp.repeat(sm["attn_sinks"].reshape(4, 4), CHUNK, axis=1), ((0, 4), (0, 0)))
    grp = jnp.arange(4 * KP)[:, None] // KP == jnp.arange(4 * HEAD_DIM)[None, :] // HEAD_DIM
    mask = (grp & (jnp.arange(4 * KP)[:, None] % KP < KB)).astype(BF16)
    wa2 = _pair_blocks(sm["rg_a_w"]).astype(BF16)
    wx2 = _pair_blocks(sm["rg_x_w"]).astype(BF16)

    h1, a1, b1, hm1, f1 = _ffn_fwd(x, row(sm["ffn1_pre_g"]), wg["ffn1_w1"], wg["ffn1_w3"], wg["ffn1_w2"],
                                   row(sm["ffn1_post_g"]), "ffn1_fwd")
    wg.update(weights("mix", h1))
    w_lru = wg["w_lru_out"].reshape(D, D)
    w_att = wg["w_attn_out"].reshape(D, D)
    w_o = wg["w_o"].reshape(D, D)
    u, q, k, v, xr, xg, gate = _mix_proj(h1, row(sm["mix_pre_g"]), wg["w_in"], wg["w_gate"], row(sm["b_gate"]))
    hr, yain, xc, r, ig = _rglru_fwd(xr, xg, sm["conv_w"], row(sm["conv_b"]), wa2, row(sm["rg_a_b"]), wx2,
                                     row(sm["rg_x_b"]), row(sm["lru_lambda"]))
    kp = jnp.pad(k, ((PAD_KEYS, KP - KB), (0, 0)))
    vp = jnp.pad(v, ((PAD_KEYS, KP - KB), (0, 0)))
    o = _attn_fwd(sink_rows, q, kp, vp, bias_t, mask)
    wg.update(weights("ffn2", o))
    h2, mo, merged, ya, yb = _merge_fwd(yain, o, gate, h1, w_lru, w_att, w_o, row(sm["mix_post_g"]))
    dy, a2, b2, hm2, f2, sq = _ffn_fwd(h2, row(sm["ffn2_pre_g"]), wg["ffn2_w1"], wg["ffn2_w3"], wg["ffn2_w2"],
                                       row(sm["ffn2_post_g"]), "ffn2_fwd", target)

    big, small = {}, {}
    dh2, n2, da2, db2, df2, small["ffn2_pre_g"], small["ffn2_post_g"] = _ffn_bwd(
        dy, h2, f2, a2, b2, row(sm["ffn2_pre_g"]), row(sm["ffn2_post_g"]), wg["ffn2_w1"], wg["ffn2_w3"], wg["ffn2_w2"],
        "ffn2_bwd")
    big["ffn2_w1"] = _wgrad_rows(da2, n2, "dw_ffn2_w1")
    big["ffn2_w3"] = _wgrad_rows(db2, n2, "dw_ffn2_w3")
    big["ffn2_w2"] = _wgrad_rows(hm2, df2, "dw_ffn2_w2")
    token = reducer.begin("ffn2", {n: big[n] for n in ("ffn2_w1", "ffn2_w3", "ffn2_w2")})
    dmo, dya, dyb, dgate, dhr, dxg, do, small["mix_post_g"], small["b_gate"] = _mix_bwd1(
        dh2, mo, row(sm["mix_post_g"]), gate, ya, yb, xg, hr, w_o, w_lru, w_att, token)
    big["w_o"] = _wgrad_sq(merged, dmo, "dw_w_o").reshape(NSH, D // NSH, D)
    big["w_lru_out"] = _wgrad_sq(yain, dya, "dw_w_lru_out").reshape(NSH, D // NSH, D)
    big["w_attn_out"] = _wgrad_sq(o, dyb, "dw_w_attn_out").reshape(NSH, D // NSH, D)
    token = reducer.advance("ffn2", big["w_attn_out"])
    (dxr, dwa2, dwx2, small["rg_a_b"], small["rg_x_b"], small["lru_lambda"], small["conv_w"], small["conv_b"]) = _rglru_bwd(
        dhr, hr, xc, r, ig, xr, sm["conv_w"], wa2, wx2, row(sm["lru_lambda"]), token)
    small["rg_a_w"] = _unpair_blocks(dwa2)
    small["rg_x_w"] = _unpair_blocks(dwx2)
    dq, dkp, dvp, dbias_t, ds_rows = _attn_bwd(sink_rows, q, kp, vp, bias_t, mask, do)
    dbias = jnp.transpose(dbias_t.reshape(4, KP, 4, CHUNK)[:, :KB], (0, 2, 3, 1)).reshape(N_HEADS, CHUNK * KB)
    drel_t, dsinks = _bias_bwd(dbias, onehot_t, ds_rows)
    small["attn_sinks"] = dsinks[0:4, 0:4].reshape(N_HEADS)
    small["rel_bias"] = drel_t.T
    t = x.shape[0]
    dproj = jnp.concatenate([dq, dkp[PAD_KEYS:PAD_KEYS + t].astype(BF16), dvp[PAD_KEYS:PAD_KEYS + t].astype(BF16), dxr, dxg],
                            axis=1)
    big["w_in"] = _wgrad_cols(u, dproj, IN_S, "dw_w_in")
    big["w_gate"] = _wgrad_cols(u, dgate, GATE_S, "dw_w_gate")
    token = reducer.begin("mix", {n: big[n] for n in ("w_in", "w_gate", "w_lru_out", "w_attn_out", "w_o")})
    dh1, small["mix_pre_g"] = _mix_bwd2(dproj, dgate, h1, dh2, row(sm["mix_pre_g"]), wg["w_in"], wg["w_gate"], token)
    n1, da1, db1, df1, small["ffn1_post_g"] = _ffn_bwd_acts(
        dh1, x, f1, a1, b1, row(sm["ffn1_pre_g"]), row(sm["ffn1_post_g"]), wg["ffn1_w2"], "ffn1_bwd_acts")
    token = reducer.advance("mix", df1)
    big["ffn1_w1"] = _wgrad_rows(da1, n1, "dw_ffn1_w1", token)
    big["ffn1_w3"] = _wgrad_rows(db1, n1, "dw_ffn1_w3", token)
    big["ffn1_w2"] = _wgrad_rows(hm1, df1, "dw_ffn1_w2", token)
    token = reducer.begin("ffn1", {n: big[n] for n in ("ffn1_w1", "ffn1_w3", "ffn1_w2")})
    dx, small["ffn1_pre_g"] = _ffn_bwd_input(dh1, x, da1, db1, row(sm["ffn1_pre_g"]), wg["ffn1_w1"], wg["ffn1_w3"],
                                             "ffn1_bwd_input", token)
    return sq, dx, big, small


_ANY = pl.BlockSpec(memory_space=pl.ANY)


def _place():
    return lax.axis_index("x"), lax.axis_index("y"), lax.axis_index("c")


def _other_chips(x, y):
    return [(1 - x, y), (x, 1 - y), (1 - x, 1 - y)]


_HBM = pl.BlockSpec(memory_space=pltpu.HBM)
_SEM = pl.BlockSpec(memory_space=pltpu.SEMAPHORE)
_EFFECT = pltpu.SideEffectType.DATAFLOW_SIDE_EFFECTING


def _cast_into_slot(w, chip, name, after=None):
    r, cc = w.shape
    rows = r // 4

    def body(chip_ref, *refs):
        w_ref, o_ref = refs[-2:]
        o_ref[...] = w_ref[...].astype(BF16)

    extra = [] if after is None else [after]
    return pl.pallas_call(
        body, name=name, out_shape=jax.ShapeDtypeStruct((NSH, r, cc), BF16),
        grid_spec=pltpu.PrefetchScalarGridSpec(
            num_scalar_prefetch=1, grid=(4,), in_specs=[_ANY] * len(extra) + [pl.BlockSpec((rows, cc), lambda i, chip: (i, 0))],
            out_specs=pl.BlockSpec((None, rows, cc), lambda i, chip: (chip[0], i, 0))),
        compiler_params=_params("arbitrary"))(chip, *extra, w)


def _piece(ref, slot, c):
    if ref.dtype == F32:
        return ref.at[slot]
    rh = ref.shape[1] // 2
    return ref.at[slot, pl.ds(pl.multiple_of(c * rh, 16), rh), :]


def _gather_start(stages, name):
    flat = [b for stage in stages for b in stage]
    n, ns = len(flat), len(stages)

    def body(*refs):
        ins, sems, token = refs[:n], refs[n:n + 2 * ns], refs[-1]
        x, y, c = _place()
        me = 2 * x + y
        k = 0
        for s, stage in enumerate(stages):
            for i in range(len(stage)):
                for j, (px, py) in enumerate(_other_chips(x, y)):
                    piece = _piece(ins[k], me, c)
                    pltpu.make_async_remote_copy(src_ref=piece, dst_ref=piece, send_sem=sems[2 * s].at[3 * i + j],
                                                 recv_sem=sems[2 * s + 1].at[3 * i + j], device_id=(px, py, c),
                                                 device_id_type=MESH).start()
                k += 1
        token[...] = jnp.zeros_like(token)

    sem_shapes = [pltpu.SemaphoreType.DMA((3 * len(stage),)) for stage in stages for _ in range(2)]
    outs = pl.pallas_call(
        body, name=name, in_specs=[_HBM] * n,
        out_specs=[_SEM] * (2 * ns) + [_HBM] * n + [pl.BlockSpec(memory_space=pltpu.VMEM)],
        out_shape=sem_shapes + [pltpu.HBM(b.shape, b.dtype) for b in flat] + [jax.ShapeDtypeStruct((8, 128), F32)],
        input_output_aliases={i: 2 * ns + i for i in range(n)},
        compiler_params=pltpu.CompilerParams(has_side_effects=_EFFECT),
    )(*[pltpu.with_memory_space_constraint(b, pltpu.HBM) for b in flat])
    sems, bufs, token = outs[:2 * ns], list(outs[2 * ns:2 * ns + n]), outs[-1]
    per_stage, k = [], 0
    for s, stage in enumerate(stages):
        per_stage.append((sems[2 * s], sems[2 * s + 1], bufs[k:k + len(stage)]))
        k += len(stage)
    return per_stage, token


def _gather_wait(send_sems, recv_sems, bufs, after, name):
    n = len(bufs)

    def body(*refs):
        ins, ssem, rsem = refs[:n], refs[n], refs[n + 1]
        x, y, c = _place()
        me = 2 * x + y
        for i in range(n):
            for j, (px, py) in enumerate(_other_chips(x, y)):
                cp = pltpu.make_async_remote_copy(src_ref=_piece(ins[i], me, c), dst_ref=_piece(ins[i], 2 * px + py, c),
                                                  send_sem=ssem.at[3 * i + j], recv_sem=rsem.at[3 * i + j],
                                                  device_id=(px, py, c), device_id_type=MESH)
                cp.wait_send()
                cp.wait_recv()

    return pl.pallas_call(
        body, name=name, in_specs=[_HBM] * n + [_SEM, _SEM, _ANY], out_specs=[_HBM] * n,
        out_shape=[pltpu.HBM(b.shape, b.dtype) for b in bufs], input_output_aliases={i: i for i in range(n)},
        compiler_params=pltpu.CompilerParams(has_side_effects=_EFFECT),
    )(*bufs, send_sems, recv_sems, after)


def _sibling_fill(bufs, name):
    n = len(bufs)

    def body(*refs):
        ins, outs = refs[:n], refs[n:2 * n]
        send_sems, recv_sems = refs[2 * n:]
        x, y, c = _place()
        copies = []
        for i in range(n):
            for j, (px, py) in enumerate(_other_chips(x, y)):
                copies.append(pltpu.make_async_remote_copy(
                    src_ref=_piece(ins[i], 2 * px + py, c), dst_ref=_piece(outs[i], 2 * px + py, c),
                    send_sem=send_sems.at[3 * i + j], recv_sem=recv_sems.at[3 * i + j], device_id=(x, y, 1 - c),
                    device_id_type=MESH))
                copies[-1].start()
        for cp in copies:
            cp.wait()

    return pl.pallas_call(
        body, name=name, in_specs=[_ANY] * n, out_specs=[_ANY] * n,
        out_shape=[jax.ShapeDtypeStruct(b.shape, b.dtype) for b in bufs], input_output_aliases={i: i for i in range(n)},
        scratch_shapes=[pltpu.SemaphoreType.DMA((3 * n,)), pltpu.SemaphoreType.DMA((3 * n,))],
        compiler_params=pltpu.CompilerParams(has_side_effects=True),
    )(*bufs)


def _swap_plan(srcs, lands):
    x, y, c = _place()
    plan = []
    for src, land in zip(srcs, lands):
        rh = src.shape[1] // 2
        plan.append((src.at[:, pl.ds(pl.multiple_of((1 - c) * rh, 16), rh), :], land, (x, y, 1 - c)))
    return plan


def _owners_plan(srcs, lands):
    x, y, c = _place()
    return [(src.at[2 * px + py], land.at[j], (px, py, c))
            for src, land in zip(srcs, lands) for j, (px, py) in enumerate(_other_chips(x, y))]


def _exchange_start(srcs, lands, plan, copies, name):
    n = len(srcs)

    def body(*refs):
        send_sems, recv_sems, token = refs[2 * n], refs[2 * n + 1], refs[-1]
        for k, (src, dst, dev) in enumerate(plan(refs[:n], refs[n:2 * n])):
            pltpu.make_async_remote_copy(src_ref=src, dst_ref=dst, send_sem=send_sems.at[k], recv_sem=recv_sems.at[k],
                                         device_id=dev, device_id_type=MESH).start()
        token[...] = jnp.zeros_like(token)

    both = list(srcs) + list(lands)
    outs = pl.pallas_call(
        body, name=name, in_specs=[_HBM] * (2 * n),
        out_specs=[_SEM, _SEM] + [_HBM] * (2 * n) + [pl.BlockSpec(memory_space=pltpu.VMEM)],
        out_shape=[pltpu.SemaphoreType.DMA((copies,)), pltpu.SemaphoreType.DMA((copies,))]
        + [pltpu.HBM(b.shape, b.dtype) for b in both] + [jax.ShapeDtypeStruct((8, 128), F32)],
        input_output_aliases={i: 2 + i for i in range(2 * n)},
        compiler_params=pltpu.CompilerParams(has_side_effects=_EFFECT),
    )(*[pltpu.with_memory_space_constraint(b, pltpu.HBM) for b in both])
    return (outs[0], outs[1]), list(outs[2:2 + n]), list(outs[2 + n:2 + 2 * n]), outs[-1]


def _exchange_wait(sems, srcs, lands, plan, after, name):
    n = len(srcs)

    def body(*refs):
        send_sems, recv_sems = refs[2 * n], refs[2 * n + 1]
        for k, (src, dst, dev) in enumerate(plan(refs[:n], refs[n:2 * n])):
            cp = pltpu.make_async_remote_copy(src_ref=src, dst_ref=dst, send_sem=send_sems.at[k], recv_sem=recv_sems.at[k],
                                              device_id=dev, device_id_type=MESH)
            cp.wait_send()
            cp.wait_recv()

    both = list(srcs) + list(lands)
    afters = list(after) if isinstance(after, (list, tuple)) else [after]
    outs = pl.pallas_call(
        body, name=name, in_specs=[_HBM] * (2 * n) + [_SEM, _SEM] + [_ANY] * len(afters), out_specs=[_HBM] * (2 * n),
        out_shape=[pltpu.HBM(b.shape, b.dtype) for b in both], input_output_aliases={i: i for i in range(2 * n)},
        compiler_params=pltpu.CompilerParams(has_side_effects=_EFFECT),
    )(*both, sems[0], sems[1], *afters)
    return list(outs[:n]), list(outs[n:])


class _Reducer:
    def __init__(self, where):
        self.state = {}
        self.where = where

    def begin(self, stage, grads):
        names = list(grads)
        full = [grads[n] for n in names]
        lands = [lax.empty((NSH, g.shape[1] // 2, g.shape[2]), g.dtype) for g in full]
        sems, full, lands, token = _exchange_start(full, lands, _swap_plan, len(full), "swap_start_" + stage)
        self.state[stage] = (names, sems, full, lands)
        return token

    def advance(self, stage, after):
        names, sems, full, lands = self.state[stage]
        full, got = _exchange_wait(sems, full, lands, _swap_plan, after, "swap_wait_" + stage)
        sums = [_chip_sum(g, a, self.where, "chip_sum_" + n) for n, g, a in zip(names, full, got)]
        lands = [lax.empty((3,) + s[0].shape[1:], BF16) for s in sums]
        sems, sent, lands, token = _exchange_start([s[0] for s in sums], lands, _owners_plan, 3 * len(sums),
                                                   "owners_start_" + stage)
        self.state[stage] = (names, [s[1] for s in sums], sems, sent, lands)
        return token

    def finish(self, stage, after):
        names, own, sems, sent, lands = self.state[stage]
        _, got = _exchange_wait(sems, sent, lands, _owners_plan, after, "owners_wait_" + stage)
        return {n: _owner_sum(o, g, "owner_sum_" + n) for n, o, g in zip(names, own, got)}


def _chip_sum(g, got, where, name):
    _, r, cc = g.shape
    rh = r // 2

    def body(where_ref, g_ref, got_ref, hb_ref, own_ref):
        h = g_ref[...].astype(F32) + got_ref[...].astype(F32)
        hb_ref[...] = h.astype(BF16)

        @pl.when(pl.program_id(0) == where_ref[1])
        def _():
            own_ref[...] = h

    return pl.pallas_call(
        body, name=name,
        grid_spec=pltpu.PrefetchScalarGridSpec(
            num_scalar_prefetch=1, grid=(NSH,),
            in_specs=[pl.BlockSpec((None, rh, cc), lambda s, where: (s, where[0], 0)),
                      pl.BlockSpec((None, rh, cc), lambda s, where: (s, 0, 0))],
            out_specs=[pl.BlockSpec((None, rh, cc), lambda s, where: (s, 0, 0)),
                       pl.BlockSpec((rh, cc), lambda s, where: (0, 0))]),
        out_shape=[jax.ShapeDtypeStruct((NSH, rh, cc), BF16), jax.ShapeDtypeStruct((rh, cc), F32)],
        compiler_params=_params("arbitrary"),
    )(where, g, got)


def _owner_sum(own, got, name):
    rh, cc = own.shape
    rows = rh // 2

    def body(own_ref, got_ref, o_ref):
        o_ref[...] = ((own_ref[...] + got_ref[0].astype(F32)) + got_ref[1].astype(F32)) + got_ref[2].astype(F32)

    return pl.pallas_call(
        body, grid=(2,), name=name,
        in_specs=[pl.BlockSpec((rows, cc), lambda i: (i, 0)), pl.BlockSpec((3, rows, cc), lambda i: (0, i, 0))],
        out_specs=pl.BlockSpec((rows, cc), lambda i: (i, 0)),
        out_shape=jax.ShapeDtypeStruct((rh, cc), F32), compiler_params=_params("arbitrary"),
    )(own, got)


def _send_halves(halves, name):
    n = len(halves)

    def body(*refs):
        ins, outs = refs[:n], refs[n:2 * n]
        send_sems, recv_sems = refs[2 * n:]
        x, y, c = _place()
        copies = [pltpu.make_async_remote_copy(src_ref=ins[w], dst_ref=outs[w], send_sem=send_sems.at[w], recv_sem=recv_sems.at[w],
                                               device_id=(x, y, 1 - c), device_id_type=MESH) for w in range(n)]
        for cp in copies:
            cp.start()
        for cp in copies:
            cp.wait()

    return pl.pallas_call(
        body, name=name, in_specs=[_ANY] * n, out_specs=[_ANY] * n,
        out_shape=[jax.ShapeDtypeStruct(h.shape, F32) for h in halves],
        scratch_shapes=[pltpu.SemaphoreType.DMA((n,)), pltpu.SemaphoreType.DMA((n,))],
        compiler_params=pltpu.CompilerParams(has_side_effects=True),
    )(*halves)


def _all_reduce_small(part):
    def body(p_ref, o_ref, rbuf, send1, recv1, send2, recv2):
        x, y, c = _place()
        me = 4 * x + 2 * y + c
        peers = []
        for k in range(1, 8):
            px, py, pc = x ^ ((k >> 2) & 1), y ^ ((k >> 1) & 1), c ^ (k & 1)
            peers.append((k, (px, py, pc), 4 * px + 2 * py + pc))

        def rows(d):
            return pl.ds(pl.multiple_of(d * SMALL_SLICE, 8), SMALL_SLICE)

        first = [pltpu.make_async_remote_copy(src_ref=p_ref.at[rows(idx), :], dst_ref=rbuf.at[me], send_sem=send1.at[k],
                                              recv_sem=recv1.at[k], device_id=dev, device_id_type=MESH)
                 for k, dev, idx in peers]
        for cp in first:
            cp.start()
        rbuf[me] = p_ref[rows(me), :]
        for k, dev, idx in peers:
            pltpu.make_async_remote_copy(src_ref=p_ref.at[rows(idx), :], dst_ref=rbuf.at[idx], send_sem=send1.at[k],
                                         recv_sem=recv1.at[k], device_id=dev, device_id_type=MESH).wait_recv()
        acc = rbuf[0]
        for d in range(1, 8):
            acc = acc + rbuf[d]
        o_ref[rows(me), :] = acc
        second = [pltpu.make_async_remote_copy(src_ref=o_ref.at[rows(me), :], dst_ref=o_ref.at[rows(me), :],
                                               send_sem=send2.at[k], recv_sem=recv2.at[k], device_id=dev, device_id_type=MESH)
                  for k, dev, idx in peers]
        for cp in second:
            cp.start()
        for k, dev, idx in peers:
            pltpu.make_async_remote_copy(src_ref=o_ref.at[rows(me), :], dst_ref=o_ref.at[rows(idx), :], send_sem=send2.at[k],
                                         recv_sem=recv2.at[k], device_id=dev, device_id_type=MESH).wait_recv()
        for cp in first + second:
            cp.wait_send()

    return pl.pallas_call(
        body, name="all_reduce_small", in_specs=[_WHOLE], out_specs=_WHOLE,
        out_shape=jax.ShapeDtypeStruct((SMALL_ROWS, 128), F32),
        scratch_shapes=[pltpu.VMEM((8, SMALL_SLICE, 128), F32)] + [pltpu.SemaphoreType.DMA((8,))] * 4,
        compiler_params=pltpu.CompilerParams(has_side_effects=True),
    )(part)


def _adamw_update(w, gv, m, v):
    nm = ADAM_B1 * m + (1.0 - ADAM_B1) * gv
    nv = ADAM_B2 * v + (1.0 - ADAM_B2) * (gv * gv)
    m_hat = nm / (1.0 - ADAM_B1 ** ADAM_STEP)
    v_hat = nv / (1.0 - ADAM_B2 ** ADAM_STEP)
    return -ADAM_LR * (m_hat / (jnp.sqrt(v_hat) + ADAM_EPS) + ADAM_WD * w), nm, nv


def _adamw(w, g, m, v, name):
    rows = w.shape[0] // 4

    def body(w_ref, g_ref, m_ref, v_ref, d_ref, nm_ref, nv_ref):
        d_ref[...], nm_ref[...], nv_ref[...] = _adamw_update(w_ref[...], g_ref[...], m_ref[...], v_ref[...])

    spec = pl.BlockSpec((rows, w.shape[1]), lambda i: (i, 0))
    out = jax.ShapeDtypeStruct(w.shape, F32)
    return pl.pallas_call(body, grid=(4,), in_specs=[spec] * 4, out_specs=[spec] * 3, out_shape=[out] * 3, name=name,
                          compiler_params=_params("arbitrary"))(w, g, m, v)


def _adamw_halves(w, mine, theirs, m, v, name):
    rh, cc = mine.shape
    steps = 1
    rows = rh // steps

    def body(w_ref, mine_ref, theirs_ref, m_ref, v_ref, g_ref, d_ref, nm_ref, nv_ref):
        gv = jnp.where(pl.program_id(0) == lax.axis_index("c"), mine_ref[...], theirs_ref[...])
        g_ref[...] = gv
        d_ref[...], nm_ref[...], nv_ref[...] = _adamw_update(w_ref[...], gv, m_ref[...], v_ref[...])

    spec = pl.BlockSpec((rows, cc), lambda h, i: (steps * h + i, 0))
    half = pl.BlockSpec((rows, cc), lambda h, i: (i, 0))
    out = jax.ShapeDtypeStruct(w.shape, F32)
    return pl.pallas_call(body, grid=(2, steps), in_specs=[spec, half, half, spec, spec], out_specs=[spec] * 4,
                          out_shape=[out] * 4, name=name, compiler_params=_params("arbitrary", "arbitrary"))(w, mine, theirs, m, v)


SMALL_USED = sum(size for _, size in SMALL) // 128


def _pack_small(vals, tail=None):
    parts = []
    for name, size in SMALL:
        flat = vals[name].reshape(-1).astype(F32)
        parts.append(jnp.pad(flat, (0, size - flat.shape[0])))
    if tail is not None:
        parts.append(tail.reshape(128))
    flat = jnp.concatenate(parts)
    return jnp.pad(flat, (0, SMALL_ROWS * 128 - flat.shape[0])).reshape(SMALL_ROWS, 128)


def _unpack_small(packed, shapes):
    flat = packed.reshape(-1)
    out, off = {}, 0
    for name, size in SMALL:
        n = math.prod(shapes[name])
        out[name] = flat[off:off + n].reshape(shapes[name])
        off += size
    return out


def kernel(x, ffn1_pre_g, ffn1_w1, ffn1_w3, ffn1_w2, ffn1_post_g, mix_pre_g, w_in, conv_w, conv_b, rg_a_w, rg_a_b, rg_x_w, rg_x_b, lru_lambda, w_lru_out, attn_sinks, rel_bias, w_attn_out, w_gate, b_gate, w_o, mix_post_g, ffn2_pre_g, ffn2_w1, ffn2_w3, ffn2_w2, ffn2_post_g, loss_target, m_ffn1_pre_g, m_ffn1_w1, m_ffn1_w3, m_ffn1_w2, m_ffn1_post_g, m_mix_pre_g, m_w_in, m_conv_w, m_conv_b, m_rg_a_w, m_rg_a_b, m_rg_x_w, m_rg_x_b, m_lru_lambda, m_w_lru_out, m_attn_sinks, m_rel_bias, m_w_attn_out, m_w_gate, m_b_gate, m_w_o, m_mix_post_g, m_ffn2_pre_g, m_ffn2_w1, m_ffn2_w3, m_ffn2_w2, m_ffn2_post_g, v_ffn1_pre_g, v_ffn1_w1, v_ffn1_w3, v_ffn1_w2, v_ffn1_post_g, v_mix_pre_g, v_w_in, v_conv_w, v_conv_b, v_rg_a_w, v_rg_a_b, v_rg_x_w, v_rg_x_b, v_lru_lambda, v_w_lru_out, v_attn_sinks, v_rel_bias, v_w_attn_out, v_w_gate, v_b_gate, v_w_o, v_mix_post_g, v_ffn2_pre_g, v_ffn2_w1, v_ffn2_w3, v_ffn2_w2, v_ffn2_post_g):
    given = dict(locals())
    chip = 2 * lax.axis_index("x") + lax.axis_index("y")
    transposed = ("ffn1_w1", "ffn1_w3", "ffn2_w1", "ffn2_w3")

    def shard(name, moment=""):
        w = given[moment + name][0]
        return w.T if name in transposed else w

    def unshard(name, w):
        return (w.T if name in transposed else w)[None]

    def my_columns(a):
        own = lax.broadcasted_iota(jnp.int32, (1, 4, D), 2) // (D // NSH) == chip
        return jnp.where(own, jnp.tile(a, (1, 1, NSH)), 0.0)

    def only_my_columns(a):
        parts = a.reshape(1, 4, NSH, D // NSH)
        return sum(jnp.where(chip == s, parts[:, :, s], 0.0) for s in range(NSH))

    chip_arr = jnp.reshape(chip, (1,)).astype(jnp.int32)
    stage_names = {"ffn1": ["ffn1_w1", "ffn1_w3", "ffn1_w2", "conv_w"],
                   "mix": ["w_in", "w_gate", "w_lru_out", "w_attn_out", "w_o"],
                   "ffn2": ["ffn2_w1", "ffn2_w3", "ffn2_w2"]}
    in_flight, started = {}, None
    for stage, names in stage_names.items():
        bufs = [jnp.where(lax.broadcasted_iota(jnp.int32, (NSH, 4, D // NSH), 0) == chip, given[n], 0.0) if n == "conv_w"
                else _cast_into_slot(shard(n), chip_arr, "cast_" + n, started) for n in names]
        (in_flight[stage],), started = _gather_start([bufs], "gather_start_" + stage)
    all_started = started

    def weights(stage, after):
        names = stage_names[stage]
        send_sems, recv_sems, landing = in_flight[stage]
        if stage == "ffn1":
            after = all_started
        landed = _gather_wait(send_sems, recv_sems, landing, after, "gather_wait_" + stage)
        halves = [b for b in landed if b.dtype == BF16]
        out = dict(zip([n for n, b in zip(names, landed) if b.dtype == BF16], _sibling_fill(halves, "sibling_fill_" + stage)))
        if "conv_w" in names:
            out["conv_w"] = jnp.transpose(landed[names.index("conv_w")], (1, 0, 2)).reshape(4, D)
        return out

    small_shapes = {n: given[n].shape for n, _ in SMALL}
    small_shapes["conv_w"] = (1, 4, D)
    sm = {n: (given[n][0] if given[n].shape[0] == 1 and n != "rel_bias" else given[n]) for n, _ in SMALL if n != "conv_w"}

    reducer = _Reducer(jnp.stack([lax.axis_index("c"), chip]).astype(jnp.int32))
    sq, dx, _, small = _local_step(x[0], loss_target[0], weights, sm, reducer)

    reducer.advance("ffn1", dx)
    reduced_small = _all_reduce_small(_pack_small(small, tail=sq))
    loss = reduced_small[SMALL_USED, 0] * (0.5 / D)
    small_g = _unpack_small(reduced_small, small_shapes)
    grads, delta, new_m, new_v = {}, {}, {}, {}
    after = [reduced_small]
    for stage in ("ffn2", "mix", "ffn1"):
        halves = reducer.finish(stage, after)
        from_sibling = _send_halves(list(halves.values()), "send_halves_" + stage)
        for (n, mine), theirs in zip(halves.items(), from_sibling):
            grads[n], delta[n], new_m[n], new_v[n] = (unshard(n, r) for r in _adamw_halves(
                shard(n), mine, theirs, shard(n, "m_"), shard(n, "v_"), "adamw_" + n))
            after.append(new_v[n])

    packed = [_pack_small({n: (my_columns(given[pre + n]) if n == "conv_w" else given[pre + n]) for n, _ in SMALL})
              for pre in ("", "m_", "v_")]
    outs = _adamw(packed[0], reduced_small, packed[1], packed[2], "adamw_small")
    for dst, arr in zip((delta, new_m, new_v), outs):
        dst.update(_unpack_small(arr, small_shapes))
    small_out = dict(small_g)
    for d in (small_out, delta, new_m, new_v):
        d["conv_w"] = only_my_columns(d["conv_w"])
    grads.update(small_out)
    return (loss, dx[None], *[grads[n] for n in WEIGHTS], *[delta[n] for n in WEIGHTS], *[new_m[n] for n in WEIGHTS],
            *[new_v[n] for n in WEIGHTS])
```

```python
import functools
import math

import jax
import jax.numpy as jnp
from jax import lax
from jax.experimental import pallas as pl
from jax.experimental.pallas import tpu as pltpu

F32, BF16 = jnp.float32, jnp.bfloat16
D = 1024
NSH = 4
FF_S = 704
IN_S = 896
GATE_S = 512
KV_W = 256
CHUNK = 64
KB = 192
N_HEADS = 16
HEAD_DIM = 64
N_BUCKETS = 32
KP = 192
PAD_KEYS = 128
RMS_EPS = 1e-6
NEG_INF = -1e30
LRU_C = 8.0
TM = 512
TM_SCAN = 256
VMEM_LIMIT = 56 * 1024 * 1024
ADAM_LR, ADAM_B1, ADAM_B2, ADAM_EPS, ADAM_WD, ADAM_STEP = 0.001, 0.9, 0.999, 1e-08, 0.01, 10
SMALL_ROWS = 1216
SMALL_SLICE = SMALL_ROWS // 8
MESH = pl.DeviceIdType.MESH

BIG = ["ffn1_w1", "ffn1_w3", "ffn1_w2", "w_in", "w_lru_out", "w_attn_out", "w_gate", "w_o", "ffn2_w1", "ffn2_w3", "ffn2_w2"]
SMALL = [("ffn1_pre_g", 1024), ("ffn1_post_g", 1024), ("mix_pre_g", 1024), ("conv_w", 4096), ("conv_b", 1024),
         ("rg_a_w", 65536), ("rg_a_b", 1024), ("rg_x_w", 65536), ("rg_x_b", 1024), ("lru_lambda", 1024),
         ("attn_sinks", 1024), ("rel_bias", 1024), ("b_gate", 2048), ("mix_post_g", 1024), ("ffn2_pre_g", 1024),
         ("ffn2_post_g", 1024)]
WEIGHTS = ["ffn1_pre_g", "ffn1_w1", "ffn1_w3", "ffn1_w2", "ffn1_post_g", "mix_pre_g", "w_in", "conv_w", "conv_b", "rg_a_w",
           "rg_a_b", "rg_x_w", "rg_x_b", "lru_lambda", "w_lru_out", "attn_sinks", "rel_bias", "w_attn_out", "w_gate", "b_gate",
           "w_o", "mix_post_g", "ffn2_pre_g", "ffn2_w1", "ffn2_w3", "ffn2_w2", "ffn2_post_g"]


def _params(*sem):
    return pltpu.CompilerParams(dimension_semantics=sem or None, vmem_limit_bytes=VMEM_LIMIT)


def _nn(a, b):
    return jnp.dot(a, b, preferred_element_type=F32)


def _nt(a, b):
    return lax.dot_general(a, b, (((1,), (1,)), ((), ())), preferred_element_type=F32)


def _tn(a, b):
    return lax.dot_general(a, b, (((0,), (0,)), ((), ())), preferred_element_type=F32)


def _rms(x, g):
    rstd = lax.rsqrt(jnp.mean(x * x, axis=-1, keepdims=True) + RMS_EPS)
    return (x * rstd) * g


def _rms_bwd(dout, x, g):
    rstd = lax.rsqrt(jnp.mean(x * x, axis=-1, keepdims=True) + RMS_EPS)
    xhat = x * rstd
    dg = jnp.sum(dout * xhat, axis=0, keepdims=True)
    dxhat = dout * g
    dx = rstd * (dxhat - xhat * jnp.mean(dxhat * xhat, axis=-1, keepdims=True))
    return dx, dg


_GELU_K = math.sqrt(2.0 / math.pi)


def _gelu(x):
    return x * (0.5 * (1.0 + jnp.tanh(_GELU_K * (x + 0.044715 * (x * x * x)))))


def _gelu_and_grad(x):
    x2 = x * x
    t = jnp.tanh(_GELU_K * (x + 0.044715 * (x2 * x)))
    cdf = 0.5 * (1.0 + t)
    return x * cdf, cdf + x * (0.5 * (1.0 - t * t) * (_GELU_K * (1.0 + 3.0 * 0.044715 * x2)))


def _softplus_neg(lam):
    z = -lam
    u = jnp.exp(-jnp.abs(z))
    w = 1.0 + u
    log1p_u = jnp.where(w == 1.0, u, jnp.log(w) * (u / (w - 1.0)))
    return jnp.maximum(z, 0.0) + log1p_u


def _lru_coeffs(r, sp):
    log_a = (-LRU_C * r) * sp
    a = jnp.exp(log_a)
    t = jnp.tanh(log_a)
    s = jnp.sqrt(-2.0 * t / (1.0 - t))
    return a, s


def _row_spec(tm, width):
    return pl.BlockSpec((tm, width), lambda i: (i, 0))


def _vec_spec(width):
    return pl.BlockSpec((1, width), lambda i: (0, 0))


_WHOLE = pl.BlockSpec(memory_space=pltpu.VMEM)


def _tile(t, tm=TM):
    return min(tm, t)


def _ffn_fwd(x, gpre, w1g, w3g, w2g, gpost, name, target=None):
    t = x.shape[0]
    tm = _tile(t)
    last = target is not None

    def body(x_ref, gpre_ref, w1_ref, w3_ref, w2_ref, gpost_ref, *refs):
        t_ref, (h_ref, a_ref, b_ref, hm_ref, f_ref), l_ref = (refs[0] if last else None), refs[last:last + 5], refs[-1]
        xv = x_ref[...]
        nb = _rms(xv, gpre_ref[...]).astype(BF16)
        f = jnp.zeros((tm, D), F32)
        for s in range(NSH):
            a = _nt(nb, w1_ref[s])
            b = _nt(nb, w3_ref[s])
            hmb = ((a * jax.nn.sigmoid(a)) * b).astype(BF16)
            a_ref[s] = a.astype(BF16)
            b_ref[s] = b.astype(BF16)
            hm_ref[s] = hmb
            f = f + _nn(hmb, w2_ref[s])
        f_ref[...] = f
        h = xv + 0.5 * _rms(f, gpost_ref[...])
        if last:
            @pl.when(pl.program_id(0) == 0)
            def _():
                l_ref[...] = jnp.zeros_like(l_ref)

            e = h - t_ref[...]
            h_ref[...] = e * (1.0 / D)
            l_ref[...] += jnp.sum(jnp.sum(e * e, axis=0, keepdims=True), axis=1, keepdims=True)
        else:
            h_ref[...] = h

    sh = pl.BlockSpec((NSH, tm, FF_S), lambda i: (0, i, 0))
    act = jax.ShapeDtypeStruct((NSH, t, FF_S), BF16)
    return pl.pallas_call(
        body, grid=(t // tm,), name=name,
        in_specs=[_row_spec(tm, D), _vec_spec(D), _WHOLE, _WHOLE, _WHOLE, _vec_spec(D)] + [_row_spec(tm, D)] * last,
        out_specs=[_row_spec(tm, D), sh, sh, sh, _row_spec(tm, D)] + [pl.BlockSpec((1, 128), lambda i: (0, 0))] * last,
        out_shape=[jax.ShapeDtypeStruct((t, D), F32), act, act, act, jax.ShapeDtypeStruct((t, D), F32)]
        + [jax.ShapeDtypeStruct((1, 128), F32)] * last,
        compiler_params=_params("arbitrary"),
    )(x, gpre, w1g, w3g, w2g, gpost, *([target] if last else []))


def _mix_proj(h1, gmix, w_in_g, w_gate_g, b_gate):
    t = h1.shape[0]
    tm = _tile(t)

    def body(h_ref, g_ref, win_ref, wg_ref, bg_ref, u_ref, q_ref, k_ref, v_ref, xr_ref, xg_ref, gate_ref):
        ub = _rms(h_ref[...], g_ref[...]).astype(BF16)
        u_ref[...] = ub
        p0 = _nn(ub, win_ref[0])
        q_ref[:, 0:896] = p0.astype(BF16)
        p1 = _nn(ub, win_ref[1])
        q_ref[:, 896:1024] = p1[:, 0:128].astype(BF16)
        k_ref[...] = p1[:, 128:384].astype(BF16)
        v_ref[...] = p1[:, 384:640].astype(BF16)
        xr_ref[:, 0:256] = p1[:, 640:896]
        p2 = _nn(ub, win_ref[2])
        xr_ref[:, 256:1024] = p2[:, 0:768]
        xg_ref[:, 0:128] = p2[:, 768:896]
        xg_ref[:, 128:1024] = _nn(ub, win_ref[3])
        for s in range(NSH):
            sl = slice(s * GATE_S, (s + 1) * GATE_S)
            gate_ref[:, sl] = jax.nn.sigmoid(_nn(ub, wg_ref[s]) + bg_ref[:, sl])

    return pl.pallas_call(
        body, grid=(t // tm,), name="mix_proj",
        in_specs=[_row_spec(tm, D), _vec_spec(D), _WHOLE, _WHOLE, _vec_spec(2 * D)],
        out_specs=[_row_spec(tm, D), _row_spec(tm, D), _row_spec(tm, KV_W), _row_spec(tm, KV_W), _row_spec(tm, D),
                   _row_spec(tm, D), _row_spec(tm, 2 * D)],
        out_shape=[jax.ShapeDtypeStruct((t, D), BF16), jax.ShapeDtypeStruct((t, D), BF16),
                   jax.ShapeDtypeStruct((t, KV_W), BF16), jax.ShapeDtypeStruct((t, KV_W), BF16),
                   jax.ShapeDtypeStruct((t, D), F32), jax.ShapeDtypeStruct((t, D), F32),
                   jax.ShapeDtypeStruct((t, 2 * D), F32)],
        compiler_params=_params("arbitrary"),
    )(h1, gmix, w_in_g, w_gate_g, b_gate)


def _rglru_fwd(xr, xg, conv_w, conv_b, wa2, ba, wx2, bx, lam):
    t = xr.shape[0]
    tm = _tile(t, TM_SCAN)
    nb8 = tm // 8

    def body(xr_ref, xrp_ref, xg_ref, cw_ref, cb_ref, wa_ref, ba_ref, wx_ref, bx_ref, lam_ref,
             hr_ref, yain_ref, xc_ref, r_ref, ig_ref, a_sc, s_ref, ext, h_sc):
        i = pl.program_id(0)

        @pl.when(i == 0)
        def _():
            h_sc[...] = jnp.zeros_like(h_sc)

        ext[0:8, :] = jnp.where(i == 0, 0.0, xrp_ref[...])
        ext[8:8 + tm, :] = xr_ref[...]
        xc = jnp.broadcast_to(cb_ref[...], (tm, D))
        for tap in range(4):
            xc = xc + ext[pl.ds(5 + tap, tm), :] * cw_ref[tap:tap + 1, :]
        xc_ref[...] = xc
        xcb = xc.astype(BF16)
        for p in range(8):
            sl = slice(p * 128, (p + 1) * 128)
            r_ref[:, sl] = jax.nn.sigmoid(_nn(xcb[:, sl], wa_ref[p]) + ba_ref[:, sl])
            ig_ref[:, sl] = jax.nn.sigmoid(_nn(xcb[:, sl], wx_ref[p]) + bx_ref[:, sl])
        a, s = _lru_coeffs(r_ref[...], _softplus_neg(lam_ref[...]))
        a_sc[...] = a
        s_ref[...] = s
        hr_ref[...] = s * (ig_ref[...] * xc)

        def blk(j, h):
            st = pl.multiple_of(j * 8, 8)
            a8 = a_sc[pl.ds(st, 8), :]
            u8 = hr_ref[pl.ds(st, 8), :]
            rows = []
            for k in range(8):
                h = a8[k:k + 1, :] * h + u8[k:k + 1, :]
                rows.append(h)
            hr_ref[pl.ds(st, 8), :] = jnp.concatenate(rows, axis=0)
            return h

        h_sc[0:1, :] = lax.fori_loop(0, nb8, blk, h_sc[0:1, :])
        yain_ref[...] = (hr_ref[...] * _gelu(xg_ref[...])).astype(BF16)

    prev = pl.BlockSpec((8, D), lambda i: (jnp.maximum(i * nb8 - 1, 0), 0))
    full = lambda shape: pl.BlockSpec(shape, lambda i: tuple(0 for _ in shape))
    f32 = jax.ShapeDtypeStruct((t, D), F32)
    return pl.pallas_call(
        body, grid=(t // tm,), name="rglru_fwd",
        in_specs=[_row_spec(tm, D), prev, _row_spec(tm, D), full((4, D)), _vec_spec(D), full((8, 128, 128)), _vec_spec(D),
                  full((8, 128, 128)), _vec_spec(D), _vec_spec(D)],
        out_specs=[_row_spec(tm, D)] * 7,
        out_shape=[f32, jax.ShapeDtypeStruct((t, D), BF16), f32, f32, f32, f32, f32],
        scratch_shapes=[pltpu.VMEM((tm + 8, D), F32), pltpu.VMEM((8, D), F32)],
        compiler_params=_params("arbitrary"),
    )(xr, xr, xg, conv_w, conv_b, wa2, ba, wx2, bx, lam)


def _bias_fwd(table_t, onehot_t):
    def body(t_ref, e_ref, o_ref):
        o_ref[...] = jnp.dot(t_ref[...], e_ref[...], preferred_element_type=F32, precision=lax.Precision.HIGHEST)

    return pl.pallas_call(body, out_shape=jax.ShapeDtypeStruct((N_HEADS, CHUNK * KB), F32), name="bias_fwd",
                          compiler_params=_params())(table_t, onehot_t)


def _bias_bwd(dbias_flat, onehot_t, ds_rows):
    def body(d_ref, e_ref, s_ref, o_ref, so_ref):
        o_ref[...] = lax.dot_general(d_ref[...], e_ref[...], (((1,), (1,)), ((), ())), preferred_element_type=F32,
                                     precision=lax.Precision.HIGHEST)
        so_ref[...] = jnp.zeros_like(so_ref)
        for r in range(4):
            so_ref[:, r:r + 1] = jnp.sum(s_ref[:, r * CHUNK:(r + 1) * CHUNK], axis=1, keepdims=True)

    return pl.pallas_call(body, out_shape=[jax.ShapeDtypeStruct((N_HEADS, N_BUCKETS), F32), jax.ShapeDtypeStruct((8, 128), F32)],
                          name="bias_bwd", compiler_params=_params())(dbias_flat, onehot_t, ds_rows)


def _stack_heads(q):
    return jnp.concatenate(
        [jnp.concatenate([q[:, (4 * g + r) * HEAD_DIM:(4 * g + r + 1) * HEAD_DIM] for g in range(4)], axis=1)
         for r in range(4)], axis=0)


def _unstack_heads(o):
    return jnp.concatenate([o[r * CHUNK:(r + 1) * CHUNK, g * HEAD_DIM:(g + 1) * HEAD_DIM] for g in range(4) for r in range(4)],
                           axis=1)


def _block_diag(w, mask):
    return jnp.concatenate([w] * 4, axis=0) * mask


def _group_softmax(qk, bias_g, sink, valid):
    s = qk * (HEAD_DIM ** -0.5) + bias_g
    s = jnp.where(valid, s, NEG_INF)
    m = jnp.maximum(jnp.max(s, axis=0, keepdims=True), sink)
    e = jnp.exp(s - m)
    es = jnp.exp(sink - m)
    inv = 1.0 / (jnp.sum(e, axis=0, keepdims=True) + es)
    return e * inv, es * inv


def _attn_fwd(sink_rows, q, kp, vp, bias_t, mask):
    t = q.shape[0]
    per_step = 4

    def body(sink_ref, q_ref, kp_ref, vp_ref, bias_ref, mask_ref, o_ref):
        owns = [mask_ref[g * KP:(g + 1) * KP, :] for g in range(4)]
        for k in range(per_step):
            c = pl.program_id(0) * per_step + k
            rows = slice(k * CHUNK, (k + 1) * CHUNK)
            st = pl.multiple_of(c * CHUNK, CHUNK)
            kw = kp_ref[pl.ds(st, KP), :]
            vw = vp_ref[pl.ds(st, KP), :]
            q_all = _stack_heads(q_ref[rows, :])
            valid = lax.broadcasted_iota(jnp.int32, (KP, 1), 0) + c * CHUNK >= PAD_KEYS
            scores = [_nt(kw * owns[g], q_all) for g in range(4)]
            ps = [_group_softmax(scores[g], bias_ref[g * KP:(g + 1) * KP, :], sink_ref[g:g + 1, :], valid)[0]
                  for g in range(4)]
            o_all = sum(_tn(ps[g].astype(BF16), vw * owns[g]) for g in range(4))
            o_ref[rows, :] = _unstack_heads(o_all).astype(BF16)

    return pl.pallas_call(
        body, grid=(t // (per_step * CHUNK),), name="attn_fwd",
        in_specs=[_WHOLE, _row_spec(per_step * CHUNK, D), _WHOLE, _WHOLE, _WHOLE, _WHOLE],
        out_specs=_row_spec(per_step * CHUNK, D),
        out_shape=jax.ShapeDtypeStruct((t, D), BF16),
        compiler_params=_params("arbitrary"),
    )(sink_rows, q, kp, vp, bias_t, mask)


def _merge_fwd(yain, o, gate, h1, w_lru, w_att, w_o, gpost):
    t = h1.shape[0]
    tm = _tile(t)

    def body(ya_ref, o_ref, g_ref, h_ref, wl_ref, wa_ref, wo_ref, gp_ref, h2_ref, mo_ref, mg_ref, ya_out, yb_out):
        ya = _nn(ya_ref[...], wl_ref[...])
        yb = _nn(o_ref[...], wa_ref[...])
        mg = (g_ref[:, 0:D] * ya + g_ref[:, D:2 * D] * yb).astype(BF16)
        mo = _nn(mg, wo_ref[...])
        ya_out[...] = ya.astype(BF16)
        yb_out[...] = yb.astype(BF16)
        mg_ref[...] = mg
        mo_ref[...] = mo
        h2_ref[...] = h_ref[...] + _rms(mo, gp_ref[...])

    f32 = jax.ShapeDtypeStruct((t, D), F32)
    b16 = jax.ShapeDtypeStruct((t, D), BF16)
    return pl.pallas_call(
        body, grid=(t // tm,), name="merge_fwd",
        in_specs=[_row_spec(tm, D), _row_spec(tm, D), _row_spec(tm, 2 * D), _row_spec(tm, D), _WHOLE, _WHOLE, _WHOLE,
                  _vec_spec(D)],
        out_specs=[_row_spec(tm, D)] * 5,
        out_shape=[f32, f32, b16, b16, b16],
        compiler_params=_params("arbitrary"),
    )(yain, o, gate, h1, w_lru, w_att, w_o, gpost)


def _ffn_bwd(dh, x, f, a, b, gpre, gpost, w1g, w3g, w2g, name):
    t = x.shape[0]
    tm = _tile(t, TM_SCAN)

    def body(dh_ref, x_ref, f_ref, a_ref, b_ref, gpre_ref, gpost_ref, w1_ref, w3_ref, w2_ref,
             dx_ref, n_ref, da_ref, db_ref, df_ref, dgpre_ref, dgpost_ref):
        @pl.when(pl.program_id(0) == 0)
        def _():
            dgpre_ref[...] = jnp.zeros_like(dgpre_ref)
            dgpost_ref[...] = jnp.zeros_like(dgpost_ref)

        dhv = dh_ref[...]
        xv = x_ref[...]
        df, dgp = _rms_bwd(0.5 * dhv, f_ref[...], gpost_ref[...])
        dgpost_ref[...] += dgp
        dfb = df.astype(BF16)
        df_ref[...] = dfb
        n_ref[...] = _rms(xv, gpre_ref[...]).astype(BF16)
        dn = jnp.zeros((tm, D), F32)
        for s in range(NSH):
            av = a_ref[s].astype(F32)
            bv = b_ref[s].astype(F32)
            sg = jax.nn.sigmoid(av)
            dhm = _nt(dfb, w2_ref[s])
            dab = (dhm * bv * (sg * (1.0 + av * (1.0 - sg)))).astype(BF16)
            dbb = (dhm * (av * sg)).astype(BF16)
            da_ref[s] = dab
            db_ref[s] = dbb
            dn = dn + _nn(dab, w1_ref[s]) + _nn(dbb, w3_ref[s])
        dxn, dg = _rms_bwd(dn, xv, gpre_ref[...])
        dgpre_ref[...] += dg
        dx_ref[...] = dhv + dxn

    sh = pl.BlockSpec((NSH, tm, FF_S), lambda i: (0, i, 0))
    act = jax.ShapeDtypeStruct((NSH, t, FF_S), BF16)
    vec = jax.ShapeDtypeStruct((1, D), F32)
    return pl.pallas_call(
        body, grid=(t // tm,), name=name,
        in_specs=[_row_spec(tm, D), _row_spec(tm, D), _row_spec(tm, D), sh, sh, _vec_spec(D), _vec_spec(D), _WHOLE, _WHOLE,
                  _WHOLE],
        out_specs=[_row_spec(tm, D), _row_spec(tm, D), sh, sh, _row_spec(tm, D), _vec_spec(D), _vec_spec(D)],
        out_shape=[jax.ShapeDtypeStruct((t, D), F32), jax.ShapeDtypeStruct((t, D), BF16), act, act,
                   jax.ShapeDtypeStruct((t, D), BF16), vec, vec],
        compiler_params=_params("arbitrary"),
    )(dh, x, f, a, b, gpre, gpost, w1g, w3g, w2g)


def _behind(body, after):
    if after is None:
        return body, [], []

    def ordered(_, *refs):
        body(*refs)

    return ordered, [_ANY], [after]


def _ffn_bwd_acts(dh, x, f, a, b, gpre, gpost, w2g, name):
    t = x.shape[0]
    tm = _tile(t)

    def body(dh_ref, x_ref, f_ref, a_ref, b_ref, gpre_ref, gpost_ref, w2_ref, n_ref, da_ref, db_ref, df_ref, dgpost_ref):
        @pl.when(pl.program_id(0) == 0)
        def _():
            dgpost_ref[...] = jnp.zeros_like(dgpost_ref)

        df, dgp = _rms_bwd(0.5 * dh_ref[...], f_ref[...], gpost_ref[...])
        dgpost_ref[...] += dgp
        dfb = df.astype(BF16)
        df_ref[...] = dfb
        n_ref[...] = _rms(x_ref[...], gpre_ref[...]).astype(BF16)
        for s in range(NSH):
            av = a_ref[s].astype(F32)
            bv = b_ref[s].astype(F32)
            sg = jax.nn.sigmoid(av)
            dhm = _nt(dfb, w2_ref[s])
            da_ref[s] = (dhm * bv * (sg * (1.0 + av * (1.0 - sg)))).astype(BF16)
            db_ref[s] = (dhm * (av * sg)).astype(BF16)

    sh = pl.BlockSpec((NSH, tm, FF_S), lambda i: (0, i, 0))
    act = jax.ShapeDtypeStruct((NSH, t, FF_S), BF16)
    b16 = jax.ShapeDtypeStruct((t, D), BF16)
    return pl.pallas_call(
        body, grid=(t // tm,), name=name,
        in_specs=[_row_spec(tm, D), _row_spec(tm, D), _row_spec(tm, D), sh, sh, _vec_spec(D), _vec_spec(D), _WHOLE],
        out_specs=[_row_spec(tm, D), sh, sh, _row_spec(tm, D), _vec_spec(D)],
        out_shape=[b16, act, act, b16, jax.ShapeDtypeStruct((1, D), F32)],
        compiler_params=_params("arbitrary"),
    )(dh, x, f, a, b, gpre, gpost, w2g)


def _ffn_bwd_input(dh, x, da, db, gpre, w1g, w3g, name, after):
    t = x.shape[0]
    tm = _tile(t)

    def body(dh_ref, x_ref, da_ref, db_ref, gpre_ref, w1_ref, w3_ref, dx_ref, dgpre_ref):
        @pl.when(pl.program_id(0) == 0)
        def _():
            dgpre_ref[...] = jnp.zeros_like(dgpre_ref)

        dn = jnp.zeros((tm, D), F32)
        for s in range(NSH):
            dn = dn + _nn(da_ref[s], w1_ref[s]) + _nn(db_ref[s], w3_ref[s])
        dxn, dg = _rms_bwd(dn, x_ref[...], gpre_ref[...])
        dgpre_ref[...] += dg
        dx_ref[...] = dh_ref[...] + dxn

    sh = pl.BlockSpec((NSH, tm, FF_S), lambda i: (0, i, 0))
    body, specs, operands = _behind(body, after)
    return pl.pallas_call(
        body, grid=(t // tm,), name=name,
        in_specs=specs + [_row_spec(tm, D), _row_spec(tm, D), sh, sh, _vec_spec(D), _WHOLE, _WHOLE],
        out_specs=[_row_spec(tm, D), _vec_spec(D)],
        out_shape=[jax.ShapeDtypeStruct((t, D), F32), jax.ShapeDtypeStruct((1, D), F32)],
        compiler_params=_params("arbitrary"),
    )(*operands, dh, x, da, db, gpre, w1g, w3g)


def _wgrad(a, b, a_spec, b_spec, out_spec, out_shape, grid, name, after=None):
    def body(a_ref, b_ref, o_ref):
        o_ref[...] = _tn(a_ref[...], b_ref[...]).astype(BF16)

    body, specs, operands = _behind(body, after)
    return pl.pallas_call(body, grid=grid, name=name, in_specs=specs + [a_spec, b_spec], out_specs=out_spec,
                          out_shape=jax.ShapeDtypeStruct(out_shape, BF16),
                          compiler_params=_params(*("arbitrary",) * len(grid)))(*operands, a, b)


def _wgrad_cols(act, dsh, width, name, after=None):
    t = act.shape[0]
    if dsh.ndim == 3:
        b_spec = pl.BlockSpec((None, t, width), lambda s, k: (s, 0, 0))
    else:
        b_spec = pl.BlockSpec((t, width), lambda s, k: (0, s))
    return _wgrad(act, dsh, pl.BlockSpec((t, 512), lambda s, k: (0, k)), b_spec,
                  pl.BlockSpec((None, 512, width), lambda s, k: (s, k, 0)), (NSH, D, width), (NSH, 2), name, after)


def _wgrad_rows(hm, df, name, after=None):
    t = df.shape[0]
    return _wgrad(hm, df, pl.BlockSpec((None, t, FF_S), lambda s: (s, 0, 0)), pl.BlockSpec((t, D), lambda s: (0, 0)),
                  pl.BlockSpec((None, FF_S, D), lambda s: (s, 0, 0)), (NSH, FF_S, D), (NSH,), name, after)


def _wgrad_sq(a, b, name, after=None):
    t = a.shape[0]
    return _wgrad(a, b, pl.BlockSpec((t, 512), lambda i, j: (0, i)), pl.BlockSpec((t, 512), lambda i, j: (0, j)),
                  pl.BlockSpec((512, 512), lambda i, j: (i, j)), (D, D), (2, 2), name, after)


def _mix_bwd1(dh2, mo, gpost, gate, ya, yb, xg, hr, w_o, w_lru, w_att, after):
    t = dh2.shape[0]
    tm = _tile(t, TM_SCAN)

    def body(dh_ref, mo_ref, gp_ref, g_ref, ya_ref, yb_ref, xg_ref, hr_ref, wo_ref, wl_ref, wa_ref,
             dmo_ref, dya_ref, dyb_ref, dgate_ref, dhr_ref, dxg_ref, do_ref, dgp_ref, dbg_ref):
        @pl.when(pl.program_id(0) == 0)
        def _():
            dgp_ref[...] = jnp.zeros_like(dgp_ref)
            dbg_ref[...] = jnp.zeros_like(dbg_ref)

        dmo, dgp = _rms_bwd(dh_ref[...], mo_ref[...], gp_ref[...])
        dgp_ref[...] += dgp
        dmob = dmo.astype(BF16)
        dmo_ref[...] = dmob
        dm = _nt(dmob, wo_ref[...])
        g0 = g_ref[:, 0:D]
        g1 = g_ref[:, D:2 * D]
        dyab = (dm * g0).astype(BF16)
        dybb = (dm * g1).astype(BF16)
        dya_ref[...] = dyab
        dyb_ref[...] = dybb
        dg0 = dm * ya_ref[...].astype(F32) * (g0 * (1.0 - g0))
        dg1 = dm * yb_ref[...].astype(F32) * (g1 * (1.0 - g1))
        dgate_ref[:, 0:D] = dg0.astype(BF16)
        dgate_ref[:, D:2 * D] = dg1.astype(BF16)
        dbg_ref[:, 0:D] += jnp.sum(dg0, axis=0, keepdims=True)
        dbg_ref[:, D:2 * D] += jnp.sum(dg1, axis=0, keepdims=True)
        dyain = _nt(dyab, wl_ref[...])
        do_ref[...] = _nt(dybb, wa_ref[...]).astype(BF16)
        xgv = xg_ref[...]
        gelu, gelu_grad = _gelu_and_grad(xgv)
        dhr_ref[...] = dyain * gelu
        dxg_ref[...] = (dyain * hr_ref[...] * gelu_grad).astype(BF16)

    b16 = jax.ShapeDtypeStruct((t, D), BF16)
    body, specs, operands = _behind(body, after)
    return pl.pallas_call(
        body, grid=(t // tm,), name="mix_bwd1",
        in_specs=specs + [_row_spec(tm, D), _row_spec(tm, D), _vec_spec(D), _row_spec(tm, 2 * D), _row_spec(tm, D),
                          _row_spec(tm, D), _row_spec(tm, D), _row_spec(tm, D), _WHOLE, _WHOLE, _WHOLE],
        out_specs=[_row_spec(tm, D), _row_spec(tm, D), _row_spec(tm, D), _row_spec(tm, 2 * D), _row_spec(tm, D),
                   _row_spec(tm, D), _row_spec(tm, D), _vec_spec(D), _vec_spec(2 * D)],
        out_shape=[b16, b16, b16, jax.ShapeDtypeStruct((t, 2 * D), BF16), jax.ShapeDtypeStruct((t, D), F32), b16, b16,
                   jax.ShapeDtypeStruct((1, D), F32), jax.ShapeDtypeStruct((1, 2 * D), F32)],
        compiler_params=_params("arbitrary"),
    )(*operands, dh2, mo, gpost, gate, ya, yb, xg, hr, w_o, w_lru, w_att)


def _rglru_bwd(dhr, hr, xc, r, ig, a, s, xr, conv_w, wa2, wx2, lam, after):
    t = dhr.shape[0]
    tm = _tile(t, TM_SCAN)
    nb8 = tm // 8
    nt = t // tm

    def body(dhr_ref, hr_ref, hrp_ref, xc_ref, r_ref, ig_ref, a_sc, s_ref, xr_ref, xrp_ref, cw_ref, wa_ref, wx_ref, lam_ref,
             dxr_ref, dwa_ref, dwx_ref, dba_ref, dbx_ref, dlam_ref, dcw_ref, dcb_ref,
             ext_h, ext_x, ext_d, g_sc, c_sc, nxt_sc):
        i = pl.program_id(0)
        first_tile = i == nt - 1

        @pl.when(i == 0)
        def _():
            c_sc[...] = jnp.zeros_like(c_sc)
            nxt_sc[...] = jnp.zeros_like(nxt_sc)
            for ref in (dwa_ref, dwx_ref, dba_ref, dbx_ref, dlam_ref, dcw_ref, dcb_ref):
                ref[...] = jnp.zeros_like(ref)

        lamv = lam_ref[...]
        sp = _softplus_neg(lamv)
        rv = r_ref[...]
        igv = ig_ref[...]
        xcv = xc_ref[...]
        a = a_sc[...]
        s = s_ref[...]

        def blk(jj, c):
            st = pl.multiple_of((nb8 - 1 - jj) * 8, 8)
            d8 = dhr_ref[pl.ds(st, 8), :]
            a8 = a_sc[pl.ds(st, 8), :]
            rows = [None] * 8
            for k in range(7, -1, -1):
                g = d8[k:k + 1, :] + c
                c = a8[k:k + 1, :] * g
                rows[k] = g
            g_sc[pl.ds(st, 8), :] = jnp.concatenate(rows, axis=0)
            return c

        c_sc[0:1, :] = lax.fori_loop(0, nb8, blk, c_sc[0:1, :])
        g = g_sc[...]
        ext_h[0:8, :] = jnp.where(first_tile, 0.0, hrp_ref[...])
        ext_h[8:8 + tm, :] = hr_ref[...]
        hprev = ext_h[pl.ds(7, tm), :]
        d_s = g * (igv * xcv)
        dig = g * s * xcv
        dxc = g * s * igv
        dla = (g * hprev) * a - d_s * ((a * a) / s)
        dr_pre = (dla * (-LRU_C * sp)) * (rv * (1.0 - rv))
        di_pre = dig * (igv * (1.0 - igv))
        dlam_ref[...] += jnp.sum(dla * (LRU_C * rv), axis=0, keepdims=True) * jax.nn.sigmoid(-lamv)
        dba_ref[...] += jnp.sum(dr_pre, axis=0, keepdims=True)
        dbx_ref[...] += jnp.sum(di_pre, axis=0, keepdims=True)
        drb = dr_pre.astype(BF16)
        dib = di_pre.astype(BF16)
        xcb = xcv.astype(BF16)
        ext_d[tm:tm + 8, :] = nxt_sc[...]
        for p in range(8):
            sl = slice(p * 128, (p + 1) * 128)
            ext_d[0:tm, sl] = dxc[:, sl] + _nt(drb[:, sl], wa_ref[p]) + _nt(dib[:, sl], wx_ref[p])
            dwa_ref[p] += _tn(xcb[:, sl], drb[:, sl])
            dwx_ref[p] += _tn(xcb[:, sl], dib[:, sl])
        dxcv = ext_d[0:tm, :]
        nxt_sc[...] = ext_d[0:8, :]
        dcb_ref[...] += jnp.sum(dxcv, axis=0, keepdims=True)
        ext_x[0:8, :] = jnp.where(first_tile, 0.0, xrp_ref[...])
        ext_x[8:8 + tm, :] = xr_ref[...]
        dxr = jnp.zeros((tm, D), F32)
        for tap in range(4):
            dxr = dxr + ext_d[pl.ds(3 - tap, tm), :] * cw_ref[tap:tap + 1, :]
            dcw_ref[tap:tap + 1, :] += jnp.sum(dxcv * ext_x[pl.ds(5 + tap, tm), :], axis=0, keepdims=True)
        dxr_ref[...] = dxr.astype(BF16)

    rev = pl.BlockSpec((tm, D), lambda i: (nt - 1 - i, 0))
    prev = pl.BlockSpec((8, D), lambda i: (jnp.maximum((nt - 1 - i) * nb8 - 1, 0), 0))
    full = lambda shape: pl.BlockSpec(shape, lambda i: tuple(0 for _ in shape))
    vec = jax.ShapeDtypeStruct((1, D), F32)
    blocks = jax.ShapeDtypeStruct((8, 128, 128), F32)
    body, specs, operands = _behind(body, after)
    return pl.pallas_call(
        body, grid=(nt,), name="rglru_bwd",
        in_specs=specs + [rev, rev, prev, rev, rev, rev, rev, rev, rev, prev, full((4, D)), full((8, 128, 128)),
                          full((8, 128, 128)), _vec_spec(D)],
        out_specs=[rev, full((8, 128, 128)), full((8, 128, 128)), _vec_spec(D), _vec_spec(D), _vec_spec(D), full((4, D)),
                   _vec_spec(D)],
        out_shape=[jax.ShapeDtypeStruct((t, D), BF16), blocks, blocks, vec, vec, vec, jax.ShapeDtypeStruct((4, D), F32), vec],
        scratch_shapes=[pltpu.VMEM((tm + 8, D), F32), pltpu.VMEM((tm + 8, D), F32), pltpu.VMEM((tm + 8, D), F32),
                        pltpu.VMEM((tm, D), F32), pltpu.VMEM((8, D), F32), pltpu.VMEM((8, D), F32)],
        compiler_params=_params("arbitrary"),
    )(*operands, dhr, hr, hr, xc, r, ig, a, s, xr, xr, conv_w, wa2, wx2, lam)


def _attn_bwd(sink_rows, q, kp, vp, bias_t, mask, do):
    t = q.shape[0]
    tp = kp.shape[0]
    per_step = 4

    def body(sink_ref, q_ref, kp_ref, vp_ref, bias_ref, mask_ref, do_ref, dq_ref, dk_ref, dv_ref, dbias_ref, ds_ref):
        @pl.when(pl.program_id(0) == 0)
        def _():
            for ref in (dk_ref, dv_ref, dbias_ref, ds_ref):
                ref[...] = jnp.zeros_like(ref)

        maskv = mask_ref[...]
        lane_group = lax.broadcasted_iota(jnp.int32, (1, 4 * HEAD_DIM), 1) // HEAD_DIM

        def own_blocks(full):
            out = full[0:KP]
            for g in range(1, 4):
                out = jnp.where(lane_group == g, full[g * KP:(g + 1) * KP], out)
            return out

        dsc_sum, dsinks, dks, dvs = 0.0, [0.0] * 4, [], []
        for k in range(per_step):
            c = pl.program_id(0) * per_step + k
            chunk = slice(k * CHUNK, (k + 1) * CHUNK)
            st = pl.multiple_of(c * CHUNK, CHUNK)
            kbd = _block_diag(kp_ref[pl.ds(st, KP), :], maskv)
            vbd = _block_diag(vp_ref[pl.ds(st, KP), :], maskv)
            q_all = _stack_heads(q_ref[chunk, :])
            do_all = _stack_heads(do_ref[chunk, :])
            valid = lax.broadcasted_iota(jnp.int32, (KP, 1), 0) + c * CHUNK >= PAD_KEYS
            qk = _nt(kbd, q_all)
            dp = _nt(vbd, do_all)
            ps, dscs = [], []
            for g in range(4):
                rows = slice(g * KP, (g + 1) * KP)
                p, sink_p = _group_softmax(qk[rows], bias_ref[rows, :], sink_ref[g:g + 1, :], valid)
                delta = jnp.sum(p * dp[rows], axis=0, keepdims=True)
                ps.append(p)
                dscs.append(p * (dp[rows] - delta))
                dsinks[g] = dsinks[g] - sink_p * delta
            dsc = jnp.concatenate(dscs, axis=0)
            dsc_sum = dsc_sum + dsc
            dsb = (dsc * (HEAD_DIM ** -0.5)).astype(BF16)
            dq_ref[chunk, :] = _unstack_heads(_tn(dsb, kbd)).astype(BF16)
            dks.append((st, own_blocks(_nn(dsb, q_all))))
            dvs.append((st, own_blocks(_nn(jnp.concatenate(ps, axis=0).astype(BF16), do_all))))
        dbias_ref[...] += dsc_sum
        for g in range(4):
            ds_ref[g:g + 1, :] += dsinks[g]
        for (st, dkw), (_, dvw) in zip(dks, dvs):
            dk_ref[pl.ds(st, KP), :] += dkw
            dv_ref[pl.ds(st, KP), :] += dvw

    full = lambda shape: pl.BlockSpec(shape, lambda i: tuple(0 for _ in shape))
    return pl.pallas_call(
        body, grid=(t // (per_step * CHUNK),), name="attn_bwd",
        in_specs=[_WHOLE, _row_spec(per_step * CHUNK, D), _WHOLE, _WHOLE, _WHOLE, _WHOLE, _row_spec(per_step * CHUNK, D)],
        out_specs=[_row_spec(per_step * CHUNK, D), full((tp, KV_W)), full((tp, KV_W)), full((4 * KP, 4 * CHUNK)),
                   full((8, 4 * CHUNK))],
        out_shape=[jax.ShapeDtypeStruct((t, D), BF16), jax.ShapeDtypeStruct((tp, KV_W), F32),
                   jax.ShapeDtypeStruct((tp, KV_W), F32), jax.ShapeDtypeStruct((4 * KP, 4 * CHUNK), F32),
                   jax.ShapeDtypeStruct((8, 4 * CHUNK), F32)],
        compiler_params=_params("arbitrary"),
    )(sink_rows, q, kp, vp, bias_t, mask, do)


def _mix_bwd2(dproj, dgate, h1, dh2, gmix, w_in_g, w_gate_g, after):
    t = h1.shape[0]
    tm = _tile(t)

    def body(dp_ref, dg_ref, h_ref, dh_ref, g_ref, win_ref, wg_ref, dh1_ref, dgm_ref):
        @pl.when(pl.program_id(0) == 0)
        def _():
            dgm_ref[...] = jnp.zeros_like(dgm_ref)

        du = jnp.zeros((tm, D), F32)
        for s in range(NSH):
            du = du + _nt(dp_ref[:, s * IN_S:(s + 1) * IN_S], win_ref[s])
            du = du + _nt(dg_ref[:, s * GATE_S:(s + 1) * GATE_S], wg_ref[s])
        dxn, dg = _rms_bwd(du, h_ref[...], g_ref[...])
        dgm_ref[...] += dg
        dh1_ref[...] = dh_ref[...] + dxn

    body, specs, operands = _behind(body, after)
    return pl.pallas_call(
        body, grid=(t // tm,), name="mix_bwd2",
        in_specs=specs + [_row_spec(tm, NSH * IN_S), _row_spec(tm, 2 * D), _row_spec(tm, D), _row_spec(tm, D), _vec_spec(D),
                          _WHOLE, _WHOLE],
        out_specs=[_row_spec(tm, D), _vec_spec(D)],
        out_shape=[jax.ShapeDtypeStruct((t, D), F32), jax.ShapeDtypeStruct((1, D), F32)],
        compiler_params=_params("arbitrary"),
    )(*operands, dproj, dgate, h1, dh2, gmix, w_in_g, w_gate_g)


def _band_onehot():
    nb = N_BUCKETS // 2
    max_exact = nb // 2
    rel = jnp.arange(KB)[None, :] - PAD_KEYS - jnp.arange(CHUNK)[:, None]
    ret = jnp.where(rel > 0, nb, 0)
    n = jnp.abs(rel)
    nf = jnp.maximum(n, 1).astype(jnp.float32)
    large = max_exact + (jnp.log(nf / max_exact) / math.log(128 / max_exact) * (nb - max_exact)).astype(jnp.int32)
    large = jnp.minimum(large, nb - 1)
    buckets = (ret + jnp.where(n < max_exact, n, large)).reshape(1, CHUNK * KB)
    return (buckets == jnp.arange(N_BUCKETS)[:, None]).astype(F32)


def _pair_blocks(w):
    pairs = w.reshape(8, 2, 64, 64)
    z = jnp.zeros((8, 64, 64), w.dtype)
    return jnp.concatenate([jnp.concatenate([pairs[:, 0], z], axis=2), jnp.concatenate([z, pairs[:, 1]], axis=2)], axis=1)


def _unpair_blocks(w2):
    return jnp.stack([w2[:, 0:64, 0:64], w2[:, 64:128, 64:128]], axis=1).reshape(16, 64, 64)


def _local_step(x, target, weights, sm, reducer):
    row = lambda v: v.reshape(1, -1)
    wg = dict(weights("ffn1", x))
    sm = dict(sm, conv_w=wg["conv_w"])
    onehot_t = _band_onehot()
    bias = _bias_fwd(sm["rel_bias"].T, onehot_t).reshape(4, 4, CHUNK, KB)
    bias_t = jnp.pad(jnp.transpose(bias, (0, 3, 1, 2)), ((0, 0), (0, KP - KB), (0, 0), (0, 0))).reshape(4 * KP, 4 * CHUNK)
    sink_rows = jnp.pad(jnp.repeat(sm["attn_sinks"].reshape(4, 4), CHUNK, axis=1), ((0, 4), (0, 0)))
    grp = jnp.arange(4 * KP)[:, None] // KP == jnp.arange(4 * HEAD_DIM)[None, :] // HEAD_DIM
    mask = (grp & (jnp.arange(4 * KP)[:, None] % KP < KB)).astype(BF16)
    wa2 = _pair_blocks(sm["rg_a_w"]).astype(BF16)
    wx2 = _pair_blocks(sm["rg_x_w"]).astype(BF16)

    h1, a1, b1, hm1, f1 = _ffn_fwd(x, row(sm["ffn1_pre_g"]), wg["ffn1_w1"], wg["ffn1_w3"], wg["ffn1_w2"],
                                   row(sm["ffn1_post_g"]), "ffn1_fwd")
    wg.update(weights("mix", h1))
    w_lru = wg["w_lru_out"].reshape(D, D)
    w_att = wg["w_attn_out"].reshape(D, D)
    w_o = wg["w_o"].reshape(D, D)
    u, q, k, v, xr, xg, gate = _mix_proj(h1, row(sm["mix_pre_g"]), wg["w_in"], wg["w_gate"], row(sm["b_gate"]))
    hr, yain, xc, r, ig, lru_a, lru_s = _rglru_fwd(xr, xg, sm["conv_w"], row(sm["conv_b"]), wa2, row(sm["rg_a_b"]), wx2,
                                     row(sm["rg_x_b"]), row(sm["lru_lambda"]))
    kp = jnp.pad(k, ((PAD_KEYS, KP - KB), (0, 0)))
    vp = jnp.pad(v, ((PAD_KEYS, KP - KB), (0, 0)))
    o = _attn_fwd(sink_rows, q, kp, vp, bias_t, mask)
    wg.update(weights("ffn2", o))
    h2, mo, merged, ya, yb = _merge_fwd(yain, o, gate, h1, w_lru, w_att, w_o, row(sm["mix_post_g"]))
    dy, a2, b2, hm2, f2, sq = _ffn_fwd(h2, row(sm["ffn2_pre_g"]), wg["ffn2_w1"], wg["ffn2_w3"], wg["ffn2_w2"],
                                       row(sm["ffn2_post_g"]), "ffn2_fwd", target)

    big, small = {}, {}
    dh2, n2, da2, db2, df2, small["ffn2_pre_g"], small["ffn2_post_g"] = _ffn_bwd(
        dy, h2, f2, a2, b2, row(sm["ffn2_pre_g"]), row(sm["ffn2_post_g"]), wg["ffn2_w1"], wg["ffn2_w3"], wg["ffn2_w2"],
        "ffn2_bwd")
    big["ffn2_w1"] = _wgrad_rows(da2, n2, "dw_ffn2_w1")
    big["ffn2_w3"] = _wgrad_rows(db2, n2, "dw_ffn2_w3")
    big["ffn2_w2"] = _wgrad_rows(hm2, df2, "dw_ffn2_w2")
    token = reducer.begin("ffn2", {n: big[n] for n in ("ffn2_w1", "ffn2_w3", "ffn2_w2")})
    dmo, dya, dyb, dgate, dhr, dxg, do, small["mix_post_g"], small["b_gate"] = _mix_bwd1(
        dh2, mo, row(sm["mix_post_g"]), gate, ya, yb, xg, hr, w_o, w_lru, w_att, token)
    big["w_o"] = _wgrad_sq(merged, dmo, "dw_w_o").reshape(NSH, D // NSH, D)
    big["w_lru_out"] = _wgrad_sq(yain, dya, "dw_w_lru_out").reshape(NSH, D // NSH, D)
    big["w_attn_out"] = _wgrad_sq(o, dyb, "dw_w_attn_out").reshape(NSH, D // NSH, D)
    token = reducer.advance("ffn2", big["w_attn_out"])
    (dxr, dwa2, dwx2, small["rg_a_b"], small["rg_x_b"], small["lru_lambda"], small["conv_w"], small["conv_b"]) = _rglru_bwd(
        dhr, hr, xc, r, ig, lru_a, lru_s, xr, sm["conv_w"], wa2, wx2, row(sm["lru_lambda"]), token)
    small["rg_a_w"] = _unpair_blocks(dwa2)
    small["rg_x_w"] = _unpair_blocks(dwx2)
    dq, dkp, dvp, dbias_t, ds_rows = _attn_bwd(sink_rows, q, kp, vp, bias_t, mask, do)
    dbias = jnp.transpose(dbias_t.reshape(4, KP, 4, CHUNK)[:, :KB], (0, 2, 3, 1)).reshape(N_HEADS, CHUNK * KB)
    drel_t, dsinks = _bias_bwd(dbias, onehot_t, ds_rows)
    small["attn_sinks"] = dsinks[0:4, 0:4].reshape(N_HEADS)
    small["rel_bias"] = drel_t.T
    t = x.shape[0]
    dproj = jnp.concatenate([dq, dkp[PAD_KEYS:PAD_KEYS + t].astype(BF16), dvp[PAD_KEYS:PAD_KEYS + t].astype(BF16), dxr, dxg],
                            axis=1)
    big["w_in"] = _wgrad_cols(u, dproj, IN_S, "dw_w_in")
    big["w_gate"] = _wgrad_cols(u, dgate, GATE_S, "dw_w_gate")
    token = reducer.begin("mix", {n: big[n] for n in ("w_in", "w_gate", "w_lru_out", "w_attn_out", "w_o")})
    dh1, small["mix_pre_g"] = _mix_bwd2(dproj, dgate, h1, dh2, row(sm["mix_pre_g"]), wg["w_in"], wg["w_gate"], token)
    n1, da1, db1, df1, small["ffn1_post_g"] = _ffn_bwd_acts(
        dh1, x, f1, a1, b1, row(sm["ffn1_pre_g"]), row(sm["ffn1_post_g"]), wg["ffn1_w2"], "ffn1_bwd_acts")
    token = reducer.advance("mix", df1)
    big["ffn1_w1"] = _wgrad_rows(da1, n1, "dw_ffn1_w1", token)
    big["ffn1_w3"] = _wgrad_rows(db1, n1, "dw_ffn1_w3", token)
    big["ffn1_w2"] = _wgrad_rows(hm1, df1, "dw_ffn1_w2", token)
    token = reducer.begin("ffn1", {n: big[n] for n in ("ffn1_w1", "ffn1_w3", "ffn1_w2")})
    dx, small["ffn1_pre_g"] = _ffn_bwd_input(dh1, x, da1, db1, row(sm["ffn1_pre_g"]), wg["ffn1_w1"], wg["ffn1_w3"],
                                             "ffn1_bwd_input", token)
    return sq, dx, big, small


_ANY = pl.BlockSpec(memory_space=pl.ANY)


def _place():
    return lax.axis_index("x"), lax.axis_index("y"), lax.axis_index("c")


def _other_chips(x, y):
    return [(1 - x, y), (x, 1 - y), (1 - x, 1 - y)]


_HBM = pl.BlockSpec(memory_space=pltpu.HBM)
_SEM = pl.BlockSpec(memory_space=pltpu.SEMAPHORE)
_EFFECT = pltpu.SideEffectType.DATAFLOW_SIDE_EFFECTING


def _cast_into_slot(w, chip, name, after=None):
    r, cc = w.shape
    rows = r // 4

    def body(chip_ref, *refs):
        w_ref, o_ref = refs[-2:]
        o_ref[...] = w_ref[...].astype(BF16)

    extra = [] if after is None else [after]
    return pl.pallas_call(
        body, name=name, out_shape=jax.ShapeDtypeStruct((NSH, r, cc), BF16),
        grid_spec=pltpu.PrefetchScalarGridSpec(
            num_scalar_prefetch=1, grid=(4,), in_specs=[_ANY] * len(extra) + [pl.BlockSpec((rows, cc), lambda i, chip: (i, 0))],
            out_specs=pl.BlockSpec((None, rows, cc), lambda i, chip: (chip[0], i, 0))),
        compiler_params=_params("arbitrary"))(chip, *extra, w)


def _piece(ref, slot, c):
    if ref.dtype == F32:
        return ref.at[slot]
    rh = ref.shape[1] // 2
    return ref.at[slot, pl.ds(pl.multiple_of(c * rh, 16), rh), :]


def _gather_start(stages, name):
    flat = [b for stage in stages for b in stage]
    n, ns = len(flat), len(stages)

    def body(*refs):
        ins, sems, token = refs[:n], refs[n:n + 2 * ns], refs[-1]
        x, y, c = _place()
        me = 2 * x + y
        k = 0
        for s, stage in enumerate(stages):
            for i in range(len(stage)):
                for j, (px, py) in enumerate(_other_chips(x, y)):
                    piece = _piece(ins[k], me, c)
                    pltpu.make_async_remote_copy(src_ref=piece, dst_ref=piece, send_sem=sems[2 * s].at[3 * i + j],
                                                 recv_sem=sems[2 * s + 1].at[3 * i + j], device_id=(px, py, c),
                                                 device_id_type=MESH).start()
                k += 1
        token[...] = jnp.zeros_like(token)

    sem_shapes = [pltpu.SemaphoreType.DMA((3 * len(stage),)) for stage in stages for _ in range(2)]
    outs = pl.pallas_call(
        body, name=name, in_specs=[_HBM] * n,
        out_specs=[_SEM] * (2 * ns) + [_HBM] * n + [pl.BlockSpec(memory_space=pltpu.VMEM)],
        out_shape=sem_shapes + [pltpu.HBM(b.shape, b.dtype) for b in flat] + [jax.ShapeDtypeStruct((8, 128), F32)],
        input_output_aliases={i: 2 * ns + i for i in range(n)},
        compiler_params=pltpu.CompilerParams(has_side_effects=_EFFECT),
    )(*[pltpu.with_memory_space_constraint(b, pltpu.HBM) for b in flat])
    sems, bufs, token = outs[:2 * ns], list(outs[2 * ns:2 * ns + n]), outs[-1]
    per_stage, k = [], 0
    for s, stage in enumerate(stages):
        per_stage.append((sems[2 * s], sems[2 * s + 1], bufs[k:k + len(stage)]))
        k += len(stage)
    return per_stage, token


def _gather_wait(send_sems, recv_sems, bufs, after, name):
    n = len(bufs)

    def body(*refs):
        ins, ssem, rsem = refs[:n], refs[n], refs[n + 1]
        x, y, c = _place()
        me = 2 * x + y
        for i in range(n):
            for j, (px, py) in enumerate(_other_chips(x, y)):
                cp = pltpu.make_async_remote_copy(src_ref=_piece(ins[i], me, c), dst_ref=_piece(ins[i], 2 * px + py, c),
                                                  send_sem=ssem.at[3 * i + j], recv_sem=rsem.at[3 * i + j],
                                                  device_id=(px, py, c), device_id_type=MESH)
                cp.wait_send()
                cp.wait_recv()

    return pl.pallas_call(
        body, name=name, in_specs=[_HBM] * n + [_SEM, _SEM, _ANY], out_specs=[_HBM] * n,
        out_shape=[pltpu.HBM(b.shape, b.dtype) for b in bufs], input_output_aliases={i: i for i in range(n)},
        compiler_params=pltpu.CompilerParams(has_side_effects=_EFFECT),
    )(*bufs, send_sems, recv_sems, after)


def _sibling_fill(bufs, name):
    n = len(bufs)

    def body(*refs):
        ins, outs = refs[:n], refs[n:2 * n]
        send_sems, recv_sems = refs[2 * n:]
        x, y, c = _place()
        copies = []
        for i in range(n):
            for j, (px, py) in enumerate(_other_chips(x, y)):
                copies.append(pltpu.make_async_remote_copy(
                    src_ref=_piece(ins[i], 2 * px + py, c), dst_ref=_piece(outs[i], 2 * px + py, c),
                    send_sem=send_sems.at[3 * i + j], recv_sem=recv_sems.at[3 * i + j], device_id=(x, y, 1 - c),
                    device_id_type=MESH))
                copies[-1].start()
        for cp in copies:
            cp.wait()

    return pl.pallas_call(
        body, name=name, in_specs=[_ANY] * n, out_specs=[_ANY] * n,
        out_shape=[jax.ShapeDtypeStruct(b.shape, b.dtype) for b in bufs], input_output_aliases={i: i for i in range(n)},
        scratch_shapes=[pltpu.SemaphoreType.DMA((3 * n,)), pltpu.SemaphoreType.DMA((3 * n,))],
        compiler_params=pltpu.CompilerParams(has_side_effects=True),
    )(*bufs)


def _swap_plan(srcs, lands):
    x, y, c = _place()
    plan = []
    for src, land in zip(srcs, lands):
        rh = src.shape[1] // 2
        plan.append((src.at[:, pl.ds(pl.multiple_of((1 - c) * rh, 16), rh), :], land, (x, y, 1 - c)))
    return plan


def _owners_plan(srcs, lands):
    x, y, c = _place()
    return [(src.at[2 * px + py], land.at[j], (px, py, c))
            for src, land in zip(srcs, lands) for j, (px, py) in enumerate(_other_chips(x, y))]


def _exchange_start(srcs, lands, plan, copies, name):
    n = len(srcs)

    def body(*refs):
        send_sems, recv_sems, token = refs[2 * n], refs[2 * n + 1], refs[-1]
        for k, (src, dst, dev) in enumerate(plan(refs[:n], refs[n:2 * n])):
            pltpu.make_async_remote_copy(src_ref=src, dst_ref=dst, send_sem=send_sems.at[k], recv_sem=recv_sems.at[k],
                                         device_id=dev, device_id_type=MESH).start()
        token[...] = jnp.zeros_like(token)

    both = list(srcs) + list(lands)
    outs = pl.pallas_call(
        body, name=name, in_specs=[_HBM] * (2 * n),
        out_specs=[_SEM, _SEM] + [_HBM] * (2 * n) + [pl.BlockSpec(memory_space=pltpu.VMEM)],
        out_shape=[pltpu.SemaphoreType.DMA((copies,)), pltpu.SemaphoreType.DMA((copies,))]
        + [pltpu.HBM(b.shape, b.dtype) for b in both] + [jax.ShapeDtypeStruct((8, 128), F32)],
        input_output_aliases={i: 2 + i for i in range(2 * n)},
        compiler_params=pltpu.CompilerParams(has_side_effects=_EFFECT),
    )(*[pltpu.with_memory_space_constraint(b, pltpu.HBM) for b in both])
    return (outs[0], outs[1]), list(outs[2:2 + n]), list(outs[2 + n:2 + 2 * n]), outs[-1]


def _exchange_wait(sems, srcs, lands, plan, after, name):
    n = len(srcs)

    def body(*refs):
        send_sems, recv_sems = refs[2 * n], refs[2 * n + 1]
        for k, (src, dst, dev) in enumerate(plan(refs[:n], refs[n:2 * n])):
            cp = pltpu.make_async_remote_copy(src_ref=src, dst_ref=dst, send_sem=send_sems.at[k], recv_sem=recv_sems.at[k],
                                              device_id=dev, device_id_type=MESH)
            cp.wait_send()
            cp.wait_recv()

    both = list(srcs) + list(lands)
    afters = list(after) if isinstance(after, (list, tuple)) else [after]
    outs = pl.pallas_call(
        body, name=name, in_specs=[_HBM] * (2 * n) + [_SEM, _SEM] + [_ANY] * len(afters), out_specs=[_HBM] * (2 * n),
        out_shape=[pltpu.HBM(b.shape, b.dtype) for b in both], input_output_aliases={i: i for i in range(2 * n)},
        compiler_params=pltpu.CompilerParams(has_side_effects=_EFFECT),
    )(*both, sems[0], sems[1], *afters)
    return list(outs[:n]), list(outs[n:])


class _Reducer:
    def __init__(self, where):
        self.state = {}
        self.where = where

    def begin(self, stage, grads):
        names = list(grads)
        full = [grads[n] for n in names]
        lands = [lax.empty((NSH, g.shape[1] // 2, g.shape[2]), g.dtype) for g in full]
        sems, full, lands, token = _exchange_start(full, lands, _swap_plan, len(full), "swap_start_" + stage)
        self.state[stage] = (names, sems, full, lands)
        return token

    def advance(self, stage, after):
        names, sems, full, lands = self.state[stage]
        full, got = _exchange_wait(sems, full, lands, _swap_plan, after, "swap_wait_" + stage)
        sums = [_chip_sum(g, a, self.where, "chip_sum_" + n) for n, g, a in zip(names, full, got)]
        lands = [lax.empty((3,) + s[0].shape[1:], BF16) for s in sums]
        sems, sent, lands, token = _exchange_start([s[0] for s in sums], lands, _owners_plan, 3 * len(sums),
                                                   "owners_start_" + stage)
        self.state[stage] = (names, [s[1] for s in sums], sems, sent, lands)
        return token

    def finish(self, stage, after):
        names, own, sems, sent, lands = self.state[stage]
        _, got = _exchange_wait(sems, sent, lands, _owners_plan, after, "owners_wait_" + stage)
        return {n: _owner_sum(o, g, "owner_sum_" + n) for n, o, g in zip(names, own, got)}


def _chip_sum(g, got, where, name):
    _, r, cc = g.shape
    rh = r // 2

    def body(where_ref, g_ref, got_ref, hb_ref, own_ref):
        h = g_ref[...].astype(F32) + got_ref[...].astype(F32)
        hb_ref[...] = h.astype(BF16)

        @pl.when(pl.program_id(0) == where_ref[1])
        def _():
            own_ref[...] = h

    return pl.pallas_call(
        body, name=name,
        grid_spec=pltpu.PrefetchScalarGridSpec(
            num_scalar_prefetch=1, grid=(NSH,),
            in_specs=[pl.BlockSpec((None, rh, cc), lambda s, where: (s, where[0], 0)),
                      pl.BlockSpec((None, rh, cc), lambda s, where: (s, 0, 0))],
            out_specs=[pl.BlockSpec((None, rh, cc), lambda s, where: (s, 0, 0)),
                       pl.BlockSpec((rh, cc), lambda s, where: (0, 0))]),
        out_shape=[jax.ShapeDtypeStruct((NSH, rh, cc), BF16), jax.ShapeDtypeStruct((rh, cc), F32)],
        compiler_params=_params("arbitrary"),
    )(where, g, got)


def _owner_sum(own, got, name):
    rh, cc = own.shape
    rows = rh // 2

    def body(own_ref, got_ref, o_ref):
        o_ref[...] = ((own_ref[...] + got_ref[0].astype(F32)) + got_ref[1].astype(F32)) + got_ref[2].astype(F32)

    return pl.pallas_call(
        body, grid=(2,), name=name,
        in_specs=[pl.BlockSpec((rows, cc), lambda i: (i, 0)), pl.BlockSpec((3, rows, cc), lambda i: (0, i, 0))],
        out_specs=pl.BlockSpec((rows, cc), lambda i: (i, 0)),
        out_shape=jax.ShapeDtypeStruct((rh, cc), F32), compiler_params=_params("arbitrary"),
    )(own, got)


def _send_halves(halves, name):
    n = len(halves)

    def body(*refs):
        ins, outs = refs[:n], refs[n:2 * n]
        send_sems, recv_sems = refs[2 * n:]
        x, y, c = _place()
        copies = [pltpu.make_async_remote_copy(src_ref=ins[w], dst_ref=outs[w], send_sem=send_sems.at[w], recv_sem=recv_sems.at[w],
                                               device_id=(x, y, 1 - c), device_id_type=MESH) for w in range(n)]
        for cp in copies:
            cp.start()
        for cp in copies:
            cp.wait()

    return pl.pallas_call(
        body, name=name, in_specs=[_ANY] * n, out_specs=[_ANY] * n,
        out_shape=[jax.ShapeDtypeStruct(h.shape, F32) for h in halves],
        scratch_shapes=[pltpu.SemaphoreType.DMA((n,)), pltpu.SemaphoreType.DMA((n,))],
        compiler_params=pltpu.CompilerParams(has_side_effects=True),
    )(*halves)


def _all_reduce_small(part):
    def body(p_ref, o_ref, rbuf, send1, recv1, send2, recv2):
        x, y, c = _place()
        me = 4 * x + 2 * y + c
        peers = []
        for k in range(1, 8):
            px, py, pc = x ^ ((k >> 2) & 1), y ^ ((k >> 1) & 1), c ^ (k & 1)
            peers.append((k, (px, py, pc), 4 * px + 2 * py + pc))

        def rows(d):
            return pl.ds(pl.multiple_of(d * SMALL_SLICE, 8), SMALL_SLICE)

        first = [pltpu.make_async_remote_copy(src_ref=p_ref.at[rows(idx), :], dst_ref=rbuf.at[me], send_sem=send1.at[k],
                                              recv_sem=recv1.at[k], device_id=dev, device_id_type=MESH)
                 for k, dev, idx in peers]
        for cp in first:
            cp.start()
        rbuf[me] = p_ref[rows(me), :]
        for k, dev, idx in peers:
            pltpu.make_async_remote_copy(src_ref=p_ref.at[rows(idx), :], dst_ref=rbuf.at[idx], send_sem=send1.at[k],
                                         recv_sem=recv1.at[k], device_id=dev, device_id_type=MESH).wait_recv()
        acc = rbuf[0]
        for d in range(1, 8):
            acc = acc + rbuf[d]
        o_ref[rows(me), :] = acc
        second = [pltpu.make_async_remote_copy(src_ref=o_ref.at[rows(me), :], dst_ref=o_ref.at[rows(me), :],
                                               send_sem=send2.at[k], recv_sem=recv2.at[k], device_id=dev, device_id_type=MESH)
                  for k, dev, idx in peers]
        for cp in second:
            cp.start()
        for k, dev, idx in peers:
            pltpu.make_async_remote_copy(src_ref=o_ref.at[rows(me), :], dst_ref=o_ref.at[rows(idx), :], send_sem=send2.at[k],
                                         recv_sem=recv2.at[k], device_id=dev, device_id_type=MESH).wait_recv()
        for cp in first + second:
            cp.wait_send()

    return pl.pallas_call(
        body, name="all_reduce_small", in_specs=[_WHOLE], out_specs=_WHOLE,
        out_shape=jax.ShapeDtypeStruct((SMALL_ROWS, 128), F32),
        scratch_shapes=[pltpu.VMEM((8, SMALL_SLICE, 128), F32)] + [pltpu.SemaphoreType.DMA((8,))] * 4,
        compiler_params=pltpu.CompilerParams(has_side_effects=True),
    )(part)


def _adamw_update(w, gv, m, v):
    nm = ADAM_B1 * m + (1.0 - ADAM_B1) * gv
    nv = ADAM_B2 * v + (1.0 - ADAM_B2) * (gv * gv)
    m_hat = nm / (1.0 - ADAM_B1 ** ADAM_STEP)
    v_hat = nv / (1.0 - ADAM_B2 ** ADAM_STEP)
    return -ADAM_LR * (m_hat / (jnp.sqrt(v_hat) + ADAM_EPS) + ADAM_WD * w), nm, nv


def _adamw_small(ws, gs, ms, vs):
    n = len(ws)

    def body(*refs):
        w_refs, g_refs, m_refs, v_refs, d_refs, nm_refs, nv_refs = (refs[k * n:(k + 1) * n] for k in range(7))
        for i in range(n):
            d_refs[i][...], nm_refs[i][...], nv_refs[i][...] = _adamw_update(
                w_refs[i][...], g_refs[i][...], m_refs[i][...], v_refs[i][...])

    out = [jax.ShapeDtypeStruct(w.shape, F32) for w in ws]
    outs = pl.pallas_call(body, in_specs=[_WHOLE] * (4 * n), out_specs=[_WHOLE] * (3 * n), out_shape=out * 3,
                          name="adamw_small", compiler_params=_params())(*ws, *gs, *ms, *vs)
    return outs[:n], outs[n:2 * n], outs[2 * n:]


def _adamw_halves(w, mine, theirs, m, v, name):
    rh, cc = mine.shape
    steps = 1
    rows = rh // steps

    def body(w_ref, mine_ref, theirs_ref, m_ref, v_ref, g_ref, d_ref, nm_ref, nv_ref):
        gv = jnp.where(pl.program_id(0) == lax.axis_index("c"), mine_ref[...], theirs_ref[...])
        g_ref[...] = gv
        d_ref[...], nm_ref[...], nv_ref[...] = _adamw_update(w_ref[...], gv, m_ref[...], v_ref[...])

    spec = pl.BlockSpec((rows, cc), lambda h, i: (steps * h + i, 0))
    half = pl.BlockSpec((rows, cc), lambda h, i: (i, 0))
    out = jax.ShapeDtypeStruct(w.shape, F32)
    return pl.pallas_call(body, grid=(2, steps), in_specs=[spec, half, half, spec, spec], out_specs=[spec] * 4,
                          out_shape=[out] * 4, name=name, compiler_params=_params("arbitrary", "arbitrary"))(w, mine, theirs, m, v)


SMALL_USED = sum(size for _, size in SMALL) // 128


def _pack_small(vals, tail=None):
    parts = []
    for name, size in SMALL:
        flat = vals[name].reshape(-1).astype(F32)
        parts.append(jnp.pad(flat, (0, size - flat.shape[0])))
    if tail is not None:
        parts.append(tail.reshape(128))
    flat = jnp.concatenate(parts)
    return jnp.pad(flat, (0, SMALL_ROWS * 128 - flat.shape[0])).reshape(SMALL_ROWS, 128)


def _unpack_small(packed, shapes):
    flat = packed.reshape(-1)
    out, off = {}, 0
    for name, size in SMALL:
        n = math.prod(shapes[name])
        out[name] = flat[off:off + n].reshape(shapes[name])
        off += size
    return out


def kernel(x, ffn1_pre_g, ffn1_w1, ffn1_w3, ffn1_w2, ffn1_post_g, mix_pre_g, w_in, conv_w, conv_b, rg_a_w, rg_a_b, rg_x_w, rg_x_b, lru_lambda, w_lru_out, attn_sinks, rel_bias, w_attn_out, w_gate, b_gate, w_o, mix_post_g, ffn2_pre_g, ffn2_w1, ffn2_w3, ffn2_w2, ffn2_post_g, loss_target, m_ffn1_pre_g, m_ffn1_w1, m_ffn1_w3, m_ffn1_w2, m_ffn1_post_g, m_mix_pre_g, m_w_in, m_conv_w, m_conv_b, m_rg_a_w, m_rg_a_b, m_rg_x_w, m_rg_x_b, m_lru_lambda, m_w_lru_out, m_attn_sinks, m_rel_bias, m_w_attn_out, m_w_gate, m_b_gate, m_w_o, m_mix_post_g, m_ffn2_pre_g, m_ffn2_w1, m_ffn2_w3, m_ffn2_w2, m_ffn2_post_g, v_ffn1_pre_g, v_ffn1_w1, v_ffn1_w3, v_ffn1_w2, v_ffn1_post_g, v_mix_pre_g, v_w_in, v_conv_w, v_conv_b, v_rg_a_w, v_rg_a_b, v_rg_x_w, v_rg_x_b, v_lru_lambda, v_w_lru_out, v_attn_sinks, v_rel_bias, v_w_attn_out, v_w_gate, v_b_gate, v_w_o, v_mix_post_g, v_ffn2_pre_g, v_ffn2_w1, v_ffn2_w3, v_ffn2_w2, v_ffn2_post_g):
    given = dict(locals())
    chip = 2 * lax.axis_index("x") + lax.axis_index("y")
    transposed = ("ffn1_w1", "ffn1_w3", "ffn2_w1", "ffn2_w3")

    def shard(name, moment=""):
        w = given[moment + name][0]
        return w.T if name in transposed else w

    def unshard(name, w):
        return (w.T if name in transposed else w)[None]

    def only_my_columns(a):
        parts = a.reshape(1, 4, NSH, D // NSH)
        return sum(jnp.where(chip == s, parts[:, :, s], 0.0) for s in range(NSH))

    chip_arr = jnp.reshape(chip, (1,)).astype(jnp.int32)
    stage_names = {"ffn1": ["ffn1_w1", "ffn1_w3", "ffn1_w2", "conv_w"],
                   "mix": ["w_in", "w_gate", "w_lru_out", "w_attn_out", "w_o"],
                   "ffn2": ["ffn2_w1", "ffn2_w3", "ffn2_w2"]}
    in_flight, started = {}, None
    for stage, names in stage_names.items():
        bufs = [jnp.where(lax.broadcasted_iota(jnp.int32, (NSH, 4, D // NSH), 0) == chip, given[n], 0.0) if n == "conv_w"
                else _cast_into_slot(shard(n), chip_arr, "cast_" + n, started) for n in names]
        (in_flight[stage],), started = _gather_start([bufs], "gather_start_" + stage)
    all_started = started

    def weights(stage, after):
        names = stage_names[stage]
        send_sems, recv_sems, landing = in_flight[stage]
        if stage == "ffn1":
            after = all_started
        landed = _gather_wait(send_sems, recv_sems, landing, after, "gather_wait_" + stage)
        halves = [b for b in landed if b.dtype == BF16]
        out = dict(zip([n for n, b in zip(names, landed) if b.dtype == BF16], _sibling_fill(halves, "sibling_fill_" + stage)))
        if "conv_w" in names:
            out["conv_w"] = jnp.transpose(landed[names.index("conv_w")], (1, 0, 2)).reshape(4, D)
        return out

    small_shapes = {n: given[n].shape for n, _ in SMALL}
    small_shapes["conv_w"] = (1, 4, D)
    sm = {n: (given[n][0] if given[n].shape[0] == 1 and n != "rel_bias" else given[n]) for n, _ in SMALL if n != "conv_w"}

    reducer = _Reducer(jnp.stack([lax.axis_index("c"), chip]).astype(jnp.int32))
    sq, dx, _, small = _local_step(x[0], loss_target[0], weights, sm, reducer)

    reducer.advance("ffn1", dx)
    reduced_small = _all_reduce_small(_pack_small(small, tail=sq))
    loss = reduced_small[SMALL_USED, 0] * (0.5 / D)
    small_g = _unpack_small(reduced_small, small_shapes)
    grads, delta, new_m, new_v = {}, {}, {}, {}
    after = [reduced_small]
    for stage in ("ffn2", "mix", "ffn1"):
        halves = reducer.finish(stage, after)
        from_sibling = _send_halves(list(halves.values()), "send_halves_" + stage)
        for (n, mine), theirs in zip(halves.items(), from_sibling):
            grads[n], delta[n], new_m[n], new_v[n] = (unshard(n, r) for r in _adamw_halves(
                shard(n), mine, theirs, shard(n, "m_"), shard(n, "v_"), "adamw_" + n))
            after.append(new_v[n])

    small_g["conv_w"] = only_my_columns(small_g["conv_w"])
    names = [n for n, _ in SMALL]
    flat2d = lambda a: a.reshape(-1, a.shape[-1])
    outs = _adamw_small(*[[flat2d(given[pre + n]) if pre != "g" else flat2d(small_g[n]) for n in names]
                          for pre in ("", "g", "m_", "v_")])
    for dst, arrs in zip((delta, new_m, new_v), outs):
        dst.update({n: a.reshape(given[n].shape) for n, a in zip(names, arrs)})
    grads.update(small_g)
    return (loss, dx[None], *[grads[n] for n in WEIGHTS], *[delta[n] for n in WEIGHTS], *[new_m[n] for n in WEIGHTS],
            *[new_v[n] for n in WEIGHTS])
```

```python
import functools
import math

import jax
import jax.numpy as jnp
from jax import lax
from jax.experimental import pallas as pl
from jax.experimental.pallas import tpu as pltpu

F32, BF16 = jnp.float32, jnp.bfloat16
D = 1024
NSH = 4
FF_S = 704
IN_S = 896
GATE_S = 512
KV_W = 256
CHUNK = 64
KB = 192
N_HEADS = 16
HEAD_DIM = 64
N_BUCKETS = 32
KP = 192
PAD_KEYS = 128
RMS_EPS = 1e-6
NEG_INF = -1e30
LRU_C = 8.0
TM = 512
TM_SCAN = 256
VMEM_LIMIT = 56 * 1024 * 1024
ADAM_LR, ADAM_B1, ADAM_B2, ADAM_EPS, ADAM_WD, ADAM_STEP = 0.001, 0.9, 0.999, 1e-08, 0.01, 10
SMALL_ROWS = 1216
SMALL_SLICE = SMALL_ROWS // 8
MESH = pl.DeviceIdType.MESH

BIG = ["ffn1_w1", "ffn1_w3", "ffn1_w2", "w_in", "w_lru_out", "w_attn_out", "w_gate", "w_o", "ffn2_w1", "ffn2_w3", "ffn2_w2"]
SMALL = [("ffn1_pre_g", 1024), ("ffn1_post_g", 1024), ("mix_pre_g", 1024), ("conv_w", 4096), ("conv_b", 1024),
         ("rg_a_w", 65536), ("rg_a_b", 1024), ("rg_x_w", 65536), ("rg_x_b", 1024), ("lru_lambda", 1024),
         ("attn_sinks", 1024), ("rel_bias", 1024), ("b_gate", 2048), ("mix_post_g", 1024), ("ffn2_pre_g", 1024),
         ("ffn2_post_g", 1024)]
WEIGHTS = ["ffn1_pre_g", "ffn1_w1", "ffn1_w3", "ffn1_w2", "ffn1_post_g", "mix_pre_g", "w_in", "conv_w", "conv_b", "rg_a_w",
           "rg_a_b", "rg_x_w", "rg_x_b", "lru_lambda", "w_lru_out", "attn_sinks", "rel_bias", "w_attn_out", "w_gate", "b_gate",
           "w_o", "mix_post_g", "ffn2_pre_g", "ffn2_w1", "ffn2_w3", "ffn2_w2", "ffn2_post_g"]


def _params(*sem):
    return pltpu.CompilerParams(dimension_semantics=sem or None, vmem_limit_bytes=VMEM_LIMIT)


def _nn(a, b):
    return jnp.dot(a, b, preferred_element_type=F32)


def _nt(a, b):
    return lax.dot_general(a, b, (((1,), (1,)), ((), ())), preferred_element_type=F32)


def _tn(a, b):
    return lax.dot_general(a, b, (((0,), (0,)), ((), ())), preferred_element_type=F32)


def _rms(x, g):
    rstd = lax.rsqrt(jnp.mean(x * x, axis=-1, keepdims=True) + RMS_EPS)
    return (x * rstd) * g


def _rms_bwd(dout, x, g):
    rstd = lax.rsqrt(jnp.mean(x * x, axis=-1, keepdims=True) + RMS_EPS)
    xhat = x * rstd
    dg = jnp.sum(dout * xhat, axis=0, keepdims=True)
    dxhat = dout * g
    dx = rstd * (dxhat - xhat * jnp.mean(dxhat * xhat, axis=-1, keepdims=True))
    return dx, dg


_GELU_K = math.sqrt(2.0 / math.pi)


def _gelu(x):
    return x * (0.5 * (1.0 + jnp.tanh(_GELU_K * (x + 0.044715 * (x * x * x)))))


def _gelu_and_grad(x):
    x2 = x * x
    t = jnp.tanh(_GELU_K * (x + 0.044715 * (x2 * x)))
    cdf = 0.5 * (1.0 + t)
    return x * cdf, cdf + x * (0.5 * (1.0 - t * t) * (_GELU_K * (1.0 + 3.0 * 0.044715 * x2)))


def _softplus_neg(lam):
    z = -lam
    u = jnp.exp(-jnp.abs(z))
    w = 1.0 + u
    log1p_u = jnp.where(w == 1.0, u, jnp.log(w) * (u / (w - 1.0)))
    return jnp.maximum(z, 0.0) + log1p_u


def _lru_coeffs(r, sp):
    log_a = (-LRU_C * r) * sp
    a = jnp.exp(log_a)
    t = jnp.tanh(log_a)
    s = jnp.sqrt(-2.0 * t / (1.0 - t))
    return a, s


def _row_spec(tm, width):
    return pl.BlockSpec((tm, width), lambda i: (i, 0))


def _vec_spec(width):
    return pl.BlockSpec((1, width), lambda i: (0, 0))


_WHOLE = pl.BlockSpec(memory_space=pltpu.VMEM)


def _tile(t, tm=TM):
    return min(tm, t)


def _ffn_fwd(x, gpre, w1g, w3g, w2g, gpost, name, target=None):
    t = x.shape[0]
    tm = _tile(t)
    last = target is not None

    def body(x_ref, gpre_ref, w1_ref, w3_ref, w2_ref, gpost_ref, *refs):
        t_ref, (h_ref, a_ref, b_ref, hm_ref, f_ref), l_ref = (refs[0] if last else None), refs[last:last + 5], refs[-1]
        xv = x_ref[...]
        nb = _rms(xv, gpre_ref[...]).astype(BF16)
        f = jnp.zeros((tm, D), F32)
        for s in range(NSH):
            a = _nt(nb, w1_ref[s])
            b = _nt(nb, w3_ref[s])
            sg = jax.nn.sigmoid(a)
            silu = a * sg
            hmb = (silu * b).astype(BF16)
            a_ref[s] = (b * (sg * (1.0 + a * (1.0 - sg)))).astype(BF16)
            b_ref[s] = silu.astype(BF16)
            hm_ref[s] = hmb
            f = f + _nn(hmb, w2_ref[s])
        f_ref[...] = f
        h = xv + 0.5 * _rms(f, gpost_ref[...])
        if last:
            @pl.when(pl.program_id(0) == 0)
            def _():
                l_ref[...] = jnp.zeros_like(l_ref)

            e = h - t_ref[...]
            h_ref[...] = e * (1.0 / D)
            l_ref[...] += jnp.sum(jnp.sum(e * e, axis=0, keepdims=True), axis=1, keepdims=True)
        else:
            h_ref[...] = h

    sh = pl.BlockSpec((NSH, tm, FF_S), lambda i: (0, i, 0))
    act = jax.ShapeDtypeStruct((NSH, t, FF_S), BF16)
    return pl.pallas_call(
        body, grid=(t // tm,), name=name,
        in_specs=[_row_spec(tm, D), _vec_spec(D), _WHOLE, _WHOLE, _WHOLE, _vec_spec(D)] + [_row_spec(tm, D)] * last,
        out_specs=[_row_spec(tm, D), sh, sh, sh, _row_spec(tm, D)] + [pl.BlockSpec((1, 128), lambda i: (0, 0))] * last,
        out_shape=[jax.ShapeDtypeStruct((t, D), F32), act, act, act, jax.ShapeDtypeStruct((t, D), F32)]
        + [jax.ShapeDtypeStruct((1, 128), F32)] * last,
        compiler_params=_params("arbitrary"),
    )(x, gpre, w1g, w3g, w2g, gpost, *([target] if last else []))


def _mix_proj(h1, gmix, w_in_g, w_gate_g, b_gate):
    t = h1.shape[0]
    tm = _tile(t)

    def body(h_ref, g_ref, win_ref, wg_ref, bg_ref, u_ref, q_ref, k_ref, v_ref, xr_ref, xg_ref, gate_ref):
        ub = _rms(h_ref[...], g_ref[...]).astype(BF16)
        u_ref[...] = ub
        p0 = _nn(ub, win_ref[0])
        q_ref[:, 0:896] = p0.astype(BF16)
        p1 = _nn(ub, win_ref[1])
        q_ref[:, 896:1024] = p1[:, 0:128].astype(BF16)
        k_ref[...] = p1[:, 128:384].astype(BF16)
        v_ref[...] = p1[:, 384:640].astype(BF16)
        xr_ref[:, 0:256] = p1[:, 640:896]
        p2 = _nn(ub, win_ref[2])
        xr_ref[:, 256:1024] = p2[:, 0:768]
        xg_ref[:, 0:128] = p2[:, 768:896]
        xg_ref[:, 128:1024] = _nn(ub, win_ref[3])
        for s in range(NSH):
            sl = slice(s * GATE_S, (s + 1) * GATE_S)
            gate_ref[:, sl] = jax.nn.sigmoid(_nn(ub, wg_ref[s]) + bg_ref[:, sl])

    return pl.pallas_call(
        body, grid=(t // tm,), name="mix_proj",
        in_specs=[_row_spec(tm, D), _vec_spec(D), _WHOLE, _WHOLE, _vec_spec(2 * D)],
        out_specs=[_row_spec(tm, D), _row_spec(tm, D), _row_spec(tm, KV_W), _row_spec(tm, KV_W), _row_spec(tm, D),
                   _row_spec(tm, D), _row_spec(tm, 2 * D)],
        out_shape=[jax.ShapeDtypeStruct((t, D), BF16), jax.ShapeDtypeStruct((t, D), BF16),
                   jax.ShapeDtypeStruct((t, KV_W), BF16), jax.ShapeDtypeStruct((t, KV_W), BF16),
                   jax.ShapeDtypeStruct((t, D), F32), jax.ShapeDtypeStruct((t, D), F32),
                   jax.ShapeDtypeStruct((t, 2 * D), F32)],
        compiler_params=_params("arbitrary"),
    )(h1, gmix, w_in_g, w_gate_g, b_gate)


def _rglru_fwd(xr, xg, conv_w, conv_b, wa2, ba, wx2, bx, lam):
    t = xr.shape[0]
    tm = _tile(t, TM_SCAN)
    nb8 = tm // 8

    def body(xr_ref, xrp_ref, xg_ref, cw_ref, cb_ref, wa_ref, ba_ref, wx_ref, bx_ref, lam_ref,
             hr_ref, yain_ref, xc_ref, r_ref, ig_ref, a_sc, s_ref, ext, h_sc):
        i = pl.program_id(0)

        @pl.when(i == 0)
        def _():
            h_sc[...] = jnp.zeros_like(h_sc)

        ext[0:8, :] = jnp.where(i == 0, 0.0, xrp_ref[...])
        ext[8:8 + tm, :] = xr_ref[...]
        xc = jnp.broadcast_to(cb_ref[...], (tm, D))
        for tap in range(4):
            xc = xc + ext[pl.ds(5 + tap, tm), :] * cw_ref[tap:tap + 1, :]
        xc_ref[...] = xc
        xcb = xc.astype(BF16)
        for p in range(8):
            sl = slice(p * 128, (p + 1) * 128)
            r_ref[:, sl] = jax.nn.sigmoid(_nn(xcb[:, sl], wa_ref[p]) + ba_ref[:, sl])
            ig_ref[:, sl] = jax.nn.sigmoid(_nn(xcb[:, sl], wx_ref[p]) + bx_ref[:, sl])
        a, s = _lru_coeffs(r_ref[...], _softplus_neg(lam_ref[...]))
        a_sc[...] = a
        s_ref[...] = s
        hr_ref[...] = s * (ig_ref[...] * xc)

        def blk(j, h):
            st = pl.multiple_of(j * 8, 8)
            a8 = a_sc[pl.ds(st, 8), :]
            u8 = hr_ref[pl.ds(st, 8), :]
            rows = []
            for k in range(8):
                h = a8[k:k + 1, :] * h + u8[k:k + 1, :]
                rows.append(h)
            hr_ref[pl.ds(st, 8), :] = jnp.concatenate(rows, axis=0)
            return h

        h_sc[0:1, :] = lax.fori_loop(0, nb8, blk, h_sc[0:1, :])
        yain_ref[...] = (hr_ref[...] * _gelu(xg_ref[...])).astype(BF16)

    prev = pl.BlockSpec((8, D), lambda i: (jnp.maximum(i * nb8 - 1, 0), 0))
    full = lambda shape: pl.BlockSpec(shape, lambda i: tuple(0 for _ in shape))
    f32 = jax.ShapeDtypeStruct((t, D), F32)
    return pl.pallas_call(
        body, grid=(t // tm,), name="rglru_fwd",
        in_specs=[_row_spec(tm, D), prev, _row_spec(tm, D), full((4, D)), _vec_spec(D), full((8, 128, 128)), _vec_spec(D),
                  full((8, 128, 128)), _vec_spec(D), _vec_spec(D)],
        out_specs=[_row_spec(tm, D)] * 7,
        out_shape=[f32, jax.ShapeDtypeStruct((t, D), BF16), f32, f32, f32, f32, f32],
        scratch_shapes=[pltpu.VMEM((tm + 8, D), F32), pltpu.VMEM((8, D), F32)],
        compiler_params=_params("arbitrary"),
    )(xr, xr, xg, conv_w, conv_b, wa2, ba, wx2, bx, lam)


def _bias_fwd(table_t, onehot_t):
    def body(t_ref, e_ref, o_ref):
        o_ref[...] = jnp.dot(t_ref[...], e_ref[...], preferred_element_type=F32, precision=lax.Precision.HIGHEST)

    return pl.pallas_call(body, out_shape=jax.ShapeDtypeStruct((N_HEADS, CHUNK * KB), F32), name="bias_fwd",
                          compiler_params=_params())(table_t, onehot_t)


def _bias_bwd(dbias_flat, onehot_t, ds_rows):
    def body(d_ref, e_ref, s_ref, o_ref, so_ref):
        o_ref[...] = lax.dot_general(d_ref[...], e_ref[...], (((1,), (1,)), ((), ())), preferred_element_type=F32,
                                     precision=lax.Precision.HIGHEST)
        so_ref[...] = jnp.zeros_like(so_ref)
        for r in range(4):
            so_ref[:, r:r + 1] = jnp.sum(s_ref[:, r * CHUNK:(r + 1) * CHUNK], axis=1, keepdims=True)

    return pl.pallas_call(body, out_shape=[jax.ShapeDtypeStruct((N_HEADS, N_BUCKETS), F32), jax.ShapeDtypeStruct((8, 128), F32)],
                          name="bias_bwd", compiler_params=_params())(dbias_flat, onehot_t, ds_rows)


def _stack_heads(q):
    return jnp.concatenate(
        [jnp.concatenate([q[:, (4 * g + r) * HEAD_DIM:(4 * g + r + 1) * HEAD_DIM] for g in range(4)], axis=1)
         for r in range(4)], axis=0)


def _unstack_heads(o):
    return jnp.concatenate([o[r * CHUNK:(r + 1) * CHUNK, g * HEAD_DIM:(g + 1) * HEAD_DIM] for g in range(4) for r in range(4)],
                           axis=1)


def _block_diag(w, mask):
    return jnp.concatenate([w] * 4, axis=0) * mask


def _group_softmax(qk, bias_g, sink, valid):
    s = qk * (HEAD_DIM ** -0.5) + bias_g
    s = jnp.where(valid, s, NEG_INF)
    m = jnp.maximum(jnp.max(s, axis=0, keepdims=True), sink)
    e = jnp.exp(s - m)
    es = jnp.exp(sink - m)
    inv = 1.0 / (jnp.sum(e, axis=0, keepdims=True) + es)
    return e * inv, es * inv


def _attn_fwd(sink_rows, q, kp, vp, bias_t, mask):
    t = q.shape[0]
    per_step = 4

    def body(sink_ref, q_ref, kp_ref, vp_ref, bias_ref, mask_ref, o_ref):
        owns = [mask_ref[g * KP:(g + 1) * KP, :] for g in range(4)]
        for k in range(per_step):
            c = pl.program_id(0) * per_step + k
            rows = slice(k * CHUNK, (k + 1) * CHUNK)
            st = pl.multiple_of(c * CHUNK, CHUNK)
            kw = kp_ref[pl.ds(st, KP), :]
            vw = vp_ref[pl.ds(st, KP), :]
            q_all = _stack_heads(q_ref[rows, :])
            valid = lax.broadcasted_iota(jnp.int32, (KP, 1), 0) + c * CHUNK >= PAD_KEYS
            scores = [_nt(kw * owns[g], q_all) for g in range(4)]
            ps = [_group_softmax(scores[g], bias_ref[g * KP:(g + 1) * KP, :], sink_ref[g:g + 1, :], valid)[0]
                  for g in range(4)]
            o_all = sum(_tn(ps[g].astype(BF16), vw * owns[g]) for g in range(4))
            o_ref[rows, :] = _unstack_heads(o_all).astype(BF16)

    return pl.pallas_call(
        body, grid=(t // (per_step * CHUNK),), name="attn_fwd",
        in_specs=[_WHOLE, _row_spec(per_step * CHUNK, D), _WHOLE, _WHOLE, _WHOLE, _WHOLE],
        out_specs=_row_spec(per_step * CHUNK, D),
        out_shape=jax.ShapeDtypeStruct((t, D), BF16),
        compiler_params=_params("arbitrary"),
    )(sink_rows, q, kp, vp, bias_t, mask)


def _merge_fwd(yain, o, gate, h1, w_lru, w_att, w_o, gpost):
    t = h1.shape[0]
    tm = _tile(t)

    def body(ya_ref, o_ref, g_ref, h_ref, wl_ref, wa_ref, wo_ref, gp_ref, h2_ref, mo_ref, mg_ref, ya_out, yb_out):
        ya = _nn(ya_ref[...], wl_ref[...])
        yb = _nn(o_ref[...], wa_ref[...])
        g0 = g_ref[:, 0:D]
        g1 = g_ref[:, D:2 * D]
        mg = (g0 * ya + g1 * yb).astype(BF16)
        mo = _nn(mg, wo_ref[...])
        ya_out[...] = (ya * (g0 * (1.0 - g0))).astype(BF16)
        yb_out[...] = (yb * (g1 * (1.0 - g1))).astype(BF16)
        mg_ref[...] = mg
        mo_ref[...] = mo
        h2_ref[...] = h_ref[...] + _rms(mo, gp_ref[...])

    f32 = jax.ShapeDtypeStruct((t, D), F32)
    b16 = jax.ShapeDtypeStruct((t, D), BF16)
    return pl.pallas_call(
        body, grid=(t // tm,), name="merge_fwd",
        in_specs=[_row_spec(tm, D), _row_spec(tm, D), _row_spec(tm, 2 * D), _row_spec(tm, D), _WHOLE, _WHOLE, _WHOLE,
                  _vec_spec(D)],
        out_specs=[_row_spec(tm, D)] * 5,
        out_shape=[f32, f32, b16, b16, b16],
        compiler_params=_params("arbitrary"),
    )(yain, o, gate, h1, w_lru, w_att, w_o, gpost)


def _ffn_bwd(dh, x, f, a, b, gpre, gpost, w1g, w3g, w2g, name):
    t = x.shape[0]
    tm = _tile(t)

    def body(dh_ref, x_ref, f_ref, a_ref, b_ref, gpre_ref, gpost_ref, w1_ref, w3_ref, w2_ref,
             dx_ref, n_ref, da_ref, db_ref, df_ref, dgpre_ref, dgpost_ref):
        @pl.when(pl.program_id(0) == 0)
        def _():
            dgpre_ref[...] = jnp.zeros_like(dgpre_ref)
            dgpost_ref[...] = jnp.zeros_like(dgpost_ref)

        dhv = dh_ref[...]
        xv = x_ref[...]
        df, dgp = _rms_bwd(0.5 * dhv, f_ref[...], gpost_ref[...])
        dgpost_ref[...] += dgp
        dfb = df.astype(BF16)
        df_ref[...] = dfb
        n_ref[...] = _rms(xv, gpre_ref[...]).astype(BF16)
        dn = jnp.zeros((tm, D), F32)
        for s in range(NSH):
            dhm = _nt(dfb, w2_ref[s])
            dab = (dhm * a_ref[s].astype(F32)).astype(BF16)
            dbb = (dhm * b_ref[s].astype(F32)).astype(BF16)
            da_ref[s] = dab
            db_ref[s] = dbb
            dn = dn + _nn(dab, w1_ref[s]) + _nn(dbb, w3_ref[s])
        dxn, dg = _rms_bwd(dn, xv, gpre_ref[...])
        dgpre_ref[...] += dg
        dx_ref[...] = dhv + dxn

    sh = pl.BlockSpec((NSH, tm, FF_S), lambda i: (0, i, 0), pipeline_mode=pl.Buffered(1))
    act = jax.ShapeDtypeStruct((NSH, t, FF_S), BF16)
    vec = jax.ShapeDtypeStruct((1, D), F32)
    return pl.pallas_call(
        body, grid=(t // tm,), name=name,
        in_specs=[_row_spec(tm, D), _row_spec(tm, D), _row_spec(tm, D), sh, sh, _vec_spec(D), _vec_spec(D), _WHOLE, _WHOLE,
                  _WHOLE],
        out_specs=[_row_spec(tm, D), _row_spec(tm, D), sh, sh, _row_spec(tm, D), _vec_spec(D), _vec_spec(D)],
        out_shape=[jax.ShapeDtypeStruct((t, D), F32), jax.ShapeDtypeStruct((t, D), BF16), act, act,
                   jax.ShapeDtypeStruct((t, D), BF16), vec, vec],
        compiler_params=_params("arbitrary"),
    )(dh, x, f, a, b, gpre, gpost, w1g, w3g, w2g)


def _behind(body, after):
    if after is None:
        return body, [], []

    def ordered(_, *refs):
        body(*refs)

    return ordered, [_ANY], [after]


def _ffn_bwd_acts(dh, x, f, a, b, gpre, gpost, w2g, name):
    t = x.shape[0]
    tm = _tile(t)

    def body(dh_ref, x_ref, f_ref, a_ref, b_ref, gpre_ref, gpost_ref, w2_ref, n_ref, da_ref, db_ref, df_ref, dgpost_ref):
        @pl.when(pl.program_id(0) == 0)
        def _():
            dgpost_ref[...] = jnp.zeros_like(dgpost_ref)

        df, dgp = _rms_bwd(0.5 * dh_ref[...], f_ref[...], gpost_ref[...])
        dgpost_ref[...] += dgp
        dfb = df.astype(BF16)
        df_ref[...] = dfb
        n_ref[...] = _rms(x_ref[...], gpre_ref[...]).astype(BF16)
        for s in range(NSH):
            dhm = _nt(dfb, w2_ref[s])
            da_ref[s] = (dhm * a_ref[s].astype(F32)).astype(BF16)
            db_ref[s] = (dhm * b_ref[s].astype(F32)).astype(BF16)

    sh = pl.BlockSpec((NSH, tm, FF_S), lambda i: (0, i, 0))
    act = jax.ShapeDtypeStruct((NSH, t, FF_S), BF16)
    b16 = jax.ShapeDtypeStruct((t, D), BF16)
    return pl.pallas_call(
        body, grid=(t // tm,), name=name,
        in_specs=[_row_spec(tm, D), _row_spec(tm, D), _row_spec(tm, D), sh, sh, _vec_spec(D), _vec_spec(D), _WHOLE],
        out_specs=[_row_spec(tm, D), sh, sh, _row_spec(tm, D), _vec_spec(D)],
        out_shape=[b16, act, act, b16, jax.ShapeDtypeStruct((1, D), F32)],
        compiler_params=_params("arbitrary"),
    )(dh, x, f, a, b, gpre, gpost, w2g)


def _ffn_bwd_input(dh, x, da, db, gpre, w1g, w3g, name, after):
    t = x.shape[0]
    tm = _tile(t)

    def body(dh_ref, x_ref, da_ref, db_ref, gpre_ref, w1_ref, w3_ref, dx_ref, dgpre_ref):
        @pl.when(pl.program_id(0) == 0)
        def _():
            dgpre_ref[...] = jnp.zeros_like(dgpre_ref)

        dn = jnp.zeros((tm, D), F32)
        for s in range(NSH):
            dn = dn + _nn(da_ref[s], w1_ref[s]) + _nn(db_ref[s], w3_ref[s])
        dxn, dg = _rms_bwd(dn, x_ref[...], gpre_ref[...])
        dgpre_ref[...] += dg
        dx_ref[...] = dh_ref[...] + dxn

    sh = pl.BlockSpec((NSH, tm, FF_S), lambda i: (0, i, 0))
    body, specs, operands = _behind(body, after)
    return pl.pallas_call(
        body, grid=(t // tm,), name=name,
        in_specs=specs + [_row_spec(tm, D), _row_spec(tm, D), sh, sh, _vec_spec(D), _WHOLE, _WHOLE],
        out_specs=[_row_spec(tm, D), _vec_spec(D)],
        out_shape=[jax.ShapeDtypeStruct((t, D), F32), jax.ShapeDtypeStruct((1, D), F32)],
        compiler_params=_params("arbitrary"),
    )(*operands, dh, x, da, db, gpre, w1g, w3g)


def _wgrad(a, b, a_spec, b_spec, out_spec, out_shape, grid, name, after=None):
    def body(a_ref, b_ref, o_ref):
        o_ref[...] = _tn(a_ref[...], b_ref[...]).astype(BF16)

    body, specs, operands = _behind(body, after)
    return pl.pallas_call(body, grid=grid, name=name, in_specs=specs + [a_spec, b_spec], out_specs=out_spec,
                          out_shape=jax.ShapeDtypeStruct(out_shape, BF16),
                          compiler_params=_params(*("arbitrary",) * len(grid)))(*operands, a, b)


def _wgrad_cols(act, dsh, width, name, after=None):
    t = act.shape[0]
    if dsh.ndim == 3:
        b_spec = pl.BlockSpec((None, t, width), lambda s, k: (s, 0, 0))
    else:
        b_spec = pl.BlockSpec((t, width), lambda s, k: (0, s))
    return _wgrad(act, dsh, pl.BlockSpec((t, 512), lambda s, k: (0, k)), b_spec,
                  pl.BlockSpec((None, 512, width), lambda s, k: (s, k, 0)), (NSH, D, width), (NSH, 2), name, after)


def _wgrad_rows(hm, df, name, after=None):
    t = df.shape[0]
    return _wgrad(hm, df, pl.BlockSpec((None, t, FF_S), lambda s: (s, 0, 0)), pl.BlockSpec((t, D), lambda s: (0, 0)),
                  pl.BlockSpec((None, FF_S, D), lambda s: (s, 0, 0)), (NSH, FF_S, D), (NSH,), name, after)


def _wgrad_sq(a, b, name, after=None):
    t = a.shape[0]
    return _wgrad(a, b, pl.BlockSpec((t, 512), lambda i, j: (0, i)), pl.BlockSpec((t, 512), lambda i, j: (0, j)),
                  pl.BlockSpec((512, 512), lambda i, j: (i, j)), (D, D), (2, 2), name, after)


def _mix_bwd1(dh2, mo, gpost, gate, ya, yb, xg, hr, w_o, w_lru, w_att, after):
    t = dh2.shape[0]
    tm = _tile(t, TM_SCAN)

    def body(dh_ref, mo_ref, gp_ref, g_ref, ya_ref, yb_ref, xg_ref, hr_ref, wo_ref, wl_ref, wa_ref,
             dmo_ref, dya_ref, dyb_ref, dgate_ref, dhr_ref, dxg_ref, do_ref, dgp_ref, dbg_ref):
        @pl.when(pl.program_id(0) == 0)
        def _():
            dgp_ref[...] = jnp.zeros_like(dgp_ref)
            dbg_ref[...] = jnp.zeros_like(dbg_ref)

        dmo, dgp = _rms_bwd(dh_ref[...], mo_ref[...], gp_ref[...])
        dgp_ref[...] += dgp
        dmob = dmo.astype(BF16)
        dmo_ref[...] = dmob
        dm = _nt(dmob, wo_ref[...])
        g0 = g_ref[:, 0:D]
        g1 = g_ref[:, D:2 * D]
        dyab = (dm * g0).astype(BF16)
        dybb = (dm * g1).astype(BF16)
        dya_ref[...] = dyab
        dyb_ref[...] = dybb
        dg0 = dm * ya_ref[...].astype(F32)
        dg1 = dm * yb_ref[...].astype(F32)
        dgate_ref[:, 0:D] = dg0.astype(BF16)
        dgate_ref[:, D:2 * D] = dg1.astype(BF16)
        dbg_ref[:, 0:D] += jnp.sum(dg0, axis=0, keepdims=True)
        dbg_ref[:, D:2 * D] += jnp.sum(dg1, axis=0, keepdims=True)
        dyain = _nt(dyab, wl_ref[...])
        do_ref[...] = _nt(dybb, wa_ref[...]).astype(BF16)
        xgv = xg_ref[...]
        gelu, gelu_grad = _gelu_and_grad(xgv)
        dhr_ref[...] = dyain * gelu
        dxg_ref[...] = (dyain * hr_ref[...] * gelu_grad).astype(BF16)

    b16 = jax.ShapeDtypeStruct((t, D), BF16)
    body, specs, operands = _behind(body, after)
    return pl.pallas_call(
        body, grid=(t // tm,), name="mix_bwd1",
        in_specs=specs + [_row_spec(tm, D), _row_spec(tm, D), _vec_spec(D), _row_spec(tm, 2 * D), _row_spec(tm, D),
                          _row_spec(tm, D), _row_spec(tm, D), _row_spec(tm, D), _WHOLE, _WHOLE, _WHOLE],
        out_specs=[_row_spec(tm, D), _row_spec(tm, D), _row_spec(tm, D), _row_spec(tm, 2 * D), _row_spec(tm, D),
                   _row_spec(tm, D), _row_spec(tm, D), _vec_spec(D), _vec_spec(2 * D)],
        out_shape=[b16, b16, b16, jax.ShapeDtypeStruct((t, 2 * D), BF16), jax.ShapeDtypeStruct((t, D), F32), b16, b16,
                   jax.ShapeDtypeStruct((1, D), F32), jax.ShapeDtypeStruct((1, 2 * D), F32)],
        compiler_params=_params("arbitrary"),
    )(*operands, dh2, mo, gpost, gate, ya, yb, xg, hr, w_o, w_lru, w_att)


def _rglru_bwd(dhr, hr, xc, r, ig, a, s, xr, conv_w, wa2, wx2, lam, after):
    t = dhr.shape[0]
    tm = _tile(t, TM_SCAN)
    nb8 = tm // 8
    nt = t // tm

    def body(dhr_ref, hr_ref, hrp_ref, xc_ref, r_ref, ig_ref, a_sc, s_ref, xr_ref, xrp_ref, cw_ref, wa_ref, wx_ref, lam_ref,
             dxr_ref, dwa_ref, dwx_ref, dba_ref, dbx_ref, dlam_ref, dcw_ref, dcb_ref,
             ext_h, ext_x, ext_d, g_sc, c_sc, nxt_sc):
        i = pl.program_id(0)
        first_tile = i == nt - 1

        @pl.when(i == 0)
        def _():
            c_sc[...] = jnp.zeros_like(c_sc)
            nxt_sc[...] = jnp.zeros_like(nxt_sc)
            for ref in (dwa_ref, dwx_ref, dba_ref, dbx_ref, dlam_ref, dcw_ref, dcb_ref):
                ref[...] = jnp.zeros_like(ref)

        lamv = lam_ref[...]
        sp = _softplus_neg(lamv)
        rv = r_ref[...]
        igv = ig_ref[...]
        xcv = xc_ref[...]
        a = a_sc[...]
        s = s_ref[...]

        def blk(jj, c):
            st = pl.multiple_of((nb8 - 1 - jj) * 8, 8)
            d8 = dhr_ref[pl.ds(st, 8), :]
            a8 = a_sc[pl.ds(st, 8), :]
            rows = [None] * 8
            for k in range(7, -1, -1):
                g = d8[k:k + 1, :] + c
                c = a8[k:k + 1, :] * g
                rows[k] = g
            g_sc[pl.ds(st, 8), :] = jnp.concatenate(rows, axis=0)
            return c

        c_sc[0:1, :] = lax.fori_loop(0, nb8, blk, c_sc[0:1, :])
        g = g_sc[...]
        ext_h[0:8, :] = jnp.where(first_tile, 0.0, hrp_ref[...])
        ext_h[8:8 + tm, :] = hr_ref[...]
        hprev = ext_h[pl.ds(7, tm), :]
        d_s = g * (igv * xcv)
        dig = g * s * xcv
        dxc = g * s * igv
        dla = (g * hprev) * a - d_s * ((a * a) / s)
        dr_pre = (dla * (-LRU_C * sp)) * (rv * (1.0 - rv))
        di_pre = dig * (igv * (1.0 - igv))
        dlam_ref[...] += jnp.sum(dla * (LRU_C * rv), axis=0, keepdims=True) * jax.nn.sigmoid(-lamv)
        dba_ref[...] += jnp.sum(dr_pre, axis=0, keepdims=True)
        dbx_ref[...] += jnp.sum(di_pre, axis=0, keepdims=True)
        drb = dr_pre.astype(BF16)
        dib = di_pre.astype(BF16)
        xcb = xcv.astype(BF16)
        ext_d[tm:tm + 8, :] = nxt_sc[...]
        for p in range(8):
            sl = slice(p * 128, (p + 1) * 128)
            ext_d[0:tm, sl] = dxc[:, sl] + _nt(drb[:, sl], wa_ref[p]) + _nt(dib[:, sl], wx_ref[p])
            dwa_ref[p] += _tn(xcb[:, sl], drb[:, sl])
            dwx_ref[p] += _tn(xcb[:, sl], dib[:, sl])
        dxcv = ext_d[0:tm, :]
        nxt_sc[...] = ext_d[0:8, :]
        dcb_ref[...] += jnp.sum(dxcv, axis=0, keepdims=True)
        ext_x[0:8, :] = jnp.where(first_tile, 0.0, xrp_ref[...])
        ext_x[8:8 + tm, :] = xr_ref[...]
        dxr = jnp.zeros((tm, D), F32)
        for tap in range(4):
            dxr = dxr + ext_d[pl.ds(3 - tap, tm), :] * cw_ref[tap:tap + 1, :]
            dcw_ref[tap:tap + 1, :] += jnp.sum(dxcv * ext_x[pl.ds(5 + tap, tm), :], axis=0, keepdims=True)
        dxr_ref[...] = dxr.astype(BF16)

    rev = pl.BlockSpec((tm, D), lambda i: (nt - 1 - i, 0))
    prev = pl.BlockSpec((8, D), lambda i: (jnp.maximum((nt - 1 - i) * nb8 - 1, 0), 0))
    full = lambda shape: pl.BlockSpec(shape, lambda i: tuple(0 for _ in shape))
    vec = jax.ShapeDtypeStruct((1, D), F32)
    blocks = jax.ShapeDtypeStruct((8, 128, 128), F32)
    body, specs, operands = _behind(body, after)
    return pl.pallas_call(
        body, grid=(nt,), name="rglru_bwd",
        in_specs=specs + [rev, rev, prev, rev, rev, rev, rev, rev, rev, prev, full((4, D)), full((8, 128, 128)),
                          full((8, 128, 128)), _vec_spec(D)],
        out_specs=[rev, full((8, 128, 128)), full((8, 128, 128)), _vec_spec(D), _vec_spec(D), _vec_spec(D), full((4, D)),
                   _vec_spec(D)],
        out_shape=[jax.ShapeDtypeStruct((t, D), BF16), blocks, blocks, vec, vec, vec, jax.ShapeDtypeStruct((4, D), F32), vec],
        scratch_shapes=[pltpu.VMEM((tm + 8, D), F32), pltpu.VMEM((tm + 8, D), F32), pltpu.VMEM((tm + 8, D), F32),
                        pltpu.VMEM((tm, D), F32), pltpu.VMEM((8, D), F32), pltpu.VMEM((8, D), F32)],
        compiler_params=_params("arbitrary"),
    )(*operands, dhr, hr, hr, xc, r, ig, a, s, xr, xr, conv_w, wa2, wx2, lam)


def _attn_bwd(sink_rows, q, kp, vp, bias_t, mask, do):
    t = q.shape[0]
    tp = kp.shape[0]
    per_step = 4

    def body(sink_ref, q_ref, kp_ref, vp_ref, bias_ref, mask_ref, do_ref, dq_ref, dk_ref, dv_ref, dbias_ref, ds_ref):
        @pl.when(pl.program_id(0) == 0)
        def _():
            for ref in (dk_ref, dv_ref, dbias_ref, ds_ref):
                ref[...] = jnp.zeros_like(ref)

        maskv = mask_ref[...]
        lane_group = lax.broadcasted_iota(jnp.int32, (1, 4 * HEAD_DIM), 1) // HEAD_DIM

        def own_blocks(full):
            out = full[0:KP]
            for g in range(1, 4):
                out = jnp.where(lane_group == g, full[g * KP:(g + 1) * KP], out)
            return out

        dsc_sum, dsinks, dks, dvs = 0.0, [0.0] * 4, [], []
        for k in range(per_step):
            c = pl.program_id(0) * per_step + k
            chunk = slice(k * CHUNK, (k + 1) * CHUNK)
            st = pl.multiple_of(c * CHUNK, CHUNK)
            kbd = _block_diag(kp_ref[pl.ds(st, KP), :], maskv)
            vbd = _block_diag(vp_ref[pl.ds(st, KP), :], maskv)
            q_all = _stack_heads(q_ref[chunk, :])
            do_all = _stack_heads(do_ref[chunk, :])
            valid = lax.broadcasted_iota(jnp.int32, (KP, 1), 0) + c * CHUNK >= PAD_KEYS
            qk = _nt(kbd, q_all)
            dp = _nt(vbd, do_all)
            ps, dscs = [], []
            for g in range(4):
                rows = slice(g * KP, (g + 1) * KP)
                p, sink_p = _group_softmax(qk[rows], bias_ref[rows, :], sink_ref[g:g + 1, :], valid)
                delta = jnp.sum(p * dp[rows], axis=0, keepdims=True)
                ps.append(p)
                dscs.append(p * (dp[rows] - delta))
                dsinks[g] = dsinks[g] - sink_p * delta
            dsc = jnp.concatenate(dscs, axis=0)
            dsc_sum = dsc_sum + dsc
            dsb = (dsc * (HEAD_DIM ** -0.5)).astype(BF16)
            dq_ref[chunk, :] = _unstack_heads(_tn(dsb, kbd)).astype(BF16)
            dks.append((st, own_blocks(_nn(dsb, q_all))))
            dvs.append((st, own_blocks(_nn(jnp.concatenate(ps, axis=0).astype(BF16), do_all))))
        dbias_ref[...] += dsc_sum
        for g in range(4):
            ds_ref[g:g + 1, :] += dsinks[g]
        for (st, dkw), (_, dvw) in zip(dks, dvs):
            dk_ref[pl.ds(st, KP), :] += dkw
            dv_ref[pl.ds(st, KP), :] += dvw

    full = lambda shape: pl.BlockSpec(shape, lambda i: tuple(0 for _ in shape))
    return pl.pallas_call(
        body, grid=(t // (per_step * CHUNK),), name="attn_bwd",
        in_specs=[_WHOLE, _row_spec(per_step * CHUNK, D), _WHOLE, _WHOLE, _WHOLE, _WHOLE, _row_spec(per_step * CHUNK, D)],
        out_specs=[_row_spec(per_step * CHUNK, D), full((tp, KV_W)), full((tp, KV_W)), full((4 * KP, 4 * CHUNK)),
                   full((8, 4 * CHUNK))],
        out_shape=[jax.ShapeDtypeStruct((t, D), BF16), jax.ShapeDtypeStruct((tp, KV_W), F32),
                   jax.ShapeDtypeStruct((tp, KV_W), F32), jax.ShapeDtypeStruct((4 * KP, 4 * CHUNK), F32),
                   jax.ShapeDtypeStruct((8, 4 * CHUNK), F32)],
        compiler_params=_params("arbitrary"),
    )(sink_rows, q, kp, vp, bias_t, mask, do)


def _mix_bwd2(dproj, dgate, h1, dh2, gmix, w_in_g, w_gate_g, after):
    t = h1.shape[0]
    tm = _tile(t)

    def body(dp_ref, dg_ref, h_ref, dh_ref, g_ref, win_ref, wg_ref, dh1_ref, dgm_ref):
        @pl.when(pl.program_id(0) == 0)
        def _():
            dgm_ref[...] = jnp.zeros_like(dgm_ref)

        du = jnp.zeros((tm, D), F32)
        for s in range(NSH):
            du = du + _nt(dp_ref[:, s * IN_S:(s + 1) * IN_S], win_ref[s])
            du = du + _nt(dg_ref[:, s * GATE_S:(s + 1) * GATE_S], wg_ref[s])
        dxn, dg = _rms_bwd(du, h_ref[...], g_ref[...])
        dgm_ref[...] += dg
        dh1_ref[...] = dh_ref[...] + dxn

    body, specs, operands = _behind(body, after)
    return pl.pallas_call(
        body, grid=(t // tm,), name="mix_bwd2",
        in_specs=specs + [_row_spec(tm, NSH * IN_S), _row_spec(tm, 2 * D), _row_spec(tm, D), _row_spec(tm, D), _vec_spec(D),
                          _WHOLE, _WHOLE],
        out_specs=[_row_spec(tm, D), _vec_spec(D)],
        out_shape=[jax.ShapeDtypeStruct((t, D), F32), jax.ShapeDtypeStruct((1, D), F32)],
        compiler_params=_params("arbitrary"),
    )(*operands, dproj, dgate, h1, dh2, gmix, w_in_g, w_gate_g)


def _band_onehot():
    nb = N_BUCKETS // 2
    max_exact = nb // 2
    rel = jnp.arange(KB)[None, :] - PAD_KEYS - jnp.arange(CHUNK)[:, None]
    ret = jnp.where(rel > 0, nb, 0)
    n = jnp.abs(rel)
    nf = jnp.maximum(n, 1).astype(jnp.float32)
    large = max_exact + (jnp.log(nf / max_exact) / math.log(128 / max_exact) * (nb - max_exact)).astype(jnp.int32)
    large = jnp.minimum(large, nb - 1)
    buckets = (ret + jnp.where(n < max_exact, n, large)).reshape(1, CHUNK * KB)
    return (buckets == jnp.arange(N_BUCKETS)[:, None]).astype(F32)


def _pair_blocks(w):
    pairs = w.reshape(8, 2, 64, 64)
    z = jnp.zeros((8, 64, 64), w.dtype)
    return jnp.concatenate([jnp.concatenate([pairs[:, 0], z], axis=2), jnp.concatenate([z, pairs[:, 1]], axis=2)], axis=1)


def _unpair_blocks(w2):
    return jnp.stack([w2[:, 0:64, 0:64], w2[:, 64:128, 64:128]], axis=1).reshape(16, 64, 64)


def _local_step(x, target, weights, sm, reducer):
    row = lambda v: v.reshape(1, -1)
    wg = dict(weights("ffn1", x))
    sm = dict(sm, conv_w=wg["conv_w"])
    onehot_t = _band_onehot()
    bias = _bias_fwd(sm["rel_bias"].T, onehot_t).reshape(4, 4, CHUNK, KB)
    bias_t = jnp.pad(jnp.transpose(bias, (0, 3, 1, 2)), ((0, 0), (0, KP - KB), (0, 0), (0, 0))).reshape(4 * KP, 4 * CHUNK)
    sink_rows = jnp.pad(jnp.repeat(sm["attn_sinks"].reshape(4, 4), CHUNK, axis=1), ((0, 4), (0, 0)))
    grp = jnp.arange(4 * KP)[:, None] // KP == jnp.arange(4 * HEAD_DIM)[None, :] // HEAD_DIM
    mask = (grp & (jnp.arange(4 * KP)[:, None] % KP < KB)).astype(BF16)
    wa2 = _pair_blocks(sm["rg_a_w"]).astype(BF16)
    wx2 = _pair_blocks(sm["rg_x_w"]).astype(BF16)

    h1, a1, b1, hm1, f1 = _ffn_fwd(x, row(sm["ffn1_pre_g"]), wg["ffn1_w1"], wg["ffn1_w3"], wg["ffn1_w2"],
                                   row(sm["ffn1_post_g"]), "ffn1_fwd")
    wg.update(weights("mix", h1))
    w_lru = wg["w_lru_out"].reshape(D, D)
    w_att = wg["w_attn_out"].reshape(D, D)
    w_o = wg["w_o"].reshape(D, D)
    u, q, k, v, xr, xg, gate = _mix_proj(h1, row(sm["mix_pre_g"]), wg["w_in"], wg["w_gate"], row(sm["b_gate"]))
    hr, yain, xc, r, ig, lru_a, lru_s = _rglru_fwd(xr, xg, sm["conv_w"], row(sm["conv_b"]), wa2, row(sm["rg_a_b"]), wx2,
                                     row(sm["rg_x_b"]), row(sm["lru_lambda"]))
    kp = jnp.pad(k, ((PAD_KEYS, KP - KB), (0, 0)))
    vp = jnp.pad(v, ((PAD_KEYS, KP - KB), (0, 0)))
    o = _attn_fwd(sink_rows, q, kp, vp, bias_t, mask)
    wg.update(weights("ffn2", o))
    h2, mo, merged, ya, yb = _merge_fwd(yain, o, gate, h1, w_lru, w_att, w_o, row(sm["mix_post_g"]))
    dy, a2, b2, hm2, f2, sq = _ffn_fwd(h2, row(sm["ffn2_pre_g"]), wg["ffn2_w1"], wg["ffn2_w3"], wg["ffn2_w2"],
                                       row(sm["ffn2_post_g"]), "ffn2_fwd", target)

    big, small = {}, {}
    dh2, n2, da2, db2, df2, small["ffn2_pre_g"], small["ffn2_post_g"] = _ffn_bwd(
        dy, h2, f2, a2, b2, row(sm["ffn2_pre_g"]), row(sm["ffn2_post_g"]), wg["ffn2_w1"], wg["ffn2_w3"], wg["ffn2_w2"],
        "ffn2_bwd")
    big["ffn2_w1"] = _wgrad_rows(da2, n2, "dw_ffn2_w1")
    big["ffn2_w3"] = _wgrad_rows(db2, n2, "dw_ffn2_w3")
    big["ffn2_w2"] = _wgrad_rows(hm2, df2, "dw_ffn2_w2")
    token = reducer.begin("ffn2", {n: big[n] for n in ("ffn2_w1", "ffn2_w3", "ffn2_w2")})
    dmo, dya, dyb, dgate, dhr, dxg, do, small["mix_post_g"], small["b_gate"] = _mix_bwd1(
        dh2, mo, row(sm["mix_post_g"]), gate, ya, yb, xg, hr, w_o, w_lru, w_att, token)
    big["w_o"] = _wgrad_sq(merged, dmo, "dw_w_o").reshape(NSH, D // NSH, D)
    big["w_lru_out"] = _wgrad_sq(yain, dya, "dw_w_lru_out").reshape(NSH, D // NSH, D)
    big["w_attn_out"] = _wgrad_sq(o, dyb, "dw_w_attn_out").reshape(NSH, D // NSH, D)
    token = reducer.advance("ffn2", big["w_attn_out"])
    (dxr, dwa2, dwx2, small["rg_a_b"], small["rg_x_b"], small["lru_lambda"], small["conv_w"], small["conv_b"]) = _rglru_bwd(
        dhr, hr, xc, r, ig, lru_a, lru_s, xr, sm["conv_w"], wa2, wx2, row(sm["lru_lambda"]), token)
    small["rg_a_w"] = _unpair_blocks(dwa2)
    small["rg_x_w"] = _unpair_blocks(dwx2)
    dq, dkp, dvp, dbias_t, ds_rows = _attn_bwd(sink_rows, q, kp, vp, bias_t, mask, do)
    dbias = jnp.transpose(dbias_t.reshape(4, KP, 4, CHUNK)[:, :KB], (0, 2, 3, 1)).reshape(N_HEADS, CHUNK * KB)
    drel_t, dsinks = _bias_bwd(dbias, onehot_t, ds_rows)
    small["attn_sinks"] = dsinks[0:4, 0:4].reshape(N_HEADS)
    small["rel_bias"] = drel_t.T
    t = x.shape[0]
    dproj = jnp.concatenate([dq, dkp[PAD_KEYS:PAD_KEYS + t].astype(BF16), dvp[PAD_KEYS:PAD_KEYS + t].astype(BF16), dxr, dxg],
                            axis=1)
    big["w_in"] = _wgrad_cols(u, dproj, IN_S, "dw_w_in")
    big["w_gate"] = _wgrad_cols(u, dgate, GATE_S, "dw_w_gate")
    token = reducer.begin("mix", {n: big[n] for n in ("w_in", "w_gate", "w_lru_out", "w_attn_out", "w_o")})
    dh1, small["mix_pre_g"] = _mix_bwd2(dproj, dgate, h1, dh2, row(sm["mix_pre_g"]), wg["w_in"], wg["w_gate"], token)
    n1, da1, db1, df1, small["ffn1_post_g"] = _ffn_bwd_acts(
        dh1, x, f1, a1, b1, row(sm["ffn1_pre_g"]), row(sm["ffn1_post_g"]), wg["ffn1_w2"], "ffn1_bwd_acts")
    token = reducer.advance("mix", df1)
    big["ffn1_w1"] = _wgrad_rows(da1, n1, "dw_ffn1_w1", token)
    big["ffn1_w3"] = _wgrad_rows(db1, n1, "dw_ffn1_w3", token)
    big["ffn1_w2"] = _wgrad_rows(hm1, df1, "dw_ffn1_w2", token)
    token = reducer.begin("ffn1", {n: big[n] for n in ("ffn1_w1", "ffn1_w3", "ffn1_w2")})
    dx, small["ffn1_pre_g"] = _ffn_bwd_input(dh1, x, da1, db1, row(sm["ffn1_pre_g"]), wg["ffn1_w1"], wg["ffn1_w3"],
                                             "ffn1_bwd_input", token)
    return sq, dx, big, small


_ANY = pl.BlockSpec(memory_space=pl.ANY)


def _place():
    return lax.axis_index("x"), lax.axis_index("y"), lax.axis_index("c")


def _other_chips(x, y):
    return [(1 - x, y), (x, 1 - y), (1 - x, 1 - y)]


_HBM = pl.BlockSpec(memory_space=pltpu.HBM)
_SEM = pl.BlockSpec(memory_space=pltpu.SEMAPHORE)
_EFFECT = pltpu.SideEffectType.DATAFLOW_SIDE_EFFECTING


def _cast_into_slot(w, chip, name, after=None):
    r, cc = w.shape
    rows = r // 4

    def body(chip_ref, *refs):
        w_ref, o_ref = refs[-2:]
        o_ref[...] = w_ref[...].astype(BF16)

    extra = [] if after is None else [after]
    return pl.pallas_call(
        body, name=name, out_shape=jax.ShapeDtypeStruct((NSH, r, cc), BF16),
        grid_spec=pltpu.PrefetchScalarGridSpec(
            num_scalar_prefetch=1, grid=(4,), in_specs=[_ANY] * len(extra) + [pl.BlockSpec((rows, cc), lambda i, chip: (i, 0))],
            out_specs=pl.BlockSpec((None, rows, cc), lambda i, chip: (chip[0], i, 0))),
        compiler_params=_params("arbitrary"))(chip, *extra, w)


def _piece(ref, slot, c):
    if ref.dtype == F32:
        return ref.at[slot]
    rh = ref.shape[1] // 2
    return ref.at[slot, pl.ds(pl.multiple_of(c * rh, 16), rh), :]


def _gather_start(stages, name):
    flat = [b for stage in stages for b in stage]
    n, ns = len(flat), len(stages)

    def body(*refs):
        ins, sems, token = refs[:n], refs[n:n + 2 * ns], refs[-1]
        x, y, c = _place()
        me = 2 * x + y
        k = 0
        for s, stage in enumerate(stages):
            for i in range(len(stage)):
                for j, (px, py) in enumerate(_other_chips(x, y)):
                    piece = _piece(ins[k], me, c)
                    pltpu.make_async_remote_copy(src_ref=piece, dst_ref=piece, send_sem=sems[2 * s].at[3 * i + j],
                                                 recv_sem=sems[2 * s + 1].at[3 * i + j], device_id=(px, py, c),
                                                 device_id_type=MESH).start()
                k += 1
        token[...] = jnp.zeros_like(token)

    sem_shapes = [pltpu.SemaphoreType.DMA((3 * len(stage),)) for stage in stages for _ in range(2)]
    outs = pl.pallas_call(
        body, name=name, in_specs=[_HBM] * n,
        out_specs=[_SEM] * (2 * ns) + [_HBM] * n + [pl.BlockSpec(memory_space=pltpu.VMEM)],
        out_shape=sem_shapes + [pltpu.HBM(b.shape, b.dtype) for b in flat] + [jax.ShapeDtypeStruct((8, 128), F32)],
        input_output_aliases={i: 2 * ns + i for i in range(n)},
        compiler_params=pltpu.CompilerParams(has_side_effects=_EFFECT),
    )(*[pltpu.with_memory_space_constraint(b, pltpu.HBM) for b in flat])
    sems, bufs, token = outs[:2 * ns], list(outs[2 * ns:2 * ns + n]), outs[-1]
    per_stage, k = [], 0
    for s, stage in enumerate(stages):
        per_stage.append((sems[2 * s], sems[2 * s + 1], bufs[k:k + len(stage)]))
        k += len(stage)
    return per_stage, token


def _gather_wait(send_sems, recv_sems, bufs, after, name):
    n = len(bufs)

    def body(*refs):
        ins, ssem, rsem = refs[:n], refs[n], refs[n + 1]
        x, y, c = _place()
        me = 2 * x + y
        for i in range(n):
            for j, (px, py) in enumerate(_other_chips(x, y)):
                cp = pltpu.make_async_remote_copy(src_ref=_piece(ins[i], me, c), dst_ref=_piece(ins[i], 2 * px + py, c),
                                                  send_sem=ssem.at[3 * i + j], recv_sem=rsem.at[3 * i + j],
                                                  device_id=(px, py, c), device_id_type=MESH)
                cp.wait_send()
                cp.wait_recv()

    return pl.pallas_call(
        body, name=name, in_specs=[_HBM] * n + [_SEM, _SEM, _ANY], out_specs=[_HBM] * n,
        out_shape=[pltpu.HBM(b.shape, b.dtype) for b in bufs], input_output_aliases={i: i for i in range(n)},
        compiler_params=pltpu.CompilerParams(has_side_effects=_EFFECT),
    )(*bufs, send_sems, recv_sems, after)


def _sibling_fill(bufs, name):
    n = len(bufs)

    def body(*refs):
        ins, outs = refs[:n], refs[n:2 * n]
        send_sems, recv_sems = refs[2 * n:]
        x, y, c = _place()
        copies = []
        for i in range(n):
            for j, (px, py) in enumerate(_other_chips(x, y)):
                copies.append(pltpu.make_async_remote_copy(
                    src_ref=_piece(ins[i], 2 * px + py, c), dst_ref=_piece(outs[i], 2 * px + py, c),
                    send_sem=send_sems.at[3 * i + j], recv_sem=recv_sems.at[3 * i + j], device_id=(x, y, 1 - c),
                    device_id_type=MESH))
                copies[-1].start()
        for cp in copies:
            cp.wait()

    return pl.pallas_call(
        body, name=name, in_specs=[_ANY] * n, out_specs=[_ANY] * n,
        out_shape=[jax.ShapeDtypeStruct(b.shape, b.dtype) for b in bufs], input_output_aliases={i: i for i in range(n)},
        scratch_shapes=[pltpu.SemaphoreType.DMA((3 * n,)), pltpu.SemaphoreType.DMA((3 * n,))],
        compiler_params=pltpu.CompilerParams(has_side_effects=True),
    )(*bufs)


def _swap_plan(srcs, lands):
    x, y, c = _place()
    plan = []
    for src, land in zip(srcs, lands):
        rh = src.shape[1] // 2
        plan.append((src.at[:, pl.ds(pl.multiple_of((1 - c) * rh, 16), rh), :], land, (x, y, 1 - c)))
    return plan


def _owners_plan(srcs, lands):
    x, y, c = _place()
    return [(src.at[2 * px + py], land.at[j], (px, py, c))
            for src, land in zip(srcs, lands) for j, (px, py) in enumerate(_other_chips(x, y))]


def _exchange_start(srcs, lands, plan, copies, name):
    n = len(srcs)

    def body(*refs):
        send_sems, recv_sems, token = refs[2 * n], refs[2 * n + 1], refs[-1]
        for k, (src, dst, dev) in enumerate(plan(refs[:n], refs[n:2 * n])):
            pltpu.make_async_remote_copy(src_ref=src, dst_ref=dst, send_sem=send_sems.at[k], recv_sem=recv_sems.at[k],
                                         device_id=dev, device_id_type=MESH).start()
        token[...] = jnp.zeros_like(token)

    both = list(srcs) + list(lands)
    outs = pl.pallas_call(
        body, name=name, in_specs=[_HBM] * (2 * n),
        out_specs=[_SEM, _SEM] + [_HBM] * (2 * n) + [pl.BlockSpec(memory_space=pltpu.VMEM)],
        out_shape=[pltpu.SemaphoreType.DMA((copies,)), pltpu.SemaphoreType.DMA((copies,))]
        + [pltpu.HBM(b.shape, b.dtype) for b in both] + [jax.ShapeDtypeStruct((8, 128), F32)],
        input_output_aliases={i: 2 + i for i in range(2 * n)},
        compiler_params=pltpu.CompilerParams(has_side_effects=_EFFECT),
    )(*[pltpu.with_memory_space_constraint(b, pltpu.HBM) for b in both])
    return (outs[0], outs[1]), list(outs[2:2 + n]), list(outs[2 + n:2 + 2 * n]), outs[-1]


def _exchange_wait(sems, srcs, lands, plan, after, name):
    n = len(srcs)

    def body(*refs):
        send_sems, recv_sems = refs[2 * n], refs[2 * n + 1]
        for k, (src, dst, dev) in enumerate(plan(refs[:n], refs[n:2 * n])):
            cp = pltpu.make_async_remote_copy(src_ref=src, dst_ref=dst, send_sem=send_sems.at[k], recv_sem=recv_sems.at[k],
                                              device_id=dev, device_id_type=MESH)
            cp.wait_send()
            cp.wait_recv()

    both = list(srcs) + list(lands)
    afters = list(after) if isinstance(after, (list, tuple)) else [after]
    outs = pl.pallas_call(
        body, name=name, in_specs=[_HBM] * (2 * n) + [_SEM, _SEM] + [_ANY] * len(afters), out_specs=[_HBM] * (2 * n),
        out_shape=[pltpu.HBM(b.shape, b.dtype) for b in both], input_output_aliases={i: i for i in range(2 * n)},
        compiler_params=pltpu.CompilerParams(has_side_effects=_EFFECT),
    )(*both, sems[0], sems[1], *afters)
    return list(outs[:n]), list(outs[n:])


class _Reducer:
    def __init__(self, where):
        self.state = {}
        self.where = where

    def begin(self, stage, grads):
        names = list(grads)
        full = [grads[n] for n in names]
        lands = [lax.empty((NSH, g.shape[1] // 2, g.shape[2]), g.dtype) for g in full]
        sems, full, lands, token = _exchange_start(full, lands, _swap_plan, len(full), "swap_start_" + stage)
        self.state[stage] = (names, sems, full, lands)
        return token

    def advance(self, stage, after):
        names, sems, full, lands = self.state[stage]
        full, got = _exchange_wait(sems, full, lands, _swap_plan, after, "swap_wait_" + stage)
        sums, own = _chip_sums(full, got, self.where, "chip_sums_" + stage)
        lands = [lax.empty((3,) + s.shape[1:], BF16) for s in sums]
        sems, sent, lands, token = _exchange_start(sums, lands, _owners_plan, 3 * len(sums), "owners_start_" + stage)
        self.state[stage] = (names, own, sems, sent, lands)
        return token

    def finish(self, stage, after):
        names, own, sems, sent, lands = self.state[stage]
        _, got = _exchange_wait(sems, sent, lands, _owners_plan, after, "owners_wait_" + stage)
        return dict(zip(names, _owner_sums(own, got, "owner_sums_" + stage)))


def _chip_sums(gs, gots, where, name):
    n = len(gs)

    def body(where_ref, *refs):
        g_refs, got_refs, hb_refs, own_refs = (refs[k * n:(k + 1) * n] for k in range(4))
        mine = pl.program_id(0) == where_ref[1]
        for g_ref, got_ref, hb_ref, own_ref in zip(g_refs, got_refs, hb_refs, own_refs):
            h = g_ref[...].astype(F32) + got_ref[...].astype(F32)
            hb_ref[...] = h.astype(BF16)

            @pl.when(mine)
            def _():
                own_ref[...] = h

    halves = [(g.shape[1] // 2, g.shape[2]) for g in gs]
    slot = [pl.BlockSpec((None, rh, cc), lambda s, where: (s, 0, 0)) for rh, cc in halves]
    outs = pl.pallas_call(
        body, name=name,
        grid_spec=pltpu.PrefetchScalarGridSpec(
            num_scalar_prefetch=1, grid=(NSH,),
            in_specs=[pl.BlockSpec((None, rh, cc), lambda s, where: (s, where[0], 0)) for rh, cc in halves] + slot,
            out_specs=slot + [pl.BlockSpec((rh, cc), lambda s, where: (0, 0)) for rh, cc in halves]),
        out_shape=[jax.ShapeDtypeStruct((NSH, rh, cc), BF16) for rh, cc in halves]
        + [jax.ShapeDtypeStruct((rh, cc), F32) for rh, cc in halves],
        compiler_params=_params("arbitrary"),
    )(where, *gs, *gots)
    return list(outs[:n]), list(outs[n:])


def _owner_sums(owns, gots, name):
    n = len(owns)

    def body(*refs):
        own_refs, got_refs, o_refs = (refs[k * n:(k + 1) * n] for k in range(3))
        for own_ref, got_ref, o_ref in zip(own_refs, got_refs, o_refs):
            o_ref[...] = ((own_ref[...] + got_ref[0].astype(F32)) + got_ref[1].astype(F32)) + got_ref[2].astype(F32)

    blocks = [(o.shape[0] // 2, o.shape[1]) for o in owns]
    rows = [pl.BlockSpec(b, lambda i: (i, 0)) for b in blocks]
    return pl.pallas_call(
        body, grid=(2,), name=name,
        in_specs=rows + [pl.BlockSpec((3,) + b, lambda i: (0, i, 0)) for b in blocks], out_specs=rows,
        out_shape=[jax.ShapeDtypeStruct(o.shape, F32) for o in owns], compiler_params=_params("arbitrary"),
    )(*owns, *gots)


def _send_halves(halves, name):
    n = len(halves)

    def body(*refs):
        ins, outs = refs[:n], refs[n:2 * n]
        send_sems, recv_sems = refs[2 * n:]
        x, y, c = _place()
        copies = [pltpu.make_async_remote_copy(src_ref=ins[w], dst_ref=outs[w], send_sem=send_sems.at[w], recv_sem=recv_sems.at[w],
                                               device_id=(x, y, 1 - c), device_id_type=MESH) for w in range(n)]
        for cp in copies:
            cp.start()
        for cp in copies:
            cp.wait()

    return pl.pallas_call(
        body, name=name, in_specs=[_ANY] * n, out_specs=[_ANY] * n,
        out_shape=[jax.ShapeDtypeStruct(h.shape, F32) for h in halves],
        scratch_shapes=[pltpu.SemaphoreType.DMA((n,)), pltpu.SemaphoreType.DMA((n,))],
        compiler_params=pltpu.CompilerParams(has_side_effects=True),
    )(*halves)


def _all_reduce_small(part):
    def body(p_ref, o_ref, rbuf, send1, recv1, send2, recv2):
        x, y, c = _place()
        me = 4 * x + 2 * y + c
        peers = []
        for k in range(1, 8):
            px, py, pc = x ^ ((k >> 2) & 1), y ^ ((k >> 1) & 1), c ^ (k & 1)
            peers.append((k, (px, py, pc), 4 * px + 2 * py + pc))

        def rows(d):
            return pl.ds(pl.multiple_of(d * SMALL_SLICE, 8), SMALL_SLICE)

        first = [pltpu.make_async_remote_copy(src_ref=p_ref.at[rows(idx), :], dst_ref=rbuf.at[me], send_sem=send1.at[k],
                                              recv_sem=recv1.at[k], device_id=dev, device_id_type=MESH)
                 for k, dev, idx in peers]
        for cp in first:
            cp.start()
        rbuf[me] = p_ref[rows(me), :]
        for k, dev, idx in peers:
            pltpu.make_async_remote_copy(src_ref=p_ref.at[rows(idx), :], dst_ref=rbuf.at[idx], send_sem=send1.at[k],
                                         recv_sem=recv1.at[k], device_id=dev, device_id_type=MESH).wait_recv()
        acc = rbuf[0]
        for d in range(1, 8):
            acc = acc + rbuf[d]
        o_ref[rows(me), :] = acc
        second = [pltpu.make_async_remote_copy(src_ref=o_ref.at[rows(me), :], dst_ref=o_ref.at[rows(me), :],
                                               send_sem=send2.at[k], recv_sem=recv2.at[k], device_id=dev, device_id_type=MESH)
                  for k, dev, idx in peers]
        for cp in second:
            cp.start()
        for k, dev, idx in peers:
            pltpu.make_async_remote_copy(src_ref=o_ref.at[rows(me), :], dst_ref=o_ref.at[rows(idx), :], send_sem=send2.at[k],
                                         recv_sem=recv2.at[k], device_id=dev, device_id_type=MESH).wait_recv()
        for cp in first + second:
            cp.wait_send()

    return pl.pallas_call(
        body, name="all_reduce_small", in_specs=[_WHOLE], out_specs=_WHOLE,
        out_shape=jax.ShapeDtypeStruct((SMALL_ROWS, 128), F32),
        scratch_shapes=[pltpu.VMEM((8, SMALL_SLICE, 128), F32)] + [pltpu.SemaphoreType.DMA((8,))] * 4,
        compiler_params=pltpu.CompilerParams(has_side_effects=True),
    )(part)


def _adamw_update(w, gv, m, v):
    nm = ADAM_B1 * m + (1.0 - ADAM_B1) * gv
    nv = ADAM_B2 * v + (1.0 - ADAM_B2) * (gv * gv)
    m_hat = nm / (1.0 - ADAM_B1 ** ADAM_STEP)
    v_hat = nv / (1.0 - ADAM_B2 ** ADAM_STEP)
    return -ADAM_LR * (m_hat / (jnp.sqrt(v_hat) + ADAM_EPS) + ADAM_WD * w), nm, nv


def _adamw_small(ws, gs, ms, vs):
    n = len(ws)

    def body(*refs):
        w_refs, g_refs, m_refs, v_refs, d_refs, nm_refs, nv_refs = (refs[k * n:(k + 1) * n] for k in range(7))
        for i in range(n):
            d_refs[i][...], nm_refs[i][...], nv_refs[i][...] = _adamw_update(
                w_refs[i][...], g_refs[i][...], m_refs[i][...], v_refs[i][...])

    out = [jax.ShapeDtypeStruct(w.shape, F32) for w in ws]
    outs = pl.pallas_call(body, in_specs=[_WHOLE] * (4 * n), out_specs=[_WHOLE] * (3 * n), out_shape=out * 3,
                          name="adamw_small", compiler_params=_params())(*ws, *gs, *ms, *vs)
    return outs[:n], outs[n:2 * n], outs[2 * n:]


def _adamw_halves(ws, mines, theirs, ms, vs, name):
    n = len(ws)
    steps = 2

    def body(*refs):
        w_refs, mine_refs, theirs_refs, m_refs, v_refs, g_refs, d_refs, nm_refs, nv_refs = (
            refs[k * n:(k + 1) * n] for k in range(9))
        is_mine = pl.program_id(0) == lax.axis_index("c")
        for i in range(n):
            gv = jnp.where(is_mine, mine_refs[i][...], theirs_refs[i][...])
            g_refs[i][...] = gv
            d_refs[i][...], nm_refs[i][...], nv_refs[i][...] = _adamw_update(w_refs[i][...], gv, m_refs[i][...], v_refs[i][...])

    blocks = [(h.shape[0] // steps, h.shape[1]) for h in mines]
    whole = [pl.BlockSpec(b, lambda h, i: (steps * h + i, 0)) for b in blocks]
    half = [pl.BlockSpec(b, lambda h, i: (i, 0)) for b in blocks]
    out = [jax.ShapeDtypeStruct(w.shape, F32) for w in ws]
    outs = pl.pallas_call(body, grid=(2, steps), in_specs=whole + half + half + whole + whole, out_specs=whole * 4,
                          out_shape=out * 4, name=name, compiler_params=_params("arbitrary", "arbitrary"),
                          )(*ws, *mines, *theirs, *ms, *vs)
    return [tuple(outs[k * n + i] for k in range(4)) for i in range(n)]


SMALL_USED = sum(size for _, size in SMALL) // 128


def _pack_small(vals, tail=None):
    parts = []
    for name, size in SMALL:
        flat = vals[name].reshape(-1).astype(F32)
        parts.append(jnp.pad(flat, (0, size - flat.shape[0])))
    if tail is not None:
        parts.append(tail.reshape(128))
    flat = jnp.concatenate(parts)
    return jnp.pad(flat, (0, SMALL_ROWS * 128 - flat.shape[0])).reshape(SMALL_ROWS, 128)


def _unpack_small(packed, shapes):
    flat = packed.reshape(-1)
    out, off = {}, 0
    for name, size in SMALL:
        n = math.prod(shapes[name])
        out[name] = flat[off:off + n].reshape(shapes[name])
        off += size
    return out


def kernel(x, ffn1_pre_g, ffn1_w1, ffn1_w3, ffn1_w2, ffn1_post_g, mix_pre_g, w_in, conv_w, conv_b, rg_a_w, rg_a_b, rg_x_w, rg_x_b, lru_lambda, w_lru_out, attn_sinks, rel_bias, w_attn_out, w_gate, b_gate, w_o, mix_post_g, ffn2_pre_g, ffn2_w1, ffn2_w3, ffn2_w2, ffn2_post_g, loss_target, m_ffn1_pre_g, m_ffn1_w1, m_ffn1_w3, m_ffn1_w2, m_ffn1_post_g, m_mix_pre_g, m_w_in, m_conv_w, m_conv_b, m_rg_a_w, m_rg_a_b, m_rg_x_w, m_rg_x_b, m_lru_lambda, m_w_lru_out, m_attn_sinks, m_rel_bias, m_w_attn_out, m_w_gate, m_b_gate, m_w_o, m_mix_post_g, m_ffn2_pre_g, m_ffn2_w1, m_ffn2_w3, m_ffn2_w2, m_ffn2_post_g, v_ffn1_pre_g, v_ffn1_w1, v_ffn1_w3, v_ffn1_w2, v_ffn1_post_g, v_mix_pre_g, v_w_in, v_conv_w, v_conv_b, v_rg_a_w, v_rg_a_b, v_rg_x_w, v_rg_x_b, v_lru_lambda, v_w_lru_out, v_attn_sinks, v_rel_bias, v_w_attn_out, v_w_gate, v_b_gate, v_w_o, v_mix_post_g, v_ffn2_pre_g, v_ffn2_w1, v_ffn2_w3, v_ffn2_w2, v_ffn2_post_g):
    given = dict(locals())
    chip = 2 * lax.axis_index("x") + lax.axis_index("y")
    transposed = ("ffn1_w1", "ffn1_w3", "ffn2_w1", "ffn2_w3")

    def shard(name, moment=""):
        w = given[moment + name][0]
        return w.T if name in transposed else w

    def unshard(name, w):
        return (w.T if name in transposed else w)[None]

    def only_my_columns(a):
        parts = a.reshape(1, 4, NSH, D // NSH)
        return sum(jnp.where(chip == s, parts[:, :, s], 0.0) for s in range(NSH))

    chip_arr = jnp.reshape(chip, (1,)).astype(jnp.int32)
    stage_names = {"ffn1": ["ffn1_w1", "ffn1_w3", "ffn1_w2", "conv_w"],
                   "mix": ["w_in", "w_gate", "w_lru_out", "w_attn_out", "w_o"],
                   "ffn2": ["ffn2_w1", "ffn2_w3", "ffn2_w2"]}
    in_flight, started = {}, None
    for stage, names in stage_names.items():
        bufs = [jnp.where(lax.broadcasted_iota(jnp.int32, (NSH, 4, D // NSH), 0) == chip, given[n], 0.0) if n == "conv_w"
                else _cast_into_slot(shard(n), chip_arr, "cast_" + n, started) for n in names]
        (in_flight[stage],), started = _gather_start([bufs], "gather_start_" + stage)
    all_started = started

    def weights(stage, after):
        names = stage_names[stage]
        send_sems, recv_sems, landing = in_flight[stage]
        if stage == "ffn1":
            after = all_started
        landed = _gather_wait(send_sems, recv_sems, landing, after, "gather_wait_" + stage)
        halves = [b for b in landed if b.dtype == BF16]
        out = dict(zip([n for n, b in zip(names, landed) if b.dtype == BF16], _sibling_fill(halves, "sibling_fill_" + stage)))
        if "conv_w" in names:
            out["conv_w"] = jnp.transpose(landed[names.index("conv_w")], (1, 0, 2)).reshape(4, D)
        return out

    small_shapes = {n: given[n].shape for n, _ in SMALL}
    small_shapes["conv_w"] = (1, 4, D)
    sm = {n: (given[n][0] if given[n].shape[0] == 1 and n != "rel_bias" else given[n]) for n, _ in SMALL if n != "conv_w"}

    reducer = _Reducer(jnp.stack([lax.axis_index("c"), chip]).astype(jnp.int32))
    sq, dx, _, small = _local_step(x[0], loss_target[0], weights, sm, reducer)

    reducer.advance("ffn1", dx)
    reduced_small = _all_reduce_small(_pack_small(small, tail=sq))
    loss = reduced_small[SMALL_USED, 0] * (0.5 / D)
    small_g = _unpack_small(reduced_small, small_shapes)
    grads, delta, new_m, new_v = {}, {}, {}, {}
    after = [reduced_small]
    for stage in ("ffn2", "mix", "ffn1"):
        halves = reducer.finish(stage, after)
        from_sibling = _send_halves(list(halves.values()), "send_halves_" + stage)
        names = list(halves)
        updated = _adamw_halves([shard(n) for n in names], list(halves.values()), from_sibling,
                                [shard(n, "m_") for n in names], [shard(n, "v_") for n in names], "adamw_" + stage)
        for n, results in zip(names, updated):
            grads[n], delta[n], new_m[n], new_v[n] = (unshard(n, r) for r in results)
        after.append(new_v[names[-1]])

    small_g["conv_w"] = only_my_columns(small_g["conv_w"])
    names = [n for n, _ in SMALL]
    flat2d = lambda a: a.reshape(-1, a.shape[-1])
    outs = _adamw_small(*[[flat2d(given[pre + n]) if pre != "g" else flat2d(small_g[n]) for n in names]
                          for pre in ("", "g", "m_", "v_")])
    for dst, arrs in zip((delta, new_m, new_v), outs):
        dst.update({n: a.reshape(given[n].shape) for n, a in zip(names, arrs)})
    grads.update(small_g)
    return (loss, dx[None], *[grads[n] for n in WEIGHTS], *[delta[n] for n in WEIGHTS], *[new_m[n] for n in WEIGHTS],
            *[new_v[n] for n in WEIGHTS])
```

```python
import functools
import math

import jax
import jax.numpy as jnp
from jax import lax
from jax.experimental import pallas as pl
from jax.experimental.pallas import tpu as pltpu

F32, BF16 = jnp.float32, jnp.bfloat16
D = 1024
NSH = 4
FF_S = 704
IN_S = 896
GATE_S = 512
KV_W = 256
CHUNK = 64
KB = 192
N_HEADS = 16
HEAD_DIM = 64
N_BUCKETS = 32
KP = 192
PAD_KEYS = 128
RMS_EPS = 1e-6
NEG_INF = -1e30
LRU_C = 8.0
TM = 512
TM_SCAN = 256
VMEM_LIMIT = 56 * 1024 * 1024
ADAM_LR, ADAM_B1, ADAM_B2, ADAM_EPS, ADAM_WD, ADAM_STEP = 0.001, 0.9, 0.999, 1e-08, 0.01, 10
SMALL_ROWS = 1216
SMALL_SLICE = SMALL_ROWS // 8
MESH = pl.DeviceIdType.MESH

BIG = ["ffn1_w1", "ffn1_w3", "ffn1_w2", "w_in", "w_lru_out", "w_attn_out", "w_gate", "w_o", "ffn2_w1", "ffn2_w3", "ffn2_w2"]
SMALL = [("ffn1_pre_g", 1024), ("ffn1_post_g", 1024), ("mix_pre_g", 1024), ("conv_w", 4096), ("conv_b", 1024),
         ("rg_a_w", 65536), ("rg_a_b", 1024), ("rg_x_w", 65536), ("rg_x_b", 1024), ("lru_lambda", 1024),
         ("attn_sinks", 1024), ("rel_bias", 1024), ("b_gate", 2048), ("mix_post_g", 1024), ("ffn2_pre_g", 1024),
         ("ffn2_post_g", 1024)]
WEIGHTS = ["ffn1_pre_g", "ffn1_w1", "ffn1_w3", "ffn1_w2", "ffn1_post_g", "mix_pre_g", "w_in", "conv_w", "conv_b", "rg_a_w",
           "rg_a_b", "rg_x_w", "rg_x_b", "lru_lambda", "w_lru_out", "attn_sinks", "rel_bias", "w_attn_out", "w_gate", "b_gate",
           "w_o", "mix_post_g", "ffn2_pre_g", "ffn2_w1", "ffn2_w3", "ffn2_w2", "ffn2_post_g"]


def _params(*sem):
    return pltpu.CompilerParams(dimension_semantics=sem or None, vmem_limit_bytes=VMEM_LIMIT)


def _nn(a, b):
    return jnp.dot(a, b, preferred_element_type=F32)


def _nt(a, b):
    return lax.dot_general(a, b, (((1,), (1,)), ((), ())), preferred_element_type=F32)


def _tn(a, b):
    return lax.dot_general(a, b, (((0,), (0,)), ((), ())), preferred_element_type=F32)


def _rms(x, g):
    rstd = lax.rsqrt(jnp.mean(x * x, axis=-1, keepdims=True) + RMS_EPS)
    return (x * rstd) * g


def _rms_bwd(dout, x, g):
    rstd = lax.rsqrt(jnp.mean(x * x, axis=-1, keepdims=True) + RMS_EPS)
    xhat = x * rstd
    dg = jnp.sum(dout * xhat, axis=0, keepdims=True)
    dxhat = dout * g
    dx = rstd * (dxhat - xhat * jnp.mean(dxhat * xhat, axis=-1, keepdims=True))
    return dx, dg


_GELU_K = math.sqrt(2.0 / math.pi)


def _gelu(x):
    return x * (0.5 * (1.0 + jnp.tanh(_GELU_K * (x + 0.044715 * (x * x * x)))))


def _gelu_and_grad(x):
    x2 = x * x
    t = jnp.tanh(_GELU_K * (x + 0.044715 * (x2 * x)))
    cdf = 0.5 * (1.0 + t)
    return x * cdf, cdf + x * (0.5 * (1.0 - t * t) * (_GELU_K * (1.0 + 3.0 * 0.044715 * x2)))


def _softplus_neg(lam):
    z = -lam
    u = jnp.exp(-jnp.abs(z))
    w = 1.0 + u
    log1p_u = jnp.where(w == 1.0, u, jnp.log(w) * (u / (w - 1.0)))
    return jnp.maximum(z, 0.0) + log1p_u


def _lru_coeffs(r, sp):
    log_a = (-LRU_C * r) * sp
    a = jnp.exp(log_a)
    t = jnp.tanh(log_a)
    s = jnp.sqrt(-2.0 * t / (1.0 - t))
    return a, s


def _row_spec(tm, width):
    return pl.BlockSpec((tm, width), lambda i: (i, 0))


def _vec_spec(width):
    return pl.BlockSpec((1, width), lambda i: (0, 0))


_WHOLE = pl.BlockSpec(memory_space=pltpu.VMEM)


def _tile(t, tm=TM):
    return min(tm, t)


def _ffn_fwd(x, gpre, w1g, w3g, w2g, gpost, name, target=None):
    t = x.shape[0]
    tm = _tile(t)
    last = target is not None

    def body(x_ref, gpre_ref, w1_ref, w3_ref, w2_ref, gpost_ref, *refs):
        t_ref, (h_ref, a_ref, b_ref, hm_ref, f_ref), l_ref = (refs[0] if last else None), refs[last:last + 5], refs[-1]
        xv = x_ref[...]
        nb = _rms(xv, gpre_ref[...]).astype(BF16)
        f = jnp.zeros((tm, D), F32)
        for s in range(NSH):
            a = _nt(nb, w1_ref[s])
            b = _nt(nb, w3_ref[s])
            hmb = ((a * jax.nn.sigmoid(a)) * b).astype(BF16)
            a_ref[s] = a.astype(BF16)
            b_ref[s] = b.astype(BF16)
            hm_ref[s] = hmb
            f = f + _nn(hmb, w2_ref[s])
        f_ref[...] = f
        h = xv + 0.5 * _rms(f, gpost_ref[...])
        if last:
            @pl.when(pl.program_id(0) == 0)
            def _():
                l_ref[...] = jnp.zeros_like(l_ref)

            e = h - t_ref[...]
            h_ref[...] = e * (1.0 / D)
            l_ref[...] += jnp.sum(jnp.sum(e * e, axis=0, keepdims=True), axis=1, keepdims=True)
        else:
            h_ref[...] = h

    sh = pl.BlockSpec((NSH, tm, FF_S), lambda i: (0, i, 0))
    act = jax.ShapeDtypeStruct((NSH, t, FF_S), BF16)
    return pl.pallas_call(
        body, grid=(t // tm,), name=name,
        in_specs=[_row_spec(tm, D), _vec_spec(D), _WHOLE, _WHOLE, _WHOLE, _vec_spec(D)] + [_row_spec(tm, D)] * last,
        out_specs=[_row_spec(tm, D), sh, sh, sh, _row_spec(tm, D)] + [pl.BlockSpec((1, 128), lambda i: (0, 0))] * last,
        out_shape=[jax.ShapeDtypeStruct((t, D), F32), act, act, act, jax.ShapeDtypeStruct((t, D), F32)]
        + [jax.ShapeDtypeStruct((1, 128), F32)] * last,
        compiler_params=_params("arbitrary"),
    )(x, gpre, w1g, w3g, w2g, gpost, *([target] if last else []))


def _mix_proj(h1, gmix, w_in_g, w_gate_g, b_gate):
    t = h1.shape[0]
    tm = _tile(t)

    def body(h_ref, g_ref, win_ref, wg_ref, bg_ref, u_ref, q_ref, k_ref, v_ref, xr_ref, xg_ref, gate_ref):
        ub = _rms(h_ref[...], g_ref[...]).astype(BF16)
        u_ref[...] = ub
        p0 = _nn(ub, win_ref[0])
        q_ref[:, 0:896] = p0.astype(BF16)
        p1 = _nn(ub, win_ref[1])
        q_ref[:, 896:1024] = p1[:, 0:128].astype(BF16)
        k_ref[...] = p1[:, 128:384].astype(BF16)
        v_ref[...] = p1[:, 384:640].astype(BF16)
        xr_ref[:, 0:256] = p1[:, 640:896]
        p2 = _nn(ub, win_ref[2])
        xr_ref[:, 256:1024] = p2[:, 0:768]
        xg_ref[:, 0:128] = p2[:, 768:896]
        xg_ref[:, 128:1024] = _nn(ub, win_ref[3])
        for s in range(NSH):
            sl = slice(s * GATE_S, (s + 1) * GATE_S)
            gate_ref[:, sl] = jax.nn.sigmoid(_nn(ub, wg_ref[s]) + bg_ref[:, sl])

    return pl.pallas_call(
        body, grid=(t // tm,), name="mix_proj",
        in_specs=[_row_spec(tm, D), _vec_spec(D), _WHOLE, _WHOLE, _vec_spec(2 * D)],
        out_specs=[_row_spec(tm, D), _row_spec(tm, D), _row_spec(tm, KV_W), _row_spec(tm, KV_W), _row_spec(tm, D),
                   _row_spec(tm, D), _row_spec(tm, 2 * D)],
        out_shape=[jax.ShapeDtypeStruct((t, D), BF16), jax.ShapeDtypeStruct((t, D), BF16),
                   jax.ShapeDtypeStruct((t, KV_W), BF16), jax.ShapeDtypeStruct((t, KV_W), BF16),
                   jax.ShapeDtypeStruct((t, D), F32), jax.ShapeDtypeStruct((t, D), F32),
                   jax.ShapeDtypeStruct((t, 2 * D), F32)],
        compiler_params=_params("arbitrary"),
    )(h1, gmix, w_in_g, w_gate_g, b_gate)


def _rglru_fwd(xr, xg, conv_w, conv_b, wa2, ba, wx2, bx, lam):
    t = xr.shape[0]
    tm = _tile(t, TM_SCAN)
    nb8 = tm // 8

    def body(xr_ref, xrp_ref, xg_ref, cw_ref, cb_ref, wa_ref, ba_ref, wx_ref, bx_ref, lam_ref,
             hr_ref, yain_ref, xc_ref, r_ref, ig_ref, a_sc, s_ref, ext, h_sc):
        i = pl.program_id(0)

        @pl.when(i == 0)
        def _():
            h_sc[...] = jnp.zeros_like(h_sc)

        ext[0:8, :] = jnp.where(i == 0, 0.0, xrp_ref[...])
        ext[8:8 + tm, :] = xr_ref[...]
        xc = jnp.broadcast_to(cb_ref[...], (tm, D))
        for tap in range(4):
            xc = xc + ext[pl.ds(5 + tap, tm), :] * cw_ref[tap:tap + 1, :]
        xc_ref[...] = xc
        xcb = xc.astype(BF16)
        for p in range(8):
            sl = slice(p * 128, (p + 1) * 128)
            r_ref[:, sl] = jax.nn.sigmoid(_nn(xcb[:, sl], wa_ref[p]) + ba_ref[:, sl])
            ig_ref[:, sl] = jax.nn.sigmoid(_nn(xcb[:, sl], wx_ref[p]) + bx_ref[:, sl])
        a, s = _lru_coeffs(r_ref[...], _softplus_neg(lam_ref[...]))
        a_sc[...] = a
        s_ref[...] = s
        hr_ref[...] = s * (ig_ref[...] * xc)

        def blk(j, h):
            st = pl.multiple_of(j * 8, 8)
            a8 = a_sc[pl.ds(st, 8), :]
            u8 = hr_ref[pl.ds(st, 8), :]
            rows = []
            for k in range(8):
                h = a8[k:k + 1, :] * h + u8[k:k + 1, :]
                rows.append(h)
            hr_ref[pl.ds(st, 8), :] = jnp.concatenate(rows, axis=0)
            return h

        h_sc[0:1, :] = lax.fori_loop(0, nb8, blk, h_sc[0:1, :])
        yain_ref[...] = (hr_ref[...] * _gelu(xg_ref[...])).astype(BF16)

    prev = pl.BlockSpec((8, D), lambda i: (jnp.maximum(i * nb8 - 1, 0), 0))
    full = lambda shape: pl.BlockSpec(shape, lambda i: tuple(0 for _ in shape))
    f32 = jax.ShapeDtypeStruct((t, D), F32)
    return pl.pallas_call(
        body, grid=(t // tm,), name="rglru_fwd",
        in_specs=[_row_spec(tm, D), prev, _row_spec(tm, D), full((4, D)), _vec_spec(D), full((8, 128, 128)), _vec_spec(D),
                  full((8, 128, 128)), _vec_spec(D), _vec_spec(D)],
        out_specs=[_row_spec(tm, D)] * 7,
        out_shape=[f32, jax.ShapeDtypeStruct((t, D), BF16), f32, f32, f32, f32, f32],
        scratch_shapes=[pltpu.VMEM((tm + 8, D), F32), pltpu.VMEM((8, D), F32)],
        compiler_params=_params("arbitrary"),
    )(xr, xr, xg, conv_w, conv_b, wa2, ba, wx2, bx, lam)


def _bias_fwd(table_t, onehot_t):
    def body(t_ref, e_ref, o_ref):
        o_ref[...] = jnp.dot(t_ref[...], e_ref[...], preferred_element_type=F32, precision=lax.Precision.HIGHEST)

    return pl.pallas_call(body, out_shape=jax.ShapeDtypeStruct((N_HEADS, CHUNK * KB), F32), name="bias_fwd",
                          compiler_params=_params())(table_t, onehot_t)


def _bias_bwd(dbias_flat, onehot_t, ds_rows):
    def body(d_ref, e_ref, s_ref, o_ref, so_ref):
        o_ref[...] = lax.dot_general(d_ref[...], e_ref[...], (((1,), (1,)), ((), ())), preferred_element_type=F32,
                                     precision=lax.Precision.HIGHEST)
        so_ref[...] = jnp.zeros_like(so_ref)
        for r in range(4):
            so_ref[:, r:r + 1] = jnp.sum(s_ref[:, r * CHUNK:(r + 1) * CHUNK], axis=1, keepdims=True)

    return pl.pallas_call(body, out_shape=[jax.ShapeDtypeStruct((N_HEADS, N_BUCKETS), F32), jax.ShapeDtypeStruct((8, 128), F32)],
                          name="bias_bwd", compiler_params=_params())(dbias_flat, onehot_t, ds_rows)


def _stack_heads(q):
    return jnp.concatenate(
        [jnp.concatenate([q[:, (4 * g + r) * HEAD_DIM:(4 * g + r + 1) * HEAD_DIM] for g in range(4)], axis=1)
         for r in range(4)], axis=0)


def _unstack_heads(o):
    return jnp.concatenate([o[r * CHUNK:(r + 1) * CHUNK, g * HEAD_DIM:(g + 1) * HEAD_DIM] for g in range(4) for r in range(4)],
                           axis=1)


def _block_diag(w, mask):
    return jnp.concatenate([w] * 4, axis=0) * mask


def _group_softmax(qk, bias_g, sink, valid):
    s = qk * (HEAD_DIM ** -0.5) + bias_g
    s = jnp.where(valid, s, NEG_INF)
    m = jnp.maximum(jnp.max(s, axis=0, keepdims=True), sink)
    e = jnp.exp(s - m)
    es = jnp.exp(sink - m)
    inv = 1.0 / (jnp.sum(e, axis=0, keepdims=True) + es)
    return e * inv, es * inv


def _attn_fwd(sink_rows, q, kp, vp, bias_t, mask):
    t = q.shape[0]
    per_step = 4

    def body(sink_ref, q_ref, kp_ref, vp_ref, bias_ref, mask_ref, o_ref):
        owns = [mask_ref[g * KP:(g + 1) * KP, :] for g in range(4)]
        for k in range(per_step):
            c = pl.program_id(0) * per_step + k
            rows = slice(k * CHUNK, (k + 1) * CHUNK)
            st = pl.multiple_of(c * CHUNK, CHUNK)
            kw = kp_ref[pl.ds(st, KP), :]
            vw = vp_ref[pl.ds(st, KP), :]
            q_all = _stack_heads(q_ref[rows, :])
            valid = lax.broadcasted_iota(jnp.int32, (KP, 1), 0) + c * CHUNK >= PAD_KEYS
            scores = [_nt(kw * owns[g], q_all) for g in range(4)]
            ps = [_group_softmax(scores[g], bias_ref[g * KP:(g + 1) * KP, :], sink_ref[g:g + 1, :], valid)[0]
                  for g in range(4)]
            o_all = sum(_tn(ps[g].astype(BF16), vw * owns[g]) for g in range(4))
            o_ref[rows, :] = _unstack_heads(o_all).astype(BF16)

    return pl.pallas_call(
        body, grid=(t // (per_step * CHUNK),), name="attn_fwd",
        in_specs=[_WHOLE, _row_spec(per_step * CHUNK, D), _WHOLE, _WHOLE, _WHOLE, _WHOLE],
        out_specs=_row_spec(per_step * CHUNK, D),
        out_shape=jax.ShapeDtypeStruct((t, D), BF16),
        compiler_params=_params("arbitrary"),
    )(sink_rows, q, kp, vp, bias_t, mask)


def _merge_fwd(yain, o, gate, h1, w_lru, w_att, w_o, gpost):
    t = h1.shape[0]
    tm = _tile(t)

    def body(ya_ref, o_ref, g_ref, h_ref, wl_ref, wa_ref, wo_ref, gp_ref, h2_ref, mo_ref, mg_ref, ya_out, yb_out):
        ya = _nn(ya_ref[...], wl_ref[...])
        yb = _nn(o_ref[...], wa_ref[...])
        g0 = g_ref[:, 0:D]
        g1 = g_ref[:, D:2 * D]
        mg = (g0 * ya + g1 * yb).astype(BF16)
        mo = _nn(mg, wo_ref[...])
        ya_out[...] = (ya * (g0 * (1.0 - g0))).astype(BF16)
        yb_out[...] = (yb * (g1 * (1.0 - g1))).astype(BF16)
        mg_ref[...] = mg
        mo_ref[...] = mo
        h2_ref[...] = h_ref[...] + _rms(mo, gp_ref[...])

    f32 = jax.ShapeDtypeStruct((t, D), F32)
    b16 = jax.ShapeDtypeStruct((t, D), BF16)
    return pl.pallas_call(
        body, grid=(t // tm,), name="merge_fwd",
        in_specs=[_row_spec(tm, D), _row_spec(tm, D), _row_spec(tm, 2 * D), _row_spec(tm, D), _WHOLE, _WHOLE, _WHOLE,
                  _vec_spec(D)],
        out_specs=[_row_spec(tm, D)] * 5,
        out_shape=[f32, f32, b16, b16, b16],
        compiler_params=_params("arbitrary"),
    )(yain, o, gate, h1, w_lru, w_att, w_o, gpost)


def _ffn_bwd(dh, x, f, a, b, gpre, gpost, w1g, w3g, w2g, name):
    t = x.shape[0]
    tm = _tile(t, TM_SCAN)

    def body(dh_ref, x_ref, f_ref, a_ref, b_ref, gpre_ref, gpost_ref, w1_ref, w3_ref, w2_ref,
             dx_ref, n_ref, da_ref, db_ref, df_ref, dgpre_ref, dgpost_ref):
        @pl.when(pl.program_id(0) == 0)
        def _():
            dgpre_ref[...] = jnp.zeros_like(dgpre_ref)
            dgpost_ref[...] = jnp.zeros_like(dgpost_ref)

        dhv = dh_ref[...]
        xv = x_ref[...]
        df, dgp = _rms_bwd(0.5 * dhv, f_ref[...], gpost_ref[...])
        dgpost_ref[...] += dgp
        dfb = df.astype(BF16)
        df_ref[...] = dfb
        n_ref[...] = _rms(xv, gpre_ref[...]).astype(BF16)
        dn = jnp.zeros((tm, D), F32)
        for s in range(NSH):
            av = a_ref[s].astype(F32)
            bv = b_ref[s].astype(F32)
            sg = jax.nn.sigmoid(av)
            dhm = _nt(dfb, w2_ref[s])
            dab = (dhm * bv * (sg * (1.0 + av * (1.0 - sg)))).astype(BF16)
            dbb = (dhm * (av * sg)).astype(BF16)
            da_ref[s] = dab
            db_ref[s] = dbb
            dn = dn + _nn(dab, w1_ref[s]) + _nn(dbb, w3_ref[s])
        dxn, dg = _rms_bwd(dn, xv, gpre_ref[...])
        dgpre_ref[...] += dg
        dx_ref[...] = dhv + dxn

    sh = pl.BlockSpec((NSH, tm, FF_S), lambda i: (0, i, 0))
    act = jax.ShapeDtypeStruct((NSH, t, FF_S), BF16)
    vec = jax.ShapeDtypeStruct((1, D), F32)
    return pl.pallas_call(
        body, grid=(t // tm,), name=name,
        in_specs=[_row_spec(tm, D), _row_spec(tm, D), _row_spec(tm, D), sh, sh, _vec_spec(D), _vec_spec(D), _WHOLE, _WHOLE,
                  _WHOLE],
        out_specs=[_row_spec(tm, D), _row_spec(tm, D), sh, sh, _row_spec(tm, D), _vec_spec(D), _vec_spec(D)],
        out_shape=[jax.ShapeDtypeStruct((t, D), F32), jax.ShapeDtypeStruct((t, D), BF16), act, act,
                   jax.ShapeDtypeStruct((t, D), BF16), vec, vec],
        compiler_params=_params("arbitrary"),
    )(dh, x, f, a, b, gpre, gpost, w1g, w3g, w2g)


def _behind(body, after):
    if after is None:
        return body, [], []

    def ordered(_, *refs):
        body(*refs)

    return ordered, [_ANY], [after]


def _ffn_bwd_acts(dh, x, f, a, b, gpre, gpost, w2g, name):
    t = x.shape[0]
    tm = _tile(t)

    def body(dh_ref, x_ref, f_ref, a_ref, b_ref, gpre_ref, gpost_ref, w2_ref, n_ref, da_ref, db_ref, df_ref, dgpost_ref):
        @pl.when(pl.program_id(0) == 0)
        def _():
            dgpost_ref[...] = jnp.zeros_like(dgpost_ref)

        df, dgp = _rms_bwd(0.5 * dh_ref[...], f_ref[...], gpost_ref[...])
        dgpost_ref[...] += dgp
        dfb = df.astype(BF16)
        df_ref[...] = dfb
        n_ref[...] = _rms(x_ref[...], gpre_ref[...]).astype(BF16)
        for s in range(NSH):
            av = a_ref[s].astype(F32)
            bv = b_ref[s].astype(F32)
            sg = jax.nn.sigmoid(av)
            dhm = _nt(dfb, w2_ref[s])
            da_ref[s] = (dhm * bv * (sg * (1.0 + av * (1.0 - sg)))).astype(BF16)
            db_ref[s] = (dhm * (av * sg)).astype(BF16)

    sh = pl.BlockSpec((NSH, tm, FF_S), lambda i: (0, i, 0))
    act = jax.ShapeDtypeStruct((NSH, t, FF_S), BF16)
    b16 = jax.ShapeDtypeStruct((t, D), BF16)
    return pl.pallas_call(
        body, grid=(t // tm,), name=name,
        in_specs=[_row_spec(tm, D), _row_spec(tm, D), _row_spec(tm, D), sh, sh, _vec_spec(D), _vec_spec(D), _WHOLE],
        out_specs=[_row_spec(tm, D), sh, sh, _row_spec(tm, D), _vec_spec(D)],
        out_shape=[b16, act, act, b16, jax.ShapeDtypeStruct((1, D), F32)],
        compiler_params=_params("arbitrary"),
    )(dh, x, f, a, b, gpre, gpost, w2g)


def _ffn_bwd_input(dh, x, da, db, gpre, w1g, w3g, name, after):
    t = x.shape[0]
    tm = _tile(t)

    def body(dh_ref, x_ref, da_ref, db_ref, gpre_ref, w1_ref, w3_ref, dx_ref, dgpre_ref):
        @pl.when(pl.program_id(0) == 0)
        def _():
            dgpre_ref[...] = jnp.zeros_like(dgpre_ref)

        dn = jnp.zeros((tm, D), F32)
        for s in range(NSH):
            dn = dn + _nn(da_ref[s], w1_ref[s]) + _nn(db_ref[s], w3_ref[s])
        dxn, dg = _rms_bwd(dn, x_ref[...], gpre_ref[...])
        dgpre_ref[...] += dg
        dx_ref[...] = dh_ref[...] + dxn

    sh = pl.BlockSpec((NSH, tm, FF_S), lambda i: (0, i, 0))
    body, specs, operands = _behind(body, after)
    return pl.pallas_call(
        body, grid=(t // tm,), name=name,
        in_specs=specs + [_row_spec(tm, D), _row_spec(tm, D), sh, sh, _vec_spec(D), _WHOLE, _WHOLE],
        out_specs=[_row_spec(tm, D), _vec_spec(D)],
        out_shape=[jax.ShapeDtypeStruct((t, D), F32), jax.ShapeDtypeStruct((1, D), F32)],
        compiler_params=_params("arbitrary"),
    )(*operands, dh, x, da, db, gpre, w1g, w3g)


def _wgrad(a, b, a_spec, b_spec, out_spec, out_shape, grid, name, after=None):
    def body(a_ref, b_ref, o_ref):
        o_ref[...] = _tn(a_ref[...], b_ref[...]).astype(BF16)

    body, specs, operands = _behind(body, after)
    return pl.pallas_call(body, grid=grid, name=name, in_specs=specs + [a_spec, b_spec], out_specs=out_spec,
                          out_shape=jax.ShapeDtypeStruct(out_shape, BF16),
                          compiler_params=_params(*("arbitrary",) * len(grid)))(*operands, a, b)


def _wgrad_cols(act, dsh, width, name, after=None):
    t = act.shape[0]
    if dsh.ndim == 3:
        b_spec = pl.BlockSpec((None, t, width), lambda s, k: (s, 0, 0))
    else:
        b_spec = pl.BlockSpec((t, width), lambda s, k: (0, s))
    return _wgrad(act, dsh, pl.BlockSpec((t, 512), lambda s, k: (0, k)), b_spec,
                  pl.BlockSpec((None, 512, width), lambda s, k: (s, k, 0)), (NSH, D, width), (NSH, 2), name, after)


def _wgrad_rows(hm, df, name, after=None):
    t = df.shape[0]
    return _wgrad(hm, df, pl.BlockSpec((None, t, FF_S), lambda s: (s, 0, 0)), pl.BlockSpec((t, D), lambda s: (0, 0)),
                  pl.BlockSpec((None, FF_S, D), lambda s: (s, 0, 0)), (NSH, FF_S, D), (NSH,), name, after)


def _wgrad_sq(a, b, name, after=None):
    t = a.shape[0]
    return _wgrad(a, b, pl.BlockSpec((t, 512), lambda i, j: (0, i)), pl.BlockSpec((t, 512), lambda i, j: (0, j)),
                  pl.BlockSpec((512, 512), lambda i, j: (i, j)), (D, D), (2, 2), name, after)


def _mix_bwd1(dh2, mo, gpost, gate, ya, yb, xg, hr, w_o, w_lru, w_att, after):
    t = dh2.shape[0]
    tm = _tile(t, TM_SCAN)

    def body(dh_ref, mo_ref, gp_ref, g_ref, ya_ref, yb_ref, xg_ref, hr_ref, wo_ref, wl_ref, wa_ref,
             dmo_ref, dya_ref, dyb_ref, dgate_ref, dhr_ref, dxg_ref, do_ref, dgp_ref, dbg_ref):
        @pl.when(pl.program_id(0) == 0)
        def _():
            dgp_ref[...] = jnp.zeros_like(dgp_ref)
            dbg_ref[...] = jnp.zeros_like(dbg_ref)

        dmo, dgp = _rms_bwd(dh_ref[...], mo_ref[...], gp_ref[...])
        dgp_ref[...] += dgp
        dmob = dmo.astype(BF16)
        dmo_ref[...] = dmob
        dm = _nt(dmob, wo_ref[...])
        g0 = g_ref[:, 0:D]
        g1 = g_ref[:, D:2 * D]
        dyab = (dm * g0).astype(BF16)
        dybb = (dm * g1).astype(BF16)
        dya_ref[...] = dyab
        dyb_ref[...] = dybb
        dg0 = dm * ya_ref[...].astype(F32)
        dg1 = dm * yb_ref[...].astype(F32)
        dgate_ref[:, 0:D] = dg0.astype(BF16)
        dgate_ref[:, D:2 * D] = dg1.astype(BF16)
        dbg_ref[:, 0:D] += jnp.sum(dg0, axis=0, keepdims=True)
        dbg_ref[:, D:2 * D] += jnp.sum(dg1, axis=0, keepdims=True)
        dyain = _nt(dyab, wl_ref[...])
        do_ref[...] = _nt(dybb, wa_ref[...]).astype(BF16)
        xgv = xg_ref[...]
        gelu, gelu_grad = _gelu_and_grad(xgv)
        dhr_ref[...] = dyain * gelu
        dxg_ref[...] = (dyain * hr_ref[...] * gelu_grad).astype(BF16)

    b16 = jax.ShapeDtypeStruct((t, D), BF16)
    body, specs, operands = _behind(body, after)
    return pl.pallas_call(
        body, grid=(t // tm,), name="mix_bwd1",
        in_specs=specs + [_row_spec(tm, D), _row_spec(tm, D), _vec_spec(D), _row_spec(tm, 2 * D), _row_spec(tm, D),
                          _row_spec(tm, D), _row_spec(tm, D), _row_spec(tm, D), _WHOLE, _WHOLE, _WHOLE],
        out_specs=[_row_spec(tm, D), _row_spec(tm, D), _row_spec(tm, D), _row_spec(tm, 2 * D), _row_spec(tm, D),
                   _row_spec(tm, D), _row_spec(tm, D), _vec_spec(D), _vec_spec(2 * D)],
        out_shape=[b16, b16, b16, jax.ShapeDtypeStruct((t, 2 * D), BF16), jax.ShapeDtypeStruct((t, D), F32), b16, b16,
                   jax.ShapeDtypeStruct((1, D), F32), jax.ShapeDtypeStruct((1, 2 * D), F32)],
        compiler_params=_params("arbitrary"),
    )(*operands, dh2, mo, gpost, gate, ya, yb, xg, hr, w_o, w_lru, w_att)


def _rglru_bwd(dhr, hr, xc, r, ig, a, s, xr, conv_w, wa2, wx2, lam, after):
    t = dhr.shape[0]
    tm = _tile(t, TM_SCAN)
    nb8 = tm // 8
    nt = t // tm

    def body(dhr_ref, hr_ref, hrp_ref, xc_ref, r_ref, ig_ref, a_sc, s_ref, xr_ref, cw_ref, wa_ref, wx_ref, lam_ref,
             dxr_ref, dwa_ref, dwx_ref, dba_ref, dbx_ref, dlam_ref, dcw_ref, dcb_ref,
             ext_h, ext_d, g_sc, c_sc, nxt_sc):
        i = pl.program_id(0)
        first_tile = i == nt - 1

        @pl.when(i == 0)
        def _():
            c_sc[...] = jnp.zeros_like(c_sc)
            nxt_sc[...] = jnp.zeros_like(nxt_sc)
            for ref in (dwa_ref, dwx_ref, dba_ref, dbx_ref, dlam_ref, dcw_ref, dcb_ref):
                ref[...] = jnp.zeros_like(ref)

        lamv = lam_ref[...]
        sp = _softplus_neg(lamv)
        rv = r_ref[...]
        igv = ig_ref[...]
        xcv = xc_ref[...]
        a = a_sc[...]
        s = s_ref[...]

        def blk(jj, c):
            st = pl.multiple_of((nb8 - 1 - jj) * 8, 8)
            d8 = dhr_ref[pl.ds(st, 8), :]
            a8 = a_sc[pl.ds(st, 8), :]
            rows = [None] * 8
            for k in range(7, -1, -1):
                g = d8[k:k + 1, :] + c
                c = a8[k:k + 1, :] * g
                rows[k] = g
            g_sc[pl.ds(st, 8), :] = jnp.concatenate(rows, axis=0)
            return c

        c_sc[0:1, :] = lax.fori_loop(0, nb8, blk, c_sc[0:1, :])
        g = g_sc[...]
        ext_h[0:8, :] = jnp.where(first_tile, 0.0, hrp_ref[...])
        ext_h[8:8 + tm, :] = hr_ref[...]
        hprev = ext_h[pl.ds(7, tm), :]
        d_s = g * (igv * xcv)
        dig = g * s * xcv
        dxc = g * s * igv
        dla = (g * hprev) * a - d_s * ((a * a) / s)
        dr_pre = (dla * (-LRU_C * sp)) * (rv * (1.0 - rv))
        di_pre = dig * (igv * (1.0 - igv))
        dlam_ref[...] += jnp.sum(dla * (LRU_C * rv), axis=0, keepdims=True) * jax.nn.sigmoid(-lamv)
        dba_ref[...] += jnp.sum(dr_pre, axis=0, keepdims=True)
        dbx_ref[...] += jnp.sum(di_pre, axis=0, keepdims=True)
        drb = dr_pre.astype(BF16)
        dib = di_pre.astype(BF16)
        xcb = xcv.astype(BF16)
        ext_d[tm:tm + 8, :] = nxt_sc[...]
        for p in range(8):
            sl = slice(p * 128, (p + 1) * 128)
            ext_d[0:tm, sl] = dxc[:, sl] + _nt(drb[:, sl], wa_ref[p]) + _nt(dib[:, sl], wx_ref[p])
            dwa_ref[p] += _tn(xcb[:, sl], drb[:, sl])
            dwx_ref[p] += _tn(xcb[:, sl], dib[:, sl])
        dxcv = ext_d[0:tm, :]
        nxt_sc[...] = ext_d[0:8, :]
        dcb_ref[...] += jnp.sum(dxcv, axis=0, keepdims=True)
        xrv = xr_ref[...]
        dxr = jnp.zeros((tm, D), F32)
        for tap in range(4):
            ext_h[0:tm, :] = ext_d[pl.ds(3 - tap, tm), :]
            ahead = ext_h[0:tm, :]
            dxr = dxr + ahead * cw_ref[tap:tap + 1, :]
            dcw_ref[tap:tap + 1, :] += jnp.sum(ahead * xrv, axis=0, keepdims=True)
        dxr_ref[...] = dxr.astype(BF16)

    rev = pl.BlockSpec((tm, D), lambda i: (nt - 1 - i, 0))
    prev = pl.BlockSpec((8, D), lambda i: (jnp.maximum((nt - 1 - i) * nb8 - 1, 0), 0))
    full = lambda shape: pl.BlockSpec(shape, lambda i: tuple(0 for _ in shape))
    vec = jax.ShapeDtypeStruct((1, D), F32)
    blocks = jax.ShapeDtypeStruct((8, 128, 128), F32)
    body, specs, operands = _behind(body, after)
    return pl.pallas_call(
        body, grid=(nt,), name="rglru_bwd",
        in_specs=specs + [rev, rev, prev, rev, rev, rev, rev, rev, rev, full((4, D)), full((8, 128, 128)),
                          full((8, 128, 128)), _vec_spec(D)],
        out_specs=[rev, full((8, 128, 128)), full((8, 128, 128)), _vec_spec(D), _vec_spec(D), _vec_spec(D), full((4, D)),
                   _vec_spec(D)],
        out_shape=[jax.ShapeDtypeStruct((t, D), BF16), blocks, blocks, vec, vec, vec, jax.ShapeDtypeStruct((4, D), F32), vec],
        scratch_shapes=[pltpu.VMEM((tm + 8, D), F32), pltpu.VMEM((tm + 8, D), F32),
                        pltpu.VMEM((tm, D), F32), pltpu.VMEM((8, D), F32), pltpu.VMEM((8, D), F32)],
        compiler_params=_params("arbitrary"),
    )(*operands, dhr, hr, hr, xc, r, ig, a, s, xr, conv_w, wa2, wx2, lam)


def _attn_bwd(sink_rows, q, kp, vp, bias_t, mask, do):
    t = q.shape[0]
    tp = kp.shape[0]
    per_step = 4

    def body(sink_ref, q_ref, kp_ref, vp_ref, bias_ref, mask_ref, do_ref, dq_ref, dk_ref, dv_ref, dbias_ref, ds_ref):
        @pl.when(pl.program_id(0) == 0)
        def _():
            for ref in (dk_ref, dv_ref, dbias_ref, ds_ref):
                ref[...] = jnp.zeros_like(ref)

        maskv = mask_ref[...]
        lane_group = lax.broadcasted_iota(jnp.int32, (1, 4 * HEAD_DIM), 1) // HEAD_DIM

        def own_blocks(full):
            out = full[0:KP]
            for g in range(1, 4):
                out = jnp.where(lane_group == g, full[g * KP:(g + 1) * KP], out)
            return out

        dsc_sum, dsinks, dks, dvs = 0.0, [0.0] * 4, [], []
        for k in range(per_step):
            c = pl.program_id(0) * per_step + k
            chunk = slice(k * CHUNK, (k + 1) * CHUNK)
            st = pl.multiple_of(c * CHUNK, CHUNK)
            kbd = _block_diag(kp_ref[pl.ds(st, KP), :], maskv)
            vbd = _block_diag(vp_ref[pl.ds(st, KP), :], maskv)
            q_all = _stack_heads(q_ref[chunk, :])
            do_all = _stack_heads(do_ref[chunk, :])
            valid = lax.broadcasted_iota(jnp.int32, (KP, 1), 0) + c * CHUNK >= PAD_KEYS
            qk = _nt(kbd, q_all)
            dp = _nt(vbd, do_all)
            ps, dscs = [], []
            for g in range(4):
                rows = slice(g * KP, (g + 1) * KP)
                p, sink_p = _group_softmax(qk[rows], bias_ref[rows, :], sink_ref[g:g + 1, :], valid)
                delta = jnp.sum(p * dp[rows], axis=0, keepdims=True)
                ps.append(p)
                dscs.append(p * (dp[rows] - delta))
                dsinks[g] = dsinks[g] - sink_p * delta
            dsc = jnp.concatenate(dscs, axis=0)
            dsc_sum = dsc_sum + dsc
            dsb = (dsc * (HEAD_DIM ** -0.5)).astype(BF16)
            dq_ref[chunk, :] = _unstack_heads(_tn(dsb, kbd)).astype(BF16)
            dks.append((st, own_blocks(_nn(dsb, q_all))))
            dvs.append((st, own_blocks(_nn(jnp.concatenate(ps, axis=0).astype(BF16), do_all))))
        dbias_ref[...] += dsc_sum
        for g in range(4):
            ds_ref[g:g + 1, :] += dsinks[g]
        for (st, dkw), (_, dvw) in zip(dks, dvs):
            dk_ref[pl.ds(st, KP), :] += dkw
            dv_ref[pl.ds(st, KP), :] += dvw

    full = lambda shape: pl.BlockSpec(shape, lambda i: tuple(0 for _ in shape))
    return pl.pallas_call(
        body, grid=(t // (per_step * CHUNK),), name="attn_bwd",
        in_specs=[_WHOLE, _row_spec(per_step * CHUNK, D), _WHOLE, _WHOLE, _WHOLE, _WHOLE, _row_spec(per_step * CHUNK, D)],
        out_specs=[_row_spec(per_step * CHUNK, D), full((tp, KV_W)), full((tp, KV_W)), full((4 * KP, 4 * CHUNK)),
                   full((8, 4 * CHUNK))],
        out_shape=[jax.ShapeDtypeStruct((t, D), BF16), jax.ShapeDtypeStruct((tp, KV_W), F32),
                   jax.ShapeDtypeStruct((tp, KV_W), F32), jax.ShapeDtypeStruct((4 * KP, 4 * CHUNK), F32),
                   jax.ShapeDtypeStruct((8, 4 * CHUNK), F32)],
        compiler_params=_params("arbitrary"),
    )(sink_rows, q, kp, vp, bias_t, mask, do)


def _mix_bwd2(dproj, dgate, h1, dh2, gmix, w_in_g, w_gate_g, after):
    t = h1.shape[0]
    tm = _tile(t)

    def body(dp_ref, dg_ref, h_ref, dh_ref, g_ref, win_ref, wg_ref, dh1_ref, dgm_ref):
        @pl.when(pl.program_id(0) == 0)
        def _():
            dgm_ref[...] = jnp.zeros_like(dgm_ref)

        du = jnp.zeros((tm, D), F32)
        for s in range(NSH):
            du = du + _nt(dp_ref[:, s * IN_S:(s + 1) * IN_S], win_ref[s])
            du = du + _nt(dg_ref[:, s * GATE_S:(s + 1) * GATE_S], wg_ref[s])
        dxn, dg = _rms_bwd(du, h_ref[...], g_ref[...])
        dgm_ref[...] += dg
        dh1_ref[...] = dh_ref[...] + dxn

    body, specs, operands = _behind(body, after)
    return pl.pallas_call(
        body, grid=(t // tm,), name="mix_bwd2",
        in_specs=specs + [_row_spec(tm, NSH * IN_S), _row_spec(tm, 2 * D), _row_spec(tm, D), _row_spec(tm, D), _vec_spec(D),
                          _WHOLE, _WHOLE],
        out_specs=[_row_spec(tm, D), _vec_spec(D)],
        out_shape=[jax.ShapeDtypeStruct((t, D), F32), jax.ShapeDtypeStruct((1, D), F32)],
        compiler_params=_params("arbitrary"),
    )(*operands, dproj, dgate, h1, dh2, gmix, w_in_g, w_gate_g)


def _band_onehot():
    nb = N_BUCKETS // 2
    max_exact = nb // 2
    rel = jnp.arange(KB)[None, :] - PAD_KEYS - jnp.arange(CHUNK)[:, None]
    ret = jnp.where(rel > 0, nb, 0)
    n = jnp.abs(rel)
    nf = jnp.maximum(n, 1).astype(jnp.float32)
    large = max_exact + (jnp.log(nf / max_exact) / math.log(128 / max_exact) * (nb - max_exact)).astype(jnp.int32)
    large = jnp.minimum(large, nb - 1)
    buckets = (ret + jnp.where(n < max_exact, n, large)).reshape(1, CHUNK * KB)
    return (buckets == jnp.arange(N_BUCKETS)[:, None]).astype(F32)


def _pair_blocks(w):
    pairs = w.reshape(8, 2, 64, 64)
    z = jnp.zeros((8, 64, 64), w.dtype)
    return jnp.concatenate([jnp.concatenate([pairs[:, 0], z], axis=2), jnp.concatenate([z, pairs[:, 1]], axis=2)], axis=1)


def _unpair_blocks(w2):
    return jnp.stack([w2[:, 0:64, 0:64], w2[:, 64:128, 64:128]], axis=1).reshape(16, 64, 64)


def _local_step(x, target, weights, sm, reducer):
    row = lambda v: v.reshape(1, -1)
    wg = dict(weights("ffn1", x))
    sm = dict(sm, conv_w=wg["conv_w"])
    onehot_t = _band_onehot()
    bias = _bias_fwd(sm["rel_bias"].T, onehot_t).reshape(4, 4, CHUNK, KB)
    bias_t = jnp.pad(jnp.transpose(bias, (0, 3, 1, 2)), ((0, 0), (0, KP - KB), (0, 0), (0, 0))).reshape(4 * KP, 4 * CHUNK)
    sink_rows = jnp.pad(jnp.repeat(sm["attn_sinks"].reshape(4, 4), CHUNK, axis=1), ((0, 4), (0, 0)))
    grp = jnp.arange(4 * KP)[:, None] // KP == jnp.arange(4 * HEAD_DIM)[None, :] // HEAD_DIM
    mask = (grp & (jnp.arange(4 * KP)[:, None] % KP < KB)).astype(BF16)
    wa2 = _pair_blocks(sm["rg_a_w"]).astype(BF16)
    wx2 = _pair_blocks(sm["rg_x_w"]).astype(BF16)

    h1, a1, b1, hm1, f1 = _ffn_fwd(x, row(sm["ffn1_pre_g"]), wg["ffn1_w1"], wg["ffn1_w3"], wg["ffn1_w2"],
                                   row(sm["ffn1_post_g"]), "ffn1_fwd")
    wg.update(weights("mix", h1))
    w_lru = wg["w_lru_out"].reshape(D, D)
    w_att = wg["w_attn_out"].reshape(D, D)
    w_o = wg["w_o"].reshape(D, D)
    u, q, k, v, xr, xg, gate = _mix_proj(h1, row(sm["mix_pre_g"]), wg["w_in"], wg["w_gate"], row(sm["b_gate"]))
    hr, yain, xc, r, ig, lru_a, lru_s = _rglru_fwd(xr, xg, sm["conv_w"], row(sm["conv_b"]), wa2, row(sm["rg_a_b"]), wx2,
                                     row(sm["rg_x_b"]), row(sm["lru_lambda"]))
    kp = jnp.pad(k, ((PAD_KEYS, KP - KB), (0, 0)))
    vp = jnp.pad(v, ((PAD_KEYS, KP - KB), (0, 0)))
    o = _attn_fwd(sink_rows, q, kp, vp, bias_t, mask)
    wg.update(weights("ffn2", o))
    h2, mo, merged, ya, yb = _merge_fwd(yain, o, gate, h1, w_lru, w_att, w_o, row(sm["mix_post_g"]))
    dy, a2, b2, hm2, f2, sq = _ffn_fwd(h2, row(sm["ffn2_pre_g"]), wg["ffn2_w1"], wg["ffn2_w3"], wg["ffn2_w2"],
                                       row(sm["ffn2_post_g"]), "ffn2_fwd", target)

    big, small = {}, {}
    dh2, n2, da2, db2, df2, small["ffn2_pre_g"], small["ffn2_post_g"] = _ffn_bwd(
        dy, h2, f2, a2, b2, row(sm["ffn2_pre_g"]), row(sm["ffn2_post_g"]), wg["ffn2_w1"], wg["ffn2_w3"], wg["ffn2_w2"],
        "ffn2_bwd")
    big["ffn2_w1"] = _wgrad_rows(da2, n2, "dw_ffn2_w1")
    big["ffn2_w3"] = _wgrad_rows(db2, n2, "dw_ffn2_w3")
    big["ffn2_w2"] = _wgrad_rows(hm2, df2, "dw_ffn2_w2")
    token = reducer.begin("ffn2", {n: big[n] for n in ("ffn2_w1", "ffn2_w3", "ffn2_w2")})
    dmo, dya, dyb, dgate, dhr, dxg, do, small["mix_post_g"], small["b_gate"] = _mix_bwd1(
        dh2, mo, row(sm["mix_post_g"]), gate, ya, yb, xg, hr, w_o, w_lru, w_att, token)
    big["w_o"] = _wgrad_sq(merged, dmo, "dw_w_o").reshape(NSH, D // NSH, D)
    big["w_lru_out"] = _wgrad_sq(yain, dya, "dw_w_lru_out").reshape(NSH, D // NSH, D)
    big["w_attn_out"] = _wgrad_sq(o, dyb, "dw_w_attn_out").reshape(NSH, D // NSH, D)
    token = reducer.advance("ffn2", big["w_attn_out"])
    (dxr, dwa2, dwx2, small["rg_a_b"], small["rg_x_b"], small["lru_lambda"], small["conv_w"], small["conv_b"]) = _rglru_bwd(
        dhr, hr, xc, r, ig, lru_a, lru_s, xr, sm["conv_w"], wa2, wx2, row(sm["lru_lambda"]), token)
    small["rg_a_w"] = _unpair_blocks(dwa2)
    small["rg_x_w"] = _unpair_blocks(dwx2)
    dq, dkp, dvp, dbias_t, ds_rows = _attn_bwd(sink_rows, q, kp, vp, bias_t, mask, do)
    dbias = jnp.transpose(dbias_t.reshape(4, KP, 4, CHUNK)[:, :KB], (0, 2, 3, 1)).reshape(N_HEADS, CHUNK * KB)
    drel_t, dsinks = _bias_bwd(dbias, onehot_t, ds_rows)
    small["attn_sinks"] = dsinks[0:4, 0:4].reshape(N_HEADS)
    small["rel_bias"] = drel_t.T
    t = x.shape[0]
    dproj = jnp.concatenate([dq, dkp[PAD_KEYS:PAD_KEYS + t].astype(BF16), dvp[PAD_KEYS:PAD_KEYS + t].astype(BF16), dxr, dxg],
                            axis=1)
    big["w_in"] = _wgrad_cols(u, dproj, IN_S, "dw_w_in")
    big["w_gate"] = _wgrad_cols(u, dgate, GATE_S, "dw_w_gate")
    token = reducer.begin("mix", {n: big[n] for n in ("w_in", "w_gate", "w_lru_out", "w_attn_out", "w_o")})
    dh1, small["mix_pre_g"] = _mix_bwd2(dproj, dgate, h1, dh2, row(sm["mix_pre_g"]), wg["w_in"], wg["w_gate"], token)
    n1, da1, db1, df1, small["ffn1_post_g"] = _ffn_bwd_acts(
        dh1, x, f1, a1, b1, row(sm["ffn1_pre_g"]), row(sm["ffn1_post_g"]), wg["ffn1_w2"], "ffn1_bwd_acts")
    token = reducer.advance("mix", df1)
    big["ffn1_w1"] = _wgrad_rows(da1, n1, "dw_ffn1_w1", token)
    big["ffn1_w3"] = _wgrad_rows(db1, n1, "dw_ffn1_w3", token)
    big["ffn1_w2"] = _wgrad_rows(hm1, df1, "dw_ffn1_w2", token)
    token = reducer.begin("ffn1", {n: big[n] for n in ("ffn1_w1", "ffn1_w3", "ffn1_w2")})
    dx, small["ffn1_pre_g"] = _ffn_bwd_input(dh1, x, da1, db1, row(sm["ffn1_pre_g"]), wg["ffn1_w1"], wg["ffn1_w3"],
                                             "ffn1_bwd_input", token)
    return sq, dx, big, small


_ANY = pl.BlockSpec(memory_space=pl.ANY)


def _place():
    return lax.axis_index("x"), lax.axis_index("y"), lax.axis_index("c")


def _other_chips(x, y):
    return [(1 - x, y), (x, 1 - y), (1 - x, 1 - y)]


_HBM = pl.BlockSpec(memory_space=pltpu.HBM)
_SEM = pl.BlockSpec(memory_space=pltpu.SEMAPHORE)
_EFFECT = pltpu.SideEffectType.DATAFLOW_SIDE_EFFECTING


def _cast_into_slot(w, chip, name, after=None):
    r, cc = w.shape
    rows = r // 4

    def body(chip_ref, *refs):
        w_ref, o_ref = refs[-2:]
        o_ref[...] = w_ref[...].astype(BF16)

    extra = [] if after is None else [after]
    return pl.pallas_call(
        body, name=name, out_shape=jax.ShapeDtypeStruct((NSH, r, cc), BF16),
        grid_spec=pltpu.PrefetchScalarGridSpec(
            num_scalar_prefetch=1, grid=(4,), in_specs=[_ANY] * len(extra) + [pl.BlockSpec((rows, cc), lambda i, chip: (i, 0))],
            out_specs=pl.BlockSpec((None, rows, cc), lambda i, chip: (chip[0], i, 0))),
        compiler_params=_params("arbitrary"))(chip, *extra, w)


def _piece(ref, slot, c):
    if ref.dtype == F32:
        return ref.at[slot]
    rh = ref.shape[1] // 2
    return ref.at[slot, pl.ds(pl.multiple_of(c * rh, 16), rh), :]


def _gather_start(stages, name):
    flat = [b for stage in stages for b in stage]
    n, ns = len(flat), len(stages)

    def body(*refs):
        ins, sems, token = refs[:n], refs[n:n + 2 * ns], refs[-1]
        x, y, c = _place()
        me = 2 * x + y
        k = 0
        for s, stage in enumerate(stages):
            for i in range(len(stage)):
                for j, (px, py) in enumerate(_other_chips(x, y)):
                    piece = _piece(ins[k], me, c)
                    pltpu.make_async_remote_copy(src_ref=piece, dst_ref=piece, send_sem=sems[2 * s].at[3 * i + j],
                                                 recv_sem=sems[2 * s + 1].at[3 * i + j], device_id=(px, py, c),
                                                 device_id_type=MESH).start()
                k += 1
        token[...] = jnp.zeros_like(token)

    sem_shapes = [pltpu.SemaphoreType.DMA((3 * len(stage),)) for stage in stages for _ in range(2)]
    outs = pl.pallas_call(
        body, name=name, in_specs=[_HBM] * n,
        out_specs=[_SEM] * (2 * ns) + [_HBM] * n + [pl.BlockSpec(memory_space=pltpu.VMEM)],
        out_shape=sem_shapes + [pltpu.HBM(b.shape, b.dtype) for b in flat] + [jax.ShapeDtypeStruct((8, 128), F32)],
        input_output_aliases={i: 2 * ns + i for i in range(n)},
        compiler_params=pltpu.CompilerParams(has_side_effects=_EFFECT),
    )(*[pltpu.with_memory_space_constraint(b, pltpu.HBM) for b in flat])
    sems, bufs, token = outs[:2 * ns], list(outs[2 * ns:2 * ns + n]), outs[-1]
    per_stage, k = [], 0
    for s, stage in enumerate(stages):
        per_stage.append((sems[2 * s], sems[2 * s + 1], bufs[k:k + len(stage)]))
        k += len(stage)
    return per_stage, token


def _gather_wait(send_sems, recv_sems, bufs, after, name):
    n = len(bufs)

    def body(*refs):
        ins, ssem, rsem = refs[:n], refs[n], refs[n + 1]
        x, y, c = _place()
        me = 2 * x + y
        for i in range(n):
            for j, (px, py) in enumerate(_other_chips(x, y)):
                cp = pltpu.make_async_remote_copy(src_ref=_piece(ins[i], me, c), dst_ref=_piece(ins[i], 2 * px + py, c),
                                                  send_sem=ssem.at[3 * i + j], recv_sem=rsem.at[3 * i + j],
                                                  device_id=(px, py, c), device_id_type=MESH)
                cp.wait_send()
                cp.wait_recv()

    return pl.pallas_call(
        body, name=name, in_specs=[_HBM] * n + [_SEM, _SEM, _ANY], out_specs=[_HBM] * n,
        out_shape=[pltpu.HBM(b.shape, b.dtype) for b in bufs], input_output_aliases={i: i for i in range(n)},
        compiler_params=pltpu.CompilerParams(has_side_effects=_EFFECT),
    )(*bufs, send_sems, recv_sems, after)


def _sibling_fill(bufs, name):
    n = len(bufs)

    def body(*refs):
        ins, outs = refs[:n], refs[n:2 * n]
        send_sems, recv_sems = refs[2 * n:]
        x, y, c = _place()
        copies = []
        for i in range(n):
            for j, (px, py) in enumerate(_other_chips(x, y)):
                copies.append(pltpu.make_async_remote_copy(
                    src_ref=_piece(ins[i], 2 * px + py, c), dst_ref=_piece(outs[i], 2 * px + py, c),
                    send_sem=send_sems.at[3 * i + j], recv_sem=recv_sems.at[3 * i + j], device_id=(x, y, 1 - c),
                    device_id_type=MESH))
                copies[-1].start()
        for cp in copies:
            cp.wait()

    return pl.pallas_call(
        body, name=name, in_specs=[_ANY] * n, out_specs=[_ANY] * n,
        out_shape=[jax.ShapeDtypeStruct(b.shape, b.dtype) for b in bufs], input_output_aliases={i: i for i in range(n)},
        scratch_shapes=[pltpu.SemaphoreType.DMA((3 * n,)), pltpu.SemaphoreType.DMA((3 * n,))],
        compiler_params=pltpu.CompilerParams(has_side_effects=True),
    )(*bufs)


def _swap_plan(srcs, lands):
    x, y, c = _place()
    plan = []
    for src, land in zip(srcs, lands):
        rh = src.shape[1] // 2
        plan.append((src.at[:, pl.ds(pl.multiple_of((1 - c) * rh, 16), rh), :], land, (x, y, 1 - c)))
    return plan


def _owners_plan(srcs, lands):
    x, y, c = _place()
    return [(src.at[2 * px + py], land.at[j], (px, py, c))
            for src, land in zip(srcs, lands) for j, (px, py) in enumerate(_other_chips(x, y))]


def _exchange_start(srcs, lands, plan, copies, name):
    n = len(srcs)

    def body(*refs):
        send_sems, recv_sems, token = refs[2 * n], refs[2 * n + 1], refs[-1]
        for k, (src, dst, dev) in enumerate(plan(refs[:n], refs[n:2 * n])):
            pltpu.make_async_remote_copy(src_ref=src, dst_ref=dst, send_sem=send_sems.at[k], recv_sem=recv_sems.at[k],
                                         device_id=dev, device_id_type=MESH).start()
        token[...] = jnp.zeros_like(token)

    both = list(srcs) + list(lands)
    outs = pl.pallas_call(
        body, name=name, in_specs=[_HBM] * (2 * n),
        out_specs=[_SEM, _SEM] + [_HBM] * (2 * n) + [pl.BlockSpec(memory_space=pltpu.VMEM)],
        out_shape=[pltpu.SemaphoreType.DMA((copies,)), pltpu.SemaphoreType.DMA((copies,))]
        + [pltpu.HBM(b.shape, b.dtype) for b in both] + [jax.ShapeDtypeStruct((8, 128), F32)],
        input_output_aliases={i: 2 + i for i in range(2 * n)},
        compiler_params=pltpu.CompilerParams(has_side_effects=_EFFECT),
    )(*[pltpu.with_memory_space_constraint(b, pltpu.HBM) for b in both])
    return (outs[0], outs[1]), list(outs[2:2 + n]), list(outs[2 + n:2 + 2 * n]), outs[-1]


def _exchange_wait(sems, srcs, lands, plan, after, name):
    n = len(srcs)

    def body(*refs):
        send_sems, recv_sems = refs[2 * n], refs[2 * n + 1]
        for k, (src, dst, dev) in enumerate(plan(refs[:n], refs[n:2 * n])):
            cp = pltpu.make_async_remote_copy(src_ref=src, dst_ref=dst, send_sem=send_sems.at[k], recv_sem=recv_sems.at[k],
                                              device_id=dev, device_id_type=MESH)
            cp.wait_send()
            cp.wait_recv()

    both = list(srcs) + list(lands)
    afters = list(after) if isinstance(after, (list, tuple)) else [after]
    outs = pl.pallas_call(
        body, name=name, in_specs=[_HBM] * (2 * n) + [_SEM, _SEM] + [_ANY] * len(afters), out_specs=[_HBM] * (2 * n),
        out_shape=[pltpu.HBM(b.shape, b.dtype) for b in both], input_output_aliases={i: i for i in range(2 * n)},
        compiler_params=pltpu.CompilerParams(has_side_effects=_EFFECT),
    )(*both, sems[0], sems[1], *afters)
    return list(outs[:n]), list(outs[n:])


class _Reducer:
    def __init__(self, where):
        self.state = {}
        self.where = where

    def begin(self, stage, grads):
        names = list(grads)
        full = [grads[n] for n in names]
        lands = [lax.empty((NSH, g.shape[1] // 2, g.shape[2]), g.dtype) for g in full]
        sems, full, lands, token = _exchange_start(full, lands, _swap_plan, len(full), "swap_start_" + stage)
        self.state[stage] = (names, sems, full, lands)
        return token

    def advance(self, stage, after):
        names, sems, full, lands = self.state[stage]
        full, got = _exchange_wait(sems, full, lands, _swap_plan, after, "swap_wait_" + stage)
        sums, own = _chip_sums(full, got, self.where, "chip_sums_" + stage)
        lands = [lax.empty((3,) + s.shape[1:], BF16) for s in sums]
        sems, sent, lands, token = _exchange_start(sums, lands, _owners_plan, 3 * len(sums), "owners_start_" + stage)
        self.state[stage] = (names, own, sems, sent, lands)
        return token

    def finish(self, stage, after):
        names, own, sems, sent, lands = self.state[stage]
        _, got = _exchange_wait(sems, sent, lands, _owners_plan, after, "owners_wait_" + stage)
        return dict(zip(names, _owner_sums(own, got, "owner_sums_" + stage)))


def _chip_sums(gs, gots, where, name):
    n = len(gs)

    def body(where_ref, *refs):
        g_refs, got_refs, hb_refs, own_refs = (refs[k * n:(k + 1) * n] for k in range(4))
        mine = pl.program_id(0) == where_ref[1]
        for g_ref, got_ref, hb_ref, own_ref in zip(g_refs, got_refs, hb_refs, own_refs):
            h = g_ref[...].astype(F32) + got_ref[...].astype(F32)
            hb_ref[...] = h.astype(BF16)

            @pl.when(mine)
            def _():
                own_ref[...] = h

    halves = [(g.shape[1] // 2, g.shape[2]) for g in gs]
    slot = [pl.BlockSpec((None, rh, cc), lambda s, where: (s, 0, 0)) for rh, cc in halves]
    outs = pl.pallas_call(
        body, name=name,
        grid_spec=pltpu.PrefetchScalarGridSpec(
            num_scalar_prefetch=1, grid=(NSH,),
            in_specs=[pl.BlockSpec((None, rh, cc), lambda s, where: (s, where[0], 0)) for rh, cc in halves] + slot,
            out_specs=slot + [pl.BlockSpec((rh, cc), lambda s, where: (0, 0)) for rh, cc in halves]),
        out_shape=[jax.ShapeDtypeStruct((NSH, rh, cc), BF16) for rh, cc in halves]
        + [jax.ShapeDtypeStruct((rh, cc), F32) for rh, cc in halves],
        compiler_params=_params("arbitrary"),
    )(where, *gs, *gots)
    return list(outs[:n]), list(outs[n:])


def _owner_sums(owns, gots, name):
    n = len(owns)

    def body(*refs):
        own_refs, got_refs, o_refs = (refs[k * n:(k + 1) * n] for k in range(3))
        for own_ref, got_ref, o_ref in zip(own_refs, got_refs, o_refs):
            o_ref[...] = ((own_ref[...] + got_ref[0].astype(F32)) + got_ref[1].astype(F32)) + got_ref[2].astype(F32)

    blocks = [(o.shape[0] // 2, o.shape[1]) for o in owns]
    rows = [pl.BlockSpec(b, lambda i: (i, 0)) for b in blocks]
    return pl.pallas_call(
        body, grid=(2,), name=name,
        in_specs=rows + [pl.BlockSpec((3,) + b, lambda i: (0, i, 0)) for b in blocks], out_specs=rows,
        out_shape=[jax.ShapeDtypeStruct(o.shape, F32) for o in owns], compiler_params=_params("arbitrary"),
    )(*owns, *gots)


def _send_halves(halves, name):
    n = len(halves)

    def body(*refs):
        ins, outs = refs[:n], refs[n:2 * n]
        send_sems, recv_sems = refs[2 * n:]
        x, y, c = _place()
        copies = [pltpu.make_async_remote_copy(src_ref=ins[w], dst_ref=outs[w], send_sem=send_sems.at[w], recv_sem=recv_sems.at[w],
                                               device_id=(x, y, 1 - c), device_id_type=MESH) for w in range(n)]
        for cp in copies:
            cp.start()
        for cp in copies:
            cp.wait()

    return pl.pallas_call(
        body, name=name, in_specs=[_ANY] * n, out_specs=[_ANY] * n,
        out_shape=[jax.ShapeDtypeStruct(h.shape, F32) for h in halves],
        scratch_shapes=[pltpu.SemaphoreType.DMA((n,)), pltpu.SemaphoreType.DMA((n,))],
        compiler_params=pltpu.CompilerParams(has_side_effects=True),
    )(*halves)


def _all_reduce_small(part):
    def body(p_ref, o_ref, rbuf, send1, recv1, send2, recv2):
        x, y, c = _place()
        me = 4 * x + 2 * y + c
        peers = []
        for k in range(1, 8):
            px, py, pc = x ^ ((k >> 2) & 1), y ^ ((k >> 1) & 1), c ^ (k & 1)
            peers.append((k, (px, py, pc), 4 * px + 2 * py + pc))

        def rows(d):
            return pl.ds(pl.multiple_of(d * SMALL_SLICE, 8), SMALL_SLICE)

        first = [pltpu.make_async_remote_copy(src_ref=p_ref.at[rows(idx), :], dst_ref=rbuf.at[me], send_sem=send1.at[k],
                                              recv_sem=recv1.at[k], device_id=dev, device_id_type=MESH)
                 for k, dev, idx in peers]
        for cp in first:
            cp.start()
        rbuf[me] = p_ref[rows(me), :]
        for k, dev, idx in peers:
            pltpu.make_async_remote_copy(src_ref=p_ref.at[rows(idx), :], dst_ref=rbuf.at[idx], send_sem=send1.at[k],
                                         recv_sem=recv1.at[k], device_id=dev, device_id_type=MESH).wait_recv()
        acc = rbuf[0]
        for d in range(1, 8):
            acc = acc + rbuf[d]
        o_ref[rows(me), :] = acc
        second = [pltpu.make_async_remote_copy(src_ref=o_ref.at[rows(me), :], dst_ref=o_ref.at[rows(me), :],
                                               send_sem=send2.at[k], recv_sem=recv2.at[k], device_id=dev, device_id_type=MESH)
                  for k, dev, idx in peers]
        for cp in second:
            cp.start()
        for k, dev, idx in peers:
            pltpu.make_async_remote_copy(src_ref=o_ref.at[rows(me), :], dst_ref=o_ref.at[rows(idx), :], send_sem=send2.at[k],
                                         recv_sem=recv2.at[k], device_id=dev, device_id_type=MESH).wait_recv()
        for cp in first + second:
            cp.wait_send()

    return pl.pallas_call(
        body, name="all_reduce_small", in_specs=[_WHOLE], out_specs=_WHOLE,
        out_shape=jax.ShapeDtypeStruct((SMALL_ROWS, 128), F32),
        scratch_shapes=[pltpu.VMEM((8, SMALL_SLICE, 128), F32)] + [pltpu.SemaphoreType.DMA((8,))] * 4,
        compiler_params=pltpu.CompilerParams(has_side_effects=True),
    )(part)


def _adamw_update(w, gv, m, v):
    nm = ADAM_B1 * m + (1.0 - ADAM_B1) * gv
    nv = ADAM_B2 * v + (1.0 - ADAM_B2) * (gv * gv)
    m_hat = nm / (1.0 - ADAM_B1 ** ADAM_STEP)
    v_hat = nv / (1.0 - ADAM_B2 ** ADAM_STEP)
    return -ADAM_LR * (m_hat / (jnp.sqrt(v_hat) + ADAM_EPS) + ADAM_WD * w), nm, nv


def _adamw_small(ws, gs, ms, vs):
    n = len(ws)

    def body(*refs):
        w_refs, g_refs, m_refs, v_refs, d_refs, nm_refs, nv_refs = (refs[k * n:(k + 1) * n] for k in range(7))
        for i in range(n):
            d_refs[i][...], nm_refs[i][...], nv_refs[i][...] = _adamw_update(
                w_refs[i][...], g_refs[i][...], m_refs[i][...], v_refs[i][...])

    out = [jax.ShapeDtypeStruct(w.shape, F32) for w in ws]
    outs = pl.pallas_call(body, in_specs=[_WHOLE] * (4 * n), out_specs=[_WHOLE] * (3 * n), out_shape=out * 3,
                          name="adamw_small", compiler_params=_params())(*ws, *gs, *ms, *vs)
    return outs[:n], outs[n:2 * n], outs[2 * n:]


def _adamw_halves(ws, mines, theirs, ms, vs, name):
    n = len(ws)
    steps = 2

    def body(*refs):
        w_refs, mine_refs, theirs_refs, m_refs, v_refs, g_refs, d_refs, nm_refs, nv_refs = (
            refs[k * n:(k + 1) * n] for k in range(9))
        is_mine = pl.program_id(0) == lax.axis_index("c")
        for i in range(n):
            gv = jnp.where(is_mine, mine_refs[i][...], theirs_refs[i][...])
            g_refs[i][...] = gv
            d_refs[i][...], nm_refs[i][...], nv_refs[i][...] = _adamw_update(w_refs[i][...], gv, m_refs[i][...], v_refs[i][...])

    blocks = [(h.shape[0] // steps, h.shape[1]) for h in mines]
    whole = [pl.BlockSpec(b, lambda h, i: (steps * h + i, 0)) for b in blocks]
    half = [pl.BlockSpec(b, lambda h, i: (i, 0)) for b in blocks]
    out = [jax.ShapeDtypeStruct(w.shape, F32) for w in ws]
    outs = pl.pallas_call(body, grid=(2, steps), in_specs=whole + half + half + whole + whole, out_specs=whole * 4,
                          out_shape=out * 4, name=name, compiler_params=_params("arbitrary", "arbitrary"),
                          )(*ws, *mines, *theirs, *ms, *vs)
    return [tuple(outs[k * n + i] for k in range(4)) for i in range(n)]


SMALL_USED = sum(size for _, size in SMALL) // 128


def _pack_small(vals, tail=None):
    parts = []
    for name, size in SMALL:
        flat = vals[name].reshape(-1).astype(F32)
        parts.append(jnp.pad(flat, (0, size - flat.shape[0])))
    if tail is not None:
        parts.append(tail.reshape(128))
    flat = jnp.concatenate(parts)
    return jnp.pad(flat, (0, SMALL_ROWS * 128 - flat.shape[0])).reshape(SMALL_ROWS, 128)


def _unpack_small(packed, shapes):
    flat = packed.reshape(-1)
    out, off = {}, 0
    for name, size in SMALL:
        n = math.prod(shapes[name])
        out[name] = flat[off:off + n].reshape(shapes[name])
        off += size
    return out


def kernel(x, ffn1_pre_g, ffn1_w1, ffn1_w3, ffn1_w2, ffn1_post_g, mix_pre_g, w_in, conv_w, conv_b, rg_a_w, rg_a_b, rg_x_w, rg_x_b, lru_lambda, w_lru_out, attn_sinks, rel_bias, w_attn_out, w_gate, b_gate, w_o, mix_post_g, ffn2_pre_g, ffn2_w1, ffn2_w3, ffn2_w2, ffn2_post_g, loss_target, m_ffn1_pre_g, m_ffn1_w1, m_ffn1_w3, m_ffn1_w2, m_ffn1_post_g, m_mix_pre_g, m_w_in, m_conv_w, m_conv_b, m_rg_a_w, m_rg_a_b, m_rg_x_w, m_rg_x_b, m_lru_lambda, m_w_lru_out, m_attn_sinks, m_rel_bias, m_w_attn_out, m_w_gate, m_b_gate, m_w_o, m_mix_post_g, m_ffn2_pre_g, m_ffn2_w1, m_ffn2_w3, m_ffn2_w2, m_ffn2_post_g, v_ffn1_pre_g, v_ffn1_w1, v_ffn1_w3, v_ffn1_w2, v_ffn1_post_g, v_mix_pre_g, v_w_in, v_conv_w, v_conv_b, v_rg_a_w, v_rg_a_b, v_rg_x_w, v_rg_x_b, v_lru_lambda, v_w_lru_out, v_attn_sinks, v_rel_bias, v_w_attn_out, v_w_gate, v_b_gate, v_w_o, v_mix_post_g, v_ffn2_pre_g, v_ffn2_w1, v_ffn2_w3, v_ffn2_w2, v_ffn2_post_g):
    given = dict(locals())
    chip = 2 * lax.axis_index("x") + lax.axis_index("y")
    transposed = ("ffn1_w1", "ffn1_w3", "ffn2_w1", "ffn2_w3")

    def shard(name, moment=""):
        w = given[moment + name][0]
        return w.T if name in transposed else w

    def unshard(name, w):
        return (w.T if name in transposed else w)[None]

    def only_my_columns(a):
        parts = a.reshape(1, 4, NSH, D // NSH)
        return sum(jnp.where(chip == s, parts[:, :, s], 0.0) for s in range(NSH))

    chip_arr = jnp.reshape(chip, (1,)).astype(jnp.int32)
    stage_names = {"ffn1": ["ffn1_w1", "ffn1_w3", "ffn1_w2", "conv_w"],
                   "mix": ["w_in", "w_gate", "w_lru_out", "w_attn_out", "w_o"],
                   "ffn2": ["ffn2_w1", "ffn2_w3", "ffn2_w2"]}
    in_flight, started = {}, None
    for stage, names in stage_names.items():
        bufs = [jnp.where(lax.broadcasted_iota(jnp.int32, (NSH, 4, D // NSH), 0) == chip, given[n], 0.0) if n == "conv_w"
                else _cast_into_slot(shard(n), chip_arr, "cast_" + n, started) for n in names]
        (in_flight[stage],), started = _gather_start([bufs], "gather_start_" + stage)
    all_started = started

    def weights(stage, after):
        names = stage_names[stage]
        send_sems, recv_sems, landing = in_flight[stage]
        if stage == "ffn1":
            after = all_started
        landed = _gather_wait(send_sems, recv_sems, landing, after, "gather_wait_" + stage)
        halves = [b for b in landed if b.dtype == BF16]
        out = dict(zip([n for n, b in zip(names, landed) if b.dtype == BF16], _sibling_fill(halves, "sibling_fill_" + stage)))
        if "conv_w" in names:
            out["conv_w"] = jnp.transpose(landed[names.index("conv_w")], (1, 0, 2)).reshape(4, D)
        return out

    small_shapes = {n: given[n].shape for n, _ in SMALL}
    small_shapes["conv_w"] = (1, 4, D)
    sm = {n: (given[n][0] if given[n].shape[0] == 1 and n != "rel_bias" else given[n]) for n, _ in SMALL if n != "conv_w"}

    reducer = _Reducer(jnp.stack([lax.axis_index("c"), chip]).astype(jnp.int32))
    sq, dx, _, small = _local_step(x[0], loss_target[0], weights, sm, reducer)

    reducer.advance("ffn1", dx)
    reduced_small = _all_reduce_small(_pack_small(small, tail=sq))
    loss = reduced_small[SMALL_USED, 0] * (0.5 / D)
    small_g = _unpack_small(reduced_small, small_shapes)
    grads, delta, new_m, new_v = {}, {}, {}, {}
    after = [reduced_small]
    for stage in ("ffn2", "mix", "ffn1"):
        halves = reducer.finish(stage, after)
        from_sibling = _send_halves(list(halves.values()), "send_halves_" + stage)
        names = list(halves)
        updated = _adamw_halves([shard(n) for n in names], list(halves.values()), from_sibling,
                                [shard(n, "m_") for n in names], [shard(n, "v_") for n in names], "adamw_" + stage)
        for n, results in zip(names, updated):
            grads[n], delta[n], new_m[n], new_v[n] = (unshard(n, r) for r in results)
        after.append(new_v[names[-1]])

    small_g["conv_w"] = only_my_columns(small_g["conv_w"])
    names = [n for n, _ in SMALL]
    flat2d = lambda a: a.reshape(-1, a.shape[-1])
    outs = _adamw_small(*[[flat2d(given[pre + n]) if pre != "g" else flat2d(small_g[n]) for n in names]
                          for pre in ("", "g", "m_", "v_")])
    for dst, arrs in zip((delta, new_m, new_v), outs):
        dst.update({n: a.reshape(given[n].shape) for n, a in zip(names, arrs)})
    grads.update(small_g)
    return (loss, dx[None], *[grads[n] for n in WEIGHTS], *[delta[n] for n in WEIGHTS], *[new_m[n] for n in WEIGHTS],
            *[new_v[n] for n in WEIGHTS])
```

```python
import functools
import math

import jax
import jax.numpy as jnp
from jax import lax
from jax.experimental import pallas as pl
from jax.experimental.pallas import tpu as pltpu

F32, BF16 = jnp.float32, jnp.bfloat16
D = 1024
NSH = 4
FF_S = 704
IN_S = 896
GATE_S = 512
KV_W = 256
CHUNK = 64
KB = 192
N_HEADS = 16
HEAD_DIM = 64
N_BUCKETS = 32
KP = 192
PAD_KEYS = 128
RMS_EPS = 1e-6
NEG_INF = -1e30
LRU_C = 8.0
TM = 512
TM_SCAN = 256
VMEM_LIMIT = 56 * 1024 * 1024
ADAM_LR, ADAM_B1, ADAM_B2, ADAM_EPS, ADAM_WD, ADAM_STEP = 0.001, 0.9, 0.999, 1e-08, 0.01, 10
SMALL_ROWS = 1216
SMALL_SLICE = SMALL_ROWS // 8
MESH = pl.DeviceIdType.MESH

BIG = ["ffn1_w1", "ffn1_w3", "ffn1_w2", "w_in", "w_lru_out", "w_attn_out", "w_gate", "w_o", "ffn2_w1", "ffn2_w3", "ffn2_w2"]
SMALL = [("ffn1_pre_g", 1024), ("ffn1_post_g", 1024), ("mix_pre_g", 1024), ("conv_w", 4096), ("conv_b", 1024),
         ("rg_a_w", 65536), ("rg_a_b", 1024), ("rg_x_w", 65536), ("rg_x_b", 1024), ("lru_lambda", 1024),
         ("attn_sinks", 1024), ("rel_bias", 1024), ("b_gate", 2048), ("mix_post_g", 1024), ("ffn2_pre_g", 1024),
         ("ffn2_post_g", 1024)]
WEIGHTS = ["ffn1_pre_g", "ffn1_w1", "ffn1_w3", "ffn1_w2", "ffn1_post_g", "mix_pre_g", "w_in", "conv_w", "conv_b", "rg_a_w",
           "rg_a_b", "rg_x_w", "rg_x_b", "lru_lambda", "w_lru_out", "attn_sinks", "rel_bias", "w_attn_out", "w_gate", "b_gate",
           "w_o", "mix_post_g", "ffn2_pre_g", "ffn2_w1", "ffn2_w3", "ffn2_w2", "ffn2_post_g"]


def _params(*sem):
    return pltpu.CompilerParams(dimension_semantics=sem or None, vmem_limit_bytes=VMEM_LIMIT)


def _nn(a, b):
    return jnp.dot(a, b, preferred_element_type=F32)


def _nt(a, b):
    return lax.dot_general(a, b, (((1,), (1,)), ((), ())), preferred_element_type=F32)


def _tn(a, b):
    return lax.dot_general(a, b, (((0,), (0,)), ((), ())), preferred_element_type=F32)


def _rms(x, g):
    rstd = lax.rsqrt(jnp.mean(x * x, axis=-1, keepdims=True) + RMS_EPS)
    return (x * rstd) * g


def _rms_bwd(dout, x, g):
    rstd = lax.rsqrt(jnp.mean(x * x, axis=-1, keepdims=True) + RMS_EPS)
    xhat = x * rstd
    dg = jnp.sum(dout * xhat, axis=0, keepdims=True)
    dxhat = dout * g
    dx = rstd * (dxhat - xhat * jnp.mean(dxhat * xhat, axis=-1, keepdims=True))
    return dx, dg


_GELU_K = math.sqrt(2.0 / math.pi)


def _gelu(x):
    return x * (0.5 * (1.0 + jnp.tanh(_GELU_K * (x + 0.044715 * (x * x * x)))))


def _gelu_and_grad(x):
    x2 = x * x
    t = jnp.tanh(_GELU_K * (x + 0.044715 * (x2 * x)))
    cdf = 0.5 * (1.0 + t)
    return x * cdf, cdf + x * (0.5 * (1.0 - t * t) * (_GELU_K * (1.0 + 3.0 * 0.044715 * x2)))


def _softplus_neg(lam):
    z = -lam
    u = jnp.exp(-jnp.abs(z))
    w = 1.0 + u
    log1p_u = jnp.where(w == 1.0, u, jnp.log(w) * (u / (w - 1.0)))
    return jnp.maximum(z, 0.0) + log1p_u


def _lru_coeffs(r, sp):
    log_a = (-LRU_C * r) * sp
    a = jnp.exp(log_a)
    t = jnp.tanh(log_a)
    s = jnp.sqrt(-2.0 * t / (1.0 - t))
    return a, s


def _row_spec(tm, width):
    return pl.BlockSpec((tm, width), lambda i: (i, 0))


def _vec_spec(width):
    return pl.BlockSpec((1, width), lambda i: (0, 0))


_WHOLE = pl.BlockSpec(memory_space=pltpu.VMEM)


def _tile(t, tm=TM):
    return min(tm, t)


def _ffn_fwd(x, gpre, w1g, w3g, w2g, gpost, name, target=None):
    t = x.shape[0]
    tm = _tile(t)
    last = target is not None

    def body(x_ref, gpre_ref, w1_ref, w3_ref, w2_ref, gpost_ref, *refs):
        t_ref, (h_ref, a_ref, b_ref, hm_ref, f_ref), l_ref = (refs[0] if last else None), refs[last:last + 5], refs[-1]
        xv = x_ref[...]
        nb = _rms(xv, gpre_ref[...]).astype(BF16)
        f = jnp.zeros((tm, D), F32)
        for s in range(NSH):
            a = _nt(nb, w1_ref[s])
            b = _nt(nb, w3_ref[s])
            hmb = ((a * jax.nn.sigmoid(a)) * b).astype(BF16)
            a_ref[s] = a.astype(BF16)
            b_ref[s] = b.astype(BF16)
            hm_ref[s] = hmb
            f = f + _nn(hmb, w2_ref[s])
        f_ref[...] = f
        h = xv + 0.5 * _rms(f, gpost_ref[...])
        if last:
            @pl.when(pl.program_id(0) == 0)
            def _():
                l_ref[...] = jnp.zeros_like(l_ref)

            e = h - t_ref[...]
            h_ref[...] = e * (1.0 / D)
            l_ref[...] += jnp.sum(jnp.sum(e * e, axis=0, keepdims=True), axis=1, keepdims=True)
        else:
            h_ref[...] = h

    sh = pl.BlockSpec((NSH, tm, FF_S), lambda i: (0, i, 0))
    act = jax.ShapeDtypeStruct((NSH, t, FF_S), BF16)
    return pl.pallas_call(
        body, grid=(t // tm,), name=name,
        in_specs=[_row_spec(tm, D), _vec_spec(D), _WHOLE, _WHOLE, _WHOLE, _vec_spec(D)] + [_row_spec(tm, D)] * last,
        out_specs=[_row_spec(tm, D), sh, sh, sh, _row_spec(tm, D)] + [pl.BlockSpec((1, 128), lambda i: (0, 0))] * last,
        out_shape=[jax.ShapeDtypeStruct((t, D), F32), act, act, act, jax.ShapeDtypeStruct((t, D), F32)]
        + [jax.ShapeDtypeStruct((1, 128), F32)] * last,
        compiler_params=_params("arbitrary"),
    )(x, gpre, w1g, w3g, w2g, gpost, *([target] if last else []))


def _mix_proj(h1, gmix, w_in_g, w_gate_g, b_gate):
    t = h1.shape[0]
    tm = _tile(t)

    def body(h_ref, g_ref, win_ref, wg_ref, bg_ref, u_ref, q_ref, k_ref, v_ref, xr_ref, xg_ref, gate_ref):
        ub = _rms(h_ref[...], g_ref[...]).astype(BF16)
        u_ref[...] = ub
        p0 = _nn(ub, win_ref[0])
        q_ref[:, 0:896] = p0.astype(BF16)
        p1 = _nn(ub, win_ref[1])
        q_ref[:, 896:1024] = p1[:, 0:128].astype(BF16)
        k_ref[...] = p1[:, 128:384].astype(BF16)
        v_ref[...] = p1[:, 384:640].astype(BF16)
        xr_ref[:, 0:256] = p1[:, 640:896]
        p2 = _nn(ub, win_ref[2])
        xr_ref[:, 256:1024] = p2[:, 0:768]
        xg_ref[:, 0:128] = p2[:, 768:896]
        xg_ref[:, 128:1024] = _nn(ub, win_ref[3])
        for s in range(NSH):
            sl = slice(s * GATE_S, (s + 1) * GATE_S)
            gate_ref[:, sl] = jax.nn.sigmoid(_nn(ub, wg_ref[s]) + bg_ref[:, sl])

    return pl.pallas_call(
        body, grid=(t // tm,), name="mix_proj",
        in_specs=[_row_spec(tm, D), _vec_spec(D), _WHOLE, _WHOLE, _vec_spec(2 * D)],
        out_specs=[_row_spec(tm, D), _row_spec(tm, D), _row_spec(tm, KV_W), _row_spec(tm, KV_W), _row_spec(tm, D),
                   _row_spec(tm, D), _row_spec(tm, 2 * D)],
        out_shape=[jax.ShapeDtypeStruct((t, D), BF16), jax.ShapeDtypeStruct((t, D), BF16),
                   jax.ShapeDtypeStruct((t, KV_W), BF16), jax.ShapeDtypeStruct((t, KV_W), BF16),
                   jax.ShapeDtypeStruct((t, D), F32), jax.ShapeDtypeStruct((t, D), F32),
                   jax.ShapeDtypeStruct((t, 2 * D), F32)],
        compiler_params=_params("arbitrary"),
    )(h1, gmix, w_in_g, w_gate_g, b_gate)


def _rglru_fwd(xr, xg, conv_w, conv_b, wa2, ba, wx2, bx, lam, after=None):
    t = xr.shape[0]
    tm = _tile(t, TM_SCAN)
    nb8 = tm // 8

    def body(xr_ref, xrp_ref, xg_ref, cw_ref, cb_ref, wa_ref, ba_ref, wx_ref, bx_ref, lam_ref,
             hr_ref, yain_ref, xc_ref, r_ref, ig_ref, a_sc, s_ref, ext, h_sc):
        i = pl.program_id(0)

        @pl.when(i == 0)
        def _():
            h_sc[...] = jnp.zeros_like(h_sc)

        ext[0:8, :] = jnp.where(i == 0, 0.0, xrp_ref[...])
        ext[8:8 + tm, :] = xr_ref[...]
        xc = jnp.broadcast_to(cb_ref[...], (tm, D))
        for tap in range(4):
            xc = xc + ext[pl.ds(5 + tap, tm), :] * cw_ref[tap:tap + 1, :]
        xc_ref[...] = xc
        xcb = xc.astype(BF16)
        for p in range(8):
            sl = slice(p * 128, (p + 1) * 128)
            r_ref[:, sl] = jax.nn.sigmoid(_nn(xcb[:, sl], wa_ref[p]) + ba_ref[:, sl])
            ig_ref[:, sl] = jax.nn.sigmoid(_nn(xcb[:, sl], wx_ref[p]) + bx_ref[:, sl])
        a, s = _lru_coeffs(r_ref[...], _softplus_neg(lam_ref[...]))
        a_sc[...] = a
        s_ref[...] = s
        hr_ref[...] = s * (ig_ref[...] * xc)

        def blk(j, h):
            st = pl.multiple_of(j * 8, 8)
            a8 = a_sc[pl.ds(st, 8), :]
            u8 = hr_ref[pl.ds(st, 8), :]
            rows = []
            for k in range(8):
                h = a8[k:k + 1, :] * h + u8[k:k + 1, :]
                rows.append(h)
            hr_ref[pl.ds(st, 8), :] = jnp.concatenate(rows, axis=0)
            return h

        h_sc[0:1, :] = lax.fori_loop(0, nb8, blk, h_sc[0:1, :])
        yain_ref[...] = (hr_ref[...] * _gelu(xg_ref[...])).astype(BF16)

    prev = pl.BlockSpec((8, D), lambda i: (jnp.maximum(i * nb8 - 1, 0), 0))
    full = lambda shape: pl.BlockSpec(shape, lambda i: tuple(0 for _ in shape))
    f32 = jax.ShapeDtypeStruct((t, D), F32)
    body, specs, operands = _behind(body, after)
    return pl.pallas_call(
        body, grid=(t // tm,), name="rglru_fwd",
        in_specs=specs + [_row_spec(tm, D), prev, _row_spec(tm, D), full((4, D)), _vec_spec(D), full((8, 128, 128)),
                          _vec_spec(D), full((8, 128, 128)), _vec_spec(D), _vec_spec(D)],
        out_specs=[_row_spec(tm, D)] * 7,
        out_shape=[f32, jax.ShapeDtypeStruct((t, D), BF16), f32, f32, f32, f32, f32],
        scratch_shapes=[pltpu.VMEM((tm + 8, D), F32), pltpu.VMEM((8, D), F32)],
        compiler_params=_params("arbitrary"),
    )(*operands, xr, xr, xg, conv_w, conv_b, wa2, ba, wx2, bx, lam)


def _bias_fwd(table_t, onehot_t):
    def body(t_ref, e_ref, o_ref):
        o_ref[...] = jnp.dot(t_ref[...], e_ref[...], preferred_element_type=F32, precision=lax.Precision.HIGHEST)

    return pl.pallas_call(body, out_shape=jax.ShapeDtypeStruct((N_HEADS, CHUNK * KB), F32), name="bias_fwd",
                          compiler_params=_params())(table_t, onehot_t)


def _bias_bwd(dbias_flat, onehot_t, ds_rows):
    def body(d_ref, e_ref, s_ref, o_ref, so_ref):
        o_ref[...] = lax.dot_general(d_ref[...], e_ref[...], (((1,), (1,)), ((), ())), preferred_element_type=F32,
                                     precision=lax.Precision.HIGHEST)
        so_ref[...] = jnp.zeros_like(so_ref)
        for r in range(4):
            so_ref[:, r:r + 1] = jnp.sum(s_ref[:, r * CHUNK:(r + 1) * CHUNK], axis=1, keepdims=True)

    return pl.pallas_call(body, out_shape=[jax.ShapeDtypeStruct((N_HEADS, N_BUCKETS), F32), jax.ShapeDtypeStruct((8, 128), F32)],
                          name="bias_bwd", compiler_params=_params())(dbias_flat, onehot_t, ds_rows)


def _stack_heads(q):
    return jnp.concatenate(
        [jnp.concatenate([q[:, (4 * g + r) * HEAD_DIM:(4 * g + r + 1) * HEAD_DIM] for g in range(4)], axis=1)
         for r in range(4)], axis=0)


def _unstack_heads(o):
    return jnp.concatenate([o[r * CHUNK:(r + 1) * CHUNK, g * HEAD_DIM:(g + 1) * HEAD_DIM] for g in range(4) for r in range(4)],
                           axis=1)


def _block_diag(w, mask):
    return jnp.concatenate([w] * 4, axis=0) * mask


def _group_softmax(qk, bias_g, sink, valid):
    s = qk * (HEAD_DIM ** -0.5) + bias_g
    s = jnp.where(valid, s, NEG_INF)
    m = jnp.maximum(jnp.max(s, axis=0, keepdims=True), sink)
    e = jnp.exp(s - m)
    es = jnp.exp(sink - m)
    inv = 1.0 / (jnp.sum(e, axis=0, keepdims=True) + es)
    return e * inv, es * inv


def _attn_fwd(sink_rows, q, kp, vp, bias_t, mask, after=None):
    t = q.shape[0]
    per_step = 4

    def body(sink_ref, q_ref, kp_ref, vp_ref, bias_ref, mask_ref, o_ref):
        owns = [mask_ref[g * KP:(g + 1) * KP, :] for g in range(4)]
        for k in range(per_step):
            c = pl.program_id(0) * per_step + k
            rows = slice(k * CHUNK, (k + 1) * CHUNK)
            st = pl.multiple_of(c * CHUNK, CHUNK)
            kw = kp_ref[pl.ds(st, KP), :]
            vw = vp_ref[pl.ds(st, KP), :]
            q_all = _stack_heads(q_ref[rows, :])
            valid = lax.broadcasted_iota(jnp.int32, (KP, 1), 0) + c * CHUNK >= PAD_KEYS
            scores = [_nt(kw * owns[g], q_all) for g in range(4)]
            ps = [_group_softmax(scores[g], bias_ref[g * KP:(g + 1) * KP, :], sink_ref[g:g + 1, :], valid)[0]
                  for g in range(4)]
            o_all = sum(_tn(ps[g].astype(BF16), vw * owns[g]) for g in range(4))
            o_ref[rows, :] = _unstack_heads(o_all).astype(BF16)

    body, specs, operands = _behind(body, after)
    return pl.pallas_call(
        body, grid=(t // (per_step * CHUNK),), name="attn_fwd",
        in_specs=specs + [_WHOLE, _row_spec(per_step * CHUNK, D), _WHOLE, _WHOLE, _WHOLE, _WHOLE],
        out_specs=_row_spec(per_step * CHUNK, D),
        out_shape=jax.ShapeDtypeStruct((t, D), BF16),
        compiler_params=_params("arbitrary"),
    )(*operands, sink_rows, q, kp, vp, bias_t, mask)


def _merge_fwd(yain, o, gate, h1, w_lru, w_att, w_o, gpost):
    t = h1.shape[0]
    tm = _tile(t)

    def body(ya_ref, o_ref, g_ref, h_ref, wl_ref, wa_ref, wo_ref, gp_ref, h2_ref, mo_ref, mg_ref, ya_out, yb_out):
        ya = _nn(ya_ref[...], wl_ref[...])
        yb = _nn(o_ref[...], wa_ref[...])
        g0 = g_ref[:, 0:D]
        g1 = g_ref[:, D:2 * D]
        mg = (g0 * ya + g1 * yb).astype(BF16)
        mo = _nn(mg, wo_ref[...])
        ya_out[...] = (ya * (g0 * (1.0 - g0))).astype(BF16)
        yb_out[...] = (yb * (g1 * (1.0 - g1))).astype(BF16)
        mg_ref[...] = mg
        mo_ref[...] = mo
        h2_ref[...] = h_ref[...] + _rms(mo, gp_ref[...])

    f32 = jax.ShapeDtypeStruct((t, D), F32)
    b16 = jax.ShapeDtypeStruct((t, D), BF16)
    return pl.pallas_call(
        body, grid=(t // tm,), name="merge_fwd",
        in_specs=[_row_spec(tm, D), _row_spec(tm, D), _row_spec(tm, 2 * D), _row_spec(tm, D), _WHOLE, _WHOLE, _WHOLE,
                  _vec_spec(D)],
        out_specs=[_row_spec(tm, D)] * 5,
        out_shape=[f32, f32, b16, b16, b16],
        compiler_params=_params("arbitrary"),
    )(yain, o, gate, h1, w_lru, w_att, w_o, gpost)


def _ffn_bwd(dh, x, f, a, b, gpre, gpost, w1g, w3g, w2g, name):
    t = x.shape[0]
    tm = _tile(t, TM_SCAN)

    def body(dh_ref, x_ref, f_ref, a_ref, b_ref, gpre_ref, gpost_ref, w1_ref, w3_ref, w2_ref,
             dx_ref, n_ref, da_ref, db_ref, df_ref, dgpre_ref, dgpost_ref):
        @pl.when(pl.program_id(0) == 0)
        def _():
            dgpre_ref[...] = jnp.zeros_like(dgpre_ref)
            dgpost_ref[...] = jnp.zeros_like(dgpost_ref)

        dhv = dh_ref[...]
        xv = x_ref[...]
        df, dgp = _rms_bwd(0.5 * dhv, f_ref[...], gpost_ref[...])
        dgpost_ref[...] += dgp
        dfb = df.astype(BF16)
        df_ref[...] = dfb
        n_ref[...] = _rms(xv, gpre_ref[...]).astype(BF16)
        dn = jnp.zeros((tm, D), F32)
        for s in range(NSH):
            av = a_ref[s].astype(F32)
            bv = b_ref[s].astype(F32)
            sg = jax.nn.sigmoid(av)
            dhm = _nt(dfb, w2_ref[s])
            dab = (dhm * bv * (sg * (1.0 + av * (1.0 - sg)))).astype(BF16)
            dbb = (dhm * (av * sg)).astype(BF16)
            da_ref[s] = dab
            db_ref[s] = dbb
            dn = dn + _nn(dab, w1_ref[s]) + _nn(dbb, w3_ref[s])
        dxn, dg = _rms_bwd(dn, xv, gpre_ref[...])
        dgpre_ref[...] += dg
        dx_ref[...] = dhv + dxn

    sh = pl.BlockSpec((NSH, tm, FF_S), lambda i: (0, i, 0))
    act = jax.ShapeDtypeStruct((NSH, t, FF_S), BF16)
    vec = jax.ShapeDtypeStruct((1, D), F32)
    return pl.pallas_call(
        body, grid=(t // tm,), name=name,
        in_specs=[_row_spec(tm, D), _row_spec(tm, D), _row_spec(tm, D), sh, sh, _vec_spec(D), _vec_spec(D), _WHOLE, _WHOLE,
                  _WHOLE],
        out_specs=[_row_spec(tm, D), _row_spec(tm, D), sh, sh, _row_spec(tm, D), _vec_spec(D), _vec_spec(D)],
        out_shape=[jax.ShapeDtypeStruct((t, D), F32), jax.ShapeDtypeStruct((t, D), BF16), act, act,
                   jax.ShapeDtypeStruct((t, D), BF16), vec, vec],
        compiler_params=_params("arbitrary"),
    )(dh, x, f, a, b, gpre, gpost, w1g, w3g, w2g)


def _behind(body, after):
    if after is None:
        return body, [], []

    def ordered(_, *refs):
        body(*refs)

    return ordered, [_ANY], [after]


def _ffn_bwd_acts(dh, x, f, a, b, gpre, gpost, w2g, name):
    t = x.shape[0]
    tm = _tile(t)

    def body(dh_ref, x_ref, f_ref, a_ref, b_ref, gpre_ref, gpost_ref, w2_ref, n_ref, da_ref, db_ref, df_ref, dgpost_ref):
        @pl.when(pl.program_id(0) == 0)
        def _():
            dgpost_ref[...] = jnp.zeros_like(dgpost_ref)

        df, dgp = _rms_bwd(0.5 * dh_ref[...], f_ref[...], gpost_ref[...])
        dgpost_ref[...] += dgp
        dfb = df.astype(BF16)
        df_ref[...] = dfb
        n_ref[...] = _rms(x_ref[...], gpre_ref[...]).astype(BF16)
        for s in range(NSH):
            av = a_ref[s].astype(F32)
            bv = b_ref[s].astype(F32)
            sg = jax.nn.sigmoid(av)
            dhm = _nt(dfb, w2_ref[s])
            da_ref[s] = (dhm * bv * (sg * (1.0 + av * (1.0 - sg)))).astype(BF16)
            db_ref[s] = (dhm * (av * sg)).astype(BF16)

    sh = pl.BlockSpec((NSH, tm, FF_S), lambda i: (0, i, 0))
    act = jax.ShapeDtypeStruct((NSH, t, FF_S), BF16)
    b16 = jax.ShapeDtypeStruct((t, D), BF16)
    return pl.pallas_call(
        body, grid=(t // tm,), name=name,
        in_specs=[_row_spec(tm, D), _row_spec(tm, D), _row_spec(tm, D), sh, sh, _vec_spec(D), _vec_spec(D), _WHOLE],
        out_specs=[_row_spec(tm, D), sh, sh, _row_spec(tm, D), _vec_spec(D)],
        out_shape=[b16, act, act, b16, jax.ShapeDtypeStruct((1, D), F32)],
        compiler_params=_params("arbitrary"),
    )(dh, x, f, a, b, gpre, gpost, w2g)


def _ffn_bwd_input(dh, x, da, db, gpre, w1g, w3g, name, after):
    t = x.shape[0]
    tm = _tile(t)

    def body(dh_ref, x_ref, da_ref, db_ref, gpre_ref, w1_ref, w3_ref, dx_ref, dgpre_ref):
        @pl.when(pl.program_id(0) == 0)
        def _():
            dgpre_ref[...] = jnp.zeros_like(dgpre_ref)

        dn = jnp.zeros((tm, D), F32)
        for s in range(NSH):
            dn = dn + _nn(da_ref[s], w1_ref[s]) + _nn(db_ref[s], w3_ref[s])
        dxn, dg = _rms_bwd(dn, x_ref[...], gpre_ref[...])
        dgpre_ref[...] += dg
        dx_ref[...] = dh_ref[...] + dxn

    sh = pl.BlockSpec((NSH, tm, FF_S), lambda i: (0, i, 0))
    body, specs, operands = _behind(body, after)
    return pl.pallas_call(
        body, grid=(t // tm,), name=name,
        in_specs=specs + [_row_spec(tm, D), _row_spec(tm, D), sh, sh, _vec_spec(D), _WHOLE, _WHOLE],
        out_specs=[_row_spec(tm, D), _vec_spec(D)],
        out_shape=[jax.ShapeDtypeStruct((t, D), F32), jax.ShapeDtypeStruct((1, D), F32)],
        compiler_params=_params("arbitrary"),
    )(*operands, dh, x, da, db, gpre, w1g, w3g)


def _wgrad(a, b, a_spec, b_spec, out_spec, out_shape, grid, name, after=None):
    def body(a_ref, b_ref, o_ref):
        o_ref[...] = _tn(a_ref[...], b_ref[...]).astype(BF16)

    body, specs, operands = _behind(body, after)
    return pl.pallas_call(body, grid=grid, name=name, in_specs=specs + [a_spec, b_spec], out_specs=out_spec,
                          out_shape=jax.ShapeDtypeStruct(out_shape, BF16),
                          compiler_params=_params(*("arbitrary",) * len(grid)))(*operands, a, b)


def _wgrad_cols(act, dsh, width, name, after=None):
    t = act.shape[0]
    if dsh.ndim == 3:
        b_spec = pl.BlockSpec((None, t, width), lambda s, k: (s, 0, 0))
    else:
        b_spec = pl.BlockSpec((t, width), lambda s, k: (0, s))
    return _wgrad(act, dsh, pl.BlockSpec((t, 512), lambda s, k: (0, k)), b_spec,
                  pl.BlockSpec((None, 512, width), lambda s, k: (s, k, 0)), (NSH, D, width), (NSH, 2), name, after)


def _wgrad_rows(hm, df, name, after=None):
    t = df.shape[0]
    return _wgrad(hm, df, pl.BlockSpec((None, t, FF_S), lambda s: (s, 0, 0)), pl.BlockSpec((t, D), lambda s: (0, 0)),
                  pl.BlockSpec((None, FF_S, D), lambda s: (s, 0, 0)), (NSH, FF_S, D), (NSH,), name, after)


def _wgrad_sq(a, b, name, after=None):
    t = a.shape[0]
    return _wgrad(a, b, pl.BlockSpec((t, 512), lambda i, j: (0, i)), pl.BlockSpec((t, 512), lambda i, j: (0, j)),
                  pl.BlockSpec((512, 512), lambda i, j: (i, j)), (D, D), (2, 2), name, after)


def _mix_bwd1(dh2, mo, gpost, gate, ya, yb, xg, hr, w_o, w_lru, w_att, after):
    t = dh2.shape[0]
    tm = _tile(t, TM_SCAN)

    def body(dh_ref, mo_ref, gp_ref, g_ref, ya_ref, yb_ref, xg_ref, hr_ref, wo_ref, wl_ref, wa_ref,
             dmo_ref, dya_ref, dyb_ref, dgate_ref, dhr_ref, dxg_ref, do_ref, dgp_ref, dbg_ref):
        @pl.when(pl.program_id(0) == 0)
        def _():
            dgp_ref[...] = jnp.zeros_like(dgp_ref)
            dbg_ref[...] = jnp.zeros_like(dbg_ref)

        dmo, dgp = _rms_bwd(dh_ref[...], mo_ref[...], gp_ref[...])
        dgp_ref[...] += dgp
        dmob = dmo.astype(BF16)
        dmo_ref[...] = dmob
        dm = _nt(dmob, wo_ref[...])
        g0 = g_ref[:, 0:D]
        g1 = g_ref[:, D:2 * D]
        dyab = (dm * g0).astype(BF16)
        dybb = (dm * g1).astype(BF16)
        dya_ref[...] = dyab
        dyb_ref[...] = dybb
        dg0 = dm * ya_ref[...].astype(F32)
        dg1 = dm * yb_ref[...].astype(F32)
        dgate_ref[:, 0:D] = dg0.astype(BF16)
        dgate_ref[:, D:2 * D] = dg1.astype(BF16)
        dbg_ref[:, 0:D] += jnp.sum(dg0, axis=0, keepdims=True)
        dbg_ref[:, D:2 * D] += jnp.sum(dg1, axis=0, keepdims=True)
        dyain = _nt(dyab, wl_ref[...])
        do_ref[...] = _nt(dybb, wa_ref[...]).astype(BF16)
        xgv = xg_ref[...]
        gelu, gelu_grad = _gelu_and_grad(xgv)
        dhr_ref[...] = dyain * gelu
        dxg_ref[...] = (dyain * hr_ref[...] * gelu_grad).astype(BF16)

    b16 = jax.ShapeDtypeStruct((t, D), BF16)
    body, specs, operands = _behind(body, after)
    return pl.pallas_call(
        body, grid=(t // tm,), name="mix_bwd1",
        in_specs=specs + [_row_spec(tm, D), _row_spec(tm, D), _vec_spec(D), _row_spec(tm, 2 * D), _row_spec(tm, D),
                          _row_spec(tm, D), _row_spec(tm, D), _row_spec(tm, D), _WHOLE, _WHOLE, _WHOLE],
        out_specs=[_row_spec(tm, D), _row_spec(tm, D), _row_spec(tm, D), _row_spec(tm, 2 * D), _row_spec(tm, D),
                   _row_spec(tm, D), _row_spec(tm, D), _vec_spec(D), _vec_spec(2 * D)],
        out_shape=[b16, b16, b16, jax.ShapeDtypeStruct((t, 2 * D), BF16), jax.ShapeDtypeStruct((t, D), F32), b16, b16,
                   jax.ShapeDtypeStruct((1, D), F32), jax.ShapeDtypeStruct((1, 2 * D), F32)],
        compiler_params=_params("arbitrary"),
    )(*operands, dh2, mo, gpost, gate, ya, yb, xg, hr, w_o, w_lru, w_att)


def _rglru_bwd(dhr, hr, xc, r, ig, a, s, xr, conv_w, wa2, wx2, lam, after):
    t = dhr.shape[0]
    tm = _tile(t, TM_SCAN)
    nb8 = tm // 8
    nt = t // tm

    def body(dhr_ref, hr_ref, hrp_ref, xc_ref, r_ref, ig_ref, a_sc, s_ref, xr_ref, cw_ref, wa_ref, wx_ref, lam_ref,
             dxr_ref, dwa_ref, dwx_ref, dba_ref, dbx_ref, dlam_ref, dcw_ref, dcb_ref,
             ext_h, ext_d, g_sc, c_sc, nxt_sc):
        i = pl.program_id(0)
        first_tile = i == nt - 1

        @pl.when(i == 0)
        def _():
            c_sc[...] = jnp.zeros_like(c_sc)
            nxt_sc[...] = jnp.zeros_like(nxt_sc)
            for ref in (dwa_ref, dwx_ref, dba_ref, dbx_ref, dlam_ref, dcw_ref, dcb_ref):
                ref[...] = jnp.zeros_like(ref)

        lamv = lam_ref[...]
        sp = _softplus_neg(lamv)
        rv = r_ref[...]
        igv = ig_ref[...]
        xcv = xc_ref[...]
        a = a_sc[...]
        s = s_ref[...]

        def blk(jj, c):
            st = pl.multiple_of((nb8 - 1 - jj) * 8, 8)
            d8 = dhr_ref[pl.ds(st, 8), :]
            a8 = a_sc[pl.ds(st, 8), :]
            rows = [None] * 8
            for k in range(7, -1, -1):
                g = d8[k:k + 1, :] + c
                c = a8[k:k + 1, :] * g
                rows[k] = g
            g_sc[pl.ds(st, 8), :] = jnp.concatenate(rows, axis=0)
            return c

        c_sc[0:1, :] = lax.fori_loop(0, nb8, blk, c_sc[0:1, :])
        g = g_sc[...]
        ext_h[0:8, :] = jnp.where(first_tile, 0.0, hrp_ref[...])
        ext_h[8:8 + tm, :] = hr_ref[...]
        hprev = ext_h[pl.ds(7, tm), :]
        d_s = g * (igv * xcv)
        dig = g * s * xcv
        dxc = g * s * igv
        dla = (g * hprev) * a - d_s * ((a * a) / s)
        dr_pre = (dla * (-LRU_C * sp)) * (rv * (1.0 - rv))
        di_pre = dig * (igv * (1.0 - igv))
        dlam_ref[...] += jnp.sum(dla * (LRU_C * rv), axis=0, keepdims=True) * jax.nn.sigmoid(-lamv)
        dba_ref[...] += jnp.sum(dr_pre, axis=0, keepdims=True)
        dbx_ref[...] += jnp.sum(di_pre, axis=0, keepdims=True)
        drb = dr_pre.astype(BF16)
        dib = di_pre.astype(BF16)
        xcb = xcv.astype(BF16)
        ext_d[tm:tm + 8, :] = nxt_sc[...]
        for p in range(8):
            sl = slice(p * 128, (p + 1) * 128)
            ext_d[0:tm, sl] = dxc[:, sl] + _nt(drb[:, sl], wa_ref[p]) + _nt(dib[:, sl], wx_ref[p])
            dwa_ref[p] += _tn(xcb[:, sl], drb[:, sl])
            dwx_ref[p] += _tn(xcb[:, sl], dib[:, sl])
        dxcv = ext_d[0:tm, :]
        nxt_sc[...] = ext_d[0:8, :]
        dcb_ref[...] += jnp.sum(dxcv, axis=0, keepdims=True)
        xrv = xr_ref[...]
        dxr = jnp.zeros((tm, D), F32)
        for tap in range(4):
            ext_h[0:tm, :] = ext_d[pl.ds(3 - tap, tm), :]
            ahead = ext_h[0:tm, :]
            dxr = dxr + ahead * cw_ref[tap:tap + 1, :]
            dcw_ref[tap:tap + 1, :] += jnp.sum(ahead * xrv, axis=0, keepdims=True)
        dxr_ref[...] = dxr.astype(BF16)

    rev = pl.BlockSpec((tm, D), lambda i: (nt - 1 - i, 0))
    prev = pl.BlockSpec((8, D), lambda i: (jnp.maximum((nt - 1 - i) * nb8 - 1, 0), 0))
    full = lambda shape: pl.BlockSpec(shape, lambda i: tuple(0 for _ in shape))
    vec = jax.ShapeDtypeStruct((1, D), F32)
    blocks = jax.ShapeDtypeStruct((8, 128, 128), F32)
    body, specs, operands = _behind(body, after)
    return pl.pallas_call(
        body, grid=(nt,), name="rglru_bwd",
        in_specs=specs + [rev, rev, prev, rev, rev, rev, rev, rev, rev, full((4, D)), full((8, 128, 128)),
                          full((8, 128, 128)), _vec_spec(D)],
        out_specs=[rev, full((8, 128, 128)), full((8, 128, 128)), _vec_spec(D), _vec_spec(D), _vec_spec(D), full((4, D)),
                   _vec_spec(D)],
        out_shape=[jax.ShapeDtypeStruct((t, D), BF16), blocks, blocks, vec, vec, vec, jax.ShapeDtypeStruct((4, D), F32), vec],
        scratch_shapes=[pltpu.VMEM((tm + 8, D), F32), pltpu.VMEM((tm + 8, D), F32),
                        pltpu.VMEM((tm, D), F32), pltpu.VMEM((8, D), F32), pltpu.VMEM((8, D), F32)],
        compiler_params=_params("arbitrary"),
    )(*operands, dhr, hr, hr, xc, r, ig, a, s, xr, conv_w, wa2, wx2, lam)


def _attn_bwd(sink_rows, q, kp, vp, bias_t, mask, do):
    t = q.shape[0]
    tp = kp.shape[0]
    per_step = 4

    def body(sink_ref, q_ref, kp_ref, vp_ref, bias_ref, mask_ref, do_ref, dq_ref, dk_ref, dv_ref, dbias_ref, ds_ref):
        @pl.when(pl.program_id(0) == 0)
        def _():
            for ref in (dk_ref, dv_ref, dbias_ref, ds_ref):
                ref[...] = jnp.zeros_like(ref)

        maskv = mask_ref[...]
        lane_group = lax.broadcasted_iota(jnp.int32, (1, 4 * HEAD_DIM), 1) // HEAD_DIM

        def own_blocks(full):
            out = full[0:KP]
            for g in range(1, 4):
                out = jnp.where(lane_group == g, full[g * KP:(g + 1) * KP], out)
            return out

        dsc_sum, dsinks, dks, dvs = 0.0, [0.0] * 4, [], []
        for k in range(per_step):
            c = pl.program_id(0) * per_step + k
            chunk = slice(k * CHUNK, (k + 1) * CHUNK)
            st = pl.multiple_of(c * CHUNK, CHUNK)
            kbd = _block_diag(kp_ref[pl.ds(st, KP), :], maskv)
            vbd = _block_diag(vp_ref[pl.ds(st, KP), :], maskv)
            q_all = _stack_heads(q_ref[chunk, :])
            do_all = _stack_heads(do_ref[chunk, :])
            valid = lax.broadcasted_iota(jnp.int32, (KP, 1), 0) + c * CHUNK >= PAD_KEYS
            qk = _nt(kbd, q_all)
            dp = _nt(vbd, do_all)
            ps, dscs = [], []
            for g in range(4):
                rows = slice(g * KP, (g + 1) * KP)
                p, sink_p = _group_softmax(qk[rows], bias_ref[rows, :], sink_ref[g:g + 1, :], valid)
                delta = jnp.sum(p * dp[rows], axis=0, keepdims=True)
                ps.append(p)
                dscs.append(p * (dp[rows] - delta))
                dsinks[g] = dsinks[g] - sink_p * delta
            dsc = jnp.concatenate(dscs, axis=0)
            dsc_sum = dsc_sum + dsc
            dsb = (dsc * (HEAD_DIM ** -0.5)).astype(BF16)
            dq_ref[chunk, :] = _unstack_heads(_tn(dsb, kbd)).astype(BF16)
            dks.append((st, own_blocks(_nn(dsb, q_all))))
            dvs.append((st, own_blocks(_nn(jnp.concatenate(ps, axis=0).astype(BF16), do_all))))
        dbias_ref[...] += dsc_sum
        for g in range(4):
            ds_ref[g:g + 1, :] += dsinks[g]
        for (st, dkw), (_, dvw) in zip(dks, dvs):
            dk_ref[pl.ds(st, KP), :] += dkw
            dv_ref[pl.ds(st, KP), :] += dvw

    full = lambda shape: pl.BlockSpec(shape, lambda i: tuple(0 for _ in shape))
    return pl.pallas_call(
        body, grid=(t // (per_step * CHUNK),), name="attn_bwd",
        in_specs=[_WHOLE, _row_spec(per_step * CHUNK, D), _WHOLE, _WHOLE, _WHOLE, _WHOLE, _row_spec(per_step * CHUNK, D)],
        out_specs=[_row_spec(per_step * CHUNK, D), full((tp, KV_W)), full((tp, KV_W)), full((4 * KP, 4 * CHUNK)),
                   full((8, 4 * CHUNK))],
        out_shape=[jax.ShapeDtypeStruct((t, D), BF16), jax.ShapeDtypeStruct((tp, KV_W), F32),
                   jax.ShapeDtypeStruct((tp, KV_W), F32), jax.ShapeDtypeStruct((4 * KP, 4 * CHUNK), F32),
                   jax.ShapeDtypeStruct((8, 4 * CHUNK), F32)],
        compiler_params=_params("arbitrary"),
    )(sink_rows, q, kp, vp, bias_t, mask, do)


def _mix_bwd2(dproj, dgate, h1, dh2, gmix, w_in_g, w_gate_g, after):
    t = h1.shape[0]
    tm = _tile(t)

    def body(dp_ref, dg_ref, h_ref, dh_ref, g_ref, win_ref, wg_ref, dh1_ref, dgm_ref):
        @pl.when(pl.program_id(0) == 0)
        def _():
            dgm_ref[...] = jnp.zeros_like(dgm_ref)

        du = jnp.zeros((tm, D), F32)
        for s in range(NSH):
            du = du + _nt(dp_ref[:, s * IN_S:(s + 1) * IN_S], win_ref[s])
            du = du + _nt(dg_ref[:, s * GATE_S:(s + 1) * GATE_S], wg_ref[s])
        dxn, dg = _rms_bwd(du, h_ref[...], g_ref[...])
        dgm_ref[...] += dg
        dh1_ref[...] = dh_ref[...] + dxn

    body, specs, operands = _behind(body, after)
    return pl.pallas_call(
        body, grid=(t // tm,), name="mix_bwd2",
        in_specs=specs + [_row_spec(tm, NSH * IN_S), _row_spec(tm, 2 * D), _row_spec(tm, D), _row_spec(tm, D), _vec_spec(D),
                          _WHOLE, _WHOLE],
        out_specs=[_row_spec(tm, D), _vec_spec(D)],
        out_shape=[jax.ShapeDtypeStruct((t, D), F32), jax.ShapeDtypeStruct((1, D), F32)],
        compiler_params=_params("arbitrary"),
    )(*operands, dproj, dgate, h1, dh2, gmix, w_in_g, w_gate_g)


def _band_onehot():
    nb = N_BUCKETS // 2
    max_exact = nb // 2
    rel = jnp.arange(KB)[None, :] - PAD_KEYS - jnp.arange(CHUNK)[:, None]
    ret = jnp.where(rel > 0, nb, 0)
    n = jnp.abs(rel)
    nf = jnp.maximum(n, 1).astype(jnp.float32)
    large = max_exact + (jnp.log(nf / max_exact) / math.log(128 / max_exact) * (nb - max_exact)).astype(jnp.int32)
    large = jnp.minimum(large, nb - 1)
    buckets = (ret + jnp.where(n < max_exact, n, large)).reshape(1, CHUNK * KB)
    return (buckets == jnp.arange(N_BUCKETS)[:, None]).astype(F32)


def _pair_blocks(w):
    pairs = w.reshape(8, 2, 64, 64)
    z = jnp.zeros((8, 64, 64), w.dtype)
    return jnp.concatenate([jnp.concatenate([pairs[:, 0], z], axis=2), jnp.concatenate([z, pairs[:, 1]], axis=2)], axis=1)


def _unpair_blocks(w2):
    return jnp.stack([w2[:, 0:64, 0:64], w2[:, 64:128, 64:128]], axis=1).reshape(16, 64, 64)


def _local_step(x, target, weights, sm, reducer):
    row = lambda v: v.reshape(1, -1)
    wg = dict(weights("ffn1", x))
    sm = dict(sm, conv_w=wg["conv_w"])
    onehot_t = _band_onehot()
    bias = _bias_fwd(sm["rel_bias"].T, onehot_t).reshape(4, 4, CHUNK, KB)
    bias_t = jnp.pad(jnp.transpose(bias, (0, 3, 1, 2)), ((0, 0), (0, KP - KB), (0, 0), (0, 0))).reshape(4 * KP, 4 * CHUNK)
    sink_rows = jnp.pad(jnp.repeat(sm["attn_sinks"].reshape(4, 4), CHUNK, axis=1), ((0, 4), (0, 0)))
    grp = jnp.arange(4 * KP)[:, None] // KP == jnp.arange(4 * HEAD_DIM)[None, :] // HEAD_DIM
    mask = (grp & (jnp.arange(4 * KP)[:, None] % KP < KB)).astype(BF16)
    wa2 = _pair_blocks(sm["rg_a_w"]).astype(BF16)
    wx2 = _pair_blocks(sm["rg_x_w"]).astype(BF16)

    h1, a1, b1, hm1, f1 = _ffn_fwd(x, row(sm["ffn1_pre_g"]), wg["ffn1_w1"], wg["ffn1_w3"], wg["ffn1_w2"],
                                   row(sm["ffn1_post_g"]), "ffn1_fwd")
    wg.update(weights("mix_in", h1))
    u, q, k, v, xr, xg, gate = _mix_proj(h1, row(sm["mix_pre_g"]), wg["w_in"], wg["w_gate"], row(sm["b_gate"]))
    token = weights("mix_out", u, begin=True)
    hr, yain, xc, r, ig, lru_a, lru_s = _rglru_fwd(xr, xg, sm["conv_w"], row(sm["conv_b"]), wa2, row(sm["rg_a_b"]), wx2,
                                                   row(sm["rg_x_b"]), row(sm["lru_lambda"]), token)
    token = weights("ffn2", hr, begin=True)
    kp = jnp.pad(k, ((PAD_KEYS, KP - KB), (0, 0)))
    vp = jnp.pad(v, ((PAD_KEYS, KP - KB), (0, 0)))
    o = _attn_fwd(sink_rows, q, kp, vp, bias_t, mask, token)
    wg.update(weights("mix_out", o))
    w_lru = wg["w_lru_out"].reshape(D, D)
    w_att = wg["w_attn_out"].reshape(D, D)
    w_o = wg["w_o"].reshape(D, D)
    wg.update(weights("ffn2", o))
    h2, mo, merged, ya, yb = _merge_fwd(yain, o, gate, h1, w_lru, w_att, w_o, row(sm["mix_post_g"]))
    dy, a2, b2, hm2, f2, sq = _ffn_fwd(h2, row(sm["ffn2_pre_g"]), wg["ffn2_w1"], wg["ffn2_w3"], wg["ffn2_w2"],
                                       row(sm["ffn2_post_g"]), "ffn2_fwd", target)

    big, small = {}, {}
    dh2, n2, da2, db2, df2, small["ffn2_pre_g"], small["ffn2_post_g"] = _ffn_bwd(
        dy, h2, f2, a2, b2, row(sm["ffn2_pre_g"]), row(sm["ffn2_post_g"]), wg["ffn2_w1"], wg["ffn2_w3"], wg["ffn2_w2"],
        "ffn2_bwd")
    big["ffn2_w1"] = _wgrad_rows(da2, n2, "dw_ffn2_w1")
    big["ffn2_w3"] = _wgrad_rows(db2, n2, "dw_ffn2_w3")
    big["ffn2_w2"] = _wgrad_rows(hm2, df2, "dw_ffn2_w2")
    token = reducer.begin("ffn2", {n: big[n] for n in ("ffn2_w1", "ffn2_w3", "ffn2_w2")})
    dmo, dya, dyb, dgate, dhr, dxg, do, small["mix_post_g"], small["b_gate"] = _mix_bwd1(
        dh2, mo, row(sm["mix_post_g"]), gate, ya, yb, xg, hr, w_o, w_lru, w_att, token)
    big["w_o"] = _wgrad_sq(merged, dmo, "dw_w_o").reshape(NSH, D // NSH, D)
    big["w_lru_out"] = _wgrad_sq(yain, dya, "dw_w_lru_out").reshape(NSH, D // NSH, D)
    big["w_attn_out"] = _wgrad_sq(o, dyb, "dw_w_attn_out").reshape(NSH, D // NSH, D)
    token = reducer.advance("ffn2", big["w_attn_out"])
    (dxr, dwa2, dwx2, small["rg_a_b"], small["rg_x_b"], small["lru_lambda"], small["conv_w"], small["conv_b"]) = _rglru_bwd(
        dhr, hr, xc, r, ig, lru_a, lru_s, xr, sm["conv_w"], wa2, wx2, row(sm["lru_lambda"]), token)
    small["rg_a_w"] = _unpair_blocks(dwa2)
    small["rg_x_w"] = _unpair_blocks(dwx2)
    dq, dkp, dvp, dbias_t, ds_rows = _attn_bwd(sink_rows, q, kp, vp, bias_t, mask, do)
    dbias = jnp.transpose(dbias_t.reshape(4, KP, 4, CHUNK)[:, :KB], (0, 2, 3, 1)).reshape(N_HEADS, CHUNK * KB)
    drel_t, dsinks = _bias_bwd(dbias, onehot_t, ds_rows)
    small["attn_sinks"] = dsinks[0:4, 0:4].reshape(N_HEADS)
    small["rel_bias"] = drel_t.T
    t = x.shape[0]
    dproj = jnp.concatenate([dq, dkp[PAD_KEYS:PAD_KEYS + t].astype(BF16), dvp[PAD_KEYS:PAD_KEYS + t].astype(BF16), dxr, dxg],
                            axis=1)
    big["w_in"] = _wgrad_cols(u, dproj, IN_S, "dw_w_in")
    big["w_gate"] = _wgrad_cols(u, dgate, GATE_S, "dw_w_gate")
    token = reducer.begin("mix", {n: big[n] for n in ("w_in", "w_gate", "w_lru_out", "w_attn_out", "w_o")})
    dh1, small["mix_pre_g"] = _mix_bwd2(dproj, dgate, h1, dh2, row(sm["mix_pre_g"]), wg["w_in"], wg["w_gate"], token)
    n1, da1, db1, df1, small["ffn1_post_g"] = _ffn_bwd_acts(
        dh1, x, f1, a1, b1, row(sm["ffn1_pre_g"]), row(sm["ffn1_post_g"]), wg["ffn1_w2"], "ffn1_bwd_acts")
    token = reducer.advance("mix", df1)
    big["ffn1_w1"] = _wgrad_rows(da1, n1, "dw_ffn1_w1", token)
    big["ffn1_w3"] = _wgrad_rows(db1, n1, "dw_ffn1_w3", token)
    big["ffn1_w2"] = _wgrad_rows(hm1, df1, "dw_ffn1_w2", token)
    token = reducer.begin("ffn1", {n: big[n] for n in ("ffn1_w1", "ffn1_w3", "ffn1_w2")})
    dx, small["ffn1_pre_g"] = _ffn_bwd_input(dh1, x, da1, db1, row(sm["ffn1_pre_g"]), wg["ffn1_w1"], wg["ffn1_w3"],
                                             "ffn1_bwd_input", token)
    return sq, dx, big, small


_ANY = pl.BlockSpec(memory_space=pl.ANY)


def _place():
    return lax.axis_index("x"), lax.axis_index("y"), lax.axis_index("c")


def _other_chips(x, y):
    return [(1 - x, y), (x, 1 - y), (1 - x, 1 - y)]


_HBM = pl.BlockSpec(memory_space=pltpu.HBM)
_SEM = pl.BlockSpec(memory_space=pltpu.SEMAPHORE)
_EFFECT = pltpu.SideEffectType.DATAFLOW_SIDE_EFFECTING


def _cast_into_slot(w, chip, name, after=None):
    r, cc = w.shape
    rows = r // 4

    def body(chip_ref, *refs):
        w_ref, o_ref = refs[-2:]
        o_ref[...] = w_ref[...].astype(BF16)

    extra = [] if after is None else [after]
    return pl.pallas_call(
        body, name=name, out_shape=jax.ShapeDtypeStruct((NSH, r, cc), BF16),
        grid_spec=pltpu.PrefetchScalarGridSpec(
            num_scalar_prefetch=1, grid=(4,), in_specs=[_ANY] * len(extra) + [pl.BlockSpec((rows, cc), lambda i, chip: (i, 0))],
            out_specs=pl.BlockSpec((None, rows, cc), lambda i, chip: (chip[0], i, 0))),
        compiler_params=_params("arbitrary"))(chip, *extra, w)


def _piece(ref, slot, c):
    if ref.dtype == F32:
        return ref.at[slot]
    rh = ref.shape[1] // 2
    return ref.at[slot, pl.ds(pl.multiple_of(c * rh, 16), rh), :]


def _gather_start(stages, name):
    flat = [b for stage in stages for b in stage]
    n, ns = len(flat), len(stages)

    def body(*refs):
        ins, sems, token = refs[:n], refs[n:n + 2 * ns], refs[-1]
        x, y, c = _place()
        me = 2 * x + y
        k = 0
        for s, stage in enumerate(stages):
            for i in range(len(stage)):
                for j, (px, py) in enumerate(_other_chips(x, y)):
                    piece = _piece(ins[k], me, c)
                    pltpu.make_async_remote_copy(src_ref=piece, dst_ref=piece, send_sem=sems[2 * s].at[3 * i + j],
                                                 recv_sem=sems[2 * s + 1].at[3 * i + j], device_id=(px, py, c),
                                                 device_id_type=MESH).start()
                k += 1
        token[...] = jnp.zeros_like(token)

    sem_shapes = [pltpu.SemaphoreType.DMA((3 * len(stage),)) for stage in stages for _ in range(2)]
    outs = pl.pallas_call(
        body, name=name, in_specs=[_HBM] * n,
        out_specs=[_SEM] * (2 * ns) + [_HBM] * n + [pl.BlockSpec(memory_space=pltpu.VMEM)],
        out_shape=sem_shapes + [pltpu.HBM(b.shape, b.dtype) for b in flat] + [jax.ShapeDtypeStruct((8, 128), F32)],
        input_output_aliases={i: 2 * ns + i for i in range(n)},
        compiler_params=pltpu.CompilerParams(has_side_effects=_EFFECT),
    )(*[pltpu.with_memory_space_constraint(b, pltpu.HBM) for b in flat])
    sems, bufs, token = outs[:2 * ns], list(outs[2 * ns:2 * ns + n]), outs[-1]
    per_stage, k = [], 0
    for s, stage in enumerate(stages):
        per_stage.append((sems[2 * s], sems[2 * s + 1], bufs[k:k + len(stage)]))
        k += len(stage)
    return per_stage, token


def _gather_wait(send_sems, recv_sems, bufs, after, name):
    n = len(bufs)

    def body(*refs):
        ins, ssem, rsem = refs[:n], refs[n], refs[n + 1]
        x, y, c = _place()
        me = 2 * x + y
        for i in range(n):
            for j, (px, py) in enumerate(_other_chips(x, y)):
                cp = pltpu.make_async_remote_copy(src_ref=_piece(ins[i], me, c), dst_ref=_piece(ins[i], 2 * px + py, c),
                                                  send_sem=ssem.at[3 * i + j], recv_sem=rsem.at[3 * i + j],
                                                  device_id=(px, py, c), device_id_type=MESH)
                cp.wait_send()
                cp.wait_recv()

    return pl.pallas_call(
        body, name=name, in_specs=[_HBM] * n + [_SEM, _SEM, _ANY], out_specs=[_HBM] * n,
        out_shape=[pltpu.HBM(b.shape, b.dtype) for b in bufs], input_output_aliases={i: i for i in range(n)},
        compiler_params=pltpu.CompilerParams(has_side_effects=_EFFECT),
    )(*bufs, send_sems, recv_sems, after)


def _sibling_fill(bufs, name):
    n = len(bufs)

    def body(*refs):
        ins, outs = refs[:n], refs[n:2 * n]
        send_sems, recv_sems = refs[2 * n:]
        x, y, c = _place()
        copies = []
        for i in range(n):
            for j, (px, py) in enumerate(_other_chips(x, y)):
                copies.append(pltpu.make_async_remote_copy(
                    src_ref=_piece(ins[i], 2 * px + py, c), dst_ref=_piece(outs[i], 2 * px + py, c),
                    send_sem=send_sems.at[3 * i + j], recv_sem=recv_sems.at[3 * i + j], device_id=(x, y, 1 - c),
                    device_id_type=MESH))
                copies[-1].start()
        for cp in copies:
            cp.wait()

    return pl.pallas_call(
        body, name=name, in_specs=[_ANY] * n, out_specs=[_ANY] * n,
        out_shape=[jax.ShapeDtypeStruct(b.shape, b.dtype) for b in bufs], input_output_aliases={i: i for i in range(n)},
        scratch_shapes=[pltpu.SemaphoreType.DMA((3 * n,)), pltpu.SemaphoreType.DMA((3 * n,))],
        compiler_params=pltpu.CompilerParams(has_side_effects=True),
    )(*bufs)


def _swap_plan(srcs, lands):
    x, y, c = _place()
    plan = []
    for src, land in zip(srcs, lands):
        rh = src.shape[1] // 2
        plan.append((src.at[:, pl.ds(pl.multiple_of((1 - c) * rh, 16), rh), :], land, (x, y, 1 - c)))
    return plan


def _owners_plan(srcs, lands):
    x, y, c = _place()
    return [(src.at[2 * px + py], land.at[j], (px, py, c))
            for src, land in zip(srcs, lands) for j, (px, py) in enumerate(_other_chips(x, y))]


def _exchange_start(srcs, lands, plan, copies, name):
    n, m = len(srcs), len(srcs) + len(lands)

    def body(*refs):
        send_sems, recv_sems, token = refs[m], refs[m + 1], refs[-1]
        for k, (src, dst, dev) in enumerate(plan(refs[:n], refs[n:m])):
            pltpu.make_async_remote_copy(src_ref=src, dst_ref=dst, send_sem=send_sems.at[k], recv_sem=recv_sems.at[k],
                                         device_id=dev, device_id_type=MESH).start()
        token[...] = jnp.zeros_like(token)

    both = list(srcs) + list(lands)
    outs = pl.pallas_call(
        body, name=name, in_specs=[_HBM] * m,
        out_specs=[_SEM, _SEM] + [_HBM] * m + [pl.BlockSpec(memory_space=pltpu.VMEM)],
        out_shape=[pltpu.SemaphoreType.DMA((copies,)), pltpu.SemaphoreType.DMA((copies,))]
        + [pltpu.HBM(b.shape, b.dtype) for b in both] + [jax.ShapeDtypeStruct((8, 128), F32)],
        input_output_aliases={i: 2 + i for i in range(m)},
        compiler_params=pltpu.CompilerParams(has_side_effects=_EFFECT),
    )(*[pltpu.with_memory_space_constraint(b, pltpu.HBM) for b in both])
    return (outs[0], outs[1]), list(outs[2:2 + n]), list(outs[2 + n:2 + m]), outs[-1]


def _exchange_wait(sems, srcs, lands, plan, after, name):
    n, m = len(srcs), len(srcs) + len(lands)

    def body(*refs):
        send_sems, recv_sems = refs[m], refs[m + 1]
        for k, (src, dst, dev) in enumerate(plan(refs[:n], refs[n:m])):
            cp = pltpu.make_async_remote_copy(src_ref=src, dst_ref=dst, send_sem=send_sems.at[k], recv_sem=recv_sems.at[k],
                                              device_id=dev, device_id_type=MESH)
            cp.wait_send()
            cp.wait_recv()

    both = list(srcs) + list(lands)
    afters = list(after) if isinstance(after, (list, tuple)) else [after]
    outs = pl.pallas_call(
        body, name=name, in_specs=[_HBM] * m + [_SEM, _SEM] + [_ANY] * len(afters), out_specs=[_HBM] * m,
        out_shape=[pltpu.HBM(b.shape, b.dtype) for b in both], input_output_aliases={i: i for i in range(m)},
        compiler_params=pltpu.CompilerParams(has_side_effects=_EFFECT),
    )(*both, sems[0], sems[1], *afters)
    return list(outs[:n]), list(outs[n:])


def _fill_plan(bufs, _):
    x, y, c = _place()
    return [(_piece(buf, 2 * px + py, c), _piece(buf, 2 * px + py, c), (x, y, 1 - c))
            for buf in bufs for px, py in _other_chips(x, y)]


class _Reducer:
    def __init__(self, where):
        self.state = {}
        self.where = where

    def begin(self, stage, grads):
        names = list(grads)
        full = [grads[n] for n in names]
        lands = [lax.empty((NSH, g.shape[1] // 2, g.shape[2]), g.dtype) for g in full]
        sems, full, lands, token = _exchange_start(full, lands, _swap_plan, len(full), "swap_start_" + stage)
        self.state[stage] = (names, sems, full, lands)
        return token

    def advance(self, stage, after):
        names, sems, full, lands = self.state[stage]
        full, got = _exchange_wait(sems, full, lands, _swap_plan, after, "swap_wait_" + stage)
        sums, own = _chip_sums(full, got, self.where, "chip_sums_" + stage)
        lands = [lax.empty((3,) + s.shape[1:], BF16) for s in sums]
        sems, sent, lands, token = _exchange_start(sums, lands, _owners_plan, 3 * len(sums), "owners_start_" + stage)
        self.state[stage] = (names, own, sems, sent, lands)
        return token

    def finish(self, stage, after):
        names, own, sems, sent, lands = self.state[stage]
        _, got = _exchange_wait(sems, sent, lands, _owners_plan, after, "owners_wait_" + stage)
        return dict(zip(names, _owner_sums(own, got, "owner_sums_" + stage)))


def _chip_sums(gs, gots, where, name):
    n = len(gs)

    def body(where_ref, *refs):
        g_refs, got_refs, hb_refs, own_refs = (refs[k * n:(k + 1) * n] for k in range(4))
        mine = pl.program_id(0) == where_ref[1]
        for g_ref, got_ref, hb_ref, own_ref in zip(g_refs, got_refs, hb_refs, own_refs):
            h = g_ref[...].astype(F32) + got_ref[...].astype(F32)
            hb_ref[...] = h.astype(BF16)

            @pl.when(mine)
            def _():
                own_ref[...] = h

    halves = [(g.shape[1] // 2, g.shape[2]) for g in gs]
    slot = [pl.BlockSpec((None, rh, cc), lambda s, where: (s, 0, 0)) for rh, cc in halves]
    outs = pl.pallas_call(
        body, name=name,
        grid_spec=pltpu.PrefetchScalarGridSpec(
            num_scalar_prefetch=1, grid=(NSH,),
            in_specs=[pl.BlockSpec((None, rh, cc), lambda s, where: (s, where[0], 0)) for rh, cc in halves] + slot,
            out_specs=slot + [pl.BlockSpec((rh, cc), lambda s, where: (0, 0)) for rh, cc in halves]),
        out_shape=[jax.ShapeDtypeStruct((NSH, rh, cc), BF16) for rh, cc in halves]
        + [jax.ShapeDtypeStruct((rh, cc), F32) for rh, cc in halves],
        compiler_params=_params("arbitrary"),
    )(where, *gs, *gots)
    return list(outs[:n]), list(outs[n:])


def _owner_sums(owns, gots, name):
    n = len(owns)

    def body(*refs):
        own_refs, got_refs, o_refs = (refs[k * n:(k + 1) * n] for k in range(3))
        for own_ref, got_ref, o_ref in zip(own_refs, got_refs, o_refs):
            o_ref[...] = ((own_ref[...] + got_ref[0].astype(F32)) + got_ref[1].astype(F32)) + got_ref[2].astype(F32)

    blocks = [(o.shape[0] // 2, o.shape[1]) for o in owns]
    rows = [pl.BlockSpec(b, lambda i: (i, 0)) for b in blocks]
    return pl.pallas_call(
        body, grid=(2,), name=name,
        in_specs=rows + [pl.BlockSpec((3,) + b, lambda i: (0, i, 0)) for b in blocks], out_specs=rows,
        out_shape=[jax.ShapeDtypeStruct(o.shape, F32) for o in owns], compiler_params=_params("arbitrary"),
    )(*owns, *gots)


def _sibling_plan(srcs, lands):
    x, y, c = _place()
    return [(src, land, (x, y, 1 - c)) for src, land in zip(srcs, lands)]


def _all_reduce_small(part):
    def body(p_ref, o_ref, rbuf, send1, recv1, send2, recv2):
        x, y, c = _place()
        me = 4 * x + 2 * y + c
        peers = []
        for k in range(1, 8):
            px, py, pc = x ^ ((k >> 2) & 1), y ^ ((k >> 1) & 1), c ^ (k & 1)
            peers.append((k, (px, py, pc), 4 * px + 2 * py + pc))

        def rows(d):
            return pl.ds(pl.multiple_of(d * SMALL_SLICE, 8), SMALL_SLICE)

        first = [pltpu.make_async_remote_copy(src_ref=p_ref.at[rows(idx), :], dst_ref=rbuf.at[me], send_sem=send1.at[k],
                                              recv_sem=recv1.at[k], device_id=dev, device_id_type=MESH)
                 for k, dev, idx in peers]
        for cp in first:
            cp.start()
        rbuf[me] = p_ref[rows(me), :]
        for k, dev, idx in peers:
            pltpu.make_async_remote_copy(src_ref=p_ref.at[rows(idx), :], dst_ref=rbuf.at[idx], send_sem=send1.at[k],
                                         recv_sem=recv1.at[k], device_id=dev, device_id_type=MESH).wait_recv()
        acc = rbuf[0]
        for d in range(1, 8):
            acc = acc + rbuf[d]
        o_ref[rows(me), :] = acc
        second = [pltpu.make_async_remote_copy(src_ref=o_ref.at[rows(me), :], dst_ref=o_ref.at[rows(me), :],
                                               send_sem=send2.at[k], recv_sem=recv2.at[k], device_id=dev, device_id_type=MESH)
                  for k, dev, idx in peers]
        for cp in second:
            cp.start()
        for k, dev, idx in peers:
            pltpu.make_async_remote_copy(src_ref=o_ref.at[rows(me), :], dst_ref=o_ref.at[rows(idx), :], send_sem=send2.at[k],
                                         recv_sem=recv2.at[k], device_id=dev, device_id_type=MESH).wait_recv()
        for cp in first + second:
            cp.wait_send()

    return pl.pallas_call(
        body, name="all_reduce_small", in_specs=[_WHOLE], out_specs=_WHOLE,
        out_shape=jax.ShapeDtypeStruct((SMALL_ROWS, 128), F32),
        scratch_shapes=[pltpu.VMEM((8, SMALL_SLICE, 128), F32)] + [pltpu.SemaphoreType.DMA((8,))] * 4,
        compiler_params=pltpu.CompilerParams(has_side_effects=True),
    )(part)


def _adamw_update(w, gv, m, v):
    nm = ADAM_B1 * m + (1.0 - ADAM_B1) * gv
    nv = ADAM_B2 * v + (1.0 - ADAM_B2) * (gv * gv)
    m_hat = nm / (1.0 - ADAM_B1 ** ADAM_STEP)
    v_hat = nv / (1.0 - ADAM_B2 ** ADAM_STEP)
    return -ADAM_LR * (m_hat / (jnp.sqrt(v_hat) + ADAM_EPS) + ADAM_WD * w), nm, nv


def _adamw_small(ws, gs, ms, vs):
    n = len(ws)

    def body(*refs):
        w_refs, g_refs, m_refs, v_refs, d_refs, nm_refs, nv_refs = (refs[k * n:(k + 1) * n] for k in range(7))
        for i in range(n):
            d_refs[i][...], nm_refs[i][...], nv_refs[i][...] = _adamw_update(
                w_refs[i][...], g_refs[i][...], m_refs[i][...], v_refs[i][...])

    out = [jax.ShapeDtypeStruct(w.shape, F32) for w in ws]
    outs = pl.pallas_call(body, in_specs=[_WHOLE] * (4 * n), out_specs=[_WHOLE] * (3 * n), out_shape=out * 3,
                          name="adamw_small", compiler_params=_params())(*ws, *gs, *ms, *vs)
    return outs[:n], outs[n:2 * n], outs[2 * n:]


def _adamw_halves(ws, mines, theirs, ms, vs, name):
    n = len(ws)
    steps = 2

    def body(*refs):
        w_refs, mine_refs, theirs_refs, m_refs, v_refs, g_refs, d_refs, nm_refs, nv_refs = (
            refs[k * n:(k + 1) * n] for k in range(9))
        is_mine = pl.program_id(0) == lax.axis_index("c")
        for i in range(n):
            gv = jnp.where(is_mine, mine_refs[i][...], theirs_refs[i][...])
            g_refs[i][...] = gv
            d_refs[i][...], nm_refs[i][...], nv_refs[i][...] = _adamw_update(w_refs[i][...], gv, m_refs[i][...], v_refs[i][...])

    blocks = [(h.shape[0] // steps, h.shape[1]) for h in mines]
    whole = [pl.BlockSpec(b, lambda h, i: (steps * h + i, 0)) for b in blocks]
    half = [pl.BlockSpec(b, lambda h, i: (i, 0)) for b in blocks]
    out = [jax.ShapeDtypeStruct(w.shape, F32) for w in ws]
    outs = pl.pallas_call(body, grid=(2, steps), in_specs=whole + half + half + whole + whole, out_specs=whole * 4,
                          out_shape=out * 4, name=name, compiler_params=_params("arbitrary", "arbitrary"),
                          )(*ws, *mines, *theirs, *ms, *vs)
    return [tuple(outs[k * n + i] for k in range(4)) for i in range(n)]


SMALL_USED = sum(size for _, size in SMALL) // 128


def _pack_small(vals, tail=None):
    parts = []
    for name, size in SMALL:
        flat = vals[name].reshape(-1).astype(F32)
        parts.append(jnp.pad(flat, (0, size - flat.shape[0])))
    if tail is not None:
        parts.append(tail.reshape(128))
    flat = jnp.concatenate(parts)
    return jnp.pad(flat, (0, SMALL_ROWS * 128 - flat.shape[0])).reshape(SMALL_ROWS, 128)


def _unpack_small(packed, shapes):
    flat = packed.reshape(-1)
    out, off = {}, 0
    for name, size in SMALL:
        n = math.prod(shapes[name])
        out[name] = flat[off:off + n].reshape(shapes[name])
        off += size
    return out


def kernel(x, ffn1_pre_g, ffn1_w1, ffn1_w3, ffn1_w2, ffn1_post_g, mix_pre_g, w_in, conv_w, conv_b, rg_a_w, rg_a_b, rg_x_w, rg_x_b, lru_lambda, w_lru_out, attn_sinks, rel_bias, w_attn_out, w_gate, b_gate, w_o, mix_post_g, ffn2_pre_g, ffn2_w1, ffn2_w3, ffn2_w2, ffn2_post_g, loss_target, m_ffn1_pre_g, m_ffn1_w1, m_ffn1_w3, m_ffn1_w2, m_ffn1_post_g, m_mix_pre_g, m_w_in, m_conv_w, m_conv_b, m_rg_a_w, m_rg_a_b, m_rg_x_w, m_rg_x_b, m_lru_lambda, m_w_lru_out, m_attn_sinks, m_rel_bias, m_w_attn_out, m_w_gate, m_b_gate, m_w_o, m_mix_post_g, m_ffn2_pre_g, m_ffn2_w1, m_ffn2_w3, m_ffn2_w2, m_ffn2_post_g, v_ffn1_pre_g, v_ffn1_w1, v_ffn1_w3, v_ffn1_w2, v_ffn1_post_g, v_mix_pre_g, v_w_in, v_conv_w, v_conv_b, v_rg_a_w, v_rg_a_b, v_rg_x_w, v_rg_x_b, v_lru_lambda, v_w_lru_out, v_attn_sinks, v_rel_bias, v_w_attn_out, v_w_gate, v_b_gate, v_w_o, v_mix_post_g, v_ffn2_pre_g, v_ffn2_w1, v_ffn2_w3, v_ffn2_w2, v_ffn2_post_g):
    given = dict(locals())
    chip = 2 * lax.axis_index("x") + lax.axis_index("y")
    transposed = ("ffn1_w1", "ffn1_w3", "ffn2_w1", "ffn2_w3")

    def shard(name, moment=""):
        w = given[moment + name][0]
        return w.T if name in transposed else w

    def unshard(name, w):
        return (w.T if name in transposed else w)[None]

    def only_my_columns(a):
        parts = a.reshape(1, 4, NSH, D // NSH)
        return sum(jnp.where(chip == s, parts[:, :, s], 0.0) for s in range(NSH))

    chip_arr = jnp.reshape(chip, (1,)).astype(jnp.int32)
    stage_names = {"ffn1": ["ffn1_w1", "ffn1_w3", "ffn1_w2", "conv_w"],
                   "mix_in": ["w_in", "w_gate"],
                   "mix_out": ["w_lru_out", "w_attn_out", "w_o"],
                   "ffn2": ["ffn2_w1", "ffn2_w3", "ffn2_w2"]}
    in_flight, started = {}, None
    for stage, names in stage_names.items():
        bufs = [jnp.where(lax.broadcasted_iota(jnp.int32, (NSH, 4, D // NSH), 0) == chip, given[n], 0.0) if n == "conv_w"
                else _cast_into_slot(shard(n), chip_arr, "cast_" + n, started) for n in names]
        (in_flight[stage],), started = _gather_start([bufs], "gather_start_" + stage)
    all_started = started

    filling = {}

    def weights(stage, after, begin=False):
        names = stage_names[stage]
        halves_of = [n for n in names if n != "conv_w"]
        if stage in filling:
            filled, _ = _exchange_wait(filling.pop(stage), *filling.pop(stage + "/bufs"), _fill_plan, after,
                                       "fill_wait_" + stage)
            return dict(zip(halves_of, filled))
        send_sems, recv_sems, landing = in_flight[stage]
        if stage == "ffn1":
            after = all_started
        landed = dict(zip(names, _gather_wait(send_sems, recv_sems, landing, after, "gather_wait_" + stage)))
        halves = [landed[n] for n in halves_of]
        if begin:
            filling[stage], bufs, _, token = _exchange_start(halves, [], _fill_plan, 3 * len(halves), "fill_start_" + stage)
            filling[stage + "/bufs"] = (bufs, [])
            return token
        out = dict(zip(halves_of, _sibling_fill(halves, "sibling_fill_" + stage)))
        if "conv_w" in names:
            out["conv_w"] = jnp.transpose(landed["conv_w"], (1, 0, 2)).reshape(4, D)
        return out

    small_shapes = {n: given[n].shape for n, _ in SMALL}
    small_shapes["conv_w"] = (1, 4, D)
    sm = {n: (given[n][0] if given[n].shape[0] == 1 and n != "rel_bias" else given[n]) for n, _ in SMALL if n != "conv_w"}

    reducer = _Reducer(jnp.stack([lax.axis_index("c"), chip]).astype(jnp.int32))
    sq, dx, _, small = _local_step(x[0], loss_target[0], weights, sm, reducer)

    reducer.advance("ffn1", dx)
    reduced_small = _all_reduce_small(_pack_small(small, tail=sq))
    loss = reduced_small[SMALL_USED, 0] * (0.5 / D)
    small_g = _unpack_small(reduced_small, small_shapes)
    grads, delta, new_m, new_v = {}, {}, {}, {}
    in_transit = {}

    def send(stage, after):
        halves = reducer.finish(stage, after)
        lands = [lax.empty(h.shape, F32) for h in halves.values()]
        sems, mine, lands, token = _exchange_start(list(halves.values()), lands, _sibling_plan, len(lands),
                                                   "halves_start_" + stage)
        in_transit[stage] = (list(halves), sems, mine, lands)
        return token

    def update(stage, after):
        names, sems, mine, lands = in_transit[stage]
        mine, theirs = _exchange_wait(sems, mine, lands, _sibling_plan, after, "halves_wait_" + stage)
        updated = _adamw_halves([shard(n) for n in names], mine, theirs, [shard(n, "m_") for n in names],
                                [shard(n, "v_") for n in names], "adamw_" + stage)
        for n, results in zip(names, updated):
            grads[n], delta[n], new_m[n], new_v[n] = (unshard(n, r) for r in results)
        return new_v[names[-1]]

    token = send("ffn2", reduced_small)
    token = send("mix", token)
    done = update("ffn2", token)
    token = send("ffn1", done)
    done = update("mix", token)
    update("ffn1", done)

    small_g["conv_w"] = only_my_columns(small_g["conv_w"])
    names = [n for n, _ in SMALL]
    flat2d = lambda a: a.reshape(-1, a.shape[-1])
    outs = _adamw_small(*[[flat2d(given[pre + n]) if pre != "g" else flat2d(small_g[n]) for n in names]
                          for pre in ("", "g", "m_", "v_")])
    for dst, arrs in zip((delta, new_m, new_v), outs):
        dst.update({n: a.reshape(given[n].shape) for n, a in zip(names, arrs)})
    grads.update(small_g)
    return (loss, dx[None], *[grads[n] for n in WEIGHTS], *[delta[n] for n in WEIGHTS], *[new_m[n] for n in WEIGHTS],
            *[new_v[n] for n in WEIGHTS])
```

```python
import functools
import math

import jax
import jax.numpy as jnp
from jax import lax
from jax.experimental import pallas as pl
from jax.experimental.pallas import tpu as pltpu

F32, BF16 = jnp.float32, jnp.bfloat16
D = 1024
NSH = 4
FF_S = 704
IN_S = 896
GATE_S = 512
KV_W = 256
CHUNK = 64
KB = 192
N_HEADS = 16
HEAD_DIM = 64
N_BUCKETS = 32
KP = 192
PAD_KEYS = 128
RMS_EPS = 1e-6
NEG_INF = -1e30
LRU_C = 8.0
TM = 512
TM_SCAN = 256
VMEM_LIMIT = 56 * 1024 * 1024
ADAM_LR, ADAM_B1, ADAM_B2, ADAM_EPS, ADAM_WD, ADAM_STEP = 0.001, 0.9, 0.999, 1e-08, 0.01, 10
SMALL_ROWS = 1216
SMALL_SLICE = SMALL_ROWS // 8
MESH = pl.DeviceIdType.MESH

BIG = ["ffn1_w1", "ffn1_w3", "ffn1_w2", "w_in", "w_lru_out", "w_attn_out", "w_gate", "w_o", "ffn2_w1", "ffn2_w3", "ffn2_w2"]
SMALL = [("ffn1_pre_g", 1024), ("ffn1_post_g", 1024), ("mix_pre_g", 1024), ("conv_w", 4096), ("conv_b", 1024),
         ("rg_a_w", 65536), ("rg_a_b", 1024), ("rg_x_w", 65536), ("rg_x_b", 1024), ("lru_lambda", 1024),
         ("attn_sinks", 1024), ("rel_bias", 1024), ("b_gate", 2048), ("mix_post_g", 1024), ("ffn2_pre_g", 1024),
         ("ffn2_post_g", 1024)]
WEIGHTS = ["ffn1_pre_g", "ffn1_w1", "ffn1_w3", "ffn1_w2", "ffn1_post_g", "mix_pre_g", "w_in", "conv_w", "conv_b", "rg_a_w",
           "rg_a_b", "rg_x_w", "rg_x_b", "lru_lambda", "w_lru_out", "attn_sinks", "rel_bias", "w_attn_out", "w_gate", "b_gate",
           "w_o", "mix_post_g", "ffn2_pre_g", "ffn2_w1", "ffn2_w3", "ffn2_w2", "ffn2_post_g"]


def _params(*sem):
    return pltpu.CompilerParams(dimension_semantics=sem or None, vmem_limit_bytes=VMEM_LIMIT)


def _nn(a, b):
    return jnp.dot(a, b, preferred_element_type=F32)


def _nt(a, b):
    return lax.dot_general(a, b, (((1,), (1,)), ((), ())), preferred_element_type=F32)


def _tn(a, b):
    return lax.dot_general(a, b, (((0,), (0,)), ((), ())), preferred_element_type=F32)


def _rms(x, g):
    rstd = lax.rsqrt(jnp.mean(x * x, axis=-1, keepdims=True) + RMS_EPS)
    return (x * rstd) * g


def _rms_bwd(dout, x, g):
    rstd = lax.rsqrt(jnp.mean(x * x, axis=-1, keepdims=True) + RMS_EPS)
    xhat = x * rstd
    dg = jnp.sum(dout * xhat, axis=0, keepdims=True)
    dxhat = dout * g
    dx = rstd * (dxhat - xhat * jnp.mean(dxhat * xhat, axis=-1, keepdims=True))
    return dx, dg


_GELU_K = math.sqrt(2.0 / math.pi)


def _gelu(x):
    return x * (0.5 * (1.0 + jnp.tanh(_GELU_K * (x + 0.044715 * (x * x * x)))))


def _gelu_and_grad(x):
    x2 = x * x
    t = jnp.tanh(_GELU_K * (x + 0.044715 * (x2 * x)))
    cdf = 0.5 * (1.0 + t)
    return x * cdf, cdf + x * (0.5 * (1.0 - t * t) * (_GELU_K * (1.0 + 3.0 * 0.044715 * x2)))


def _softplus_neg(lam):
    z = -lam
    u = jnp.exp(-jnp.abs(z))
    w = 1.0 + u
    log1p_u = jnp.where(w == 1.0, u, jnp.log(w) * (u / (w - 1.0)))
    return jnp.maximum(z, 0.0) + log1p_u


def _lru_coeffs(r, sp):
    log_a = (-LRU_C * r) * sp
    a = jnp.exp(log_a)
    t = jnp.tanh(log_a)
    s = jnp.sqrt(-2.0 * t / (1.0 - t))
    return a, s


def _row_spec(tm, width):
    return pl.BlockSpec((tm, width), lambda i: (i, 0))


def _vec_spec(width):
    return pl.BlockSpec((1, width), lambda i: (0, 0))


_WHOLE = pl.BlockSpec(memory_space=pltpu.VMEM)


def _tile(t, tm=TM):
    return min(tm, t)


def _ffn_fwd(x, gpre, w1g, w3g, w2g, gpost, name, target=None):
    t = x.shape[0]
    tm = _tile(t)
    last = target is not None

    def body(x_ref, gpre_ref, w1_ref, w3_ref, w2_ref, gpost_ref, *refs):
        t_ref, (h_ref, a_ref, b_ref, hm_ref, f_ref), l_ref = (refs[0] if last else None), refs[last:last + 5], refs[-1]
        xv = x_ref[...]
        nb = _rms(xv, gpre_ref[...]).astype(BF16)
        f = jnp.zeros((tm, D), F32)
        for s in range(NSH):
            a = _nt(nb, w1_ref[s])
            b = _nt(nb, w3_ref[s])
            hmb = ((a * jax.nn.sigmoid(a)) * b).astype(BF16)
            a_ref[s] = a.astype(BF16)
            b_ref[s] = b.astype(BF16)
            hm_ref[s] = hmb
            f = f + _nn(hmb, w2_ref[s])
        f_ref[...] = f
        h = xv + 0.5 * _rms(f, gpost_ref[...])
        if last:
            @pl.when(pl.program_id(0) == 0)
            def _():
                l_ref[...] = jnp.zeros_like(l_ref)

            e = h - t_ref[...]
            h_ref[...] = e * (1.0 / D)
            l_ref[...] += jnp.sum(jnp.sum(e * e, axis=0, keepdims=True), axis=1, keepdims=True)
        else:
            h_ref[...] = h

    sh = pl.BlockSpec((NSH, tm, FF_S), lambda i: (0, i, 0))
    act = jax.ShapeDtypeStruct((NSH, t, FF_S), BF16)
    return pl.pallas_call(
        body, grid=(t // tm,), name=name,
        in_specs=[_row_spec(tm, D), _vec_spec(D), _WHOLE, _WHOLE, _WHOLE, _vec_spec(D)] + [_row_spec(tm, D)] * last,
        out_specs=[_row_spec(tm, D), sh, sh, sh, _row_spec(tm, D)] + [pl.BlockSpec((1, 128), lambda i: (0, 0))] * last,
        out_shape=[jax.ShapeDtypeStruct((t, D), F32), act, act, act, jax.ShapeDtypeStruct((t, D), F32)]
        + [jax.ShapeDtypeStruct((1, 128), F32)] * last,
        compiler_params=_params("arbitrary"),
    )(x, gpre, w1g, w3g, w2g, gpost, *([target] if last else []))


def _mix_proj(h1, gmix, w_in_g, w_gate_g, b_gate):
    t = h1.shape[0]
    tm = _tile(t)

    def body(h_ref, g_ref, win_ref, wg_ref, bg_ref, u_ref, q_ref, k_ref, v_ref, xr_ref, xg_ref, gate_ref):
        ub = _rms(h_ref[...], g_ref[...]).astype(BF16)
        u_ref[...] = ub
        p0 = _nn(ub, win_ref[0])
        q_ref[:, 0:896] = p0.astype(BF16)
        p1 = _nn(ub, win_ref[1])
        q_ref[:, 896:1024] = p1[:, 0:128].astype(BF16)
        k_ref[...] = p1[:, 128:384].astype(BF16)
        v_ref[...] = p1[:, 384:640].astype(BF16)
        xr_ref[:, 0:256] = p1[:, 640:896]
        p2 = _nn(ub, win_ref[2])
        xr_ref[:, 256:1024] = p2[:, 0:768]
        xg_ref[:, 0:128] = p2[:, 768:896]
        xg_ref[:, 128:1024] = _nn(ub, win_ref[3])
        for s in range(NSH):
            sl = slice(s * GATE_S, (s + 1) * GATE_S)
            gate_ref[:, sl] = jax.nn.sigmoid(_nn(ub, wg_ref[s]) + bg_ref[:, sl])

    return pl.pallas_call(
        body, grid=(t // tm,), name="mix_proj",
        in_specs=[_row_spec(tm, D), _vec_spec(D), _WHOLE, _WHOLE, _vec_spec(2 * D)],
        out_specs=[_row_spec(tm, D), _row_spec(tm, D), _row_spec(tm, KV_W), _row_spec(tm, KV_W), _row_spec(tm, D),
                   _row_spec(tm, D), _row_spec(tm, 2 * D)],
        out_shape=[jax.ShapeDtypeStruct((t, D), BF16), jax.ShapeDtypeStruct((t, D), BF16),
                   jax.ShapeDtypeStruct((t, KV_W), BF16), jax.ShapeDtypeStruct((t, KV_W), BF16),
                   jax.ShapeDtypeStruct((t, D), F32), jax.ShapeDtypeStruct((t, D), F32),
                   jax.ShapeDtypeStruct((t, 2 * D), F32)],
        compiler_params=_params("arbitrary"),
    )(h1, gmix, w_in_g, w_gate_g, b_gate)


def _rglru_fwd(xr, xg, conv_w, conv_b, wa2, ba, wx2, bx, lam, after=None):
    t = xr.shape[0]
    tm = _tile(t, TM_SCAN)
    nb8 = tm // 8

    def body(xr_ref, xrp_ref, xg_ref, cw_ref, cb_ref, wa_ref, ba_ref, wx_ref, bx_ref, lam_ref,
             hr_ref, yain_ref, xc_ref, r_ref, ig_ref, a_sc, s_ref, ext, h_sc):
        i = pl.program_id(0)

        @pl.when(i == 0)
        def _():
            h_sc[...] = jnp.zeros_like(h_sc)

        ext[0:8, :] = jnp.where(i == 0, 0.0, xrp_ref[...])
        ext[8:8 + tm, :] = xr_ref[...]
        xc = jnp.broadcast_to(cb_ref[...], (tm, D))
        for tap in range(4):
            xc = xc + ext[pl.ds(5 + tap, tm), :] * cw_ref[tap:tap + 1, :]
        xc_ref[...] = xc
        xcb = xc.astype(BF16)
        for p in range(8):
            sl = slice(p * 128, (p + 1) * 128)
            r_ref[:, sl] = jax.nn.sigmoid(_nn(xcb[:, sl], wa_ref[p]) + ba_ref[:, sl])
            ig_ref[:, sl] = jax.nn.sigmoid(_nn(xcb[:, sl], wx_ref[p]) + bx_ref[:, sl])
        a, s = _lru_coeffs(r_ref[...], _softplus_neg(lam_ref[...]))
        a_sc[...] = a
        s_ref[...] = s
        hr_ref[...] = s * (ig_ref[...] * xc)

        def blk(j, h):
            st = pl.multiple_of(j * 8, 8)
            a8 = a_sc[pl.ds(st, 8), :]
            u8 = hr_ref[pl.ds(st, 8), :]
            rows = []
            for k in range(8):
                h = a8[k:k + 1, :] * h + u8[k:k + 1, :]
                rows.append(h)
            hr_ref[pl.ds(st, 8), :] = jnp.concatenate(rows, axis=0)
            return h

        h_sc[0:1, :] = lax.fori_loop(0, nb8, blk, h_sc[0:1, :])
        yain_ref[...] = (hr_ref[...] * _gelu(xg_ref[...])).astype(BF16)

    prev = pl.BlockSpec((8, D), lambda i: (jnp.maximum(i * nb8 - 1, 0), 0))
    full = lambda shape: pl.BlockSpec(shape, lambda i: tuple(0 for _ in shape))
    f32 = jax.ShapeDtypeStruct((t, D), F32)
    body, specs, operands = _behind(body, after)
    return pl.pallas_call(
        body, grid=(t // tm,), name="rglru_fwd",
        in_specs=specs + [_row_spec(tm, D), prev, _row_spec(tm, D), full((4, D)), _vec_spec(D), full((8, 128, 128)),
                          _vec_spec(D), full((8, 128, 128)), _vec_spec(D), _vec_spec(D)],
        out_specs=[_row_spec(tm, D)] * 7,
        out_shape=[f32, jax.ShapeDtypeStruct((t, D), BF16), f32, f32, f32, f32, f32],
        scratch_shapes=[pltpu.VMEM((tm + 8, D), F32), pltpu.VMEM((8, D), F32)],
        compiler_params=_params("arbitrary"),
    )(*operands, xr, xr, xg, conv_w, conv_b, wa2, ba, wx2, bx, lam)


def _bias_fwd(table_t, onehot_t):
    def body(t_ref, e_ref, o_ref):
        o_ref[...] = jnp.dot(t_ref[...], e_ref[...], preferred_element_type=F32, precision=lax.Precision.HIGHEST)

    return pl.pallas_call(body, out_shape=jax.ShapeDtypeStruct((N_HEADS, CHUNK * KB), F32), name="bias_fwd",
                          compiler_params=_params())(table_t, onehot_t)


def _bias_bwd(dbias_flat, onehot_t, ds_rows):
    def body(d_ref, e_ref, s_ref, o_ref, so_ref):
        o_ref[...] = lax.dot_general(d_ref[...], e_ref[...], (((1,), (1,)), ((), ())), preferred_element_type=F32,
                                     precision=lax.Precision.HIGHEST)
        so_ref[...] = jnp.zeros_like(so_ref)
        for r in range(4):
            so_ref[:, r:r + 1] = jnp.sum(s_ref[:, r * CHUNK:(r + 1) * CHUNK], axis=1, keepdims=True)

    return pl.pallas_call(body, out_shape=[jax.ShapeDtypeStruct((N_HEADS, N_BUCKETS), F32), jax.ShapeDtypeStruct((8, 128), F32)],
                          name="bias_bwd", compiler_params=_params())(dbias_flat, onehot_t, ds_rows)


def _stack_heads(q):
    return jnp.concatenate(
        [jnp.concatenate([q[:, (4 * g + r) * HEAD_DIM:(4 * g + r + 1) * HEAD_DIM] for g in range(4)], axis=1)
         for r in range(4)], axis=0)


def _unstack_heads(o):
    return jnp.concatenate([o[r * CHUNK:(r + 1) * CHUNK, g * HEAD_DIM:(g + 1) * HEAD_DIM] for g in range(4) for r in range(4)],
                           axis=1)


def _block_diag(w, mask):
    return jnp.concatenate([w] * 4, axis=0) * mask


def _group_softmax(qk, bias_g, sink, valid):
    s = qk * (HEAD_DIM ** -0.5) + bias_g
    s = jnp.where(valid, s, NEG_INF)
    m = jnp.maximum(jnp.max(s, axis=0, keepdims=True), sink)
    e = jnp.exp(s - m)
    es = jnp.exp(sink - m)
    inv = 1.0 / (jnp.sum(e, axis=0, keepdims=True) + es)
    return e * inv, es * inv


def _attn_fwd(sink_rows, q, kp, vp, bias_t, mask, after=None):
    t = q.shape[0]
    per_step = 4

    def body(sink_ref, q_ref, kp_ref, vp_ref, bias_ref, mask_ref, o_ref):
        owns = [mask_ref[g * KP:(g + 1) * KP, :] for g in range(4)]
        for k in range(per_step):
            c = pl.program_id(0) * per_step + k
            rows = slice(k * CHUNK, (k + 1) * CHUNK)
            st = pl.multiple_of(c * CHUNK, CHUNK)
            kw = kp_ref[pl.ds(st, KP), :]
            vw = vp_ref[pl.ds(st, KP), :]
            q_all = _stack_heads(q_ref[rows, :])
            valid = lax.broadcasted_iota(jnp.int32, (KP, 1), 0) + c * CHUNK >= PAD_KEYS
            scores = [_nt(kw * owns[g], q_all) for g in range(4)]
            ps = [_group_softmax(scores[g], bias_ref[g * KP:(g + 1) * KP, :], sink_ref[g:g + 1, :], valid)[0]
                  for g in range(4)]
            o_all = sum(_tn(ps[g].astype(BF16), vw * owns[g]) for g in range(4))
            o_ref[rows, :] = _unstack_heads(o_all).astype(BF16)

    body, specs, operands = _behind(body, after)
    return pl.pallas_call(
        body, grid=(t // (per_step * CHUNK),), name="attn_fwd",
        in_specs=specs + [_WHOLE, _row_spec(per_step * CHUNK, D), _WHOLE, _WHOLE, _WHOLE, _WHOLE],
        out_specs=_row_spec(per_step * CHUNK, D),
        out_shape=jax.ShapeDtypeStruct((t, D), BF16),
        compiler_params=_params("arbitrary"),
    )(*operands, sink_rows, q, kp, vp, bias_t, mask)


def _merge_fwd(yain, o, gate, h1, w_lru, w_att, w_o, gpost):
    t = h1.shape[0]
    tm = _tile(t)

    def body(ya_ref, o_ref, g_ref, h_ref, wl_ref, wa_ref, wo_ref, gp_ref, h2_ref, mo_ref, mg_ref, ya_out, yb_out):
        ya = _nn(ya_ref[...], wl_ref[...])
        yb = _nn(o_ref[...], wa_ref[...])
        g0 = g_ref[:, 0:D]
        g1 = g_ref[:, D:2 * D]
        mg = (g0 * ya + g1 * yb).astype(BF16)
        mo = _nn(mg, wo_ref[...])
        ya_out[...] = (ya * (g0 * (1.0 - g0))).astype(BF16)
        yb_out[...] = (yb * (g1 * (1.0 - g1))).astype(BF16)
        mg_ref[...] = mg
        mo_ref[...] = mo
        h2_ref[...] = h_ref[...] + _rms(mo, gp_ref[...])

    f32 = jax.ShapeDtypeStruct((t, D), F32)
    b16 = jax.ShapeDtypeStruct((t, D), BF16)
    return pl.pallas_call(
        body, grid=(t // tm,), name="merge_fwd",
        in_specs=[_row_spec(tm, D), _row_spec(tm, D), _row_spec(tm, 2 * D), _row_spec(tm, D), _WHOLE, _WHOLE, _WHOLE,
                  _vec_spec(D)],
        out_specs=[_row_spec(tm, D)] * 5,
        out_shape=[f32, f32, b16, b16, b16],
        compiler_params=_params("arbitrary"),
    )(yain, o, gate, h1, w_lru, w_att, w_o, gpost)


def _ffn_bwd(dh, x, f, a, b, gpre, gpost, w1g, w3g, w2g, name):
    t = x.shape[0]
    tm = _tile(t, TM_SCAN)

    def body(dh_ref, x_ref, f_ref, a_ref, b_ref, gpre_ref, gpost_ref, w1_ref, w3_ref, w2_ref,
             dx_ref, n_ref, da_ref, db_ref, df_ref, dgpre_ref, dgpost_ref):
        @pl.when(pl.program_id(0) == 0)
        def _():
            dgpre_ref[...] = jnp.zeros_like(dgpre_ref)
            dgpost_ref[...] = jnp.zeros_like(dgpost_ref)

        dhv = dh_ref[...]
        xv = x_ref[...]
        df, dgp = _rms_bwd(0.5 * dhv, f_ref[...], gpost_ref[...])
        dgpost_ref[...] += dgp
        dfb = df.astype(BF16)
        df_ref[...] = dfb
        n_ref[...] = _rms(xv, gpre_ref[...]).astype(BF16)
        dn = jnp.zeros((tm, D), F32)
        for s in range(NSH):
            av = a_ref[s].astype(F32)
            bv = b_ref[s].astype(F32)
            sg = jax.nn.sigmoid(av)
            dhm = _nt(dfb, w2_ref[s])
            dab = (dhm * bv * (sg * (1.0 + av * (1.0 - sg)))).astype(BF16)
            dbb = (dhm * (av * sg)).astype(BF16)
            da_ref[s] = dab
            db_ref[s] = dbb
            dn = dn + _nn(dab, w1_ref[s]) + _nn(dbb, w3_ref[s])
        dxn, dg = _rms_bwd(dn, xv, gpre_ref[...])
        dgpre_ref[...] += dg
        dx_ref[...] = dhv + dxn

    sh = pl.BlockSpec((NSH, tm, FF_S), lambda i: (0, i, 0))
    act = jax.ShapeDtypeStruct((NSH, t, FF_S), BF16)
    vec = jax.ShapeDtypeStruct((1, D), F32)
    return pl.pallas_call(
        body, grid=(t // tm,), name=name,
        in_specs=[_row_spec(tm, D), _row_spec(tm, D), _row_spec(tm, D), sh, sh, _vec_spec(D), _vec_spec(D), _WHOLE, _WHOLE,
                  _WHOLE],
        out_specs=[_row_spec(tm, D), _row_spec(tm, D), sh, sh, _row_spec(tm, D), _vec_spec(D), _vec_spec(D)],
        out_shape=[jax.ShapeDtypeStruct((t, D), F32), jax.ShapeDtypeStruct((t, D), BF16), act, act,
                   jax.ShapeDtypeStruct((t, D), BF16), vec, vec],
        compiler_params=_params("arbitrary"),
    )(dh, x, f, a, b, gpre, gpost, w1g, w3g, w2g)


def _behind(body, after):
    if after is None:
        return body, [], []

    def ordered(_, *refs):
        body(*refs)

    return ordered, [_ANY], [after]


def _ffn_bwd_acts(dh, x, f, a, b, gpre, gpost, w2g, name):
    t = x.shape[0]
    tm = _tile(t)

    def body(dh_ref, x_ref, f_ref, a_ref, b_ref, gpre_ref, gpost_ref, w2_ref, n_ref, da_ref, db_ref, df_ref, dgpost_ref):
        @pl.when(pl.program_id(0) == 0)
        def _():
            dgpost_ref[...] = jnp.zeros_like(dgpost_ref)

        df, dgp = _rms_bwd(0.5 * dh_ref[...], f_ref[...], gpost_ref[...])
        dgpost_ref[...] += dgp
        dfb = df.astype(BF16)
        df_ref[...] = dfb
        n_ref[...] = _rms(x_ref[...], gpre_ref[...]).astype(BF16)
        for s in range(NSH):
            av = a_ref[s].astype(F32)
            bv = b_ref[s].astype(F32)
            sg = jax.nn.sigmoid(av)
            dhm = _nt(dfb, w2_ref[s])
            da_ref[s] = (dhm * bv * (sg * (1.0 + av * (1.0 - sg)))).astype(BF16)
            db_ref[s] = (dhm * (av * sg)).astype(BF16)

    sh = pl.BlockSpec((NSH, tm, FF_S), lambda i: (0, i, 0))
    act = jax.ShapeDtypeStruct((NSH, t, FF_S), BF16)
    b16 = jax.ShapeDtypeStruct((t, D), BF16)
    return pl.pallas_call(
        body, grid=(t // tm,), name=name,
        in_specs=[_row_spec(tm, D), _row_spec(tm, D), _row_spec(tm, D), sh, sh, _vec_spec(D), _vec_spec(D), _WHOLE],
        out_specs=[_row_spec(tm, D), sh, sh, _row_spec(tm, D), _vec_spec(D)],
        out_shape=[b16, act, act, b16, jax.ShapeDtypeStruct((1, D), F32)],
        compiler_params=_params("arbitrary"),
    )(dh, x, f, a, b, gpre, gpost, w2g)


def _ffn_bwd_input(dh, x, da, db, gpre, w1g, w3g, name, after):
    t = x.shape[0]
    tm = _tile(t)

    def body(dh_ref, x_ref, da_ref, db_ref, gpre_ref, w1_ref, w3_ref, dx_ref, dgpre_ref):
        @pl.when(pl.program_id(0) == 0)
        def _():
            dgpre_ref[...] = jnp.zeros_like(dgpre_ref)

        dn = jnp.zeros((tm, D), F32)
        for s in range(NSH):
            dn = dn + _nn(da_ref[s], w1_ref[s]) + _nn(db_ref[s], w3_ref[s])
        dxn, dg = _rms_bwd(dn, x_ref[...], gpre_ref[...])
        dgpre_ref[...] += dg
        dx_ref[...] = dh_ref[...] + dxn

    sh = pl.BlockSpec((NSH, tm, FF_S), lambda i: (0, i, 0))
    body, specs, operands = _behind(body, after)
    return pl.pallas_call(
        body, grid=(t // tm,), name=name,
        in_specs=specs + [_row_spec(tm, D), _row_spec(tm, D), sh, sh, _vec_spec(D), _WHOLE, _WHOLE],
        out_specs=[_row_spec(tm, D), _vec_spec(D)],
        out_shape=[jax.ShapeDtypeStruct((t, D), F32), jax.ShapeDtypeStruct((1, D), F32)],
        compiler_params=_params("arbitrary"),
    )(*operands, dh, x, da, db, gpre, w1g, w3g)


def _wgrad(a, b, a_spec, b_spec, out_spec, out_shape, grid, name, after=None):
    def body(a_ref, b_ref, o_ref):
        o_ref[...] = _tn(a_ref[...], b_ref[...]).astype(BF16)

    body, specs, operands = _behind(body, after)
    return pl.pallas_call(body, grid=grid, name=name, in_specs=specs + [a_spec, b_spec], out_specs=out_spec,
                          out_shape=jax.ShapeDtypeStruct(out_shape, BF16),
                          compiler_params=_params(*("arbitrary",) * len(grid)))(*operands, a, b)


def _wgrad_cols(act, dsh, width, name, after=None):
    t = act.shape[0]
    if dsh.ndim == 3:
        b_spec = pl.BlockSpec((None, t, width), lambda s, k: (s, 0, 0))
    else:
        b_spec = pl.BlockSpec((t, width), lambda s, k: (0, s))
    return _wgrad(act, dsh, pl.BlockSpec((t, 512), lambda s, k: (0, k)), b_spec,
                  pl.BlockSpec((None, 512, width), lambda s, k: (s, k, 0)), (NSH, D, width), (NSH, 2), name, after)


def _wgrad_rows(hm, df, name, after=None):
    t = df.shape[0]
    return _wgrad(hm, df, pl.BlockSpec((None, t, FF_S), lambda s: (s, 0, 0)), pl.BlockSpec((t, D), lambda s: (0, 0)),
                  pl.BlockSpec((None, FF_S, D), lambda s: (s, 0, 0)), (NSH, FF_S, D), (NSH,), name, after)


def _wgrad_sq(a, b, name, after=None):
    t = a.shape[0]
    return _wgrad(a, b, pl.BlockSpec((t, 512), lambda i, j: (0, i)), pl.BlockSpec((t, 512), lambda i, j: (0, j)),
                  pl.BlockSpec((512, 512), lambda i, j: (i, j)), (D, D), (2, 2), name, after)


def _mix_bwd1(dh2, mo, gpost, gate, ya, yb, xg, hr, w_o, w_lru, w_att, after):
    t = dh2.shape[0]
    tm = _tile(t, TM_SCAN)

    def body(dh_ref, mo_ref, gp_ref, g_ref, ya_ref, yb_ref, xg_ref, hr_ref, wo_ref, wl_ref, wa_ref,
             dmo_ref, dya_ref, dyb_ref, dgate_ref, dhr_ref, dxg_ref, do_ref, dgp_ref, dbg_ref):
        @pl.when(pl.program_id(0) == 0)
        def _():
            dgp_ref[...] = jnp.zeros_like(dgp_ref)
            dbg_ref[...] = jnp.zeros_like(dbg_ref)

        dmo, dgp = _rms_bwd(dh_ref[...], mo_ref[...], gp_ref[...])
        dgp_ref[...] += dgp
        dmob = dmo.astype(BF16)
        dmo_ref[...] = dmob
        dm = _nt(dmob, wo_ref[...])
        g0 = g_ref[:, 0:D]
        g1 = g_ref[:, D:2 * D]
        dyab = (dm * g0).astype(BF16)
        dybb = (dm * g1).astype(BF16)
        dya_ref[...] = dyab
        dyb_ref[...] = dybb
        dg0 = dm * ya_ref[...].astype(F32)
        dg1 = dm * yb_ref[...].astype(F32)
        dgate_ref[:, 0:D] = dg0.astype(BF16)
        dgate_ref[:, D:2 * D] = dg1.astype(BF16)
        dbg_ref[:, 0:D] += jnp.sum(dg0, axis=0, keepdims=True)
        dbg_ref[:, D:2 * D] += jnp.sum(dg1, axis=0, keepdims=True)
        dyain = _nt(dyab, wl_ref[...])
        do_ref[...] = _nt(dybb, wa_ref[...]).astype(BF16)
        xgv = xg_ref[...]
        gelu, gelu_grad = _gelu_and_grad(xgv)
        dhr_ref[...] = dyain * gelu
        dxg_ref[...] = (dyain * hr_ref[...] * gelu_grad).astype(BF16)

    b16 = jax.ShapeDtypeStruct((t, D), BF16)
    body, specs, operands = _behind(body, after)
    return pl.pallas_call(
        body, grid=(t // tm,), name="mix_bwd1",
        in_specs=specs + [_row_spec(tm, D), _row_spec(tm, D), _vec_spec(D), _row_spec(tm, 2 * D), _row_spec(tm, D),
                          _row_spec(tm, D), _row_spec(tm, D), _row_spec(tm, D), _WHOLE, _WHOLE, _WHOLE],
        out_specs=[_row_spec(tm, D), _row_spec(tm, D), _row_spec(tm, D), _row_spec(tm, 2 * D), _row_spec(tm, D),
                   _row_spec(tm, D), _row_spec(tm, D), _vec_spec(D), _vec_spec(2 * D)],
        out_shape=[b16, b16, b16, jax.ShapeDtypeStruct((t, 2 * D), BF16), jax.ShapeDtypeStruct((t, D), F32), b16, b16,
                   jax.ShapeDtypeStruct((1, D), F32), jax.ShapeDtypeStruct((1, 2 * D), F32)],
        compiler_params=_params("arbitrary"),
    )(*operands, dh2, mo, gpost, gate, ya, yb, xg, hr, w_o, w_lru, w_att)


def _rglru_bwd(dhr, hr, xc, r, ig, a, s, xr, conv_w, wa2, wx2, lam, after):
    t = dhr.shape[0]
    tm = _tile(t, TM_SCAN)
    nb8 = tm // 8
    nt = t // tm

    def body(dhr_ref, hr_ref, hrp_ref, xc_ref, r_ref, ig_ref, a_sc, s_ref, xr_ref, cw_ref, wa_ref, wx_ref, lam_ref,
             dxr_ref, dwa_ref, dwx_ref, dba_ref, dbx_ref, dlam_ref, dcw_ref, dcb_ref,
             ext_h, ext_d, g_sc, c_sc, nxt_sc):
        i = pl.program_id(0)
        first_tile = i == nt - 1

        @pl.when(i == 0)
        def _():
            c_sc[...] = jnp.zeros_like(c_sc)
            nxt_sc[...] = jnp.zeros_like(nxt_sc)
            for ref in (dwa_ref, dwx_ref, dba_ref, dbx_ref, dlam_ref, dcw_ref, dcb_ref):
                ref[...] = jnp.zeros_like(ref)

        lamv = lam_ref[...]
        sp = _softplus_neg(lamv)
        rv = r_ref[...]
        igv = ig_ref[...]
        xcv = xc_ref[...]
        a = a_sc[...]
        s = s_ref[...]

        def blk(jj, c):
            st = pl.multiple_of((nb8 - 1 - jj) * 8, 8)
            d8 = dhr_ref[pl.ds(st, 8), :]
            a8 = a_sc[pl.ds(st, 8), :]
            rows = [None] * 8
            for k in range(7, -1, -1):
                g = d8[k:k + 1, :] + c
                c = a8[k:k + 1, :] * g
                rows[k] = g
            g_sc[pl.ds(st, 8), :] = jnp.concatenate(rows, axis=0)
            return c

        c_sc[0:1, :] = lax.fori_loop(0, nb8, blk, c_sc[0:1, :])
        g = g_sc[...]
        ext_h[0:8, :] = jnp.where(first_tile, 0.0, hrp_ref[...])
        ext_h[8:8 + tm, :] = hr_ref[...]
        hprev = ext_h[pl.ds(7, tm), :]
        d_s = g * (igv * xcv)
        dig = g * s * xcv
        dxc = g * s * igv
        dla = (g * hprev) * a - d_s * ((a * a) / s)
        dr_pre = (dla * (-LRU_C * sp)) * (rv * (1.0 - rv))
        di_pre = dig * (igv * (1.0 - igv))
        dlam_ref[...] += jnp.sum(dla * (LRU_C * rv), axis=0, keepdims=True) * jax.nn.sigmoid(-lamv)
        dba_ref[...] += jnp.sum(dr_pre, axis=0, keepdims=True)
        dbx_ref[...] += jnp.sum(di_pre, axis=0, keepdims=True)
        drb = dr_pre.astype(BF16)
        dib = di_pre.astype(BF16)
        xcb = xcv.astype(BF16)
        ext_d[tm:tm + 8, :] = nxt_sc[...]
        for p in range(8):
            sl = slice(p * 128, (p + 1) * 128)
            ext_d[0:tm, sl] = dxc[:, sl] + _nt(drb[:, sl], wa_ref[p]) + _nt(dib[:, sl], wx_ref[p])
            dwa_ref[p] += _tn(xcb[:, sl], drb[:, sl])
            dwx_ref[p] += _tn(xcb[:, sl], dib[:, sl])
        dxcv = ext_d[0:tm, :]
        nxt_sc[...] = ext_d[0:8, :]
        dcb_ref[...] += jnp.sum(dxcv, axis=0, keepdims=True)
        xrv = xr_ref[...]
        dxr = jnp.zeros((tm, D), F32)
        for tap in range(4):
            ext_h[0:tm, :] = ext_d[pl.ds(3 - tap, tm), :]
            ahead = ext_h[0:tm, :]
            dxr = dxr + ahead * cw_ref[tap:tap + 1, :]
            dcw_ref[tap:tap + 1, :] += jnp.sum(ahead * xrv, axis=0, keepdims=True)
        dxr_ref[...] = dxr.astype(BF16)

    rev = pl.BlockSpec((tm, D), lambda i: (nt - 1 - i, 0))
    prev = pl.BlockSpec((8, D), lambda i: (jnp.maximum((nt - 1 - i) * nb8 - 1, 0), 0))
    full = lambda shape: pl.BlockSpec(shape, lambda i: tuple(0 for _ in shape))
    vec = jax.ShapeDtypeStruct((1, D), F32)
    blocks = jax.ShapeDtypeStruct((8, 128, 128), F32)
    body, specs, operands = _behind(body, after)
    return pl.pallas_call(
        body, grid=(nt,), name="rglru_bwd",
        in_specs=specs + [rev, rev, prev, rev, rev, rev, rev, rev, rev, full((4, D)), full((8, 128, 128)),
                          full((8, 128, 128)), _vec_spec(D)],
        out_specs=[rev, full((8, 128, 128)), full((8, 128, 128)), _vec_spec(D), _vec_spec(D), _vec_spec(D), full((4, D)),
                   _vec_spec(D)],
        out_shape=[jax.ShapeDtypeStruct((t, D), BF16), blocks, blocks, vec, vec, vec, jax.ShapeDtypeStruct((4, D), F32), vec],
        scratch_shapes=[pltpu.VMEM((tm + 8, D), F32), pltpu.VMEM((tm + 8, D), F32),
                        pltpu.VMEM((tm, D), F32), pltpu.VMEM((8, D), F32), pltpu.VMEM((8, D), F32)],
        compiler_params=_params("arbitrary"),
    )(*operands, dhr, hr, hr, xc, r, ig, a, s, xr, conv_w, wa2, wx2, lam)


def _attn_bwd(sink_rows, q, kp, vp, bias_t, mask, do):
    t = q.shape[0]
    tp = kp.shape[0]
    per_step = 4

    def body(sink_ref, q_ref, kp_ref, vp_ref, bias_ref, mask_ref, do_ref, dq_ref, dk_ref, dv_ref, dbias_ref, ds_ref):
        @pl.when(pl.program_id(0) == 0)
        def _():
            for ref in (dk_ref, dv_ref, dbias_ref, ds_ref):
                ref[...] = jnp.zeros_like(ref)

        maskv = mask_ref[...]
        lane_group = lax.broadcasted_iota(jnp.int32, (1, 4 * HEAD_DIM), 1) // HEAD_DIM

        def own_blocks(full):
            out = full[0:KP]
            for g in range(1, 4):
                out = jnp.where(lane_group == g, full[g * KP:(g + 1) * KP], out)
            return out

        dsc_sum, dsinks, dks, dvs = 0.0, [0.0] * 4, [], []
        for k in range(per_step):
            c = pl.program_id(0) * per_step + k
            chunk = slice(k * CHUNK, (k + 1) * CHUNK)
            st = pl.multiple_of(c * CHUNK, CHUNK)
            kbd = _block_diag(kp_ref[pl.ds(st, KP), :], maskv)
            vbd = _block_diag(vp_ref[pl.ds(st, KP), :], maskv)
            q_all = _stack_heads(q_ref[chunk, :])
            do_all = _stack_heads(do_ref[chunk, :])
            valid = lax.broadcasted_iota(jnp.int32, (KP, 1), 0) + c * CHUNK >= PAD_KEYS
            qk = _nt(kbd, q_all)
            dp = _nt(vbd, do_all)
            ps, dscs = [], []
            for g in range(4):
                rows = slice(g * KP, (g + 1) * KP)
                p, sink_p = _group_softmax(qk[rows], bias_ref[rows, :], sink_ref[g:g + 1, :], valid)
                delta = jnp.sum(p * dp[rows], axis=0, keepdims=True)
                ps.append(p)
                dscs.append(p * (dp[rows] - delta))
                dsinks[g] = dsinks[g] - sink_p * delta
            dsc = jnp.concatenate(dscs, axis=0)
            dsc_sum = dsc_sum + dsc
            dsb = (dsc * (HEAD_DIM ** -0.5)).astype(BF16)
            dq_ref[chunk, :] = _unstack_heads(_tn(dsb, kbd)).astype(BF16)
            dks.append((st, own_blocks(_nn(dsb, q_all))))
            dvs.append((st, own_blocks(_nn(jnp.concatenate(ps, axis=0).astype(BF16), do_all))))
        dbias_ref[...] += dsc_sum
        for g in range(4):
            ds_ref[g:g + 1, :] += dsinks[g]
        for (st, dkw), (_, dvw) in zip(dks, dvs):
            dk_ref[pl.ds(st, KP), :] += dkw
            dv_ref[pl.ds(st, KP), :] += dvw

    full = lambda shape: pl.BlockSpec(shape, lambda i: tuple(0 for _ in shape))
    return pl.pallas_call(
        body, grid=(t // (per_step * CHUNK),), name="attn_bwd",
        in_specs=[_WHOLE, _row_spec(per_step * CHUNK, D), _WHOLE, _WHOLE, _WHOLE, _WHOLE, _row_spec(per_step * CHUNK, D)],
        out_specs=[_row_spec(per_step * CHUNK, D), full((tp, KV_W)), full((tp, KV_W)), full((4 * KP, 4 * CHUNK)),
                   full((8, 4 * CHUNK))],
        out_shape=[jax.ShapeDtypeStruct((t, D), BF16), jax.ShapeDtypeStruct((tp, KV_W), F32),
                   jax.ShapeDtypeStruct((tp, KV_W), F32), jax.ShapeDtypeStruct((4 * KP, 4 * CHUNK), F32),
                   jax.ShapeDtypeStruct((8, 4 * CHUNK), F32)],
        compiler_params=_params("arbitrary"),
    )(sink_rows, q, kp, vp, bias_t, mask, do)


def _mix_bwd2(dproj, dgate, h1, dh2, gmix, w_in_g, w_gate_g, after):
    t = h1.shape[0]
    tm = _tile(t)

    def body(dp_ref, dg_ref, h_ref, dh_ref, g_ref, win_ref, wg_ref, dh1_ref, dgm_ref):
        @pl.when(pl.program_id(0) == 0)
        def _():
            dgm_ref[...] = jnp.zeros_like(dgm_ref)

        du = jnp.zeros((tm, D), F32)
        for s in range(NSH):
            du = du + _nt(dp_ref[:, s * IN_S:(s + 1) * IN_S], win_ref[s])
            du = du + _nt(dg_ref[:, s * GATE_S:(s + 1) * GATE_S], wg_ref[s])
        dxn, dg = _rms_bwd(du, h_ref[...], g_ref[...])
        dgm_ref[...] += dg
        dh1_ref[...] = dh_ref[...] + dxn

    body, specs, operands = _behind(body, after)
    return pl.pallas_call(
        body, grid=(t // tm,), name="mix_bwd2",
        in_specs=specs + [_row_spec(tm, NSH * IN_S), _row_spec(tm, 2 * D), _row_spec(tm, D), _row_spec(tm, D), _vec_spec(D),
                          _WHOLE, _WHOLE],
        out_specs=[_row_spec(tm, D), _vec_spec(D)],
        out_shape=[jax.ShapeDtypeStruct((t, D), F32), jax.ShapeDtypeStruct((1, D), F32)],
        compiler_params=_params("arbitrary"),
    )(*operands, dproj, dgate, h1, dh2, gmix, w_in_g, w_gate_g)


def _band_onehot():
    nb = N_BUCKETS // 2
    max_exact = nb // 2
    rel = jnp.arange(KB)[None, :] - PAD_KEYS - jnp.arange(CHUNK)[:, None]
    ret = jnp.where(rel > 0, nb, 0)
    n = jnp.abs(rel)
    nf = jnp.maximum(n, 1).astype(jnp.float32)
    large = max_exact + (jnp.log(nf / max_exact) / math.log(128 / max_exact) * (nb - max_exact)).astype(jnp.int32)
    large = jnp.minimum(large, nb - 1)
    buckets = (ret + jnp.where(n < max_exact, n, large)).reshape(1, CHUNK * KB)
    return (buckets == jnp.arange(N_BUCKETS)[:, None]).astype(F32)


def _pair_blocks(w):
    pairs = w.reshape(8, 2, 64, 64)
    z = jnp.zeros((8, 64, 64), w.dtype)
    return jnp.concatenate([jnp.concatenate([pairs[:, 0], z], axis=2), jnp.concatenate([z, pairs[:, 1]], axis=2)], axis=1)


def _unpair_blocks(w2):
    return jnp.stack([w2[:, 0:64, 0:64], w2[:, 64:128, 64:128]], axis=1).reshape(16, 64, 64)


def _local_step(x, target, weights, sm, reducer):
    row = lambda v: v.reshape(1, -1)
    wg = dict(weights("ffn1", x))
    sm = dict(sm, conv_w=wg["conv_w"])
    onehot_t = _band_onehot()
    bias = _bias_fwd(sm["rel_bias"].T, onehot_t).reshape(4, 4, CHUNK, KB)
    bias_t = jnp.pad(jnp.transpose(bias, (0, 3, 1, 2)), ((0, 0), (0, KP - KB), (0, 0), (0, 0))).reshape(4 * KP, 4 * CHUNK)
    sink_rows = jnp.pad(jnp.repeat(sm["attn_sinks"].reshape(4, 4), CHUNK, axis=1), ((0, 4), (0, 0)))
    grp = jnp.arange(4 * KP)[:, None] // KP == jnp.arange(4 * HEAD_DIM)[None, :] // HEAD_DIM
    mask = (grp & (jnp.arange(4 * KP)[:, None] % KP < KB)).astype(BF16)
    wa2 = _pair_blocks(sm["rg_a_w"]).astype(BF16)
    wx2 = _pair_blocks(sm["rg_x_w"]).astype(BF16)

    h1, a1, b1, hm1, f1 = _ffn_fwd(x, row(sm["ffn1_pre_g"]), wg["ffn1_w1"], wg["ffn1_w3"], wg["ffn1_w2"],
                                   row(sm["ffn1_post_g"]), "ffn1_fwd")
    wg.update(weights("mix_in", h1))
    u, q, k, v, xr, xg, gate = _mix_proj(h1, row(sm["mix_pre_g"]), wg["w_in"], wg["w_gate"], row(sm["b_gate"]))
    token = weights("mix_out", u, begin=True)
    hr, yain, xc, r, ig, lru_a, lru_s = _rglru_fwd(xr, xg, sm["conv_w"], row(sm["conv_b"]), wa2, row(sm["rg_a_b"]), wx2,
                                                   row(sm["rg_x_b"]), row(sm["lru_lambda"]), token)
    token = weights("ffn2", hr, begin=True)
    kp = jnp.pad(k, ((PAD_KEYS, KP - KB), (0, 0)))
    vp = jnp.pad(v, ((PAD_KEYS, KP - KB), (0, 0)))
    o = _attn_fwd(sink_rows, q, kp, vp, bias_t, mask, token)
    wg.update(weights("mix_out", o))
    w_lru = wg["w_lru_out"].reshape(D, D)
    w_att = wg["w_attn_out"].reshape(D, D)
    w_o = wg["w_o"].reshape(D, D)
    wg.update(weights("ffn2", o))
    h2, mo, merged, ya, yb = _merge_fwd(yain, o, gate, h1, w_lru, w_att, w_o, row(sm["mix_post_g"]))
    dy, a2, b2, hm2, f2, sq = _ffn_fwd(h2, row(sm["ffn2_pre_g"]), wg["ffn2_w1"], wg["ffn2_w3"], wg["ffn2_w2"],
                                       row(sm["ffn2_post_g"]), "ffn2_fwd", target)

    big, small = {}, {}
    dh2, n2, da2, db2, df2, small["ffn2_pre_g"], small["ffn2_post_g"] = _ffn_bwd(
        dy, h2, f2, a2, b2, row(sm["ffn2_pre_g"]), row(sm["ffn2_post_g"]), wg["ffn2_w1"], wg["ffn2_w3"], wg["ffn2_w2"],
        "ffn2_bwd")
    big["ffn2_w1"] = _wgrad_rows(da2, n2, "dw_ffn2_w1")
    big["ffn2_w3"] = _wgrad_rows(db2, n2, "dw_ffn2_w3")
    big["ffn2_w2"] = _wgrad_rows(hm2, df2, "dw_ffn2_w2")
    token = reducer.begin("ffn2", {n: big[n] for n in ("ffn2_w1", "ffn2_w3", "ffn2_w2")})
    dmo, dya, dyb, dgate, dhr, dxg, do, small["mix_post_g"], small["b_gate"] = _mix_bwd1(
        dh2, mo, row(sm["mix_post_g"]), gate, ya, yb, xg, hr, w_o, w_lru, w_att, token)
    big["w_o"] = _wgrad_sq(merged, dmo, "dw_w_o").reshape(NSH, D // NSH, D)
    big["w_lru_out"] = _wgrad_sq(yain, dya, "dw_w_lru_out").reshape(NSH, D // NSH, D)
    big["w_attn_out"] = _wgrad_sq(o, dyb, "dw_w_attn_out").reshape(NSH, D // NSH, D)
    token = reducer.advance("ffn2", big["w_attn_out"])
    (dxr, dwa2, dwx2, small["rg_a_b"], small["rg_x_b"], small["lru_lambda"], small["conv_w"], small["conv_b"]) = _rglru_bwd(
        dhr, hr, xc, r, ig, lru_a, lru_s, xr, sm["conv_w"], wa2, wx2, row(sm["lru_lambda"]), token)
    small["rg_a_w"] = _unpair_blocks(dwa2)
    small["rg_x_w"] = _unpair_blocks(dwx2)
    dq, dkp, dvp, dbias_t, ds_rows = _attn_bwd(sink_rows, q, kp, vp, bias_t, mask, do)
    dbias = jnp.transpose(dbias_t.reshape(4, KP, 4, CHUNK)[:, :KB], (0, 2, 3, 1)).reshape(N_HEADS, CHUNK * KB)
    drel_t, dsinks = _bias_bwd(dbias, onehot_t, ds_rows)
    small["attn_sinks"] = dsinks[0:4, 0:4].reshape(N_HEADS)
    small["rel_bias"] = drel_t.T
    t = x.shape[0]
    dproj = jnp.concatenate([dq, dkp[PAD_KEYS:PAD_KEYS + t].astype(BF16), dvp[PAD_KEYS:PAD_KEYS + t].astype(BF16), dxr, dxg],
                            axis=1)
    big["w_in"] = _wgrad_cols(u, dproj, IN_S, "dw_w_in")
    big["w_gate"] = _wgrad_cols(u, dgate, GATE_S, "dw_w_gate")
    token = reducer.begin("mix", {n: big[n] for n in ("w_in", "w_gate", "w_lru_out", "w_attn_out", "w_o")})
    dh1, small["mix_pre_g"] = _mix_bwd2(dproj, dgate, h1, dh2, row(sm["mix_pre_g"]), wg["w_in"], wg["w_gate"], token)
    n1, da1, db1, df1, small["ffn1_post_g"] = _ffn_bwd_acts(
        dh1, x, f1, a1, b1, row(sm["ffn1_pre_g"]), row(sm["ffn1_post_g"]), wg["ffn1_w2"], "ffn1_bwd_acts")
    token = reducer.advance("mix", df1)
    big["ffn1_w1"] = _wgrad_rows(da1, n1, "dw_ffn1_w1", token)
    big["ffn1_w3"] = _wgrad_rows(db1, n1, "dw_ffn1_w3", token)
    big["ffn1_w2"] = _wgrad_rows(hm1, df1, "dw_ffn1_w2", token)
    token = reducer.begin("ffn1", {n: big[n] for n in ("ffn1_w1", "ffn1_w3", "ffn1_w2")})
    dx, small["ffn1_pre_g"] = _ffn_bwd_input(dh1, x, da1, db1, row(sm["ffn1_pre_g"]), wg["ffn1_w1"], wg["ffn1_w3"],
                                             "ffn1_bwd_input", token)
    return sq, dx, big, small


_ANY = pl.BlockSpec(memory_space=pl.ANY)


def _place():
    return lax.axis_index("x"), lax.axis_index("y"), lax.axis_index("c")


def _other_chips(x, y):
    return [(1 - x, y), (x, 1 - y), (1 - x, 1 - y)]


_HBM = pl.BlockSpec(memory_space=pltpu.HBM)
_SEM = pl.BlockSpec(memory_space=pltpu.SEMAPHORE)
_EFFECT = pltpu.SideEffectType.DATAFLOW_SIDE_EFFECTING


def _cast_into_slot(w, chip, name, after=None):
    r, cc = w.shape
    rows = r // 4

    def body(chip_ref, *refs):
        w_ref, o_ref = refs[-2:]
        o_ref[...] = w_ref[...].astype(BF16)

    extra = [] if after is None else [after]
    return pl.pallas_call(
        body, name=name, out_shape=jax.ShapeDtypeStruct((NSH, r, cc), BF16),
        grid_spec=pltpu.PrefetchScalarGridSpec(
            num_scalar_prefetch=1, grid=(4,), in_specs=[_ANY] * len(extra) + [pl.BlockSpec((rows, cc), lambda i, chip: (i, 0))],
            out_specs=pl.BlockSpec((None, rows, cc), lambda i, chip: (chip[0], i, 0))),
        compiler_params=_params("arbitrary"))(chip, *extra, w)


def _piece(ref, slot, c):
    if ref.dtype == F32:
        return ref.at[slot]
    rh = ref.shape[1] // 2
    return ref.at[slot, pl.ds(pl.multiple_of(c * rh, 16), rh), :]


def _gather_start(stages, name):
    flat = [b for stage in stages for b in stage]
    n, ns = len(flat), len(stages)

    def body(*refs):
        ins, sems, token = refs[:n], refs[n:n + 2 * ns], refs[-1]
        x, y, c = _place()
        me = 2 * x + y
        k = 0
        for s, stage in enumerate(stages):
            for i in range(len(stage)):
                for j, (px, py) in enumerate(_other_chips(x, y)):
                    piece = _piece(ins[k], me, c)
                    pltpu.make_async_remote_copy(src_ref=piece, dst_ref=piece, send_sem=sems[2 * s].at[3 * i + j],
                                                 recv_sem=sems[2 * s + 1].at[3 * i + j], device_id=(px, py, c),
                                                 device_id_type=MESH).start()
                k += 1
        token[...] = jnp.zeros_like(token)

    sem_shapes = [pltpu.SemaphoreType.DMA((3 * len(stage),)) for stage in stages for _ in range(2)]
    outs = pl.pallas_call(
        body, name=name, in_specs=[_HBM] * n,
        out_specs=[_SEM] * (2 * ns) + [_HBM] * n + [pl.BlockSpec(memory_space=pltpu.VMEM)],
        out_shape=sem_shapes + [pltpu.HBM(b.shape, b.dtype) for b in flat] + [jax.ShapeDtypeStruct((8, 128), F32)],
        input_output_aliases={i: 2 * ns + i for i in range(n)},
        compiler_params=pltpu.CompilerParams(has_side_effects=_EFFECT),
    )(*[pltpu.with_memory_space_constraint(b, pltpu.HBM) for b in flat])
    sems, bufs, token = outs[:2 * ns], list(outs[2 * ns:2 * ns + n]), outs[-1]
    per_stage, k = [], 0
    for s, stage in enumerate(stages):
        per_stage.append((sems[2 * s], sems[2 * s + 1], bufs[k:k + len(stage)]))
        k += len(stage)
    return per_stage, token


def _gather_wait(send_sems, recv_sems, bufs, after, name):
    n = len(bufs)

    def body(*refs):
        ins, ssem, rsem = refs[:n], refs[n], refs[n + 1]
        x, y, c = _place()
        me = 2 * x + y
        for i in range(n):
            for j, (px, py) in enumerate(_other_chips(x, y)):
                cp = pltpu.make_async_remote_copy(src_ref=_piece(ins[i], me, c), dst_ref=_piece(ins[i], 2 * px + py, c),
                                                  send_sem=ssem.at[3 * i + j], recv_sem=rsem.at[3 * i + j],
                                                  device_id=(px, py, c), device_id_type=MESH)
                cp.wait_send()
                cp.wait_recv()

    return pl.pallas_call(
        body, name=name, in_specs=[_HBM] * n + [_SEM, _SEM, _ANY], out_specs=[_HBM] * n,
        out_shape=[pltpu.HBM(b.shape, b.dtype) for b in bufs], input_output_aliases={i: i for i in range(n)},
        compiler_params=pltpu.CompilerParams(has_side_effects=_EFFECT),
    )(*bufs, send_sems, recv_sems, after)


def _sibling_fill(bufs, name):
    n = len(bufs)

    def body(*refs):
        ins, outs = refs[:n], refs[n:2 * n]
        send_sems, recv_sems = refs[2 * n:]
        x, y, c = _place()
        copies = []
        for i in range(n):
            for j, (px, py) in enumerate(_other_chips(x, y)):
                copies.append(pltpu.make_async_remote_copy(
                    src_ref=_piece(ins[i], 2 * px + py, c), dst_ref=_piece(outs[i], 2 * px + py, c),
                    send_sem=send_sems.at[3 * i + j], recv_sem=recv_sems.at[3 * i + j], device_id=(x, y, 1 - c),
                    device_id_type=MESH))
                copies[-1].start()
        for cp in copies:
            cp.wait()

    return pl.pallas_call(
        body, name=name, in_specs=[_ANY] * n, out_specs=[_ANY] * n,
        out_shape=[jax.ShapeDtypeStruct(b.shape, b.dtype) for b in bufs], input_output_aliases={i: i for i in range(n)},
        scratch_shapes=[pltpu.SemaphoreType.DMA((3 * n,)), pltpu.SemaphoreType.DMA((3 * n,))],
        compiler_params=pltpu.CompilerParams(has_side_effects=True),
    )(*bufs)


def _swap_plan(srcs, lands):
    x, y, c = _place()
    plan = []
    for src, land in zip(srcs, lands):
        rh = src.shape[1] // 2
        plan.append((src.at[:, pl.ds(pl.multiple_of((1 - c) * rh, 16), rh), :], land, (x, y, 1 - c)))
    return plan


def _owners_plan(srcs, lands):
    x, y, c = _place()
    return [(src.at[2 * px + py], land.at[j], (px, py, c))
            for src, land in zip(srcs, lands) for j, (px, py) in enumerate(_other_chips(x, y))]


def _exchange_start(srcs, lands, plan, copies, name):
    n, m = len(srcs), len(srcs) + len(lands)

    def body(*refs):
        send_sems, recv_sems, token = refs[m], refs[m + 1], refs[-1]
        for k, (src, dst, dev) in enumerate(plan(refs[:n], refs[n:m])):
            pltpu.make_async_remote_copy(src_ref=src, dst_ref=dst, send_sem=send_sems.at[k], recv_sem=recv_sems.at[k],
                                         device_id=dev, device_id_type=MESH).start()
        token[...] = jnp.zeros_like(token)

    both = list(srcs) + list(lands)
    outs = pl.pallas_call(
        body, name=name, in_specs=[_HBM] * m,
        out_specs=[_SEM, _SEM] + [_HBM] * m + [pl.BlockSpec(memory_space=pltpu.VMEM)],
        out_shape=[pltpu.SemaphoreType.DMA((copies,)), pltpu.SemaphoreType.DMA((copies,))]
        + [pltpu.HBM(b.shape, b.dtype) for b in both] + [jax.ShapeDtypeStruct((8, 128), F32)],
        input_output_aliases={i: 2 + i for i in range(m)},
        compiler_params=pltpu.CompilerParams(has_side_effects=_EFFECT),
    )(*[pltpu.with_memory_space_constraint(b, pltpu.HBM) for b in both])
    return (outs[0], outs[1]), list(outs[2:2 + n]), list(outs[2 + n:2 + m]), outs[-1]


def _exchange_wait(sems, srcs, lands, plan, after, name):
    n, m = len(srcs), len(srcs) + len(lands)

    def body(*refs):
        send_sems, recv_sems = refs[m], refs[m + 1]
        for k, (src, dst, dev) in enumerate(plan(refs[:n], refs[n:m])):
            cp = pltpu.make_async_remote_copy(src_ref=src, dst_ref=dst, send_sem=send_sems.at[k], recv_sem=recv_sems.at[k],
                                              device_id=dev, device_id_type=MESH)
            cp.wait_send()
            cp.wait_recv()

    both = list(srcs) + list(lands)
    afters = list(after) if isinstance(after, (list, tuple)) else [after]
    outs = pl.pallas_call(
        body, name=name, in_specs=[_HBM] * m + [_SEM, _SEM] + [_ANY] * len(afters), out_specs=[_HBM] * m,
        out_shape=[pltpu.HBM(b.shape, b.dtype) for b in both], input_output_aliases={i: i for i in range(m)},
        compiler_params=pltpu.CompilerParams(has_side_effects=_EFFECT),
    )(*both, sems[0], sems[1], *afters)
    return list(outs[:n]), list(outs[n:])


def _fill_plan(bufs, _):
    x, y, c = _place()
    return [(_piece(buf, 2 * px + py, c), _piece(buf, 2 * px + py, c), (x, y, 1 - c))
            for buf in bufs for px, py in _other_chips(x, y)]


class _Reducer:
    def __init__(self, where):
        self.state = {}
        self.where = where

    def begin(self, stage, grads):
        names = list(grads)
        full = [grads[n] for n in names]
        lands = [lax.empty((NSH, g.shape[1] // 2, g.shape[2]), g.dtype) for g in full]
        sems, full, lands, token = _exchange_start(full, lands, _swap_plan, len(full), "swap_start_" + stage)
        self.state[stage] = (names, sems, full, lands)
        return token

    def advance(self, stage, after):
        names, sems, full, lands = self.state[stage]
        full, got = _exchange_wait(sems, full, lands, _swap_plan, after, "swap_wait_" + stage)
        sums, own = _chip_sums(full, got, self.where, "chip_sums_" + stage)
        lands = [lax.empty((3,) + s.shape[1:], BF16) for s in sums]
        sems, sent, lands, token = _exchange_start(sums, lands, _owners_plan, 3 * len(sums), "owners_start_" + stage)
        self.state[stage] = (names, own, sems, sent, lands)
        return token

    def finish(self, stage, after):
        names, own, sems, sent, lands = self.state[stage]
        _, got = _exchange_wait(sems, sent, lands, _owners_plan, after, "owners_wait_" + stage)
        return dict(zip(names, _owner_sums(own, got, "owner_sums_" + stage)))


def _chip_sums(gs, gots, where, name):
    n = len(gs)

    def body(where_ref, *refs):
        g_refs, got_refs, hb_refs, own_refs = (refs[k * n:(k + 1) * n] for k in range(4))
        mine = pl.program_id(0) == where_ref[1]
        for g_ref, got_ref, hb_ref, own_ref in zip(g_refs, got_refs, hb_refs, own_refs):
            h = g_ref[...].astype(F32) + got_ref[...].astype(F32)
            hb_ref[...] = h.astype(BF16)

            @pl.when(mine)
            def _():
                own_ref[...] = h

    halves = [(g.shape[1] // 2, g.shape[2]) for g in gs]
    slot = [pl.BlockSpec((None, rh, cc), lambda s, where: (s, 0, 0)) for rh, cc in halves]
    outs = pl.pallas_call(
        body, name=name,
        grid_spec=pltpu.PrefetchScalarGridSpec(
            num_scalar_prefetch=1, grid=(NSH,),
            in_specs=[pl.BlockSpec((None, rh, cc), lambda s, where: (s, where[0], 0)) for rh, cc in halves] + slot,
            out_specs=slot + [pl.BlockSpec((rh, cc), lambda s, where: (0, 0)) for rh, cc in halves]),
        out_shape=[jax.ShapeDtypeStruct((NSH, rh, cc), BF16) for rh, cc in halves]
        + [jax.ShapeDtypeStruct((rh, cc), F32) for rh, cc in halves],
        compiler_params=_params("arbitrary"),
    )(where, *gs, *gots)
    return list(outs[:n]), list(outs[n:])


def _owner_sums(owns, gots, name):
    n = len(owns)

    def body(*refs):
        own_refs, got_refs, o_refs = (refs[k * n:(k + 1) * n] for k in range(3))
        for own_ref, got_ref, o_ref in zip(own_refs, got_refs, o_refs):
            o_ref[...] = ((own_ref[...] + got_ref[0].astype(F32)) + got_ref[1].astype(F32)) + got_ref[2].astype(F32)

    blocks = [(o.shape[0] // 2, o.shape[1]) for o in owns]
    rows = [pl.BlockSpec(b, lambda i: (i, 0)) for b in blocks]
    return pl.pallas_call(
        body, grid=(2,), name=name,
        in_specs=rows + [pl.BlockSpec((3,) + b, lambda i: (0, i, 0)) for b in blocks], out_specs=rows,
        out_shape=[jax.ShapeDtypeStruct(o.shape, F32) for o in owns], compiler_params=_params("arbitrary"),
    )(*owns, *gots)


def _sibling_plan(srcs, lands):
    x, y, c = _place()
    return [(src, land, (x, y, 1 - c)) for src, land in zip(srcs, lands)]


def _all_reduce_small(part):
    def body(p_ref, o_ref, rbuf, send1, recv1, send2, recv2):
        x, y, c = _place()
        me = 4 * x + 2 * y + c
        peers = []
        for k in range(1, 8):
            px, py, pc = x ^ ((k >> 2) & 1), y ^ ((k >> 1) & 1), c ^ (k & 1)
            peers.append((k, (px, py, pc), 4 * px + 2 * py + pc))

        def rows(d):
            return pl.ds(pl.multiple_of(d * SMALL_SLICE, 8), SMALL_SLICE)

        first = [pltpu.make_async_remote_copy(src_ref=p_ref.at[rows(idx), :], dst_ref=rbuf.at[me], send_sem=send1.at[k],
                                              recv_sem=recv1.at[k], device_id=dev, device_id_type=MESH)
                 for k, dev, idx in peers]
        for cp in first:
            cp.start()
        rbuf[me] = p_ref[rows(me), :]
        for k, dev, idx in peers:
            pltpu.make_async_remote_copy(src_ref=p_ref.at[rows(idx), :], dst_ref=rbuf.at[idx], send_sem=send1.at[k],
                                         recv_sem=recv1.at[k], device_id=dev, device_id_type=MESH).wait_recv()
        acc = rbuf[0]
        for d in range(1, 8):
            acc = acc + rbuf[d]
        o_ref[rows(me), :] = acc
        second = [pltpu.make_async_remote_copy(src_ref=o_ref.at[rows(me), :], dst_ref=o_ref.at[rows(me), :],
                                               send_sem=send2.at[k], recv_sem=recv2.at[k], device_id=dev, device_id_type=MESH)
                  for k, dev, idx in peers]
        for cp in second:
            cp.start()
        for k, dev, idx in peers:
            pltpu.make_async_remote_copy(src_ref=o_ref.at[rows(me), :], dst_ref=o_ref.at[rows(idx), :], send_sem=send2.at[k],
                                         recv_sem=recv2.at[k], device_id=dev, device_id_type=MESH).wait_recv()
        for cp in first + second:
            cp.wait_send()

    return pl.pallas_call(
        body, name="all_reduce_small", in_specs=[_WHOLE], out_specs=_WHOLE,
        out_shape=jax.ShapeDtypeStruct((SMALL_ROWS, 128), F32),
        scratch_shapes=[pltpu.VMEM((8, SMALL_SLICE, 128), F32)] + [pltpu.SemaphoreType.DMA((8,))] * 4,
        compiler_params=pltpu.CompilerParams(has_side_effects=True),
    )(part)


def _adamw_update(w, gv, m, v):
    nm = ADAM_B1 * m + (1.0 - ADAM_B1) * gv
    nv = ADAM_B2 * v + (1.0 - ADAM_B2) * (gv * gv)
    m_hat = nm / (1.0 - ADAM_B1 ** ADAM_STEP)
    v_hat = nv / (1.0 - ADAM_B2 ** ADAM_STEP)
    return -ADAM_LR * (m_hat / (jnp.sqrt(v_hat) + ADAM_EPS) + ADAM_WD * w), nm, nv


def _adamw_small(ws, gs, ms, vs):
    n = len(ws)

    def body(*refs):
        w_refs, g_refs, m_refs, v_refs, d_refs, nm_refs, nv_refs = (refs[k * n:(k + 1) * n] for k in range(7))
        for i in range(n):
            d_refs[i][...], nm_refs[i][...], nv_refs[i][...] = _adamw_update(
                w_refs[i][...], g_refs[i][...], m_refs[i][...], v_refs[i][...])

    out = [jax.ShapeDtypeStruct(w.shape, F32) for w in ws]
    outs = pl.pallas_call(body, in_specs=[_WHOLE] * (4 * n), out_specs=[_WHOLE] * (3 * n), out_shape=out * 3,
                          name="adamw_small", compiler_params=_params())(*ws, *gs, *ms, *vs)
    return outs[:n], outs[n:2 * n], outs[2 * n:]


def _adamw_halves(ws, mines, theirs, ms, vs, name):
    n = len(ws)
    steps = 2

    def body(*refs):
        w_refs, mine_refs, theirs_refs, m_refs, v_refs, g_refs, d_refs, nm_refs, nv_refs = (
            refs[k * n:(k + 1) * n] for k in range(9))
        is_mine = pl.program_id(0) == lax.axis_index("c")
        for i in range(n):
            gv = jnp.where(is_mine, mine_refs[i][...], theirs_refs[i][...])
            g_refs[i][...] = gv
            d_refs[i][...], nm_refs[i][...], nv_refs[i][...] = _adamw_update(w_refs[i][...], gv, m_refs[i][...], v_refs[i][...])

    blocks = [(h.shape[0] // steps, h.shape[1]) for h in mines]
    whole = [pl.BlockSpec(b, lambda h, i: (steps * h + i, 0)) for b in blocks]
    half = [pl.BlockSpec(b, lambda h, i: (i, 0)) for b in blocks]
    out = [jax.ShapeDtypeStruct(w.shape, F32) for w in ws]
    outs = pl.pallas_call(body, grid=(2, steps), in_specs=whole + half + half + whole + whole, out_specs=whole * 4,
                          out_shape=out * 4, name=name, compiler_params=_params("arbitrary", "arbitrary"),
                          )(*ws, *mines, *theirs, *ms, *vs)
    return [tuple(outs[k * n + i] for k in range(4)) for i in range(n)]


SMALL_USED = sum(size for _, size in SMALL) // 128


def _pack_small(vals, tail=None):
    parts = []
    for name, size in SMALL:
        flat = vals[name].reshape(-1).astype(F32)
        parts.append(jnp.pad(flat, (0, size - flat.shape[0])))
    if tail is not None:
        parts.append(tail.reshape(128))
    flat = jnp.concatenate(parts)
    return jnp.pad(flat, (0, SMALL_ROWS * 128 - flat.shape[0])).reshape(SMALL_ROWS, 128)


def _unpack_small(packed, shapes):
    flat = packed.reshape(-1)
    out, off = {}, 0
    for name, size in SMALL:
        n = math.prod(shapes[name])
        out[name] = flat[off:off + n].reshape(shapes[name])
        off += size
    return out


def kernel(x, ffn1_pre_g, ffn1_w1, ffn1_w3, ffn1_w2, ffn1_post_g, mix_pre_g, w_in, conv_w, conv_b, rg_a_w, rg_a_b, rg_x_w, rg_x_b, lru_lambda, w_lru_out, attn_sinks, rel_bias, w_attn_out, w_gate, b_gate, w_o, mix_post_g, ffn2_pre_g, ffn2_w1, ffn2_w3, ffn2_w2, ffn2_post_g, loss_target, m_ffn1_pre_g, m_ffn1_w1, m_ffn1_w3, m_ffn1_w2, m_ffn1_post_g, m_mix_pre_g, m_w_in, m_conv_w, m_conv_b, m_rg_a_w, m_rg_a_b, m_rg_x_w, m_rg_x_b, m_lru_lambda, m_w_lru_out, m_attn_sinks, m_rel_bias, m_w_attn_out, m_w_gate, m_b_gate, m_w_o, m_mix_post_g, m_ffn2_pre_g, m_ffn2_w1, m_ffn2_w3, m_ffn2_w2, m_ffn2_post_g, v_ffn1_pre_g, v_ffn1_w1, v_ffn1_w3, v_ffn1_w2, v_ffn1_post_g, v_mix_pre_g, v_w_in, v_conv_w, v_conv_b, v_rg_a_w, v_rg_a_b, v_rg_x_w, v_rg_x_b, v_lru_lambda, v_w_lru_out, v_attn_sinks, v_rel_bias, v_w_attn_out, v_w_gate, v_b_gate, v_w_o, v_mix_post_g, v_ffn2_pre_g, v_ffn2_w1, v_ffn2_w3, v_ffn2_w2, v_ffn2_post_g):
    given = dict(locals())
    chip = 2 * lax.axis_index("x") + lax.axis_index("y")
    transposed = ("ffn1_w1", "ffn1_w3", "ffn2_w1", "ffn2_w3")

    def shard(name, moment=""):
        w = given[moment + name][0]
        return w.T if name in transposed else w

    def unshard(name, w):
        return (w.T if name in transposed else w)[None]

    def only_my_columns(a):
        parts = a.reshape(1, 4, NSH, D // NSH)
        return sum(jnp.where(chip == s, parts[:, :, s], 0.0) for s in range(NSH))

    chip_arr = jnp.reshape(chip, (1,)).astype(jnp.int32)
    stage_names = {"ffn1": ["ffn1_w1", "ffn1_w3", "ffn1_w2", "conv_w"],
                   "mix_in": ["w_in", "w_gate"],
                   "mix_out": ["w_lru_out", "w_attn_out", "w_o"],
                   "ffn2": ["ffn2_w1", "ffn2_w3", "ffn2_w2"]}
    in_flight, started = {}, None
    for stage, names in stage_names.items():
        bufs = [jnp.where(lax.broadcasted_iota(jnp.int32, (NSH, 4, D // NSH), 0) == chip, given[n], 0.0) if n == "conv_w"
                else _cast_into_slot(shard(n), chip_arr, "cast_" + n, started) for n in names]
        (in_flight[stage],), started = _gather_start([bufs], "gather_start_" + stage)
    all_started = started

    filling = {}

    def weights(stage, after, begin=False):
        names = stage_names[stage]
        halves_of = [n for n in names if n != "conv_w"]
        if stage in filling:
            filled, _ = _exchange_wait(filling.pop(stage), *filling.pop(stage + "/bufs"), _fill_plan, after,
                                       "fill_wait_" + stage)
            return dict(zip(halves_of, filled))
        send_sems, recv_sems, landing = in_flight[stage]
        if stage == "ffn1":
            after = all_started
        landed = dict(zip(names, _gather_wait(send_sems, recv_sems, landing, after, "gather_wait_" + stage)))
        halves = [landed[n] for n in halves_of]
        if begin:
            filling[stage], bufs, _, token = _exchange_start(halves, [], _fill_plan, 3 * len(halves), "fill_start_" + stage)
            filling[stage + "/bufs"] = (bufs, [])
            return token
        out = dict(zip(halves_of, _sibling_fill(halves, "sibling_fill_" + stage)))
        if "conv_w" in names:
            out["conv_w"] = jnp.transpose(landed["conv_w"], (1, 0, 2)).reshape(4, D)
        return out

    small_shapes = {n: given[n].shape for n, _ in SMALL}
    small_shapes["conv_w"] = (1, 4, D)
    sm = {n: (given[n][0] if given[n].shape[0] == 1 and n != "rel_bias" else given[n]) for n, _ in SMALL if n != "conv_w"}

    reducer = _Reducer(jnp.stack([lax.axis_index("c"), chip]).astype(jnp.int32))
    sq, dx, _, small = _local_step(x[0], loss_target[0], weights, sm, reducer)

    last_started = reducer.advance("ffn1", dx)
    reduced_small = _all_reduce_small(_pack_small(small, tail=sq))
    loss = reduced_small[SMALL_USED, 0] * (0.5 / D)
    small_g = _unpack_small(reduced_small, small_shapes)
    grads, delta, new_m, new_v = {}, {}, {}, {}
    in_transit = {}

    def send(stage, after):
        halves = reducer.finish(stage, after)
        lands = [lax.empty(h.shape, F32) for h in halves.values()]
        sems, mine, lands, token = _exchange_start(list(halves.values()), lands, _sibling_plan, len(lands),
                                                   "halves_start_" + stage)
        in_transit[stage] = (list(halves), sems, mine, lands)
        return token

    def update(stage, after):
        names, sems, mine, lands = in_transit[stage]
        mine, theirs = _exchange_wait(sems, mine, lands, _sibling_plan, after, "halves_wait_" + stage)
        updated = _adamw_halves([shard(n) for n in names], mine, theirs, [shard(n, "m_") for n in names],
                                [shard(n, "v_") for n in names], "adamw_" + stage)
        for n, results in zip(names, updated):
            grads[n], delta[n], new_m[n], new_v[n] = (unshard(n, r) for r in results)
        return new_v[names[-1]]

    token = send("ffn2", [reduced_small, last_started])
    token = send("mix", token)
    done = update("ffn2", token)
    done = update("mix", done)
    token = send("ffn1", done)
    update("ffn1", token)

    small_g["conv_w"] = only_my_columns(small_g["conv_w"])
    names = [n for n, _ in SMALL]
    flat2d = lambda a: a.reshape(-1, a.shape[-1])
    outs = _adamw_small(*[[flat2d(given[pre + n]) if pre != "g" else flat2d(small_g[n]) for n in names]
                          for pre in ("", "g", "m_", "v_")])
    for dst, arrs in zip((delta, new_m, new_v), outs):
        dst.update({n: a.reshape(given[n].shape) for n, a in zip(names, arrs)})
    grads.update(small_g)
    return (loss, dx[None], *[grads[n] for n in WEIGHTS], *[delta[n] for n in WEIGHTS], *[new_m[n] for n in WEIGHTS],
            *[new_v[n] for n in WEIGHTS])
```

```python
import functools
import math

import jax
import jax.numpy as jnp
from jax import lax
from jax.experimental import pallas as pl
from jax.experimental.pallas import tpu as pltpu

F32, BF16 = jnp.float32, jnp.bfloat16
D = 1024
NSH = 4
FF_S = 704
IN_S = 896
GATE_S = 512
KV_W = 256
CHUNK = 64
KB = 192
N_HEADS = 16
HEAD_DIM = 64
N_BUCKETS = 32
KP = 192
PAD_KEYS = 128
RMS_EPS = 1e-6
NEG_INF = -1e30
LRU_C = 8.0
TM = 512
TM_SCAN = 256
VMEM_LIMIT = 56 * 1024 * 1024
ADAM_LR, ADAM_B1, ADAM_B2, ADAM_EPS, ADAM_WD, ADAM_STEP = 0.001, 0.9, 0.999, 1e-08, 0.01, 10
SMALL_ROWS = 1216
SMALL_SLICE = SMALL_ROWS // 8
MESH = pl.DeviceIdType.MESH

BIG = ["ffn1_w1", "ffn1_w3", "ffn1_w2", "w_in", "w_lru_out", "w_attn_out", "w_gate", "w_o", "ffn2_w1", "ffn2_w3", "ffn2_w2"]
SMALL = [("ffn1_pre_g", 1024), ("ffn1_post_g", 1024), ("mix_pre_g", 1024), ("conv_w", 4096), ("conv_b", 1024),
         ("rg_a_w", 65536), ("rg_a_b", 1024), ("rg_x_w", 65536), ("rg_x_b", 1024), ("lru_lambda", 1024),
         ("attn_sinks", 1024), ("rel_bias", 1024), ("b_gate", 2048), ("mix_post_g", 1024), ("ffn2_pre_g", 1024),
         ("ffn2_post_g", 1024)]
WEIGHTS = ["ffn1_pre_g", "ffn1_w1", "ffn1_w3", "ffn1_w2", "ffn1_post_g", "mix_pre_g", "w_in", "conv_w", "conv_b", "rg_a_w",
           "rg_a_b", "rg_x_w", "rg_x_b", "lru_lambda", "w_lru_out", "attn_sinks", "rel_bias", "w_attn_out", "w_gate", "b_gate",
           "w_o", "mix_post_g", "ffn2_pre_g", "ffn2_w1", "ffn2_w3", "ffn2_w2", "ffn2_post_g"]


def _params(*sem):
    return pltpu.CompilerParams(dimension_semantics=sem or None, vmem_limit_bytes=VMEM_LIMIT)


def _nn(a, b):
    return jnp.dot(a, b, preferred_element_type=F32)


def _nt(a, b):
    return lax.dot_general(a, b, (((1,), (1,)), ((), ())), preferred_element_type=F32)


def _tn(a, b):
    return lax.dot_general(a, b, (((0,), (0,)), ((), ())), preferred_element_type=F32)


def _rms(x, g):
    rstd = lax.rsqrt(jnp.mean(x * x, axis=-1, keepdims=True) + RMS_EPS)
    return (x * rstd) * g


def _rms_bwd(dout, x, g):
    rstd = lax.rsqrt(jnp.mean(x * x, axis=-1, keepdims=True) + RMS_EPS)
    xhat = x * rstd
    dg = jnp.sum(dout * xhat, axis=0, keepdims=True)
    dxhat = dout * g
    dx = rstd * (dxhat - xhat * jnp.mean(dxhat * xhat, axis=-1, keepdims=True))
    return dx, dg


_GELU_K = math.sqrt(2.0 / math.pi)


def _gelu(x):
    return x * (0.5 * (1.0 + jnp.tanh(_GELU_K * (x + 0.044715 * (x * x * x)))))


def _gelu_and_grad(x):
    x2 = x * x
    t = jnp.tanh(_GELU_K * (x + 0.044715 * (x2 * x)))
    cdf = 0.5 * (1.0 + t)
    return x * cdf, cdf + x * (0.5 * (1.0 - t * t) * (_GELU_K * (1.0 + 3.0 * 0.044715 * x2)))


def _softplus_neg(lam):
    z = -lam
    u = jnp.exp(-jnp.abs(z))
    w = 1.0 + u
    log1p_u = jnp.where(w == 1.0, u, jnp.log(w) * (u / (w - 1.0)))
    return jnp.maximum(z, 0.0) + log1p_u


def _lru_coeffs(r, sp):
    log_a = (-LRU_C * r) * sp
    a = jnp.exp(log_a)
    t = jnp.tanh(log_a)
    s = jnp.sqrt(-2.0 * t / (1.0 - t))
    return a, s


def _row_spec(tm, width):
    return pl.BlockSpec((tm, width), lambda i: (i, 0))


def _vec_spec(width):
    return pl.BlockSpec((1, width), lambda i: (0, 0))


_WHOLE = pl.BlockSpec(memory_space=pltpu.VMEM)


def _tile(t, tm=TM):
    return min(tm, t)


def _ffn_fwd(x, gpre, w1g, w3g, w2g, gpost, name, target=None):
    t = x.shape[0]
    tm = _tile(t)
    last = target is not None

    def body(x_ref, gpre_ref, w1_ref, w3_ref, w2_ref, gpost_ref, *refs):
        t_ref, (h_ref, a_ref, b_ref, hm_ref, f_ref), l_ref = (refs[0] if last else None), refs[last:last + 5], refs[-1]
        xv = x_ref[...]
        nb = _rms(xv, gpre_ref[...]).astype(BF16)
        f = jnp.zeros((tm, D), F32)
        for s in range(NSH):
            a = _nt(nb, w1_ref[s])
            b = _nt(nb, w3_ref[s])
            hmb = ((a * jax.nn.sigmoid(a)) * b).astype(BF16)
            a_ref[s] = a.astype(BF16)
            b_ref[s] = b.astype(BF16)
            hm_ref[s] = hmb
            f = f + _nn(hmb, w2_ref[s])
        f_ref[...] = f
        h = xv + 0.5 * _rms(f, gpost_ref[...])
        if last:
            @pl.when(pl.program_id(0) == 0)
            def _():
                l_ref[...] = jnp.zeros_like(l_ref)

            e = h - t_ref[...]
            h_ref[...] = e * (1.0 / D)
            l_ref[...] += jnp.sum(jnp.sum(e * e, axis=0, keepdims=True), axis=1, keepdims=True)
        else:
            h_ref[...] = h

    sh = pl.BlockSpec((NSH, tm, FF_S), lambda i: (0, i, 0))
    act = jax.ShapeDtypeStruct((NSH, t, FF_S), BF16)
    return pl.pallas_call(
        body, grid=(t // tm,), name=name,
        in_specs=[_row_spec(tm, D), _vec_spec(D), _WHOLE, _WHOLE, _WHOLE, _vec_spec(D)] + [_row_spec(tm, D)] * last,
        out_specs=[_row_spec(tm, D), sh, sh, sh, _row_spec(tm, D)] + [pl.BlockSpec((1, 128), lambda i: (0, 0))] * last,
        out_shape=[jax.ShapeDtypeStruct((t, D), F32), act, act, act, jax.ShapeDtypeStruct((t, D), F32)]
        + [jax.ShapeDtypeStruct((1, 128), F32)] * last,
        compiler_params=_params("arbitrary"),
    )(x, gpre, w1g, w3g, w2g, gpost, *([target] if last else []))


def _mix_proj(h1, gmix, w_in_g, w_gate_g, b_gate):
    t = h1.shape[0]
    tm = _tile(t)

    def body(h_ref, g_ref, win_ref, wg_ref, bg_ref, u_ref, q_ref, k_ref, v_ref, xr_ref, xg_ref, gate_ref):
        ub = _rms(h_ref[...], g_ref[...]).astype(BF16)
        u_ref[...] = ub
        p0 = _nn(ub, win_ref[0])
        q_ref[:, 0:896] = p0.astype(BF16)
        p1 = _nn(ub, win_ref[1])
        q_ref[:, 896:1024] = p1[:, 0:128].astype(BF16)
        k_ref[...] = p1[:, 128:384].astype(BF16)
        v_ref[...] = p1[:, 384:640].astype(BF16)
        xr_ref[:, 0:256] = p1[:, 640:896]
        p2 = _nn(ub, win_ref[2])
        xr_ref[:, 256:1024] = p2[:, 0:768]
        xg_ref[:, 0:128] = p2[:, 768:896]
        xg_ref[:, 128:1024] = _nn(ub, win_ref[3])
        for s in range(NSH):
            sl = slice(s * GATE_S, (s + 1) * GATE_S)
            gate_ref[:, sl] = jax.nn.sigmoid(_nn(ub, wg_ref[s]) + bg_ref[:, sl])

    return pl.pallas_call(
        body, grid=(t // tm,), name="mix_proj",
        in_specs=[_row_spec(tm, D), _vec_spec(D), _WHOLE, _WHOLE, _vec_spec(2 * D)],
        out_specs=[_row_spec(tm, D), _row_spec(tm, D), _row_spec(tm, KV_W), _row_spec(tm, KV_W), _row_spec(tm, D),
                   _row_spec(tm, D), _row_spec(tm, 2 * D)],
        out_shape=[jax.ShapeDtypeStruct((t, D), BF16), jax.ShapeDtypeStruct((t, D), BF16),
                   jax.ShapeDtypeStruct((t, KV_W), BF16), jax.ShapeDtypeStruct((t, KV_W), BF16),
                   jax.ShapeDtypeStruct((t, D), F32), jax.ShapeDtypeStruct((t, D), F32),
                   jax.ShapeDtypeStruct((t, 2 * D), F32)],
        compiler_params=_params("arbitrary"),
    )(h1, gmix, w_in_g, w_gate_g, b_gate)


def _rglru_fwd(xr, xg, conv_w, conv_b, wa2, ba, wx2, bx, lam, after=None):
    t = xr.shape[0]
    tm = _tile(t, TM_SCAN)
    nb8 = tm // 8

    def body(xr_ref, xrp_ref, xg_ref, cw_ref, cb_ref, wa_ref, ba_ref, wx_ref, bx_ref, lam_ref,
             hr_ref, yain_ref, xc_ref, r_ref, ig_ref, a_sc, s_ref, ext, h_sc):
        i = pl.program_id(0)

        @pl.when(i == 0)
        def _():
            h_sc[...] = jnp.zeros_like(h_sc)

        ext[0:8, :] = jnp.where(i == 0, 0.0, xrp_ref[...])
        ext[8:8 + tm, :] = xr_ref[...]
        xc = jnp.broadcast_to(cb_ref[...], (tm, D))
        for tap in range(4):
            xc = xc + ext[pl.ds(5 + tap, tm), :] * cw_ref[tap:tap + 1, :]
        xc_ref[...] = xc
        xcb = xc.astype(BF16)
        for p in range(8):
            sl = slice(p * 128, (p + 1) * 128)
            r_ref[:, sl] = jax.nn.sigmoid(_nn(xcb[:, sl], wa_ref[p]) + ba_ref[:, sl])
            ig_ref[:, sl] = jax.nn.sigmoid(_nn(xcb[:, sl], wx_ref[p]) + bx_ref[:, sl])
        a, s = _lru_coeffs(r_ref[...], _softplus_neg(lam_ref[...]))
        a_sc[...] = a
        s_ref[...] = s
        hr_ref[...] = s * (ig_ref[...] * xc)

        def blk(j, h):
            st = pl.multiple_of(j * 8, 8)
            a8 = a_sc[pl.ds(st, 8), :]
            u8 = hr_ref[pl.ds(st, 8), :]
            rows = []
            for k in range(8):
                h = a8[k:k + 1, :] * h + u8[k:k + 1, :]
                rows.append(h)
            hr_ref[pl.ds(st, 8), :] = jnp.concatenate(rows, axis=0)
            return h

        h_sc[0:1, :] = lax.fori_loop(0, nb8, blk, h_sc[0:1, :])
        yain_ref[...] = (hr_ref[...] * _gelu(xg_ref[...])).astype(BF16)

    prev = pl.BlockSpec((8, D), lambda i: (jnp.maximum(i * nb8 - 1, 0), 0))
    full = lambda shape: pl.BlockSpec(shape, lambda i: tuple(0 for _ in shape))
    f32 = jax.ShapeDtypeStruct((t, D), F32)
    body, specs, operands = _behind(body, after)
    return pl.pallas_call(
        body, grid=(t // tm,), name="rglru_fwd",
        in_specs=specs + [_row_spec(tm, D), prev, _row_spec(tm, D), full((4, D)), _vec_spec(D), full((8, 128, 128)),
                          _vec_spec(D), full((8, 128, 128)), _vec_spec(D), _vec_spec(D)],
        out_specs=[_row_spec(tm, D)] * 7,
        out_shape=[f32, jax.ShapeDtypeStruct((t, D), BF16), f32, f32, f32, f32, f32],
        scratch_shapes=[pltpu.VMEM((tm + 8, D), F32), pltpu.VMEM((8, D), F32)],
        compiler_params=_params("arbitrary"),
    )(*operands, xr, xr, xg, conv_w, conv_b, wa2, ba, wx2, bx, lam)


def _bias_fwd(table_t, onehot_t):
    def body(t_ref, e_ref, o_ref):
        o_ref[...] = jnp.dot(t_ref[...], e_ref[...], preferred_element_type=F32, precision=lax.Precision.HIGHEST)

    return pl.pallas_call(body, out_shape=jax.ShapeDtypeStruct((N_HEADS, CHUNK * KB), F32), name="bias_fwd",
                          compiler_params=_params())(table_t, onehot_t)


def _bias_bwd(dbias_flat, onehot_t, ds_rows):
    def body(d_ref, e_ref, s_ref, o_ref, so_ref):
        o_ref[...] = lax.dot_general(d_ref[...], e_ref[...], (((1,), (1,)), ((), ())), preferred_element_type=F32,
                                     precision=lax.Precision.HIGHEST)
        so_ref[...] = jnp.zeros_like(so_ref)
        for r in range(4):
            so_ref[:, r:r + 1] = jnp.sum(s_ref[:, r * CHUNK:(r + 1) * CHUNK], axis=1, keepdims=True)

    return pl.pallas_call(body, out_shape=[jax.ShapeDtypeStruct((N_HEADS, N_BUCKETS), F32), jax.ShapeDtypeStruct((8, 128), F32)],
                          name="bias_bwd", compiler_params=_params())(dbias_flat, onehot_t, ds_rows)


def _stack_heads(q):
    return jnp.concatenate(
        [jnp.concatenate([q[:, (4 * g + r) * HEAD_DIM:(4 * g + r + 1) * HEAD_DIM] for g in range(4)], axis=1)
         for r in range(4)], axis=0)


def _unstack_heads(o):
    return jnp.concatenate([o[r * CHUNK:(r + 1) * CHUNK, g * HEAD_DIM:(g + 1) * HEAD_DIM] for g in range(4) for r in range(4)],
                           axis=1)


def _block_diag(w, mask):
    return jnp.concatenate([w] * 4, axis=0) * mask


def _group_softmax(qk, bias_g, sink, valid):
    s = qk * (HEAD_DIM ** -0.5) + bias_g
    s = jnp.where(valid, s, NEG_INF)
    m = jnp.maximum(jnp.max(s, axis=0, keepdims=True), sink)
    e = jnp.exp(s - m)
    es = jnp.exp(sink - m)
    inv = 1.0 / (jnp.sum(e, axis=0, keepdims=True) + es)
    return e * inv, es * inv


def _attn_fwd(sink_rows, q, kp, vp, bias_t, mask, after=None):
    t = q.shape[0]
    per_step = 4

    def body(sink_ref, q_ref, kp_ref, vp_ref, bias_ref, mask_ref, o_ref):
        owns = [mask_ref[g * KP:(g + 1) * KP, :] for g in range(4)]
        for k in range(per_step):
            c = pl.program_id(0) * per_step + k
            rows = slice(k * CHUNK, (k + 1) * CHUNK)
            st = pl.multiple_of(c * CHUNK, CHUNK)
            kw = kp_ref[pl.ds(st, KP), :]
            vw = vp_ref[pl.ds(st, KP), :]
            q_all = _stack_heads(q_ref[rows, :])
            valid = lax.broadcasted_iota(jnp.int32, (KP, 1), 0) + c * CHUNK >= PAD_KEYS
            scores = [_nt(kw * owns[g], q_all) for g in range(4)]
            ps = [_group_softmax(scores[g], bias_ref[g * KP:(g + 1) * KP, :], sink_ref[g:g + 1, :], valid)[0]
                  for g in range(4)]
            o_all = sum(_tn(ps[g].astype(BF16), vw * owns[g]) for g in range(4))
            o_ref[rows, :] = _unstack_heads(o_all).astype(BF16)

    body, specs, operands = _behind(body, after)
    return pl.pallas_call(
        body, grid=(t // (per_step * CHUNK),), name="attn_fwd",
        in_specs=specs + [_WHOLE, _row_spec(per_step * CHUNK, D), _WHOLE, _WHOLE, _WHOLE, _WHOLE],
        out_specs=_row_spec(per_step * CHUNK, D),
        out_shape=jax.ShapeDtypeStruct((t, D), BF16),
        compiler_params=_params("arbitrary"),
    )(*operands, sink_rows, q, kp, vp, bias_t, mask)


def _merge_fwd(yain, o, gate, h1, w_lru, w_att, w_o, gpost):
    t = h1.shape[0]
    tm = _tile(t)

    def body(ya_ref, o_ref, g_ref, h_ref, wl_ref, wa_ref, wo_ref, gp_ref, h2_ref, mo_ref, mg_ref, ya_out, yb_out):
        ya = _nn(ya_ref[...], wl_ref[...])
        yb = _nn(o_ref[...], wa_ref[...])
        g0 = g_ref[:, 0:D]
        g1 = g_ref[:, D:2 * D]
        mg = (g0 * ya + g1 * yb).astype(BF16)
        mo = _nn(mg, wo_ref[...])
        ya_out[...] = (ya * (g0 * (1.0 - g0))).astype(BF16)
        yb_out[...] = (yb * (g1 * (1.0 - g1))).astype(BF16)
        mg_ref[...] = mg
        mo_ref[...] = mo
        h2_ref[...] = h_ref[...] + _rms(mo, gp_ref[...])

    f32 = jax.ShapeDtypeStruct((t, D), F32)
    b16 = jax.ShapeDtypeStruct((t, D), BF16)
    return pl.pallas_call(
        body, grid=(t // tm,), name="merge_fwd",
        in_specs=[_row_spec(tm, D), _row_spec(tm, D), _row_spec(tm, 2 * D), _row_spec(tm, D), _WHOLE, _WHOLE, _WHOLE,
                  _vec_spec(D)],
        out_specs=[_row_spec(tm, D)] * 5,
        out_shape=[f32, f32, b16, b16, b16],
        compiler_params=_params("arbitrary"),
    )(yain, o, gate, h1, w_lru, w_att, w_o, gpost)


def _ffn_bwd(dh, x, f, a, b, gpre, gpost, w1g, w3g, w2g, name):
    t = x.shape[0]
    tm = _tile(t, TM_SCAN)

    def body(dh_ref, x_ref, f_ref, a_ref, b_ref, gpre_ref, gpost_ref, w1_ref, w3_ref, w2_ref,
             dx_ref, n_ref, da_ref, db_ref, df_ref, dgpre_ref, dgpost_ref):
        @pl.when(pl.program_id(0) == 0)
        def _():
            dgpre_ref[...] = jnp.zeros_like(dgpre_ref)
            dgpost_ref[...] = jnp.zeros_like(dgpost_ref)

        dhv = dh_ref[...]
        xv = x_ref[...]
        df, dgp = _rms_bwd(0.5 * dhv, f_ref[...], gpost_ref[...])
        dgpost_ref[...] += dgp
        dfb = df.astype(BF16)
        df_ref[...] = dfb
        n_ref[...] = _rms(xv, gpre_ref[...]).astype(BF16)
        dn = jnp.zeros((tm, D), F32)
        for s in range(NSH):
            av = a_ref[s].astype(F32)
            bv = b_ref[s].astype(F32)
            sg = jax.nn.sigmoid(av)
            dhm = _nt(dfb, w2_ref[s])
            dab = (dhm * bv * (sg * (1.0 + av * (1.0 - sg)))).astype(BF16)
            dbb = (dhm * (av * sg)).astype(BF16)
            da_ref[s] = dab
            db_ref[s] = dbb
            dn = dn + _nn(dab, w1_ref[s]) + _nn(dbb, w3_ref[s])
        dxn, dg = _rms_bwd(dn, xv, gpre_ref[...])
        dgpre_ref[...] += dg
        dx_ref[...] = dhv + dxn

    sh = pl.BlockSpec((NSH, tm, FF_S), lambda i: (0, i, 0))
    act = jax.ShapeDtypeStruct((NSH, t, FF_S), BF16)
    vec = jax.ShapeDtypeStruct((1, D), F32)
    return pl.pallas_call(
        body, grid=(t // tm,), name=name,
        in_specs=[_row_spec(tm, D), _row_spec(tm, D), _row_spec(tm, D), sh, sh, _vec_spec(D), _vec_spec(D), _WHOLE, _WHOLE,
                  _WHOLE],
        out_specs=[_row_spec(tm, D), _row_spec(tm, D), sh, sh, _row_spec(tm, D), _vec_spec(D), _vec_spec(D)],
        out_shape=[jax.ShapeDtypeStruct((t, D), F32), jax.ShapeDtypeStruct((t, D), BF16), act, act,
                   jax.ShapeDtypeStruct((t, D), BF16), vec, vec],
        compiler_params=_params("arbitrary"),
    )(dh, x, f, a, b, gpre, gpost, w1g, w3g, w2g)


def _behind(body, after):
    if after is None:
        return body, [], []

    def ordered(_, *refs):
        body(*refs)

    return ordered, [_ANY], [after]


def _ffn_bwd_acts(dh, x, f, a, b, gpre, gpost, w2g, name):
    t = x.shape[0]
    tm = _tile(t)

    def body(dh_ref, x_ref, f_ref, a_ref, b_ref, gpre_ref, gpost_ref, w2_ref, n_ref, da_ref, db_ref, df_ref, dgpost_ref):
        @pl.when(pl.program_id(0) == 0)
        def _():
            dgpost_ref[...] = jnp.zeros_like(dgpost_ref)

        df, dgp = _rms_bwd(0.5 * dh_ref[...], f_ref[...], gpost_ref[...])
        dgpost_ref[...] += dgp
        dfb = df.astype(BF16)
        df_ref[...] = dfb
        n_ref[...] = _rms(x_ref[...], gpre_ref[...]).astype(BF16)
        for s in range(NSH):
            av = a_ref[s].astype(F32)
            bv = b_ref[s].astype(F32)
            sg = jax.nn.sigmoid(av)
            dhm = _nt(dfb, w2_ref[s])
            da_ref[s] = (dhm * bv * (sg * (1.0 + av * (1.0 - sg)))).astype(BF16)
            db_ref[s] = (dhm * (av * sg)).astype(BF16)

    sh = pl.BlockSpec((NSH, tm, FF_S), lambda i: (0, i, 0))
    act = jax.ShapeDtypeStruct((NSH, t, FF_S), BF16)
    b16 = jax.ShapeDtypeStruct((t, D), BF16)
    return pl.pallas_call(
        body, grid=(t // tm,), name=name,
        in_specs=[_row_spec(tm, D), _row_spec(tm, D), _row_spec(tm, D), sh, sh, _vec_spec(D), _vec_spec(D), _WHOLE],
        out_specs=[_row_spec(tm, D), sh, sh, _row_spec(tm, D), _vec_spec(D)],
        out_shape=[b16, act, act, b16, jax.ShapeDtypeStruct((1, D), F32)],
        compiler_params=_params("arbitrary"),
    )(dh, x, f, a, b, gpre, gpost, w2g)


def _ffn_bwd_input(dh, x, da, db, gpre, w1g, w3g, name, after):
    t = x.shape[0]
    tm = _tile(t)

    def body(dh_ref, x_ref, da_ref, db_ref, gpre_ref, w1_ref, w3_ref, dx_ref, dgpre_ref):
        @pl.when(pl.program_id(0) == 0)
        def _():
            dgpre_ref[...] = jnp.zeros_like(dgpre_ref)

        dn = jnp.zeros((tm, D), F32)
        for s in range(NSH):
            dn = dn + _nn(da_ref[s], w1_ref[s]) + _nn(db_ref[s], w3_ref[s])
        dxn, dg = _rms_bwd(dn, x_ref[...], gpre_ref[...])
        dgpre_ref[...] += dg
        dx_ref[...] = dh_ref[...] + dxn

    sh = pl.BlockSpec((NSH, tm, FF_S), lambda i: (0, i, 0))
    body, specs, operands = _behind(body, after)
    return pl.pallas_call(
        body, grid=(t // tm,), name=name,
        in_specs=specs + [_row_spec(tm, D), _row_spec(tm, D), sh, sh, _vec_spec(D), _WHOLE, _WHOLE],
        out_specs=[_row_spec(tm, D), _vec_spec(D)],
        out_shape=[jax.ShapeDtypeStruct((t, D), F32), jax.ShapeDtypeStruct((1, D), F32)],
        compiler_params=_params("arbitrary"),
    )(*operands, dh, x, da, db, gpre, w1g, w3g)


def _wgrad(a, b, a_spec, b_spec, out_spec, out_shape, grid, name, after=None):
    def body(a_ref, b_ref, o_ref):
        o_ref[...] = _tn(a_ref[...], b_ref[...]).astype(BF16)

    body, specs, operands = _behind(body, after)
    return pl.pallas_call(body, grid=grid, name=name, in_specs=specs + [a_spec, b_spec], out_specs=out_spec,
                          out_shape=jax.ShapeDtypeStruct(out_shape, BF16),
                          compiler_params=_params(*("arbitrary",) * len(grid)))(*operands, a, b)


def _wgrad_cols(act, dsh, width, name, after=None):
    t = act.shape[0]
    if dsh.ndim == 3:
        b_spec = pl.BlockSpec((None, t, width), lambda s, k: (s, 0, 0))
    else:
        b_spec = pl.BlockSpec((t, width), lambda s, k: (0, s))
    return _wgrad(act, dsh, pl.BlockSpec((t, 512), lambda s, k: (0, k)), b_spec,
                  pl.BlockSpec((None, 512, width), lambda s, k: (s, k, 0)), (NSH, D, width), (NSH, 2), name, after)


def _wgrad_rows(hm, df, name, after=None):
    t = df.shape[0]
    return _wgrad(hm, df, pl.BlockSpec((None, t, FF_S), lambda s: (s, 0, 0)), pl.BlockSpec((t, D), lambda s: (0, 0)),
                  pl.BlockSpec((None, FF_S, D), lambda s: (s, 0, 0)), (NSH, FF_S, D), (NSH,), name, after)


def _wgrad_sq(a, b, name, after=None):
    t = a.shape[0]
    return _wgrad(a, b, pl.BlockSpec((t, 512), lambda i, j: (0, i)), pl.BlockSpec((t, 512), lambda i, j: (0, j)),
                  pl.BlockSpec((512, 512), lambda i, j: (i, j)), (D, D), (2, 2), name, after)


def _mix_bwd1(dh2, mo, gpost, gate, ya, yb, xg, hr, w_o, w_lru, w_att, after):
    t = dh2.shape[0]
    tm = _tile(t, TM_SCAN)

    def body(dh_ref, mo_ref, gp_ref, g_ref, ya_ref, yb_ref, xg_ref, hr_ref, wo_ref, wl_ref, wa_ref,
             dmo_ref, dya_ref, dyb_ref, dgate_ref, dhr_ref, dxg_ref, do_ref, dgp_ref, dbg_ref):
        @pl.when(pl.program_id(0) == 0)
        def _():
            dgp_ref[...] = jnp.zeros_like(dgp_ref)
            dbg_ref[...] = jnp.zeros_like(dbg_ref)

        dmo, dgp = _rms_bwd(dh_ref[...], mo_ref[...], gp_ref[...])
        dgp_ref[...] += dgp
        dmob = dmo.astype(BF16)
        dmo_ref[...] = dmob
        dm = _nt(dmob, wo_ref[...])
        g0 = g_ref[:, 0:D]
        g1 = g_ref[:, D:2 * D]
        dyab = (dm * g0).astype(BF16)
        dybb = (dm * g1).astype(BF16)
        dya_ref[...] = dyab
        dyb_ref[...] = dybb
        dg0 = dm * ya_ref[...].astype(F32)
        dg1 = dm * yb_ref[...].astype(F32)
        dgate_ref[:, 0:D] = dg0.astype(BF16)
        dgate_ref[:, D:2 * D] = dg1.astype(BF16)
        dbg_ref[:, 0:D] += jnp.sum(dg0, axis=0, keepdims=True)
        dbg_ref[:, D:2 * D] += jnp.sum(dg1, axis=0, keepdims=True)
        dyain = _nt(dyab, wl_ref[...])
        do_ref[...] = _nt(dybb, wa_ref[...]).astype(BF16)
        xgv = xg_ref[...]
        gelu, gelu_grad = _gelu_and_grad(xgv)
        dhr_ref[...] = dyain * gelu
        dxg_ref[...] = (dyain * hr_ref[...] * gelu_grad).astype(BF16)

    b16 = jax.ShapeDtypeStruct((t, D), BF16)
    body, specs, operands = _behind(body, after)
    return pl.pallas_call(
        body, grid=(t // tm,), name="mix_bwd1",
        in_specs=specs + [_row_spec(tm, D), _row_spec(tm, D), _vec_spec(D), _row_spec(tm, 2 * D), _row_spec(tm, D),
                          _row_spec(tm, D), _row_spec(tm, D), _row_spec(tm, D), _WHOLE, _WHOLE, _WHOLE],
        out_specs=[_row_spec(tm, D), _row_spec(tm, D), _row_spec(tm, D), _row_spec(tm, 2 * D), _row_spec(tm, D),
                   _row_spec(tm, D), _row_spec(tm, D), _vec_spec(D), _vec_spec(2 * D)],
        out_shape=[b16, b16, b16, jax.ShapeDtypeStruct((t, 2 * D), BF16), jax.ShapeDtypeStruct((t, D), F32), b16, b16,
                   jax.ShapeDtypeStruct((1, D), F32), jax.ShapeDtypeStruct((1, 2 * D), F32)],
        compiler_params=_params("arbitrary"),
    )(*operands, dh2, mo, gpost, gate, ya, yb, xg, hr, w_o, w_lru, w_att)


def _rglru_bwd(dhr, hr, xc, r, ig, a, s, xr, conv_w, wa2, wx2, lam, after):
    t = dhr.shape[0]
    tm = _tile(t, TM_SCAN)
    nb8 = tm // 8
    nt = t // tm

    def body(dhr_ref, hr_ref, hrp_ref, xc_ref, r_ref, ig_ref, a_sc, s_ref, xr_ref, cw_ref, wa_ref, wx_ref, lam_ref,
             dxr_ref, dwa_ref, dwx_ref, dba_ref, dbx_ref, dlam_ref, dcw_ref, dcb_ref,
             ext_h, ext_d, g_sc, c_sc, nxt_sc):
        i = pl.program_id(0)
        first_tile = i == nt - 1

        @pl.when(i == 0)
        def _():
            c_sc[...] = jnp.zeros_like(c_sc)
            nxt_sc[...] = jnp.zeros_like(nxt_sc)
            for ref in (dwa_ref, dwx_ref, dba_ref, dbx_ref, dlam_ref, dcw_ref, dcb_ref):
                ref[...] = jnp.zeros_like(ref)

        lamv = lam_ref[...]
        sp = _softplus_neg(lamv)
        rv = r_ref[...]
        igv = ig_ref[...]
        xcv = xc_ref[...]
        a = a_sc[...]
        s = s_ref[...]

        def blk(jj, c):
            st = pl.multiple_of((nb8 - 1 - jj) * 8, 8)
            d8 = dhr_ref[pl.ds(st, 8), :]
            a8 = a_sc[pl.ds(st, 8), :]
            rows = [None] * 8
            for k in range(7, -1, -1):
                g = d8[k:k + 1, :] + c
                c = a8[k:k + 1, :] * g
                rows[k] = g
            g_sc[pl.ds(st, 8), :] = jnp.concatenate(rows, axis=0)
            return c

        c_sc[0:1, :] = lax.fori_loop(0, nb8, blk, c_sc[0:1, :])
        g = g_sc[...]
        ext_h[0:8, :] = jnp.where(first_tile, 0.0, hrp_ref[...])
        ext_h[8:8 + tm, :] = hr_ref[...]
        hprev = ext_h[pl.ds(7, tm), :]
        d_s = g * (igv * xcv)
        dig = g * s * xcv
        dxc = g * s * igv
        dla = (g * hprev) * a - d_s * ((a * a) / s)
        dr_pre = (dla * (-LRU_C * sp)) * (rv * (1.0 - rv))
        di_pre = dig * (igv * (1.0 - igv))
        dlam_ref[...] += jnp.sum(dla * (LRU_C * rv), axis=0, keepdims=True) * jax.nn.sigmoid(-lamv)
        dba_ref[...] += jnp.sum(dr_pre, axis=0, keepdims=True)
        dbx_ref[...] += jnp.sum(di_pre, axis=0, keepdims=True)
        drb = dr_pre.astype(BF16)
        dib = di_pre.astype(BF16)
        xcb = xcv.astype(BF16)
        ext_d[tm:tm + 8, :] = nxt_sc[...]
        for p in range(8):
            sl = slice(p * 128, (p + 1) * 128)
            ext_d[0:tm, sl] = dxc[:, sl] + _nt(drb[:, sl], wa_ref[p]) + _nt(dib[:, sl], wx_ref[p])
            dwa_ref[p] += _tn(xcb[:, sl], drb[:, sl])
            dwx_ref[p] += _tn(xcb[:, sl], dib[:, sl])
        dxcv = ext_d[0:tm, :]
        nxt_sc[...] = ext_d[0:8, :]
        dcb_ref[...] += jnp.sum(dxcv, axis=0, keepdims=True)
        xrv = xr_ref[...]
        dxr = jnp.zeros((tm, D), F32)
        for tap in range(4):
            ext_h[0:tm, :] = ext_d[pl.ds(3 - tap, tm), :]
            ahead = ext_h[0:tm, :]
            dxr = dxr + ahead * cw_ref[tap:tap + 1, :]
            dcw_ref[tap:tap + 1, :] += jnp.sum(ahead * xrv, axis=0, keepdims=True)
        dxr_ref[...] = dxr.astype(BF16)

    rev = pl.BlockSpec((tm, D), lambda i: (nt - 1 - i, 0))
    prev = pl.BlockSpec((8, D), lambda i: (jnp.maximum((nt - 1 - i) * nb8 - 1, 0), 0))
    full = lambda shape: pl.BlockSpec(shape, lambda i: tuple(0 for _ in shape))
    vec = jax.ShapeDtypeStruct((1, D), F32)
    blocks = jax.ShapeDtypeStruct((8, 128, 128), F32)
    body, specs, operands = _behind(body, after)
    return pl.pallas_call(
        body, grid=(nt,), name="rglru_bwd",
        in_specs=specs + [rev, rev, prev, rev, rev, rev, rev, rev, rev, full((4, D)), full((8, 128, 128)),
                          full((8, 128, 128)), _vec_spec(D)],
        out_specs=[rev, full((8, 128, 128)), full((8, 128, 128)), _vec_spec(D), _vec_spec(D), _vec_spec(D), full((4, D)),
                   _vec_spec(D)],
        out_shape=[jax.ShapeDtypeStruct((t, D), BF16), blocks, blocks, vec, vec, vec, jax.ShapeDtypeStruct((4, D), F32), vec],
        scratch_shapes=[pltpu.VMEM((tm + 8, D), F32), pltpu.VMEM((tm + 8, D), F32),
                        pltpu.VMEM((tm, D), F32), pltpu.VMEM((8, D), F32), pltpu.VMEM((8, D), F32)],
        compiler_params=_params("arbitrary"),
    )(*operands, dhr, hr, hr, xc, r, ig, a, s, xr, conv_w, wa2, wx2, lam)


def _attn_bwd(sink_rows, q, kp, vp, bias_t, mask, do):
    t = q.shape[0]
    tp = kp.shape[0]
    per_step = 4

    def body(sink_ref, q_ref, kp_ref, vp_ref, bias_ref, mask_ref, do_ref, dq_ref, dk_ref, dv_ref, dbias_ref, ds_ref):
        @pl.when(pl.program_id(0) == 0)
        def _():
            for ref in (dk_ref, dv_ref, dbias_ref, ds_ref):
                ref[...] = jnp.zeros_like(ref)

        maskv = mask_ref[...]
        lane_group = lax.broadcasted_iota(jnp.int32, (1, 4 * HEAD_DIM), 1) // HEAD_DIM

        def own_blocks(full):
            out = full[0:KP]
            for g in range(1, 4):
                out = jnp.where(lane_group == g, full[g * KP:(g + 1) * KP], out)
            return out

        dsc_sum, dsinks, dks, dvs = 0.0, [0.0] * 4, [], []
        for k in range(per_step):
            c = pl.program_id(0) * per_step + k
            chunk = slice(k * CHUNK, (k + 1) * CHUNK)
            st = pl.multiple_of(c * CHUNK, CHUNK)
            kbd = _block_diag(kp_ref[pl.ds(st, KP), :], maskv)
            vbd = _block_diag(vp_ref[pl.ds(st, KP), :], maskv)
            q_all = _stack_heads(q_ref[chunk, :])
            do_all = _stack_heads(do_ref[chunk, :])
            valid = lax.broadcasted_iota(jnp.int32, (KP, 1), 0) + c * CHUNK >= PAD_KEYS
            qk = _nt(kbd, q_all)
            dp = _nt(vbd, do_all)
            ps, dscs = [], []
            for g in range(4):
                rows = slice(g * KP, (g + 1) * KP)
                p, sink_p = _group_softmax(qk[rows], bias_ref[rows, :], sink_ref[g:g + 1, :], valid)
                delta = jnp.sum(p * dp[rows], axis=0, keepdims=True)
                ps.append(p)
                dscs.append(p * (dp[rows] - delta))
                dsinks[g] = dsinks[g] - sink_p * delta
            dsc = jnp.concatenate(dscs, axis=0)
            dsc_sum = dsc_sum + dsc
            dsb = (dsc * (HEAD_DIM ** -0.5)).astype(BF16)
            dq_ref[chunk, :] = _unstack_heads(_tn(dsb, kbd)).astype(BF16)
            dks.append((st, own_blocks(_nn(dsb, q_all))))
            dvs.append((st, own_blocks(_nn(jnp.concatenate(ps, axis=0).astype(BF16), do_all))))
        dbias_ref[...] += dsc_sum
        for g in range(4):
            ds_ref[g:g + 1, :] += dsinks[g]
        for (st, dkw), (_, dvw) in zip(dks, dvs):
            dk_ref[pl.ds(st, KP), :] += dkw
            dv_ref[pl.ds(st, KP), :] += dvw

    full = lambda shape: pl.BlockSpec(shape, lambda i: tuple(0 for _ in shape))
    return pl.pallas_call(
        body, grid=(t // (per_step * CHUNK),), name="attn_bwd",
        in_specs=[_WHOLE, _row_spec(per_step * CHUNK, D), _WHOLE, _WHOLE, _WHOLE, _WHOLE, _row_spec(per_step * CHUNK, D)],
        out_specs=[_row_spec(per_step * CHUNK, D), full((tp, KV_W)), full((tp, KV_W)), full((4 * KP, 4 * CHUNK)),
                   full((8, 4 * CHUNK))],
        out_shape=[jax.ShapeDtypeStruct((t, D), BF16), jax.ShapeDtypeStruct((tp, KV_W), F32),
                   jax.ShapeDtypeStruct((tp, KV_W), F32), jax.ShapeDtypeStruct((4 * KP, 4 * CHUNK), F32),
                   jax.ShapeDtypeStruct((8, 4 * CHUNK), F32)],
        compiler_params=_params("arbitrary"),
    )(sink_rows, q, kp, vp, bias_t, mask, do)


def _mix_bwd2(dproj, dgate, h1, dh2, gmix, w_in_g, w_gate_g, after):
    t = h1.shape[0]
    tm = _tile(t)

    def body(dp_ref, dg_ref, h_ref, dh_ref, g_ref, win_ref, wg_ref, dh1_ref, dgm_ref):
        @pl.when(pl.program_id(0) == 0)
        def _():
            dgm_ref[...] = jnp.zeros_like(dgm_ref)

        du = jnp.zeros((tm, D), F32)
        for s in range(NSH):
            du = du + _nt(dp_ref[:, s * IN_S:(s + 1) * IN_S], win_ref[s])
            du = du + _nt(dg_ref[:, s * GATE_S:(s + 1) * GATE_S], wg_ref[s])
        dxn, dg = _rms_bwd(du, h_ref[...], g_ref[...])
        dgm_ref[...] += dg
        dh1_ref[...] = dh_ref[...] + dxn

    body, specs, operands = _behind(body, after)
    return pl.pallas_call(
        body, grid=(t // tm,), name="mix_bwd2",
        in_specs=specs + [_row_spec(tm, NSH * IN_S), _row_spec(tm, 2 * D), _row_spec(tm, D), _row_spec(tm, D), _vec_spec(D),
                          _WHOLE, _WHOLE],
        out_specs=[_row_spec(tm, D), _vec_spec(D)],
        out_shape=[jax.ShapeDtypeStruct((t, D), F32), jax.ShapeDtypeStruct((1, D), F32)],
        compiler_params=_params("arbitrary"),
    )(*operands, dproj, dgate, h1, dh2, gmix, w_in_g, w_gate_g)


def _band_onehot():
    nb = N_BUCKETS // 2
    max_exact = nb // 2
    rel = jnp.arange(KB)[None, :] - PAD_KEYS - jnp.arange(CHUNK)[:, None]
    ret = jnp.where(rel > 0, nb, 0)
    n = jnp.abs(rel)
    nf = jnp.maximum(n, 1).astype(jnp.float32)
    large = max_exact + (jnp.log(nf / max_exact) / math.log(128 / max_exact) * (nb - max_exact)).astype(jnp.int32)
    large = jnp.minimum(large, nb - 1)
    buckets = (ret + jnp.where(n < max_exact, n, large)).reshape(1, CHUNK * KB)
    return (buckets == jnp.arange(N_BUCKETS)[:, None]).astype(F32)


def _pair_blocks(w):
    pairs = w.reshape(8, 2, 64, 64)
    z = jnp.zeros((8, 64, 64), w.dtype)
    return jnp.concatenate([jnp.concatenate([pairs[:, 0], z], axis=2), jnp.concatenate([z, pairs[:, 1]], axis=2)], axis=1)


def _unpair_blocks(w2):
    return jnp.stack([w2[:, 0:64, 0:64], w2[:, 64:128, 64:128]], axis=1).reshape(16, 64, 64)


def _local_step(x, target, weights, sm, reducer):
    row = lambda v: v.reshape(1, -1)
    onehot_t = _band_onehot()
    bias = _bias_fwd(sm["rel_bias"].T, onehot_t).reshape(4, 4, CHUNK, KB)
    bias_t = jnp.pad(jnp.transpose(bias, (0, 3, 1, 2)), ((0, 0), (0, KP - KB), (0, 0), (0, 0))).reshape(4 * KP, 4 * CHUNK)
    sink_rows = jnp.pad(jnp.repeat(sm["attn_sinks"].reshape(4, 4), CHUNK, axis=1), ((0, 4), (0, 0)))
    grp = jnp.arange(4 * KP)[:, None] // KP == jnp.arange(4 * HEAD_DIM)[None, :] // HEAD_DIM
    mask = (grp & (jnp.arange(4 * KP)[:, None] % KP < KB)).astype(BF16)
    wa2 = _pair_blocks(sm["rg_a_w"]).astype(BF16)
    wx2 = _pair_blocks(sm["rg_x_w"]).astype(BF16)
    wg = dict(weights("ffn1", [bias_t, sink_rows, mask, wa2, wx2]))
    sm = dict(sm, conv_w=wg["conv_w"])

    h1, a1, b1, hm1, f1 = _ffn_fwd(x, row(sm["ffn1_pre_g"]), wg["ffn1_w1"], wg["ffn1_w3"], wg["ffn1_w2"],
                                   row(sm["ffn1_post_g"]), "ffn1_fwd")
    wg.update(weights("mix_in", h1))
    u, q, k, v, xr, xg, gate = _mix_proj(h1, row(sm["mix_pre_g"]), wg["w_in"], wg["w_gate"], row(sm["b_gate"]))
    token = weights("mix_out", u, begin=True)
    hr, yain, xc, r, ig, lru_a, lru_s = _rglru_fwd(xr, xg, sm["conv_w"], row(sm["conv_b"]), wa2, row(sm["rg_a_b"]), wx2,
                                                   row(sm["rg_x_b"]), row(sm["lru_lambda"]), token)
    token = weights("ffn2", hr, begin=True)
    kp = jnp.pad(k, ((PAD_KEYS, KP - KB), (0, 0)))
    vp = jnp.pad(v, ((PAD_KEYS, KP - KB), (0, 0)))
    o = _attn_fwd(sink_rows, q, kp, vp, bias_t, mask, token)
    wg.update(weights("mix_out", o))
    w_lru = wg["w_lru_out"].reshape(D, D)
    w_att = wg["w_attn_out"].reshape(D, D)
    w_o = wg["w_o"].reshape(D, D)
    wg.update(weights("ffn2", o))
    h2, mo, merged, ya, yb = _merge_fwd(yain, o, gate, h1, w_lru, w_att, w_o, row(sm["mix_post_g"]))
    dy, a2, b2, hm2, f2, sq = _ffn_fwd(h2, row(sm["ffn2_pre_g"]), wg["ffn2_w1"], wg["ffn2_w3"], wg["ffn2_w2"],
                                       row(sm["ffn2_post_g"]), "ffn2_fwd", target)

    big, small = {}, {}
    dh2, n2, da2, db2, df2, small["ffn2_pre_g"], small["ffn2_post_g"] = _ffn_bwd(
        dy, h2, f2, a2, b2, row(sm["ffn2_pre_g"]), row(sm["ffn2_post_g"]), wg["ffn2_w1"], wg["ffn2_w3"], wg["ffn2_w2"],
        "ffn2_bwd")
    big["ffn2_w1"] = _wgrad_rows(da2, n2, "dw_ffn2_w1")
    big["ffn2_w3"] = _wgrad_rows(db2, n2, "dw_ffn2_w3")
    big["ffn2_w2"] = _wgrad_rows(hm2, df2, "dw_ffn2_w2")
    token = reducer.begin("ffn2", {n: big[n] for n in ("ffn2_w1", "ffn2_w3", "ffn2_w2")})
    dmo, dya, dyb, dgate, dhr, dxg, do, small["mix_post_g"], small["b_gate"] = _mix_bwd1(
        dh2, mo, row(sm["mix_post_g"]), gate, ya, yb, xg, hr, w_o, w_lru, w_att, token)
    big["w_o"] = _wgrad_sq(merged, dmo, "dw_w_o").reshape(NSH, D // NSH, D)
    big["w_lru_out"] = _wgrad_sq(yain, dya, "dw_w_lru_out").reshape(NSH, D // NSH, D)
    big["w_attn_out"] = _wgrad_sq(o, dyb, "dw_w_attn_out").reshape(NSH, D // NSH, D)
    token = reducer.advance("ffn2", big["w_attn_out"])
    (dxr, dwa2, dwx2, small["rg_a_b"], small["rg_x_b"], small["lru_lambda"], small["conv_w"], small["conv_b"]) = _rglru_bwd(
        dhr, hr, xc, r, ig, lru_a, lru_s, xr, sm["conv_w"], wa2, wx2, row(sm["lru_lambda"]), token)
    small["rg_a_w"] = _unpair_blocks(dwa2)
    small["rg_x_w"] = _unpair_blocks(dwx2)
    dq, dkp, dvp, dbias_t, ds_rows = _attn_bwd(sink_rows, q, kp, vp, bias_t, mask, do)
    dbias = jnp.transpose(dbias_t.reshape(4, KP, 4, CHUNK)[:, :KB], (0, 2, 3, 1)).reshape(N_HEADS, CHUNK * KB)
    drel_t, dsinks = _bias_bwd(dbias, onehot_t, ds_rows)
    small["attn_sinks"] = dsinks[0:4, 0:4].reshape(N_HEADS)
    small["rel_bias"] = drel_t.T
    t = x.shape[0]
    dproj = jnp.concatenate([dq, dkp[PAD_KEYS:PAD_KEYS + t].astype(BF16), dvp[PAD_KEYS:PAD_KEYS + t].astype(BF16), dxr, dxg],
                            axis=1)
    big["w_in"] = _wgrad_cols(u, dproj, IN_S, "dw_w_in")
    big["w_gate"] = _wgrad_cols(u, dgate, GATE_S, "dw_w_gate")
    token = reducer.begin("mix", {n: big[n] for n in ("w_in", "w_gate", "w_lru_out", "w_attn_out", "w_o")})
    dh1, small["mix_pre_g"] = _mix_bwd2(dproj, dgate, h1, dh2, row(sm["mix_pre_g"]), wg["w_in"], wg["w_gate"], token)
    n1, da1, db1, df1, small["ffn1_post_g"] = _ffn_bwd_acts(
        dh1, x, f1, a1, b1, row(sm["ffn1_pre_g"]), row(sm["ffn1_post_g"]), wg["ffn1_w2"], "ffn1_bwd_acts")
    token = reducer.advance("mix", df1)
    big["ffn1_w1"] = _wgrad_rows(da1, n1, "dw_ffn1_w1", token)
    big["ffn1_w3"] = _wgrad_rows(db1, n1, "dw_ffn1_w3", token)
    big["ffn1_w2"] = _wgrad_rows(hm1, df1, "dw_ffn1_w2", token)
    token = reducer.begin("ffn1", {n: big[n] for n in ("ffn1_w1", "ffn1_w3", "ffn1_w2")})
    dx, small["ffn1_pre_g"] = _ffn_bwd_input(dh1, x, da1, db1, row(sm["ffn1_pre_g"]), wg["ffn1_w1"], wg["ffn1_w3"],
                                             "ffn1_bwd_input", token)
    return sq, dx, big, small


_ANY = pl.BlockSpec(memory_space=pl.ANY)


def _place():
    return lax.axis_index("x"), lax.axis_index("y"), lax.axis_index("c")


def _other_chips(x, y):
    return [(1 - x, y), (x, 1 - y), (1 - x, 1 - y)]


_HBM = pl.BlockSpec(memory_space=pltpu.HBM)
_SEM = pl.BlockSpec(memory_space=pltpu.SEMAPHORE)
_EFFECT = pltpu.SideEffectType.DATAFLOW_SIDE_EFFECTING


def _cast_into_slot(w, chip, name, after=None):
    r, cc = w.shape
    rows = r // 4

    def body(chip_ref, *refs):
        w_ref, o_ref = refs[-2:]
        o_ref[...] = w_ref[...].astype(BF16)

    extra = [] if after is None else [after]
    return pl.pallas_call(
        body, name=name, out_shape=jax.ShapeDtypeStruct((NSH, r, cc), BF16),
        grid_spec=pltpu.PrefetchScalarGridSpec(
            num_scalar_prefetch=1, grid=(4,), in_specs=[_ANY] * len(extra) + [pl.BlockSpec((rows, cc), lambda i, chip: (i, 0))],
            out_specs=pl.BlockSpec((None, rows, cc), lambda i, chip: (chip[0], i, 0))),
        compiler_params=_params("arbitrary"))(chip, *extra, w)


def _piece(ref, slot, c):
    if ref.dtype == F32:
        return ref.at[slot]
    rh = ref.shape[1] // 2
    return ref.at[slot, pl.ds(pl.multiple_of(c * rh, 16), rh), :]


def _gather_start(stages, name):
    flat = [b for stage in stages for b in stage]
    n, ns = len(flat), len(stages)

    def body(*refs):
        ins, sems, token = refs[:n], refs[n:n + 2 * ns], refs[-1]
        x, y, c = _place()
        me = 2 * x + y
        k = 0
        for s, stage in enumerate(stages):
            for i in range(len(stage)):
                for j, (px, py) in enumerate(_other_chips(x, y)):
                    piece = _piece(ins[k], me, c)
                    pltpu.make_async_remote_copy(src_ref=piece, dst_ref=piece, send_sem=sems[2 * s].at[3 * i + j],
                                                 recv_sem=sems[2 * s + 1].at[3 * i + j], device_id=(px, py, c),
                                                 device_id_type=MESH).start()
                k += 1
        token[...] = jnp.zeros_like(token)

    sem_shapes = [pltpu.SemaphoreType.DMA((3 * len(stage),)) for stage in stages for _ in range(2)]
    outs = pl.pallas_call(
        body, name=name, in_specs=[_HBM] * n,
        out_specs=[_SEM] * (2 * ns) + [_HBM] * n + [pl.BlockSpec(memory_space=pltpu.VMEM)],
        out_shape=sem_shapes + [pltpu.HBM(b.shape, b.dtype) for b in flat] + [jax.ShapeDtypeStruct((8, 128), F32)],
        input_output_aliases={i: 2 * ns + i for i in range(n)},
        compiler_params=pltpu.CompilerParams(has_side_effects=_EFFECT),
    )(*[pltpu.with_memory_space_constraint(b, pltpu.HBM) for b in flat])
    sems, bufs, token = outs[:2 * ns], list(outs[2 * ns:2 * ns + n]), outs[-1]
    per_stage, k = [], 0
    for s, stage in enumerate(stages):
        per_stage.append((sems[2 * s], sems[2 * s + 1], bufs[k:k + len(stage)]))
        k += len(stage)
    return per_stage, token


def _gather_wait(send_sems, recv_sems, bufs, after, name):
    n = len(bufs)

    def body(*refs):
        ins, ssem, rsem = refs[:n], refs[n], refs[n + 1]
        x, y, c = _place()
        me = 2 * x + y
        for i in range(n):
            for j, (px, py) in enumerate(_other_chips(x, y)):
                cp = pltpu.make_async_remote_copy(src_ref=_piece(ins[i], me, c), dst_ref=_piece(ins[i], 2 * px + py, c),
                                                  send_sem=ssem.at[3 * i + j], recv_sem=rsem.at[3 * i + j],
                                                  device_id=(px, py, c), device_id_type=MESH)
                cp.wait_send()
                cp.wait_recv()

    afters = list(after) if isinstance(after, (list, tuple)) else [after]
    return pl.pallas_call(
        body, name=name, in_specs=[_HBM] * n + [_SEM, _SEM] + [_ANY] * len(afters), out_specs=[_HBM] * n,
        out_shape=[pltpu.HBM(b.shape, b.dtype) for b in bufs], input_output_aliases={i: i for i in range(n)},
        compiler_params=pltpu.CompilerParams(has_side_effects=_EFFECT),
    )(*bufs, send_sems, recv_sems, *afters)


def _sibling_fill(bufs, name):
    n = len(bufs)

    def body(*refs):
        ins, outs = refs[:n], refs[n:2 * n]
        send_sems, recv_sems = refs[2 * n:]
        x, y, c = _place()
        copies = []
        for i in range(n):
            for j, (px, py) in enumerate(_other_chips(x, y)):
                copies.append(pltpu.make_async_remote_copy(
                    src_ref=_piece(ins[i], 2 * px + py, c), dst_ref=_piece(outs[i], 2 * px + py, c),
                    send_sem=send_sems.at[3 * i + j], recv_sem=recv_sems.at[3 * i + j], device_id=(x, y, 1 - c),
                    device_id_type=MESH))
                copies[-1].start()
        for cp in copies:
            cp.wait()

    return pl.pallas_call(
        body, name=name, in_specs=[_ANY] * n, out_specs=[_ANY] * n,
        out_shape=[jax.ShapeDtypeStruct(b.shape, b.dtype) for b in bufs], input_output_aliases={i: i for i in range(n)},
        scratch_shapes=[pltpu.SemaphoreType.DMA((3 * n,)), pltpu.SemaphoreType.DMA((3 * n,))],
        compiler_params=pltpu.CompilerParams(has_side_effects=True),
    )(*bufs)


def _swap_plan(srcs, lands):
    x, y, c = _place()
    plan = []
    for src, land in zip(srcs, lands):
        rh = src.shape[1] // 2
        plan.append((src.at[:, pl.ds(pl.multiple_of((1 - c) * rh, 16), rh), :], land, (x, y, 1 - c)))
    return plan


def _owners_plan(srcs, lands):
    x, y, c = _place()
    return [(src.at[2 * px + py], land.at[j], (px, py, c))
            for src, land in zip(srcs, lands) for j, (px, py) in enumerate(_other_chips(x, y))]


def _exchange_start(srcs, lands, plan, copies, name):
    n, m = len(srcs), len(srcs) + len(lands)

    def body(*refs):
        send_sems, recv_sems, token = refs[m], refs[m + 1], refs[-1]
        for k, (src, dst, dev) in enumerate(plan(refs[:n], refs[n:m])):
            pltpu.make_async_remote_copy(src_ref=src, dst_ref=dst, send_sem=send_sems.at[k], recv_sem=recv_sems.at[k],
                                         device_id=dev, device_id_type=MESH).start()
        token[...] = jnp.zeros_like(token)

    both = list(srcs) + list(lands)
    outs = pl.pallas_call(
        body, name=name, in_specs=[_HBM] * m,
        out_specs=[_SEM, _SEM] + [_HBM] * m + [pl.BlockSpec(memory_space=pltpu.VMEM)],
        out_shape=[pltpu.SemaphoreType.DMA((copies,)), pltpu.SemaphoreType.DMA((copies,))]
        + [pltpu.HBM(b.shape, b.dtype) for b in both] + [jax.ShapeDtypeStruct((8, 128), F32)],
        input_output_aliases={i: 2 + i for i in range(m)},
        compiler_params=pltpu.CompilerParams(has_side_effects=_EFFECT),
    )(*[pltpu.with_memory_space_constraint(b, pltpu.HBM) for b in both])
    return (outs[0], outs[1]), list(outs[2:2 + n]), list(outs[2 + n:2 + m]), outs[-1]


def _exchange_wait(sems, srcs, lands, plan, after, name):
    n, m = len(srcs), len(srcs) + len(lands)

    def body(*refs):
        send_sems, recv_sems = refs[m], refs[m + 1]
        for k, (src, dst, dev) in enumerate(plan(refs[:n], refs[n:m])):
            cp = pltpu.make_async_remote_copy(src_ref=src, dst_ref=dst, send_sem=send_sems.at[k], recv_sem=recv_sems.at[k],
                                              device_id=dev, device_id_type=MESH)
            cp.wait_send()
            cp.wait_recv()

    both = list(srcs) + list(lands)
    afters = list(after) if isinstance(after, (list, tuple)) else [after]
    outs = pl.pallas_call(
        body, name=name, in_specs=[_HBM] * m + [_SEM, _SEM] + [_ANY] * len(afters), out_specs=[_HBM] * m,
        out_shape=[pltpu.HBM(b.shape, b.dtype) for b in both], input_output_aliases={i: i for i in range(m)},
        compiler_params=pltpu.CompilerParams(has_side_effects=_EFFECT),
    )(*both, sems[0], sems[1], *afters)
    return list(outs[:n]), list(outs[n:])


def _fill_plan(bufs, _):
    x, y, c = _place()
    return [(_piece(buf, 2 * px + py, c), _piece(buf, 2 * px + py, c), (x, y, 1 - c))
            for buf in bufs for px, py in _other_chips(x, y)]


class _Reducer:
    def __init__(self, where):
        self.state = {}
        self.where = where

    def begin(self, stage, grads):
        names = list(grads)
        full = [grads[n] for n in names]
        lands = [lax.empty((NSH, g.shape[1] // 2, g.shape[2]), g.dtype) for g in full]
        sems, full, lands, token = _exchange_start(full, lands, _swap_plan, len(full), "swap_start_" + stage)
        self.state[stage] = (names, sems, full, lands)
        return token

    def advance(self, stage, after):
        names, sems, full, lands = self.state[stage]
        full, got = _exchange_wait(sems, full, lands, _swap_plan, after, "swap_wait_" + stage)
        sums, own = _chip_sums(full, got, self.where, "chip_sums_" + stage)
        lands = [lax.empty((3,) + s.shape[1:], BF16) for s in sums]
        sems, sent, lands, token = _exchange_start(sums, lands, _owners_plan, 3 * len(sums), "owners_start_" + stage)
        self.state[stage] = (names, own, sems, sent, lands)
        return token

    def finish(self, stage, after):
        names, own, sems, sent, lands = self.state[stage]
        _, got = _exchange_wait(sems, sent, lands, _owners_plan, after, "owners_wait_" + stage)
        return dict(zip(names, _owner_sums(own, got, "owner_sums_" + stage)))


def _chip_sums(gs, gots, where, name):
    n = len(gs)

    def body(where_ref, *refs):
        g_refs, got_refs, hb_refs, own_refs = (refs[k * n:(k + 1) * n] for k in range(4))
        mine = pl.program_id(0) == where_ref[1]
        for g_ref, got_ref, hb_ref, own_ref in zip(g_refs, got_refs, hb_refs, own_refs):
            h = g_ref[...].astype(F32) + got_ref[...].astype(F32)
            hb_ref[...] = h.astype(BF16)

            @pl.when(mine)
            def _():
                own_ref[...] = h

    halves = [(g.shape[1] // 2, g.shape[2]) for g in gs]
    slot = [pl.BlockSpec((None, rh, cc), lambda s, where: (s, 0, 0)) for rh, cc in halves]
    outs = pl.pallas_call(
        body, name=name,
        grid_spec=pltpu.PrefetchScalarGridSpec(
            num_scalar_prefetch=1, grid=(NSH,),
            in_specs=[pl.BlockSpec((None, rh, cc), lambda s, where: (s, where[0], 0)) for rh, cc in halves] + slot,
            out_specs=slot + [pl.BlockSpec((rh, cc), lambda s, where: (0, 0)) for rh, cc in halves]),
        out_shape=[jax.ShapeDtypeStruct((NSH, rh, cc), BF16) for rh, cc in halves]
        + [jax.ShapeDtypeStruct((rh, cc), F32) for rh, cc in halves],
        compiler_params=_params("arbitrary"),
    )(where, *gs, *gots)
    return list(outs[:n]), list(outs[n:])


def _owner_sums(owns, gots, name):
    n = len(owns)

    def body(*refs):
        own_refs, got_refs, o_refs = (refs[k * n:(k + 1) * n] for k in range(3))
        for own_ref, got_ref, o_ref in zip(own_refs, got_refs, o_refs):
            o_ref[...] = ((own_ref[...] + got_ref[0].astype(F32)) + got_ref[1].astype(F32)) + got_ref[2].astype(F32)

    blocks = [(o.shape[0] // 2, o.shape[1]) for o in owns]
    rows = [pl.BlockSpec(b, lambda i: (i, 0)) for b in blocks]
    return pl.pallas_call(
        body, grid=(2,), name=name,
        in_specs=rows + [pl.BlockSpec((3,) + b, lambda i: (0, i, 0)) for b in blocks], out_specs=rows,
        out_shape=[jax.ShapeDtypeStruct(o.shape, F32) for o in owns], compiler_params=_params("arbitrary"),
    )(*owns, *gots)


def _sibling_plan(srcs, lands):
    x, y, c = _place()
    return [(src, land, (x, y, 1 - c)) for src, land in zip(srcs, lands)]


def _all_reduce_small(part):
    def body(p_ref, o_ref, rbuf, send1, recv1, send2, recv2):
        x, y, c = _place()
        me = 4 * x + 2 * y + c
        peers = []
        for k in range(1, 8):
            px, py, pc = x ^ ((k >> 2) & 1), y ^ ((k >> 1) & 1), c ^ (k & 1)
            peers.append((k, (px, py, pc), 4 * px + 2 * py + pc))

        def rows(d):
            return pl.ds(pl.multiple_of(d * SMALL_SLICE, 8), SMALL_SLICE)

        first = [pltpu.make_async_remote_copy(src_ref=p_ref.at[rows(idx), :], dst_ref=rbuf.at[me], send_sem=send1.at[k],
                                              recv_sem=recv1.at[k], device_id=dev, device_id_type=MESH)
                 for k, dev, idx in peers]
        for cp in first:
            cp.start()
        rbuf[me] = p_ref[rows(me), :]
        for k, dev, idx in peers:
            pltpu.make_async_remote_copy(src_ref=p_ref.at[rows(idx), :], dst_ref=rbuf.at[idx], send_sem=send1.at[k],
                                         recv_sem=recv1.at[k], device_id=dev, device_id_type=MESH).wait_recv()
        acc = rbuf[0]
        for d in range(1, 8):
            acc = acc + rbuf[d]
        o_ref[rows(me), :] = acc
        second = [pltpu.make_async_remote_copy(src_ref=o_ref.at[rows(me), :], dst_ref=o_ref.at[rows(me), :],
                                               send_sem=send2.at[k], recv_sem=recv2.at[k], device_id=dev, device_id_type=MESH)
                  for k, dev, idx in peers]
        for cp in second:
            cp.start()
        for k, dev, idx in peers:
            pltpu.make_async_remote_copy(src_ref=o_ref.at[rows(me), :], dst_ref=o_ref.at[rows(idx), :], send_sem=send2.at[k],
                                         recv_sem=recv2.at[k], device_id=dev, device_id_type=MESH).wait_recv()
        for cp in first + second:
            cp.wait_send()

    return pl.pallas_call(
        body, name="all_reduce_small", in_specs=[_WHOLE], out_specs=_WHOLE,
        out_shape=jax.ShapeDtypeStruct((SMALL_ROWS, 128), F32),
        scratch_shapes=[pltpu.VMEM((8, SMALL_SLICE, 128), F32)] + [pltpu.SemaphoreType.DMA((8,))] * 4,
        compiler_params=pltpu.CompilerParams(has_side_effects=True),
    )(part)


def _adamw_update(w, gv, m, v):
    nm = ADAM_B1 * m + (1.0 - ADAM_B1) * gv
    nv = ADAM_B2 * v + (1.0 - ADAM_B2) * (gv * gv)
    m_hat = nm / (1.0 - ADAM_B1 ** ADAM_STEP)
    v_hat = nv / (1.0 - ADAM_B2 ** ADAM_STEP)
    return -ADAM_LR * (m_hat / (jnp.sqrt(v_hat) + ADAM_EPS) + ADAM_WD * w), nm, nv


def _adamw_small(ws, gs, ms, vs):
    n = len(ws)

    def body(*refs):
        w_refs, g_refs, m_refs, v_refs, d_refs, nm_refs, nv_refs = (refs[k * n:(k + 1) * n] for k in range(7))
        for i in range(n):
            d_refs[i][...], nm_refs[i][...], nv_refs[i][...] = _adamw_update(
                w_refs[i][...], g_refs[i][...], m_refs[i][...], v_refs[i][...])

    out = [jax.ShapeDtypeStruct(w.shape, F32) for w in ws]
    outs = pl.pallas_call(body, in_specs=[_WHOLE] * (4 * n), out_specs=[_WHOLE] * (3 * n), out_shape=out * 3,
                          name="adamw_small", compiler_params=_params())(*ws, *gs, *ms, *vs)
    return outs[:n], outs[n:2 * n], outs[2 * n:]


def _adamw_halves(ws, mines, theirs, ms, vs, name):
    n = len(ws)
    steps = 2

    def body(*refs):
        w_refs, mine_refs, theirs_refs, m_refs, v_refs, g_refs, d_refs, nm_refs, nv_refs = (
            refs[k * n:(k + 1) * n] for k in range(9))
        is_mine = pl.program_id(0) == lax.axis_index("c")
        for i in range(n):
            gv = jnp.where(is_mine, mine_refs[i][...], theirs_refs[i][...])
            g_refs[i][...] = gv
            d_refs[i][...], nm_refs[i][...], nv_refs[i][...] = _adamw_update(w_refs[i][...], gv, m_refs[i][...], v_refs[i][...])

    blocks = [(h.shape[0] // steps, h.shape[1]) for h in mines]
    whole = [pl.BlockSpec(b, lambda h, i: (steps * h + i, 0)) for b in blocks]
    half = [pl.BlockSpec(b, lambda h, i: (i, 0)) for b in blocks]
    out = [jax.ShapeDtypeStruct(w.shape, F32) for w in ws]
    outs = pl.pallas_call(body, grid=(2, steps), in_specs=whole + half + half + whole + whole, out_specs=whole * 4,
                          out_shape=out * 4, name=name, compiler_params=_params("arbitrary", "arbitrary"),
                          )(*ws, *mines, *theirs, *ms, *vs)
    return [tuple(outs[k * n + i] for k in range(4)) for i in range(n)]


SMALL_USED = sum(size for _, size in SMALL) // 128


def _pack_small(vals, tail=None):
    parts = []
    for name, size in SMALL:
        flat = vals[name].reshape(-1).astype(F32)
        parts.append(jnp.pad(flat, (0, size - flat.shape[0])))
    if tail is not None:
        parts.append(tail.reshape(128))
    flat = jnp.concatenate(parts)
    return jnp.pad(flat, (0, SMALL_ROWS * 128 - flat.shape[0])).reshape(SMALL_ROWS, 128)


def _unpack_small(packed, shapes):
    flat = packed.reshape(-1)
    out, off = {}, 0
    for name, size in SMALL:
        n = math.prod(shapes[name])
        out[name] = flat[off:off + n].reshape(shapes[name])
        off += size
    return out


def kernel(x, ffn1_pre_g, ffn1_w1, ffn1_w3, ffn1_w2, ffn1_post_g, mix_pre_g, w_in, conv_w, conv_b, rg_a_w, rg_a_b, rg_x_w, rg_x_b, lru_lambda, w_lru_out, attn_sinks, rel_bias, w_attn_out, w_gate, b_gate, w_o, mix_post_g, ffn2_pre_g, ffn2_w1, ffn2_w3, ffn2_w2, ffn2_post_g, loss_target, m_ffn1_pre_g, m_ffn1_w1, m_ffn1_w3, m_ffn1_w2, m_ffn1_post_g, m_mix_pre_g, m_w_in, m_conv_w, m_conv_b, m_rg_a_w, m_rg_a_b, m_rg_x_w, m_rg_x_b, m_lru_lambda, m_w_lru_out, m_attn_sinks, m_rel_bias, m_w_attn_out, m_w_gate, m_b_gate, m_w_o, m_mix_post_g, m_ffn2_pre_g, m_ffn2_w1, m_ffn2_w3, m_ffn2_w2, m_ffn2_post_g, v_ffn1_pre_g, v_ffn1_w1, v_ffn1_w3, v_ffn1_w2, v_ffn1_post_g, v_mix_pre_g, v_w_in, v_conv_w, v_conv_b, v_rg_a_w, v_rg_a_b, v_rg_x_w, v_rg_x_b, v_lru_lambda, v_w_lru_out, v_attn_sinks, v_rel_bias, v_w_attn_out, v_w_gate, v_b_gate, v_w_o, v_mix_post_g, v_ffn2_pre_g, v_ffn2_w1, v_ffn2_w3, v_ffn2_w2, v_ffn2_post_g):
    given = dict(locals())
    chip = 2 * lax.axis_index("x") + lax.axis_index("y")
    transposed = ("ffn1_w1", "ffn1_w3", "ffn2_w1", "ffn2_w3")

    def shard(name, moment=""):
        w = given[moment + name][0]
        return w.T if name in transposed else w

    def unshard(name, w):
        return (w.T if name in transposed else w)[None]

    def only_my_columns(a):
        parts = a.reshape(1, 4, NSH, D // NSH)
        return sum(jnp.where(chip == s, parts[:, :, s], 0.0) for s in range(NSH))

    chip_arr = jnp.reshape(chip, (1,)).astype(jnp.int32)
    stage_names = {"ffn1": ["ffn1_w1", "ffn1_w3", "ffn1_w2", "conv_w"],
                   "mix_in": ["w_in", "w_gate"],
                   "mix_out": ["w_lru_out", "w_attn_out", "w_o"],
                   "ffn2": ["ffn2_w1", "ffn2_w3", "ffn2_w2"]}
    in_flight, started = {}, None
    for stage, names in stage_names.items():
        bufs = [jnp.where(lax.broadcasted_iota(jnp.int32, (NSH, 4, D // NSH), 0) == chip, given[n], 0.0) if n == "conv_w"
                else _cast_into_slot(shard(n), chip_arr, "cast_" + n, started) for n in names]
        (in_flight[stage],), started = _gather_start([bufs], "gather_start_" + stage)
    all_started = started

    filling = {}

    def weights(stage, after, begin=False):
        names = stage_names[stage]
        halves_of = [n for n in names if n != "conv_w"]
        if stage in filling:
            filled, _ = _exchange_wait(filling.pop(stage), *filling.pop(stage + "/bufs"), _fill_plan, after,
                                       "fill_wait_" + stage)
            return dict(zip(halves_of, filled))
        send_sems, recv_sems, landing = in_flight[stage]
        if stage == "ffn1":
            after = [all_started] + list(after)
        landed = dict(zip(names, _gather_wait(send_sems, recv_sems, landing, after, "gather_wait_" + stage)))
        halves = [landed[n] for n in halves_of]
        if begin:
            filling[stage], bufs, _, token = _exchange_start(halves, [], _fill_plan, 3 * len(halves), "fill_start_" + stage)
            filling[stage + "/bufs"] = (bufs, [])
            return token
        out = dict(zip(halves_of, _sibling_fill(halves, "sibling_fill_" + stage)))
        if "conv_w" in names:
            out["conv_w"] = jnp.transpose(landed["conv_w"], (1, 0, 2)).reshape(4, D)
        return out

    small_shapes = {n: given[n].shape for n, _ in SMALL}
    small_shapes["conv_w"] = (1, 4, D)
    sm = {n: (given[n][0] if given[n].shape[0] == 1 and n != "rel_bias" else given[n]) for n, _ in SMALL if n != "conv_w"}

    reducer = _Reducer(jnp.stack([lax.axis_index("c"), chip]).astype(jnp.int32))
    sq, dx, _, small = _local_step(x[0], loss_target[0], weights, sm, reducer)

    last_started = reducer.advance("ffn1", dx)
    reduced_small = _all_reduce_small(_pack_small(small, tail=sq + last_started[0:1]))
    loss = reduced_small[SMALL_USED, 0] * (0.5 / D)
    small_g = _unpack_small(reduced_small, small_shapes)
    grads, delta, new_m, new_v = {}, {}, {}, {}
    in_transit = {}

    def send(stage, after):
        halves = reducer.finish(stage, after)
        lands = [lax.empty(h.shape, F32) for h in halves.values()]
        sems, mine, lands, token = _exchange_start(list(halves.values()), lands, _sibling_plan, len(lands),
                                                   "halves_start_" + stage)
        in_transit[stage] = (list(halves), sems, mine, lands)
        return token

    def update(stage, after):
        names, sems, mine, lands = in_transit[stage]
        mine, theirs = _exchange_wait(sems, mine, lands, _sibling_plan, after, "halves_wait_" + stage)
        updated = _adamw_halves([shard(n) for n in names], mine, theirs, [shard(n, "m_") for n in names],
                                [shard(n, "v_") for n in names], "adamw_" + stage)
        for n, results in zip(names, updated):
            grads[n], delta[n], new_m[n], new_v[n] = (unshard(n, r) for r in results)
        return new_v[names[-1]]

    token = send("ffn2", [reduced_small, last_started])
    token = send("mix", token)
    done = update("ffn2", token)
    done = update("mix", done)
    token = send("ffn1", done)
    update("ffn1", token)

    small_g["conv_w"] = only_my_columns(small_g["conv_w"])
    names = [n for n, _ in SMALL]
    flat2d = lambda a: a.reshape(-1, a.shape[-1])
    outs = _adamw_small(*[[flat2d(given[pre + n]) if pre != "g" else flat2d(small_g[n]) for n in names]
                          for pre in ("", "g", "m_", "v_")])
    for dst, arrs in zip((delta, new_m, new_v), outs):
        dst.update({n: a.reshape(given[n].shape) for n, a in zip(names, arrs)})
    grads.update(small_g)
    return (loss, dx[None], *[grads[n] for n in WEIGHTS], *[delta[n] for n in WEIGHTS], *[new_m[n] for n in WEIGHTS],
            *[new_v[n] for n in WEIGHTS])
```

```python
import functools
import math

import jax
import jax.numpy as jnp
from jax import lax
from jax.experimental import pallas as pl
from jax.experimental.pallas import tpu as pltpu

F32, BF16 = jnp.float32, jnp.bfloat16
D = 1024
NSH = 4
FF_S = 704
IN_S = 896
GATE_S = 512
KV_W = 256
CHUNK = 64
KB = 192
N_HEADS = 16
HEAD_DIM = 64
N_BUCKETS = 32
KP = 192
PAD_KEYS = 128
RMS_EPS = 1e-6
NEG_INF = -1e30
LRU_C = 8.0
TM = 512
TM_SCAN = 256
VMEM_LIMIT = 56 * 1024 * 1024
ADAM_LR, ADAM_B1, ADAM_B2, ADAM_EPS, ADAM_WD, ADAM_STEP = 0.001, 0.9, 0.999, 1e-08, 0.01, 10
SMALL_ROWS = 1216
SMALL_SLICE = SMALL_ROWS // 8
MESH = pl.DeviceIdType.MESH

BIG = ["ffn1_w1", "ffn1_w3", "ffn1_w2", "w_in", "w_lru_out", "w_attn_out", "w_gate", "w_o", "ffn2_w1", "ffn2_w3", "ffn2_w2"]
SMALL = [("ffn1_pre_g", 1024), ("ffn1_post_g", 1024), ("mix_pre_g", 1024), ("conv_w", 4096), ("conv_b", 1024),
         ("rg_a_w", 65536), ("rg_a_b", 1024), ("rg_x_w", 65536), ("rg_x_b", 1024), ("lru_lambda", 1024),
         ("attn_sinks", 1024), ("rel_bias", 1024), ("b_gate", 2048), ("mix_post_g", 1024), ("ffn2_pre_g", 1024),
         ("ffn2_post_g", 1024)]
WEIGHTS = ["ffn1_pre_g", "ffn1_w1", "ffn1_w3", "ffn1_w2", "ffn1_post_g", "mix_pre_g", "w_in", "conv_w", "conv_b", "rg_a_w",
           "rg_a_b", "rg_x_w", "rg_x_b", "lru_lambda", "w_lru_out", "attn_sinks", "rel_bias", "w_attn_out", "w_gate", "b_gate",
           "w_o", "mix_post_g", "ffn2_pre_g", "ffn2_w1", "ffn2_w3", "ffn2_w2", "ffn2_post_g"]


def _params(*sem):
    return pltpu.CompilerParams(dimension_semantics=sem or None, vmem_limit_bytes=VMEM_LIMIT)


def _nn(a, b):
    return jnp.dot(a, b, preferred_element_type=F32)


def _nt(a, b):
    return lax.dot_general(a, b, (((1,), (1,)), ((), ())), preferred_element_type=F32)


def _tn(a, b):
    return lax.dot_general(a, b, (((0,), (0,)), ((), ())), preferred_element_type=F32)


def _rms(x, g):
    rstd = lax.rsqrt(jnp.mean(x * x, axis=-1, keepdims=True) + RMS_EPS)
    return (x * rstd) * g


def _rms_bwd(dout, x, g):
    rstd = lax.rsqrt(jnp.mean(x * x, axis=-1, keepdims=True) + RMS_EPS)
    xhat = x * rstd
    dg = jnp.sum(dout * xhat, axis=0, keepdims=True)
    dxhat = dout * g
    dx = rstd * (dxhat - xhat * jnp.mean(dxhat * xhat, axis=-1, keepdims=True))
    return dx, dg


_GELU_K = math.sqrt(2.0 / math.pi)


def _gelu(x):
    return x * (0.5 * (1.0 + jnp.tanh(_GELU_K * (x + 0.044715 * (x * x * x)))))


def _gelu_and_grad(x):
    x2 = x * x
    t = jnp.tanh(_GELU_K * (x + 0.044715 * (x2 * x)))
    cdf = 0.5 * (1.0 + t)
    return x * cdf, cdf + x * (0.5 * (1.0 - t * t) * (_GELU_K * (1.0 + 3.0 * 0.044715 * x2)))


def _softplus_neg(lam):
    z = -lam
    u = jnp.exp(-jnp.abs(z))
    w = 1.0 + u
    log1p_u = jnp.where(w == 1.0, u, jnp.log(w) * (u / (w - 1.0)))
    return jnp.maximum(z, 0.0) + log1p_u


def _lru_coeffs(r, sp):
    log_a = (-LRU_C * r) * sp
    a = jnp.exp(log_a)
    t = jnp.tanh(log_a)
    s = jnp.sqrt(-2.0 * t / (1.0 - t))
    return a, s


def _row_spec(tm, width):
    return pl.BlockSpec((tm, width), lambda i: (i, 0))


def _vec_spec(width):
    return pl.BlockSpec((1, width), lambda i: (0, 0))


_WHOLE = pl.BlockSpec(memory_space=pltpu.VMEM)


def _tile(t, tm=TM):
    return min(tm, t)


def _ffn_fwd(x, gpre, w1g, w3g, w2g, gpost, name, target=None):
    t = x.shape[0]
    tm = _tile(t)
    last = target is not None

    def body(x_ref, gpre_ref, w1_ref, w3_ref, w2_ref, gpost_ref, *refs):
        t_ref, (h_ref, a_ref, b_ref, hm_ref, f_ref), l_ref = (refs[0] if last else None), refs[last:last + 5], refs[-1]
        xv = x_ref[...]
        nb = _rms(xv, gpre_ref[...]).astype(BF16)
        f = jnp.zeros((tm, D), F32)
        for s in range(NSH):
            a = _nt(nb, w1_ref[s])
            b = _nt(nb, w3_ref[s])
            hmb = ((a * jax.nn.sigmoid(a)) * b).astype(BF16)
            a_ref[s] = a.astype(BF16)
            b_ref[s] = b.astype(BF16)
            hm_ref[s] = hmb
            f = f + _nn(hmb, w2_ref[s])
        f_ref[...] = f
        h = xv + 0.5 * _rms(f, gpost_ref[...])
        if last:
            @pl.when(pl.program_id(0) == 0)
            def _():
                l_ref[...] = jnp.zeros_like(l_ref)

            e = h - t_ref[...]
            h_ref[...] = e * (1.0 / D)
            l_ref[...] += jnp.sum(jnp.sum(e * e, axis=0, keepdims=True), axis=1, keepdims=True)
        else:
            h_ref[...] = h

    sh = pl.BlockSpec((NSH, tm, FF_S), lambda i: (0, i, 0))
    act = jax.ShapeDtypeStruct((NSH, t, FF_S), BF16)
    return pl.pallas_call(
        body, grid=(t // tm,), name=name,
        in_specs=[_row_spec(tm, D), _vec_spec(D), _WHOLE, _WHOLE, _WHOLE, _vec_spec(D)] + [_row_spec(tm, D)] * last,
        out_specs=[_row_spec(tm, D), sh, sh, sh, _row_spec(tm, D)] + [pl.BlockSpec((1, 128), lambda i: (0, 0))] * last,
        out_shape=[jax.ShapeDtypeStruct((t, D), F32), act, act, act, jax.ShapeDtypeStruct((t, D), F32)]
        + [jax.ShapeDtypeStruct((1, 128), F32)] * last,
        compiler_params=_params("arbitrary"),
    )(x, gpre, w1g, w3g, w2g, gpost, *([target] if last else []))


def _mix_proj(h1, gmix, w_in_g, w_gate_g, b_gate):
    t = h1.shape[0]
    tm = _tile(t)

    def body(h_ref, g_ref, win_ref, wg_ref, bg_ref, u_ref, q_ref, k_ref, v_ref, xr_ref, xg_ref, gate_ref):
        ub = _rms(h_ref[...], g_ref[...]).astype(BF16)
        u_ref[...] = ub
        p0 = _nn(ub, win_ref[0])
        q_ref[:, 0:896] = p0.astype(BF16)
        p1 = _nn(ub, win_ref[1])
        q_ref[:, 896:1024] = p1[:, 0:128].astype(BF16)
        k_ref[...] = p1[:, 128:384].astype(BF16)
        v_ref[...] = p1[:, 384:640].astype(BF16)
        xr_ref[:, 0:256] = p1[:, 640:896]
        p2 = _nn(ub, win_ref[2])
        xr_ref[:, 256:1024] = p2[:, 0:768]
        xg_ref[:, 0:128] = p2[:, 768:896]
        xg_ref[:, 128:1024] = _nn(ub, win_ref[3])
        for s in range(NSH):
            sl = slice(s * GATE_S, (s + 1) * GATE_S)
            gate_ref[:, sl] = jax.nn.sigmoid(_nn(ub, wg_ref[s]) + bg_ref[:, sl])

    return pl.pallas_call(
        body, grid=(t // tm,), name="mix_proj",
        in_specs=[_row_spec(tm, D), _vec_spec(D), _WHOLE, _WHOLE, _vec_spec(2 * D)],
        out_specs=[_row_spec(tm, D), _row_spec(tm, D), _row_spec(tm, KV_W), _row_spec(tm, KV_W), _row_spec(tm, D),
                   _row_spec(tm, D), _row_spec(tm, 2 * D)],
        out_shape=[jax.ShapeDtypeStruct((t, D), BF16), jax.ShapeDtypeStruct((t, D), BF16),
                   jax.ShapeDtypeStruct((t, KV_W), BF16), jax.ShapeDtypeStruct((t, KV_W), BF16),
                   jax.ShapeDtypeStruct((t, D), F32), jax.ShapeDtypeStruct((t, D), F32),
                   jax.ShapeDtypeStruct((t, 2 * D), F32)],
        compiler_params=_params("arbitrary"),
    )(h1, gmix, w_in_g, w_gate_g, b_gate)


def _rglru_fwd(xr, xg, conv_w, conv_b, wa2, ba, wx2, bx, lam, after=None):
    t = xr.shape[0]
    tm = _tile(t, TM_SCAN)
    nb8 = tm // 8

    def body(xr_ref, xrp_ref, xg_ref, cw_ref, cb_ref, wa_ref, ba_ref, wx_ref, bx_ref, lam_ref,
             hr_ref, yain_ref, xc_ref, r_ref, ig_ref, a_sc, s_ref, ext, h_sc):
        i = pl.program_id(0)

        @pl.when(i == 0)
        def _():
            h_sc[...] = jnp.zeros_like(h_sc)

        ext[0:8, :] = jnp.where(i == 0, 0.0, xrp_ref[...])
        ext[8:8 + tm, :] = xr_ref[...]
        xc = jnp.broadcast_to(cb_ref[...], (tm, D))
        for tap in range(4):
            xc = xc + ext[pl.ds(5 + tap, tm), :] * cw_ref[tap:tap + 1, :]
        xc_ref[...] = xc
        xcb = xc.astype(BF16)
        for p in range(8):
            sl = slice(p * 128, (p + 1) * 128)
            r_ref[:, sl] = jax.nn.sigmoid(_nn(xcb[:, sl], wa_ref[p]) + ba_ref[:, sl])
            ig_ref[:, sl] = jax.nn.sigmoid(_nn(xcb[:, sl], wx_ref[p]) + bx_ref[:, sl])
        a, s = _lru_coeffs(r_ref[...], _softplus_neg(lam_ref[...]))
        a_sc[...] = a
        s_ref[...] = s
        hr_ref[...] = s * (ig_ref[...] * xc)

        def blk(j, h):
            st = pl.multiple_of(j * 8, 8)
            a8 = a_sc[pl.ds(st, 8), :]
            u8 = hr_ref[pl.ds(st, 8), :]
            rows = []
            for k in range(8):
                h = a8[k:k + 1, :] * h + u8[k:k + 1, :]
                rows.append(h)
            hr_ref[pl.ds(st, 8), :] = jnp.concatenate(rows, axis=0)
            return h

        h_sc[0:1, :] = lax.fori_loop(0, nb8, blk, h_sc[0:1, :])
        yain_ref[...] = (hr_ref[...] * _gelu(xg_ref[...])).astype(BF16)

    prev = pl.BlockSpec((8, D), lambda i: (jnp.maximum(i * nb8 - 1, 0), 0))
    full = lambda shape: pl.BlockSpec(shape, lambda i: tuple(0 for _ in shape))
    f32 = jax.ShapeDtypeStruct((t, D), F32)
    body, specs, operands = _behind(body, after)
    return pl.pallas_call(
        body, grid=(t // tm,), name="rglru_fwd",
        in_specs=specs + [_row_spec(tm, D), prev, _row_spec(tm, D), full((4, D)), _vec_spec(D), full((8, 128, 128)),
                          _vec_spec(D), full((8, 128, 128)), _vec_spec(D), _vec_spec(D)],
        out_specs=[_row_spec(tm, D)] * 7,
        out_shape=[f32, jax.ShapeDtypeStruct((t, D), BF16), f32, f32, f32, f32, f32],
        scratch_shapes=[pltpu.VMEM((tm + 8, D), F32), pltpu.VMEM((8, D), F32)],
        compiler_params=_params("arbitrary"),
    )(*operands, xr, xr, xg, conv_w, conv_b, wa2, ba, wx2, bx, lam)


def _bias_fwd(table_t, onehot_t):
    def body(t_ref, e_ref, o_ref):
        o_ref[...] = jnp.dot(t_ref[...], e_ref[...], preferred_element_type=F32, precision=lax.Precision.HIGHEST)

    return pl.pallas_call(body, out_shape=jax.ShapeDtypeStruct((N_HEADS, CHUNK * KB), F32), name="bias_fwd",
                          compiler_params=_params())(table_t, onehot_t)


def _bias_bwd(dbias_flat, onehot_t, ds_rows):
    def body(d_ref, e_ref, s_ref, o_ref, so_ref):
        o_ref[...] = lax.dot_general(d_ref[...], e_ref[...], (((1,), (1,)), ((), ())), preferred_element_type=F32,
                                     precision=lax.Precision.HIGHEST)
        so_ref[...] = jnp.zeros_like(so_ref)
        for r in range(4):
            so_ref[:, r:r + 1] = jnp.sum(s_ref[:, r * CHUNK:(r + 1) * CHUNK], axis=1, keepdims=True)

    return pl.pallas_call(body, out_shape=[jax.ShapeDtypeStruct((N_HEADS, N_BUCKETS), F32), jax.ShapeDtypeStruct((8, 128), F32)],
                          name="bias_bwd", compiler_params=_params())(dbias_flat, onehot_t, ds_rows)


def _stack_heads(q):
    return jnp.concatenate(
        [jnp.concatenate([q[:, (4 * g + r) * HEAD_DIM:(4 * g + r + 1) * HEAD_DIM] for g in range(4)], axis=1)
         for r in range(4)], axis=0)


def _unstack_heads(o):
    return jnp.concatenate([o[r * CHUNK:(r + 1) * CHUNK, g * HEAD_DIM:(g + 1) * HEAD_DIM] for g in range(4) for r in range(4)],
                           axis=1)


def _block_diag(w, mask):
    return jnp.concatenate([w] * 4, axis=0) * mask


def _group_softmax(qk, bias_g, sink, valid):
    s = qk * (HEAD_DIM ** -0.5) + bias_g
    s = jnp.where(valid, s, NEG_INF)
    m = jnp.maximum(jnp.max(s, axis=0, keepdims=True), sink)
    e = jnp.exp(s - m)
    es = jnp.exp(sink - m)
    inv = 1.0 / (jnp.sum(e, axis=0, keepdims=True) + es)
    return e * inv, es * inv


def _attn_fwd(sink_rows, q, kp, vp, bias_t, mask, after=None):
    t = q.shape[0]
    per_step = 4

    def body(sink_ref, q_ref, kp_ref, vp_ref, bias_ref, mask_ref, o_ref):
        owns = [mask_ref[g * KP:(g + 1) * KP, :] for g in range(4)]
        for k in range(per_step):
            c = pl.program_id(0) * per_step + k
            rows = slice(k * CHUNK, (k + 1) * CHUNK)
            st = pl.multiple_of(c * CHUNK, CHUNK)
            kw = kp_ref[pl.ds(st, KP), :]
            vw = vp_ref[pl.ds(st, KP), :]
            q_all = _stack_heads(q_ref[rows, :])
            valid = lax.broadcasted_iota(jnp.int32, (KP, 1), 0) + c * CHUNK >= PAD_KEYS
            scores = [_nt(kw * owns[g], q_all) for g in range(4)]
            ps = [_group_softmax(scores[g], bias_ref[g * KP:(g + 1) * KP, :], sink_ref[g:g + 1, :], valid)[0]
                  for g in range(4)]
            o_all = sum(_tn(ps[g].astype(BF16), vw * owns[g]) for g in range(4))
            o_ref[rows, :] = _unstack_heads(o_all).astype(BF16)

    body, specs, operands = _behind(body, after)
    return pl.pallas_call(
        body, grid=(t // (per_step * CHUNK),), name="attn_fwd",
        in_specs=specs + [_WHOLE, _row_spec(per_step * CHUNK, D), _WHOLE, _WHOLE, _WHOLE, _WHOLE],
        out_specs=_row_spec(per_step * CHUNK, D),
        out_shape=jax.ShapeDtypeStruct((t, D), BF16),
        compiler_params=_params("arbitrary"),
    )(*operands, sink_rows, q, kp, vp, bias_t, mask)


def _merge_fwd(yain, o, gate, h1, w_lru, w_att, w_o, gpost):
    t = h1.shape[0]
    tm = _tile(t)

    def body(ya_ref, o_ref, g_ref, h_ref, wl_ref, wa_ref, wo_ref, gp_ref, h2_ref, mo_ref, mg_ref, ya_out, yb_out):
        ya = _nn(ya_ref[...], wl_ref[...])
        yb = _nn(o_ref[...], wa_ref[...])
        g0 = g_ref[:, 0:D]
        g1 = g_ref[:, D:2 * D]
        mg = (g0 * ya + g1 * yb).astype(BF16)
        mo = _nn(mg, wo_ref[...])
        ya_out[...] = (ya * (g0 * (1.0 - g0))).astype(BF16)
        yb_out[...] = (yb * (g1 * (1.0 - g1))).astype(BF16)
        mg_ref[...] = mg
        mo_ref[...] = mo
        h2_ref[...] = h_ref[...] + _rms(mo, gp_ref[...])

    f32 = jax.ShapeDtypeStruct((t, D), F32)
    b16 = jax.ShapeDtypeStruct((t, D), BF16)
    return pl.pallas_call(
        body, grid=(t // tm,), name="merge_fwd",
        in_specs=[_row_spec(tm, D), _row_spec(tm, D), _row_spec(tm, 2 * D), _row_spec(tm, D), _WHOLE, _WHOLE, _WHOLE,
                  _vec_spec(D)],
        out_specs=[_row_spec(tm, D)] * 5,
        out_shape=[f32, f32, b16, b16, b16],
        compiler_params=_params("arbitrary"),
    )(yain, o, gate, h1, w_lru, w_att, w_o, gpost)


def _ffn_bwd(dh, x, f, a, b, gpre, gpost, w1g, w3g, w2g, name):
    t = x.shape[0]
    tm = _tile(t, TM_SCAN)

    def body(dh_ref, x_ref, f_ref, a_ref, b_ref, gpre_ref, gpost_ref, w1_ref, w3_ref, w2_ref,
             dx_ref, n_ref, da_ref, db_ref, df_ref, dgpre_ref, dgpost_ref):
        @pl.when(pl.program_id(0) == 0)
        def _():
            dgpre_ref[...] = jnp.zeros_like(dgpre_ref)
            dgpost_ref[...] = jnp.zeros_like(dgpost_ref)

        dhv = dh_ref[...]
        xv = x_ref[...]
        df, dgp = _rms_bwd(0.5 * dhv, f_ref[...], gpost_ref[...])
        dgpost_ref[...] += dgp
        dfb = df.astype(BF16)
        df_ref[...] = dfb
        n_ref[...] = _rms(xv, gpre_ref[...]).astype(BF16)
        dn = jnp.zeros((tm, D), F32)
        for s in range(NSH):
            av = a_ref[s].astype(F32)
            bv = b_ref[s].astype(F32)
            sg = jax.nn.sigmoid(av)
            dhm = _nt(dfb, w2_ref[s])
            dab = (dhm * bv * (sg * (1.0 + av * (1.0 - sg)))).astype(BF16)
            dbb = (dhm * (av * sg)).astype(BF16)
            da_ref[s] = dab
            db_ref[s] = dbb
            dn = dn + _nn(dab, w1_ref[s]) + _nn(dbb, w3_ref[s])
        dxn, dg = _rms_bwd(dn, xv, gpre_ref[...])
        dgpre_ref[...] += dg
        dx_ref[...] = dhv + dxn

    sh = pl.BlockSpec((NSH, tm, FF_S), lambda i: (0, i, 0))
    act = jax.ShapeDtypeStruct((NSH, t, FF_S), BF16)
    vec = jax.ShapeDtypeStruct((1, D), F32)
    return pl.pallas_call(
        body, grid=(t // tm,), name=name,
        in_specs=[_row_spec(tm, D), _row_spec(tm, D), _row_spec(tm, D), sh, sh, _vec_spec(D), _vec_spec(D), _WHOLE, _WHOLE,
                  _WHOLE],
        out_specs=[_row_spec(tm, D), _row_spec(tm, D), sh, sh, _row_spec(tm, D), _vec_spec(D), _vec_spec(D)],
        out_shape=[jax.ShapeDtypeStruct((t, D), F32), jax.ShapeDtypeStruct((t, D), BF16), act, act,
                   jax.ShapeDtypeStruct((t, D), BF16), vec, vec],
        compiler_params=_params("arbitrary"),
    )(dh, x, f, a, b, gpre, gpost, w1g, w3g, w2g)


def _behind(body, after):
    if after is None:
        return body, [], []

    def ordered(_, *refs):
        body(*refs)

    return ordered, [_ANY], [after]


def _ffn_bwd_acts(dh, x, f, a, b, gpre, gpost, w2g, name):
    t = x.shape[0]
    tm = _tile(t)

    def body(dh_ref, x_ref, f_ref, a_ref, b_ref, gpre_ref, gpost_ref, w2_ref, n_ref, da_ref, db_ref, df_ref, dgpost_ref):
        @pl.when(pl.program_id(0) == 0)
        def _():
            dgpost_ref[...] = jnp.zeros_like(dgpost_ref)

        df, dgp = _rms_bwd(0.5 * dh_ref[...], f_ref[...], gpost_ref[...])
        dgpost_ref[...] += dgp
        dfb = df.astype(BF16)
        df_ref[...] = dfb
        n_ref[...] = _rms(x_ref[...], gpre_ref[...]).astype(BF16)
        for s in range(NSH):
            av = a_ref[s].astype(F32)
            bv = b_ref[s].astype(F32)
            sg = jax.nn.sigmoid(av)
            dhm = _nt(dfb, w2_ref[s])
            da_ref[s] = (dhm * bv * (sg * (1.0 + av * (1.0 - sg)))).astype(BF16)
            db_ref[s] = (dhm * (av * sg)).astype(BF16)

    sh = pl.BlockSpec((NSH, tm, FF_S), lambda i: (0, i, 0))
    act = jax.ShapeDtypeStruct((NSH, t, FF_S), BF16)
    b16 = jax.ShapeDtypeStruct((t, D), BF16)
    return pl.pallas_call(
        body, grid=(t // tm,), name=name,
        in_specs=[_row_spec(tm, D), _row_spec(tm, D), _row_spec(tm, D), sh, sh, _vec_spec(D), _vec_spec(D), _WHOLE],
        out_specs=[_row_spec(tm, D), sh, sh, _row_spec(tm, D), _vec_spec(D)],
        out_shape=[b16, act, act, b16, jax.ShapeDtypeStruct((1, D), F32)],
        compiler_params=_params("arbitrary"),
    )(dh, x, f, a, b, gpre, gpost, w2g)


def _ffn_bwd_input(dh, x, da, db, gpre, w1g, w3g, name, after):
    t = x.shape[0]
    tm = _tile(t)

    def body(dh_ref, x_ref, da_ref, db_ref, gpre_ref, w1_ref, w3_ref, dx_ref, dgpre_ref):
        @pl.when(pl.program_id(0) == 0)
        def _():
            dgpre_ref[...] = jnp.zeros_like(dgpre_ref)

        dn = jnp.zeros((tm, D), F32)
        for s in range(NSH):
            dn = dn + _nn(da_ref[s], w1_ref[s]) + _nn(db_ref[s], w3_ref[s])
        dxn, dg = _rms_bwd(dn, x_ref[...], gpre_ref[...])
        dgpre_ref[...] += dg
        dx_ref[...] = dh_ref[...] + dxn

    sh = pl.BlockSpec((NSH, tm, FF_S), lambda i: (0, i, 0))
    body, specs, operands = _behind(body, after)
    return pl.pallas_call(
        body, grid=(t // tm,), name=name,
        in_specs=specs + [_row_spec(tm, D), _row_spec(tm, D), sh, sh, _vec_spec(D), _WHOLE, _WHOLE],
        out_specs=[_row_spec(tm, D), _vec_spec(D)],
        out_shape=[jax.ShapeDtypeStruct((t, D), F32), jax.ShapeDtypeStruct((1, D), F32)],
        compiler_params=_params("arbitrary"),
    )(*operands, dh, x, da, db, gpre, w1g, w3g)


def _wgrad(a, b, a_spec, b_spec, out_spec, out_shape, grid, name, after=None):
    def body(a_ref, b_ref, o_ref):
        o_ref[...] = _tn(a_ref[...], b_ref[...]).astype(BF16)

    body, specs, operands = _behind(body, after)
    return pl.pallas_call(body, grid=grid, name=name, in_specs=specs + [a_spec, b_spec], out_specs=out_spec,
                          out_shape=jax.ShapeDtypeStruct(out_shape, BF16),
                          compiler_params=_params(*("arbitrary",) * len(grid)))(*operands, a, b)


def _wgrad_cols(act, dsh, width, name, after=None):
    t = act.shape[0]
    if dsh.ndim == 3:
        b_spec = pl.BlockSpec((None, t, width), lambda s, k: (s, 0, 0))
    else:
        b_spec = pl.BlockSpec((t, width), lambda s, k: (0, s))
    return _wgrad(act, dsh, pl.BlockSpec((t, 512), lambda s, k: (0, k)), b_spec,
                  pl.BlockSpec((None, 512, width), lambda s, k: (s, k, 0)), (NSH, D, width), (NSH, 2), name, after)


def _wgrad_rows(hm, df, name, after=None):
    t = df.shape[0]
    return _wgrad(hm, df, pl.BlockSpec((None, t, FF_S), lambda s: (s, 0, 0)), pl.BlockSpec((t, D), lambda s: (0, 0)),
                  pl.BlockSpec((None, FF_S, D), lambda s: (s, 0, 0)), (NSH, FF_S, D), (NSH,), name, after)


def _wgrad_sq(a, b, name, after=None):
    t = a.shape[0]
    return _wgrad(a, b, pl.BlockSpec((t, 512), lambda i, j: (0, i)), pl.BlockSpec((t, 512), lambda i, j: (0, j)),
                  pl.BlockSpec((512, 512), lambda i, j: (i, j)), (D, D), (2, 2), name, after)


def _mix_bwd1(dh2, mo, gpost, gate, ya, yb, xg, hr, w_o, w_lru, w_att, after):
    t = dh2.shape[0]
    tm = _tile(t, TM_SCAN)

    def body(dh_ref, mo_ref, gp_ref, g_ref, ya_ref, yb_ref, xg_ref, hr_ref, wo_ref, wl_ref, wa_ref,
             dmo_ref, dya_ref, dyb_ref, dgate_ref, dhr_ref, dxg_ref, do_ref, dgp_ref, dbg_ref):
        @pl.when(pl.program_id(0) == 0)
        def _():
            dgp_ref[...] = jnp.zeros_like(dgp_ref)
            dbg_ref[...] = jnp.zeros_like(dbg_ref)

        dmo, dgp = _rms_bwd(dh_ref[...], mo_ref[...], gp_ref[...])
        dgp_ref[...] += dgp
        dmob = dmo.astype(BF16)
        dmo_ref[...] = dmob
        dm = _nt(dmob, wo_ref[...])
        g0 = g_ref[:, 0:D]
        g1 = g_ref[:, D:2 * D]
        dyab = (dm * g0).astype(BF16)
        dybb = (dm * g1).astype(BF16)
        dya_ref[...] = dyab
        dyb_ref[...] = dybb
        dg0 = dm * ya_ref[...].astype(F32)
        dg1 = dm * yb_ref[...].astype(F32)
        dgate_ref[:, 0:D] = dg0.astype(BF16)
        dgate_ref[:, D:2 * D] = dg1.astype(BF16)
        dbg_ref[:, 0:D] += jnp.sum(dg0, axis=0, keepdims=True)
        dbg_ref[:, D:2 * D] += jnp.sum(dg1, axis=0, keepdims=True)
        dyain = _nt(dyab, wl_ref[...])
        do_ref[...] = _nt(dybb, wa_ref[...]).astype(BF16)
        xgv = xg_ref[...]
        gelu, gelu_grad = _gelu_and_grad(xgv)
        dhr_ref[...] = dyain * gelu
        dxg_ref[...] = (dyain * hr_ref[...] * gelu_grad).astype(BF16)

    b16 = jax.ShapeDtypeStruct((t, D), BF16)
    body, specs, operands = _behind(body, after)
    return pl.pallas_call(
        body, grid=(t // tm,), name="mix_bwd1",
        in_specs=specs + [_row_spec(tm, D), _row_spec(tm, D), _vec_spec(D), _row_spec(tm, 2 * D), _row_spec(tm, D),
                          _row_spec(tm, D), _row_spec(tm, D), _row_spec(tm, D), _WHOLE, _WHOLE, _WHOLE],
        out_specs=[_row_spec(tm, D), _row_spec(tm, D), _row_spec(tm, D), _row_spec(tm, 2 * D), _row_spec(tm, D),
                   _row_spec(tm, D), _row_spec(tm, D), _vec_spec(D), _vec_spec(2 * D)],
        out_shape=[b16, b16, b16, jax.ShapeDtypeStruct((t, 2 * D), BF16), jax.ShapeDtypeStruct((t, D), F32), b16, b16,
                   jax.ShapeDtypeStruct((1, D), F32), jax.ShapeDtypeStruct((1, 2 * D), F32)],
        compiler_params=_params("arbitrary"),
    )(*operands, dh2, mo, gpost, gate, ya, yb, xg, hr, w_o, w_lru, w_att)


def _rglru_bwd(dhr, hr, xc, r, ig, a, s, xr, conv_w, wa2, wx2, lam, after):
    t = dhr.shape[0]
    tm = _tile(t, TM_SCAN)
    nb8 = tm // 8
    nt = t // tm

    def body(dhr_ref, hr_ref, hrp_ref, xc_ref, r_ref, ig_ref, a_sc, s_ref, xr_ref, cw_ref, wa_ref, wx_ref, lam_ref,
             dxr_ref, dwa_ref, dwx_ref, dba_ref, dbx_ref, dlam_ref, dcw_ref, dcb_ref,
             ext_h, ext_d, g_sc, c_sc, nxt_sc):
        i = pl.program_id(0)
        first_tile = i == nt - 1

        @pl.when(i == 0)
        def _():
            c_sc[...] = jnp.zeros_like(c_sc)
            nxt_sc[...] = jnp.zeros_like(nxt_sc)
            for ref in (dwa_ref, dwx_ref, dba_ref, dbx_ref, dlam_ref, dcw_ref, dcb_ref):
                ref[...] = jnp.zeros_like(ref)

        lamv = lam_ref[...]
        sp = _softplus_neg(lamv)
        rv = r_ref[...]
        igv = ig_ref[...]
        xcv = xc_ref[...]
        a = a_sc[...]
        s = s_ref[...]

        def blk(jj, c):
            st = pl.multiple_of((nb8 - 1 - jj) * 8, 8)
            d8 = dhr_ref[pl.ds(st, 8), :]
            a8 = a_sc[pl.ds(st, 8), :]
            rows = [None] * 8
            for k in range(7, -1, -1):
                g = d8[k:k + 1, :] + c
                c = a8[k:k + 1, :] * g
                rows[k] = g
            g_sc[pl.ds(st, 8), :] = jnp.concatenate(rows, axis=0)
            return c

        c_sc[0:1, :] = lax.fori_loop(0, nb8, blk, c_sc[0:1, :])
        g = g_sc[...]
        ext_h[0:8, :] = jnp.where(first_tile, 0.0, hrp_ref[...])
        ext_h[8:8 + tm, :] = hr_ref[...]
        hprev = ext_h[pl.ds(7, tm), :]
        d_s = g * (igv * xcv)
        dig = g * s * xcv
        dxc = g * s * igv
        dla = (g * hprev) * a - d_s * ((a * a) / s)
        dr_pre = (dla * (-LRU_C * sp)) * (rv * (1.0 - rv))
        di_pre = dig * (igv * (1.0 - igv))
        dlam_ref[...] += jnp.sum(dla * (LRU_C * rv), axis=0, keepdims=True) * jax.nn.sigmoid(-lamv)
        dba_ref[...] += jnp.sum(dr_pre, axis=0, keepdims=True)
        dbx_ref[...] += jnp.sum(di_pre, axis=0, keepdims=True)
        drb = dr_pre.astype(BF16)
        dib = di_pre.astype(BF16)
        xcb = xcv.astype(BF16)
        ext_d[tm:tm + 8, :] = nxt_sc[...]
        for p in range(8):
            sl = slice(p * 128, (p + 1) * 128)
            ext_d[0:tm, sl] = dxc[:, sl] + _nt(drb[:, sl], wa_ref[p]) + _nt(dib[:, sl], wx_ref[p])
            dwa_ref[p] += _tn(xcb[:, sl], drb[:, sl])
            dwx_ref[p] += _tn(xcb[:, sl], dib[:, sl])
        dxcv = ext_d[0:tm, :]
        nxt_sc[...] = ext_d[0:8, :]
        dcb_ref[...] += jnp.sum(dxcv, axis=0, keepdims=True)
        xrv = xr_ref[...]
        dxr = jnp.zeros((tm, D), F32)
        for tap in range(4):
            ext_h[0:tm, :] = ext_d[pl.ds(3 - tap, tm), :]
            ahead = ext_h[0:tm, :]
            dxr = dxr + ahead * cw_ref[tap:tap + 1, :]
            dcw_ref[tap:tap + 1, :] += jnp.sum(ahead * xrv, axis=0, keepdims=True)
        dxr_ref[...] = dxr.astype(BF16)

    rev = pl.BlockSpec((tm, D), lambda i: (nt - 1 - i, 0))
    prev = pl.BlockSpec((8, D), lambda i: (jnp.maximum((nt - 1 - i) * nb8 - 1, 0), 0))
    full = lambda shape: pl.BlockSpec(shape, lambda i: tuple(0 for _ in shape))
    vec = jax.ShapeDtypeStruct((1, D), F32)
    blocks = jax.ShapeDtypeStruct((8, 128, 128), F32)
    body, specs, operands = _behind(body, after)
    return pl.pallas_call(
        body, grid=(nt,), name="rglru_bwd",
        in_specs=specs + [rev, rev, prev, rev, rev, rev, rev, rev, rev, full((4, D)), full((8, 128, 128)),
                          full((8, 128, 128)), _vec_spec(D)],
        out_specs=[rev, full((8, 128, 128)), full((8, 128, 128)), _vec_spec(D), _vec_spec(D), _vec_spec(D), full((4, D)),
                   _vec_spec(D)],
        out_shape=[jax.ShapeDtypeStruct((t, D), BF16), blocks, blocks, vec, vec, vec, jax.ShapeDtypeStruct((4, D), F32), vec],
        scratch_shapes=[pltpu.VMEM((tm + 8, D), F32), pltpu.VMEM((tm + 8, D), F32),
                        pltpu.VMEM((tm, D), F32), pltpu.VMEM((8, D), F32), pltpu.VMEM((8, D), F32)],
        compiler_params=_params("arbitrary"),
    )(*operands, dhr, hr, hr, xc, r, ig, a, s, xr, conv_w, wa2, wx2, lam)


def _attn_bwd(sink_rows, q, kp, vp, bias_t, mask, do):
    t = q.shape[0]
    tp = kp.shape[0]
    per_step = 4

    def body(sink_ref, q_ref, kp_ref, vp_ref, bias_ref, mask_ref, do_ref, dq_ref, dk_ref, dv_ref, dbias_ref, ds_ref):
        @pl.when(pl.program_id(0) == 0)
        def _():
            for ref in (dk_ref, dv_ref, dbias_ref, ds_ref):
                ref[...] = jnp.zeros_like(ref)

        maskv = mask_ref[...]
        lane_group = lax.broadcasted_iota(jnp.int32, (1, 4 * HEAD_DIM), 1) // HEAD_DIM

        def own_blocks(full):
            out = full[0:KP]
            for g in range(1, 4):
                out = jnp.where(lane_group == g, full[g * KP:(g + 1) * KP], out)
            return out

        dsc_sum, dsinks, dks, dvs = 0.0, [0.0] * 4, [], []
        for k in range(per_step):
            c = pl.program_id(0) * per_step + k
            chunk = slice(k * CHUNK, (k + 1) * CHUNK)
            st = pl.multiple_of(c * CHUNK, CHUNK)
            kbd = _block_diag(kp_ref[pl.ds(st, KP), :], maskv)
            vbd = _block_diag(vp_ref[pl.ds(st, KP), :], maskv)
            q_all = _stack_heads(q_ref[chunk, :])
            do_all = _stack_heads(do_ref[chunk, :])
            valid = lax.broadcasted_iota(jnp.int32, (KP, 1), 0) + c * CHUNK >= PAD_KEYS
            qk = _nt(kbd, q_all)
            dp = _nt(vbd, do_all)
            ps, dscs = [], []
            for g in range(4):
                rows = slice(g * KP, (g + 1) * KP)
                p, sink_p = _group_softmax(qk[rows], bias_ref[rows, :], sink_ref[g:g + 1, :], valid)
                delta = jnp.sum(p * dp[rows], axis=0, keepdims=True)
                ps.append(p)
                dscs.append(p * (dp[rows] - delta))
                dsinks[g] = dsinks[g] - sink_p * delta
            dsc = jnp.concatenate(dscs, axis=0)
            dsc_sum = dsc_sum + dsc
            dsb = (dsc * (HEAD_DIM ** -0.5)).astype(BF16)
            dq_ref[chunk, :] = _unstack_heads(_tn(dsb, kbd)).astype(BF16)
            dks.append((st, own_blocks(_nn(dsb, q_all))))
            dvs.append((st, own_blocks(_nn(jnp.concatenate(ps, axis=0).astype(BF16), do_all))))
        dbias_ref[...] += dsc_sum
        for g in range(4):
            ds_ref[g:g + 1, :] += dsinks[g]
        for (st, dkw), (_, dvw) in zip(dks, dvs):
            dk_ref[pl.ds(st, KP), :] += dkw
            dv_ref[pl.ds(st, KP), :] += dvw

    full = lambda shape: pl.BlockSpec(shape, lambda i: tuple(0 for _ in shape))
    return pl.pallas_call(
        body, grid=(t // (per_step * CHUNK),), name="attn_bwd",
        in_specs=[_WHOLE, _row_spec(per_step * CHUNK, D), _WHOLE, _WHOLE, _WHOLE, _WHOLE, _row_spec(per_step * CHUNK, D)],
        out_specs=[_row_spec(per_step * CHUNK, D), full((tp, KV_W)), full((tp, KV_W)), full((4 * KP, 4 * CHUNK)),
                   full((8, 4 * CHUNK))],
        out_shape=[jax.ShapeDtypeStruct((t, D), BF16), jax.ShapeDtypeStruct((tp, KV_W), F32),
                   jax.ShapeDtypeStruct((tp, KV_W), F32), jax.ShapeDtypeStruct((4 * KP, 4 * CHUNK), F32),
                   jax.ShapeDtypeStruct((8, 4 * CHUNK), F32)],
        compiler_params=_params("arbitrary"),
    )(sink_rows, q, kp, vp, bias_t, mask, do)


def _mix_bwd2(dproj, dgate, h1, dh2, gmix, w_in_g, w_gate_g, after):
    t = h1.shape[0]
    tm = _tile(t)

    def body(dp_ref, dg_ref, h_ref, dh_ref, g_ref, win_ref, wg_ref, dh1_ref, dgm_ref):
        @pl.when(pl.program_id(0) == 0)
        def _():
            dgm_ref[...] = jnp.zeros_like(dgm_ref)

        du = jnp.zeros((tm, D), F32)
        for s in range(NSH):
            du = du + _nt(dp_ref[:, s * IN_S:(s + 1) * IN_S], win_ref[s])
            du = du + _nt(dg_ref[:, s * GATE_S:(s + 1) * GATE_S], wg_ref[s])
        dxn, dg = _rms_bwd(du, h_ref[...], g_ref[...])
        dgm_ref[...] += dg
        dh1_ref[...] = dh_ref[...] + dxn

    body, specs, operands = _behind(body, after)
    return pl.pallas_call(
        body, grid=(t // tm,), name="mix_bwd2",
        in_specs=specs + [_row_spec(tm, NSH * IN_S), _row_spec(tm, 2 * D), _row_spec(tm, D), _row_spec(tm, D), _vec_spec(D),
                          _WHOLE, _WHOLE],
        out_specs=[_row_spec(tm, D), _vec_spec(D)],
        out_shape=[jax.ShapeDtypeStruct((t, D), F32), jax.ShapeDtypeStruct((1, D), F32)],
        compiler_params=_params("arbitrary"),
    )(*operands, dproj, dgate, h1, dh2, gmix, w_in_g, w_gate_g)


def _band_onehot():
    nb = N_BUCKETS // 2
    max_exact = nb // 2
    rel = jnp.arange(KB)[None, :] - PAD_KEYS - jnp.arange(CHUNK)[:, None]
    ret = jnp.where(rel > 0, nb, 0)
    n = jnp.abs(rel)
    nf = jnp.maximum(n, 1).astype(jnp.float32)
    large = max_exact + (jnp.log(nf / max_exact) / math.log(128 / max_exact) * (nb - max_exact)).astype(jnp.int32)
    large = jnp.minimum(large, nb - 1)
    buckets = (ret + jnp.where(n < max_exact, n, large)).reshape(1, CHUNK * KB)
    return (buckets == jnp.arange(N_BUCKETS)[:, None]).astype(F32)


def _pair_blocks(w):
    pairs = w.reshape(8, 2, 64, 64)
    z = jnp.zeros((8, 64, 64), w.dtype)
    return jnp.concatenate([jnp.concatenate([pairs[:, 0], z], axis=2), jnp.concatenate([z, pairs[:, 1]], axis=2)], axis=1)


def _unpair_blocks(w2):
    return jnp.stack([w2[:, 0:64, 0:64], w2[:, 64:128, 64:128]], axis=1).reshape(16, 64, 64)


def _local_step(x, target, weights, sm, reducer):
    row = lambda v: v.reshape(1, -1)
    onehot_t = _band_onehot()
    bias = _bias_fwd(sm["rel_bias"].T, onehot_t).reshape(4, 4, CHUNK, KB)
    bias_t = jnp.pad(jnp.transpose(bias, (0, 3, 1, 2)), ((0, 0), (0, KP - KB), (0, 0), (0, 0))).reshape(4 * KP, 4 * CHUNK)
    sink_rows = jnp.pad(jnp.repeat(sm["attn_sinks"].reshape(4, 4), CHUNK, axis=1), ((0, 4), (0, 0)))
    grp = jnp.arange(4 * KP)[:, None] // KP == jnp.arange(4 * HEAD_DIM)[None, :] // HEAD_DIM
    mask = (grp & (jnp.arange(4 * KP)[:, None] % KP < KB)).astype(BF16)
    wa2 = _pair_blocks(sm["rg_a_w"]).astype(BF16)
    wx2 = _pair_blocks(sm["rg_x_w"]).astype(BF16)
    wg = dict(weights("ffn1", [bias_t, sink_rows, mask, wa2, wx2]))
    sm = dict(sm, conv_w=wg["conv_w"])

    h1, a1, b1, hm1, f1 = _ffn_fwd(x, row(sm["ffn1_pre_g"]), wg["ffn1_w1"], wg["ffn1_w3"], wg["ffn1_w2"],
                                   row(sm["ffn1_post_g"]), "ffn1_fwd")
    wg.update(weights("mix_in", h1))
    u, q, k, v, xr, xg, gate = _mix_proj(h1, row(sm["mix_pre_g"]), wg["w_in"], wg["w_gate"], row(sm["b_gate"]))
    token = weights("mix_out", u, begin=True)
    hr, yain, xc, r, ig, lru_a, lru_s = _rglru_fwd(xr, xg, sm["conv_w"], row(sm["conv_b"]), wa2, row(sm["rg_a_b"]), wx2,
                                                   row(sm["rg_x_b"]), row(sm["lru_lambda"]), token)
    token = weights("ffn2", hr, begin=True)
    kp = jnp.pad(k, ((PAD_KEYS, KP - KB), (0, 0)))
    vp = jnp.pad(v, ((PAD_KEYS, KP - KB), (0, 0)))
    o = _attn_fwd(sink_rows, q, kp, vp, bias_t, mask, token)
    wg.update(weights("mix_out", o))
    w_lru = wg["w_lru_out"].reshape(D, D)
    w_att = wg["w_attn_out"].reshape(D, D)
    w_o = wg["w_o"].reshape(D, D)
    wg.update(weights("ffn2", o))
    h2, mo, merged, ya, yb = _merge_fwd(yain, o, gate, h1, w_lru, w_att, w_o, row(sm["mix_post_g"]))
    dy, a2, b2, hm2, f2, sq = _ffn_fwd(h2, row(sm["ffn2_pre_g"]), wg["ffn2_w1"], wg["ffn2_w3"], wg["ffn2_w2"],
                                       row(sm["ffn2_post_g"]), "ffn2_fwd", target)

    big, small = {}, {}
    dh2, n2, da2, db2, df2, small["ffn2_pre_g"], small["ffn2_post_g"] = _ffn_bwd(
        dy, h2, f2, a2, b2, row(sm["ffn2_pre_g"]), row(sm["ffn2_post_g"]), wg["ffn2_w1"], wg["ffn2_w3"], wg["ffn2_w2"],
        "ffn2_bwd")
    big["ffn2_w1"] = _wgrad_rows(da2, n2, "dw_ffn2_w1")
    big["ffn2_w3"] = _wgrad_rows(db2, n2, "dw_ffn2_w3")
    big["ffn2_w2"] = _wgrad_rows(hm2, df2, "dw_ffn2_w2")
    token = reducer.begin("ffn2", {n: big[n] for n in ("ffn2_w1", "ffn2_w3", "ffn2_w2")})
    dmo, dya, dyb, dgate, dhr, dxg, do, small["mix_post_g"], small["b_gate"] = _mix_bwd1(
        dh2, mo, row(sm["mix_post_g"]), gate, ya, yb, xg, hr, w_o, w_lru, w_att, token)
    big["w_o"] = _wgrad_sq(merged, dmo, "dw_w_o").reshape(NSH, D // NSH, D)
    big["w_lru_out"] = _wgrad_sq(yain, dya, "dw_w_lru_out").reshape(NSH, D // NSH, D)
    big["w_attn_out"] = _wgrad_sq(o, dyb, "dw_w_attn_out").reshape(NSH, D // NSH, D)
    token = reducer.advance("ffn2", big["w_attn_out"])
    (dxr, dwa2, dwx2, small["rg_a_b"], small["rg_x_b"], small["lru_lambda"], small["conv_w"], small["conv_b"]) = _rglru_bwd(
        dhr, hr, xc, r, ig, lru_a, lru_s, xr, sm["conv_w"], wa2, wx2, row(sm["lru_lambda"]), token)
    small["rg_a_w"] = _unpair_blocks(dwa2)
    small["rg_x_w"] = _unpair_blocks(dwx2)
    dq, dkp, dvp, dbias_t, ds_rows = _attn_bwd(sink_rows, q, kp, vp, bias_t, mask, do)
    dbias = jnp.transpose(dbias_t.reshape(4, KP, 4, CHUNK)[:, :KB], (0, 2, 3, 1)).reshape(N_HEADS, CHUNK * KB)
    drel_t, dsinks = _bias_bwd(dbias, onehot_t, ds_rows)
    small["attn_sinks"] = dsinks[0:4, 0:4].reshape(N_HEADS)
    small["rel_bias"] = drel_t.T
    t = x.shape[0]
    dproj = jnp.concatenate([dq, dkp[PAD_KEYS:PAD_KEYS + t].astype(BF16), dvp[PAD_KEYS:PAD_KEYS + t].astype(BF16), dxr, dxg],
                            axis=1)
    big["w_in"] = _wgrad_cols(u, dproj, IN_S, "dw_w_in")
    big["w_gate"] = _wgrad_cols(u, dgate, GATE_S, "dw_w_gate")
    token = reducer.begin("mix", {n: big[n] for n in ("w_in", "w_gate", "w_lru_out", "w_attn_out", "w_o")})
    dh1, small["mix_pre_g"] = _mix_bwd2(dproj, dgate, h1, dh2, row(sm["mix_pre_g"]), wg["w_in"], wg["w_gate"], token)
    n1, da1, db1, df1, small["ffn1_post_g"] = _ffn_bwd_acts(
        dh1, x, f1, a1, b1, row(sm["ffn1_pre_g"]), row(sm["ffn1_post_g"]), wg["ffn1_w2"], "ffn1_bwd_acts")
    token = reducer.advance("mix", df1)
    big["ffn1_w1"] = _wgrad_rows(da1, n1, "dw_ffn1_w1", token)
    big["ffn1_w3"] = _wgrad_rows(db1, n1, "dw_ffn1_w3", token)
    big["ffn1_w2"] = _wgrad_rows(hm1, df1, "dw_ffn1_w2", token)
    token = reducer.begin("ffn1", {n: big[n] for n in ("ffn1_w1", "ffn1_w3", "ffn1_w2")})
    dx, small["ffn1_pre_g"] = _ffn_bwd_input(dh1, x, da1, db1, row(sm["ffn1_pre_g"]), wg["ffn1_w1"], wg["ffn1_w3"],
                                             "ffn1_bwd_input", token)
    return sq, dx, big, small


_ANY = pl.BlockSpec(memory_space=pl.ANY)


def _place():
    return lax.axis_index("x"), lax.axis_index("y"), lax.axis_index("c")


def _other_chips(x, y):
    return [(1 - x, y), (x, 1 - y), (1 - x, 1 - y)]


_HBM = pl.BlockSpec(memory_space=pltpu.HBM)
_SEM = pl.BlockSpec(memory_space=pltpu.SEMAPHORE)
_EFFECT = pltpu.SideEffectType.DATAFLOW_SIDE_EFFECTING


def _cast_into_slot(w, chip, name, after=None):
    r, cc = w.shape
    rows = r // 4

    def body(chip_ref, *refs):
        w_ref, o_ref = refs[-2:]
        o_ref[...] = w_ref[...].astype(BF16)

    extra = [] if after is None else [after]
    return pl.pallas_call(
        body, name=name, out_shape=jax.ShapeDtypeStruct((NSH, r, cc), BF16),
        grid_spec=pltpu.PrefetchScalarGridSpec(
            num_scalar_prefetch=1, grid=(4,), in_specs=[_ANY] * len(extra) + [pl.BlockSpec((rows, cc), lambda i, chip: (i, 0))],
            out_specs=pl.BlockSpec((None, rows, cc), lambda i, chip: (chip[0], i, 0))),
        compiler_params=_params("arbitrary"))(chip, *extra, w)


def _piece(ref, slot, c):
    if ref.dtype == F32:
        return ref.at[slot]
    rh = ref.shape[1] // 2
    return ref.at[slot, pl.ds(pl.multiple_of(c * rh, 16), rh), :]


def _gather_start(stages, name):
    flat = [b for stage in stages for b in stage]
    n, ns = len(flat), len(stages)

    def body(*refs):
        ins, sems, token = refs[:n], refs[n:n + 2 * ns], refs[-1]
        x, y, c = _place()
        me = 2 * x + y
        k = 0
        for s, stage in enumerate(stages):
            for i in range(len(stage)):
                for j, (px, py) in enumerate(_other_chips(x, y)):
                    piece = _piece(ins[k], me, c)
                    pltpu.make_async_remote_copy(src_ref=piece, dst_ref=piece, send_sem=sems[2 * s].at[3 * i + j],
                                                 recv_sem=sems[2 * s + 1].at[3 * i + j], device_id=(px, py, c),
                                                 device_id_type=MESH).start()
                k += 1
        token[...] = jnp.zeros_like(token)

    sem_shapes = [pltpu.SemaphoreType.DMA((3 * len(stage),)) for stage in stages for _ in range(2)]
    outs = pl.pallas_call(
        body, name=name, in_specs=[_HBM] * n,
        out_specs=[_SEM] * (2 * ns) + [_HBM] * n + [pl.BlockSpec(memory_space=pltpu.VMEM)],
        out_shape=sem_shapes + [pltpu.HBM(b.shape, b.dtype) for b in flat] + [jax.ShapeDtypeStruct((8, 128), F32)],
        input_output_aliases={i: 2 * ns + i for i in range(n)},
        compiler_params=pltpu.CompilerParams(has_side_effects=_EFFECT),
    )(*[pltpu.with_memory_space_constraint(b, pltpu.HBM) for b in flat])
    sems, bufs, token = outs[:2 * ns], list(outs[2 * ns:2 * ns + n]), outs[-1]
    per_stage, k = [], 0
    for s, stage in enumerate(stages):
        per_stage.append((sems[2 * s], sems[2 * s + 1], bufs[k:k + len(stage)]))
        k += len(stage)
    return per_stage, token


def _gather_wait(send_sems, recv_sems, bufs, after, name):
    n = len(bufs)

    def body(*refs):
        ins, ssem, rsem = refs[:n], refs[n], refs[n + 1]
        x, y, c = _place()
        me = 2 * x + y
        for i in range(n):
            for j, (px, py) in enumerate(_other_chips(x, y)):
                cp = pltpu.make_async_remote_copy(src_ref=_piece(ins[i], me, c), dst_ref=_piece(ins[i], 2 * px + py, c),
                                                  send_sem=ssem.at[3 * i + j], recv_sem=rsem.at[3 * i + j],
                                                  device_id=(px, py, c), device_id_type=MESH)
                cp.wait_send()
                cp.wait_recv()

    afters = list(after) if isinstance(after, (list, tuple)) else [after]
    return pl.pallas_call(
        body, name=name, in_specs=[_HBM] * n + [_SEM, _SEM] + [_ANY] * len(afters), out_specs=[_HBM] * n,
        out_shape=[pltpu.HBM(b.shape, b.dtype) for b in bufs], input_output_aliases={i: i for i in range(n)},
        compiler_params=pltpu.CompilerParams(has_side_effects=_EFFECT),
    )(*bufs, send_sems, recv_sems, *afters)


def _sibling_fill(bufs, name):
    n = len(bufs)

    def body(*refs):
        ins, outs = refs[:n], refs[n:2 * n]
        send_sems, recv_sems = refs[2 * n:]
        x, y, c = _place()
        copies = []
        for i in range(n):
            for j, (px, py) in enumerate(_other_chips(x, y)):
                copies.append(pltpu.make_async_remote_copy(
                    src_ref=_piece(ins[i], 2 * px + py, c), dst_ref=_piece(outs[i], 2 * px + py, c),
                    send_sem=send_sems.at[3 * i + j], recv_sem=recv_sems.at[3 * i + j], device_id=(x, y, 1 - c),
                    device_id_type=MESH))
                copies[-1].start()
        for cp in copies:
            cp.wait()

    return pl.pallas_call(
        body, name=name, in_specs=[_ANY] * n, out_specs=[_ANY] * n,
        out_shape=[jax.ShapeDtypeStruct(b.shape, b.dtype) for b in bufs], input_output_aliases={i: i for i in range(n)},
        scratch_shapes=[pltpu.SemaphoreType.DMA((3 * n,)), pltpu.SemaphoreType.DMA((3 * n,))],
        compiler_params=pltpu.CompilerParams(has_side_effects=True),
    )(*bufs)


def _swap_plan(srcs, lands):
    x, y, c = _place()
    plan = []
    for src, land in zip(srcs, lands):
        rh = src.shape[1] // 2
        plan.append((src.at[:, pl.ds(pl.multiple_of((1 - c) * rh, 16), rh), :], land, (x, y, 1 - c)))
    return plan


def _owners_plan(srcs, lands):
    x, y, c = _place()
    return [(src.at[2 * px + py], land.at[j], (px, py, c))
            for src, land in zip(srcs, lands) for j, (px, py) in enumerate(_other_chips(x, y))]


def _exchange_start(srcs, lands, plan, copies, name):
    n, m = len(srcs), len(srcs) + len(lands)

    def body(*refs):
        send_sems, recv_sems, token = refs[m], refs[m + 1], refs[-1]
        for k, (src, dst, dev) in enumerate(plan(refs[:n], refs[n:m])):
            pltpu.make_async_remote_copy(src_ref=src, dst_ref=dst, send_sem=send_sems.at[k], recv_sem=recv_sems.at[k],
                                         device_id=dev, device_id_type=MESH).start()
        token[...] = jnp.zeros_like(token)

    both = list(srcs) + list(lands)
    outs = pl.pallas_call(
        body, name=name, in_specs=[_HBM] * m,
        out_specs=[_SEM, _SEM] + [_HBM] * m + [pl.BlockSpec(memory_space=pltpu.VMEM)],
        out_shape=[pltpu.SemaphoreType.DMA((copies,)), pltpu.SemaphoreType.DMA((copies,))]
        + [pltpu.HBM(b.shape, b.dtype) for b in both] + [jax.ShapeDtypeStruct((8, 128), F32)],
        input_output_aliases={i: 2 + i for i in range(m)},
        compiler_params=pltpu.CompilerParams(has_side_effects=_EFFECT),
    )(*[pltpu.with_memory_space_constraint(b, pltpu.HBM) for b in both])
    return (outs[0], outs[1]), list(outs[2:2 + n]), list(outs[2 + n:2 + m]), outs[-1]


def _exchange_wait(sems, srcs, lands, plan, after, name):
    n, m = len(srcs), len(srcs) + len(lands)

    def body(*refs):
        send_sems, recv_sems = refs[m], refs[m + 1]
        for k, (src, dst, dev) in enumerate(plan(refs[:n], refs[n:m])):
            cp = pltpu.make_async_remote_copy(src_ref=src, dst_ref=dst, send_sem=send_sems.at[k], recv_sem=recv_sems.at[k],
                                              device_id=dev, device_id_type=MESH)
            cp.wait_send()
            cp.wait_recv()

    both = list(srcs) + list(lands)
    afters = list(after) if isinstance(after, (list, tuple)) else [after]
    outs = pl.pallas_call(
        body, name=name, in_specs=[_HBM] * m + [_SEM, _SEM] + [_ANY] * len(afters), out_specs=[_HBM] * m,
        out_shape=[pltpu.HBM(b.shape, b.dtype) for b in both], input_output_aliases={i: i for i in range(m)},
        compiler_params=pltpu.CompilerParams(has_side_effects=_EFFECT),
    )(*both, sems[0], sems[1], *afters)
    return list(outs[:n]), list(outs[n:])


def _fill_plan(bufs, _):
    x, y, c = _place()
    return [(_piece(buf, 2 * px + py, c), _piece(buf, 2 * px + py, c), (x, y, 1 - c))
            for buf in bufs for px, py in _other_chips(x, y)]


class _Reducer:
    def __init__(self, where):
        self.state = {}
        self.where = where

    def begin(self, stage, grads):
        names = list(grads)
        full = [grads[n] for n in names]
        lands = [lax.empty((NSH, g.shape[1] // 2, g.shape[2]), g.dtype) for g in full]
        sems, full, lands, token = _exchange_start(full, lands, _swap_plan, len(full), "swap_start_" + stage)
        self.state[stage] = (names, sems, full, lands)
        return token

    def advance(self, stage, after):
        names, sems, full, lands = self.state[stage]
        full, got = _exchange_wait(sems, full, lands, _swap_plan, after, "swap_wait_" + stage)
        sums, own = _chip_sums(full, got, self.where, "chip_sums_" + stage)
        lands = [lax.empty((3,) + s.shape[1:], BF16) for s in sums]
        sems, sent, lands, token = _exchange_start(sums, lands, _owners_plan, 3 * len(sums), "owners_start_" + stage)
        self.state[stage] = (names, own, sems, sent, lands)
        return token

    def finish(self, stage, after):
        names, own, sems, sent, lands = self.state[stage]
        _, got = _exchange_wait(sems, sent, lands, _owners_plan, after, "owners_wait_" + stage)
        return dict(zip(names, _owner_sums(own, got, "owner_sums_" + stage)))


def _chip_sums(gs, gots, where, name):
    n = len(gs)

    def body(where_ref, *refs):
        g_refs, got_refs, hb_refs, own_refs = (refs[k * n:(k + 1) * n] for k in range(4))
        mine = pl.program_id(0) == where_ref[1]
        for g_ref, got_ref, hb_ref, own_ref in zip(g_refs, got_refs, hb_refs, own_refs):
            h = g_ref[...].astype(F32) + got_ref[...].astype(F32)
            hb_ref[...] = h.astype(BF16)

            @pl.when(mine)
            def _():
                own_ref[...] = h

    halves = [(g.shape[1] // 2, g.shape[2]) for g in gs]
    slot = [pl.BlockSpec((None, rh, cc), lambda s, where: (s, 0, 0)) for rh, cc in halves]
    outs = pl.pallas_call(
        body, name=name,
        grid_spec=pltpu.PrefetchScalarGridSpec(
            num_scalar_prefetch=1, grid=(NSH,),
            in_specs=[pl.BlockSpec((None, rh, cc), lambda s, where: (s, where[0], 0)) for rh, cc in halves] + slot,
            out_specs=slot + [pl.BlockSpec((rh, cc), lambda s, where: (0, 0)) for rh, cc in halves]),
        out_shape=[jax.ShapeDtypeStruct((NSH, rh, cc), BF16) for rh, cc in halves]
        + [jax.ShapeDtypeStruct((rh, cc), F32) for rh, cc in halves],
        compiler_params=_params("arbitrary"),
    )(where, *gs, *gots)
    return list(outs[:n]), list(outs[n:])


def _owner_sums(owns, gots, name):
    n = len(owns)

    def body(*refs):
        own_refs, got_refs, o_refs = (refs[k * n:(k + 1) * n] for k in range(3))
        for own_ref, got_ref, o_ref in zip(own_refs, got_refs, o_refs):
            o_ref[...] = ((own_ref[...] + got_ref[0].astype(F32)) + got_ref[1].astype(F32)) + got_ref[2].astype(F32)

    blocks = [(o.shape[0] // 2, o.shape[1]) for o in owns]
    rows = [pl.BlockSpec(b, lambda i: (i, 0)) for b in blocks]
    return pl.pallas_call(
        body, grid=(2,), name=name,
        in_specs=rows + [pl.BlockSpec((3,) + b, lambda i: (0, i, 0)) for b in blocks], out_specs=rows,
        out_shape=[jax.ShapeDtypeStruct(o.shape, F32) for o in owns], compiler_params=_params("arbitrary"),
    )(*owns, *gots)


def _sibling_plan(srcs, lands):
    x, y, c = _place()
    return [(src, land, (x, y, 1 - c)) for src, land in zip(srcs, lands)]


def _all_reduce_small(part):
    def body(p_ref, o_ref, rbuf, send1, recv1, send2, recv2):
        x, y, c = _place()
        me = 4 * x + 2 * y + c
        peers = []
        for k in range(1, 8):
            px, py, pc = x ^ ((k >> 2) & 1), y ^ ((k >> 1) & 1), c ^ (k & 1)
            peers.append((k, (px, py, pc), 4 * px + 2 * py + pc))

        def rows(d):
            return pl.ds(pl.multiple_of(d * SMALL_SLICE, 8), SMALL_SLICE)

        first = [pltpu.make_async_remote_copy(src_ref=p_ref.at[rows(idx), :], dst_ref=rbuf.at[me], send_sem=send1.at[k],
                                              recv_sem=recv1.at[k], device_id=dev, device_id_type=MESH)
                 for k, dev, idx in peers]
        for cp in first:
            cp.start()
        rbuf[me] = p_ref[rows(me), :]
        for k, dev, idx in peers:
            pltpu.make_async_remote_copy(src_ref=p_ref.at[rows(idx), :], dst_ref=rbuf.at[idx], send_sem=send1.at[k],
                                         recv_sem=recv1.at[k], device_id=dev, device_id_type=MESH).wait_recv()
        acc = rbuf[0]
        for d in range(1, 8):
            acc = acc + rbuf[d]
        o_ref[rows(me), :] = acc
        second = [pltpu.make_async_remote_copy(src_ref=o_ref.at[rows(me), :], dst_ref=o_ref.at[rows(me), :],
                                               send_sem=send2.at[k], recv_sem=recv2.at[k], device_id=dev, device_id_type=MESH)
                  for k, dev, idx in peers]
        for cp in second:
            cp.start()
        for k, dev, idx in peers:
            pltpu.make_async_remote_copy(src_ref=o_ref.at[rows(me), :], dst_ref=o_ref.at[rows(idx), :], send_sem=send2.at[k],
                                         recv_sem=recv2.at[k], device_id=dev, device_id_type=MESH).wait_recv()
        for cp in first + second:
            cp.wait_send()

    return pl.pallas_call(
        body, name="all_reduce_small", in_specs=[_WHOLE], out_specs=_WHOLE,
        out_shape=jax.ShapeDtypeStruct((SMALL_ROWS, 128), F32),
        scratch_shapes=[pltpu.VMEM((8, SMALL_SLICE, 128), F32)] + [pltpu.SemaphoreType.DMA((8,))] * 4,
        compiler_params=pltpu.CompilerParams(has_side_effects=True),
    )(part)


def _adamw_update(w, gv, m, v):
    nm = ADAM_B1 * m + (1.0 - ADAM_B1) * gv
    nv = ADAM_B2 * v + (1.0 - ADAM_B2) * (gv * gv)
    m_hat = nm / (1.0 - ADAM_B1 ** ADAM_STEP)
    v_hat = nv / (1.0 - ADAM_B2 ** ADAM_STEP)
    return -ADAM_LR * (m_hat / (jnp.sqrt(v_hat) + ADAM_EPS) + ADAM_WD * w), nm, nv


def _adamw_small(ws, gs, ms, vs):
    n = len(ws)

    def body(*refs):
        w_refs, g_refs, m_refs, v_refs, d_refs, nm_refs, nv_refs = (refs[k * n:(k + 1) * n] for k in range(7))
        for i in range(n):
            d_refs[i][...], nm_refs[i][...], nv_refs[i][...] = _adamw_update(
                w_refs[i][...], g_refs[i][...], m_refs[i][...], v_refs[i][...])

    out = [jax.ShapeDtypeStruct(w.shape, F32) for w in ws]
    outs = pl.pallas_call(body, in_specs=[_WHOLE] * (4 * n), out_specs=[_WHOLE] * (3 * n), out_shape=out * 3,
                          name="adamw_small", compiler_params=_params())(*ws, *gs, *ms, *vs)
    return outs[:n], outs[n:2 * n], outs[2 * n:]


def _adamw_halves(ws, mines, theirs, ms, vs, name):
    n = len(ws)
    steps = 2

    def body(*refs):
        w_refs, mine_refs, theirs_refs, m_refs, v_refs, g_refs, d_refs, nm_refs, nv_refs = (
            refs[k * n:(k + 1) * n] for k in range(9))
        is_mine = pl.program_id(0) == lax.axis_index("c")
        for i in range(n):
            gv = jnp.where(is_mine, mine_refs[i][...], theirs_refs[i][...])
            g_refs[i][...] = gv
            d_refs[i][...], nm_refs[i][...], nv_refs[i][...] = _adamw_update(w_refs[i][...], gv, m_refs[i][...], v_refs[i][...])

    blocks = [(h.shape[0] // steps, h.shape[1]) for h in mines]
    whole = [pl.BlockSpec(b, lambda h, i: (steps * h + i, 0)) for b in blocks]
    half = [pl.BlockSpec(b, lambda h, i: (i, 0)) for b in blocks]
    out = [jax.ShapeDtypeStruct(w.shape, F32) for w in ws]
    outs = pl.pallas_call(body, grid=(2, steps), in_specs=whole + half + half + whole + whole, out_specs=whole * 4,
                          out_shape=out * 4, name=name, compiler_params=_params("arbitrary", "arbitrary"),
                          )(*ws, *mines, *theirs, *ms, *vs)
    return [tuple(outs[k * n + i] for k in range(4)) for i in range(n)]


SMALL_USED = sum(size for _, size in SMALL) // 128


def _pack_small(vals, tail=None):
    parts = []
    for name, size in SMALL:
        flat = vals[name].reshape(-1).astype(F32)
        parts.append(jnp.pad(flat, (0, size - flat.shape[0])))
    if tail is not None:
        parts.append(tail.reshape(128))
    flat = jnp.concatenate(parts)
    return jnp.pad(flat, (0, SMALL_ROWS * 128 - flat.shape[0])).reshape(SMALL_ROWS, 128)


def _unpack_small(packed, shapes):
    flat = packed.reshape(-1)
    out, off = {}, 0
    for name, size in SMALL:
        n = math.prod(shapes[name])
        out[name] = flat[off:off + n].reshape(shapes[name])
        off += size
    return out


def kernel(x, ffn1_pre_g, ffn1_w1, ffn1_w3, ffn1_w2, ffn1_post_g, mix_pre_g, w_in, conv_w, conv_b, rg_a_w, rg_a_b, rg_x_w, rg_x_b, lru_lambda, w_lru_out, attn_sinks, rel_bias, w_attn_out, w_gate, b_gate, w_o, mix_post_g, ffn2_pre_g, ffn2_w1, ffn2_w3, ffn2_w2, ffn2_post_g, loss_target, m_ffn1_pre_g, m_ffn1_w1, m_ffn1_w3, m_ffn1_w2, m_ffn1_post_g, m_mix_pre_g, m_w_in, m_conv_w, m_conv_b, m_rg_a_w, m_rg_a_b, m_rg_x_w, m_rg_x_b, m_lru_lambda, m_w_lru_out, m_attn_sinks, m_rel_bias, m_w_attn_out, m_w_gate, m_b_gate, m_w_o, m_mix_post_g, m_ffn2_pre_g, m_ffn2_w1, m_ffn2_w3, m_ffn2_w2, m_ffn2_post_g, v_ffn1_pre_g, v_ffn1_w1, v_ffn1_w3, v_ffn1_w2, v_ffn1_post_g, v_mix_pre_g, v_w_in, v_conv_w, v_conv_b, v_rg_a_w, v_rg_a_b, v_rg_x_w, v_rg_x_b, v_lru_lambda, v_w_lru_out, v_attn_sinks, v_rel_bias, v_w_attn_out, v_w_gate, v_b_gate, v_w_o, v_mix_post_g, v_ffn2_pre_g, v_ffn2_w1, v_ffn2_w3, v_ffn2_w2, v_ffn2_post_g):
    given = dict(locals())
    chip = 2 * lax.axis_index("x") + lax.axis_index("y")
    transposed = ("ffn1_w1", "ffn1_w3", "ffn2_w1", "ffn2_w3")

    def shard(name, moment=""):
        w = given[moment + name][0]
        return w.T if name in transposed else w

    def unshard(name, w):
        return (w.T if name in transposed else w)[None]

    def only_my_columns(a):
        parts = a.reshape(1, 4, NSH, D // NSH)
        return sum(jnp.where(chip == s, parts[:, :, s], 0.0) for s in range(NSH))

    chip_arr = jnp.reshape(chip, (1,)).astype(jnp.int32)
    stage_names = {"ffn1": ["ffn1_w1", "ffn1_w3", "ffn1_w2", "conv_w"],
                   "mix_in": ["w_in", "w_gate"],
                   "mix_out": ["w_lru_out", "w_attn_out", "w_o"],
                   "ffn2": ["ffn2_w1", "ffn2_w3", "ffn2_w2"]}
    in_flight, started = {}, None
    for stage, names in stage_names.items():
        bufs = [jnp.where(lax.broadcasted_iota(jnp.int32, (NSH, 4, D // NSH), 0) == chip, given[n], 0.0) if n == "conv_w"
                else _cast_into_slot(shard(n), chip_arr, "cast_" + n, started) for n in names]
        (in_flight[stage],), started = _gather_start([bufs], "gather_start_" + stage)
    all_started = started

    filling = {}

    def weights(stage, after, begin=False):
        names = stage_names[stage]
        halves_of = [n for n in names if n != "conv_w"]
        if stage in filling:
            filled, _ = _exchange_wait(filling.pop(stage), *filling.pop(stage + "/bufs"), _fill_plan, after,
                                       "fill_wait_" + stage)
            return dict(zip(halves_of, filled))
        send_sems, recv_sems, landing = in_flight[stage]
        if stage == "ffn1":
            after = [all_started] + list(after)
        landed = dict(zip(names, _gather_wait(send_sems, recv_sems, landing, after, "gather_wait_" + stage)))
        halves = [landed[n] for n in halves_of]
        if begin:
            filling[stage], bufs, _, token = _exchange_start(halves, [], _fill_plan, 3 * len(halves), "fill_start_" + stage)
            filling[stage + "/bufs"] = (bufs, [])
            return token
        out = dict(zip(halves_of, _sibling_fill(halves, "sibling_fill_" + stage)))
        if "conv_w" in names:
            out["conv_w"] = jnp.transpose(landed["conv_w"], (1, 0, 2)).reshape(4, D)
        return out

    small_shapes = {n: given[n].shape for n, _ in SMALL}
    small_shapes["conv_w"] = (1, 4, D)
    sm = {n: (given[n][0] if given[n].shape[0] == 1 and n != "rel_bias" else given[n]) for n, _ in SMALL if n != "conv_w"}

    reducer = _Reducer(jnp.stack([lax.axis_index("c"), chip]).astype(jnp.int32))
    sq, dx, _, small = _local_step(x[0], loss_target[0], weights, sm, reducer)

    reduced_small = _all_reduce_small(_pack_small(small, tail=sq))
    last_started = reducer.advance("ffn1", [dx, reduced_small])
    loss = reduced_small[SMALL_USED, 0] * (0.5 / D)
    small_g = _unpack_small(reduced_small, small_shapes)
    grads, delta, new_m, new_v = {}, {}, {}, {}
    in_transit = {}

    def send(stage, after):
        halves = reducer.finish(stage, after)
        lands = [lax.empty(h.shape, F32) for h in halves.values()]
        sems, mine, lands, token = _exchange_start(list(halves.values()), lands, _sibling_plan, len(lands),
                                                   "halves_start_" + stage)
        in_transit[stage] = (list(halves), sems, mine, lands)
        return token

    def update(stage, after):
        names, sems, mine, lands = in_transit[stage]
        mine, theirs = _exchange_wait(sems, mine, lands, _sibling_plan, after, "halves_wait_" + stage)
        updated = _adamw_halves([shard(n) for n in names], mine, theirs, [shard(n, "m_") for n in names],
                                [shard(n, "v_") for n in names], "adamw_" + stage)
        for n, results in zip(names, updated):
            grads[n], delta[n], new_m[n], new_v[n] = (unshard(n, r) for r in results)
        return new_v[names[-1]]

    token = send("ffn2", [reduced_small, last_started])
    token = send("mix", token)
    done = update("ffn2", token)
    done = update("mix", done)
    token = send("ffn1", done)
    update("ffn1", token)

    small_g["conv_w"] = only_my_columns(small_g["conv_w"])
    names = [n for n, _ in SMALL]
    flat2d = lambda a: a.reshape(-1, a.shape[-1])
    outs = _adamw_small(*[[flat2d(given[pre + n]) if pre != "g" else flat2d(small_g[n]) for n in names]
                          for pre in ("", "g", "m_", "v_")])
    for dst, arrs in zip((delta, new_m, new_v), outs):
        dst.update({n: a.reshape(given[n].shape) for n, a in zip(names, arrs)})
    grads.update(small_g)
    return (loss, dx[None], *[grads[n] for n in WEIGHTS], *[delta[n] for n in WEIGHTS], *[new_m[n] for n in WEIGHTS],
            *[new_v[n] for n in WEIGHTS])
```

```python
import functools
import math

import jax
import jax.numpy as jnp
from jax import lax
from jax.experimental import pallas as pl
from jax.experimental.pallas import tpu as pltpu

F32, BF16 = jnp.float32, jnp.bfloat16
D = 1024
NSH = 4
FF_S = 704
IN_S = 896
GATE_S = 512
KV_W = 256
CHUNK = 64
KB = 192
N_HEADS = 16
HEAD_DIM = 64
N_BUCKETS = 32
KP = 192
PAD_KEYS = 128
RMS_EPS = 1e-6
NEG_INF = -1e30
LRU_C = 8.0
TM = 512
TM_SCAN = 256
VMEM_LIMIT = 56 * 1024 * 1024
ADAM_LR, ADAM_B1, ADAM_B2, ADAM_EPS, ADAM_WD, ADAM_STEP = 0.001, 0.9, 0.999, 1e-08, 0.01, 10
SMALL_ROWS = 1216
SMALL_SLICE = SMALL_ROWS // 8
MESH = pl.DeviceIdType.MESH

BIG = ["ffn1_w1", "ffn1_w3", "ffn1_w2", "w_in", "w_lru_out", "w_attn_out", "w_gate", "w_o", "ffn2_w1", "ffn2_w3", "ffn2_w2"]
SMALL = [("ffn1_pre_g", 1024), ("ffn1_post_g", 1024), ("mix_pre_g", 1024), ("conv_w", 4096), ("conv_b", 1024),
         ("rg_a_w", 65536), ("rg_a_b", 1024), ("rg_x_w", 65536), ("rg_x_b", 1024), ("lru_lambda", 1024),
         ("attn_sinks", 1024), ("rel_bias", 1024), ("b_gate", 2048), ("mix_post_g", 1024), ("ffn2_pre_g", 1024),
         ("ffn2_post_g", 1024)]
WEIGHTS = ["ffn1_pre_g", "ffn1_w1", "ffn1_w3", "ffn1_w2", "ffn1_post_g", "mix_pre_g", "w_in", "conv_w", "conv_b", "rg_a_w",
           "rg_a_b", "rg_x_w", "rg_x_b", "lru_lambda", "w_lru_out", "attn_sinks", "rel_bias", "w_attn_out", "w_gate", "b_gate",
           "w_o", "mix_post_g", "ffn2_pre_g", "ffn2_w1", "ffn2_w3", "ffn2_w2", "ffn2_post_g"]


def _params(*sem):
    return pltpu.CompilerParams(dimension_semantics=sem or None, vmem_limit_bytes=VMEM_LIMIT)


def _nn(a, b):
    return jnp.dot(a, b, preferred_element_type=F32)


def _nt(a, b):
    return lax.dot_general(a, b, (((1,), (1,)), ((), ())), preferred_element_type=F32)


def _tn(a, b):
    return lax.dot_general(a, b, (((0,), (0,)), ((), ())), preferred_element_type=F32)


def _rms(x, g):
    rstd = lax.rsqrt(jnp.mean(x * x, axis=-1, keepdims=True) + RMS_EPS)
    return (x * rstd) * g


def _rms_bwd(dout, x, g):
    rstd = lax.rsqrt(jnp.mean(x * x, axis=-1, keepdims=True) + RMS_EPS)
    xhat = x * rstd
    dg = jnp.sum(dout * xhat, axis=0, keepdims=True)
    dxhat = dout * g
    dx = rstd * (dxhat - xhat * jnp.mean(dxhat * xhat, axis=-1, keepdims=True))
    return dx, dg


_GELU_K = math.sqrt(2.0 / math.pi)


_GELU_C = 0.044715 * _GELU_K


def _gelu(x):
    return x * (0.5 + 0.5 * jnp.tanh(x * (_GELU_K + _GELU_C * (x * x))))


def _gelu_and_grad(x):
    x2 = x * x
    t = jnp.tanh(x * (_GELU_K + _GELU_C * x2))
    cdf = 0.5 + 0.5 * t
    return x * cdf, cdf + (x * (_GELU_K + (3.0 * _GELU_C) * x2)) * (0.5 - 0.5 * (t * t))


def _softplus_neg(lam):
    z = -lam
    u = jnp.exp(-jnp.abs(z))
    w = 1.0 + u
    log1p_u = jnp.where(w == 1.0, u, jnp.log(w) * (u / (w - 1.0)))
    return jnp.maximum(z, 0.0) + log1p_u


def _lru_coeffs(r, sp):
    log_a = (-LRU_C * r) * sp
    a = jnp.exp(log_a)
    t = jnp.tanh(log_a)
    s = jnp.sqrt(-2.0 * t / (1.0 - t))
    return a, s


def _row_spec(tm, width):
    return pl.BlockSpec((tm, width), lambda i: (i, 0))


def _vec_spec(width):
    return pl.BlockSpec((1, width), lambda i: (0, 0))


_WHOLE = pl.BlockSpec(memory_space=pltpu.VMEM)


def _tile(t, tm=TM):
    return min(tm, t)


def _ffn_fwd(x, gpre, w1g, w3g, w2g, gpost, name, target=None):
    t = x.shape[0]
    tm = _tile(t)
    last = target is not None

    def body(x_ref, gpre_ref, w1_ref, w3_ref, w2_ref, gpost_ref, *refs):
        t_ref, (h_ref, a_ref, b_ref, hm_ref, f_ref), l_ref = (refs[0] if last else None), refs[last:last + 5], refs[-1]
        xv = x_ref[...]
        nb = _rms(xv, gpre_ref[...]).astype(BF16)
        f = jnp.zeros((tm, D), F32)
        for s in range(NSH):
            a = _nt(nb, w1_ref[s])
            b = _nt(nb, w3_ref[s])
            hmb = ((a * jax.nn.sigmoid(a)) * b).astype(BF16)
            a_ref[s] = a.astype(BF16)
            b_ref[s] = b.astype(BF16)
            hm_ref[s] = hmb
            f = f + _nn(hmb, w2_ref[s])
        f_ref[...] = f
        h = xv + 0.5 * _rms(f, gpost_ref[...])
        if last:
            @pl.when(pl.program_id(0) == 0)
            def _():
                l_ref[...] = jnp.zeros_like(l_ref)

            e = h - t_ref[...]
            h_ref[...] = e * (1.0 / D)
            l_ref[...] += jnp.sum(jnp.sum(e * e, axis=0, keepdims=True), axis=1, keepdims=True)
        else:
            h_ref[...] = h

    sh = pl.BlockSpec((NSH, tm, FF_S), lambda i: (0, i, 0))
    act = jax.ShapeDtypeStruct((NSH, t, FF_S), BF16)
    return pl.pallas_call(
        body, grid=(t // tm,), name=name,
        in_specs=[_row_spec(tm, D), _vec_spec(D), _WHOLE, _WHOLE, _WHOLE, _vec_spec(D)] + [_row_spec(tm, D)] * last,
        out_specs=[_row_spec(tm, D), sh, sh, sh, _row_spec(tm, D)] + [pl.BlockSpec((1, 128), lambda i: (0, 0))] * last,
        out_shape=[jax.ShapeDtypeStruct((t, D), F32), act, act, act, jax.ShapeDtypeStruct((t, D), F32)]
        + [jax.ShapeDtypeStruct((1, 128), F32)] * last,
        compiler_params=_params("arbitrary"),
    )(x, gpre, w1g, w3g, w2g, gpost, *([target] if last else []))


def _ffn_up(x, gpre, w1g, w3g, name):
    t = x.shape[0]
    tm = _tile(t)

    def body(x_ref, gpre_ref, w1_ref, w3_ref, a_ref, b_ref, hm_ref):
        nb = _rms(x_ref[...], gpre_ref[...]).astype(BF16)
        for s in range(NSH):
            a = _nt(nb, w1_ref[s])
            b = _nt(nb, w3_ref[s])
            a_ref[s] = a.astype(BF16)
            b_ref[s] = b.astype(BF16)
            hm_ref[s] = ((a * jax.nn.sigmoid(a)) * b).astype(BF16)

    sh = pl.BlockSpec((NSH, tm, FF_S), lambda i: (0, i, 0))
    act = jax.ShapeDtypeStruct((NSH, t, FF_S), BF16)
    return pl.pallas_call(
        body, grid=(t // tm,), name=name, in_specs=[_row_spec(tm, D), _vec_spec(D), _WHOLE, _WHOLE],
        out_specs=[sh, sh, sh], out_shape=[act, act, act], compiler_params=_params("arbitrary"),
    )(x, gpre, w1g, w3g)


def _ffn_down(x, hm, w2g, gpost, name):
    t = x.shape[0]
    tm = _tile(t)

    def body(x_ref, hm_ref, w2_ref, gpost_ref, h_ref, f_ref):
        f = jnp.zeros((tm, D), F32)
        for s in range(NSH):
            f = f + _nn(hm_ref[s], w2_ref[s])
        f_ref[...] = f
        h_ref[...] = x_ref[...] + 0.5 * _rms(f, gpost_ref[...])

    sh = pl.BlockSpec((NSH, tm, FF_S), lambda i: (0, i, 0))
    f32 = jax.ShapeDtypeStruct((t, D), F32)
    return pl.pallas_call(
        body, grid=(t // tm,), name=name, in_specs=[_row_spec(tm, D), sh, _WHOLE, _vec_spec(D)],
        out_specs=[_row_spec(tm, D), _row_spec(tm, D)], out_shape=[f32, f32], compiler_params=_params("arbitrary"),
    )(x, hm, w2g, gpost)


def _mix_proj(h1, gmix, w_in_g, w_gate_g, b_gate):
    t = h1.shape[0]
    tm = _tile(t)

    def body(h_ref, g_ref, win_ref, wg_ref, bg_ref, u_ref, q_ref, k_ref, v_ref, xr_ref, xg_ref, gate_ref):
        ub = _rms(h_ref[...], g_ref[...]).astype(BF16)
        u_ref[...] = ub
        p0 = _nn(ub, win_ref[0])
        q_ref[:, 0:896] = p0.astype(BF16)
        p1 = _nn(ub, win_ref[1])
        q_ref[:, 896:1024] = p1[:, 0:128].astype(BF16)
        k_ref[...] = p1[:, 128:384].astype(BF16)
        v_ref[...] = p1[:, 384:640].astype(BF16)
        xr_ref[:, 0:256] = p1[:, 640:896]
        p2 = _nn(ub, win_ref[2])
        xr_ref[:, 256:1024] = p2[:, 0:768]
        xg_ref[:, 0:128] = p2[:, 768:896]
        xg_ref[:, 128:1024] = _nn(ub, win_ref[3])
        for s in range(NSH):
            sl = slice(s * GATE_S, (s + 1) * GATE_S)
            gate_ref[:, sl] = jax.nn.sigmoid(_nn(ub, wg_ref[s]) + bg_ref[:, sl])

    return pl.pallas_call(
        body, grid=(t // tm,), name="mix_proj",
        in_specs=[_row_spec(tm, D), _vec_spec(D), _WHOLE, _WHOLE, _vec_spec(2 * D)],
        out_specs=[_row_spec(tm, D), _row_spec(tm, D), _row_spec(tm, KV_W), _row_spec(tm, KV_W), _row_spec(tm, D),
                   _row_spec(tm, D), _row_spec(tm, 2 * D)],
        out_shape=[jax.ShapeDtypeStruct((t, D), BF16), jax.ShapeDtypeStruct((t, D), BF16),
                   jax.ShapeDtypeStruct((t, KV_W), BF16), jax.ShapeDtypeStruct((t, KV_W), BF16),
                   jax.ShapeDtypeStruct((t, D), F32), jax.ShapeDtypeStruct((t, D), F32),
                   jax.ShapeDtypeStruct((t, 2 * D), F32)],
        compiler_params=_params("arbitrary"),
    )(h1, gmix, w_in_g, w_gate_g, b_gate)


def _rglru_fwd(xr, xg, conv_w, conv_b, wa2, ba, wx2, bx, lam, after=None):
    t = xr.shape[0]
    tm = _tile(t, TM_SCAN)
    nb8 = tm // 8

    def body(xr_ref, xrp_ref, xg_ref, cw_ref, cb_ref, wa_ref, ba_ref, wx_ref, bx_ref, lam_ref,
             hr_ref, yain_ref, xc_ref, r_ref, ig_ref, a_sc, s_ref, ext, h_sc):
        i = pl.program_id(0)

        @pl.when(i == 0)
        def _():
            h_sc[...] = jnp.zeros_like(h_sc)

        ext[0:8, :] = jnp.where(i == 0, 0.0, xrp_ref[...])
        ext[8:8 + tm, :] = xr_ref[...]
        xc = jnp.broadcast_to(cb_ref[...], (tm, D))
        for tap in range(4):
            xc = xc + ext[pl.ds(5 + tap, tm), :] * cw_ref[tap:tap + 1, :]
        xc_ref[...] = xc
        xcb = xc.astype(BF16)
        for p in range(8):
            sl = slice(p * 128, (p + 1) * 128)
            r_ref[:, sl] = jax.nn.sigmoid(_nn(xcb[:, sl], wa_ref[p]) + ba_ref[:, sl])
            ig_ref[:, sl] = jax.nn.sigmoid(_nn(xcb[:, sl], wx_ref[p]) + bx_ref[:, sl])
        a, s = _lru_coeffs(r_ref[...], _softplus_neg(lam_ref[...]))
        a_sc[...] = a
        s_ref[...] = s
        hr_ref[...] = s * (ig_ref[...] * xc)

        def blk(j, h):
            st = pl.multiple_of(j * 8, 8)
            a8 = a_sc[pl.ds(st, 8), :]
            u8 = hr_ref[pl.ds(st, 8), :]
            rows = []
            for k in range(8):
                h = a8[k:k + 1, :] * h + u8[k:k + 1, :]
                rows.append(h)
            hr_ref[pl.ds(st, 8), :] = jnp.concatenate(rows, axis=0)
            return h

        h_sc[0:1, :] = lax.fori_loop(0, nb8, blk, h_sc[0:1, :])
        yain_ref[...] = (hr_ref[...] * _gelu(xg_ref[...])).astype(BF16)

    prev = pl.BlockSpec((8, D), lambda i: (jnp.maximum(i * nb8 - 1, 0), 0))
    full = lambda shape: pl.BlockSpec(shape, lambda i: tuple(0 for _ in shape))
    f32 = jax.ShapeDtypeStruct((t, D), F32)
    body, specs, operands = _behind(body, after)
    return pl.pallas_call(
        body, grid=(t // tm,), name="rglru_fwd",
        in_specs=specs + [_row_spec(tm, D), prev, _row_spec(tm, D), full((4, D)), _vec_spec(D), full((8, 128, 128)),
                          _vec_spec(D), full((8, 128, 128)), _vec_spec(D), _vec_spec(D)],
        out_specs=[_row_spec(tm, D)] * 7,
        out_shape=[f32, jax.ShapeDtypeStruct((t, D), BF16), f32, f32, f32, f32, f32],
        scratch_shapes=[pltpu.VMEM((tm + 8, D), F32), pltpu.VMEM((8, D), F32)],
        compiler_params=_params("arbitrary"),
    )(*operands, xr, xr, xg, conv_w, conv_b, wa2, ba, wx2, bx, lam)


def _bias_fwd(table_t, onehot_t):
    def body(t_ref, e_ref, o_ref):
        o_ref[...] = jnp.dot(t_ref[...], e_ref[...], preferred_element_type=F32, precision=lax.Precision.HIGHEST)

    return pl.pallas_call(body, out_shape=jax.ShapeDtypeStruct((N_HEADS, CHUNK * KB), F32), name="bias_fwd",
                          compiler_params=_params())(table_t, onehot_t)


def _bias_bwd(dbias_flat, onehot_t, ds_rows):
    def body(d_ref, e_ref, s_ref, o_ref, so_ref):
        o_ref[...] = lax.dot_general(d_ref[...], e_ref[...], (((1,), (1,)), ((), ())), preferred_element_type=F32,
                                     precision=lax.Precision.HIGHEST)
        so_ref[...] = jnp.zeros_like(so_ref)
        for r in range(4):
            so_ref[:, r:r + 1] = jnp.sum(s_ref[:, r * CHUNK:(r + 1) * CHUNK], axis=1, keepdims=True)

    return pl.pallas_call(body, out_shape=[jax.ShapeDtypeStruct((N_HEADS, N_BUCKETS), F32), jax.ShapeDtypeStruct((8, 128), F32)],
                          name="bias_bwd", compiler_params=_params())(dbias_flat, onehot_t, ds_rows)


def _stack_heads(q):
    return jnp.concatenate(
        [jnp.concatenate([q[:, (4 * g + r) * HEAD_DIM:(4 * g + r + 1) * HEAD_DIM] for g in range(4)], axis=1)
         for r in range(4)], axis=0)


def _unstack_heads(o):
    return jnp.concatenate([o[r * CHUNK:(r + 1) * CHUNK, g * HEAD_DIM:(g + 1) * HEAD_DIM] for g in range(4) for r in range(4)],
                           axis=1)


def _block_diag(w, mask):
    return jnp.concatenate([w] * 4, axis=0) * mask


def _group_softmax(qk, bias_g, sink, valid):
    s = qk * (HEAD_DIM ** -0.5) + bias_g
    s = jnp.where(valid, s, NEG_INF)
    m = jnp.maximum(jnp.max(s, axis=0, keepdims=True), sink)
    e = jnp.exp(s - m)
    es = jnp.exp(sink - m)
    inv = 1.0 / (jnp.sum(e, axis=0, keepdims=True) + es)
    return e * inv, es * inv


def _attn_fwd(sink_rows, q, kp, vp, bias_t, mask, after=None):
    t = q.shape[0]
    per_step = 8

    def body(sink_ref, q_ref, kp_ref, vp_ref, bias_ref, mask_ref, o_ref):
        owns = [mask_ref[g * KP:(g + 1) * KP, :] for g in range(4)]
        for k in range(per_step):
            c = pl.program_id(0) * per_step + k
            rows = slice(k * CHUNK, (k + 1) * CHUNK)
            st = pl.multiple_of(c * CHUNK, CHUNK)
            kw = kp_ref[pl.ds(st, KP), :]
            vw = vp_ref[pl.ds(st, KP), :]
            q_all = _stack_heads(q_ref[rows, :])
            valid = lax.broadcasted_iota(jnp.int32, (KP, 1), 0) + c * CHUNK >= PAD_KEYS
            scores = [_nt(kw * owns[g], q_all) for g in range(4)]
            ps = [_group_softmax(scores[g], bias_ref[g * KP:(g + 1) * KP, :], sink_ref[g:g + 1, :], valid)[0]
                  for g in range(4)]
            o_all = sum(_tn(ps[g].astype(BF16), vw * owns[g]) for g in range(4))
            o_ref[rows, :] = _unstack_heads(o_all).astype(BF16)

    body, specs, operands = _behind(body, after)
    return pl.pallas_call(
        body, grid=(t // (per_step * CHUNK),), name="attn_fwd",
        in_specs=specs + [_WHOLE, _row_spec(per_step * CHUNK, D), _WHOLE, _WHOLE, _WHOLE, _WHOLE],
        out_specs=_row_spec(per_step * CHUNK, D),
        out_shape=jax.ShapeDtypeStruct((t, D), BF16),
        compiler_params=_params("arbitrary"),
    )(*operands, sink_rows, q, kp, vp, bias_t, mask)


def _merge_fwd(yain, o, gate, h1, w_lru, w_att, w_o, gpost):
    t = h1.shape[0]
    tm = _tile(t)

    def body(ya_ref, o_ref, g_ref, h_ref, wl_ref, wa_ref, wo_ref, gp_ref, h2_ref, mo_ref, mg_ref, ya_out, yb_out):
        ya = _nn(ya_ref[...], wl_ref[...])
        yb = _nn(o_ref[...], wa_ref[...])
        g0 = g_ref[:, 0:D]
        g1 = g_ref[:, D:2 * D]
        mg = (g0 * ya + g1 * yb).astype(BF16)
        mo = _nn(mg, wo_ref[...])
        ya_out[...] = (ya * (g0 * (1.0 - g0))).astype(BF16)
        yb_out[...] = (yb * (g1 * (1.0 - g1))).astype(BF16)
        mg_ref[...] = mg
        mo_ref[...] = mo
        h2_ref[...] = h_ref[...] + _rms(mo, gp_ref[...])

    f32 = jax.ShapeDtypeStruct((t, D), F32)
    b16 = jax.ShapeDtypeStruct((t, D), BF16)
    return pl.pallas_call(
        body, grid=(t // tm,), name="merge_fwd",
        in_specs=[_row_spec(tm, D), _row_spec(tm, D), _row_spec(tm, 2 * D), _row_spec(tm, D), _WHOLE, _WHOLE, _WHOLE,
                  _vec_spec(D)],
        out_specs=[_row_spec(tm, D)] * 5,
        out_shape=[f32, f32, b16, b16, b16],
        compiler_params=_params("arbitrary"),
    )(yain, o, gate, h1, w_lru, w_att, w_o, gpost)


def _ffn_bwd(dh, x, f, a, b, gpre, gpost, w1g, w3g, w2g, name):
    t = x.shape[0]
    tm = _tile(t, TM_SCAN)

    def body(dh_ref, x_ref, f_ref, a_ref, b_ref, gpre_ref, gpost_ref, w1_ref, w3_ref, w2_ref,
             dx_ref, n_ref, da_ref, db_ref, df_ref, dgpre_ref, dgpost_ref):
        @pl.when(pl.program_id(0) == 0)
        def _():
            dgpre_ref[...] = jnp.zeros_like(dgpre_ref)
            dgpost_ref[...] = jnp.zeros_like(dgpost_ref)

        dhv = dh_ref[...]
        xv = x_ref[...]
        df, dgp = _rms_bwd(0.5 * dhv, f_ref[...], gpost_ref[...])
        dgpost_ref[...] += dgp
        dfb = df.astype(BF16)
        df_ref[...] = dfb
        n_ref[...] = _rms(xv, gpre_ref[...]).astype(BF16)
        dn = jnp.zeros((tm, D), F32)
        for s in range(NSH):
            av = a_ref[s].astype(F32)
            bv = b_ref[s].astype(F32)
            sg = jax.nn.sigmoid(av)
            dhm = _nt(dfb, w2_ref[s])
            dab = (dhm * bv * (sg * (1.0 + av * (1.0 - sg)))).astype(BF16)
            dbb = (dhm * (av * sg)).astype(BF16)
            da_ref[s] = dab
            db_ref[s] = dbb
            dn = dn + _nn(dab, w1_ref[s]) + _nn(dbb, w3_ref[s])
        dxn, dg = _rms_bwd(dn, xv, gpre_ref[...])
        dgpre_ref[...] += dg
        dx_ref[...] = dhv + dxn

    sh = pl.BlockSpec((NSH, tm, FF_S), lambda i: (0, i, 0))
    act = jax.ShapeDtypeStruct((NSH, t, FF_S), BF16)
    vec = jax.ShapeDtypeStruct((1, D), F32)
    return pl.pallas_call(
        body, grid=(t // tm,), name=name,
        in_specs=[_row_spec(tm, D), _row_spec(tm, D), _row_spec(tm, D), sh, sh, _vec_spec(D), _vec_spec(D), _WHOLE, _WHOLE,
                  _WHOLE],
        out_specs=[_row_spec(tm, D), _row_spec(tm, D), sh, sh, _row_spec(tm, D), _vec_spec(D), _vec_spec(D)],
        out_shape=[jax.ShapeDtypeStruct((t, D), F32), jax.ShapeDtypeStruct((t, D), BF16), act, act,
                   jax.ShapeDtypeStruct((t, D), BF16), vec, vec],
        compiler_params=_params("arbitrary"),
    )(dh, x, f, a, b, gpre, gpost, w1g, w3g, w2g)


def _behind(body, after):
    if after is None:
        return body, [], []

    def ordered(_, *refs):
        body(*refs)

    return ordered, [_ANY], [after]


def _ffn_bwd_acts(dh, x, f, a, b, gpre, gpost, w2g, name):
    t = x.shape[0]
    tm = _tile(t)

    def body(dh_ref, x_ref, f_ref, a_ref, b_ref, gpre_ref, gpost_ref, w2_ref, n_ref, da_ref, db_ref, df_ref, dgpost_ref):
        @pl.when(pl.program_id(0) == 0)
        def _():
            dgpost_ref[...] = jnp.zeros_like(dgpost_ref)

        df, dgp = _rms_bwd(0.5 * dh_ref[...], f_ref[...], gpost_ref[...])
        dgpost_ref[...] += dgp
        dfb = df.astype(BF16)
        df_ref[...] = dfb
        n_ref[...] = _rms(x_ref[...], gpre_ref[...]).astype(BF16)
        for s in range(NSH):
            av = a_ref[s].astype(F32)
            bv = b_ref[s].astype(F32)
            sg = jax.nn.sigmoid(av)
            dhm = _nt(dfb, w2_ref[s])
            da_ref[s] = (dhm * bv * (sg * (1.0 + av * (1.0 - sg)))).astype(BF16)
            db_ref[s] = (dhm * (av * sg)).astype(BF16)

    sh = pl.BlockSpec((NSH, tm, FF_S), lambda i: (0, i, 0))
    act = jax.ShapeDtypeStruct((NSH, t, FF_S), BF16)
    b16 = jax.ShapeDtypeStruct((t, D), BF16)
    return pl.pallas_call(
        body, grid=(t // tm,), name=name,
        in_specs=[_row_spec(tm, D), _row_spec(tm, D), _row_spec(tm, D), sh, sh, _vec_spec(D), _vec_spec(D), _WHOLE],
        out_specs=[_row_spec(tm, D), sh, sh, _row_spec(tm, D), _vec_spec(D)],
        out_shape=[b16, act, act, b16, jax.ShapeDtypeStruct((1, D), F32)],
        compiler_params=_params("arbitrary"),
    )(dh, x, f, a, b, gpre, gpost, w2g)


def _ffn_bwd_input(dh, x, da, db, gpre, w1g, w3g, name, after):
    t = x.shape[0]
    tm = _tile(t)

    def body(dh_ref, x_ref, da_ref, db_ref, gpre_ref, w1_ref, w3_ref, dx_ref, dgpre_ref):
        @pl.when(pl.program_id(0) == 0)
        def _():
            dgpre_ref[...] = jnp.zeros_like(dgpre_ref)

        dn = jnp.zeros((tm, D), F32)
        for s in range(NSH):
            dn = dn + _nn(da_ref[s], w1_ref[s]) + _nn(db_ref[s], w3_ref[s])
        dxn, dg = _rms_bwd(dn, x_ref[...], gpre_ref[...])
        dgpre_ref[...] += dg
        dx_ref[...] = dh_ref[...] + dxn

    sh = pl.BlockSpec((NSH, tm, FF_S), lambda i: (0, i, 0))
    body, specs, operands = _behind(body, after)
    return pl.pallas_call(
        body, grid=(t // tm,), name=name,
        in_specs=specs + [_row_spec(tm, D), _row_spec(tm, D), sh, sh, _vec_spec(D), _WHOLE, _WHOLE],
        out_specs=[_row_spec(tm, D), _vec_spec(D)],
        out_shape=[jax.ShapeDtypeStruct((t, D), F32), jax.ShapeDtypeStruct((1, D), F32)],
        compiler_params=_params("arbitrary"),
    )(*operands, dh, x, da, db, gpre, w1g, w3g)


def _wgrad(a, b, a_spec, b_spec, out_spec, out_shape, grid, name, after=None):
    def body(a_ref, b_ref, o_ref):
        o_ref[...] = _tn(a_ref[...], b_ref[...]).astype(BF16)

    body, specs, operands = _behind(body, after)
    return pl.pallas_call(body, grid=grid, name=name, in_specs=specs + [a_spec, b_spec], out_specs=out_spec,
                          out_shape=jax.ShapeDtypeStruct(out_shape, BF16),
                          compiler_params=_params(*("arbitrary",) * len(grid)))(*operands, a, b)


def _wgrad_cols(act, dsh, width, name, after=None):
    t = act.shape[0]
    if dsh.ndim == 3:
        b_spec = pl.BlockSpec((None, t, width), lambda s, k: (s, 0, 0))
    else:
        b_spec = pl.BlockSpec((t, width), lambda s, k: (0, s))
    return _wgrad(act, dsh, pl.BlockSpec((t, 512), lambda s, k: (0, k)), b_spec,
                  pl.BlockSpec((None, 512, width), lambda s, k: (s, k, 0)), (NSH, D, width), (NSH, 2), name, after)


def _wgrad_rows(hm, df, name, after=None):
    t = df.shape[0]
    return _wgrad(hm, df, pl.BlockSpec((None, t, FF_S), lambda s: (s, 0, 0)), pl.BlockSpec((t, D), lambda s: (0, 0)),
                  pl.BlockSpec((None, FF_S, D), lambda s: (s, 0, 0)), (NSH, FF_S, D), (NSH,), name, after)


def _wgrad_sq(a, b, name, after=None):
    t = a.shape[0]
    return _wgrad(a, b, pl.BlockSpec((t, 512), lambda i, j: (0, i)), pl.BlockSpec((t, 512), lambda i, j: (0, j)),
                  pl.BlockSpec((512, 512), lambda i, j: (i, j)), (D, D), (2, 2), name, after)


def _mix_bwd1(dh2, mo, gpost, gate, ya, yb, xg, hr, w_o, w_lru, w_att, after):
    t = dh2.shape[0]
    tm = _tile(t, TM_SCAN)

    def body(dh_ref, mo_ref, gp_ref, g_ref, ya_ref, yb_ref, xg_ref, hr_ref, wo_ref, wl_ref, wa_ref,
             dmo_ref, dya_ref, dyb_ref, dgate_ref, dhr_ref, dxg_ref, do_ref, dgp_ref, dbg_ref):
        @pl.when(pl.program_id(0) == 0)
        def _():
            dgp_ref[...] = jnp.zeros_like(dgp_ref)
            dbg_ref[...] = jnp.zeros_like(dbg_ref)

        dmo, dgp = _rms_bwd(dh_ref[...], mo_ref[...], gp_ref[...])
        dgp_ref[...] += dgp
        dmob = dmo.astype(BF16)
        dmo_ref[...] = dmob
        dm = _nt(dmob, wo_ref[...])
        g0 = g_ref[:, 0:D]
        g1 = g_ref[:, D:2 * D]
        dyab = (dm * g0).astype(BF16)
        dybb = (dm * g1).astype(BF16)
        dya_ref[...] = dyab
        dyb_ref[...] = dybb
        dg0 = dm * ya_ref[...].astype(F32)
        dg1 = dm * yb_ref[...].astype(F32)
        dgate_ref[:, 0:D] = dg0.astype(BF16)
        dgate_ref[:, D:2 * D] = dg1.astype(BF16)
        dbg_ref[:, 0:D] += jnp.sum(dg0, axis=0, keepdims=True)
        dbg_ref[:, D:2 * D] += jnp.sum(dg1, axis=0, keepdims=True)
        dyain = _nt(dyab, wl_ref[...])
        do_ref[...] = _nt(dybb, wa_ref[...]).astype(BF16)
        xgv = xg_ref[...]
        gelu, gelu_grad = _gelu_and_grad(xgv)
        dhr_ref[...] = dyain * gelu
        dxg_ref[...] = (dyain * hr_ref[...] * gelu_grad).astype(BF16)

    b16 = jax.ShapeDtypeStruct((t, D), BF16)
    body, specs, operands = _behind(body, after)
    return pl.pallas_call(
        body, grid=(t // tm,), name="mix_bwd1",
        in_specs=specs + [_row_spec(tm, D), _row_spec(tm, D), _vec_spec(D), _row_spec(tm, 2 * D), _row_spec(tm, D),
                          _row_spec(tm, D), _row_spec(tm, D), _row_spec(tm, D), _WHOLE, _WHOLE, _WHOLE],
        out_specs=[_row_spec(tm, D), _row_spec(tm, D), _row_spec(tm, D), _row_spec(tm, 2 * D), _row_spec(tm, D),
                   _row_spec(tm, D), _row_spec(tm, D), _vec_spec(D), _vec_spec(2 * D)],
        out_shape=[b16, b16, b16, jax.ShapeDtypeStruct((t, 2 * D), BF16), jax.ShapeDtypeStruct((t, D), F32), b16, b16,
                   jax.ShapeDtypeStruct((1, D), F32), jax.ShapeDtypeStruct((1, 2 * D), F32)],
        compiler_params=_params("arbitrary"),
    )(*operands, dh2, mo, gpost, gate, ya, yb, xg, hr, w_o, w_lru, w_att)


def _rglru_bwd(dhr, hr, xc, r, ig, a, s, xr, conv_w, wa2, wx2, lam, after):
    t = dhr.shape[0]
    tm = _tile(t, TM_SCAN)
    nb8 = tm // 8
    nt = t // tm

    def body(dhr_ref, hr_ref, hrp_ref, xc_ref, r_ref, ig_ref, a_sc, s_ref, xr_ref, cw_ref, wa_ref, wx_ref, lam_ref,
             dxr_ref, dwa_ref, dwx_ref, dba_ref, dbx_ref, dlam_ref, dcw_ref, dcb_ref,
             ext_h, ext_d, g_sc, c_sc, nxt_sc):
        i = pl.program_id(0)
        first_tile = i == nt - 1

        @pl.when(i == 0)
        def _():
            c_sc[...] = jnp.zeros_like(c_sc)
            nxt_sc[...] = jnp.zeros_like(nxt_sc)
            for ref in (dwa_ref, dwx_ref, dba_ref, dbx_ref, dlam_ref, dcw_ref, dcb_ref):
                ref[...] = jnp.zeros_like(ref)

        lamv = lam_ref[...]
        sp = _softplus_neg(lamv)
        rv = r_ref[...]
        igv = ig_ref[...]
        xcv = xc_ref[...]
        a = a_sc[...]
        s = s_ref[...]

        def blk(jj, c):
            st = pl.multiple_of((nb8 - 1 - jj) * 8, 8)
            d8 = dhr_ref[pl.ds(st, 8), :]
            a8 = a_sc[pl.ds(st, 8), :]
            rows = [None] * 8
            for k in range(7, -1, -1):
                g = d8[k:k + 1, :] + c
                c = a8[k:k + 1, :] * g
                rows[k] = g
            g_sc[pl.ds(st, 8), :] = jnp.concatenate(rows, axis=0)
            return c

        c_sc[0:1, :] = lax.fori_loop(0, nb8, blk, c_sc[0:1, :])
        g = g_sc[...]
        ext_h[0:8, :] = jnp.where(first_tile, 0.0, hrp_ref[...])
        ext_h[8:8 + tm, :] = hr_ref[...]
        hprev = ext_h[pl.ds(7, tm), :]
        d_s = g * (igv * xcv)
        dig = g * s * xcv
        dxc = g * s * igv
        dla = (g * hprev) * a - d_s * ((a * a) / s)
        dr_pre = (dla * (-LRU_C * sp)) * (rv * (1.0 - rv))
        di_pre = dig * (igv * (1.0 - igv))
        dlam_ref[...] += jnp.sum(dla * (LRU_C * rv), axis=0, keepdims=True) * jax.nn.sigmoid(-lamv)
        dba_ref[...] += jnp.sum(dr_pre, axis=0, keepdims=True)
        dbx_ref[...] += jnp.sum(di_pre, axis=0, keepdims=True)
        drb = dr_pre.astype(BF16)
        dib = di_pre.astype(BF16)
        xcb = xcv.astype(BF16)
        ext_d[tm:tm + 8, :] = nxt_sc[...]
        for p in range(8):
            sl = slice(p * 128, (p + 1) * 128)
            ext_d[0:tm, sl] = dxc[:, sl] + _nt(drb[:, sl], wa_ref[p]) + _nt(dib[:, sl], wx_ref[p])
            dwa_ref[p] += _tn(xcb[:, sl], drb[:, sl])
            dwx_ref[p] += _tn(xcb[:, sl], dib[:, sl])
        dxcv = ext_d[0:tm, :]
        nxt_sc[...] = ext_d[0:8, :]
        dcb_ref[...] += jnp.sum(dxcv, axis=0, keepdims=True)
        xrv = xr_ref[...]
        dxr = jnp.zeros((tm, D), F32)
        for tap in range(4):
            ext_h[0:tm, :] = ext_d[pl.ds(3 - tap, tm), :]
            ahead = ext_h[0:tm, :]
            dxr = dxr + ahead * cw_ref[tap:tap + 1, :]
            dcw_ref[tap:tap + 1, :] += jnp.sum(ahead * xrv, axis=0, keepdims=True)
        dxr_ref[...] = dxr.astype(BF16)

    rev = pl.BlockSpec((tm, D), lambda i: (nt - 1 - i, 0))
    prev = pl.BlockSpec((8, D), lambda i: (jnp.maximum((nt - 1 - i) * nb8 - 1, 0), 0))
    full = lambda shape: pl.BlockSpec(shape, lambda i: tuple(0 for _ in shape))
    vec = jax.ShapeDtypeStruct((1, D), F32)
    blocks = jax.ShapeDtypeStruct((8, 128, 128), F32)
    body, specs, operands = _behind(body, after)
    return pl.pallas_call(
        body, grid=(nt,), name="rglru_bwd",
        in_specs=specs + [rev, rev, prev, rev, rev, rev, rev, rev, rev, full((4, D)), full((8, 128, 128)),
                          full((8, 128, 128)), _vec_spec(D)],
        out_specs=[rev, full((8, 128, 128)), full((8, 128, 128)), _vec_spec(D), _vec_spec(D), _vec_spec(D), full((4, D)),
                   _vec_spec(D)],
        out_shape=[jax.ShapeDtypeStruct((t, D), BF16), blocks, blocks, vec, vec, vec, jax.ShapeDtypeStruct((4, D), F32), vec],
        scratch_shapes=[pltpu.VMEM((tm + 8, D), F32), pltpu.VMEM((tm + 8, D), F32),
                        pltpu.VMEM((tm, D), F32), pltpu.VMEM((8, D), F32), pltpu.VMEM((8, D), F32)],
        compiler_params=_params("arbitrary"),
    )(*operands, dhr, hr, hr, xc, r, ig, a, s, xr, conv_w, wa2, wx2, lam)


def _attn_bwd(sink_rows, q, kp, vp, bias_t, mask, do):
    t = q.shape[0]
    tp = kp.shape[0]
    per_step = 8

    def body(sink_ref, q_ref, kp_ref, vp_ref, bias_ref, mask_ref, do_ref, dq_ref, dk_ref, dv_ref, dbias_ref, ds_ref):
        @pl.when(pl.program_id(0) == 0)
        def _():
            for ref in (dk_ref, dv_ref, dbias_ref, ds_ref):
                ref[...] = jnp.zeros_like(ref)

        maskv = mask_ref[...]
        lane_group = lax.broadcasted_iota(jnp.int32, (1, 4 * HEAD_DIM), 1) // HEAD_DIM

        def own_blocks(full):
            out = full[0:KP]
            for g in range(1, 4):
                out = jnp.where(lane_group == g, full[g * KP:(g + 1) * KP], out)
            return out

        dsc_sum, dsinks, dks, dvs = 0.0, [0.0] * 4, [], []
        for k in range(per_step):
            c = pl.program_id(0) * per_step + k
            chunk = slice(k * CHUNK, (k + 1) * CHUNK)
            st = pl.multiple_of(c * CHUNK, CHUNK)
            kbd = _block_diag(kp_ref[pl.ds(st, KP), :], maskv)
            vbd = _block_diag(vp_ref[pl.ds(st, KP), :], maskv)
            q_all = _stack_heads(q_ref[chunk, :])
            do_all = _stack_heads(do_ref[chunk, :])
            valid = lax.broadcasted_iota(jnp.int32, (KP, 1), 0) + c * CHUNK >= PAD_KEYS
            qk = _nt(kbd, q_all)
            dp = _nt(vbd, do_all)
            ps, dscs = [], []
            for g in range(4):
                rows = slice(g * KP, (g + 1) * KP)
                p, sink_p = _group_softmax(qk[rows], bias_ref[rows, :], sink_ref[g:g + 1, :], valid)
                delta = jnp.sum(p * dp[rows], axis=0, keepdims=True)
                ps.append(p)
                dscs.append(p * (dp[rows] - delta))
                dsinks[g] = dsinks[g] - sink_p * delta
            dsc = jnp.concatenate(dscs, axis=0)
            dsc_sum = dsc_sum + dsc
            dsb = (dsc * (HEAD_DIM ** -0.5)).astype(BF16)
            dq_ref[chunk, :] = _unstack_heads(_tn(dsb, kbd)).astype(BF16)
            dks.append((st, own_blocks(_nn(dsb, q_all))))
            dvs.append((st, own_blocks(_nn(jnp.concatenate(ps, axis=0).astype(BF16), do_all))))
        dbias_ref[...] += dsc_sum
        for g in range(4):
            ds_ref[g:g + 1, :] += dsinks[g]
        for (st, dkw), (_, dvw) in zip(dks, dvs):
            dk_ref[pl.ds(st, KP), :] += dkw
            dv_ref[pl.ds(st, KP), :] += dvw

    full = lambda shape: pl.BlockSpec(shape, lambda i: tuple(0 for _ in shape))
    return pl.pallas_call(
        body, grid=(t // (per_step * CHUNK),), name="attn_bwd",
        in_specs=[_WHOLE, _row_spec(per_step * CHUNK, D), _WHOLE, _WHOLE, _WHOLE, _WHOLE, _row_spec(per_step * CHUNK, D)],
        out_specs=[_row_spec(per_step * CHUNK, D), full((tp, KV_W)), full((tp, KV_W)), full((4 * KP, 4 * CHUNK)),
                   full((8, 4 * CHUNK))],
        out_shape=[jax.ShapeDtypeStruct((t, D), BF16), jax.ShapeDtypeStruct((tp, KV_W), F32),
                   jax.ShapeDtypeStruct((tp, KV_W), F32), jax.ShapeDtypeStruct((4 * KP, 4 * CHUNK), F32),
                   jax.ShapeDtypeStruct((8, 4 * CHUNK), F32)],
        compiler_params=_params("arbitrary"),
    )(sink_rows, q, kp, vp, bias_t, mask, do)


def _mix_bwd2(dproj, dgate, h1, dh2, gmix, w_in_g, w_gate_g, after):
    t = h1.shape[0]
    tm = _tile(t)

    def body(dp_ref, dg_ref, h_ref, dh_ref, g_ref, win_ref, wg_ref, dh1_ref, dgm_ref):
        @pl.when(pl.program_id(0) == 0)
        def _():
            dgm_ref[...] = jnp.zeros_like(dgm_ref)

        du = jnp.zeros((tm, D), F32)
        for s in range(NSH):
            du = du + _nt(dp_ref[:, s * IN_S:(s + 1) * IN_S], win_ref[s])
            du = du + _nt(dg_ref[:, s * GATE_S:(s + 1) * GATE_S], wg_ref[s])
        dxn, dg = _rms_bwd(du, h_ref[...], g_ref[...])
        dgm_ref[...] += dg
        dh1_ref[...] = dh_ref[...] + dxn

    body, specs, operands = _behind(body, after)
    return pl.pallas_call(
        body, grid=(t // tm,), name="mix_bwd2",
        in_specs=specs + [_row_spec(tm, NSH * IN_S), _row_spec(tm, 2 * D), _row_spec(tm, D), _row_spec(tm, D), _vec_spec(D),
                          _WHOLE, _WHOLE],
        out_specs=[_row_spec(tm, D), _vec_spec(D)],
        out_shape=[jax.ShapeDtypeStruct((t, D), F32), jax.ShapeDtypeStruct((1, D), F32)],
        compiler_params=_params("arbitrary"),
    )(*operands, dproj, dgate, h1, dh2, gmix, w_in_g, w_gate_g)


def _band_onehot():
    nb = N_BUCKETS // 2
    max_exact = nb // 2
    rel = jnp.arange(KB)[None, :] - PAD_KEYS - jnp.arange(CHUNK)[:, None]
    ret = jnp.where(rel > 0, nb, 0)
    n = jnp.abs(rel)
    nf = jnp.maximum(n, 1).astype(jnp.float32)
    large = max_exact + (jnp.log(nf / max_exact) / math.log(128 / max_exact) * (nb - max_exact)).astype(jnp.int32)
    large = jnp.minimum(large, nb - 1)
    buckets = (ret + jnp.where(n < max_exact, n, large)).reshape(1, CHUNK * KB)
    return (buckets == jnp.arange(N_BUCKETS)[:, None]).astype(F32)


def _pair_blocks(w):
    pairs = w.reshape(8, 2, 64, 64)
    z = jnp.zeros((8, 64, 64), w.dtype)
    return jnp.concatenate([jnp.concatenate([pairs[:, 0], z], axis=2), jnp.concatenate([z, pairs[:, 1]], axis=2)], axis=1)


def _unpair_blocks(w2):
    return jnp.stack([w2[:, 0:64, 0:64], w2[:, 64:128, 64:128]], axis=1).reshape(16, 64, 64)


def _local_step(x, target, weights, sm, reducer):
    row = lambda v: v.reshape(1, -1)
    onehot_t = _band_onehot()
    bias = _bias_fwd(sm["rel_bias"].T, onehot_t).reshape(4, 4, CHUNK, KB)
    bias_t = jnp.pad(jnp.transpose(bias, (0, 3, 1, 2)), ((0, 0), (0, KP - KB), (0, 0), (0, 0))).reshape(4 * KP, 4 * CHUNK)
    sink_rows = jnp.pad(jnp.repeat(sm["attn_sinks"].reshape(4, 4), CHUNK, axis=1), ((0, 4), (0, 0)))
    grp = jnp.arange(4 * KP)[:, None] // KP == jnp.arange(4 * HEAD_DIM)[None, :] // HEAD_DIM
    mask = (grp & (jnp.arange(4 * KP)[:, None] % KP < KB)).astype(BF16)
    wa2 = _pair_blocks(sm["rg_a_w"]).astype(BF16)
    wx2 = _pair_blocks(sm["rg_x_w"]).astype(BF16)
    wg = dict(weights("ffn1_up", [bias_t, sink_rows, mask, wa2, wx2]))
    sm = dict(sm, conv_w=wg["conv_w"])

    a1, b1, hm1 = _ffn_up(x, row(sm["ffn1_pre_g"]), wg["ffn1_w1"], wg["ffn1_w3"], "ffn1_up")
    wg.update(weights("ffn1_down", hm1))
    h1, f1 = _ffn_down(x, hm1, wg["ffn1_w2"], row(sm["ffn1_post_g"]), "ffn1_down")
    wg.update(weights("mix_in", h1))
    u, q, k, v, xr, xg, gate = _mix_proj(h1, row(sm["mix_pre_g"]), wg["w_in"], wg["w_gate"], row(sm["b_gate"]))
    token = weights("mix_out", u, begin=True)
    hr, yain, xc, r, ig, lru_a, lru_s = _rglru_fwd(xr, xg, sm["conv_w"], row(sm["conv_b"]), wa2, row(sm["rg_a_b"]), wx2,
                                                   row(sm["rg_x_b"]), row(sm["lru_lambda"]), token)
    token = weights("ffn2", hr, begin=True)
    kp = jnp.pad(k, ((PAD_KEYS, KP - KB), (0, 0)))
    vp = jnp.pad(v, ((PAD_KEYS, KP - KB), (0, 0)))
    o = _attn_fwd(sink_rows, q, kp, vp, bias_t, mask, token)
    wg.update(weights("mix_out", o))
    w_lru = wg["w_lru_out"].reshape(D, D)
    w_att = wg["w_attn_out"].reshape(D, D)
    w_o = wg["w_o"].reshape(D, D)
    wg.update(weights("ffn2", o))
    h2, mo, merged, ya, yb = _merge_fwd(yain, o, gate, h1, w_lru, w_att, w_o, row(sm["mix_post_g"]))
    dy, a2, b2, hm2, f2, sq = _ffn_fwd(h2, row(sm["ffn2_pre_g"]), wg["ffn2_w1"], wg["ffn2_w3"], wg["ffn2_w2"],
                                       row(sm["ffn2_post_g"]), "ffn2_fwd", target)

    big, small = {}, {}
    dh2, n2, da2, db2, df2, small["ffn2_pre_g"], small["ffn2_post_g"] = _ffn_bwd(
        dy, h2, f2, a2, b2, row(sm["ffn2_pre_g"]), row(sm["ffn2_post_g"]), wg["ffn2_w1"], wg["ffn2_w3"], wg["ffn2_w2"],
        "ffn2_bwd")
    big["ffn2_w1"] = _wgrad_rows(da2, n2, "dw_ffn2_w1")
    big["ffn2_w3"] = _wgrad_rows(db2, n2, "dw_ffn2_w3")
    big["ffn2_w2"] = _wgrad_rows(hm2, df2, "dw_ffn2_w2")
    token = reducer.begin("ffn2", {n: big[n] for n in ("ffn2_w1", "ffn2_w3", "ffn2_w2")})
    dmo, dya, dyb, dgate, dhr, dxg, do, small["mix_post_g"], small["b_gate"] = _mix_bwd1(
        dh2, mo, row(sm["mix_post_g"]), gate, ya, yb, xg, hr, w_o, w_lru, w_att, token)
    big["w_o"] = _wgrad_sq(merged, dmo, "dw_w_o").reshape(NSH, D // NSH, D)
    big["w_lru_out"] = _wgrad_sq(yain, dya, "dw_w_lru_out").reshape(NSH, D // NSH, D)
    big["w_attn_out"] = _wgrad_sq(o, dyb, "dw_w_attn_out").reshape(NSH, D // NSH, D)
    token = reducer.advance("ffn2", big["w_attn_out"])
    (dxr, dwa2, dwx2, small["rg_a_b"], small["rg_x_b"], small["lru_lambda"], small["conv_w"], small["conv_b"]) = _rglru_bwd(
        dhr, hr, xc, r, ig, lru_a, lru_s, xr, sm["conv_w"], wa2, wx2, row(sm["lru_lambda"]), token)
    small["rg_a_w"] = _unpair_blocks(dwa2)
    small["rg_x_w"] = _unpair_blocks(dwx2)
    dq, dkp, dvp, dbias_t, ds_rows = _attn_bwd(sink_rows, q, kp, vp, bias_t, mask, do)
    dbias = jnp.transpose(dbias_t.reshape(4, KP, 4, CHUNK)[:, :KB], (0, 2, 3, 1)).reshape(N_HEADS, CHUNK * KB)
    drel_t, dsinks = _bias_bwd(dbias, onehot_t, ds_rows)
    small["attn_sinks"] = dsinks[0:4, 0:4].reshape(N_HEADS)
    small["rel_bias"] = drel_t.T
    t = x.shape[0]
    dproj = jnp.concatenate([dq, dkp[PAD_KEYS:PAD_KEYS + t].astype(BF16), dvp[PAD_KEYS:PAD_KEYS + t].astype(BF16), dxr, dxg],
                            axis=1)
    big["w_in"] = _wgrad_cols(u, dproj, IN_S, "dw_w_in")
    big["w_gate"] = _wgrad_cols(u, dgate, GATE_S, "dw_w_gate")
    token = reducer.begin("mix", {n: big[n] for n in ("w_in", "w_gate", "w_lru_out", "w_attn_out", "w_o")})
    dh1, small["mix_pre_g"] = _mix_bwd2(dproj, dgate, h1, dh2, row(sm["mix_pre_g"]), wg["w_in"], wg["w_gate"], token)
    n1, da1, db1, df1, small["ffn1_post_g"] = _ffn_bwd_acts(
        dh1, x, f1, a1, b1, row(sm["ffn1_pre_g"]), row(sm["ffn1_post_g"]), wg["ffn1_w2"], "ffn1_bwd_acts")
    token = reducer.advance("mix", df1)
    big["ffn1_w1"] = _wgrad_rows(da1, n1, "dw_ffn1_w1", token)
    big["ffn1_w3"] = _wgrad_rows(db1, n1, "dw_ffn1_w3", token)
    big["ffn1_w2"] = _wgrad_rows(hm1, df1, "dw_ffn1_w2", token)
    token = reducer.begin("ffn1", {n: big[n] for n in ("ffn1_w1", "ffn1_w3", "ffn1_w2")})
    dx, small["ffn1_pre_g"] = _ffn_bwd_input(dh1, x, da1, db1, row(sm["ffn1_pre_g"]), wg["ffn1_w1"], wg["ffn1_w3"],
                                             "ffn1_bwd_input", token)
    return sq, dx, big, small


_ANY = pl.BlockSpec(memory_space=pl.ANY)


def _place():
    return lax.axis_index("x"), lax.axis_index("y"), lax.axis_index("c")


def _other_chips(x, y):
    return [(1 - x, y), (x, 1 - y), (1 - x, 1 - y)]


_HBM = pl.BlockSpec(memory_space=pltpu.HBM)
_SEM = pl.BlockSpec(memory_space=pltpu.SEMAPHORE)
_EFFECT = pltpu.SideEffectType.DATAFLOW_SIDE_EFFECTING


def _cast_into_slot(w, chip, name, after=None):
    r, cc = w.shape
    rows = r // 4

    def body(chip_ref, *refs):
        w_ref, o_ref = refs[-2:]
        o_ref[...] = w_ref[...].astype(BF16)

    extra = [] if after is None else [after]
    return pl.pallas_call(
        body, name=name, out_shape=jax.ShapeDtypeStruct((NSH, r, cc), BF16),
        grid_spec=pltpu.PrefetchScalarGridSpec(
            num_scalar_prefetch=1, grid=(4,), in_specs=[_ANY] * len(extra) + [pl.BlockSpec((rows, cc), lambda i, chip: (i, 0))],
            out_specs=pl.BlockSpec((None, rows, cc), lambda i, chip: (chip[0], i, 0))),
        compiler_params=_params("arbitrary"))(chip, *extra, w)


def _piece(ref, slot, c):
    if ref.dtype == F32:
        return ref.at[slot]
    rh = ref.shape[1] // 2
    return ref.at[slot, pl.ds(pl.multiple_of(c * rh, 16), rh), :]


def _gather_start(stages, name):
    flat = [b for stage in stages for b in stage]
    n, ns = len(flat), len(stages)

    def body(*refs):
        ins, sems, token = refs[:n], refs[n:n + 2 * ns], refs[-1]
        x, y, c = _place()
        me = 2 * x + y
        k = 0
        for s, stage in enumerate(stages):
            for i in range(len(stage)):
                for j, (px, py) in enumerate(_other_chips(x, y)):
                    piece = _piece(ins[k], me, c)
                    pltpu.make_async_remote_copy(src_ref=piece, dst_ref=piece, send_sem=sems[2 * s].at[3 * i + j],
                                                 recv_sem=sems[2 * s + 1].at[3 * i + j], device_id=(px, py, c),
                                                 device_id_type=MESH).start()
                k += 1
        token[...] = jnp.zeros_like(token)

    sem_shapes = [pltpu.SemaphoreType.DMA((3 * len(stage),)) for stage in stages for _ in range(2)]
    outs = pl.pallas_call(
        body, name=name, in_specs=[_HBM] * n,
        out_specs=[_SEM] * (2 * ns) + [_HBM] * n + [pl.BlockSpec(memory_space=pltpu.VMEM)],
        out_shape=sem_shapes + [pltpu.HBM(b.shape, b.dtype) for b in flat] + [jax.ShapeDtypeStruct((8, 128), F32)],
        input_output_aliases={i: 2 * ns + i for i in range(n)},
        compiler_params=pltpu.CompilerParams(has_side_effects=_EFFECT),
    )(*[pltpu.with_memory_space_constraint(b, pltpu.HBM) for b in flat])
    sems, bufs, token = outs[:2 * ns], list(outs[2 * ns:2 * ns + n]), outs[-1]
    per_stage, k = [], 0
    for s, stage in enumerate(stages):
        per_stage.append((sems[2 * s], sems[2 * s + 1], bufs[k:k + len(stage)]))
        k += len(stage)
    return per_stage, token


def _gather_wait(send_sems, recv_sems, bufs, after, name):
    n = len(bufs)

    def body(*refs):
        ins, ssem, rsem = refs[:n], refs[n], refs[n + 1]
        x, y, c = _place()
        me = 2 * x + y
        for i in range(n):
            for j, (px, py) in enumerate(_other_chips(x, y)):
                cp = pltpu.make_async_remote_copy(src_ref=_piece(ins[i], me, c), dst_ref=_piece(ins[i], 2 * px + py, c),
                                                  send_sem=ssem.at[3 * i + j], recv_sem=rsem.at[3 * i + j],
                                                  device_id=(px, py, c), device_id_type=MESH)
                cp.wait_send()
                cp.wait_recv()

    afters = list(after) if isinstance(after, (list, tuple)) else [after]
    return pl.pallas_call(
        body, name=name, in_specs=[_HBM] * n + [_SEM, _SEM] + [_ANY] * len(afters), out_specs=[_HBM] * n,
        out_shape=[pltpu.HBM(b.shape, b.dtype) for b in bufs], input_output_aliases={i: i for i in range(n)},
        compiler_params=pltpu.CompilerParams(has_side_effects=_EFFECT),
    )(*bufs, send_sems, recv_sems, *afters)


def _sibling_fill(bufs, name):
    n = len(bufs)

    def body(*refs):
        ins, outs = refs[:n], refs[n:2 * n]
        send_sems, recv_sems = refs[2 * n:]
        x, y, c = _place()
        copies = []
        for i in range(n):
            for j, (px, py) in enumerate(_other_chips(x, y)):
                copies.append(pltpu.make_async_remote_copy(
                    src_ref=_piece(ins[i], 2 * px + py, c), dst_ref=_piece(outs[i], 2 * px + py, c),
                    send_sem=send_sems.at[3 * i + j], recv_sem=recv_sems.at[3 * i + j], device_id=(x, y, 1 - c),
                    device_id_type=MESH))
                copies[-1].start()
        for cp in copies:
            cp.wait()

    return pl.pallas_call(
        body, name=name, in_specs=[_ANY] * n, out_specs=[_ANY] * n,
        out_shape=[jax.ShapeDtypeStruct(b.shape, b.dtype) for b in bufs], input_output_aliases={i: i for i in range(n)},
        scratch_shapes=[pltpu.SemaphoreType.DMA((3 * n,)), pltpu.SemaphoreType.DMA((3 * n,))],
        compiler_params=pltpu.CompilerParams(has_side_effects=True),
    )(*bufs)


def _swap_plan(srcs, lands):
    x, y, c = _place()
    plan = []
    for src, land in zip(srcs, lands):
        rh = src.shape[1] // 2
        plan.append((src.at[:, pl.ds(pl.multiple_of((1 - c) * rh, 16), rh), :], land, (x, y, 1 - c)))
    return plan


def _owners_plan(srcs, lands):
    x, y, c = _place()
    return [(src.at[2 * px + py], land.at[j], (px, py, c))
            for src, land in zip(srcs, lands) for j, (px, py) in enumerate(_other_chips(x, y))]


def _exchange_start(srcs, lands, plan, copies, name):
    n, m = len(srcs), len(srcs) + len(lands)

    def body(*refs):
        send_sems, recv_sems, token = refs[m], refs[m + 1], refs[-1]
        for k, (src, dst, dev) in enumerate(plan(refs[:n], refs[n:m])):
            pltpu.make_async_remote_copy(src_ref=src, dst_ref=dst, send_sem=send_sems.at[k], recv_sem=recv_sems.at[k],
                                         device_id=dev, device_id_type=MESH).start()
        token[...] = jnp.zeros_like(token)

    both = list(srcs) + list(lands)
    outs = pl.pallas_call(
        body, name=name, in_specs=[_HBM] * m,
        out_specs=[_SEM, _SEM] + [_HBM] * m + [pl.BlockSpec(memory_space=pltpu.VMEM)],
        out_shape=[pltpu.SemaphoreType.DMA((copies,)), pltpu.SemaphoreType.DMA((copies,))]
        + [pltpu.HBM(b.shape, b.dtype) for b in both] + [jax.ShapeDtypeStruct((8, 128), F32)],
        input_output_aliases={i: 2 + i for i in range(m)},
        compiler_params=pltpu.CompilerParams(has_side_effects=_EFFECT),
    )(*[pltpu.with_memory_space_constraint(b, pltpu.HBM) for b in both])
    return (outs[0], outs[1]), list(outs[2:2 + n]), list(outs[2 + n:2 + m]), outs[-1]


def _exchange_wait(sems, srcs, lands, plan, after, name):
    n, m = len(srcs), len(srcs) + len(lands)

    def body(*refs):
        send_sems, recv_sems = refs[m], refs[m + 1]
        for k, (src, dst, dev) in enumerate(plan(refs[:n], refs[n:m])):
            cp = pltpu.make_async_remote_copy(src_ref=src, dst_ref=dst, send_sem=send_sems.at[k], recv_sem=recv_sems.at[k],
                                              device_id=dev, device_id_type=MESH)
            cp.wait_send()
            cp.wait_recv()

    both = list(srcs) + list(lands)
    afters = list(after) if isinstance(after, (list, tuple)) else [after]
    outs = pl.pallas_call(
        body, name=name, in_specs=[_HBM] * m + [_SEM, _SEM] + [_ANY] * len(afters), out_specs=[_HBM] * m,
        out_shape=[pltpu.HBM(b.shape, b.dtype) for b in both], input_output_aliases={i: i for i in range(m)},
        compiler_params=pltpu.CompilerParams(has_side_effects=_EFFECT),
    )(*both, sems[0], sems[1], *afters)
    return list(outs[:n]), list(outs[n:])


def _fill_plan(bufs, _):
    x, y, c = _place()
    return [(_piece(buf, 2 * px + py, c), _piece(buf, 2 * px + py, c), (x, y, 1 - c))
            for buf in bufs for px, py in _other_chips(x, y)]


class _Reducer:
    def __init__(self, where):
        self.state = {}
        self.where = where

    def begin(self, stage, grads):
        names = list(grads)
        full = [grads[n] for n in names]
        lands = [lax.empty((NSH, g.shape[1] // 2, g.shape[2]), g.dtype) for g in full]
        sems, full, lands, token = _exchange_start(full, lands, _swap_plan, len(full), "swap_start_" + stage)
        self.state[stage] = (names, sems, full, lands)
        return token

    def advance(self, stage, after):
        names, sems, full, lands = self.state[stage]
        full, got = _exchange_wait(sems, full, lands, _swap_plan, after, "swap_wait_" + stage)
        sums, own = _chip_sums(full, got, self.where, "chip_sums_" + stage)
        lands = [lax.empty((3,) + s.shape[1:], BF16) for s in sums]
        sems, sent, lands, token = _exchange_start(sums, lands, _owners_plan, 3 * len(sums), "owners_start_" + stage)
        self.state[stage] = (names, own, sems, sent, lands)
        return token

    def finish(self, stage, after):
        names, own, sems, sent, lands = self.state[stage]
        _, got = _exchange_wait(sems, sent, lands, _owners_plan, after, "owners_wait_" + stage)
        return dict(zip(names, _owner_sums(own, got, "owner_sums_" + stage)))


def _chip_sums(gs, gots, where, name):
    n = len(gs)

    def body(where_ref, *refs):
        g_refs, got_refs, hb_refs, own_refs = (refs[k * n:(k + 1) * n] for k in range(4))
        mine = pl.program_id(0) == where_ref[1]
        for g_ref, got_ref, hb_ref, own_ref in zip(g_refs, got_refs, hb_refs, own_refs):
            h = g_ref[...].astype(F32) + got_ref[...].astype(F32)
            hb_ref[...] = h.astype(BF16)

            @pl.when(mine)
            def _():
                own_ref[...] = h

    halves = [(g.shape[1] // 2, g.shape[2]) for g in gs]
    slot = [pl.BlockSpec((None, rh, cc), lambda s, where: (s, 0, 0)) for rh, cc in halves]
    outs = pl.pallas_call(
        body, name=name,
        grid_spec=pltpu.PrefetchScalarGridSpec(
            num_scalar_prefetch=1, grid=(NSH,),
            in_specs=[pl.BlockSpec((None, rh, cc), lambda s, where: (s, where[0], 0)) for rh, cc in halves] + slot,
            out_specs=slot + [pl.BlockSpec((rh, cc), lambda s, where: (0, 0)) for rh, cc in halves]),
        out_shape=[jax.ShapeDtypeStruct((NSH, rh, cc), BF16) for rh, cc in halves]
        + [jax.ShapeDtypeStruct((rh, cc), F32) for rh, cc in halves],
        compiler_params=_params("arbitrary"),
    )(where, *gs, *gots)
    return list(outs[:n]), list(outs[n:])


def _owner_sums(owns, gots, name):
    n = len(owns)

    def body(*refs):
        own_refs, got_refs, o_refs = (refs[k * n:(k + 1) * n] for k in range(3))
        for own_ref, got_ref, o_ref in zip(own_refs, got_refs, o_refs):
            o_ref[...] = ((own_ref[...] + got_ref[0].astype(F32)) + got_ref[1].astype(F32)) + got_ref[2].astype(F32)

    blocks = [(o.shape[0] // 2, o.shape[1]) for o in owns]
    rows = [pl.BlockSpec(b, lambda i: (i, 0)) for b in blocks]
    return pl.pallas_call(
        body, grid=(2,), name=name,
        in_specs=rows + [pl.BlockSpec((3,) + b, lambda i: (0, i, 0)) for b in blocks], out_specs=rows,
        out_shape=[jax.ShapeDtypeStruct(o.shape, F32) for o in owns], compiler_params=_params("arbitrary"),
    )(*owns, *gots)


def _sibling_plan(srcs, lands):
    x, y, c = _place()
    return [(src, land, (x, y, 1 - c)) for src, land in zip(srcs, lands)]


def _all_reduce_small(part):
    def body(p_ref, o_ref, rbuf, send1, recv1, send2, recv2):
        x, y, c = _place()
        me = 4 * x + 2 * y + c
        peers = []
        for k in range(1, 8):
            px, py, pc = x ^ ((k >> 2) & 1), y ^ ((k >> 1) & 1), c ^ (k & 1)
            peers.append((k, (px, py, pc), 4 * px + 2 * py + pc))

        def rows(d):
            return pl.ds(pl.multiple_of(d * SMALL_SLICE, 8), SMALL_SLICE)

        first = [pltpu.make_async_remote_copy(src_ref=p_ref.at[rows(idx), :], dst_ref=rbuf.at[me], send_sem=send1.at[k],
                                              recv_sem=recv1.at[k], device_id=dev, device_id_type=MESH)
                 for k, dev, idx in peers]
        for cp in first:
            cp.start()
        rbuf[me] = p_ref[rows(me), :]
        for k, dev, idx in peers:
            pltpu.make_async_remote_copy(src_ref=p_ref.at[rows(idx), :], dst_ref=rbuf.at[idx], send_sem=send1.at[k],
                                         recv_sem=recv1.at[k], device_id=dev, device_id_type=MESH).wait_recv()
        acc = rbuf[0]
        for d in range(1, 8):
            acc = acc + rbuf[d]
        o_ref[rows(me), :] = acc
        second = [pltpu.make_async_remote_copy(src_ref=o_ref.at[rows(me), :], dst_ref=o_ref.at[rows(me), :],
                                               send_sem=send2.at[k], recv_sem=recv2.at[k], device_id=dev, device_id_type=MESH)
                  for k, dev, idx in peers]
        for cp in second:
            cp.start()
        for k, dev, idx in peers:
            pltpu.make_async_remote_copy(src_ref=o_ref.at[rows(me), :], dst_ref=o_ref.at[rows(idx), :], send_sem=send2.at[k],
                                         recv_sem=recv2.at[k], device_id=dev, device_id_type=MESH).wait_recv()
        for cp in first + second:
            cp.wait_send()

    return pl.pallas_call(
        body, name="all_reduce_small", in_specs=[_WHOLE], out_specs=_WHOLE,
        out_shape=jax.ShapeDtypeStruct((SMALL_ROWS, 128), F32),
        scratch_shapes=[pltpu.VMEM((8, SMALL_SLICE, 128), F32)] + [pltpu.SemaphoreType.DMA((8,))] * 4,
        compiler_params=pltpu.CompilerParams(has_side_effects=True),
    )(part)


def _adamw_update(w, gv, m, v):
    nm = ADAM_B1 * m + (1.0 - ADAM_B1) * gv
    nv = ADAM_B2 * v + (1.0 - ADAM_B2) * (gv * gv)
    m_hat = nm / (1.0 - ADAM_B1 ** ADAM_STEP)
    v_hat = nv / (1.0 - ADAM_B2 ** ADAM_STEP)
    return -ADAM_LR * (m_hat / (jnp.sqrt(v_hat) + ADAM_EPS) + ADAM_WD * w), nm, nv


def _adamw_small(ws, gs, ms, vs):
    n = len(ws)

    def body(*refs):
        w_refs, g_refs, m_refs, v_refs, d_refs, nm_refs, nv_refs = (refs[k * n:(k + 1) * n] for k in range(7))
        for i in range(n):
            d_refs[i][...], nm_refs[i][...], nv_refs[i][...] = _adamw_update(
                w_refs[i][...], g_refs[i][...], m_refs[i][...], v_refs[i][...])

    out = [jax.ShapeDtypeStruct(w.shape, F32) for w in ws]
    outs = pl.pallas_call(body, in_specs=[_WHOLE] * (4 * n), out_specs=[_WHOLE] * (3 * n), out_shape=out * 3,
                          name="adamw_small", compiler_params=_params())(*ws, *gs, *ms, *vs)
    return outs[:n], outs[n:2 * n], outs[2 * n:]


def _adamw_halves(ws, mines, theirs, ms, vs, name):
    n = len(ws)
    steps = 2

    def body(*refs):
        w_refs, mine_refs, theirs_refs, m_refs, v_refs, g_refs, d_refs, nm_refs, nv_refs = (
            refs[k * n:(k + 1) * n] for k in range(9))
        is_mine = pl.program_id(0) == lax.axis_index("c")
        for i in range(n):
            gv = jnp.where(is_mine, mine_refs[i][...], theirs_refs[i][...])
            g_refs[i][...] = gv
            d_refs[i][...], nm_refs[i][...], nv_refs[i][...] = _adamw_update(w_refs[i][...], gv, m_refs[i][...], v_refs[i][...])

    blocks = [(h.shape[0] // steps, h.shape[1]) for h in mines]
    whole = [pl.BlockSpec(b, lambda h, i: (steps * h + i, 0)) for b in blocks]
    half = [pl.BlockSpec(b, lambda h, i: (i, 0)) for b in blocks]
    out = [jax.ShapeDtypeStruct(w.shape, F32) for w in ws]
    outs = pl.pallas_call(body, grid=(2, steps), in_specs=whole + half + half + whole + whole, out_specs=whole * 4,
                          out_shape=out * 4, name=name, compiler_params=_params("arbitrary", "arbitrary"),
                          )(*ws, *mines, *theirs, *ms, *vs)
    return [tuple(outs[k * n + i] for k in range(4)) for i in range(n)]


SMALL_USED = sum(size for _, size in SMALL) // 128


def _pack_small(vals, tail=None):
    parts = []
    for name, size in SMALL:
        flat = vals[name].reshape(-1).astype(F32)
        parts.append(jnp.pad(flat, (0, size - flat.shape[0])))
    if tail is not None:
        parts.append(tail.reshape(128))
    flat = jnp.concatenate(parts)
    return jnp.pad(flat, (0, SMALL_ROWS * 128 - flat.shape[0])).reshape(SMALL_ROWS, 128)


def _unpack_small(packed, shapes):
    flat = packed.reshape(-1)
    out, off = {}, 0
    for name, size in SMALL:
        n = math.prod(shapes[name])
        out[name] = flat[off:off + n].reshape(shapes[name])
        off += size
    return out


def kernel(x, ffn1_pre_g, ffn1_w1, ffn1_w3, ffn1_w2, ffn1_post_g, mix_pre_g, w_in, conv_w, conv_b, rg_a_w, rg_a_b, rg_x_w, rg_x_b, lru_lambda, w_lru_out, attn_sinks, rel_bias, w_attn_out, w_gate, b_gate, w_o, mix_post_g, ffn2_pre_g, ffn2_w1, ffn2_w3, ffn2_w2, ffn2_post_g, loss_target, m_ffn1_pre_g, m_ffn1_w1, m_ffn1_w3, m_ffn1_w2, m_ffn1_post_g, m_mix_pre_g, m_w_in, m_conv_w, m_conv_b, m_rg_a_w, m_rg_a_b, m_rg_x_w, m_rg_x_b, m_lru_lambda, m_w_lru_out, m_attn_sinks, m_rel_bias, m_w_attn_out, m_w_gate, m_b_gate, m_w_o, m_mix_post_g, m_ffn2_pre_g, m_ffn2_w1, m_ffn2_w3, m_ffn2_w2, m_ffn2_post_g, v_ffn1_pre_g, v_ffn1_w1, v_ffn1_w3, v_ffn1_w2, v_ffn1_post_g, v_mix_pre_g, v_w_in, v_conv_w, v_conv_b, v_rg_a_w, v_rg_a_b, v_rg_x_w, v_rg_x_b, v_lru_lambda, v_w_lru_out, v_attn_sinks, v_rel_bias, v_w_attn_out, v_w_gate, v_b_gate, v_w_o, v_mix_post_g, v_ffn2_pre_g, v_ffn2_w1, v_ffn2_w3, v_ffn2_w2, v_ffn2_post_g):
    given = dict(locals())
    chip = 2 * lax.axis_index("x") + lax.axis_index("y")
    transposed = ("ffn1_w1", "ffn1_w3", "ffn2_w1", "ffn2_w3")

    def shard(name, moment=""):
        w = given[moment + name][0]
        return w.T if name in transposed else w

    def unshard(name, w):
        return (w.T if name in transposed else w)[None]

    def only_my_columns(a):
        parts = a.reshape(1, 4, NSH, D // NSH)
        return sum(jnp.where(chip == s, parts[:, :, s], 0.0) for s in range(NSH))

    chip_arr = jnp.reshape(chip, (1,)).astype(jnp.int32)
    stage_names = {"ffn1_up": ["ffn1_w1", "ffn1_w3", "conv_w"],
                   "ffn1_down": ["ffn1_w2"],
                   "mix_in": ["w_in", "w_gate"],
                   "mix_out": ["w_lru_out", "w_attn_out", "w_o"],
                   "ffn2": ["ffn2_w1", "ffn2_w3", "ffn2_w2"]}
    in_flight, started = {}, None
    for stage, names in stage_names.items():
        bufs = [jnp.where(lax.broadcasted_iota(jnp.int32, (NSH, 4, D // NSH), 0) == chip, given[n], 0.0) if n == "conv_w"
                else _cast_into_slot(shard(n), chip_arr, "cast_" + n, started) for n in names]
        (in_flight[stage],), started = _gather_start([bufs], "gather_start_" + stage)
    all_started = started

    filling = {}

    def weights(stage, after, begin=False):
        names = stage_names[stage]
        halves_of = [n for n in names if n != "conv_w"]
        if stage in filling:
            filled, _ = _exchange_wait(filling.pop(stage), *filling.pop(stage + "/bufs"), _fill_plan, after,
                                       "fill_wait_" + stage)
            return dict(zip(halves_of, filled))
        send_sems, recv_sems, landing = in_flight[stage]
        if stage == "ffn1_up":
            after = [all_started] + list(after)
        landed = dict(zip(names, _gather_wait(send_sems, recv_sems, landing, after, "gather_wait_" + stage)))
        halves = [landed[n] for n in halves_of]
        if begin:
            filling[stage], bufs, _, token = _exchange_start(halves, [], _fill_plan, 3 * len(halves), "fill_start_" + stage)
            filling[stage + "/bufs"] = (bufs, [])
            return token
        out = dict(zip(halves_of, _sibling_fill(halves, "sibling_fill_" + stage)))
        if "conv_w" in names:
            out["conv_w"] = jnp.transpose(landed["conv_w"], (1, 0, 2)).reshape(4, D)
        return out

    small_shapes = {n: given[n].shape for n, _ in SMALL}
    small_shapes["conv_w"] = (1, 4, D)
    sm = {n: (given[n][0] if given[n].shape[0] == 1 and n != "rel_bias" else given[n]) for n, _ in SMALL if n != "conv_w"}

    reducer = _Reducer(jnp.stack([lax.axis_index("c"), chip]).astype(jnp.int32))
    sq, dx, _, small = _local_step(x[0], loss_target[0], weights, sm, reducer)

    reduced_small = _all_reduce_small(_pack_small(small, tail=sq))
    last_started = reducer.advance("ffn1", [dx, reduced_small])
    loss = reduced_small[SMALL_USED, 0] * (0.5 / D)
    small_g = _unpack_small(reduced_small, small_shapes)
    grads, delta, new_m, new_v = {}, {}, {}, {}
    in_transit = {}

    def send(stage, after):
        halves = reducer.finish(stage, after)
        lands = [lax.empty(h.shape, F32) for h in halves.values()]
        sems, mine, lands, token = _exchange_start(list(halves.values()), lands, _sibling_plan, len(lands),
                                                   "halves_start_" + stage)
        in_transit[stage] = (list(halves), sems, mine, lands)
        return token

    def update(stage, after):
        names, sems, mine, lands = in_transit[stage]
        mine, theirs = _exchange_wait(sems, mine, lands, _sibling_plan, after, "halves_wait_" + stage)
        updated = _adamw_halves([shard(n) for n in names], mine, theirs, [shard(n, "m_") for n in names],
                                [shard(n, "v_") for n in names], "adamw_" + stage)
        for n, results in zip(names, updated):
            grads[n], delta[n], new_m[n], new_v[n] = (unshard(n, r) for r in results)
        return new_v[names[-1]]

    token = send("ffn2", [reduced_small, last_started])
    token = send("mix", token)
    done = update("ffn2", token)
    done = update("mix", done)
    token = send("ffn1", done)
    update("ffn1", token)

    small_g["conv_w"] = only_my_columns(small_g["conv_w"])
    names = [n for n, _ in SMALL]
    flat2d = lambda a: a.reshape(-1, a.shape[-1])
    outs = _adamw_small(*[[flat2d(given[pre + n]) if pre != "g" else flat2d(small_g[n]) for n in names]
                          for pre in ("", "g", "m_", "v_")])
    for dst, arrs in zip((delta, new_m, new_v), outs):
        dst.update({n: a.reshape(given[n].shape) for n, a in zip(names, arrs)})
    grads.update(small_g)
    return (loss, dx[None], *[grads[n] for n in WEIGHTS], *[delta[n] for n in WEIGHTS], *[new_m[n] for n in WEIGHTS],
            *[new_v[n] for n in WEIGHTS])
```

```python
import functools
import math

import jax
import jax.numpy as jnp
from jax import lax
from jax.experimental import pallas as pl
from jax.experimental.pallas import tpu as pltpu

F32, BF16 = jnp.float32, jnp.bfloat16
D = 1024
NSH = 4
FF_S = 704
IN_S = 896
GATE_S = 512
KV_W = 256
CHUNK = 64
KB = 192
N_HEADS = 16
HEAD_DIM = 64
N_BUCKETS = 32
KP = 192
PAD_KEYS = 128
RMS_EPS = 1e-6
NEG_INF = -1e30
LRU_C = 8.0
TM = 512
TM_SCAN = 256
VMEM_LIMIT = 56 * 1024 * 1024
ADAM_LR, ADAM_B1, ADAM_B2, ADAM_EPS, ADAM_WD, ADAM_STEP = 0.001, 0.9, 0.999, 1e-08, 0.01, 10
SMALL_ROWS = 1216
SMALL_SLICE = SMALL_ROWS // 8
MESH = pl.DeviceIdType.MESH

BIG = ["ffn1_w1", "ffn1_w3", "ffn1_w2", "w_in", "w_lru_out", "w_attn_out", "w_gate", "w_o", "ffn2_w1", "ffn2_w3", "ffn2_w2"]
SMALL = [("ffn1_pre_g", 1024), ("ffn1_post_g", 1024), ("mix_pre_g", 1024), ("conv_w", 4096), ("conv_b", 1024),
         ("rg_a_w", 65536), ("rg_a_b", 1024), ("rg_x_w", 65536), ("rg_x_b", 1024), ("lru_lambda", 1024),
         ("attn_sinks", 1024), ("rel_bias", 1024), ("b_gate", 2048), ("mix_post_g", 1024), ("ffn2_pre_g", 1024),
         ("ffn2_post_g", 1024)]
WEIGHTS = ["ffn1_pre_g", "ffn1_w1", "ffn1_w3", "ffn1_w2", "ffn1_post_g", "mix_pre_g", "w_in", "conv_w", "conv_b", "rg_a_w",
           "rg_a_b", "rg_x_w", "rg_x_b", "lru_lambda", "w_lru_out", "attn_sinks", "rel_bias", "w_attn_out", "w_gate", "b_gate",
           "w_o", "mix_post_g", "ffn2_pre_g", "ffn2_w1", "ffn2_w3", "ffn2_w2", "ffn2_post_g"]


def _params(*sem):
    return pltpu.CompilerParams(dimension_semantics=sem or None, vmem_limit_bytes=VMEM_LIMIT)


def _nn(a, b):
    return jnp.dot(a, b, preferred_element_type=F32)


def _nt(a, b):
    return lax.dot_general(a, b, (((1,), (1,)), ((), ())), preferred_element_type=F32)


def _tn(a, b):
    return lax.dot_general(a, b, (((0,), (0,)), ((), ())), preferred_element_type=F32)


def _rms(x, g):
    rstd = lax.rsqrt(jnp.mean(x * x, axis=-1, keepdims=True) + RMS_EPS)
    return (x * rstd) * g


def _rms_bwd(dout, x, g):
    rstd = lax.rsqrt(jnp.mean(x * x, axis=-1, keepdims=True) + RMS_EPS)
    xhat = x * rstd
    dg = jnp.sum(dout * xhat, axis=0, keepdims=True)
    dxhat = dout * g
    dx = rstd * (dxhat - xhat * jnp.mean(dxhat * xhat, axis=-1, keepdims=True))
    return dx, dg


_GELU_K = math.sqrt(2.0 / math.pi)


_GELU_C = 0.044715 * _GELU_K


def _gelu(x):
    return x * (0.5 + 0.5 * jnp.tanh(x * (_GELU_K + _GELU_C * (x * x))))


def _gelu_and_grad(x):
    x2 = x * x
    t = jnp.tanh(x * (_GELU_K + _GELU_C * x2))
    cdf = 0.5 + 0.5 * t
    return x * cdf, cdf + (x * (_GELU_K + (3.0 * _GELU_C) * x2)) * (0.5 - 0.5 * (t * t))


def _softplus_neg(lam):
    z = -lam
    u = jnp.exp(-jnp.abs(z))
    w = 1.0 + u
    log1p_u = jnp.where(w == 1.0, u, jnp.log(w) * (u / (w - 1.0)))
    return jnp.maximum(z, 0.0) + log1p_u


def _lru_coeffs(r, sp):
    log_a = (-LRU_C * r) * sp
    a = jnp.exp(log_a)
    t = jnp.tanh(log_a)
    s = jnp.sqrt(-2.0 * t / (1.0 - t))
    return a, s


def _row_spec(tm, width):
    return pl.BlockSpec((tm, width), lambda i: (i, 0))


def _vec_spec(width):
    return pl.BlockSpec((1, width), lambda i: (0, 0))


_WHOLE = pl.BlockSpec(memory_space=pltpu.VMEM)


def _tile(t, tm=TM):
    return min(tm, t)


def _ffn_fwd(x, gpre, w1g, w3g, w2g, gpost, name, target=None):
    t = x.shape[0]
    tm = _tile(t)
    last = target is not None

    def body(x_ref, gpre_ref, w1_ref, w3_ref, w2_ref, gpost_ref, *refs):
        t_ref, (h_ref, a_ref, b_ref, hm_ref, f_ref), l_ref = (refs[0] if last else None), refs[last:last + 5], refs[-1]
        xv = x_ref[...]
        nb = _rms(xv, gpre_ref[...]).astype(BF16)
        f = jnp.zeros((tm, D), F32)
        for s in range(NSH):
            a = _nt(nb, w1_ref[s])
            b = _nt(nb, w3_ref[s])
            hmb = ((a * jax.nn.sigmoid(a)) * b).astype(BF16)
            a_ref[s] = a.astype(BF16)
            b_ref[s] = b.astype(BF16)
            hm_ref[s] = hmb
            f = f + _nn(hmb, w2_ref[s])
        f_ref[...] = f
        h = xv + 0.5 * _rms(f, gpost_ref[...])
        if last:
            @pl.when(pl.program_id(0) == 0)
            def _():
                l_ref[...] = jnp.zeros_like(l_ref)

            e = h - t_ref[...]
            h_ref[...] = e * (1.0 / D)
            l_ref[...] += jnp.sum(jnp.sum(e * e, axis=0, keepdims=True), axis=1, keepdims=True)
        else:
            h_ref[...] = h

    sh = pl.BlockSpec((NSH, tm, FF_S), lambda i: (0, i, 0))
    act = jax.ShapeDtypeStruct((NSH, t, FF_S), BF16)
    return pl.pallas_call(
        body, grid=(t // tm,), name=name,
        in_specs=[_row_spec(tm, D), _vec_spec(D), _WHOLE, _WHOLE, _WHOLE, _vec_spec(D)] + [_row_spec(tm, D)] * last,
        out_specs=[_row_spec(tm, D), sh, sh, sh, _row_spec(tm, D)] + [pl.BlockSpec((1, 128), lambda i: (0, 0))] * last,
        out_shape=[jax.ShapeDtypeStruct((t, D), F32), act, act, act, jax.ShapeDtypeStruct((t, D), F32)]
        + [jax.ShapeDtypeStruct((1, 128), F32)] * last,
        compiler_params=_params("arbitrary"),
    )(x, gpre, w1g, w3g, w2g, gpost, *([target] if last else []))


def _ffn_up(x, gpre, w1g, w3g, name):
    t = x.shape[0]
    tm = _tile(t)

    def body(x_ref, gpre_ref, w1_ref, w3_ref, a_ref, b_ref, hm_ref):
        nb = _rms(x_ref[...], gpre_ref[...]).astype(BF16)
        for s in range(NSH):
            a = _nt(nb, w1_ref[s])
            b = _nt(nb, w3_ref[s])
            a_ref[s] = a.astype(BF16)
            b_ref[s] = b.astype(BF16)
            hm_ref[s] = ((a * jax.nn.sigmoid(a)) * b).astype(BF16)

    sh = pl.BlockSpec((NSH, tm, FF_S), lambda i: (0, i, 0))
    act = jax.ShapeDtypeStruct((NSH, t, FF_S), BF16)
    return pl.pallas_call(
        body, grid=(t // tm,), name=name, in_specs=[_row_spec(tm, D), _vec_spec(D), _WHOLE, _WHOLE],
        out_specs=[sh, sh, sh], out_shape=[act, act, act], compiler_params=_params("arbitrary"),
    )(x, gpre, w1g, w3g)


def _ffn_down(x, hm, w2g, gpost, name):
    t = x.shape[0]
    tm = _tile(t)

    def body(x_ref, hm_ref, w2_ref, gpost_ref, h_ref, f_ref):
        f = jnp.zeros((tm, D), F32)
        for s in range(NSH):
            f = f + _nn(hm_ref[s], w2_ref[s])
        f_ref[...] = f
        h_ref[...] = x_ref[...] + 0.5 * _rms(f, gpost_ref[...])

    sh = pl.BlockSpec((NSH, tm, FF_S), lambda i: (0, i, 0))
    f32 = jax.ShapeDtypeStruct((t, D), F32)
    return pl.pallas_call(
        body, grid=(t // tm,), name=name, in_specs=[_row_spec(tm, D), sh, _WHOLE, _vec_spec(D)],
        out_specs=[_row_spec(tm, D), _row_spec(tm, D)], out_shape=[f32, f32], compiler_params=_params("arbitrary"),
    )(x, hm, w2g, gpost)


def _mix_proj(h1, gmix, w_in_g, w_gate_g, b_gate):
    t = h1.shape[0]
    tm = _tile(t)

    def body(h_ref, g_ref, win_ref, wg_ref, bg_ref, u_ref, q_ref, k_ref, v_ref, xr_ref, xg_ref, gate_ref):
        ub = _rms(h_ref[...], g_ref[...]).astype(BF16)
        u_ref[...] = ub
        p0 = _nn(ub, win_ref[0])
        q_ref[:, 0:896] = p0.astype(BF16)
        p1 = _nn(ub, win_ref[1])
        q_ref[:, 896:1024] = p1[:, 0:128].astype(BF16)
        k_ref[...] = p1[:, 128:384].astype(BF16)
        v_ref[...] = p1[:, 384:640].astype(BF16)
        xr_ref[:, 0:256] = p1[:, 640:896]
        p2 = _nn(ub, win_ref[2])
        xr_ref[:, 256:1024] = p2[:, 0:768]
        xg_ref[:, 0:128] = p2[:, 768:896]
        xg_ref[:, 128:1024] = _nn(ub, win_ref[3])
        for s in range(NSH):
            sl = slice(s * GATE_S, (s + 1) * GATE_S)
            gate_ref[:, sl] = jax.nn.sigmoid(_nn(ub, wg_ref[s]) + bg_ref[:, sl])

    return pl.pallas_call(
        body, grid=(t // tm,), name="mix_proj",
        in_specs=[_row_spec(tm, D), _vec_spec(D), _WHOLE, _WHOLE, _vec_spec(2 * D)],
        out_specs=[_row_spec(tm, D), _row_spec(tm, D), _row_spec(tm, KV_W), _row_spec(tm, KV_W), _row_spec(tm, D),
                   _row_spec(tm, D), _row_spec(tm, 2 * D)],
        out_shape=[jax.ShapeDtypeStruct((t, D), BF16), jax.ShapeDtypeStruct((t, D), BF16),
                   jax.ShapeDtypeStruct((t, KV_W), BF16), jax.ShapeDtypeStruct((t, KV_W), BF16),
                   jax.ShapeDtypeStruct((t, D), F32), jax.ShapeDtypeStruct((t, D), F32),
                   jax.ShapeDtypeStruct((t, 2 * D), F32)],
        compiler_params=_params("arbitrary"),
    )(h1, gmix, w_in_g, w_gate_g, b_gate)


def _rglru_fwd(xr, xg, conv_w, conv_b, wa2, ba, wx2, bx, lam, after=None):
    t = xr.shape[0]
    tm = _tile(t, TM_SCAN)
    nb8 = tm // 8

    def body(xr_ref, xrp_ref, xg_ref, cw_ref, cb_ref, wa_ref, ba_ref, wx_ref, bx_ref, lam_ref,
             hr_ref, yain_ref, xc_ref, r_ref, ig_ref, a_sc, s_ref, ext, h_sc):
        i = pl.program_id(0)

        @pl.when(i == 0)
        def _():
            h_sc[...] = jnp.zeros_like(h_sc)

        ext[0:8, :] = jnp.where(i == 0, 0.0, xrp_ref[...])
        ext[8:8 + tm, :] = xr_ref[...]
        xc = jnp.broadcast_to(cb_ref[...], (tm, D))
        for tap in range(4):
            xc = xc + ext[pl.ds(5 + tap, tm), :] * cw_ref[tap:tap + 1, :]
        xc_ref[...] = xc
        xcb = xc.astype(BF16)
        for p in range(8):
            sl = slice(p * 128, (p + 1) * 128)
            r_ref[:, sl] = jax.nn.sigmoid(_nn(xcb[:, sl], wa_ref[p]) + ba_ref[:, sl])
            ig_ref[:, sl] = jax.nn.sigmoid(_nn(xcb[:, sl], wx_ref[p]) + bx_ref[:, sl])
        a, s = _lru_coeffs(r_ref[...], _softplus_neg(lam_ref[...]))
        a_sc[...] = a
        s_ref[...] = s
        hr_ref[...] = s * (ig_ref[...] * xc)

        def blk(j, h):
            st = pl.multiple_of(j * 8, 8)
            a8 = a_sc[pl.ds(st, 8), :]
            u8 = hr_ref[pl.ds(st, 8), :]
            rows = []
            for k in range(8):
                h = a8[k:k + 1, :] * h + u8[k:k + 1, :]
                rows.append(h)
            hr_ref[pl.ds(st, 8), :] = jnp.concatenate(rows, axis=0)
            return h

        h_sc[0:1, :] = lax.fori_loop(0, nb8, blk, h_sc[0:1, :])
        yain_ref[...] = (hr_ref[...] * _gelu(xg_ref[...])).astype(BF16)

    prev = pl.BlockSpec((8, D), lambda i: (jnp.maximum(i * nb8 - 1, 0), 0))
    full = lambda shape: pl.BlockSpec(shape, lambda i: tuple(0 for _ in shape))
    f32 = jax.ShapeDtypeStruct((t, D), F32)
    body, specs, operands = _behind(body, after)
    return pl.pallas_call(
        body, grid=(t // tm,), name="rglru_fwd",
        in_specs=specs + [_row_spec(tm, D), prev, _row_spec(tm, D), full((4, D)), _vec_spec(D), full((8, 128, 128)),
                          _vec_spec(D), full((8, 128, 128)), _vec_spec(D), _vec_spec(D)],
        out_specs=[_row_spec(tm, D)] * 7,
        out_shape=[f32, jax.ShapeDtypeStruct((t, D), BF16), f32, f32, f32, f32, f32],
        scratch_shapes=[pltpu.VMEM((tm + 8, D), F32), pltpu.VMEM((8, D), F32)],
        compiler_params=_params("arbitrary"),
    )(*operands, xr, xr, xg, conv_w, conv_b, wa2, ba, wx2, bx, lam)


def _bias_fwd(table_t, onehot_t):
    def body(t_ref, e_ref, o_ref):
        o_ref[...] = jnp.dot(t_ref[...], e_ref[...], preferred_element_type=F32, precision=lax.Precision.HIGHEST)

    return pl.pallas_call(body, out_shape=jax.ShapeDtypeStruct((N_HEADS, CHUNK * KB), F32), name="bias_fwd",
                          compiler_params=_params())(table_t, onehot_t)


def _bias_bwd(dbias_flat, onehot_t, ds_rows):
    def body(d_ref, e_ref, s_ref, o_ref, so_ref):
        o_ref[...] = lax.dot_general(d_ref[...], e_ref[...], (((1,), (1,)), ((), ())), preferred_element_type=F32,
                                     precision=lax.Precision.HIGHEST)
        so_ref[...] = jnp.zeros_like(so_ref)
        for r in range(4):
            so_ref[:, r:r + 1] = jnp.sum(s_ref[:, r * CHUNK:(r + 1) * CHUNK], axis=1, keepdims=True)

    return pl.pallas_call(body, out_shape=[jax.ShapeDtypeStruct((N_HEADS, N_BUCKETS), F32), jax.ShapeDtypeStruct((8, 128), F32)],
                          name="bias_bwd", compiler_params=_params())(dbias_flat, onehot_t, ds_rows)


def _stack_heads(q):
    return jnp.concatenate(
        [jnp.concatenate([q[:, (4 * g + r) * HEAD_DIM:(4 * g + r + 1) * HEAD_DIM] for g in range(4)], axis=1)
         for r in range(4)], axis=0)


def _unstack_heads(o):
    return jnp.concatenate([o[r * CHUNK:(r + 1) * CHUNK, g * HEAD_DIM:(g + 1) * HEAD_DIM] for g in range(4) for r in range(4)],
                           axis=1)


def _block_diag(w, mask):
    return jnp.concatenate([w] * 4, axis=0) * mask


def _group_softmax(qk, bias_g, sink, valid):
    s = qk * (HEAD_DIM ** -0.5) + bias_g
    s = jnp.where(valid, s, NEG_INF)
    m = jnp.maximum(jnp.max(s, axis=0, keepdims=True), sink)
    e = jnp.exp(s - m)
    es = jnp.exp(sink - m)
    inv = 1.0 / (jnp.sum(e, axis=0, keepdims=True) + es)
    return e * inv, es * inv


def _attn_fwd(sink_rows, q, kp, vp, bias_t, mask, after=None):
    t = q.shape[0]
    per_step = 8

    def body(sink_ref, q_ref, kp_ref, vp_ref, bias_ref, mask_ref, o_ref):
        owns = [mask_ref[g * KP:(g + 1) * KP, :] for g in range(4)]
        for k in range(per_step):
            c = pl.program_id(0) * per_step + k
            rows = slice(k * CHUNK, (k + 1) * CHUNK)
            st = pl.multiple_of(c * CHUNK, CHUNK)
            kw = kp_ref[pl.ds(st, KP), :]
            vw = vp_ref[pl.ds(st, KP), :]
            q_all = _stack_heads(q_ref[rows, :])
            valid = lax.broadcasted_iota(jnp.int32, (KP, 1), 0) + c * CHUNK >= PAD_KEYS
            scores = [_nt(kw * owns[g], q_all) for g in range(4)]
            ps = [_group_softmax(scores[g], bias_ref[g * KP:(g + 1) * KP, :], sink_ref[g:g + 1, :], valid)[0]
                  for g in range(4)]
            o_all = sum(_tn(ps[g].astype(BF16), vw * owns[g]) for g in range(4))
            o_ref[rows, :] = _unstack_heads(o_all).astype(BF16)

    body, specs, operands = _behind(body, after)
    return pl.pallas_call(
        body, grid=(t // (per_step * CHUNK),), name="attn_fwd",
        in_specs=specs + [_WHOLE, _row_spec(per_step * CHUNK, D), _WHOLE, _WHOLE, _WHOLE, _WHOLE],
        out_specs=_row_spec(per_step * CHUNK, D),
        out_shape=jax.ShapeDtypeStruct((t, D), BF16),
        compiler_params=_params("arbitrary"),
    )(*operands, sink_rows, q, kp, vp, bias_t, mask)


def _merge_fwd(yain, o, gate, h1, w_lru, w_att, w_o, gpost):
    t = h1.shape[0]
    tm = _tile(t)

    def body(ya_ref, o_ref, g_ref, h_ref, wl_ref, wa_ref, wo_ref, gp_ref, h2_ref, mo_ref, mg_ref, ya_out, yb_out):
        ya = _nn(ya_ref[...], wl_ref[...])
        yb = _nn(o_ref[...], wa_ref[...])
        g0 = g_ref[:, 0:D]
        g1 = g_ref[:, D:2 * D]
        mg = (g0 * ya + g1 * yb).astype(BF16)
        mo = _nn(mg, wo_ref[...])
        ya_out[...] = (ya * (g0 * (1.0 - g0))).astype(BF16)
        yb_out[...] = (yb * (g1 * (1.0 - g1))).astype(BF16)
        mg_ref[...] = mg
        mo_ref[...] = mo
        h2_ref[...] = h_ref[...] + _rms(mo, gp_ref[...])

    f32 = jax.ShapeDtypeStruct((t, D), F32)
    b16 = jax.ShapeDtypeStruct((t, D), BF16)
    return pl.pallas_call(
        body, grid=(t // tm,), name="merge_fwd",
        in_specs=[_row_spec(tm, D), _row_spec(tm, D), _row_spec(tm, 2 * D), _row_spec(tm, D), _WHOLE, _WHOLE, _WHOLE,
                  _vec_spec(D)],
        out_specs=[_row_spec(tm, D)] * 5,
        out_shape=[f32, f32, b16, b16, b16],
        compiler_params=_params("arbitrary"),
    )(yain, o, gate, h1, w_lru, w_att, w_o, gpost)


def _ffn_bwd(dh, x, f, a, b, gpre, gpost, w1g, w3g, w2g, name):
    t = x.shape[0]
    tm = _tile(t, TM_SCAN)

    def body(dh_ref, x_ref, f_ref, a_ref, b_ref, gpre_ref, gpost_ref, w1_ref, w3_ref, w2_ref,
             dx_ref, n_ref, da_ref, db_ref, df_ref, dgpre_ref, dgpost_ref):
        @pl.when(pl.program_id(0) == 0)
        def _():
            dgpre_ref[...] = jnp.zeros_like(dgpre_ref)
            dgpost_ref[...] = jnp.zeros_like(dgpost_ref)

        dhv = dh_ref[...]
        xv = x_ref[...]
        df, dgp = _rms_bwd(0.5 * dhv, f_ref[...], gpost_ref[...])
        dgpost_ref[...] += dgp
        dfb = df.astype(BF16)
        df_ref[...] = dfb
        n_ref[...] = _rms(xv, gpre_ref[...]).astype(BF16)
        dn = jnp.zeros((tm, D), F32)
        for s in range(NSH):
            av = a_ref[s].astype(F32)
            bv = b_ref[s].astype(F32)
            sg = jax.nn.sigmoid(av)
            dhm = _nt(dfb, w2_ref[s])
            dab = (dhm * bv * (sg * (1.0 + av * (1.0 - sg)))).astype(BF16)
            dbb = (dhm * (av * sg)).astype(BF16)
            da_ref[s] = dab
            db_ref[s] = dbb
            dn = dn + _nn(dab, w1_ref[s]) + _nn(dbb, w3_ref[s])
        dxn, dg = _rms_bwd(dn, xv, gpre_ref[...])
        dgpre_ref[...] += dg
        dx_ref[...] = dhv + dxn

    sh = pl.BlockSpec((NSH, tm, FF_S), lambda i: (0, i, 0))
    act = jax.ShapeDtypeStruct((NSH, t, FF_S), BF16)
    vec = jax.ShapeDtypeStruct((1, D), F32)
    return pl.pallas_call(
        body, grid=(t // tm,), name=name,
        in_specs=[_row_spec(tm, D), _row_spec(tm, D), _row_spec(tm, D), sh, sh, _vec_spec(D), _vec_spec(D), _WHOLE, _WHOLE,
                  _WHOLE],
        out_specs=[_row_spec(tm, D), _row_spec(tm, D), sh, sh, _row_spec(tm, D), _vec_spec(D), _vec_spec(D)],
        out_shape=[jax.ShapeDtypeStruct((t, D), F32), jax.ShapeDtypeStruct((t, D), BF16), act, act,
                   jax.ShapeDtypeStruct((t, D), BF16), vec, vec],
        compiler_params=_params("arbitrary"),
    )(dh, x, f, a, b, gpre, gpost, w1g, w3g, w2g)


def _behind(body, after):
    if after is None:
        return body, [], []

    def ordered(_, *refs):
        body(*refs)

    return ordered, [_ANY], [after]


def _ffn_bwd_acts(dh, x, f, a, b, gpre, gpost, w2g, name):
    t = x.shape[0]
    tm = _tile(t)

    def body(dh_ref, x_ref, f_ref, a_ref, b_ref, gpre_ref, gpost_ref, w2_ref, n_ref, da_ref, db_ref, df_ref, dgpost_ref):
        @pl.when(pl.program_id(0) == 0)
        def _():
            dgpost_ref[...] = jnp.zeros_like(dgpost_ref)

        df, dgp = _rms_bwd(0.5 * dh_ref[...], f_ref[...], gpost_ref[...])
        dgpost_ref[...] += dgp
        dfb = df.astype(BF16)
        df_ref[...] = dfb
        n_ref[...] = _rms(x_ref[...], gpre_ref[...]).astype(BF16)
        for s in range(NSH):
            av = a_ref[s].astype(F32)
            bv = b_ref[s].astype(F32)
            sg = jax.nn.sigmoid(av)
            dhm = _nt(dfb, w2_ref[s])
            da_ref[s] = (dhm * bv * (sg * (1.0 + av * (1.0 - sg)))).astype(BF16)
            db_ref[s] = (dhm * (av * sg)).astype(BF16)

    sh = pl.BlockSpec((NSH, tm, FF_S), lambda i: (0, i, 0))
    act = jax.ShapeDtypeStruct((NSH, t, FF_S), BF16)
    b16 = jax.ShapeDtypeStruct((t, D), BF16)
    return pl.pallas_call(
        body, grid=(t // tm,), name=name,
        in_specs=[_row_spec(tm, D), _row_spec(tm, D), _row_spec(tm, D), sh, sh, _vec_spec(D), _vec_spec(D), _WHOLE],
        out_specs=[_row_spec(tm, D), sh, sh, _row_spec(tm, D), _vec_spec(D)],
        out_shape=[b16, act, act, b16, jax.ShapeDtypeStruct((1, D), F32)],
        compiler_params=_params("arbitrary"),
    )(dh, x, f, a, b, gpre, gpost, w2g)


def _ffn_bwd_input(dh, x, da, db, gpre, w1g, w3g, name, after):
    t = x.shape[0]
    tm = _tile(t)

    def body(dh_ref, x_ref, da_ref, db_ref, gpre_ref, w1_ref, w3_ref, dx_ref, dgpre_ref):
        @pl.when(pl.program_id(0) == 0)
        def _():
            dgpre_ref[...] = jnp.zeros_like(dgpre_ref)

        dn = jnp.zeros((tm, D), F32)
        for s in range(NSH):
            dn = dn + _nn(da_ref[s], w1_ref[s]) + _nn(db_ref[s], w3_ref[s])
        dxn, dg = _rms_bwd(dn, x_ref[...], gpre_ref[...])
        dgpre_ref[...] += dg
        dx_ref[...] = dh_ref[...] + dxn

    sh = pl.BlockSpec((NSH, tm, FF_S), lambda i: (0, i, 0))
    body, specs, operands = _behind(body, after)
    return pl.pallas_call(
        body, grid=(t // tm,), name=name,
        in_specs=specs + [_row_spec(tm, D), _row_spec(tm, D), sh, sh, _vec_spec(D), _WHOLE, _WHOLE],
        out_specs=[_row_spec(tm, D), _vec_spec(D)],
        out_shape=[jax.ShapeDtypeStruct((t, D), F32), jax.ShapeDtypeStruct((1, D), F32)],
        compiler_params=_params("arbitrary"),
    )(*operands, dh, x, da, db, gpre, w1g, w3g)


def _wgrad(a, b, a_spec, b_spec, out_spec, out_shape, grid, name, after=None):
    def body(a_ref, b_ref, o_ref):
        o_ref[...] = _tn(a_ref[...], b_ref[...]).astype(BF16)

    body, specs, operands = _behind(body, after)
    return pl.pallas_call(body, grid=grid, name=name, in_specs=specs + [a_spec, b_spec], out_specs=out_spec,
                          out_shape=jax.ShapeDtypeStruct(out_shape, BF16),
                          compiler_params=_params(*("arbitrary",) * len(grid)))(*operands, a, b)


def _wgrad_cols(act, dsh, width, name, after=None):
    t = act.shape[0]
    if dsh.ndim == 3:
        b_spec = pl.BlockSpec((None, t, width), lambda s, k: (s, 0, 0))
    else:
        b_spec = pl.BlockSpec((t, width), lambda s, k: (0, s))
    return _wgrad(act, dsh, pl.BlockSpec((t, 512), lambda s, k: (0, k)), b_spec,
                  pl.BlockSpec((None, 512, width), lambda s, k: (s, k, 0)), (NSH, D, width), (NSH, 2), name, after)


def _wgrad_rows(hm, df, name, after=None):
    t = df.shape[0]
    return _wgrad(hm, df, pl.BlockSpec((None, t, FF_S), lambda s: (s, 0, 0)), pl.BlockSpec((t, D), lambda s: (0, 0)),
                  pl.BlockSpec((None, FF_S, D), lambda s: (s, 0, 0)), (NSH, FF_S, D), (NSH,), name, after)


def _wgrad_sq(a, b, name, after=None):
    t = a.shape[0]
    return _wgrad(a, b, pl.BlockSpec((t, 512), lambda i, j: (0, i)), pl.BlockSpec((t, 512), lambda i, j: (0, j)),
                  pl.BlockSpec((512, 512), lambda i, j: (i, j)), (D, D), (2, 2), name, after)


def _mix_bwd1(dh2, mo, gpost, gate, ya, yb, xg, hr, w_o, w_lru, w_att, after):
    t = dh2.shape[0]
    tm = _tile(t, TM_SCAN)

    def body(dh_ref, mo_ref, gp_ref, g_ref, ya_ref, yb_ref, xg_ref, hr_ref, wo_ref, wl_ref, wa_ref,
             dmo_ref, dya_ref, dyb_ref, dgate_ref, dhr_ref, dxg_ref, do_ref, dgp_ref, dbg_ref):
        @pl.when(pl.program_id(0) == 0)
        def _():
            dgp_ref[...] = jnp.zeros_like(dgp_ref)
            dbg_ref[...] = jnp.zeros_like(dbg_ref)

        dmo, dgp = _rms_bwd(dh_ref[...], mo_ref[...], gp_ref[...])
        dgp_ref[...] += dgp
        dmob = dmo.astype(BF16)
        dmo_ref[...] = dmob
        dm = _nt(dmob, wo_ref[...])
        g0 = g_ref[:, 0:D]
        g1 = g_ref[:, D:2 * D]
        dyab = (dm * g0).astype(BF16)
        dybb = (dm * g1).astype(BF16)
        dya_ref[...] = dyab
        dyb_ref[...] = dybb
        dg0 = dm * ya_ref[...].astype(F32)
        dg1 = dm * yb_ref[...].astype(F32)
        dgate_ref[:, 0:D] = dg0.astype(BF16)
        dgate_ref[:, D:2 * D] = dg1.astype(BF16)
        dbg_ref[:, 0:D] += jnp.sum(dg0, axis=0, keepdims=True)
        dbg_ref[:, D:2 * D] += jnp.sum(dg1, axis=0, keepdims=True)
        dyain = _nt(dyab, wl_ref[...])
        do_ref[...] = _nt(dybb, wa_ref[...]).astype(BF16)
        xgv = xg_ref[...]
        gelu, gelu_grad = _gelu_and_grad(xgv)
        dhr_ref[...] = dyain * gelu
        dxg_ref[...] = (dyain * hr_ref[...] * gelu_grad).astype(BF16)

    b16 = jax.ShapeDtypeStruct((t, D), BF16)
    body, specs, operands = _behind(body, after)
    return pl.pallas_call(
        body, grid=(t // tm,), name="mix_bwd1",
        in_specs=specs + [_row_spec(tm, D), _row_spec(tm, D), _vec_spec(D), _row_spec(tm, 2 * D), _row_spec(tm, D),
                          _row_spec(tm, D), _row_spec(tm, D), _row_spec(tm, D), _WHOLE, _WHOLE, _WHOLE],
        out_specs=[_row_spec(tm, D), _row_spec(tm, D), _row_spec(tm, D), _row_spec(tm, 2 * D), _row_spec(tm, D),
                   _row_spec(tm, D), _row_spec(tm, D), _vec_spec(D), _vec_spec(2 * D)],
        out_shape=[b16, b16, b16, jax.ShapeDtypeStruct((t, 2 * D), BF16), jax.ShapeDtypeStruct((t, D), F32), b16, b16,
                   jax.ShapeDtypeStruct((1, D), F32), jax.ShapeDtypeStruct((1, 2 * D), F32)],
        compiler_params=_params("arbitrary"),
    )(*operands, dh2, mo, gpost, gate, ya, yb, xg, hr, w_o, w_lru, w_att)


def _rglru_bwd(dhr, hr, xc, r, ig, a, s, xr, conv_w, wa2, wx2, lam, after):
    t = dhr.shape[0]
    tm = _tile(t, TM_SCAN)
    nb8 = tm // 8
    nt = t // tm

    def body(dhr_ref, hr_ref, hrp_ref, xc_ref, r_ref, ig_ref, a_sc, s_ref, xr_ref, cw_ref, wa_ref, wx_ref, lam_ref,
             dxr_ref, dwa_ref, dwx_ref, dba_ref, dbx_ref, dlam_ref, dcw_ref, dcb_ref,
             ext_h, ext_d, g_sc, c_sc, nxt_sc):
        i = pl.program_id(0)
        first_tile = i == nt - 1

        @pl.when(i == 0)
        def _():
            c_sc[...] = jnp.zeros_like(c_sc)
            nxt_sc[...] = jnp.zeros_like(nxt_sc)
            for ref in (dwa_ref, dwx_ref, dba_ref, dbx_ref, dlam_ref, dcw_ref, dcb_ref):
                ref[...] = jnp.zeros_like(ref)

        lamv = lam_ref[...]
        sp = _softplus_neg(lamv)
        rv = r_ref[...]
        igv = ig_ref[...]
        xcv = xc_ref[...]
        a = a_sc[...]
        s = s_ref[...]

        def blk(jj, c):
            st = pl.multiple_of((nb8 - 1 - jj) * 8, 8)
            d8 = dhr_ref[pl.ds(st, 8), :]
            a8 = a_sc[pl.ds(st, 8), :]
            rows = [None] * 8
            for k in range(7, -1, -1):
                g = d8[k:k + 1, :] + c
                c = a8[k:k + 1, :] * g
                rows[k] = g
            g_sc[pl.ds(st, 8), :] = jnp.concatenate(rows, axis=0)
            return c

        c_sc[0:1, :] = lax.fori_loop(0, nb8, blk, c_sc[0:1, :])
        g = g_sc[...]
        ext_h[0:8, :] = jnp.where(first_tile, 0.0, hrp_ref[...])
        ext_h[8:8 + tm, :] = hr_ref[...]
        hprev = ext_h[pl.ds(7, tm), :]
        d_s = g * (igv * xcv)
        dig = g * s * xcv
        dxc = g * s * igv
        dla = (g * hprev) * a - d_s * ((a * a) / s)
        dr_pre = (dla * (-LRU_C * sp)) * (rv * (1.0 - rv))
        di_pre = dig * (igv * (1.0 - igv))
        dlam_ref[...] += jnp.sum(dla * (LRU_C * rv), axis=0, keepdims=True) * jax.nn.sigmoid(-lamv)
        dba_ref[...] += jnp.sum(dr_pre, axis=0, keepdims=True)
        dbx_ref[...] += jnp.sum(di_pre, axis=0, keepdims=True)
        drb = dr_pre.astype(BF16)
        dib = di_pre.astype(BF16)
        xcb = xcv.astype(BF16)
        ext_d[tm:tm + 8, :] = nxt_sc[...]
        for p in range(8):
            sl = slice(p * 128, (p + 1) * 128)
            ext_d[0:tm, sl] = dxc[:, sl] + _nt(drb[:, sl], wa_ref[p]) + _nt(dib[:, sl], wx_ref[p])
            dwa_ref[p] += _tn(xcb[:, sl], drb[:, sl])
            dwx_ref[p] += _tn(xcb[:, sl], dib[:, sl])
        dxcv = ext_d[0:tm, :]
        nxt_sc[...] = ext_d[0:8, :]
        dcb_ref[...] += jnp.sum(dxcv, axis=0, keepdims=True)
        xrv = xr_ref[...]
        dxr = jnp.zeros((tm, D), F32)
        for tap in range(4):
            ext_h[0:tm, :] = ext_d[pl.ds(3 - tap, tm), :]
            ahead = ext_h[0:tm, :]
            dxr = dxr + ahead * cw_ref[tap:tap + 1, :]
            dcw_ref[tap:tap + 1, :] += jnp.sum(ahead * xrv, axis=0, keepdims=True)
        dxr_ref[...] = dxr.astype(BF16)

    rev = pl.BlockSpec((tm, D), lambda i: (nt - 1 - i, 0))
    prev = pl.BlockSpec((8, D), lambda i: (jnp.maximum((nt - 1 - i) * nb8 - 1, 0), 0))
    full = lambda shape: pl.BlockSpec(shape, lambda i: tuple(0 for _ in shape))
    vec = jax.ShapeDtypeStruct((1, D), F32)
    blocks = jax.ShapeDtypeStruct((8, 128, 128), F32)
    body, specs, operands = _behind(body, after)
    return pl.pallas_call(
        body, grid=(nt,), name="rglru_bwd",
        in_specs=specs + [rev, rev, prev, rev, rev, rev, rev, rev, rev, full((4, D)), full((8, 128, 128)),
                          full((8, 128, 128)), _vec_spec(D)],
        out_specs=[rev, full((8, 128, 128)), full((8, 128, 128)), _vec_spec(D), _vec_spec(D), _vec_spec(D), full((4, D)),
                   _vec_spec(D)],
        out_shape=[jax.ShapeDtypeStruct((t, D), BF16), blocks, blocks, vec, vec, vec, jax.ShapeDtypeStruct((4, D), F32), vec],
        scratch_shapes=[pltpu.VMEM((tm + 8, D), F32), pltpu.VMEM((tm + 8, D), F32),
                        pltpu.VMEM((tm, D), F32), pltpu.VMEM((8, D), F32), pltpu.VMEM((8, D), F32)],
        compiler_params=_params("arbitrary"),
    )(*operands, dhr, hr, hr, xc, r, ig, a, s, xr, conv_w, wa2, wx2, lam)


def _attn_bwd(sink_rows, q, kp, vp, bias_t, mask, do):
    t = q.shape[0]
    tp = kp.shape[0]
    per_step = 8

    def body(sink_ref, q_ref, kp_ref, vp_ref, bias_ref, mask_ref, do_ref, dq_ref, dk_ref, dv_ref, dbias_ref, ds_ref):
        @pl.when(pl.program_id(0) == 0)
        def _():
            for ref in (dk_ref, dv_ref, dbias_ref, ds_ref):
                ref[...] = jnp.zeros_like(ref)

        maskv = mask_ref[...]
        lane_group = lax.broadcasted_iota(jnp.int32, (1, 4 * HEAD_DIM), 1) // HEAD_DIM

        def own_blocks(full):
            out = full[0:KP]
            for g in range(1, 4):
                out = jnp.where(lane_group == g, full[g * KP:(g + 1) * KP], out)
            return out

        dsc_sum, dsinks, dks, dvs = 0.0, [0.0] * 4, [], []
        for k in range(per_step):
            c = pl.program_id(0) * per_step + k
            chunk = slice(k * CHUNK, (k + 1) * CHUNK)
            st = pl.multiple_of(c * CHUNK, CHUNK)
            kbd = _block_diag(kp_ref[pl.ds(st, KP), :], maskv)
            vbd = _block_diag(vp_ref[pl.ds(st, KP), :], maskv)
            q_all = _stack_heads(q_ref[chunk, :])
            do_all = _stack_heads(do_ref[chunk, :])
            valid = lax.broadcasted_iota(jnp.int32, (KP, 1), 0) + c * CHUNK >= PAD_KEYS
            qk = _nt(kbd, q_all)
            dp = _nt(vbd, do_all)
            ps, dscs = [], []
            for g in range(4):
                rows = slice(g * KP, (g + 1) * KP)
                p, sink_p = _group_softmax(qk[rows], bias_ref[rows, :], sink_ref[g:g + 1, :], valid)
                delta = jnp.sum(p * dp[rows], axis=0, keepdims=True)
                ps.append(p)
                dscs.append(p * (dp[rows] - delta))
                dsinks[g] = dsinks[g] - sink_p * delta
            dsc = jnp.concatenate(dscs, axis=0)
            dsc_sum = dsc_sum + dsc
            dsb = (dsc * (HEAD_DIM ** -0.5)).astype(BF16)
            dq_ref[chunk, :] = _unstack_heads(_tn(dsb, kbd)).astype(BF16)
            dks.append((st, own_blocks(_nn(dsb, q_all))))
            dvs.append((st, own_blocks(_nn(jnp.concatenate(ps, axis=0).astype(BF16), do_all))))
        dbias_ref[...] += dsc_sum
        for g in range(4):
            ds_ref[g:g + 1, :] += dsinks[g]
        for (st, dkw), (_, dvw) in zip(dks, dvs):
            dk_ref[pl.ds(st, KP), :] += dkw
            dv_ref[pl.ds(st, KP), :] += dvw

    full = lambda shape: pl.BlockSpec(shape, lambda i: tuple(0 for _ in shape))
    return pl.pallas_call(
        body, grid=(t // (per_step * CHUNK),), name="attn_bwd",
        in_specs=[_WHOLE, _row_spec(per_step * CHUNK, D), _WHOLE, _WHOLE, _WHOLE, _WHOLE, _row_spec(per_step * CHUNK, D)],
        out_specs=[_row_spec(per_step * CHUNK, D), full((tp, KV_W)), full((tp, KV_W)), full((4 * KP, 4 * CHUNK)),
                   full((8, 4 * CHUNK))],
        out_shape=[jax.ShapeDtypeStruct((t, D), BF16), jax.ShapeDtypeStruct((tp, KV_W), F32),
                   jax.ShapeDtypeStruct((tp, KV_W), F32), jax.ShapeDtypeStruct((4 * KP, 4 * CHUNK), F32),
                   jax.ShapeDtypeStruct((8, 4 * CHUNK), F32)],
        compiler_params=_params("arbitrary"),
    )(sink_rows, q, kp, vp, bias_t, mask, do)


def _mix_bwd2(dproj, dgate, h1, dh2, gmix, w_in_g, w_gate_g, after):
    t = h1.shape[0]
    tm = _tile(t)

    def body(dp_ref, dg_ref, h_ref, dh_ref, g_ref, win_ref, wg_ref, dh1_ref, dgm_ref):
        @pl.when(pl.program_id(0) == 0)
        def _():
            dgm_ref[...] = jnp.zeros_like(dgm_ref)

        du = jnp.zeros((tm, D), F32)
        for s in range(NSH):
            du = du + _nt(dp_ref[:, s * IN_S:(s + 1) * IN_S], win_ref[s])
            du = du + _nt(dg_ref[:, s * GATE_S:(s + 1) * GATE_S], wg_ref[s])
        dxn, dg = _rms_bwd(du, h_ref[...], g_ref[...])
        dgm_ref[...] += dg
        dh1_ref[...] = dh_ref[...] + dxn

    body, specs, operands = _behind(body, after)
    return pl.pallas_call(
        body, grid=(t // tm,), name="mix_bwd2",
        in_specs=specs + [_row_spec(tm, NSH * IN_S), _row_spec(tm, 2 * D), _row_spec(tm, D), _row_spec(tm, D), _vec_spec(D),
                          _WHOLE, _WHOLE],
        out_specs=[_row_spec(tm, D), _vec_spec(D)],
        out_shape=[jax.ShapeDtypeStruct((t, D), F32), jax.ShapeDtypeStruct((1, D), F32)],
        compiler_params=_params("arbitrary"),
    )(*operands, dproj, dgate, h1, dh2, gmix, w_in_g, w_gate_g)


def _band_onehot():
    nb = N_BUCKETS // 2
    max_exact = nb // 2
    rel = jnp.arange(KB)[None, :] - PAD_KEYS - jnp.arange(CHUNK)[:, None]
    ret = jnp.where(rel > 0, nb, 0)
    n = jnp.abs(rel)
    nf = jnp.maximum(n, 1).astype(jnp.float32)
    large = max_exact + (jnp.log(nf / max_exact) / math.log(128 / max_exact) * (nb - max_exact)).astype(jnp.int32)
    large = jnp.minimum(large, nb - 1)
    buckets = (ret + jnp.where(n < max_exact, n, large)).reshape(1, CHUNK * KB)
    return (buckets == jnp.arange(N_BUCKETS)[:, None]).astype(F32)


def _pair_blocks(w):
    pairs = w.reshape(8, 2, 64, 64)
    z = jnp.zeros((8, 64, 64), w.dtype)
    return jnp.concatenate([jnp.concatenate([pairs[:, 0], z], axis=2), jnp.concatenate([z, pairs[:, 1]], axis=2)], axis=1)


def _unpair_blocks(w2):
    return jnp.stack([w2[:, 0:64, 0:64], w2[:, 64:128, 64:128]], axis=1).reshape(16, 64, 64)


def _local_step(x, target, weights, sm, reducer):
    row = lambda v: v.reshape(1, -1)
    onehot_t = _band_onehot()
    bias = _bias_fwd(sm["rel_bias"].T, onehot_t).reshape(4, 4, CHUNK, KB)
    bias_t = jnp.pad(jnp.transpose(bias, (0, 3, 1, 2)), ((0, 0), (0, KP - KB), (0, 0), (0, 0))).reshape(4 * KP, 4 * CHUNK)
    sink_rows = jnp.pad(jnp.repeat(sm["attn_sinks"].reshape(4, 4), CHUNK, axis=1), ((0, 4), (0, 0)))
    grp = jnp.arange(4 * KP)[:, None] // KP == jnp.arange(4 * HEAD_DIM)[None, :] // HEAD_DIM
    mask = (grp & (jnp.arange(4 * KP)[:, None] % KP < KB)).astype(BF16)
    wa2 = _pair_blocks(sm["rg_a_w"]).astype(BF16)
    wx2 = _pair_blocks(sm["rg_x_w"]).astype(BF16)
    wg = dict(weights("ffn1_up", [bias_t, sink_rows, mask, wa2, wx2]))
    sm = dict(sm, conv_w=wg["conv_w"])

    a1, b1, hm1 = _ffn_up(x, row(sm["ffn1_pre_g"]), wg["ffn1_w1"], wg["ffn1_w3"], "ffn1_up")
    wg.update(weights("ffn1_down", hm1))
    h1, f1 = _ffn_down(x, hm1, wg["ffn1_w2"], row(sm["ffn1_post_g"]), "ffn1_down")
    wg.update(weights("mix_in", h1))
    u, q, k, v, xr, xg, gate = _mix_proj(h1, row(sm["mix_pre_g"]), wg["w_in"], wg["w_gate"], row(sm["b_gate"]))
    token = weights("mix_out", u, begin=True)
    hr, yain, xc, r, ig, lru_a, lru_s = _rglru_fwd(xr, xg, sm["conv_w"], row(sm["conv_b"]), wa2, row(sm["rg_a_b"]), wx2,
                                                   row(sm["rg_x_b"]), row(sm["lru_lambda"]), token)
    token = weights("ffn2", hr, begin=True)
    kp = jnp.pad(k, ((PAD_KEYS, KP - KB), (0, 0)))
    vp = jnp.pad(v, ((PAD_KEYS, KP - KB), (0, 0)))
    o = _attn_fwd(sink_rows, q, kp, vp, bias_t, mask, token)
    wg.update(weights("mix_out", o))
    w_lru = wg["w_lru_out"].reshape(D, D)
    w_att = wg["w_attn_out"].reshape(D, D)
    w_o = wg["w_o"].reshape(D, D)
    wg.update(weights("ffn2", o))
    h2, mo, merged, ya, yb = _merge_fwd(yain, o, gate, h1, w_lru, w_att, w_o, row(sm["mix_post_g"]))
    dy, a2, b2, hm2, f2, sq = _ffn_fwd(h2, row(sm["ffn2_pre_g"]), wg["ffn2_w1"], wg["ffn2_w3"], wg["ffn2_w2"],
                                       row(sm["ffn2_post_g"]), "ffn2_fwd", target)

    big, small = {}, {}
    dh2, n2, da2, db2, df2, small["ffn2_pre_g"], small["ffn2_post_g"] = _ffn_bwd(
        dy, h2, f2, a2, b2, row(sm["ffn2_pre_g"]), row(sm["ffn2_post_g"]), wg["ffn2_w1"], wg["ffn2_w3"], wg["ffn2_w2"],
        "ffn2_bwd")
    big["ffn2_w1"] = _wgrad_rows(da2, n2, "dw_ffn2_w1")
    big["ffn2_w3"] = _wgrad_rows(db2, n2, "dw_ffn2_w3")
    big["ffn2_w2"] = _wgrad_rows(hm2, df2, "dw_ffn2_w2")
    token = reducer.begin("ffn2", {n: big[n] for n in ("ffn2_w1", "ffn2_w3", "ffn2_w2")})
    dmo, dya, dyb, dgate, dhr, dxg, do, small["mix_post_g"], small["b_gate"] = _mix_bwd1(
        dh2, mo, row(sm["mix_post_g"]), gate, ya, yb, xg, hr, w_o, w_lru, w_att, token)
    big["w_o"] = _wgrad_sq(merged, dmo, "dw_w_o").reshape(NSH, D // NSH, D)
    big["w_lru_out"] = _wgrad_sq(yain, dya, "dw_w_lru_out").reshape(NSH, D // NSH, D)
    big["w_attn_out"] = _wgrad_sq(o, dyb, "dw_w_attn_out").reshape(NSH, D // NSH, D)
    token = reducer.advance("ffn2", big["w_attn_out"])
    (dxr, dwa2, dwx2, small["rg_a_b"], small["rg_x_b"], small["lru_lambda"], small["conv_w"], small["conv_b"]) = _rglru_bwd(
        dhr, hr, xc, r, ig, lru_a, lru_s, xr, sm["conv_w"], wa2, wx2, row(sm["lru_lambda"]), token)
    small["rg_a_w"] = _unpair_blocks(dwa2)
    small["rg_x_w"] = _unpair_blocks(dwx2)
    dq, dkp, dvp, dbias_t, ds_rows = _attn_bwd(sink_rows, q, kp, vp, bias_t, mask, do)
    dbias = jnp.transpose(dbias_t.reshape(4, KP, 4, CHUNK)[:, :KB], (0, 2, 3, 1)).reshape(N_HEADS, CHUNK * KB)
    drel_t, dsinks = _bias_bwd(dbias, onehot_t, ds_rows)
    small["attn_sinks"] = dsinks[0:4, 0:4].reshape(N_HEADS)
    small["rel_bias"] = drel_t.T
    t = x.shape[0]
    dproj = jnp.concatenate([dq, dkp[PAD_KEYS:PAD_KEYS + t].astype(BF16), dvp[PAD_KEYS:PAD_KEYS + t].astype(BF16), dxr, dxg],
                            axis=1)
    big["w_in"] = _wgrad_cols(u, dproj, IN_S, "dw_w_in")
    big["w_gate"] = _wgrad_cols(u, dgate, GATE_S, "dw_w_gate")
    token = reducer.begin("mix", {n: big[n] for n in ("w_in", "w_gate", "w_lru_out", "w_attn_out", "w_o")})
    dh1, small["mix_pre_g"] = _mix_bwd2(dproj, dgate, h1, dh2, row(sm["mix_pre_g"]), wg["w_in"], wg["w_gate"], token)
    n1, da1, db1, df1, small["ffn1_post_g"] = _ffn_bwd_acts(
        dh1, x, f1, a1, b1, row(sm["ffn1_pre_g"]), row(sm["ffn1_post_g"]), wg["ffn1_w2"], "ffn1_bwd_acts")
    token = reducer.advance("mix", df1)
    big["ffn1_w1"] = _wgrad_rows(da1, n1, "dw_ffn1_w1", token)
    big["ffn1_w3"] = _wgrad_rows(db1, n1, "dw_ffn1_w3", token)
    big["ffn1_w2"] = _wgrad_rows(hm1, df1, "dw_ffn1_w2", token)
    token = reducer.begin("ffn1", {n: big[n] for n in ("ffn1_w1", "ffn1_w3", "ffn1_w2")})
    dx, small["ffn1_pre_g"] = _ffn_bwd_input(dh1, x, da1, db1, row(sm["ffn1_pre_g"]), wg["ffn1_w1"], wg["ffn1_w3"],
                                             "ffn1_bwd_input", token)
    return sq, dx, big, small


_ANY = pl.BlockSpec(memory_space=pl.ANY)


def _place():
    return lax.axis_index("x"), lax.axis_index("y"), lax.axis_index("c")


def _other_chips(x, y):
    return [(1 - x, y), (x, 1 - y), (1 - x, 1 - y)]


_HBM = pl.BlockSpec(memory_space=pltpu.HBM)
_SEM = pl.BlockSpec(memory_space=pltpu.SEMAPHORE)
_EFFECT = pltpu.SideEffectType.DATAFLOW_SIDE_EFFECTING


def _cast_into_slot(w, chip, name, after=None):
    r, cc = w.shape
    rows = r // 4

    def body(chip_ref, *refs):
        w_ref, o_ref = refs[-2:]
        o_ref[...] = w_ref[...].astype(BF16)

    extra = [] if after is None else [after]
    return pl.pallas_call(
        body, name=name, out_shape=jax.ShapeDtypeStruct((NSH, r, cc), BF16),
        grid_spec=pltpu.PrefetchScalarGridSpec(
            num_scalar_prefetch=1, grid=(4,), in_specs=[_ANY] * len(extra) + [pl.BlockSpec((rows, cc), lambda i, chip: (i, 0))],
            out_specs=pl.BlockSpec((None, rows, cc), lambda i, chip: (chip[0], i, 0))),
        compiler_params=_params("arbitrary"))(chip, *extra, w)


def _piece(ref, slot, c):
    if ref.dtype == F32:
        return ref.at[slot]
    rh = ref.shape[1] // 2
    return ref.at[slot, pl.ds(pl.multiple_of(c * rh, 16), rh), :]


def _gather_start(stages, name):
    flat = [b for stage in stages for b in stage]
    n, ns = len(flat), len(stages)

    def body(*refs):
        ins, sems, token = refs[:n], refs[n:n + 2 * ns], refs[-1]
        x, y, c = _place()
        me = 2 * x + y
        k = 0
        for s, stage in enumerate(stages):
            for i in range(len(stage)):
                for j, (px, py) in enumerate(_other_chips(x, y)):
                    piece = _piece(ins[k], me, c)
                    pltpu.make_async_remote_copy(src_ref=piece, dst_ref=piece, send_sem=sems[2 * s].at[3 * i + j],
                                                 recv_sem=sems[2 * s + 1].at[3 * i + j], device_id=(px, py, c),
                                                 device_id_type=MESH).start()
                k += 1
        token[...] = jnp.zeros_like(token)

    sem_shapes = [pltpu.SemaphoreType.DMA((3 * len(stage),)) for stage in stages for _ in range(2)]
    outs = pl.pallas_call(
        body, name=name, in_specs=[_HBM] * n,
        out_specs=[_SEM] * (2 * ns) + [_HBM] * n + [pl.BlockSpec(memory_space=pltpu.VMEM)],
        out_shape=sem_shapes + [pltpu.HBM(b.shape, b.dtype) for b in flat] + [jax.ShapeDtypeStruct((8, 128), F32)],
        input_output_aliases={i: 2 * ns + i for i in range(n)},
        compiler_params=pltpu.CompilerParams(has_side_effects=_EFFECT),
    )(*[pltpu.with_memory_space_constraint(b, pltpu.HBM) for b in flat])
    sems, bufs, token = outs[:2 * ns], list(outs[2 * ns:2 * ns + n]), outs[-1]
    per_stage, k = [], 0
    for s, stage in enumerate(stages):
        per_stage.append((sems[2 * s], sems[2 * s + 1], bufs[k:k + len(stage)]))
        k += len(stage)
    return per_stage, token


def _gather_wait(send_sems, recv_sems, bufs, after, name):
    n = len(bufs)

    def body(*refs):
        ins, ssem, rsem = refs[:n], refs[n], refs[n + 1]
        x, y, c = _place()
        me = 2 * x + y
        for i in range(n):
            for j, (px, py) in enumerate(_other_chips(x, y)):
                cp = pltpu.make_async_remote_copy(src_ref=_piece(ins[i], me, c), dst_ref=_piece(ins[i], 2 * px + py, c),
                                                  send_sem=ssem.at[3 * i + j], recv_sem=rsem.at[3 * i + j],
                                                  device_id=(px, py, c), device_id_type=MESH)
                cp.wait_send()
                cp.wait_recv()

    afters = list(after) if isinstance(after, (list, tuple)) else [after]
    return pl.pallas_call(
        body, name=name, in_specs=[_HBM] * n + [_SEM, _SEM] + [_ANY] * len(afters), out_specs=[_HBM] * n,
        out_shape=[pltpu.HBM(b.shape, b.dtype) for b in bufs], input_output_aliases={i: i for i in range(n)},
        compiler_params=pltpu.CompilerParams(has_side_effects=_EFFECT),
    )(*bufs, send_sems, recv_sems, *afters)


def _sibling_fill(bufs, name):
    n = len(bufs)

    def body(*refs):
        ins, outs = refs[:n], refs[n:2 * n]
        send_sems, recv_sems = refs[2 * n:]
        x, y, c = _place()
        copies = []
        for i in range(n):
            for j, (px, py) in enumerate(_other_chips(x, y)):
                copies.append(pltpu.make_async_remote_copy(
                    src_ref=_piece(ins[i], 2 * px + py, c), dst_ref=_piece(outs[i], 2 * px + py, c),
                    send_sem=send_sems.at[3 * i + j], recv_sem=recv_sems.at[3 * i + j], device_id=(x, y, 1 - c),
                    device_id_type=MESH))
                copies[-1].start()
        for cp in copies:
            cp.wait()

    return pl.pallas_call(
        body, name=name, in_specs=[_ANY] * n, out_specs=[_ANY] * n,
        out_shape=[jax.ShapeDtypeStruct(b.shape, b.dtype) for b in bufs], input_output_aliases={i: i for i in range(n)},
        scratch_shapes=[pltpu.SemaphoreType.DMA((3 * n,)), pltpu.SemaphoreType.DMA((3 * n,))],
        compiler_params=pltpu.CompilerParams(has_side_effects=True),
    )(*bufs)


def _swap_plan(srcs, lands):
    x, y, c = _place()
    plan = []
    for src, land in zip(srcs, lands):
        rh = src.shape[1] // 2
        plan.append((src.at[:, pl.ds(pl.multiple_of((1 - c) * rh, 16), rh), :], land, (x, y, 1 - c)))
    return plan


def _owners_plan(srcs, lands):
    x, y, c = _place()
    return [(src.at[2 * px + py], land.at[j], (px, py, c))
            for src, land in zip(srcs, lands) for j, (px, py) in enumerate(_other_chips(x, y))]


def _exchange_start(srcs, lands, plan, copies, name):
    n, m = len(srcs), len(srcs) + len(lands)

    def body(*refs):
        send_sems, recv_sems, token = refs[m], refs[m + 1], refs[-1]
        for k, (src, dst, dev) in enumerate(plan(refs[:n], refs[n:m])):
            pltpu.make_async_remote_copy(src_ref=src, dst_ref=dst, send_sem=send_sems.at[k], recv_sem=recv_sems.at[k],
                                         device_id=dev, device_id_type=MESH).start()
        token[...] = jnp.zeros_like(token)

    both = list(srcs) + list(lands)
    outs = pl.pallas_call(
        body, name=name, in_specs=[_HBM] * m,
        out_specs=[_SEM, _SEM] + [_HBM] * m + [pl.BlockSpec(memory_space=pltpu.VMEM)],
        out_shape=[pltpu.SemaphoreType.DMA((copies,)), pltpu.SemaphoreType.DMA((copies,))]
        + [pltpu.HBM(b.shape, b.dtype) for b in both] + [jax.ShapeDtypeStruct((8, 128), F32)],
        input_output_aliases={i: 2 + i for i in range(m)},
        compiler_params=pltpu.CompilerParams(has_side_effects=_EFFECT),
    )(*[pltpu.with_memory_space_constraint(b, pltpu.HBM) for b in both])
    return (outs[0], outs[1]), list(outs[2:2 + n]), list(outs[2 + n:2 + m]), outs[-1]


def _exchange_wait(sems, srcs, lands, plan, after, name):
    n, m = len(srcs), len(srcs) + len(lands)

    def body(*refs):
        send_sems, recv_sems = refs[m], refs[m + 1]
        for k, (src, dst, dev) in enumerate(plan(refs[:n], refs[n:m])):
            cp = pltpu.make_async_remote_copy(src_ref=src, dst_ref=dst, send_sem=send_sems.at[k], recv_sem=recv_sems.at[k],
                                              device_id=dev, device_id_type=MESH)
            cp.wait_send()
            cp.wait_recv()

    both = list(srcs) + list(lands)
    afters = list(after) if isinstance(after, (list, tuple)) else [after]
    outs = pl.pallas_call(
        body, name=name, in_specs=[_HBM] * m + [_SEM, _SEM] + [_ANY] * len(afters), out_specs=[_HBM] * m,
        out_shape=[pltpu.HBM(b.shape, b.dtype) for b in both], input_output_aliases={i: i for i in range(m)},
        compiler_params=pltpu.CompilerParams(has_side_effects=_EFFECT),
    )(*both, sems[0], sems[1], *afters)
    return list(outs[:n]), list(outs[n:])


def _fill_plan(bufs, _):
    x, y, c = _place()
    return [(_piece(buf, 2 * px + py, c), _piece(buf, 2 * px + py, c), (x, y, 1 - c))
            for buf in bufs for px, py in _other_chips(x, y)]


class _Reducer:
    def __init__(self, where):
        self.state = {}
        self.where = where

    def begin(self, stage, grads):
        names = list(grads)
        full = [grads[n] for n in names]
        lands = [lax.empty((NSH, g.shape[1] // 2, g.shape[2]), g.dtype) for g in full]
        sems, full, lands, token = _exchange_start(full, lands, _swap_plan, len(full), "swap_start_" + stage)
        self.state[stage] = (names, sems, full, lands)
        return token

    def advance(self, stage, after):
        names, sems, full, lands = self.state[stage]
        full, got = _exchange_wait(sems, full, lands, _swap_plan, after, "swap_wait_" + stage)
        sums, own = _chip_sums(full, got, self.where, "chip_sums_" + stage)
        lands = [lax.empty((3,) + s.shape[1:], BF16) for s in sums]
        sems, sent, lands, token = _exchange_start(sums, lands, _owners_plan, 3 * len(sums), "owners_start_" + stage)
        self.state[stage] = (names, own, sems, sent, lands)
        return token

    def finish(self, stage, after):
        names, own, sems, sent, lands = self.state[stage]
        _, got = _exchange_wait(sems, sent, lands, _owners_plan, after, "owners_wait_" + stage)
        return dict(zip(names, _owner_sums(own, got, "owner_sums_" + stage)))


def _chip_sums(gs, gots, where, name):
    n = len(gs)

    def body(where_ref, *refs):
        g_refs, got_refs, hb_refs, own_refs = (refs[k * n:(k + 1) * n] for k in range(4))
        mine = pl.program_id(0) == where_ref[1]
        for g_ref, got_ref, hb_ref, own_ref in zip(g_refs, got_refs, hb_refs, own_refs):
            h = g_ref[...].astype(F32) + got_ref[...].astype(F32)
            hb_ref[...] = h.astype(BF16)

            @pl.when(mine)
            def _():
                own_ref[...] = h

    halves = [(g.shape[1] // 2, g.shape[2]) for g in gs]
    slot = [pl.BlockSpec((None, rh, cc), lambda s, where: (s, 0, 0)) for rh, cc in halves]
    outs = pl.pallas_call(
        body, name=name,
        grid_spec=pltpu.PrefetchScalarGridSpec(
            num_scalar_prefetch=1, grid=(NSH,),
            in_specs=[pl.BlockSpec((None, rh, cc), lambda s, where: (s, where[0], 0)) for rh, cc in halves] + slot,
            out_specs=slot + [pl.BlockSpec((rh, cc), lambda s, where: (0, 0)) for rh, cc in halves]),
        out_shape=[jax.ShapeDtypeStruct((NSH, rh, cc), BF16) for rh, cc in halves]
        + [jax.ShapeDtypeStruct((rh, cc), F32) for rh, cc in halves],
        compiler_params=_params("arbitrary"),
    )(where, *gs, *gots)
    return list(outs[:n]), list(outs[n:])


def _owner_sums(owns, gots, name):
    n = len(owns)

    def body(*refs):
        own_refs, got_refs, o_refs = (refs[k * n:(k + 1) * n] for k in range(3))
        for own_ref, got_ref, o_ref in zip(own_refs, got_refs, o_refs):
            o_ref[...] = ((own_ref[...] + got_ref[0].astype(F32)) + got_ref[1].astype(F32)) + got_ref[2].astype(F32)

    blocks = [(o.shape[0] // 2, o.shape[1]) for o in owns]
    rows = [pl.BlockSpec(b, lambda i: (i, 0)) for b in blocks]
    return pl.pallas_call(
        body, grid=(2,), name=name,
        in_specs=rows + [pl.BlockSpec((3,) + b, lambda i: (0, i, 0)) for b in blocks], out_specs=rows,
        out_shape=[jax.ShapeDtypeStruct(o.shape, F32) for o in owns], compiler_params=_params("arbitrary"),
    )(*owns, *gots)


def _sibling_plan(srcs, lands):
    x, y, c = _place()
    return [(src, land, (x, y, 1 - c)) for src, land in zip(srcs, lands)]


def _all_reduce_small(part):
    def body(p_ref, o_ref, rbuf, send1, recv1, send2, recv2):
        x, y, c = _place()
        me = 4 * x + 2 * y + c
        peers = []
        for k in range(1, 8):
            px, py, pc = x ^ ((k >> 2) & 1), y ^ ((k >> 1) & 1), c ^ (k & 1)
            peers.append((k, (px, py, pc), 4 * px + 2 * py + pc))

        def rows(d):
            return pl.ds(pl.multiple_of(d * SMALL_SLICE, 8), SMALL_SLICE)

        first = [pltpu.make_async_remote_copy(src_ref=p_ref.at[rows(idx), :], dst_ref=rbuf.at[me], send_sem=send1.at[k],
                                              recv_sem=recv1.at[k], device_id=dev, device_id_type=MESH)
                 for k, dev, idx in peers]
        for cp in first:
            cp.start()
        rbuf[me] = p_ref[rows(me), :]
        for k, dev, idx in peers:
            pltpu.make_async_remote_copy(src_ref=p_ref.at[rows(idx), :], dst_ref=rbuf.at[idx], send_sem=send1.at[k],
                                         recv_sem=recv1.at[k], device_id=dev, device_id_type=MESH).wait_recv()
        acc = rbuf[0]
        for d in range(1, 8):
            acc = acc + rbuf[d]
        o_ref[rows(me), :] = acc
        second = [pltpu.make_async_remote_copy(src_ref=o_ref.at[rows(me), :], dst_ref=o_ref.at[rows(me), :],
                                               send_sem=send2.at[k], recv_sem=recv2.at[k], device_id=dev, device_id_type=MESH)
                  for k, dev, idx in peers]
        for cp in second:
            cp.start()
        for k, dev, idx in peers:
            pltpu.make_async_remote_copy(src_ref=o_ref.at[rows(me), :], dst_ref=o_ref.at[rows(idx), :], send_sem=send2.at[k],
                                         recv_sem=recv2.at[k], device_id=dev, device_id_type=MESH).wait_recv()
        for cp in first + second:
            cp.wait_send()

    return pl.pallas_call(
        body, name="all_reduce_small", in_specs=[_WHOLE], out_specs=_WHOLE,
        out_shape=jax.ShapeDtypeStruct((SMALL_ROWS, 128), F32),
        scratch_shapes=[pltpu.VMEM((8, SMALL_SLICE, 128), F32)] + [pltpu.SemaphoreType.DMA((8,))] * 4,
        compiler_params=pltpu.CompilerParams(has_side_effects=True),
    )(part)


def _adamw_update(w, gv, m, v):
    nm = ADAM_B1 * m + (1.0 - ADAM_B1) * gv
    nv = ADAM_B2 * v + (1.0 - ADAM_B2) * (gv * gv)
    m_hat = nm / (1.0 - ADAM_B1 ** ADAM_STEP)
    v_hat = nv / (1.0 - ADAM_B2 ** ADAM_STEP)
    return -ADAM_LR * (m_hat / (jnp.sqrt(v_hat) + ADAM_EPS) + ADAM_WD * w), nm, nv


def _adamw_small(ws, gs, ms, vs, after):
    n = len(ws)

    def body(*refs):
        w_refs, g_refs, m_refs, v_refs, d_refs, nm_refs, nv_refs = (refs[k * n:(k + 1) * n] for k in range(7))
        for i in range(n):
            d_refs[i][...], nm_refs[i][...], nv_refs[i][...] = _adamw_update(
                w_refs[i][...], g_refs[i][...], m_refs[i][...], v_refs[i][...])

    out = [jax.ShapeDtypeStruct(w.shape, F32) for w in ws]
    body, specs, operands = _behind(body, after)
    outs = pl.pallas_call(body, in_specs=specs + [_WHOLE] * (4 * n), out_specs=[_WHOLE] * (3 * n), out_shape=out * 3,
                          name="adamw_small", compiler_params=_params())(*operands, *ws, *gs, *ms, *vs)
    return outs[:n], outs[n:2 * n], outs[2 * n:]


def _adamw_halves(ws, mines, theirs, ms, vs, name):
    n = len(ws)
    steps = 2

    def body(*refs):
        w_refs, mine_refs, theirs_refs, m_refs, v_refs, g_refs, d_refs, nm_refs, nv_refs = (
            refs[k * n:(k + 1) * n] for k in range(9))
        is_mine = pl.program_id(0) == lax.axis_index("c")
        for i in range(n):
            gv = jnp.where(is_mine, mine_refs[i][...], theirs_refs[i][...])
            g_refs[i][...] = gv
            d_refs[i][...], nm_refs[i][...], nv_refs[i][...] = _adamw_update(w_refs[i][...], gv, m_refs[i][...], v_refs[i][...])

    blocks = [(h.shape[0] // steps, h.shape[1]) for h in mines]
    whole = [pl.BlockSpec(b, lambda h, i: (steps * h + i, 0)) for b in blocks]
    half = [pl.BlockSpec(b, lambda h, i: (i, 0)) for b in blocks]
    out = [jax.ShapeDtypeStruct(w.shape, F32) for w in ws]
    outs = pl.pallas_call(body, grid=(2, steps), in_specs=whole + half + half + whole + whole, out_specs=whole * 4,
                          out_shape=out * 4, name=name, compiler_params=_params("arbitrary", "arbitrary"),
                          )(*ws, *mines, *theirs, *ms, *vs)
    return [tuple(outs[k * n + i] for k in range(4)) for i in range(n)]


SMALL_USED = sum(size for _, size in SMALL) // 128


def _pack_small(vals, tail=None):
    parts = []
    for name, size in SMALL:
        flat = vals[name].reshape(-1).astype(F32)
        parts.append(jnp.pad(flat, (0, size - flat.shape[0])))
    if tail is not None:
        parts.append(tail.reshape(128))
    flat = jnp.concatenate(parts)
    return jnp.pad(flat, (0, SMALL_ROWS * 128 - flat.shape[0])).reshape(SMALL_ROWS, 128)


def _unpack_small(packed, shapes):
    flat = packed.reshape(-1)
    out, off = {}, 0
    for name, size in SMALL:
        n = math.prod(shapes[name])
        out[name] = flat[off:off + n].reshape(shapes[name])
        off += size
    return out


def kernel(x, ffn1_pre_g, ffn1_w1, ffn1_w3, ffn1_w2, ffn1_post_g, mix_pre_g, w_in, conv_w, conv_b, rg_a_w, rg_a_b, rg_x_w, rg_x_b, lru_lambda, w_lru_out, attn_sinks, rel_bias, w_attn_out, w_gate, b_gate, w_o, mix_post_g, ffn2_pre_g, ffn2_w1, ffn2_w3, ffn2_w2, ffn2_post_g, loss_target, m_ffn1_pre_g, m_ffn1_w1, m_ffn1_w3, m_ffn1_w2, m_ffn1_post_g, m_mix_pre_g, m_w_in, m_conv_w, m_conv_b, m_rg_a_w, m_rg_a_b, m_rg_x_w, m_rg_x_b, m_lru_lambda, m_w_lru_out, m_attn_sinks, m_rel_bias, m_w_attn_out, m_w_gate, m_b_gate, m_w_o, m_mix_post_g, m_ffn2_pre_g, m_ffn2_w1, m_ffn2_w3, m_ffn2_w2, m_ffn2_post_g, v_ffn1_pre_g, v_ffn1_w1, v_ffn1_w3, v_ffn1_w2, v_ffn1_post_g, v_mix_pre_g, v_w_in, v_conv_w, v_conv_b, v_rg_a_w, v_rg_a_b, v_rg_x_w, v_rg_x_b, v_lru_lambda, v_w_lru_out, v_attn_sinks, v_rel_bias, v_w_attn_out, v_w_gate, v_b_gate, v_w_o, v_mix_post_g, v_ffn2_pre_g, v_ffn2_w1, v_ffn2_w3, v_ffn2_w2, v_ffn2_post_g):
    given = dict(locals())
    chip = 2 * lax.axis_index("x") + lax.axis_index("y")
    transposed = ("ffn1_w1", "ffn1_w3", "ffn2_w1", "ffn2_w3")

    def shard(name, moment=""):
        w = given[moment + name][0]
        return w.T if name in transposed else w

    def unshard(name, w):
        return (w.T if name in transposed else w)[None]

    def only_my_columns(a):
        parts = a.reshape(1, 4, NSH, D // NSH)
        return sum(jnp.where(chip == s, parts[:, :, s], 0.0) for s in range(NSH))

    chip_arr = jnp.reshape(chip, (1,)).astype(jnp.int32)
    stage_names = {"ffn1_up": ["ffn1_w1", "ffn1_w3", "conv_w"],
                   "ffn1_down": ["ffn1_w2"],
                   "mix_in": ["w_in", "w_gate"],
                   "mix_out": ["w_lru_out", "w_attn_out", "w_o"],
                   "ffn2": ["ffn2_w1", "ffn2_w3", "ffn2_w2"]}
    in_flight, started = {}, None
    for stage, names in stage_names.items():
        bufs = [jnp.where(lax.broadcasted_iota(jnp.int32, (NSH, 4, D // NSH), 0) == chip, given[n], 0.0) if n == "conv_w"
                else _cast_into_slot(shard(n), chip_arr, "cast_" + n, started) for n in names]
        (in_flight[stage],), started = _gather_start([bufs], "gather_start_" + stage)
    all_started = started

    filling = {}

    def weights(stage, after, begin=False):
        names = stage_names[stage]
        halves_of = [n for n in names if n != "conv_w"]
        if stage in filling:
            filled, _ = _exchange_wait(filling.pop(stage), *filling.pop(stage + "/bufs"), _fill_plan, after,
                                       "fill_wait_" + stage)
            return dict(zip(halves_of, filled))
        send_sems, recv_sems, landing = in_flight[stage]
        if stage == "ffn1_up":
            after = [all_started] + list(after)
        landed = dict(zip(names, _gather_wait(send_sems, recv_sems, landing, after, "gather_wait_" + stage)))
        halves = [landed[n] for n in halves_of]
        if begin:
            filling[stage], bufs, _, token = _exchange_start(halves, [], _fill_plan, 3 * len(halves), "fill_start_" + stage)
            filling[stage + "/bufs"] = (bufs, [])
            return token
        out = dict(zip(halves_of, _sibling_fill(halves, "sibling_fill_" + stage)))
        if "conv_w" in names:
            out["conv_w"] = jnp.transpose(landed["conv_w"], (1, 0, 2)).reshape(4, D)
        return out

    small_shapes = {n: given[n].shape for n, _ in SMALL}
    small_shapes["conv_w"] = (1, 4, D)
    sm = {n: (given[n][0] if given[n].shape[0] == 1 and n != "rel_bias" else given[n]) for n, _ in SMALL if n != "conv_w"}

    reducer = _Reducer(jnp.stack([lax.axis_index("c"), chip]).astype(jnp.int32))
    sq, dx, _, small = _local_step(x[0], loss_target[0], weights, sm, reducer)

    reduced_small = _all_reduce_small(_pack_small(small, tail=sq))
    last_started = reducer.advance("ffn1", [dx, reduced_small])
    loss = reduced_small[SMALL_USED, 0] * (0.5 / D)
    small_g = _unpack_small(reduced_small, small_shapes)
    grads, delta, new_m, new_v = {}, {}, {}, {}
    in_transit = {}

    def send(stage, after):
        halves = reducer.finish(stage, after)
        lands = [lax.empty(h.shape, F32) for h in halves.values()]
        sems, mine, lands, token = _exchange_start(list(halves.values()), lands, _sibling_plan, len(lands),
                                                   "halves_start_" + stage)
        in_transit[stage] = (list(halves), sems, mine, lands)
        return token

    def update(stage, after):
        names, sems, mine, lands = in_transit[stage]
        mine, theirs = _exchange_wait(sems, mine, lands, _sibling_plan, after, "halves_wait_" + stage)
        updated = _adamw_halves([shard(n) for n in names], mine, theirs, [shard(n, "m_") for n in names],
                                [shard(n, "v_") for n in names], "adamw_" + stage)
        for n, results in zip(names, updated):
            grads[n], delta[n], new_m[n], new_v[n] = (unshard(n, r) for r in results)
        return new_v[names[-1]]

    token = send("ffn2", [reduced_small, last_started])
    token = send("mix", token)
    done = update("ffn2", token)
    done = update("mix", done)
    token = send("ffn1", done)
    update("ffn1", token)

    small_g["conv_w"] = only_my_columns(small_g["conv_w"])
    names = [n for n, _ in SMALL]
    flat2d = lambda a: a.reshape(-1, a.shape[-1])
    outs = _adamw_small(*[[flat2d(given[pre + n]) if pre != "g" else flat2d(small_g[n]) for n in names]
                          for pre in ("", "g", "m_", "v_")], after=last_started)
    for dst, arrs in zip((delta, new_m, new_v), outs):
        dst.update({n: a.reshape(given[n].shape) for n, a in zip(names, arrs)})
    grads.update(small_g)
    return (loss, dx[None], *[grads[n] for n in WEIGHTS], *[delta[n] for n in WEIGHTS], *[new_m[n] for n in WEIGHTS],
            *[new_v[n] for n in WEIGHTS])
```

```python
import functools
import math

import jax
import jax.numpy as jnp
from jax import lax
from jax.experimental import pallas as pl
from jax.experimental.pallas import tpu as pltpu

F32, BF16 = jnp.float32, jnp.bfloat16
D = 1024
NSH = 4
FF_S = 704
IN_S = 896
GATE_S = 512
KV_W = 256
CHUNK = 64
KB = 192
N_HEADS = 16
HEAD_DIM = 64
N_BUCKETS = 32
KP = 192
PAD_KEYS = 128
RMS_EPS = 1e-6
NEG_INF = -1e30
LRU_C = 8.0
TM = 512
TM_SCAN = 256
VMEM_LIMIT = 56 * 1024 * 1024
ADAM_LR, ADAM_B1, ADAM_B2, ADAM_EPS, ADAM_WD, ADAM_STEP = 0.001, 0.9, 0.999, 1e-08, 0.01, 10
SMALL_ROWS = 1216
SMALL_SLICE = SMALL_ROWS // 8
MESH = pl.DeviceIdType.MESH

BIG = ["ffn1_w1", "ffn1_w3", "ffn1_w2", "w_in", "w_lru_out", "w_attn_out", "w_gate", "w_o", "ffn2_w1", "ffn2_w3", "ffn2_w2"]
SMALL = [("ffn1_pre_g", 1024), ("ffn1_post_g", 1024), ("mix_pre_g", 1024), ("conv_w", 4096), ("conv_b", 1024),
         ("rg_a_w", 65536), ("rg_a_b", 1024), ("rg_x_w", 65536), ("rg_x_b", 1024), ("lru_lambda", 1024),
         ("attn_sinks", 1024), ("rel_bias", 1024), ("b_gate", 2048), ("mix_post_g", 1024), ("ffn2_pre_g", 1024),
         ("ffn2_post_g", 1024)]
WEIGHTS = ["ffn1_pre_g", "ffn1_w1", "ffn1_w3", "ffn1_w2", "ffn1_post_g", "mix_pre_g", "w_in", "conv_w", "conv_b", "rg_a_w",
           "rg_a_b", "rg_x_w", "rg_x_b", "lru_lambda", "w_lru_out", "attn_sinks", "rel_bias", "w_attn_out", "w_gate", "b_gate",
           "w_o", "mix_post_g", "ffn2_pre_g", "ffn2_w1", "ffn2_w3", "ffn2_w2", "ffn2_post_g"]


def _params(*sem):
    return pltpu.CompilerParams(dimension_semantics=sem or None, vmem_limit_bytes=VMEM_LIMIT)


def _nn(a, b):
    return jnp.dot(a, b, preferred_element_type=F32)


def _nt(a, b):
    return lax.dot_general(a, b, (((1,), (1,)), ((), ())), preferred_element_type=F32)


def _tn(a, b):
    return lax.dot_general(a, b, (((0,), (0,)), ((), ())), preferred_element_type=F32)


def _rms(x, g):
    rstd = lax.rsqrt(jnp.mean(x * x, axis=-1, keepdims=True) + RMS_EPS)
    return (x * rstd) * g


def _rms_bwd(dout, x, g):
    rstd = lax.rsqrt(jnp.mean(x * x, axis=-1, keepdims=True) + RMS_EPS)
    xhat = x * rstd
    dg = jnp.sum(dout * xhat, axis=0, keepdims=True)
    dxhat = dout * g
    dx = rstd * (dxhat - xhat * jnp.mean(dxhat * xhat, axis=-1, keepdims=True))
    return dx, dg


_GELU_K = math.sqrt(2.0 / math.pi)


_GELU_C = 0.044715 * _GELU_K


def _gelu_and_grad(x):
    x2 = x * x
    t = jnp.tanh(x * (_GELU_K + _GELU_C * x2))
    cdf = 0.5 + 0.5 * t
    return x * cdf, cdf + (x * (_GELU_K + (3.0 * _GELU_C) * x2)) * (0.5 - 0.5 * (t * t))


def _softplus_neg(lam):
    z = -lam
    u = jnp.exp(-jnp.abs(z))
    w = 1.0 + u
    log1p_u = jnp.where(w == 1.0, u, jnp.log(w) * (u / (w - 1.0)))
    return jnp.maximum(z, 0.0) + log1p_u


def _lru_coeffs(r, sp):
    log_a = (-LRU_C * r) * sp
    a = jnp.exp(log_a)
    t = jnp.tanh(log_a)
    s = jnp.sqrt(-2.0 * t / (1.0 - t))
    return a, s


def _row_spec(tm, width):
    return pl.BlockSpec((tm, width), lambda i: (i, 0))


def _vec_spec(width):
    return pl.BlockSpec((1, width), lambda i: (0, 0))


_WHOLE = pl.BlockSpec(memory_space=pltpu.VMEM)


def _tile(t, tm=TM):
    return min(tm, t)


def _ffn_fwd(x, gpre, w1g, w3g, w2g, gpost, name, target=None):
    t = x.shape[0]
    tm = _tile(t)
    last = target is not None

    def body(x_ref, gpre_ref, w1_ref, w3_ref, w2_ref, gpost_ref, *refs):
        t_ref, (h_ref, a_ref, b_ref, hm_ref, f_ref), l_ref = (refs[0] if last else None), refs[last:last + 5], refs[-1]
        xv = x_ref[...]
        nb = _rms(xv, gpre_ref[...]).astype(BF16)
        f = jnp.zeros((tm, D), F32)
        for s in range(NSH):
            a = _nt(nb, w1_ref[s])
            b = _nt(nb, w3_ref[s])
            hmb = ((a * jax.nn.sigmoid(a)) * b).astype(BF16)
            a_ref[s] = a.astype(BF16)
            b_ref[s] = b.astype(BF16)
            hm_ref[s] = hmb
            f = f + _nn(hmb, w2_ref[s])
        f_ref[...] = f
        h = xv + 0.5 * _rms(f, gpost_ref[...])
        if last:
            @pl.when(pl.program_id(0) == 0)
            def _():
                l_ref[...] = jnp.zeros_like(l_ref)

            e = h - t_ref[...]
            h_ref[...] = e * (1.0 / D)
            l_ref[...] += jnp.sum(jnp.sum(e * e, axis=0, keepdims=True), axis=1, keepdims=True)
        else:
            h_ref[...] = h

    sh = pl.BlockSpec((NSH, tm, FF_S), lambda i: (0, i, 0))
    act = jax.ShapeDtypeStruct((NSH, t, FF_S), BF16)
    return pl.pallas_call(
        body, grid=(t // tm,), name=name,
        in_specs=[_row_spec(tm, D), _vec_spec(D), _WHOLE, _WHOLE, _WHOLE, _vec_spec(D)] + [_row_spec(tm, D)] * last,
        out_specs=[_row_spec(tm, D), sh, sh, sh, _row_spec(tm, D)] + [pl.BlockSpec((1, 128), lambda i: (0, 0))] * last,
        out_shape=[jax.ShapeDtypeStruct((t, D), F32), act, act, act, jax.ShapeDtypeStruct((t, D), F32)]
        + [jax.ShapeDtypeStruct((1, 128), F32)] * last,
        compiler_params=_params("arbitrary"),
    )(x, gpre, w1g, w3g, w2g, gpost, *([target] if last else []))


def _ffn_up(x, gpre, w1g, w3g, name):
    t = x.shape[0]
    tm = _tile(t)

    def body(x_ref, gpre_ref, w1_ref, w3_ref, a_ref, b_ref, hm_ref):
        nb = _rms(x_ref[...], gpre_ref[...]).astype(BF16)
        for s in range(NSH):
            a = _nt(nb, w1_ref[s])
            b = _nt(nb, w3_ref[s])
            a_ref[s] = a.astype(BF16)
            b_ref[s] = b.astype(BF16)
            hm_ref[s] = ((a * jax.nn.sigmoid(a)) * b).astype(BF16)

    sh = pl.BlockSpec((NSH, tm, FF_S), lambda i: (0, i, 0))
    act = jax.ShapeDtypeStruct((NSH, t, FF_S), BF16)
    return pl.pallas_call(
        body, grid=(t // tm,), name=name, in_specs=[_row_spec(tm, D), _vec_spec(D), _WHOLE, _WHOLE],
        out_specs=[sh, sh, sh], out_shape=[act, act, act], compiler_params=_params("arbitrary"),
    )(x, gpre, w1g, w3g)


def _ffn_down(x, hm, w2g, gpost, name):
    t = x.shape[0]
    tm = _tile(t)

    def body(x_ref, hm_ref, w2_ref, gpost_ref, h_ref, f_ref):
        f = jnp.zeros((tm, D), F32)
        for s in range(NSH):
            f = f + _nn(hm_ref[s], w2_ref[s])
        f_ref[...] = f
        h_ref[...] = x_ref[...] + 0.5 * _rms(f, gpost_ref[...])

    sh = pl.BlockSpec((NSH, tm, FF_S), lambda i: (0, i, 0))
    f32 = jax.ShapeDtypeStruct((t, D), F32)
    return pl.pallas_call(
        body, grid=(t // tm,), name=name, in_specs=[_row_spec(tm, D), sh, _WHOLE, _vec_spec(D)],
        out_specs=[_row_spec(tm, D), _row_spec(tm, D)], out_shape=[f32, f32], compiler_params=_params("arbitrary"),
    )(x, hm, w2g, gpost)


def _mix_proj(h1, gmix, w_in_g, w_gate_g, b_gate):
    t = h1.shape[0]
    tm = _tile(t)

    def body(h_ref, g_ref, win_ref, wg_ref, bg_ref, u_ref, q_ref, k_ref, v_ref, xr_ref, gx_ref, gdx_ref, gate_ref):
        ub = _rms(h_ref[...], g_ref[...]).astype(BF16)
        u_ref[...] = ub
        p0 = _nn(ub, win_ref[0])
        q_ref[:, 0:896] = p0.astype(BF16)
        p1 = _nn(ub, win_ref[1])
        q_ref[:, 896:1024] = p1[:, 0:128].astype(BF16)
        k_ref[...] = p1[:, 128:384].astype(BF16)
        v_ref[...] = p1[:, 384:640].astype(BF16)
        xr_ref[:, 0:256] = p1[:, 640:896]
        p2 = _nn(ub, win_ref[2])
        xr_ref[:, 256:1024] = p2[:, 0:768]
        gx_ref[:, 0:128], gdx_ref[:, 0:128] = _gelu_and_grad(p2[:, 768:896])
        gx_ref[:, 128:1024], gdx_ref[:, 128:1024] = _gelu_and_grad(_nn(ub, win_ref[3]))
        for s in range(NSH):
            sl = slice(s * GATE_S, (s + 1) * GATE_S)
            gate_ref[:, sl] = jax.nn.sigmoid(_nn(ub, wg_ref[s]) + bg_ref[:, sl])

    return pl.pallas_call(
        body, grid=(t // tm,), name="mix_proj",
        in_specs=[_row_spec(tm, D), _vec_spec(D), _WHOLE, _WHOLE, _vec_spec(2 * D)],
        out_specs=[_row_spec(tm, D), _row_spec(tm, D), _row_spec(tm, KV_W), _row_spec(tm, KV_W), _row_spec(tm, D),
                   _row_spec(tm, D), _row_spec(tm, D), _row_spec(tm, 2 * D)],
        out_shape=[jax.ShapeDtypeStruct((t, D), BF16), jax.ShapeDtypeStruct((t, D), BF16),
                   jax.ShapeDtypeStruct((t, KV_W), BF16), jax.ShapeDtypeStruct((t, KV_W), BF16),
                   jax.ShapeDtypeStruct((t, D), F32), jax.ShapeDtypeStruct((t, D), F32), jax.ShapeDtypeStruct((t, D), F32),
                   jax.ShapeDtypeStruct((t, 2 * D), F32)],
        compiler_params=_params("arbitrary"),
    )(h1, gmix, w_in_g, w_gate_g, b_gate)


def _rglru_fwd(xr, xg, conv_w, conv_b, wa2, ba, wx2, bx, lam, after=None):
    t = xr.shape[0]
    tm = _tile(t, TM_SCAN)
    nb8 = tm // 8

    def body(xr_ref, xrp_ref, xg_ref, cw_ref, cb_ref, wa_ref, ba_ref, wx_ref, bx_ref, lam_ref,
             hr_ref, yain_ref, xc_ref, r_ref, ig_ref, a_sc, s_ref, ext, h_sc):
        i = pl.program_id(0)

        @pl.when(i == 0)
        def _():
            h_sc[...] = jnp.zeros_like(h_sc)

        ext[0:8, :] = jnp.where(i == 0, 0.0, xrp_ref[...])
        ext[8:8 + tm, :] = xr_ref[...]
        xc = jnp.broadcast_to(cb_ref[...], (tm, D))
        for tap in range(4):
            xc = xc + ext[pl.ds(5 + tap, tm), :] * cw_ref[tap:tap + 1, :]
        xc_ref[...] = xc
        xcb = xc.astype(BF16)
        for p in range(8):
            sl = slice(p * 128, (p + 1) * 128)
            r_ref[:, sl] = jax.nn.sigmoid(_nn(xcb[:, sl], wa_ref[p]) + ba_ref[:, sl])
            ig_ref[:, sl] = jax.nn.sigmoid(_nn(xcb[:, sl], wx_ref[p]) + bx_ref[:, sl])
        a, s = _lru_coeffs(r_ref[...], _softplus_neg(lam_ref[...]))
        a_sc[...] = a
        s_ref[...] = s
        hr_ref[...] = s * (ig_ref[...] * xc)

        def blk(j, h):
            st = pl.multiple_of(j * 8, 8)
            a8 = a_sc[pl.ds(st, 8), :]
            u8 = hr_ref[pl.ds(st, 8), :]
            rows = []
            for k in range(8):
                h = a8[k:k + 1, :] * h + u8[k:k + 1, :]
                rows.append(h)
            hr_ref[pl.ds(st, 8), :] = jnp.concatenate(rows, axis=0)
            return h

        h_sc[0:1, :] = lax.fori_loop(0, nb8, blk, h_sc[0:1, :])
        yain_ref[...] = (hr_ref[...] * xg_ref[...]).astype(BF16)

    prev = pl.BlockSpec((8, D), lambda i: (jnp.maximum(i * nb8 - 1, 0), 0))
    full = lambda shape: pl.BlockSpec(shape, lambda i: tuple(0 for _ in shape))
    f32 = jax.ShapeDtypeStruct((t, D), F32)
    body, specs, operands = _behind(body, after)
    return pl.pallas_call(
        body, grid=(t // tm,), name="rglru_fwd",
        in_specs=specs + [_row_spec(tm, D), prev, _row_spec(tm, D), full((4, D)), _vec_spec(D), full((8, 128, 128)),
                          _vec_spec(D), full((8, 128, 128)), _vec_spec(D), _vec_spec(D)],
        out_specs=[_row_spec(tm, D)] * 7,
        out_shape=[f32, jax.ShapeDtypeStruct((t, D), BF16), f32, f32, f32, f32, f32],
        scratch_shapes=[pltpu.VMEM((tm + 8, D), F32), pltpu.VMEM((8, D), F32)],
        compiler_params=_params("arbitrary"),
    )(*operands, xr, xr, xg, conv_w, conv_b, wa2, ba, wx2, bx, lam)


def _bias_fwd(table_t, onehot_t):
    def body(t_ref, e_ref, o_ref):
        o_ref[...] = jnp.dot(t_ref[...], e_ref[...], preferred_element_type=F32, precision=lax.Precision.HIGHEST)

    return pl.pallas_call(body, out_shape=jax.ShapeDtypeStruct((N_HEADS, CHUNK * KB), F32), name="bias_fwd",
                          compiler_params=_params())(table_t, onehot_t)


def _bias_bwd(dbias_flat, onehot_t, ds_rows):
    def body(d_ref, e_ref, s_ref, o_ref, so_ref):
        o_ref[...] = lax.dot_general(d_ref[...], e_ref[...], (((1,), (1,)), ((), ())), preferred_element_type=F32,
                                     precision=lax.Precision.HIGHEST)
        so_ref[...] = jnp.zeros_like(so_ref)
        for r in range(4):
            so_ref[:, r:r + 1] = jnp.sum(s_ref[:, r * CHUNK:(r + 1) * CHUNK], axis=1, keepdims=True)

    return pl.pallas_call(body, out_shape=[jax.ShapeDtypeStruct((N_HEADS, N_BUCKETS), F32), jax.ShapeDtypeStruct((8, 128), F32)],
                          name="bias_bwd", compiler_params=_params())(dbias_flat, onehot_t, ds_rows)


def _stack_heads(q):
    return jnp.concatenate(
        [jnp.concatenate([q[:, (4 * g + r) * HEAD_DIM:(4 * g + r + 1) * HEAD_DIM] for g in range(4)], axis=1)
         for r in range(4)], axis=0)


def _unstack_heads(o):
    return jnp.concatenate([o[r * CHUNK:(r + 1) * CHUNK, g * HEAD_DIM:(g + 1) * HEAD_DIM] for g in range(4) for r in range(4)],
                           axis=1)


def _block_diag(w, mask):
    return jnp.concatenate([w] * 4, axis=0) * mask


def _group_softmax(qk, bias_g, sink, valid):
    s = qk * (HEAD_DIM ** -0.5) + bias_g
    s = jnp.where(valid, s, NEG_INF)
    m = jnp.maximum(jnp.max(s, axis=0, keepdims=True), sink)
    e = jnp.exp(s - m)
    es = jnp.exp(sink - m)
    inv = 1.0 / (jnp.sum(e, axis=0, keepdims=True) + es)
    return e * inv, es * inv


def _attn_fwd(sink_rows, q, kp, vp, bias_t, mask, after=None):
    t = q.shape[0]
    per_step = 8

    def body(sink_ref, q_ref, kp_ref, vp_ref, bias_ref, mask_ref, o_ref):
        owns = [mask_ref[g * KP:(g + 1) * KP, :] for g in range(4)]
        for k in range(per_step):
            c = pl.program_id(0) * per_step + k
            rows = slice(k * CHUNK, (k + 1) * CHUNK)
            st = pl.multiple_of(c * CHUNK, CHUNK)
            kw = kp_ref[pl.ds(st, KP), :]
            vw = vp_ref[pl.ds(st, KP), :]
            q_all = _stack_heads(q_ref[rows, :])
            valid = lax.broadcasted_iota(jnp.int32, (KP, 1), 0) + c * CHUNK >= PAD_KEYS
            scores = [_nt(kw * owns[g], q_all) for g in range(4)]
            ps = [_group_softmax(scores[g], bias_ref[g * KP:(g + 1) * KP, :], sink_ref[g:g + 1, :], valid)[0]
                  for g in range(4)]
            o_all = sum(_tn(ps[g].astype(BF16), vw * owns[g]) for g in range(4))
            o_ref[rows, :] = _unstack_heads(o_all).astype(BF16)

    body, specs, operands = _behind(body, after)
    return pl.pallas_call(
        body, grid=(t // (per_step * CHUNK),), name="attn_fwd",
        in_specs=specs + [_WHOLE, _row_spec(per_step * CHUNK, D), _WHOLE, _WHOLE, _WHOLE, _WHOLE],
        out_specs=_row_spec(per_step * CHUNK, D),
        out_shape=jax.ShapeDtypeStruct((t, D), BF16),
        compiler_params=_params("arbitrary"),
    )(*operands, sink_rows, q, kp, vp, bias_t, mask)


def _merge_fwd(yain, o, gate, h1, w_lru, w_att, w_o, gpost):
    t = h1.shape[0]
    tm = _tile(t)

    def body(ya_ref, o_ref, g_ref, h_ref, wl_ref, wa_ref, wo_ref, gp_ref, h2_ref, mo_ref, mg_ref, ya_out, yb_out):
        ya = _nn(ya_ref[...], wl_ref[...])
        yb = _nn(o_ref[...], wa_ref[...])
        g0 = g_ref[:, 0:D]
        g1 = g_ref[:, D:2 * D]
        mg = (g0 * ya + g1 * yb).astype(BF16)
        mo = _nn(mg, wo_ref[...])
        ya_out[...] = (ya * (g0 * (1.0 - g0))).astype(BF16)
        yb_out[...] = (yb * (g1 * (1.0 - g1))).astype(BF16)
        mg_ref[...] = mg
        mo_ref[...] = mo
        h2_ref[...] = h_ref[...] + _rms(mo, gp_ref[...])

    f32 = jax.ShapeDtypeStruct((t, D), F32)
    b16 = jax.ShapeDtypeStruct((t, D), BF16)
    return pl.pallas_call(
        body, grid=(t // tm,), name="merge_fwd",
        in_specs=[_row_spec(tm, D), _row_spec(tm, D), _row_spec(tm, 2 * D), _row_spec(tm, D), _WHOLE, _WHOLE, _WHOLE,
                  _vec_spec(D)],
        out_specs=[_row_spec(tm, D)] * 5,
        out_shape=[f32, f32, b16, b16, b16],
        compiler_params=_params("arbitrary"),
    )(yain, o, gate, h1, w_lru, w_att, w_o, gpost)


def _ffn_bwd(dh, x, f, a, b, gpre, gpost, w1g, w3g, w2g, name):
    t = x.shape[0]
    tm = _tile(t, TM_SCAN)

    def body(dh_ref, x_ref, f_ref, a_ref, b_ref, gpre_ref, gpost_ref, w1_ref, w3_ref, w2_ref,
             dx_ref, n_ref, da_ref, db_ref, df_ref, dgpre_ref, dgpost_ref):
        @pl.when(pl.program_id(0) == 0)
        def _():
            dgpre_ref[...] = jnp.zeros_like(dgpre_ref)
            dgpost_ref[...] = jnp.zeros_like(dgpost_ref)

        dhv = dh_ref[...]
        xv = x_ref[...]
        df, dgp = _rms_bwd(0.5 * dhv, f_ref[...], gpost_ref[...])
        dgpost_ref[...] += dgp
        dfb = df.astype(BF16)
        df_ref[...] = dfb
        n_ref[...] = _rms(xv, gpre_ref[...]).astype(BF16)
        dn = jnp.zeros((tm, D), F32)
        for s in range(NSH):
            av = a_ref[s].astype(F32)
            bv = b_ref[s].astype(F32)
            sg = jax.nn.sigmoid(av)
            dhm = _nt(dfb, w2_ref[s])
            dab = (dhm * bv * (sg * (1.0 + av * (1.0 - sg)))).astype(BF16)
            dbb = (dhm * (av * sg)).astype(BF16)
            da_ref[s] = dab
            db_ref[s] = dbb
            dn = dn + _nn(dab, w1_ref[s]) + _nn(dbb, w3_ref[s])
        dxn, dg = _rms_bwd(dn, xv, gpre_ref[...])
        dgpre_ref[...] += dg
        dx_ref[...] = dhv + dxn

    sh = pl.BlockSpec((NSH, tm, FF_S), lambda i: (0, i, 0))
    act = jax.ShapeDtypeStruct((NSH, t, FF_S), BF16)
    vec = jax.ShapeDtypeStruct((1, D), F32)
    return pl.pallas_call(
        body, grid=(t // tm,), name=name,
        in_specs=[_row_spec(tm, D), _row_spec(tm, D), _row_spec(tm, D), sh, sh, _vec_spec(D), _vec_spec(D), _WHOLE, _WHOLE,
                  _WHOLE],
        out_specs=[_row_spec(tm, D), _row_spec(tm, D), sh, sh, _row_spec(tm, D), _vec_spec(D), _vec_spec(D)],
        out_shape=[jax.ShapeDtypeStruct((t, D), F32), jax.ShapeDtypeStruct((t, D), BF16), act, act,
                   jax.ShapeDtypeStruct((t, D), BF16), vec, vec],
        compiler_params=_params("arbitrary"),
    )(dh, x, f, a, b, gpre, gpost, w1g, w3g, w2g)


def _behind(body, after):
    if after is None:
        return body, [], []

    def ordered(_, *refs):
        body(*refs)

    return ordered, [_ANY], [after]


def _ffn_bwd_acts(dh, x, f, a, b, gpre, gpost, w2g, name):
    t = x.shape[0]
    tm = _tile(t)

    def body(dh_ref, x_ref, f_ref, a_ref, b_ref, gpre_ref, gpost_ref, w2_ref, n_ref, da_ref, db_ref, df_ref, dgpost_ref):
        @pl.when(pl.program_id(0) == 0)
        def _():
            dgpost_ref[...] = jnp.zeros_like(dgpost_ref)

        df, dgp = _rms_bwd(0.5 * dh_ref[...], f_ref[...], gpost_ref[...])
        dgpost_ref[...] += dgp
        dfb = df.astype(BF16)
        df_ref[...] = dfb
        n_ref[...] = _rms(x_ref[...], gpre_ref[...]).astype(BF16)
        for s in range(NSH):
            av = a_ref[s].astype(F32)
            bv = b_ref[s].astype(F32)
            sg = jax.nn.sigmoid(av)
            dhm = _nt(dfb, w2_ref[s])
            da_ref[s] = (dhm * bv * (sg * (1.0 + av * (1.0 - sg)))).astype(BF16)
            db_ref[s] = (dhm * (av * sg)).astype(BF16)

    sh = pl.BlockSpec((NSH, tm, FF_S), lambda i: (0, i, 0))
    act = jax.ShapeDtypeStruct((NSH, t, FF_S), BF16)
    b16 = jax.ShapeDtypeStruct((t, D), BF16)
    return pl.pallas_call(
        body, grid=(t // tm,), name=name,
        in_specs=[_row_spec(tm, D), _row_spec(tm, D), _row_spec(tm, D), sh, sh, _vec_spec(D), _vec_spec(D), _WHOLE],
        out_specs=[_row_spec(tm, D), sh, sh, _row_spec(tm, D), _vec_spec(D)],
        out_shape=[b16, act, act, b16, jax.ShapeDtypeStruct((1, D), F32)],
        compiler_params=_params("arbitrary"),
    )(dh, x, f, a, b, gpre, gpost, w2g)


def _ffn_bwd_input(dh, x, da, db, gpre, w1g, w3g, name, after):
    t = x.shape[0]
    tm = _tile(t)

    def body(dh_ref, x_ref, da_ref, db_ref, gpre_ref, w1_ref, w3_ref, dx_ref, dgpre_ref):
        @pl.when(pl.program_id(0) == 0)
        def _():
            dgpre_ref[...] = jnp.zeros_like(dgpre_ref)

        dn = jnp.zeros((tm, D), F32)
        for s in range(NSH):
            dn = dn + _nn(da_ref[s], w1_ref[s]) + _nn(db_ref[s], w3_ref[s])
        dxn, dg = _rms_bwd(dn, x_ref[...], gpre_ref[...])
        dgpre_ref[...] += dg
        dx_ref[...] = dh_ref[...] + dxn

    sh = pl.BlockSpec((NSH, tm, FF_S), lambda i: (0, i, 0))
    body, specs, operands = _behind(body, after)
    return pl.pallas_call(
        body, grid=(t // tm,), name=name,
        in_specs=specs + [_row_spec(tm, D), _row_spec(tm, D), sh, sh, _vec_spec(D), _WHOLE, _WHOLE],
        out_specs=[_row_spec(tm, D), _vec_spec(D)],
        out_shape=[jax.ShapeDtypeStruct((t, D), F32), jax.ShapeDtypeStruct((1, D), F32)],
        compiler_params=_params("arbitrary"),
    )(*operands, dh, x, da, db, gpre, w1g, w3g)


def _wgrad(a, b, a_spec, b_spec, out_spec, out_shape, grid, name, after=None):
    def body(a_ref, b_ref, o_ref):
        o_ref[...] = _tn(a_ref[...], b_ref[...]).astype(BF16)

    body, specs, operands = _behind(body, after)
    return pl.pallas_call(body, grid=grid, name=name, in_specs=specs + [a_spec, b_spec], out_specs=out_spec,
                          out_shape=jax.ShapeDtypeStruct(out_shape, BF16),
                          compiler_params=_params(*("arbitrary",) * len(grid)))(*operands, a, b)


def _wgrad_cols(act, dsh, width, name, after=None):
    t = act.shape[0]
    if dsh.ndim == 3:
        b_spec = pl.BlockSpec((None, t, width), lambda s, k: (s, 0, 0))
    else:
        b_spec = pl.BlockSpec((t, width), lambda s, k: (0, s))
    return _wgrad(act, dsh, pl.BlockSpec((t, 512), lambda s, k: (0, k)), b_spec,
                  pl.BlockSpec((None, 512, width), lambda s, k: (s, k, 0)), (NSH, D, width), (NSH, 2), name, after)


def _wgrad_rows(hm, df, name, after=None):
    t = df.shape[0]
    return _wgrad(hm, df, pl.BlockSpec((None, t, FF_S), lambda s: (s, 0, 0)), pl.BlockSpec((t, D), lambda s: (0, 0)),
                  pl.BlockSpec((None, FF_S, D), lambda s: (s, 0, 0)), (NSH, FF_S, D), (NSH,), name, after)


def _wgrad_sq(a, b, name, after=None):
    t = a.shape[0]
    return _wgrad(a, b, pl.BlockSpec((t, 512), lambda i, j: (0, i)), pl.BlockSpec((t, 512), lambda i, j: (0, j)),
                  pl.BlockSpec((512, 512), lambda i, j: (i, j)), (D, D), (2, 2), name, after)


def _mix_bwd1(dh2, mo, gpost, gate, ya, yb, gx, gdx, hr, w_o, w_lru, w_att, after):
    t = dh2.shape[0]
    tm = _tile(t, TM_SCAN)

    def body(dh_ref, mo_ref, gp_ref, g_ref, ya_ref, yb_ref, gx_ref, gdx_ref, hr_ref, wo_ref, wl_ref, wa_ref,
             dmo_ref, dya_ref, dyb_ref, dgate_ref, dhr_ref, dxg_ref, do_ref, dgp_ref, dbg_ref):
        @pl.when(pl.program_id(0) == 0)
        def _():
            dgp_ref[...] = jnp.zeros_like(dgp_ref)
            dbg_ref[...] = jnp.zeros_like(dbg_ref)

        dmo, dgp = _rms_bwd(dh_ref[...], mo_ref[...], gp_ref[...])
        dgp_ref[...] += dgp
        dmob = dmo.astype(BF16)
        dmo_ref[...] = dmob
        dm = _nt(dmob, wo_ref[...])
        g0 = g_ref[:, 0:D]
        g1 = g_ref[:, D:2 * D]
        dyab = (dm * g0).astype(BF16)
        dybb = (dm * g1).astype(BF16)
        dya_ref[...] = dyab
        dyb_ref[...] = dybb
        dg0 = dm * ya_ref[...].astype(F32)
        dg1 = dm * yb_ref[...].astype(F32)
        dgate_ref[:, 0:D] = dg0.astype(BF16)
        dgate_ref[:, D:2 * D] = dg1.astype(BF16)
        dbg_ref[:, 0:D] += jnp.sum(dg0, axis=0, keepdims=True)
        dbg_ref[:, D:2 * D] += jnp.sum(dg1, axis=0, keepdims=True)
        dyain = _nt(dyab, wl_ref[...])
        do_ref[...] = _nt(dybb, wa_ref[...]).astype(BF16)
        dhr_ref[...] = dyain * gx_ref[...]
        dxg_ref[...] = (dyain * hr_ref[...] * gdx_ref[...]).astype(BF16)

    b16 = jax.ShapeDtypeStruct((t, D), BF16)
    body, specs, operands = _behind(body, after)
    return pl.pallas_call(
        body, grid=(t // tm,), name="mix_bwd1",
        in_specs=specs + [_row_spec(tm, D), _row_spec(tm, D), _vec_spec(D), _row_spec(tm, 2 * D), _row_spec(tm, D),
                          _row_spec(tm, D), _row_spec(tm, D), _row_spec(tm, D), _row_spec(tm, D), _WHOLE, _WHOLE, _WHOLE],
        out_specs=[_row_spec(tm, D), _row_spec(tm, D), _row_spec(tm, D), _row_spec(tm, 2 * D), _row_spec(tm, D),
                   _row_spec(tm, D), _row_spec(tm, D), _vec_spec(D), _vec_spec(2 * D)],
        out_shape=[b16, b16, b16, jax.ShapeDtypeStruct((t, 2 * D), BF16), jax.ShapeDtypeStruct((t, D), F32), b16, b16,
                   jax.ShapeDtypeStruct((1, D), F32), jax.ShapeDtypeStruct((1, 2 * D), F32)],
        compiler_params=_params("arbitrary"),
    )(*operands, dh2, mo, gpost, gate, ya, yb, gx, gdx, hr, w_o, w_lru, w_att)


def _rglru_bwd(dhr, hr, xc, r, ig, a, s, xr, conv_w, wa2, wx2, lam, after):
    t = dhr.shape[0]
    tm = _tile(t, TM_SCAN)
    nb8 = tm // 8
    nt = t // tm

    def body(dhr_ref, hr_ref, hrp_ref, xc_ref, r_ref, ig_ref, a_sc, s_ref, xr_ref, cw_ref, wa_ref, wx_ref, lam_ref,
             dxr_ref, dwa_ref, dwx_ref, dba_ref, dbx_ref, dlam_ref, dcw_ref, dcb_ref,
             ext_h, ext_d, g_sc, c_sc, nxt_sc):
        i = pl.program_id(0)
        first_tile = i == nt - 1

        @pl.when(i == 0)
        def _():
            c_sc[...] = jnp.zeros_like(c_sc)
            nxt_sc[...] = jnp.zeros_like(nxt_sc)
            for ref in (dwa_ref, dwx_ref, dba_ref, dbx_ref, dlam_ref, dcw_ref, dcb_ref):
                ref[...] = jnp.zeros_like(ref)

        lamv = lam_ref[...]
        sp = _softplus_neg(lamv)
        rv = r_ref[...]
        igv = ig_ref[...]
        xcv = xc_ref[...]
        a = a_sc[...]
        s = s_ref[...]

        def blk(jj, c):
            st = pl.multiple_of((nb8 - 1 - jj) * 8, 8)
            d8 = dhr_ref[pl.ds(st, 8), :]
            a8 = a_sc[pl.ds(st, 8), :]
            rows = [None] * 8
            for k in range(7, -1, -1):
                g = d8[k:k + 1, :] + c
                c = a8[k:k + 1, :] * g
                rows[k] = g
            g_sc[pl.ds(st, 8), :] = jnp.concatenate(rows, axis=0)
            return c

        c_sc[0:1, :] = lax.fori_loop(0, nb8, blk, c_sc[0:1, :])
        g = g_sc[...]
        ext_h[0:8, :] = jnp.where(first_tile, 0.0, hrp_ref[...])
        ext_h[8:8 + tm, :] = hr_ref[...]
        hprev = ext_h[pl.ds(7, tm), :]
        d_s = g * (igv * xcv)
        dig = g * s * xcv
        dxc = g * s * igv
        dla = (g * hprev) * a - d_s * ((a * a) / s)
        dr_pre = (dla * (-LRU_C * sp)) * (rv * (1.0 - rv))
        di_pre = dig * (igv * (1.0 - igv))
        dlam_ref[...] += jnp.sum(dla * (LRU_C * rv), axis=0, keepdims=True) * jax.nn.sigmoid(-lamv)
        dba_ref[...] += jnp.sum(dr_pre, axis=0, keepdims=True)
        dbx_ref[...] += jnp.sum(di_pre, axis=0, keepdims=True)
        drb = dr_pre.astype(BF16)
        dib = di_pre.astype(BF16)
        xcb = xcv.astype(BF16)
        ext_d[tm:tm + 8, :] = nxt_sc[...]
        for p in range(8):
            sl = slice(p * 128, (p + 1) * 128)
            ext_d[0:tm, sl] = dxc[:, sl] + _nt(drb[:, sl], wa_ref[p]) + _nt(dib[:, sl], wx_ref[p])
            dwa_ref[p] += _tn(xcb[:, sl], drb[:, sl])
            dwx_ref[p] += _tn(xcb[:, sl], dib[:, sl])
        dxcv = ext_d[0:tm, :]
        nxt_sc[...] = ext_d[0:8, :]
        dcb_ref[...] += jnp.sum(dxcv, axis=0, keepdims=True)
        xrv = xr_ref[...]
        dxr = jnp.zeros((tm, D), F32)
        for tap in range(4):
            ext_h[0:tm, :] = ext_d[pl.ds(3 - tap, tm), :]
            ahead = ext_h[0:tm, :]
            dxr = dxr + ahead * cw_ref[tap:tap + 1, :]
            dcw_ref[tap:tap + 1, :] += jnp.sum(ahead * xrv, axis=0, keepdims=True)
        dxr_ref[...] = dxr.astype(BF16)

    rev = pl.BlockSpec((tm, D), lambda i: (nt - 1 - i, 0))
    prev = pl.BlockSpec((8, D), lambda i: (jnp.maximum((nt - 1 - i) * nb8 - 1, 0), 0))
    full = lambda shape: pl.BlockSpec(shape, lambda i: tuple(0 for _ in shape))
    vec = jax.ShapeDtypeStruct((1, D), F32)
    blocks = jax.ShapeDtypeStruct((8, 128, 128), F32)
    body, specs, operands = _behind(body, after)
    return pl.pallas_call(
        body, grid=(nt,), name="rglru_bwd",
        in_specs=specs + [rev, rev, prev, rev, rev, rev, rev, rev, rev, full((4, D)), full((8, 128, 128)),
                          full((8, 128, 128)), _vec_spec(D)],
        out_specs=[rev, full((8, 128, 128)), full((8, 128, 128)), _vec_spec(D), _vec_spec(D), _vec_spec(D), full((4, D)),
                   _vec_spec(D)],
        out_shape=[jax.ShapeDtypeStruct((t, D), BF16), blocks, blocks, vec, vec, vec, jax.ShapeDtypeStruct((4, D), F32), vec],
        scratch_shapes=[pltpu.VMEM((tm + 8, D), F32), pltpu.VMEM((tm + 8, D), F32),
                        pltpu.VMEM((tm, D), F32), pltpu.VMEM((8, D), F32), pltpu.VMEM((8, D), F32)],
        compiler_params=_params("arbitrary"),
    )(*operands, dhr, hr, hr, xc, r, ig, a, s, xr, conv_w, wa2, wx2, lam)


def _attn_bwd(sink_rows, q, kp, vp, bias_t, mask, do):
    t = q.shape[0]
    tp = kp.shape[0]
    per_step = 8

    def body(sink_ref, q_ref, kp_ref, vp_ref, bias_ref, mask_ref, do_ref, dq_ref, dk_ref, dv_ref, dbias_ref, ds_ref):
        @pl.when(pl.program_id(0) == 0)
        def _():
            for ref in (dk_ref, dv_ref, dbias_ref, ds_ref):
                ref[...] = jnp.zeros_like(ref)

        maskv = mask_ref[...]
        lane_group = lax.broadcasted_iota(jnp.int32, (1, 4 * HEAD_DIM), 1) // HEAD_DIM

        def own_blocks(full):
            out = full[0:KP]
            for g in range(1, 4):
                out = jnp.where(lane_group == g, full[g * KP:(g + 1) * KP], out)
            return out

        dsc_sum, dsinks, dks, dvs = 0.0, [0.0] * 4, [], []
        for k in range(per_step):
            c = pl.program_id(0) * per_step + k
            chunk = slice(k * CHUNK, (k + 1) * CHUNK)
            st = pl.multiple_of(c * CHUNK, CHUNK)
            kbd = _block_diag(kp_ref[pl.ds(st, KP), :], maskv)
            vbd = _block_diag(vp_ref[pl.ds(st, KP), :], maskv)
            q_all = _stack_heads(q_ref[chunk, :])
            do_all = _stack_heads(do_ref[chunk, :])
            valid = lax.broadcasted_iota(jnp.int32, (KP, 1), 0) + c * CHUNK >= PAD_KEYS
            qk = _nt(kbd, q_all)
            dp = _nt(vbd, do_all)
            ps, dscs = [], []
            for g in range(4):
                rows = slice(g * KP, (g + 1) * KP)
                p, sink_p = _group_softmax(qk[rows], bias_ref[rows, :], sink_ref[g:g + 1, :], valid)
                delta = jnp.sum(p * dp[rows], axis=0, keepdims=True)
                ps.append(p)
                dscs.append(p * (dp[rows] - delta))
                dsinks[g] = dsinks[g] - sink_p * delta
            dsc = jnp.concatenate(dscs, axis=0)
            dsc_sum = dsc_sum + dsc
            dsb = (dsc * (HEAD_DIM ** -0.5)).astype(BF16)
            dq_ref[chunk, :] = _unstack_heads(_tn(dsb, kbd)).astype(BF16)
            dks.append((st, own_blocks(_nn(dsb, q_all))))
            dvs.append((st, own_blocks(_nn(jnp.concatenate(ps, axis=0).astype(BF16), do_all))))
        dbias_ref[...] += dsc_sum
        for g in range(4):
            ds_ref[g:g + 1, :] += dsinks[g]
        for (st, dkw), (_, dvw) in zip(dks, dvs):
            dk_ref[pl.ds(st, KP), :] += dkw
            dv_ref[pl.ds(st, KP), :] += dvw

    full = lambda shape: pl.BlockSpec(shape, lambda i: tuple(0 for _ in shape))
    return pl.pallas_call(
        body, grid=(t // (per_step * CHUNK),), name="attn_bwd",
        in_specs=[_WHOLE, _row_spec(per_step * CHUNK, D), _WHOLE, _WHOLE, _WHOLE, _WHOLE, _row_spec(per_step * CHUNK, D)],
        out_specs=[_row_spec(per_step * CHUNK, D), full((tp, KV_W)), full((tp, KV_W)), full((4 * KP, 4 * CHUNK)),
                   full((8, 4 * CHUNK))],
        out_shape=[jax.ShapeDtypeStruct((t, D), BF16), jax.ShapeDtypeStruct((tp, KV_W), F32),
                   jax.ShapeDtypeStruct((tp, KV_W), F32), jax.ShapeDtypeStruct((4 * KP, 4 * CHUNK), F32),
                   jax.ShapeDtypeStruct((8, 4 * CHUNK), F32)],
        compiler_params=_params("arbitrary"),
    )(sink_rows, q, kp, vp, bias_t, mask, do)


def _mix_bwd2(dproj, dgate, h1, dh2, gmix, w_in_g, w_gate_g, after):
    t = h1.shape[0]
    tm = _tile(t)

    def body(dp_ref, dg_ref, h_ref, dh_ref, g_ref, win_ref, wg_ref, dh1_ref, dgm_ref):
        @pl.when(pl.program_id(0) == 0)
        def _():
            dgm_ref[...] = jnp.zeros_like(dgm_ref)

        du = jnp.zeros((tm, D), F32)
        for s in range(NSH):
            du = du + _nt(dp_ref[:, s * IN_S:(s + 1) * IN_S], win_ref[s])
            du = du + _nt(dg_ref[:, s * GATE_S:(s + 1) * GATE_S], wg_ref[s])
        dxn, dg = _rms_bwd(du, h_ref[...], g_ref[...])
        dgm_ref[...] += dg
        dh1_ref[...] = dh_ref[...] + dxn

    body, specs, operands = _behind(body, after)
    return pl.pallas_call(
        body, grid=(t // tm,), name="mix_bwd2",
        in_specs=specs + [_row_spec(tm, NSH * IN_S), _row_spec(tm, 2 * D), _row_spec(tm, D), _row_spec(tm, D), _vec_spec(D),
                          _WHOLE, _WHOLE],
        out_specs=[_row_spec(tm, D), _vec_spec(D)],
        out_shape=[jax.ShapeDtypeStruct((t, D), F32), jax.ShapeDtypeStruct((1, D), F32)],
        compiler_params=_params("arbitrary"),
    )(*operands, dproj, dgate, h1, dh2, gmix, w_in_g, w_gate_g)


def _band_onehot():
    nb = N_BUCKETS // 2
    max_exact = nb // 2
    rel = jnp.arange(KB)[None, :] - PAD_KEYS - jnp.arange(CHUNK)[:, None]
    ret = jnp.where(rel > 0, nb, 0)
    n = jnp.abs(rel)
    nf = jnp.maximum(n, 1).astype(jnp.float32)
    large = max_exact + (jnp.log(nf / max_exact) / math.log(128 / max_exact) * (nb - max_exact)).astype(jnp.int32)
    large = jnp.minimum(large, nb - 1)
    buckets = (ret + jnp.where(n < max_exact, n, large)).reshape(1, CHUNK * KB)
    return (buckets == jnp.arange(N_BUCKETS)[:, None]).astype(F32)


def _pair_blocks(w):
    pairs = w.reshape(8, 2, 64, 64)
    z = jnp.zeros((8, 64, 64), w.dtype)
    return jnp.concatenate([jnp.concatenate([pairs[:, 0], z], axis=2), jnp.concatenate([z, pairs[:, 1]], axis=2)], axis=1)


def _unpair_blocks(w2):
    return jnp.stack([w2[:, 0:64, 0:64], w2[:, 64:128, 64:128]], axis=1).reshape(16, 64, 64)


def _local_step(x, target, weights, sm, reducer):
    row = lambda v: v.reshape(1, -1)
    onehot_t = _band_onehot()
    bias = _bias_fwd(sm["rel_bias"].T, onehot_t).reshape(4, 4, CHUNK, KB)
    bias_t = jnp.pad(jnp.transpose(bias, (0, 3, 1, 2)), ((0, 0), (0, KP - KB), (0, 0), (0, 0))).reshape(4 * KP, 4 * CHUNK)
    sink_rows = jnp.pad(jnp.repeat(sm["attn_sinks"].reshape(4, 4), CHUNK, axis=1), ((0, 4), (0, 0)))
    grp = jnp.arange(4 * KP)[:, None] // KP == jnp.arange(4 * HEAD_DIM)[None, :] // HEAD_DIM
    mask = (grp & (jnp.arange(4 * KP)[:, None] % KP < KB)).astype(BF16)
    wa2 = _pair_blocks(sm["rg_a_w"]).astype(BF16)
    wx2 = _pair_blocks(sm["rg_x_w"]).astype(BF16)
    wg = dict(weights("ffn1_up", [bias_t, sink_rows, mask, wa2, wx2]))
    sm = dict(sm, conv_w=wg["conv_w"])

    a1, b1, hm1 = _ffn_up(x, row(sm["ffn1_pre_g"]), wg["ffn1_w1"], wg["ffn1_w3"], "ffn1_up")
    wg.update(weights("ffn1_down", hm1))
    h1, f1 = _ffn_down(x, hm1, wg["ffn1_w2"], row(sm["ffn1_post_g"]), "ffn1_down")
    wg.update(weights("mix_in", h1))
    u, q, k, v, xr, gx, gdx, gate = _mix_proj(h1, row(sm["mix_pre_g"]), wg["w_in"], wg["w_gate"], row(sm["b_gate"]))
    token = weights("mix_out", u, begin=True)
    hr, yain, xc, r, ig, lru_a, lru_s = _rglru_fwd(xr, gx, sm["conv_w"], row(sm["conv_b"]), wa2, row(sm["rg_a_b"]), wx2,
                                                   row(sm["rg_x_b"]), row(sm["lru_lambda"]), token)
    token = weights("ffn2", hr, begin=True)
    kp = jnp.pad(k, ((PAD_KEYS, KP - KB), (0, 0)))
    vp = jnp.pad(v, ((PAD_KEYS, KP - KB), (0, 0)))
    o = _attn_fwd(sink_rows, q, kp, vp, bias_t, mask, token)
    wg.update(weights("mix_out", o))
    w_lru = wg["w_lru_out"].reshape(D, D)
    w_att = wg["w_attn_out"].reshape(D, D)
    w_o = wg["w_o"].reshape(D, D)
    wg.update(weights("ffn2", o))
    h2, mo, merged, ya, yb = _merge_fwd(yain, o, gate, h1, w_lru, w_att, w_o, row(sm["mix_post_g"]))
    dy, a2, b2, hm2, f2, sq = _ffn_fwd(h2, row(sm["ffn2_pre_g"]), wg["ffn2_w1"], wg["ffn2_w3"], wg["ffn2_w2"],
                                       row(sm["ffn2_post_g"]), "ffn2_fwd", target)

    big, small = {}, {}
    dh2, n2, da2, db2, df2, small["ffn2_pre_g"], small["ffn2_post_g"] = _ffn_bwd(
        dy, h2, f2, a2, b2, row(sm["ffn2_pre_g"]), row(sm["ffn2_post_g"]), wg["ffn2_w1"], wg["ffn2_w3"], wg["ffn2_w2"],
        "ffn2_bwd")
    big["ffn2_w1"] = _wgrad_rows(da2, n2, "dw_ffn2_w1")
    big["ffn2_w3"] = _wgrad_rows(db2, n2, "dw_ffn2_w3")
    big["ffn2_w2"] = _wgrad_rows(hm2, df2, "dw_ffn2_w2")
    token = reducer.begin("ffn2", {n: big[n] for n in ("ffn2_w1", "ffn2_w3", "ffn2_w2")})
    dmo, dya, dyb, dgate, dhr, dxg, do, small["mix_post_g"], small["b_gate"] = _mix_bwd1(
        dh2, mo, row(sm["mix_post_g"]), gate, ya, yb, gx, gdx, hr, w_o, w_lru, w_att, token)
    big["w_o"] = _wgrad_sq(merged, dmo, "dw_w_o").reshape(NSH, D // NSH, D)
    big["w_lru_out"] = _wgrad_sq(yain, dya, "dw_w_lru_out").reshape(NSH, D // NSH, D)
    big["w_attn_out"] = _wgrad_sq(o, dyb, "dw_w_attn_out").reshape(NSH, D // NSH, D)
    token = reducer.advance("ffn2", big["w_attn_out"])
    (dxr, dwa2, dwx2, small["rg_a_b"], small["rg_x_b"], small["lru_lambda"], small["conv_w"], small["conv_b"]) = _rglru_bwd(
        dhr, hr, xc, r, ig, lru_a, lru_s, xr, sm["conv_w"], wa2, wx2, row(sm["lru_lambda"]), token)
    small["rg_a_w"] = _unpair_blocks(dwa2)
    small["rg_x_w"] = _unpair_blocks(dwx2)
    dq, dkp, dvp, dbias_t, ds_rows = _attn_bwd(sink_rows, q, kp, vp, bias_t, mask, do)
    dbias = jnp.transpose(dbias_t.reshape(4, KP, 4, CHUNK)[:, :KB], (0, 2, 3, 1)).reshape(N_HEADS, CHUNK * KB)
    drel_t, dsinks = _bias_bwd(dbias, onehot_t, ds_rows)
    small["attn_sinks"] = dsinks[0:4, 0:4].reshape(N_HEADS)
    small["rel_bias"] = drel_t.T
    t = x.shape[0]
    dproj = jnp.concatenate([dq, dkp[PAD_KEYS:PAD_KEYS + t].astype(BF16), dvp[PAD_KEYS:PAD_KEYS + t].astype(BF16), dxr, dxg],
                            axis=1)
    big["w_in"] = _wgrad_cols(u, dproj, IN_S, "dw_w_in")
    big["w_gate"] = _wgrad_cols(u, dgate, GATE_S, "dw_w_gate")
    token = reducer.begin("mix", {n: big[n] for n in ("w_in", "w_gate", "w_lru_out", "w_attn_out", "w_o")})
    dh1, small["mix_pre_g"] = _mix_bwd2(dproj, dgate, h1, dh2, row(sm["mix_pre_g"]), wg["w_in"], wg["w_gate"], token)
    n1, da1, db1, df1, small["ffn1_post_g"] = _ffn_bwd_acts(
        dh1, x, f1, a1, b1, row(sm["ffn1_pre_g"]), row(sm["ffn1_post_g"]), wg["ffn1_w2"], "ffn1_bwd_acts")
    token = reducer.advance("mix", df1)
    big["ffn1_w1"] = _wgrad_rows(da1, n1, "dw_ffn1_w1", token)
    big["ffn1_w3"] = _wgrad_rows(db1, n1, "dw_ffn1_w3", token)
    big["ffn1_w2"] = _wgrad_rows(hm1, df1, "dw_ffn1_w2", token)
    token = reducer.begin("ffn1", {n: big[n] for n in ("ffn1_w1", "ffn1_w3", "ffn1_w2")})
    dx, small["ffn1_pre_g"] = _ffn_bwd_input(dh1, x, da1, db1, row(sm["ffn1_pre_g"]), wg["ffn1_w1"], wg["ffn1_w3"],
                                             "ffn1_bwd_input", token)
    return sq, dx, big, small


_ANY = pl.BlockSpec(memory_space=pl.ANY)


def _place():
    return lax.axis_index("x"), lax.axis_index("y"), lax.axis_index("c")


def _other_chips(x, y):
    return [(1 - x, y), (x, 1 - y), (1 - x, 1 - y)]


_HBM = pl.BlockSpec(memory_space=pltpu.HBM)
_SEM = pl.BlockSpec(memory_space=pltpu.SEMAPHORE)
_EFFECT = pltpu.SideEffectType.DATAFLOW_SIDE_EFFECTING


def _cast_into_slot(w, chip, name, after=None):
    r, cc = w.shape
    rows = r // 4

    def body(chip_ref, *refs):
        w_ref, o_ref = refs[-2:]
        o_ref[...] = w_ref[...].astype(BF16)

    extra = [] if after is None else [after]
    return pl.pallas_call(
        body, name=name, out_shape=jax.ShapeDtypeStruct((NSH, r, cc), BF16),
        grid_spec=pltpu.PrefetchScalarGridSpec(
            num_scalar_prefetch=1, grid=(4,), in_specs=[_ANY] * len(extra) + [pl.BlockSpec((rows, cc), lambda i, chip: (i, 0))],
            out_specs=pl.BlockSpec((None, rows, cc), lambda i, chip: (chip[0], i, 0))),
        compiler_params=_params("arbitrary"))(chip, *extra, w)


def _piece(ref, slot, c):
    if ref.dtype == F32:
        return ref.at[slot]
    rh = ref.shape[1] // 2
    return ref.at[slot, pl.ds(pl.multiple_of(c * rh, 16), rh), :]


def _gather_start(stages, name):
    flat = [b for stage in stages for b in stage]
    n, ns = len(flat), len(stages)

    def body(*refs):
        ins, sems, token = refs[:n], refs[n:n + 2 * ns], refs[-1]
        x, y, c = _place()
        me = 2 * x + y
        k = 0
        for s, stage in enumerate(stages):
            for i in range(len(stage)):
                for j, (px, py) in enumerate(_other_chips(x, y)):
                    piece = _piece(ins[k], me, c)
                    pltpu.make_async_remote_copy(src_ref=piece, dst_ref=piece, send_sem=sems[2 * s].at[3 * i + j],
                                                 recv_sem=sems[2 * s + 1].at[3 * i + j], device_id=(px, py, c),
                                                 device_id_type=MESH).start()
                k += 1
        token[...] = jnp.zeros_like(token)

    sem_shapes = [pltpu.SemaphoreType.DMA((3 * len(stage),)) for stage in stages for _ in range(2)]
    outs = pl.pallas_call(
        body, name=name, in_specs=[_HBM] * n,
        out_specs=[_SEM] * (2 * ns) + [_HBM] * n + [pl.BlockSpec(memory_space=pltpu.VMEM)],
        out_shape=sem_shapes + [pltpu.HBM(b.shape, b.dtype) for b in flat] + [jax.ShapeDtypeStruct((8, 128), F32)],
        input_output_aliases={i: 2 * ns + i for i in range(n)},
        compiler_params=pltpu.CompilerParams(has_side_effects=_EFFECT),
    )(*[pltpu.with_memory_space_constraint(b, pltpu.HBM) for b in flat])
    sems, bufs, token = outs[:2 * ns], list(outs[2 * ns:2 * ns + n]), outs[-1]
    per_stage, k = [], 0
    for s, stage in enumerate(stages):
        per_stage.append((sems[2 * s], sems[2 * s + 1], bufs[k:k + len(stage)]))
        k += len(stage)
    return per_stage, token


def _gather_wait(send_sems, recv_sems, bufs, after, name):
    n = len(bufs)

    def body(*refs):
        ins, ssem, rsem = refs[:n], refs[n], refs[n + 1]
        x, y, c = _place()
        me = 2 * x + y
        for i in range(n):
            for j, (px, py) in enumerate(_other_chips(x, y)):
                cp = pltpu.make_async_remote_copy(src_ref=_piece(ins[i], me, c), dst_ref=_piece(ins[i], 2 * px + py, c),
                                                  send_sem=ssem.at[3 * i + j], recv_sem=rsem.at[3 * i + j],
                                                  device_id=(px, py, c), device_id_type=MESH)
                cp.wait_send()
                cp.wait_recv()

    afters = list(after) if isinstance(after, (list, tuple)) else [after]
    return pl.pallas_call(
        body, name=name, in_specs=[_HBM] * n + [_SEM, _SEM] + [_ANY] * len(afters), out_specs=[_HBM] * n,
        out_shape=[pltpu.HBM(b.shape, b.dtype) for b in bufs], input_output_aliases={i: i for i in range(n)},
        compiler_params=pltpu.CompilerParams(has_side_effects=_EFFECT),
    )(*bufs, send_sems, recv_sems, *afters)


def _sibling_fill(bufs, name):
    n = len(bufs)

    def body(*refs):
        ins, outs = refs[:n], refs[n:2 * n]
        send_sems, recv_sems = refs[2 * n:]
        x, y, c = _place()
        copies = []
        for i in range(n):
            for j, (px, py) in enumerate(_other_chips(x, y)):
                copies.append(pltpu.make_async_remote_copy(
                    src_ref=_piece(ins[i], 2 * px + py, c), dst_ref=_piece(outs[i], 2 * px + py, c),
                    send_sem=send_sems.at[3 * i + j], recv_sem=recv_sems.at[3 * i + j], device_id=(x, y, 1 - c),
                    device_id_type=MESH))
                copies[-1].start()
        for cp in copies:
            cp.wait()

    return pl.pallas_call(
        body, name=name, in_specs=[_ANY] * n, out_specs=[_ANY] * n,
        out_shape=[jax.ShapeDtypeStruct(b.shape, b.dtype) for b in bufs], input_output_aliases={i: i for i in range(n)},
        scratch_shapes=[pltpu.SemaphoreType.DMA((3 * n,)), pltpu.SemaphoreType.DMA((3 * n,))],
        compiler_params=pltpu.CompilerParams(has_side_effects=True),
    )(*bufs)


def _swap_plan(srcs, lands):
    x, y, c = _place()
    plan = []
    for src, land in zip(srcs, lands):
        rh = src.shape[1] // 2
        plan.append((src.at[:, pl.ds(pl.multiple_of((1 - c) * rh, 16), rh), :], land, (x, y, 1 - c)))
    return plan


def _owners_plan(srcs, lands):
    x, y, c = _place()
    return [(src.at[2 * px + py], land.at[j], (px, py, c))
            for src, land in zip(srcs, lands) for j, (px, py) in enumerate(_other_chips(x, y))]


def _exchange_start(srcs, lands, plan, copies, name):
    n, m = len(srcs), len(srcs) + len(lands)

    def body(*refs):
        send_sems, recv_sems, token = refs[m], refs[m + 1], refs[-1]
        for k, (src, dst, dev) in enumerate(plan(refs[:n], refs[n:m])):
            pltpu.make_async_remote_copy(src_ref=src, dst_ref=dst, send_sem=send_sems.at[k], recv_sem=recv_sems.at[k],
                                         device_id=dev, device_id_type=MESH).start()
        token[...] = jnp.zeros_like(token)

    both = list(srcs) + list(lands)
    outs = pl.pallas_call(
        body, name=name, in_specs=[_HBM] * m,
        out_specs=[_SEM, _SEM] + [_HBM] * m + [pl.BlockSpec(memory_space=pltpu.VMEM)],
        out_shape=[pltpu.SemaphoreType.DMA((copies,)), pltpu.SemaphoreType.DMA((copies,))]
        + [pltpu.HBM(b.shape, b.dtype) for b in both] + [jax.ShapeDtypeStruct((8, 128), F32)],
        input_output_aliases={i: 2 + i for i in range(m)},
        compiler_params=pltpu.CompilerParams(has_side_effects=_EFFECT),
    )(*[pltpu.with_memory_space_constraint(b, pltpu.HBM) for b in both])
    return (outs[0], outs[1]), list(outs[2:2 + n]), list(outs[2 + n:2 + m]), outs[-1]


def _exchange_wait(sems, srcs, lands, plan, after, name):
    n, m = len(srcs), len(srcs) + len(lands)

    def body(*refs):
        send_sems, recv_sems = refs[m], refs[m + 1]
        for k, (src, dst, dev) in enumerate(plan(refs[:n], refs[n:m])):
            cp = pltpu.make_async_remote_copy(src_ref=src, dst_ref=dst, send_sem=send_sems.at[k], recv_sem=recv_sems.at[k],
                                              device_id=dev, device_id_type=MESH)
            cp.wait_send()
            cp.wait_recv()

    both = list(srcs) + list(lands)
    afters = list(after) if isinstance(after, (list, tuple)) else [after]
    outs = pl.pallas_call(
        body, name=name, in_specs=[_HBM] * m + [_SEM, _SEM] + [_ANY] * len(afters), out_specs=[_HBM] * m,
        out_shape=[pltpu.HBM(b.shape, b.dtype) for b in both], input_output_aliases={i: i for i in range(m)},
        compiler_params=pltpu.CompilerParams(has_side_effects=_EFFECT),
    )(*both, sems[0], sems[1], *afters)
    return list(outs[:n]), list(outs[n:])


def _fill_plan(bufs, _):
    x, y, c = _place()
    return [(_piece(buf, 2 * px + py, c), _piece(buf, 2 * px + py, c), (x, y, 1 - c))
            for buf in bufs for px, py in _other_chips(x, y)]


class _Reducer:
    def __init__(self, where):
        self.state = {}
        self.where = where

    def begin(self, stage, grads):
        names = list(grads)
        full = [grads[n] for n in names]
        lands = [lax.empty((NSH, g.shape[1] // 2, g.shape[2]), g.dtype) for g in full]
        sems, full, lands, token = _exchange_start(full, lands, _swap_plan, len(full), "swap_start_" + stage)
        self.state[stage] = (names, sems, full, lands)
        return token

    def advance(self, stage, after):
        names, sems, full, lands = self.state[stage]
        full, got = _exchange_wait(sems, full, lands, _swap_plan, after, "swap_wait_" + stage)
        sums, own = _chip_sums(full, got, self.where, "chip_sums_" + stage)
        lands = [lax.empty((3,) + s.shape[1:], BF16) for s in sums]
        sems, sent, lands, token = _exchange_start(sums, lands, _owners_plan, 3 * len(sums), "owners_start_" + stage)
        self.state[stage] = (names, own, sems, sent, lands)
        return token

    def finish(self, stage, after):
        names, own, sems, sent, lands = self.state[stage]
        _, got = _exchange_wait(sems, sent, lands, _owners_plan, after, "owners_wait_" + stage)
        return dict(zip(names, _owner_sums(own, got, "owner_sums_" + stage)))


def _chip_sums(gs, gots, where, name):
    n = len(gs)

    def body(where_ref, *refs):
        g_refs, got_refs, hb_refs, own_refs = (refs[k * n:(k + 1) * n] for k in range(4))
        mine = pl.program_id(0) == where_ref[1]
        for g_ref, got_ref, hb_ref, own_ref in zip(g_refs, got_refs, hb_refs, own_refs):
            h = g_ref[...].astype(F32) + got_ref[...].astype(F32)
            hb_ref[...] = h.astype(BF16)

            @pl.when(mine)
            def _():
                own_ref[...] = h

    halves = [(g.shape[1] // 2, g.shape[2]) for g in gs]
    slot = [pl.BlockSpec((None, rh, cc), lambda s, where: (s, 0, 0)) for rh, cc in halves]
    outs = pl.pallas_call(
        body, name=name,
        grid_spec=pltpu.PrefetchScalarGridSpec(
            num_scalar_prefetch=1, grid=(NSH,),
            in_specs=[pl.BlockSpec((None, rh, cc), lambda s, where: (s, where[0], 0)) for rh, cc in halves] + slot,
            out_specs=slot + [pl.BlockSpec((rh, cc), lambda s, where: (0, 0)) for rh, cc in halves]),
        out_shape=[jax.ShapeDtypeStruct((NSH, rh, cc), BF16) for rh, cc in halves]
        + [jax.ShapeDtypeStruct((rh, cc), F32) for rh, cc in halves],
        compiler_params=_params("arbitrary"),
    )(where, *gs, *gots)
    return list(outs[:n]), list(outs[n:])


def _owner_sums(owns, gots, name):
    n = len(owns)

    def body(*refs):
        own_refs, got_refs, o_refs = (refs[k * n:(k + 1) * n] for k in range(3))
        for own_ref, got_ref, o_ref in zip(own_refs, got_refs, o_refs):
            o_ref[...] = ((own_ref[...] + got_ref[0].astype(F32)) + got_ref[1].astype(F32)) + got_ref[2].astype(F32)

    blocks = [(o.shape[0] // 2, o.shape[1]) for o in owns]
    rows = [pl.BlockSpec(b, lambda i: (i, 0)) for b in blocks]
    return pl.pallas_call(
        body, grid=(2,), name=name,
        in_specs=rows + [pl.BlockSpec((3,) + b, lambda i: (0, i, 0)) for b in blocks], out_specs=rows,
        out_shape=[jax.ShapeDtypeStruct(o.shape, F32) for o in owns], compiler_params=_params("arbitrary"),
    )(*owns, *gots)


def _sibling_plan(srcs, lands):
    x, y, c = _place()
    return [(src, land, (x, y, 1 - c)) for src, land in zip(srcs, lands)]


def _all_reduce_small(part):
    def body(p_ref, o_ref, rbuf, send1, recv1, send2, recv2):
        x, y, c = _place()
        me = 4 * x + 2 * y + c
        peers = []
        for k in range(1, 8):
            px, py, pc = x ^ ((k >> 2) & 1), y ^ ((k >> 1) & 1), c ^ (k & 1)
            peers.append((k, (px, py, pc), 4 * px + 2 * py + pc))

        def rows(d):
            return pl.ds(pl.multiple_of(d * SMALL_SLICE, 8), SMALL_SLICE)

        first = [pltpu.make_async_remote_copy(src_ref=p_ref.at[rows(idx), :], dst_ref=rbuf.at[me], send_sem=send1.at[k],
                                              recv_sem=recv1.at[k], device_id=dev, device_id_type=MESH)
                 for k, dev, idx in peers]
        for cp in first:
            cp.start()
        rbuf[me] = p_ref[rows(me), :]
        for k, dev, idx in peers:
            pltpu.make_async_remote_copy(src_ref=p_ref.at[rows(idx), :], dst_ref=rbuf.at[idx], send_sem=send1.at[k],
                                         recv_sem=recv1.at[k], device_id=dev, device_id_type=MESH).wait_recv()
        acc = rbuf[0]
        for d in range(1, 8):
            acc = acc + rbuf[d]
        o_ref[rows(me), :] = acc
        second = [pltpu.make_async_remote_copy(src_ref=o_ref.at[rows(me), :], dst_ref=o_ref.at[rows(me), :],
                                               send_sem=send2.at[k], recv_sem=recv2.at[k], device_id=dev, device_id_type=MESH)
                  for k, dev, idx in peers]
        for cp in second:
            cp.start()
        for k, dev, idx in peers:
            pltpu.make_async_remote_copy(src_ref=o_ref.at[rows(me), :], dst_ref=o_ref.at[rows(idx), :], send_sem=send2.at[k],
                                         recv_sem=recv2.at[k], device_id=dev, device_id_type=MESH).wait_recv()
        for cp in first + second:
            cp.wait_send()

    return pl.pallas_call(
        body, name="all_reduce_small", in_specs=[_WHOLE], out_specs=_WHOLE,
        out_shape=jax.ShapeDtypeStruct((SMALL_ROWS, 128), F32),
        scratch_shapes=[pltpu.VMEM((8, SMALL_SLICE, 128), F32)] + [pltpu.SemaphoreType.DMA((8,))] * 4,
        compiler_params=pltpu.CompilerParams(has_side_effects=True),
    )(part)


def _adamw_update(w, gv, m, v):
    nm = ADAM_B1 * m + (1.0 - ADAM_B1) * gv
    nv = ADAM_B2 * v + (1.0 - ADAM_B2) * (gv * gv)
    m_hat = nm / (1.0 - ADAM_B1 ** ADAM_STEP)
    v_hat = nv / (1.0 - ADAM_B2 ** ADAM_STEP)
    return -ADAM_LR * (m_hat / (jnp.sqrt(v_hat) + ADAM_EPS) + ADAM_WD * w), nm, nv


def _adamw_small(ws, gs, ms, vs, after):
    n = len(ws)

    def body(*refs):
        w_refs, g_refs, m_refs, v_refs, d_refs, nm_refs, nv_refs = (refs[k * n:(k + 1) * n] for k in range(7))
        for i in range(n):
            d_refs[i][...], nm_refs[i][...], nv_refs[i][...] = _adamw_update(
                w_refs[i][...], g_refs[i][...], m_refs[i][...], v_refs[i][...])

    out = [jax.ShapeDtypeStruct(w.shape, F32) for w in ws]
    body, specs, operands = _behind(body, after)
    outs = pl.pallas_call(body, in_specs=specs + [_WHOLE] * (4 * n), out_specs=[_WHOLE] * (3 * n), out_shape=out * 3,
                          name="adamw_small", compiler_params=_params())(*operands, *ws, *gs, *ms, *vs)
    return outs[:n], outs[n:2 * n], outs[2 * n:]


def _adamw_halves(ws, mines, theirs, ms, vs, name):
    n = len(ws)
    steps = 2

    def body(*refs):
        w_refs, mine_refs, theirs_refs, m_refs, v_refs, g_refs, d_refs, nm_refs, nv_refs = (
            refs[k * n:(k + 1) * n] for k in range(9))
        is_mine = pl.program_id(0) == lax.axis_index("c")
        for i in range(n):
            gv = jnp.where(is_mine, mine_refs[i][...], theirs_refs[i][...])
            g_refs[i][...] = gv
            d_refs[i][...], nm_refs[i][...], nv_refs[i][...] = _adamw_update(w_refs[i][...], gv, m_refs[i][...], v_refs[i][...])

    blocks = [(h.shape[0] // steps, h.shape[1]) for h in mines]
    whole = [pl.BlockSpec(b, lambda h, i: (steps * h + i, 0)) for b in blocks]
    half = [pl.BlockSpec(b, lambda h, i: (i, 0)) for b in blocks]
    out = [jax.ShapeDtypeStruct(w.shape, F32) for w in ws]
    outs = pl.pallas_call(body, grid=(2, steps), in_specs=whole + half + half + whole + whole, out_specs=whole * 4,
                          out_shape=out * 4, name=name, compiler_params=_params("arbitrary", "arbitrary"),
                          )(*ws, *mines, *theirs, *ms, *vs)
    return [tuple(outs[k * n + i] for k in range(4)) for i in range(n)]


SMALL_USED = sum(size for _, size in SMALL) // 128


def _pack_small(vals, tail=None):
    parts = []
    for name, size in SMALL:
        flat = vals[name].reshape(-1).astype(F32)
        parts.append(jnp.pad(flat, (0, size - flat.shape[0])))
    if tail is not None:
        parts.append(tail.reshape(128))
    flat = jnp.concatenate(parts)
    return jnp.pad(flat, (0, SMALL_ROWS * 128 - flat.shape[0])).reshape(SMALL_ROWS, 128)


def _unpack_small(packed, shapes):
    flat = packed.reshape(-1)
    out, off = {}, 0
    for name, size in SMALL:
        n = math.prod(shapes[name])
        out[name] = flat[off:off + n].reshape(shapes[name])
        off += size
    return out


def kernel(x, ffn1_pre_g, ffn1_w1, ffn1_w3, ffn1_w2, ffn1_post_g, mix_pre_g, w_in, conv_w, conv_b, rg_a_w, rg_a_b, rg_x_w, rg_x_b, lru_lambda, w_lru_out, attn_sinks, rel_bias, w_attn_out, w_gate, b_gate, w_o, mix_post_g, ffn2_pre_g, ffn2_w1, ffn2_w3, ffn2_w2, ffn2_post_g, loss_target, m_ffn1_pre_g, m_ffn1_w1, m_ffn1_w3, m_ffn1_w2, m_ffn1_post_g, m_mix_pre_g, m_w_in, m_conv_w, m_conv_b, m_rg_a_w, m_rg_a_b, m_rg_x_w, m_rg_x_b, m_lru_lambda, m_w_lru_out, m_attn_sinks, m_rel_bias, m_w_attn_out, m_w_gate, m_b_gate, m_w_o, m_mix_post_g, m_ffn2_pre_g, m_ffn2_w1, m_ffn2_w3, m_ffn2_w2, m_ffn2_post_g, v_ffn1_pre_g, v_ffn1_w1, v_ffn1_w3, v_ffn1_w2, v_ffn1_post_g, v_mix_pre_g, v_w_in, v_conv_w, v_conv_b, v_rg_a_w, v_rg_a_b, v_rg_x_w, v_rg_x_b, v_lru_lambda, v_w_lru_out, v_attn_sinks, v_rel_bias, v_w_attn_out, v_w_gate, v_b_gate, v_w_o, v_mix_post_g, v_ffn2_pre_g, v_ffn2_w1, v_ffn2_w3, v_ffn2_w2, v_ffn2_post_g):
    given = dict(locals())
    chip = 2 * lax.axis_index("x") + lax.axis_index("y")
    transposed = ("ffn1_w1", "ffn1_w3", "ffn2_w1", "ffn2_w3")

    def shard(name, moment=""):
        w = given[moment + name][0]
        return w.T if name in transposed else w

    def unshard(name, w):
        return (w.T if name in transposed else w)[None]

    def only_my_columns(a):
        parts = a.reshape(1, 4, NSH, D // NSH)
        return sum(jnp.where(chip == s, parts[:, :, s], 0.0) for s in range(NSH))

    chip_arr = jnp.reshape(chip, (1,)).astype(jnp.int32)
    stage_names = {"ffn1_up": ["ffn1_w1", "ffn1_w3", "conv_w"],
                   "ffn1_down": ["ffn1_w2"],
                   "mix_in": ["w_in", "w_gate"],
                   "mix_out": ["w_lru_out", "w_attn_out", "w_o"],
                   "ffn2": ["ffn2_w1", "ffn2_w3", "ffn2_w2"]}
    in_flight, started = {}, None
    for stage, names in stage_names.items():
        bufs = [jnp.where(lax.broadcasted_iota(jnp.int32, (NSH, 4, D // NSH), 0) == chip, given[n], 0.0) if n == "conv_w"
                else _cast_into_slot(shard(n), chip_arr, "cast_" + n, started) for n in names]
        (in_flight[stage],), started = _gather_start([bufs], "gather_start_" + stage)
    all_started = started

    filling = {}

    def weights(stage, after, begin=False):
        names = stage_names[stage]
        halves_of = [n for n in names if n != "conv_w"]
        if stage in filling:
            filled, _ = _exchange_wait(filling.pop(stage), *filling.pop(stage + "/bufs"), _fill_plan, after,
                                       "fill_wait_" + stage)
            return dict(zip(halves_of, filled))
        send_sems, recv_sems, landing = in_flight[stage]
        if stage == "ffn1_up":
            after = [all_started] + list(after)
        landed = dict(zip(names, _gather_wait(send_sems, recv_sems, landing, after, "gather_wait_" + stage)))
        halves = [landed[n] for n in halves_of]
        if begin:
            filling[stage], bufs, _, token = _exchange_start(halves, [], _fill_plan, 3 * len(halves), "fill_start_" + stage)
            filling[stage + "/bufs"] = (bufs, [])
            return token
        out = dict(zip(halves_of, _sibling_fill(halves, "sibling_fill_" + stage)))
        if "conv_w" in names:
            out["conv_w"] = jnp.transpose(landed["conv_w"], (1, 0, 2)).reshape(4, D)
        return out

    small_shapes = {n: given[n].shape for n, _ in SMALL}
    small_shapes["conv_w"] = (1, 4, D)
    sm = {n: (given[n][0] if given[n].shape[0] == 1 and n != "rel_bias" else given[n]) for n, _ in SMALL if n != "conv_w"}

    reducer = _Reducer(jnp.stack([lax.axis_index("c"), chip]).astype(jnp.int32))
    sq, dx, _, small = _local_step(x[0], loss_target[0], weights, sm, reducer)

    reduced_small = _all_reduce_small(_pack_small(small, tail=sq))
    last_started = reducer.advance("ffn1", [dx, reduced_small])
    loss = reduced_small[SMALL_USED, 0] * (0.5 / D)
    small_g = _unpack_small(reduced_small, small_shapes)
    grads, delta, new_m, new_v = {}, {}, {}, {}
    in_transit = {}

    def send(stage, after):
        halves = reducer.finish(stage, after)
        lands = [lax.empty(h.shape, F32) for h in halves.values()]
        sems, mine, lands, token = _exchange_start(list(halves.values()), lands, _sibling_plan, len(lands),
                                                   "halves_start_" + stage)
        in_transit[stage] = (list(halves), sems, mine, lands)
        return token

    def update(stage, after):
        names, sems, mine, lands = in_transit[stage]
        mine, theirs = _exchange_wait(sems, mine, lands, _sibling_plan, after, "halves_wait_" + stage)
        updated = _adamw_halves([shard(n) for n in names], mine, theirs, [shard(n, "m_") for n in names],
                                [shard(n, "v_") for n in names], "adamw_" + stage)
        for n, results in zip(names, updated):
            grads[n], delta[n], new_m[n], new_v[n] = (unshard(n, r) for r in results)
        return new_v[names[-1]]

    token = send("ffn2", [reduced_small, last_started])
    token = send("mix", token)
    done = update("ffn2", token)
    done = update("mix", done)
    token = send("ffn1", done)
    update("ffn1", token)

    small_g["conv_w"] = only_my_columns(small_g["conv_w"])
    names = [n for n, _ in SMALL]
    flat2d = lambda a: a.reshape(-1, a.shape[-1])
    outs = _adamw_small(*[[flat2d(given[pre + n]) if pre != "g" else flat2d(small_g[n]) for n in names]
                          for pre in ("", "g", "m_", "v_")], after=last_started)
    for dst, arrs in zip((delta, new_m, new_v), outs):
        dst.update({n: a.reshape(given[n].shape) for n, a in zip(names, arrs)})
    grads.update(small_g)
    return (loss, dx[None], *[grads[n] for n in WEIGHTS], *[delta[n] for n in WEIGHTS], *[new_m[n] for n in WEIGHTS],
            *[new_v[n] for n in WEIGHTS])
```

```python
import functools
import math

import jax
import jax.numpy as jnp
from jax import lax
from jax.experimental import pallas as pl
from jax.experimental.pallas import tpu as pltpu

F32, BF16 = jnp.float32, jnp.bfloat16
D = 1024
NSH = 4
FF_S = 704
IN_S = 896
GATE_S = 512
KV_W = 256
CHUNK = 64
KB = 192
N_HEADS = 16
HEAD_DIM = 64
N_BUCKETS = 32
KP = 192
PAD_KEYS = 128
RMS_EPS = 1e-6
NEG_INF = -1e30
LRU_C = 8.0
TM = 512
TM_SCAN = 256
VMEM_LIMIT = 56 * 1024 * 1024
ADAM_LR, ADAM_B1, ADAM_B2, ADAM_EPS, ADAM_WD, ADAM_STEP = 0.001, 0.9, 0.999, 1e-08, 0.01, 10
SMALL_ROWS = 1216
SMALL_SLICE = SMALL_ROWS // 8
MESH = pl.DeviceIdType.MESH

BIG = ["ffn1_w1", "ffn1_w3", "ffn1_w2", "w_in", "w_lru_out", "w_attn_out", "w_gate", "w_o", "ffn2_w1", "ffn2_w3", "ffn2_w2"]
SMALL = [("ffn1_pre_g", 1024), ("ffn1_post_g", 1024), ("mix_pre_g", 1024), ("conv_w", 4096), ("conv_b", 1024),
         ("rg_a_w", 65536), ("rg_a_b", 1024), ("rg_x_w", 65536), ("rg_x_b", 1024), ("lru_lambda", 1024),
         ("attn_sinks", 1024), ("rel_bias", 1024), ("b_gate", 2048), ("mix_post_g", 1024), ("ffn2_pre_g", 1024),
         ("ffn2_post_g", 1024)]
WEIGHTS = ["ffn1_pre_g", "ffn1_w1", "ffn1_w3", "ffn1_w2", "ffn1_post_g", "mix_pre_g", "w_in", "conv_w", "conv_b", "rg_a_w",
           "rg_a_b", "rg_x_w", "rg_x_b", "lru_lambda", "w_lru_out", "attn_sinks", "rel_bias", "w_attn_out", "w_gate", "b_gate",
           "w_o", "mix_post_g", "ffn2_pre_g", "ffn2_w1", "ffn2_w3", "ffn2_w2", "ffn2_post_g"]


def _params(*sem):
    return pltpu.CompilerParams(dimension_semantics=sem or None, vmem_limit_bytes=VMEM_LIMIT)


def _nn(a, b):
    return jnp.dot(a, b, preferred_element_type=F32)


def _nt(a, b):
    return lax.dot_general(a, b, (((1,), (1,)), ((), ())), preferred_element_type=F32)


def _tn(a, b):
    return lax.dot_general(a, b, (((0,), (0,)), ((), ())), preferred_element_type=F32)


def _rms(x, g):
    rstd = lax.rsqrt(jnp.mean(x * x, axis=-1, keepdims=True) + RMS_EPS)
    return (x * rstd) * g


def _rms_bwd(dout, x, g):
    rstd = lax.rsqrt(jnp.mean(x * x, axis=-1, keepdims=True) + RMS_EPS)
    xhat = x * rstd
    dg = jnp.sum(dout * xhat, axis=0, keepdims=True)
    dxhat = dout * g
    dx = rstd * (dxhat - xhat * jnp.mean(dxhat * xhat, axis=-1, keepdims=True))
    return dx, dg


_GELU_K = math.sqrt(2.0 / math.pi)


_GELU_C = 0.044715 * _GELU_K


def _gelu_and_grad(x):
    x2 = x * x
    t = jnp.tanh(x * (_GELU_K + _GELU_C * x2))
    cdf = 0.5 + 0.5 * t
    return x * cdf, cdf + (x * (_GELU_K + (3.0 * _GELU_C) * x2)) * (0.5 - 0.5 * (t * t))


def _softplus_neg(lam):
    z = -lam
    u = jnp.exp(-jnp.abs(z))
    w = 1.0 + u
    log1p_u = jnp.where(w == 1.0, u, jnp.log(w) * (u / (w - 1.0)))
    return jnp.maximum(z, 0.0) + log1p_u


def _lru_coeffs(r, sp):
    log_a = (-LRU_C * r) * sp
    a = jnp.exp(log_a)
    t = jnp.tanh(log_a)
    s = jnp.sqrt(-2.0 * t / (1.0 - t))
    return a, s


def _row_spec(tm, width):
    return pl.BlockSpec((tm, width), lambda i: (i, 0))


def _vec_spec(width):
    return pl.BlockSpec((1, width), lambda i: (0, 0))


_WHOLE = pl.BlockSpec(memory_space=pltpu.VMEM)


def _tile(t, tm=TM):
    return min(tm, t)


def _ffn_fwd(x, gpre, w1g, w3g, w2g, gpost, name, target=None):
    t = x.shape[0]
    tm = _tile(t)
    last = target is not None

    def body(x_ref, gpre_ref, w1_ref, w3_ref, w2_ref, gpost_ref, *refs):
        t_ref, (h_ref, a_ref, b_ref, hm_ref, f_ref), l_ref = (refs[0] if last else None), refs[last:last + 5], refs[-1]
        xv = x_ref[...]
        nb = _rms(xv, gpre_ref[...]).astype(BF16)
        f = jnp.zeros((tm, D), F32)
        for s in range(NSH):
            a = _nt(nb, w1_ref[s])
            b = _nt(nb, w3_ref[s])
            hmb = ((a * jax.nn.sigmoid(a)) * b).astype(BF16)
            a_ref[s] = a.astype(BF16)
            b_ref[s] = b.astype(BF16)
            hm_ref[s] = hmb
            f = f + _nn(hmb, w2_ref[s])
        f_ref[...] = f
        h = xv + 0.5 * _rms(f, gpost_ref[...])
        if last:
            @pl.when(pl.program_id(0) == 0)
            def _():
                l_ref[...] = jnp.zeros_like(l_ref)

            e = h - t_ref[...]
            h_ref[...] = e * (1.0 / D)
            l_ref[...] += jnp.sum(jnp.sum(e * e, axis=0, keepdims=True), axis=1, keepdims=True)
        else:
            h_ref[...] = h

    sh = pl.BlockSpec((NSH, tm, FF_S), lambda i: (0, i, 0))
    act = jax.ShapeDtypeStruct((NSH, t, FF_S), BF16)
    return pl.pallas_call(
        body, grid=(t // tm,), name=name,
        in_specs=[_row_spec(tm, D), _vec_spec(D), _WHOLE, _WHOLE, _WHOLE, _vec_spec(D)] + [_row_spec(tm, D)] * last,
        out_specs=[_row_spec(tm, D), sh, sh, sh, _row_spec(tm, D)] + [pl.BlockSpec((1, 128), lambda i: (0, 0))] * last,
        out_shape=[jax.ShapeDtypeStruct((t, D), F32), act, act, act, jax.ShapeDtypeStruct((t, D), F32)]
        + [jax.ShapeDtypeStruct((1, 128), F32)] * last,
        compiler_params=_params("arbitrary"),
    )(x, gpre, w1g, w3g, w2g, gpost, *([target] if last else []))


def _ffn_up(x, gpre, w1g, w3g, name):
    t = x.shape[0]
    tm = _tile(t)

    def body(x_ref, gpre_ref, w1_ref, w3_ref, a_ref, b_ref, hm_ref):
        nb = _rms(x_ref[...], gpre_ref[...]).astype(BF16)
        for s in range(NSH):
            a = _nt(nb, w1_ref[s])
            b = _nt(nb, w3_ref[s])
            a_ref[s] = a.astype(BF16)
            b_ref[s] = b.astype(BF16)
            hm_ref[s] = ((a * jax.nn.sigmoid(a)) * b).astype(BF16)

    sh = pl.BlockSpec((NSH, tm, FF_S), lambda i: (0, i, 0))
    act = jax.ShapeDtypeStruct((NSH, t, FF_S), BF16)
    return pl.pallas_call(
        body, grid=(t // tm,), name=name, in_specs=[_row_spec(tm, D), _vec_spec(D), _WHOLE, _WHOLE],
        out_specs=[sh, sh, sh], out_shape=[act, act, act], compiler_params=_params("arbitrary"),
    )(x, gpre, w1g, w3g)


def _ffn_down(x, hm, w2g, gpost, name):
    t = x.shape[0]
    tm = _tile(t)

    def body(x_ref, hm_ref, w2_ref, gpost_ref, h_ref, f_ref):
        f = jnp.zeros((tm, D), F32)
        for s in range(NSH):
            f = f + _nn(hm_ref[s], w2_ref[s])
        f_ref[...] = f
        h_ref[...] = x_ref[...] + 0.5 * _rms(f, gpost_ref[...])

    sh = pl.BlockSpec((NSH, tm, FF_S), lambda i: (0, i, 0))
    f32 = jax.ShapeDtypeStruct((t, D), F32)
    return pl.pallas_call(
        body, grid=(t // tm,), name=name, in_specs=[_row_spec(tm, D), sh, _WHOLE, _vec_spec(D)],
        out_specs=[_row_spec(tm, D), _row_spec(tm, D)], out_shape=[f32, f32], compiler_params=_params("arbitrary"),
    )(x, hm, w2g, gpost)


def _mix_proj(h1, gmix, w_in_g, w_gate_g, b_gate):
    t = h1.shape[0]
    tm = _tile(t)

    def body(h_ref, g_ref, win_ref, wg_ref, bg_ref, u_ref, q_ref, k_ref, v_ref, xr_ref, xg_ref, gate_ref):
        ub = _rms(h_ref[...], g_ref[...]).astype(BF16)
        u_ref[...] = ub
        p0 = _nn(ub, win_ref[0])
        q_ref[:, 0:896] = p0.astype(BF16)
        p1 = _nn(ub, win_ref[1])
        q_ref[:, 896:1024] = p1[:, 0:128].astype(BF16)
        k_ref[...] = p1[:, 128:384].astype(BF16)
        v_ref[...] = p1[:, 384:640].astype(BF16)
        xr_ref[:, 0:256] = p1[:, 640:896]
        p2 = _nn(ub, win_ref[2])
        xr_ref[:, 256:1024] = p2[:, 0:768]
        xg_ref[:, 0:128] = p2[:, 768:896]
        xg_ref[:, 128:1024] = _nn(ub, win_ref[3])
        for s in range(NSH):
            sl = slice(s * GATE_S, (s + 1) * GATE_S)
            gate_ref[:, sl] = jax.nn.sigmoid(_nn(ub, wg_ref[s]) + bg_ref[:, sl]).astype(BF16)

    return pl.pallas_call(
        body, grid=(t // tm,), name="mix_proj",
        in_specs=[_row_spec(tm, D), _vec_spec(D), _WHOLE, _WHOLE, _vec_spec(2 * D)],
        out_specs=[_row_spec(tm, D), _row_spec(tm, D), _row_spec(tm, KV_W), _row_spec(tm, KV_W), _row_spec(tm, D),
                   _row_spec(tm, D), _row_spec(tm, 2 * D)],
        out_shape=[jax.ShapeDtypeStruct((t, D), BF16), jax.ShapeDtypeStruct((t, D), BF16),
                   jax.ShapeDtypeStruct((t, KV_W), BF16), jax.ShapeDtypeStruct((t, KV_W), BF16),
                   jax.ShapeDtypeStruct((t, D), F32), jax.ShapeDtypeStruct((t, D), F32),
                   jax.ShapeDtypeStruct((t, 2 * D), BF16)],
        compiler_params=_params("arbitrary"),
    )(h1, gmix, w_in_g, w_gate_g, b_gate)


def _rglru_fwd(xr, xg, conv_w, conv_b, wa2, ba, wx2, bx, lam, after=None):
    t = xr.shape[0]
    tm = _tile(t, TM_SCAN)
    nb8 = tm // 8

    def body(xr_ref, xrp_ref, xg_ref, cw_ref, cb_ref, wa_ref, ba_ref, wx_ref, bx_ref, lam_ref,
             hr_ref, yain_ref, xc_ref, r_ref, ig_ref, a_sc, s_ref, ext, h_sc):
        i = pl.program_id(0)

        @pl.when(i == 0)
        def _():
            h_sc[...] = jnp.zeros_like(h_sc)

        ext[0:8, :] = jnp.where(i == 0, 0.0, xrp_ref[...])
        ext[8:8 + tm, :] = xr_ref[...]
        xc = jnp.broadcast_to(cb_ref[...], (tm, D))
        for tap in range(4):
            xc = xc + ext[pl.ds(5 + tap, tm), :] * cw_ref[tap:tap + 1, :]
        xc_ref[...] = xc
        xcb = xc.astype(BF16)
        for p in range(8):
            sl = slice(p * 128, (p + 1) * 128)
            r_ref[:, sl] = jax.nn.sigmoid(_nn(xcb[:, sl], wa_ref[p]) + ba_ref[:, sl])
            ig_ref[:, sl] = jax.nn.sigmoid(_nn(xcb[:, sl], wx_ref[p]) + bx_ref[:, sl])
        a, s = _lru_coeffs(r_ref[...], _softplus_neg(lam_ref[...]))
        a_sc[...] = a
        s_ref[...] = s
        hr_ref[...] = s * (ig_ref[...] * xc)

        def blk(j, h):
            st = pl.multiple_of(j * 8, 8)
            a8 = a_sc[pl.ds(st, 8), :]
            u8 = hr_ref[pl.ds(st, 8), :]
            rows = []
            for k in range(8):
                h = a8[k:k + 1, :] * h + u8[k:k + 1, :]
                rows.append(h)
            hr_ref[pl.ds(st, 8), :] = jnp.concatenate(rows, axis=0)
            return h

        h_sc[0:1, :] = lax.fori_loop(0, nb8, blk, h_sc[0:1, :])
        yain_ref[...] = (hr_ref[...] * _gelu_and_grad(xg_ref[...])[0]).astype(BF16)

    prev = pl.BlockSpec((8, D), lambda i: (jnp.maximum(i * nb8 - 1, 0), 0))
    full = lambda shape: pl.BlockSpec(shape, lambda i: tuple(0 for _ in shape))
    f32 = jax.ShapeDtypeStruct((t, D), F32)
    body, specs, operands = _behind(body, after)
    return pl.pallas_call(
        body, grid=(t // tm,), name="rglru_fwd",
        in_specs=specs + [_row_spec(tm, D), prev, _row_spec(tm, D), full((4, D)), _vec_spec(D), full((8, 128, 128)),
                          _vec_spec(D), full((8, 128, 128)), _vec_spec(D), _vec_spec(D)],
        out_specs=[_row_spec(tm, D)] * 7,
        out_shape=[f32, jax.ShapeDtypeStruct((t, D), BF16), f32, f32, f32, f32, f32],
        scratch_shapes=[pltpu.VMEM((tm + 8, D), F32), pltpu.VMEM((8, D), F32)],
        compiler_params=_params("arbitrary"),
    )(*operands, xr, xr, xg, conv_w, conv_b, wa2, ba, wx2, bx, lam)


def _bias_fwd(table_t, onehot_t):
    def body(t_ref, e_ref, o_ref):
        o_ref[...] = jnp.dot(t_ref[...], e_ref[...], preferred_element_type=F32, precision=lax.Precision.HIGHEST)

    return pl.pallas_call(body, out_shape=jax.ShapeDtypeStruct((N_HEADS, CHUNK * KB), F32), name="bias_fwd",
                          compiler_params=_params())(table_t, onehot_t)


def _bias_bwd(dbias_flat, onehot_t, ds_rows):
    def body(d_ref, e_ref, s_ref, o_ref, so_ref):
        o_ref[...] = lax.dot_general(d_ref[...], e_ref[...], (((1,), (1,)), ((), ())), preferred_element_type=F32,
                                     precision=lax.Precision.HIGHEST)
        so_ref[...] = jnp.zeros_like(so_ref)
        for r in range(4):
            so_ref[:, r:r + 1] = jnp.sum(s_ref[:, r * CHUNK:(r + 1) * CHUNK], axis=1, keepdims=True)

    return pl.pallas_call(body, out_shape=[jax.ShapeDtypeStruct((N_HEADS, N_BUCKETS), F32), jax.ShapeDtypeStruct((8, 128), F32)],
                          name="bias_bwd", compiler_params=_params())(dbias_flat, onehot_t, ds_rows)


def _stack_heads(q):
    return jnp.concatenate(
        [jnp.concatenate([q[:, (4 * g + r) * HEAD_DIM:(4 * g + r + 1) * HEAD_DIM] for g in range(4)], axis=1)
         for r in range(4)], axis=0)


def _unstack_heads(o):
    return jnp.concatenate([o[r * CHUNK:(r + 1) * CHUNK, g * HEAD_DIM:(g + 1) * HEAD_DIM] for g in range(4) for r in range(4)],
                           axis=1)


def _block_diag(w, mask):
    return jnp.concatenate([w] * 4, axis=0) * mask


def _group_softmax(qk, bias_g, sink, valid):
    s = qk * (HEAD_DIM ** -0.5) + bias_g
    s = jnp.where(valid, s, NEG_INF)
    m = jnp.maximum(jnp.max(s, axis=0, keepdims=True), sink)
    e = jnp.exp(s - m)
    es = jnp.exp(sink - m)
    inv = 1.0 / (jnp.sum(e, axis=0, keepdims=True) + es)
    return e * inv, es * inv


def _attn_fwd(sink_rows, q, kp, vp, bias_t, mask, after=None):
    t = q.shape[0]
    per_step = 8

    def body(sink_ref, q_ref, kp_ref, vp_ref, bias_ref, mask_ref, o_ref):
        owns = [mask_ref[g * KP:(g + 1) * KP, :] for g in range(4)]
        for k in range(per_step):
            c = pl.program_id(0) * per_step + k
            rows = slice(k * CHUNK, (k + 1) * CHUNK)
            st = pl.multiple_of(c * CHUNK, CHUNK)
            kw = kp_ref[pl.ds(st, KP), :]
            vw = vp_ref[pl.ds(st, KP), :]
            q_all = _stack_heads(q_ref[rows, :])
            valid = lax.broadcasted_iota(jnp.int32, (KP, 1), 0) + c * CHUNK >= PAD_KEYS
            scores = [_nt(kw * owns[g], q_all) for g in range(4)]
            ps = [_group_softmax(scores[g], bias_ref[g * KP:(g + 1) * KP, :], sink_ref[g:g + 1, :], valid)[0]
                  for g in range(4)]
            o_all = sum(_tn(ps[g].astype(BF16), vw * owns[g]) for g in range(4))
            o_ref[rows, :] = _unstack_heads(o_all).astype(BF16)

    body, specs, operands = _behind(body, after)
    return pl.pallas_call(
        body, grid=(t // (per_step * CHUNK),), name="attn_fwd",
        in_specs=specs + [_WHOLE, _row_spec(per_step * CHUNK, D), _WHOLE, _WHOLE, _WHOLE, _WHOLE],
        out_specs=_row_spec(per_step * CHUNK, D),
        out_shape=jax.ShapeDtypeStruct((t, D), BF16),
        compiler_params=_params("arbitrary"),
    )(*operands, sink_rows, q, kp, vp, bias_t, mask)


def _merge_fwd(yain, o, gate, h1, w_lru, w_att, w_o, gpost):
    t = h1.shape[0]
    tm = _tile(t)

    def body(ya_ref, o_ref, g_ref, h_ref, wl_ref, wa_ref, wo_ref, gp_ref, h2_ref, mo_ref, mg_ref, ya_out, yb_out):
        ya = _nn(ya_ref[...], wl_ref[...])
        yb = _nn(o_ref[...], wa_ref[...])
        g0 = g_ref[:, 0:D].astype(F32)
        g1 = g_ref[:, D:2 * D].astype(F32)
        mg = (g0 * ya + g1 * yb).astype(BF16)
        mo = _nn(mg, wo_ref[...])
        ya_out[...] = (ya * (g0 * (1.0 - g0))).astype(BF16)
        yb_out[...] = (yb * (g1 * (1.0 - g1))).astype(BF16)
        mg_ref[...] = mg
        mo_ref[...] = mo
        h2_ref[...] = h_ref[...] + _rms(mo, gp_ref[...])

    f32 = jax.ShapeDtypeStruct((t, D), F32)
    b16 = jax.ShapeDtypeStruct((t, D), BF16)
    return pl.pallas_call(
        body, grid=(t // tm,), name="merge_fwd",
        in_specs=[_row_spec(tm, D), _row_spec(tm, D), _row_spec(tm, 2 * D), _row_spec(tm, D), _WHOLE, _WHOLE, _WHOLE,
                  _vec_spec(D)],
        out_specs=[_row_spec(tm, D)] * 5,
        out_shape=[f32, f32, b16, b16, b16],
        compiler_params=_params("arbitrary"),
    )(yain, o, gate, h1, w_lru, w_att, w_o, gpost)


def _ffn_bwd(dh, x, f, a, b, gpre, gpost, w1g, w3g, w2g, name):
    t = x.shape[0]
    tm = _tile(t, TM_SCAN)

    def body(dh_ref, x_ref, f_ref, a_ref, b_ref, gpre_ref, gpost_ref, w1_ref, w3_ref, w2_ref,
             dx_ref, n_ref, da_ref, db_ref, df_ref, dgpre_ref, dgpost_ref):
        @pl.when(pl.program_id(0) == 0)
        def _():
            dgpre_ref[...] = jnp.zeros_like(dgpre_ref)
            dgpost_ref[...] = jnp.zeros_like(dgpost_ref)

        dhv = dh_ref[...]
        xv = x_ref[...]
        df, dgp = _rms_bwd(0.5 * dhv, f_ref[...], gpost_ref[...])
        dgpost_ref[...] += dgp
        dfb = df.astype(BF16)
        df_ref[...] = dfb
        n_ref[...] = _rms(xv, gpre_ref[...]).astype(BF16)
        dn = jnp.zeros((tm, D), F32)
        for s in range(NSH):
            av = a_ref[s].astype(F32)
            bv = b_ref[s].astype(F32)
            sg = jax.nn.sigmoid(av)
            dhm = _nt(dfb, w2_ref[s])
            dab = (dhm * bv * (sg * (1.0 + av * (1.0 - sg)))).astype(BF16)
            dbb = (dhm * (av * sg)).astype(BF16)
            da_ref[s] = dab
            db_ref[s] = dbb
            dn = dn + _nn(dab, w1_ref[s]) + _nn(dbb, w3_ref[s])
        dxn, dg = _rms_bwd(dn, xv, gpre_ref[...])
        dgpre_ref[...] += dg
        dx_ref[...] = dhv + dxn

    sh = pl.BlockSpec((NSH, tm, FF_S), lambda i: (0, i, 0))
    act = jax.ShapeDtypeStruct((NSH, t, FF_S), BF16)
    vec = jax.ShapeDtypeStruct((1, D), F32)
    return pl.pallas_call(
        body, grid=(t // tm,), name=name,
        in_specs=[_row_spec(tm, D), _row_spec(tm, D), _row_spec(tm, D), sh, sh, _vec_spec(D), _vec_spec(D), _WHOLE, _WHOLE,
                  _WHOLE],
        out_specs=[_row_spec(tm, D), _row_spec(tm, D), sh, sh, _row_spec(tm, D), _vec_spec(D), _vec_spec(D)],
        out_shape=[jax.ShapeDtypeStruct((t, D), F32), jax.ShapeDtypeStruct((t, D), BF16), act, act,
                   jax.ShapeDtypeStruct((t, D), BF16), vec, vec],
        compiler_params=_params("arbitrary"),
    )(dh, x, f, a, b, gpre, gpost, w1g, w3g, w2g)


def _behind(body, after):
    if after is None:
        return body, [], []

    def ordered(_, *refs):
        body(*refs)

    return ordered, [_ANY], [after]


def _ffn_bwd_acts(dh, x, f, a, b, gpre, gpost, w2g, name):
    t = x.shape[0]
    tm = _tile(t)

    def body(dh_ref, x_ref, f_ref, a_ref, b_ref, gpre_ref, gpost_ref, w2_ref, n_ref, da_ref, db_ref, df_ref, dgpost_ref):
        @pl.when(pl.program_id(0) == 0)
        def _():
            dgpost_ref[...] = jnp.zeros_like(dgpost_ref)

        df, dgp = _rms_bwd(0.5 * dh_ref[...], f_ref[...], gpost_ref[...])
        dgpost_ref[...] += dgp
        dfb = df.astype(BF16)
        df_ref[...] = dfb
        n_ref[...] = _rms(x_ref[...], gpre_ref[...]).astype(BF16)
        for s in range(NSH):
            av = a_ref[s].astype(F32)
            bv = b_ref[s].astype(F32)
            sg = jax.nn.sigmoid(av)
            dhm = _nt(dfb, w2_ref[s])
            da_ref[s] = (dhm * bv * (sg * (1.0 + av * (1.0 - sg)))).astype(BF16)
            db_ref[s] = (dhm * (av * sg)).astype(BF16)

    sh = pl.BlockSpec((NSH, tm, FF_S), lambda i: (0, i, 0))
    act = jax.ShapeDtypeStruct((NSH, t, FF_S), BF16)
    b16 = jax.ShapeDtypeStruct((t, D), BF16)
    return pl.pallas_call(
        body, grid=(t // tm,), name=name,
        in_specs=[_row_spec(tm, D), _row_spec(tm, D), _row_spec(tm, D), sh, sh, _vec_spec(D), _vec_spec(D), _WHOLE],
        out_specs=[_row_spec(tm, D), sh, sh, _row_spec(tm, D), _vec_spec(D)],
        out_shape=[b16, act, act, b16, jax.ShapeDtypeStruct((1, D), F32)],
        compiler_params=_params("arbitrary"),
    )(dh, x, f, a, b, gpre, gpost, w2g)


def _ffn_bwd_input(dh, x, da, db, gpre, w1g, w3g, name, after):
    t = x.shape[0]
    tm = _tile(t)

    def body(dh_ref, x_ref, da_ref, db_ref, gpre_ref, w1_ref, w3_ref, dx_ref, dgpre_ref):
        @pl.when(pl.program_id(0) == 0)
        def _():
            dgpre_ref[...] = jnp.zeros_like(dgpre_ref)

        dn = jnp.zeros((tm, D), F32)
        for s in range(NSH):
            dn = dn + _nn(da_ref[s], w1_ref[s]) + _nn(db_ref[s], w3_ref[s])
        dxn, dg = _rms_bwd(dn, x_ref[...], gpre_ref[...])
        dgpre_ref[...] += dg
        dx_ref[...] = dh_ref[...] + dxn

    sh = pl.BlockSpec((NSH, tm, FF_S), lambda i: (0, i, 0))
    body, specs, operands = _behind(body, after)
    return pl.pallas_call(
        body, grid=(t // tm,), name=name,
        in_specs=specs + [_row_spec(tm, D), _row_spec(tm, D), sh, sh, _vec_spec(D), _WHOLE, _WHOLE],
        out_specs=[_row_spec(tm, D), _vec_spec(D)],
        out_shape=[jax.ShapeDtypeStruct((t, D), F32), jax.ShapeDtypeStruct((1, D), F32)],
        compiler_params=_params("arbitrary"),
    )(*operands, dh, x, da, db, gpre, w1g, w3g)


def _wgrad(a, b, a_spec, b_spec, out_spec, out_shape, grid, name, after=None):
    def body(a_ref, b_ref, o_ref):
        o_ref[...] = _tn(a_ref[...], b_ref[...]).astype(BF16)

    body, specs, operands = _behind(body, after)
    return pl.pallas_call(body, grid=grid, name=name, in_specs=specs + [a_spec, b_spec], out_specs=out_spec,
                          out_shape=jax.ShapeDtypeStruct(out_shape, BF16),
                          compiler_params=_params(*("arbitrary",) * len(grid)))(*operands, a, b)


def _wgrad_cols(act, dsh, width, name, after=None):
    t = act.shape[0]
    if dsh.ndim == 3:
        b_spec = pl.BlockSpec((None, t, width), lambda s, k: (s, 0, 0))
    else:
        b_spec = pl.BlockSpec((t, width), lambda s, k: (0, s))
    return _wgrad(act, dsh, pl.BlockSpec((t, 512), lambda s, k: (0, k)), b_spec,
                  pl.BlockSpec((None, 512, width), lambda s, k: (s, k, 0)), (NSH, D, width), (NSH, 2), name, after)


def _wgrad_rows(hm, df, name, after=None):
    t = df.shape[0]
    return _wgrad(hm, df, pl.BlockSpec((None, t, FF_S), lambda s: (s, 0, 0)), pl.BlockSpec((t, D), lambda s: (0, 0)),
                  pl.BlockSpec((None, FF_S, D), lambda s: (s, 0, 0)), (NSH, FF_S, D), (NSH,), name, after)


def _wgrad_sq(a, b, name, after=None):
    t = a.shape[0]
    return _wgrad(a, b, pl.BlockSpec((t, 512), lambda i, j: (0, i)), pl.BlockSpec((t, 512), lambda i, j: (0, j)),
                  pl.BlockSpec((512, 512), lambda i, j: (i, j)), (D, D), (2, 2), name, after)


def _mix_bwd1(dh2, mo, gpost, gate, ya, yb, xg, hr, w_o, w_lru, w_att, after):
    t = dh2.shape[0]
    tm = _tile(t, TM_SCAN)

    def body(dh_ref, mo_ref, gp_ref, g_ref, ya_ref, yb_ref, xg_ref, hr_ref, wo_ref, wl_ref, wa_ref,
             dmo_ref, dya_ref, dyb_ref, dgate_ref, dhr_ref, dxg_ref, do_ref, dgp_ref, dbg_ref):
        @pl.when(pl.program_id(0) == 0)
        def _():
            dgp_ref[...] = jnp.zeros_like(dgp_ref)
            dbg_ref[...] = jnp.zeros_like(dbg_ref)

        dmo, dgp = _rms_bwd(dh_ref[...], mo_ref[...], gp_ref[...])
        dgp_ref[...] += dgp
        dmob = dmo.astype(BF16)
        dmo_ref[...] = dmob
        dm = _nt(dmob, wo_ref[...])
        g0 = g_ref[:, 0:D].astype(F32)
        g1 = g_ref[:, D:2 * D].astype(F32)
        dyab = (dm * g0).astype(BF16)
        dybb = (dm * g1).astype(BF16)
        dya_ref[...] = dyab
        dyb_ref[...] = dybb
        dg0 = dm * ya_ref[...].astype(F32)
        dg1 = dm * yb_ref[...].astype(F32)
        dgate_ref[:, 0:D] = dg0.astype(BF16)
        dgate_ref[:, D:2 * D] = dg1.astype(BF16)
        dbg_ref[:, 0:D] += jnp.sum(dg0, axis=0, keepdims=True)
        dbg_ref[:, D:2 * D] += jnp.sum(dg1, axis=0, keepdims=True)
        dyain = _nt(dyab, wl_ref[...])
        do_ref[...] = _nt(dybb, wa_ref[...]).astype(BF16)
        gelu, gelu_grad = _gelu_and_grad(xg_ref[...])
        dhr_ref[...] = dyain * gelu
        dxg_ref[...] = (dyain * hr_ref[...] * gelu_grad).astype(BF16)

    b16 = jax.ShapeDtypeStruct((t, D), BF16)
    body, specs, operands = _behind(body, after)
    return pl.pallas_call(
        body, grid=(t // tm,), name="mix_bwd1",
        in_specs=specs + [_row_spec(tm, D), _row_spec(tm, D), _vec_spec(D), _row_spec(tm, 2 * D), _row_spec(tm, D),
                          _row_spec(tm, D), _row_spec(tm, D), _row_spec(tm, D), _WHOLE, _WHOLE, _WHOLE],
        out_specs=[_row_spec(tm, D), _row_spec(tm, D), _row_spec(tm, D), _row_spec(tm, 2 * D), _row_spec(tm, D),
                   _row_spec(tm, D), _row_spec(tm, D), _vec_spec(D), _vec_spec(2 * D)],
        out_shape=[b16, b16, b16, jax.ShapeDtypeStruct((t, 2 * D), BF16), jax.ShapeDtypeStruct((t, D), F32), b16, b16,
                   jax.ShapeDtypeStruct((1, D), F32), jax.ShapeDtypeStruct((1, 2 * D), F32)],
        compiler_params=_params("arbitrary"),
    )(*operands, dh2, mo, gpost, gate, ya, yb, xg, hr, w_o, w_lru, w_att)


def _rglru_bwd(dhr, hr, xc, r, ig, a, s, xr, conv_w, wa2, wx2, lam, after):
    t = dhr.shape[0]
    tm = _tile(t, TM_SCAN)
    nb8 = tm // 8
    nt = t // tm

    def body(dhr_ref, hr_ref, hrp_ref, xc_ref, r_ref, ig_ref, a_sc, s_ref, xr_ref, cw_ref, wa_ref, wx_ref, lam_ref,
             dxr_ref, dwa_ref, dwx_ref, dba_ref, dbx_ref, dlam_ref, dcw_ref, dcb_ref,
             ext_h, ext_d, g_sc, c_sc, nxt_sc):
        i = pl.program_id(0)
        first_tile = i == nt - 1

        @pl.when(i == 0)
        def _():
            c_sc[...] = jnp.zeros_like(c_sc)
            nxt_sc[...] = jnp.zeros_like(nxt_sc)
            for ref in (dwa_ref, dwx_ref, dba_ref, dbx_ref, dlam_ref, dcw_ref, dcb_ref):
                ref[...] = jnp.zeros_like(ref)

        lamv = lam_ref[...]
        sp = _softplus_neg(lamv)
        rv = r_ref[...]
        igv = ig_ref[...]
        xcv = xc_ref[...]
        a = a_sc[...]
        s = s_ref[...]

        def blk(jj, c):
            st = pl.multiple_of((nb8 - 1 - jj) * 8, 8)
            d8 = dhr_ref[pl.ds(st, 8), :]
            a8 = a_sc[pl.ds(st, 8), :]
            rows = [None] * 8
            for k in range(7, -1, -1):
                g = d8[k:k + 1, :] + c
                c = a8[k:k + 1, :] * g
                rows[k] = g
            g_sc[pl.ds(st, 8), :] = jnp.concatenate(rows, axis=0)
            return c

        c_sc[0:1, :] = lax.fori_loop(0, nb8, blk, c_sc[0:1, :])
        g = g_sc[...]
        ext_h[0:8, :] = jnp.where(first_tile, 0.0, hrp_ref[...])
        ext_h[8:8 + tm, :] = hr_ref[...]
        hprev = ext_h[pl.ds(7, tm), :]
        d_s = g * (igv * xcv)
        dig = g * s * xcv
        dxc = g * s * igv
        dla = (g * hprev) * a - d_s * ((a * a) / s)
        dr_pre = (dla * (-LRU_C * sp)) * (rv * (1.0 - rv))
        di_pre = dig * (igv * (1.0 - igv))
        dlam_ref[...] += jnp.sum(dla * (LRU_C * rv), axis=0, keepdims=True) * jax.nn.sigmoid(-lamv)
        dba_ref[...] += jnp.sum(dr_pre, axis=0, keepdims=True)
        dbx_ref[...] += jnp.sum(di_pre, axis=0, keepdims=True)
        drb = dr_pre.astype(BF16)
        dib = di_pre.astype(BF16)
        xcb = xcv.astype(BF16)
        ext_d[tm:tm + 8, :] = nxt_sc[...]
        for p in range(8):
            sl = slice(p * 128, (p + 1) * 128)
            ext_d[0:tm, sl] = dxc[:, sl] + _nt(drb[:, sl], wa_ref[p]) + _nt(dib[:, sl], wx_ref[p])
            dwa_ref[p] += _tn(xcb[:, sl], drb[:, sl])
            dwx_ref[p] += _tn(xcb[:, sl], dib[:, sl])
        dxcv = ext_d[0:tm, :]
        nxt_sc[...] = ext_d[0:8, :]
        dcb_ref[...] += jnp.sum(dxcv, axis=0, keepdims=True)
        xrv = xr_ref[...]
        dxr = jnp.zeros((tm, D), F32)
        for tap in range(4):
            ext_h[0:tm, :] = ext_d[pl.ds(3 - tap, tm), :]
            ahead = ext_h[0:tm, :]
            dxr = dxr + ahead * cw_ref[tap:tap + 1, :]
            dcw_ref[tap:tap + 1, :] += jnp.sum(ahead * xrv, axis=0, keepdims=True)
        dxr_ref[...] = dxr.astype(BF16)

    rev = pl.BlockSpec((tm, D), lambda i: (nt - 1 - i, 0))
    prev = pl.BlockSpec((8, D), lambda i: (jnp.maximum((nt - 1 - i) * nb8 - 1, 0), 0))
    full = lambda shape: pl.BlockSpec(shape, lambda i: tuple(0 for _ in shape))
    vec = jax.ShapeDtypeStruct((1, D), F32)
    blocks = jax.ShapeDtypeStruct((8, 128, 128), F32)
    body, specs, operands = _behind(body, after)
    return pl.pallas_call(
        body, grid=(nt,), name="rglru_bwd",
        in_specs=specs + [rev, rev, prev, rev, rev, rev, rev, rev, rev, full((4, D)), full((8, 128, 128)),
                          full((8, 128, 128)), _vec_spec(D)],
        out_specs=[rev, full((8, 128, 128)), full((8, 128, 128)), _vec_spec(D), _vec_spec(D), _vec_spec(D), full((4, D)),
                   _vec_spec(D)],
        out_shape=[jax.ShapeDtypeStruct((t, D), BF16), blocks, blocks, vec, vec, vec, jax.ShapeDtypeStruct((4, D), F32), vec],
        scratch_shapes=[pltpu.VMEM((tm + 8, D), F32), pltpu.VMEM((tm + 8, D), F32),
                        pltpu.VMEM((tm, D), F32), pltpu.VMEM((8, D), F32), pltpu.VMEM((8, D), F32)],
        compiler_params=_params("arbitrary"),
    )(*operands, dhr, hr, hr, xc, r, ig, a, s, xr, conv_w, wa2, wx2, lam)


def _attn_bwd(sink_rows, q, kp, vp, bias_t, mask, do):
    t = q.shape[0]
    tp = kp.shape[0]
    per_step = 8

    def body(sink_ref, q_ref, kp_ref, vp_ref, bias_ref, mask_ref, do_ref, dq_ref, dk_ref, dv_ref, dbias_ref, ds_ref):
        @pl.when(pl.program_id(0) == 0)
        def _():
            for ref in (dk_ref, dv_ref, dbias_ref, ds_ref):
                ref[...] = jnp.zeros_like(ref)

        maskv = mask_ref[...]
        lane_group = lax.broadcasted_iota(jnp.int32, (1, 4 * HEAD_DIM), 1) // HEAD_DIM

        def own_blocks(full):
            out = full[0:KP]
            for g in range(1, 4):
                out = jnp.where(lane_group == g, full[g * KP:(g + 1) * KP], out)
            return out

        dsc_sum, dsinks, dks, dvs = 0.0, [0.0] * 4, [], []
        for k in range(per_step):
            c = pl.program_id(0) * per_step + k
            chunk = slice(k * CHUNK, (k + 1) * CHUNK)
            st = pl.multiple_of(c * CHUNK, CHUNK)
            kbd = _block_diag(kp_ref[pl.ds(st, KP), :], maskv)
            vbd = _block_diag(vp_ref[pl.ds(st, KP), :], maskv)
            q_all = _stack_heads(q_ref[chunk, :])
            do_all = _stack_heads(do_ref[chunk, :])
            valid = lax.broadcasted_iota(jnp.int32, (KP, 1), 0) + c * CHUNK >= PAD_KEYS
            qk = _nt(kbd, q_all)
            dp = _nt(vbd, do_all)
            ps, dscs = [], []
            for g in range(4):
                rows = slice(g * KP, (g + 1) * KP)
                p, sink_p = _group_softmax(qk[rows], bias_ref[rows, :], sink_ref[g:g + 1, :], valid)
                delta = jnp.sum(p * dp[rows], axis=0, keepdims=True)
                ps.append(p)
                dscs.append(p * (dp[rows] - delta))
                dsinks[g] = dsinks[g] - sink_p * delta
            dsc = jnp.concatenate(dscs, axis=0)
            dsc_sum = dsc_sum + dsc
            dsb = (dsc * (HEAD_DIM ** -0.5)).astype(BF16)
            dq_ref[chunk, :] = _unstack_heads(_tn(dsb, kbd)).astype(BF16)
            dks.append((st, own_blocks(_nn(dsb, q_all))))
            dvs.append((st, own_blocks(_nn(jnp.concatenate(ps, axis=0).astype(BF16), do_all))))
        dbias_ref[...] += dsc_sum
        for g in range(4):
            ds_ref[g:g + 1, :] += dsinks[g]
        for (st, dkw), (_, dvw) in zip(dks, dvs):
            dk_ref[pl.ds(st, KP), :] += dkw
            dv_ref[pl.ds(st, KP), :] += dvw

    full = lambda shape: pl.BlockSpec(shape, lambda i: tuple(0 for _ in shape))
    return pl.pallas_call(
        body, grid=(t // (per_step * CHUNK),), name="attn_bwd",
        in_specs=[_WHOLE, _row_spec(per_step * CHUNK, D), _WHOLE, _WHOLE, _WHOLE, _WHOLE, _row_spec(per_step * CHUNK, D)],
        out_specs=[_row_spec(per_step * CHUNK, D), full((tp, KV_W)), full((tp, KV_W)), full((4 * KP, 4 * CHUNK)),
                   full((8, 4 * CHUNK))],
        out_shape=[jax.ShapeDtypeStruct((t, D), BF16), jax.ShapeDtypeStruct((tp, KV_W), F32),
                   jax.ShapeDtypeStruct((tp, KV_W), F32), jax.ShapeDtypeStruct((4 * KP, 4 * CHUNK), F32),
                   jax.ShapeDtypeStruct((8, 4 * CHUNK), F32)],
        compiler_params=_params("arbitrary"),
    )(sink_rows, q, kp, vp, bias_t, mask, do)


def _mix_bwd2(dproj, dgate, h1, dh2, gmix, w_in_g, w_gate_g, after):
    t = h1.shape[0]
    tm = _tile(t)

    def body(dp_ref, dg_ref, h_ref, dh_ref, g_ref, win_ref, wg_ref, dh1_ref, dgm_ref):
        @pl.when(pl.program_id(0) == 0)
        def _():
            dgm_ref[...] = jnp.zeros_like(dgm_ref)

        du = jnp.zeros((tm, D), F32)
        for s in range(NSH):
            du = du + _nt(dp_ref[:, s * IN_S:(s + 1) * IN_S], win_ref[s])
            du = du + _nt(dg_ref[:, s * GATE_S:(s + 1) * GATE_S], wg_ref[s])
        dxn, dg = _rms_bwd(du, h_ref[...], g_ref[...])
        dgm_ref[...] += dg
        dh1_ref[...] = dh_ref[...] + dxn

    body, specs, operands = _behind(body, after)
    return pl.pallas_call(
        body, grid=(t // tm,), name="mix_bwd2",
        in_specs=specs + [_row_spec(tm, NSH * IN_S), _row_spec(tm, 2 * D), _row_spec(tm, D), _row_spec(tm, D), _vec_spec(D),
                          _WHOLE, _WHOLE],
        out_specs=[_row_spec(tm, D), _vec_spec(D)],
        out_shape=[jax.ShapeDtypeStruct((t, D), F32), jax.ShapeDtypeStruct((1, D), F32)],
        compiler_params=_params("arbitrary"),
    )(*operands, dproj, dgate, h1, dh2, gmix, w_in_g, w_gate_g)


def _band_onehot():
    nb = N_BUCKETS // 2
    max_exact = nb // 2
    rel = jnp.arange(KB)[None, :] - PAD_KEYS - jnp.arange(CHUNK)[:, None]
    ret = jnp.where(rel > 0, nb, 0)
    n = jnp.abs(rel)
    nf = jnp.maximum(n, 1).astype(jnp.float32)
    large = max_exact + (jnp.log(nf / max_exact) / math.log(128 / max_exact) * (nb - max_exact)).astype(jnp.int32)
    large = jnp.minimum(large, nb - 1)
    buckets = (ret + jnp.where(n < max_exact, n, large)).reshape(1, CHUNK * KB)
    return (buckets == jnp.arange(N_BUCKETS)[:, None]).astype(F32)


def _pair_blocks(w):
    pairs = w.reshape(8, 2, 64, 64)
    z = jnp.zeros((8, 64, 64), w.dtype)
    return jnp.concatenate([jnp.concatenate([pairs[:, 0], z], axis=2), jnp.concatenate([z, pairs[:, 1]], axis=2)], axis=1)


def _unpair_blocks(w2):
    return jnp.stack([w2[:, 0:64, 0:64], w2[:, 64:128, 64:128]], axis=1).reshape(16, 64, 64)


def _local_step(x, target, weights, sm, reducer):
    row = lambda v: v.reshape(1, -1)
    onehot_t = _band_onehot()
    bias = _bias_fwd(sm["rel_bias"].T, onehot_t).reshape(4, 4, CHUNK, KB)
    bias_t = jnp.pad(jnp.transpose(bias, (0, 3, 1, 2)), ((0, 0), (0, KP - KB), (0, 0), (0, 0))).reshape(4 * KP, 4 * CHUNK)
    sink_rows = jnp.pad(jnp.repeat(sm["attn_sinks"].reshape(4, 4), CHUNK, axis=1), ((0, 4), (0, 0)))
    grp = jnp.arange(4 * KP)[:, None] // KP == jnp.arange(4 * HEAD_DIM)[None, :] // HEAD_DIM
    mask = (grp & (jnp.arange(4 * KP)[:, None] % KP < KB)).astype(BF16)
    wa2 = _pair_blocks(sm["rg_a_w"]).astype(BF16)
    wx2 = _pair_blocks(sm["rg_x_w"]).astype(BF16)
    wg = dict(weights("ffn1_up", [bias_t, sink_rows, mask, wa2, wx2]))
    sm = dict(sm, conv_w=wg["conv_w"])

    a1, b1, hm1 = _ffn_up(x, row(sm["ffn1_pre_g"]), wg["ffn1_w1"], wg["ffn1_w3"], "ffn1_up")
    wg.update(weights("ffn1_down", hm1))
    h1, f1 = _ffn_down(x, hm1, wg["ffn1_w2"], row(sm["ffn1_post_g"]), "ffn1_down")
    wg.update(weights("mix_in", h1))
    u, q, k, v, xr, xg, gate = _mix_proj(h1, row(sm["mix_pre_g"]), wg["w_in"], wg["w_gate"], row(sm["b_gate"]))
    token = weights("mix_out", u, begin=True)
    hr, yain, xc, r, ig, lru_a, lru_s = _rglru_fwd(xr, xg, sm["conv_w"], row(sm["conv_b"]), wa2, row(sm["rg_a_b"]), wx2,
                                                   row(sm["rg_x_b"]), row(sm["lru_lambda"]), token)
    token = weights("ffn2", hr, begin=True)
    kp = jnp.pad(k, ((PAD_KEYS, KP - KB), (0, 0)))
    vp = jnp.pad(v, ((PAD_KEYS, KP - KB), (0, 0)))
    o = _attn_fwd(sink_rows, q, kp, vp, bias_t, mask, token)
    wg.update(weights("mix_out", o))
    w_lru = wg["w_lru_out"].reshape(D, D)
    w_att = wg["w_attn_out"].reshape(D, D)
    w_o = wg["w_o"].reshape(D, D)
    wg.update(weights("ffn2", o))
    h2, mo, merged, ya, yb = _merge_fwd(yain, o, gate, h1, w_lru, w_att, w_o, row(sm["mix_post_g"]))
    dy, a2, b2, hm2, f2, sq = _ffn_fwd(h2, row(sm["ffn2_pre_g"]), wg["ffn2_w1"], wg["ffn2_w3"], wg["ffn2_w2"],
                                       row(sm["ffn2_post_g"]), "ffn2_fwd", target)

    big, small = {}, {}
    dh2, n2, da2, db2, df2, small["ffn2_pre_g"], small["ffn2_post_g"] = _ffn_bwd(
        dy, h2, f2, a2, b2, row(sm["ffn2_pre_g"]), row(sm["ffn2_post_g"]), wg["ffn2_w1"], wg["ffn2_w3"], wg["ffn2_w2"],
        "ffn2_bwd")
    big["ffn2_w1"] = _wgrad_rows(da2, n2, "dw_ffn2_w1")
    big["ffn2_w3"] = _wgrad_rows(db2, n2, "dw_ffn2_w3")
    big["ffn2_w2"] = _wgrad_rows(hm2, df2, "dw_ffn2_w2")
    token = reducer.begin("ffn2", {n: big[n] for n in ("ffn2_w1", "ffn2_w3", "ffn2_w2")})
    dmo, dya, dyb, dgate, dhr, dxg, do, small["mix_post_g"], small["b_gate"] = _mix_bwd1(
        dh2, mo, row(sm["mix_post_g"]), gate, ya, yb, xg, hr, w_o, w_lru, w_att, token)
    big["w_o"] = _wgrad_sq(merged, dmo, "dw_w_o").reshape(NSH, D // NSH, D)
    big["w_lru_out"] = _wgrad_sq(yain, dya, "dw_w_lru_out").reshape(NSH, D // NSH, D)
    big["w_attn_out"] = _wgrad_sq(o, dyb, "dw_w_attn_out").reshape(NSH, D // NSH, D)
    token = reducer.advance("ffn2", big["w_attn_out"])
    (dxr, dwa2, dwx2, small["rg_a_b"], small["rg_x_b"], small["lru_lambda"], small["conv_w"], small["conv_b"]) = _rglru_bwd(
        dhr, hr, xc, r, ig, lru_a, lru_s, xr, sm["conv_w"], wa2, wx2, row(sm["lru_lambda"]), token)
    small["rg_a_w"] = _unpair_blocks(dwa2)
    small["rg_x_w"] = _unpair_blocks(dwx2)
    dq, dkp, dvp, dbias_t, ds_rows = _attn_bwd(sink_rows, q, kp, vp, bias_t, mask, do)
    dbias = jnp.transpose(dbias_t.reshape(4, KP, 4, CHUNK)[:, :KB], (0, 2, 3, 1)).reshape(N_HEADS, CHUNK * KB)
    drel_t, dsinks = _bias_bwd(dbias, onehot_t, ds_rows)
    small["attn_sinks"] = dsinks[0:4, 0:4].reshape(N_HEADS)
    small["rel_bias"] = drel_t.T
    t = x.shape[0]
    dproj = jnp.concatenate([dq, dkp[PAD_KEYS:PAD_KEYS + t].astype(BF16), dvp[PAD_KEYS:PAD_KEYS + t].astype(BF16), dxr, dxg],
                            axis=1)
    big["w_in"] = _wgrad_cols(u, dproj, IN_S, "dw_w_in")
    big["w_gate"] = _wgrad_cols(u, dgate, GATE_S, "dw_w_gate")
    token = reducer.begin("mix", {n: big[n] for n in ("w_in", "w_gate", "w_lru_out", "w_attn_out", "w_o")})
    dh1, small["mix_pre_g"] = _mix_bwd2(dproj, dgate, h1, dh2, row(sm["mix_pre_g"]), wg["w_in"], wg["w_gate"], token)
    n1, da1, db1, df1, small["ffn1_post_g"] = _ffn_bwd_acts(
        dh1, x, f1, a1, b1, row(sm["ffn1_pre_g"]), row(sm["ffn1_post_g"]), wg["ffn1_w2"], "ffn1_bwd_acts")
    token = reducer.advance("mix", df1)
    big["ffn1_w1"] = _wgrad_rows(da1, n1, "dw_ffn1_w1", token)
    big["ffn1_w3"] = _wgrad_rows(db1, n1, "dw_ffn1_w3", token)
    big["ffn1_w2"] = _wgrad_rows(hm1, df1, "dw_ffn1_w2", token)
    token = reducer.begin("ffn1", {n: big[n] for n in ("ffn1_w1", "ffn1_w3", "ffn1_w2")})
    dx, small["ffn1_pre_g"] = _ffn_bwd_input(dh1, x, da1, db1, row(sm["ffn1_pre_g"]), wg["ffn1_w1"], wg["ffn1_w3"],
                                             "ffn1_bwd_input", token)
    return sq, dx, big, small


_ANY = pl.BlockSpec(memory_space=pl.ANY)


def _place():
    return lax.axis_index("x"), lax.axis_index("y"), lax.axis_index("c")


def _other_chips(x, y):
    return [(1 - x, y), (x, 1 - y), (1 - x, 1 - y)]


_HBM = pl.BlockSpec(memory_space=pltpu.HBM)
_SEM = pl.BlockSpec(memory_space=pltpu.SEMAPHORE)
_EFFECT = pltpu.SideEffectType.DATAFLOW_SIDE_EFFECTING


def _cast_into_slot(w, chip, name, after=None):
    r, cc = w.shape
    rows = r // 4

    def body(chip_ref, *refs):
        w_ref, o_ref = refs[-2:]
        o_ref[...] = w_ref[...].astype(BF16)

    extra = [] if after is None else [after]
    return pl.pallas_call(
        body, name=name, out_shape=jax.ShapeDtypeStruct((NSH, r, cc), BF16),
        grid_spec=pltpu.PrefetchScalarGridSpec(
            num_scalar_prefetch=1, grid=(4,), in_specs=[_ANY] * len(extra) + [pl.BlockSpec((rows, cc), lambda i, chip: (i, 0))],
            out_specs=pl.BlockSpec((None, rows, cc), lambda i, chip: (chip[0], i, 0))),
        compiler_params=_params("arbitrary"))(chip, *extra, w)


def _piece(ref, slot, c):
    if ref.dtype == F32:
        return ref.at[slot]
    rh = ref.shape[1] // 2
    return ref.at[slot, pl.ds(pl.multiple_of(c * rh, 16), rh), :]


def _gather_start(stages, name):
    flat = [b for stage in stages for b in stage]
    n, ns = len(flat), len(stages)

    def body(*refs):
        ins, sems, token = refs[:n], refs[n:n + 2 * ns], refs[-1]
        x, y, c = _place()
        me = 2 * x + y
        k = 0
        for s, stage in enumerate(stages):
            for i in range(len(stage)):
                for j, (px, py) in enumerate(_other_chips(x, y)):
                    piece = _piece(ins[k], me, c)
                    pltpu.make_async_remote_copy(src_ref=piece, dst_ref=piece, send_sem=sems[2 * s].at[3 * i + j],
                                                 recv_sem=sems[2 * s + 1].at[3 * i + j], device_id=(px, py, c),
                                                 device_id_type=MESH).start()
                k += 1
        token[...] = jnp.zeros_like(token)

    sem_shapes = [pltpu.SemaphoreType.DMA((3 * len(stage),)) for stage in stages for _ in range(2)]
    outs = pl.pallas_call(
        body, name=name, in_specs=[_HBM] * n,
        out_specs=[_SEM] * (2 * ns) + [_HBM] * n + [pl.BlockSpec(memory_space=pltpu.VMEM)],
        out_shape=sem_shapes + [pltpu.HBM(b.shape, b.dtype) for b in flat] + [jax.ShapeDtypeStruct((8, 128), F32)],
        input_output_aliases={i: 2 * ns + i for i in range(n)},
        compiler_params=pltpu.CompilerParams(has_side_effects=_EFFECT),
    )(*[pltpu.with_memory_space_constraint(b, pltpu.HBM) for b in flat])
    sems, bufs, token = outs[:2 * ns], list(outs[2 * ns:2 * ns + n]), outs[-1]
    per_stage, k = [], 0
    for s, stage in enumerate(stages):
        per_stage.append((sems[2 * s], sems[2 * s + 1], bufs[k:k + len(stage)]))
        k += len(stage)
    return per_stage, token


def _gather_wait(send_sems, recv_sems, bufs, after, name):
    n = len(bufs)

    def body(*refs):
        ins, ssem, rsem = refs[:n], refs[n], refs[n + 1]
        x, y, c = _place()
        me = 2 * x + y
        for i in range(n):
            for j, (px, py) in enumerate(_other_chips(x, y)):
                cp = pltpu.make_async_remote_copy(src_ref=_piece(ins[i], me, c), dst_ref=_piece(ins[i], 2 * px + py, c),
                                                  send_sem=ssem.at[3 * i + j], recv_sem=rsem.at[3 * i + j],
                                                  device_id=(px, py, c), device_id_type=MESH)
                cp.wait_send()
                cp.wait_recv()

    afters = list(after) if isinstance(after, (list, tuple)) else [after]
    return pl.pallas_call(
        body, name=name, in_specs=[_HBM] * n + [_SEM, _SEM] + [_ANY] * len(afters), out_specs=[_HBM] * n,
        out_shape=[pltpu.HBM(b.shape, b.dtype) for b in bufs], input_output_aliases={i: i for i in range(n)},
        compiler_params=pltpu.CompilerParams(has_side_effects=_EFFECT),
    )(*bufs, send_sems, recv_sems, *afters)


def _sibling_fill(bufs, name):
    n = len(bufs)

    def body(*refs):
        ins, outs = refs[:n], refs[n:2 * n]
        send_sems, recv_sems = refs[2 * n:]
        x, y, c = _place()
        copies = []
        for i in range(n):
            for j, (px, py) in enumerate(_other_chips(x, y)):
                copies.append(pltpu.make_async_remote_copy(
                    src_ref=_piece(ins[i], 2 * px + py, c), dst_ref=_piece(outs[i], 2 * px + py, c),
                    send_sem=send_sems.at[3 * i + j], recv_sem=recv_sems.at[3 * i + j], device_id=(x, y, 1 - c),
                    device_id_type=MESH))
                copies[-1].start()
        for cp in copies:
            cp.wait()

    return pl.pallas_call(
        body, name=name, in_specs=[_ANY] * n, out_specs=[_ANY] * n,
        out_shape=[jax.ShapeDtypeStruct(b.shape, b.dtype) for b in bufs], input_output_aliases={i: i for i in range(n)},
        scratch_shapes=[pltpu.SemaphoreType.DMA((3 * n,)), pltpu.SemaphoreType.DMA((3 * n,))],
        compiler_params=pltpu.CompilerParams(has_side_effects=True),
    )(*bufs)


def _swap_plan(srcs, lands):
    x, y, c = _place()
    plan = []
    for src, land in zip(srcs, lands):
        rh = src.shape[1] // 2
        plan.append((src.at[:, pl.ds(pl.multiple_of((1 - c) * rh, 16), rh), :], land, (x, y, 1 - c)))
    return plan


def _owners_plan(srcs, lands):
    x, y, c = _place()
    return [(src.at[2 * px + py], land.at[j], (px, py, c))
            for src, land in zip(srcs, lands) for j, (px, py) in enumerate(_other_chips(x, y))]


def _exchange_start(srcs, lands, plan, copies, name):
    n, m = len(srcs), len(srcs) + len(lands)

    def body(*refs):
        send_sems, recv_sems, token = refs[m], refs[m + 1], refs[-1]
        for k, (src, dst, dev) in enumerate(plan(refs[:n], refs[n:m])):
            pltpu.make_async_remote_copy(src_ref=src, dst_ref=dst, send_sem=send_sems.at[k], recv_sem=recv_sems.at[k],
                                         device_id=dev, device_id_type=MESH).start()
        token[...] = jnp.zeros_like(token)

    both = list(srcs) + list(lands)
    outs = pl.pallas_call(
        body, name=name, in_specs=[_HBM] * m,
        out_specs=[_SEM, _SEM] + [_HBM] * m + [pl.BlockSpec(memory_space=pltpu.VMEM)],
        out_shape=[pltpu.SemaphoreType.DMA((copies,)), pltpu.SemaphoreType.DMA((copies,))]
        + [pltpu.HBM(b.shape, b.dtype) for b in both] + [jax.ShapeDtypeStruct((8, 128), F32)],
        input_output_aliases={i: 2 + i for i in range(m)},
        compiler_params=pltpu.CompilerParams(has_side_effects=_EFFECT),
    )(*[pltpu.with_memory_space_constraint(b, pltpu.HBM) for b in both])
    return (outs[0], outs[1]), list(outs[2:2 + n]), list(outs[2 + n:2 + m]), outs[-1]


def _exchange_wait(sems, srcs, lands, plan, after, name):
    n, m = len(srcs), len(srcs) + len(lands)

    def body(*refs):
        send_sems, recv_sems = refs[m], refs[m + 1]
        for k, (src, dst, dev) in enumerate(plan(refs[:n], refs[n:m])):
            cp = pltpu.make_async_remote_copy(src_ref=src, dst_ref=dst, send_sem=send_sems.at[k], recv_sem=recv_sems.at[k],
                                              device_id=dev, device_id_type=MESH)
            cp.wait_send()
            cp.wait_recv()

    both = list(srcs) + list(lands)
    afters = list(after) if isinstance(after, (list, tuple)) else [after]
    outs = pl.pallas_call(
        body, name=name, in_specs=[_HBM] * m + [_SEM, _SEM] + [_ANY] * len(afters), out_specs=[_HBM] * m,
        out_shape=[pltpu.HBM(b.shape, b.dtype) for b in both], input_output_aliases={i: i for i in range(m)},
        compiler_params=pltpu.CompilerParams(has_side_effects=_EFFECT),
    )(*both, sems[0], sems[1], *afters)
    return list(outs[:n]), list(outs[n:])


def _fill_plan(bufs, _):
    x, y, c = _place()
    return [(_piece(buf, 2 * px + py, c), _piece(buf, 2 * px + py, c), (x, y, 1 - c))
            for buf in bufs for px, py in _other_chips(x, y)]


class _Reducer:
    def __init__(self, where):
        self.state = {}
        self.where = where

    def begin(self, stage, grads):
        names = list(grads)
        full = [grads[n] for n in names]
        lands = [lax.empty((NSH, g.shape[1] // 2, g.shape[2]), g.dtype) for g in full]
        sems, full, lands, token = _exchange_start(full, lands, _swap_plan, len(full), "swap_start_" + stage)
        self.state[stage] = (names, sems, full, lands)
        return token

    def advance(self, stage, after):
        names, sems, full, lands = self.state[stage]
        full, got = _exchange_wait(sems, full, lands, _swap_plan, after, "swap_wait_" + stage)
        sums, own = _chip_sums(full, got, self.where, "chip_sums_" + stage)
        lands = [lax.empty((3,) + s.shape[1:], BF16) for s in sums]
        sems, sent, lands, token = _exchange_start(sums, lands, _owners_plan, 3 * len(sums), "owners_start_" + stage)
        self.state[stage] = (names, own, sems, sent, lands)
        return token

    def finish(self, stage, after):
        names, own, sems, sent, lands = self.state[stage]
        _, got = _exchange_wait(sems, sent, lands, _owners_plan, after, "owners_wait_" + stage)
        return dict(zip(names, _owner_sums(own, got, "owner_sums_" + stage)))


def _chip_sums(gs, gots, where, name):
    n = len(gs)

    def body(where_ref, *refs):
        g_refs, got_refs, hb_refs, own_refs = (refs[k * n:(k + 1) * n] for k in range(4))
        mine = pl.program_id(0) == where_ref[1]
        for g_ref, got_ref, hb_ref, own_ref in zip(g_refs, got_refs, hb_refs, own_refs):
            h = g_ref[...].astype(F32) + got_ref[...].astype(F32)
            hb_ref[...] = h.astype(BF16)

            @pl.when(mine)
            def _():
                own_ref[...] = h

    halves = [(g.shape[1] // 2, g.shape[2]) for g in gs]
    slot = [pl.BlockSpec((None, rh, cc), lambda s, where: (s, 0, 0)) for rh, cc in halves]
    outs = pl.pallas_call(
        body, name=name,
        grid_spec=pltpu.PrefetchScalarGridSpec(
            num_scalar_prefetch=1, grid=(NSH,),
            in_specs=[pl.BlockSpec((None, rh, cc), lambda s, where: (s, where[0], 0)) for rh, cc in halves] + slot,
            out_specs=slot + [pl.BlockSpec((rh, cc), lambda s, where: (0, 0)) for rh, cc in halves]),
        out_shape=[jax.ShapeDtypeStruct((NSH, rh, cc), BF16) for rh, cc in halves]
        + [jax.ShapeDtypeStruct((rh, cc), F32) for rh, cc in halves],
        compiler_params=_params("arbitrary"),
    )(where, *gs, *gots)
    return list(outs[:n]), list(outs[n:])


def _owner_sums(owns, gots, name):
    n = len(owns)

    def body(*refs):
        own_refs, got_refs, o_refs = (refs[k * n:(k + 1) * n] for k in range(3))
        for own_ref, got_ref, o_ref in zip(own_refs, got_refs, o_refs):
            o_ref[...] = ((own_ref[...] + got_ref[0].astype(F32)) + got_ref[1].astype(F32)) + got_ref[2].astype(F32)

    blocks = [(o.shape[0] // 2, o.shape[1]) for o in owns]
    rows = [pl.BlockSpec(b, lambda i: (i, 0)) for b in blocks]
    return pl.pallas_call(
        body, grid=(2,), name=name,
        in_specs=rows + [pl.BlockSpec((3,) + b, lambda i: (0, i, 0)) for b in blocks], out_specs=rows,
        out_shape=[jax.ShapeDtypeStruct(o.shape, F32) for o in owns], compiler_params=_params("arbitrary"),
    )(*owns, *gots)


def _sibling_plan(srcs, lands):
    x, y, c = _place()
    return [(src, land, (x, y, 1 - c)) for src, land in zip(srcs, lands)]


def _all_reduce_small(part):
    def body(p_ref, o_ref, rbuf, send1, recv1, send2, recv2):
        x, y, c = _place()
        me = 4 * x + 2 * y + c
        peers = []
        for k in range(1, 8):
            px, py, pc = x ^ ((k >> 2) & 1), y ^ ((k >> 1) & 1), c ^ (k & 1)
            peers.append((k, (px, py, pc), 4 * px + 2 * py + pc))

        def rows(d):
            return pl.ds(pl.multiple_of(d * SMALL_SLICE, 8), SMALL_SLICE)

        first = [pltpu.make_async_remote_copy(src_ref=p_ref.at[rows(idx), :], dst_ref=rbuf.at[me], send_sem=send1.at[k],
                                              recv_sem=recv1.at[k], device_id=dev, device_id_type=MESH)
                 for k, dev, idx in peers]
        for cp in first:
            cp.start()
        rbuf[me] = p_ref[rows(me), :]
        for k, dev, idx in peers:
            pltpu.make_async_remote_copy(src_ref=p_ref.at[rows(idx), :], dst_ref=rbuf.at[idx], send_sem=send1.at[k],
                                         recv_sem=recv1.at[k], device_id=dev, device_id_type=MESH).wait_recv()
        acc = rbuf[0]
        for d in range(1, 8):
            acc = acc + rbuf[d]
        o_ref[rows(me), :] = acc
        second = [pltpu.make_async_remote_copy(src_ref=o_ref.at[rows(me), :], dst_ref=o_ref.at[rows(me), :],
                                               send_sem=send2.at[k], recv_sem=recv2.at[k], device_id=dev, device_id_type=MESH)
                  for k, dev, idx in peers]
        for cp in second:
            cp.start()
        for k, dev, idx in peers:
            pltpu.make_async_remote_copy(src_ref=o_ref.at[rows(me), :], dst_ref=o_ref.at[rows(idx), :], send_sem=send2.at[k],
                                         recv_sem=recv2.at[k], device_id=dev, device_id_type=MESH).wait_recv()
        for cp in first + second:
            cp.wait_send()

    return pl.pallas_call(
        body, name="all_reduce_small", in_specs=[_WHOLE], out_specs=_WHOLE,
        out_shape=jax.ShapeDtypeStruct((SMALL_ROWS, 128), F32),
        scratch_shapes=[pltpu.VMEM((8, SMALL_SLICE, 128), F32)] + [pltpu.SemaphoreType.DMA((8,))] * 4,
        compiler_params=pltpu.CompilerParams(has_side_effects=True),
    )(part)


def _adamw_update(w, gv, m, v):
    nm = ADAM_B1 * m + (1.0 - ADAM_B1) * gv
    nv = ADAM_B2 * v + (1.0 - ADAM_B2) * (gv * gv)
    m_hat = nm / (1.0 - ADAM_B1 ** ADAM_STEP)
    v_hat = nv / (1.0 - ADAM_B2 ** ADAM_STEP)
    return -ADAM_LR * (m_hat / (jnp.sqrt(v_hat) + ADAM_EPS) + ADAM_WD * w), nm, nv


def _adamw_small(ws, gs, ms, vs, after):
    n = len(ws)

    def body(*refs):
        w_refs, g_refs, m_refs, v_refs, d_refs, nm_refs, nv_refs = (refs[k * n:(k + 1) * n] for k in range(7))
        for i in range(n):
            d_refs[i][...], nm_refs[i][...], nv_refs[i][...] = _adamw_update(
                w_refs[i][...], g_refs[i][...], m_refs[i][...], v_refs[i][...])

    out = [jax.ShapeDtypeStruct(w.shape, F32) for w in ws]
    body, specs, operands = _behind(body, after)
    outs = pl.pallas_call(body, in_specs=specs + [_WHOLE] * (4 * n), out_specs=[_WHOLE] * (3 * n), out_shape=out * 3,
                          name="adamw_small", compiler_params=_params())(*operands, *ws, *gs, *ms, *vs)
    return outs[:n], outs[n:2 * n], outs[2 * n:]


def _adamw_halves(ws, mines, theirs, ms, vs, name):
    n = len(ws)
    steps = 2

    def body(*refs):
        w_refs, mine_refs, theirs_refs, m_refs, v_refs, g_refs, d_refs, nm_refs, nv_refs = (
            refs[k * n:(k + 1) * n] for k in range(9))
        is_mine = pl.program_id(0) == lax.axis_index("c")
        for i in range(n):
            gv = jnp.where(is_mine, mine_refs[i][...], theirs_refs[i][...])
            g_refs[i][...] = gv
            d_refs[i][...], nm_refs[i][...], nv_refs[i][...] = _adamw_update(w_refs[i][...], gv, m_refs[i][...], v_refs[i][...])

    blocks = [(h.shape[0] // steps, h.shape[1]) for h in mines]
    whole = [pl.BlockSpec(b, lambda h, i: (steps * h + i, 0)) for b in blocks]
    half = [pl.BlockSpec(b, lambda h, i: (i, 0)) for b in blocks]
    out = [jax.ShapeDtypeStruct(w.shape, F32) for w in ws]
    outs = pl.pallas_call(body, grid=(2, steps), in_specs=whole + half + half + whole + whole, out_specs=whole * 4,
                          out_shape=out * 4, name=name, compiler_params=_params("arbitrary", "arbitrary"),
                          )(*ws, *mines, *theirs, *ms, *vs)
    return [tuple(outs[k * n + i] for k in range(4)) for i in range(n)]


SMALL_USED = sum(size for _, size in SMALL) // 128


def _pack_small(vals, tail=None):
    parts = []
    for name, size in SMALL:
        flat = vals[name].reshape(-1).astype(F32)
        parts.append(jnp.pad(flat, (0, size - flat.shape[0])))
    if tail is not None:
        parts.append(tail.reshape(128))
    flat = jnp.concatenate(parts)
    return jnp.pad(flat, (0, SMALL_ROWS * 128 - flat.shape[0])).reshape(SMALL_ROWS, 128)


def _unpack_small(packed, shapes):
    flat = packed.reshape(-1)
    out, off = {}, 0
    for name, size in SMALL:
        n = math.prod(shapes[name])
        out[name] = flat[off:off + n].reshape(shapes[name])
        off += size
    return out


def kernel(x, ffn1_pre_g, ffn1_w1, ffn1_w3, ffn1_w2, ffn1_post_g, mix_pre_g, w_in, conv_w, conv_b, rg_a_w, rg_a_b, rg_x_w, rg_x_b, lru_lambda, w_lru_out, attn_sinks, rel_bias, w_attn_out, w_gate, b_gate, w_o, mix_post_g, ffn2_pre_g, ffn2_w1, ffn2_w3, ffn2_w2, ffn2_post_g, loss_target, m_ffn1_pre_g, m_ffn1_w1, m_ffn1_w3, m_ffn1_w2, m_ffn1_post_g, m_mix_pre_g, m_w_in, m_conv_w, m_conv_b, m_rg_a_w, m_rg_a_b, m_rg_x_w, m_rg_x_b, m_lru_lambda, m_w_lru_out, m_attn_sinks, m_rel_bias, m_w_attn_out, m_w_gate, m_b_gate, m_w_o, m_mix_post_g, m_ffn2_pre_g, m_ffn2_w1, m_ffn2_w3, m_ffn2_w2, m_ffn2_post_g, v_ffn1_pre_g, v_ffn1_w1, v_ffn1_w3, v_ffn1_w2, v_ffn1_post_g, v_mix_pre_g, v_w_in, v_conv_w, v_conv_b, v_rg_a_w, v_rg_a_b, v_rg_x_w, v_rg_x_b, v_lru_lambda, v_w_lru_out, v_attn_sinks, v_rel_bias, v_w_attn_out, v_w_gate, v_b_gate, v_w_o, v_mix_post_g, v_ffn2_pre_g, v_ffn2_w1, v_ffn2_w3, v_ffn2_w2, v_ffn2_post_g):
    given = dict(locals())
    chip = 2 * lax.axis_index("x") + lax.axis_index("y")
    transposed = ("ffn1_w1", "ffn1_w3", "ffn2_w1", "ffn2_w3")

    def shard(name, moment=""):
        w = given[moment + name][0]
        return w.T if name in transposed else w

    def unshard(name, w):
        return (w.T if name in transposed else w)[None]

    def only_my_columns(a):
        parts = a.reshape(1, 4, NSH, D // NSH)
        return sum(jnp.where(chip == s, parts[:, :, s], 0.0) for s in range(NSH))

    chip_arr = jnp.reshape(chip, (1,)).astype(jnp.int32)
    stage_names = {"ffn1_up": ["ffn1_w1", "ffn1_w3", "conv_w"],
                   "ffn1_down": ["ffn1_w2"],
                   "mix_in": ["w_in", "w_gate"],
                   "mix_out": ["w_lru_out", "w_attn_out", "w_o"],
                   "ffn2": ["ffn2_w1", "ffn2_w3", "ffn2_w2"]}
    in_flight, started = {}, None
    for stage, names in stage_names.items():
        bufs = [jnp.where(lax.broadcasted_iota(jnp.int32, (NSH, 4, D // NSH), 0) == chip, given[n], 0.0) if n == "conv_w"
                else _cast_into_slot(shard(n), chip_arr, "cast_" + n, started) for n in names]
        (in_flight[stage],), started = _gather_start([bufs], "gather_start_" + stage)
    all_started = started

    filling = {}

    def weights(stage, after, begin=False):
        names = stage_names[stage]
        halves_of = [n for n in names if n != "conv_w"]
        if stage in filling:
            filled, _ = _exchange_wait(filling.pop(stage), *filling.pop(stage + "/bufs"), _fill_plan, after,
                                       "fill_wait_" + stage)
            return dict(zip(halves_of, filled))
        send_sems, recv_sems, landing = in_flight[stage]
        if stage == "ffn1_up":
            after = [all_started] + list(after)
        landed = dict(zip(names, _gather_wait(send_sems, recv_sems, landing, after, "gather_wait_" + stage)))
        halves = [landed[n] for n in halves_of]
        if begin:
            filling[stage], bufs, _, token = _exchange_start(halves, [], _fill_plan, 3 * len(halves), "fill_start_" + stage)
            filling[stage + "/bufs"] = (bufs, [])
            return token
        out = dict(zip(halves_of, _sibling_fill(halves, "sibling_fill_" + stage)))
        if "conv_w" in names:
            out["conv_w"] = jnp.transpose(landed["conv_w"], (1, 0, 2)).reshape(4, D)
        return out

    small_shapes = {n: given[n].shape for n, _ in SMALL}
    small_shapes["conv_w"] = (1, 4, D)
    sm = {n: (given[n][0] if given[n].shape[0] == 1 and n != "rel_bias" else given[n]) for n, _ in SMALL if n != "conv_w"}

    reducer = _Reducer(jnp.stack([lax.axis_index("c"), chip]).astype(jnp.int32))
    sq, dx, _, small = _local_step(x[0], loss_target[0], weights, sm, reducer)

    reduced_small = _all_reduce_small(_pack_small(small, tail=sq))
    last_started = reducer.advance("ffn1", [dx, reduced_small])
    loss = reduced_small[SMALL_USED, 0] * (0.5 / D)
    small_g = _unpack_small(reduced_small, small_shapes)
    grads, delta, new_m, new_v = {}, {}, {}, {}
    in_transit = {}

    def send(stage, after):
        halves = reducer.finish(stage, after)
        lands = [lax.empty(h.shape, F32) for h in halves.values()]
        sems, mine, lands, token = _exchange_start(list(halves.values()), lands, _sibling_plan, len(lands),
                                                   "halves_start_" + stage)
        in_transit[stage] = (list(halves), sems, mine, lands)
        return token

    def update(stage, after):
        names, sems, mine, lands = in_transit[stage]
        mine, theirs = _exchange_wait(sems, mine, lands, _sibling_plan, after, "halves_wait_" + stage)
        updated = _adamw_halves([shard(n) for n in names], mine, theirs, [shard(n, "m_") for n in names],
                                [shard(n, "v_") for n in names], "adamw_" + stage)
        for n, results in zip(names, updated):
            grads[n], delta[n], new_m[n], new_v[n] = (unshard(n, r) for r in results)
        return new_v[names[-1]]

    token = send("ffn2", [reduced_small, last_started])
    token = send("mix", token)
    done = update("ffn2", token)
    done = update("mix", done)
    token = send("ffn1", done)
    update("ffn1", token)

    small_g["conv_w"] = only_my_columns(small_g["conv_w"])
    names = [n for n, _ in SMALL]
    flat2d = lambda a: a.reshape(-1, a.shape[-1])
    outs = _adamw_small(*[[flat2d(given[pre + n]) if pre != "g" else flat2d(small_g[n]) for n in names]
                          for pre in ("", "g", "m_", "v_")], after=last_started)
    for dst, arrs in zip((delta, new_m, new_v), outs):
        dst.update({n: a.reshape(given[n].shape) for n, a in zip(names, arrs)})
    grads.update(small_g)
    return (loss, dx[None], *[grads[n] for n in WEIGHTS], *[delta[n] for n in WEIGHTS], *[new_m[n] for n in WEIGHTS],
            *[new_v[n] for n in WEIGHTS])
```

```python
import functools
import math

import jax
import jax.numpy as jnp
from jax import lax
from jax.experimental import pallas as pl
from jax.experimental.pallas import tpu as pltpu

F32, BF16 = jnp.float32, jnp.bfloat16
D = 1024
NSH = 4
FF_S = 704
IN_S = 896
GATE_S = 512
KV_W = 256
CHUNK = 64
KB = 192
N_HEADS = 16
HEAD_DIM = 64
N_BUCKETS = 32
KP = 192
PAD_KEYS = 128
RMS_EPS = 1e-6
NEG_INF = -1e30
LRU_C = 8.0
TM = 512
TM_SCAN = 256
VMEM_LIMIT = 56 * 1024 * 1024
ADAM_LR, ADAM_B1, ADAM_B2, ADAM_EPS, ADAM_WD, ADAM_STEP = 0.001, 0.9, 0.999, 1e-08, 0.01, 10
SMALL_ROWS = 1216
SMALL_SLICE = SMALL_ROWS // 8
MESH = pl.DeviceIdType.MESH

BIG = ["ffn1_w1", "ffn1_w3", "ffn1_w2", "w_in", "w_lru_out", "w_attn_out", "w_gate", "w_o", "ffn2_w1", "ffn2_w3", "ffn2_w2"]
SMALL = [("ffn1_pre_g", 1024), ("ffn1_post_g", 1024), ("mix_pre_g", 1024), ("conv_w", 4096), ("conv_b", 1024),
         ("rg_a_w", 65536), ("rg_a_b", 1024), ("rg_x_w", 65536), ("rg_x_b", 1024), ("lru_lambda", 1024),
         ("attn_sinks", 1024), ("rel_bias", 1024), ("b_gate", 2048), ("mix_post_g", 1024), ("ffn2_pre_g", 1024),
         ("ffn2_post_g", 1024)]
WEIGHTS = ["ffn1_pre_g", "ffn1_w1", "ffn1_w3", "ffn1_w2", "ffn1_post_g", "mix_pre_g", "w_in", "conv_w", "conv_b", "rg_a_w",
           "rg_a_b", "rg_x_w", "rg_x_b", "lru_lambda", "w_lru_out", "attn_sinks", "rel_bias", "w_attn_out", "w_gate", "b_gate",
           "w_o", "mix_post_g", "ffn2_pre_g", "ffn2_w1", "ffn2_w3", "ffn2_w2", "ffn2_post_g"]


def _params(*sem):
    return pltpu.CompilerParams(dimension_semantics=sem or None, vmem_limit_bytes=VMEM_LIMIT)


def _nn(a, b):
    return jnp.dot(a, b, preferred_element_type=F32)


def _nt(a, b):
    return lax.dot_general(a, b, (((1,), (1,)), ((), ())), preferred_element_type=F32)


def _tn(a, b):
    return lax.dot_general(a, b, (((0,), (0,)), ((), ())), preferred_element_type=F32)


def _rms(x, g):
    rstd = lax.rsqrt(jnp.mean(x * x, axis=-1, keepdims=True) + RMS_EPS)
    return (x * rstd) * g


def _rms_bwd(dout, x, g):
    rstd = lax.rsqrt(jnp.mean(x * x, axis=-1, keepdims=True) + RMS_EPS)
    xhat = x * rstd
    dg = jnp.sum(dout * xhat, axis=0, keepdims=True)
    dxhat = dout * g
    dx = rstd * (dxhat - xhat * jnp.mean(dxhat * xhat, axis=-1, keepdims=True))
    return dx, dg


_GELU_K = math.sqrt(2.0 / math.pi)


_GELU_C = 0.044715 * _GELU_K


def _gelu_and_grad(x):
    x2 = x * x
    t = jnp.tanh(x * (_GELU_K + _GELU_C * x2))
    cdf = 0.5 + 0.5 * t
    return x * cdf, cdf + (x * (_GELU_K + (3.0 * _GELU_C) * x2)) * (0.5 - 0.5 * (t * t))


def _softplus_neg(lam):
    z = -lam
    u = jnp.exp(-jnp.abs(z))
    w = 1.0 + u
    log1p_u = jnp.where(w == 1.0, u, jnp.log(w) * (u / (w - 1.0)))
    return jnp.maximum(z, 0.0) + log1p_u


def _lru_coeffs(r, sp):
    log_a = (-LRU_C * r) * sp
    a = jnp.exp(log_a)
    t = jnp.tanh(log_a)
    s = jnp.sqrt(-2.0 * t / (1.0 - t))
    return a, s


def _row_spec(tm, width):
    return pl.BlockSpec((tm, width), lambda i: (i, 0))


def _vec_spec(width):
    return pl.BlockSpec((1, width), lambda i: (0, 0))


_WHOLE = pl.BlockSpec(memory_space=pltpu.VMEM)


def _tile(t, tm=TM):
    return min(tm, t)


def _ffn_fwd(x, gpre, w1g, w3g, w2g, gpost, name, target=None):
    t = x.shape[0]
    tm = _tile(t)
    last = target is not None

    def body(x_ref, gpre_ref, w1_ref, w3_ref, w2_ref, gpost_ref, *refs):
        t_ref, (h_ref, a_ref, b_ref, hm_ref, f_ref), l_ref = (refs[0] if last else None), refs[last:last + 5], refs[-1]
        xv = x_ref[...]
        nb = _rms(xv, gpre_ref[...]).astype(BF16)
        f = jnp.zeros((tm, D), F32)
        for s in range(NSH):
            a = _nt(nb, w1_ref[s])
            b = _nt(nb, w3_ref[s])
            hmb = ((a * jax.nn.sigmoid(a)) * b).astype(BF16)
            a_ref[s] = a.astype(BF16)
            b_ref[s] = b.astype(BF16)
            hm_ref[s] = hmb
            f = f + _nn(hmb, w2_ref[s])
        f_ref[...] = f
        h = xv + 0.5 * _rms(f, gpost_ref[...])
        if last:
            @pl.when(pl.program_id(0) == 0)
            def _():
                l_ref[...] = jnp.zeros_like(l_ref)

            e = h - t_ref[...]
            h_ref[...] = e * (1.0 / D)
            l_ref[...] += jnp.sum(jnp.sum(e * e, axis=0, keepdims=True), axis=1, keepdims=True)
        else:
            h_ref[...] = h

    sh = pl.BlockSpec((NSH, tm, FF_S), lambda i: (0, i, 0))
    act = jax.ShapeDtypeStruct((NSH, t, FF_S), BF16)
    return pl.pallas_call(
        body, grid=(t // tm,), name=name,
        in_specs=[_row_spec(tm, D), _vec_spec(D), _WHOLE, _WHOLE, _WHOLE, _vec_spec(D)] + [_row_spec(tm, D)] * last,
        out_specs=[_row_spec(tm, D), sh, sh, sh, _row_spec(tm, D)] + [pl.BlockSpec((1, 128), lambda i: (0, 0))] * last,
        out_shape=[jax.ShapeDtypeStruct((t, D), F32), act, act, act, jax.ShapeDtypeStruct((t, D), F32)]
        + [jax.ShapeDtypeStruct((1, 128), F32)] * last,
        compiler_params=_params("arbitrary"),
    )(x, gpre, w1g, w3g, w2g, gpost, *([target] if last else []))


def _ffn_up(x, gpre, w1g, w3g, name):
    t = x.shape[0]
    tm = _tile(t)

    def body(x_ref, gpre_ref, w1_ref, w3_ref, n_ref, a_ref, b_ref, hm_ref):
        nb = _rms(x_ref[...], gpre_ref[...]).astype(BF16)
        n_ref[...] = nb
        for s in range(NSH):
            a = _nt(nb, w1_ref[s])
            b = _nt(nb, w3_ref[s])
            a_ref[s] = a.astype(BF16)
            b_ref[s] = b.astype(BF16)
            hm_ref[s] = ((a * jax.nn.sigmoid(a)) * b).astype(BF16)

    sh = pl.BlockSpec((NSH, tm, FF_S), lambda i: (0, i, 0))
    act = jax.ShapeDtypeStruct((NSH, t, FF_S), BF16)
    return pl.pallas_call(
        body, grid=(t // tm,), name=name, in_specs=[_row_spec(tm, D), _vec_spec(D), _WHOLE, _WHOLE],
        out_specs=[_row_spec(tm, D), sh, sh, sh], out_shape=[jax.ShapeDtypeStruct((t, D), BF16), act, act, act],
        compiler_params=_params("arbitrary"),
    )(x, gpre, w1g, w3g)


def _ffn_down(x, hm, w2g, gpost, name):
    t = x.shape[0]
    tm = _tile(t)

    def body(x_ref, hm_ref, w2_ref, gpost_ref, h_ref, f_ref):
        f = jnp.zeros((tm, D), F32)
        for s in range(NSH):
            f = f + _nn(hm_ref[s], w2_ref[s])
        f_ref[...] = f
        h_ref[...] = x_ref[...] + 0.5 * _rms(f, gpost_ref[...])

    sh = pl.BlockSpec((NSH, tm, FF_S), lambda i: (0, i, 0))
    f32 = jax.ShapeDtypeStruct((t, D), F32)
    return pl.pallas_call(
        body, grid=(t // tm,), name=name, in_specs=[_row_spec(tm, D), sh, _WHOLE, _vec_spec(D)],
        out_specs=[_row_spec(tm, D), _row_spec(tm, D)], out_shape=[f32, f32], compiler_params=_params("arbitrary"),
    )(x, hm, w2g, gpost)


def _mix_proj(h1, gmix, w_in_g, w_gate_g, b_gate):
    t = h1.shape[0]
    tm = _tile(t)

    def body(h_ref, g_ref, win_ref, wg_ref, bg_ref, u_ref, q_ref, k_ref, v_ref, xr_ref, xg_ref, gate_ref):
        ub = _rms(h_ref[...], g_ref[...]).astype(BF16)
        u_ref[...] = ub
        p0 = _nn(ub, win_ref[0])
        q_ref[:, 0:896] = p0.astype(BF16)
        p1 = _nn(ub, win_ref[1])
        q_ref[:, 896:1024] = p1[:, 0:128].astype(BF16)
        k_ref[...] = p1[:, 128:384].astype(BF16)
        v_ref[...] = p1[:, 384:640].astype(BF16)
        xr_ref[:, 0:256] = p1[:, 640:896]
        p2 = _nn(ub, win_ref[2])
        xr_ref[:, 256:1024] = p2[:, 0:768]
        xg_ref[:, 0:128] = p2[:, 768:896]
        xg_ref[:, 128:1024] = _nn(ub, win_ref[3])
        for s in range(NSH):
            sl = slice(s * GATE_S, (s + 1) * GATE_S)
            gate_ref[:, sl] = jax.nn.sigmoid(_nn(ub, wg_ref[s]) + bg_ref[:, sl]).astype(BF16)

    return pl.pallas_call(
        body, grid=(t // tm,), name="mix_proj",
        in_specs=[_row_spec(tm, D), _vec_spec(D), _WHOLE, _WHOLE, _vec_spec(2 * D)],
        out_specs=[_row_spec(tm, D), _row_spec(tm, D), _row_spec(tm, KV_W), _row_spec(tm, KV_W), _row_spec(tm, D),
                   _row_spec(tm, D), _row_spec(tm, 2 * D)],
        out_shape=[jax.ShapeDtypeStruct((t, D), BF16), jax.ShapeDtypeStruct((t, D), BF16),
                   jax.ShapeDtypeStruct((t, KV_W), BF16), jax.ShapeDtypeStruct((t, KV_W), BF16),
                   jax.ShapeDtypeStruct((t, D), F32), jax.ShapeDtypeStruct((t, D), F32),
                   jax.ShapeDtypeStruct((t, 2 * D), BF16)],
        compiler_params=_params("arbitrary"),
    )(h1, gmix, w_in_g, w_gate_g, b_gate)


def _rglru_fwd(xr, xg, conv_w, conv_b, wa2, ba, wx2, bx, lam, after=None):
    t = xr.shape[0]
    tm = _tile(t, TM_SCAN)
    nb8 = tm // 8

    def body(xr_ref, xrp_ref, xg_ref, cw_ref, cb_ref, wa_ref, ba_ref, wx_ref, bx_ref, lam_ref,
             hr_ref, yain_ref, xc_ref, r_ref, ig_ref, a_sc, s_ref, ext, h_sc):
        i = pl.program_id(0)

        @pl.when(i == 0)
        def _():
            h_sc[...] = jnp.zeros_like(h_sc)

        ext[0:8, :] = jnp.where(i == 0, 0.0, xrp_ref[...])
        ext[8:8 + tm, :] = xr_ref[...]
        xc = jnp.broadcast_to(cb_ref[...], (tm, D))
        for tap in range(4):
            xc = xc + ext[pl.ds(5 + tap, tm), :] * cw_ref[tap:tap + 1, :]
        xc_ref[...] = xc
        xcb = xc.astype(BF16)
        for p in range(8):
            sl = slice(p * 128, (p + 1) * 128)
            r_ref[:, sl] = jax.nn.sigmoid(_nn(xcb[:, sl], wa_ref[p]) + ba_ref[:, sl])
            ig_ref[:, sl] = jax.nn.sigmoid(_nn(xcb[:, sl], wx_ref[p]) + bx_ref[:, sl])
        a, s = _lru_coeffs(r_ref[...], _softplus_neg(lam_ref[...]))
        a_sc[...] = a
        s_ref[...] = s
        hr_ref[...] = s * (ig_ref[...] * xc)

        def blk(j, h):
            st = pl.multiple_of(j * 8, 8)
            a8 = a_sc[pl.ds(st, 8), :]
            u8 = hr_ref[pl.ds(st, 8), :]
            rows = []
            for k in range(8):
                h = a8[k:k + 1, :] * h + u8[k:k + 1, :]
                rows.append(h)
            hr_ref[pl.ds(st, 8), :] = jnp.concatenate(rows, axis=0)
            return h

        h_sc[0:1, :] = lax.fori_loop(0, nb8, blk, h_sc[0:1, :])
        yain_ref[...] = (hr_ref[...] * _gelu_and_grad(xg_ref[...])[0]).astype(BF16)

    prev = pl.BlockSpec((8, D), lambda i: (jnp.maximum(i * nb8 - 1, 0), 0))
    full = lambda shape: pl.BlockSpec(shape, lambda i: tuple(0 for _ in shape))
    f32 = jax.ShapeDtypeStruct((t, D), F32)
    body, specs, operands = _behind(body, after)
    return pl.pallas_call(
        body, grid=(t // tm,), name="rglru_fwd",
        in_specs=specs + [_row_spec(tm, D), prev, _row_spec(tm, D), full((4, D)), _vec_spec(D), full((8, 128, 128)),
                          _vec_spec(D), full((8, 128, 128)), _vec_spec(D), _vec_spec(D)],
        out_specs=[_row_spec(tm, D)] * 7,
        out_shape=[f32, jax.ShapeDtypeStruct((t, D), BF16), f32, f32, f32, f32, f32],
        scratch_shapes=[pltpu.VMEM((tm + 8, D), F32), pltpu.VMEM((8, D), F32)],
        compiler_params=_params("arbitrary"),
    )(*operands, xr, xr, xg, conv_w, conv_b, wa2, ba, wx2, bx, lam)


def _bias_fwd(table_t, onehot_t):
    def body(t_ref, e_ref, o_ref):
        o_ref[...] = jnp.dot(t_ref[...], e_ref[...], preferred_element_type=F32, precision=lax.Precision.HIGHEST)

    return pl.pallas_call(body, out_shape=jax.ShapeDtypeStruct((N_HEADS, CHUNK * KB), F32), name="bias_fwd",
                          compiler_params=_params())(table_t, onehot_t)


def _bias_bwd(dbias_flat, onehot_t, ds_rows):
    def body(d_ref, e_ref, s_ref, o_ref, so_ref):
        o_ref[...] = lax.dot_general(d_ref[...], e_ref[...], (((1,), (1,)), ((), ())), preferred_element_type=F32,
                                     precision=lax.Precision.HIGHEST)
        so_ref[...] = jnp.zeros_like(so_ref)
        for r in range(4):
            so_ref[:, r:r + 1] = jnp.sum(s_ref[:, r * CHUNK:(r + 1) * CHUNK], axis=1, keepdims=True)

    return pl.pallas_call(body, out_shape=[jax.ShapeDtypeStruct((N_HEADS, N_BUCKETS), F32), jax.ShapeDtypeStruct((8, 128), F32)],
                          name="bias_bwd", compiler_params=_params())(dbias_flat, onehot_t, ds_rows)


def _stack_heads(q):
    return jnp.concatenate(
        [jnp.concatenate([q[:, (4 * g + r) * HEAD_DIM:(4 * g + r + 1) * HEAD_DIM] for g in range(4)], axis=1)
         for r in range(4)], axis=0)


def _unstack_heads(o):
    return jnp.concatenate([o[r * CHUNK:(r + 1) * CHUNK, g * HEAD_DIM:(g + 1) * HEAD_DIM] for g in range(4) for r in range(4)],
                           axis=1)


def _block_diag(w, mask):
    return jnp.concatenate([w] * 4, axis=0) * mask


def _group_softmax(qk, bias_g, sink, valid):
    s = qk * (HEAD_DIM ** -0.5) + bias_g
    s = jnp.where(valid, s, NEG_INF)
    m = jnp.maximum(jnp.max(s, axis=0, keepdims=True), sink)
    e = jnp.exp(s - m)
    es = jnp.exp(sink - m)
    inv = 1.0 / (jnp.sum(e, axis=0, keepdims=True) + es)
    return e * inv, es * inv


def _attn_fwd(sink_rows, q, kp, vp, bias_t, mask, after=None):
    t = q.shape[0]
    per_step = 8

    def body(sink_ref, q_ref, kp_ref, vp_ref, bias_ref, mask_ref, o_ref):
        owns = [mask_ref[g * KP:(g + 1) * KP, :] for g in range(4)]
        for k in range(per_step):
            c = pl.program_id(0) * per_step + k
            rows = slice(k * CHUNK, (k + 1) * CHUNK)
            st = pl.multiple_of(c * CHUNK, CHUNK)
            kw = kp_ref[pl.ds(st, KP), :]
            vw = vp_ref[pl.ds(st, KP), :]
            q_all = _stack_heads(q_ref[rows, :])
            valid = lax.broadcasted_iota(jnp.int32, (KP, 1), 0) + c * CHUNK >= PAD_KEYS
            scores = [_nt(kw * owns[g], q_all) for g in range(4)]
            ps = [_group_softmax(scores[g], bias_ref[g * KP:(g + 1) * KP, :], sink_ref[g:g + 1, :], valid)[0]
                  for g in range(4)]
            o_all = sum(_tn(ps[g].astype(BF16), vw * owns[g]) for g in range(4))
            o_ref[rows, :] = _unstack_heads(o_all).astype(BF16)

    body, specs, operands = _behind(body, after)
    return pl.pallas_call(
        body, grid=(t // (per_step * CHUNK),), name="attn_fwd",
        in_specs=specs + [_WHOLE, _row_spec(per_step * CHUNK, D), _WHOLE, _WHOLE, _WHOLE, _WHOLE],
        out_specs=_row_spec(per_step * CHUNK, D),
        out_shape=jax.ShapeDtypeStruct((t, D), BF16),
        compiler_params=_params("arbitrary"),
    )(*operands, sink_rows, q, kp, vp, bias_t, mask)


def _merge_fwd(yain, o, gate, h1, w_lru, w_att, w_o, gpost):
    t = h1.shape[0]
    tm = _tile(t)

    def body(ya_ref, o_ref, g_ref, h_ref, wl_ref, wa_ref, wo_ref, gp_ref, h2_ref, mo_ref, mg_ref, ya_out, yb_out):
        ya = _nn(ya_ref[...], wl_ref[...])
        yb = _nn(o_ref[...], wa_ref[...])
        g0 = g_ref[:, 0:D].astype(F32)
        g1 = g_ref[:, D:2 * D].astype(F32)
        mg = (g0 * ya + g1 * yb).astype(BF16)
        mo = _nn(mg, wo_ref[...])
        ya_out[...] = (ya * (g0 * (1.0 - g0))).astype(BF16)
        yb_out[...] = (yb * (g1 * (1.0 - g1))).astype(BF16)
        mg_ref[...] = mg
        mo_ref[...] = mo
        h2_ref[...] = h_ref[...] + _rms(mo, gp_ref[...])

    f32 = jax.ShapeDtypeStruct((t, D), F32)
    b16 = jax.ShapeDtypeStruct((t, D), BF16)
    return pl.pallas_call(
        body, grid=(t // tm,), name="merge_fwd",
        in_specs=[_row_spec(tm, D), _row_spec(tm, D), _row_spec(tm, 2 * D), _row_spec(tm, D), _WHOLE, _WHOLE, _WHOLE,
                  _vec_spec(D)],
        out_specs=[_row_spec(tm, D)] * 5,
        out_shape=[f32, f32, b16, b16, b16],
        compiler_params=_params("arbitrary"),
    )(yain, o, gate, h1, w_lru, w_att, w_o, gpost)


def _ffn_bwd(dh, x, f, a, b, gpre, gpost, w1g, w3g, w2g, name):
    t = x.shape[0]
    tm = _tile(t, TM_SCAN)

    def body(dh_ref, x_ref, f_ref, a_ref, b_ref, gpre_ref, gpost_ref, w1_ref, w3_ref, w2_ref,
             dx_ref, n_ref, da_ref, db_ref, df_ref, dgpre_ref, dgpost_ref):
        @pl.when(pl.program_id(0) == 0)
        def _():
            dgpre_ref[...] = jnp.zeros_like(dgpre_ref)
            dgpost_ref[...] = jnp.zeros_like(dgpost_ref)

        dhv = dh_ref[...]
        xv = x_ref[...]
        df, dgp = _rms_bwd(0.5 * dhv, f_ref[...], gpost_ref[...])
        dgpost_ref[...] += dgp
        dfb = df.astype(BF16)
        df_ref[...] = dfb
        n_ref[...] = _rms(xv, gpre_ref[...]).astype(BF16)
        dn = jnp.zeros((tm, D), F32)
        for s in range(NSH):
            av = a_ref[s].astype(F32)
            bv = b_ref[s].astype(F32)
            sg = jax.nn.sigmoid(av)
            dhm = _nt(dfb, w2_ref[s])
            dab = (dhm * bv * (sg * (1.0 + av * (1.0 - sg)))).astype(BF16)
            dbb = (dhm * (av * sg)).astype(BF16)
            da_ref[s] = dab
            db_ref[s] = dbb
            dn = dn + _nn(dab, w1_ref[s]) + _nn(dbb, w3_ref[s])
        dxn, dg = _rms_bwd(dn, xv, gpre_ref[...])
        dgpre_ref[...] += dg
        dx_ref[...] = dhv + dxn

    sh = pl.BlockSpec((NSH, tm, FF_S), lambda i: (0, i, 0))
    act = jax.ShapeDtypeStruct((NSH, t, FF_S), BF16)
    vec = jax.ShapeDtypeStruct((1, D), F32)
    return pl.pallas_call(
        body, grid=(t // tm,), name=name,
        in_specs=[_row_spec(tm, D), _row_spec(tm, D), _row_spec(tm, D), sh, sh, _vec_spec(D), _vec_spec(D), _WHOLE, _WHOLE,
                  _WHOLE],
        out_specs=[_row_spec(tm, D), _row_spec(tm, D), sh, sh, _row_spec(tm, D), _vec_spec(D), _vec_spec(D)],
        out_shape=[jax.ShapeDtypeStruct((t, D), F32), jax.ShapeDtypeStruct((t, D), BF16), act, act,
                   jax.ShapeDtypeStruct((t, D), BF16), vec, vec],
        compiler_params=_params("arbitrary"),
    )(dh, x, f, a, b, gpre, gpost, w1g, w3g, w2g)


def _behind(body, after):
    if after is None:
        return body, [], []

    def ordered(_, *refs):
        body(*refs)

    return ordered, [_ANY], [after]


def _ffn_bwd_acts(dh, f, a, b, gpost, w2g, name):
    t = dh.shape[0]
    tm = _tile(t)

    def body(dh_ref, f_ref, a_ref, b_ref, gpost_ref, w2_ref, da_ref, db_ref, df_ref, dgpost_ref):
        @pl.when(pl.program_id(0) == 0)
        def _():
            dgpost_ref[...] = jnp.zeros_like(dgpost_ref)

        df, dgp = _rms_bwd(0.5 * dh_ref[...], f_ref[...], gpost_ref[...])
        dgpost_ref[...] += dgp
        dfb = df.astype(BF16)
        df_ref[...] = dfb
        for s in range(NSH):
            av = a_ref[s].astype(F32)
            bv = b_ref[s].astype(F32)
            sg = jax.nn.sigmoid(av)
            dhm = _nt(dfb, w2_ref[s])
            da_ref[s] = (dhm * bv * (sg * (1.0 + av * (1.0 - sg)))).astype(BF16)
            db_ref[s] = (dhm * (av * sg)).astype(BF16)

    sh = pl.BlockSpec((NSH, tm, FF_S), lambda i: (0, i, 0))
    act = jax.ShapeDtypeStruct((NSH, t, FF_S), BF16)
    b16 = jax.ShapeDtypeStruct((t, D), BF16)
    return pl.pallas_call(
        body, grid=(t // tm,), name=name,
        in_specs=[_row_spec(tm, D), _row_spec(tm, D), sh, sh, _vec_spec(D), _WHOLE],
        out_specs=[sh, sh, _row_spec(tm, D), _vec_spec(D)],
        out_shape=[act, act, b16, jax.ShapeDtypeStruct((1, D), F32)],
        compiler_params=_params("arbitrary"),
    )(dh, f, a, b, gpost, w2g)


def _ffn_bwd_input(dh, x, da, db, gpre, w1g, w3g, name, after):
    t = x.shape[0]
    tm = _tile(t)

    def body(dh_ref, x_ref, da_ref, db_ref, gpre_ref, w1_ref, w3_ref, dx_ref, dgpre_ref):
        @pl.when(pl.program_id(0) == 0)
        def _():
            dgpre_ref[...] = jnp.zeros_like(dgpre_ref)

        dn = jnp.zeros((tm, D), F32)
        for s in range(NSH):
            dn = dn + _nn(da_ref[s], w1_ref[s]) + _nn(db_ref[s], w3_ref[s])
        dxn, dg = _rms_bwd(dn, x_ref[...], gpre_ref[...])
        dgpre_ref[...] += dg
        dx_ref[...] = dh_ref[...] + dxn

    sh = pl.BlockSpec((NSH, tm, FF_S), lambda i: (0, i, 0))
    body, specs, operands = _behind(body, after)
    return pl.pallas_call(
        body, grid=(t // tm,), name=name,
        in_specs=specs + [_row_spec(tm, D), _row_spec(tm, D), sh, sh, _vec_spec(D), _WHOLE, _WHOLE],
        out_specs=[_row_spec(tm, D), _vec_spec(D)],
        out_shape=[jax.ShapeDtypeStruct((t, D), F32), jax.ShapeDtypeStruct((1, D), F32)],
        compiler_params=_params("arbitrary"),
    )(*operands, dh, x, da, db, gpre, w1g, w3g)


def _wgrad(a, b, a_spec, b_spec, out_spec, out_shape, grid, name, after=None):
    def body(a_ref, b_ref, o_ref):
        o_ref[...] = _tn(a_ref[...], b_ref[...]).astype(BF16)

    body, specs, operands = _behind(body, after)
    return pl.pallas_call(body, grid=grid, name=name, in_specs=specs + [a_spec, b_spec], out_specs=out_spec,
                          out_shape=jax.ShapeDtypeStruct(out_shape, BF16),
                          compiler_params=_params(*("arbitrary",) * len(grid)))(*operands, a, b)


def _wgrad_cols(act, dsh, width, name, after=None):
    t = act.shape[0]
    if dsh.ndim == 3:
        b_spec = pl.BlockSpec((None, t, width), lambda s, k: (s, 0, 0))
    else:
        b_spec = pl.BlockSpec((t, width), lambda s, k: (0, s))
    return _wgrad(act, dsh, pl.BlockSpec((t, 512), lambda s, k: (0, k)), b_spec,
                  pl.BlockSpec((None, 512, width), lambda s, k: (s, k, 0)), (NSH, D, width), (NSH, 2), name, after)


def _wgrad_rows(hm, df, name, after=None):
    t = df.shape[0]
    return _wgrad(hm, df, pl.BlockSpec((None, t, FF_S), lambda s: (s, 0, 0)), pl.BlockSpec((t, D), lambda s: (0, 0)),
                  pl.BlockSpec((None, FF_S, D), lambda s: (s, 0, 0)), (NSH, FF_S, D), (NSH,), name, after)


def _wgrad_sq(a, b, name, after=None):
    t = a.shape[0]
    return _wgrad(a, b, pl.BlockSpec((t, 512), lambda i, j: (0, i)), pl.BlockSpec((t, 512), lambda i, j: (0, j)),
                  pl.BlockSpec((512, 512), lambda i, j: (i, j)), (D, D), (2, 2), name, after)


def _mix_bwd1(dh2, mo, gpost, gate, ya, yb, xg, hr, w_o, w_lru, w_att, after):
    t = dh2.shape[0]
    tm = _tile(t, TM_SCAN)

    def body(dh_ref, mo_ref, gp_ref, g_ref, ya_ref, yb_ref, xg_ref, hr_ref, wo_ref, wl_ref, wa_ref,
             dmo_ref, dya_ref, dyb_ref, dgate_ref, dhr_ref, dxg_ref, do_ref, dgp_ref, dbg_ref):
        @pl.when(pl.program_id(0) == 0)
        def _():
            dgp_ref[...] = jnp.zeros_like(dgp_ref)
            dbg_ref[...] = jnp.zeros_like(dbg_ref)

        dmo, dgp = _rms_bwd(dh_ref[...], mo_ref[...], gp_ref[...])
        dgp_ref[...] += dgp
        dmob = dmo.astype(BF16)
        dmo_ref[...] = dmob
        dm = _nt(dmob, wo_ref[...])
        g0 = g_ref[:, 0:D].astype(F32)
        g1 = g_ref[:, D:2 * D].astype(F32)
        dyab = (dm * g0).astype(BF16)
        dybb = (dm * g1).astype(BF16)
        dya_ref[...] = dyab
        dyb_ref[...] = dybb
        dg0 = dm * ya_ref[...].astype(F32)
        dg1 = dm * yb_ref[...].astype(F32)
        dgate_ref[:, 0:D] = dg0.astype(BF16)
        dgate_ref[:, D:2 * D] = dg1.astype(BF16)
        dbg_ref[:, 0:D] += jnp.sum(dg0, axis=0, keepdims=True)
        dbg_ref[:, D:2 * D] += jnp.sum(dg1, axis=0, keepdims=True)
        dyain = _nt(dyab, wl_ref[...])
        do_ref[...] = _nt(dybb, wa_ref[...]).astype(BF16)
        gelu, gelu_grad = _gelu_and_grad(xg_ref[...])
        dhr_ref[...] = dyain * gelu
        dxg_ref[...] = (dyain * hr_ref[...] * gelu_grad).astype(BF16)

    b16 = jax.ShapeDtypeStruct((t, D), BF16)
    body, specs, operands = _behind(body, after)
    return pl.pallas_call(
        body, grid=(t // tm,), name="mix_bwd1",
        in_specs=specs + [_row_spec(tm, D), _row_spec(tm, D), _vec_spec(D), _row_spec(tm, 2 * D), _row_spec(tm, D),
                          _row_spec(tm, D), _row_spec(tm, D), _row_spec(tm, D), _WHOLE, _WHOLE, _WHOLE],
        out_specs=[_row_spec(tm, D), _row_spec(tm, D), _row_spec(tm, D), _row_spec(tm, 2 * D), _row_spec(tm, D),
                   _row_spec(tm, D), _row_spec(tm, D), _vec_spec(D), _vec_spec(2 * D)],
        out_shape=[b16, b16, b16, jax.ShapeDtypeStruct((t, 2 * D), BF16), jax.ShapeDtypeStruct((t, D), F32), b16, b16,
                   jax.ShapeDtypeStruct((1, D), F32), jax.ShapeDtypeStruct((1, 2 * D), F32)],
        compiler_params=_params("arbitrary"),
    )(*operands, dh2, mo, gpost, gate, ya, yb, xg, hr, w_o, w_lru, w_att)


def _rglru_bwd(dhr, hr, xc, r, ig, a, s, xr, conv_w, wa2, wx2, lam, after):
    t = dhr.shape[0]
    tm = _tile(t, TM_SCAN)
    nb8 = tm // 8
    nt = t // tm

    def body(dhr_ref, hr_ref, hrp_ref, xc_ref, r_ref, ig_ref, a_sc, s_ref, xr_ref, cw_ref, wa_ref, wx_ref, lam_ref,
             dxr_ref, dwa_ref, dwx_ref, dba_ref, dbx_ref, dlam_ref, dcw_ref, dcb_ref,
             ext_h, ext_d, g_sc, c_sc, nxt_sc):
        i = pl.program_id(0)
        first_tile = i == nt - 1

        @pl.when(i == 0)
        def _():
            c_sc[...] = jnp.zeros_like(c_sc)
            nxt_sc[...] = jnp.zeros_like(nxt_sc)
            for ref in (dwa_ref, dwx_ref, dba_ref, dbx_ref, dlam_ref, dcw_ref, dcb_ref):
                ref[...] = jnp.zeros_like(ref)

        lamv = lam_ref[...]
        sp = _softplus_neg(lamv)
        rv = r_ref[...]
        igv = ig_ref[...]
        xcv = xc_ref[...]
        a = a_sc[...]
        s = s_ref[...]

        def blk(jj, c):
            st = pl.multiple_of((nb8 - 1 - jj) * 8, 8)
            d8 = dhr_ref[pl.ds(st, 8), :]
            a8 = a_sc[pl.ds(st, 8), :]
            rows = [None] * 8
            for k in range(7, -1, -1):
                g = d8[k:k + 1, :] + c
                c = a8[k:k + 1, :] * g
                rows[k] = g
            g_sc[pl.ds(st, 8), :] = jnp.concatenate(rows, axis=0)
            return c

        c_sc[0:1, :] = lax.fori_loop(0, nb8, blk, c_sc[0:1, :])
        g = g_sc[...]
        ext_h[0:8, :] = jnp.where(first_tile, 0.0, hrp_ref[...])
        ext_h[8:8 + tm, :] = hr_ref[...]
        hprev = ext_h[pl.ds(7, tm), :]
        d_s = g * (igv * xcv)
        dig = g * s * xcv
        dxc = g * s * igv
        dla = (g * hprev) * a - d_s * ((a * a) / s)
        dr_pre = (dla * (-LRU_C * sp)) * (rv * (1.0 - rv))
        di_pre = dig * (igv * (1.0 - igv))
        dlam_ref[...] += jnp.sum(dla * (LRU_C * rv), axis=0, keepdims=True) * jax.nn.sigmoid(-lamv)
        dba_ref[...] += jnp.sum(dr_pre, axis=0, keepdims=True)
        dbx_ref[...] += jnp.sum(di_pre, axis=0, keepdims=True)
        drb = dr_pre.astype(BF16)
        dib = di_pre.astype(BF16)
        xcb = xcv.astype(BF16)
        ext_d[tm:tm + 8, :] = nxt_sc[...]
        for p in range(8):
            sl = slice(p * 128, (p + 1) * 128)
            ext_d[0:tm, sl] = dxc[:, sl] + _nt(drb[:, sl], wa_ref[p]) + _nt(dib[:, sl], wx_ref[p])
            dwa_ref[p] += _tn(xcb[:, sl], drb[:, sl])
            dwx_ref[p] += _tn(xcb[:, sl], dib[:, sl])
        dxcv = ext_d[0:tm, :]
        nxt_sc[...] = ext_d[0:8, :]
        dcb_ref[...] += jnp.sum(dxcv, axis=0, keepdims=True)
        xrv = xr_ref[...]
        dxr = jnp.zeros((tm, D), F32)
        for tap in range(4):
            ext_h[0:tm, :] = ext_d[pl.ds(3 - tap, tm), :]
            ahead = ext_h[0:tm, :]
            dxr = dxr + ahead * cw_ref[tap:tap + 1, :]
            dcw_ref[tap:tap + 1, :] += jnp.sum(ahead * xrv, axis=0, keepdims=True)
        dxr_ref[...] = dxr.astype(BF16)

    rev = pl.BlockSpec((tm, D), lambda i: (nt - 1 - i, 0))
    prev = pl.BlockSpec((8, D), lambda i: (jnp.maximum((nt - 1 - i) * nb8 - 1, 0), 0))
    full = lambda shape: pl.BlockSpec(shape, lambda i: tuple(0 for _ in shape))
    vec = jax.ShapeDtypeStruct((1, D), F32)
    blocks = jax.ShapeDtypeStruct((8, 128, 128), F32)
    body, specs, operands = _behind(body, after)
    return pl.pallas_call(
        body, grid=(nt,), name="rglru_bwd",
        in_specs=specs + [rev, rev, prev, rev, rev, rev, rev, rev, rev, full((4, D)), full((8, 128, 128)),
                          full((8, 128, 128)), _vec_spec(D)],
        out_specs=[rev, full((8, 128, 128)), full((8, 128, 128)), _vec_spec(D), _vec_spec(D), _vec_spec(D), full((4, D)),
                   _vec_spec(D)],
        out_shape=[jax.ShapeDtypeStruct((t, D), BF16), blocks, blocks, vec, vec, vec, jax.ShapeDtypeStruct((4, D), F32), vec],
        scratch_shapes=[pltpu.VMEM((tm + 8, D), F32), pltpu.VMEM((tm + 8, D), F32),
                        pltpu.VMEM((tm, D), F32), pltpu.VMEM((8, D), F32), pltpu.VMEM((8, D), F32)],
        compiler_params=_params("arbitrary"),
    )(*operands, dhr, hr, hr, xc, r, ig, a, s, xr, conv_w, wa2, wx2, lam)


def _attn_bwd(sink_rows, q, kp, vp, bias_t, mask, do):
    t = q.shape[0]
    tp = kp.shape[0]
    per_step = 8

    def body(sink_ref, q_ref, kp_ref, vp_ref, bias_ref, mask_ref, do_ref, dq_ref, dk_ref, dv_ref, dbias_ref, ds_ref):
        @pl.when(pl.program_id(0) == 0)
        def _():
            for ref in (dk_ref, dv_ref, dbias_ref, ds_ref):
                ref[...] = jnp.zeros_like(ref)

        maskv = mask_ref[...]
        lane_group = lax.broadcasted_iota(jnp.int32, (1, 4 * HEAD_DIM), 1) // HEAD_DIM

        def own_blocks(full):
            out = full[0:KP]
            for g in range(1, 4):
                out = jnp.where(lane_group == g, full[g * KP:(g + 1) * KP], out)
            return out

        dsc_sum, dsinks, dks, dvs = 0.0, [0.0] * 4, [], []
        for k in range(per_step):
            c = pl.program_id(0) * per_step + k
            chunk = slice(k * CHUNK, (k + 1) * CHUNK)
            st = pl.multiple_of(c * CHUNK, CHUNK)
            kbd = _block_diag(kp_ref[pl.ds(st, KP), :], maskv)
            vbd = _block_diag(vp_ref[pl.ds(st, KP), :], maskv)
            q_all = _stack_heads(q_ref[chunk, :])
            do_all = _stack_heads(do_ref[chunk, :])
            valid = lax.broadcasted_iota(jnp.int32, (KP, 1), 0) + c * CHUNK >= PAD_KEYS
            qk = _nt(kbd, q_all)
            dp = _nt(vbd, do_all)
            ps, dscs = [], []
            for g in range(4):
                rows = slice(g * KP, (g + 1) * KP)
                p, sink_p = _group_softmax(qk[rows], bias_ref[rows, :], sink_ref[g:g + 1, :], valid)
                delta = jnp.sum(p * dp[rows], axis=0, keepdims=True)
                ps.append(p)
                dscs.append(p * (dp[rows] - delta))
                dsinks[g] = dsinks[g] - sink_p * delta
            dsc = jnp.concatenate(dscs, axis=0)
            dsc_sum = dsc_sum + dsc
            dsb = (dsc * (HEAD_DIM ** -0.5)).astype(BF16)
            dq_ref[chunk, :] = _unstack_heads(_tn(dsb, kbd)).astype(BF16)
            dks.append((st, own_blocks(_nn(dsb, q_all))))
            dvs.append((st, own_blocks(_nn(jnp.concatenate(ps, axis=0).astype(BF16), do_all))))
        dbias_ref[...] += dsc_sum
        for g in range(4):
            ds_ref[g:g + 1, :] += dsinks[g]
        for (st, dkw), (_, dvw) in zip(dks, dvs):
            dk_ref[pl.ds(st, KP), :] += dkw
            dv_ref[pl.ds(st, KP), :] += dvw

    full = lambda shape: pl.BlockSpec(shape, lambda i: tuple(0 for _ in shape))
    return pl.pallas_call(
        body, grid=(t // (per_step * CHUNK),), name="attn_bwd",
        in_specs=[_WHOLE, _row_spec(per_step * CHUNK, D), _WHOLE, _WHOLE, _WHOLE, _WHOLE, _row_spec(per_step * CHUNK, D)],
        out_specs=[_row_spec(per_step * CHUNK, D), full((tp, KV_W)), full((tp, KV_W)), full((4 * KP, 4 * CHUNK)),
                   full((8, 4 * CHUNK))],
        out_shape=[jax.ShapeDtypeStruct((t, D), BF16), jax.ShapeDtypeStruct((tp, KV_W), F32),
                   jax.ShapeDtypeStruct((tp, KV_W), F32), jax.ShapeDtypeStruct((4 * KP, 4 * CHUNK), F32),
                   jax.ShapeDtypeStruct((8, 4 * CHUNK), F32)],
        compiler_params=_params("arbitrary"),
    )(sink_rows, q, kp, vp, bias_t, mask, do)


def _mix_bwd2(dproj, dgate, h1, dh2, gmix, w_in_g, w_gate_g, after):
    t = h1.shape[0]
    tm = _tile(t)

    def body(dp_ref, dg_ref, h_ref, dh_ref, g_ref, win_ref, wg_ref, dh1_ref, dgm_ref):
        @pl.when(pl.program_id(0) == 0)
        def _():
            dgm_ref[...] = jnp.zeros_like(dgm_ref)

        du = jnp.zeros((tm, D), F32)
        for s in range(NSH):
            du = du + _nt(dp_ref[:, s * IN_S:(s + 1) * IN_S], win_ref[s])
            du = du + _nt(dg_ref[:, s * GATE_S:(s + 1) * GATE_S], wg_ref[s])
        dxn, dg = _rms_bwd(du, h_ref[...], g_ref[...])
        dgm_ref[...] += dg
        dh1_ref[...] = dh_ref[...] + dxn

    body, specs, operands = _behind(body, after)
    return pl.pallas_call(
        body, grid=(t // tm,), name="mix_bwd2",
        in_specs=specs + [_row_spec(tm, NSH * IN_S), _row_spec(tm, 2 * D), _row_spec(tm, D), _row_spec(tm, D), _vec_spec(D),
                          _WHOLE, _WHOLE],
        out_specs=[_row_spec(tm, D), _vec_spec(D)],
        out_shape=[jax.ShapeDtypeStruct((t, D), F32), jax.ShapeDtypeStruct((1, D), F32)],
        compiler_params=_params("arbitrary"),
    )(*operands, dproj, dgate, h1, dh2, gmix, w_in_g, w_gate_g)


def _band_onehot():
    nb = N_BUCKETS // 2
    max_exact = nb // 2
    rel = jnp.arange(KB)[None, :] - PAD_KEYS - jnp.arange(CHUNK)[:, None]
    ret = jnp.where(rel > 0, nb, 0)
    n = jnp.abs(rel)
    nf = jnp.maximum(n, 1).astype(jnp.float32)
    large = max_exact + (jnp.log(nf / max_exact) / math.log(128 / max_exact) * (nb - max_exact)).astype(jnp.int32)
    large = jnp.minimum(large, nb - 1)
    buckets = (ret + jnp.where(n < max_exact, n, large)).reshape(1, CHUNK * KB)
    return (buckets == jnp.arange(N_BUCKETS)[:, None]).astype(F32)


def _pair_blocks(w):
    pairs = w.reshape(8, 2, 64, 64)
    z = jnp.zeros((8, 64, 64), w.dtype)
    return jnp.concatenate([jnp.concatenate([pairs[:, 0], z], axis=2), jnp.concatenate([z, pairs[:, 1]], axis=2)], axis=1)


def _unpair_blocks(w2):
    return jnp.stack([w2[:, 0:64, 0:64], w2[:, 64:128, 64:128]], axis=1).reshape(16, 64, 64)


def _local_step(x, target, weights, sm, reducer):
    row = lambda v: v.reshape(1, -1)
    onehot_t = _band_onehot()
    bias = _bias_fwd(sm["rel_bias"].T, onehot_t).reshape(4, 4, CHUNK, KB)
    bias_t = jnp.pad(jnp.transpose(bias, (0, 3, 1, 2)), ((0, 0), (0, KP - KB), (0, 0), (0, 0))).reshape(4 * KP, 4 * CHUNK)
    sink_rows = jnp.pad(jnp.repeat(sm["attn_sinks"].reshape(4, 4), CHUNK, axis=1), ((0, 4), (0, 0)))
    grp = jnp.arange(4 * KP)[:, None] // KP == jnp.arange(4 * HEAD_DIM)[None, :] // HEAD_DIM
    mask = (grp & (jnp.arange(4 * KP)[:, None] % KP < KB)).astype(BF16)
    wa2 = _pair_blocks(sm["rg_a_w"]).astype(BF16)
    wx2 = _pair_blocks(sm["rg_x_w"]).astype(BF16)
    wg = dict(weights("ffn1_up", [bias_t, sink_rows, mask, wa2, wx2]))
    sm = dict(sm, conv_w=wg["conv_w"])

    n1, a1, b1, hm1 = _ffn_up(x, row(sm["ffn1_pre_g"]), wg["ffn1_w1"], wg["ffn1_w3"], "ffn1_up")
    wg.update(weights("ffn1_down", hm1))
    h1, f1 = _ffn_down(x, hm1, wg["ffn1_w2"], row(sm["ffn1_post_g"]), "ffn1_down")
    wg.update(weights("mix_in", h1))
    u, q, k, v, xr, xg, gate = _mix_proj(h1, row(sm["mix_pre_g"]), wg["w_in"], wg["w_gate"], row(sm["b_gate"]))
    token = weights("mix_out", u, begin=True)
    hr, yain, xc, r, ig, lru_a, lru_s = _rglru_fwd(xr, xg, sm["conv_w"], row(sm["conv_b"]), wa2, row(sm["rg_a_b"]), wx2,
                                                   row(sm["rg_x_b"]), row(sm["lru_lambda"]), token)
    token = weights("ffn2", hr, begin=True)
    kp = jnp.pad(k, ((PAD_KEYS, KP - KB), (0, 0)))
    vp = jnp.pad(v, ((PAD_KEYS, KP - KB), (0, 0)))
    o = _attn_fwd(sink_rows, q, kp, vp, bias_t, mask, token)
    wg.update(weights("mix_out", o))
    w_lru = wg["w_lru_out"].reshape(D, D)
    w_att = wg["w_attn_out"].reshape(D, D)
    w_o = wg["w_o"].reshape(D, D)
    wg.update(weights("ffn2", o))
    h2, mo, merged, ya, yb = _merge_fwd(yain, o, gate, h1, w_lru, w_att, w_o, row(sm["mix_post_g"]))
    dy, a2, b2, hm2, f2, sq = _ffn_fwd(h2, row(sm["ffn2_pre_g"]), wg["ffn2_w1"], wg["ffn2_w3"], wg["ffn2_w2"],
                                       row(sm["ffn2_post_g"]), "ffn2_fwd", target)

    big, small = {}, {}
    dh2, n2, da2, db2, df2, small["ffn2_pre_g"], small["ffn2_post_g"] = _ffn_bwd(
        dy, h2, f2, a2, b2, row(sm["ffn2_pre_g"]), row(sm["ffn2_post_g"]), wg["ffn2_w1"], wg["ffn2_w3"], wg["ffn2_w2"],
        "ffn2_bwd")
    big["ffn2_w1"] = _wgrad_rows(da2, n2, "dw_ffn2_w1")
    big["ffn2_w3"] = _wgrad_rows(db2, n2, "dw_ffn2_w3")
    big["ffn2_w2"] = _wgrad_rows(hm2, df2, "dw_ffn2_w2")
    token = reducer.begin("ffn2", {n: big[n] for n in ("ffn2_w1", "ffn2_w3", "ffn2_w2")})
    dmo, dya, dyb, dgate, dhr, dxg, do, small["mix_post_g"], small["b_gate"] = _mix_bwd1(
        dh2, mo, row(sm["mix_post_g"]), gate, ya, yb, xg, hr, w_o, w_lru, w_att, token)
    big["w_o"] = _wgrad_sq(merged, dmo, "dw_w_o").reshape(NSH, D // NSH, D)
    big["w_lru_out"] = _wgrad_sq(yain, dya, "dw_w_lru_out").reshape(NSH, D // NSH, D)
    big["w_attn_out"] = _wgrad_sq(o, dyb, "dw_w_attn_out").reshape(NSH, D // NSH, D)
    token = reducer.advance("ffn2", big["w_attn_out"])
    (dxr, dwa2, dwx2, small["rg_a_b"], small["rg_x_b"], small["lru_lambda"], small["conv_w"], small["conv_b"]) = _rglru_bwd(
        dhr, hr, xc, r, ig, lru_a, lru_s, xr, sm["conv_w"], wa2, wx2, row(sm["lru_lambda"]), token)
    small["rg_a_w"] = _unpair_blocks(dwa2)
    small["rg_x_w"] = _unpair_blocks(dwx2)
    dq, dkp, dvp, dbias_t, ds_rows = _attn_bwd(sink_rows, q, kp, vp, bias_t, mask, do)
    dbias = jnp.transpose(dbias_t.reshape(4, KP, 4, CHUNK)[:, :KB], (0, 2, 3, 1)).reshape(N_HEADS, CHUNK * KB)
    drel_t, dsinks = _bias_bwd(dbias, onehot_t, ds_rows)
    small["attn_sinks"] = dsinks[0:4, 0:4].reshape(N_HEADS)
    small["rel_bias"] = drel_t.T
    t = x.shape[0]
    dproj = jnp.concatenate([dq, dkp[PAD_KEYS:PAD_KEYS + t].astype(BF16), dvp[PAD_KEYS:PAD_KEYS + t].astype(BF16), dxr, dxg],
                            axis=1)
    big["w_in"] = _wgrad_cols(u, dproj, IN_S, "dw_w_in")
    big["w_gate"] = _wgrad_cols(u, dgate, GATE_S, "dw_w_gate")
    token = reducer.begin("mix", {n: big[n] for n in ("w_in", "w_gate", "w_lru_out", "w_attn_out", "w_o")})
    dh1, small["mix_pre_g"] = _mix_bwd2(dproj, dgate, h1, dh2, row(sm["mix_pre_g"]), wg["w_in"], wg["w_gate"], token)
    da1, db1, df1, small["ffn1_post_g"] = _ffn_bwd_acts(dh1, f1, a1, b1, row(sm["ffn1_post_g"]), wg["ffn1_w2"],
                                                        "ffn1_bwd_acts")
    token = reducer.advance("mix", df1)
    big["ffn1_w1"] = _wgrad_rows(da1, n1, "dw_ffn1_w1", token)
    big["ffn1_w3"] = _wgrad_rows(db1, n1, "dw_ffn1_w3", token)
    big["ffn1_w2"] = _wgrad_rows(hm1, df1, "dw_ffn1_w2", token)
    token = reducer.begin("ffn1", {n: big[n] for n in ("ffn1_w1", "ffn1_w3", "ffn1_w2")})
    dx, small["ffn1_pre_g"] = _ffn_bwd_input(dh1, x, da1, db1, row(sm["ffn1_pre_g"]), wg["ffn1_w1"], wg["ffn1_w3"],
                                             "ffn1_bwd_input", token)
    return sq, dx, big, small


_ANY = pl.BlockSpec(memory_space=pl.ANY)


def _place():
    return lax.axis_index("x"), lax.axis_index("y"), lax.axis_index("c")


def _other_chips(x, y):
    return [(1 - x, y), (x, 1 - y), (1 - x, 1 - y)]


_HBM = pl.BlockSpec(memory_space=pltpu.HBM)
_SEM = pl.BlockSpec(memory_space=pltpu.SEMAPHORE)
_EFFECT = pltpu.SideEffectType.DATAFLOW_SIDE_EFFECTING


def _cast_into_slot(w, chip, name, after=None):
    r, cc = w.shape
    rows = r // 4

    def body(chip_ref, *refs):
        w_ref, o_ref = refs[-2:]
        o_ref[...] = w_ref[...].astype(BF16)

    extra = [] if after is None else [after]
    return pl.pallas_call(
        body, name=name, out_shape=jax.ShapeDtypeStruct((NSH, r, cc), BF16),
        grid_spec=pltpu.PrefetchScalarGridSpec(
            num_scalar_prefetch=1, grid=(4,), in_specs=[_ANY] * len(extra) + [pl.BlockSpec((rows, cc), lambda i, chip: (i, 0))],
            out_specs=pl.BlockSpec((None, rows, cc), lambda i, chip: (chip[0], i, 0))),
        compiler_params=_params("arbitrary"))(chip, *extra, w)


def _piece(ref, slot, c):
    if ref.dtype == F32:
        return ref.at[slot]
    rh = ref.shape[1] // 2
    return ref.at[slot, pl.ds(pl.multiple_of(c * rh, 16), rh), :]


def _gather_start(stages, name):
    flat = [b for stage in stages for b in stage]
    n, ns = len(flat), len(stages)

    def body(*refs):
        ins, sems, token = refs[:n], refs[n:n + 2 * ns], refs[-1]
        x, y, c = _place()
        me = 2 * x + y
        k = 0
        for s, stage in enumerate(stages):
            for i in range(len(stage)):
                for j, (px, py) in enumerate(_other_chips(x, y)):
                    piece = _piece(ins[k], me, c)
                    pltpu.make_async_remote_copy(src_ref=piece, dst_ref=piece, send_sem=sems[2 * s].at[3 * i + j],
                                                 recv_sem=sems[2 * s + 1].at[3 * i + j], device_id=(px, py, c),
                                                 device_id_type=MESH).start()
                k += 1
        token[...] = jnp.zeros_like(token)

    sem_shapes = [pltpu.SemaphoreType.DMA((3 * len(stage),)) for stage in stages for _ in range(2)]
    outs = pl.pallas_call(
        body, name=name, in_specs=[_HBM] * n,
        out_specs=[_SEM] * (2 * ns) + [_HBM] * n + [pl.BlockSpec(memory_space=pltpu.VMEM)],
        out_shape=sem_shapes + [pltpu.HBM(b.shape, b.dtype) for b in flat] + [jax.ShapeDtypeStruct((8, 128), F32)],
        input_output_aliases={i: 2 * ns + i for i in range(n)},
        compiler_params=pltpu.CompilerParams(has_side_effects=_EFFECT),
    )(*[pltpu.with_memory_space_constraint(b, pltpu.HBM) for b in flat])
    sems, bufs, token = outs[:2 * ns], list(outs[2 * ns:2 * ns + n]), outs[-1]
    per_stage, k = [], 0
    for s, stage in enumerate(stages):
        per_stage.append((sems[2 * s], sems[2 * s + 1], bufs[k:k + len(stage)]))
        k += len(stage)
    return per_stage, token


def _gather_wait(send_sems, recv_sems, bufs, after, name):
    n = len(bufs)

    def body(*refs):
        ins, ssem, rsem = refs[:n], refs[n], refs[n + 1]
        x, y, c = _place()
        me = 2 * x + y
        for i in range(n):
            for j, (px, py) in enumerate(_other_chips(x, y)):
                cp = pltpu.make_async_remote_copy(src_ref=_piece(ins[i], me, c), dst_ref=_piece(ins[i], 2 * px + py, c),
                                                  send_sem=ssem.at[3 * i + j], recv_sem=rsem.at[3 * i + j],
                                                  device_id=(px, py, c), device_id_type=MESH)
                cp.wait_send()
                cp.wait_recv()

    afters = list(after) if isinstance(after, (list, tuple)) else [after]
    return pl.pallas_call(
        body, name=name, in_specs=[_HBM] * n + [_SEM, _SEM] + [_ANY] * len(afters), out_specs=[_HBM] * n,
        out_shape=[pltpu.HBM(b.shape, b.dtype) for b in bufs], input_output_aliases={i: i for i in range(n)},
        compiler_params=pltpu.CompilerParams(has_side_effects=_EFFECT),
    )(*bufs, send_sems, recv_sems, *afters)


def _sibling_fill(bufs, name):
    n = len(bufs)

    def body(*refs):
        ins, outs = refs[:n], refs[n:2 * n]
        send_sems, recv_sems = refs[2 * n:]
        x, y, c = _place()
        copies = []
        for i in range(n):
            for j, (px, py) in enumerate(_other_chips(x, y)):
                copies.append(pltpu.make_async_remote_copy(
                    src_ref=_piece(ins[i], 2 * px + py, c), dst_ref=_piece(outs[i], 2 * px + py, c),
                    send_sem=send_sems.at[3 * i + j], recv_sem=recv_sems.at[3 * i + j], device_id=(x, y, 1 - c),
                    device_id_type=MESH))
                copies[-1].start()
        for cp in copies:
            cp.wait()

    return pl.pallas_call(
        body, name=name, in_specs=[_ANY] * n, out_specs=[_ANY] * n,
        out_shape=[jax.ShapeDtypeStruct(b.shape, b.dtype) for b in bufs], input_output_aliases={i: i for i in range(n)},
        scratch_shapes=[pltpu.SemaphoreType.DMA((3 * n,)), pltpu.SemaphoreType.DMA((3 * n,))],
        compiler_params=pltpu.CompilerParams(has_side_effects=True),
    )(*bufs)


def _swap_plan(srcs, lands):
    x, y, c = _place()
    plan = []
    for src, land in zip(srcs, lands):
        rh = src.shape[1] // 2
        plan.append((src.at[:, pl.ds(pl.multiple_of((1 - c) * rh, 16), rh), :], land, (x, y, 1 - c)))
    return plan


def _owners_plan(srcs, lands):
    x, y, c = _place()
    return [(src.at[2 * px + py], land.at[j], (px, py, c))
            for src, land in zip(srcs, lands) for j, (px, py) in enumerate(_other_chips(x, y))]


def _exchange_start(srcs, lands, plan, copies, name):
    n, m = len(srcs), len(srcs) + len(lands)

    def body(*refs):
        send_sems, recv_sems, token = refs[m], refs[m + 1], refs[-1]
        for k, (src, dst, dev) in enumerate(plan(refs[:n], refs[n:m])):
            pltpu.make_async_remote_copy(src_ref=src, dst_ref=dst, send_sem=send_sems.at[k], recv_sem=recv_sems.at[k],
                                         device_id=dev, device_id_type=MESH).start()
        token[...] = jnp.zeros_like(token)

    both = list(srcs) + list(lands)
    outs = pl.pallas_call(
        body, name=name, in_specs=[_HBM] * m,
        out_specs=[_SEM, _SEM] + [_HBM] * m + [pl.BlockSpec(memory_space=pltpu.VMEM)],
        out_shape=[pltpu.SemaphoreType.DMA((copies,)), pltpu.SemaphoreType.DMA((copies,))]
        + [pltpu.HBM(b.shape, b.dtype) for b in both] + [jax.ShapeDtypeStruct((8, 128), F32)],
        input_output_aliases={i: 2 + i for i in range(m)},
        compiler_params=pltpu.CompilerParams(has_side_effects=_EFFECT),
    )(*[pltpu.with_memory_space_constraint(b, pltpu.HBM) for b in both])
    return (outs[0], outs[1]), list(outs[2:2 + n]), list(outs[2 + n:2 + m]), outs[-1]


def _exchange_wait(sems, srcs, lands, plan, after, name):
    n, m = len(srcs), len(srcs) + len(lands)

    def body(*refs):
        send_sems, recv_sems = refs[m], refs[m + 1]
        for k, (src, dst, dev) in enumerate(plan(refs[:n], refs[n:m])):
            cp = pltpu.make_async_remote_copy(src_ref=src, dst_ref=dst, send_sem=send_sems.at[k], recv_sem=recv_sems.at[k],
                                              device_id=dev, device_id_type=MESH)
            cp.wait_send()
            cp.wait_recv()

    both = list(srcs) + list(lands)
    afters = list(after) if isinstance(after, (list, tuple)) else [after]
    outs = pl.pallas_call(
        body, name=name, in_specs=[_HBM] * m + [_SEM, _SEM] + [_ANY] * len(afters), out_specs=[_HBM] * m,
        out_shape=[pltpu.HBM(b.shape, b.dtype) for b in both], input_output_aliases={i: i for i in range(m)},
        compiler_params=pltpu.CompilerParams(has_side_effects=_EFFECT),
    )(*both, sems[0], sems[1], *afters)
    return list(outs[:n]), list(outs[n:])


def _fill_plan(bufs, _):
    x, y, c = _place()
    return [(_piece(buf, 2 * px + py, c), _piece(buf, 2 * px + py, c), (x, y, 1 - c))
            for buf in bufs for px, py in _other_chips(x, y)]


class _Reducer:
    def __init__(self, where):
        self.state = {}
        self.where = where

    def begin(self, stage, grads):
        names = list(grads)
        full = [grads[n] for n in names]
        lands = [lax.empty((NSH, g.shape[1] // 2, g.shape[2]), g.dtype) for g in full]
        sems, full, lands, token = _exchange_start(full, lands, _swap_plan, len(full), "swap_start_" + stage)
        self.state[stage] = (names, sems, full, lands)
        return token

    def advance(self, stage, after):
        names, sems, full, lands = self.state[stage]
        full, got = _exchange_wait(sems, full, lands, _swap_plan, after, "swap_wait_" + stage)
        sums, own = _chip_sums(full, got, self.where, "chip_sums_" + stage)
        lands = [lax.empty((3,) + s.shape[1:], BF16) for s in sums]
        sems, sent, lands, token = _exchange_start(sums, lands, _owners_plan, 3 * len(sums), "owners_start_" + stage)
        self.state[stage] = (names, own, sems, sent, lands)
        return token

    def finish(self, stage, after):
        names, own, sems, sent, lands = self.state[stage]
        _, got = _exchange_wait(sems, sent, lands, _owners_plan, after, "owners_wait_" + stage)
        return dict(zip(names, _owner_sums(own, got, "owner_sums_" + stage)))


def _chip_sums(gs, gots, where, name):
    n = len(gs)

    def body(where_ref, *refs):
        g_refs, got_refs, hb_refs, own_refs = (refs[k * n:(k + 1) * n] for k in range(4))
        mine = pl.program_id(0) == where_ref[1]
        for g_ref, got_ref, hb_ref, own_ref in zip(g_refs, got_refs, hb_refs, own_refs):
            h = g_ref[...].astype(F32) + got_ref[...].astype(F32)
            hb_ref[...] = h.astype(BF16)

            @pl.when(mine)
            def _():
                own_ref[...] = h

    halves = [(g.shape[1] // 2, g.shape[2]) for g in gs]
    slot = [pl.BlockSpec((None, rh, cc), lambda s, where: (s, 0, 0)) for rh, cc in halves]
    outs = pl.pallas_call(
        body, name=name,
        grid_spec=pltpu.PrefetchScalarGridSpec(
            num_scalar_prefetch=1, grid=(NSH,),
            in_specs=[pl.BlockSpec((None, rh, cc), lambda s, where: (s, where[0], 0)) for rh, cc in halves] + slot,
            out_specs=slot + [pl.BlockSpec((rh, cc), lambda s, where: (0, 0)) for rh, cc in halves]),
        out_shape=[jax.ShapeDtypeStruct((NSH, rh, cc), BF16) for rh, cc in halves]
        + [jax.ShapeDtypeStruct((rh, cc), F32) for rh, cc in halves],
        compiler_params=_params("arbitrary"),
    )(where, *gs, *gots)
    return list(outs[:n]), list(outs[n:])


def _owner_sums(owns, gots, name):
    n = len(owns)

    def body(*refs):
        own_refs, got_refs, o_refs = (refs[k * n:(k + 1) * n] for k in range(3))
        for own_ref, got_ref, o_ref in zip(own_refs, got_refs, o_refs):
            o_ref[...] = ((own_ref[...] + got_ref[0].astype(F32)) + got_ref[1].astype(F32)) + got_ref[2].astype(F32)

    blocks = [(o.shape[0] // 2, o.shape[1]) for o in owns]
    rows = [pl.BlockSpec(b, lambda i: (i, 0)) for b in blocks]
    return pl.pallas_call(
        body, grid=(2,), name=name,
        in_specs=rows + [pl.BlockSpec((3,) + b, lambda i: (0, i, 0)) for b in blocks], out_specs=rows,
        out_shape=[jax.ShapeDtypeStruct(o.shape, F32) for o in owns], compiler_params=_params("arbitrary"),
    )(*owns, *gots)


def _sibling_plan(srcs, lands):
    x, y, c = _place()
    return [(src, land, (x, y, 1 - c)) for src, land in zip(srcs, lands)]


def _all_reduce_small(part):
    def body(p_ref, o_ref, rbuf, send1, recv1, send2, recv2):
        x, y, c = _place()
        me = 4 * x + 2 * y + c
        peers = []
        for k in range(1, 8):
            px, py, pc = x ^ ((k >> 2) & 1), y ^ ((k >> 1) & 1), c ^ (k & 1)
            peers.append((k, (px, py, pc), 4 * px + 2 * py + pc))

        def rows(d):
            return pl.ds(pl.multiple_of(d * SMALL_SLICE, 8), SMALL_SLICE)

        first = [pltpu.make_async_remote_copy(src_ref=p_ref.at[rows(idx), :], dst_ref=rbuf.at[me], send_sem=send1.at[k],
                                              recv_sem=recv1.at[k], device_id=dev, device_id_type=MESH)
                 for k, dev, idx in peers]
        for cp in first:
            cp.start()
        rbuf[me] = p_ref[rows(me), :]
        for k, dev, idx in peers:
            pltpu.make_async_remote_copy(src_ref=p_ref.at[rows(idx), :], dst_ref=rbuf.at[idx], send_sem=send1.at[k],
                                         recv_sem=recv1.at[k], device_id=dev, device_id_type=MESH).wait_recv()
        acc = rbuf[0]
        for d in range(1, 8):
            acc = acc + rbuf[d]
        o_ref[rows(me), :] = acc
        second = [pltpu.make_async_remote_copy(src_ref=o_ref.at[rows(me), :], dst_ref=o_ref.at[rows(me), :],
                                               send_sem=send2.at[k], recv_sem=recv2.at[k], device_id=dev, device_id_type=MESH)
                  for k, dev, idx in peers]
        for cp in second:
            cp.start()
        for k, dev, idx in peers:
            pltpu.make_async_remote_copy(src_ref=o_ref.at[rows(me), :], dst_ref=o_ref.at[rows(idx), :], send_sem=send2.at[k],
                                         recv_sem=recv2.at[k], device_id=dev, device_id_type=MESH).wait_recv()
        for cp in first + second:
            cp.wait_send()

    return pl.pallas_call(
        body, name="all_reduce_small", in_specs=[_WHOLE], out_specs=_WHOLE,
        out_shape=jax.ShapeDtypeStruct((SMALL_ROWS, 128), F32),
        scratch_shapes=[pltpu.VMEM((8, SMALL_SLICE, 128), F32)] + [pltpu.SemaphoreType.DMA((8,))] * 4,
        compiler_params=pltpu.CompilerParams(has_side_effects=True),
    )(part)


def _adamw_update(w, gv, m, v):
    nm = ADAM_B1 * m + (1.0 - ADAM_B1) * gv
    nv = ADAM_B2 * v + (1.0 - ADAM_B2) * (gv * gv)
    m_hat = nm / (1.0 - ADAM_B1 ** ADAM_STEP)
    v_hat = nv / (1.0 - ADAM_B2 ** ADAM_STEP)
    return -ADAM_LR * (m_hat / (jnp.sqrt(v_hat) + ADAM_EPS) + ADAM_WD * w), nm, nv


def _adamw_small(ws, gs, ms, vs, after):
    n = len(ws)

    def body(*refs):
        w_refs, g_refs, m_refs, v_refs, d_refs, nm_refs, nv_refs = (refs[k * n:(k + 1) * n] for k in range(7))
        for i in range(n):
            d_refs[i][...], nm_refs[i][...], nv_refs[i][...] = _adamw_update(
                w_refs[i][...], g_refs[i][...], m_refs[i][...], v_refs[i][...])

    out = [jax.ShapeDtypeStruct(w.shape, F32) for w in ws]
    body, specs, operands = _behind(body, after)
    outs = pl.pallas_call(body, in_specs=specs + [_WHOLE] * (4 * n), out_specs=[_WHOLE] * (3 * n), out_shape=out * 3,
                          name="adamw_small", compiler_params=_params())(*operands, *ws, *gs, *ms, *vs)
    return outs[:n], outs[n:2 * n], outs[2 * n:]


def _adamw_halves(ws, mines, theirs, ms, vs, name):
    n = len(ws)
    steps = 2

    def body(*refs):
        w_refs, mine_refs, theirs_refs, m_refs, v_refs, g_refs, d_refs, nm_refs, nv_refs = (
            refs[k * n:(k + 1) * n] for k in range(9))
        is_mine = pl.program_id(0) == lax.axis_index("c")
        for i in range(n):
            gv = jnp.where(is_mine, mine_refs[i][...], theirs_refs[i][...])
            g_refs[i][...] = gv
            d_refs[i][...], nm_refs[i][...], nv_refs[i][...] = _adamw_update(w_refs[i][...], gv, m_refs[i][...], v_refs[i][...])

    blocks = [(h.shape[0] // steps, h.shape[1]) for h in mines]
    whole = [pl.BlockSpec(b, lambda h, i: (steps * h + i, 0)) for b in blocks]
    half = [pl.BlockSpec(b, lambda h, i: (i, 0)) for b in blocks]
    out = [jax.ShapeDtypeStruct(w.shape, F32) for w in ws]
    outs = pl.pallas_call(body, grid=(2, steps), in_specs=whole + half + half + whole + whole, out_specs=whole * 4,
                          out_shape=out * 4, name=name, compiler_params=_params("arbitrary", "arbitrary"),
                          )(*ws, *mines, *theirs, *ms, *vs)
    return [tuple(outs[k * n + i] for k in range(4)) for i in range(n)]


SMALL_USED = sum(size for _, size in SMALL) // 128


def _pack_small(vals, tail=None):
    parts = []
    for name, size in SMALL:
        flat = vals[name].reshape(-1).astype(F32)
        parts.append(jnp.pad(flat, (0, size - flat.shape[0])))
    if tail is not None:
        parts.append(tail.reshape(128))
    flat = jnp.concatenate(parts)
    return jnp.pad(flat, (0, SMALL_ROWS * 128 - flat.shape[0])).reshape(SMALL_ROWS, 128)


def _unpack_small(packed, shapes):
    flat = packed.reshape(-1)
    out, off = {}, 0
    for name, size in SMALL:
        n = math.prod(shapes[name])
        out[name] = flat[off:off + n].reshape(shapes[name])
        off += size
    return out


def kernel(x, ffn1_pre_g, ffn1_w1, ffn1_w3, ffn1_w2, ffn1_post_g, mix_pre_g, w_in, conv_w, conv_b, rg_a_w, rg_a_b, rg_x_w, rg_x_b, lru_lambda, w_lru_out, attn_sinks, rel_bias, w_attn_out, w_gate, b_gate, w_o, mix_post_g, ffn2_pre_g, ffn2_w1, ffn2_w3, ffn2_w2, ffn2_post_g, loss_target, m_ffn1_pre_g, m_ffn1_w1, m_ffn1_w3, m_ffn1_w2, m_ffn1_post_g, m_mix_pre_g, m_w_in, m_conv_w, m_conv_b, m_rg_a_w, m_rg_a_b, m_rg_x_w, m_rg_x_b, m_lru_lambda, m_w_lru_out, m_attn_sinks, m_rel_bias, m_w_attn_out, m_w_gate, m_b_gate, m_w_o, m_mix_post_g, m_ffn2_pre_g, m_ffn2_w1, m_ffn2_w3, m_ffn2_w2, m_ffn2_post_g, v_ffn1_pre_g, v_ffn1_w1, v_ffn1_w3, v_ffn1_w2, v_ffn1_post_g, v_mix_pre_g, v_w_in, v_conv_w, v_conv_b, v_rg_a_w, v_rg_a_b, v_rg_x_w, v_rg_x_b, v_lru_lambda, v_w_lru_out, v_attn_sinks, v_rel_bias, v_w_attn_out, v_w_gate, v_b_gate, v_w_o, v_mix_post_g, v_ffn2_pre_g, v_ffn2_w1, v_ffn2_w3, v_ffn2_w2, v_ffn2_post_g):
    given = dict(locals())
    chip = 2 * lax.axis_index("x") + lax.axis_index("y")
    transposed = ("ffn1_w1", "ffn1_w3", "ffn2_w1", "ffn2_w3")

    def shard(name, moment=""):
        w = given[moment + name][0]
        return w.T if name in transposed else w

    def unshard(name, w):
        return (w.T if name in transposed else w)[None]

    def only_my_columns(a):
        parts = a.reshape(1, 4, NSH, D // NSH)
        return sum(jnp.where(chip == s, parts[:, :, s], 0.0) for s in range(NSH))

    chip_arr = jnp.reshape(chip, (1,)).astype(jnp.int32)
    stage_names = {"ffn1_up": ["ffn1_w1", "ffn1_w3", "conv_w"],
                   "ffn1_down": ["ffn1_w2"],
                   "mix_in": ["w_in", "w_gate"],
                   "mix_out": ["w_lru_out", "w_attn_out", "w_o"],
                   "ffn2": ["ffn2_w1", "ffn2_w3", "ffn2_w2"]}
    in_flight, started = {}, None
    for stage, names in stage_names.items():
        bufs = [jnp.where(lax.broadcasted_iota(jnp.int32, (NSH, 4, D // NSH), 0) == chip, given[n], 0.0) if n == "conv_w"
                else _cast_into_slot(shard(n), chip_arr, "cast_" + n, started) for n in names]
        (in_flight[stage],), started = _gather_start([bufs], "gather_start_" + stage)
    all_started = started

    filling = {}

    def weights(stage, after, begin=False):
        names = stage_names[stage]
        halves_of = [n for n in names if n != "conv_w"]
        if stage in filling:
            filled, _ = _exchange_wait(filling.pop(stage), *filling.pop(stage + "/bufs"), _fill_plan, after,
                                       "fill_wait_" + stage)
            return dict(zip(halves_of, filled))
        send_sems, recv_sems, landing = in_flight[stage]
        if stage == "ffn1_up":
            after = [all_started] + list(after)
        landed = dict(zip(names, _gather_wait(send_sems, recv_sems, landing, after, "gather_wait_" + stage)))
        halves = [landed[n] for n in halves_of]
        if begin:
            filling[stage], bufs, _, token = _exchange_start(halves, [], _fill_plan, 3 * len(halves), "fill_start_" + stage)
            filling[stage + "/bufs"] = (bufs, [])
            return token
        out = dict(zip(halves_of, _sibling_fill(halves, "sibling_fill_" + stage)))
        if "conv_w" in names:
            out["conv_w"] = jnp.transpose(landed["conv_w"], (1, 0, 2)).reshape(4, D)
        return out

    small_shapes = {n: given[n].shape for n, _ in SMALL}
    small_shapes["conv_w"] = (1, 4, D)
    sm = {n: (given[n][0] if given[n].shape[0] == 1 and n != "rel_bias" else given[n]) for n, _ in SMALL if n != "conv_w"}

    reducer = _Reducer(jnp.stack([lax.axis_index("c"), chip]).astype(jnp.int32))
    sq, dx, _, small = _local_step(x[0], loss_target[0], weights, sm, reducer)

    reduced_small = _all_reduce_small(_pack_small(small, tail=sq))
    last_started = reducer.advance("ffn1", [dx, reduced_small])
    loss = reduced_small[SMALL_USED, 0] * (0.5 / D)
    small_g = _unpack_small(reduced_small, small_shapes)
    grads, delta, new_m, new_v = {}, {}, {}, {}
    in_transit = {}

    def send(stage, after):
        halves = reducer.finish(stage, after)
        lands = [lax.empty(h.shape, F32) for h in halves.values()]
        sems, mine, lands, token = _exchange_start(list(halves.values()), lands, _sibling_plan, len(lands),
                                                   "halves_start_" + stage)
        in_transit[stage] = (list(halves), sems, mine, lands)
        return token

    def update(stage, after):
        names, sems, mine, lands = in_transit[stage]
        mine, theirs = _exchange_wait(sems, mine, lands, _sibling_plan, after, "halves_wait_" + stage)
        updated = _adamw_halves([shard(n) for n in names], mine, theirs, [shard(n, "m_") for n in names],
                                [shard(n, "v_") for n in names], "adamw_" + stage)
        for n, results in zip(names, updated):
            grads[n], delta[n], new_m[n], new_v[n] = (unshard(n, r) for r in results)
        return new_v[names[-1]]

    token = send("ffn2", [reduced_small, last_started])
    token = send("mix", token)
    done = update("ffn2", token)
    done = update("mix", done)
    token = send("ffn1", done)
    update("ffn1", token)

    small_g["conv_w"] = only_my_columns(small_g["conv_w"])
    names = [n for n, _ in SMALL]
    flat2d = lambda a: a.reshape(-1, a.shape[-1])
    outs = _adamw_small(*[[flat2d(given[pre + n]) if pre != "g" else flat2d(small_g[n]) for n in names]
                          for pre in ("", "g", "m_", "v_")], after=last_started)
    for dst, arrs in zip((delta, new_m, new_v), outs):
        dst.update({n: a.reshape(given[n].shape) for n, a in zip(names, arrs)})
    grads.update(small_g)
    return (loss, dx[None], *[grads[n] for n in WEIGHTS], *[delta[n] for n in WEIGHTS], *[new_m[n] for n in WEIGHTS],
            *[new_v[n] for n in WEIGHTS])
```

```python
import math

import jax
import jax.numpy as jnp
from jax import lax
from jax.experimental import pallas as pl
from jax.experimental.pallas import tpu as pltpu

F32, BF16 = jnp.float32, jnp.bfloat16
D = 1024
NSH = 4
FF_S = 704
IN_S = 896
GATE_S = 512
KV_W = 256
CHUNK = 64
KB = 192
N_HEADS = 16
HEAD_DIM = 64
N_BUCKETS = 32
KP = 192
PAD_KEYS = 128
RMS_EPS = 1e-6
NEG_INF = -1e30
LRU_C = 8.0
TM = 512
TM_SCAN = 256
VMEM_LIMIT = 56 * 1024 * 1024
ADAM_LR, ADAM_B1, ADAM_B2, ADAM_EPS, ADAM_WD, ADAM_STEP = 0.001, 0.9, 0.999, 1e-08, 0.01, 10
SMALL_ROWS = 1216
SMALL_SLICE = SMALL_ROWS // 8
MESH = pl.DeviceIdType.MESH

BIG = ["ffn1_w1", "ffn1_w3", "ffn1_w2", "w_in", "w_lru_out", "w_attn_out", "w_gate", "w_o", "ffn2_w1", "ffn2_w3", "ffn2_w2"]
SMALL = [("ffn1_pre_g", 1024), ("ffn1_post_g", 1024), ("mix_pre_g", 1024), ("conv_w", 4096), ("conv_b", 1024),
         ("rg_a_w", 65536), ("rg_a_b", 1024), ("rg_x_w", 65536), ("rg_x_b", 1024), ("lru_lambda", 1024),
         ("attn_sinks", 1024), ("rel_bias", 1024), ("b_gate", 2048), ("mix_post_g", 1024), ("ffn2_pre_g", 1024),
         ("ffn2_post_g", 1024)]
WEIGHTS = ["ffn1_pre_g", "ffn1_w1", "ffn1_w3", "ffn1_w2", "ffn1_post_g", "mix_pre_g", "w_in", "conv_w", "conv_b", "rg_a_w",
           "rg_a_b", "rg_x_w", "rg_x_b", "lru_lambda", "w_lru_out", "attn_sinks", "rel_bias", "w_attn_out", "w_gate", "b_gate",
           "w_o", "mix_post_g", "ffn2_pre_g", "ffn2_w1", "ffn2_w3", "ffn2_w2", "ffn2_post_g"]


def _params(*sem):
    return pltpu.CompilerParams(dimension_semantics=sem or None, vmem_limit_bytes=VMEM_LIMIT)


def _nn(a, b):
    return jnp.dot(a, b, preferred_element_type=F32)


def _nt(a, b):
    return lax.dot_general(a, b, (((1,), (1,)), ((), ())), preferred_element_type=F32)


def _tn(a, b):
    return lax.dot_general(a, b, (((0,), (0,)), ((), ())), preferred_element_type=F32)


def _rms(x, g):
    rstd = lax.rsqrt(jnp.mean(x * x, axis=-1, keepdims=True) + RMS_EPS)
    return (x * rstd) * g


def _rms_bwd(dout, x, g):
    rstd = lax.rsqrt(jnp.mean(x * x, axis=-1, keepdims=True) + RMS_EPS)
    xhat = x * rstd
    dg = jnp.sum(dout * xhat, axis=0, keepdims=True)
    dxhat = dout * g
    dx = rstd * (dxhat - xhat * jnp.mean(dxhat * xhat, axis=-1, keepdims=True))
    return dx, dg


_GELU_K = math.sqrt(2.0 / math.pi)


_GELU_C = 0.044715 * _GELU_K


def _gelu_and_grad(x):
    x2 = x * x
    t = jnp.tanh(x * (_GELU_K + _GELU_C * x2))
    cdf = 0.5 + 0.5 * t
    return x * cdf, cdf + (x * (_GELU_K + (3.0 * _GELU_C) * x2)) * (0.5 - 0.5 * (t * t))


def _softplus_neg(lam):
    z = -lam
    u = jnp.exp(-jnp.abs(z))
    w = 1.0 + u
    log1p_u = jnp.where(w == 1.0, u, jnp.log(w) * (u / (w - 1.0)))
    return jnp.maximum(z, 0.0) + log1p_u


def _lru_coeffs(r, sp):
    log_a = (-LRU_C * r) * sp
    a = jnp.exp(log_a)
    t = jnp.tanh(log_a)
    s = jnp.sqrt(-2.0 * t / (1.0 - t))
    return a, s


def _row_spec(tm, width):
    return pl.BlockSpec((tm, width), lambda i: (i, 0))


def _vec_spec(width):
    return pl.BlockSpec((1, width), lambda i: (0, 0))


_WHOLE = pl.BlockSpec(memory_space=pltpu.VMEM)


def _tile(t, tm=TM):
    return min(tm, t)


def _ffn_fwd(x, gpre, w1g, w3g, w2g, gpost, name, target=None):
    t = x.shape[0]
    tm = _tile(t)
    last = target is not None

    def body(x_ref, gpre_ref, w1_ref, w3_ref, w2_ref, gpost_ref, *refs):
        t_ref, (h_ref, a_ref, b_ref, hm_ref, f_ref), l_ref = (refs[0] if last else None), refs[last:last + 5], refs[-1]
        xv = x_ref[...]
        nb = _rms(xv, gpre_ref[...]).astype(BF16)
        f = jnp.zeros((tm, D), F32)
        for s in range(NSH):
            a = _nt(nb, w1_ref[s])
            b = _nt(nb, w3_ref[s])
            hmb = ((a * jax.nn.sigmoid(a)) * b).astype(BF16)
            a_ref[s] = a.astype(BF16)
            b_ref[s] = b.astype(BF16)
            hm_ref[s] = hmb
            f = f + _nn(hmb, w2_ref[s])
        f_ref[...] = f
        h = xv + 0.5 * _rms(f, gpost_ref[...])
        if last:
            @pl.when(pl.program_id(0) == 0)
            def _():
                l_ref[...] = jnp.zeros_like(l_ref)

            e = h - t_ref[...]
            h_ref[...] = e * (1.0 / D)
            l_ref[...] += jnp.sum(jnp.sum(e * e, axis=0, keepdims=True), axis=1, keepdims=True)
        else:
            h_ref[...] = h

    sh = pl.BlockSpec((NSH, tm, FF_S), lambda i: (0, i, 0))
    act = jax.ShapeDtypeStruct((NSH, t, FF_S), BF16)
    return pl.pallas_call(
        body, grid=(t // tm,), name=name,
        in_specs=[_row_spec(tm, D), _vec_spec(D), _WHOLE, _WHOLE, _WHOLE, _vec_spec(D)] + [_row_spec(tm, D)] * last,
        out_specs=[_row_spec(tm, D), sh, sh, sh, _row_spec(tm, D)] + [pl.BlockSpec((1, 128), lambda i: (0, 0))] * last,
        out_shape=[jax.ShapeDtypeStruct((t, D), F32), act, act, act, jax.ShapeDtypeStruct((t, D), F32)]
        + [jax.ShapeDtypeStruct((1, 128), F32)] * last,
        compiler_params=_params("arbitrary"),
    )(x, gpre, w1g, w3g, w2g, gpost, *([target] if last else []))


def _ffn_up(x, gpre, w1g, w3g, name):
    t = x.shape[0]
    tm = _tile(t)

    def body(x_ref, gpre_ref, w1_ref, w3_ref, n_ref, a_ref, b_ref, hm_ref):
        nb = _rms(x_ref[...], gpre_ref[...]).astype(BF16)
        n_ref[...] = nb
        for s in range(NSH):
            a = _nt(nb, w1_ref[s])
            b = _nt(nb, w3_ref[s])
            a_ref[s] = a.astype(BF16)
            b_ref[s] = b.astype(BF16)
            hm_ref[s] = ((a * jax.nn.sigmoid(a)) * b).astype(BF16)

    sh = pl.BlockSpec((NSH, tm, FF_S), lambda i: (0, i, 0))
    act = jax.ShapeDtypeStruct((NSH, t, FF_S), BF16)
    return pl.pallas_call(
        body, grid=(t // tm,), name=name, in_specs=[_row_spec(tm, D), _vec_spec(D), _WHOLE, _WHOLE],
        out_specs=[_row_spec(tm, D), sh, sh, sh], out_shape=[jax.ShapeDtypeStruct((t, D), BF16), act, act, act],
        compiler_params=_params("arbitrary"),
    )(x, gpre, w1g, w3g)


def _ffn_down(x, hm, w2g, gpost, name):
    t = x.shape[0]
    tm = _tile(t)

    def body(x_ref, hm_ref, w2_ref, gpost_ref, h_ref, f_ref):
        f = jnp.zeros((tm, D), F32)
        for s in range(NSH):
            f = f + _nn(hm_ref[s], w2_ref[s])
        f_ref[...] = f
        h_ref[...] = x_ref[...] + 0.5 * _rms(f, gpost_ref[...])

    sh = pl.BlockSpec((NSH, tm, FF_S), lambda i: (0, i, 0))
    f32 = jax.ShapeDtypeStruct((t, D), F32)
    return pl.pallas_call(
        body, grid=(t // tm,), name=name, in_specs=[_row_spec(tm, D), sh, _WHOLE, _vec_spec(D)],
        out_specs=[_row_spec(tm, D), _row_spec(tm, D)], out_shape=[f32, f32], compiler_params=_params("arbitrary"),
    )(x, hm, w2g, gpost)


def _mix_proj(h1, gmix, w_in_g, w_gate_g, b_gate):
    t = h1.shape[0]
    tm = _tile(t)

    def body(h_ref, g_ref, win_ref, wg_ref, bg_ref, u_ref, q_ref, k_ref, v_ref, xr_ref, xg_ref, gate_ref):
        ub = _rms(h_ref[...], g_ref[...]).astype(BF16)
        u_ref[...] = ub
        p0 = _nn(ub, win_ref[0])
        q_ref[:, 0:896] = p0.astype(BF16)
        p1 = _nn(ub, win_ref[1])
        q_ref[:, 896:1024] = p1[:, 0:128].astype(BF16)
        k_ref[...] = p1[:, 128:384].astype(BF16)
        v_ref[...] = p1[:, 384:640].astype(BF16)
        xr_ref[:, 0:256] = p1[:, 640:896]
        p2 = _nn(ub, win_ref[2])
        xr_ref[:, 256:1024] = p2[:, 0:768]
        xg_ref[:, 0:128] = p2[:, 768:896]
        xg_ref[:, 128:1024] = _nn(ub, win_ref[3])
        for s in range(NSH):
            sl = slice(s * GATE_S, (s + 1) * GATE_S)
            gate_ref[:, sl] = jax.nn.sigmoid(_nn(ub, wg_ref[s]) + bg_ref[:, sl]).astype(BF16)

    return pl.pallas_call(
        body, grid=(t // tm,), name="mix_proj",
        in_specs=[_row_spec(tm, D), _vec_spec(D), _WHOLE, _WHOLE, _vec_spec(2 * D)],
        out_specs=[_row_spec(tm, D), _row_spec(tm, D), _row_spec(tm, KV_W), _row_spec(tm, KV_W), _row_spec(tm, D),
                   _row_spec(tm, D), _row_spec(tm, 2 * D)],
        out_shape=[jax.ShapeDtypeStruct((t, D), BF16), jax.ShapeDtypeStruct((t, D), BF16),
                   jax.ShapeDtypeStruct((t, KV_W), BF16), jax.ShapeDtypeStruct((t, KV_W), BF16),
                   jax.ShapeDtypeStruct((t, D), F32), jax.ShapeDtypeStruct((t, D), F32),
                   jax.ShapeDtypeStruct((t, 2 * D), BF16)],
        compiler_params=_params("arbitrary"),
    )(h1, gmix, w_in_g, w_gate_g, b_gate)


def _rglru_fwd(xr, xg, conv_w, conv_b, wa2, ba, wx2, bx, lam, after=None):
    t = xr.shape[0]
    tm = _tile(t, TM_SCAN)
    nb8 = tm // 8

    def body(xr_ref, xrp_ref, xg_ref, cw_ref, cb_ref, wa_ref, ba_ref, wx_ref, bx_ref, lam_ref,
             hr_ref, yain_ref, xc_ref, r_ref, ig_ref, a_sc, s_ref, ext, h_sc):
        i = pl.program_id(0)

        @pl.when(i == 0)
        def _():
            h_sc[...] = jnp.zeros_like(h_sc)

        ext[0:8, :] = jnp.where(i == 0, 0.0, xrp_ref[...])
        ext[8:8 + tm, :] = xr_ref[...]
        xc = jnp.broadcast_to(cb_ref[...], (tm, D))
        for tap in range(4):
            xc = xc + ext[pl.ds(5 + tap, tm), :] * cw_ref[tap:tap + 1, :]
        xc_ref[...] = xc
        xcb = xc.astype(BF16)
        for p in range(8):
            sl = slice(p * 128, (p + 1) * 128)
            r_ref[:, sl] = jax.nn.sigmoid(_nn(xcb[:, sl], wa_ref[p]) + ba_ref[:, sl])
            ig_ref[:, sl] = jax.nn.sigmoid(_nn(xcb[:, sl], wx_ref[p]) + bx_ref[:, sl])
        a, s = _lru_coeffs(r_ref[...], _softplus_neg(lam_ref[...]))
        a_sc[...] = a
        s_ref[...] = s
        hr_ref[...] = s * (ig_ref[...] * xc)

        def blk(j, h):
            st = pl.multiple_of(j * 8, 8)
            a8 = a_sc[pl.ds(st, 8), :]
            u8 = hr_ref[pl.ds(st, 8), :]
            rows = []
            for k in range(8):
                h = a8[k:k + 1, :] * h + u8[k:k + 1, :]
                rows.append(h)
            hr_ref[pl.ds(st, 8), :] = jnp.concatenate(rows, axis=0)
            return h

        h_sc[0:1, :] = lax.fori_loop(0, nb8, blk, h_sc[0:1, :])
        yain_ref[...] = (hr_ref[...] * _gelu_and_grad(xg_ref[...])[0]).astype(BF16)

    prev = pl.BlockSpec((8, D), lambda i: (jnp.maximum(i * nb8 - 1, 0), 0))
    full = lambda shape: pl.BlockSpec(shape, lambda i: tuple(0 for _ in shape))
    f32 = jax.ShapeDtypeStruct((t, D), F32)
    body, specs, operands = _behind(body, after)
    return pl.pallas_call(
        body, grid=(t // tm,), name="rglru_fwd",
        in_specs=specs + [_row_spec(tm, D), prev, _row_spec(tm, D), full((4, D)), _vec_spec(D), full((8, 128, 128)),
                          _vec_spec(D), full((8, 128, 128)), _vec_spec(D), _vec_spec(D)],
        out_specs=[_row_spec(tm, D)] * 7,
        out_shape=[f32, jax.ShapeDtypeStruct((t, D), BF16), f32, f32, f32, f32, f32],
        scratch_shapes=[pltpu.VMEM((tm + 8, D), F32), pltpu.VMEM((8, D), F32)],
        compiler_params=_params("arbitrary"),
    )(*operands, xr, xr, xg, conv_w, conv_b, wa2, ba, wx2, bx, lam)


def _bias_fwd(table_t, onehot_t):
    def body(t_ref, e_ref, o_ref):
        o_ref[...] = jnp.dot(t_ref[...], e_ref[...], preferred_element_type=F32, precision=lax.Precision.HIGHEST)

    return pl.pallas_call(body, out_shape=jax.ShapeDtypeStruct((N_HEADS, CHUNK * KB), F32), name="bias_fwd",
                          compiler_params=_params())(table_t, onehot_t)


def _bias_bwd(dbias_flat, onehot_t, ds_rows):
    def body(d_ref, e_ref, s_ref, o_ref, so_ref):
        o_ref[...] = lax.dot_general(d_ref[...], e_ref[...], (((1,), (1,)), ((), ())), preferred_element_type=F32,
                                     precision=lax.Precision.HIGHEST)
        so_ref[...] = jnp.zeros_like(so_ref)
        for r in range(4):
            so_ref[:, r:r + 1] = jnp.sum(s_ref[:, r * CHUNK:(r + 1) * CHUNK], axis=1, keepdims=True)

    return pl.pallas_call(body, out_shape=[jax.ShapeDtypeStruct((N_HEADS, N_BUCKETS), F32), jax.ShapeDtypeStruct((8, 128), F32)],
                          name="bias_bwd", compiler_params=_params())(dbias_flat, onehot_t, ds_rows)


def _stack_heads(q):
    return jnp.concatenate(
        [jnp.concatenate([q[:, (4 * g + r) * HEAD_DIM:(4 * g + r + 1) * HEAD_DIM] for g in range(4)], axis=1)
         for r in range(4)], axis=0)


def _unstack_heads(o):
    return jnp.concatenate([o[r * CHUNK:(r + 1) * CHUNK, g * HEAD_DIM:(g + 1) * HEAD_DIM] for g in range(4) for r in range(4)],
                           axis=1)


def _block_diag(w, mask):
    return jnp.concatenate([w] * 4, axis=0) * mask


def _group_softmax(qk, bias_g, sink, valid):
    s = qk * (HEAD_DIM ** -0.5) + bias_g
    s = jnp.where(valid, s, NEG_INF)
    m = jnp.maximum(jnp.max(s, axis=0, keepdims=True), sink)
    e = jnp.exp(s - m)
    es = jnp.exp(sink - m)
    inv = 1.0 / (jnp.sum(e, axis=0, keepdims=True) + es)
    return e * inv, es * inv


def _attn_fwd(sink_rows, q, kp, vp, bias_t, mask, after=None):
    t = q.shape[0]
    per_step = 8

    def body(sink_ref, q_ref, kp_ref, vp_ref, bias_ref, mask_ref, o_ref):
        owns = [mask_ref[g * KP:(g + 1) * KP, :] for g in range(4)]
        for k in range(per_step):
            c = pl.program_id(0) * per_step + k
            rows = slice(k * CHUNK, (k + 1) * CHUNK)
            st = pl.multiple_of(c * CHUNK, CHUNK)
            kw = kp_ref[pl.ds(st, KP), :]
            vw = vp_ref[pl.ds(st, KP), :]
            q_all = _stack_heads(q_ref[rows, :])
            valid = lax.broadcasted_iota(jnp.int32, (KP, 1), 0) + c * CHUNK >= PAD_KEYS
            scores = [_nt(kw * owns[g], q_all) for g in range(4)]
            ps = [_group_softmax(scores[g], bias_ref[g * KP:(g + 1) * KP, :], sink_ref[g:g + 1, :], valid)[0]
                  for g in range(4)]
            o_all = sum(_tn(ps[g].astype(BF16), vw * owns[g]) for g in range(4))
            o_ref[rows, :] = _unstack_heads(o_all).astype(BF16)

    body, specs, operands = _behind(body, after)
    return pl.pallas_call(
        body, grid=(t // (per_step * CHUNK),), name="attn_fwd",
        in_specs=specs + [_WHOLE, _row_spec(per_step * CHUNK, D), _WHOLE, _WHOLE, _WHOLE, _WHOLE],
        out_specs=_row_spec(per_step * CHUNK, D),
        out_shape=jax.ShapeDtypeStruct((t, D), BF16),
        compiler_params=_params("arbitrary"),
    )(*operands, sink_rows, q, kp, vp, bias_t, mask)


def _merge_fwd(yain, o, gate, h1, w_lru, w_att, w_o, gpost):
    t = h1.shape[0]
    tm = _tile(t)

    def body(ya_ref, o_ref, g_ref, h_ref, wl_ref, wa_ref, wo_ref, gp_ref, h2_ref, mo_ref, mg_ref, ya_out, yb_out):
        ya = _nn(ya_ref[...], wl_ref[...])
        yb = _nn(o_ref[...], wa_ref[...])
        g0 = g_ref[:, 0:D].astype(F32)
        g1 = g_ref[:, D:2 * D].astype(F32)
        mg = (g0 * ya + g1 * yb).astype(BF16)
        mo = _nn(mg, wo_ref[...])
        ya_out[...] = (ya * (g0 * (1.0 - g0))).astype(BF16)
        yb_out[...] = (yb * (g1 * (1.0 - g1))).astype(BF16)
        mg_ref[...] = mg
        mo_ref[...] = mo
        h2_ref[...] = h_ref[...] + _rms(mo, gp_ref[...])

    f32 = jax.ShapeDtypeStruct((t, D), F32)
    b16 = jax.ShapeDtypeStruct((t, D), BF16)
    return pl.pallas_call(
        body, grid=(t // tm,), name="merge_fwd",
        in_specs=[_row_spec(tm, D), _row_spec(tm, D), _row_spec(tm, 2 * D), _row_spec(tm, D), _WHOLE, _WHOLE, _WHOLE,
                  _vec_spec(D)],
        out_specs=[_row_spec(tm, D)] * 5,
        out_shape=[f32, f32, b16, b16, b16],
        compiler_params=_params("arbitrary"),
    )(yain, o, gate, h1, w_lru, w_att, w_o, gpost)


def _ffn_bwd(dh, x, f, a, b, gpre, gpost, w1g, w3g, w2g, name):
    t = x.shape[0]
    tm = _tile(t, TM_SCAN)

    def body(dh_ref, x_ref, f_ref, a_ref, b_ref, gpre_ref, gpost_ref, w1_ref, w3_ref, w2_ref,
             dx_ref, n_ref, da_ref, db_ref, df_ref, dgpre_ref, dgpost_ref):
        @pl.when(pl.program_id(0) == 0)
        def _():
            dgpre_ref[...] = jnp.zeros_like(dgpre_ref)
            dgpost_ref[...] = jnp.zeros_like(dgpost_ref)

        dhv = dh_ref[...]
        xv = x_ref[...]
        df, dgp = _rms_bwd(0.5 * dhv, f_ref[...], gpost_ref[...])
        dgpost_ref[...] += dgp
        dfb = df.astype(BF16)
        df_ref[...] = dfb
        n_ref[...] = _rms(xv, gpre_ref[...]).astype(BF16)
        dn = jnp.zeros((tm, D), F32)
        for s in range(NSH):
            av = a_ref[s].astype(F32)
            bv = b_ref[s].astype(F32)
            sg = jax.nn.sigmoid(av)
            dhm = _nt(dfb, w2_ref[s])
            dab = (dhm * bv * (sg * (1.0 + av * (1.0 - sg)))).astype(BF16)
            dbb = (dhm * (av * sg)).astype(BF16)
            da_ref[s] = dab
            db_ref[s] = dbb
            dn = dn + _nn(dab, w1_ref[s]) + _nn(dbb, w3_ref[s])
        dxn, dg = _rms_bwd(dn, xv, gpre_ref[...])
        dgpre_ref[...] += dg
        dx_ref[...] = dhv + dxn

    sh = pl.BlockSpec((NSH, tm, FF_S), lambda i: (0, i, 0))
    act = jax.ShapeDtypeStruct((NSH, t, FF_S), BF16)
    vec = jax.ShapeDtypeStruct((1, D), F32)
    return pl.pallas_call(
        body, grid=(t // tm,), name=name,
        in_specs=[_row_spec(tm, D), _row_spec(tm, D), _row_spec(tm, D), sh, sh, _vec_spec(D), _vec_spec(D), _WHOLE, _WHOLE,
                  _WHOLE],
        out_specs=[_row_spec(tm, D), _row_spec(tm, D), sh, sh, _row_spec(tm, D), _vec_spec(D), _vec_spec(D)],
        out_shape=[jax.ShapeDtypeStruct((t, D), F32), jax.ShapeDtypeStruct((t, D), BF16), act, act,
                   jax.ShapeDtypeStruct((t, D), BF16), vec, vec],
        compiler_params=_params("arbitrary"),
    )(dh, x, f, a, b, gpre, gpost, w1g, w3g, w2g)


def _behind(body, after):
    if after is None:
        return body, [], []

    def ordered(_, *refs):
        body(*refs)

    return ordered, [_ANY], [after]


def _ffn_bwd_acts(dh, f, a, b, gpost, w2g, name):
    t = dh.shape[0]
    tm = _tile(t)

    def body(dh_ref, f_ref, a_ref, b_ref, gpost_ref, w2_ref, da_ref, db_ref, df_ref, dgpost_ref):
        @pl.when(pl.program_id(0) == 0)
        def _():
            dgpost_ref[...] = jnp.zeros_like(dgpost_ref)

        df, dgp = _rms_bwd(0.5 * dh_ref[...], f_ref[...], gpost_ref[...])
        dgpost_ref[...] += dgp
        dfb = df.astype(BF16)
        df_ref[...] = dfb
        for s in range(NSH):
            av = a_ref[s].astype(F32)
            bv = b_ref[s].astype(F32)
            sg = jax.nn.sigmoid(av)
            dhm = _nt(dfb, w2_ref[s])
            da_ref[s] = (dhm * bv * (sg * (1.0 + av * (1.0 - sg)))).astype(BF16)
            db_ref[s] = (dhm * (av * sg)).astype(BF16)

    sh = pl.BlockSpec((NSH, tm, FF_S), lambda i: (0, i, 0))
    act = jax.ShapeDtypeStruct((NSH, t, FF_S), BF16)
    b16 = jax.ShapeDtypeStruct((t, D), BF16)
    return pl.pallas_call(
        body, grid=(t // tm,), name=name,
        in_specs=[_row_spec(tm, D), _row_spec(tm, D), sh, sh, _vec_spec(D), _WHOLE],
        out_specs=[sh, sh, _row_spec(tm, D), _vec_spec(D)],
        out_shape=[act, act, b16, jax.ShapeDtypeStruct((1, D), F32)],
        compiler_params=_params("arbitrary"),
    )(dh, f, a, b, gpost, w2g)


def _ffn_bwd_input(dh, x, da, db, gpre, w1g, w3g, name, after):
    t = x.shape[0]
    tm = _tile(t)

    def body(dh_ref, x_ref, da_ref, db_ref, gpre_ref, w1_ref, w3_ref, dx_ref, dgpre_ref):
        @pl.when(pl.program_id(0) == 0)
        def _():
            dgpre_ref[...] = jnp.zeros_like(dgpre_ref)

        dn = jnp.zeros((tm, D), F32)
        for s in range(NSH):
            dn = dn + _nn(da_ref[s], w1_ref[s]) + _nn(db_ref[s], w3_ref[s])
        dxn, dg = _rms_bwd(dn, x_ref[...], gpre_ref[...])
        dgpre_ref[...] += dg
        dx_ref[...] = dh_ref[...] + dxn

    sh = pl.BlockSpec((NSH, tm, FF_S), lambda i: (0, i, 0))
    body, specs, operands = _behind(body, after)
    return pl.pallas_call(
        body, grid=(t // tm,), name=name,
        in_specs=specs + [_row_spec(tm, D), _row_spec(tm, D), sh, sh, _vec_spec(D), _WHOLE, _WHOLE],
        out_specs=[_row_spec(tm, D), _vec_spec(D)],
        out_shape=[jax.ShapeDtypeStruct((t, D), F32), jax.ShapeDtypeStruct((1, D), F32)],
        compiler_params=_params("arbitrary"),
    )(*operands, dh, x, da, db, gpre, w1g, w3g)


def _wgrad(a, b, a_spec, b_spec, out_spec, out_shape, grid, name, after=None):
    def body(a_ref, b_ref, o_ref):
        o_ref[...] = _tn(a_ref[...], b_ref[...]).astype(BF16)

    body, specs, operands = _behind(body, after)
    return pl.pallas_call(body, grid=grid, name=name, in_specs=specs + [a_spec, b_spec], out_specs=out_spec,
                          out_shape=jax.ShapeDtypeStruct(out_shape, BF16),
                          compiler_params=_params(*("arbitrary",) * len(grid)))(*operands, a, b)


def _wgrad_cols(act, dsh, width, name, after=None):
    t = act.shape[0]
    if dsh.ndim == 3:
        b_spec = pl.BlockSpec((None, t, width), lambda s, k: (s, 0, 0))
    else:
        b_spec = pl.BlockSpec((t, width), lambda s, k: (0, s))
    return _wgrad(act, dsh, pl.BlockSpec((t, 512), lambda s, k: (0, k)), b_spec,
                  pl.BlockSpec((None, 512, width), lambda s, k: (s, k, 0)), (NSH, D, width), (NSH, 2), name, after)


def _wgrad_rows(hm, df, name, after=None):
    t = df.shape[0]
    return _wgrad(hm, df, pl.BlockSpec((None, t, FF_S), lambda s: (s, 0, 0)), pl.BlockSpec((t, D), lambda s: (0, 0)),
                  pl.BlockSpec((None, FF_S, D), lambda s: (s, 0, 0)), (NSH, FF_S, D), (NSH,), name, after)


def _wgrad_sq(a, b, name, after=None):
    t = a.shape[0]
    return _wgrad(a, b, pl.BlockSpec((t, 512), lambda i, j: (0, i)), pl.BlockSpec((t, 512), lambda i, j: (0, j)),
                  pl.BlockSpec((512, 512), lambda i, j: (i, j)), (D, D), (2, 2), name, after)


def _mix_bwd1(dh2, mo, gpost, gate, ya, yb, xg, hr, w_o, w_lru, w_att, after):
    t = dh2.shape[0]
    tm = _tile(t, TM_SCAN)

    def body(dh_ref, mo_ref, gp_ref, g_ref, ya_ref, yb_ref, xg_ref, hr_ref, wo_ref, wl_ref, wa_ref,
             dmo_ref, dya_ref, dyb_ref, dgate_ref, dhr_ref, dxg_ref, do_ref, dgp_ref, dbg_ref):
        @pl.when(pl.program_id(0) == 0)
        def _():
            dgp_ref[...] = jnp.zeros_like(dgp_ref)
            dbg_ref[...] = jnp.zeros_like(dbg_ref)

        dmo, dgp = _rms_bwd(dh_ref[...], mo_ref[...], gp_ref[...])
        dgp_ref[...] += dgp
        dmob = dmo.astype(BF16)
        dmo_ref[...] = dmob
        dm = _nt(dmob, wo_ref[...])
        g0 = g_ref[:, 0:D].astype(F32)
        g1 = g_ref[:, D:2 * D].astype(F32)
        dyab = (dm * g0).astype(BF16)
        dybb = (dm * g1).astype(BF16)
        dya_ref[...] = dyab
        dyb_ref[...] = dybb
        dg0 = dm * ya_ref[...].astype(F32)
        dg1 = dm * yb_ref[...].astype(F32)
        dgate_ref[:, 0:D] = dg0.astype(BF16)
        dgate_ref[:, D:2 * D] = dg1.astype(BF16)
        dbg_ref[:, 0:D] += jnp.sum(dg0, axis=0, keepdims=True)
        dbg_ref[:, D:2 * D] += jnp.sum(dg1, axis=0, keepdims=True)
        dyain = _nt(dyab, wl_ref[...])
        do_ref[...] = _nt(dybb, wa_ref[...]).astype(BF16)
        gelu, gelu_grad = _gelu_and_grad(xg_ref[...])
        dhr_ref[...] = dyain * gelu
        dxg_ref[...] = (dyain * hr_ref[...] * gelu_grad).astype(BF16)

    b16 = jax.ShapeDtypeStruct((t, D), BF16)
    body, specs, operands = _behind(body, after)
    return pl.pallas_call(
        body, grid=(t // tm,), name="mix_bwd1",
        in_specs=specs + [_row_spec(tm, D), _row_spec(tm, D), _vec_spec(D), _row_spec(tm, 2 * D), _row_spec(tm, D),
                          _row_spec(tm, D), _row_spec(tm, D), _row_spec(tm, D), _WHOLE, _WHOLE, _WHOLE],
        out_specs=[_row_spec(tm, D), _row_spec(tm, D), _row_spec(tm, D), _row_spec(tm, 2 * D), _row_spec(tm, D),
                   _row_spec(tm, D), _row_spec(tm, D), _vec_spec(D), _vec_spec(2 * D)],
        out_shape=[b16, b16, b16, jax.ShapeDtypeStruct((t, 2 * D), BF16), jax.ShapeDtypeStruct((t, D), F32), b16, b16,
                   jax.ShapeDtypeStruct((1, D), F32), jax.ShapeDtypeStruct((1, 2 * D), F32)],
        compiler_params=_params("arbitrary"),
    )(*operands, dh2, mo, gpost, gate, ya, yb, xg, hr, w_o, w_lru, w_att)


def _rglru_bwd(dhr, hr, xc, r, ig, a, s, xr, conv_w, wa2, wx2, lam, after):
    t = dhr.shape[0]
    tm = _tile(t, TM_SCAN)
    nb8 = tm // 8
    nt = t // tm

    def body(dhr_ref, hr_ref, hrp_ref, xc_ref, r_ref, ig_ref, a_sc, s_ref, xr_ref, cw_ref, wa_ref, wx_ref, lam_ref,
             dxr_ref, dwa_ref, dwx_ref, dba_ref, dbx_ref, dlam_ref, dcw_ref, dcb_ref,
             ext_h, ext_d, g_sc, c_sc, nxt_sc):
        i = pl.program_id(0)
        first_tile = i == nt - 1

        @pl.when(i == 0)
        def _():
            c_sc[...] = jnp.zeros_like(c_sc)
            nxt_sc[...] = jnp.zeros_like(nxt_sc)
            for ref in (dwa_ref, dwx_ref, dba_ref, dbx_ref, dlam_ref, dcw_ref, dcb_ref):
                ref[...] = jnp.zeros_like(ref)

        lamv = lam_ref[...]
        sp = _softplus_neg(lamv)
        rv = r_ref[...]
        igv = ig_ref[...]
        xcv = xc_ref[...]
        a = a_sc[...]
        s = s_ref[...]

        def blk(jj, c):
            st = pl.multiple_of((nb8 - 1 - jj) * 8, 8)
            d8 = dhr_ref[pl.ds(st, 8), :]
            a8 = a_sc[pl.ds(st, 8), :]
            rows = [None] * 8
            for k in range(7, -1, -1):
                g = d8[k:k + 1, :] + c
                c = a8[k:k + 1, :] * g
                rows[k] = g
            g_sc[pl.ds(st, 8), :] = jnp.concatenate(rows, axis=0)
            return c

        c_sc[0:1, :] = lax.fori_loop(0, nb8, blk, c_sc[0:1, :])
        g = g_sc[...]
        ext_h[0:8, :] = jnp.where(first_tile, 0.0, hrp_ref[...])
        ext_h[8:8 + tm, :] = hr_ref[...]
        hprev = ext_h[pl.ds(7, tm), :]
        d_s = g * (igv * xcv)
        dig = g * s * xcv
        dxc = g * s * igv
        dla = (g * hprev) * a - d_s * ((a * a) / s)
        dr_pre = (dla * (-LRU_C * sp)) * (rv * (1.0 - rv))
        di_pre = dig * (igv * (1.0 - igv))
        dlam_ref[...] += jnp.sum(dla * (LRU_C * rv), axis=0, keepdims=True) * jax.nn.sigmoid(-lamv)
        dba_ref[...] += jnp.sum(dr_pre, axis=0, keepdims=True)
        dbx_ref[...] += jnp.sum(di_pre, axis=0, keepdims=True)
        drb = dr_pre.astype(BF16)
        dib = di_pre.astype(BF16)
        xcb = xcv.astype(BF16)
        ext_d[tm:tm + 8, :] = nxt_sc[...]
        for p in range(8):
            sl = slice(p * 128, (p + 1) * 128)
            ext_d[0:tm, sl] = dxc[:, sl] + _nt(drb[:, sl], wa_ref[p]) + _nt(dib[:, sl], wx_ref[p])
            dwa_ref[p] += _tn(xcb[:, sl], drb[:, sl])
            dwx_ref[p] += _tn(xcb[:, sl], dib[:, sl])
        dxcv = ext_d[0:tm, :]
        nxt_sc[...] = ext_d[0:8, :]
        dcb_ref[...] += jnp.sum(dxcv, axis=0, keepdims=True)
        xrv = xr_ref[...]
        dxr = jnp.zeros((tm, D), F32)
        for tap in range(4):
            ext_h[0:tm, :] = ext_d[pl.ds(3 - tap, tm), :]
            ahead = ext_h[0:tm, :]
            dxr = dxr + ahead * cw_ref[tap:tap + 1, :]
            dcw_ref[tap:tap + 1, :] += jnp.sum(ahead * xrv, axis=0, keepdims=True)
        dxr_ref[...] = dxr.astype(BF16)

    rev = pl.BlockSpec((tm, D), lambda i: (nt - 1 - i, 0))
    prev = pl.BlockSpec((8, D), lambda i: (jnp.maximum((nt - 1 - i) * nb8 - 1, 0), 0))
    full = lambda shape: pl.BlockSpec(shape, lambda i: tuple(0 for _ in shape))
    vec = jax.ShapeDtypeStruct((1, D), F32)
    blocks = jax.ShapeDtypeStruct((8, 128, 128), F32)
    body, specs, operands = _behind(body, after)
    return pl.pallas_call(
        body, grid=(nt,), name="rglru_bwd",
        in_specs=specs + [rev, rev, prev, rev, rev, rev, rev, rev, rev, full((4, D)), full((8, 128, 128)),
                          full((8, 128, 128)), _vec_spec(D)],
        out_specs=[rev, full((8, 128, 128)), full((8, 128, 128)), _vec_spec(D), _vec_spec(D), _vec_spec(D), full((4, D)),
                   _vec_spec(D)],
        out_shape=[jax.ShapeDtypeStruct((t, D), BF16), blocks, blocks, vec, vec, vec, jax.ShapeDtypeStruct((4, D), F32), vec],
        scratch_shapes=[pltpu.VMEM((tm + 8, D), F32), pltpu.VMEM((tm + 8, D), F32),
                        pltpu.VMEM((tm, D), F32), pltpu.VMEM((8, D), F32), pltpu.VMEM((8, D), F32)],
        compiler_params=_params("arbitrary"),
    )(*operands, dhr, hr, hr, xc, r, ig, a, s, xr, conv_w, wa2, wx2, lam)


def _attn_bwd(sink_rows, q, kp, vp, bias_t, mask, do):
    t = q.shape[0]
    tp = kp.shape[0]
    per_step = 16

    def body(sink_ref, q_ref, kp_ref, vp_ref, bias_ref, mask_ref, do_ref, dq_ref, dk_ref, dv_ref, dbias_ref, ds_ref):
        @pl.when(pl.program_id(0) == 0)
        def _():
            for ref in (dk_ref, dv_ref, dbias_ref, ds_ref):
                ref[...] = jnp.zeros_like(ref)

        maskv = mask_ref[...]
        lane_group = lax.broadcasted_iota(jnp.int32, (1, 4 * HEAD_DIM), 1) // HEAD_DIM

        def own_blocks(full):
            out = full[0:KP]
            for g in range(1, 4):
                out = jnp.where(lane_group == g, full[g * KP:(g + 1) * KP], out)
            return out

        dsc_sum, dsinks, dks, dvs = 0.0, [0.0] * 4, [], []
        for k in range(per_step):
            c = pl.program_id(0) * per_step + k
            chunk = slice(k * CHUNK, (k + 1) * CHUNK)
            st = pl.multiple_of(c * CHUNK, CHUNK)
            kbd = _block_diag(kp_ref[pl.ds(st, KP), :], maskv)
            vbd = _block_diag(vp_ref[pl.ds(st, KP), :], maskv)
            q_all = _stack_heads(q_ref[chunk, :])
            do_all = _stack_heads(do_ref[chunk, :])
            valid = lax.broadcasted_iota(jnp.int32, (KP, 1), 0) + c * CHUNK >= PAD_KEYS
            qk = _nt(kbd, q_all)
            dp = _nt(vbd, do_all)
            ps, dscs = [], []
            for g in range(4):
                rows = slice(g * KP, (g + 1) * KP)
                p, sink_p = _group_softmax(qk[rows], bias_ref[rows, :], sink_ref[g:g + 1, :], valid)
                delta = jnp.sum(p * dp[rows], axis=0, keepdims=True)
                ps.append(p)
                dscs.append(p * (dp[rows] - delta))
                dsinks[g] = dsinks[g] - sink_p * delta
            dsc = jnp.concatenate(dscs, axis=0)
            dsc_sum = dsc_sum + dsc
            dsb = (dsc * (HEAD_DIM ** -0.5)).astype(BF16)
            dq_ref[chunk, :] = _unstack_heads(_tn(dsb, kbd)).astype(BF16)
            dks.append((st, own_blocks(_nn(dsb, q_all))))
            dvs.append((st, own_blocks(_nn(jnp.concatenate(ps, axis=0).astype(BF16), do_all))))
        dbias_ref[...] += dsc_sum
        for g in range(4):
            ds_ref[g:g + 1, :] += dsinks[g]
        for (st, dkw), (_, dvw) in zip(dks, dvs):
            dk_ref[pl.ds(st, KP), :] += dkw
            dv_ref[pl.ds(st, KP), :] += dvw

    full = lambda shape: pl.BlockSpec(shape, lambda i: tuple(0 for _ in shape))
    return pl.pallas_call(
        body, grid=(t // (per_step * CHUNK),), name="attn_bwd",
        in_specs=[_WHOLE, _row_spec(per_step * CHUNK, D), _WHOLE, _WHOLE, _WHOLE, _WHOLE, _row_spec(per_step * CHUNK, D)],
        out_specs=[_row_spec(per_step * CHUNK, D), full((tp, KV_W)), full((tp, KV_W)), full((4 * KP, 4 * CHUNK)),
                   full((8, 4 * CHUNK))],
        out_shape=[jax.ShapeDtypeStruct((t, D), BF16), jax.ShapeDtypeStruct((tp, KV_W), F32),
                   jax.ShapeDtypeStruct((tp, KV_W), F32), jax.ShapeDtypeStruct((4 * KP, 4 * CHUNK), F32),
                   jax.ShapeDtypeStruct((8, 4 * CHUNK), F32)],
        compiler_params=_params("arbitrary"),
    )(sink_rows, q, kp, vp, bias_t, mask, do)


def _mix_bwd2(dproj, dgate, h1, dh2, gmix, w_in_g, w_gate_g, after):
    t = h1.shape[0]
    tm = _tile(t)

    def body(dp_ref, dg_ref, h_ref, dh_ref, g_ref, win_ref, wg_ref, dh1_ref, dgm_ref):
        @pl.when(pl.program_id(0) == 0)
        def _():
            dgm_ref[...] = jnp.zeros_like(dgm_ref)

        du = jnp.zeros((tm, D), F32)
        for s in range(NSH):
            du = du + _nt(dp_ref[:, s * IN_S:(s + 1) * IN_S], win_ref[s])
            du = du + _nt(dg_ref[:, s * GATE_S:(s + 1) * GATE_S], wg_ref[s])
        dxn, dg = _rms_bwd(du, h_ref[...], g_ref[...])
        dgm_ref[...] += dg
        dh1_ref[...] = dh_ref[...] + dxn

    body, specs, operands = _behind(body, after)
    return pl.pallas_call(
        body, grid=(t // tm,), name="mix_bwd2",
        in_specs=specs + [_row_spec(tm, NSH * IN_S), _row_spec(tm, 2 * D), _row_spec(tm, D), _row_spec(tm, D), _vec_spec(D),
                          _WHOLE, _WHOLE],
        out_specs=[_row_spec(tm, D), _vec_spec(D)],
        out_shape=[jax.ShapeDtypeStruct((t, D), F32), jax.ShapeDtypeStruct((1, D), F32)],
        compiler_params=_params("arbitrary"),
    )(*operands, dproj, dgate, h1, dh2, gmix, w_in_g, w_gate_g)


def _band_onehot():
    nb = N_BUCKETS // 2
    max_exact = nb // 2
    rel = jnp.arange(KB)[None, :] - PAD_KEYS - jnp.arange(CHUNK)[:, None]
    ret = jnp.where(rel > 0, nb, 0)
    n = jnp.abs(rel)
    nf = jnp.maximum(n, 1).astype(jnp.float32)
    large = max_exact + (jnp.log(nf / max_exact) / math.log(128 / max_exact) * (nb - max_exact)).astype(jnp.int32)
    large = jnp.minimum(large, nb - 1)
    buckets = (ret + jnp.where(n < max_exact, n, large)).reshape(1, CHUNK * KB)
    return (buckets == jnp.arange(N_BUCKETS)[:, None]).astype(F32)


def _pair_blocks(w):
    pairs = w.reshape(8, 2, 64, 64)
    z = jnp.zeros((8, 64, 64), w.dtype)
    return jnp.concatenate([jnp.concatenate([pairs[:, 0], z], axis=2), jnp.concatenate([z, pairs[:, 1]], axis=2)], axis=1)


def _unpair_blocks(w2):
    return jnp.stack([w2[:, 0:64, 0:64], w2[:, 64:128, 64:128]], axis=1).reshape(16, 64, 64)


def _local_step(x, target, weights, sm, reducer):
    row = lambda v: v.reshape(1, -1)
    onehot_t = _band_onehot()
    bias = _bias_fwd(sm["rel_bias"].T, onehot_t).reshape(4, 4, CHUNK, KB)
    bias_t = jnp.pad(jnp.transpose(bias, (0, 3, 1, 2)), ((0, 0), (0, KP - KB), (0, 0), (0, 0))).reshape(4 * KP, 4 * CHUNK)
    sink_rows = jnp.pad(jnp.repeat(sm["attn_sinks"].reshape(4, 4), CHUNK, axis=1), ((0, 4), (0, 0)))
    grp = jnp.arange(4 * KP)[:, None] // KP == jnp.arange(4 * HEAD_DIM)[None, :] // HEAD_DIM
    mask = (grp & (jnp.arange(4 * KP)[:, None] % KP < KB)).astype(BF16)
    wa2 = _pair_blocks(sm["rg_a_w"]).astype(BF16)
    wx2 = _pair_blocks(sm["rg_x_w"]).astype(BF16)
    wg = dict(weights("ffn1_up", [bias_t, sink_rows, mask, wa2, wx2]))
    sm = dict(sm, conv_w=wg["conv_w"])

    n1, a1, b1, hm1 = _ffn_up(x, row(sm["ffn1_pre_g"]), wg["ffn1_w1"], wg["ffn1_w3"], "ffn1_up")
    wg.update(weights("ffn1_down", hm1))
    h1, f1 = _ffn_down(x, hm1, wg["ffn1_w2"], row(sm["ffn1_post_g"]), "ffn1_down")
    wg.update(weights("mix_in", h1))
    u, q, k, v, xr, xg, gate = _mix_proj(h1, row(sm["mix_pre_g"]), wg["w_in"], wg["w_gate"], row(sm["b_gate"]))
    token = weights("mix_out", u, begin=True)
    hr, yain, xc, r, ig, lru_a, lru_s = _rglru_fwd(xr, xg, sm["conv_w"], row(sm["conv_b"]), wa2, row(sm["rg_a_b"]), wx2,
                                                   row(sm["rg_x_b"]), row(sm["lru_lambda"]), token)
    token = weights("ffn2", hr, begin=True)
    kp = jnp.pad(k, ((PAD_KEYS, KP - KB), (0, 0)))
    vp = jnp.pad(v, ((PAD_KEYS, KP - KB), (0, 0)))
    o = _attn_fwd(sink_rows, q, kp, vp, bias_t, mask, token)
    wg.update(weights("mix_out", o))
    w_lru = wg["w_lru_out"].reshape(D, D)
    w_att = wg["w_attn_out"].reshape(D, D)
    w_o = wg["w_o"].reshape(D, D)
    wg.update(weights("ffn2", o))
    h2, mo, merged, ya, yb = _merge_fwd(yain, o, gate, h1, w_lru, w_att, w_o, row(sm["mix_post_g"]))
    dy, a2, b2, hm2, f2, sq = _ffn_fwd(h2, row(sm["ffn2_pre_g"]), wg["ffn2_w1"], wg["ffn2_w3"], wg["ffn2_w2"],
                                       row(sm["ffn2_post_g"]), "ffn2_fwd", target)

    big, small = {}, {}
    dh2, n2, da2, db2, df2, small["ffn2_pre_g"], small["ffn2_post_g"] = _ffn_bwd(
        dy, h2, f2, a2, b2, row(sm["ffn2_pre_g"]), row(sm["ffn2_post_g"]), wg["ffn2_w1"], wg["ffn2_w3"], wg["ffn2_w2"],
        "ffn2_bwd")
    big["ffn2_w1"] = _wgrad_rows(da2, n2, "dw_ffn2_w1")
    big["ffn2_w3"] = _wgrad_rows(db2, n2, "dw_ffn2_w3")
    big["ffn2_w2"] = _wgrad_rows(hm2, df2, "dw_ffn2_w2")
    token = reducer.begin("ffn2", {n: big[n] for n in ("ffn2_w1", "ffn2_w3", "ffn2_w2")})
    dmo, dya, dyb, dgate, dhr, dxg, do, small["mix_post_g"], small["b_gate"] = _mix_bwd1(
        dh2, mo, row(sm["mix_post_g"]), gate, ya, yb, xg, hr, w_o, w_lru, w_att, token)
    big["w_o"] = _wgrad_sq(merged, dmo, "dw_w_o").reshape(NSH, D // NSH, D)
    big["w_lru_out"] = _wgrad_sq(yain, dya, "dw_w_lru_out").reshape(NSH, D // NSH, D)
    big["w_attn_out"] = _wgrad_sq(o, dyb, "dw_w_attn_out").reshape(NSH, D // NSH, D)
    token = reducer.advance("ffn2", big["w_attn_out"])
    (dxr, dwa2, dwx2, small["rg_a_b"], small["rg_x_b"], small["lru_lambda"], small["conv_w"], small["conv_b"]) = _rglru_bwd(
        dhr, hr, xc, r, ig, lru_a, lru_s, xr, sm["conv_w"], wa2, wx2, row(sm["lru_lambda"]), token)
    small["rg_a_w"] = _unpair_blocks(dwa2)
    small["rg_x_w"] = _unpair_blocks(dwx2)
    dq, dkp, dvp, dbias_t, ds_rows = _attn_bwd(sink_rows, q, kp, vp, bias_t, mask, do)
    dbias = jnp.transpose(dbias_t.reshape(4, KP, 4, CHUNK)[:, :KB], (0, 2, 3, 1)).reshape(N_HEADS, CHUNK * KB)
    drel_t, dsinks = _bias_bwd(dbias, onehot_t, ds_rows)
    small["attn_sinks"] = dsinks[0:4, 0:4].reshape(N_HEADS)
    small["rel_bias"] = drel_t.T
    t = x.shape[0]
    dproj = jnp.concatenate([dq, dkp[PAD_KEYS:PAD_KEYS + t].astype(BF16), dvp[PAD_KEYS:PAD_KEYS + t].astype(BF16), dxr, dxg],
                            axis=1)
    big["w_in"] = _wgrad_cols(u, dproj, IN_S, "dw_w_in")
    big["w_gate"] = _wgrad_cols(u, dgate, GATE_S, "dw_w_gate")
    token = reducer.begin("mix", {n: big[n] for n in ("w_in", "w_gate", "w_lru_out", "w_attn_out", "w_o")})
    dh1, small["mix_pre_g"] = _mix_bwd2(dproj, dgate, h1, dh2, row(sm["mix_pre_g"]), wg["w_in"], wg["w_gate"], token)
    da1, db1, df1, small["ffn1_post_g"] = _ffn_bwd_acts(dh1, f1, a1, b1, row(sm["ffn1_post_g"]), wg["ffn1_w2"],
                                                        "ffn1_bwd_acts")
    token = reducer.advance("mix", df1)
    big["ffn1_w1"] = _wgrad_rows(da1, n1, "dw_ffn1_w1", token)
    big["ffn1_w3"] = _wgrad_rows(db1, n1, "dw_ffn1_w3", token)
    big["ffn1_w2"] = _wgrad_rows(hm1, df1, "dw_ffn1_w2", token)
    token = reducer.begin("ffn1", {n: big[n] for n in ("ffn1_w1", "ffn1_w3", "ffn1_w2")})
    dx, small["ffn1_pre_g"] = _ffn_bwd_input(dh1, x, da1, db1, row(sm["ffn1_pre_g"]), wg["ffn1_w1"], wg["ffn1_w3"],
                                             "ffn1_bwd_input", token)
    return sq, dx, big, small


_ANY = pl.BlockSpec(memory_space=pl.ANY)


def _place():
    return lax.axis_index("x"), lax.axis_index("y"), lax.axis_index("c")


def _other_chips(x, y):
    return [(1 - x, y), (x, 1 - y), (1 - x, 1 - y)]


_HBM = pl.BlockSpec(memory_space=pltpu.HBM)
_SEM = pl.BlockSpec(memory_space=pltpu.SEMAPHORE)
_EFFECT = pltpu.SideEffectType.DATAFLOW_SIDE_EFFECTING


def _cast_into_slot(w, chip, name, after=None):
    r, cc = w.shape
    rows = r // 4

    def body(chip_ref, *refs):
        w_ref, o_ref = refs[-2:]
        o_ref[...] = w_ref[...].astype(BF16)

    extra = [] if after is None else [after]
    return pl.pallas_call(
        body, name=name, out_shape=jax.ShapeDtypeStruct((NSH, r, cc), BF16),
        grid_spec=pltpu.PrefetchScalarGridSpec(
            num_scalar_prefetch=1, grid=(4,), in_specs=[_ANY] * len(extra) + [pl.BlockSpec((rows, cc), lambda i, chip: (i, 0))],
            out_specs=pl.BlockSpec((None, rows, cc), lambda i, chip: (chip[0], i, 0))),
        compiler_params=_params("arbitrary"))(chip, *extra, w)


def _piece(ref, slot, c):
    if ref.dtype == F32:
        return ref.at[slot]
    rh = ref.shape[1] // 2
    return ref.at[slot, pl.ds(pl.multiple_of(c * rh, 16), rh), :]


def _gather_start(stages, name):
    flat = [b for stage in stages for b in stage]
    n, ns = len(flat), len(stages)

    def body(*refs):
        ins, sems, token = refs[:n], refs[n:n + 2 * ns], refs[-1]
        x, y, c = _place()
        me = 2 * x + y
        k = 0
        for s, stage in enumerate(stages):
            for i in range(len(stage)):
                for j, (px, py) in enumerate(_other_chips(x, y)):
                    piece = _piece(ins[k], me, c)
                    pltpu.make_async_remote_copy(src_ref=piece, dst_ref=piece, send_sem=sems[2 * s].at[3 * i + j],
                                                 recv_sem=sems[2 * s + 1].at[3 * i + j], device_id=(px, py, c),
                                                 device_id_type=MESH).start()
                k += 1
        token[...] = jnp.zeros_like(token)

    sem_shapes = [pltpu.SemaphoreType.DMA((3 * len(stage),)) for stage in stages for _ in range(2)]
    outs = pl.pallas_call(
        body, name=name, in_specs=[_HBM] * n,
        out_specs=[_SEM] * (2 * ns) + [_HBM] * n + [pl.BlockSpec(memory_space=pltpu.VMEM)],
        out_shape=sem_shapes + [pltpu.HBM(b.shape, b.dtype) for b in flat] + [jax.ShapeDtypeStruct((8, 128), F32)],
        input_output_aliases={i: 2 * ns + i for i in range(n)},
        compiler_params=pltpu.CompilerParams(has_side_effects=_EFFECT),
    )(*[pltpu.with_memory_space_constraint(b, pltpu.HBM) for b in flat])
    sems, bufs, token = outs[:2 * ns], list(outs[2 * ns:2 * ns + n]), outs[-1]
    per_stage, k = [], 0
    for s, stage in enumerate(stages):
        per_stage.append((sems[2 * s], sems[2 * s + 1], bufs[k:k + len(stage)]))
        k += len(stage)
    return per_stage, token


def _gather_wait(send_sems, recv_sems, bufs, after, name):
    n = len(bufs)

    def body(*refs):
        ins, ssem, rsem = refs[:n], refs[n], refs[n + 1]
        x, y, c = _place()
        me = 2 * x + y
        for i in range(n):
            for j, (px, py) in enumerate(_other_chips(x, y)):
                cp = pltpu.make_async_remote_copy(src_ref=_piece(ins[i], me, c), dst_ref=_piece(ins[i], 2 * px + py, c),
                                                  send_sem=ssem.at[3 * i + j], recv_sem=rsem.at[3 * i + j],
                                                  device_id=(px, py, c), device_id_type=MESH)
                cp.wait_send()
                cp.wait_recv()

    afters = list(after) if isinstance(after, (list, tuple)) else [after]
    return pl.pallas_call(
        body, name=name, in_specs=[_HBM] * n + [_SEM, _SEM] + [_ANY] * len(afters), out_specs=[_HBM] * n,
        out_shape=[pltpu.HBM(b.shape, b.dtype) for b in bufs], input_output_aliases={i: i for i in range(n)},
        compiler_params=pltpu.CompilerParams(has_side_effects=_EFFECT),
    )(*bufs, send_sems, recv_sems, *afters)


def _sibling_fill(bufs, name):
    n = len(bufs)

    def body(*refs):
        ins, outs = refs[:n], refs[n:2 * n]
        send_sems, recv_sems = refs[2 * n:]
        x, y, c = _place()
        copies = []
        for i in range(n):
            for j, (px, py) in enumerate(_other_chips(x, y)):
                copies.append(pltpu.make_async_remote_copy(
                    src_ref=_piece(ins[i], 2 * px + py, c), dst_ref=_piece(outs[i], 2 * px + py, c),
                    send_sem=send_sems.at[3 * i + j], recv_sem=recv_sems.at[3 * i + j], device_id=(x, y, 1 - c),
                    device_id_type=MESH))
                copies[-1].start()
        for cp in copies:
            cp.wait()

    return pl.pallas_call(
        body, name=name, in_specs=[_ANY] * n, out_specs=[_ANY] * n,
        out_shape=[jax.ShapeDtypeStruct(b.shape, b.dtype) for b in bufs], input_output_aliases={i: i for i in range(n)},
        scratch_shapes=[pltpu.SemaphoreType.DMA((3 * n,)), pltpu.SemaphoreType.DMA((3 * n,))],
        compiler_params=pltpu.CompilerParams(has_side_effects=True),
    )(*bufs)


def _swap_plan(srcs, lands):
    x, y, c = _place()
    plan = []
    for src, land in zip(srcs, lands):
        rh = src.shape[1] // 2
        plan.append((src.at[:, pl.ds(pl.multiple_of((1 - c) * rh, 16), rh), :], land, (x, y, 1 - c)))
    return plan


def _owners_plan(srcs, lands):
    x, y, c = _place()
    return [(src.at[2 * px + py], land.at[j], (px, py, c))
            for src, land in zip(srcs, lands) for j, (px, py) in enumerate(_other_chips(x, y))]


def _exchange_start(srcs, lands, plan, copies, name):
    n, m = len(srcs), len(srcs) + len(lands)

    def body(*refs):
        send_sems, recv_sems, token = refs[m], refs[m + 1], refs[-1]
        for k, (src, dst, dev) in enumerate(plan(refs[:n], refs[n:m])):
            pltpu.make_async_remote_copy(src_ref=src, dst_ref=dst, send_sem=send_sems.at[k], recv_sem=recv_sems.at[k],
                                         device_id=dev, device_id_type=MESH).start()
        token[...] = jnp.zeros_like(token)

    both = list(srcs) + list(lands)
    outs = pl.pallas_call(
        body, name=name, in_specs=[_HBM] * m,
        out_specs=[_SEM, _SEM] + [_HBM] * m + [pl.BlockSpec(memory_space=pltpu.VMEM)],
        out_shape=[pltpu.SemaphoreType.DMA((copies,)), pltpu.SemaphoreType.DMA((copies,))]
        + [pltpu.HBM(b.shape, b.dtype) for b in both] + [jax.ShapeDtypeStruct((8, 128), F32)],
        input_output_aliases={i: 2 + i for i in range(m)},
        compiler_params=pltpu.CompilerParams(has_side_effects=_EFFECT),
    )(*[pltpu.with_memory_space_constraint(b, pltpu.HBM) for b in both])
    return (outs[0], outs[1]), list(outs[2:2 + n]), list(outs[2 + n:2 + m]), outs[-1]


def _exchange_wait(sems, srcs, lands, plan, after, name):
    n, m = len(srcs), len(srcs) + len(lands)

    def body(*refs):
        send_sems, recv_sems = refs[m], refs[m + 1]
        for k, (src, dst, dev) in enumerate(plan(refs[:n], refs[n:m])):
            cp = pltpu.make_async_remote_copy(src_ref=src, dst_ref=dst, send_sem=send_sems.at[k], recv_sem=recv_sems.at[k],
                                              device_id=dev, device_id_type=MESH)
            cp.wait_send()
            cp.wait_recv()

    both = list(srcs) + list(lands)
    afters = list(after) if isinstance(after, (list, tuple)) else [after]
    outs = pl.pallas_call(
        body, name=name, in_specs=[_HBM] * m + [_SEM, _SEM] + [_ANY] * len(afters), out_specs=[_HBM] * m,
        out_shape=[pltpu.HBM(b.shape, b.dtype) for b in both], input_output_aliases={i: i for i in range(m)},
        compiler_params=pltpu.CompilerParams(has_side_effects=_EFFECT),
    )(*both, sems[0], sems[1], *afters)
    return list(outs[:n]), list(outs[n:])


def _fill_plan(bufs, _):
    x, y, c = _place()
    return [(_piece(buf, 2 * px + py, c), _piece(buf, 2 * px + py, c), (x, y, 1 - c))
            for buf in bufs for px, py in _other_chips(x, y)]


class _Reducer:
    def __init__(self, where):
        self.state = {}
        self.where = where

    def begin(self, stage, grads):
        names = list(grads)
        full = [grads[n] for n in names]
        lands = [lax.empty((NSH, g.shape[1] // 2, g.shape[2]), g.dtype) for g in full]
        sems, full, lands, token = _exchange_start(full, lands, _swap_plan, len(full), "swap_start_" + stage)
        self.state[stage] = (names, sems, full, lands)
        return token

    def advance(self, stage, after):
        names, sems, full, lands = self.state[stage]
        full, got = _exchange_wait(sems, full, lands, _swap_plan, after, "swap_wait_" + stage)
        sums, own = _chip_sums(full, got, self.where, "chip_sums_" + stage)
        lands = [lax.empty((3,) + s.shape[1:], BF16) for s in sums]
        sems, sent, lands, token = _exchange_start(sums, lands, _owners_plan, 3 * len(sums), "owners_start_" + stage)
        self.state[stage] = (names, own, sems, sent, lands)
        return token

    def finish(self, stage, after):
        names, own, sems, sent, lands = self.state[stage]
        _, got = _exchange_wait(sems, sent, lands, _owners_plan, after, "owners_wait_" + stage)
        return dict(zip(names, _owner_sums(own, got, "owner_sums_" + stage)))


def _chip_sums(gs, gots, where, name):
    n = len(gs)

    def body(where_ref, *refs):
        g_refs, got_refs, hb_refs, own_refs = (refs[k * n:(k + 1) * n] for k in range(4))
        mine = pl.program_id(0) == where_ref[1]
        for g_ref, got_ref, hb_ref, own_ref in zip(g_refs, got_refs, hb_refs, own_refs):
            h = g_ref[...].astype(F32) + got_ref[...].astype(F32)
            hb_ref[...] = h.astype(BF16)

            @pl.when(mine)
            def _():
                own_ref[...] = h

    halves = [(g.shape[1] // 2, g.shape[2]) for g in gs]
    slot = [pl.BlockSpec((None, rh, cc), lambda s, where: (s, 0, 0)) for rh, cc in halves]
    outs = pl.pallas_call(
        body, name=name,
        grid_spec=pltpu.PrefetchScalarGridSpec(
            num_scalar_prefetch=1, grid=(NSH,),
            in_specs=[pl.BlockSpec((None, rh, cc), lambda s, where: (s, where[0], 0)) for rh, cc in halves] + slot,
            out_specs=slot + [pl.BlockSpec((rh, cc), lambda s, where: (0, 0)) for rh, cc in halves]),
        out_shape=[jax.ShapeDtypeStruct((NSH, rh, cc), BF16) for rh, cc in halves]
        + [jax.ShapeDtypeStruct((rh, cc), F32) for rh, cc in halves],
        compiler_params=_params("arbitrary"),
    )(where, *gs, *gots)
    return list(outs[:n]), list(outs[n:])


def _owner_sums(owns, gots, name):
    n = len(owns)

    def body(*refs):
        own_refs, got_refs, o_refs = (refs[k * n:(k + 1) * n] for k in range(3))
        for own_ref, got_ref, o_ref in zip(own_refs, got_refs, o_refs):
            o_ref[...] = ((own_ref[...] + got_ref[0].astype(F32)) + got_ref[1].astype(F32)) + got_ref[2].astype(F32)

    blocks = [(o.shape[0] // 2, o.shape[1]) for o in owns]
    rows = [pl.BlockSpec(b, lambda i: (i, 0)) for b in blocks]
    return pl.pallas_call(
        body, grid=(2,), name=name,
        in_specs=rows + [pl.BlockSpec((3,) + b, lambda i: (0, i, 0)) for b in blocks], out_specs=rows,
        out_shape=[jax.ShapeDtypeStruct(o.shape, F32) for o in owns], compiler_params=_params("arbitrary"),
    )(*owns, *gots)


def _sibling_plan(srcs, lands):
    x, y, c = _place()
    return [(src, land, (x, y, 1 - c)) for src, land in zip(srcs, lands)]


def _all_reduce_small(part):
    def body(p_ref, o_ref, rbuf, send1, recv1, send2, recv2):
        x, y, c = _place()
        me = 4 * x + 2 * y + c
        peers = []
        for k in range(1, 8):
            px, py, pc = x ^ ((k >> 2) & 1), y ^ ((k >> 1) & 1), c ^ (k & 1)
            peers.append((k, (px, py, pc), 4 * px + 2 * py + pc))

        def rows(d):
            return pl.ds(pl.multiple_of(d * SMALL_SLICE, 8), SMALL_SLICE)

        first = [pltpu.make_async_remote_copy(src_ref=p_ref.at[rows(idx), :], dst_ref=rbuf.at[me], send_sem=send1.at[k],
                                              recv_sem=recv1.at[k], device_id=dev, device_id_type=MESH)
                 for k, dev, idx in peers]
        for cp in first:
            cp.start()
        rbuf[me] = p_ref[rows(me), :]
        for k, dev, idx in peers:
            pltpu.make_async_remote_copy(src_ref=p_ref.at[rows(idx), :], dst_ref=rbuf.at[idx], send_sem=send1.at[k],
                                         recv_sem=recv1.at[k], device_id=dev, device_id_type=MESH).wait_recv()
        acc = rbuf[0]
        for d in range(1, 8):
            acc = acc + rbuf[d]
        o_ref[rows(me), :] = acc
        second = [pltpu.make_async_remote_copy(src_ref=o_ref.at[rows(me), :], dst_ref=o_ref.at[rows(me), :],
                                               send_sem=send2.at[k], recv_sem=recv2.at[k], device_id=dev, device_id_type=MESH)
                  for k, dev, idx in peers]
        for cp in second:
            cp.start()
        for k, dev, idx in peers:
            pltpu.make_async_remote_copy(src_ref=o_ref.at[rows(me), :], dst_ref=o_ref.at[rows(idx), :], send_sem=send2.at[k],
                                         recv_sem=recv2.at[k], device_id=dev, device_id_type=MESH).wait_recv()
        for cp in first + second:
            cp.wait_send()

    return pl.pallas_call(
        body, name="all_reduce_small", in_specs=[_WHOLE], out_specs=_WHOLE,
        out_shape=jax.ShapeDtypeStruct((SMALL_ROWS, 128), F32),
        scratch_shapes=[pltpu.VMEM((8, SMALL_SLICE, 128), F32)] + [pltpu.SemaphoreType.DMA((8,))] * 4,
        compiler_params=pltpu.CompilerParams(has_side_effects=True),
    )(part)


def _adamw_update(w, gv, m, v):
    nm = ADAM_B1 * m + (1.0 - ADAM_B1) * gv
    nv = ADAM_B2 * v + (1.0 - ADAM_B2) * (gv * gv)
    m_hat = nm / (1.0 - ADAM_B1 ** ADAM_STEP)
    v_hat = nv / (1.0 - ADAM_B2 ** ADAM_STEP)
    return -ADAM_LR * (m_hat / (jnp.sqrt(v_hat) + ADAM_EPS) + ADAM_WD * w), nm, nv


def _adamw_small(ws, gs, ms, vs, after):
    n = len(ws)

    def body(*refs):
        w_refs, g_refs, m_refs, v_refs, d_refs, nm_refs, nv_refs = (refs[k * n:(k + 1) * n] for k in range(7))
        for i in range(n):
            d_refs[i][...], nm_refs[i][...], nv_refs[i][...] = _adamw_update(
                w_refs[i][...], g_refs[i][...], m_refs[i][...], v_refs[i][...])

    out = [jax.ShapeDtypeStruct(w.shape, F32) for w in ws]
    body, specs, operands = _behind(body, after)
    outs = pl.pallas_call(body, in_specs=specs + [_WHOLE] * (4 * n), out_specs=[_WHOLE] * (3 * n), out_shape=out * 3,
                          name="adamw_small", compiler_params=_params())(*operands, *ws, *gs, *ms, *vs)
    return outs[:n], outs[n:2 * n], outs[2 * n:]


def _adamw_halves(ws, mines, theirs, ms, vs, name):
    n = len(ws)
    steps = 2

    def body(*refs):
        w_refs, mine_refs, theirs_refs, m_refs, v_refs, g_refs, d_refs, nm_refs, nv_refs = (
            refs[k * n:(k + 1) * n] for k in range(9))
        is_mine = pl.program_id(0) == lax.axis_index("c")
        for i in range(n):
            gv = jnp.where(is_mine, mine_refs[i][...], theirs_refs[i][...])
            g_refs[i][...] = gv
            d_refs[i][...], nm_refs[i][...], nv_refs[i][...] = _adamw_update(w_refs[i][...], gv, m_refs[i][...], v_refs[i][...])

    blocks = [(h.shape[0] // steps, h.shape[1]) for h in mines]
    whole = [pl.BlockSpec(b, lambda h, i: (steps * h + i, 0)) for b in blocks]
    half = [pl.BlockSpec(b, lambda h, i: (i, 0)) for b in blocks]
    out = [jax.ShapeDtypeStruct(w.shape, F32) for w in ws]
    outs = pl.pallas_call(body, grid=(2, steps), in_specs=whole + half + half + whole + whole, out_specs=whole * 4,
                          out_shape=out * 4, name=name, compiler_params=_params("arbitrary", "arbitrary"),
                          )(*ws, *mines, *theirs, *ms, *vs)
    return [tuple(outs[k * n + i] for k in range(4)) for i in range(n)]


SMALL_USED = sum(size for _, size in SMALL) // 128


def _pack_small(vals, tail=None):
    parts = []
    for name, size in SMALL:
        flat = vals[name].reshape(-1).astype(F32)
        parts.append(jnp.pad(flat, (0, size - flat.shape[0])))
    if tail is not None:
        parts.append(tail.reshape(128))
    flat = jnp.concatenate(parts)
    return jnp.pad(flat, (0, SMALL_ROWS * 128 - flat.shape[0])).reshape(SMALL_ROWS, 128)


def _unpack_small(packed, shapes):
    flat = packed.reshape(-1)
    out, off = {}, 0
    for name, size in SMALL:
        n = math.prod(shapes[name])
        out[name] = flat[off:off + n].reshape(shapes[name])
        off += size
    return out


def kernel(x, ffn1_pre_g, ffn1_w1, ffn1_w3, ffn1_w2, ffn1_post_g, mix_pre_g, w_in, conv_w, conv_b, rg_a_w, rg_a_b, rg_x_w, rg_x_b, lru_lambda, w_lru_out, attn_sinks, rel_bias, w_attn_out, w_gate, b_gate, w_o, mix_post_g, ffn2_pre_g, ffn2_w1, ffn2_w3, ffn2_w2, ffn2_post_g, loss_target, m_ffn1_pre_g, m_ffn1_w1, m_ffn1_w3, m_ffn1_w2, m_ffn1_post_g, m_mix_pre_g, m_w_in, m_conv_w, m_conv_b, m_rg_a_w, m_rg_a_b, m_rg_x_w, m_rg_x_b, m_lru_lambda, m_w_lru_out, m_attn_sinks, m_rel_bias, m_w_attn_out, m_w_gate, m_b_gate, m_w_o, m_mix_post_g, m_ffn2_pre_g, m_ffn2_w1, m_ffn2_w3, m_ffn2_w2, m_ffn2_post_g, v_ffn1_pre_g, v_ffn1_w1, v_ffn1_w3, v_ffn1_w2, v_ffn1_post_g, v_mix_pre_g, v_w_in, v_conv_w, v_conv_b, v_rg_a_w, v_rg_a_b, v_rg_x_w, v_rg_x_b, v_lru_lambda, v_w_lru_out, v_attn_sinks, v_rel_bias, v_w_attn_out, v_w_gate, v_b_gate, v_w_o, v_mix_post_g, v_ffn2_pre_g, v_ffn2_w1, v_ffn2_w3, v_ffn2_w2, v_ffn2_post_g):
    given = dict(locals())
    chip = 2 * lax.axis_index("x") + lax.axis_index("y")
    transposed = ("ffn1_w1", "ffn1_w3", "ffn2_w1", "ffn2_w3")

    def shard(name, moment=""):
        w = given[moment + name][0]
        return w.T if name in transposed else w

    def unshard(name, w):
        return (w.T if name in transposed else w)[None]

    def only_my_columns(a):
        parts = a.reshape(1, 4, NSH, D // NSH)
        return sum(jnp.where(chip == s, parts[:, :, s], 0.0) for s in range(NSH))

    chip_arr = jnp.reshape(chip, (1,)).astype(jnp.int32)
    stage_names = {"ffn1_up": ["ffn1_w1", "ffn1_w3", "conv_w"],
                   "ffn1_down": ["ffn1_w2"],
                   "mix_in": ["w_in", "w_gate"],
                   "mix_out": ["w_lru_out", "w_attn_out", "w_o"],
                   "ffn2": ["ffn2_w1", "ffn2_w3", "ffn2_w2"]}
    in_flight, started = {}, None
    for stage, names in stage_names.items():
        bufs = [jnp.where(lax.broadcasted_iota(jnp.int32, (NSH, 4, D // NSH), 0) == chip, given[n], 0.0) if n == "conv_w"
                else _cast_into_slot(shard(n), chip_arr, "cast_" + n, started) for n in names]
        (in_flight[stage],), started = _gather_start([bufs], "gather_start_" + stage)
    all_started = started

    filling = {}

    def weights(stage, after, begin=False):
        names = stage_names[stage]
        halves_of = [n for n in names if n != "conv_w"]
        if stage in filling:
            filled, _ = _exchange_wait(filling.pop(stage), *filling.pop(stage + "/bufs"), _fill_plan, after,
                                       "fill_wait_" + stage)
            return dict(zip(halves_of, filled))
        send_sems, recv_sems, landing = in_flight[stage]
        if stage == "ffn1_up":
            after = [all_started] + list(after)
        landed = dict(zip(names, _gather_wait(send_sems, recv_sems, landing, after, "gather_wait_" + stage)))
        halves = [landed[n] for n in halves_of]
        if begin:
            filling[stage], bufs, _, token = _exchange_start(halves, [], _fill_plan, 3 * len(halves), "fill_start_" + stage)
            filling[stage + "/bufs"] = (bufs, [])
            return token
        out = dict(zip(halves_of, _sibling_fill(halves, "sibling_fill_" + stage)))
        if "conv_w" in names:
            out["conv_w"] = jnp.transpose(landed["conv_w"], (1, 0, 2)).reshape(4, D)
        return out

    small_shapes = {n: given[n].shape for n, _ in SMALL}
    small_shapes["conv_w"] = (1, 4, D)
    sm = {n: (given[n][0] if given[n].shape[0] == 1 and n != "rel_bias" else given[n]) for n, _ in SMALL if n != "conv_w"}

    reducer = _Reducer(jnp.stack([lax.axis_index("c"), chip]).astype(jnp.int32))
    sq, dx, _, small = _local_step(x[0], loss_target[0], weights, sm, reducer)

    reduced_small = _all_reduce_small(_pack_small(small, tail=sq))
    last_started = reducer.advance("ffn1", [dx, reduced_small])
    loss = reduced_small[SMALL_USED, 0] * (0.5 / D)
    small_g = _unpack_small(reduced_small, small_shapes)
    grads, delta, new_m, new_v = {}, {}, {}, {}
    in_transit = {}

    def send(stage, after):
        halves = reducer.finish(stage, after)
        lands = [lax.empty(h.shape, F32) for h in halves.values()]
        sems, mine, lands, token = _exchange_start(list(halves.values()), lands, _sibling_plan, len(lands),
                                                   "halves_start_" + stage)
        in_transit[stage] = (list(halves), sems, mine, lands)
        return token

    def update(stage, after):
        names, sems, mine, lands = in_transit[stage]
        mine, theirs = _exchange_wait(sems, mine, lands, _sibling_plan, after, "halves_wait_" + stage)
        updated = _adamw_halves([shard(n) for n in names], mine, theirs, [shard(n, "m_") for n in names],
                                [shard(n, "v_") for n in names], "adamw_" + stage)
        for n, results in zip(names, updated):
            grads[n], delta[n], new_m[n], new_v[n] = (unshard(n, r) for r in results)
        return new_v[names[-1]]

    token = send("ffn2", [reduced_small, last_started])
    token = send("mix", token)
    done = update("ffn2", token)
    done = update("mix", done)
    token = send("ffn1", done)
    update("ffn1", token)

    small_g["conv_w"] = only_my_columns(small_g["conv_w"])
    names = [n for n, _ in SMALL]
    flat2d = lambda a: a.reshape(-1, a.shape[-1])
    outs = _adamw_small(*[[flat2d(given[pre + n]) if pre != "g" else flat2d(small_g[n]) for n in names]
                          for pre in ("", "g", "m_", "v_")], after=last_started)
    for dst, arrs in zip((delta, new_m, new_v), outs):
        dst.update({n: a.reshape(given[n].shape) for n, a in zip(names, arrs)})
    grads.update(small_g)
    return (loss, dx[None], *[grads[n] for n in WEIGHTS], *[delta[n] for n in WEIGHTS], *[new_m[n] for n in WEIGHTS],
            *[new_v[n] for n in WEIGHTS])
```

```python
import math

import jax
import jax.numpy as jnp
from jax import lax
from jax.experimental import pallas as pl
from jax.experimental.pallas import tpu as pltpu

F32, BF16 = jnp.float32, jnp.bfloat16
D = 1024
NSH = 4
FF_S = 704
IN_S = 896
GATE_S = 512
KV_W = 256
CHUNK = 64
KB = 192
N_HEADS = 16
HEAD_DIM = 64
N_BUCKETS = 32
KP = 192
PAD_KEYS = 128
RMS_EPS = 1e-6
NEG_INF = -1e30
LRU_C = 8.0
TM = 512
TM_SCAN = 256
VMEM_LIMIT = 56 * 1024 * 1024
ADAM_LR, ADAM_B1, ADAM_B2, ADAM_EPS, ADAM_WD, ADAM_STEP = 0.001, 0.9, 0.999, 1e-08, 0.01, 10
SMALL_ROWS = 1216
SMALL_SLICE = SMALL_ROWS // 8
MESH = pl.DeviceIdType.MESH

BIG = ["ffn1_w1", "ffn1_w3", "ffn1_w2", "w_in", "w_lru_out", "w_attn_out", "w_gate", "w_o", "ffn2_w1", "ffn2_w3", "ffn2_w2"]
SMALL = [("ffn1_pre_g", 1024), ("ffn1_post_g", 1024), ("mix_pre_g", 1024), ("conv_w", 4096), ("conv_b", 1024),
         ("rg_a_w", 65536), ("rg_a_b", 1024), ("rg_x_w", 65536), ("rg_x_b", 1024), ("lru_lambda", 1024),
         ("attn_sinks", 1024), ("rel_bias", 1024), ("b_gate", 2048), ("mix_post_g", 1024), ("ffn2_pre_g", 1024),
         ("ffn2_post_g", 1024)]
WEIGHTS = ["ffn1_pre_g", "ffn1_w1", "ffn1_w3", "ffn1_w2", "ffn1_post_g", "mix_pre_g", "w_in", "conv_w", "conv_b", "rg_a_w",
           "rg_a_b", "rg_x_w", "rg_x_b", "lru_lambda", "w_lru_out", "attn_sinks", "rel_bias", "w_attn_out", "w_gate", "b_gate",
           "w_o", "mix_post_g", "ffn2_pre_g", "ffn2_w1", "ffn2_w3", "ffn2_w2", "ffn2_post_g"]


def _params(*sem):
    return pltpu.CompilerParams(dimension_semantics=sem or None, vmem_limit_bytes=VMEM_LIMIT)


def _nn(a, b):
    return jnp.dot(a, b, preferred_element_type=F32)


def _nt(a, b):
    return lax.dot_general(a, b, (((1,), (1,)), ((), ())), preferred_element_type=F32)


def _tn(a, b):
    return lax.dot_general(a, b, (((0,), (0,)), ((), ())), preferred_element_type=F32)


def _rms(x, g):
    rstd = lax.rsqrt(jnp.mean(x * x, axis=-1, keepdims=True) + RMS_EPS)
    return (x * rstd) * g


def _rms_bwd(dout, x, g):
    rstd = lax.rsqrt(jnp.mean(x * x, axis=-1, keepdims=True) + RMS_EPS)
    xhat = x * rstd
    dg = jnp.sum(dout * xhat, axis=0, keepdims=True)
    dxhat = dout * g
    dx = rstd * (dxhat - xhat * jnp.mean(dxhat * xhat, axis=-1, keepdims=True))
    return dx, dg


_GELU_K = math.sqrt(2.0 / math.pi)


_GELU_C = 0.044715 * _GELU_K


def _gelu_and_grad(x):
    x2 = x * x
    t = jnp.tanh(x * (_GELU_K + _GELU_C * x2))
    cdf = 0.5 + 0.5 * t
    return x * cdf, cdf + (x * (_GELU_K + (3.0 * _GELU_C) * x2)) * (0.5 - 0.5 * (t * t))


def _softplus_neg(lam):
    z = -lam
    u = jnp.exp(-jnp.abs(z))
    w = 1.0 + u
    log1p_u = jnp.where(w == 1.0, u, jnp.log(w) * (u / (w - 1.0)))
    return jnp.maximum(z, 0.0) + log1p_u


def _lru_coeffs(r, sp):
    log_a = (-LRU_C * r) * sp
    a = jnp.exp(log_a)
    t = jnp.tanh(log_a)
    s = jnp.sqrt(-2.0 * t / (1.0 - t))
    return a, s


def _row_spec(tm, width):
    return pl.BlockSpec((tm, width), lambda i: (i, 0))


def _vec_spec(width):
    return pl.BlockSpec((1, width), lambda i: (0, 0))


_WHOLE = pl.BlockSpec(memory_space=pltpu.VMEM)


def _tile(t, tm=TM):
    return min(tm, t)


def _ffn_fwd(x, gpre, w1g, w3g, w2g, gpost, name, target=None):
    t = x.shape[0]
    tm = _tile(t)
    last = target is not None

    def body(x_ref, gpre_ref, w1_ref, w3_ref, w2_ref, gpost_ref, *refs):
        t_ref, (h_ref, a_ref, b_ref, hm_ref, f_ref), l_ref = (refs[0] if last else None), refs[last:last + 5], refs[-1]
        xv = x_ref[...]
        nb = _rms(xv, gpre_ref[...]).astype(BF16)
        f = jnp.zeros((tm, D), F32)
        for s in range(NSH):
            a = _nt(nb, w1_ref[s])
            b = _nt(nb, w3_ref[s])
            hmb = ((a * jax.nn.sigmoid(a)) * b).astype(BF16)
            a_ref[s] = a.astype(BF16)
            b_ref[s] = b.astype(BF16)
            hm_ref[s] = hmb
            f = f + _nn(hmb, w2_ref[s])
        f_ref[...] = f
        h = xv + 0.5 * _rms(f, gpost_ref[...])
        if last:
            @pl.when(pl.program_id(0) == 0)
            def _():
                l_ref[...] = jnp.zeros_like(l_ref)

            e = h - t_ref[...]
            h_ref[...] = e * (1.0 / D)
            l_ref[...] += jnp.sum(jnp.sum(e * e, axis=0, keepdims=True), axis=1, keepdims=True)
        else:
            h_ref[...] = h

    sh = pl.BlockSpec((NSH, tm, FF_S), lambda i: (0, i, 0))
    act = jax.ShapeDtypeStruct((NSH, t, FF_S), BF16)
    return pl.pallas_call(
        body, grid=(t // tm,), name=name,
        in_specs=[_row_spec(tm, D), _vec_spec(D), _WHOLE, _WHOLE, _WHOLE, _vec_spec(D)] + [_row_spec(tm, D)] * last,
        out_specs=[_row_spec(tm, D), sh, sh, sh, _row_spec(tm, D)] + [pl.BlockSpec((1, 128), lambda i: (0, 0))] * last,
        out_shape=[jax.ShapeDtypeStruct((t, D), F32), act, act, act, jax.ShapeDtypeStruct((t, D), F32)]
        + [jax.ShapeDtypeStruct((1, 128), F32)] * last,
        compiler_params=_params("arbitrary"),
    )(x, gpre, w1g, w3g, w2g, gpost, *([target] if last else []))


def _ffn_up(x, gpre, w1g, w3g, name):
    t = x.shape[0]
    tm = _tile(t)

    def body(x_ref, gpre_ref, w1_ref, w3_ref, n_ref, a_ref, b_ref, hm_ref):
        nb = _rms(x_ref[...], gpre_ref[...]).astype(BF16)
        n_ref[...] = nb
        for s in range(NSH):
            a = _nt(nb, w1_ref[s])
            b = _nt(nb, w3_ref[s])
            a_ref[s] = a.astype(BF16)
            b_ref[s] = b.astype(BF16)
            hm_ref[s] = ((a * jax.nn.sigmoid(a)) * b).astype(BF16)

    sh = pl.BlockSpec((NSH, tm, FF_S), lambda i: (0, i, 0))
    act = jax.ShapeDtypeStruct((NSH, t, FF_S), BF16)
    return pl.pallas_call(
        body, grid=(t // tm,), name=name, in_specs=[_row_spec(tm, D), _vec_spec(D), _WHOLE, _WHOLE],
        out_specs=[_row_spec(tm, D), sh, sh, sh], out_shape=[jax.ShapeDtypeStruct((t, D), BF16), act, act, act],
        compiler_params=_params("arbitrary"),
    )(x, gpre, w1g, w3g)


def _ffn_down(x, hm, w2g, gpost, name):
    t = x.shape[0]
    tm = _tile(t)

    def body(x_ref, hm_ref, w2_ref, gpost_ref, h_ref, f_ref):
        f = jnp.zeros((tm, D), F32)
        for s in range(NSH):
            f = f + _nn(hm_ref[s], w2_ref[s])
        f_ref[...] = f
        h_ref[...] = x_ref[...] + 0.5 * _rms(f, gpost_ref[...])

    sh = pl.BlockSpec((NSH, tm, FF_S), lambda i: (0, i, 0))
    f32 = jax.ShapeDtypeStruct((t, D), F32)
    return pl.pallas_call(
        body, grid=(t // tm,), name=name, in_specs=[_row_spec(tm, D), sh, _WHOLE, _vec_spec(D)],
        out_specs=[_row_spec(tm, D), _row_spec(tm, D)], out_shape=[f32, f32], compiler_params=_params("arbitrary"),
    )(x, hm, w2g, gpost)


def _mix_proj(h1, gmix, w_in_g, w_gate_g, b_gate):
    t = h1.shape[0]
    tm = _tile(t)

    def body(h_ref, g_ref, win_ref, wg_ref, bg_ref, u_ref, q_ref, k_ref, v_ref, xr_ref, xg_ref, gate_ref):
        ub = _rms(h_ref[...], g_ref[...]).astype(BF16)
        u_ref[...] = ub
        p0 = _nn(ub, win_ref[0])
        q_ref[:, 0:896] = p0.astype(BF16)
        p1 = _nn(ub, win_ref[1])
        q_ref[:, 896:1024] = p1[:, 0:128].astype(BF16)
        k_ref[...] = p1[:, 128:384].astype(BF16)
        v_ref[...] = p1[:, 384:640].astype(BF16)
        xr_ref[:, 0:256] = p1[:, 640:896]
        p2 = _nn(ub, win_ref[2])
        xr_ref[:, 256:1024] = p2[:, 0:768]
        xg_ref[:, 0:128] = p2[:, 768:896]
        xg_ref[:, 128:1024] = _nn(ub, win_ref[3])
        for s in range(NSH):
            sl = slice(s * GATE_S, (s + 1) * GATE_S)
            gate_ref[:, sl] = jax.nn.sigmoid(_nn(ub, wg_ref[s]) + bg_ref[:, sl]).astype(BF16)

    return pl.pallas_call(
        body, grid=(t // tm,), name="mix_proj",
        in_specs=[_row_spec(tm, D), _vec_spec(D), _WHOLE, _WHOLE, _vec_spec(2 * D)],
        out_specs=[_row_spec(tm, D), _row_spec(tm, D), _row_spec(tm, KV_W), _row_spec(tm, KV_W), _row_spec(tm, D),
                   _row_spec(tm, D), _row_spec(tm, 2 * D)],
        out_shape=[jax.ShapeDtypeStruct((t, D), BF16), jax.ShapeDtypeStruct((t, D), BF16),
                   jax.ShapeDtypeStruct((t, KV_W), BF16), jax.ShapeDtypeStruct((t, KV_W), BF16),
                   jax.ShapeDtypeStruct((t, D), F32), jax.ShapeDtypeStruct((t, D), F32),
                   jax.ShapeDtypeStruct((t, 2 * D), BF16)],
        compiler_params=_params("arbitrary"),
    )(h1, gmix, w_in_g, w_gate_g, b_gate)


def _rglru_fwd(xr, xg, conv_w, conv_b, wa2, ba, wx2, bx, lam, after=None):
    t = xr.shape[0]
    tm = _tile(t, TM_SCAN)
    nb8 = tm // 8

    def body(xr_ref, xrp_ref, xg_ref, cw_ref, cb_ref, wa_ref, ba_ref, wx_ref, bx_ref, lam_ref,
             hr_ref, yain_ref, xc_ref, r_ref, ig_ref, a_sc, s_ref, ext, h_sc):
        i = pl.program_id(0)

        @pl.when(i == 0)
        def _():
            h_sc[...] = jnp.zeros_like(h_sc)

        ext[0:8, :] = jnp.where(i == 0, 0.0, xrp_ref[...])
        ext[8:8 + tm, :] = xr_ref[...]
        xc = jnp.broadcast_to(cb_ref[...], (tm, D))
        for tap in range(4):
            xc = xc + ext[pl.ds(5 + tap, tm), :] * cw_ref[tap:tap + 1, :]
        xc_ref[...] = xc
        xcb = xc.astype(BF16)
        for p in range(8):
            sl = slice(p * 128, (p + 1) * 128)
            r_ref[:, sl] = jax.nn.sigmoid(_nn(xcb[:, sl], wa_ref[p]) + ba_ref[:, sl])
            ig_ref[:, sl] = jax.nn.sigmoid(_nn(xcb[:, sl], wx_ref[p]) + bx_ref[:, sl])
        a, s = _lru_coeffs(r_ref[...], _softplus_neg(lam_ref[...]))
        a_sc[...] = a
        s_ref[...] = s
        hr_ref[...] = s * (ig_ref[...] * xc)

        def blk(j, h):
            st = pl.multiple_of(j * 8, 8)
            a8 = a_sc[pl.ds(st, 8), :]
            u8 = hr_ref[pl.ds(st, 8), :]
            rows = []
            for k in range(8):
                h = a8[k:k + 1, :] * h + u8[k:k + 1, :]
                rows.append(h)
            hr_ref[pl.ds(st, 8), :] = jnp.concatenate(rows, axis=0)
            return h

        h_sc[0:1, :] = lax.fori_loop(0, nb8, blk, h_sc[0:1, :])
        yain_ref[...] = (hr_ref[...] * _gelu_and_grad(xg_ref[...])[0]).astype(BF16)

    prev = pl.BlockSpec((8, D), lambda i: (jnp.maximum(i * nb8 - 1, 0), 0))
    full = lambda shape: pl.BlockSpec(shape, lambda i: tuple(0 for _ in shape))
    f32 = jax.ShapeDtypeStruct((t, D), F32)
    body, specs, operands = _behind(body, after)
    return pl.pallas_call(
        body, grid=(t // tm,), name="rglru_fwd",
        in_specs=specs + [_row_spec(tm, D), prev, _row_spec(tm, D), full((4, D)), _vec_spec(D), full((8, 128, 128)),
                          _vec_spec(D), full((8, 128, 128)), _vec_spec(D), _vec_spec(D)],
        out_specs=[_row_spec(tm, D)] * 7,
        out_shape=[f32, jax.ShapeDtypeStruct((t, D), BF16), f32, f32, f32, f32, f32],
        scratch_shapes=[pltpu.VMEM((tm + 8, D), F32), pltpu.VMEM((8, D), F32)],
        compiler_params=_params("arbitrary"),
    )(*operands, xr, xr, xg, conv_w, conv_b, wa2, ba, wx2, bx, lam)


def _bias_fwd(table_t, onehot_t):
    def body(t_ref, e_ref, o_ref):
        o_ref[...] = jnp.dot(t_ref[...], e_ref[...], preferred_element_type=F32, precision=lax.Precision.HIGHEST)

    return pl.pallas_call(body, out_shape=jax.ShapeDtypeStruct((N_HEADS, CHUNK * KB), F32), name="bias_fwd",
                          compiler_params=_params())(table_t, onehot_t)


def _bias_bwd(dbias_flat, onehot_t, ds_rows):
    def body(d_ref, e_ref, s_ref, o_ref, so_ref):
        o_ref[...] = lax.dot_general(d_ref[...], e_ref[...], (((1,), (1,)), ((), ())), preferred_element_type=F32,
                                     precision=lax.Precision.HIGHEST)
        so_ref[...] = jnp.zeros_like(so_ref)
        for r in range(4):
            so_ref[:, r:r + 1] = jnp.sum(s_ref[:, r * CHUNK:(r + 1) * CHUNK], axis=1, keepdims=True)

    return pl.pallas_call(body, out_shape=[jax.ShapeDtypeStruct((N_HEADS, N_BUCKETS), F32), jax.ShapeDtypeStruct((8, 128), F32)],
                          name="bias_bwd", compiler_params=_params())(dbias_flat, onehot_t, ds_rows)


def _stack_heads(q):
    return jnp.concatenate(
        [jnp.concatenate([q[:, (4 * g + r) * HEAD_DIM:(4 * g + r + 1) * HEAD_DIM] for g in range(4)], axis=1)
         for r in range(4)], axis=0)


def _unstack_heads(o):
    return jnp.concatenate([o[r * CHUNK:(r + 1) * CHUNK, g * HEAD_DIM:(g + 1) * HEAD_DIM] for g in range(4) for r in range(4)],
                           axis=1)


def _block_diag(w, mask):
    return jnp.concatenate([w] * 4, axis=0) * mask


def _group_softmax(qk, bias_g, sink, valid):
    s = qk * (HEAD_DIM ** -0.5) + bias_g
    s = jnp.where(valid, s, NEG_INF)
    m = jnp.maximum(jnp.max(s, axis=0, keepdims=True), sink)
    e = jnp.exp(s - m)
    es = jnp.exp(sink - m)
    inv = 1.0 / (jnp.sum(e, axis=0, keepdims=True) + es)
    return e * inv, es * inv


def _attn_fwd(sink_rows, q, kp, vp, bias_t, mask, after=None):
    t = q.shape[0]
    per_step = 8

    def body(sink_ref, q_ref, kp_ref, vp_ref, bias_ref, mask_ref, o_ref):
        owns = [mask_ref[g * KP:(g + 1) * KP, :] for g in range(4)]
        for k in range(per_step):
            c = pl.program_id(0) * per_step + k
            rows = slice(k * CHUNK, (k + 1) * CHUNK)
            st = pl.multiple_of(c * CHUNK, CHUNK)
            kw = kp_ref[pl.ds(st, KP), :]
            vw = vp_ref[pl.ds(st, KP), :]
            q_all = _stack_heads(q_ref[rows, :])
            valid = lax.broadcasted_iota(jnp.int32, (KP, 1), 0) + c * CHUNK >= PAD_KEYS
            scores = [_nt(kw * owns[g], q_all) for g in range(4)]
            ps = [_group_softmax(scores[g], bias_ref[g * KP:(g + 1) * KP, :], sink_ref[g:g + 1, :], valid)[0]
                  for g in range(4)]
            o_all = sum(_tn(ps[g].astype(BF16), vw * owns[g]) for g in range(4))
            o_ref[rows, :] = _unstack_heads(o_all).astype(BF16)

    body, specs, operands = _behind(body, after)
    return pl.pallas_call(
        body, grid=(t // (per_step * CHUNK),), name="attn_fwd",
        in_specs=specs + [_WHOLE, _row_spec(per_step * CHUNK, D), _WHOLE, _WHOLE, _WHOLE, _WHOLE],
        out_specs=_row_spec(per_step * CHUNK, D),
        out_shape=jax.ShapeDtypeStruct((t, D), BF16),
        compiler_params=_params("arbitrary"),
    )(*operands, sink_rows, q, kp, vp, bias_t, mask)


def _merge_fwd(yain, o, gate, h1, w_lru, w_att, w_o, gpost):
    t = h1.shape[0]
    tm = _tile(t)

    def body(ya_ref, o_ref, g_ref, h_ref, wl_ref, wa_ref, wo_ref, gp_ref, h2_ref, mo_ref, mg_ref, ya_out, yb_out):
        ya = _nn(ya_ref[...], wl_ref[...])
        yb = _nn(o_ref[...], wa_ref[...])
        g0 = g_ref[:, 0:D].astype(F32)
        g1 = g_ref[:, D:2 * D].astype(F32)
        mg = (g0 * ya + g1 * yb).astype(BF16)
        mo = _nn(mg, wo_ref[...])
        ya_out[...] = (ya * (g0 * (1.0 - g0))).astype(BF16)
        yb_out[...] = (yb * (g1 * (1.0 - g1))).astype(BF16)
        mg_ref[...] = mg
        mo_ref[...] = mo
        h2_ref[...] = h_ref[...] + _rms(mo, gp_ref[...])

    f32 = jax.ShapeDtypeStruct((t, D), F32)
    b16 = jax.ShapeDtypeStruct((t, D), BF16)
    return pl.pallas_call(
        body, grid=(t // tm,), name="merge_fwd",
        in_specs=[_row_spec(tm, D), _row_spec(tm, D), _row_spec(tm, 2 * D), _row_spec(tm, D), _WHOLE, _WHOLE, _WHOLE,
                  _vec_spec(D)],
        out_specs=[_row_spec(tm, D)] * 5,
        out_shape=[f32, f32, b16, b16, b16],
        compiler_params=_params("arbitrary"),
    )(yain, o, gate, h1, w_lru, w_att, w_o, gpost)


def _ffn_bwd(dh, x, f, a, b, gpre, gpost, w1g, w3g, w2g, name):
    t = x.shape[0]
    tm = _tile(t, TM_SCAN)

    def body(dh_ref, x_ref, f_ref, a_ref, b_ref, gpre_ref, gpost_ref, w1_ref, w3_ref, w2_ref,
             dx_ref, n_ref, da_ref, db_ref, df_ref, dgpre_ref, dgpost_ref):
        @pl.when(pl.program_id(0) == 0)
        def _():
            dgpre_ref[...] = jnp.zeros_like(dgpre_ref)
            dgpost_ref[...] = jnp.zeros_like(dgpost_ref)

        dhv = dh_ref[...]
        xv = x_ref[...]
        df, dgp = _rms_bwd(0.5 * dhv, f_ref[...], gpost_ref[...])
        dgpost_ref[...] += dgp
        dfb = df.astype(BF16)
        df_ref[...] = dfb
        n_ref[...] = _rms(xv, gpre_ref[...]).astype(BF16)
        dn = jnp.zeros((tm, D), F32)
        for s in range(NSH):
            av = a_ref[s].astype(F32)
            bv = b_ref[s].astype(F32)
            sg = jax.nn.sigmoid(av)
            dhm = _nt(dfb, w2_ref[s])
            dab = (dhm * bv * (sg * (1.0 + av * (1.0 - sg)))).astype(BF16)
            dbb = (dhm * (av * sg)).astype(BF16)
            da_ref[s] = dab
            db_ref[s] = dbb
            dn = dn + _nn(dab, w1_ref[s]) + _nn(dbb, w3_ref[s])
        dxn, dg = _rms_bwd(dn, xv, gpre_ref[...])
        dgpre_ref[...] += dg
        dx_ref[...] = dhv + dxn

    sh = pl.BlockSpec((NSH, tm, FF_S), lambda i: (0, i, 0))
    act = jax.ShapeDtypeStruct((NSH, t, FF_S), BF16)
    vec = jax.ShapeDtypeStruct((1, D), F32)
    return pl.pallas_call(
        body, grid=(t // tm,), name=name,
        in_specs=[_row_spec(tm, D), _row_spec(tm, D), _row_spec(tm, D), sh, sh, _vec_spec(D), _vec_spec(D), _WHOLE, _WHOLE,
                  _WHOLE],
        out_specs=[_row_spec(tm, D), _row_spec(tm, D), sh, sh, _row_spec(tm, D), _vec_spec(D), _vec_spec(D)],
        out_shape=[jax.ShapeDtypeStruct((t, D), F32), jax.ShapeDtypeStruct((t, D), BF16), act, act,
                   jax.ShapeDtypeStruct((t, D), BF16), vec, vec],
        compiler_params=_params("arbitrary"),
    )(dh, x, f, a, b, gpre, gpost, w1g, w3g, w2g)


def _behind(body, after):
    if after is None:
        return body, [], []

    def ordered(_, *refs):
        body(*refs)

    return ordered, [_ANY], [after]


def _ffn_bwd_acts(dh, f, a, b, gpost, w2g, name):
    t = dh.shape[0]
    tm = _tile(t)

    def body(dh_ref, f_ref, a_ref, b_ref, gpost_ref, w2_ref, da_ref, db_ref, df_ref, dgpost_ref):
        @pl.when(pl.program_id(0) == 0)
        def _():
            dgpost_ref[...] = jnp.zeros_like(dgpost_ref)

        df, dgp = _rms_bwd(0.5 * dh_ref[...], f_ref[...], gpost_ref[...])
        dgpost_ref[...] += dgp
        dfb = df.astype(BF16)
        df_ref[...] = dfb
        for s in range(NSH):
            av = a_ref[s].astype(F32)
            bv = b_ref[s].astype(F32)
            sg = jax.nn.sigmoid(av)
            dhm = _nt(dfb, w2_ref[s])
            da_ref[s] = (dhm * bv * (sg * (1.0 + av * (1.0 - sg)))).astype(BF16)
            db_ref[s] = (dhm * (av * sg)).astype(BF16)

    sh = pl.BlockSpec((NSH, tm, FF_S), lambda i: (0, i, 0))
    act = jax.ShapeDtypeStruct((NSH, t, FF_S), BF16)
    b16 = jax.ShapeDtypeStruct((t, D), BF16)
    return pl.pallas_call(
        body, grid=(t // tm,), name=name,
        in_specs=[_row_spec(tm, D), _row_spec(tm, D), sh, sh, _vec_spec(D), _WHOLE],
        out_specs=[sh, sh, _row_spec(tm, D), _vec_spec(D)],
        out_shape=[act, act, b16, jax.ShapeDtypeStruct((1, D), F32)],
        compiler_params=_params("arbitrary"),
    )(dh, f, a, b, gpost, w2g)


def _ffn_bwd_input(dh, x, da, db, gpre, w1g, w3g, name, after):
    t = x.shape[0]
    tm = _tile(t)

    def body(dh_ref, x_ref, da_ref, db_ref, gpre_ref, w1_ref, w3_ref, dx_ref, dgpre_ref):
        @pl.when(pl.program_id(0) == 0)
        def _():
            dgpre_ref[...] = jnp.zeros_like(dgpre_ref)

        dn = jnp.zeros((tm, D), F32)
        for s in range(NSH):
            dn = dn + _nn(da_ref[s], w1_ref[s]) + _nn(db_ref[s], w3_ref[s])
        dxn, dg = _rms_bwd(dn, x_ref[...], gpre_ref[...])
        dgpre_ref[...] += dg
        dx_ref[...] = dh_ref[...] + dxn

    sh = pl.BlockSpec((NSH, tm, FF_S), lambda i: (0, i, 0))
    body, specs, operands = _behind(body, after)
    return pl.pallas_call(
        body, grid=(t // tm,), name=name,
        in_specs=specs + [_row_spec(tm, D), _row_spec(tm, D), sh, sh, _vec_spec(D), _WHOLE, _WHOLE],
        out_specs=[_row_spec(tm, D), _vec_spec(D)],
        out_shape=[jax.ShapeDtypeStruct((t, D), F32), jax.ShapeDtypeStruct((1, D), F32)],
        compiler_params=_params("arbitrary"),
    )(*operands, dh, x, da, db, gpre, w1g, w3g)


def _wgrad(a, b, a_spec, b_spec, out_spec, out_shape, grid, name, after=None):
    def body(a_ref, b_ref, o_ref):
        o_ref[...] = _tn(a_ref[...], b_ref[...]).astype(BF16)

    body, specs, operands = _behind(body, after)
    return pl.pallas_call(body, grid=grid, name=name, in_specs=specs + [a_spec, b_spec], out_specs=out_spec,
                          out_shape=jax.ShapeDtypeStruct(out_shape, BF16),
                          compiler_params=_params(*("arbitrary",) * len(grid)))(*operands, a, b)


def _wgrad_cols(act, dsh, width, name, after=None):
    t = act.shape[0]
    if dsh.ndim == 3:
        b_spec = pl.BlockSpec((None, t, width), lambda s, k: (s, 0, 0))
    else:
        b_spec = pl.BlockSpec((t, width), lambda s, k: (0, s))
    return _wgrad(act, dsh, pl.BlockSpec((t, 512), lambda s, k: (0, k)), b_spec,
                  pl.BlockSpec((None, 512, width), lambda s, k: (s, k, 0)), (NSH, D, width), (NSH, 2), name, after)


def _wgrad_rows(hm, df, name, after=None):
    t = df.shape[0]
    return _wgrad(hm, df, pl.BlockSpec((None, t, FF_S), lambda s: (s, 0, 0)), pl.BlockSpec((t, D), lambda s: (0, 0)),
                  pl.BlockSpec((None, FF_S, D), lambda s: (s, 0, 0)), (NSH, FF_S, D), (NSH,), name, after)


def _wgrad_sq(a, b, name, after=None):
    t = a.shape[0]
    return _wgrad(a, b, pl.BlockSpec((t, D), lambda j: (0, 0)), pl.BlockSpec((t, 512), lambda j: (0, j)),
                  pl.BlockSpec((D, 512), lambda j: (0, j)), (D, D), (2,), name, after)


def _mix_bwd1(dh2, mo, gpost, gate, ya, yb, xg, hr, w_o, w_lru, w_att, after):
    t = dh2.shape[0]
    tm = _tile(t, TM_SCAN)

    def body(dh_ref, mo_ref, gp_ref, g_ref, ya_ref, yb_ref, xg_ref, hr_ref, wo_ref, wl_ref, wa_ref,
             dmo_ref, dya_ref, dyb_ref, dgate_ref, dhr_ref, dxg_ref, do_ref, dgp_ref, dbg_ref):
        @pl.when(pl.program_id(0) == 0)
        def _():
            dgp_ref[...] = jnp.zeros_like(dgp_ref)
            dbg_ref[...] = jnp.zeros_like(dbg_ref)

        dmo, dgp = _rms_bwd(dh_ref[...], mo_ref[...], gp_ref[...])
        dgp_ref[...] += dgp
        dmob = dmo.astype(BF16)
        dmo_ref[...] = dmob
        dm = _nt(dmob, wo_ref[...])
        g0 = g_ref[:, 0:D].astype(F32)
        g1 = g_ref[:, D:2 * D].astype(F32)
        dyab = (dm * g0).astype(BF16)
        dybb = (dm * g1).astype(BF16)
        dya_ref[...] = dyab
        dyb_ref[...] = dybb
        dg0 = dm * ya_ref[...].astype(F32)
        dg1 = dm * yb_ref[...].astype(F32)
        dgate_ref[:, 0:D] = dg0.astype(BF16)
        dgate_ref[:, D:2 * D] = dg1.astype(BF16)
        dbg_ref[:, 0:D] += jnp.sum(dg0, axis=0, keepdims=True)
        dbg_ref[:, D:2 * D] += jnp.sum(dg1, axis=0, keepdims=True)
        dyain = _nt(dyab, wl_ref[...])
        do_ref[...] = _nt(dybb, wa_ref[...]).astype(BF16)
        gelu, gelu_grad = _gelu_and_grad(xg_ref[...])
        dhr_ref[...] = dyain * gelu
        dxg_ref[...] = (dyain * hr_ref[...] * gelu_grad).astype(BF16)

    b16 = jax.ShapeDtypeStruct((t, D), BF16)
    body, specs, operands = _behind(body, after)
    return pl.pallas_call(
        body, grid=(t // tm,), name="mix_bwd1",
        in_specs=specs + [_row_spec(tm, D), _row_spec(tm, D), _vec_spec(D), _row_spec(tm, 2 * D), _row_spec(tm, D),
                          _row_spec(tm, D), _row_spec(tm, D), _row_spec(tm, D), _WHOLE, _WHOLE, _WHOLE],
        out_specs=[_row_spec(tm, D), _row_spec(tm, D), _row_spec(tm, D), _row_spec(tm, 2 * D), _row_spec(tm, D),
                   _row_spec(tm, D), _row_spec(tm, D), _vec_spec(D), _vec_spec(2 * D)],
        out_shape=[b16, b16, b16, jax.ShapeDtypeStruct((t, 2 * D), BF16), jax.ShapeDtypeStruct((t, D), F32), b16, b16,
                   jax.ShapeDtypeStruct((1, D), F32), jax.ShapeDtypeStruct((1, 2 * D), F32)],
        compiler_params=_params("arbitrary"),
    )(*operands, dh2, mo, gpost, gate, ya, yb, xg, hr, w_o, w_lru, w_att)


def _rglru_bwd(dhr, hr, xc, r, ig, a, s, xr, conv_w, wa2, wx2, lam, after):
    t = dhr.shape[0]
    tm = _tile(t, TM_SCAN)
    nb8 = tm // 8
    nt = t // tm

    def body(dhr_ref, hr_ref, hrp_ref, xc_ref, r_ref, ig_ref, a_sc, s_ref, xr_ref, cw_ref, wa_ref, wx_ref, lam_ref,
             dxr_ref, dwa_ref, dwx_ref, dba_ref, dbx_ref, dlam_ref, dcw_ref, dcb_ref,
             ext_h, ext_d, g_sc, c_sc, nxt_sc):
        i = pl.program_id(0)
        first_tile = i == nt - 1

        @pl.when(i == 0)
        def _():
            c_sc[...] = jnp.zeros_like(c_sc)
            nxt_sc[...] = jnp.zeros_like(nxt_sc)
            for ref in (dwa_ref, dwx_ref, dba_ref, dbx_ref, dlam_ref, dcw_ref, dcb_ref):
                ref[...] = jnp.zeros_like(ref)

        lamv = lam_ref[...]
        sp = _softplus_neg(lamv)
        rv = r_ref[...]
        igv = ig_ref[...]
        xcv = xc_ref[...]
        a = a_sc[...]
        s = s_ref[...]

        def blk(jj, c):
            st = pl.multiple_of((nb8 - 1 - jj) * 8, 8)
            d8 = dhr_ref[pl.ds(st, 8), :]
            a8 = a_sc[pl.ds(st, 8), :]
            rows = [None] * 8
            for k in range(7, -1, -1):
                g = d8[k:k + 1, :] + c
                c = a8[k:k + 1, :] * g
                rows[k] = g
            g_sc[pl.ds(st, 8), :] = jnp.concatenate(rows, axis=0)
            return c

        c_sc[0:1, :] = lax.fori_loop(0, nb8, blk, c_sc[0:1, :])
        g = g_sc[...]
        ext_h[0:8, :] = jnp.where(first_tile, 0.0, hrp_ref[...])
        ext_h[8:8 + tm, :] = hr_ref[...]
        hprev = ext_h[pl.ds(7, tm), :]
        d_s = g * (igv * xcv)
        dig = g * s * xcv
        dxc = g * s * igv
        dla = (g * hprev) * a - d_s * ((a * a) / s)
        dr_pre = (dla * (-LRU_C * sp)) * (rv * (1.0 - rv))
        di_pre = dig * (igv * (1.0 - igv))
        dlam_ref[...] += jnp.sum(dla * (LRU_C * rv), axis=0, keepdims=True) * jax.nn.sigmoid(-lamv)
        dba_ref[...] += jnp.sum(dr_pre, axis=0, keepdims=True)
        dbx_ref[...] += jnp.sum(di_pre, axis=0, keepdims=True)
        drb = dr_pre.astype(BF16)
        dib = di_pre.astype(BF16)
        xcb = xcv.astype(BF16)
        ext_d[tm:tm + 8, :] = nxt_sc[...]
        for p in range(8):
            sl = slice(p * 128, (p + 1) * 128)
            ext_d[0:tm, sl] = dxc[:, sl] + _nt(drb[:, sl], wa_ref[p]) + _nt(dib[:, sl], wx_ref[p])
            dwa_ref[p] += _tn(xcb[:, sl], drb[:, sl])
            dwx_ref[p] += _tn(xcb[:, sl], dib[:, sl])
        dxcv = ext_d[0:tm, :]
        nxt_sc[...] = ext_d[0:8, :]
        dcb_ref[...] += jnp.sum(dxcv, axis=0, keepdims=True)
        xrv = xr_ref[...]
        dxr = jnp.zeros((tm, D), F32)
        for tap in range(4):
            ext_h[0:tm, :] = ext_d[pl.ds(3 - tap, tm), :]
            ahead = ext_h[0:tm, :]
            dxr = dxr + ahead * cw_ref[tap:tap + 1, :]
            dcw_ref[tap:tap + 1, :] += jnp.sum(ahead * xrv, axis=0, keepdims=True)
        dxr_ref[...] = dxr.astype(BF16)

    rev = pl.BlockSpec((tm, D), lambda i: (nt - 1 - i, 0))
    prev = pl.BlockSpec((8, D), lambda i: (jnp.maximum((nt - 1 - i) * nb8 - 1, 0), 0))
    full = lambda shape: pl.BlockSpec(shape, lambda i: tuple(0 for _ in shape))
    vec = jax.ShapeDtypeStruct((1, D), F32)
    blocks = jax.ShapeDtypeStruct((8, 128, 128), F32)
    body, specs, operands = _behind(body, after)
    return pl.pallas_call(
        body, grid=(nt,), name="rglru_bwd",
        in_specs=specs + [rev, rev, prev, rev, rev, rev, rev, rev, rev, full((4, D)), full((8, 128, 128)),
                          full((8, 128, 128)), _vec_spec(D)],
        out_specs=[rev, full((8, 128, 128)), full((8, 128, 128)), _vec_spec(D), _vec_spec(D), _vec_spec(D), full((4, D)),
                   _vec_spec(D)],
        out_shape=[jax.ShapeDtypeStruct((t, D), BF16), blocks, blocks, vec, vec, vec, jax.ShapeDtypeStruct((4, D), F32), vec],
        scratch_shapes=[pltpu.VMEM((tm + 8, D), F32), pltpu.VMEM((tm + 8, D), F32),
                        pltpu.VMEM((tm, D), F32), pltpu.VMEM((8, D), F32), pltpu.VMEM((8, D), F32)],
        compiler_params=_params("arbitrary"),
    )(*operands, dhr, hr, hr, xc, r, ig, a, s, xr, conv_w, wa2, wx2, lam)


def _attn_bwd(sink_rows, q, kp, vp, bias_t, mask, do):
    t = q.shape[0]
    tp = kp.shape[0]
    per_step = 16

    def body(sink_ref, q_ref, kp_ref, vp_ref, bias_ref, mask_ref, do_ref, dq_ref, dk_ref, dv_ref, dbias_ref, ds_ref):
        @pl.when(pl.program_id(0) == 0)
        def _():
            for ref in (dk_ref, dv_ref, dbias_ref, ds_ref):
                ref[...] = jnp.zeros_like(ref)

        maskv = mask_ref[...]
        lane_group = lax.broadcasted_iota(jnp.int32, (1, 4 * HEAD_DIM), 1) // HEAD_DIM

        def own_blocks(full):
            out = full[0:KP]
            for g in range(1, 4):
                out = jnp.where(lane_group == g, full[g * KP:(g + 1) * KP], out)
            return out

        dsc_sum, dsinks, dks, dvs = 0.0, [0.0] * 4, [], []
        for k in range(per_step):
            c = pl.program_id(0) * per_step + k
            chunk = slice(k * CHUNK, (k + 1) * CHUNK)
            st = pl.multiple_of(c * CHUNK, CHUNK)
            kbd = _block_diag(kp_ref[pl.ds(st, KP), :], maskv)
            vbd = _block_diag(vp_ref[pl.ds(st, KP), :], maskv)
            q_all = _stack_heads(q_ref[chunk, :])
            do_all = _stack_heads(do_ref[chunk, :])
            valid = lax.broadcasted_iota(jnp.int32, (KP, 1), 0) + c * CHUNK >= PAD_KEYS
            qk = _nt(kbd, q_all)
            dp = _nt(vbd, do_all)
            ps, dscs = [], []
            for g in range(4):
                rows = slice(g * KP, (g + 1) * KP)
                p, sink_p = _group_softmax(qk[rows], bias_ref[rows, :], sink_ref[g:g + 1, :], valid)
                delta = jnp.sum(p * dp[rows], axis=0, keepdims=True)
                ps.append(p)
                dscs.append(p * (dp[rows] - delta))
                dsinks[g] = dsinks[g] - sink_p * delta
            dsc = jnp.concatenate(dscs, axis=0)
            dsc_sum = dsc_sum + dsc
            dsb = (dsc * (HEAD_DIM ** -0.5)).astype(BF16)
            dq_ref[chunk, :] = _unstack_heads(_tn(dsb, kbd)).astype(BF16)
            dks.append((st, own_blocks(_nn(dsb, q_all))))
            dvs.append((st, own_blocks(_nn(jnp.concatenate(ps, axis=0).astype(BF16), do_all))))
        dbias_ref[...] += dsc_sum
        for g in range(4):
            ds_ref[g:g + 1, :] += dsinks[g]
        for (st, dkw), (_, dvw) in zip(dks, dvs):
            dk_ref[pl.ds(st, KP), :] += dkw
            dv_ref[pl.ds(st, KP), :] += dvw

    full = lambda shape: pl.BlockSpec(shape, lambda i: tuple(0 for _ in shape))
    return pl.pallas_call(
        body, grid=(t // (per_step * CHUNK),), name="attn_bwd",
        in_specs=[_WHOLE, _row_spec(per_step * CHUNK, D), _WHOLE, _WHOLE, _WHOLE, _WHOLE, _row_spec(per_step * CHUNK, D)],
        out_specs=[_row_spec(per_step * CHUNK, D), full((tp, KV_W)), full((tp, KV_W)), full((4 * KP, 4 * CHUNK)),
                   full((8, 4 * CHUNK))],
        out_shape=[jax.ShapeDtypeStruct((t, D), BF16), jax.ShapeDtypeStruct((tp, KV_W), F32),
                   jax.ShapeDtypeStruct((tp, KV_W), F32), jax.ShapeDtypeStruct((4 * KP, 4 * CHUNK), F32),
                   jax.ShapeDtypeStruct((8, 4 * CHUNK), F32)],
        compiler_params=_params("arbitrary"),
    )(sink_rows, q, kp, vp, bias_t, mask, do)


def _mix_bwd2(dproj, dgate, h1, dh2, gmix, w_in_g, w_gate_g, after):
    t = h1.shape[0]
    tm = _tile(t)

    def body(dp_ref, dg_ref, h_ref, dh_ref, g_ref, win_ref, wg_ref, dh1_ref, dgm_ref):
        @pl.when(pl.program_id(0) == 0)
        def _():
            dgm_ref[...] = jnp.zeros_like(dgm_ref)

        du = jnp.zeros((tm, D), F32)
        for s in range(NSH):
            du = du + _nt(dp_ref[:, s * IN_S:(s + 1) * IN_S], win_ref[s])
            du = du + _nt(dg_ref[:, s * GATE_S:(s + 1) * GATE_S], wg_ref[s])
        dxn, dg = _rms_bwd(du, h_ref[...], g_ref[...])
        dgm_ref[...] += dg
        dh1_ref[...] = dh_ref[...] + dxn

    body, specs, operands = _behind(body, after)
    return pl.pallas_call(
        body, grid=(t // tm,), name="mix_bwd2",
        in_specs=specs + [_row_spec(tm, NSH * IN_S), _row_spec(tm, 2 * D), _row_spec(tm, D), _row_spec(tm, D), _vec_spec(D),
                          _WHOLE, _WHOLE],
        out_specs=[_row_spec(tm, D), _vec_spec(D)],
        out_shape=[jax.ShapeDtypeStruct((t, D), F32), jax.ShapeDtypeStruct((1, D), F32)],
        compiler_params=_params("arbitrary"),
    )(*operands, dproj, dgate, h1, dh2, gmix, w_in_g, w_gate_g)


def _band_onehot():
    nb = N_BUCKETS // 2
    max_exact = nb // 2
    rel = jnp.arange(KB)[None, :] - PAD_KEYS - jnp.arange(CHUNK)[:, None]
    ret = jnp.where(rel > 0, nb, 0)
    n = jnp.abs(rel)
    nf = jnp.maximum(n, 1).astype(jnp.float32)
    large = max_exact + (jnp.log(nf / max_exact) / math.log(128 / max_exact) * (nb - max_exact)).astype(jnp.int32)
    large = jnp.minimum(large, nb - 1)
    buckets = (ret + jnp.where(n < max_exact, n, large)).reshape(1, CHUNK * KB)
    return (buckets == jnp.arange(N_BUCKETS)[:, None]).astype(F32)


def _pair_blocks(w):
    pairs = w.reshape(8, 2, 64, 64)
    z = jnp.zeros((8, 64, 64), w.dtype)
    return jnp.concatenate([jnp.concatenate([pairs[:, 0], z], axis=2), jnp.concatenate([z, pairs[:, 1]], axis=2)], axis=1)


def _unpair_blocks(w2):
    return jnp.stack([w2[:, 0:64, 0:64], w2[:, 64:128, 64:128]], axis=1).reshape(16, 64, 64)


def _local_step(x, target, weights, sm, reducer):
    row = lambda v: v.reshape(1, -1)
    onehot_t = _band_onehot()
    bias = _bias_fwd(sm["rel_bias"].T, onehot_t).reshape(4, 4, CHUNK, KB)
    bias_t = jnp.pad(jnp.transpose(bias, (0, 3, 1, 2)), ((0, 0), (0, KP - KB), (0, 0), (0, 0))).reshape(4 * KP, 4 * CHUNK)
    sink_rows = jnp.pad(jnp.repeat(sm["attn_sinks"].reshape(4, 4), CHUNK, axis=1), ((0, 4), (0, 0)))
    grp = jnp.arange(4 * KP)[:, None] // KP == jnp.arange(4 * HEAD_DIM)[None, :] // HEAD_DIM
    mask = (grp & (jnp.arange(4 * KP)[:, None] % KP < KB)).astype(BF16)
    wa2 = _pair_blocks(sm["rg_a_w"]).astype(BF16)
    wx2 = _pair_blocks(sm["rg_x_w"]).astype(BF16)
    wg = dict(weights("ffn1_up", [bias_t, sink_rows, mask, wa2, wx2]))
    sm = dict(sm, conv_w=wg["conv_w"])

    n1, a1, b1, hm1 = _ffn_up(x, row(sm["ffn1_pre_g"]), wg["ffn1_w1"], wg["ffn1_w3"], "ffn1_up")
    wg.update(weights("ffn1_down", hm1))
    h1, f1 = _ffn_down(x, hm1, wg["ffn1_w2"], row(sm["ffn1_post_g"]), "ffn1_down")
    wg.update(weights("mix_in", h1))
    u, q, k, v, xr, xg, gate = _mix_proj(h1, row(sm["mix_pre_g"]), wg["w_in"], wg["w_gate"], row(sm["b_gate"]))
    token = weights("mix_out", u, begin=True)
    hr, yain, xc, r, ig, lru_a, lru_s = _rglru_fwd(xr, xg, sm["conv_w"], row(sm["conv_b"]), wa2, row(sm["rg_a_b"]), wx2,
                                                   row(sm["rg_x_b"]), row(sm["lru_lambda"]), token)
    token = weights("ffn2", hr, begin=True)
    kp = jnp.pad(k, ((PAD_KEYS, KP - KB), (0, 0)))
    vp = jnp.pad(v, ((PAD_KEYS, KP - KB), (0, 0)))
    o = _attn_fwd(sink_rows, q, kp, vp, bias_t, mask, token)
    wg.update(weights("mix_out", o))
    w_lru = wg["w_lru_out"].reshape(D, D)
    w_att = wg["w_attn_out"].reshape(D, D)
    w_o = wg["w_o"].reshape(D, D)
    wg.update(weights("ffn2", o))
    h2, mo, merged, ya, yb = _merge_fwd(yain, o, gate, h1, w_lru, w_att, w_o, row(sm["mix_post_g"]))
    dy, a2, b2, hm2, f2, sq = _ffn_fwd(h2, row(sm["ffn2_pre_g"]), wg["ffn2_w1"], wg["ffn2_w3"], wg["ffn2_w2"],
                                       row(sm["ffn2_post_g"]), "ffn2_fwd", target)

    big, small = {}, {}
    dh2, n2, da2, db2, df2, small["ffn2_pre_g"], small["ffn2_post_g"] = _ffn_bwd(
        dy, h2, f2, a2, b2, row(sm["ffn2_pre_g"]), row(sm["ffn2_post_g"]), wg["ffn2_w1"], wg["ffn2_w3"], wg["ffn2_w2"],
        "ffn2_bwd")
    big["ffn2_w1"] = _wgrad_rows(da2, n2, "dw_ffn2_w1")
    big["ffn2_w3"] = _wgrad_rows(db2, n2, "dw_ffn2_w3")
    big["ffn2_w2"] = _wgrad_rows(hm2, df2, "dw_ffn2_w2")
    token = reducer.begin("ffn2", {n: big[n] for n in ("ffn2_w1", "ffn2_w3", "ffn2_w2")})
    dmo, dya, dyb, dgate, dhr, dxg, do, small["mix_post_g"], small["b_gate"] = _mix_bwd1(
        dh2, mo, row(sm["mix_post_g"]), gate, ya, yb, xg, hr, w_o, w_lru, w_att, token)
    big["w_o"] = _wgrad_sq(merged, dmo, "dw_w_o").reshape(NSH, D // NSH, D)
    big["w_lru_out"] = _wgrad_sq(yain, dya, "dw_w_lru_out").reshape(NSH, D // NSH, D)
    big["w_attn_out"] = _wgrad_sq(o, dyb, "dw_w_attn_out").reshape(NSH, D // NSH, D)
    token = reducer.advance("ffn2", big["w_attn_out"])
    (dxr, dwa2, dwx2, small["rg_a_b"], small["rg_x_b"], small["lru_lambda"], small["conv_w"], small["conv_b"]) = _rglru_bwd(
        dhr, hr, xc, r, ig, lru_a, lru_s, xr, sm["conv_w"], wa2, wx2, row(sm["lru_lambda"]), token)
    small["rg_a_w"] = _unpair_blocks(dwa2)
    small["rg_x_w"] = _unpair_blocks(dwx2)
    dq, dkp, dvp, dbias_t, ds_rows = _attn_bwd(sink_rows, q, kp, vp, bias_t, mask, do)
    dbias = jnp.transpose(dbias_t.reshape(4, KP, 4, CHUNK)[:, :KB], (0, 2, 3, 1)).reshape(N_HEADS, CHUNK * KB)
    drel_t, dsinks = _bias_bwd(dbias, onehot_t, ds_rows)
    small["attn_sinks"] = dsinks[0:4, 0:4].reshape(N_HEADS)
    small["rel_bias"] = drel_t.T
    t = x.shape[0]
    dproj = jnp.concatenate([dq, dkp[PAD_KEYS:PAD_KEYS + t].astype(BF16), dvp[PAD_KEYS:PAD_KEYS + t].astype(BF16), dxr, dxg],
                            axis=1)
    big["w_in"] = _wgrad_cols(u, dproj, IN_S, "dw_w_in")
    big["w_gate"] = _wgrad_cols(u, dgate, GATE_S, "dw_w_gate")
    token = reducer.begin("mix", {n: big[n] for n in ("w_in", "w_gate", "w_lru_out", "w_attn_out", "w_o")})
    dh1, small["mix_pre_g"] = _mix_bwd2(dproj, dgate, h1, dh2, row(sm["mix_pre_g"]), wg["w_in"], wg["w_gate"], token)
    da1, db1, df1, small["ffn1_post_g"] = _ffn_bwd_acts(dh1, f1, a1, b1, row(sm["ffn1_post_g"]), wg["ffn1_w2"],
                                                        "ffn1_bwd_acts")
    token = reducer.advance("mix", df1)
    big["ffn1_w1"] = _wgrad_rows(da1, n1, "dw_ffn1_w1", token)
    big["ffn1_w3"] = _wgrad_rows(db1, n1, "dw_ffn1_w3", token)
    big["ffn1_w2"] = _wgrad_rows(hm1, df1, "dw_ffn1_w2", token)
    token = reducer.begin("ffn1", {n: big[n] for n in ("ffn1_w1", "ffn1_w3", "ffn1_w2")})
    dx, small["ffn1_pre_g"] = _ffn_bwd_input(dh1, x, da1, db1, row(sm["ffn1_pre_g"]), wg["ffn1_w1"], wg["ffn1_w3"],
                                             "ffn1_bwd_input", token)
    return sq, dx, big, small


_ANY = pl.BlockSpec(memory_space=pl.ANY)


def _place():
    return lax.axis_index("x"), lax.axis_index("y"), lax.axis_index("c")


def _other_chips(x, y):
    return [(1 - x, y), (x, 1 - y), (1 - x, 1 - y)]


_HBM = pl.BlockSpec(memory_space=pltpu.HBM)
_SEM = pl.BlockSpec(memory_space=pltpu.SEMAPHORE)
_EFFECT = pltpu.SideEffectType.DATAFLOW_SIDE_EFFECTING


def _cast_into_slot(w, chip, name, after=None):
    r, cc = w.shape
    rows = r // 4

    def body(chip_ref, *refs):
        w_ref, o_ref = refs[-2:]
        o_ref[...] = w_ref[...].astype(BF16)

    extra = [] if after is None else [after]
    return pl.pallas_call(
        body, name=name, out_shape=jax.ShapeDtypeStruct((NSH, r, cc), BF16),
        grid_spec=pltpu.PrefetchScalarGridSpec(
            num_scalar_prefetch=1, grid=(4,), in_specs=[_ANY] * len(extra) + [pl.BlockSpec((rows, cc), lambda i, chip: (i, 0))],
            out_specs=pl.BlockSpec((None, rows, cc), lambda i, chip: (chip[0], i, 0))),
        compiler_params=_params("arbitrary"))(chip, *extra, w)


def _piece(ref, slot, c):
    if ref.dtype == F32:
        return ref.at[slot]
    rh = ref.shape[1] // 2
    return ref.at[slot, pl.ds(pl.multiple_of(c * rh, 16), rh), :]


def _gather_start(stages, name):
    flat = [b for stage in stages for b in stage]
    n, ns = len(flat), len(stages)

    def body(*refs):
        ins, sems, token = refs[:n], refs[n:n + 2 * ns], refs[-1]
        x, y, c = _place()
        me = 2 * x + y
        k = 0
        for s, stage in enumerate(stages):
            for i in range(len(stage)):
                for j, (px, py) in enumerate(_other_chips(x, y)):
                    piece = _piece(ins[k], me, c)
                    pltpu.make_async_remote_copy(src_ref=piece, dst_ref=piece, send_sem=sems[2 * s].at[3 * i + j],
                                                 recv_sem=sems[2 * s + 1].at[3 * i + j], device_id=(px, py, c),
                                                 device_id_type=MESH).start()
                k += 1
        token[...] = jnp.zeros_like(token)

    sem_shapes = [pltpu.SemaphoreType.DMA((3 * len(stage),)) for stage in stages for _ in range(2)]
    outs = pl.pallas_call(
        body, name=name, in_specs=[_HBM] * n,
        out_specs=[_SEM] * (2 * ns) + [_HBM] * n + [pl.BlockSpec(memory_space=pltpu.VMEM)],
        out_shape=sem_shapes + [pltpu.HBM(b.shape, b.dtype) for b in flat] + [jax.ShapeDtypeStruct((8, 128), F32)],
        input_output_aliases={i: 2 * ns + i for i in range(n)},
        compiler_params=pltpu.CompilerParams(has_side_effects=_EFFECT),
    )(*[pltpu.with_memory_space_constraint(b, pltpu.HBM) for b in flat])
    sems, bufs, token = outs[:2 * ns], list(outs[2 * ns:2 * ns + n]), outs[-1]
    per_stage, k = [], 0
    for s, stage in enumerate(stages):
        per_stage.append((sems[2 * s], sems[2 * s + 1], bufs[k:k + len(stage)]))
        k += len(stage)
    return per_stage, token


def _gather_wait(send_sems, recv_sems, bufs, after, name):
    n = len(bufs)

    def body(*refs):
        ins, ssem, rsem = refs[:n], refs[n], refs[n + 1]
        x, y, c = _place()
        me = 2 * x + y
        for i in range(n):
            for j, (px, py) in enumerate(_other_chips(x, y)):
                cp = pltpu.make_async_remote_copy(src_ref=_piece(ins[i], me, c), dst_ref=_piece(ins[i], 2 * px + py, c),
                                                  send_sem=ssem.at[3 * i + j], recv_sem=rsem.at[3 * i + j],
                                                  device_id=(px, py, c), device_id_type=MESH)
                cp.wait_send()
                cp.wait_recv()

    afters = list(after) if isinstance(after, (list, tuple)) else [after]
    return pl.pallas_call(
        body, name=name, in_specs=[_HBM] * n + [_SEM, _SEM] + [_ANY] * len(afters), out_specs=[_HBM] * n,
        out_shape=[pltpu.HBM(b.shape, b.dtype) for b in bufs], input_output_aliases={i: i for i in range(n)},
        compiler_params=pltpu.CompilerParams(has_side_effects=_EFFECT),
    )(*bufs, send_sems, recv_sems, *afters)


def _sibling_fill(bufs, name):
    n = len(bufs)

    def body(*refs):
        ins, outs = refs[:n], refs[n:2 * n]
        send_sems, recv_sems = refs[2 * n:]
        x, y, c = _place()
        copies = []
        for i in range(n):
            for j, (px, py) in enumerate(_other_chips(x, y)):
                copies.append(pltpu.make_async_remote_copy(
                    src_ref=_piece(ins[i], 2 * px + py, c), dst_ref=_piece(outs[i], 2 * px + py, c),
                    send_sem=send_sems.at[3 * i + j], recv_sem=recv_sems.at[3 * i + j], device_id=(x, y, 1 - c),
                    device_id_type=MESH))
                copies[-1].start()
        for cp in copies:
            cp.wait()

    return pl.pallas_call(
        body, name=name, in_specs=[_ANY] * n, out_specs=[_ANY] * n,
        out_shape=[jax.ShapeDtypeStruct(b.shape, b.dtype) for b in bufs], input_output_aliases={i: i for i in range(n)},
        scratch_shapes=[pltpu.SemaphoreType.DMA((3 * n,)), pltpu.SemaphoreType.DMA((3 * n,))],
        compiler_params=pltpu.CompilerParams(has_side_effects=True),
    )(*bufs)


def _swap_plan(srcs, lands):
    x, y, c = _place()
    plan = []
    for src, land in zip(srcs, lands):
        rh = src.shape[1] // 2
        plan.append((src.at[:, pl.ds(pl.multiple_of((1 - c) * rh, 16), rh), :], land, (x, y, 1 - c)))
    return plan


def _owners_plan(srcs, lands):
    x, y, c = _place()
    return [(src.at[2 * px + py], land.at[j], (px, py, c))
            for src, land in zip(srcs, lands) for j, (px, py) in enumerate(_other_chips(x, y))]


def _exchange_start(srcs, lands, plan, copies, name):
    n, m = len(srcs), len(srcs) + len(lands)

    def body(*refs):
        send_sems, recv_sems, token = refs[m], refs[m + 1], refs[-1]
        for k, (src, dst, dev) in enumerate(plan(refs[:n], refs[n:m])):
            pltpu.make_async_remote_copy(src_ref=src, dst_ref=dst, send_sem=send_sems.at[k], recv_sem=recv_sems.at[k],
                                         device_id=dev, device_id_type=MESH).start()
        token[...] = jnp.zeros_like(token)

    both = list(srcs) + list(lands)
    outs = pl.pallas_call(
        body, name=name, in_specs=[_HBM] * m,
        out_specs=[_SEM, _SEM] + [_HBM] * m + [pl.BlockSpec(memory_space=pltpu.VMEM)],
        out_shape=[pltpu.SemaphoreType.DMA((copies,)), pltpu.SemaphoreType.DMA((copies,))]
        + [pltpu.HBM(b.shape, b.dtype) for b in both] + [jax.ShapeDtypeStruct((8, 128), F32)],
        input_output_aliases={i: 2 + i for i in range(m)},
        compiler_params=pltpu.CompilerParams(has_side_effects=_EFFECT),
    )(*[pltpu.with_memory_space_constraint(b, pltpu.HBM) for b in both])
    return (outs[0], outs[1]), list(outs[2:2 + n]), list(outs[2 + n:2 + m]), outs[-1]


def _exchange_wait(sems, srcs, lands, plan, after, name):
    n, m = len(srcs), len(srcs) + len(lands)

    def body(*refs):
        send_sems, recv_sems = refs[m], refs[m + 1]
        for k, (src, dst, dev) in enumerate(plan(refs[:n], refs[n:m])):
            cp = pltpu.make_async_remote_copy(src_ref=src, dst_ref=dst, send_sem=send_sems.at[k], recv_sem=recv_sems.at[k],
                                              device_id=dev, device_id_type=MESH)
            cp.wait_send()
            cp.wait_recv()

    both = list(srcs) + list(lands)
    afters = list(after) if isinstance(after, (list, tuple)) else [after]
    outs = pl.pallas_call(
        body, name=name, in_specs=[_HBM] * m + [_SEM, _SEM] + [_ANY] * len(afters), out_specs=[_HBM] * m,
        out_shape=[pltpu.HBM(b.shape, b.dtype) for b in both], input_output_aliases={i: i for i in range(m)},
        compiler_params=pltpu.CompilerParams(has_side_effects=_EFFECT),
    )(*both, sems[0], sems[1], *afters)
    return list(outs[:n]), list(outs[n:])


def _fill_plan(bufs, _):
    x, y, c = _place()
    return [(_piece(buf, 2 * px + py, c), _piece(buf, 2 * px + py, c), (x, y, 1 - c))
            for buf in bufs for px, py in _other_chips(x, y)]


class _Reducer:
    def __init__(self, where):
        self.state = {}
        self.where = where

    def begin(self, stage, grads):
        names = list(grads)
        full = [grads[n] for n in names]
        lands = [lax.empty((NSH, g.shape[1] // 2, g.shape[2]), g.dtype) for g in full]
        sems, full, lands, token = _exchange_start(full, lands, _swap_plan, len(full), "swap_start_" + stage)
        self.state[stage] = (names, sems, full, lands)
        return token

    def advance(self, stage, after):
        names, sems, full, lands = self.state[stage]
        full, got = _exchange_wait(sems, full, lands, _swap_plan, after, "swap_wait_" + stage)
        sums, own = _chip_sums(full, got, self.where, "chip_sums_" + stage)
        lands = [lax.empty((3,) + s.shape[1:], BF16) for s in sums]
        sems, sent, lands, token = _exchange_start(sums, lands, _owners_plan, 3 * len(sums), "owners_start_" + stage)
        self.state[stage] = (names, own, sems, sent, lands)
        return token

    def finish(self, stage, after):
        names, own, sems, sent, lands = self.state[stage]
        _, got = _exchange_wait(sems, sent, lands, _owners_plan, after, "owners_wait_" + stage)
        return dict(zip(names, _owner_sums(own, got, "owner_sums_" + stage)))


def _chip_sums(gs, gots, where, name):
    n = len(gs)

    def body(where_ref, *refs):
        g_refs, got_refs, hb_refs, own_refs = (refs[k * n:(k + 1) * n] for k in range(4))
        mine = pl.program_id(0) == where_ref[1]
        for g_ref, got_ref, hb_ref, own_ref in zip(g_refs, got_refs, hb_refs, own_refs):
            h = g_ref[...].astype(F32) + got_ref[...].astype(F32)
            hb_ref[...] = h.astype(BF16)

            @pl.when(mine)
            def _():
                own_ref[...] = h

    halves = [(g.shape[1] // 2, g.shape[2]) for g in gs]
    slot = [pl.BlockSpec((None, rh, cc), lambda s, where: (s, 0, 0)) for rh, cc in halves]
    outs = pl.pallas_call(
        body, name=name,
        grid_spec=pltpu.PrefetchScalarGridSpec(
            num_scalar_prefetch=1, grid=(NSH,),
            in_specs=[pl.BlockSpec((None, rh, cc), lambda s, where: (s, where[0], 0)) for rh, cc in halves] + slot,
            out_specs=slot + [pl.BlockSpec((rh, cc), lambda s, where: (0, 0)) for rh, cc in halves]),
        out_shape=[jax.ShapeDtypeStruct((NSH, rh, cc), BF16) for rh, cc in halves]
        + [jax.ShapeDtypeStruct((rh, cc), F32) for rh, cc in halves],
        compiler_params=_params("arbitrary"),
    )(where, *gs, *gots)
    return list(outs[:n]), list(outs[n:])


def _owner_sums(owns, gots, name):
    n = len(owns)

    def body(*refs):
        own_refs, got_refs, o_refs = (refs[k * n:(k + 1) * n] for k in range(3))
        for own_ref, got_ref, o_ref in zip(own_refs, got_refs, o_refs):
            o_ref[...] = ((own_ref[...] + got_ref[0].astype(F32)) + got_ref[1].astype(F32)) + got_ref[2].astype(F32)

    blocks = [(o.shape[0] // 2, o.shape[1]) for o in owns]
    rows = [pl.BlockSpec(b, lambda i: (i, 0)) for b in blocks]
    return pl.pallas_call(
        body, grid=(2,), name=name,
        in_specs=rows + [pl.BlockSpec((3,) + b, lambda i: (0, i, 0)) for b in blocks], out_specs=rows,
        out_shape=[jax.ShapeDtypeStruct(o.shape, F32) for o in owns], compiler_params=_params("arbitrary"),
    )(*owns, *gots)


def _sibling_plan(srcs, lands):
    x, y, c = _place()
    return [(src, land, (x, y, 1 - c)) for src, land in zip(srcs, lands)]


def _all_reduce_small(part):
    def body(p_ref, o_ref, rbuf, send1, recv1, send2, recv2):
        x, y, c = _place()
        me = 4 * x + 2 * y + c
        peers = []
        for k in range(1, 8):
            px, py, pc = x ^ ((k >> 2) & 1), y ^ ((k >> 1) & 1), c ^ (k & 1)
            peers.append((k, (px, py, pc), 4 * px + 2 * py + pc))

        def rows(d):
            return pl.ds(pl.multiple_of(d * SMALL_SLICE, 8), SMALL_SLICE)

        first = [pltpu.make_async_remote_copy(src_ref=p_ref.at[rows(idx), :], dst_ref=rbuf.at[me], send_sem=send1.at[k],
                                              recv_sem=recv1.at[k], device_id=dev, device_id_type=MESH)
                 for k, dev, idx in peers]
        for cp in first:
            cp.start()
        rbuf[me] = p_ref[rows(me), :]
        for k, dev, idx in peers:
            pltpu.make_async_remote_copy(src_ref=p_ref.at[rows(idx), :], dst_ref=rbuf.at[idx], send_sem=send1.at[k],
                                         recv_sem=recv1.at[k], device_id=dev, device_id_type=MESH).wait_recv()
        acc = rbuf[0]
        for d in range(1, 8):
            acc = acc + rbuf[d]
        o_ref[rows(me), :] = acc
        second = [pltpu.make_async_remote_copy(src_ref=o_ref.at[rows(me), :], dst_ref=o_ref.at[rows(me), :],
                                               send_sem=send2.at[k], recv_sem=recv2.at[k], device_id=dev, device_id_type=MESH)
                  for k, dev, idx in peers]
        for cp in second:
            cp.start()
        for k, dev, idx in peers:
            pltpu.make_async_remote_copy(src_ref=o_ref.at[rows(me), :], dst_ref=o_ref.at[rows(idx), :], send_sem=send2.at[k],
                                         recv_sem=recv2.at[k], device_id=dev, device_id_type=MESH).wait_recv()
        for cp in first + second:
            cp.wait_send()

    return pl.pallas_call(
        body, name="all_reduce_small", in_specs=[_WHOLE], out_specs=_WHOLE,
        out_shape=jax.ShapeDtypeStruct((SMALL_ROWS, 128), F32),
        scratch_shapes=[pltpu.VMEM((8, SMALL_SLICE, 128), F32)] + [pltpu.SemaphoreType.DMA((8,))] * 4,
        compiler_params=pltpu.CompilerParams(has_side_effects=True),
    )(part)


def _adamw_update(w, gv, m, v):
    nm = ADAM_B1 * m + (1.0 - ADAM_B1) * gv
    nv = ADAM_B2 * v + (1.0 - ADAM_B2) * (gv * gv)
    m_hat = nm / (1.0 - ADAM_B1 ** ADAM_STEP)
    v_hat = nv / (1.0 - ADAM_B2 ** ADAM_STEP)
    return -ADAM_LR * (m_hat / (jnp.sqrt(v_hat) + ADAM_EPS) + ADAM_WD * w), nm, nv


def _adamw_small(ws, gs, ms, vs, after):
    n = len(ws)

    def body(*refs):
        w_refs, g_refs, m_refs, v_refs, d_refs, nm_refs, nv_refs = (refs[k * n:(k + 1) * n] for k in range(7))
        for i in range(n):
            d_refs[i][...], nm_refs[i][...], nv_refs[i][...] = _adamw_update(
                w_refs[i][...], g_refs[i][...], m_refs[i][...], v_refs[i][...])

    out = [jax.ShapeDtypeStruct(w.shape, F32) for w in ws]
    body, specs, operands = _behind(body, after)
    outs = pl.pallas_call(body, in_specs=specs + [_WHOLE] * (4 * n), out_specs=[_WHOLE] * (3 * n), out_shape=out * 3,
                          name="adamw_small", compiler_params=_params())(*operands, *ws, *gs, *ms, *vs)
    return outs[:n], outs[n:2 * n], outs[2 * n:]


def _adamw_halves(ws, mines, theirs, ms, vs, name):
    n = len(ws)
    steps = 2

    def body(*refs):
        w_refs, mine_refs, theirs_refs, m_refs, v_refs, g_refs, d_refs, nm_refs, nv_refs = (
            refs[k * n:(k + 1) * n] for k in range(9))
        is_mine = pl.program_id(0) == lax.axis_index("c")
        for i in range(n):
            gv = jnp.where(is_mine, mine_refs[i][...], theirs_refs[i][...])
            g_refs[i][...] = gv
            d_refs[i][...], nm_refs[i][...], nv_refs[i][...] = _adamw_update(w_refs[i][...], gv, m_refs[i][...], v_refs[i][...])

    blocks = [(h.shape[0] // steps, h.shape[1]) for h in mines]
    whole = [pl.BlockSpec(b, lambda h, i: (steps * h + i, 0)) for b in blocks]
    half = [pl.BlockSpec(b, lambda h, i: (i, 0)) for b in blocks]
    out = [jax.ShapeDtypeStruct(w.shape, F32) for w in ws]
    outs = pl.pallas_call(body, grid=(2, steps), in_specs=whole + half + half + whole + whole, out_specs=whole * 4,
                          out_shape=out * 4, name=name, compiler_params=_params("arbitrary", "arbitrary"),
                          )(*ws, *mines, *theirs, *ms, *vs)
    return [tuple(outs[k * n + i] for k in range(4)) for i in range(n)]


SMALL_USED = sum(size for _, size in SMALL) // 128


def _pack_small(vals, tail=None):
    parts = []
    for name, size in SMALL:
        flat = vals[name].reshape(-1).astype(F32)
        parts.append(jnp.pad(flat, (0, size - flat.shape[0])))
    if tail is not None:
        parts.append(tail.reshape(128))
    flat = jnp.concatenate(parts)
    return jnp.pad(flat, (0, SMALL_ROWS * 128 - flat.shape[0])).reshape(SMALL_ROWS, 128)


def _unpack_small(packed, shapes):
    flat = packed.reshape(-1)
    out, off = {}, 0
    for name, size in SMALL:
        n = math.prod(shapes[name])
        out[name] = flat[off:off + n].reshape(shapes[name])
        off += size
    return out


def kernel(x, ffn1_pre_g, ffn1_w1, ffn1_w3, ffn1_w2, ffn1_post_g, mix_pre_g, w_in, conv_w, conv_b, rg_a_w, rg_a_b, rg_x_w, rg_x_b, lru_lambda, w_lru_out, attn_sinks, rel_bias, w_attn_out, w_gate, b_gate, w_o, mix_post_g, ffn2_pre_g, ffn2_w1, ffn2_w3, ffn2_w2, ffn2_post_g, loss_target, m_ffn1_pre_g, m_ffn1_w1, m_ffn1_w3, m_ffn1_w2, m_ffn1_post_g, m_mix_pre_g, m_w_in, m_conv_w, m_conv_b, m_rg_a_w, m_rg_a_b, m_rg_x_w, m_rg_x_b, m_lru_lambda, m_w_lru_out, m_attn_sinks, m_rel_bias, m_w_attn_out, m_w_gate, m_b_gate, m_w_o, m_mix_post_g, m_ffn2_pre_g, m_ffn2_w1, m_ffn2_w3, m_ffn2_w2, m_ffn2_post_g, v_ffn1_pre_g, v_ffn1_w1, v_ffn1_w3, v_ffn1_w2, v_ffn1_post_g, v_mix_pre_g, v_w_in, v_conv_w, v_conv_b, v_rg_a_w, v_rg_a_b, v_rg_x_w, v_rg_x_b, v_lru_lambda, v_w_lru_out, v_attn_sinks, v_rel_bias, v_w_attn_out, v_w_gate, v_b_gate, v_w_o, v_mix_post_g, v_ffn2_pre_g, v_ffn2_w1, v_ffn2_w3, v_ffn2_w2, v_ffn2_post_g):
    given = dict(locals())
    chip = 2 * lax.axis_index("x") + lax.axis_index("y")
    transposed = ("ffn1_w1", "ffn1_w3", "ffn2_w1", "ffn2_w3")

    def shard(name, moment=""):
        w = given[moment + name][0]
        return w.T if name in transposed else w

    def unshard(name, w):
        return (w.T if name in transposed else w)[None]

    def only_my_columns(a):
        parts = a.reshape(1, 4, NSH, D // NSH)
        return sum(jnp.where(chip == s, parts[:, :, s], 0.0) for s in range(NSH))

    chip_arr = jnp.reshape(chip, (1,)).astype(jnp.int32)
    stage_names = {"ffn1_up": ["ffn1_w1", "ffn1_w3", "conv_w"],
                   "ffn1_down": ["ffn1_w2"],
                   "mix_in": ["w_in", "w_gate"],
                   "mix_out": ["w_lru_out", "w_attn_out", "w_o"],
                   "ffn2": ["ffn2_w1", "ffn2_w3", "ffn2_w2"]}
    in_flight, started = {}, None
    for stage, names in stage_names.items():
        bufs = [jnp.where(lax.broadcasted_iota(jnp.int32, (NSH, 4, D // NSH), 0) == chip, given[n], 0.0) if n == "conv_w"
                else _cast_into_slot(shard(n), chip_arr, "cast_" + n, started) for n in names]
        (in_flight[stage],), started = _gather_start([bufs], "gather_start_" + stage)
    all_started = started

    filling = {}

    def weights(stage, after, begin=False):
        names = stage_names[stage]
        halves_of = [n for n in names if n != "conv_w"]
        if stage in filling:
            filled, _ = _exchange_wait(filling.pop(stage), *filling.pop(stage + "/bufs"), _fill_plan, after,
                                       "fill_wait_" + stage)
            return dict(zip(halves_of, filled))
        send_sems, recv_sems, landing = in_flight[stage]
        if stage == "ffn1_up":
            after = [all_started] + list(after)
        landed = dict(zip(names, _gather_wait(send_sems, recv_sems, landing, after, "gather_wait_" + stage)))
        halves = [landed[n] for n in halves_of]
        if begin:
            filling[stage], bufs, _, token = _exchange_start(halves, [], _fill_plan, 3 * len(halves), "fill_start_" + stage)
            filling[stage + "/bufs"] = (bufs, [])
            return token
        out = dict(zip(halves_of, _sibling_fill(halves, "sibling_fill_" + stage)))
        if "conv_w" in names:
            out["conv_w"] = jnp.transpose(landed["conv_w"], (1, 0, 2)).reshape(4, D)
        return out

    small_shapes = {n: given[n].shape for n, _ in SMALL}
    small_shapes["conv_w"] = (1, 4, D)
    sm = {n: (given[n][0] if given[n].shape[0] == 1 and n != "rel_bias" else given[n]) for n, _ in SMALL if n != "conv_w"}

    reducer = _Reducer(jnp.stack([lax.axis_index("c"), chip]).astype(jnp.int32))
    sq, dx, _, small = _local_step(x[0], loss_target[0], weights, sm, reducer)

    reduced_small = _all_reduce_small(_pack_small(small, tail=sq))
    last_started = reducer.advance("ffn1", [dx, reduced_small])
    loss = reduced_small[SMALL_USED, 0] * (0.5 / D)
    small_g = _unpack_small(reduced_small, small_shapes)
    grads, delta, new_m, new_v = {}, {}, {}, {}
    in_transit = {}

    def send(stage, after):
        halves = reducer.finish(stage, after)
        lands = [lax.empty(h.shape, F32) for h in halves.values()]
        sems, mine, lands, token = _exchange_start(list(halves.values()), lands, _sibling_plan, len(lands),
                                                   "halves_start_" + stage)
        in_transit[stage] = (list(halves), sems, mine, lands)
        return token

    def update(stage, after):
        names, sems, mine, lands = in_transit[stage]
        mine, theirs = _exchange_wait(sems, mine, lands, _sibling_plan, after, "halves_wait_" + stage)
        updated = _adamw_halves([shard(n) for n in names], mine, theirs, [shard(n, "m_") for n in names],
                                [shard(n, "v_") for n in names], "adamw_" + stage)
        for n, results in zip(names, updated):
            grads[n], delta[n], new_m[n], new_v[n] = (unshard(n, r) for r in results)
        return new_v[names[-1]]

    token = send("ffn2", [reduced_small, last_started])
    token = send("mix", token)
    done = update("ffn2", token)
    done = update("mix", done)
    token = send("ffn1", done)
    update("ffn1", token)

    small_g["conv_w"] = only_my_columns(small_g["conv_w"])
    names = [n for n, _ in SMALL]
    flat2d = lambda a: a.reshape(-1, a.shape[-1])
    outs = _adamw_small(*[[flat2d(given[pre + n]) if pre != "g" else flat2d(small_g[n]) for n in names]
                          for pre in ("", "g", "m_", "v_")], after=last_started)
    for dst, arrs in zip((delta, new_m, new_v), outs):
        dst.update({n: a.reshape(given[n].shape) for n, a in zip(names, arrs)})
    grads.update(small_g)
    return (loss, dx[None], *[grads[n] for n in WEIGHTS], *[delta[n] for n in WEIGHTS], *[new_m[n] for n in WEIGHTS],
            *[new_v[n] for n in WEIGHTS])
```

```python
import math

import jax
import jax.numpy as jnp
from jax import lax
from jax.experimental import pallas as pl
from jax.experimental.pallas import tpu as pltpu

F32, BF16 = jnp.float32, jnp.bfloat16
D = 1024
NSH = 4
FF_S = 704
IN_S = 896
GATE_S = 512
KV_W = 256
CHUNK = 64
KB = 192
N_HEADS = 16
HEAD_DIM = 64
N_BUCKETS = 32
KP = 192
PAD_KEYS = 128
RMS_EPS = 1e-6
NEG_INF = -1e30
LRU_C = 8.0
TM = 512
TM_SCAN = 256
VMEM_LIMIT = 56 * 1024 * 1024
ADAM_LR, ADAM_B1, ADAM_B2, ADAM_EPS, ADAM_WD, ADAM_STEP = 0.001, 0.9, 0.999, 1e-08, 0.01, 10
SMALL_ROWS = 1216
SMALL_SLICE = SMALL_ROWS // 8
MESH = pl.DeviceIdType.MESH

BIG = ["ffn1_w1", "ffn1_w3", "ffn1_w2", "w_in", "w_lru_out", "w_attn_out", "w_gate", "w_o", "ffn2_w1", "ffn2_w3", "ffn2_w2"]
SMALL = [("ffn1_pre_g", 1024), ("ffn1_post_g", 1024), ("mix_pre_g", 1024), ("conv_w", 4096), ("conv_b", 1024),
         ("rg_a_w", 65536), ("rg_a_b", 1024), ("rg_x_w", 65536), ("rg_x_b", 1024), ("lru_lambda", 1024),
         ("attn_sinks", 1024), ("rel_bias", 1024), ("b_gate", 2048), ("mix_post_g", 1024), ("ffn2_pre_g", 1024),
         ("ffn2_post_g", 1024)]
WEIGHTS = ["ffn1_pre_g", "ffn1_w1", "ffn1_w3", "ffn1_w2", "ffn1_post_g", "mix_pre_g", "w_in", "conv_w", "conv_b", "rg_a_w",
           "rg_a_b", "rg_x_w", "rg_x_b", "lru_lambda", "w_lru_out", "attn_sinks", "rel_bias", "w_attn_out", "w_gate", "b_gate",
           "w_o", "mix_post_g", "ffn2_pre_g", "ffn2_w1", "ffn2_w3", "ffn2_w2", "ffn2_post_g"]


def _params(*sem):
    return pltpu.CompilerParams(dimension_semantics=sem or None, vmem_limit_bytes=VMEM_LIMIT)


def _nn(a, b):
    return jnp.dot(a, b, preferred_element_type=F32)


def _nt(a, b):
    return lax.dot_general(a, b, (((1,), (1,)), ((), ())), preferred_element_type=F32)


def _tn(a, b):
    return lax.dot_general(a, b, (((0,), (0,)), ((), ())), preferred_element_type=F32)


def _rms(x, g):
    rstd = lax.rsqrt(jnp.mean(x * x, axis=-1, keepdims=True) + RMS_EPS)
    return (x * rstd) * g


def _rms_bwd(dout, x, g):
    rstd = lax.rsqrt(jnp.mean(x * x, axis=-1, keepdims=True) + RMS_EPS)
    xhat = x * rstd
    dg = jnp.sum(dout * xhat, axis=0, keepdims=True)
    dxhat = dout * g
    dx = rstd * (dxhat - xhat * jnp.mean(dxhat * xhat, axis=-1, keepdims=True))
    return dx, dg


_GELU_K = math.sqrt(2.0 / math.pi)


_GELU_C = 0.044715 * _GELU_K


def _gelu_and_grad(x):
    x2 = x * x
    t = jnp.tanh(x * (_GELU_K + _GELU_C * x2))
    cdf = 0.5 + 0.5 * t
    return x * cdf, cdf + (x * (_GELU_K + (3.0 * _GELU_C) * x2)) * (0.5 - 0.5 * (t * t))


def _softplus_neg(lam):
    z = -lam
    u = jnp.exp(-jnp.abs(z))
    w = 1.0 + u
    log1p_u = jnp.where(w == 1.0, u, jnp.log(w) * (u / (w - 1.0)))
    return jnp.maximum(z, 0.0) + log1p_u


def _lru_coeffs(r, sp):
    log_a = (-LRU_C * r) * sp
    a = jnp.exp(log_a)
    t = jnp.tanh(log_a)
    s = jnp.sqrt(-2.0 * t / (1.0 - t))
    return a, s


def _row_spec(tm, width):
    return pl.BlockSpec((tm, width), lambda i: (i, 0))


def _vec_spec(width):
    return pl.BlockSpec((1, width), lambda i: (0, 0))


_WHOLE = pl.BlockSpec(memory_space=pltpu.VMEM)


def _tile(t, tm=TM):
    return min(tm, t)


def _ffn_fwd(x, gpre, w1g, w3g, w2g, gpost, name, target=None):
    t = x.shape[0]
    tm = _tile(t)
    last = target is not None

    def body(x_ref, gpre_ref, w1_ref, w3_ref, w2_ref, gpost_ref, *refs):
        t_ref, (h_ref, a_ref, b_ref, hm_ref, f_ref), l_ref = (refs[0] if last else None), refs[last:last + 5], refs[-1]
        xv = x_ref[...]
        nb = _rms(xv, gpre_ref[...]).astype(BF16)
        f = jnp.zeros((tm, D), F32)
        for s in range(NSH):
            a = _nt(nb, w1_ref[s])
            b = _nt(nb, w3_ref[s])
            hmb = ((a * jax.nn.sigmoid(a)) * b).astype(BF16)
            a_ref[s] = a.astype(BF16)
            b_ref[s] = b.astype(BF16)
            hm_ref[s] = hmb
            f = f + _nn(hmb, w2_ref[s])
        f_ref[...] = f
        h = xv + 0.5 * _rms(f, gpost_ref[...])
        if last:
            @pl.when(pl.program_id(0) == 0)
            def _():
                l_ref[...] = jnp.zeros_like(l_ref)

            e = h - t_ref[...]
            h_ref[...] = e * (1.0 / D)
            l_ref[...] += jnp.sum(jnp.sum(e * e, axis=0, keepdims=True), axis=1, keepdims=True)
        else:
            h_ref[...] = h

    sh = pl.BlockSpec((NSH, tm, FF_S), lambda i: (0, i, 0))
    act = jax.ShapeDtypeStruct((NSH, t, FF_S), BF16)
    return pl.pallas_call(
        body, grid=(t // tm,), name=name,
        in_specs=[_row_spec(tm, D), _vec_spec(D), _WHOLE, _WHOLE, _WHOLE, _vec_spec(D)] + [_row_spec(tm, D)] * last,
        out_specs=[_row_spec(tm, D), sh, sh, sh, _row_spec(tm, D)] + [pl.BlockSpec((1, 128), lambda i: (0, 0))] * last,
        out_shape=[jax.ShapeDtypeStruct((t, D), F32), act, act, act, jax.ShapeDtypeStruct((t, D), F32)]
        + [jax.ShapeDtypeStruct((1, 128), F32)] * last,
        compiler_params=_params("arbitrary"),
    )(x, gpre, w1g, w3g, w2g, gpost, *([target] if last else []))


def _ffn_up(x, gpre, w1g, w3g, name):
    t = x.shape[0]
    tm = _tile(t)

    def body(x_ref, gpre_ref, w1_ref, w3_ref, n_ref, a_ref, b_ref, hm_ref):
        nb = _rms(x_ref[...], gpre_ref[...]).astype(BF16)
        n_ref[...] = nb
        for s in range(NSH):
            a = _nt(nb, w1_ref[s])
            b = _nt(nb, w3_ref[s])
            a_ref[s] = a.astype(BF16)
            b_ref[s] = b.astype(BF16)
            hm_ref[s] = ((a * jax.nn.sigmoid(a)) * b).astype(BF16)

    sh = pl.BlockSpec((NSH, tm, FF_S), lambda i: (0, i, 0))
    act = jax.ShapeDtypeStruct((NSH, t, FF_S), BF16)
    return pl.pallas_call(
        body, grid=(t // tm,), name=name, in_specs=[_row_spec(tm, D), _vec_spec(D), _WHOLE, _WHOLE],
        out_specs=[_row_spec(tm, D), sh, sh, sh], out_shape=[jax.ShapeDtypeStruct((t, D), BF16), act, act, act],
        compiler_params=_params("arbitrary"),
    )(x, gpre, w1g, w3g)


def _ffn_down(x, hm, w2g, gpost, name):
    t = x.shape[0]
    tm = _tile(t)

    def body(x_ref, hm_ref, w2_ref, gpost_ref, h_ref, f_ref):
        f = jnp.zeros((tm, D), F32)
        for s in range(NSH):
            f = f + _nn(hm_ref[s], w2_ref[s])
        f_ref[...] = f
        h_ref[...] = x_ref[...] + 0.5 * _rms(f, gpost_ref[...])

    sh = pl.BlockSpec((NSH, tm, FF_S), lambda i: (0, i, 0))
    f32 = jax.ShapeDtypeStruct((t, D), F32)
    return pl.pallas_call(
        body, grid=(t // tm,), name=name, in_specs=[_row_spec(tm, D), sh, _WHOLE, _vec_spec(D)],
        out_specs=[_row_spec(tm, D), _row_spec(tm, D)], out_shape=[f32, f32], compiler_params=_params("arbitrary"),
    )(x, hm, w2g, gpost)


def _mix_proj(h1, gmix, w_in_g, w_gate_g, b_gate):
    t = h1.shape[0]
    tm = _tile(t)

    def body(h_ref, g_ref, win_ref, wg_ref, bg_ref, u_ref, q_ref, k_ref, v_ref, xr_ref, xg_ref, gate_ref):
        ub = _rms(h_ref[...], g_ref[...]).astype(BF16)
        u_ref[...] = ub
        p0 = _nn(ub, win_ref[0])
        q_ref[:, 0:896] = p0.astype(BF16)
        p1 = _nn(ub, win_ref[1])
        q_ref[:, 896:1024] = p1[:, 0:128].astype(BF16)
        k_ref[...] = p1[:, 128:384].astype(BF16)
        v_ref[...] = p1[:, 384:640].astype(BF16)
        xr_ref[:, 0:256] = p1[:, 640:896]
        p2 = _nn(ub, win_ref[2])
        xr_ref[:, 256:1024] = p2[:, 0:768]
        xg_ref[:, 0:128] = p2[:, 768:896].astype(BF16)
        xg_ref[:, 128:1024] = _nn(ub, win_ref[3]).astype(BF16)
        for s in range(NSH):
            sl = slice(s * GATE_S, (s + 1) * GATE_S)
            gate_ref[:, sl] = jax.nn.sigmoid(_nn(ub, wg_ref[s]) + bg_ref[:, sl]).astype(BF16)

    return pl.pallas_call(
        body, grid=(t // tm,), name="mix_proj",
        in_specs=[_row_spec(tm, D), _vec_spec(D), _WHOLE, _WHOLE, _vec_spec(2 * D)],
        out_specs=[_row_spec(tm, D), _row_spec(tm, D), _row_spec(tm, KV_W), _row_spec(tm, KV_W), _row_spec(tm, D),
                   _row_spec(tm, D), _row_spec(tm, 2 * D)],
        out_shape=[jax.ShapeDtypeStruct((t, D), BF16), jax.ShapeDtypeStruct((t, D), BF16),
                   jax.ShapeDtypeStruct((t, KV_W), BF16), jax.ShapeDtypeStruct((t, KV_W), BF16),
                   jax.ShapeDtypeStruct((t, D), F32), jax.ShapeDtypeStruct((t, D), BF16),
                   jax.ShapeDtypeStruct((t, 2 * D), BF16)],
        compiler_params=_params("arbitrary"),
    )(h1, gmix, w_in_g, w_gate_g, b_gate)


def _rglru_fwd(xr, xg, conv_w, conv_b, wa2, ba, wx2, bx, lam, after=None):
    t = xr.shape[0]
    tm = _tile(t, TM_SCAN)
    nb8 = tm // 8

    def body(xr_ref, xrp_ref, xg_ref, cw_ref, cb_ref, wa_ref, ba_ref, wx_ref, bx_ref, lam_ref,
             hr_ref, yain_ref, xc_ref, r_ref, ig_ref, a_sc, s_ref, ext, h_sc):
        i = pl.program_id(0)

        @pl.when(i == 0)
        def _():
            h_sc[...] = jnp.zeros_like(h_sc)

        ext[0:8, :] = jnp.where(i == 0, 0.0, xrp_ref[...])
        ext[8:8 + tm, :] = xr_ref[...]
        xc = jnp.broadcast_to(cb_ref[...], (tm, D))
        for tap in range(4):
            xc = xc + ext[pl.ds(5 + tap, tm), :] * cw_ref[tap:tap + 1, :]
        xc_ref[...] = xc
        xcb = xc.astype(BF16)
        for p in range(8):
            sl = slice(p * 128, (p + 1) * 128)
            r_ref[:, sl] = jax.nn.sigmoid(_nn(xcb[:, sl], wa_ref[p]) + ba_ref[:, sl])
            ig_ref[:, sl] = jax.nn.sigmoid(_nn(xcb[:, sl], wx_ref[p]) + bx_ref[:, sl])
        a, s = _lru_coeffs(r_ref[...], _softplus_neg(lam_ref[...]))
        a_sc[...] = a
        s_ref[...] = s
        hr_ref[...] = s * (ig_ref[...] * xc)

        def blk(j, h):
            st = pl.multiple_of(j * 8, 8)
            a8 = a_sc[pl.ds(st, 8), :]
            u8 = hr_ref[pl.ds(st, 8), :]
            rows = []
            for k in range(8):
                h = a8[k:k + 1, :] * h + u8[k:k + 1, :]
                rows.append(h)
            hr_ref[pl.ds(st, 8), :] = jnp.concatenate(rows, axis=0)
            return h

        h_sc[0:1, :] = lax.fori_loop(0, nb8, blk, h_sc[0:1, :])
        yain_ref[...] = (hr_ref[...] * _gelu_and_grad(xg_ref[...].astype(F32))[0]).astype(BF16)

    prev = pl.BlockSpec((8, D), lambda i: (jnp.maximum(i * nb8 - 1, 0), 0))
    full = lambda shape: pl.BlockSpec(shape, lambda i: tuple(0 for _ in shape))
    f32 = jax.ShapeDtypeStruct((t, D), F32)
    body, specs, operands = _behind(body, after)
    return pl.pallas_call(
        body, grid=(t // tm,), name="rglru_fwd",
        in_specs=specs + [_row_spec(tm, D), prev, _row_spec(tm, D), full((4, D)), _vec_spec(D), full((8, 128, 128)),
                          _vec_spec(D), full((8, 128, 128)), _vec_spec(D), _vec_spec(D)],
        out_specs=[_row_spec(tm, D)] * 7,
        out_shape=[f32, jax.ShapeDtypeStruct((t, D), BF16), f32, f32, f32, f32, f32],
        scratch_shapes=[pltpu.VMEM((tm + 8, D), F32), pltpu.VMEM((8, D), F32)],
        compiler_params=_params("arbitrary"),
    )(*operands, xr, xr, xg, conv_w, conv_b, wa2, ba, wx2, bx, lam)


def _bias_fwd(table_t, onehot_t):
    def body(t_ref, e_ref, o_ref):
        o_ref[...] = jnp.dot(t_ref[...], e_ref[...], preferred_element_type=F32, precision=lax.Precision.HIGHEST)

    return pl.pallas_call(body, out_shape=jax.ShapeDtypeStruct((N_HEADS, CHUNK * KB), F32), name="bias_fwd",
                          compiler_params=_params())(table_t, onehot_t)


def _bias_bwd(dbias_flat, onehot_t, ds_rows):
    def body(d_ref, e_ref, s_ref, o_ref, so_ref):
        o_ref[...] = lax.dot_general(d_ref[...], e_ref[...], (((1,), (1,)), ((), ())), preferred_element_type=F32,
                                     precision=lax.Precision.HIGHEST)
        so_ref[...] = jnp.zeros_like(so_ref)
        for r in range(4):
            so_ref[:, r:r + 1] = jnp.sum(s_ref[:, r * CHUNK:(r + 1) * CHUNK], axis=1, keepdims=True)

    return pl.pallas_call(body, out_shape=[jax.ShapeDtypeStruct((N_HEADS, N_BUCKETS), F32), jax.ShapeDtypeStruct((8, 128), F32)],
                          name="bias_bwd", compiler_params=_params())(dbias_flat, onehot_t, ds_rows)


def _stack_heads(q):
    return jnp.concatenate(
        [jnp.concatenate([q[:, (4 * g + r) * HEAD_DIM:(4 * g + r + 1) * HEAD_DIM] for g in range(4)], axis=1)
         for r in range(4)], axis=0)


def _unstack_heads(o):
    return jnp.concatenate([o[r * CHUNK:(r + 1) * CHUNK, g * HEAD_DIM:(g + 1) * HEAD_DIM] for g in range(4) for r in range(4)],
                           axis=1)


def _block_diag(w, mask):
    return jnp.concatenate([w] * 4, axis=0) * mask


def _group_softmax(qk, bias_g, sink, valid):
    s = qk * (HEAD_DIM ** -0.5) + bias_g
    s = jnp.where(valid, s, NEG_INF)
    m = jnp.maximum(jnp.max(s, axis=0, keepdims=True), sink)
    e = jnp.exp(s - m)
    es = jnp.exp(sink - m)
    inv = 1.0 / (jnp.sum(e, axis=0, keepdims=True) + es)
    return e * inv, es * inv


def _attn_fwd(sink_rows, q, kp, vp, bias_t, mask, after=None):
    t = q.shape[0]
    per_step = 8

    def body(sink_ref, q_ref, kp_ref, vp_ref, bias_ref, mask_ref, o_ref):
        owns = [mask_ref[g * KP:(g + 1) * KP, :] for g in range(4)]
        for k in range(per_step):
            c = pl.program_id(0) * per_step + k
            rows = slice(k * CHUNK, (k + 1) * CHUNK)
            st = pl.multiple_of(c * CHUNK, CHUNK)
            kw = kp_ref[pl.ds(st, KP), :]
            vw = vp_ref[pl.ds(st, KP), :]
            q_all = _stack_heads(q_ref[rows, :])
            valid = lax.broadcasted_iota(jnp.int32, (KP, 1), 0) + c * CHUNK >= PAD_KEYS
            scores = [_nt(kw * owns[g], q_all) for g in range(4)]
            ps = [_group_softmax(scores[g], bias_ref[g * KP:(g + 1) * KP, :], sink_ref[g:g + 1, :], valid)[0]
                  for g in range(4)]
            o_all = sum(_tn(ps[g].astype(BF16), vw * owns[g]) for g in range(4))
            o_ref[rows, :] = _unstack_heads(o_all).astype(BF16)

    body, specs, operands = _behind(body, after)
    return pl.pallas_call(
        body, grid=(t // (per_step * CHUNK),), name="attn_fwd",
        in_specs=specs + [_WHOLE, _row_spec(per_step * CHUNK, D), _WHOLE, _WHOLE, _WHOLE, _WHOLE],
        out_specs=_row_spec(per_step * CHUNK, D),
        out_shape=jax.ShapeDtypeStruct((t, D), BF16),
        compiler_params=_params("arbitrary"),
    )(*operands, sink_rows, q, kp, vp, bias_t, mask)


def _merge_fwd(yain, o, gate, h1, w_lru, w_att, w_o, gpost):
    t = h1.shape[0]
    tm = _tile(t)

    def body(ya_ref, o_ref, g_ref, h_ref, wl_ref, wa_ref, wo_ref, gp_ref, h2_ref, mo_ref, mg_ref, ya_out, yb_out):
        ya = _nn(ya_ref[...], wl_ref[...])
        yb = _nn(o_ref[...], wa_ref[...])
        g0 = g_ref[:, 0:D].astype(F32)
        g1 = g_ref[:, D:2 * D].astype(F32)
        mg = (g0 * ya + g1 * yb).astype(BF16)
        mo = _nn(mg, wo_ref[...])
        ya_out[...] = (ya * (g0 * (1.0 - g0))).astype(BF16)
        yb_out[...] = (yb * (g1 * (1.0 - g1))).astype(BF16)
        mg_ref[...] = mg
        mo_ref[...] = mo
        h2_ref[...] = h_ref[...] + _rms(mo, gp_ref[...])

    f32 = jax.ShapeDtypeStruct((t, D), F32)
    b16 = jax.ShapeDtypeStruct((t, D), BF16)
    return pl.pallas_call(
        body, grid=(t // tm,), name="merge_fwd",
        in_specs=[_row_spec(tm, D), _row_spec(tm, D), _row_spec(tm, 2 * D), _row_spec(tm, D), _WHOLE, _WHOLE, _WHOLE,
                  _vec_spec(D)],
        out_specs=[_row_spec(tm, D)] * 5,
        out_shape=[f32, f32, b16, b16, b16],
        compiler_params=_params("arbitrary"),
    )(yain, o, gate, h1, w_lru, w_att, w_o, gpost)


def _ffn_bwd(dh, x, f, a, b, gpre, gpost, w1g, w3g, w2g, name):
    t = x.shape[0]
    tm = _tile(t, TM_SCAN)

    def body(dh_ref, x_ref, f_ref, a_ref, b_ref, gpre_ref, gpost_ref, w1_ref, w3_ref, w2_ref,
             dx_ref, n_ref, da_ref, db_ref, df_ref, dgpre_ref, dgpost_ref):
        @pl.when(pl.program_id(0) == 0)
        def _():
            dgpre_ref[...] = jnp.zeros_like(dgpre_ref)
            dgpost_ref[...] = jnp.zeros_like(dgpost_ref)

        dhv = dh_ref[...]
        xv = x_ref[...]
        df, dgp = _rms_bwd(0.5 * dhv, f_ref[...], gpost_ref[...])
        dgpost_ref[...] += dgp
        dfb = df.astype(BF16)
        df_ref[...] = dfb
        n_ref[...] = _rms(xv, gpre_ref[...]).astype(BF16)
        dn = jnp.zeros((tm, D), F32)
        for s in range(NSH):
            av = a_ref[s].astype(F32)
            bv = b_ref[s].astype(F32)
            sg = jax.nn.sigmoid(av)
            dhm = _nt(dfb, w2_ref[s])
            dab = (dhm * bv * (sg * (1.0 + av * (1.0 - sg)))).astype(BF16)
            dbb = (dhm * (av * sg)).astype(BF16)
            da_ref[s] = dab
            db_ref[s] = dbb
            dn = dn + _nn(dab, w1_ref[s]) + _nn(dbb, w3_ref[s])
        dxn, dg = _rms_bwd(dn, xv, gpre_ref[...])
        dgpre_ref[...] += dg
        dx_ref[...] = dhv + dxn

    sh = pl.BlockSpec((NSH, tm, FF_S), lambda i: (0, i, 0))
    act = jax.ShapeDtypeStruct((NSH, t, FF_S), BF16)
    vec = jax.ShapeDtypeStruct((1, D), F32)
    return pl.pallas_call(
        body, grid=(t // tm,), name=name,
        in_specs=[_row_spec(tm, D), _row_spec(tm, D), _row_spec(tm, D), sh, sh, _vec_spec(D), _vec_spec(D), _WHOLE, _WHOLE,
                  _WHOLE],
        out_specs=[_row_spec(tm, D), _row_spec(tm, D), sh, sh, _row_spec(tm, D), _vec_spec(D), _vec_spec(D)],
        out_shape=[jax.ShapeDtypeStruct((t, D), F32), jax.ShapeDtypeStruct((t, D), BF16), act, act,
                   jax.ShapeDtypeStruct((t, D), BF16), vec, vec],
        compiler_params=_params("arbitrary"),
    )(dh, x, f, a, b, gpre, gpost, w1g, w3g, w2g)


def _behind(body, after):
    if after is None:
        return body, [], []

    def ordered(_, *refs):
        body(*refs)

    return ordered, [_ANY], [after]


def _ffn_bwd_acts(dh, f, a, b, gpost, w2g, name):
    t = dh.shape[0]
    tm = _tile(t)

    def body(dh_ref, f_ref, a_ref, b_ref, gpost_ref, w2_ref, da_ref, db_ref, df_ref, dgpost_ref):
        @pl.when(pl.program_id(0) == 0)
        def _():
            dgpost_ref[...] = jnp.zeros_like(dgpost_ref)

        df, dgp = _rms_bwd(0.5 * dh_ref[...], f_ref[...], gpost_ref[...])
        dgpost_ref[...] += dgp
        dfb = df.astype(BF16)
        df_ref[...] = dfb
        for s in range(NSH):
            av = a_ref[s].astype(F32)
            bv = b_ref[s].astype(F32)
            sg = jax.nn.sigmoid(av)
            dhm = _nt(dfb, w2_ref[s])
            da_ref[s] = (dhm * bv * (sg * (1.0 + av * (1.0 - sg)))).astype(BF16)
            db_ref[s] = (dhm * (av * sg)).astype(BF16)

    sh = pl.BlockSpec((NSH, tm, FF_S), lambda i: (0, i, 0))
    act = jax.ShapeDtypeStruct((NSH, t, FF_S), BF16)
    b16 = jax.ShapeDtypeStruct((t, D), BF16)
    return pl.pallas_call(
        body, grid=(t // tm,), name=name,
        in_specs=[_row_spec(tm, D), _row_spec(tm, D), sh, sh, _vec_spec(D), _WHOLE],
        out_specs=[sh, sh, _row_spec(tm, D), _vec_spec(D)],
        out_shape=[act, act, b16, jax.ShapeDtypeStruct((1, D), F32)],
        compiler_params=_params("arbitrary"),
    )(dh, f, a, b, gpost, w2g)


def _ffn_bwd_input(dh, x, da, db, gpre, w1g, w3g, name, after):
    t = x.shape[0]
    tm = _tile(t)

    def body(dh_ref, x_ref, da_ref, db_ref, gpre_ref, w1_ref, w3_ref, dx_ref, dgpre_ref):
        @pl.when(pl.program_id(0) == 0)
        def _():
            dgpre_ref[...] = jnp.zeros_like(dgpre_ref)

        dn = jnp.zeros((tm, D), F32)
        for s in range(NSH):
            dn = dn + _nn(da_ref[s], w1_ref[s]) + _nn(db_ref[s], w3_ref[s])
        dxn, dg = _rms_bwd(dn, x_ref[...], gpre_ref[...])
        dgpre_ref[...] += dg
        dx_ref[...] = dh_ref[...] + dxn

    sh = pl.BlockSpec((NSH, tm, FF_S), lambda i: (0, i, 0))
    body, specs, operands = _behind(body, after)
    return pl.pallas_call(
        body, grid=(t // tm,), name=name,
        in_specs=specs + [_row_spec(tm, D), _row_spec(tm, D), sh, sh, _vec_spec(D), _WHOLE, _WHOLE],
        out_specs=[_row_spec(tm, D), _vec_spec(D)],
        out_shape=[jax.ShapeDtypeStruct((t, D), F32), jax.ShapeDtypeStruct((1, D), F32)],
        compiler_params=_params("arbitrary"),
    )(*operands, dh, x, da, db, gpre, w1g, w3g)


def _wgrad(a, b, a_spec, b_spec, out_spec, out_shape, grid, name, after=None):
    def body(a_ref, b_ref, o_ref):
        o_ref[...] = _tn(a_ref[...], b_ref[...]).astype(BF16)

    body, specs, operands = _behind(body, after)
    return pl.pallas_call(body, grid=grid, name=name, in_specs=specs + [a_spec, b_spec], out_specs=out_spec,
                          out_shape=jax.ShapeDtypeStruct(out_shape, BF16),
                          compiler_params=_params(*("arbitrary",) * len(grid)))(*operands, a, b)


def _wgrad_cols(act, dsh, width, name, after=None):
    t = act.shape[0]
    if dsh.ndim == 3:
        b_spec = pl.BlockSpec((None, t, width), lambda s, k: (s, 0, 0))
    else:
        b_spec = pl.BlockSpec((t, width), lambda s, k: (0, s))
    return _wgrad(act, dsh, pl.BlockSpec((t, 512), lambda s, k: (0, k)), b_spec,
                  pl.BlockSpec((None, 512, width), lambda s, k: (s, k, 0)), (NSH, D, width), (NSH, 2), name, after)


def _wgrad_rows(hm, df, name, after=None):
    t = df.shape[0]
    return _wgrad(hm, df, pl.BlockSpec((None, t, FF_S), lambda s: (s, 0, 0)), pl.BlockSpec((t, D), lambda s: (0, 0)),
                  pl.BlockSpec((None, FF_S, D), lambda s: (s, 0, 0)), (NSH, FF_S, D), (NSH,), name, after)


def _wgrad_sq(a, b, name, after=None):
    t = a.shape[0]
    return _wgrad(a, b, pl.BlockSpec((t, D), lambda j: (0, 0)), pl.BlockSpec((t, 512), lambda j: (0, j)),
                  pl.BlockSpec((D, 512), lambda j: (0, j)), (D, D), (2,), name, after)


def _mix_bwd1(dh2, mo, gpost, gate, ya, yb, xg, hr, w_o, w_lru, w_att, after):
    t = dh2.shape[0]
    tm = _tile(t, TM_SCAN)

    def body(dh_ref, mo_ref, gp_ref, g_ref, ya_ref, yb_ref, xg_ref, hr_ref, wo_ref, wl_ref, wa_ref,
             dmo_ref, dya_ref, dyb_ref, dgate_ref, dhr_ref, dxg_ref, do_ref, dgp_ref, dbg_ref):
        @pl.when(pl.program_id(0) == 0)
        def _():
            dgp_ref[...] = jnp.zeros_like(dgp_ref)
            dbg_ref[...] = jnp.zeros_like(dbg_ref)

        dmo, dgp = _rms_bwd(dh_ref[...], mo_ref[...], gp_ref[...])
        dgp_ref[...] += dgp
        dmob = dmo.astype(BF16)
        dmo_ref[...] = dmob
        dm = _nt(dmob, wo_ref[...])
        g0 = g_ref[:, 0:D].astype(F32)
        g1 = g_ref[:, D:2 * D].astype(F32)
        dyab = (dm * g0).astype(BF16)
        dybb = (dm * g1).astype(BF16)
        dya_ref[...] = dyab
        dyb_ref[...] = dybb
        dg0 = dm * ya_ref[...].astype(F32)
        dg1 = dm * yb_ref[...].astype(F32)
        dgate_ref[:, 0:D] = dg0.astype(BF16)
        dgate_ref[:, D:2 * D] = dg1.astype(BF16)
        dbg_ref[:, 0:D] += jnp.sum(dg0, axis=0, keepdims=True)
        dbg_ref[:, D:2 * D] += jnp.sum(dg1, axis=0, keepdims=True)
        dyain = _nt(dyab, wl_ref[...])
        do_ref[...] = _nt(dybb, wa_ref[...]).astype(BF16)
        gelu, gelu_grad = _gelu_and_grad(xg_ref[...].astype(F32))
        dhr_ref[...] = dyain * gelu
        dxg_ref[...] = (dyain * hr_ref[...] * gelu_grad).astype(BF16)

    b16 = jax.ShapeDtypeStruct((t, D), BF16)
    body, specs, operands = _behind(body, after)
    return pl.pallas_call(
        body, grid=(t // tm,), name="mix_bwd1",
        in_specs=specs + [_row_spec(tm, D), _row_spec(tm, D), _vec_spec(D), _row_spec(tm, 2 * D), _row_spec(tm, D),
                          _row_spec(tm, D), _row_spec(tm, D), _row_spec(tm, D), _WHOLE, _WHOLE, _WHOLE],
        out_specs=[_row_spec(tm, D), _row_spec(tm, D), _row_spec(tm, D), _row_spec(tm, 2 * D), _row_spec(tm, D),
                   _row_spec(tm, D), _row_spec(tm, D), _vec_spec(D), _vec_spec(2 * D)],
        out_shape=[b16, b16, b16, jax.ShapeDtypeStruct((t, 2 * D), BF16), jax.ShapeDtypeStruct((t, D), F32), b16, b16,
                   jax.ShapeDtypeStruct((1, D), F32), jax.ShapeDtypeStruct((1, 2 * D), F32)],
        compiler_params=_params("arbitrary"),
    )(*operands, dh2, mo, gpost, gate, ya, yb, xg, hr, w_o, w_lru, w_att)


def _rglru_bwd(dhr, hr, xc, r, ig, a, s, xr, conv_w, wa2, wx2, lam, after):
    t = dhr.shape[0]
    tm = _tile(t, TM_SCAN)
    nb8 = tm // 8
    nt = t // tm

    def body(dhr_ref, hr_ref, hrp_ref, xc_ref, r_ref, ig_ref, a_sc, s_ref, xr_ref, cw_ref, wa_ref, wx_ref, lam_ref,
             dxr_ref, dwa_ref, dwx_ref, dba_ref, dbx_ref, dlam_ref, dcw_ref, dcb_ref,
             ext_h, ext_d, g_sc, c_sc, nxt_sc):
        i = pl.program_id(0)
        first_tile = i == nt - 1

        @pl.when(i == 0)
        def _():
            c_sc[...] = jnp.zeros_like(c_sc)
            nxt_sc[...] = jnp.zeros_like(nxt_sc)
            for ref in (dwa_ref, dwx_ref, dba_ref, dbx_ref, dlam_ref, dcw_ref, dcb_ref):
                ref[...] = jnp.zeros_like(ref)

        lamv = lam_ref[...]
        sp = _softplus_neg(lamv)
        rv = r_ref[...]
        igv = ig_ref[...]
        xcv = xc_ref[...]
        a = a_sc[...]
        s = s_ref[...]

        def blk(jj, c):
            st = pl.multiple_of((nb8 - 1 - jj) * 8, 8)
            d8 = dhr_ref[pl.ds(st, 8), :]
            a8 = a_sc[pl.ds(st, 8), :]
            rows = [None] * 8
            for k in range(7, -1, -1):
                g = d8[k:k + 1, :] + c
                c = a8[k:k + 1, :] * g
                rows[k] = g
            g_sc[pl.ds(st, 8), :] = jnp.concatenate(rows, axis=0)
            return c

        c_sc[0:1, :] = lax.fori_loop(0, nb8, blk, c_sc[0:1, :])
        g = g_sc[...]
        ext_h[0:8, :] = jnp.where(first_tile, 0.0, hrp_ref[...])
        ext_h[8:8 + tm, :] = hr_ref[...]
        hprev = ext_h[pl.ds(7, tm), :]
        d_s = g * (igv * xcv)
        dig = g * s * xcv
        dxc = g * s * igv
        dla = (g * hprev) * a - d_s * ((a * a) / s)
        dr_pre = (dla * (-LRU_C * sp)) * (rv * (1.0 - rv))
        di_pre = dig * (igv * (1.0 - igv))
        dlam_ref[...] += jnp.sum(dla * (LRU_C * rv), axis=0, keepdims=True) * jax.nn.sigmoid(-lamv)
        dba_ref[...] += jnp.sum(dr_pre, axis=0, keepdims=True)
        dbx_ref[...] += jnp.sum(di_pre, axis=0, keepdims=True)
        drb = dr_pre.astype(BF16)
        dib = di_pre.astype(BF16)
        xcb = xcv.astype(BF16)
        ext_d[tm:tm + 8, :] = nxt_sc[...]
        for p in range(8):
            sl = slice(p * 128, (p + 1) * 128)
            ext_d[0:tm, sl] = dxc[:, sl] + _nt(drb[:, sl], wa_ref[p]) + _nt(dib[:, sl], wx_ref[p])
            dwa_ref[p] += _tn(xcb[:, sl], drb[:, sl])
            dwx_ref[p] += _tn(xcb[:, sl], dib[:, sl])
        dxcv = ext_d[0:tm, :]
        nxt_sc[...] = ext_d[0:8, :]
        dcb_ref[...] += jnp.sum(dxcv, axis=0, keepdims=True)
        xrv = xr_ref[...]
        dxr = jnp.zeros((tm, D), F32)
        for tap in range(4):
            ext_h[0:tm, :] = ext_d[pl.ds(3 - tap, tm), :]
            ahead = ext_h[0:tm, :]
            dxr = dxr + ahead * cw_ref[tap:tap + 1, :]
            dcw_ref[tap:tap + 1, :] += jnp.sum(ahead * xrv, axis=0, keepdims=True)
        dxr_ref[...] = dxr.astype(BF16)

    rev = pl.BlockSpec((tm, D), lambda i: (nt - 1 - i, 0))
    prev = pl.BlockSpec((8, D), lambda i: (jnp.maximum((nt - 1 - i) * nb8 - 1, 0), 0))
    full = lambda shape: pl.BlockSpec(shape, lambda i: tuple(0 for _ in shape))
    vec = jax.ShapeDtypeStruct((1, D), F32)
    blocks = jax.ShapeDtypeStruct((8, 128, 128), F32)
    body, specs, operands = _behind(body, after)
    return pl.pallas_call(
        body, grid=(nt,), name="rglru_bwd",
        in_specs=specs + [rev, rev, prev, rev, rev, rev, rev, rev, rev, full((4, D)), full((8, 128, 128)),
                          full((8, 128, 128)), _vec_spec(D)],
        out_specs=[rev, full((8, 128, 128)), full((8, 128, 128)), _vec_spec(D), _vec_spec(D), _vec_spec(D), full((4, D)),
                   _vec_spec(D)],
        out_shape=[jax.ShapeDtypeStruct((t, D), BF16), blocks, blocks, vec, vec, vec, jax.ShapeDtypeStruct((4, D), F32), vec],
        scratch_shapes=[pltpu.VMEM((tm + 8, D), F32), pltpu.VMEM((tm + 8, D), F32),
                        pltpu.VMEM((tm, D), F32), pltpu.VMEM((8, D), F32), pltpu.VMEM((8, D), F32)],
        compiler_params=_params("arbitrary"),
    )(*operands, dhr, hr, hr, xc, r, ig, a, s, xr, conv_w, wa2, wx2, lam)


def _attn_bwd(sink_rows, q, kp, vp, bias_t, mask, do):
    t = q.shape[0]
    tp = kp.shape[0]
    per_step = 16

    def body(sink_ref, q_ref, kp_ref, vp_ref, bias_ref, mask_ref, do_ref, dq_ref, dk_ref, dv_ref, dbias_ref, ds_ref):
        @pl.when(pl.program_id(0) == 0)
        def _():
            for ref in (dk_ref, dv_ref, dbias_ref, ds_ref):
                ref[...] = jnp.zeros_like(ref)

        maskv = mask_ref[...]
        lane_group = lax.broadcasted_iota(jnp.int32, (1, 4 * HEAD_DIM), 1) // HEAD_DIM

        def own_blocks(full):
            out = full[0:KP]
            for g in range(1, 4):
                out = jnp.where(lane_group == g, full[g * KP:(g + 1) * KP], out)
            return out

        dsc_sum, dsinks, dks, dvs = 0.0, [0.0] * 4, [], []
        for k in range(per_step):
            c = pl.program_id(0) * per_step + k
            chunk = slice(k * CHUNK, (k + 1) * CHUNK)
            st = pl.multiple_of(c * CHUNK, CHUNK)
            kbd = _block_diag(kp_ref[pl.ds(st, KP), :], maskv)
            vbd = _block_diag(vp_ref[pl.ds(st, KP), :], maskv)
            q_all = _stack_heads(q_ref[chunk, :])
            do_all = _stack_heads(do_ref[chunk, :])
            valid = lax.broadcasted_iota(jnp.int32, (KP, 1), 0) + c * CHUNK >= PAD_KEYS
            qk = _nt(kbd, q_all)
            dp = _nt(vbd, do_all)
            ps, dscs = [], []
            for g in range(4):
                rows = slice(g * KP, (g + 1) * KP)
                p, sink_p = _group_softmax(qk[rows], bias_ref[rows, :], sink_ref[g:g + 1, :], valid)
                delta = jnp.sum(p * dp[rows], axis=0, keepdims=True)
                ps.append(p)
                dscs.append(p * (dp[rows] - delta))
                dsinks[g] = dsinks[g] - sink_p * delta
            dsc = jnp.concatenate(dscs, axis=0)
            dsc_sum = dsc_sum + dsc
            dsb = (dsc * (HEAD_DIM ** -0.5)).astype(BF16)
            dq_ref[chunk, :] = _unstack_heads(_tn(dsb, kbd)).astype(BF16)
            dks.append((st, own_blocks(_nn(dsb, q_all))))
            dvs.append((st, own_blocks(_nn(jnp.concatenate(ps, axis=0).astype(BF16), do_all))))
        dbias_ref[...] += dsc_sum
        for g in range(4):
            ds_ref[g:g + 1, :] += dsinks[g]
        for (st, dkw), (_, dvw) in zip(dks, dvs):
            dk_ref[pl.ds(st, KP), :] += dkw
            dv_ref[pl.ds(st, KP), :] += dvw

    full = lambda shape: pl.BlockSpec(shape, lambda i: tuple(0 for _ in shape))
    return pl.pallas_call(
        body, grid=(t // (per_step * CHUNK),), name="attn_bwd",
        in_specs=[_WHOLE, _row_spec(per_step * CHUNK, D), _WHOLE, _WHOLE, _WHOLE, _WHOLE, _row_spec(per_step * CHUNK, D)],
        out_specs=[_row_spec(per_step * CHUNK, D), full((tp, KV_W)), full((tp, KV_W)), full((4 * KP, 4 * CHUNK)),
                   full((8, 4 * CHUNK))],
        out_shape=[jax.ShapeDtypeStruct((t, D), BF16), jax.ShapeDtypeStruct((tp, KV_W), F32),
                   jax.ShapeDtypeStruct((tp, KV_W), F32), jax.ShapeDtypeStruct((4 * KP, 4 * CHUNK), F32),
                   jax.ShapeDtypeStruct((8, 4 * CHUNK), F32)],
        compiler_params=_params("arbitrary"),
    )(sink_rows, q, kp, vp, bias_t, mask, do)


def _mix_bwd2(dproj, dgate, h1, dh2, gmix, w_in_g, w_gate_g, after):
    t = h1.shape[0]
    tm = _tile(t)

    def body(dp_ref, dg_ref, h_ref, dh_ref, g_ref, win_ref, wg_ref, dh1_ref, dgm_ref):
        @pl.when(pl.program_id(0) == 0)
        def _():
            dgm_ref[...] = jnp.zeros_like(dgm_ref)

        du = jnp.zeros((tm, D), F32)
        for s in range(NSH):
            du = du + _nt(dp_ref[:, s * IN_S:(s + 1) * IN_S], win_ref[s])
            du = du + _nt(dg_ref[:, s * GATE_S:(s + 1) * GATE_S], wg_ref[s])
        dxn, dg = _rms_bwd(du, h_ref[...], g_ref[...])
        dgm_ref[...] += dg
        dh1_ref[...] = dh_ref[...] + dxn

    body, specs, operands = _behind(body, after)
    return pl.pallas_call(
        body, grid=(t // tm,), name="mix_bwd2",
        in_specs=specs + [_row_spec(tm, NSH * IN_S), _row_spec(tm, 2 * D), _row_spec(tm, D), _row_spec(tm, D), _vec_spec(D),
                          _WHOLE, _WHOLE],
        out_specs=[_row_spec(tm, D), _vec_spec(D)],
        out_shape=[jax.ShapeDtypeStruct((t, D), F32), jax.ShapeDtypeStruct((1, D), F32)],
        compiler_params=_params("arbitrary"),
    )(*operands, dproj, dgate, h1, dh2, gmix, w_in_g, w_gate_g)


def _band_onehot():
    nb = N_BUCKETS // 2
    max_exact = nb // 2
    rel = jnp.arange(KB)[None, :] - PAD_KEYS - jnp.arange(CHUNK)[:, None]
    ret = jnp.where(rel > 0, nb, 0)
    n = jnp.abs(rel)
    nf = jnp.maximum(n, 1).astype(jnp.float32)
    large = max_exact + (jnp.log(nf / max_exact) / math.log(128 / max_exact) * (nb - max_exact)).astype(jnp.int32)
    large = jnp.minimum(large, nb - 1)
    buckets = (ret + jnp.where(n < max_exact, n, large)).reshape(1, CHUNK * KB)
    return (buckets == jnp.arange(N_BUCKETS)[:, None]).astype(F32)


def _pair_blocks(w):
    pairs = w.reshape(8, 2, 64, 64)
    z = jnp.zeros((8, 64, 64), w.dtype)
    return jnp.concatenate([jnp.concatenate([pairs[:, 0], z], axis=2), jnp.concatenate([z, pairs[:, 1]], axis=2)], axis=1)


def _unpair_blocks(w2):
    return jnp.stack([w2[:, 0:64, 0:64], w2[:, 64:128, 64:128]], axis=1).reshape(16, 64, 64)


def _local_step(x, target, weights, sm, reducer):
    row = lambda v: v.reshape(1, -1)
    onehot_t = _band_onehot()
    bias = _bias_fwd(sm["rel_bias"].T, onehot_t).reshape(4, 4, CHUNK, KB)
    bias_t = jnp.pad(jnp.transpose(bias, (0, 3, 1, 2)), ((0, 0), (0, KP - KB), (0, 0), (0, 0))).reshape(4 * KP, 4 * CHUNK)
    sink_rows = jnp.pad(jnp.repeat(sm["attn_sinks"].reshape(4, 4), CHUNK, axis=1), ((0, 4), (0, 0)))
    grp = jnp.arange(4 * KP)[:, None] // KP == jnp.arange(4 * HEAD_DIM)[None, :] // HEAD_DIM
    mask = (grp & (jnp.arange(4 * KP)[:, None] % KP < KB)).astype(BF16)
    wa2 = _pair_blocks(sm["rg_a_w"]).astype(BF16)
    wx2 = _pair_blocks(sm["rg_x_w"]).astype(BF16)
    wg = dict(weights("ffn1_up", [bias_t, sink_rows, mask, wa2, wx2]))
    sm = dict(sm, conv_w=wg["conv_w"])

    n1, a1, b1, hm1 = _ffn_up(x, row(sm["ffn1_pre_g"]), wg["ffn1_w1"], wg["ffn1_w3"], "ffn1_up")
    wg.update(weights("ffn1_down", hm1))
    h1, f1 = _ffn_down(x, hm1, wg["ffn1_w2"], row(sm["ffn1_post_g"]), "ffn1_down")
    wg.update(weights("mix_in", h1))
    u, q, k, v, xr, xg, gate = _mix_proj(h1, row(sm["mix_pre_g"]), wg["w_in"], wg["w_gate"], row(sm["b_gate"]))
    token = weights("mix_out", u, begin=True)
    hr, yain, xc, r, ig, lru_a, lru_s = _rglru_fwd(xr, xg, sm["conv_w"], row(sm["conv_b"]), wa2, row(sm["rg_a_b"]), wx2,
                                                   row(sm["rg_x_b"]), row(sm["lru_lambda"]), token)
    token = weights("ffn2", hr, begin=True)
    kp = jnp.pad(k, ((PAD_KEYS, KP - KB), (0, 0)))
    vp = jnp.pad(v, ((PAD_KEYS, KP - KB), (0, 0)))
    o = _attn_fwd(sink_rows, q, kp, vp, bias_t, mask, token)
    wg.update(weights("mix_out", o))
    w_lru = wg["w_lru_out"].reshape(D, D)
    w_att = wg["w_attn_out"].reshape(D, D)
    w_o = wg["w_o"].reshape(D, D)
    wg.update(weights("ffn2", o))
    h2, mo, merged, ya, yb = _merge_fwd(yain, o, gate, h1, w_lru, w_att, w_o, row(sm["mix_post_g"]))
    dy, a2, b2, hm2, f2, sq = _ffn_fwd(h2, row(sm["ffn2_pre_g"]), wg["ffn2_w1"], wg["ffn2_w3"], wg["ffn2_w2"],
                                       row(sm["ffn2_post_g"]), "ffn2_fwd", target)

    big, small = {}, {}
    dh2, n2, da2, db2, df2, small["ffn2_pre_g"], small["ffn2_post_g"] = _ffn_bwd(
        dy, h2, f2, a2, b2, row(sm["ffn2_pre_g"]), row(sm["ffn2_post_g"]), wg["ffn2_w1"], wg["ffn2_w3"], wg["ffn2_w2"],
        "ffn2_bwd")
    big["ffn2_w1"] = _wgrad_rows(da2, n2, "dw_ffn2_w1")
    big["ffn2_w3"] = _wgrad_rows(db2, n2, "dw_ffn2_w3")
    big["ffn2_w2"] = _wgrad_rows(hm2, df2, "dw_ffn2_w2")
    token = reducer.begin("ffn2", {n: big[n] for n in ("ffn2_w1", "ffn2_w3", "ffn2_w2")})
    dmo, dya, dyb, dgate, dhr, dxg, do, small["mix_post_g"], small["b_gate"] = _mix_bwd1(
        dh2, mo, row(sm["mix_post_g"]), gate, ya, yb, xg, hr, w_o, w_lru, w_att, token)
    big["w_o"] = _wgrad_sq(merged, dmo, "dw_w_o").reshape(NSH, D // NSH, D)
    big["w_lru_out"] = _wgrad_sq(yain, dya, "dw_w_lru_out").reshape(NSH, D // NSH, D)
    big["w_attn_out"] = _wgrad_sq(o, dyb, "dw_w_attn_out").reshape(NSH, D // NSH, D)
    token = reducer.advance("ffn2", big["w_attn_out"])
    (dxr, dwa2, dwx2, small["rg_a_b"], small["rg_x_b"], small["lru_lambda"], small["conv_w"], small["conv_b"]) = _rglru_bwd(
        dhr, hr, xc, r, ig, lru_a, lru_s, xr, sm["conv_w"], wa2, wx2, row(sm["lru_lambda"]), token)
    small["rg_a_w"] = _unpair_blocks(dwa2)
    small["rg_x_w"] = _unpair_blocks(dwx2)
    dq, dkp, dvp, dbias_t, ds_rows = _attn_bwd(sink_rows, q, kp, vp, bias_t, mask, do)
    dbias = jnp.transpose(dbias_t.reshape(4, KP, 4, CHUNK)[:, :KB], (0, 2, 3, 1)).reshape(N_HEADS, CHUNK * KB)
    drel_t, dsinks = _bias_bwd(dbias, onehot_t, ds_rows)
    small["attn_sinks"] = dsinks[0:4, 0:4].reshape(N_HEADS)
    small["rel_bias"] = drel_t.T
    t = x.shape[0]
    dproj = jnp.concatenate([dq, dkp[PAD_KEYS:PAD_KEYS + t].astype(BF16), dvp[PAD_KEYS:PAD_KEYS + t].astype(BF16), dxr, dxg],
                            axis=1)
    big["w_in"] = _wgrad_cols(u, dproj, IN_S, "dw_w_in")
    big["w_gate"] = _wgrad_cols(u, dgate, GATE_S, "dw_w_gate")
    token = reducer.begin("mix", {n: big[n] for n in ("w_in", "w_gate", "w_lru_out", "w_attn_out", "w_o")})
    dh1, small["mix_pre_g"] = _mix_bwd2(dproj, dgate, h1, dh2, row(sm["mix_pre_g"]), wg["w_in"], wg["w_gate"], token)
    da1, db1, df1, small["ffn1_post_g"] = _ffn_bwd_acts(dh1, f1, a1, b1, row(sm["ffn1_post_g"]), wg["ffn1_w2"],
                                                        "ffn1_bwd_acts")
    token = reducer.advance("mix", df1)
    big["ffn1_w1"] = _wgrad_rows(da1, n1, "dw_ffn1_w1", token)
    big["ffn1_w3"] = _wgrad_rows(db1, n1, "dw_ffn1_w3", token)
    big["ffn1_w2"] = _wgrad_rows(hm1, df1, "dw_ffn1_w2", token)
    token = reducer.begin("ffn1", {n: big[n] for n in ("ffn1_w1", "ffn1_w3", "ffn1_w2")})
    dx, small["ffn1_pre_g"] = _ffn_bwd_input(dh1, x, da1, db1, row(sm["ffn1_pre_g"]), wg["ffn1_w1"], wg["ffn1_w3"],
                                             "ffn1_bwd_input", token)
    return sq, dx, big, small


_ANY = pl.BlockSpec(memory_space=pl.ANY)


def _place():
    return lax.axis_index("x"), lax.axis_index("y"), lax.axis_index("c")


def _other_chips(x, y):
    return [(1 - x, y), (x, 1 - y), (1 - x, 1 - y)]


_HBM = pl.BlockSpec(memory_space=pltpu.HBM)
_SEM = pl.BlockSpec(memory_space=pltpu.SEMAPHORE)
_EFFECT = pltpu.SideEffectType.DATAFLOW_SIDE_EFFECTING


def _cast_into_slot(w, chip, name, after=None):
    r, cc = w.shape
    rows = r // 4

    def body(chip_ref, *refs):
        w_ref, o_ref = refs[-2:]
        o_ref[...] = w_ref[...].astype(BF16)

    extra = [] if after is None else [after]
    return pl.pallas_call(
        body, name=name, out_shape=jax.ShapeDtypeStruct((NSH, r, cc), BF16),
        grid_spec=pltpu.PrefetchScalarGridSpec(
            num_scalar_prefetch=1, grid=(4,), in_specs=[_ANY] * len(extra) + [pl.BlockSpec((rows, cc), lambda i, chip: (i, 0))],
            out_specs=pl.BlockSpec((None, rows, cc), lambda i, chip: (chip[0], i, 0))),
        compiler_params=_params("arbitrary"))(chip, *extra, w)


def _piece(ref, slot, c):
    if ref.dtype == F32:
        return ref.at[slot]
    rh = ref.shape[1] // 2
    return ref.at[slot, pl.ds(pl.multiple_of(c * rh, 16), rh), :]


def _gather_start(stages, name):
    flat = [b for stage in stages for b in stage]
    n, ns = len(flat), len(stages)

    def body(*refs):
        ins, sems, token = refs[:n], refs[n:n + 2 * ns], refs[-1]
        x, y, c = _place()
        me = 2 * x + y
        k = 0
        for s, stage in enumerate(stages):
            for i in range(len(stage)):
                for j, (px, py) in enumerate(_other_chips(x, y)):
                    piece = _piece(ins[k], me, c)
                    pltpu.make_async_remote_copy(src_ref=piece, dst_ref=piece, send_sem=sems[2 * s].at[3 * i + j],
                                                 recv_sem=sems[2 * s + 1].at[3 * i + j], device_id=(px, py, c),
                                                 device_id_type=MESH).start()
                k += 1
        token[...] = jnp.zeros_like(token)

    sem_shapes = [pltpu.SemaphoreType.DMA((3 * len(stage),)) for stage in stages for _ in range(2)]
    outs = pl.pallas_call(
        body, name=name, in_specs=[_HBM] * n,
        out_specs=[_SEM] * (2 * ns) + [_HBM] * n + [pl.BlockSpec(memory_space=pltpu.VMEM)],
        out_shape=sem_shapes + [pltpu.HBM(b.shape, b.dtype) for b in flat] + [jax.ShapeDtypeStruct((8, 128), F32)],
        input_output_aliases={i: 2 * ns + i for i in range(n)},
        compiler_params=pltpu.CompilerParams(has_side_effects=_EFFECT),
    )(*[pltpu.with_memory_space_constraint(b, pltpu.HBM) for b in flat])
    sems, bufs, token = outs[:2 * ns], list(outs[2 * ns:2 * ns + n]), outs[-1]
    per_stage, k = [], 0
    for s, stage in enumerate(stages):
        per_stage.append((sems[2 * s], sems[2 * s + 1], bufs[k:k + len(stage)]))
        k += len(stage)
    return per_stage, token


def _gather_wait(send_sems, recv_sems, bufs, after, name):
    n = len(bufs)

    def body(*refs):
        ins, ssem, rsem = refs[:n], refs[n], refs[n + 1]
        x, y, c = _place()
        me = 2 * x + y
        for i in range(n):
            for j, (px, py) in enumerate(_other_chips(x, y)):
                cp = pltpu.make_async_remote_copy(src_ref=_piece(ins[i], me, c), dst_ref=_piece(ins[i], 2 * px + py, c),
                                                  send_sem=ssem.at[3 * i + j], recv_sem=rsem.at[3 * i + j],
                                                  device_id=(px, py, c), device_id_type=MESH)
                cp.wait_send()
                cp.wait_recv()

    afters = list(after) if isinstance(after, (list, tuple)) else [after]
    return pl.pallas_call(
        body, name=name, in_specs=[_HBM] * n + [_SEM, _SEM] + [_ANY] * len(afters), out_specs=[_HBM] * n,
        out_shape=[pltpu.HBM(b.shape, b.dtype) for b in bufs], input_output_aliases={i: i for i in range(n)},
        compiler_params=pltpu.CompilerParams(has_side_effects=_EFFECT),
    )(*bufs, send_sems, recv_sems, *afters)


def _sibling_fill(bufs, name):
    n = len(bufs)

    def body(*refs):
        ins, outs = refs[:n], refs[n:2 * n]
        send_sems, recv_sems = refs[2 * n:]
        x, y, c = _place()
        copies = []
        for i in range(n):
            for j, (px, py) in enumerate(_other_chips(x, y)):
                copies.append(pltpu.make_async_remote_copy(
                    src_ref=_piece(ins[i], 2 * px + py, c), dst_ref=_piece(outs[i], 2 * px + py, c),
                    send_sem=send_sems.at[3 * i + j], recv_sem=recv_sems.at[3 * i + j], device_id=(x, y, 1 - c),
                    device_id_type=MESH))
                copies[-1].start()
        for cp in copies:
            cp.wait()

    return pl.pallas_call(
        body, name=name, in_specs=[_ANY] * n, out_specs=[_ANY] * n,
        out_shape=[jax.ShapeDtypeStruct(b.shape, b.dtype) for b in bufs], input_output_aliases={i: i for i in range(n)},
        scratch_shapes=[pltpu.SemaphoreType.DMA((3 * n,)), pltpu.SemaphoreType.DMA((3 * n,))],
        compiler_params=pltpu.CompilerParams(has_side_effects=True),
    )(*bufs)


def _swap_plan(srcs, lands):
    x, y, c = _place()
    plan = []
    for src, land in zip(srcs, lands):
        rh = src.shape[1] // 2
        plan.append((src.at[:, pl.ds(pl.multiple_of((1 - c) * rh, 16), rh), :], land, (x, y, 1 - c)))
    return plan


def _owners_plan(srcs, lands):
    x, y, c = _place()
    return [(src.at[2 * px + py], land.at[j], (px, py, c))
            for src, land in zip(srcs, lands) for j, (px, py) in enumerate(_other_chips(x, y))]


def _exchange_start(srcs, lands, plan, copies, name):
    n, m = len(srcs), len(srcs) + len(lands)

    def body(*refs):
        send_sems, recv_sems, token = refs[m], refs[m + 1], refs[-1]
        for k, (src, dst, dev) in enumerate(plan(refs[:n], refs[n:m])):
            pltpu.make_async_remote_copy(src_ref=src, dst_ref=dst, send_sem=send_sems.at[k], recv_sem=recv_sems.at[k],
                                         device_id=dev, device_id_type=MESH).start()
        token[...] = jnp.zeros_like(token)

    both = list(srcs) + list(lands)
    outs = pl.pallas_call(
        body, name=name, in_specs=[_HBM] * m,
        out_specs=[_SEM, _SEM] + [_HBM] * m + [pl.BlockSpec(memory_space=pltpu.VMEM)],
        out_shape=[pltpu.SemaphoreType.DMA((copies,)), pltpu.SemaphoreType.DMA((copies,))]
        + [pltpu.HBM(b.shape, b.dtype) for b in both] + [jax.ShapeDtypeStruct((8, 128), F32)],
        input_output_aliases={i: 2 + i for i in range(m)},
        compiler_params=pltpu.CompilerParams(has_side_effects=_EFFECT),
    )(*[pltpu.with_memory_space_constraint(b, pltpu.HBM) for b in both])
    return (outs[0], outs[1]), list(outs[2:2 + n]), list(outs[2 + n:2 + m]), outs[-1]


def _exchange_wait(sems, srcs, lands, plan, after, name):
    n, m = len(srcs), len(srcs) + len(lands)

    def body(*refs):
        send_sems, recv_sems = refs[m], refs[m + 1]
        for k, (src, dst, dev) in enumerate(plan(refs[:n], refs[n:m])):
            cp = pltpu.make_async_remote_copy(src_ref=src, dst_ref=dst, send_sem=send_sems.at[k], recv_sem=recv_sems.at[k],
                                              device_id=dev, device_id_type=MESH)
            cp.wait_send()
            cp.wait_recv()

    both = list(srcs) + list(lands)
    afters = list(after) if isinstance(after, (list, tuple)) else [after]
    outs = pl.pallas_call(
        body, name=name, in_specs=[_HBM] * m + [_SEM, _SEM] + [_ANY] * len(afters), out_specs=[_HBM] * m,
        out_shape=[pltpu.HBM(b.shape, b.dtype) for b in both], input_output_aliases={i: i for i in range(m)},
        compiler_params=pltpu.CompilerParams(has_side_effects=_EFFECT),
    )(*both, sems[0], sems[1], *afters)
    return list(outs[:n]), list(outs[n:])


def _fill_plan(bufs, _):
    x, y, c = _place()
    return [(_piece(buf, 2 * px + py, c), _piece(buf, 2 * px + py, c), (x, y, 1 - c))
            for buf in bufs for px, py in _other_chips(x, y)]


class _Reducer:
    def __init__(self, where):
        self.state = {}
        self.where = where

    def begin(self, stage, grads):
        names = list(grads)
        full = [grads[n] for n in names]
        lands = [lax.empty((NSH, g.shape[1] // 2, g.shape[2]), g.dtype) for g in full]
        sems, full, lands, token = _exchange_start(full, lands, _swap_plan, len(full), "swap_start_" + stage)
        self.state[stage] = (names, sems, full, lands)
        return token

    def advance(self, stage, after):
        names, sems, full, lands = self.state[stage]
        full, got = _exchange_wait(sems, full, lands, _swap_plan, after, "swap_wait_" + stage)
        sums, own = _chip_sums(full, got, self.where, "chip_sums_" + stage)
        lands = [lax.empty((3,) + s.shape[1:], BF16) for s in sums]
        sems, sent, lands, token = _exchange_start(sums, lands, _owners_plan, 3 * len(sums), "owners_start_" + stage)
        self.state[stage] = (names, own, sems, sent, lands)
        return token

    def finish(self, stage, after):
        names, own, sems, sent, lands = self.state[stage]
        _, got = _exchange_wait(sems, sent, lands, _owners_plan, after, "owners_wait_" + stage)
        return dict(zip(names, _owner_sums(own, got, "owner_sums_" + stage)))


def _chip_sums(gs, gots, where, name):
    n = len(gs)

    def body(where_ref, *refs):
        g_refs, got_refs, hb_refs, own_refs = (refs[k * n:(k + 1) * n] for k in range(4))
        mine = pl.program_id(0) == where_ref[1]
        for g_ref, got_ref, hb_ref, own_ref in zip(g_refs, got_refs, hb_refs, own_refs):
            h = g_ref[...].astype(F32) + got_ref[...].astype(F32)
            hb_ref[...] = h.astype(BF16)

            @pl.when(mine)
            def _():
                own_ref[...] = h

    halves = [(g.shape[1] // 2, g.shape[2]) for g in gs]
    slot = [pl.BlockSpec((None, rh, cc), lambda s, where: (s, 0, 0)) for rh, cc in halves]
    outs = pl.pallas_call(
        body, name=name,
        grid_spec=pltpu.PrefetchScalarGridSpec(
            num_scalar_prefetch=1, grid=(NSH,),
            in_specs=[pl.BlockSpec((None, rh, cc), lambda s, where: (s, where[0], 0)) for rh, cc in halves] + slot,
            out_specs=slot + [pl.BlockSpec((rh, cc), lambda s, where: (0, 0)) for rh, cc in halves]),
        out_shape=[jax.ShapeDtypeStruct((NSH, rh, cc), BF16) for rh, cc in halves]
        + [jax.ShapeDtypeStruct((rh, cc), F32) for rh, cc in halves],
        compiler_params=_params("arbitrary"),
    )(where, *gs, *gots)
    return list(outs[:n]), list(outs[n:])


def _owner_sums(owns, gots, name):
    n = len(owns)

    def body(*refs):
        own_refs, got_refs, o_refs = (refs[k * n:(k + 1) * n] for k in range(3))
        for own_ref, got_ref, o_ref in zip(own_refs, got_refs, o_refs):
            o_ref[...] = ((own_ref[...] + got_ref[0].astype(F32)) + got_ref[1].astype(F32)) + got_ref[2].astype(F32)

    blocks = [(o.shape[0] // 2, o.shape[1]) for o in owns]
    rows = [pl.BlockSpec(b, lambda i: (i, 0)) for b in blocks]
    return pl.pallas_call(
        body, grid=(2,), name=name,
        in_specs=rows + [pl.BlockSpec((3,) + b, lambda i: (0, i, 0)) for b in blocks], out_specs=rows,
        out_shape=[jax.ShapeDtypeStruct(o.shape, F32) for o in owns], compiler_params=_params("arbitrary"),
    )(*owns, *gots)


def _sibling_plan(srcs, lands):
    x, y, c = _place()
    return [(src, land, (x, y, 1 - c)) for src, land in zip(srcs, lands)]


def _all_reduce_small(part):
    def body(p_ref, o_ref, rbuf, send1, recv1, send2, recv2):
        x, y, c = _place()
        me = 4 * x + 2 * y + c
        peers = []
        for k in range(1, 8):
            px, py, pc = x ^ ((k >> 2) & 1), y ^ ((k >> 1) & 1), c ^ (k & 1)
            peers.append((k, (px, py, pc), 4 * px + 2 * py + pc))

        def rows(d):
            return pl.ds(pl.multiple_of(d * SMALL_SLICE, 8), SMALL_SLICE)

        first = [pltpu.make_async_remote_copy(src_ref=p_ref.at[rows(idx), :], dst_ref=rbuf.at[me], send_sem=send1.at[k],
                                              recv_sem=recv1.at[k], device_id=dev, device_id_type=MESH)
                 for k, dev, idx in peers]
        for cp in first:
            cp.start()
        rbuf[me] = p_ref[rows(me), :]
        for k, dev, idx in peers:
            pltpu.make_async_remote_copy(src_ref=p_ref.at[rows(idx), :], dst_ref=rbuf.at[idx], send_sem=send1.at[k],
                                         recv_sem=recv1.at[k], device_id=dev, device_id_type=MESH).wait_recv()
        acc = rbuf[0]
        for d in range(1, 8):
            acc = acc + rbuf[d]
        o_ref[rows(me), :] = acc
        second = [pltpu.make_async_remote_copy(src_ref=o_ref.at[rows(me), :], dst_ref=o_ref.at[rows(me), :],
                                               send_sem=send2.at[k], recv_sem=recv2.at[k], device_id=dev, device_id_type=MESH)
                  for k, dev, idx in peers]
        for cp in second:
            cp.start()
        for k, dev, idx in peers:
            pltpu.make_async_remote_copy(src_ref=o_ref.at[rows(me), :], dst_ref=o_ref.at[rows(idx), :], send_sem=send2.at[k],
                                         recv_sem=recv2.at[k], device_id=dev, device_id_type=MESH).wait_recv()
        for cp in first + second:
            cp.wait_send()

    return pl.pallas_call(
        body, name="all_reduce_small", in_specs=[_WHOLE], out_specs=_WHOLE,
        out_shape=jax.ShapeDtypeStruct((SMALL_ROWS, 128), F32),
        scratch_shapes=[pltpu.VMEM((8, SMALL_SLICE, 128), F32)] + [pltpu.SemaphoreType.DMA((8,))] * 4,
        compiler_params=pltpu.CompilerParams(has_side_effects=True),
    )(part)


def _adamw_update(w, gv, m, v):
    nm = ADAM_B1 * m + (1.0 - ADAM_B1) * gv
    nv = ADAM_B2 * v + (1.0 - ADAM_B2) * (gv * gv)
    m_hat = nm / (1.0 - ADAM_B1 ** ADAM_STEP)
    v_hat = nv / (1.0 - ADAM_B2 ** ADAM_STEP)
    return -ADAM_LR * (m_hat / (jnp.sqrt(v_hat) + ADAM_EPS) + ADAM_WD * w), nm, nv


def _adamw_small(ws, gs, ms, vs, after):
    n = len(ws)

    def body(*refs):
        w_refs, g_refs, m_refs, v_refs, d_refs, nm_refs, nv_refs = (refs[k * n:(k + 1) * n] for k in range(7))
        for i in range(n):
            d_refs[i][...], nm_refs[i][...], nv_refs[i][...] = _adamw_update(
                w_refs[i][...], g_refs[i][...], m_refs[i][...], v_refs[i][...])

    out = [jax.ShapeDtypeStruct(w.shape, F32) for w in ws]
    body, specs, operands = _behind(body, after)
    outs = pl.pallas_call(body, in_specs=specs + [_WHOLE] * (4 * n), out_specs=[_WHOLE] * (3 * n), out_shape=out * 3,
                          name="adamw_small", compiler_params=_params())(*operands, *ws, *gs, *ms, *vs)
    return outs[:n], outs[n:2 * n], outs[2 * n:]


def _adamw_halves(ws, mines, theirs, ms, vs, name):
    n = len(ws)
    steps = 2

    def body(*refs):
        w_refs, mine_refs, theirs_refs, m_refs, v_refs, g_refs, d_refs, nm_refs, nv_refs = (
            refs[k * n:(k + 1) * n] for k in range(9))
        is_mine = pl.program_id(0) == lax.axis_index("c")
        for i in range(n):
            gv = jnp.where(is_mine, mine_refs[i][...], theirs_refs[i][...])
            g_refs[i][...] = gv
            d_refs[i][...], nm_refs[i][...], nv_refs[i][...] = _adamw_update(w_refs[i][...], gv, m_refs[i][...], v_refs[i][...])

    blocks = [(h.shape[0] // steps, h.shape[1]) for h in mines]
    whole = [pl.BlockSpec(b, lambda h, i: (steps * h + i, 0)) for b in blocks]
    half = [pl.BlockSpec(b, lambda h, i: (i, 0)) for b in blocks]
    out = [jax.ShapeDtypeStruct(w.shape, F32) for w in ws]
    outs = pl.pallas_call(body, grid=(2, steps), in_specs=whole + half + half + whole + whole, out_specs=whole * 4,
                          out_shape=out * 4, name=name, compiler_params=_params("arbitrary", "arbitrary"),
                          )(*ws, *mines, *theirs, *ms, *vs)
    return [tuple(outs[k * n + i] for k in range(4)) for i in range(n)]


SMALL_USED = sum(size for _, size in SMALL) // 128


def _pack_small(vals, tail=None):
    parts = []
    for name, size in SMALL:
        flat = vals[name].reshape(-1).astype(F32)
        parts.append(jnp.pad(flat, (0, size - flat.shape[0])))
    if tail is not None:
        parts.append(tail.reshape(128))
    flat = jnp.concatenate(parts)
    return jnp.pad(flat, (0, SMALL_ROWS * 128 - flat.shape[0])).reshape(SMALL_ROWS, 128)


def _unpack_small(packed, shapes):
    flat = packed.reshape(-1)
    out, off = {}, 0
    for name, size in SMALL:
        n = math.prod(shapes[name])
        out[name] = flat[off:off + n].reshape(shapes[name])
        off += size
    return out


def kernel(x, ffn1_pre_g, ffn1_w1, ffn1_w3, ffn1_w2, ffn1_post_g, mix_pre_g, w_in, conv_w, conv_b, rg_a_w, rg_a_b, rg_x_w, rg_x_b, lru_lambda, w_lru_out, attn_sinks, rel_bias, w_attn_out, w_gate, b_gate, w_o, mix_post_g, ffn2_pre_g, ffn2_w1, ffn2_w3, ffn2_w2, ffn2_post_g, loss_target, m_ffn1_pre_g, m_ffn1_w1, m_ffn1_w3, m_ffn1_w2, m_ffn1_post_g, m_mix_pre_g, m_w_in, m_conv_w, m_conv_b, m_rg_a_w, m_rg_a_b, m_rg_x_w, m_rg_x_b, m_lru_lambda, m_w_lru_out, m_attn_sinks, m_rel_bias, m_w_attn_out, m_w_gate, m_b_gate, m_w_o, m_mix_post_g, m_ffn2_pre_g, m_ffn2_w1, m_ffn2_w3, m_ffn2_w2, m_ffn2_post_g, v_ffn1_pre_g, v_ffn1_w1, v_ffn1_w3, v_ffn1_w2, v_ffn1_post_g, v_mix_pre_g, v_w_in, v_conv_w, v_conv_b, v_rg_a_w, v_rg_a_b, v_rg_x_w, v_rg_x_b, v_lru_lambda, v_w_lru_out, v_attn_sinks, v_rel_bias, v_w_attn_out, v_w_gate, v_b_gate, v_w_o, v_mix_post_g, v_ffn2_pre_g, v_ffn2_w1, v_ffn2_w3, v_ffn2_w2, v_ffn2_post_g):
    given = dict(locals())
    chip = 2 * lax.axis_index("x") + lax.axis_index("y")
    transposed = ("ffn1_w1", "ffn1_w3", "ffn2_w1", "ffn2_w3")

    def shard(name, moment=""):
        w = given[moment + name][0]
        return w.T if name in transposed else w

    def unshard(name, w):
        return (w.T if name in transposed else w)[None]

    def only_my_columns(a):
        parts = a.reshape(1, 4, NSH, D // NSH)
        return sum(jnp.where(chip == s, parts[:, :, s], 0.0) for s in range(NSH))

    chip_arr = jnp.reshape(chip, (1,)).astype(jnp.int32)
    stage_names = {"ffn1_up": ["ffn1_w1", "ffn1_w3", "conv_w"],
                   "ffn1_down": ["ffn1_w2"],
                   "mix_in": ["w_in", "w_gate"],
                   "mix_out": ["w_lru_out", "w_attn_out", "w_o"],
                   "ffn2": ["ffn2_w1", "ffn2_w3", "ffn2_w2"]}
    in_flight, started = {}, None
    for stage, names in stage_names.items():
        bufs = [jnp.where(lax.broadcasted_iota(jnp.int32, (NSH, 4, D // NSH), 0) == chip, given[n], 0.0) if n == "conv_w"
                else _cast_into_slot(shard(n), chip_arr, "cast_" + n, started) for n in names]
        (in_flight[stage],), started = _gather_start([bufs], "gather_start_" + stage)
    all_started = started

    filling = {}

    def weights(stage, after, begin=False):
        names = stage_names[stage]
        halves_of = [n for n in names if n != "conv_w"]
        if stage in filling:
            filled, _ = _exchange_wait(filling.pop(stage), *filling.pop(stage + "/bufs"), _fill_plan, after,
                                       "fill_wait_" + stage)
            return dict(zip(halves_of, filled))
        send_sems, recv_sems, landing = in_flight[stage]
        if stage == "ffn1_up":
            after = [all_started] + list(after)
        landed = dict(zip(names, _gather_wait(send_sems, recv_sems, landing, after, "gather_wait_" + stage)))
        halves = [landed[n] for n in halves_of]
        if begin:
            filling[stage], bufs, _, token = _exchange_start(halves, [], _fill_plan, 3 * len(halves), "fill_start_" + stage)
            filling[stage + "/bufs"] = (bufs, [])
            return token
        out = dict(zip(halves_of, _sibling_fill(halves, "sibling_fill_" + stage)))
        if "conv_w" in names:
            out["conv_w"] = jnp.transpose(landed["conv_w"], (1, 0, 2)).reshape(4, D)
        return out

    small_shapes = {n: given[n].shape for n, _ in SMALL}
    small_shapes["conv_w"] = (1, 4, D)
    sm = {n: (given[n][0] if given[n].shape[0] == 1 and n != "rel_bias" else given[n]) for n, _ in SMALL if n != "conv_w"}

    reducer = _Reducer(jnp.stack([lax.axis_index("c"), chip]).astype(jnp.int32))
    sq, dx, _, small = _local_step(x[0], loss_target[0], weights, sm, reducer)

    reduced_small = _all_reduce_small(_pack_small(small, tail=sq))
    last_started = reducer.advance("ffn1", [dx, reduced_small])
    loss = reduced_small[SMALL_USED, 0] * (0.5 / D)
    small_g = _unpack_small(reduced_small, small_shapes)
    grads, delta, new_m, new_v = {}, {}, {}, {}
    in_transit = {}

    def send(stage, after):
        halves = reducer.finish(stage, after)
        lands = [lax.empty(h.shape, F32) for h in halves.values()]
        sems, mine, lands, token = _exchange_start(list(halves.values()), lands, _sibling_plan, len(lands),
                                                   "halves_start_" + stage)
        in_transit[stage] = (list(halves), sems, mine, lands)
        return token

    def update(stage, after):
        names, sems, mine, lands = in_transit[stage]
        mine, theirs = _exchange_wait(sems, mine, lands, _sibling_plan, after, "halves_wait_" + stage)
        updated = _adamw_halves([shard(n) for n in names], mine, theirs, [shard(n, "m_") for n in names],
                                [shard(n, "v_") for n in names], "adamw_" + stage)
        for n, results in zip(names, updated):
            grads[n], delta[n], new_m[n], new_v[n] = (unshard(n, r) for r in results)
        return new_v[names[-1]]

    token = send("ffn2", [reduced_small, last_started])
    token = send("mix", token)
    done = update("ffn2", token)
    done = update("mix", done)
    token = send("ffn1", done)
    update("ffn1", token)

    small_g["conv_w"] = only_my_columns(small_g["conv_w"])
    names = [n for n, _ in SMALL]
    flat2d = lambda a: a.reshape(-1, a.shape[-1])
    outs = _adamw_small(*[[flat2d(given[pre + n]) if pre != "g" else flat2d(small_g[n]) for n in names]
                          for pre in ("", "g", "m_", "v_")], after=last_started)
    for dst, arrs in zip((delta, new_m, new_v), outs):
        dst.update({n: a.reshape(given[n].shape) for n, a in zip(names, arrs)})
    grads.update(small_g)
    return (loss, dx[None], *[grads[n] for n in WEIGHTS], *[delta[n] for n in WEIGHTS], *[new_m[n] for n in WEIGHTS],
            *[new_v[n] for n in WEIGHTS])
```

```python
import math

import jax
import jax.numpy as jnp
from jax import lax
from jax.experimental import pallas as pl
from jax.experimental.pallas import tpu as pltpu

F32, BF16 = jnp.float32, jnp.bfloat16
D = 1024
NSH = 4
FF_S = 704
IN_S = 896
GATE_S = 512
KV_W = 256
CHUNK = 64
KB = 192
N_HEADS = 16
HEAD_DIM = 64
N_BUCKETS = 32
KP = 192
PAD_KEYS = 128
RMS_EPS = 1e-6
NEG_INF = -1e30
LRU_C = 8.0
TM = 512
TM_SCAN = 256
VMEM_LIMIT = 56 * 1024 * 1024
ADAM_LR, ADAM_B1, ADAM_B2, ADAM_EPS, ADAM_WD, ADAM_STEP = 0.001, 0.9, 0.999, 1e-08, 0.01, 10
SMALL_ROWS = 1216
SMALL_SLICE = SMALL_ROWS // 8
MESH = pl.DeviceIdType.MESH

BIG = ["ffn1_w1", "ffn1_w3", "ffn1_w2", "w_in", "w_lru_out", "w_attn_out", "w_gate", "w_o", "ffn2_w1", "ffn2_w3", "ffn2_w2"]
SMALL = [("ffn1_pre_g", 1024), ("ffn1_post_g", 1024), ("mix_pre_g", 1024), ("conv_w", 4096), ("conv_b", 1024),
         ("rg_a_w", 65536), ("rg_a_b", 1024), ("rg_x_w", 65536), ("rg_x_b", 1024), ("lru_lambda", 1024),
         ("attn_sinks", 1024), ("rel_bias", 1024), ("b_gate", 2048), ("mix_post_g", 1024), ("ffn2_pre_g", 1024),
         ("ffn2_post_g", 1024)]
WEIGHTS = ["ffn1_pre_g", "ffn1_w1", "ffn1_w3", "ffn1_w2", "ffn1_post_g", "mix_pre_g", "w_in", "conv_w", "conv_b", "rg_a_w",
           "rg_a_b", "rg_x_w", "rg_x_b", "lru_lambda", "w_lru_out", "attn_sinks", "rel_bias", "w_attn_out", "w_gate", "b_gate",
           "w_o", "mix_post_g", "ffn2_pre_g", "ffn2_w1", "ffn2_w3", "ffn2_w2", "ffn2_post_g"]


def _params(*sem):
    return pltpu.CompilerParams(dimension_semantics=sem or None, vmem_limit_bytes=VMEM_LIMIT)


def _nn(a, b):
    return jnp.dot(a, b, preferred_element_type=F32)


def _nt(a, b):
    return lax.dot_general(a, b, (((1,), (1,)), ((), ())), preferred_element_type=F32)


def _tn(a, b):
    return lax.dot_general(a, b, (((0,), (0,)), ((), ())), preferred_element_type=F32)


def _rms(x, g):
    rstd = lax.rsqrt(jnp.mean(x * x, axis=-1, keepdims=True) + RMS_EPS)
    return (x * rstd) * g


def _rms_bwd(dout, x, g):
    rstd = lax.rsqrt(jnp.mean(x * x, axis=-1, keepdims=True) + RMS_EPS)
    xhat = x * rstd
    dg = jnp.sum(dout * xhat, axis=0, keepdims=True)
    dxhat = dout * g
    dx = rstd * (dxhat - xhat * jnp.mean(dxhat * xhat, axis=-1, keepdims=True))
    return dx, dg


_GELU_K = math.sqrt(2.0 / math.pi)


_GELU_C = 0.044715 * _GELU_K


def _gelu_and_grad(x):
    x2 = x * x
    t = jnp.tanh(x * (_GELU_K + _GELU_C * x2))
    cdf = 0.5 + 0.5 * t
    return x * cdf, cdf + (x * (_GELU_K + (3.0 * _GELU_C) * x2)) * (0.5 - 0.5 * (t * t))


def _softplus_neg(lam):
    z = -lam
    u = jnp.exp(-jnp.abs(z))
    w = 1.0 + u
    log1p_u = jnp.where(w == 1.0, u, jnp.log(w) * (u / (w - 1.0)))
    return jnp.maximum(z, 0.0) + log1p_u


def _lru_coeffs(r, sp):
    log_a = (-LRU_C * r) * sp
    a = jnp.exp(log_a)
    t = jnp.tanh(log_a)
    s = jnp.sqrt(-2.0 * t / (1.0 - t))
    return a, s


def _row_spec(tm, width, buffers=None):
    if buffers is None:
        return pl.BlockSpec((tm, width), lambda i: (i, 0))
    return pl.BlockSpec((tm, width), lambda i: (i, 0), pipeline_mode=pl.Buffered(buffers))


def _vec_spec(width):
    return pl.BlockSpec((1, width), lambda i: (0, 0))


_WHOLE = pl.BlockSpec(memory_space=pltpu.VMEM)


def _tile(t, tm=TM):
    return min(tm, t)


def _ffn_fwd(x, gpre, w1g, w3g, w2g, gpost, name, target=None):
    t = x.shape[0]
    tm = _tile(t)
    last = target is not None

    def body(x_ref, gpre_ref, w1_ref, w3_ref, w2_ref, gpost_ref, *refs):
        t_ref, (h_ref, a_ref, b_ref, hm_ref, f_ref), l_ref = (refs[0] if last else None), refs[last:last + 5], refs[-1]
        xv = x_ref[...]
        nb = _rms(xv, gpre_ref[...]).astype(BF16)
        f = jnp.zeros((tm, D), F32)
        for s in range(NSH):
            a = _nt(nb, w1_ref[s])
            b = _nt(nb, w3_ref[s])
            hmb = ((a * jax.nn.sigmoid(a)) * b).astype(BF16)
            a_ref[s] = a.astype(BF16)
            b_ref[s] = b.astype(BF16)
            hm_ref[s] = hmb
            f = f + _nn(hmb, w2_ref[s])
        f_ref[...] = f
        h = xv + 0.5 * _rms(f, gpost_ref[...])
        if last:
            @pl.when(pl.program_id(0) == 0)
            def _():
                l_ref[...] = jnp.zeros_like(l_ref)

            e = h - t_ref[...]
            h_ref[...] = e * (1.0 / D)
            l_ref[...] += jnp.sum(jnp.sum(e * e, axis=0, keepdims=True), axis=1, keepdims=True)
        else:
            h_ref[...] = h

    sh = pl.BlockSpec((NSH, tm, FF_S), lambda i: (0, i, 0))
    act = jax.ShapeDtypeStruct((NSH, t, FF_S), BF16)
    return pl.pallas_call(
        body, grid=(t // tm,), name=name,
        in_specs=[_row_spec(tm, D), _vec_spec(D), _WHOLE, _WHOLE, _WHOLE, _vec_spec(D)] + [_row_spec(tm, D)] * last,
        out_specs=[_row_spec(tm, D), sh, sh, sh, _row_spec(tm, D)] + [pl.BlockSpec((1, 128), lambda i: (0, 0))] * last,
        out_shape=[jax.ShapeDtypeStruct((t, D), F32), act, act, act, jax.ShapeDtypeStruct((t, D), F32)]
        + [jax.ShapeDtypeStruct((1, 128), F32)] * last,
        compiler_params=_params("arbitrary"),
    )(x, gpre, w1g, w3g, w2g, gpost, *([target] if last else []))


def _ffn_up(x, gpre, w1g, w3g, name):
    t = x.shape[0]
    tm = _tile(t)

    def body(x_ref, gpre_ref, w1_ref, w3_ref, n_ref, a_ref, b_ref, hm_ref):
        nb = _rms(x_ref[...], gpre_ref[...]).astype(BF16)
        n_ref[...] = nb
        for s in range(NSH):
            a = _nt(nb, w1_ref[s])
            b = _nt(nb, w3_ref[s])
            a_ref[s] = a.astype(BF16)
            b_ref[s] = b.astype(BF16)
            hm_ref[s] = ((a * jax.nn.sigmoid(a)) * b).astype(BF16)

    sh = pl.BlockSpec((NSH, tm, FF_S), lambda i: (0, i, 0))
    act = jax.ShapeDtypeStruct((NSH, t, FF_S), BF16)
    return pl.pallas_call(
        body, grid=(t // tm,), name=name, in_specs=[_row_spec(tm, D), _vec_spec(D), _WHOLE, _WHOLE],
        out_specs=[_row_spec(tm, D), sh, sh, sh], out_shape=[jax.ShapeDtypeStruct((t, D), BF16), act, act, act],
        compiler_params=_params("arbitrary"),
    )(x, gpre, w1g, w3g)


def _ffn_down(x, hm, w2g, gpost, name):
    t = x.shape[0]
    tm = _tile(t)

    def body(x_ref, hm_ref, w2_ref, gpost_ref, h_ref, f_ref):
        f = jnp.zeros((tm, D), F32)
        for s in range(NSH):
            f = f + _nn(hm_ref[s], w2_ref[s])
        f_ref[...] = f
        h_ref[...] = x_ref[...] + 0.5 * _rms(f, gpost_ref[...])

    sh = pl.BlockSpec((NSH, tm, FF_S), lambda i: (0, i, 0))
    f32 = jax.ShapeDtypeStruct((t, D), F32)
    return pl.pallas_call(
        body, grid=(t // tm,), name=name, in_specs=[_row_spec(tm, D), sh, _WHOLE, _vec_spec(D)],
        out_specs=[_row_spec(tm, D), _row_spec(tm, D)], out_shape=[f32, f32], compiler_params=_params("arbitrary"),
    )(x, hm, w2g, gpost)


def _mix_proj(h1, gmix, w_in_g, w_gate_g, b_gate):
    t = h1.shape[0]
    tm = _tile(t)

    def body(h_ref, g_ref, win_ref, wg_ref, bg_ref, u_ref, q_ref, k_ref, v_ref, xr_ref, xg_ref, gate_ref):
        ub = _rms(h_ref[...], g_ref[...]).astype(BF16)
        u_ref[...] = ub
        p0 = _nn(ub, win_ref[0])
        q_ref[:, 0:896] = p0.astype(BF16)
        p1 = _nn(ub, win_ref[1])
        q_ref[:, 896:1024] = p1[:, 0:128].astype(BF16)
        k_ref[...] = p1[:, 128:384].astype(BF16)
        v_ref[...] = p1[:, 384:640].astype(BF16)
        xr_ref[:, 0:256] = p1[:, 640:896]
        p2 = _nn(ub, win_ref[2])
        xr_ref[:, 256:1024] = p2[:, 0:768]
        xg_ref[:, 0:128] = p2[:, 768:896].astype(BF16)
        xg_ref[:, 128:1024] = _nn(ub, win_ref[3]).astype(BF16)
        for s in range(NSH):
            sl = slice(s * GATE_S, (s + 1) * GATE_S)
            gate_ref[:, sl] = jax.nn.sigmoid(_nn(ub, wg_ref[s]) + bg_ref[:, sl]).astype(BF16)

    return pl.pallas_call(
        body, grid=(t // tm,), name="mix_proj",
        in_specs=[_row_spec(tm, D), _vec_spec(D), _WHOLE, _WHOLE, _vec_spec(2 * D)],
        out_specs=[_row_spec(tm, D), _row_spec(tm, D), _row_spec(tm, KV_W), _row_spec(tm, KV_W), _row_spec(tm, D),
                   _row_spec(tm, D), _row_spec(tm, 2 * D)],
        out_shape=[jax.ShapeDtypeStruct((t, D), BF16), jax.ShapeDtypeStruct((t, D), BF16),
                   jax.ShapeDtypeStruct((t, KV_W), BF16), jax.ShapeDtypeStruct((t, KV_W), BF16),
                   jax.ShapeDtypeStruct((t, D), F32), jax.ShapeDtypeStruct((t, D), BF16),
                   jax.ShapeDtypeStruct((t, 2 * D), BF16)],
        compiler_params=_params("arbitrary"),
    )(h1, gmix, w_in_g, w_gate_g, b_gate)


def _rglru_fwd(xr, xg, conv_w, conv_b, wa2, ba, wx2, bx, lam, after=None):
    t = xr.shape[0]
    tm = _tile(t, TM_SCAN)
    nb8 = tm // 8

    def body(xr_ref, xrp_ref, xg_ref, cw_ref, cb_ref, wa_ref, ba_ref, wx_ref, bx_ref, lam_ref,
             hr_ref, yain_ref, xc_ref, r_ref, ig_ref, a_sc, s_ref, ext, h_sc):
        i = pl.program_id(0)

        @pl.when(i == 0)
        def _():
            h_sc[...] = jnp.zeros_like(h_sc)

        ext[0:8, :] = jnp.where(i == 0, 0.0, xrp_ref[...])
        ext[8:8 + tm, :] = xr_ref[...]
        xc = jnp.broadcast_to(cb_ref[...], (tm, D))
        for tap in range(4):
            xc = xc + ext[pl.ds(5 + tap, tm), :] * cw_ref[tap:tap + 1, :]
        xc_ref[...] = xc
        xcb = xc.astype(BF16)
        for p in range(8):
            sl = slice(p * 128, (p + 1) * 128)
            r_ref[:, sl] = jax.nn.sigmoid(_nn(xcb[:, sl], wa_ref[p]) + ba_ref[:, sl])
            ig_ref[:, sl] = jax.nn.sigmoid(_nn(xcb[:, sl], wx_ref[p]) + bx_ref[:, sl])
        a, s = _lru_coeffs(r_ref[...], _softplus_neg(lam_ref[...]))
        a_sc[...] = a
        s_ref[...] = s
        hr_ref[...] = s * (ig_ref[...] * xc)

        def blk(j, h):
            st = pl.multiple_of(j * 8, 8)
            a8 = a_sc[pl.ds(st, 8), :]
            u8 = hr_ref[pl.ds(st, 8), :]
            rows = []
            for k in range(8):
                h = a8[k:k + 1, :] * h + u8[k:k + 1, :]
                rows.append(h)
            hr_ref[pl.ds(st, 8), :] = jnp.concatenate(rows, axis=0)
            return h

        h_sc[0:1, :] = lax.fori_loop(0, nb8, blk, h_sc[0:1, :])
        yain_ref[...] = (hr_ref[...] * _gelu_and_grad(xg_ref[...].astype(F32))[0]).astype(BF16)

    prev = pl.BlockSpec((8, D), lambda i: (jnp.maximum(i * nb8 - 1, 0), 0))
    full = lambda shape: pl.BlockSpec(shape, lambda i: tuple(0 for _ in shape))
    f32 = jax.ShapeDtypeStruct((t, D), F32)
    body, specs, operands = _behind(body, after)
    return pl.pallas_call(
        body, grid=(t // tm,), name="rglru_fwd",
        in_specs=specs + [_row_spec(tm, D), prev, _row_spec(tm, D), full((4, D)), _vec_spec(D), full((8, 128, 128)),
                          _vec_spec(D), full((8, 128, 128)), _vec_spec(D), _vec_spec(D)],
        out_specs=[_row_spec(tm, D)] * 7,
        out_shape=[f32, jax.ShapeDtypeStruct((t, D), BF16), f32, f32, f32, f32, f32],
        scratch_shapes=[pltpu.VMEM((tm + 8, D), F32), pltpu.VMEM((8, D), F32)],
        compiler_params=_params("arbitrary"),
    )(*operands, xr, xr, xg, conv_w, conv_b, wa2, ba, wx2, bx, lam)


def _bias_fwd(table_t, onehot_t):
    def body(t_ref, e_ref, o_ref):
        o_ref[...] = jnp.dot(t_ref[...], e_ref[...], preferred_element_type=F32, precision=lax.Precision.HIGHEST)

    return pl.pallas_call(body, out_shape=jax.ShapeDtypeStruct((N_HEADS, CHUNK * KB), F32), name="bias_fwd",
                          compiler_params=_params())(table_t, onehot_t)


def _bias_bwd(dbias_flat, onehot_t, ds_rows):
    def body(d_ref, e_ref, s_ref, o_ref, so_ref):
        o_ref[...] = lax.dot_general(d_ref[...], e_ref[...], (((1,), (1,)), ((), ())), preferred_element_type=F32,
                                     precision=lax.Precision.HIGHEST)
        so_ref[...] = jnp.zeros_like(so_ref)
        for r in range(4):
            so_ref[:, r:r + 1] = jnp.sum(s_ref[:, r * CHUNK:(r + 1) * CHUNK], axis=1, keepdims=True)

    return pl.pallas_call(body, out_shape=[jax.ShapeDtypeStruct((N_HEADS, N_BUCKETS), F32), jax.ShapeDtypeStruct((8, 128), F32)],
                          name="bias_bwd", compiler_params=_params())(dbias_flat, onehot_t, ds_rows)


def _stack_heads(q):
    return jnp.concatenate(
        [jnp.concatenate([q[:, (4 * g + r) * HEAD_DIM:(4 * g + r + 1) * HEAD_DIM] for g in range(4)], axis=1)
         for r in range(4)], axis=0)


def _unstack_heads(o):
    return jnp.concatenate([o[r * CHUNK:(r + 1) * CHUNK, g * HEAD_DIM:(g + 1) * HEAD_DIM] for g in range(4) for r in range(4)],
                           axis=1)


def _block_diag(w, mask):
    return jnp.concatenate([w] * 4, axis=0) * mask


def _group_softmax(qk, bias_g, sink, valid):
    s = qk * (HEAD_DIM ** -0.5) + bias_g
    s = jnp.where(valid, s, NEG_INF)
    m = jnp.maximum(jnp.max(s, axis=0, keepdims=True), sink)
    e = jnp.exp(s - m)
    es = jnp.exp(sink - m)
    inv = 1.0 / (jnp.sum(e, axis=0, keepdims=True) + es)
    return e * inv, es * inv


def _attn_fwd(sink_rows, q, kp, vp, bias_t, mask, after=None):
    t = q.shape[0]
    per_step = 8

    def body(sink_ref, q_ref, kp_ref, vp_ref, bias_ref, mask_ref, o_ref):
        owns = [mask_ref[g * KP:(g + 1) * KP, :] for g in range(4)]
        for k in range(per_step):
            c = pl.program_id(0) * per_step + k
            rows = slice(k * CHUNK, (k + 1) * CHUNK)
            st = pl.multiple_of(c * CHUNK, CHUNK)
            kw = kp_ref[pl.ds(st, KP), :]
            vw = vp_ref[pl.ds(st, KP), :]
            q_all = _stack_heads(q_ref[rows, :])
            valid = lax.broadcasted_iota(jnp.int32, (KP, 1), 0) + c * CHUNK >= PAD_KEYS
            scores = [_nt(kw * owns[g], q_all) for g in range(4)]
            ps = [_group_softmax(scores[g], bias_ref[g * KP:(g + 1) * KP, :], sink_ref[g:g + 1, :], valid)[0]
                  for g in range(4)]
            o_all = sum(_tn(ps[g].astype(BF16), vw * owns[g]) for g in range(4))
            o_ref[rows, :] = _unstack_heads(o_all).astype(BF16)

    body, specs, operands = _behind(body, after)
    return pl.pallas_call(
        body, grid=(t // (per_step * CHUNK),), name="attn_fwd",
        in_specs=specs + [_WHOLE, _row_spec(per_step * CHUNK, D), _WHOLE, _WHOLE, _WHOLE, _WHOLE],
        out_specs=_row_spec(per_step * CHUNK, D),
        out_shape=jax.ShapeDtypeStruct((t, D), BF16),
        compiler_params=_params("arbitrary"),
    )(*operands, sink_rows, q, kp, vp, bias_t, mask)


def _merge_fwd(yain, o, gate, h1, w_lru, w_att, w_o, gpost):
    t = h1.shape[0]
    tm = _tile(t)

    def body(ya_ref, o_ref, g_ref, h_ref, wl_ref, wa_ref, wo_ref, gp_ref, h2_ref, mo_ref, mg_ref, ya_out, yb_out):
        ya = _nn(ya_ref[...], wl_ref[...])
        yb = _nn(o_ref[...], wa_ref[...])
        g0 = g_ref[:, 0:D].astype(F32)
        g1 = g_ref[:, D:2 * D].astype(F32)
        mg = (g0 * ya + g1 * yb).astype(BF16)
        mo = _nn(mg, wo_ref[...])
        ya_out[...] = (ya * (g0 * (1.0 - g0))).astype(BF16)
        yb_out[...] = (yb * (g1 * (1.0 - g1))).astype(BF16)
        mg_ref[...] = mg
        mo_ref[...] = mo
        h2_ref[...] = h_ref[...] + _rms(mo, gp_ref[...])

    f32 = jax.ShapeDtypeStruct((t, D), F32)
    b16 = jax.ShapeDtypeStruct((t, D), BF16)
    return pl.pallas_call(
        body, grid=(t // tm,), name="merge_fwd",
        in_specs=[_row_spec(tm, D), _row_spec(tm, D), _row_spec(tm, 2 * D), _row_spec(tm, D), _WHOLE, _WHOLE, _WHOLE,
                  _vec_spec(D)],
        out_specs=[_row_spec(tm, D)] * 5,
        out_shape=[f32, f32, b16, b16, b16],
        compiler_params=_params("arbitrary"),
    )(yain, o, gate, h1, w_lru, w_att, w_o, gpost)


def _ffn_bwd(dh, x, f, a, b, gpre, gpost, w1g, w3g, w2g, name):
    t = x.shape[0]
    tm = _tile(t, TM_SCAN)

    def body(dh_ref, x_ref, f_ref, a_ref, b_ref, gpre_ref, gpost_ref, w1_ref, w3_ref, w2_ref,
             dx_ref, n_ref, da_ref, db_ref, df_ref, dgpre_ref, dgpost_ref):
        @pl.when(pl.program_id(0) == 0)
        def _():
            dgpre_ref[...] = jnp.zeros_like(dgpre_ref)
            dgpost_ref[...] = jnp.zeros_like(dgpost_ref)

        dhv = dh_ref[...]
        xv = x_ref[...]
        df, dgp = _rms_bwd(0.5 * dhv, f_ref[...], gpost_ref[...])
        dgpost_ref[...] += dgp
        dfb = df.astype(BF16)
        df_ref[...] = dfb
        n_ref[...] = _rms(xv, gpre_ref[...]).astype(BF16)
        dn = jnp.zeros((tm, D), F32)
        for s in range(NSH):
            av = a_ref[s].astype(F32)
            bv = b_ref[s].astype(F32)
            sg = jax.nn.sigmoid(av)
            dhm = _nt(dfb, w2_ref[s])
            dab = (dhm * bv * (sg * (1.0 + av * (1.0 - sg)))).astype(BF16)
            dbb = (dhm * (av * sg)).astype(BF16)
            da_ref[s] = dab
            db_ref[s] = dbb
            dn = dn + _nn(dab, w1_ref[s]) + _nn(dbb, w3_ref[s])
        dxn, dg = _rms_bwd(dn, xv, gpre_ref[...])
        dgpre_ref[...] += dg
        dx_ref[...] = dhv + dxn

    sh = pl.BlockSpec((NSH, tm, FF_S), lambda i: (0, i, 0))
    act = jax.ShapeDtypeStruct((NSH, t, FF_S), BF16)
    vec = jax.ShapeDtypeStruct((1, D), F32)
    return pl.pallas_call(
        body, grid=(t // tm,), name=name,
        in_specs=[_row_spec(tm, D), _row_spec(tm, D), _row_spec(tm, D), sh, sh, _vec_spec(D), _vec_spec(D), _WHOLE, _WHOLE,
                  _WHOLE],
        out_specs=[_row_spec(tm, D), _row_spec(tm, D), sh, sh, _row_spec(tm, D), _vec_spec(D), _vec_spec(D)],
        out_shape=[jax.ShapeDtypeStruct((t, D), F32), jax.ShapeDtypeStruct((t, D), BF16), act, act,
                   jax.ShapeDtypeStruct((t, D), BF16), vec, vec],
        compiler_params=_params("arbitrary"),
    )(dh, x, f, a, b, gpre, gpost, w1g, w3g, w2g)


def _behind(body, after):
    if after is None:
        return body, [], []

    def ordered(_, *refs):
        body(*refs)

    return ordered, [_ANY], [after]


def _ffn_bwd_acts(dh, f, a, b, gpost, w2g, name):
    t = dh.shape[0]
    tm = _tile(t)

    def body(dh_ref, f_ref, a_ref, b_ref, gpost_ref, w2_ref, da_ref, db_ref, df_ref, dgpost_ref):
        @pl.when(pl.program_id(0) == 0)
        def _():
            dgpost_ref[...] = jnp.zeros_like(dgpost_ref)

        df, dgp = _rms_bwd(0.5 * dh_ref[...], f_ref[...], gpost_ref[...])
        dgpost_ref[...] += dgp
        dfb = df.astype(BF16)
        df_ref[...] = dfb
        for s in range(NSH):
            av = a_ref[s].astype(F32)
            bv = b_ref[s].astype(F32)
            sg = jax.nn.sigmoid(av)
            dhm = _nt(dfb, w2_ref[s])
            da_ref[s] = (dhm * bv * (sg * (1.0 + av * (1.0 - sg)))).astype(BF16)
            db_ref[s] = (dhm * (av * sg)).astype(BF16)

    sh = pl.BlockSpec((NSH, tm, FF_S), lambda i: (0, i, 0))
    act = jax.ShapeDtypeStruct((NSH, t, FF_S), BF16)
    b16 = jax.ShapeDtypeStruct((t, D), BF16)
    return pl.pallas_call(
        body, grid=(t // tm,), name=name,
        in_specs=[_row_spec(tm, D), _row_spec(tm, D), sh, sh, _vec_spec(D), _WHOLE],
        out_specs=[sh, sh, _row_spec(tm, D), _vec_spec(D)],
        out_shape=[act, act, b16, jax.ShapeDtypeStruct((1, D), F32)],
        compiler_params=_params("arbitrary"),
    )(dh, f, a, b, gpost, w2g)


def _ffn_bwd_input(dh, x, da, db, gpre, w1g, w3g, name, after):
    t = x.shape[0]
    tm = _tile(t)

    def body(dh_ref, x_ref, da_ref, db_ref, gpre_ref, w1_ref, w3_ref, dx_ref, dgpre_ref):
        @pl.when(pl.program_id(0) == 0)
        def _():
            dgpre_ref[...] = jnp.zeros_like(dgpre_ref)

        dn = jnp.zeros((tm, D), F32)
        for s in range(NSH):
            dn = dn + _nn(da_ref[s], w1_ref[s]) + _nn(db_ref[s], w3_ref[s])
        dxn, dg = _rms_bwd(dn, x_ref[...], gpre_ref[...])
        dgpre_ref[...] += dg
        dx_ref[...] = dh_ref[...] + dxn

    sh = pl.BlockSpec((NSH, tm, FF_S), lambda i: (0, i, 0))
    body, specs, operands = _behind(body, after)
    return pl.pallas_call(
        body, grid=(t // tm,), name=name,
        in_specs=specs + [_row_spec(tm, D), _row_spec(tm, D), sh, sh, _vec_spec(D), _WHOLE, _WHOLE],
        out_specs=[_row_spec(tm, D), _vec_spec(D)],
        out_shape=[jax.ShapeDtypeStruct((t, D), F32), jax.ShapeDtypeStruct((1, D), F32)],
        compiler_params=_params("arbitrary"),
    )(*operands, dh, x, da, db, gpre, w1g, w3g)


def _wgrad(a, b, a_spec, b_spec, out_spec, out_shape, grid, name, after=None):
    def body(a_ref, b_ref, o_ref):
        o_ref[...] = _tn(a_ref[...], b_ref[...]).astype(BF16)

    body, specs, operands = _behind(body, after)
    return pl.pallas_call(body, grid=grid, name=name, in_specs=specs + [a_spec, b_spec], out_specs=out_spec,
                          out_shape=jax.ShapeDtypeStruct(out_shape, BF16),
                          compiler_params=_params(*("arbitrary",) * len(grid)))(*operands, a, b)


def _wgrad_cols(act, dsh, width, name, after=None):
    t = act.shape[0]
    if dsh.ndim == 3:
        b_spec = pl.BlockSpec((None, t, width), lambda s, k: (s, 0, 0))
    else:
        b_spec = pl.BlockSpec((t, width), lambda s, k: (0, s))
    return _wgrad(act, dsh, pl.BlockSpec((t, 512), lambda s, k: (0, k)), b_spec,
                  pl.BlockSpec((None, 512, width), lambda s, k: (s, k, 0)), (NSH, D, width), (NSH, 2), name, after)


def _wgrad_rows(hm, df, name, after=None):
    t = df.shape[0]
    return _wgrad(hm, df, pl.BlockSpec((None, t, FF_S), lambda s: (s, 0, 0)), pl.BlockSpec((t, D), lambda s: (0, 0)),
                  pl.BlockSpec((None, FF_S, D), lambda s: (s, 0, 0)), (NSH, FF_S, D), (NSH,), name, after)


def _wgrad_sq(a, b, name, after=None):
    t = a.shape[0]
    return _wgrad(a, b, pl.BlockSpec((t, D), lambda j: (0, 0)), pl.BlockSpec((t, 512), lambda j: (0, j)),
                  pl.BlockSpec((D, 512), lambda j: (0, j)), (D, D), (2,), name, after)


def _mix_bwd1(dh2, mo, gpost, gate, ya, yb, xg, hr, w_o, w_lru, w_att, after):
    t = dh2.shape[0]
    tm = _tile(t, TM_SCAN)

    def step(dh_ref, mo_ref, gp_ref, g_ref, ya_ref, yb_ref, xg_ref, hr_ref, wo_ref, wl_ref, wa_ref,
             dmo_ref, dya_ref, dyb_ref, dgate_ref, dhr_ref, dxg_ref, do_ref, dgp_ref, dbg_ref):
        dmo, dgp = _rms_bwd(dh_ref[...], mo_ref[...], gp_ref[...])
        dgp_ref[...] += dgp
        dmob = dmo.astype(BF16)
        dmo_ref[...] = dmob
        dm = _nt(dmob, wo_ref[...])
        g0 = g_ref[:, 0:D].astype(F32)
        g1 = g_ref[:, D:2 * D].astype(F32)
        dyab = (dm * g0).astype(BF16)
        dybb = (dm * g1).astype(BF16)
        dya_ref[...] = dyab
        dyb_ref[...] = dybb
        dg0 = dm * ya_ref[...].astype(F32)
        dg1 = dm * yb_ref[...].astype(F32)
        dgate_ref[:, 0:D] = dg0.astype(BF16)
        dgate_ref[:, D:2 * D] = dg1.astype(BF16)
        dbg_ref[:, 0:D] += jnp.sum(dg0, axis=0, keepdims=True)
        dbg_ref[:, D:2 * D] += jnp.sum(dg1, axis=0, keepdims=True)
        dyain = _nt(dyab, wl_ref[...])
        do_ref[...] = _nt(dybb, wa_ref[...]).astype(BF16)
        gelu, gelu_grad = _gelu_and_grad(xg_ref[...].astype(F32))
        dhr_ref[...] = dyain * gelu
        dxg_ref[...] = (dyain * hr_ref[...] * gelu_grad).astype(BF16)

    def body(dh_hbm, mo_hbm, gp_ref, g_hbm, ya_hbm, yb_hbm, xg_hbm, hr_hbm, wo_ref, wl_ref, wa_ref,
             dmo_hbm, dya_hbm, dyb_hbm, dgate_hbm, dhr_hbm, dxg_hbm, do_hbm, dgp_ref, dbg_ref):
        dgp_ref[...] = jnp.zeros_like(dgp_ref)
        dbg_ref[...] = jnp.zeros_like(dbg_ref)

        def tile(dh_ref, mo_ref, g_ref, ya_ref, yb_ref, xg_ref, hr_ref,
                 dmo_ref, dya_ref, dyb_ref, dgate_ref, dhr_ref, dxg_ref, do_ref):
            step(dh_ref, mo_ref, gp_ref, g_ref, ya_ref, yb_ref, xg_ref, hr_ref, wo_ref, wl_ref, wa_ref,
                 dmo_ref, dya_ref, dyb_ref, dgate_ref, dhr_ref, dxg_ref, do_ref, dgp_ref, dbg_ref)

        pltpu.emit_pipeline(
            tile, grid=(t // tm,),
            in_specs=[_row_spec(tm, D, 3), _row_spec(tm, D, 3), _row_spec(tm, 2 * D, 3), _row_spec(tm, D, 3),
                      _row_spec(tm, D, 3), _row_spec(tm, D, 3), _row_spec(tm, D, 3)],
            out_specs=[_row_spec(tm, D), _row_spec(tm, D), _row_spec(tm, D), _row_spec(tm, 2 * D), _row_spec(tm, D),
                       _row_spec(tm, D), _row_spec(tm, D)],
        )(dh_hbm, mo_hbm, g_hbm, ya_hbm, yb_hbm, xg_hbm, hr_hbm, dmo_hbm, dya_hbm, dyb_hbm, dgate_hbm, dhr_hbm, dxg_hbm, do_hbm)

    b16 = jax.ShapeDtypeStruct((t, D), BF16)
    body, specs, operands = _behind(body, after)
    return pl.pallas_call(
        body, name="mix_bwd1",
        in_specs=specs + [_ANY, _ANY, _WHOLE, _ANY, _ANY, _ANY, _ANY, _ANY, _WHOLE, _WHOLE, _WHOLE],
        out_specs=[_ANY] * 7 + [_WHOLE, _WHOLE],
        out_shape=[b16, b16, b16, jax.ShapeDtypeStruct((t, 2 * D), BF16), jax.ShapeDtypeStruct((t, D), F32), b16, b16,
                   jax.ShapeDtypeStruct((1, D), F32), jax.ShapeDtypeStruct((1, 2 * D), F32)],
        compiler_params=_params(),
    )(*operands, dh2, mo, gpost, gate, ya, yb, xg, hr, w_o, w_lru, w_att)


def _rglru_bwd(dhr, hr, xc, r, ig, a, s, xr, conv_w, wa2, wx2, lam, after):
    t = dhr.shape[0]
    tm = _tile(t, TM_SCAN)
    nb8 = tm // 8
    nt = t // tm

    def body(dhr_ref, hr_ref, hrp_ref, xc_ref, r_ref, ig_ref, a_sc, s_ref, xr_ref, cw_ref, wa_ref, wx_ref, lam_ref,
             dxr_ref, dwa_ref, dwx_ref, dba_ref, dbx_ref, dlam_ref, dcw_ref, dcb_ref,
             ext_h, ext_d, g_sc, c_sc, nxt_sc):
        i = pl.program_id(0)
        first_tile = i == nt - 1

        @pl.when(i == 0)
        def _():
            c_sc[...] = jnp.zeros_like(c_sc)
            nxt_sc[...] = jnp.zeros_like(nxt_sc)
            for ref in (dwa_ref, dwx_ref, dba_ref, dbx_ref, dlam_ref, dcw_ref, dcb_ref):
                ref[...] = jnp.zeros_like(ref)

        lamv = lam_ref[...]
        sp = _softplus_neg(lamv)
        rv = r_ref[...]
        igv = ig_ref[...]
        xcv = xc_ref[...]
        a = a_sc[...]
        s = s_ref[...]

        def blk(jj, c):
            st = pl.multiple_of((nb8 - 1 - jj) * 8, 8)
            d8 = dhr_ref[pl.ds(st, 8), :]
            a8 = a_sc[pl.ds(st, 8), :]
            rows = [None] * 8
            for k in range(7, -1, -1):
                g = d8[k:k + 1, :] + c
                c = a8[k:k + 1, :] * g
                rows[k] = g
            g_sc[pl.ds(st, 8), :] = jnp.concatenate(rows, axis=0)
            return c

        c_sc[0:1, :] = lax.fori_loop(0, nb8, blk, c_sc[0:1, :])
        g = g_sc[...]
        ext_h[0:8, :] = jnp.where(first_tile, 0.0, hrp_ref[...])
        ext_h[8:8 + tm, :] = hr_ref[...]
        hprev = ext_h[pl.ds(7, tm), :]
        d_s = g * (igv * xcv)
        dig = g * s * xcv
        dxc = g * s * igv
        dla = (g * hprev) * a - d_s * ((a * a) / s)
        dr_pre = (dla * (-LRU_C * sp)) * (rv * (1.0 - rv))
        di_pre = dig * (igv * (1.0 - igv))
        dlam_ref[...] += jnp.sum(dla * (LRU_C * rv), axis=0, keepdims=True) * jax.nn.sigmoid(-lamv)
        dba_ref[...] += jnp.sum(dr_pre, axis=0, keepdims=True)
        dbx_ref[...] += jnp.sum(di_pre, axis=0, keepdims=True)
        drb = dr_pre.astype(BF16)
        dib = di_pre.astype(BF16)
        xcb = xcv.astype(BF16)
        ext_d[tm:tm + 8, :] = nxt_sc[...]
        for p in range(8):
            sl = slice(p * 128, (p + 1) * 128)
            ext_d[0:tm, sl] = dxc[:, sl] + _nt(drb[:, sl], wa_ref[p]) + _nt(dib[:, sl], wx_ref[p])
            dwa_ref[p] += _tn(xcb[:, sl], drb[:, sl])
            dwx_ref[p] += _tn(xcb[:, sl], dib[:, sl])
        dxcv = ext_d[0:tm, :]
        nxt_sc[...] = ext_d[0:8, :]
        dcb_ref[...] += jnp.sum(dxcv, axis=0, keepdims=True)
        xrv = xr_ref[...]
        dxr = jnp.zeros((tm, D), F32)
        for tap in range(4):
            ext_h[0:tm, :] = ext_d[pl.ds(3 - tap, tm), :]
            ahead = ext_h[0:tm, :]
            dxr = dxr + ahead * cw_ref[tap:tap + 1, :]
            dcw_ref[tap:tap + 1, :] += jnp.sum(ahead * xrv, axis=0, keepdims=True)
        dxr_ref[...] = dxr.astype(BF16)

    rev = pl.BlockSpec((tm, D), lambda i: (nt - 1 - i, 0))
    prev = pl.BlockSpec((8, D), lambda i: (jnp.maximum((nt - 1 - i) * nb8 - 1, 0), 0))
    full = lambda shape: pl.BlockSpec(shape, lambda i: tuple(0 for _ in shape))
    vec = jax.ShapeDtypeStruct((1, D), F32)
    blocks = jax.ShapeDtypeStruct((8, 128, 128), F32)
    body, specs, operands = _behind(body, after)
    return pl.pallas_call(
        body, grid=(nt,), name="rglru_bwd",
        in_specs=specs + [rev, rev, prev, rev, rev, rev, rev, rev, rev, full((4, D)), full((8, 128, 128)),
                          full((8, 128, 128)), _vec_spec(D)],
        out_specs=[rev, full((8, 128, 128)), full((8, 128, 128)), _vec_spec(D), _vec_spec(D), _vec_spec(D), full((4, D)),
                   _vec_spec(D)],
        out_shape=[jax.ShapeDtypeStruct((t, D), BF16), blocks, blocks, vec, vec, vec, jax.ShapeDtypeStruct((4, D), F32), vec],
        scratch_shapes=[pltpu.VMEM((tm + 8, D), F32), pltpu.VMEM((tm + 8, D), F32),
                        pltpu.VMEM((tm, D), F32), pltpu.VMEM((8, D), F32), pltpu.VMEM((8, D), F32)],
        compiler_params=_params("arbitrary"),
    )(*operands, dhr, hr, hr, xc, r, ig, a, s, xr, conv_w, wa2, wx2, lam)


def _attn_bwd(sink_rows, q, kp, vp, bias_t, mask, do):
    t = q.shape[0]
    tp = kp.shape[0]
    per_step = 16

    def body(sink_ref, q_ref, kp_ref, vp_ref, bias_ref, mask_ref, do_ref, dq_ref, dk_ref, dv_ref, dbias_ref, ds_ref):
        @pl.when(pl.program_id(0) == 0)
        def _():
            for ref in (dk_ref, dv_ref, dbias_ref, ds_ref):
                ref[...] = jnp.zeros_like(ref)

        maskv = mask_ref[...]
        lane_group = lax.broadcasted_iota(jnp.int32, (1, 4 * HEAD_DIM), 1) // HEAD_DIM

        def own_blocks(full):
            out = full[0:KP]
            for g in range(1, 4):
                out = jnp.where(lane_group == g, full[g * KP:(g + 1) * KP], out)
            return out

        dsc_sum, dsinks, dks, dvs = 0.0, [0.0] * 4, [], []
        for k in range(per_step):
            c = pl.program_id(0) * per_step + k
            chunk = slice(k * CHUNK, (k + 1) * CHUNK)
            st = pl.multiple_of(c * CHUNK, CHUNK)
            kbd = _block_diag(kp_ref[pl.ds(st, KP), :], maskv)
            vbd = _block_diag(vp_ref[pl.ds(st, KP), :], maskv)
            q_all = _stack_heads(q_ref[chunk, :])
            do_all = _stack_heads(do_ref[chunk, :])
            valid = lax.broadcasted_iota(jnp.int32, (KP, 1), 0) + c * CHUNK >= PAD_KEYS
            qk = _nt(kbd, q_all)
            dp = _nt(vbd, do_all)
            ps, dscs = [], []
            for g in range(4):
                rows = slice(g * KP, (g + 1) * KP)
                p, sink_p = _group_softmax(qk[rows], bias_ref[rows, :], sink_ref[g:g + 1, :], valid)
                delta = jnp.sum(p * dp[rows], axis=0, keepdims=True)
                ps.append(p)
                dscs.append(p * (dp[rows] - delta))
                dsinks[g] = dsinks[g] - sink_p * delta
            dsc = jnp.concatenate(dscs, axis=0)
            dsc_sum = dsc_sum + dsc
            dsb = (dsc * (HEAD_DIM ** -0.5)).astype(BF16)
            dq_ref[chunk, :] = _unstack_heads(_tn(dsb, kbd)).astype(BF16)
            dks.append((st, own_blocks(_nn(dsb, q_all))))
            dvs.append((st, own_blocks(_nn(jnp.concatenate(ps, axis=0).astype(BF16), do_all))))
        dbias_ref[...] += dsc_sum
        for g in range(4):
            ds_ref[g:g + 1, :] += dsinks[g]
        for (st, dkw), (_, dvw) in zip(dks, dvs):
            dk_ref[pl.ds(st, KP), :] += dkw
            dv_ref[pl.ds(st, KP), :] += dvw

    full = lambda shape: pl.BlockSpec(shape, lambda i: tuple(0 for _ in shape))
    return pl.pallas_call(
        body, grid=(t // (per_step * CHUNK),), name="attn_bwd",
        in_specs=[_WHOLE, _row_spec(per_step * CHUNK, D), _WHOLE, _WHOLE, _WHOLE, _WHOLE, _row_spec(per_step * CHUNK, D)],
        out_specs=[_row_spec(per_step * CHUNK, D), full((tp, KV_W)), full((tp, KV_W)), full((4 * KP, 4 * CHUNK)),
                   full((8, 4 * CHUNK))],
        out_shape=[jax.ShapeDtypeStruct((t, D), BF16), jax.ShapeDtypeStruct((tp, KV_W), F32),
                   jax.ShapeDtypeStruct((tp, KV_W), F32), jax.ShapeDtypeStruct((4 * KP, 4 * CHUNK), F32),
                   jax.ShapeDtypeStruct((8, 4 * CHUNK), F32)],
        compiler_params=_params("arbitrary"),
    )(sink_rows, q, kp, vp, bias_t, mask, do)


def _mix_bwd2(dproj, dgate, h1, dh2, gmix, w_in_g, w_gate_g, after):
    t = h1.shape[0]
    tm = _tile(t)

    def body(dp_ref, dg_ref, h_ref, dh_ref, g_ref, win_ref, wg_ref, dh1_ref, dgm_ref):
        @pl.when(pl.program_id(0) == 0)
        def _():
            dgm_ref[...] = jnp.zeros_like(dgm_ref)

        du = jnp.zeros((tm, D), F32)
        for s in range(NSH):
            du = du + _nt(dp_ref[:, s * IN_S:(s + 1) * IN_S], win_ref[s])
            du = du + _nt(dg_ref[:, s * GATE_S:(s + 1) * GATE_S], wg_ref[s])
        dxn, dg = _rms_bwd(du, h_ref[...], g_ref[...])
        dgm_ref[...] += dg
        dh1_ref[...] = dh_ref[...] + dxn

    body, specs, operands = _behind(body, after)
    return pl.pallas_call(
        body, grid=(t // tm,), name="mix_bwd2",
        in_specs=specs + [_row_spec(tm, NSH * IN_S), _row_spec(tm, 2 * D), _row_spec(tm, D), _row_spec(tm, D), _vec_spec(D),
                          _WHOLE, _WHOLE],
        out_specs=[_row_spec(tm, D), _vec_spec(D)],
        out_shape=[jax.ShapeDtypeStruct((t, D), F32), jax.ShapeDtypeStruct((1, D), F32)],
        compiler_params=_params("arbitrary"),
    )(*operands, dproj, dgate, h1, dh2, gmix, w_in_g, w_gate_g)


def _band_onehot():
    nb = N_BUCKETS // 2
    max_exact = nb // 2
    rel = jnp.arange(KB)[None, :] - PAD_KEYS - jnp.arange(CHUNK)[:, None]
    ret = jnp.where(rel > 0, nb, 0)
    n = jnp.abs(rel)
    nf = jnp.maximum(n, 1).astype(jnp.float32)
    large = max_exact + (jnp.log(nf / max_exact) / math.log(128 / max_exact) * (nb - max_exact)).astype(jnp.int32)
    large = jnp.minimum(large, nb - 1)
    buckets = (ret + jnp.where(n < max_exact, n, large)).reshape(1, CHUNK * KB)
    return (buckets == jnp.arange(N_BUCKETS)[:, None]).astype(F32)


def _pair_blocks(w):
    pairs = w.reshape(8, 2, 64, 64)
    z = jnp.zeros((8, 64, 64), w.dtype)
    return jnp.concatenate([jnp.concatenate([pairs[:, 0], z], axis=2), jnp.concatenate([z, pairs[:, 1]], axis=2)], axis=1)


def _unpair_blocks(w2):
    return jnp.stack([w2[:, 0:64, 0:64], w2[:, 64:128, 64:128]], axis=1).reshape(16, 64, 64)


def _local_step(x, target, weights, sm, reducer):
    row = lambda v: v.reshape(1, -1)
    onehot_t = _band_onehot()
    bias = _bias_fwd(sm["rel_bias"].T, onehot_t).reshape(4, 4, CHUNK, KB)
    bias_t = jnp.pad(jnp.transpose(bias, (0, 3, 1, 2)), ((0, 0), (0, KP - KB), (0, 0), (0, 0))).reshape(4 * KP, 4 * CHUNK)
    sink_rows = jnp.pad(jnp.repeat(sm["attn_sinks"].reshape(4, 4), CHUNK, axis=1), ((0, 4), (0, 0)))
    grp = jnp.arange(4 * KP)[:, None] // KP == jnp.arange(4 * HEAD_DIM)[None, :] // HEAD_DIM
    mask = (grp & (jnp.arange(4 * KP)[:, None] % KP < KB)).astype(BF16)
    wa2 = _pair_blocks(sm["rg_a_w"]).astype(BF16)
    wx2 = _pair_blocks(sm["rg_x_w"]).astype(BF16)
    wg = dict(weights("ffn1_up", [bias_t, sink_rows, mask, wa2, wx2]))
    sm = dict(sm, conv_w=wg["conv_w"])

    n1, a1, b1, hm1 = _ffn_up(x, row(sm["ffn1_pre_g"]), wg["ffn1_w1"], wg["ffn1_w3"], "ffn1_up")
    wg.update(weights("ffn1_down", hm1))
    h1, f1 = _ffn_down(x, hm1, wg["ffn1_w2"], row(sm["ffn1_post_g"]), "ffn1_down")
    wg.update(weights("mix_in", h1))
    u, q, k, v, xr, xg, gate = _mix_proj(h1, row(sm["mix_pre_g"]), wg["w_in"], wg["w_gate"], row(sm["b_gate"]))
    token = weights("mix_out", u, begin=True)
    hr, yain, xc, r, ig, lru_a, lru_s = _rglru_fwd(xr, xg, sm["conv_w"], row(sm["conv_b"]), wa2, row(sm["rg_a_b"]), wx2,
                                                   row(sm["rg_x_b"]), row(sm["lru_lambda"]), token)
    token = weights("ffn2", hr, begin=True)
    kp = jnp.pad(k, ((PAD_KEYS, KP - KB), (0, 0)))
    vp = jnp.pad(v, ((PAD_KEYS, KP - KB), (0, 0)))
    o = _attn_fwd(sink_rows, q, kp, vp, bias_t, mask, token)
    wg.update(weights("mix_out", o))
    w_lru = wg["w_lru_out"].reshape(D, D)
    w_att = wg["w_attn_out"].reshape(D, D)
    w_o = wg["w_o"].reshape(D, D)
    wg.update(weights("ffn2", o))
    h2, mo, merged, ya, yb = _merge_fwd(yain, o, gate, h1, w_lru, w_att, w_o, row(sm["mix_post_g"]))
    dy, a2, b2, hm2, f2, sq = _ffn_fwd(h2, row(sm["ffn2_pre_g"]), wg["ffn2_w1"], wg["ffn2_w3"], wg["ffn2_w2"],
                                       row(sm["ffn2_post_g"]), "ffn2_fwd", target)

    big, small = {}, {}
    dh2, n2, da2, db2, df2, small["ffn2_pre_g"], small["ffn2_post_g"] = _ffn_bwd(
        dy, h2, f2, a2, b2, row(sm["ffn2_pre_g"]), row(sm["ffn2_post_g"]), wg["ffn2_w1"], wg["ffn2_w3"], wg["ffn2_w2"],
        "ffn2_bwd")
    big["ffn2_w1"] = _wgrad_rows(da2, n2, "dw_ffn2_w1")
    big["ffn2_w3"] = _wgrad_rows(db2, n2, "dw_ffn2_w3")
    big["ffn2_w2"] = _wgrad_rows(hm2, df2, "dw_ffn2_w2")
    token = reducer.begin("ffn2", {n: big[n] for n in ("ffn2_w1", "ffn2_w3", "ffn2_w2")})
    dmo, dya, dyb, dgate, dhr, dxg, do, small["mix_post_g"], small["b_gate"] = _mix_bwd1(
        dh2, mo, row(sm["mix_post_g"]), gate, ya, yb, xg, hr, w_o, w_lru, w_att, token)
    big["w_o"] = _wgrad_sq(merged, dmo, "dw_w_o").reshape(NSH, D // NSH, D)
    big["w_lru_out"] = _wgrad_sq(yain, dya, "dw_w_lru_out").reshape(NSH, D // NSH, D)
    big["w_attn_out"] = _wgrad_sq(o, dyb, "dw_w_attn_out").reshape(NSH, D // NSH, D)
    token = reducer.advance("ffn2", big["w_attn_out"])
    (dxr, dwa2, dwx2, small["rg_a_b"], small["rg_x_b"], small["lru_lambda"], small["conv_w"], small["conv_b"]) = _rglru_bwd(
        dhr, hr, xc, r, ig, lru_a, lru_s, xr, sm["conv_w"], wa2, wx2, row(sm["lru_lambda"]), token)
    small["rg_a_w"] = _unpair_blocks(dwa2)
    small["rg_x_w"] = _unpair_blocks(dwx2)
    dq, dkp, dvp, dbias_t, ds_rows = _attn_bwd(sink_rows, q, kp, vp, bias_t, mask, do)
    dbias = jnp.transpose(dbias_t.reshape(4, KP, 4, CHUNK)[:, :KB], (0, 2, 3, 1)).reshape(N_HEADS, CHUNK * KB)
    drel_t, dsinks = _bias_bwd(dbias, onehot_t, ds_rows)
    small["attn_sinks"] = dsinks[0:4, 0:4].reshape(N_HEADS)
    small["rel_bias"] = drel_t.T
    t = x.shape[0]
    dproj = jnp.concatenate([dq, dkp[PAD_KEYS:PAD_KEYS + t].astype(BF16), dvp[PAD_KEYS:PAD_KEYS + t].astype(BF16), dxr, dxg],
                            axis=1)
    big["w_in"] = _wgrad_cols(u, dproj, IN_S, "dw_w_in")
    big["w_gate"] = _wgrad_cols(u, dgate, GATE_S, "dw_w_gate")
    token = reducer.begin("mix", {n: big[n] for n in ("w_in", "w_gate", "w_lru_out", "w_attn_out", "w_o")})
    dh1, small["mix_pre_g"] = _mix_bwd2(dproj, dgate, h1, dh2, row(sm["mix_pre_g"]), wg["w_in"], wg["w_gate"], token)
    da1, db1, df1, small["ffn1_post_g"] = _ffn_bwd_acts(dh1, f1, a1, b1, row(sm["ffn1_post_g"]), wg["ffn1_w2"],
                                                        "ffn1_bwd_acts")
    token = reducer.advance("mix", df1)
    big["ffn1_w1"] = _wgrad_rows(da1, n1, "dw_ffn1_w1", token)
    big["ffn1_w3"] = _wgrad_rows(db1, n1, "dw_ffn1_w3", token)
    big["ffn1_w2"] = _wgrad_rows(hm1, df1, "dw_ffn1_w2", token)
    token = reducer.begin("ffn1", {n: big[n] for n in ("ffn1_w1", "ffn1_w3", "ffn1_w2")})
    dx, small["ffn1_pre_g"] = _ffn_bwd_input(dh1, x, da1, db1, row(sm["ffn1_pre_g"]), wg["ffn1_w1"], wg["ffn1_w3"],
                                             "ffn1_bwd_input", token)
    return sq, dx, big, small


_ANY = pl.BlockSpec(memory_space=pl.ANY)


def _place():
    return lax.axis_index("x"), lax.axis_index("y"), lax.axis_index("c")


def _other_chips(x, y):
    return [(1 - x, y), (x, 1 - y), (1 - x, 1 - y)]


_HBM = pl.BlockSpec(memory_space=pltpu.HBM)
_SEM = pl.BlockSpec(memory_space=pltpu.SEMAPHORE)
_EFFECT = pltpu.SideEffectType.DATAFLOW_SIDE_EFFECTING


def _cast_into_slot(w, chip, name, after=None):
    r, cc = w.shape
    rows = r // 4

    def body(chip_ref, *refs):
        w_ref, o_ref = refs[-2:]
        o_ref[...] = w_ref[...].astype(BF16)

    extra = [] if after is None else [after]
    return pl.pallas_call(
        body, name=name, out_shape=jax.ShapeDtypeStruct((NSH, r, cc), BF16),
        grid_spec=pltpu.PrefetchScalarGridSpec(
            num_scalar_prefetch=1, grid=(4,), in_specs=[_ANY] * len(extra) + [pl.BlockSpec((rows, cc), lambda i, chip: (i, 0))],
            out_specs=pl.BlockSpec((None, rows, cc), lambda i, chip: (chip[0], i, 0))),
        compiler_params=_params("arbitrary"))(chip, *extra, w)


def _piece(ref, slot, c):
    if ref.dtype == F32:
        return ref.at[slot]
    rh = ref.shape[1] // 2
    return ref.at[slot, pl.ds(pl.multiple_of(c * rh, 16), rh), :]


def _gather_start(stages, name):
    flat = [b for stage in stages for b in stage]
    n, ns = len(flat), len(stages)

    def body(*refs):
        ins, sems, token = refs[:n], refs[n:n + 2 * ns], refs[-1]
        x, y, c = _place()
        me = 2 * x + y
        k = 0
        for s, stage in enumerate(stages):
            for i in range(len(stage)):
                for j, (px, py) in enumerate(_other_chips(x, y)):
                    piece = _piece(ins[k], me, c)
                    pltpu.make_async_remote_copy(src_ref=piece, dst_ref=piece, send_sem=sems[2 * s].at[3 * i + j],
                                                 recv_sem=sems[2 * s + 1].at[3 * i + j], device_id=(px, py, c),
                                                 device_id_type=MESH).start()
                k += 1
        token[...] = jnp.zeros_like(token)

    sem_shapes = [pltpu.SemaphoreType.DMA((3 * len(stage),)) for stage in stages for _ in range(2)]
    outs = pl.pallas_call(
        body, name=name, in_specs=[_HBM] * n,
        out_specs=[_SEM] * (2 * ns) + [_HBM] * n + [pl.BlockSpec(memory_space=pltpu.VMEM)],
        out_shape=sem_shapes + [pltpu.HBM(b.shape, b.dtype) for b in flat] + [jax.ShapeDtypeStruct((8, 128), F32)],
        input_output_aliases={i: 2 * ns + i for i in range(n)},
        compiler_params=pltpu.CompilerParams(has_side_effects=_EFFECT),
    )(*[pltpu.with_memory_space_constraint(b, pltpu.HBM) for b in flat])
    sems, bufs, token = outs[:2 * ns], list(outs[2 * ns:2 * ns + n]), outs[-1]
    per_stage, k = [], 0
    for s, stage in enumerate(stages):
        per_stage.append((sems[2 * s], sems[2 * s + 1], bufs[k:k + len(stage)]))
        k += len(stage)
    return per_stage, token


def _gather_wait(send_sems, recv_sems, bufs, after, name):
    n = len(bufs)

    def body(*refs):
        ins, ssem, rsem = refs[:n], refs[n], refs[n + 1]
        x, y, c = _place()
        me = 2 * x + y
        for i in range(n):
            for j, (px, py) in enumerate(_other_chips(x, y)):
                cp = pltpu.make_async_remote_copy(src_ref=_piece(ins[i], me, c), dst_ref=_piece(ins[i], 2 * px + py, c),
                                                  send_sem=ssem.at[3 * i + j], recv_sem=rsem.at[3 * i + j],
                                                  device_id=(px, py, c), device_id_type=MESH)
                cp.wait_send()
                cp.wait_recv()

    afters = list(after) if isinstance(after, (list, tuple)) else [after]
    return pl.pallas_call(
        body, name=name, in_specs=[_HBM] * n + [_SEM, _SEM] + [_ANY] * len(afters), out_specs=[_HBM] * n,
        out_shape=[pltpu.HBM(b.shape, b.dtype) for b in bufs], input_output_aliases={i: i for i in range(n)},
        compiler_params=pltpu.CompilerParams(has_side_effects=_EFFECT),
    )(*bufs, send_sems, recv_sems, *afters)


def _sibling_fill(bufs, name):
    n = len(bufs)

    def body(*refs):
        ins, outs = refs[:n], refs[n:2 * n]
        send_sems, recv_sems = refs[2 * n:]
        x, y, c = _place()
        copies = []
        for i in range(n):
            for j, (px, py) in enumerate(_other_chips(x, y)):
                copies.append(pltpu.make_async_remote_copy(
                    src_ref=_piece(ins[i], 2 * px + py, c), dst_ref=_piece(outs[i], 2 * px + py, c),
                    send_sem=send_sems.at[3 * i + j], recv_sem=recv_sems.at[3 * i + j], device_id=(x, y, 1 - c),
                    device_id_type=MESH))
                copies[-1].start()
        for cp in copies:
            cp.wait()

    return pl.pallas_call(
        body, name=name, in_specs=[_ANY] * n, out_specs=[_ANY] * n,
        out_shape=[jax.ShapeDtypeStruct(b.shape, b.dtype) for b in bufs], input_output_aliases={i: i for i in range(n)},
        scratch_shapes=[pltpu.SemaphoreType.DMA((3 * n,)), pltpu.SemaphoreType.DMA((3 * n,))],
        compiler_params=pltpu.CompilerParams(has_side_effects=True),
    )(*bufs)


def _swap_plan(srcs, lands):
    x, y, c = _place()
    plan = []
    for src, land in zip(srcs, lands):
        rh = src.shape[1] // 2
        plan.append((src.at[:, pl.ds(pl.multiple_of((1 - c) * rh, 16), rh), :], land, (x, y, 1 - c)))
    return plan


def _owners_plan(srcs, lands):
    x, y, c = _place()
    return [(src.at[2 * px + py], land.at[j], (px, py, c))
            for src, land in zip(srcs, lands) for j, (px, py) in enumerate(_other_chips(x, y))]


def _exchange_start(srcs, lands, plan, copies, name):
    n, m = len(srcs), len(srcs) + len(lands)

    def body(*refs):
        send_sems, recv_sems, token = refs[m], refs[m + 1], refs[-1]
        for k, (src, dst, dev) in enumerate(plan(refs[:n], refs[n:m])):
            pltpu.make_async_remote_copy(src_ref=src, dst_ref=dst, send_sem=send_sems.at[k], recv_sem=recv_sems.at[k],
                                         device_id=dev, device_id_type=MESH).start()
        token[...] = jnp.zeros_like(token)

    both = list(srcs) + list(lands)
    outs = pl.pallas_call(
        body, name=name, in_specs=[_HBM] * m,
        out_specs=[_SEM, _SEM] + [_HBM] * m + [pl.BlockSpec(memory_space=pltpu.VMEM)],
        out_shape=[pltpu.SemaphoreType.DMA((copies,)), pltpu.SemaphoreType.DMA((copies,))]
        + [pltpu.HBM(b.shape, b.dtype) for b in both] + [jax.ShapeDtypeStruct((8, 128), F32)],
        input_output_aliases={i: 2 + i for i in range(m)},
        compiler_params=pltpu.CompilerParams(has_side_effects=_EFFECT),
    )(*[pltpu.with_memory_space_constraint(b, pltpu.HBM) for b in both])
    return (outs[0], outs[1]), list(outs[2:2 + n]), list(outs[2 + n:2 + m]), outs[-1]


def _exchange_wait(sems, srcs, lands, plan, after, name):
    n, m = len(srcs), len(srcs) + len(lands)

    def body(*refs):
        send_sems, recv_sems = refs[m], refs[m + 1]
        for k, (src, dst, dev) in enumerate(plan(refs[:n], refs[n:m])):
            cp = pltpu.make_async_remote_copy(src_ref=src, dst_ref=dst, send_sem=send_sems.at[k], recv_sem=recv_sems.at[k],
                                              device_id=dev, device_id_type=MESH)
            cp.wait_send()
            cp.wait_recv()

    both = list(srcs) + list(lands)
    afters = list(after) if isinstance(after, (list, tuple)) else [after]
    outs = pl.pallas_call(
        body, name=name, in_specs=[_HBM] * m + [_SEM, _SEM] + [_ANY] * len(afters), out_specs=[_HBM] * m,
        out_shape=[pltpu.HBM(b.shape, b.dtype) for b in both], input_output_aliases={i: i for i in range(m)},
        compiler_params=pltpu.CompilerParams(has_side_effects=_EFFECT),
    )(*both, sems[0], sems[1], *afters)
    return list(outs[:n]), list(outs[n:])


def _fill_plan(bufs, _):
    x, y, c = _place()
    return [(_piece(buf, 2 * px + py, c), _piece(buf, 2 * px + py, c), (x, y, 1 - c))
            for buf in bufs for px, py in _other_chips(x, y)]


class _Reducer:
    def __init__(self, where):
        self.state = {}
        self.where = where

    def begin(self, stage, grads):
        names = list(grads)
        full = [grads[n] for n in names]
        lands = [lax.empty((NSH, g.shape[1] // 2, g.shape[2]), g.dtype) for g in full]
        sems, full, lands, token = _exchange_start(full, lands, _swap_plan, len(full), "swap_start_" + stage)
        self.state[stage] = (names, sems, full, lands)
        return token

    def advance(self, stage, after):
        names, sems, full, lands = self.state[stage]
        full, got = _exchange_wait(sems, full, lands, _swap_plan, after, "swap_wait_" + stage)
        sums, own = _chip_sums(full, got, self.where, "chip_sums_" + stage)
        lands = [lax.empty((3,) + s.shape[1:], BF16) for s in sums]
        sems, sent, lands, token = _exchange_start(sums, lands, _owners_plan, 3 * len(sums), "owners_start_" + stage)
        self.state[stage] = (names, own, sems, sent, lands)
        return token

    def finish(self, stage, after):
        names, own, sems, sent, lands = self.state[stage]
        _, got = _exchange_wait(sems, sent, lands, _owners_plan, after, "owners_wait_" + stage)
        return dict(zip(names, _owner_sums(own, got, "owner_sums_" + stage)))


def _chip_sums(gs, gots, where, name):
    n = len(gs)

    def body(where_ref, *refs):
        g_refs, got_refs, hb_refs, own_refs = (refs[k * n:(k + 1) * n] for k in range(4))
        mine = pl.program_id(0) == where_ref[1]
        for g_ref, got_ref, hb_ref, own_ref in zip(g_refs, got_refs, hb_refs, own_refs):
            h = g_ref[...].astype(F32) + got_ref[...].astype(F32)
            hb_ref[...] = h.astype(BF16)

            @pl.when(mine)
            def _():
                own_ref[...] = h

    halves = [(g.shape[1] // 2, g.shape[2]) for g in gs]
    slot = [pl.BlockSpec((None, rh, cc), lambda s, where: (s, 0, 0)) for rh, cc in halves]
    outs = pl.pallas_call(
        body, name=name,
        grid_spec=pltpu.PrefetchScalarGridSpec(
            num_scalar_prefetch=1, grid=(NSH,),
            in_specs=[pl.BlockSpec((None, rh, cc), lambda s, where: (s, where[0], 0)) for rh, cc in halves] + slot,
            out_specs=slot + [pl.BlockSpec((rh, cc), lambda s, where: (0, 0)) for rh, cc in halves]),
        out_shape=[jax.ShapeDtypeStruct((NSH, rh, cc), BF16) for rh, cc in halves]
        + [jax.ShapeDtypeStruct((rh, cc), F32) for rh, cc in halves],
        compiler_params=_params("arbitrary"),
    )(where, *gs, *gots)
    return list(outs[:n]), list(outs[n:])


def _owner_sums(owns, gots, name):
    n = len(owns)

    def body(*refs):
        own_refs, got_refs, o_refs = (refs[k * n:(k + 1) * n] for k in range(3))
        for own_ref, got_ref, o_ref in zip(own_refs, got_refs, o_refs):
            o_ref[...] = ((own_ref[...] + got_ref[0].astype(F32)) + got_ref[1].astype(F32)) + got_ref[2].astype(F32)

    blocks = [(o.shape[0] // 2, o.shape[1]) for o in owns]
    rows = [pl.BlockSpec(b, lambda i: (i, 0)) for b in blocks]
    return pl.pallas_call(
        body, grid=(2,), name=name,
        in_specs=rows + [pl.BlockSpec((3,) + b, lambda i: (0, i, 0)) for b in blocks], out_specs=rows,
        out_shape=[jax.ShapeDtypeStruct(o.shape, F32) for o in owns], compiler_params=_params("arbitrary"),
    )(*owns, *gots)


def _sibling_plan(srcs, lands):
    x, y, c = _place()
    return [(src, land, (x, y, 1 - c)) for src, land in zip(srcs, lands)]


def _all_reduce_small(part):
    def body(p_ref, o_ref, rbuf, send1, recv1, send2, recv2):
        x, y, c = _place()
        me = 4 * x + 2 * y + c
        peers = []
        for k in range(1, 8):
            px, py, pc = x ^ ((k >> 2) & 1), y ^ ((k >> 1) & 1), c ^ (k & 1)
            peers.append((k, (px, py, pc), 4 * px + 2 * py + pc))

        def rows(d):
            return pl.ds(pl.multiple_of(d * SMALL_SLICE, 8), SMALL_SLICE)

        first = [pltpu.make_async_remote_copy(src_ref=p_ref.at[rows(idx), :], dst_ref=rbuf.at[me], send_sem=send1.at[k],
                                              recv_sem=recv1.at[k], device_id=dev, device_id_type=MESH)
                 for k, dev, idx in peers]
        for cp in first:
            cp.start()
        rbuf[me] = p_ref[rows(me), :]
        for k, dev, idx in peers:
            pltpu.make_async_remote_copy(src_ref=p_ref.at[rows(idx), :], dst_ref=rbuf.at[idx], send_sem=send1.at[k],
                                         recv_sem=recv1.at[k], device_id=dev, device_id_type=MESH).wait_recv()
        acc = rbuf[0]
        for d in range(1, 8):
            acc = acc + rbuf[d]
        o_ref[rows(me), :] = acc
        second = [pltpu.make_async_remote_copy(src_ref=o_ref.at[rows(me), :], dst_ref=o_ref.at[rows(me), :],
                                               send_sem=send2.at[k], recv_sem=recv2.at[k], device_id=dev, device_id_type=MESH)
                  for k, dev, idx in peers]
        for cp in second:
            cp.start()
        for k, dev, idx in peers:
            pltpu.make_async_remote_copy(src_ref=o_ref.at[rows(me), :], dst_ref=o_ref.at[rows(idx), :], send_sem=send2.at[k],
                                         recv_sem=recv2.at[k], device_id=dev, device_id_type=MESH).wait_recv()
        for cp in first + second:
            cp.wait_send()

    return pl.pallas_call(
        body, name="all_reduce_small", in_specs=[_WHOLE], out_specs=_WHOLE,
        out_shape=jax.ShapeDtypeStruct((SMALL_ROWS, 128), F32),
        scratch_shapes=[pltpu.VMEM((8, SMALL_SLICE, 128), F32)] + [pltpu.SemaphoreType.DMA((8,))] * 4,
        compiler_params=pltpu.CompilerParams(has_side_effects=True),
    )(part)


def _adamw_update(w, gv, m, v):
    nm = ADAM_B1 * m + (1.0 - ADAM_B1) * gv
    nv = ADAM_B2 * v + (1.0 - ADAM_B2) * (gv * gv)
    m_hat = nm / (1.0 - ADAM_B1 ** ADAM_STEP)
    v_hat = nv / (1.0 - ADAM_B2 ** ADAM_STEP)
    return -ADAM_LR * (m_hat / (jnp.sqrt(v_hat) + ADAM_EPS) + ADAM_WD * w), nm, nv


def _adamw_small(ws, gs, ms, vs, after):
    n = len(ws)

    def body(*refs):
        w_refs, g_refs, m_refs, v_refs, d_refs, nm_refs, nv_refs = (refs[k * n:(k + 1) * n] for k in range(7))
        for i in range(n):
            d_refs[i][...], nm_refs[i][...], nv_refs[i][...] = _adamw_update(
                w_refs[i][...], g_refs[i][...], m_refs[i][...], v_refs[i][...])

    out = [jax.ShapeDtypeStruct(w.shape, F32) for w in ws]
    body, specs, operands = _behind(body, after)
    outs = pl.pallas_call(body, in_specs=specs + [_WHOLE] * (4 * n), out_specs=[_WHOLE] * (3 * n), out_shape=out * 3,
                          name="adamw_small", compiler_params=_params())(*operands, *ws, *gs, *ms, *vs)
    return outs[:n], outs[n:2 * n], outs[2 * n:]


def _adamw_halves(ws, mines, theirs, ms, vs, name):
    n = len(ws)
    steps = 2

    def body(*refs):
        w_refs, mine_refs, theirs_refs, m_refs, v_refs, g_refs, d_refs, nm_refs, nv_refs = (
            refs[k * n:(k + 1) * n] for k in range(9))
        is_mine = pl.program_id(0) == lax.axis_index("c")
        for i in range(n):
            gv = jnp.where(is_mine, mine_refs[i][...], theirs_refs[i][...])
            g_refs[i][...] = gv
            d_refs[i][...], nm_refs[i][...], nv_refs[i][...] = _adamw_update(w_refs[i][...], gv, m_refs[i][...], v_refs[i][...])

    blocks = [(h.shape[0] // steps, h.shape[1]) for h in mines]
    whole = [pl.BlockSpec(b, lambda h, i: (steps * h + i, 0)) for b in blocks]
    half = [pl.BlockSpec(b, lambda h, i: (i, 0)) for b in blocks]
    out = [jax.ShapeDtypeStruct(w.shape, F32) for w in ws]
    outs = pl.pallas_call(body, grid=(2, steps), in_specs=whole + half + half + whole + whole, out_specs=whole * 4,
                          out_shape=out * 4, name=name, compiler_params=_params("arbitrary", "arbitrary"),
                          )(*ws, *mines, *theirs, *ms, *vs)
    return [tuple(outs[k * n + i] for k in range(4)) for i in range(n)]


SMALL_USED = sum(size for _, size in SMALL) // 128


def _pack_small(vals, tail=None):
    parts = []
    for name, size in SMALL:
        flat = vals[name].reshape(-1).astype(F32)
        parts.append(jnp.pad(flat, (0, size - flat.shape[0])))
    if tail is not None:
        parts.append(tail.reshape(128))
    flat = jnp.concatenate(parts)
    return jnp.pad(flat, (0, SMALL_ROWS * 128 - flat.shape[0])).reshape(SMALL_ROWS, 128)


def _unpack_small(packed, shapes):
    flat = packed.reshape(-1)
    out, off = {}, 0
    for name, size in SMALL:
        n = math.prod(shapes[name])
        out[name] = flat[off:off + n].reshape(shapes[name])
        off += size
    return out


def kernel(x, ffn1_pre_g, ffn1_w1, ffn1_w3, ffn1_w2, ffn1_post_g, mix_pre_g, w_in, conv_w, conv_b, rg_a_w, rg_a_b, rg_x_w, rg_x_b, lru_lambda, w_lru_out, attn_sinks, rel_bias, w_attn_out, w_gate, b_gate, w_o, mix_post_g, ffn2_pre_g, ffn2_w1, ffn2_w3, ffn2_w2, ffn2_post_g, loss_target, m_ffn1_pre_g, m_ffn1_w1, m_ffn1_w3, m_ffn1_w2, m_ffn1_post_g, m_mix_pre_g, m_w_in, m_conv_w, m_conv_b, m_rg_a_w, m_rg_a_b, m_rg_x_w, m_rg_x_b, m_lru_lambda, m_w_lru_out, m_attn_sinks, m_rel_bias, m_w_attn_out, m_w_gate, m_b_gate, m_w_o, m_mix_post_g, m_ffn2_pre_g, m_ffn2_w1, m_ffn2_w3, m_ffn2_w2, m_ffn2_post_g, v_ffn1_pre_g, v_ffn1_w1, v_ffn1_w3, v_ffn1_w2, v_ffn1_post_g, v_mix_pre_g, v_w_in, v_conv_w, v_conv_b, v_rg_a_w, v_rg_a_b, v_rg_x_w, v_rg_x_b, v_lru_lambda, v_w_lru_out, v_attn_sinks, v_rel_bias, v_w_attn_out, v_w_gate, v_b_gate, v_w_o, v_mix_post_g, v_ffn2_pre_g, v_ffn2_w1, v_ffn2_w3, v_ffn2_w2, v_ffn2_post_g):
    given = dict(locals())
    chip = 2 * lax.axis_index("x") + lax.axis_index("y")
    transposed = ("ffn1_w1", "ffn1_w3", "ffn2_w1", "ffn2_w3")

    def shard(name, moment=""):
        w = given[moment + name][0]
        return w.T if name in transposed else w

    def unshard(name, w):
        return (w.T if name in transposed else w)[None]

    def only_my_columns(a):
        parts = a.reshape(1, 4, NSH, D // NSH)
        return sum(jnp.where(chip == s, parts[:, :, s], 0.0) for s in range(NSH))

    chip_arr = jnp.reshape(chip, (1,)).astype(jnp.int32)
    stage_names = {"ffn1_up": ["ffn1_w1", "ffn1_w3", "conv_w"],
                   "ffn1_down": ["ffn1_w2"],
                   "mix_in": ["w_in", "w_gate"],
                   "mix_out": ["w_lru_out", "w_attn_out", "w_o"],
                   "ffn2": ["ffn2_w1", "ffn2_w3", "ffn2_w2"]}
    in_flight, started = {}, None
    for stage, names in stage_names.items():
        bufs = [jnp.where(lax.broadcasted_iota(jnp.int32, (NSH, 4, D // NSH), 0) == chip, given[n], 0.0) if n == "conv_w"
                else _cast_into_slot(shard(n), chip_arr, "cast_" + n, started) for n in names]
        (in_flight[stage],), started = _gather_start([bufs], "gather_start_" + stage)
    all_started = started

    filling = {}

    def weights(stage, after, begin=False):
        names = stage_names[stage]
        halves_of = [n for n in names if n != "conv_w"]
        if stage in filling:
            filled, _ = _exchange_wait(filling.pop(stage), *filling.pop(stage + "/bufs"), _fill_plan, after,
                                       "fill_wait_" + stage)
            return dict(zip(halves_of, filled))
        send_sems, recv_sems, landing = in_flight[stage]
        if stage == "ffn1_up":
            after = [all_started] + list(after)
        landed = dict(zip(names, _gather_wait(send_sems, recv_sems, landing, after, "gather_wait_" + stage)))
        halves = [landed[n] for n in halves_of]
        if begin:
            filling[stage], bufs, _, token = _exchange_start(halves, [], _fill_plan, 3 * len(halves), "fill_start_" + stage)
            filling[stage + "/bufs"] = (bufs, [])
            return token
        out = dict(zip(halves_of, _sibling_fill(halves, "sibling_fill_" + stage)))
        if "conv_w" in names:
            out["conv_w"] = jnp.transpose(landed["conv_w"], (1, 0, 2)).reshape(4, D)
        return out

    small_shapes = {n: given[n].shape for n, _ in SMALL}
    small_shapes["conv_w"] = (1, 4, D)
    sm = {n: (given[n][0] if given[n].shape[0] == 1 and n != "rel_bias" else given[n]) for n, _ in SMALL if n != "conv_w"}

    reducer = _Reducer(jnp.stack([lax.axis_index("c"), chip]).astype(jnp.int32))
    sq, dx, _, small = _local_step(x[0], loss_target[0], weights, sm, reducer)

    reduced_small = _all_reduce_small(_pack_small(small, tail=sq))
    last_started = reducer.advance("ffn1", [dx, reduced_small])
    loss = reduced_small[SMALL_USED, 0] * (0.5 / D)
    small_g = _unpack_small(reduced_small, small_shapes)
    grads, delta, new_m, new_v = {}, {}, {}, {}
    in_transit = {}

    def send(stage, after):
        halves = reducer.finish(stage, after)
        lands = [lax.empty(h.shape, F32) for h in halves.values()]
        sems, mine, lands, token = _exchange_start(list(halves.values()), lands, _sibling_plan, len(lands),
                                                   "halves_start_" + stage)
        in_transit[stage] = (list(halves), sems, mine, lands)
        return token

    def update(stage, after):
        names, sems, mine, lands = in_transit[stage]
        mine, theirs = _exchange_wait(sems, mine, lands, _sibling_plan, after, "halves_wait_" + stage)
        updated = _adamw_halves([shard(n) for n in names], mine, theirs, [shard(n, "m_") for n in names],
                                [shard(n, "v_") for n in names], "adamw_" + stage)
        for n, results in zip(names, updated):
            grads[n], delta[n], new_m[n], new_v[n] = (unshard(n, r) for r in results)
        return new_v[names[-1]]

    token = send("ffn2", [reduced_small, last_started])
    token = send("mix", token)
    done = update("ffn2", token)
    done = update("mix", done)
    token = send("ffn1", done)
    update("ffn1", token)

    small_g["conv_w"] = only_my_columns(small_g["conv_w"])
    names = [n for n, _ in SMALL]
    flat2d = lambda a: a.reshape(-1, a.shape[-1])
    outs = _adamw_small(*[[flat2d(given[pre + n]) if pre != "g" else flat2d(small_g[n]) for n in names]
                          for pre in ("", "g", "m_", "v_")], after=last_started)
    for dst, arrs in zip((delta, new_m, new_v), outs):
        dst.update({n: a.reshape(given[n].shape) for n, a in zip(names, arrs)})
    grads.update(small_g)
    return (loss, dx[None], *[grads[n] for n in WEIGHTS], *[delta[n] for n in WEIGHTS], *[new_m[n] for n in WEIGHTS],
            *[new_v[n] for n in WEIGHTS])
```
